```python
import jax, jax.numpy as jnp
from jax import lax
import numpy as np

D_MODEL = 2048
BATCH = 8
SEQ = 4096
DEPTH = 1

D_SSM = D_MODEL
SSM_HEADDIM = 64
SSM_HEADS = D_SSM // SSM_HEADDIM
SSM_GROUPS = 8
SSM_STATE = 128
SSM_CONV = 4
CHUNK = 128
DT_MIN = 1e-3
DT_MAX = 1e-1
D_CONV = D_MODEL
SHORT_CONV = 3
D_MIX = D_SSM + D_CONV
D_FF = -(-(8 * D_MODEL) // (3 * 256)) * 256
EPS = 1e-5

D_XBC = D_SSM + 2 * SSM_GROUPS * SSM_STATE
OFF_Z = 0
OFF_XBC = OFF_Z + D_SSM
OFF_DT = OFF_XBC + D_XBC
OFF_CB = OFF_DT + SSM_HEADS
OFF_CC = OFF_CB + D_CONV
OFF_CX = OFF_CC + D_CONV
D_IN = OFF_CX + D_CONV

kernel_name = "hymba_ssd_shortconv_block"


def _rmsnorm(x, g):
    xf = x.astype(jnp.float32)
    y = xf * lax.rsqrt(jnp.mean(xf * xf, axis=-1, keepdims=True) + EPS)
    return (y * g.astype(jnp.float32)).astype(x.dtype)


def _causal_dwconv(u, w):
    K = w.shape[0]
    S = u.shape[1]
    up = jnp.pad(u, ((0, 0), (K - 1, 0), (0, 0)))
    y = up[:, K - 1:K - 1 + S] * w[K - 1]
    for k in range(K - 1):
        y = y + up[:, k:k + S] * w[k]
    return y


def _ssd_chunked(xh, dt, A, Bm, Cm):
    b, S, H, P = xh.shape
    G, N = Bm.shape[2], Bm.shape[3]
    R = H // G
    nc = S // CHUNK
    f32 = jnp.float32
    X = (xh.astype(f32) * dt[..., None]).reshape(b, nc, CHUNK, G, R, P)
    dA = jnp.moveaxis((dt * A).reshape(b, nc, CHUNK, G, R), 2, -1)
    Bc = Bm.astype(f32).reshape(b, nc, CHUNK, G, N)
    Cc = Cm.astype(f32).reshape(b, nc, CHUNK, G, N)
    dA_cs = jnp.cumsum(dA, axis=-1)
    causal = jnp.tril(jnp.ones((CHUNK, CHUNK), dtype=bool))
    seg = dA_cs[..., :, None] - dA_cs[..., None, :]
    L = jnp.exp(jnp.where(causal, seg, -jnp.inf))
    CB = jnp.einsum('bclgn,bcsgn->bcgls', Cc, Bc)
    M = CB[:, :, :, None] * L
    y_diag = jnp.einsum('bcgrls,bcsgrp->bclgrp', M, X)
    decay_states = jnp.exp(dA_cs[..., -1:] - dA_cs)
    states = jnp.einsum('bclgn,bcgrl,bclgrp->bcgrpn', Bc, decay_states, X)
    chunk_decay = jnp.exp(dA_cs[..., -1])

    def step(h, inp):
        dec, st = inp
        return h * dec[..., None, None] + st, h

    h0 = jnp.zeros((b, G, R, P, N), f32)
    _, prev = lax.scan(step, h0, (jnp.moveaxis(chunk_decay, 1, 0), jnp.moveaxis(states, 1, 0)))
    prev = jnp.moveaxis(prev, 0, 1)
    y_off = jnp.einsum('bclgn,bcgrpn,bcgrl->bclgrp', Cc, prev, jnp.exp(dA_cs))
    return (y_diag + y_off).reshape(b, S, H, P)


def _ssd_group(z, xbc, dt_raw, conv_w, conv_b, dt_bias, A_log, Dskip, norm_g):
    b, S, _ = z.shape
    xbc = jax.nn.silu(_causal_dwconv(xbc, conv_w) + conv_b)
    xs = xbc[..., :D_SSM].reshape(b, S, SSM_HEADS, SSM_HEADDIM)
    Bm = xbc[..., D_SSM:D_SSM + SSM_GROUPS * SSM_STATE].reshape(b, S, SSM_GROUPS, SSM_STATE)
    Cm = xbc[..., D_SSM + SSM_GROUPS * SSM_STATE:].reshape(b, S, SSM_GROUPS, SSM_STATE)
    dt = jax.nn.softplus(dt_raw.astype(jnp.float32) + dt_bias.astype(jnp.float32))
    A = -jnp.exp(A_log.astype(jnp.float32))
    y = _ssd_chunked(xs, dt, A, Bm, Cm)
    y = y + Dskip.astype(jnp.float32)[:, None] * xs.astype(jnp.float32)
    y = y.reshape(b, S, D_SSM).astype(z.dtype)
    return _rmsnorm(y * jax.nn.silu(z), norm_g)


def _shortconv_group(gb, gc, u, conv_w):
    return gb * _causal_dwconv(gc * u, conv_w)


def _fwd_setup_inputs(seed: int = 0) -> dict:
    key = jax.random.key(seed)
    ks = jax.random.split(key, 16)
    f32 = jnp.float32
    nrm = lambda k, shape, s: jax.random.normal(k, shape, f32) * s
    x = jax.random.normal(ks[0], (BATCH, SEQ, D_MODEL), f32)
    norm_mix_g = 1.0 + nrm(ks[1], (DEPTH, D_MODEL), 0.02)
    w_in = nrm(ks[2], (DEPTH, D_MODEL, D_IN), D_MODEL ** -0.5)
    ssm_conv_w = nrm(ks[3], (DEPTH, SSM_CONV, D_XBC), SSM_CONV ** -0.5)
    ssm_conv_b = nrm(ks[4], (DEPTH, D_XBC), 0.02)
    dt0 = jnp.exp(jax.random.uniform(ks[5], (DEPTH, SSM_HEADS), f32)
                  * (np.log(DT_MAX) - np.log(DT_MIN)) + np.log(DT_MIN))
    ssm_dt_bias = dt0 + jnp.log(-jnp.expm1(-dt0))
    ssm_A_log = jnp.log(jax.random.uniform(ks[6], (DEPTH, SSM_HEADS), f32, 1.0, 16.0))
    ssm_D = 1.0 + nrm(ks[7], (DEPTH, SSM_HEADS), 0.1)
    ssm_norm_g = 1.0 + nrm(ks[8], (DEPTH, D_SSM), 0.02)
    sc_conv_w = nrm(ks[9], (DEPTH, SHORT_CONV, D_CONV), SHORT_CONV ** -0.5)
    w_out = nrm(ks[10], (DEPTH, D_MIX, D_MODEL), D_MIX ** -0.5)
    norm_ffn_g = 1.0 + nrm(ks[11], (DEPTH, D_MODEL), 0.02)
    w_gate = nrm(ks[12], (DEPTH, D_MODEL, D_FF), D_MODEL ** -0.5)
    w_up = nrm(ks[13], (DEPTH, D_MODEL, D_FF), D_MODEL ** -0.5)
    w_down = nrm(ks[14], (DEPTH, D_FF, D_MODEL), D_FF ** -0.5)
    norm_final_g = 1.0 + nrm(ks[15], (D_MODEL,), 0.02)
    return {"x": x, "norm_mix_g": norm_mix_g, "w_in": w_in, "ssm_conv_w": ssm_conv_w,
            "ssm_conv_b": ssm_conv_b, "ssm_dt_bias": ssm_dt_bias, "ssm_A_log": ssm_A_log,
            "ssm_D": ssm_D, "ssm_norm_g": ssm_norm_g, "sc_conv_w": sc_conv_w, "w_out": w_out,
            "norm_ffn_g": norm_ffn_g, "w_gate": w_gate, "w_up": w_up, "w_down": w_down,
            "norm_final_g": norm_final_g}


def _fwd_reference(x, norm_mix_g, w_in, ssm_conv_w, ssm_conv_b, ssm_dt_bias, ssm_A_log, ssm_D,
              ssm_norm_g, sc_conv_w, w_out, norm_ffn_g, w_gate, w_up, w_down, norm_final_g):
    h = x
    for l in range(DEPTH):
        n = _rmsnorm(h, norm_mix_g[l])
        proj = jnp.einsum('bsd,de->bse', n, w_in[l])
        y_ssm = _ssd_group(proj[..., OFF_Z:OFF_XBC], proj[..., OFF_XBC:OFF_DT],
                           proj[..., OFF_DT:OFF_CB], ssm_conv_w[l], ssm_conv_b[l],
                           ssm_dt_bias[l], ssm_A_log[l], ssm_D[l], ssm_norm_g[l])
        y_sc = _shortconv_group(proj[..., OFF_CB:OFF_CC], proj[..., OFF_CC:OFF_CX],
                                proj[..., OFF_CX:D_IN], sc_conv_w[l])
        y_mix = jnp.concatenate([y_ssm, y_sc], axis=-1)
        h = h + jnp.einsum('bse,ed->bsd', y_mix, w_out[l])
        n2 = _rmsnorm(h, norm_ffn_g[l])
        g = jnp.einsum('bsd,df->bsf', n2, w_gate[l])
        u = jnp.einsum('bsd,df->bsf', n2, w_up[l])
        h = h + jnp.einsum('bsf,fd->bsd', jax.nn.silu(g) * u, w_down[l])
    return _rmsnorm(h, norm_final_g)


import jax as _jax
import jax.numpy as _jnp

TWIN_FORMAT = 'train_step'
FWD_PARAMS = ['x', 'norm_mix_g', 'w_in', 'ssm_conv_w', 'ssm_conv_b', 'ssm_dt_bias', 'ssm_A_log', 'ssm_D', 'ssm_norm_g', 'sc_conv_w', 'w_out', 'norm_ffn_g', 'w_gate', 'w_up', 'w_down', 'norm_final_g']
TWIN_WEIGHTS = ['norm_mix_g', 'w_in', 'ssm_conv_w', 'ssm_conv_b', 'ssm_dt_bias', 'ssm_A_log', 'ssm_D', 'ssm_norm_g', 'sc_conv_w', 'w_out', 'norm_ffn_g', 'w_gate', 'w_up', 'w_down', 'norm_final_g']
TWIN_DIFF_INPUT = 'x'
TWIN_INPUTS = ['x', 'norm_mix_g', 'w_in', 'ssm_conv_w', 'ssm_conv_b', 'ssm_dt_bias', 'ssm_A_log', 'ssm_D', 'ssm_norm_g', 'sc_conv_w', 'w_out', 'norm_ffn_g', 'w_gate', 'w_up', 'w_down', 'norm_final_g', 'loss_target', 'm_norm_mix_g', 'm_w_in', 'm_ssm_conv_w', 'm_ssm_conv_b', 'm_ssm_dt_bias', 'm_ssm_A_log', 'm_ssm_D', 'm_ssm_norm_g', 'm_sc_conv_w', 'm_w_out', 'm_norm_ffn_g', 'm_w_gate', 'm_w_up', 'm_w_down', 'm_norm_final_g', 'v_norm_mix_g', 'v_w_in', 'v_ssm_conv_w', 'v_ssm_conv_b', 'v_ssm_dt_bias', 'v_ssm_A_log', 'v_ssm_D', 'v_ssm_norm_g', 'v_sc_conv_w', 'v_w_out', 'v_norm_ffn_g', 'v_w_gate', 'v_w_up', 'v_w_down', 'v_norm_final_g']
TWIN_OUTPUTS = ['loss', 'grad_x', 'grad_norm_mix_g', 'grad_w_in', 'grad_ssm_conv_w', 'grad_ssm_conv_b', 'grad_ssm_dt_bias', 'grad_ssm_A_log', 'grad_ssm_D', 'grad_ssm_norm_g', 'grad_sc_conv_w', 'grad_w_out', 'grad_norm_ffn_g', 'grad_w_gate', 'grad_w_up', 'grad_w_down', 'grad_norm_final_g', 'delta_norm_mix_g', 'delta_w_in', 'delta_ssm_conv_w', 'delta_ssm_conv_b', 'delta_ssm_dt_bias', 'delta_ssm_A_log', 'delta_ssm_D', 'delta_ssm_norm_g', 'delta_sc_conv_w', 'delta_w_out', 'delta_norm_ffn_g', 'delta_w_gate', 'delta_w_up', 'delta_w_down', 'delta_norm_final_g', 'new_m_norm_mix_g', 'new_m_w_in', 'new_m_ssm_conv_w', 'new_m_ssm_conv_b', 'new_m_ssm_dt_bias', 'new_m_ssm_A_log', 'new_m_ssm_D', 'new_m_ssm_norm_g', 'new_m_sc_conv_w', 'new_m_w_out', 'new_m_norm_ffn_g', 'new_m_w_gate', 'new_m_w_up', 'new_m_w_down', 'new_m_norm_final_g', 'new_v_norm_mix_g', 'new_v_w_in', 'new_v_ssm_conv_w', 'new_v_ssm_conv_b', 'new_v_ssm_dt_bias', 'new_v_ssm_A_log', 'new_v_ssm_D', 'new_v_ssm_norm_g', 'new_v_sc_conv_w', 'new_v_w_out', 'new_v_norm_ffn_g', 'new_v_w_gate', 'new_v_w_up', 'new_v_w_down', 'new_v_norm_final_g']
TWIN_LEAF_KINDS = {'loss': 'loss', 'grad_x': 'grad_x', 'grad_norm_mix_g': 'grad_w', 'grad_w_in': 'grad_w', 'grad_ssm_conv_w': 'grad_w', 'grad_ssm_conv_b': 'grad_w', 'grad_ssm_dt_bias': 'grad_w', 'grad_ssm_A_log': 'grad_w', 'grad_ssm_D': 'grad_w', 'grad_ssm_norm_g': 'grad_w', 'grad_sc_conv_w': 'grad_w', 'grad_w_out': 'grad_w', 'grad_norm_ffn_g': 'grad_w', 'grad_w_gate': 'grad_w', 'grad_w_up': 'grad_w', 'grad_w_down': 'grad_w', 'grad_norm_final_g': 'grad_w', 'delta_norm_mix_g': 'delta_w', 'delta_w_in': 'delta_w', 'delta_ssm_conv_w': 'delta_w', 'delta_ssm_conv_b': 'delta_w', 'delta_ssm_dt_bias': 'delta_w', 'delta_ssm_A_log': 'delta_w', 'delta_ssm_D': 'delta_w', 'delta_ssm_norm_g': 'delta_w', 'delta_sc_conv_w': 'delta_w', 'delta_w_out': 'delta_w', 'delta_norm_ffn_g': 'delta_w', 'delta_w_gate': 'delta_w', 'delta_w_up': 'delta_w', 'delta_w_down': 'delta_w', 'delta_norm_final_g': 'delta_w', 'new_m_norm_mix_g': 'new_m', 'new_m_w_in': 'new_m', 'new_m_ssm_conv_w': 'new_m', 'new_m_ssm_conv_b': 'new_m', 'new_m_ssm_dt_bias': 'new_m', 'new_m_ssm_A_log': 'new_m', 'new_m_ssm_D': 'new_m', 'new_m_ssm_norm_g': 'new_m', 'new_m_sc_conv_w': 'new_m', 'new_m_w_out': 'new_m', 'new_m_norm_ffn_g': 'new_m', 'new_m_w_gate': 'new_m', 'new_m_w_up': 'new_m', 'new_m_w_down': 'new_m', 'new_m_norm_final_g': 'new_m', 'new_v_norm_mix_g': 'new_v', 'new_v_w_in': 'new_v', 'new_v_ssm_conv_w': 'new_v', 'new_v_ssm_conv_b': 'new_v', 'new_v_ssm_dt_bias': 'new_v', 'new_v_ssm_A_log': 'new_v', 'new_v_ssm_D': 'new_v', 'new_v_ssm_norm_g': 'new_v', 'new_v_sc_conv_w': 'new_v', 'new_v_w_out': 'new_v', 'new_v_norm_ffn_g': 'new_v', 'new_v_w_gate': 'new_v', 'new_v_w_up': 'new_v', 'new_v_w_down': 'new_v', 'new_v_norm_final_g': 'new_v'}


def _forward(args):
    return _fwd_reference(*[args[k] for k in FWD_PARAMS])


def _output_shape():
    def fwd():
        inp = _fwd_setup_inputs(0)
        return _fwd_reference(*[inp[k] for k in FWD_PARAMS])
    out = _jax.eval_shape(fwd)
    return out.shape, out.dtype

N_MICROBATCH = 1
ADAM_LR = 0.001
ADAM_B1 = 0.9
ADAM_B2 = 0.999
ADAM_EPS = 1e-08
ADAM_WD = 0.01
ADAM_STEP = 10
PER_EXAMPLE_BATCH_AXIS = {'x': 0, 'loss_target': 0}
SHARED_INPUTS = []
_WEIGHT_DTYPES = {'norm_mix_g': _jnp.float32, 'w_in': _jnp.float32, 'ssm_conv_w': _jnp.float32, 'ssm_conv_b': _jnp.float32, 'ssm_dt_bias': _jnp.float32, 'ssm_A_log': _jnp.float32, 'ssm_D': _jnp.float32, 'ssm_norm_g': _jnp.float32, 'sc_conv_w': _jnp.float32, 'w_out': _jnp.float32, 'norm_ffn_g': _jnp.float32, 'w_gate': _jnp.float32, 'w_up': _jnp.float32, 'w_down': _jnp.float32, 'norm_final_g': _jnp.float32}
MOMENT_SCALE = {'norm_mix_g': 1.125215e-01, 'w_in': 4.504404e-02, 'ssm_conv_w': 3.671547e-02, 'ssm_conv_b': 5.275092e-02, 'ssm_dt_bias': 1.340389e-01, 'ssm_A_log': 1.542133e-01, 'ssm_D': 2.872763e-01, 'ssm_norm_g': 4.786635e-02, 'sc_conv_w': 4.764091e-02, 'w_out': 6.774954e-02, 'norm_ffn_g': 5.057346e-02, 'w_gate': 2.141736e-02, 'w_up': 2.072333e-02, 'w_down': 3.434486e-02, 'norm_final_g': 1.599640e+01}


def _to_microbatches(a, axis):
    t = _jnp.moveaxis(a, axis, 0)
    t = t.reshape((N_MICROBATCH, t.shape[0] // N_MICROBATCH) + t.shape[1:])
    return _jnp.moveaxis(t, 1, axis + 1)


def setup_inputs(seed: int = 0) -> dict:
    inp = _fwd_setup_inputs(seed)
    key = _jax.random.fold_in(_jax.random.key(seed), 7919)
    shape, _ = _output_shape()
    out = dict(inp)
    out["loss_target"] = _jax.random.normal(_jax.random.fold_in(key, 0), shape, _jnp.float32)
    for i, name in enumerate(TWIN_WEIGHTS):
        w = inp[name].astype(_jnp.float32)
        if MOMENT_SCALE is None:
            s = _jnp.sqrt(_jnp.mean(_jnp.square(w)) + 1e-30)
        else:
            s = MOMENT_SCALE[name]
        km, kv = _jax.random.split(_jax.random.fold_in(key, i + 1))
        out[name] = w
        out["m_" + name] = s * _jax.random.normal(km, w.shape, _jnp.float32)
        out["v_" + name] = (s * s) * _jax.random.uniform(kv, w.shape, _jnp.float32, 0.5, 1.5)
    if N_MICROBATCH > 1:
        for name, axis in PER_EXAMPLE_BATCH_AXIS.items():
            out[name] = _to_microbatches(out[name], axis)
    return {'x': out['x'], 'norm_mix_g': out['norm_mix_g'], 'w_in': out['w_in'], 'ssm_conv_w': out['ssm_conv_w'], 'ssm_conv_b': out['ssm_conv_b'], 'ssm_dt_bias': out['ssm_dt_bias'], 'ssm_A_log': out['ssm_A_log'], 'ssm_D': out['ssm_D'], 'ssm_norm_g': out['ssm_norm_g'], 'sc_conv_w': out['sc_conv_w'], 'w_out': out['w_out'], 'norm_ffn_g': out['norm_ffn_g'], 'w_gate': out['w_gate'], 'w_up': out['w_up'], 'w_down': out['w_down'], 'norm_final_g': out['norm_final_g'], 'loss_target': out['loss_target'], 'm_norm_mix_g': out['m_norm_mix_g'], 'm_w_in': out['m_w_in'], 'm_ssm_conv_w': out['m_ssm_conv_w'], 'm_ssm_conv_b': out['m_ssm_conv_b'], 'm_ssm_dt_bias': out['m_ssm_dt_bias'], 'm_ssm_A_log': out['m_ssm_A_log'], 'm_ssm_D': out['m_ssm_D'], 'm_ssm_norm_g': out['m_ssm_norm_g'], 'm_sc_conv_w': out['m_sc_conv_w'], 'm_w_out': out['m_w_out'], 'm_norm_ffn_g': out['m_norm_ffn_g'], 'm_w_gate': out['m_w_gate'], 'm_w_up': out['m_w_up'], 'm_w_down': out['m_w_down'], 'm_norm_final_g': out['m_norm_final_g'], 'v_norm_mix_g': out['v_norm_mix_g'], 'v_w_in': out['v_w_in'], 'v_ssm_conv_w': out['v_ssm_conv_w'], 'v_ssm_conv_b': out['v_ssm_conv_b'], 'v_ssm_dt_bias': out['v_ssm_dt_bias'], 'v_ssm_A_log': out['v_ssm_A_log'], 'v_ssm_D': out['v_ssm_D'], 'v_ssm_norm_g': out['v_ssm_norm_g'], 'v_sc_conv_w': out['v_sc_conv_w'], 'v_w_out': out['v_w_out'], 'v_norm_ffn_g': out['v_norm_ffn_g'], 'v_w_gate': out['v_w_gate'], 'v_w_up': out['v_w_up'], 'v_w_down': out['v_w_down'], 'v_norm_final_g': out['v_norm_final_g']}


def _loss(weights, diff, rest, loss_target):
    with _jax.named_scope("forward"):
        args = {**rest, TWIN_DIFF_INPUT: diff, **{k: w.astype(_WEIGHT_DTYPES[k]) for k, w in weights.items()}}
        y = _forward(args)
    with _jax.named_scope("loss_head"):
        err = _jnp.square(y.astype(_jnp.float32) - loss_target)
        return 0.5 * _jnp.sum(_jnp.mean(err, axis=-1)) if err.ndim else 0.5 * err


def _adamw(w, g, m, v):
    m = ADAM_B1 * m + (1.0 - ADAM_B1) * g
    v = ADAM_B2 * v + (1.0 - ADAM_B2) * _jnp.square(g)
    m_hat = m / (1.0 - ADAM_B1 ** ADAM_STEP)
    v_hat = v / (1.0 - ADAM_B2 ** ADAM_STEP)
    delta = -ADAM_LR * (m_hat / (_jnp.sqrt(v_hat) + ADAM_EPS) + ADAM_WD * w)
    return delta, m, v


def reference(x, norm_mix_g, w_in, ssm_conv_w, ssm_conv_b, ssm_dt_bias, ssm_A_log, ssm_D, ssm_norm_g, sc_conv_w, w_out, norm_ffn_g, w_gate, w_up, w_down, norm_final_g, loss_target, m_norm_mix_g, m_w_in, m_ssm_conv_w, m_ssm_conv_b, m_ssm_dt_bias, m_ssm_A_log, m_ssm_D, m_ssm_norm_g, m_sc_conv_w, m_w_out, m_norm_ffn_g, m_w_gate, m_w_up, m_w_down, m_norm_final_g, v_norm_mix_g, v_w_in, v_ssm_conv_w, v_ssm_conv_b, v_ssm_dt_bias, v_ssm_A_log, v_ssm_D, v_ssm_norm_g, v_sc_conv_w, v_w_out, v_norm_ffn_g, v_w_gate, v_w_up, v_w_down, v_norm_final_g):
    given = dict(x=x, norm_mix_g=norm_mix_g, w_in=w_in, ssm_conv_w=ssm_conv_w, ssm_conv_b=ssm_conv_b, ssm_dt_bias=ssm_dt_bias, ssm_A_log=ssm_A_log, ssm_D=ssm_D, ssm_norm_g=ssm_norm_g, sc_conv_w=sc_conv_w, w_out=w_out, norm_ffn_g=norm_ffn_g, w_gate=w_gate, w_up=w_up, w_down=w_down, norm_final_g=norm_final_g, loss_target=loss_target, m_norm_mix_g=m_norm_mix_g, m_w_in=m_w_in, m_ssm_conv_w=m_ssm_conv_w, m_ssm_conv_b=m_ssm_conv_b, m_ssm_dt_bias=m_ssm_dt_bias, m_ssm_A_log=m_ssm_A_log, m_ssm_D=m_ssm_D, m_ssm_norm_g=m_ssm_norm_g, m_sc_conv_w=m_sc_conv_w, m_w_out=m_w_out, m_norm_ffn_g=m_norm_ffn_g, m_w_gate=m_w_gate, m_w_up=m_w_up, m_w_down=m_w_down, m_norm_final_g=m_norm_final_g, v_norm_mix_g=v_norm_mix_g, v_w_in=v_w_in, v_ssm_conv_w=v_ssm_conv_w, v_ssm_conv_b=v_ssm_conv_b, v_ssm_dt_bias=v_ssm_dt_bias, v_ssm_A_log=v_ssm_A_log, v_ssm_D=v_ssm_D, v_ssm_norm_g=v_ssm_norm_g, v_sc_conv_w=v_sc_conv_w, v_w_out=v_w_out, v_norm_ffn_g=v_norm_ffn_g, v_w_gate=v_w_gate, v_w_up=v_w_up, v_w_down=v_w_down, v_norm_final_g=v_norm_final_g)
    weights = {n: given[n] for n in TWIN_WEIGHTS}
    shared = {n: given[n] for n in SHARED_INPUTS}
    per_example = {n: given[n] for n in ['x']}
    grad_fn = _jax.value_and_grad(_loss, argnums=(0, 1))

    def one_microbatch(ex, loss_target):
        ex = dict(ex)
        diff = ex.pop(TWIN_DIFF_INPUT)
        return grad_fn(weights, diff, {**shared, **ex}, loss_target)

    if N_MICROBATCH == 1:
        loss, (grad_w, grad_x) = one_microbatch(per_example, given["loss_target"])
    else:
        def body(carry, xs):
            loss_sum, grad_sum = carry
            l_k, (gw_k, gx_k) = one_microbatch(xs[0], xs[1])
            with _jax.named_scope("update"):
                return (loss_sum + l_k, _jax.tree.map(_jnp.add, grad_sum, gw_k)), gx_k

        init = (_jnp.zeros((), _jnp.float32), _jax.tree.map(_jnp.zeros_like, weights))
        (loss, grad_w), grad_x = _jax.lax.scan(body, init, (per_example, given["loss_target"]))
    with _jax.named_scope("update"):
        delta_w, new_m, new_v = {}, {}, {}
        for n in TWIN_WEIGHTS:
            delta_w[n], new_m[n], new_v[n] = _adamw(weights[n], grad_w[n], given["m_" + n], given["v_" + n])
    return (loss, grad_x, *[grad_w[n] for n in TWIN_WEIGHTS], *[delta_w[n] for n in TWIN_WEIGHTS],
            *[new_m[n] for n in TWIN_WEIGHTS], *[new_v[n] for n in TWIN_WEIGHTS])
```

```python
import functools

import jax
import jax.numpy as jnp
from jax import lax
from jax.experimental import pallas as pl
from jax.experimental.pallas import tpu as pltpu

F32 = jnp.float32
BF16 = jnp.bfloat16
MESH = pl.DeviceIdType.MESH

D_MODEL = 2048
D_SSM = 2048
HEADDIM = 64
N_HEADS = 32
N_GROUPS = 8
HEADS_PER_GROUP = 4
N_STATE = 128
CHUNK = 128
K_SSM = 4
K_SC = 3
D_XBC = 4096
D_FF = 5632
D_IN = 12320
D_MAIN = 12288
OFF_XBC, OFF_CB, OFF_CC, OFF_CX = 2048, 6144, 8192, 10240
DT_PAD = 128
EPS = 1e-5
N_CHIPS = 4
N_DEV = 8

ADAM_LR = 0.001
ADAM_B1 = 0.9
ADAM_B2 = 0.999
ADAM_EPS = 1e-08
ADAM_WD = 0.01
ADAM_STEP = 10

V7X_VMEM_BYTES = 64 * 1024 * 1024
VMEM_LIMIT = V7X_VMEM_BYTES - 8 * 1024 * 1024


def _cparams(sem=None):
    if sem is None:
        return pltpu.CompilerParams(vmem_limit_bytes=VMEM_LIMIT)
    return pltpu.CompilerParams(dimension_semantics=sem, vmem_limit_bytes=VMEM_LIMIT)


def _tile(dim, pref, unit=128):
    best = None
    t = unit
    while t <= min(dim, pref):
        if dim % t == 0:
            best = t
        t += unit
    return best if best is not None else dim


def _sigmoid(x):
    return 1.0 / (1.0 + jnp.exp(-x))


def _silu(x):
    return x * _sigmoid(x)


def _dsilu(x):
    s = _sigmoid(x)
    return s * (1.0 + x * (1.0 - s))


def _softplus(x):
    return jnp.maximum(x, 0.0) + jnp.log(1.0 + jnp.exp(-jnp.abs(x)))


def _matmul(pairs, *, ta=False, tb=False, out_dtypes, name, tm=1024, tn=1024, tk=512, extras=(), epilogue=None):
    a0, b0 = pairs[0]
    M, K = (a0.shape[1], a0.shape[0]) if ta else a0.shape
    N = b0.shape[0] if tb else b0.shape[1]
    tm, tn, tk = _tile(M, tm, 8 if M % 128 else 128), _tile(N, tn), _tile(K, tk)
    npair, nex, nout = len(pairs), len(extras), len(out_dtypes)
    nk = K // tk
    dims = (((0 if ta else 1,), (1 if tb else 0,)), ((), ()))

    def body(*refs):
        a_refs = refs[0:2 * npair:2]
        b_refs = refs[1:2 * npair:2]
        ex_refs = refs[2 * npair:2 * npair + nex]
        o_refs = refs[2 * npair + nex:2 * npair + nex + nout]
        acc = refs[-1]
        k = pl.program_id(2)

        @pl.when(k == 0)
        def _():
            acc[...] = jnp.zeros_like(acc)

        s = None
        for a_ref, b_ref in zip(a_refs, b_refs):
            d = lax.dot_general(a_ref[...], b_ref[...], dims, preferred_element_type=F32)
            s = d if s is None else s + d
        acc[...] += s

        @pl.when(k == nk - 1)
        def _():
            r = acc[...]
            outs = (r,) if epilogue is None else epilogue(r, *[e[...] for e in ex_refs])
            for o_ref, o in zip(o_refs, outs):
                o_ref[...] = o.astype(o_ref.dtype)

    a_spec = pl.BlockSpec((tk, tm), lambda i, j, k: (k, i)) if ta else pl.BlockSpec((tm, tk), lambda i, j, k: (i, k))
    b_spec = pl.BlockSpec((tn, tk), lambda i, j, k: (j, k)) if tb else pl.BlockSpec((tk, tn), lambda i, j, k: (k, j))
    o_spec = pl.BlockSpec((tm, tn), lambda i, j, k: (i, j))
    args, in_specs = [], []
    for a, b in pairs:
        args += [a, b]
        in_specs += [a_spec, b_spec]
    args += list(extras)
    in_specs += [o_spec] * nex
    outs = pl.pallas_call(
        body,
        name=name,
        grid=(M // tm, N // tn, nk),
        in_specs=in_specs,
        out_specs=[o_spec] * nout,
        out_shape=[jax.ShapeDtypeStruct((M, N), dt) for dt in out_dtypes],
        scratch_shapes=[pltpu.VMEM((tm, tn), F32)],
        compiler_params=_cparams(("parallel", "parallel", "arbitrary")),
    )(*args)
    return outs


def _cast_bf16(w, name):
    R, C = w.shape
    tr = _tile(R, 512, 8)

    def body(w_ref, o_ref):
        o_ref[...] = w_ref[...].astype(BF16)

    return pl.pallas_call(
        body, name=name, grid=(R // tr,),
        in_specs=[pl.BlockSpec((tr, C), lambda i: (i, 0))],
        out_specs=pl.BlockSpec((tr, C), lambda i: (i, 0)),
        out_shape=jax.ShapeDtypeStruct((R, C), BF16),
        compiler_params=_cparams(("parallel",)),
    )(w)


def _rmsnorm_fwd(x, g, name):
    T, D = x.shape
    tt = _tile(T, 256)

    def body(x_ref, g_ref, n_ref):
        xv = x_ref[...]
        r = lax.rsqrt(jnp.mean(xv * xv, axis=-1, keepdims=True) + EPS)
        n_ref[...] = (xv * r * g_ref[...]).astype(BF16)

    return pl.pallas_call(
        body, name=name, grid=(T // tt,),
        in_specs=[pl.BlockSpec((tt, D), lambda i: (i, 0)), pl.BlockSpec((1, D), lambda i: (0, 0))],
        out_specs=pl.BlockSpec((tt, D), lambda i: (i, 0)),
        out_shape=jax.ShapeDtypeStruct((T, D), BF16),
        compiler_params=_cparams(("parallel",)),
    )(x, g)


def _rmsnorm_bwd(dn, x, g, res, name):
    T, D = x.shape
    tt = _tile(T, 256)

    def body(dn_ref, x_ref, g_ref, res_ref, dx_ref, dxb_ref, dg_ref):
        @pl.when(pl.program_id(0) == 0)
        def _():
            dg_ref[...] = jnp.zeros_like(dg_ref)

        xv = x_ref[...]
        dy = dn_ref[...].astype(F32)
        r = lax.rsqrt(jnp.mean(xv * xv, axis=-1, keepdims=True) + EPS)
        xhat = xv * r
        dxh = dy * g_ref[...]
        dx = res_ref[...] + r * (dxh - xhat * jnp.mean(dxh * xhat, axis=-1, keepdims=True))
        dx_ref[...] = dx
        dxb_ref[...] = dx.astype(BF16)
        dg_ref[...] += jnp.sum(dy * xhat, axis=0, keepdims=True)

    tok = pl.BlockSpec((tt, D), lambda i: (i, 0))
    vec = pl.BlockSpec((1, D), lambda i: (0, 0))
    return pl.pallas_call(
        body, name=name, grid=(T // tt,),
        in_specs=[tok, tok, vec, tok],
        out_specs=[tok, tok, vec],
        out_shape=[jax.ShapeDtypeStruct((T, D), F32), jax.ShapeDtypeStruct((T, D), BF16),
                   jax.ShapeDtypeStruct((1, D), F32)],
        compiler_params=_cparams(("arbitrary",)),
    )(dn, x, g, res)


def _loss_and_final_bwd(h2, target, gf):
    T, D = h2.shape
    tt = _tile(T, 256)

    def body(h_ref, t_ref, g_ref, dh_ref, dhb_ref, dg_ref, loss_ref):
        @pl.when(pl.program_id(0) == 0)
        def _():
            dg_ref[...] = jnp.zeros_like(dg_ref)
            loss_ref[...] = jnp.zeros_like(loss_ref)

        xv = h_ref[...]
        r = lax.rsqrt(jnp.mean(xv * xv, axis=-1, keepdims=True) + EPS)
        xhat = xv * r
        err = xhat * g_ref[...] - t_ref[...]
        loss_ref[...] += 0.5 * jnp.sum(jnp.mean(err * err, axis=-1, keepdims=True), axis=0, keepdims=True)
        dy = err * (1.0 / D)
        dxh = dy * g_ref[...]
        dx = r * (dxh - xhat * jnp.mean(dxh * xhat, axis=-1, keepdims=True))
        dh_ref[...] = dx
        dhb_ref[...] = dx.astype(BF16)
        dg_ref[...] += jnp.sum(dy * xhat, axis=0, keepdims=True)

    tok = pl.BlockSpec((tt, D), lambda i: (i, 0))
    vec = pl.BlockSpec((1, D), lambda i: (0, 0))
    return pl.pallas_call(
        body, name="loss_final_bwd", grid=(T // tt,),
        in_specs=[tok, tok, vec],
        out_specs=[tok, tok, vec, pl.BlockSpec((1, 1), lambda i: (0, 0))],
        out_shape=[jax.ShapeDtypeStruct((T, D), F32), jax.ShapeDtypeStruct((T, D), BF16),
                   jax.ShapeDtypeStruct((1, D), F32), jax.ShapeDtypeStruct((1, 1), F32)],
        compiler_params=_cparams(("arbitrary",)),
    )(h2, target, gf)


def _gated_norm_fwd(y, proj, g):
    T, D = y.shape
    tt = _tile(T, 256)

    def body(y_ref, z_ref, g_ref, o_ref):
        yg = y_ref[...] * _silu(z_ref[...])
        r = lax.rsqrt(jnp.mean(yg * yg, axis=-1, keepdims=True) + EPS)
        o_ref[...] = (yg * r * g_ref[...]).astype(BF16)

    tok = pl.BlockSpec((tt, D), lambda i: (i, 0))
    return pl.pallas_call(
        body, name="gated_norm_fwd", grid=(T // tt,),
        in_specs=[tok, tok, pl.BlockSpec((1, D), lambda i: (0, 0))],
        out_specs=tok,
        out_shape=jax.ShapeDtypeStruct((T, D), BF16),
        compiler_params=_cparams(("parallel",)),
    )(y, proj, g)


def _gated_norm_bwd(dmix, y, proj, g):
    T, D = y.shape
    tt = _tile(T, 256)

    def body(do_ref, y_ref, z_ref, g_ref, dy_ref, dz_ref, dg_ref):
        @pl.when(pl.program_id(0) == 0)
        def _():
            dg_ref[...] = jnp.zeros_like(dg_ref)

        yv, zv = y_ref[...], z_ref[...]
        do = do_ref[...].astype(F32)
        sz = _silu(zv)
        yg = yv * sz
        r = lax.rsqrt(jnp.mean(yg * yg, axis=-1, keepdims=True) + EPS)
        xhat = yg * r
        dxh = do * g_ref[...]
        dyg = r * (dxh - xhat * jnp.mean(dxh * xhat, axis=-1, keepdims=True))
        dy_ref[...] = dyg * sz
        dz_ref[...] = (dyg * yv * _dsilu(zv)).astype(BF16)
        dg_ref[...] += jnp.sum(do * xhat, axis=0, keepdims=True)

    tok = pl.BlockSpec((tt, D), lambda i: (i, 0))
    vec = pl.BlockSpec((1, D), lambda i: (0, 0))
    return pl.pallas_call(
        body, name="gated_norm_bwd", grid=(T // tt,),
        in_specs=[tok, tok, tok, vec],
        out_specs=[tok, tok, vec],
        out_shape=[jax.ShapeDtypeStruct((T, D), F32), jax.ShapeDtypeStruct((T, D), BF16),
                   jax.ShapeDtypeStruct((1, D), F32)],
        compiler_params=_cparams(("arbitrary",)),
    )(dmix, y, proj, g)


HALO = 8


def _shift_down(cur, prev8, s):
    ext = jnp.concatenate([prev8, cur], axis=0)
    return pltpu.roll(ext, s, axis=0)[HALO:]


def _shift_up(cur, next8, s):
    n = cur.shape[0]
    ext = jnp.concatenate([cur, next8], axis=0)
    return pltpu.roll(ext, n + HALO - s, axis=0)[:n]


def _conv_specs(tt, cb, col_off_blocks, nt):
    hb = tt // HALO
    cur = pl.BlockSpec((tt, cb), lambda j, i: (i, col_off_blocks + j))
    prev = pl.BlockSpec((HALO, cb), lambda j, i: (jnp.maximum(i * hb - 1, 0), col_off_blocks + j))
    nxt = pl.BlockSpec((HALO, cb), lambda j, i: (jnp.minimum((i + 1) * hb, nt * hb - 1), col_off_blocks + j))
    return cur, prev, nxt


def _causal_conv(cur, prev8, w, K):
    y = cur * w[K - 1:K, :]
    for k in range(K - 1):
        y = y + _shift_down(cur, prev8, K - 1 - k) * w[k:k + 1, :]
    return y


def _anticausal_conv(cur, next8, w, K):
    y = cur * w[K - 1:K, :]
    for k in range(K - 1):
        y = y + _shift_up(cur, next8, K - 1 - k) * w[k:k + 1, :]
    return y


def _ssm_conv_fwd(proj, w8, b):
    T = proj.shape[0]
    tt, cb = _tile(T, 512), 512
    nt = T // tt
    cur, prev, _ = _conv_specs(tt, cb, OFF_XBC // cb, nt)

    def body(u_ref, up_ref, w_ref, b_ref, o_ref):
        first = pl.program_id(1) == 0
        p8 = jnp.where(first, 0.0, up_ref[...])
        pre = _causal_conv(u_ref[...], p8, w_ref[...], K_SSM) + b_ref[...]
        o_ref[...] = _silu(pre)

    return pl.pallas_call(
        body, name="ssm_conv_fwd", grid=(D_XBC // cb, nt),
        in_specs=[cur, prev, pl.BlockSpec((8, cb), lambda j, i: (0, j)), pl.BlockSpec((1, cb), lambda j, i: (0, j))],
        out_specs=pl.BlockSpec((tt, cb), lambda j, i: (i, j)),
        out_shape=jax.ShapeDtypeStruct((T, D_XBC), F32),
        compiler_params=_cparams(("parallel", "parallel")),
    )(proj, proj, w8, b)


def _ssm_conv_bwd(dact, proj, w8, b):
    T = proj.shape[0]
    tt, cb = _tile(T, 512), 512
    nt = T // tt
    cur, prev, nxt = _conv_specs(tt, cb, OFF_XBC // cb, nt)
    dcur, dprev, dnxt = _conv_specs(tt, cb, 0, nt)

    def dpre_of(d, u, p8, w, bb):
        pre = _causal_conv(u, p8, w, K_SSM) + bb
        return d * _dsilu(pre)

    def body(d_ref, dn_ref, u_ref, up_ref, un_ref, w_ref, b_ref, dx_ref, dw_ref, db_ref):
        i = pl.program_id(1)

        @pl.when(i == 0)
        def _():
            dw_ref[...] = jnp.zeros_like(dw_ref)
            db_ref[...] = jnp.zeros_like(db_ref)

        w, bb = w_ref[...], b_ref[...]
        u = u_ref[...]
        p8 = jnp.where(i == 0, 0.0, up_ref[...])
        dpre = dpre_of(d_ref[...], u, p8, w, bb)
        un = un_ref[...]
        dpre_n = dpre_of(dn_ref[...], un, u[tt - HALO:, :], w, bb)
        dpre_n = jnp.where(i == nt - 1, 0.0, dpre_n)
        dx_ref[...] = _anticausal_conv(dpre, dpre_n, w, K_SSM).astype(BF16)
        rows = [jnp.sum(dpre * _shift_down(u, p8, K_SSM - 1 - k), axis=0, keepdims=True) for k in range(K_SSM - 1)]
        rows.append(jnp.sum(dpre * u, axis=0, keepdims=True))
        rows.append(jnp.zeros((8 - K_SSM, cb), F32))
        dw_ref[...] += jnp.concatenate(rows, axis=0)
        db_ref[...] += jnp.sum(dpre, axis=0, keepdims=True)

    wspec = pl.BlockSpec((8, cb), lambda j, i: (0, j))
    bspec = pl.BlockSpec((1, cb), lambda j, i: (0, j))
    return pl.pallas_call(
        body, name="ssm_conv_bwd", grid=(D_XBC // cb, nt),
        in_specs=[dcur, dnxt, cur, prev, nxt, wspec, bspec],
        out_specs=[pl.BlockSpec((tt, cb), lambda j, i: (i, j)), wspec, bspec],
        out_shape=[jax.ShapeDtypeStruct((T, D_XBC), BF16), jax.ShapeDtypeStruct((8, D_XBC), F32),
                   jax.ShapeDtypeStruct((1, D_XBC), F32)],
        compiler_params=_cparams(("parallel", "arbitrary")),
    )(dact, dact, proj, proj, proj, w8, b)


def _shortconv_fwd(proj, w8):
    T = proj.shape[0]
    tt, cb = _tile(T, 512), 512
    nt = T // tt
    gb_s, _, _ = _conv_specs(tt, cb, OFF_CB // cb, nt)
    gc_s, gcp_s, _ = _conv_specs(tt, cb, OFF_CC // cb, nt)
    u_s, up_s, _ = _conv_specs(tt, cb, OFF_CX // cb, nt)

    def body(gb_ref, gc_ref, gcp_ref, u_ref, up_ref, w_ref, o_ref):
        v = gc_ref[...] * u_ref[...]
        vp = jnp.where(pl.program_id(1) == 0, 0.0, gcp_ref[...] * up_ref[...])
        o_ref[...] = (gb_ref[...] * _causal_conv(v, vp, w_ref[...], K_SC)).astype(BF16)

    return pl.pallas_call(
        body, name="shortconv_fwd", grid=(D_MODEL // cb, nt),
        in_specs=[gb_s, gc_s, gcp_s, u_s, up_s, pl.BlockSpec((8, cb), lambda j, i: (0, j))],
        out_specs=pl.BlockSpec((tt, cb), lambda j, i: (i, j)),
        out_shape=jax.ShapeDtypeStruct((T, D_MODEL), BF16),
        compiler_params=_cparams(("parallel", "parallel")),
    )(proj, proj, proj, proj, proj, w8)


def _shortconv_bwd(dmix, proj, w8):
    T = proj.shape[0]
    tt, cb = _tile(T, 512), 512
    nt = T // tt
    d_s, _, dn_s = _conv_specs(tt, cb, D_SSM // cb, nt)
    gb_s, _, gbn_s = _conv_specs(tt, cb, OFF_CB // cb, nt)
    gc_s, gcp_s, _ = _conv_specs(tt, cb, OFF_CC // cb, nt)
    u_s, up_s, _ = _conv_specs(tt, cb, OFF_CX // cb, nt)

    def body(d_ref, dn_ref, gb_ref, gbn_ref, gc_ref, gcp_ref, u_ref, up_ref, w_ref,
             dgb_ref, dgc_ref, du_ref, dw_ref):
        i = pl.program_id(1)

        @pl.when(i == 0)
        def _():
            dw_ref[...] = jnp.zeros_like(dw_ref)

        w = w_ref[...]
        gc, u = gc_ref[...], u_ref[...]
        v = gc * u
        vp = jnp.where(i == 0, 0.0, gcp_ref[...] * up_ref[...])
        d = d_ref[...].astype(F32)
        dgb_ref[...] = (d * _causal_conv(v, vp, w, K_SC)).astype(BF16)
        dcv = d * gb_ref[...]
        dcv_n = jnp.where(i == nt - 1, 0.0, dn_ref[...].astype(F32) * gbn_ref[...])
        dv = _anticausal_conv(dcv, dcv_n, w, K_SC)
        dgc_ref[...] = (dv * u).astype(BF16)
        du_ref[...] = (dv * gc).astype(BF16)
        rows = [jnp.sum(dcv * _shift_down(v, vp, K_SC - 1 - k), axis=0, keepdims=True) for k in range(K_SC - 1)]
        rows.append(jnp.sum(dcv * v, axis=0, keepdims=True))
        rows.append(jnp.zeros((8 - K_SC, cb), F32))
        dw_ref[...] += jnp.concatenate(rows, axis=0)

    wspec = pl.BlockSpec((8, cb), lambda j, i: (0, j))
    tok = pl.BlockSpec((tt, cb), lambda j, i: (i, j))
    return pl.pallas_call(
        body, name="shortconv_bwd", grid=(D_MODEL // cb, nt),
        in_specs=[d_s, dn_s, gb_s, gbn_s, gc_s, gcp_s, u_s, up_s, wspec],
        out_specs=[tok, tok, tok, wspec],
        out_shape=[jax.ShapeDtypeStruct((T, D_MODEL), BF16)] * 3 + [jax.ShapeDtypeStruct((8, D_MODEL), F32)],
        compiler_params=_cparams(("parallel", "arbitrary")),
    )(dmix, dmix, proj, proj, proj, proj, proj, proj, w8)


GW = HEADS_PER_GROUP * HEADDIM
HI = lax.Precision.HIGHEST


def _dot(a, b):
    return jnp.dot(a.astype(BF16), b.astype(BF16), preferred_element_type=F32)


def _dot_nt(a, b):
    return lax.dot_general(a.astype(BF16), b.astype(BF16), (((1,), (1,)), ((), ())), preferred_element_type=F32)


def _dot_tn(a, b):
    return lax.dot_general(a.astype(BF16), b.astype(BF16), (((0,), (0,)), ((), ())), preferred_element_type=F32)


def _dot_hi(a, b):
    return jnp.dot(a, b, precision=HI, preferred_element_type=F32)


def _dot_nt_hi(a, b):
    return lax.dot_general(a, b, (((1,), (1,)), ((), ())), precision=HI, preferred_element_type=F32)


def _head_cols(rows):
    parts = [jnp.broadcast_to(rows[r:r + 1, :], (HEADDIM, CHUNK)) for r in range(HEADS_PER_GROUP)]
    return jnp.concatenate(parts, axis=0).T


def _head_rows(rows):
    parts = [jnp.broadcast_to(rows[r:r + 1, :], (HEADDIM, N_STATE)) for r in range(HEADS_PER_GROUP)]
    return jnp.concatenate(parts, axis=0)


def _ssd_common(dtr, bias, alog):
    dt = _softplus(dtr + bias)
    A = -jnp.exp(alog)
    a = dt * A
    ki = lax.broadcasted_iota(jnp.int32, (CHUNK, CHUNK), 0)
    si = lax.broadcasted_iota(jnp.int32, (CHUNK, CHUNK), 1)
    upper = (ki <= si).astype(F32)
    cs = _dot_hi(a, upper)
    cs_last = jnp.broadcast_to(cs[:, CHUNK - 1:CHUNK], (8, CHUNK))
    return dt, A, a, cs, cs_last


def _decay_matrix(cs, r):
    li = lax.broadcasted_iota(jnp.int32, (CHUNK, CHUNK), 0)
    si = lax.broadcasted_iota(jnp.int32, (CHUNK, CHUNK), 1)
    causal = li >= si
    R = jnp.broadcast_to(cs[r:r + 1, :], (CHUNK, CHUNK))
    seg = jnp.where(causal, R.T - R, 0.0)
    return jnp.where(causal, jnp.exp(seg), 0.0)


def _ssd_in_specs(nc, rev):
    cix = (lambda c: nc - 1 - c) if rev else (lambda c: c)
    x_s = pl.BlockSpec((CHUNK, GW), lambda g, c: (cix(c), g))
    b_s = pl.BlockSpec((CHUNK, N_STATE), lambda g, c: (cix(c), D_SSM // N_STATE + g))
    c_s = pl.BlockSpec((CHUNK, N_STATE), lambda g, c: (cix(c), D_SSM // N_STATE + N_GROUPS + g))
    dtr_s = pl.BlockSpec((1, 8, CHUNK), lambda g, c: (g, 0, cix(c)))
    row_s = pl.BlockSpec((1, 8, CHUNK), lambda g, c: (g, 0, 0))
    drep_s = pl.BlockSpec((1, GW), lambda g, c: (0, g))
    hs_s = pl.BlockSpec((1, GW, N_STATE), lambda g, c: (cix(c), g, 0))
    return x_s, b_s, c_s, dtr_s, row_s, drep_s, hs_s


def _ssd_fwd(xbc, dtr, bias, alog, drep):
    T = xbc.shape[0]
    nc = T // CHUNK
    x_s, b_s, c_s, dtr_s, row_s, drep_s, hs_s = _ssd_in_specs(nc, False)

    def body(x_ref, b_ref, c_ref, dtr_ref, bias_ref, alog_ref, drep_ref, y_ref, hs_ref, h_scr):
        @pl.when(pl.program_id(1) == 0)
        def _():
            h_scr[...] = jnp.zeros_like(h_scr)

        x, Bm, Cm = x_ref[...], b_ref[...], c_ref[...]
        dt, A, a, cs, cs_last = _ssd_common(dtr_ref[0], bias_ref[0], alog_ref[0])
        E = _head_cols(jnp.exp(cs))
        W = _head_cols(jnp.exp(cs_last - cs) * dt)
        X = (x * _head_cols(dt)).astype(BF16)
        CB = _dot_nt(Cm, Bm)
        col = lax.broadcasted_iota(jnp.int32, (CHUNK, GW), 1) // HEADDIM
        y = jnp.zeros((CHUNK, GW), F32)
        for r in range(HEADS_PER_GROUP):
            M = CB * _decay_matrix(cs, r)
            y = y + jnp.where(col == r, _dot(M, X), 0.0)
        h = h_scr[...]
        hs_ref[0] = h
        y = y + _dot_nt(Cm, h) * E
        y_ref[...] = y + drep_ref[...] * x
        h_scr[...] = h * _head_rows(jnp.exp(cs_last)) + _dot_tn(x * W, Bm)

    return pl.pallas_call(
        body, name="ssd_fwd", grid=(N_GROUPS, nc),
        in_specs=[x_s, b_s, c_s, dtr_s, row_s, row_s, drep_s],
        out_specs=[x_s, hs_s],
        out_shape=[jax.ShapeDtypeStruct((T, D_SSM), F32), jax.ShapeDtypeStruct((nc, D_SSM, N_STATE), F32)],
        scratch_shapes=[pltpu.VMEM((GW, N_STATE), F32)],
        compiler_params=_cparams(("parallel", "arbitrary")),
    )(xbc, xbc, xbc, dtr, bias, alog, drep)


def _ssd_bwd(xbc, dtr, bias, alog, drep, dy, hs):
    T = xbc.shape[0]
    nc = T // CHUNK
    x_s, b_s, c_s, dtr_s, row_s, drep_s, hs_s = _ssd_in_specs(nc, True)
    bc_out = pl.BlockSpec((CHUNK, N_STATE), lambda g, c: (nc - 1 - c, g))

    def body(x_ref, b_ref, c_ref, dtr_ref, bias_ref, alog_ref, drep_ref, dy_ref, hs_ref,
             dx_ref, db_ref, dc_ref, ddtr_ref, dbias_ref, dalog_ref, dd_ref, dh_scr):
        @pl.when(pl.program_id(1) == 0)
        def _():
            dh_scr[...] = jnp.zeros_like(dh_scr)
            dbias_ref[...] = jnp.zeros_like(dbias_ref)
            dalog_ref[...] = jnp.zeros_like(dalog_ref)
            dd_ref[...] = jnp.zeros_like(dd_ref)

        x, Bm, Cm, dY = x_ref[...], b_ref[...], c_ref[...], dy_ref[...]
        dt, A, a, cs, cs_last = _ssd_common(dtr_ref[0], bias_ref[0], alog_ref[0])
        E = _head_cols(jnp.exp(cs))
        DT = _head_cols(dt)
        Wd = _head_cols(jnp.exp(cs_last - cs))
        X = x * DT
        h = hs_ref[0]
        dS = dh_scr[...]
        CB = _dot_nt(Cm, Bm)
        col = lax.broadcasted_iota(jnp.int32, (CHUNK, GW), 1) // HEADDIM
        rowid = lax.broadcasted_iota(jnp.int32, (8, CHUNK), 0)
        lane = lax.broadcasted_iota(jnp.int32, (8, CHUNK), 1)
        hsel = (lax.broadcasted_iota(jnp.int32, (8, GW), 1) // HEADDIM
                == lax.broadcasted_iota(jnp.int32, (8, GW), 0)).astype(F32)
        ones8 = jnp.ones((8, CHUNK), F32)

        dX = jnp.zeros((CHUNK, GW), F32)
        dCB = jnp.zeros((CHUNK, CHUNK), F32)
        dcs = jnp.zeros((8, CHUNK), F32)
        for r in range(HEADS_PER_GROUP):
            L = _decay_matrix(cs, r)
            M = CB * L
            G = _dot_nt(jnp.where(col == r, dY, 0.0), X)
            GL = G * L
            dCB = dCB + GL
            Wm = GL * CB
            colsum = jnp.sum(Wm, axis=0, keepdims=True)
            rowsum = _dot_nt_hi(ones8, Wm)
            dcs = dcs + jnp.where(rowid == r, rowsum - colsum, 0.0)
            dX = dX + jnp.where(col == r, _dot_tn(M, dY), 0.0)
        dC = _dot(dCB, Bm)
        dB = _dot_tn(dCB, Cm)
        T1 = _dot_nt(Bm, dS)
        dX = dX + T1 * Wd
        dB = dB + _dot(X * Wd, dS)
        pdec = _dot_nt_hi(hsel, X * T1 * Wd)
        dcs = dcs - pdec
        dlast = jnp.sum(pdec, axis=1, keepdims=True) \
            + jnp.exp(cs_last[:, 0:1]) * jnp.sum(_dot_hi(hsel, dS * h), axis=1, keepdims=True)
        dYE = dY * E
        dC = dC + _dot(dYE, h)
        yoff = _dot_nt(Cm, h) * E
        dcs = dcs + _dot_nt_hi(hsel, dY * yoff)
        dcs = dcs + jnp.where(lane == CHUNK - 1, dlast, 0.0)
        ki = lax.broadcasted_iota(jnp.int32, (CHUNK, CHUNK), 0)
        si = lax.broadcasted_iota(jnp.int32, (CHUNK, CHUNK), 1)
        lower = (ki >= si).astype(F32)
        da = _dot_hi(dcs, lower)
        ddt = da * A + _dot_nt_hi(hsel, dX * x)
        ddtr = ddt * _sigmoid(dtr_ref[0] + bias_ref[0])
        ddtr_ref[0] = ddtr
        dbias_ref[0] += ddtr
        dalog_ref[0] += da * a
        dx_ref[...] = dX * DT + drep_ref[...] * dY
        dd_ref[...] += jnp.sum(dY * x, axis=0, keepdims=True)
        db_ref[...] = dB
        dc_ref[...] = dC
        dh_scr[...] = dS * _head_rows(jnp.exp(cs_last)) + _dot_tn(dYE, Cm)

    return pl.pallas_call(
        body, name="ssd_bwd", grid=(N_GROUPS, nc),
        in_specs=[x_s, b_s, c_s, dtr_s, row_s, row_s, drep_s, x_s, hs_s],
        out_specs=[x_s, bc_out, bc_out, dtr_s, row_s, row_s, drep_s],
        out_shape=[jax.ShapeDtypeStruct((T, D_SSM), F32),
                   jax.ShapeDtypeStruct((T, N_GROUPS * N_STATE), F32),
                   jax.ShapeDtypeStruct((T, N_GROUPS * N_STATE), F32),
                   jax.ShapeDtypeStruct((N_GROUPS, 8, T), F32),
                   jax.ShapeDtypeStruct((N_GROUPS, 8, CHUNK), F32),
                   jax.ShapeDtypeStruct((N_GROUPS, 8, CHUNK), F32),
                   jax.ShapeDtypeStruct((1, D_SSM), F32)],
        scratch_shapes=[pltpu.VMEM((GW, N_STATE), F32)],
        compiler_params=_cparams(("parallel", "arbitrary")),
    )(xbc, xbc, xbc, dtr, bias, alog, drep, dy, hs)


def _adamw(w, g, m, v, name):
    R, C = w.shape
    tr = _tile(R, 256, 8)

    def body(w_ref, g_ref, m_ref, v_ref, d_ref, mo_ref, vo_ref):
        gv = g_ref[...]
        mn = ADAM_B1 * m_ref[...] + (1.0 - ADAM_B1) * gv
        vn = ADAM_B2 * v_ref[...] + (1.0 - ADAM_B2) * (gv * gv)
        m_hat = mn / (1.0 - ADAM_B1 ** ADAM_STEP)
        v_hat = vn / (1.0 - ADAM_B2 ** ADAM_STEP)
        d_ref[...] = -ADAM_LR * (m_hat / (jnp.sqrt(v_hat) + ADAM_EPS) + ADAM_WD * w_ref[...])
        mo_ref[...] = mn
        vo_ref[...] = vn

    spec = pl.BlockSpec((tr, C), lambda i: (i, 0))
    return pl.pallas_call(
        body, name=name, grid=(R // tr,),
        in_specs=[spec] * 4, out_specs=[spec] * 3,
        out_shape=[jax.ShapeDtypeStruct((R, C), F32)] * 3,
        compiler_params=_cparams(("parallel",)),
    )(w, g, m, v)


ANY = pl.BlockSpec(memory_space=pl.ANY)


def _place():
    x, y, c = lax.axis_index("x"), lax.axis_index("y"), lax.axis_index("c")
    return x, y, c


def _allgather_weights(shards):
    n = len(shards)

    def body(*refs):
        s_refs, o_refs = refs[:n], refs[n:2 * n]
        send_sems, recv_sems, local_sems = refs[2 * n:]
        x, y, c = _place()
        me, sibling = (x, y, c), (x, y, 1 - c)
        chips = [(1 - x, y), (x, 1 - y), (1 - x, 1 - y)]

        def blk(k, px, py, pc):
            return o_refs[k].at[4 * px + 2 * py + pc]

        def copy(k, slot, block, to, src=None):
            return pltpu.make_async_remote_copy(
                src_ref=blk(k, *block) if src is None else src, dst_ref=blk(k, *block),
                send_sem=send_sems.at[k, slot], recv_sem=recv_sems.at[k, slot],
                device_id=to, device_id_type=MESH)

        remote, local = [], []
        for k in range(n):
            mine_src = s_refs[k].at[c]
            loc = pltpu.make_async_copy(mine_src, blk(k, *me), local_sems.at[k])
            loc.start()
            local.append(loc)
            first = [copy(k, 0, me, sibling, src=mine_src)]
            first += [copy(k, 1 + j, me, (*chip, c), src=mine_src) for j, chip in enumerate(chips)]
            for cp in first:
                cp.start()
            remote += first
        for k in range(n):
            for j, chip in enumerate(chips):
                copy(k, 1 + j, (*chip, c), me).wait_recv()
                fwd = copy(k, 4 + j, (*chip, c), sibling)
                fwd.start()
                remote.append(fwd)
        for k in range(n):
            copy(k, 0, sibling, me).wait_recv()
            for j, chip in enumerate(chips):
                copy(k, 4 + j, (*chip, 1 - c), me).wait_recv()
        for cp in remote:
            cp.wait_send()
        for cp in local:
            cp.wait()

    return pl.pallas_call(
        body, name="allgather_weights",
        in_specs=[ANY] * n, out_specs=[ANY] * n,
        out_shape=[jax.ShapeDtypeStruct((N_DEV,) + s.shape[1:], s.dtype) for s in shards],
        scratch_shapes=[pltpu.SemaphoreType.DMA((n, 7)), pltpu.SemaphoreType.DMA((n, 7)),
                        pltpu.SemaphoreType.DMA((n,))],
    )(*shards)


def _allreduce_small(p):
    R, C = p.shape

    def body(p_ref, gath_ref, sum_ref, send_sems, recv_sems, local_sem):
        x, y, c = _place()
        me, sibling = (x, y, c), (x, y, 1 - c)
        chips = [(1 - x, y), (x, 1 - y), (1 - x, 1 - y)]

        def blk(px, py, pc):
            return gath_ref.at[4 * px + 2 * py + pc]

        def copy(k, block, to, src=None):
            return pltpu.make_async_remote_copy(
                src_ref=blk(*block) if src is None else src, dst_ref=blk(*block),
                send_sem=send_sems.at[k], recv_sem=recv_sems.at[k], device_id=to, device_id_type=MESH)

        mine = pltpu.make_async_copy(p_ref, blk(*me), local_sem)
        mine.start()
        first = [copy(0, me, sibling, src=p_ref)]
        first += [copy(1 + j, me, (*chip, c), src=p_ref) for j, chip in enumerate(chips)]
        for cp in first:
            cp.start()
        passed = [copy(4 + j, (*chip, c), sibling) for j, chip in enumerate(chips)]
        for j, chip in enumerate(chips):
            copy(1 + j, (*chip, c), me).wait_recv()
            passed[j].start()
        copy(0, sibling, me).wait_recv()
        for j, chip in enumerate(chips):
            copy(4 + j, (*chip, 1 - c), me).wait_recv()
        for cp in first + passed:
            cp.wait_send()
        mine.wait()
        s = gath_ref[0]
        for d in range(1, N_DEV):
            s = s + gath_ref[d]
        sum_ref[...] = s

    vm = pl.BlockSpec(memory_space=pltpu.VMEM)
    return pl.pallas_call(
        body, name="allreduce_small",
        in_specs=[vm], out_specs=[vm, vm],
        out_shape=[jax.ShapeDtypeStruct((N_DEV, R, C), F32), jax.ShapeDtypeStruct((R, C), F32)],
        scratch_shapes=[pltpu.SemaphoreType.DMA((7,)), pltpu.SemaphoreType.DMA((7,)), pltpu.SemaphoreType.DMA],
    )(p)[1]


def _rs_sibling_swap(ps):
    n = len(ps)

    def body(*refs):
        p_refs, o_refs = refs[:n], refs[n:2 * n]
        send_sems, recv_sems = refs[2 * n:]
        x, y, c = _place()
        cps = []
        for k in range(n):
            cp = pltpu.make_async_remote_copy(
                src_ref=p_refs[k].at[:, pl.ds(1 - c, 1)], dst_ref=o_refs[k],
                send_sem=send_sems.at[k], recv_sem=recv_sems.at[k],
                device_id=(x, y, 1 - c), device_id_type=MESH)
            cp.start()
            cps.append(cp)
        for cp in cps:
            cp.wait()

    return pl.pallas_call(
        body, name="rs_sibling_swap",
        in_specs=[ANY] * n, out_specs=[ANY] * n,
        out_shape=[jax.ShapeDtypeStruct((N_CHIPS, 1) + p.shape[2:], F32) for p in ps],
        scratch_shapes=[pltpu.SemaphoreType.DMA((n,)), pltpu.SemaphoreType.DMA((n,))],
    )(*ps)


def _rs_add_pair(p, r0, c_arr, name):
    _, _, hr, cols = p.shape
    tr = _tile(hr, 256, 8)

    def body(c_ref, p_ref, r_ref, q_ref):
        q_ref[...] = (p_ref[0] + r_ref[0]).astype(BF16)

    grid_spec = pltpu.PrefetchScalarGridSpec(
        num_scalar_prefetch=1, grid=(N_CHIPS, hr // tr),
        in_specs=[pl.BlockSpec((1, 1, tr, cols), lambda j, i, c_ref: (j, c_ref[0], i, 0)),
                  pl.BlockSpec((1, 1, tr, cols), lambda j, i, c_ref: (j, 0, i, 0))],
        out_specs=pl.BlockSpec((1, tr, cols), lambda j, i, c_ref: (j, i, 0)))
    return pl.pallas_call(
        body, name=name, grid_spec=grid_spec,
        out_shape=jax.ShapeDtypeStruct((N_CHIPS, hr, cols), BF16),
        compiler_params=_cparams(("parallel", "parallel")),
    )(c_arr, p, r0)


def _rs_chip_exchange(qs):
    n = len(qs)

    def body(*refs):
        q_refs, o_refs = refs[:n], refs[n:2 * n]
        send_sems, recv_sems, local_sems = refs[2 * n:]
        x, y, c = _place()
        me = 2 * x + y
        peers = [(x, 1 - y), (1 - x, y), (1 - x, 1 - y)]
        remote, local = [], []
        for k in range(n):
            loc = pltpu.make_async_copy(q_refs[k].at[me], o_refs[k].at[me], local_sems.at[k])
            loc.start()
            local.append(loc)
            for d, (px, py) in enumerate(peers):
                cp = pltpu.make_async_remote_copy(
                    src_ref=q_refs[k].at[2 * px + py], dst_ref=o_refs[k].at[me],
                    send_sem=send_sems.at[k, d], recv_sem=recv_sems.at[k, d],
                    device_id=(px, py, c), device_id_type=MESH)
                cp.start()
                remote.append(cp)
        for k in range(n):
            for d, (px, py) in enumerate(peers):
                pltpu.make_async_remote_copy(
                    src_ref=q_refs[k].at[me], dst_ref=o_refs[k].at[2 * px + py],
                    send_sem=send_sems.at[k, d], recv_sem=recv_sems.at[k, d],
                    device_id=(px, py, c), device_id_type=MESH).wait_recv()
        for cp in remote:
            cp.wait_send()
        for cp in local:
            cp.wait()

    return pl.pallas_call(
        body, name="rs_chip_exchange",
        in_specs=[ANY] * n, out_specs=[ANY] * n,
        out_shape=[jax.ShapeDtypeStruct(q.shape, q.dtype) for q in qs],
        scratch_shapes=[pltpu.SemaphoreType.DMA((n, 3)), pltpu.SemaphoreType.DMA((n, 3)),
                        pltpu.SemaphoreType.DMA((n,))],
    )(*qs)


def _rs_add_chips(r1, name):
    _, hr, cols = r1.shape
    tr = _tile(hr, 256, 8)

    def body(r_ref, o_ref):
        s = r_ref[0].astype(F32)
        for j in range(1, N_CHIPS):
            s = s + r_ref[j].astype(F32)
        o_ref[...] = s

    return pl.pallas_call(
        body, name=name, grid=(hr // tr,),
        in_specs=[pl.BlockSpec((N_CHIPS, tr, cols), lambda i: (0, i, 0))],
        out_specs=pl.BlockSpec((tr, cols), lambda i: (i, 0)),
        out_shape=jax.ShapeDtypeStruct((hr, cols), F32),
        compiler_params=_cparams(("parallel",)),
    )(r1)


def _rs_sibling_share(reds):
    n = len(reds)

    def body(*refs):
        r_refs, o_refs = refs[:n], refs[n:2 * n]
        send_sems, recv_sems, local_sems = refs[2 * n:]
        x, y, c = _place()
        cps = []
        for k in range(n):
            loc = pltpu.make_async_copy(r_refs[k], o_refs[k].at[c], local_sems.at[k])
            loc.start()
            cp = pltpu.make_async_remote_copy(
                src_ref=r_refs[k], dst_ref=o_refs[k].at[c],
                send_sem=send_sems.at[k], recv_sem=recv_sems.at[k],
                device_id=(x, y, 1 - c), device_id_type=MESH)
            cp.start()
            cps += [cp, loc]
        for cp in cps:
            cp.wait()

    return pl.pallas_call(
        body, name="rs_sibling_share",
        in_specs=[ANY] * n, out_specs=[ANY] * n,
        out_shape=[jax.ShapeDtypeStruct((2,) + r.shape, F32) for r in reds],
        scratch_shapes=[pltpu.SemaphoreType.DMA((n,)), pltpu.SemaphoreType.DMA((n,)),
                        pltpu.SemaphoreType.DMA((n,))],
    )(*reds)


def _pad_rows(a, rows):
    return jnp.pad(a, ((0, rows - a.shape[0]), (0, 0)))


def _pad_cols(a, cols):
    return jnp.pad(a, ((0, 0), (0, cols - a.shape[1])))


def _heads_to_rows(v):
    v = v.reshape(N_GROUPS, HEADS_PER_GROUP, 1)
    v = jnp.pad(v, ((0, 0), (0, 8 - HEADS_PER_GROUP), (0, 0)))
    return jnp.broadcast_to(v, (N_GROUPS, 8, CHUNK))


def _rows_to_heads(a):
    return jnp.sum(a[:, :HEADS_PER_GROUP, :], axis=-1).reshape(N_HEADS)


def kernel(x, norm_mix_g, w_in, ssm_conv_w, ssm_conv_b, ssm_dt_bias, ssm_A_log, ssm_D, ssm_norm_g, sc_conv_w, w_out, norm_ffn_g, w_gate, w_up, w_down, norm_final_g, loss_target, m_norm_mix_g, m_w_in, m_ssm_conv_w, m_ssm_conv_b, m_ssm_dt_bias, m_ssm_A_log, m_ssm_D, m_ssm_norm_g, m_sc_conv_w, m_w_out, m_norm_ffn_g, m_w_gate, m_w_up, m_w_down, m_norm_final_g, v_norm_mix_g, v_w_in, v_ssm_conv_w, v_ssm_conv_b, v_ssm_dt_bias, v_ssm_A_log, v_ssm_D, v_ssm_norm_g, v_sc_conv_w, v_w_out, v_norm_ffn_g, v_w_gate, v_w_up, v_w_down, v_norm_final_g):
    T = x.shape[1]
    xt = x[0]
    tgt = loss_target[0]
    cx, cy, cc = lax.axis_index("x"), lax.axis_index("y"), lax.axis_index("c")
    chip = 2 * cx + cy
    c_arr = jnp.reshape(cc, (1,)).astype(jnp.int32)

    big = [w_in[0], w_out[0], w_gate[0], w_up[0], w_down[0]]
    names = ["w_in", "w_out", "w_gate", "w_up", "w_down"]
    shards16 = [_cast_bf16(w, "cast_" + nm).reshape(2, w.shape[0] // 2, w.shape[1]) for w, nm in zip(big, names)]
    gath = _allgather_weights(shards16)
    gath = [g.reshape(N_CHIPS, w.shape[0], w.shape[1]) for g, w in zip(gath, big)]
    w_in_f = gath[0].transpose(1, 0, 2).reshape(D_MODEL, D_IN)
    w_main = jnp.concatenate([w_in_f[:, :OFF_CB], w_in_f[:, OFF_CB + N_HEADS:]], axis=1)
    w_dt = _pad_cols(w_in_f[:, OFF_CB:OFF_CB + N_HEADS], DT_PAD)
    w_out_f = gath[1].reshape(2 * D_MODEL, D_MODEL)
    w_gate_f = gath[2].transpose(1, 0, 2).reshape(D_MODEL, D_FF)
    w_up_f = gath[3].transpose(1, 0, 2).reshape(D_MODEL, D_FF)
    w_down_f = gath[4].reshape(D_FF, D_MODEL)

    contrib = (cc == 0).astype(F32)
    place_ssm = jnp.zeros((8, D_XBC), F32)
    place_ssm = lax.dynamic_update_slice(place_ssm, _pad_rows(ssm_conv_w[0], 8) * contrib, (0, chip * (D_XBC // N_CHIPS)))
    place_sc = jnp.zeros((8, D_MODEL), F32)
    place_sc = lax.dynamic_update_slice(place_sc, _pad_rows(sc_conv_w[0], 8) * contrib, (0, chip * (D_MODEL // N_CHIPS)))
    convs = _allreduce_small(jnp.concatenate([place_ssm, _pad_cols(place_sc, D_XBC)], axis=0))
    ssm_w8 = convs[:8]
    sc_w8 = convs[8:, :D_MODEL]

    bias_rows = _heads_to_rows(ssm_dt_bias[0])
    alog_rows = _heads_to_rows(ssm_A_log[0])
    drep = jnp.repeat(ssm_D[0], HEADDIM).reshape(1, D_SSM)

    n1 = _rmsnorm_fwd(xt, norm_mix_g, "rmsnorm_mix")
    (proj,) = _matmul([(n1, w_main)], out_dtypes=[F32], name="mm_proj")
    (dt_raw,) = _matmul([(n1, w_dt)], out_dtypes=[F32], name="mm_proj_dt", tk=2048)
    xbc = _ssm_conv_fwd(proj, ssm_w8, ssm_conv_b)
    dtr = jnp.pad(dt_raw[:, :N_HEADS].T.reshape(N_GROUPS, HEADS_PER_GROUP, T), ((0, 0), (0, 4), (0, 0)))
    y_ssd, hs = _ssd_fwd(xbc, dtr, bias_rows, alog_rows, drep)
    y_ssm = _gated_norm_fwd(y_ssd, proj, ssm_norm_g)
    y_sc = _shortconv_fwd(proj, sc_w8)
    y_mix = jnp.concatenate([y_ssm, y_sc], axis=1)
    (h1,) = _matmul([(y_mix, w_out_f)], out_dtypes=[F32], name="mm_out", extras=[xt],
                    epilogue=lambda acc, res: (acc + res,))
    n2 = _rmsnorm_fwd(h1, norm_ffn_g, "rmsnorm_ffn")
    g_act, u_act, a_act = _ffn_fwd(n2, w_gate_f, w_up_f)
    (h2,) = _matmul([(a_act, w_down_f)], out_dtypes=[F32], name="mm_down", extras=[h1],
                    epilogue=lambda acc, res: (acc + res,))

    dh2, dh2b, dg_final, loss_part = _loss_and_final_bwd(h2, tgt, norm_final_g.reshape(1, D_MODEL))
    dg_act, du_act = _matmul([(dh2b, w_down_f)], tb=True, out_dtypes=[BF16, BF16], name="mm_down_bwd",
                             tn=512, extras=[g_act, u_act], epilogue=_swiglu_bwd)
    (dw_down,) = _matmul([(a_act, dh2b)], ta=True, out_dtypes=[F32], name="mm_dw_down", tm=1408)
    (dn2,) = _matmul([(dg_act, w_gate_f), (du_act, w_up_f)], tb=True, out_dtypes=[BF16], name="mm_ffn_in_bwd")
    (dw_gate,) = _matmul([(n2, dg_act)], ta=True, out_dtypes=[F32], name="mm_dw_gate", tn=1408)
    (dw_up,) = _matmul([(n2, du_act)], ta=True, out_dtypes=[F32], name="mm_dw_up", tn=1408)
    dh1, dh1b, dg_ffn = _rmsnorm_bwd(dn2, h1, norm_ffn_g, dh2, "rmsnorm_ffn_bwd")
    (dmix,) = _matmul([(dh1b, w_out_f)], tb=True, out_dtypes=[BF16], name="mm_out_bwd")
    (dw_out,) = _matmul([(y_mix, dh1b)], ta=True, out_dtypes=[F32], name="mm_dw_out")
    dgb, dgc, du_sc, dw_sc = _shortconv_bwd(dmix, proj, sc_w8)
    dy_ssd, dz, dg_ssmnorm = _gated_norm_bwd(dmix, y_ssd, proj, ssm_norm_g)
    dxs, dB, dC, ddtr, dbias_acc, dalog_acc, dD_acc = _ssd_bwd(xbc, dtr, bias_rows, alog_rows, drep, dy_ssd, hs)
    dxbc, dw_ssmconv, db_ssmconv = _ssm_conv_bwd(jnp.concatenate([dxs, dB, dC], axis=1), proj, ssm_w8, ssm_conv_b)
    dproj = jnp.concatenate([dz, dxbc, dgb, dgc, du_sc], axis=1)
    ddt_raw = _pad_cols(ddtr[:, :HEADS_PER_GROUP, :].reshape(N_HEADS, T).T, DT_PAD).astype(BF16)
    (dn1a,) = _matmul([(dproj, w_main)], tb=True, out_dtypes=[F32], name="mm_proj_bwd")
    (dn1,) = _matmul([(ddt_raw, w_dt)], tb=True, out_dtypes=[BF16], name="mm_proj_dt_bwd", extras=[dn1a],
                     epilogue=lambda acc, res: (acc + res,))
    (dw_main,) = _matmul([(n1, dproj)], ta=True, out_dtypes=[F32], name="mm_dw_main")
    (dw_dt,) = _matmul([(n1, ddt_raw)], ta=True, out_dtypes=[F32], name="mm_dw_dt", tk=2048)
    dx, _, dg_mix = _rmsnorm_bwd(dn1, xt, norm_mix_g, dh1, "rmsnorm_mix_bwd")

    dw_in_full = jnp.concatenate([dw_main[:, :OFF_CB], dw_dt[:, :N_HEADS], dw_main[:, OFF_CB:]], axis=1)

    def col_blocks(g):
        R, Ctot = g.shape
        return g.reshape(R, N_CHIPS, Ctot // N_CHIPS).transpose(1, 0, 2).reshape(N_CHIPS, 2, R // 2, Ctot // N_CHIPS)

    def row_blocks(g):
        Rtot, C = g.shape
        return g.reshape(N_CHIPS, 2, Rtot // N_CHIPS // 2, C)

    ps = [col_blocks(dw_in_full), row_blocks(dw_out), col_blocks(dw_gate), col_blocks(dw_up), row_blocks(dw_down)]
    r0 = _rs_sibling_swap(ps)
    qs = [_rs_add_pair(p, r, c_arr, "rs_add_pair_" + nm) for p, r, nm in zip(ps, r0, names)]
    r1 = _rs_chip_exchange(qs)
    reds = [_rs_add_chips(r, "rs_add_chips_" + nm) for r, nm in zip(r1, names)]
    gfull = _rs_sibling_share(reds)
    big_grads = [g.reshape(w.shape) for g, w in zip(gfull, big)]

    dD = jnp.sum(dD_acc.reshape(N_HEADS, HEADDIM), axis=-1)
    heads_row = jnp.concatenate([_rows_to_heads(dbias_acc), _rows_to_heads(dalog_acc), dD,
                                 loss_part.reshape(1)]).reshape(1, -1)
    small = jnp.concatenate([
        dw_ssmconv,
        _pad_cols(dw_sc, D_XBC),
        db_ssmconv,
        jnp.concatenate([dg_mix, dg_ssmnorm], axis=1),
        jnp.concatenate([dg_ffn, dg_final], axis=1),
        _pad_cols(heads_row, D_XBC),
        jnp.zeros((4, D_XBC), F32),
    ], axis=0)
    tot = _allreduce_small(small)
    loss = tot[19, 3 * N_HEADS]

    cs_ssm, cs_sc = D_XBC // N_CHIPS, D_MODEL // N_CHIPS
    g_ssm_conv = lax.dynamic_slice(tot[0:K_SSM], (0, chip * cs_ssm), (K_SSM, cs_ssm))
    g_sc_conv = lax.dynamic_slice(tot[8:8 + K_SC, :D_MODEL], (0, chip * cs_sc), (K_SC, cs_sc))
    small_grads = {
        "norm_mix_g": tot[17:18, :D_MODEL], "ssm_conv_w": g_ssm_conv, "ssm_conv_b": tot[16:17],
        "ssm_dt_bias": tot[19:20, 0:N_HEADS], "ssm_A_log": tot[19:20, N_HEADS:2 * N_HEADS],
        "ssm_D": tot[19:20, 2 * N_HEADS:3 * N_HEADS], "ssm_norm_g": tot[17:18, D_MODEL:],
        "sc_conv_w": g_sc_conv, "norm_ffn_g": tot[18:19, :D_MODEL], "norm_final_g": tot[18:19, D_MODEL:],
    }
    small_w = {"norm_mix_g": (norm_mix_g, m_norm_mix_g, v_norm_mix_g),
               "ssm_conv_w": (ssm_conv_w[0], m_ssm_conv_w[0], v_ssm_conv_w[0]),
               "ssm_conv_b": (ssm_conv_b, m_ssm_conv_b, v_ssm_conv_b),
               "ssm_dt_bias": (ssm_dt_bias, m_ssm_dt_bias, v_ssm_dt_bias),
               "ssm_A_log": (ssm_A_log, m_ssm_A_log, v_ssm_A_log),
               "ssm_D": (ssm_D, m_ssm_D, v_ssm_D),
               "ssm_norm_g": (ssm_norm_g, m_ssm_norm_g, v_ssm_norm_g),
               "sc_conv_w": (sc_conv_w[0], m_sc_conv_w[0], v_sc_conv_w[0]),
               "norm_ffn_g": (norm_ffn_g, m_norm_ffn_g, v_norm_ffn_g),
               "norm_final_g": (norm_final_g.reshape(1, -1), m_norm_final_g.reshape(1, -1),
                                v_norm_final_g.reshape(1, -1))}
    PW = 1024
    order = list(small_w)

    def pack(arrs):
        rows = []
        for a in arrs:
            flat = a.reshape(-1)
            n = -(-flat.shape[0] // PW) * PW
            rows.append(jnp.pad(flat, (0, n - flat.shape[0])).reshape(-1, PW))
        slab = jnp.concatenate(rows, axis=0)
        return _pad_rows(slab, -(-slab.shape[0] // 8) * 8)

    wp = pack([small_w[k][0] for k in order])
    mp = pack([small_w[k][1] for k in order])
    vp = pack([small_w[k][2] for k in order])
    gp = pack([small_grads[k] for k in order])
    sd, sm, sv = _adamw(wp, gp, mp, vp, "adamw_small")

    def unpack(slab):
        out, row = {}, 0
        for k in order:
            shape = small_w[k][0].shape
            size = 1
            for s in shape:
                size *= s
            nr = -(-size // PW)
            out[k] = slab[row:row + nr].reshape(-1)[:size].reshape(shape)
            row += nr
        return out

    s_delta, s_m, s_v = unpack(sd), unpack(sm), unpack(sv)

    big_m = [m_w_in[0], m_w_out[0], m_w_gate[0], m_w_up[0], m_w_down[0]]
    big_v = [v_w_in[0], v_w_out[0], v_w_gate[0], v_w_up[0], v_w_down[0]]
    big_out = {nm: _adamw(w, g, m, v, "adamw_" + nm) for nm, w, g, m, v in zip(names, big, big_grads, big_m, big_v)}
    big_g = dict(zip(names, big_grads))

    weight_order = ["norm_mix_g", "w_in", "ssm_conv_w", "ssm_conv_b", "ssm_dt_bias", "ssm_A_log", "ssm_D",
                    "ssm_norm_g", "sc_conv_w", "w_out", "norm_ffn_g", "w_gate", "w_up", "w_down", "norm_final_g"]
    lead = {"ssm_conv_w", "sc_conv_w", "w_in", "w_out", "w_gate", "w_up", "w_down"}

    def shaped(nm, a):
        if nm == "norm_final_g":
            return a.reshape(D_MODEL)
        return a[None] if nm in lead else a

    grads, deltas, new_m, new_v = [], [], [], []
    for nm in weight_order:
        if nm in big_out:
            g, (d, m, v) = big_g[nm], big_out[nm]
        else:
            g, d, m, v = small_grads[nm], s_delta[nm], s_m[nm], s_v[nm]
        grads.append(shaped(nm, g))
        deltas.append(shaped(nm, d))
        new_m.append(shaped(nm, m))
        new_v.append(shaped(nm, v))
    return (loss, dx[None], *grads, *deltas, *new_m, *new_v)


def _swiglu_bwd(da, g, u):
    gf, uf = g.astype(F32), u.astype(F32)
    return da * uf * _dsilu(gf), da * _silu(gf)


def _ffn_fwd(n2, w_gate, w_up):
    T, K = n2.shape
    N = w_gate.shape[1]
    tm, tn, tk = _tile(T, 2048), _tile(N, 512), _tile(K, 512)
    nk = K // tk

    def body(a_ref, wg_ref, wu_ref, g_ref, u_ref, act_ref, accg, accu):
        k = pl.program_id(2)

        @pl.when(k == 0)
        def _():
            accg[...] = jnp.zeros_like(accg)
            accu[...] = jnp.zeros_like(accu)

        a = a_ref[...]
        accg[...] += jnp.dot(a, wg_ref[...], preferred_element_type=F32)
        accu[...] += jnp.dot(a, wu_ref[...], preferred_element_type=F32)

        @pl.when(k == nk - 1)
        def _():
            g, u = accg[...], accu[...]
            g_ref[...] = g.astype(BF16)
            u_ref[...] = u.astype(BF16)
            act_ref[...] = (_silu(g) * u).astype(BF16)

    a_spec = pl.BlockSpec((tm, tk), lambda i, j, k: (i, k))
    b_spec = pl.BlockSpec((tk, tn), lambda i, j, k: (k, j))
    o_spec = pl.BlockSpec((tm, tn), lambda i, j, k: (i, j))
    return pl.pallas_call(
        body, name="ffn_fwd", grid=(T // tm, N // tn, nk),
        in_specs=[a_spec, b_spec, b_spec], out_specs=[o_spec] * 3,
        out_shape=[jax.ShapeDtypeStruct((T, N), BF16)] * 3,
        scratch_shapes=[pltpu.VMEM((tm, tn), F32), pltpu.VMEM((tm, tn), F32)],
        compiler_params=_cparams(("parallel", "parallel", "arbitrary")),
    )(n2, w_gate, w_up)
```

```python
import functools

import jax
import jax.numpy as jnp
from jax import lax
from jax.experimental import pallas as pl
from jax.experimental.pallas import tpu as pltpu

F32 = jnp.float32
BF16 = jnp.bfloat16
MESH = pl.DeviceIdType.MESH

D_MODEL = 2048
D_SSM = 2048
HEADDIM = 64
N_HEADS = 32
N_GROUPS = 8
HEADS_PER_GROUP = 4
N_STATE = 128
CHUNK = 128
K_SSM = 4
K_SC = 3
D_XBC = 4096
D_FF = 5632
D_IN = 12320
D_MAIN = 12288
OFF_XBC, OFF_CB, OFF_CC, OFF_CX = 2048, 6144, 8192, 10240
DT_PAD = 128
EPS = 1e-5
N_CHIPS = 4
N_DEV = 8

ADAM_LR = 0.001
ADAM_B1 = 0.9
ADAM_B2 = 0.999
ADAM_EPS = 1e-08
ADAM_WD = 0.01
ADAM_STEP = 10

V7X_VMEM_BYTES = 64 * 1024 * 1024
VMEM_LIMIT = V7X_VMEM_BYTES - 8 * 1024 * 1024


def _cparams(sem=None):
    if sem is None:
        return pltpu.CompilerParams(vmem_limit_bytes=VMEM_LIMIT)
    return pltpu.CompilerParams(dimension_semantics=sem, vmem_limit_bytes=VMEM_LIMIT)


def _tile(dim, pref, unit=128):
    best = None
    t = unit
    while t <= min(dim, pref):
        if dim % t == 0:
            best = t
        t += unit
    return best if best is not None else dim


def _sigmoid(x):
    return 1.0 / (1.0 + jnp.exp(-x))


def _silu(x):
    return x * _sigmoid(x)


def _dsilu(x):
    s = _sigmoid(x)
    return s * (1.0 + x * (1.0 - s))


def _softplus(x):
    return jnp.maximum(x, 0.0) + jnp.log(1.0 + jnp.exp(-jnp.abs(x)))


def _matmul(pairs, *, ta=False, tb=False, out_dtypes, name, tm=1024, tn=1024, tk=512, extras=(), epilogue=None,
            deps=(), col_shards=False):
    a0, b0 = pairs[0]
    M, K = (a0.shape[1], a0.shape[0]) if ta else a0.shape
    N = b0.shape[0] if tb else b0.shape[1]
    tm, tn, tk = _tile(M, tm, 8 if M % 128 else 128), _tile(N, tn), _tile(K, tk)
    npair, nex, ndep, nout = len(pairs), len(extras), len(deps), len(out_dtypes)
    nk = K // tk
    dims = (((0 if ta else 1,), (1 if tb else 0,)), ((), ()))

    def body(*refs):
        a_refs = refs[0:2 * npair:2]
        b_refs = refs[1:2 * npair:2]
        ex_refs = refs[2 * npair:2 * npair + nex]
        o_refs = refs[2 * npair + nex + ndep:2 * npair + nex + ndep + nout]
        acc = refs[-1]
        k = pl.program_id(2)

        @pl.when(k == 0)
        def _():
            acc[...] = jnp.zeros_like(acc)

        s = None
        for a_ref, b_ref in zip(a_refs, b_refs):
            d = lax.dot_general(a_ref[...], b_ref[...], dims, preferred_element_type=F32)
            s = d if s is None else s + d
        acc[...] += s

        @pl.when(k == nk - 1)
        def _():
            r = acc[...]
            outs = (r,) if epilogue is None else epilogue(r, *[e[...] for e in ex_refs])
            for o_ref, o in zip(o_refs, outs):
                o_ref[...] = o.astype(o_ref.dtype)

    a_spec = pl.BlockSpec((tk, tm), lambda i, j, k: (k, i)) if ta else pl.BlockSpec((tm, tk), lambda i, j, k: (i, k))
    b_spec = pl.BlockSpec((tn, tk), lambda i, j, k: (j, k)) if tb else pl.BlockSpec((tk, tn), lambda i, j, k: (k, j))
    e_spec = pl.BlockSpec((tm, tn), lambda i, j, k: (i, j))
    if col_shards:
        o_spec = pl.BlockSpec((None, tm, tn), lambda i, j, k: (j, i, 0))
        o_shape = (N // tn, M, tn)
    else:
        o_spec, o_shape = e_spec, (M, N)
    args, in_specs = [], []
    for a, b in pairs:
        args += [a, b]
        in_specs += [a_spec, b_spec]
    args += list(extras) + list(deps)
    in_specs += [e_spec] * nex + [ANY] * ndep
    outs = pl.pallas_call(
        body,
        name=name,
        grid=(M // tm, N // tn, nk),
        in_specs=in_specs,
        out_specs=[o_spec] * nout,
        out_shape=[jax.ShapeDtypeStruct(o_shape, dt) for dt in out_dtypes],
        scratch_shapes=[pltpu.VMEM((tm, tn), F32)],
        compiler_params=_cparams(("parallel", "parallel", "arbitrary")),
    )(*args)
    return outs


def _cast_into_gather(w, chip_arr, name):
    R, C = w.shape
    hr = R // 2
    tr = _tile(hr, 512, 8)
    nb = hr // tr

    def body(chip_ref, w_ref, o_ref):
        o_ref[...] = w_ref[...].astype(BF16)

    grid_spec = pltpu.PrefetchScalarGridSpec(
        num_scalar_prefetch=1, grid=(2, nb),
        in_specs=[pl.BlockSpec((tr, C), lambda h, i, chip_ref: (h * nb + i, 0))],
        out_specs=pl.BlockSpec((None, tr, C), lambda h, i, chip_ref: (2 * chip_ref[0] + h, i, 0)))
    return pl.pallas_call(
        body, name=name, grid_spec=grid_spec,
        out_shape=jax.ShapeDtypeStruct((N_DEV, hr, C), BF16),
        compiler_params=_cparams(("parallel", "parallel")),
    )(chip_arr, w)


def _tie(small, token, name):
    def body(s_ref, t_ref, o_ref):
        o_ref[...] = s_ref[...]

    vm = pl.BlockSpec(memory_space=pltpu.VMEM)
    return pl.pallas_call(body, name=name, in_specs=[vm, ANY], out_specs=vm,
                          out_shape=jax.ShapeDtypeStruct(small.shape, small.dtype))(small, token)


def _rmsnorm_fwd(x, g, name):
    T, D = x.shape
    tt = _tile(T, 256)

    def body(x_ref, g_ref, n_ref):
        xv = x_ref[...]
        r = lax.rsqrt(jnp.mean(xv * xv, axis=-1, keepdims=True) + EPS)
        n_ref[...] = (xv * r * g_ref[...]).astype(BF16)

    return pl.pallas_call(
        body, name=name, grid=(T // tt,),
        in_specs=[pl.BlockSpec((tt, D), lambda i: (i, 0)), pl.BlockSpec((1, D), lambda i: (0, 0))],
        out_specs=pl.BlockSpec((tt, D), lambda i: (i, 0)),
        out_shape=jax.ShapeDtypeStruct((T, D), BF16),
        compiler_params=_cparams(("parallel",)),
    )(x, g)


def _rmsnorm_bwd(dn, x, g, res, name):
    T, D = x.shape
    tt = _tile(T, 256)

    def body(dn_ref, x_ref, g_ref, res_ref, dx_ref, dxb_ref, dg_ref):
        @pl.when(pl.program_id(0) == 0)
        def _():
            dg_ref[...] = jnp.zeros_like(dg_ref)

        xv = x_ref[...]
        dy = dn_ref[...].astype(F32)
        r = lax.rsqrt(jnp.mean(xv * xv, axis=-1, keepdims=True) + EPS)
        xhat = xv * r
        dxh = dy * g_ref[...]
        dx = res_ref[...] + r * (dxh - xhat * jnp.mean(dxh * xhat, axis=-1, keepdims=True))
        dx_ref[...] = dx
        dxb_ref[...] = dx.astype(BF16)
        dg_ref[...] += jnp.sum(dy * xhat, axis=0, keepdims=True)

    tok = pl.BlockSpec((tt, D), lambda i: (i, 0))
    vec = pl.BlockSpec((1, D), lambda i: (0, 0))
    return pl.pallas_call(
        body, name=name, grid=(T // tt,),
        in_specs=[tok, tok, vec, tok],
        out_specs=[tok, tok, vec],
        out_shape=[jax.ShapeDtypeStruct((T, D), F32), jax.ShapeDtypeStruct((T, D), BF16),
                   jax.ShapeDtypeStruct((1, D), F32)],
        compiler_params=_cparams(("arbitrary",)),
    )(dn, x, g, res)


def _loss_and_final_bwd(h2, target, gf):
    T, D = h2.shape
    tt = _tile(T, 256)

    def body(h_ref, t_ref, g_ref, dh_ref, dhb_ref, dg_ref, loss_ref):
        @pl.when(pl.program_id(0) == 0)
        def _():
            dg_ref[...] = jnp.zeros_like(dg_ref)
            loss_ref[...] = jnp.zeros_like(loss_ref)

        xv = h_ref[...]
        r = lax.rsqrt(jnp.mean(xv * xv, axis=-1, keepdims=True) + EPS)
        xhat = xv * r
        err = xhat * g_ref[...] - t_ref[...]
        loss_ref[...] += 0.5 * jnp.sum(jnp.mean(err * err, axis=-1, keepdims=True), axis=0, keepdims=True)
        dy = err * (1.0 / D)
        dxh = dy * g_ref[...]
        dx = r * (dxh - xhat * jnp.mean(dxh * xhat, axis=-1, keepdims=True))
        dh_ref[...] = dx
        dhb_ref[...] = dx.astype(BF16)
        dg_ref[...] += jnp.sum(dy * xhat, axis=0, keepdims=True)

    tok = pl.BlockSpec((tt, D), lambda i: (i, 0))
    vec = pl.BlockSpec((1, D), lambda i: (0, 0))
    return pl.pallas_call(
        body, name="loss_final_bwd", grid=(T // tt,),
        in_specs=[tok, tok, vec],
        out_specs=[tok, tok, vec, pl.BlockSpec((1, 1), lambda i: (0, 0))],
        out_shape=[jax.ShapeDtypeStruct((T, D), F32), jax.ShapeDtypeStruct((T, D), BF16),
                   jax.ShapeDtypeStruct((1, D), F32), jax.ShapeDtypeStruct((1, 1), F32)],
        compiler_params=_cparams(("arbitrary",)),
    )(h2, target, gf)


def _gated_norm_fwd(y, proj, g):
    T, D = y.shape
    tt = _tile(T, 256)

    def body(y_ref, z_ref, g_ref, o_ref):
        yg = y_ref[...] * _silu(z_ref[...])
        r = lax.rsqrt(jnp.mean(yg * yg, axis=-1, keepdims=True) + EPS)
        o_ref[...] = (yg * r * g_ref[...]).astype(BF16)

    tok = pl.BlockSpec((tt, D), lambda i: (i, 0))
    return pl.pallas_call(
        body, name="gated_norm_fwd", grid=(T // tt,),
        in_specs=[tok, tok, pl.BlockSpec((1, D), lambda i: (0, 0))],
        out_specs=tok,
        out_shape=jax.ShapeDtypeStruct((T, D), BF16),
        compiler_params=_cparams(("parallel",)),
    )(y, proj, g)


def _gated_norm_bwd(dmix, y, proj, g):
    T, D = y.shape
    tt = _tile(T, 256)

    def body(do_ref, y_ref, z_ref, g_ref, dy_ref, dz_ref, dg_ref):
        @pl.when(pl.program_id(0) == 0)
        def _():
            dg_ref[...] = jnp.zeros_like(dg_ref)

        yv, zv = y_ref[...], z_ref[...]
        do = do_ref[...].astype(F32)
        sz = _silu(zv)
        yg = yv * sz
        r = lax.rsqrt(jnp.mean(yg * yg, axis=-1, keepdims=True) + EPS)
        xhat = yg * r
        dxh = do * g_ref[...]
        dyg = r * (dxh - xhat * jnp.mean(dxh * xhat, axis=-1, keepdims=True))
        dy_ref[...] = dyg * sz
        dz_ref[...] = (dyg * yv * _dsilu(zv)).astype(BF16)
        dg_ref[...] += jnp.sum(do * xhat, axis=0, keepdims=True)

    tok = pl.BlockSpec((tt, D), lambda i: (i, 0))
    vec = pl.BlockSpec((1, D), lambda i: (0, 0))
    return pl.pallas_call(
        body, name="gated_norm_bwd", grid=(T // tt,),
        in_specs=[tok, tok, tok, vec],
        out_specs=[tok, tok, vec],
        out_shape=[jax.ShapeDtypeStruct((T, D), F32), jax.ShapeDtypeStruct((T, D), BF16),
                   jax.ShapeDtypeStruct((1, D), F32)],
        compiler_params=_cparams(("arbitrary",)),
    )(dmix, y, proj, g)


HALO = 8


def _shift_down(cur, prev8, s):
    ext = jnp.concatenate([prev8, cur], axis=0)
    return pltpu.roll(ext, s, axis=0)[HALO:]


def _shift_up(cur, next8, s):
    n = cur.shape[0]
    ext = jnp.concatenate([cur, next8], axis=0)
    return pltpu.roll(ext, n + HALO - s, axis=0)[:n]


def _conv_specs(tt, cb, col_off_blocks, nt):
    hb = tt // HALO
    cur = pl.BlockSpec((tt, cb), lambda j, i: (i, col_off_blocks + j))
    prev = pl.BlockSpec((HALO, cb), lambda j, i: (jnp.maximum(i * hb - 1, 0), col_off_blocks + j))
    nxt = pl.BlockSpec((HALO, cb), lambda j, i: (jnp.minimum((i + 1) * hb, nt * hb - 1), col_off_blocks + j))
    return cur, prev, nxt


def _causal_conv(cur, prev8, w, K):
    y = cur * w[K - 1:K, :]
    for k in range(K - 1):
        y = y + _shift_down(cur, prev8, K - 1 - k) * w[k:k + 1, :]
    return y


def _anticausal_conv(cur, next8, w, K):
    y = cur * w[K - 1:K, :]
    for k in range(K - 1):
        y = y + _shift_up(cur, next8, K - 1 - k) * w[k:k + 1, :]
    return y


def _ssm_conv_fwd(proj, w8, b):
    T = proj.shape[0]
    tt, cb = _tile(T, 512), 512
    nt = T // tt
    cur, prev, _ = _conv_specs(tt, cb, OFF_XBC // cb, nt)

    def body(u_ref, up_ref, w_ref, b_ref, o_ref):
        first = pl.program_id(1) == 0
        p8 = jnp.where(first, 0.0, up_ref[...])
        pre = _causal_conv(u_ref[...], p8, w_ref[...], K_SSM) + b_ref[...]
        o_ref[...] = _silu(pre)

    return pl.pallas_call(
        body, name="ssm_conv_fwd", grid=(D_XBC // cb, nt),
        in_specs=[cur, prev, pl.BlockSpec((8, cb), lambda j, i: (0, j)), pl.BlockSpec((1, cb), lambda j, i: (0, j))],
        out_specs=pl.BlockSpec((tt, cb), lambda j, i: (i, j)),
        out_shape=jax.ShapeDtypeStruct((T, D_XBC), F32),
        compiler_params=_cparams(("parallel", "parallel")),
    )(proj, proj, w8, b)


def _ssm_conv_bwd(dact, proj, w8, b):
    T = proj.shape[0]
    tt, cb = _tile(T, 512), 512
    nt = T // tt
    cur, prev, nxt = _conv_specs(tt, cb, OFF_XBC // cb, nt)
    dcur, dprev, dnxt = _conv_specs(tt, cb, 0, nt)

    def dpre_of(d, u, p8, w, bb):
        pre = _causal_conv(u, p8, w, K_SSM) + bb
        return d * _dsilu(pre)

    def body(d_ref, dn_ref, u_ref, up_ref, un_ref, w_ref, b_ref, dx_ref, dw_ref, db_ref):
        i = pl.program_id(1)

        @pl.when(i == 0)
        def _():
            dw_ref[...] = jnp.zeros_like(dw_ref)
            db_ref[...] = jnp.zeros_like(db_ref)

        w, bb = w_ref[...], b_ref[...]
        u = u_ref[...]
        p8 = jnp.where(i == 0, 0.0, up_ref[...])
        dpre = dpre_of(d_ref[...], u, p8, w, bb)
        un = un_ref[...]
        dpre_n = dpre_of(dn_ref[...], un, u[tt - HALO:, :], w, bb)
        dpre_n = jnp.where(i == nt - 1, 0.0, dpre_n)
        dx_ref[...] = _anticausal_conv(dpre, dpre_n, w, K_SSM).astype(BF16)
        rows = [jnp.sum(dpre * _shift_down(u, p8, K_SSM - 1 - k), axis=0, keepdims=True) for k in range(K_SSM - 1)]
        rows.append(jnp.sum(dpre * u, axis=0, keepdims=True))
        rows.append(jnp.zeros((8 - K_SSM, cb), F32))
        dw_ref[...] += jnp.concatenate(rows, axis=0)
        db_ref[...] += jnp.sum(dpre, axis=0, keepdims=True)

    wspec = pl.BlockSpec((8, cb), lambda j, i: (0, j))
    bspec = pl.BlockSpec((1, cb), lambda j, i: (0, j))
    return pl.pallas_call(
        body, name="ssm_conv_bwd", grid=(D_XBC // cb, nt),
        in_specs=[dcur, dnxt, cur, prev, nxt, wspec, bspec],
        out_specs=[pl.BlockSpec((tt, cb), lambda j, i: (i, j)), wspec, bspec],
        out_shape=[jax.ShapeDtypeStruct((T, D_XBC), BF16), jax.ShapeDtypeStruct((8, D_XBC), F32),
                   jax.ShapeDtypeStruct((1, D_XBC), F32)],
        compiler_params=_cparams(("parallel", "arbitrary")),
    )(dact, dact, proj, proj, proj, w8, b)


def _shortconv_fwd(proj, w8):
    T = proj.shape[0]
    tt, cb = _tile(T, 512), 512
    nt = T // tt
    gb_s, _, _ = _conv_specs(tt, cb, OFF_CB // cb, nt)
    gc_s, gcp_s, _ = _conv_specs(tt, cb, OFF_CC // cb, nt)
    u_s, up_s, _ = _conv_specs(tt, cb, OFF_CX // cb, nt)

    def body(gb_ref, gc_ref, gcp_ref, u_ref, up_ref, w_ref, o_ref):
        v = gc_ref[...] * u_ref[...]
        vp = jnp.where(pl.program_id(1) == 0, 0.0, gcp_ref[...] * up_ref[...])
        o_ref[...] = (gb_ref[...] * _causal_conv(v, vp, w_ref[...], K_SC)).astype(BF16)

    return pl.pallas_call(
        body, name="shortconv_fwd", grid=(D_MODEL // cb, nt),
        in_specs=[gb_s, gc_s, gcp_s, u_s, up_s, pl.BlockSpec((8, cb), lambda j, i: (0, j))],
        out_specs=pl.BlockSpec((tt, cb), lambda j, i: (i, j)),
        out_shape=jax.ShapeDtypeStruct((T, D_MODEL), BF16),
        compiler_params=_cparams(("parallel", "parallel")),
    )(proj, proj, proj, proj, proj, w8)


def _shortconv_bwd(dmix, proj, w8):
    T = proj.shape[0]
    tt, cb = _tile(T, 512), 512
    nt = T // tt
    d_s, _, dn_s = _conv_specs(tt, cb, D_SSM // cb, nt)
    gb_s, _, gbn_s = _conv_specs(tt, cb, OFF_CB // cb, nt)
    gc_s, gcp_s, _ = _conv_specs(tt, cb, OFF_CC // cb, nt)
    u_s, up_s, _ = _conv_specs(tt, cb, OFF_CX // cb, nt)

    def body(d_ref, dn_ref, gb_ref, gbn_ref, gc_ref, gcp_ref, u_ref, up_ref, w_ref,
             dgb_ref, dgc_ref, du_ref, dw_ref):
        i = pl.program_id(1)

        @pl.when(i == 0)
        def _():
            dw_ref[...] = jnp.zeros_like(dw_ref)

        w = w_ref[...]
        gc, u = gc_ref[...], u_ref[...]
        v = gc * u
        vp = jnp.where(i == 0, 0.0, gcp_ref[...] * up_ref[...])
        d = d_ref[...].astype(F32)
        dgb_ref[...] = (d * _causal_conv(v, vp, w, K_SC)).astype(BF16)
        dcv = d * gb_ref[...]
        dcv_n = jnp.where(i == nt - 1, 0.0, dn_ref[...].astype(F32) * gbn_ref[...])
        dv = _anticausal_conv(dcv, dcv_n, w, K_SC)
        dgc_ref[...] = (dv * u).astype(BF16)
        du_ref[...] = (dv * gc).astype(BF16)
        rows = [jnp.sum(dcv * _shift_down(v, vp, K_SC - 1 - k), axis=0, keepdims=True) for k in range(K_SC - 1)]
        rows.append(jnp.sum(dcv * v, axis=0, keepdims=True))
        rows.append(jnp.zeros((8 - K_SC, cb), F32))
        dw_ref[...] += jnp.concatenate(rows, axis=0)

    wspec = pl.BlockSpec((8, cb), lambda j, i: (0, j))
    tok = pl.BlockSpec((tt, cb), lambda j, i: (i, j))
    return pl.pallas_call(
        body, name="shortconv_bwd", grid=(D_MODEL // cb, nt),
        in_specs=[d_s, dn_s, gb_s, gbn_s, gc_s, gcp_s, u_s, up_s, wspec],
        out_specs=[tok, tok, tok, wspec],
        out_shape=[jax.ShapeDtypeStruct((T, D_MODEL), BF16)] * 3 + [jax.ShapeDtypeStruct((8, D_MODEL), F32)],
        compiler_params=_cparams(("parallel", "arbitrary")),
    )(dmix, dmix, proj, proj, proj, proj, proj, proj, w8)


GW = HEADS_PER_GROUP * HEADDIM
HI = lax.Precision.HIGHEST


def _dot(a, b):
    return jnp.dot(a.astype(BF16), b.astype(BF16), preferred_element_type=F32)


def _dot_nt(a, b):
    return lax.dot_general(a.astype(BF16), b.astype(BF16), (((1,), (1,)), ((), ())), preferred_element_type=F32)


def _dot_tn(a, b):
    return lax.dot_general(a.astype(BF16), b.astype(BF16), (((0,), (0,)), ((), ())), preferred_element_type=F32)


def _dot_hi(a, b):
    return jnp.dot(a, b, precision=HI, preferred_element_type=F32)


def _dot_nt_hi(a, b):
    return lax.dot_general(a, b, (((1,), (1,)), ((), ())), precision=HI, preferred_element_type=F32)


def _head_cols(rows):
    parts = [jnp.broadcast_to(rows[r:r + 1, :], (HEADDIM, CHUNK)) for r in range(HEADS_PER_GROUP)]
    return jnp.concatenate(parts, axis=0).T


def _head_rows(rows):
    parts = [jnp.broadcast_to(rows[r:r + 1, :], (HEADDIM, N_STATE)) for r in range(HEADS_PER_GROUP)]
    return jnp.concatenate(parts, axis=0)


def _ssd_common(dtr, bias, alog):
    dt = _softplus(dtr + bias)
    A = -jnp.exp(alog)
    a = dt * A
    ki = lax.broadcasted_iota(jnp.int32, (CHUNK, CHUNK), 0)
    si = lax.broadcasted_iota(jnp.int32, (CHUNK, CHUNK), 1)
    upper = (ki <= si).astype(F32)
    cs = _dot_hi(a, upper)
    cs_last = jnp.broadcast_to(cs[:, CHUNK - 1:CHUNK], (8, CHUNK))
    return dt, A, a, cs, cs_last


def _decay_matrix(cs, r):
    li = lax.broadcasted_iota(jnp.int32, (CHUNK, CHUNK), 0)
    si = lax.broadcasted_iota(jnp.int32, (CHUNK, CHUNK), 1)
    causal = li >= si
    R = jnp.broadcast_to(cs[r:r + 1, :], (CHUNK, CHUNK))
    seg = jnp.where(causal, R.T - R, 0.0)
    return jnp.where(causal, jnp.exp(seg), 0.0)


def _ssd_in_specs(nc, rev):
    cix = (lambda c: nc - 1 - c) if rev else (lambda c: c)
    x_s = pl.BlockSpec((CHUNK, GW), lambda g, c: (cix(c), g))
    b_s = pl.BlockSpec((CHUNK, N_STATE), lambda g, c: (cix(c), D_SSM // N_STATE + g))
    c_s = pl.BlockSpec((CHUNK, N_STATE), lambda g, c: (cix(c), D_SSM // N_STATE + N_GROUPS + g))
    dtr_s = pl.BlockSpec((1, 8, CHUNK), lambda g, c: (g, 0, cix(c)))
    row_s = pl.BlockSpec((1, 8, CHUNK), lambda g, c: (g, 0, 0))
    drep_s = pl.BlockSpec((1, GW), lambda g, c: (0, g))
    hs_s = pl.BlockSpec((1, GW, N_STATE), lambda g, c: (cix(c), g, 0))
    return x_s, b_s, c_s, dtr_s, row_s, drep_s, hs_s


def _ssd_fwd(xbc, dtr, bias, alog, drep):
    T = xbc.shape[0]
    nc = T // CHUNK
    x_s, b_s, c_s, dtr_s, row_s, drep_s, hs_s = _ssd_in_specs(nc, False)

    def body(x_ref, b_ref, c_ref, dtr_ref, bias_ref, alog_ref, drep_ref, y_ref, hs_ref, h_scr):
        @pl.when(pl.program_id(1) == 0)
        def _():
            h_scr[...] = jnp.zeros_like(h_scr)

        x, Bm, Cm = x_ref[...], b_ref[...], c_ref[...]
        dt, A, a, cs, cs_last = _ssd_common(dtr_ref[0], bias_ref[0], alog_ref[0])
        E = _head_cols(jnp.exp(cs))
        W = _head_cols(jnp.exp(cs_last - cs) * dt)
        X = (x * _head_cols(dt)).astype(BF16)
        CB = _dot_nt(Cm, Bm)
        col = lax.broadcasted_iota(jnp.int32, (CHUNK, GW), 1) // HEADDIM
        y = jnp.zeros((CHUNK, GW), F32)
        for r in range(HEADS_PER_GROUP):
            M = CB * _decay_matrix(cs, r)
            y = y + jnp.where(col == r, _dot(M, X), 0.0)
        h = h_scr[...]
        hs_ref[0] = h
        y = y + _dot_nt(Cm, h) * E
        y_ref[...] = y + drep_ref[...] * x
        h_scr[...] = h * _head_rows(jnp.exp(cs_last)) + _dot_tn(x * W, Bm)

    return pl.pallas_call(
        body, name="ssd_fwd", grid=(N_GROUPS, nc),
        in_specs=[x_s, b_s, c_s, dtr_s, row_s, row_s, drep_s],
        out_specs=[x_s, hs_s],
        out_shape=[jax.ShapeDtypeStruct((T, D_SSM), F32), jax.ShapeDtypeStruct((nc, D_SSM, N_STATE), F32)],
        scratch_shapes=[pltpu.VMEM((GW, N_STATE), F32)],
        compiler_params=_cparams(("parallel", "arbitrary")),
    )(xbc, xbc, xbc, dtr, bias, alog, drep)


def _ssd_bwd(xbc, dtr, bias, alog, drep, dy, hs):
    T = xbc.shape[0]
    nc = T // CHUNK
    x_s, b_s, c_s, dtr_s, row_s, drep_s, hs_s = _ssd_in_specs(nc, True)
    bc_out = pl.BlockSpec((CHUNK, N_STATE), lambda g, c: (nc - 1 - c, g))

    def body(x_ref, b_ref, c_ref, dtr_ref, bias_ref, alog_ref, drep_ref, dy_ref, hs_ref,
             dx_ref, db_ref, dc_ref, ddtr_ref, dbias_ref, dalog_ref, dd_ref, dh_scr):
        @pl.when(pl.program_id(1) == 0)
        def _():
            dh_scr[...] = jnp.zeros_like(dh_scr)
            dbias_ref[...] = jnp.zeros_like(dbias_ref)
            dalog_ref[...] = jnp.zeros_like(dalog_ref)
            dd_ref[...] = jnp.zeros_like(dd_ref)

        x, Bm, Cm, dY = x_ref[...], b_ref[...], c_ref[...], dy_ref[...]
        dt, A, a, cs, cs_last = _ssd_common(dtr_ref[0], bias_ref[0], alog_ref[0])
        E = _head_cols(jnp.exp(cs))
        DT = _head_cols(dt)
        Wd = _head_cols(jnp.exp(cs_last - cs))
        X = x * DT
        h = hs_ref[0]
        dS = dh_scr[...]
        CB = _dot_nt(Cm, Bm)
        col = lax.broadcasted_iota(jnp.int32, (CHUNK, GW), 1) // HEADDIM
        rowid = lax.broadcasted_iota(jnp.int32, (8, CHUNK), 0)
        lane = lax.broadcasted_iota(jnp.int32, (8, CHUNK), 1)
        hsel = (lax.broadcasted_iota(jnp.int32, (8, GW), 1) // HEADDIM
                == lax.broadcasted_iota(jnp.int32, (8, GW), 0)).astype(F32)
        ones8 = jnp.ones((8, CHUNK), F32)

        dX = jnp.zeros((CHUNK, GW), F32)
        dCB = jnp.zeros((CHUNK, CHUNK), F32)
        dcs = jnp.zeros((8, CHUNK), F32)
        for r in range(HEADS_PER_GROUP):
            L = _decay_matrix(cs, r)
            M = CB * L
            G = _dot_nt(jnp.where(col == r, dY, 0.0), X)
            GL = G * L
            dCB = dCB + GL
            Wm = GL * CB
            colsum = jnp.sum(Wm, axis=0, keepdims=True)
            rowsum = _dot_nt_hi(ones8, Wm)
            dcs = dcs + jnp.where(rowid == r, rowsum - colsum, 0.0)
            dX = dX + jnp.where(col == r, _dot_tn(M, dY), 0.0)
        dC = _dot(dCB, Bm)
        dB = _dot_tn(dCB, Cm)
        T1 = _dot_nt(Bm, dS)
        dX = dX + T1 * Wd
        dB = dB + _dot(X * Wd, dS)
        pdec = _dot_nt_hi(hsel, X * T1 * Wd)
        dcs = dcs - pdec
        dlast = jnp.sum(pdec, axis=1, keepdims=True) \
            + jnp.exp(cs_last[:, 0:1]) * jnp.sum(_dot_hi(hsel, dS * h), axis=1, keepdims=True)
        dYE = dY * E
        dC = dC + _dot(dYE, h)
        yoff = _dot_nt(Cm, h) * E
        dcs = dcs + _dot_nt_hi(hsel, dY * yoff)
        dcs = dcs + jnp.where(lane == CHUNK - 1, dlast, 0.0)
        ki = lax.broadcasted_iota(jnp.int32, (CHUNK, CHUNK), 0)
        si = lax.broadcasted_iota(jnp.int32, (CHUNK, CHUNK), 1)
        lower = (ki >= si).astype(F32)
        da = _dot_hi(dcs, lower)
        ddt = da * A + _dot_nt_hi(hsel, dX * x)
        ddtr = ddt * _sigmoid(dtr_ref[0] + bias_ref[0])
        ddtr_ref[0] = ddtr
        dbias_ref[0] += ddtr
        dalog_ref[0] += da * a
        dx_ref[...] = dX * DT + drep_ref[...] * dY
        dd_ref[...] += jnp.sum(dY * x, axis=0, keepdims=True)
        db_ref[...] = dB
        dc_ref[...] = dC
        dh_scr[...] = dS * _head_rows(jnp.exp(cs_last)) + _dot_tn(dYE, Cm)

    return pl.pallas_call(
        body, name="ssd_bwd", grid=(N_GROUPS, nc),
        in_specs=[x_s, b_s, c_s, dtr_s, row_s, row_s, drep_s, x_s, hs_s],
        out_specs=[x_s, bc_out, bc_out, dtr_s, row_s, row_s, drep_s],
        out_shape=[jax.ShapeDtypeStruct((T, D_SSM), F32),
                   jax.ShapeDtypeStruct((T, N_GROUPS * N_STATE), F32),
                   jax.ShapeDtypeStruct((T, N_GROUPS * N_STATE), F32),
                   jax.ShapeDtypeStruct((N_GROUPS, 8, T), F32),
                   jax.ShapeDtypeStruct((N_GROUPS, 8, CHUNK), F32),
                   jax.ShapeDtypeStruct((N_GROUPS, 8, CHUNK), F32),
                   jax.ShapeDtypeStruct((1, D_SSM), F32)],
        scratch_shapes=[pltpu.VMEM((GW, N_STATE), F32)],
        compiler_params=_cparams(("parallel", "arbitrary")),
    )(xbc, xbc, xbc, dtr, bias, alog, drep, dy, hs)


def _adamw(w, g, m, v, name, deps=()):
    R, C = w.shape
    tr = _tile(R, 256, 8)
    nd = len(deps)

    def body(w_ref, g_ref, m_ref, v_ref, *rest):
        d_ref, mo_ref, vo_ref = rest[nd:]
        gv = g_ref[...]
        mn = ADAM_B1 * m_ref[...] + (1.0 - ADAM_B1) * gv
        vn = ADAM_B2 * v_ref[...] + (1.0 - ADAM_B2) * (gv * gv)
        m_hat = mn / (1.0 - ADAM_B1 ** ADAM_STEP)
        v_hat = vn / (1.0 - ADAM_B2 ** ADAM_STEP)
        d_ref[...] = -ADAM_LR * (m_hat / (jnp.sqrt(v_hat) + ADAM_EPS) + ADAM_WD * w_ref[...])
        mo_ref[...] = mn
        vo_ref[...] = vn

    spec = pl.BlockSpec((tr, C), lambda i: (i, 0))
    return pl.pallas_call(
        body, name=name, grid=(R // tr,),
        in_specs=[spec] * 4 + [ANY] * nd, out_specs=[spec] * 3,
        out_shape=[jax.ShapeDtypeStruct((R, C), F32)] * 3,
        compiler_params=_cparams(("parallel",)),
    )(w, g, m, v, *deps)


ANY = pl.BlockSpec(memory_space=pl.ANY)


def _place():
    x, y, c = lax.axis_index("x"), lax.axis_index("y"), lax.axis_index("c")
    return x, y, c


def _other_chips(x, y):
    return [(1 - x, y), (x, 1 - y), (1 - x, 1 - y)]


def _allgather_inplace(bufs):
    n = len(bufs)

    def body(*refs):
        o_refs = refs[n:2 * n]
        send_sems, recv_sems = refs[2 * n:]
        x, y, c = _place()
        sibling = (x, y, 1 - c)
        chips = _other_chips(x, y)

        def copy(k, slot, px, py, pc, to):
            blk = o_refs[k].at[4 * px + 2 * py + pc]
            return pltpu.make_async_remote_copy(
                src_ref=blk, dst_ref=blk, send_sem=send_sems.at[k, slot], recv_sem=recv_sems.at[k, slot],
                device_id=to, device_id_type=MESH)

        sent = []
        for k in range(n):
            for j, (px, py) in enumerate(chips):
                cp = copy(k, j, x, y, c, (px, py, c))
                cp.start()
                sent.append(cp)
        for k in range(n):
            for j, (px, py) in enumerate(chips):
                copy(k, j, px, py, c, (px, py, c)).wait_recv()
                fwd = copy(k, 3 + j, px, py, c, sibling)
                fwd.start()
                sent.append(fwd)
        for k in range(n):
            for j, (px, py) in enumerate(chips):
                copy(k, 3 + j, px, py, 1 - c, sibling).wait_recv()
        for cp in sent:
            cp.wait_send()

    return pl.pallas_call(
        body, name="allgather_w_in",
        in_specs=[ANY] * n, out_specs=[ANY] * n,
        out_shape=[jax.ShapeDtypeStruct(b.shape, b.dtype) for b in bufs],
        input_output_aliases={k: k for k in range(n)},
        scratch_shapes=[pltpu.SemaphoreType.DMA((n, 6)), pltpu.SemaphoreType.DMA((n, 6))],
    )(*bufs)


HBM = pl.BlockSpec(memory_space=pltpu.HBM)
SEM = pl.BlockSpec(memory_space=pltpu.SEMAPHORE)
EFFECT = pltpu.SideEffectType.DATAFLOW_SIDE_EFFECTING


def _split_start(name, arrays, build, n_copies, after=()):
    na, nd = len(arrays), len(after)

    def body(*refs):
        send_sems, recv_sems = refs[na + nd], refs[na + nd + 1]
        for cp in build(refs[:na], send_sems, recv_sems):
            cp.start()
        refs[-1][...] = jnp.zeros((8, 128), F32)

    outs = pl.pallas_call(
        body, name=name,
        out_shape=(pltpu.SemaphoreType.DMA((n_copies,)), pltpu.SemaphoreType.DMA((n_copies,)),
                   *[pltpu.HBM(a.shape, a.dtype) for a in arrays], jax.ShapeDtypeStruct((8, 128), F32)),
        in_specs=[HBM] * na + [ANY] * nd,
        out_specs=(SEM, SEM, *[HBM] * na, pl.BlockSpec(memory_space=pltpu.VMEM)),
        input_output_aliases={i: 2 + i for i in range(na)},
        compiler_params=pltpu.CompilerParams(has_side_effects=EFFECT),
    )(*[pltpu.with_memory_space_constraint(a, pltpu.HBM) for a in arrays], *after)
    return outs[0], outs[1], list(outs[2:2 + na]), outs[-1]


def _split_wait(name, send_sems, recv_sems, arrays, build, after):
    na = len(arrays)

    def body(*refs):
        for cp in build(refs[:na], refs[na], refs[na + 1]):
            cp.wait_send()
            cp.wait_recv()

    outs = pl.pallas_call(
        body, name=name,
        out_shape=tuple(pltpu.HBM(a.shape, a.dtype) for a in arrays),
        in_specs=[HBM] * na + [SEM, SEM] + [ANY] * len(after),
        out_specs=tuple([HBM] * na),
        input_output_aliases={i: i for i in range(na)},
        compiler_params=pltpu.CompilerParams(has_side_effects=EFFECT),
    )(*arrays, send_sems, recv_sems, *after)
    return list(outs)


def _remote(src, dst, send_sems, recv_sems, i, to):
    return pltpu.make_async_remote_copy(src_ref=src, dst_ref=dst, send_sem=send_sems.at[i], recv_sem=recv_sems.at[i],
                                        device_id=to, device_id_type=MESH)


def _build_ag_ici(refs, ss, rs):
    x, y, c = _place()
    cps = []
    for k, ref in enumerate(refs):
        blk = ref.at[4 * x + 2 * y + c]
        for j, (px, py) in enumerate(_other_chips(x, y)):
            cps.append(_remote(blk, blk, ss, rs, 3 * k + j, (px, py, c)))
    return cps


def _build_ag_fwd(refs, ss, rs):
    x, y, c = _place()
    cps = []
    for k, ref in enumerate(refs):
        for j, (px, py) in enumerate(_other_chips(x, y)):
            blk = ref.at[4 * px + 2 * py + c]
            cps.append(_remote(blk, blk, ss, rs, 3 * k + j, (x, y, 1 - c)))
    return cps


def _build_rs_swap(refs, ss, rs):
    x, y, c = _place()
    n = len(refs) // 2
    return [_remote(refs[k].at[:, pl.ds(1 - c, 1)], refs[n + k], ss, rs, k, (x, y, 1 - c)) for k in range(n)]


def _build_rs_ici(refs, ss, rs):
    x, y, c = _place()
    n = len(refs) // 2
    me = 2 * x + y
    cps = []
    for k in range(n):
        for j, (px, py) in enumerate(_other_chips(x, y)):
            cps.append(_remote(refs[k].at[2 * px + py], refs[n + k].at[me], ss, rs, 3 * k + j, (px, py, c)))
    return cps


def _build_rs_share(refs, ss, rs):
    x, y, c = _place()
    return [_remote(ref.at[c], ref.at[c], ss, rs, k, (x, y, 1 - c)) for k, ref in enumerate(refs)]


def _allreduce_small(p):
    R, C = p.shape

    def body(p_ref, gath_ref, sum_ref, send_sems, recv_sems, local_sem):
        x, y, c = _place()
        me, sibling = (x, y, c), (x, y, 1 - c)
        chips = [(1 - x, y), (x, 1 - y), (1 - x, 1 - y)]

        def blk(px, py, pc):
            return gath_ref.at[4 * px + 2 * py + pc]

        def copy(k, block, to, src=None):
            return pltpu.make_async_remote_copy(
                src_ref=blk(*block) if src is None else src, dst_ref=blk(*block),
                send_sem=send_sems.at[k], recv_sem=recv_sems.at[k], device_id=to, device_id_type=MESH)

        mine = pltpu.make_async_copy(p_ref, blk(*me), local_sem)
        mine.start()
        first = [copy(0, me, sibling, src=p_ref)]
        first += [copy(1 + j, me, (*chip, c), src=p_ref) for j, chip in enumerate(chips)]
        for cp in first:
            cp.start()
        passed = [copy(4 + j, (*chip, c), sibling) for j, chip in enumerate(chips)]
        for j, chip in enumerate(chips):
            copy(1 + j, (*chip, c), me).wait_recv()
            passed[j].start()
        copy(0, sibling, me).wait_recv()
        for j, chip in enumerate(chips):
            copy(4 + j, (*chip, 1 - c), me).wait_recv()
        for cp in first + passed:
            cp.wait_send()
        mine.wait()
        s = gath_ref[0]
        for d in range(1, N_DEV):
            s = s + gath_ref[d]
        sum_ref[...] = s

    vm = pl.BlockSpec(memory_space=pltpu.VMEM)
    return pl.pallas_call(
        body, name="allreduce_small",
        in_specs=[vm], out_specs=[vm, vm],
        out_shape=[jax.ShapeDtypeStruct((N_DEV, R, C), F32), jax.ShapeDtypeStruct((R, C), F32)],
        scratch_shapes=[pltpu.SemaphoreType.DMA((7,)), pltpu.SemaphoreType.DMA((7,)), pltpu.SemaphoreType.DMA],
    )(p)[1]


def _rs_add_pair(p, r0, c_arr, name):
    _, _, hr, cols = p.shape
    tr = _tile(hr, 256, 8)

    def body(c_ref, p_ref, r_ref, q_ref):
        q_ref[...] = (p_ref[0] + r_ref[0]).astype(BF16)

    grid_spec = pltpu.PrefetchScalarGridSpec(
        num_scalar_prefetch=1, grid=(N_CHIPS, hr // tr),
        in_specs=[pl.BlockSpec((1, 1, tr, cols), lambda j, i, c_ref: (j, c_ref[0], i, 0)),
                  pl.BlockSpec((1, 1, tr, cols), lambda j, i, c_ref: (j, 0, i, 0))],
        out_specs=pl.BlockSpec((1, tr, cols), lambda j, i, c_ref: (j, i, 0)))
    return pl.pallas_call(
        body, name=name, grid_spec=grid_spec,
        out_shape=jax.ShapeDtypeStruct((N_CHIPS, hr, cols), BF16),
        compiler_params=_cparams(("parallel", "parallel")),
    )(c_arr, p, r0)


def _rs_add_chips(r1, q, place_arr, name):
    _, hr, cols = r1.shape
    tr = _tile(hr, 256, 8)

    def body(place_ref, r_ref, q_ref, o_ref):
        chip = place_ref[0]
        s = None
        for j in range(N_CHIPS):
            t = jnp.where(chip == j, q_ref[j], r_ref[j]).astype(F32)
            s = t if s is None else s + t
        o_ref[...] = s

    blk = pl.BlockSpec((N_CHIPS, tr, cols), lambda i, place_ref: (0, i, 0))
    grid_spec = pltpu.PrefetchScalarGridSpec(
        num_scalar_prefetch=1, grid=(hr // tr,), in_specs=[blk, blk],
        out_specs=pl.BlockSpec((None, tr, cols), lambda i, place_ref: (place_ref[1], i, 0)))
    return pl.pallas_call(
        body, name=name, grid_spec=grid_spec,
        out_shape=jax.ShapeDtypeStruct((2, hr, cols), F32),
        compiler_params=_cparams(("parallel",)),
    )(place_arr, r1, q)


def _pad_rows(a, rows):
    return jnp.pad(a, ((0, rows - a.shape[0]), (0, 0)))


def _pad_cols(a, cols):
    return jnp.pad(a, ((0, 0), (0, cols - a.shape[1])))


def _heads_to_rows(v):
    v = v.reshape(N_GROUPS, HEADS_PER_GROUP, 1)
    v = jnp.pad(v, ((0, 0), (0, 8 - HEADS_PER_GROUP), (0, 0)))
    return jnp.broadcast_to(v, (N_GROUPS, 8, CHUNK))


def _rows_to_heads(a):
    return jnp.sum(a[:, :HEADS_PER_GROUP, :], axis=-1).reshape(N_HEADS)


def kernel(x, norm_mix_g, w_in, ssm_conv_w, ssm_conv_b, ssm_dt_bias, ssm_A_log, ssm_D, ssm_norm_g, sc_conv_w, w_out, norm_ffn_g, w_gate, w_up, w_down, norm_final_g, loss_target, m_norm_mix_g, m_w_in, m_ssm_conv_w, m_ssm_conv_b, m_ssm_dt_bias, m_ssm_A_log, m_ssm_D, m_ssm_norm_g, m_sc_conv_w, m_w_out, m_norm_ffn_g, m_w_gate, m_w_up, m_w_down, m_norm_final_g, v_norm_mix_g, v_w_in, v_ssm_conv_w, v_ssm_conv_b, v_ssm_dt_bias, v_ssm_A_log, v_ssm_D, v_ssm_norm_g, v_sc_conv_w, v_w_out, v_norm_ffn_g, v_w_gate, v_w_up, v_w_down, v_norm_final_g):
    T = x.shape[1]
    xt = x[0]
    tgt = loss_target[0]
    cx, cy, cc = lax.axis_index("x"), lax.axis_index("y"), lax.axis_index("c")
    chip = 2 * cx + cy
    c_arr = jnp.reshape(cc, (1,)).astype(jnp.int32)
    chip_arr = jnp.reshape(chip, (1,)).astype(jnp.int32)
    place_arr = jnp.stack([chip, cc]).astype(jnp.int32)

    big = [w_in[0], w_out[0], w_gate[0], w_up[0], w_down[0]]
    names = ["w_in", "w_out", "w_gate", "w_up", "w_down"]
    gbufs = [_cast_into_gather(w, chip_arr, "cast_" + nm) for w, nm in zip(big, names)]
    (g_in,) = _allgather_inplace([gbufs[0]])
    ag_ss, ag_rs, ag_bufs, ag_tok = _split_start("ag_ici_start", gbufs[1:], _build_ag_ici, 12, after=[g_in])
    w_in_f = g_in.reshape(N_CHIPS, D_MODEL, D_IN // N_CHIPS).transpose(1, 0, 2).reshape(D_MODEL, D_IN)
    w_main = jnp.concatenate([w_in_f[:, :OFF_CB], w_in_f[:, OFF_CB + N_HEADS:]], axis=1)
    w_dt = _pad_cols(w_in_f[:, OFF_CB:OFF_CB + N_HEADS], DT_PAD)

    contrib = (cc == 0).astype(F32)
    place_ssm = jnp.zeros((8, D_XBC), F32)
    place_ssm = lax.dynamic_update_slice(place_ssm, _pad_rows(ssm_conv_w[0], 8) * contrib, (0, chip * (D_XBC // N_CHIPS)))
    place_sc = jnp.zeros((8, D_MODEL), F32)
    place_sc = lax.dynamic_update_slice(place_sc, _pad_rows(sc_conv_w[0], 8) * contrib, (0, chip * (D_MODEL // N_CHIPS)))
    convs = _allreduce_small(jnp.concatenate([place_ssm, _pad_cols(place_sc, D_XBC)], axis=0))
    ssm_w8 = convs[:8]
    sc_w8 = convs[8:, :D_MODEL]

    bias_rows = _heads_to_rows(ssm_dt_bias[0])
    alog_rows = _heads_to_rows(ssm_A_log[0])
    drep = jnp.repeat(ssm_D[0], HEADDIM).reshape(1, D_SSM)

    n1 = _rmsnorm_fwd(xt, _tie(norm_mix_g, ag_tok, "tie_ag_ici"), "rmsnorm_mix")
    (proj,) = _matmul([(n1, w_main)], out_dtypes=[F32], name="mm_proj")
    (dt_raw,) = _matmul([(n1, w_dt)], out_dtypes=[F32], name="mm_proj_dt", tk=2048)
    ag_bufs = _split_wait("ag_ici_wait", ag_ss, ag_rs, ag_bufs, _build_ag_ici, after=[dt_raw])
    fw_ss, fw_rs, fw_bufs, fw_tok = _split_start("ag_fwd_start", ag_bufs, _build_ag_fwd, 12)
    xbc = _ssm_conv_fwd(proj, ssm_w8, _tie(ssm_conv_b, fw_tok, "tie_ag_fwd"))
    dtr = jnp.pad(dt_raw[:, :N_HEADS].T.reshape(N_GROUPS, HEADS_PER_GROUP, T), ((0, 0), (0, 4), (0, 0)))
    y_ssd, hs = _ssd_fwd(xbc, dtr, bias_rows, alog_rows, drep)
    y_ssm = _gated_norm_fwd(y_ssd, proj, ssm_norm_g)
    y_sc = _shortconv_fwd(proj, sc_w8)
    gath = _split_wait("ag_fwd_wait", fw_ss, fw_rs, fw_bufs, _build_ag_fwd, after=[y_sc])
    w_out_f = gath[0].reshape(2 * D_MODEL, D_MODEL)
    w_gate_f = gath[1].reshape(N_CHIPS, D_MODEL, D_FF // N_CHIPS).transpose(1, 0, 2).reshape(D_MODEL, D_FF)
    w_up_f = gath[2].reshape(N_CHIPS, D_MODEL, D_FF // N_CHIPS).transpose(1, 0, 2).reshape(D_MODEL, D_FF)
    w_down_f = gath[3].reshape(D_FF, D_MODEL)
    y_mix = jnp.concatenate([y_ssm, y_sc], axis=1)
    (h1,) = _matmul([(y_mix, w_out_f)], out_dtypes=[F32], name="mm_out", extras=[xt],
                    epilogue=lambda acc, res: (acc + res,))
    n2 = _rmsnorm_fwd(h1, norm_ffn_g, "rmsnorm_ffn")
    g_act, u_act, a_act = _ffn_fwd(n2, w_gate_f, w_up_f)
    (h2,) = _matmul([(a_act, w_down_f)], out_dtypes=[F32], name="mm_down", extras=[h1],
                    epilogue=lambda acc, res: (acc + res,))

    dh2, dh2b, dg_final, loss_part = _loss_and_final_bwd(h2, tgt, norm_final_g.reshape(1, D_MODEL))
    dg_act, du_act = _matmul([(dh2b, w_down_f)], tb=True, out_dtypes=[BF16, BF16], name="mm_down_bwd",
                             tn=512, extras=[g_act, u_act], epilogue=_swiglu_bwd)
    (dw_down,) = _matmul([(a_act, dh2b)], ta=True, out_dtypes=[F32], name="mm_dw_down", tm=1408)
    (dn2,) = _matmul([(dg_act, w_gate_f), (du_act, w_up_f)], tb=True, out_dtypes=[BF16], name="mm_ffn_in_bwd")
    (dw_gate,) = _matmul([(n2, dg_act)], ta=True, out_dtypes=[F32], name="mm_dw_gate", tn=1408, col_shards=True)
    (dw_up,) = _matmul([(n2, du_act)], ta=True, out_dtypes=[F32], name="mm_dw_up", tn=1408, col_shards=True)
    dh1, dh1b, dg_ffn = _rmsnorm_bwd(dn2, h1, norm_ffn_g, dh2, "rmsnorm_ffn_bwd")
    (dmix,) = _matmul([(dh1b, w_out_f)], tb=True, out_dtypes=[BF16], name="mm_out_bwd")
    (dw_out,) = _matmul([(y_mix, dh1b)], ta=True, out_dtypes=[F32], name="mm_dw_out")

    def col_blocks(g):
        R, Ctot = g.shape
        return g.reshape(R, N_CHIPS, Ctot // N_CHIPS).transpose(1, 0, 2).reshape(N_CHIPS, 2, R // 2, Ctot // N_CHIPS)

    def halves(g):
        return g.reshape(N_CHIPS, 2, g.shape[1] // 2, g.shape[2])

    def landing(shape, dtype):
        return lax.empty(shape, dtype)

    names1 = names[1:]
    ps1 = [halves(dw_out.reshape(N_CHIPS, -1, D_MODEL)), halves(dw_gate), halves(dw_up),
           halves(dw_down.reshape(N_CHIPS, -1, D_MODEL))]
    r0_1 = [landing((N_CHIPS, 1) + p.shape[2:], F32) for p in ps1]
    sw_ss, sw_rs, sw_arr, sw_tok = _split_start("rs1_swap_start", ps1 + r0_1, _build_rs_swap, 4)
    dgb, dgc, du_sc, dw_sc = _shortconv_bwd(dmix, proj, _tie(sc_w8, sw_tok, "tie_rs1_swap"))
    dy_ssd, dz, dg_ssmnorm = _gated_norm_bwd(dmix, y_ssd, proj, ssm_norm_g)
    sw_arr = _split_wait("rs1_swap_wait", sw_ss, sw_rs, sw_arr, _build_rs_swap, after=[dz])
    qs1 = [_rs_add_pair(p, r, c_arr, "rs_add_pair_" + nm) for p, r, nm in zip(sw_arr[:4], sw_arr[4:], names1)]
    r1_1 = [landing(q.shape, BF16) for q in qs1]
    ic_ss, ic_rs, ic_arr, ic_tok = _split_start("rs1_ici_start", qs1 + r1_1, _build_rs_ici, 12)
    dxs, dB, dC, ddtr, dbias_acc, dalog_acc, dD_acc = _ssd_bwd(
        xbc, dtr, bias_rows, alog_rows, _tie(drep, ic_tok, "tie_rs1_ici"), dy_ssd, hs)
    dxbc, dw_ssmconv, db_ssmconv = _ssm_conv_bwd(jnp.concatenate([dxs, dB, dC], axis=1), proj, ssm_w8, ssm_conv_b)
    ic_arr = _split_wait("rs1_ici_wait", ic_ss, ic_rs, ic_arr, _build_rs_ici, after=[dxbc])
    g1 = [_rs_add_chips(r, q, place_arr, "rs_add_chips_" + nm) for q, r, nm in zip(ic_arr[:4], ic_arr[4:], names1)]
    sh_ss, sh_rs, sh_arr, sh_tok = _split_start("rs1_share_start", g1, _build_rs_share, 4)

    dproj = jnp.concatenate([dz, dxbc, dgb, dgc, du_sc], axis=1)
    ddt_raw = _pad_cols(ddtr[:, :HEADS_PER_GROUP, :].reshape(N_HEADS, T).T, DT_PAD).astype(BF16)
    (dw_main,) = _matmul([(n1, dproj)], ta=True, out_dtypes=[F32], name="mm_dw_main", deps=[sh_tok])
    (dw_dt,) = _matmul([(n1, ddt_raw)], ta=True, out_dtypes=[F32], name="mm_dw_dt", tk=2048)
    dw_in_full = jnp.concatenate([dw_main[:, :OFF_CB], dw_dt[:, :N_HEADS], dw_main[:, OFF_CB:]], axis=1)
    p_in = col_blocks(dw_in_full)
    s2_ss, s2_rs, s2_arr, s2_tok = _split_start(
        "rs2_swap_start", [p_in, landing((N_CHIPS, 1) + p_in.shape[2:], F32)], _build_rs_swap, 1)
    (dn1a,) = _matmul([(dproj, w_main)], tb=True, out_dtypes=[F32], name="mm_proj_bwd", deps=[s2_tok])
    g1 = _split_wait("rs1_share_wait", sh_ss, sh_rs, sh_arr, _build_rs_share, after=[dn1a])
    s2_arr = _split_wait("rs2_swap_wait", s2_ss, s2_rs, s2_arr, _build_rs_swap, after=[dn1a])
    q_in = _rs_add_pair(s2_arr[0], s2_arr[1], c_arr, "rs_add_pair_w_in")
    i2_ss, i2_rs, i2_arr, i2_tok = _split_start(
        "rs2_ici_start", [q_in, landing(q_in.shape, BF16)], _build_rs_ici, 3)
    (dn1,) = _matmul([(ddt_raw, w_dt)], tb=True, out_dtypes=[BF16], name="mm_proj_dt_bwd", extras=[dn1a],
                     epilogue=lambda acc, res: (acc + res,), deps=[i2_tok])
    dx, _, dg_mix = _rmsnorm_bwd(dn1, xt, norm_mix_g, dh1, "rmsnorm_mix_bwd")

    dD = jnp.sum(dD_acc.reshape(N_HEADS, HEADDIM), axis=-1)
    heads_row = jnp.concatenate([_rows_to_heads(dbias_acc), _rows_to_heads(dalog_acc), dD,
                                 loss_part.reshape(1)]).reshape(1, -1)
    small = jnp.concatenate([
        dw_ssmconv,
        _pad_cols(dw_sc, D_XBC),
        db_ssmconv,
        jnp.concatenate([dg_mix, dg_ssmnorm], axis=1),
        jnp.concatenate([dg_ffn, dg_final], axis=1),
        _pad_cols(heads_row, D_XBC),
        jnp.zeros((4, D_XBC), F32),
    ], axis=0)
    tot = _allreduce_small(small)
    loss = tot[19, 3 * N_HEADS]

    cs_ssm, cs_sc = D_XBC // N_CHIPS, D_MODEL // N_CHIPS
    g_ssm_conv = lax.dynamic_slice(tot[0:K_SSM], (0, chip * cs_ssm), (K_SSM, cs_ssm))
    g_sc_conv = lax.dynamic_slice(tot[8:8 + K_SC, :D_MODEL], (0, chip * cs_sc), (K_SC, cs_sc))
    small_grads = {
        "norm_mix_g": tot[17:18, :D_MODEL], "ssm_conv_w": g_ssm_conv, "ssm_conv_b": tot[16:17],
        "ssm_dt_bias": tot[19:20, 0:N_HEADS], "ssm_A_log": tot[19:20, N_HEADS:2 * N_HEADS],
        "ssm_D": tot[19:20, 2 * N_HEADS:3 * N_HEADS], "ssm_norm_g": tot[17:18, D_MODEL:],
        "sc_conv_w": g_sc_conv, "norm_ffn_g": tot[18:19, :D_MODEL], "norm_final_g": tot[18:19, D_MODEL:],
    }
    small_w = {"norm_mix_g": (norm_mix_g, m_norm_mix_g, v_norm_mix_g),
               "ssm_conv_w": (ssm_conv_w[0], m_ssm_conv_w[0], v_ssm_conv_w[0]),
               "ssm_conv_b": (ssm_conv_b, m_ssm_conv_b, v_ssm_conv_b),
               "ssm_dt_bias": (ssm_dt_bias, m_ssm_dt_bias, v_ssm_dt_bias),
               "ssm_A_log": (ssm_A_log, m_ssm_A_log, v_ssm_A_log),
               "ssm_D": (ssm_D, m_ssm_D, v_ssm_D),
               "ssm_norm_g": (ssm_norm_g, m_ssm_norm_g, v_ssm_norm_g),
               "sc_conv_w": (sc_conv_w[0], m_sc_conv_w[0], v_sc_conv_w[0]),
               "norm_ffn_g": (norm_ffn_g, m_norm_ffn_g, v_norm_ffn_g),
               "norm_final_g": (norm_final_g.reshape(1, -1), m_norm_final_g.reshape(1, -1),
                                v_norm_final_g.reshape(1, -1))}
    PW = 1024
    order = list(small_w)

    def pack(arrs):
        rows = []
        for a in arrs:
            flat = a.reshape(-1)
            n = -(-flat.shape[0] // PW) * PW
            rows.append(jnp.pad(flat, (0, n - flat.shape[0])).reshape(-1, PW))
        slab = jnp.concatenate(rows, axis=0)
        return _pad_rows(slab, -(-slab.shape[0] // 8) * 8)

    wp = pack([small_w[k][0] for k in order])
    mp = pack([small_w[k][1] for k in order])
    vp = pack([small_w[k][2] for k in order])
    gp = pack([small_grads[k] for k in order])
    sd, sm, sv = _adamw(wp, gp, mp, vp, "adamw_small")

    def unpack(slab):
        out, row = {}, 0
        for k in order:
            shape = small_w[k][0].shape
            size = 1
            for s in shape:
                size *= s
            nr = -(-size // PW)
            out[k] = slab[row:row + nr].reshape(-1)[:size].reshape(shape)
            row += nr
        return out

    s_delta, s_m, s_v = unpack(sd), unpack(sm), unpack(sv)

    big_m = [m_w_in[0], m_w_out[0], m_w_gate[0], m_w_up[0], m_w_down[0]]
    big_v = [v_w_in[0], v_w_out[0], v_w_gate[0], v_w_up[0], v_w_down[0]]
    big_grads = [None] + [g.reshape(w.shape) for g, w in zip(g1, big[1:])]
    big_out = {}
    for k in range(1, 5):
        big_out[names[k]] = _adamw(big[k], big_grads[k], big_m[k], big_v[k], "adamw_" + names[k], deps=[i2_tok])
    i2_arr = _split_wait("rs2_ici_wait", i2_ss, i2_rs, i2_arr, _build_rs_ici,
                         after=[big_out[names[4]][0], sd])
    g_in_red = _rs_add_chips(i2_arr[1], i2_arr[0], place_arr, "rs_add_chips_w_in")
    s3_ss, s3_rs, s3_arr, _ = _split_start("rs2_share_start", [g_in_red], _build_rs_share, 1)
    (g_in_full,) = _split_wait("rs2_share_wait", s3_ss, s3_rs, s3_arr, _build_rs_share, after=[])
    big_grads[0] = g_in_full.reshape(big[0].shape)
    big_out[names[0]] = _adamw(big[0], big_grads[0], big_m[0], big_v[0], "adamw_" + names[0])
    big_g = dict(zip(names, big_grads))

    weight_order = ["norm_mix_g", "w_in", "ssm_conv_w", "ssm_conv_b", "ssm_dt_bias", "ssm_A_log", "ssm_D",
                    "ssm_norm_g", "sc_conv_w", "w_out", "norm_ffn_g", "w_gate", "w_up", "w_down", "norm_final_g"]
    lead = {"ssm_conv_w", "sc_conv_w", "w_in", "w_out", "w_gate", "w_up", "w_down"}

    def shaped(nm, a):
        if nm == "norm_final_g":
            return a.reshape(D_MODEL)
        return a[None] if nm in lead else a

    grads, deltas, new_m, new_v = [], [], [], []
    for nm in weight_order:
        if nm in big_out:
            g, (d, m, v) = big_g[nm], big_out[nm]
        else:
            g, d, m, v = small_grads[nm], s_delta[nm], s_m[nm], s_v[nm]
        grads.append(shaped(nm, g))
        deltas.append(shaped(nm, d))
        new_m.append(shaped(nm, m))
        new_v.append(shaped(nm, v))
    return (loss, dx[None], *grads, *deltas, *new_m, *new_v)


def _swiglu_bwd(da, g, u):
    gf, uf = g.astype(F32), u.astype(F32)
    return da * uf * _dsilu(gf), da * _silu(gf)


def _ffn_fwd(n2, w_gate, w_up):
    T, K = n2.shape
    N = w_gate.shape[1]
    tm, tn, tk = _tile(T, 2048), _tile(N, 512), _tile(K, 512)
    nk = K // tk

    def body(a_ref, wg_ref, wu_ref, g_ref, u_ref, act_ref, accg, accu):
        k = pl.program_id(2)

        @pl.when(k == 0)
        def _():
            accg[...] = jnp.zeros_like(accg)
            accu[...] = jnp.zeros_like(accu)

        a = a_ref[...]
        accg[...] += jnp.dot(a, wg_ref[...], preferred_element_type=F32)
        accu[...] += jnp.dot(a, wu_ref[...], preferred_element_type=F32)

        @pl.when(k == nk - 1)
        def _():
            g, u = accg[...], accu[...]
            g_ref[...] = g.astype(BF16)
            u_ref[...] = u.astype(BF16)
            act_ref[...] = (_silu(g) * u).astype(BF16)

    a_spec = pl.BlockSpec((tm, tk), lambda i, j, k: (i, k))
    b_spec = pl.BlockSpec((tk, tn), lambda i, j, k: (k, j))
    o_spec = pl.BlockSpec((tm, tn), lambda i, j, k: (i, j))
    return pl.pallas_call(
        body, name="ffn_fwd", grid=(T // tm, N // tn, nk),
        in_specs=[a_spec, b_spec, b_spec], out_specs=[o_spec] * 3,
        out_shape=[jax.ShapeDtypeStruct((T, N), BF16)] * 3,
        scratch_shapes=[pltpu.VMEM((tm, tn), F32), pltpu.VMEM((tm, tn), F32)],
        compiler_params=_cparams(("parallel", "parallel", "arbitrary")),
    )(n2, w_gate, w_up)
```

```python
import functools

import jax
import jax.numpy as jnp
from jax import lax
from jax.experimental import pallas as pl
from jax.experimental.pallas import tpu as pltpu

F32 = jnp.float32
BF16 = jnp.bfloat16
MESH = pl.DeviceIdType.MESH

D_MODEL = 2048
D_SSM = 2048
HEADDIM = 64
N_HEADS = 32
N_GROUPS = 8
HEADS_PER_GROUP = 4
N_STATE = 128
CHUNK = 128
K_SSM = 4
K_SC = 3
D_XBC = 4096
D_FF = 5632
D_IN = 12320
D_MAIN = 12288
OFF_XBC, OFF_CB, OFF_CC, OFF_CX = 2048, 6144, 8192, 10240
DT_PAD = 128
EPS = 1e-5
N_CHIPS = 4
N_DEV = 8

ADAM_LR = 0.001
ADAM_B1 = 0.9
ADAM_B2 = 0.999
ADAM_EPS = 1e-08
ADAM_WD = 0.01
ADAM_STEP = 10

V7X_VMEM_BYTES = 64 * 1024 * 1024
VMEM_LIMIT = V7X_VMEM_BYTES - 8 * 1024 * 1024


def _cparams(sem=None):
    if sem is None:
        return pltpu.CompilerParams(vmem_limit_bytes=VMEM_LIMIT)
    return pltpu.CompilerParams(dimension_semantics=sem, vmem_limit_bytes=VMEM_LIMIT)


def _tile(dim, pref, unit=128):
    best = None
    t = unit
    while t <= min(dim, pref):
        if dim % t == 0:
            best = t
        t += unit
    return best if best is not None else dim


def _sigmoid(x):
    return 1.0 / (1.0 + jnp.exp(-x))


def _silu(x):
    return x * _sigmoid(x)


def _dsilu(x):
    s = _sigmoid(x)
    return s * (1.0 + x * (1.0 - s))


def _softplus(x):
    return jnp.maximum(x, 0.0) + jnp.log(1.0 + jnp.exp(-jnp.abs(x)))


MATMUL_VMEM_BUDGET = 44 * 1024 * 1024


def _matmul(pairs, *, ta=False, tb=False, out_dtypes, name, tm=1024, tn=1024, tk=None, extras=(), epilogue=None,
            deps=(), col_shards=False, nsub=1):
    a0, b0 = pairs[0]
    M, K = (a0.shape[1], a0.shape[0]) if ta else a0.shape
    N = b0.shape[0] if tb else b0.shape[1]
    tm, tn = _tile(M, tm, 8 if M % 128 else 128), _tile(N, tn)
    npair, nex, ndep, nout = len(pairs), len(extras), len(deps), len(out_dtypes)
    if tk is None:
        fixed = 2 * tm * tn * (sum(jnp.dtype(d).itemsize for d in out_dtypes) + sum(e.dtype.itemsize for e in extras))
        tk = K
        while tk > 128 and (K % tk or tk % 128 or
                            fixed + 2 * npair * 2 * tk * (tm + tn) + (tm * tn * 4 if tk < K else 0) > MATMUL_VMEM_BUDGET):
            tk -= 128
    else:
        tk = _tile(K, tk)
    nk = K // tk
    if nk > 1 or tm % nsub or (tm // nsub) % 128:
        nsub = 1
    sub = tm // nsub
    dims = (((0 if ta else 1,), (1 if tb else 0,)), ((), ()))

    def body(*refs):
        a_refs = refs[0:2 * npair:2]
        b_refs = refs[1:2 * npair:2]
        ex_refs = refs[2 * npair:2 * npair + nex]
        o_refs = refs[2 * npair + nex + ndep:2 * npair + nex + ndep + nout]

        def dots(rows):
            s = None
            for a_ref, b_ref in zip(a_refs, b_refs):
                a = a_ref[...] if rows is None else (a_ref[:, rows] if ta else a_ref[rows, :])
                d = lax.dot_general(a, b_ref[...], dims, preferred_element_type=F32)
                s = d if s is None else s + d
            return s

        def finish(r, rows):
            ex = [e[...] if rows is None else e[rows, :] for e in ex_refs]
            outs = (r,) if epilogue is None else epilogue(r, *ex)
            for o_ref, o in zip(o_refs, outs):
                if rows is None:
                    o_ref[...] = o.astype(o_ref.dtype)
                else:
                    o_ref[rows, :] = o.astype(o_ref.dtype)

        if nk == 1:
            for s in range(nsub):
                rows = None if nsub == 1 else pl.ds(s * sub, sub)
                finish(dots(rows), rows)
            return

        acc = refs[-1]
        k = pl.program_id(2)

        @pl.when(k == 0)
        def _():
            acc[...] = dots(None)

        @pl.when(jnp.logical_and(k > 0, k < nk - 1))
        def _():
            acc[...] += dots(None)

        @pl.when(k == nk - 1)
        def _():
            finish(acc[...] + dots(None), None)

    a_spec = pl.BlockSpec((tk, tm), lambda i, j, k: (k, i)) if ta else pl.BlockSpec((tm, tk), lambda i, j, k: (i, k))
    b_spec = pl.BlockSpec((tn, tk), lambda i, j, k: (j, k)) if tb else pl.BlockSpec((tk, tn), lambda i, j, k: (k, j))
    e_spec = pl.BlockSpec((tm, tn), lambda i, j, k: (i, j))
    if col_shards:
        o_spec = pl.BlockSpec((None, tm, tn), lambda i, j, k: (j, i, 0))
        o_shape = (N // tn, M, tn)
    else:
        o_spec, o_shape = e_spec, (M, N)
    args, in_specs = [], []
    for a, b in pairs:
        args += [a, b]
        in_specs += [a_spec, b_spec]
    args += list(extras) + list(deps)
    in_specs += [e_spec] * nex + [ANY] * ndep
    outs = pl.pallas_call(
        body,
        name=name,
        grid=(M // tm, N // tn, nk),
        in_specs=in_specs,
        out_specs=[o_spec] * nout,
        out_shape=[jax.ShapeDtypeStruct(o_shape, dt) for dt in out_dtypes],
        scratch_shapes=[pltpu.VMEM((tm, tn), F32)] if nk > 1 else [],
        compiler_params=_cparams(("parallel", "parallel", "arbitrary")),
    )(*args)
    return outs


def _cast_into_gather(w, chip_arr, name):
    R, C = w.shape
    hr = R // 2
    tr = _tile(hr, 512, 8)
    nb = hr // tr

    def body(chip_ref, w_ref, o_ref):
        o_ref[...] = w_ref[...].astype(BF16)

    grid_spec = pltpu.PrefetchScalarGridSpec(
        num_scalar_prefetch=1, grid=(2, nb),
        in_specs=[pl.BlockSpec((tr, C), lambda h, i, chip_ref: (h * nb + i, 0))],
        out_specs=pl.BlockSpec((None, tr, C), lambda h, i, chip_ref: (2 * chip_ref[0] + h, i, 0)))
    return pl.pallas_call(
        body, name=name, grid_spec=grid_spec,
        out_shape=jax.ShapeDtypeStruct((N_DEV, hr, C), BF16),
        compiler_params=_cparams(("parallel", "parallel")),
    )(chip_arr, w)


def _tie(small, token, name):
    def body(s_ref, t_ref, o_ref):
        o_ref[...] = s_ref[...]

    vm = pl.BlockSpec(memory_space=pltpu.VMEM)
    return pl.pallas_call(body, name=name, in_specs=[vm, ANY], out_specs=vm,
                          out_shape=jax.ShapeDtypeStruct(small.shape, small.dtype))(small, token)


def _rmsnorm_fwd(x, g, name):
    T, D = x.shape
    tt = _tile(T, 256)

    def body(x_ref, g_ref, n_ref):
        xv = x_ref[...]
        r = lax.rsqrt(jnp.mean(xv * xv, axis=-1, keepdims=True) + EPS)
        n_ref[...] = (xv * r * g_ref[...]).astype(BF16)

    return pl.pallas_call(
        body, name=name, grid=(T // tt,),
        in_specs=[pl.BlockSpec((tt, D), lambda i: (i, 0)), pl.BlockSpec((1, D), lambda i: (0, 0))],
        out_specs=pl.BlockSpec((tt, D), lambda i: (i, 0)),
        out_shape=jax.ShapeDtypeStruct((T, D), BF16),
        compiler_params=_cparams(("parallel",)),
    )(x, g)


def _rmsnorm_bwd(dn, x, g, res, name):
    T, D = x.shape
    tt = _tile(T, 256)

    def body(dn_ref, x_ref, g_ref, res_ref, dx_ref, dxb_ref, dg_ref):
        @pl.when(pl.program_id(0) == 0)
        def _():
            dg_ref[...] = jnp.zeros_like(dg_ref)

        xv = x_ref[...]
        dy = dn_ref[...].astype(F32)
        r = lax.rsqrt(jnp.mean(xv * xv, axis=-1, keepdims=True) + EPS)
        xhat = xv * r
        dxh = dy * g_ref[...]
        dx = res_ref[...] + r * (dxh - xhat * jnp.mean(dxh * xhat, axis=-1, keepdims=True))
        dx_ref[...] = dx
        dxb_ref[...] = dx.astype(BF16)
        dg_ref[...] += jnp.sum(dy * xhat, axis=0, keepdims=True)

    tok = pl.BlockSpec((tt, D), lambda i: (i, 0))
    vec = pl.BlockSpec((1, D), lambda i: (0, 0))
    return pl.pallas_call(
        body, name=name, grid=(T // tt,),
        in_specs=[tok, tok, vec, tok],
        out_specs=[tok, tok, vec],
        out_shape=[jax.ShapeDtypeStruct((T, D), F32), jax.ShapeDtypeStruct((T, D), BF16),
                   jax.ShapeDtypeStruct((1, D), F32)],
        compiler_params=_cparams(("arbitrary",)),
    )(dn, x, g, res)


def _loss_and_final_bwd(h2, target, gf):
    T, D = h2.shape
    tt = _tile(T, 256)

    def body(h_ref, t_ref, g_ref, dh_ref, dhb_ref, dg_ref, loss_ref):
        @pl.when(pl.program_id(0) == 0)
        def _():
            dg_ref[...] = jnp.zeros_like(dg_ref)
            loss_ref[...] = jnp.zeros_like(loss_ref)

        xv = h_ref[...]
        r = lax.rsqrt(jnp.mean(xv * xv, axis=-1, keepdims=True) + EPS)
        xhat = xv * r
        err = xhat * g_ref[...] - t_ref[...]
        loss_ref[...] += 0.5 * jnp.sum(jnp.mean(err * err, axis=-1, keepdims=True), axis=0, keepdims=True)
        dy = err * (1.0 / D)
        dxh = dy * g_ref[...]
        dx = r * (dxh - xhat * jnp.mean(dxh * xhat, axis=-1, keepdims=True))
        dh_ref[...] = dx
        dhb_ref[...] = dx.astype(BF16)
        dg_ref[...] += jnp.sum(dy * xhat, axis=0, keepdims=True)

    tok = pl.BlockSpec((tt, D), lambda i: (i, 0))
    vec = pl.BlockSpec((1, D), lambda i: (0, 0))
    return pl.pallas_call(
        body, name="loss_final_bwd", grid=(T // tt,),
        in_specs=[tok, tok, vec],
        out_specs=[tok, tok, vec, pl.BlockSpec((1, 1), lambda i: (0, 0))],
        out_shape=[jax.ShapeDtypeStruct((T, D), F32), jax.ShapeDtypeStruct((T, D), BF16),
                   jax.ShapeDtypeStruct((1, D), F32), jax.ShapeDtypeStruct((1, 1), F32)],
        compiler_params=_cparams(("arbitrary",)),
    )(h2, target, gf)


def _gated_norm_fwd(y, proj, g):
    T, D = y.shape
    tt = _tile(T, 256)

    def body(y_ref, z_ref, g_ref, o_ref):
        yg = y_ref[...] * _silu(z_ref[...])
        r = lax.rsqrt(jnp.mean(yg * yg, axis=-1, keepdims=True) + EPS)
        o_ref[...] = (yg * r * g_ref[...]).astype(BF16)

    tok = pl.BlockSpec((tt, D), lambda i: (i, 0))
    return pl.pallas_call(
        body, name="gated_norm_fwd", grid=(T // tt,),
        in_specs=[tok, tok, pl.BlockSpec((1, D), lambda i: (0, 0))],
        out_specs=tok,
        out_shape=jax.ShapeDtypeStruct((T, D), BF16),
        compiler_params=_cparams(("parallel",)),
    )(y, proj, g)


def _gated_norm_bwd(dmix, y, proj, g):
    T, D = y.shape
    tt = _tile(T, 256)

    def body(do_ref, y_ref, z_ref, g_ref, dy_ref, dz_ref, dg_ref):
        @pl.when(pl.program_id(0) == 0)
        def _():
            dg_ref[...] = jnp.zeros_like(dg_ref)

        yv, zv = y_ref[...], z_ref[...]
        do = do_ref[...].astype(F32)
        sz = _silu(zv)
        yg = yv * sz
        r = lax.rsqrt(jnp.mean(yg * yg, axis=-1, keepdims=True) + EPS)
        xhat = yg * r
        dxh = do * g_ref[...]
        dyg = r * (dxh - xhat * jnp.mean(dxh * xhat, axis=-1, keepdims=True))
        dy_ref[...] = dyg * sz
        dz_ref[...] = (dyg * yv * _dsilu(zv)).astype(BF16)
        dg_ref[...] += jnp.sum(do * xhat, axis=0, keepdims=True)

    tok = pl.BlockSpec((tt, D), lambda i: (i, 0))
    vec = pl.BlockSpec((1, D), lambda i: (0, 0))
    return pl.pallas_call(
        body, name="gated_norm_bwd", grid=(T // tt,),
        in_specs=[tok, tok, tok, vec],
        out_specs=[tok, tok, vec],
        out_shape=[jax.ShapeDtypeStruct((T, D), F32), jax.ShapeDtypeStruct((T, D), BF16),
                   jax.ShapeDtypeStruct((1, D), F32)],
        compiler_params=_cparams(("arbitrary",)),
    )(dmix, y, proj, g)


HALO = 8


def _shift_down(cur, prev8, s):
    ext = jnp.concatenate([prev8, cur], axis=0)
    return pltpu.roll(ext, s, axis=0)[HALO:]


def _shift_up(cur, next8, s):
    n = cur.shape[0]
    ext = jnp.concatenate([cur, next8], axis=0)
    return pltpu.roll(ext, n + HALO - s, axis=0)[:n]


def _conv_specs(tt, cb, col_off_blocks, nt):
    hb = tt // HALO
    cur = pl.BlockSpec((tt, cb), lambda j, i: (i, col_off_blocks + j))
    prev = pl.BlockSpec((HALO, cb), lambda j, i: (jnp.maximum(i * hb - 1, 0), col_off_blocks + j))
    nxt = pl.BlockSpec((HALO, cb), lambda j, i: (jnp.minimum((i + 1) * hb, nt * hb - 1), col_off_blocks + j))
    return cur, prev, nxt


def _causal_conv(cur, prev8, w, K):
    y = cur * w[K - 1:K, :]
    for k in range(K - 1):
        y = y + _shift_down(cur, prev8, K - 1 - k) * w[k:k + 1, :]
    return y


def _anticausal_conv(cur, next8, w, K):
    y = cur * w[K - 1:K, :]
    for k in range(K - 1):
        y = y + _shift_up(cur, next8, K - 1 - k) * w[k:k + 1, :]
    return y


def _ssm_conv_fwd(proj, w8, b):
    T = proj.shape[0]
    tt, cb = _tile(T, 512), 512
    nt = T // tt
    cur, prev, _ = _conv_specs(tt, cb, OFF_XBC // cb, nt)

    def body(u_ref, up_ref, w_ref, b_ref, o_ref):
        first = pl.program_id(1) == 0
        p8 = jnp.where(first, 0.0, up_ref[...])
        pre = _causal_conv(u_ref[...], p8, w_ref[...], K_SSM) + b_ref[...]
        o_ref[...] = _silu(pre)

    return pl.pallas_call(
        body, name="ssm_conv_fwd", grid=(D_XBC // cb, nt),
        in_specs=[cur, prev, pl.BlockSpec((8, cb), lambda j, i: (0, j)), pl.BlockSpec((1, cb), lambda j, i: (0, j))],
        out_specs=pl.BlockSpec((tt, cb), lambda j, i: (i, j)),
        out_shape=jax.ShapeDtypeStruct((T, D_XBC), F32),
        compiler_params=_cparams(("parallel", "parallel")),
    )(proj, proj, w8, b)


def _ssm_conv_bwd(dact, proj, w8, b):
    T = proj.shape[0]
    tt, cb = _tile(T, 512), 512
    nt = T // tt
    cur, prev, nxt = _conv_specs(tt, cb, OFF_XBC // cb, nt)
    dcur, dprev, dnxt = _conv_specs(tt, cb, 0, nt)

    def dpre_of(d, u, p8, w, bb):
        pre = _causal_conv(u, p8, w, K_SSM) + bb
        return d * _dsilu(pre)

    def body(d_ref, dn_ref, u_ref, up_ref, un_ref, w_ref, b_ref, dx_ref, dw_ref, db_ref):
        i = pl.program_id(1)

        @pl.when(i == 0)
        def _():
            dw_ref[...] = jnp.zeros_like(dw_ref)
            db_ref[...] = jnp.zeros_like(db_ref)

        w, bb = w_ref[...], b_ref[...]
        u = u_ref[...]
        p8 = jnp.where(i == 0, 0.0, up_ref[...])
        dpre = dpre_of(d_ref[...], u, p8, w, bb)
        un = un_ref[...]
        dpre_n = dpre_of(dn_ref[...], un, u[tt - HALO:, :], w, bb)
        dpre_n = jnp.where(i == nt - 1, 0.0, dpre_n)
        dx_ref[...] = _anticausal_conv(dpre, dpre_n, w, K_SSM).astype(BF16)
        rows = [jnp.sum(dpre * _shift_down(u, p8, K_SSM - 1 - k), axis=0, keepdims=True) for k in range(K_SSM - 1)]
        rows.append(jnp.sum(dpre * u, axis=0, keepdims=True))
        rows.append(jnp.zeros((8 - K_SSM, cb), F32))
        dw_ref[...] += jnp.concatenate(rows, axis=0)
        db_ref[...] += jnp.sum(dpre, axis=0, keepdims=True)

    wspec = pl.BlockSpec((8, cb), lambda j, i: (0, j))
    bspec = pl.BlockSpec((1, cb), lambda j, i: (0, j))
    return pl.pallas_call(
        body, name="ssm_conv_bwd", grid=(D_XBC // cb, nt),
        in_specs=[dcur, dnxt, cur, prev, nxt, wspec, bspec],
        out_specs=[pl.BlockSpec((tt, cb), lambda j, i: (i, j)), wspec, bspec],
        out_shape=[jax.ShapeDtypeStruct((T, D_XBC), BF16), jax.ShapeDtypeStruct((8, D_XBC), F32),
                   jax.ShapeDtypeStruct((1, D_XBC), F32)],
        compiler_params=_cparams(("parallel", "arbitrary")),
    )(dact, dact, proj, proj, proj, w8, b)


def _shortconv_fwd(proj, w8):
    T = proj.shape[0]
    tt, cb = _tile(T, 512), 512
    nt = T // tt
    gb_s, _, _ = _conv_specs(tt, cb, OFF_CB // cb, nt)
    gc_s, gcp_s, _ = _conv_specs(tt, cb, OFF_CC // cb, nt)
    u_s, up_s, _ = _conv_specs(tt, cb, OFF_CX // cb, nt)

    def body(gb_ref, gc_ref, gcp_ref, u_ref, up_ref, w_ref, o_ref):
        v = gc_ref[...] * u_ref[...]
        vp = jnp.where(pl.program_id(1) == 0, 0.0, gcp_ref[...] * up_ref[...])
        o_ref[...] = (gb_ref[...] * _causal_conv(v, vp, w_ref[...], K_SC)).astype(BF16)

    return pl.pallas_call(
        body, name="shortconv_fwd", grid=(D_MODEL // cb, nt),
        in_specs=[gb_s, gc_s, gcp_s, u_s, up_s, pl.BlockSpec((8, cb), lambda j, i: (0, j))],
        out_specs=pl.BlockSpec((tt, cb), lambda j, i: (i, j)),
        out_shape=jax.ShapeDtypeStruct((T, D_MODEL), BF16),
        compiler_params=_cparams(("parallel", "parallel")),
    )(proj, proj, proj, proj, proj, w8)


def _shortconv_bwd(dmix, proj, w8):
    T = proj.shape[0]
    tt, cb = _tile(T, 512), 512
    nt = T // tt
    d_s, _, dn_s = _conv_specs(tt, cb, D_SSM // cb, nt)
    gb_s, _, gbn_s = _conv_specs(tt, cb, OFF_CB // cb, nt)
    gc_s, gcp_s, _ = _conv_specs(tt, cb, OFF_CC // cb, nt)
    u_s, up_s, _ = _conv_specs(tt, cb, OFF_CX // cb, nt)

    def body(d_ref, dn_ref, gb_ref, gbn_ref, gc_ref, gcp_ref, u_ref, up_ref, w_ref,
             dgb_ref, dgc_ref, du_ref, dw_ref):
        i = pl.program_id(1)

        @pl.when(i == 0)
        def _():
            dw_ref[...] = jnp.zeros_like(dw_ref)

        w = w_ref[...]
        gc, u = gc_ref[...], u_ref[...]
        v = gc * u
        vp = jnp.where(i == 0, 0.0, gcp_ref[...] * up_ref[...])
        d = d_ref[...].astype(F32)
        dgb_ref[...] = (d * _causal_conv(v, vp, w, K_SC)).astype(BF16)
        dcv = d * gb_ref[...]
        dcv_n = jnp.where(i == nt - 1, 0.0, dn_ref[...].astype(F32) * gbn_ref[...])
        dv = _anticausal_conv(dcv, dcv_n, w, K_SC)
        dgc_ref[...] = (dv * u).astype(BF16)
        du_ref[...] = (dv * gc).astype(BF16)
        rows = [jnp.sum(dcv * _shift_down(v, vp, K_SC - 1 - k), axis=0, keepdims=True) for k in range(K_SC - 1)]
        rows.append(jnp.sum(dcv * v, axis=0, keepdims=True))
        rows.append(jnp.zeros((8 - K_SC, cb), F32))
        dw_ref[...] += jnp.concatenate(rows, axis=0)

    wspec = pl.BlockSpec((8, cb), lambda j, i: (0, j))
    tok = pl.BlockSpec((tt, cb), lambda j, i: (i, j))
    return pl.pallas_call(
        body, name="shortconv_bwd", grid=(D_MODEL // cb, nt),
        in_specs=[d_s, dn_s, gb_s, gbn_s, gc_s, gcp_s, u_s, up_s, wspec],
        out_specs=[tok, tok, tok, wspec],
        out_shape=[jax.ShapeDtypeStruct((T, D_MODEL), BF16)] * 3 + [jax.ShapeDtypeStruct((8, D_MODEL), F32)],
        compiler_params=_cparams(("parallel", "arbitrary")),
    )(dmix, dmix, proj, proj, proj, proj, proj, proj, w8)


GW = HEADS_PER_GROUP * HEADDIM
HI = lax.Precision.HIGHEST


def _dot(a, b):
    return jnp.dot(a.astype(BF16), b.astype(BF16), preferred_element_type=F32)


def _dot_nt(a, b):
    return lax.dot_general(a.astype(BF16), b.astype(BF16), (((1,), (1,)), ((), ())), preferred_element_type=F32)


def _dot_tn(a, b):
    return lax.dot_general(a.astype(BF16), b.astype(BF16), (((0,), (0,)), ((), ())), preferred_element_type=F32)


def _dot_hi(a, b):
    return jnp.dot(a, b, precision=HI, preferred_element_type=F32)


def _dot_nt_hi(a, b):
    return lax.dot_general(a, b, (((1,), (1,)), ((), ())), precision=HI, preferred_element_type=F32)


def _head_cols(rows):
    parts = [jnp.broadcast_to(rows[r:r + 1, :], (HEADDIM, CHUNK)) for r in range(HEADS_PER_GROUP)]
    return jnp.concatenate(parts, axis=0).T


def _head_rows(rows):
    parts = [jnp.broadcast_to(rows[r:r + 1, :], (HEADDIM, N_STATE)) for r in range(HEADS_PER_GROUP)]
    return jnp.concatenate(parts, axis=0)


def _ssd_common(dtr, bias, alog):
    dt = _softplus(dtr + bias)
    A = -jnp.exp(alog)
    a = dt * A
    ki = lax.broadcasted_iota(jnp.int32, (CHUNK, CHUNK), 0)
    si = lax.broadcasted_iota(jnp.int32, (CHUNK, CHUNK), 1)
    upper = (ki <= si).astype(F32)
    cs = _dot_hi(a, upper)
    cs_last = jnp.broadcast_to(cs[:, CHUNK - 1:CHUNK], (8, CHUNK))
    return dt, A, a, cs, cs_last


def _decay_matrix(cs, r):
    li = lax.broadcasted_iota(jnp.int32, (CHUNK, CHUNK), 0)
    si = lax.broadcasted_iota(jnp.int32, (CHUNK, CHUNK), 1)
    causal = li >= si
    R = jnp.broadcast_to(cs[r:r + 1, :], (CHUNK, CHUNK))
    seg = jnp.where(causal, R.T - R, 0.0)
    return jnp.where(causal, jnp.exp(seg), 0.0)


def _ssd_in_specs(nc, rev):
    cix = (lambda c: nc - 1 - c) if rev else (lambda c: c)
    x_s = pl.BlockSpec((CHUNK, GW), lambda g, c: (cix(c), g))
    b_s = pl.BlockSpec((CHUNK, N_STATE), lambda g, c: (cix(c), D_SSM // N_STATE + g))
    c_s = pl.BlockSpec((CHUNK, N_STATE), lambda g, c: (cix(c), D_SSM // N_STATE + N_GROUPS + g))
    dtr_s = pl.BlockSpec((1, 8, CHUNK), lambda g, c: (g, 0, cix(c)))
    row_s = pl.BlockSpec((1, 8, CHUNK), lambda g, c: (g, 0, 0))
    drep_s = pl.BlockSpec((1, GW), lambda g, c: (0, g))
    hs_s = pl.BlockSpec((1, GW, N_STATE), lambda g, c: (cix(c), g, 0))
    return x_s, b_s, c_s, dtr_s, row_s, drep_s, hs_s


def _ssd_fwd(xbc, dtr, bias, alog, drep):
    T = xbc.shape[0]
    nc = T // CHUNK
    x_s, b_s, c_s, dtr_s, row_s, drep_s, hs_s = _ssd_in_specs(nc, False)

    def body(x_ref, b_ref, c_ref, dtr_ref, bias_ref, alog_ref, drep_ref, y_ref, hs_ref, h_scr):
        @pl.when(pl.program_id(1) == 0)
        def _():
            h_scr[...] = jnp.zeros_like(h_scr)

        x, Bm, Cm = x_ref[...], b_ref[...], c_ref[...]
        dt, A, a, cs, cs_last = _ssd_common(dtr_ref[0], bias_ref[0], alog_ref[0])
        E = _head_cols(jnp.exp(cs))
        W = _head_cols(jnp.exp(cs_last - cs) * dt)
        X = (x * _head_cols(dt)).astype(BF16)
        CB = _dot_nt(Cm, Bm)
        col = lax.broadcasted_iota(jnp.int32, (CHUNK, GW), 1) // HEADDIM
        y = jnp.zeros((CHUNK, GW), F32)
        for r in range(HEADS_PER_GROUP):
            M = CB * _decay_matrix(cs, r)
            y = y + jnp.where(col == r, _dot(M, X), 0.0)
        h = h_scr[...]
        hs_ref[0] = h
        y = y + _dot_nt(Cm, h) * E
        y_ref[...] = y + drep_ref[...] * x
        h_scr[...] = h * _head_rows(jnp.exp(cs_last)) + _dot_tn(x * W, Bm)

    return pl.pallas_call(
        body, name="ssd_fwd", grid=(N_GROUPS, nc),
        in_specs=[x_s, b_s, c_s, dtr_s, row_s, row_s, drep_s],
        out_specs=[x_s, hs_s],
        out_shape=[jax.ShapeDtypeStruct((T, D_SSM), F32), jax.ShapeDtypeStruct((nc, D_SSM, N_STATE), F32)],
        scratch_shapes=[pltpu.VMEM((GW, N_STATE), F32)],
        compiler_params=_cparams(("parallel", "arbitrary")),
    )(xbc, xbc, xbc, dtr, bias, alog, drep)


def _ssd_bwd(xbc, dtr, bias, alog, drep, dy, hs):
    T = xbc.shape[0]
    nc = T // CHUNK
    x_s, b_s, c_s, dtr_s, row_s, drep_s, hs_s = _ssd_in_specs(nc, True)
    bc_out = pl.BlockSpec((CHUNK, N_STATE), lambda g, c: (nc - 1 - c, g))

    def body(x_ref, b_ref, c_ref, dtr_ref, bias_ref, alog_ref, drep_ref, dy_ref, hs_ref,
             dx_ref, db_ref, dc_ref, ddtr_ref, dbias_ref, dalog_ref, dd_ref, dh_scr):
        @pl.when(pl.program_id(1) == 0)
        def _():
            dh_scr[...] = jnp.zeros_like(dh_scr)
            dbias_ref[...] = jnp.zeros_like(dbias_ref)
            dalog_ref[...] = jnp.zeros_like(dalog_ref)
            dd_ref[...] = jnp.zeros_like(dd_ref)

        x, Bm, Cm, dY = x_ref[...], b_ref[...], c_ref[...], dy_ref[...]
        dt, A, a, cs, cs_last = _ssd_common(dtr_ref[0], bias_ref[0], alog_ref[0])
        E = _head_cols(jnp.exp(cs))
        DT = _head_cols(dt)
        Wd = _head_cols(jnp.exp(cs_last - cs))
        X = x * DT
        h = hs_ref[0]
        dS = dh_scr[...]
        CB = _dot_nt(Cm, Bm)
        col = lax.broadcasted_iota(jnp.int32, (CHUNK, GW), 1) // HEADDIM
        rowid = lax.broadcasted_iota(jnp.int32, (8, CHUNK), 0)
        lane = lax.broadcasted_iota(jnp.int32, (8, CHUNK), 1)
        hsel = (lax.broadcasted_iota(jnp.int32, (8, GW), 1) // HEADDIM
                == lax.broadcasted_iota(jnp.int32, (8, GW), 0)).astype(F32)
        ones8 = jnp.ones((8, CHUNK), F32)

        dX = jnp.zeros((CHUNK, GW), F32)
        dCB = jnp.zeros((CHUNK, CHUNK), F32)
        dcs = jnp.zeros((8, CHUNK), F32)
        for r in range(HEADS_PER_GROUP):
            L = _decay_matrix(cs, r)
            M = CB * L
            G = _dot_nt(jnp.where(col == r, dY, 0.0), X)
            GL = G * L
            dCB = dCB + GL
            Wm = GL * CB
            colsum = jnp.sum(Wm, axis=0, keepdims=True)
            rowsum = _dot_nt_hi(ones8, Wm)
            dcs = dcs + jnp.where(rowid == r, rowsum - colsum, 0.0)
            dX = dX + jnp.where(col == r, _dot_tn(M, dY), 0.0)
        dC = _dot(dCB, Bm)
        dB = _dot_tn(dCB, Cm)
        T1 = _dot_nt(Bm, dS)
        dX = dX + T1 * Wd
        dB = dB + _dot(X * Wd, dS)
        pdec = _dot_nt_hi(hsel, X * T1 * Wd)
        dcs = dcs - pdec
        dlast = jnp.sum(pdec, axis=1, keepdims=True) \
            + jnp.exp(cs_last[:, 0:1]) * jnp.sum(_dot_hi(hsel, dS * h), axis=1, keepdims=True)
        dYE = dY * E
        dC = dC + _dot(dYE, h)
        yoff = _dot_nt(Cm, h) * E
        dcs = dcs + _dot_nt_hi(hsel, dY * yoff)
        dcs = dcs + jnp.where(lane == CHUNK - 1, dlast, 0.0)
        ki = lax.broadcasted_iota(jnp.int32, (CHUNK, CHUNK), 0)
        si = lax.broadcasted_iota(jnp.int32, (CHUNK, CHUNK), 1)
        lower = (ki >= si).astype(F32)
        da = _dot_hi(dcs, lower)
        ddt = da * A + _dot_nt_hi(hsel, dX * x)
        ddtr = ddt * _sigmoid(dtr_ref[0] + bias_ref[0])
        ddtr_ref[0] = ddtr
        dbias_ref[0] += ddtr
        dalog_ref[0] += da * a
        dx_ref[...] = dX * DT + drep_ref[...] * dY
        dd_ref[...] += jnp.sum(dY * x, axis=0, keepdims=True)
        db_ref[...] = dB
        dc_ref[...] = dC
        dh_scr[...] = dS * _head_rows(jnp.exp(cs_last)) + _dot_tn(dYE, Cm)

    return pl.pallas_call(
        body, name="ssd_bwd", grid=(N_GROUPS, nc),
        in_specs=[x_s, b_s, c_s, dtr_s, row_s, row_s, drep_s, x_s, hs_s],
        out_specs=[x_s, bc_out, bc_out, dtr_s, row_s, row_s, drep_s],
        out_shape=[jax.ShapeDtypeStruct((T, D_SSM), F32),
                   jax.ShapeDtypeStruct((T, N_GROUPS * N_STATE), F32),
                   jax.ShapeDtypeStruct((T, N_GROUPS * N_STATE), F32),
                   jax.ShapeDtypeStruct((N_GROUPS, 8, T), F32),
                   jax.ShapeDtypeStruct((N_GROUPS, 8, CHUNK), F32),
                   jax.ShapeDtypeStruct((N_GROUPS, 8, CHUNK), F32),
                   jax.ShapeDtypeStruct((1, D_SSM), F32)],
        scratch_shapes=[pltpu.VMEM((GW, N_STATE), F32)],
        compiler_params=_cparams(("parallel", "arbitrary")),
    )(xbc, xbc, xbc, dtr, bias, alog, drep, dy, hs)


def _adamw(w, g, m, v, name, deps=()):
    R, C = w.shape
    tr = _tile(R, 256, 8)
    nd = len(deps)

    def body(w_ref, g_ref, m_ref, v_ref, *rest):
        d_ref, mo_ref, vo_ref = rest[nd:]
        gv = g_ref[...]
        mn = ADAM_B1 * m_ref[...] + (1.0 - ADAM_B1) * gv
        vn = ADAM_B2 * v_ref[...] + (1.0 - ADAM_B2) * (gv * gv)
        m_hat = mn / (1.0 - ADAM_B1 ** ADAM_STEP)
        v_hat = vn / (1.0 - ADAM_B2 ** ADAM_STEP)
        d_ref[...] = -ADAM_LR * (m_hat / (jnp.sqrt(v_hat) + ADAM_EPS) + ADAM_WD * w_ref[...])
        mo_ref[...] = mn
        vo_ref[...] = vn

    spec = pl.BlockSpec((tr, C), lambda i: (i, 0))
    return pl.pallas_call(
        body, name=name, grid=(R // tr,),
        in_specs=[spec] * 4 + [ANY] * nd, out_specs=[spec] * 3,
        out_shape=[jax.ShapeDtypeStruct((R, C), F32)] * 3,
        compiler_params=_cparams(("parallel",)),
    )(w, g, m, v, *deps)


ANY = pl.BlockSpec(memory_space=pl.ANY)


def _place():
    x, y, c = lax.axis_index("x"), lax.axis_index("y"), lax.axis_index("c")
    return x, y, c


def _other_chips(x, y):
    return [(1 - x, y), (x, 1 - y), (1 - x, 1 - y)]


def _allgather_inplace(bufs):
    n = len(bufs)

    def body(*refs):
        o_refs = refs[n:2 * n]
        send_sems, recv_sems = refs[2 * n:]
        x, y, c = _place()
        sibling = (x, y, 1 - c)
        chips = _other_chips(x, y)

        def copy(k, slot, px, py, pc, to):
            blk = o_refs[k].at[4 * px + 2 * py + pc]
            return pltpu.make_async_remote_copy(
                src_ref=blk, dst_ref=blk, send_sem=send_sems.at[k, slot], recv_sem=recv_sems.at[k, slot],
                device_id=to, device_id_type=MESH)

        sent = []
        for k in range(n):
            for j, (px, py) in enumerate(chips):
                cp = copy(k, j, x, y, c, (px, py, c))
                cp.start()
                sent.append(cp)
        for k in range(n):
            for j, (px, py) in enumerate(chips):
                copy(k, j, px, py, c, (px, py, c)).wait_recv()
                fwd = copy(k, 3 + j, px, py, c, sibling)
                fwd.start()
                sent.append(fwd)
        for k in range(n):
            for j, (px, py) in enumerate(chips):
                copy(k, 3 + j, px, py, 1 - c, sibling).wait_recv()
        for cp in sent:
            cp.wait_send()

    return pl.pallas_call(
        body, name="allgather_w_in",
        in_specs=[ANY] * n, out_specs=[ANY] * n,
        out_shape=[jax.ShapeDtypeStruct(b.shape, b.dtype) for b in bufs],
        input_output_aliases={k: k for k in range(n)},
        scratch_shapes=[pltpu.SemaphoreType.DMA((n, 6)), pltpu.SemaphoreType.DMA((n, 6))],
    )(*bufs)


HBM = pl.BlockSpec(memory_space=pltpu.HBM)
SEM = pl.BlockSpec(memory_space=pltpu.SEMAPHORE)
EFFECT = pltpu.SideEffectType.DATAFLOW_SIDE_EFFECTING


def _split_start(name, arrays, build, n_copies, after=()):
    na, nd = len(arrays), len(after)

    def body(*refs):
        send_sems, recv_sems = refs[na + nd], refs[na + nd + 1]
        for cp in build(refs[:na], send_sems, recv_sems):
            cp.start()
        refs[-1][...] = jnp.zeros((8, 128), F32)

    outs = pl.pallas_call(
        body, name=name,
        out_shape=(pltpu.SemaphoreType.DMA((n_copies,)), pltpu.SemaphoreType.DMA((n_copies,)),
                   *[pltpu.HBM(a.shape, a.dtype) for a in arrays], jax.ShapeDtypeStruct((8, 128), F32)),
        in_specs=[HBM] * na + [ANY] * nd,
        out_specs=(SEM, SEM, *[HBM] * na, pl.BlockSpec(memory_space=pltpu.VMEM)),
        input_output_aliases={i: 2 + i for i in range(na)},
        compiler_params=pltpu.CompilerParams(has_side_effects=EFFECT),
    )(*[pltpu.with_memory_space_constraint(a, pltpu.HBM) for a in arrays], *after)
    return outs[0], outs[1], list(outs[2:2 + na]), outs[-1]


def _split_wait(name, send_sems, recv_sems, arrays, build, after):
    na = len(arrays)

    def body(*refs):
        for cp in build(refs[:na], refs[na], refs[na + 1]):
            cp.wait_send()
            cp.wait_recv()

    outs = pl.pallas_call(
        body, name=name,
        out_shape=tuple(pltpu.HBM(a.shape, a.dtype) for a in arrays),
        in_specs=[HBM] * na + [SEM, SEM] + [ANY] * len(after),
        out_specs=tuple([HBM] * na),
        input_output_aliases={i: i for i in range(na)},
        compiler_params=pltpu.CompilerParams(has_side_effects=EFFECT),
    )(*arrays, send_sems, recv_sems, *after)
    return list(outs)


def _remote(src, dst, send_sems, recv_sems, i, to):
    return pltpu.make_async_remote_copy(src_ref=src, dst_ref=dst, send_sem=send_sems.at[i], recv_sem=recv_sems.at[i],
                                        device_id=to, device_id_type=MESH)


def _build_ag_ici(refs, ss, rs):
    x, y, c = _place()
    cps = []
    for k, ref in enumerate(refs):
        blk = ref.at[4 * x + 2 * y + c]
        for j, (px, py) in enumerate(_other_chips(x, y)):
            cps.append(_remote(blk, blk, ss, rs, 3 * k + j, (px, py, c)))
    return cps


def _build_ag_fwd(refs, ss, rs):
    x, y, c = _place()
    cps = []
    for k, ref in enumerate(refs):
        for j, (px, py) in enumerate(_other_chips(x, y)):
            blk = ref.at[4 * px + 2 * py + c]
            cps.append(_remote(blk, blk, ss, rs, 3 * k + j, (x, y, 1 - c)))
    return cps


def _build_rs_swap(refs, ss, rs):
    x, y, c = _place()
    n = len(refs) // 2
    return [_remote(refs[k].at[:, pl.ds(1 - c, 1)], refs[n + k], ss, rs, k, (x, y, 1 - c)) for k in range(n)]


def _build_rs_ici(refs, ss, rs):
    x, y, c = _place()
    n = len(refs) // 2
    me = 2 * x + y
    cps = []
    for k in range(n):
        for j, (px, py) in enumerate(_other_chips(x, y)):
            cps.append(_remote(refs[k].at[2 * px + py], refs[n + k].at[me], ss, rs, 3 * k + j, (px, py, c)))
    return cps


def _build_rs_share(refs, ss, rs):
    x, y, c = _place()
    return [_remote(ref.at[c], ref.at[c], ss, rs, k, (x, y, 1 - c)) for k, ref in enumerate(refs)]


def _allreduce_small(p, deps=()):
    R, C = p.shape
    nd = len(deps)

    def body(p_ref, *rest):
        gath_ref, sum_ref, send_sems, recv_sems, local_sem = rest[nd:]
        x, y, c = _place()
        me, sibling = (x, y, c), (x, y, 1 - c)
        chips = [(1 - x, y), (x, 1 - y), (1 - x, 1 - y)]

        def blk(px, py, pc):
            return gath_ref.at[4 * px + 2 * py + pc]

        def copy(k, block, to, src=None):
            return pltpu.make_async_remote_copy(
                src_ref=blk(*block) if src is None else src, dst_ref=blk(*block),
                send_sem=send_sems.at[k], recv_sem=recv_sems.at[k], device_id=to, device_id_type=MESH)

        mine = pltpu.make_async_copy(p_ref, blk(*me), local_sem)
        mine.start()
        first = [copy(0, me, sibling, src=p_ref)]
        first += [copy(1 + j, me, (*chip, c), src=p_ref) for j, chip in enumerate(chips)]
        for cp in first:
            cp.start()
        passed = [copy(4 + j, (*chip, c), sibling) for j, chip in enumerate(chips)]
        for j, chip in enumerate(chips):
            copy(1 + j, (*chip, c), me).wait_recv()
            passed[j].start()
        copy(0, sibling, me).wait_recv()
        for j, chip in enumerate(chips):
            copy(4 + j, (*chip, 1 - c), me).wait_recv()
        for cp in first + passed:
            cp.wait_send()
        mine.wait()
        s = gath_ref[0]
        for d in range(1, N_DEV):
            s = s + gath_ref[d]
        sum_ref[...] = s

    vm = pl.BlockSpec(memory_space=pltpu.VMEM)
    return pl.pallas_call(
        body, name="allreduce_small",
        in_specs=[vm] + [ANY] * nd, out_specs=[vm, vm],
        out_shape=[jax.ShapeDtypeStruct((N_DEV, R, C), F32), jax.ShapeDtypeStruct((R, C), F32)],
        scratch_shapes=[pltpu.SemaphoreType.DMA((7,)), pltpu.SemaphoreType.DMA((7,)), pltpu.SemaphoreType.DMA],
    )(p, *deps)[1]


def _rs_add_pair(p, r0, c_arr, name):
    _, _, hr, cols = p.shape
    tr = _tile(hr, 256, 8)

    def body(c_ref, p_ref, r_ref, q_ref):
        q_ref[...] = (p_ref[0] + r_ref[0]).astype(BF16)

    grid_spec = pltpu.PrefetchScalarGridSpec(
        num_scalar_prefetch=1, grid=(N_CHIPS, hr // tr),
        in_specs=[pl.BlockSpec((1, 1, tr, cols), lambda j, i, c_ref: (j, c_ref[0], i, 0)),
                  pl.BlockSpec((1, 1, tr, cols), lambda j, i, c_ref: (j, 0, i, 0))],
        out_specs=pl.BlockSpec((1, tr, cols), lambda j, i, c_ref: (j, i, 0)))
    return pl.pallas_call(
        body, name=name, grid_spec=grid_spec,
        out_shape=jax.ShapeDtypeStruct((N_CHIPS, hr, cols), BF16),
        compiler_params=_cparams(("parallel", "parallel")),
    )(c_arr, p, r0)


def _rs_add_chips(r1, q, place_arr, name):
    _, hr, cols = r1.shape
    tr = _tile(hr, 256, 8)

    def body(place_ref, r_ref, q_ref, o_ref):
        chip = place_ref[0]
        s = None
        for j in range(N_CHIPS):
            t = jnp.where(chip == j, q_ref[j], r_ref[j]).astype(F32)
            s = t if s is None else s + t
        o_ref[...] = s

    blk = pl.BlockSpec((N_CHIPS, tr, cols), lambda i, place_ref: (0, i, 0))
    grid_spec = pltpu.PrefetchScalarGridSpec(
        num_scalar_prefetch=1, grid=(hr // tr,), in_specs=[blk, blk],
        out_specs=pl.BlockSpec((None, tr, cols), lambda i, place_ref: (place_ref[1], i, 0)))
    return pl.pallas_call(
        body, name=name, grid_spec=grid_spec,
        out_shape=jax.ShapeDtypeStruct((2, hr, cols), F32),
        compiler_params=_cparams(("parallel",)),
    )(place_arr, r1, q)


def _pad_rows(a, rows):
    return jnp.pad(a, ((0, rows - a.shape[0]), (0, 0)))


def _pad_cols(a, cols):
    return jnp.pad(a, ((0, 0), (0, cols - a.shape[1])))


def _heads_to_rows(v):
    v = v.reshape(N_GROUPS, HEADS_PER_GROUP, 1)
    v = jnp.pad(v, ((0, 0), (0, 8 - HEADS_PER_GROUP), (0, 0)))
    return jnp.broadcast_to(v, (N_GROUPS, 8, CHUNK))


def _rows_to_heads(a):
    return jnp.sum(a[:, :HEADS_PER_GROUP, :], axis=-1).reshape(N_HEADS)


def kernel(x, norm_mix_g, w_in, ssm_conv_w, ssm_conv_b, ssm_dt_bias, ssm_A_log, ssm_D, ssm_norm_g, sc_conv_w, w_out, norm_ffn_g, w_gate, w_up, w_down, norm_final_g, loss_target, m_norm_mix_g, m_w_in, m_ssm_conv_w, m_ssm_conv_b, m_ssm_dt_bias, m_ssm_A_log, m_ssm_D, m_ssm_norm_g, m_sc_conv_w, m_w_out, m_norm_ffn_g, m_w_gate, m_w_up, m_w_down, m_norm_final_g, v_norm_mix_g, v_w_in, v_ssm_conv_w, v_ssm_conv_b, v_ssm_dt_bias, v_ssm_A_log, v_ssm_D, v_ssm_norm_g, v_sc_conv_w, v_w_out, v_norm_ffn_g, v_w_gate, v_w_up, v_w_down, v_norm_final_g):
    T = x.shape[1]
    xt = x[0]
    tgt = loss_target[0]
    cx, cy, cc = lax.axis_index("x"), lax.axis_index("y"), lax.axis_index("c")
    chip = 2 * cx + cy
    c_arr = jnp.reshape(cc, (1,)).astype(jnp.int32)
    chip_arr = jnp.reshape(chip, (1,)).astype(jnp.int32)
    place_arr = jnp.stack([chip, cc]).astype(jnp.int32)

    big = [w_in[0], w_out[0], w_gate[0], w_up[0], w_down[0]]
    names = ["w_in", "w_out", "w_gate", "w_up", "w_down"]
    gbufs = [_cast_into_gather(w, chip_arr, "cast_" + nm) for w, nm in zip(big, names)]
    (g_in,) = _allgather_inplace([gbufs[0]])
    ag_ss, ag_rs, ag_bufs, ag_tok = _split_start("ag_ici_start", gbufs[1:], _build_ag_ici, 12, after=[g_in])
    w_in_f = g_in.reshape(N_CHIPS, D_MODEL, D_IN // N_CHIPS).transpose(1, 0, 2).reshape(D_MODEL, D_IN)
    w_main = jnp.concatenate([w_in_f[:, :OFF_CB], w_in_f[:, OFF_CB + N_HEADS:]], axis=1)
    w_dt = _pad_cols(w_in_f[:, OFF_CB:OFF_CB + N_HEADS], DT_PAD)

    contrib = (cc == 0).astype(F32)
    place_ssm = jnp.zeros((8, D_XBC), F32)
    place_ssm = lax.dynamic_update_slice(place_ssm, _pad_rows(ssm_conv_w[0], 8) * contrib, (0, chip * (D_XBC // N_CHIPS)))
    place_sc = jnp.zeros((8, D_MODEL), F32)
    place_sc = lax.dynamic_update_slice(place_sc, _pad_rows(sc_conv_w[0], 8) * contrib, (0, chip * (D_MODEL // N_CHIPS)))
    convs = _allreduce_small(jnp.concatenate([place_ssm, _pad_cols(place_sc, D_XBC)], axis=0))
    ssm_w8 = convs[:8]
    sc_w8 = convs[8:, :D_MODEL]

    bias_rows = _heads_to_rows(ssm_dt_bias[0])
    alog_rows = _heads_to_rows(ssm_A_log[0])
    drep = jnp.repeat(ssm_D[0], HEADDIM).reshape(1, D_SSM)

    n1 = _rmsnorm_fwd(xt, _tie(norm_mix_g, ag_tok, "tie_ag_ici"), "rmsnorm_mix")
    (proj,) = _matmul([(n1, w_main)], out_dtypes=[F32], name="mm_proj")
    (dt_raw,) = _matmul([(n1, w_dt)], out_dtypes=[F32], name="mm_proj_dt", tk=2048)
    ag_bufs = _split_wait("ag_ici_wait", ag_ss, ag_rs, ag_bufs, _build_ag_ici, after=[dt_raw])
    fw_ss, fw_rs, fw_bufs, fw_tok = _split_start("ag_fwd_start", ag_bufs, _build_ag_fwd, 12)
    xbc = _ssm_conv_fwd(proj, ssm_w8, _tie(ssm_conv_b, fw_tok, "tie_ag_fwd"))
    dtr = jnp.pad(dt_raw[:, :N_HEADS].T.reshape(N_GROUPS, HEADS_PER_GROUP, T), ((0, 0), (0, 4), (0, 0)))
    y_ssd, hs = _ssd_fwd(xbc, dtr, bias_rows, alog_rows, drep)
    y_ssm = _gated_norm_fwd(y_ssd, proj, ssm_norm_g)
    y_sc = _shortconv_fwd(proj, sc_w8)
    gath = _split_wait("ag_fwd_wait", fw_ss, fw_rs, fw_bufs, _build_ag_fwd, after=[y_sc])
    w_out_f = gath[0].reshape(2 * D_MODEL, D_MODEL)
    w_gate_f = gath[1].reshape(N_CHIPS, D_MODEL, D_FF // N_CHIPS).transpose(1, 0, 2).reshape(D_MODEL, D_FF)
    w_up_f = gath[2].reshape(N_CHIPS, D_MODEL, D_FF // N_CHIPS).transpose(1, 0, 2).reshape(D_MODEL, D_FF)
    w_down_f = gath[3].reshape(D_FF, D_MODEL)
    y_mix = jnp.concatenate([y_ssm, y_sc], axis=1)
    (h1,) = _matmul([(y_mix, w_out_f)], out_dtypes=[F32], name="mm_out", extras=[xt],
                    epilogue=lambda acc, res: (acc + res,))
    n2 = _rmsnorm_fwd(h1, norm_ffn_g, "rmsnorm_ffn")
    g_act, u_act, a_act = _ffn_fwd(n2, w_gate_f, w_up_f)
    (h2,) = _matmul([(a_act, w_down_f)], out_dtypes=[F32], name="mm_down", extras=[h1],
                    epilogue=lambda acc, res: (acc + res,))

    dh2, dh2b, dg_final, loss_part = _loss_and_final_bwd(h2, tgt, norm_final_g.reshape(1, D_MODEL))
    dg_act, du_act = _matmul([(dh2b, w_down_f)], tb=True, out_dtypes=[BF16, BF16], name="mm_down_bwd",
                             tn=512, extras=[g_act, u_act], epilogue=_swiglu_bwd, nsub=4)
    (dw_down,) = _matmul([(a_act, dh2b)], ta=True, out_dtypes=[F32], name="mm_dw_down", tm=1408)
    (dn2,) = _matmul([(dg_act, w_gate_f), (du_act, w_up_f)], tb=True, out_dtypes=[BF16], name="mm_ffn_in_bwd")
    (dw_gate,) = _matmul([(n2, dg_act)], ta=True, out_dtypes=[F32], name="mm_dw_gate", tn=1408, col_shards=True)
    (dw_up,) = _matmul([(n2, du_act)], ta=True, out_dtypes=[F32], name="mm_dw_up", tn=1408, col_shards=True)
    dh1, dh1b, dg_ffn = _rmsnorm_bwd(dn2, h1, norm_ffn_g, dh2, "rmsnorm_ffn_bwd")
    (dw_out,) = _matmul([(y_mix, dh1b)], ta=True, out_dtypes=[F32], name="mm_dw_out")

    def col_blocks(g):
        R, Ctot = g.shape
        return g.reshape(R, N_CHIPS, Ctot // N_CHIPS).transpose(1, 0, 2).reshape(N_CHIPS, 2, R // 2, Ctot // N_CHIPS)

    def halves(g):
        return g.reshape(N_CHIPS, 2, g.shape[1] // 2, g.shape[2])

    def landing(shape, dtype):
        return lax.empty(shape, dtype)

    names1 = names[1:]
    ps1 = [halves(dw_out.reshape(N_CHIPS, -1, D_MODEL)), halves(dw_gate), halves(dw_up),
           halves(dw_down.reshape(N_CHIPS, -1, D_MODEL))]
    r0_1 = [landing((N_CHIPS, 1) + p.shape[2:], F32) for p in ps1]
    sw_ss, sw_rs, sw_arr, sw_tok = _split_start("rs1_swap_start", ps1 + r0_1, _build_rs_swap, 4)
    (dmix,) = _matmul([(dh1b, w_out_f)], tb=True, out_dtypes=[BF16], name="mm_out_bwd", deps=[sw_tok])
    dgb, dgc, du_sc, dw_sc = _shortconv_bwd(dmix, proj, sc_w8)
    dy_ssd, dz, dg_ssmnorm = _gated_norm_bwd(dmix, y_ssd, proj, ssm_norm_g)
    sw_arr = _split_wait("rs1_swap_wait", sw_ss, sw_rs, sw_arr, _build_rs_swap, after=[dz])
    qs1 = [_rs_add_pair(p, r, c_arr, "rs_add_pair_" + nm) for p, r, nm in zip(sw_arr[:4], sw_arr[4:], names1)]
    r1_1 = [landing(q.shape, BF16) for q in qs1]
    ic_ss, ic_rs, ic_arr, ic_tok = _split_start("rs1_ici_start", qs1 + r1_1, _build_rs_ici, 12)
    dxs, dB, dC, ddtr, dbias_acc, dalog_acc, dD_acc = _ssd_bwd(
        xbc, dtr, bias_rows, alog_rows, _tie(drep, ic_tok, "tie_rs1_ici"), dy_ssd, hs)
    dxbc, dw_ssmconv, db_ssmconv = _ssm_conv_bwd(jnp.concatenate([dxs, dB, dC], axis=1), proj, ssm_w8, ssm_conv_b)
    ic_arr = _split_wait("rs1_ici_wait", ic_ss, ic_rs, ic_arr, _build_rs_ici, after=[dxbc])
    g1 = [_rs_add_chips(r, q, place_arr, "rs_add_chips_" + nm) for q, r, nm in zip(ic_arr[:4], ic_arr[4:], names1)]
    sh_ss, sh_rs, sh_arr, sh_tok = _split_start("rs1_share_start", g1, _build_rs_share, 4)

    dproj = jnp.concatenate([dz, dxbc, dgb, dgc, du_sc], axis=1)
    ddt_raw = _pad_cols(ddtr[:, :HEADS_PER_GROUP, :].reshape(N_HEADS, T).T, DT_PAD).astype(BF16)
    (dw_main,) = _matmul([(n1, dproj)], ta=True, out_dtypes=[F32], name="mm_dw_main", deps=[sh_tok])
    (dw_dt,) = _matmul([(n1, ddt_raw)], ta=True, out_dtypes=[F32], name="mm_dw_dt", tk=2048)
    dw_in_full = jnp.concatenate([dw_main[:, :OFF_CB], dw_dt[:, :N_HEADS], dw_main[:, OFF_CB:]], axis=1)
    p_in = col_blocks(dw_in_full)
    s2_ss, s2_rs, s2_arr, s2_tok = _split_start(
        "rs2_swap_start", [p_in, landing((N_CHIPS, 1) + p_in.shape[2:], F32)], _build_rs_swap, 1)
    (dn1a,) = _matmul([(dproj, w_main)], tb=True, out_dtypes=[F32], name="mm_proj_bwd", deps=[s2_tok])
    g1 = _split_wait("rs1_share_wait", sh_ss, sh_rs, sh_arr, _build_rs_share, after=[dn1a])
    s2_arr = _split_wait("rs2_swap_wait", s2_ss, s2_rs, s2_arr, _build_rs_swap, after=[dn1a])
    q_in = _rs_add_pair(s2_arr[0], s2_arr[1], c_arr, "rs_add_pair_w_in")
    i2_ss, i2_rs, i2_arr, i2_tok = _split_start(
        "rs2_ici_start", [q_in, landing(q_in.shape, BF16)], _build_rs_ici, 3)
    (dn1,) = _matmul([(ddt_raw, w_dt)], tb=True, out_dtypes=[BF16], name="mm_proj_dt_bwd", extras=[dn1a],
                     epilogue=lambda acc, res: (acc + res,), deps=[i2_tok])
    dx, _, dg_mix = _rmsnorm_bwd(dn1, xt, norm_mix_g, dh1, "rmsnorm_mix_bwd")

    big_m = [m_w_in[0], m_w_out[0], m_w_gate[0], m_w_up[0], m_w_down[0]]
    big_v = [v_w_in[0], v_w_out[0], v_w_gate[0], v_w_up[0], v_w_down[0]]
    big_grads = [None] + [g.reshape(w.shape) for g, w in zip(g1, big[1:])]
    big_out = {}
    for k in range(1, 5):
        big_out[names[k]] = _adamw(big[k], big_grads[k], big_m[k], big_v[k], "adamw_" + names[k], deps=[i2_tok])
    i2_arr = _split_wait("rs2_ici_wait", i2_ss, i2_rs, i2_arr, _build_rs_ici, after=[big_out[names[4]][0], dx])
    g_in_red = _rs_add_chips(i2_arr[1], i2_arr[0], place_arr, "rs_add_chips_w_in")
    s3_ss, s3_rs, s3_arr, s3_tok = _split_start("rs2_share_start", [g_in_red], _build_rs_share, 1)

    dD = jnp.sum(dD_acc.reshape(N_HEADS, HEADDIM), axis=-1)
    heads_row = jnp.concatenate([_rows_to_heads(dbias_acc), _rows_to_heads(dalog_acc), dD,
                                 loss_part.reshape(1)]).reshape(1, -1)
    small = jnp.concatenate([
        dw_ssmconv,
        _pad_cols(dw_sc, D_XBC),
        db_ssmconv,
        jnp.concatenate([dg_mix, dg_ssmnorm], axis=1),
        jnp.concatenate([dg_ffn, dg_final], axis=1),
        _pad_cols(heads_row, D_XBC),
        jnp.zeros((4, D_XBC), F32),
    ], axis=0)
    tot = _allreduce_small(small, deps=[s3_tok])
    loss = tot[19, 3 * N_HEADS]

    cs_ssm, cs_sc = D_XBC // N_CHIPS, D_MODEL // N_CHIPS
    g_ssm_conv = lax.dynamic_slice(tot[0:K_SSM], (0, chip * cs_ssm), (K_SSM, cs_ssm))
    g_sc_conv = lax.dynamic_slice(tot[8:8 + K_SC, :D_MODEL], (0, chip * cs_sc), (K_SC, cs_sc))
    small_grads = {
        "norm_mix_g": tot[17:18, :D_MODEL], "ssm_conv_w": g_ssm_conv, "ssm_conv_b": tot[16:17],
        "ssm_dt_bias": tot[19:20, 0:N_HEADS], "ssm_A_log": tot[19:20, N_HEADS:2 * N_HEADS],
        "ssm_D": tot[19:20, 2 * N_HEADS:3 * N_HEADS], "ssm_norm_g": tot[17:18, D_MODEL:],
        "sc_conv_w": g_sc_conv, "norm_ffn_g": tot[18:19, :D_MODEL], "norm_final_g": tot[18:19, D_MODEL:],
    }
    small_w = {"norm_mix_g": (norm_mix_g, m_norm_mix_g, v_norm_mix_g),
               "ssm_conv_w": (ssm_conv_w[0], m_ssm_conv_w[0], v_ssm_conv_w[0]),
               "ssm_conv_b": (ssm_conv_b, m_ssm_conv_b, v_ssm_conv_b),
               "ssm_dt_bias": (ssm_dt_bias, m_ssm_dt_bias, v_ssm_dt_bias),
               "ssm_A_log": (ssm_A_log, m_ssm_A_log, v_ssm_A_log),
               "ssm_D": (ssm_D, m_ssm_D, v_ssm_D),
               "ssm_norm_g": (ssm_norm_g, m_ssm_norm_g, v_ssm_norm_g),
               "sc_conv_w": (sc_conv_w[0], m_sc_conv_w[0], v_sc_conv_w[0]),
               "norm_ffn_g": (norm_ffn_g, m_norm_ffn_g, v_norm_ffn_g),
               "norm_final_g": (norm_final_g.reshape(1, -1), m_norm_final_g.reshape(1, -1),
                                v_norm_final_g.reshape(1, -1))}
    PW = 1024
    order = list(small_w)

    def pack(arrs):
        rows = []
        for a in arrs:
            flat = a.reshape(-1)
            n = -(-flat.shape[0] // PW) * PW
            rows.append(jnp.pad(flat, (0, n - flat.shape[0])).reshape(-1, PW))
        slab = jnp.concatenate(rows, axis=0)
        return _pad_rows(slab, -(-slab.shape[0] // 8) * 8)

    wp = pack([small_w[k][0] for k in order])
    mp = pack([small_w[k][1] for k in order])
    vp = pack([small_w[k][2] for k in order])
    gp = pack([small_grads[k] for k in order])
    sd, sm, sv = _adamw(wp, gp, mp, vp, "adamw_small")

    def unpack(slab):
        out, row = {}, 0
        for k in order:
            shape = small_w[k][0].shape
            size = 1
            for s in shape:
                size *= s
            nr = -(-size // PW)
            out[k] = slab[row:row + nr].reshape(-1)[:size].reshape(shape)
            row += nr
        return out

    s_delta, s_m, s_v = unpack(sd), unpack(sm), unpack(sv)

    (g_in_full,) = _split_wait("rs2_share_wait", s3_ss, s3_rs, s3_arr, _build_rs_share, after=[sd])
    big_grads[0] = g_in_full.reshape(big[0].shape)
    big_out[names[0]] = _adamw(big[0], big_grads[0], big_m[0], big_v[0], "adamw_" + names[0])
    big_g = dict(zip(names, big_grads))

    weight_order = ["norm_mix_g", "w_in", "ssm_conv_w", "ssm_conv_b", "ssm_dt_bias", "ssm_A_log", "ssm_D",
                    "ssm_norm_g", "sc_conv_w", "w_out", "norm_ffn_g", "w_gate", "w_up", "w_down", "norm_final_g"]
    lead = {"ssm_conv_w", "sc_conv_w", "w_in", "w_out", "w_gate", "w_up", "w_down"}

    def shaped(nm, a):
        if nm == "norm_final_g":
            return a.reshape(D_MODEL)
        return a[None] if nm in lead else a

    grads, deltas, new_m, new_v = [], [], [], []
    for nm in weight_order:
        if nm in big_out:
            g, (d, m, v) = big_g[nm], big_out[nm]
        else:
            g, d, m, v = small_grads[nm], s_delta[nm], s_m[nm], s_v[nm]
        grads.append(shaped(nm, g))
        deltas.append(shaped(nm, d))
        new_m.append(shaped(nm, m))
        new_v.append(shaped(nm, v))
    return (loss, dx[None], *grads, *deltas, *new_m, *new_v)


def _swiglu_bwd(da, g, u):
    gf, uf = g.astype(F32), u.astype(F32)
    return da * uf * _dsilu(gf), da * _silu(gf)


def _ffn_fwd(n2, w_gate, w_up):
    T, K = n2.shape
    N = w_gate.shape[1]
    tm, tn = _tile(T, 1024), _tile(N, 512)
    sub = _tile(tm, 256)

    def body(a_ref, wg_ref, wu_ref, g_ref, u_ref, act_ref):
        for s in range(tm // sub):
            rows = pl.ds(s * sub, sub)
            a = a_ref[rows, :]
            g = jnp.dot(a, wg_ref[...], preferred_element_type=F32)
            u = jnp.dot(a, wu_ref[...], preferred_element_type=F32)
            g_ref[rows, :] = g.astype(BF16)
            u_ref[rows, :] = u.astype(BF16)
            act_ref[rows, :] = (_silu(g) * u).astype(BF16)

    a_spec = pl.BlockSpec((tm, K), lambda i, j: (i, 0))
    b_spec = pl.BlockSpec((K, tn), lambda i, j: (0, j))
    o_spec = pl.BlockSpec((tm, tn), lambda i, j: (i, j))
    return pl.pallas_call(
        body, name="ffn_fwd", grid=(T // tm, N // tn),
        in_specs=[a_spec, b_spec, b_spec], out_specs=[o_spec] * 3,
        out_shape=[jax.ShapeDtypeStruct((T, N), BF16)] * 3,
        compiler_params=_cparams(("parallel", "parallel")),
    )(n2, w_gate, w_up)
```

```python
import functools

import jax
import jax.numpy as jnp
from jax import lax
from jax.experimental import pallas as pl
from jax.experimental.pallas import tpu as pltpu

F32 = jnp.float32
BF16 = jnp.bfloat16
MESH = pl.DeviceIdType.MESH

D_MODEL = 2048
D_SSM = 2048
HEADDIM = 64
N_HEADS = 32
N_GROUPS = 8
HEADS_PER_GROUP = 4
N_STATE = 128
CHUNK = 128
K_SSM = 4
K_SC = 3
D_XBC = 4096
D_FF = 5632
D_IN = 12320
D_MAIN = 12288
OFF_XBC, OFF_CB, OFF_CC, OFF_CX = 2048, 6144, 8192, 10240
DT_PAD = 128
EPS = 1e-5
N_CHIPS = 4
N_DEV = 8

ADAM_LR = 0.001
ADAM_B1 = 0.9
ADAM_B2 = 0.999
ADAM_EPS = 1e-08
ADAM_WD = 0.01
ADAM_STEP = 10

V7X_VMEM_BYTES = 64 * 1024 * 1024
VMEM_LIMIT = V7X_VMEM_BYTES - 8 * 1024 * 1024


def _cparams(sem=None):
    if sem is None:
        return pltpu.CompilerParams(vmem_limit_bytes=VMEM_LIMIT)
    return pltpu.CompilerParams(dimension_semantics=sem, vmem_limit_bytes=VMEM_LIMIT)


def _tile(dim, pref, unit=128):
    best = None
    t = unit
    while t <= min(dim, pref):
        if dim % t == 0:
            best = t
        t += unit
    return best if best is not None else dim


def _sigmoid(x):
    return 1.0 / (1.0 + jnp.exp(-x))


def _silu(x):
    return x * _sigmoid(x)


def _dsilu(x):
    s = _sigmoid(x)
    return s * (1.0 + x * (1.0 - s))


def _softplus(x):
    return jnp.maximum(x, 0.0) + jnp.log(1.0 + jnp.exp(-jnp.abs(x)))


MATMUL_VMEM_BUDGET = 44 * 1024 * 1024


def _matmul(pairs, *, ta=False, tb=False, out_dtypes, name, tm=1024, tn=1024, tk=None, extras=(), epilogue=None,
            deps=(), col_shards=False, nsub=1, b3d=False):
    a0, b0 = pairs[0]
    M, K = (a0.shape[1], a0.shape[0]) if ta else a0.shape
    if b3d:
        N = b0.shape[1] if tb else b0.shape[0] * b0.shape[2]
        tk, tn = (b0.shape[2], tn) if tb else (tk, b0.shape[2])
    else:
        N = b0.shape[0] if tb else b0.shape[1]
    tm, tn = _tile(M, tm, 8 if M % 128 else 128), _tile(N, tn)
    npair, nex, ndep, nout = len(pairs), len(extras), len(deps), len(out_dtypes)
    if tk is None:
        fixed = 2 * tm * tn * (sum(jnp.dtype(d).itemsize for d in out_dtypes) + sum(e.dtype.itemsize for e in extras))
        tk = K
        while tk > 128 and (K % tk or tk % 128 or
                            fixed + 2 * npair * 2 * tk * (tm + tn) + (tm * tn * 4 if tk < K else 0) > MATMUL_VMEM_BUDGET):
            tk -= 128
    else:
        tk = _tile(K, tk)
    nk = K // tk
    if nk > 1 or tm % nsub or (tm // nsub) % 128:
        nsub = 1
    sub = tm // nsub
    dims = (((0 if ta else 1,), (1 if tb else 0,)), ((), ()))

    def body(*refs):
        a_refs = refs[0:2 * npair:2]
        b_refs = refs[1:2 * npair:2]
        ex_refs = refs[2 * npair:2 * npair + nex]
        o_refs = refs[2 * npair + nex + ndep:2 * npair + nex + ndep + nout]

        def dots(rows):
            s = None
            for a_ref, b_ref in zip(a_refs, b_refs):
                a = a_ref[...] if rows is None else (a_ref[:, rows] if ta else a_ref[rows, :])
                d = lax.dot_general(a, b_ref[...], dims, preferred_element_type=F32)
                s = d if s is None else s + d
            return s

        def finish(r, rows):
            ex = [e[...] if rows is None else e[rows, :] for e in ex_refs]
            outs = (r,) if epilogue is None else epilogue(r, *ex)
            for o_ref, o in zip(o_refs, outs):
                if rows is None:
                    o_ref[...] = o.astype(o_ref.dtype)
                else:
                    o_ref[rows, :] = o.astype(o_ref.dtype)

        if nk == 1:
            for s in range(nsub):
                rows = None if nsub == 1 else pl.ds(s * sub, sub)
                finish(dots(rows), rows)
            return

        acc = refs[-1]
        k = pl.program_id(2)

        @pl.when(k == 0)
        def _():
            acc[...] = dots(None)

        @pl.when(jnp.logical_and(k > 0, k < nk - 1))
        def _():
            acc[...] += dots(None)

        @pl.when(k == nk - 1)
        def _():
            finish(acc[...] + dots(None), None)

    a_spec = pl.BlockSpec((tk, tm), lambda i, j, k: (k, i)) if ta else pl.BlockSpec((tm, tk), lambda i, j, k: (i, k))
    if b3d:
        b_spec = (pl.BlockSpec((None, tn, tk), lambda i, j, k: (k, j, 0)) if tb
                  else pl.BlockSpec((None, tk, tn), lambda i, j, k: (j, k, 0)))
    else:
        b_spec = (pl.BlockSpec((tn, tk), lambda i, j, k: (j, k)) if tb
                  else pl.BlockSpec((tk, tn), lambda i, j, k: (k, j)))
    e_spec = pl.BlockSpec((tm, tn), lambda i, j, k: (i, j))
    if col_shards:
        o_spec = pl.BlockSpec((None, tm, tn), lambda i, j, k: (j, i, 0))
        o_shape = (N // tn, M, tn)
    else:
        o_spec, o_shape = e_spec, (M, N)
    args, in_specs = [], []
    for a, b in pairs:
        args += [a, b]
        in_specs += [a_spec, b_spec]
    args += list(extras) + list(deps)
    in_specs += [e_spec] * nex + [ANY] * ndep
    outs = pl.pallas_call(
        body,
        name=name,
        grid=(M // tm, N // tn, nk),
        in_specs=in_specs,
        out_specs=[o_spec] * nout,
        out_shape=[jax.ShapeDtypeStruct(o_shape, dt) for dt in out_dtypes],
        scratch_shapes=[pltpu.VMEM((tm, tn), F32)] if nk > 1 else [],
        compiler_params=_cparams(("parallel", "parallel", "arbitrary")),
    )(*args)
    return outs


def _cast_into_gather(w, chip_arr, name, split_cols=False):
    R, C = w.shape
    hr, hc = (R, C // 2) if split_cols else (R // 2, C)
    tr = _tile(hr, 512, 8)
    nb = hr // tr

    def body(chip_ref, w_ref, o_ref):
        o_ref[...] = w_ref[...].astype(BF16)

    in_map = (lambda h, i, chip_ref: (i, h)) if split_cols else (lambda h, i, chip_ref: (h * nb + i, 0))
    grid_spec = pltpu.PrefetchScalarGridSpec(
        num_scalar_prefetch=1, grid=(2, nb),
        in_specs=[pl.BlockSpec((tr, hc), in_map)],
        out_specs=pl.BlockSpec((None, tr, hc), lambda h, i, chip_ref: (2 * chip_ref[0] + h, i, 0)))
    return pl.pallas_call(
        body, name=name, grid_spec=grid_spec,
        out_shape=jax.ShapeDtypeStruct((N_DEV, hr, hc), BF16),
        compiler_params=_cparams(("parallel", "parallel")),
    )(chip_arr, w)


def _tie(small, token, name):
    def body(s_ref, t_ref, o_ref):
        o_ref[...] = s_ref[...]

    vm = pl.BlockSpec(memory_space=pltpu.VMEM)
    return pl.pallas_call(body, name=name, in_specs=[vm, ANY], out_specs=vm,
                          out_shape=jax.ShapeDtypeStruct(small.shape, small.dtype))(small, token)


def _rmsnorm_fwd(x, g, name):
    T, D = x.shape
    tt = _tile(T, 256)

    def body(x_ref, g_ref, n_ref):
        xv = x_ref[...]
        r = lax.rsqrt(jnp.mean(xv * xv, axis=-1, keepdims=True) + EPS)
        n_ref[...] = (xv * r * g_ref[...]).astype(BF16)

    return pl.pallas_call(
        body, name=name, grid=(T // tt,),
        in_specs=[pl.BlockSpec((tt, D), lambda i: (i, 0)), pl.BlockSpec((1, D), lambda i: (0, 0))],
        out_specs=pl.BlockSpec((tt, D), lambda i: (i, 0)),
        out_shape=jax.ShapeDtypeStruct((T, D), BF16),
        compiler_params=_cparams(("parallel",)),
    )(x, g)


def _rmsnorm_bwd(dn, x, g, res, name):
    T, D = x.shape
    tt = _tile(T, 256)

    def body(dn_ref, x_ref, g_ref, res_ref, dx_ref, dxb_ref, dg_ref):
        @pl.when(pl.program_id(0) == 0)
        def _():
            dg_ref[...] = jnp.zeros_like(dg_ref)

        xv = x_ref[...]
        dy = dn_ref[...].astype(F32)
        r = lax.rsqrt(jnp.mean(xv * xv, axis=-1, keepdims=True) + EPS)
        xhat = xv * r
        dxh = dy * g_ref[...]
        dx = res_ref[...] + r * (dxh - xhat * jnp.mean(dxh * xhat, axis=-1, keepdims=True))
        dx_ref[...] = dx
        dxb_ref[...] = dx.astype(BF16)
        dg_ref[...] += jnp.sum(dy * xhat, axis=0, keepdims=True)

    tok = pl.BlockSpec((tt, D), lambda i: (i, 0))
    vec = pl.BlockSpec((1, D), lambda i: (0, 0))
    return pl.pallas_call(
        body, name=name, grid=(T // tt,),
        in_specs=[tok, tok, vec, tok],
        out_specs=[tok, tok, vec],
        out_shape=[jax.ShapeDtypeStruct((T, D), F32), jax.ShapeDtypeStruct((T, D), BF16),
                   jax.ShapeDtypeStruct((1, D), F32)],
        compiler_params=_cparams(("arbitrary",)),
    )(dn, x, g, res)


def _loss_and_final_bwd(h2, target, gf):
    T, D = h2.shape
    tt = _tile(T, 256)

    def body(h_ref, t_ref, g_ref, dh_ref, dhb_ref, dg_ref, loss_ref):
        @pl.when(pl.program_id(0) == 0)
        def _():
            dg_ref[...] = jnp.zeros_like(dg_ref)
            loss_ref[...] = jnp.zeros_like(loss_ref)

        xv = h_ref[...]
        r = lax.rsqrt(jnp.mean(xv * xv, axis=-1, keepdims=True) + EPS)
        xhat = xv * r
        err = xhat * g_ref[...] - t_ref[...]
        loss_ref[...] += 0.5 * jnp.sum(jnp.mean(err * err, axis=-1, keepdims=True), axis=0, keepdims=True)
        dy = err * (1.0 / D)
        dxh = dy * g_ref[...]
        dx = r * (dxh - xhat * jnp.mean(dxh * xhat, axis=-1, keepdims=True))
        dh_ref[...] = dx
        dhb_ref[...] = dx.astype(BF16)
        dg_ref[...] += jnp.sum(dy * xhat, axis=0, keepdims=True)

    tok = pl.BlockSpec((tt, D), lambda i: (i, 0))
    vec = pl.BlockSpec((1, D), lambda i: (0, 0))
    return pl.pallas_call(
        body, name="loss_final_bwd", grid=(T // tt,),
        in_specs=[tok, tok, vec],
        out_specs=[tok, tok, vec, pl.BlockSpec((1, 1), lambda i: (0, 0))],
        out_shape=[jax.ShapeDtypeStruct((T, D), F32), jax.ShapeDtypeStruct((T, D), BF16),
                   jax.ShapeDtypeStruct((1, D), F32), jax.ShapeDtypeStruct((1, 1), F32)],
        compiler_params=_cparams(("arbitrary",)),
    )(h2, target, gf)


def _gated_norm_fwd(y, proj, g):
    T, D = y.shape
    tt = _tile(T, 256)

    def body(y_ref, z_ref, g_ref, o_ref):
        yg = y_ref[...] * _silu(z_ref[...])
        r = lax.rsqrt(jnp.mean(yg * yg, axis=-1, keepdims=True) + EPS)
        o_ref[...] = (yg * r * g_ref[...]).astype(BF16)

    tok = pl.BlockSpec((tt, D), lambda i: (i, 0))
    return pl.pallas_call(
        body, name="gated_norm_fwd", grid=(T // tt,),
        in_specs=[tok, tok, pl.BlockSpec((1, D), lambda i: (0, 0))],
        out_specs=tok,
        out_shape=jax.ShapeDtypeStruct((T, 2 * D_MODEL), BF16),
        compiler_params=_cparams(("parallel",)),
    )(y, proj, g)


def _gated_norm_bwd(dmix, y, proj, g, dproj):
    T, D = y.shape
    tt = _tile(T, 256)

    def body(do_ref, y_ref, z_ref, g_ref, dp_ref, dy_ref, dz_ref, dg_ref):
        @pl.when(pl.program_id(0) == 0)
        def _():
            dg_ref[...] = jnp.zeros_like(dg_ref)

        yv, zv = y_ref[...], z_ref[...]
        do = do_ref[...].astype(F32)
        sz = _silu(zv)
        yg = yv * sz
        r = lax.rsqrt(jnp.mean(yg * yg, axis=-1, keepdims=True) + EPS)
        xhat = yg * r
        dxh = do * g_ref[...]
        dyg = r * (dxh - xhat * jnp.mean(dxh * xhat, axis=-1, keepdims=True))
        dy_ref[...] = dyg * sz
        dz_ref[...] = (dyg * yv * _dsilu(zv)).astype(BF16)
        dg_ref[...] += jnp.sum(do * xhat, axis=0, keepdims=True)

    tok = pl.BlockSpec((tt, D), lambda i: (i, 0))
    vec = pl.BlockSpec((1, D), lambda i: (0, 0))
    return pl.pallas_call(
        body, name="gated_norm_bwd", grid=(T // tt,),
        in_specs=[tok, tok, tok, vec, ANY],
        out_specs=[tok, tok, vec],
        out_shape=[jax.ShapeDtypeStruct((T, D), F32), jax.ShapeDtypeStruct(dproj.shape, BF16),
                   jax.ShapeDtypeStruct((1, D), F32)],
        input_output_aliases={4: 1},
        compiler_params=_cparams(("arbitrary",)),
    )(dmix, y, proj, g, dproj)


HALO = 8


def _shift_down(cur, prev8, s):
    ext = jnp.concatenate([prev8, cur], axis=0)
    return pltpu.roll(ext, s, axis=0)[HALO:]


def _shift_up(cur, next8, s):
    n = cur.shape[0]
    ext = jnp.concatenate([cur, next8], axis=0)
    return pltpu.roll(ext, n + HALO - s, axis=0)[:n]


def _conv_specs(tt, cb, col_off_blocks, nt):
    hb = tt // HALO
    cur = pl.BlockSpec((tt, cb), lambda j, i: (i, col_off_blocks + j))
    prev = pl.BlockSpec((HALO, cb), lambda j, i: (jnp.maximum(i * hb - 1, 0), col_off_blocks + j))
    nxt = pl.BlockSpec((HALO, cb), lambda j, i: (jnp.minimum((i + 1) * hb, nt * hb - 1), col_off_blocks + j))
    return cur, prev, nxt


def _causal_conv(cur, prev8, w, K):
    y = cur * w[K - 1:K, :]
    for k in range(K - 1):
        y = y + _shift_down(cur, prev8, K - 1 - k) * w[k:k + 1, :]
    return y


def _anticausal_conv(cur, next8, w, K):
    y = cur * w[K - 1:K, :]
    for k in range(K - 1):
        y = y + _shift_up(cur, next8, K - 1 - k) * w[k:k + 1, :]
    return y


def _ssm_conv_fwd(proj, w8, b):
    T = proj.shape[0]
    tt, cb = _tile(T, 512), 512
    nt = T // tt
    cur, prev, _ = _conv_specs(tt, cb, OFF_XBC // cb, nt)

    def body(u_ref, up_ref, w_ref, b_ref, o_ref):
        first = pl.program_id(1) == 0
        p8 = jnp.where(first, 0.0, up_ref[...])
        pre = _causal_conv(u_ref[...], p8, w_ref[...], K_SSM) + b_ref[...]
        o_ref[...] = _silu(pre)

    return pl.pallas_call(
        body, name="ssm_conv_fwd", grid=(D_XBC // cb, nt),
        in_specs=[cur, prev, pl.BlockSpec((8, cb), lambda j, i: (0, j)), pl.BlockSpec((1, cb), lambda j, i: (0, j))],
        out_specs=pl.BlockSpec((tt, cb), lambda j, i: (i, j)),
        out_shape=jax.ShapeDtypeStruct((T, D_XBC), F32),
        compiler_params=_cparams(("parallel", "parallel")),
    )(proj, proj, w8, b)


def _ssm_conv_bwd(dact, proj, w8, b, dproj):
    T = proj.shape[0]
    tt, cb = _tile(T, 512), 512
    nt = T // tt
    cur, prev, nxt = _conv_specs(tt, cb, OFF_XBC // cb, nt)
    dcur, dprev, dnxt = _conv_specs(tt, cb, 0, nt)

    def dpre_of(d, u, p8, w, bb):
        pre = _causal_conv(u, p8, w, K_SSM) + bb
        return d * _dsilu(pre)

    def body(d_ref, dn_ref, u_ref, up_ref, un_ref, w_ref, b_ref, dp_ref, dx_ref, dw_ref, db_ref):
        i = pl.program_id(1)

        @pl.when(i == 0)
        def _():
            dw_ref[...] = jnp.zeros_like(dw_ref)
            db_ref[...] = jnp.zeros_like(db_ref)

        w, bb = w_ref[...], b_ref[...]
        u = u_ref[...]
        p8 = jnp.where(i == 0, 0.0, up_ref[...])
        dpre = dpre_of(d_ref[...], u, p8, w, bb)
        un = un_ref[...]
        dpre_n = dpre_of(dn_ref[...], un, u[tt - HALO:, :], w, bb)
        dpre_n = jnp.where(i == nt - 1, 0.0, dpre_n)
        dx_ref[...] = _anticausal_conv(dpre, dpre_n, w, K_SSM).astype(BF16)
        rows = [jnp.sum(dpre * _shift_down(u, p8, K_SSM - 1 - k), axis=0, keepdims=True) for k in range(K_SSM - 1)]
        rows.append(jnp.sum(dpre * u, axis=0, keepdims=True))
        rows.append(jnp.zeros((8 - K_SSM, cb), F32))
        dw_ref[...] += jnp.concatenate(rows, axis=0)
        db_ref[...] += jnp.sum(dpre, axis=0, keepdims=True)

    wspec = pl.BlockSpec((8, cb), lambda j, i: (0, j))
    bspec = pl.BlockSpec((1, cb), lambda j, i: (0, j))
    return pl.pallas_call(
        body, name="ssm_conv_bwd", grid=(D_XBC // cb, nt),
        in_specs=[dcur, dnxt, cur, prev, nxt, wspec, bspec, ANY],
        out_specs=[pl.BlockSpec((tt, cb), lambda j, i: (i, OFF_XBC // cb + j)), wspec, bspec],
        out_shape=[jax.ShapeDtypeStruct(dproj.shape, BF16), jax.ShapeDtypeStruct((8, D_XBC), F32),
                   jax.ShapeDtypeStruct((1, D_XBC), F32)],
        input_output_aliases={7: 0},
        compiler_params=_cparams(("parallel", "arbitrary")),
    )(dact, dact, proj, proj, proj, w8, b, dproj)


SCB = 512
SC3 = 3 * SCB


def _sc_specs(tt, nt):
    hb = tt // HALO
    cur = pl.BlockSpec((tt, SC3), lambda j, i: (i, OFF_CB // SC3 + j))
    prev = pl.BlockSpec((HALO, SC3), lambda j, i: (jnp.maximum(i * hb - 1, 0), OFF_CB // SC3 + j))
    nxt = pl.BlockSpec((HALO, SC3), lambda j, i: (jnp.minimum((i + 1) * hb, nt * hb - 1), OFF_CB // SC3 + j))
    return cur, prev, nxt


def _shortconv_fwd(proj, w8, ymix):
    T = proj.shape[0]
    tt = _tile(T, 512)
    nt = T // tt
    cur, prev, _ = _sc_specs(tt, nt)

    def body(p_ref, pp_ref, w_ref, y_ref, o_ref):
        p, pp = p_ref[...], pp_ref[...]
        v = p[:, SCB:2 * SCB] * p[:, 2 * SCB:]
        vp = jnp.where(pl.program_id(1) == 0, 0.0, pp[:, SCB:2 * SCB] * pp[:, 2 * SCB:])
        o_ref[...] = (p[:, :SCB] * _causal_conv(v, vp, w_ref[...], K_SC)).astype(BF16)

    return pl.pallas_call(
        body, name="shortconv_fwd", grid=(D_MODEL // SCB, nt),
        in_specs=[cur, prev, pl.BlockSpec((8, SCB), lambda j, i: (0, j)), ANY],
        out_specs=pl.BlockSpec((tt, SCB), lambda j, i: (i, D_SSM // SCB + j)),
        out_shape=jax.ShapeDtypeStruct(ymix.shape, BF16),
        input_output_aliases={3: 0},
        compiler_params=_cparams(("parallel", "parallel")),
    )(proj, proj, w8, ymix)


def _shortconv_bwd(dmix, proj, w8):
    T = proj.shape[0]
    tt = _tile(T, 512)
    nt = T // tt
    hb = tt // HALO
    cur, prev, nxt = _sc_specs(tt, nt)
    d_s = pl.BlockSpec((tt, SCB), lambda j, i: (i, D_SSM // SCB + j))
    dn_s = pl.BlockSpec((HALO, SCB), lambda j, i: (jnp.minimum((i + 1) * hb, nt * hb - 1), D_SSM // SCB + j))

    def body(d_ref, dn_ref, p_ref, pp_ref, pn_ref, w_ref, dp_ref, dw_ref):
        i = pl.program_id(1)

        @pl.when(i == 0)
        def _():
            dw_ref[...] = jnp.zeros_like(dw_ref)

        w = w_ref[...]
        p, pp = p_ref[...], pp_ref[...]
        gb, gc, u = p[:, :SCB], p[:, SCB:2 * SCB], p[:, 2 * SCB:]
        v = gc * u
        vp = jnp.where(i == 0, 0.0, pp[:, SCB:2 * SCB] * pp[:, 2 * SCB:])
        d = d_ref[...].astype(F32)
        dp_ref[:, :SCB] = (d * _causal_conv(v, vp, w, K_SC)).astype(BF16)
        dcv = d * gb
        dcv_n = jnp.where(i == nt - 1, 0.0, dn_ref[...].astype(F32) * pn_ref[:, :SCB])
        dv = _anticausal_conv(dcv, dcv_n, w, K_SC)
        dp_ref[:, SCB:2 * SCB] = (dv * u).astype(BF16)
        dp_ref[:, 2 * SCB:] = (dv * gc).astype(BF16)
        rows = [jnp.sum(dcv * _shift_down(v, vp, K_SC - 1 - k), axis=0, keepdims=True) for k in range(K_SC - 1)]
        rows.append(jnp.sum(dcv * v, axis=0, keepdims=True))
        rows.append(jnp.zeros((8 - K_SC, SCB), F32))
        dw_ref[...] += jnp.concatenate(rows, axis=0)

    wspec = pl.BlockSpec((8, SCB), lambda j, i: (0, j))
    return pl.pallas_call(
        body, name="shortconv_bwd", grid=(D_MODEL // SCB, nt),
        in_specs=[d_s, dn_s, cur, prev, nxt, wspec],
        out_specs=[cur, wspec],
        out_shape=[jax.ShapeDtypeStruct((T, D_MAIN), BF16), jax.ShapeDtypeStruct((8, D_MODEL), F32)],
        compiler_params=_cparams(("parallel", "arbitrary")),
    )(dmix, dmix, proj, proj, proj, w8)


GW = HEADS_PER_GROUP * HEADDIM
HI = lax.Precision.HIGHEST


def _dot(a, b):
    return jnp.dot(a.astype(BF16), b.astype(BF16), preferred_element_type=F32)


def _dot_nt(a, b):
    return lax.dot_general(a.astype(BF16), b.astype(BF16), (((1,), (1,)), ((), ())), preferred_element_type=F32)


def _dot_tn(a, b):
    return lax.dot_general(a.astype(BF16), b.astype(BF16), (((0,), (0,)), ((), ())), preferred_element_type=F32)


def _dot_hi(a, b):
    return jnp.dot(a, b, precision=HI, preferred_element_type=F32)


def _dot_nt_hi(a, b):
    return lax.dot_general(a, b, (((1,), (1,)), ((), ())), precision=HI, preferred_element_type=F32)


def _head_cols(rows):
    parts = [jnp.broadcast_to(rows[r:r + 1, :], (HEADDIM, CHUNK)) for r in range(HEADS_PER_GROUP)]
    return jnp.concatenate(parts, axis=0).T


def _head_rows(rows):
    parts = [jnp.broadcast_to(rows[r:r + 1, :], (HEADDIM, N_STATE)) for r in range(HEADS_PER_GROUP)]
    return jnp.concatenate(parts, axis=0)


def _ssd_common(dtr, bias, alog):
    dt = _softplus(dtr + bias)
    A = -jnp.exp(alog)
    a = dt * A
    ki = lax.broadcasted_iota(jnp.int32, (CHUNK, CHUNK), 0)
    si = lax.broadcasted_iota(jnp.int32, (CHUNK, CHUNK), 1)
    upper = (ki <= si).astype(F32)
    cs = _dot_hi(a, upper)
    cs_last = jnp.broadcast_to(cs[:, CHUNK - 1:CHUNK], (8, CHUNK))
    return dt, A, a, cs, cs_last


def _decay_matrix(cs, r):
    li = lax.broadcasted_iota(jnp.int32, (CHUNK, CHUNK), 0)
    si = lax.broadcasted_iota(jnp.int32, (CHUNK, CHUNK), 1)
    causal = li >= si
    R = jnp.broadcast_to(cs[r:r + 1, :], (CHUNK, CHUNK))
    seg = jnp.where(causal, R.T - R, 0.0)
    return jnp.where(causal, jnp.exp(seg), 0.0)


GXBC = GW + 2 * N_STATE


def _ssd_in_specs(nc, rev):
    cix = (lambda c: nc - 1 - c) if rev else (lambda c: c)
    x_s = pl.BlockSpec((CHUNK, GW), lambda g, c: (cix(c), g))
    xbc_s = pl.BlockSpec((CHUNK, GXBC), lambda g, c: (cix(c), g))
    dtr_s = pl.BlockSpec((1, 8, CHUNK), lambda g, c: (g, 0, cix(c)))
    row_s = pl.BlockSpec((1, 8, CHUNK), lambda g, c: (g, 0, 0))
    drep_s = pl.BlockSpec((1, GW), lambda g, c: (0, g))
    hs_s = pl.BlockSpec((1, GW, N_STATE), lambda g, c: (cix(c), g, 0))
    return x_s, xbc_s, dtr_s, row_s, drep_s, hs_s


def _ssd_fwd(xbc, dtr, bias, alog, drep):
    T = xbc.shape[0]
    nc = T // CHUNK
    x_s, xbc_s, dtr_s, row_s, drep_s, hs_s = _ssd_in_specs(nc, False)

    def body(xbc_ref, dtr_ref, bias_ref, alog_ref, drep_ref, y_ref, hs_ref, h_scr):
        @pl.when(pl.program_id(1) == 0)
        def _():
            h_scr[...] = jnp.zeros_like(h_scr)

        x, Bm, Cm = xbc_ref[:, :GW], xbc_ref[:, GW:GW + N_STATE], xbc_ref[:, GW + N_STATE:]
        dt, A, a, cs, cs_last = _ssd_common(dtr_ref[0], bias_ref[0], alog_ref[0])
        E = _head_cols(jnp.exp(cs))
        W = _head_cols(jnp.exp(cs_last - cs) * dt)
        X = (x * _head_cols(dt)).astype(BF16)
        CB = _dot_nt(Cm, Bm)
        col = lax.broadcasted_iota(jnp.int32, (CHUNK, GW), 1) // HEADDIM
        y = jnp.zeros((CHUNK, GW), F32)
        for r in range(HEADS_PER_GROUP):
            M = CB * _decay_matrix(cs, r)
            y = y + jnp.where(col == r, _dot(M, X), 0.0)
        h = h_scr[...]
        hs_ref[0] = h
        y = y + _dot_nt(Cm, h) * E
        y_ref[...] = y + drep_ref[...] * x
        h_scr[...] = h * _head_rows(jnp.exp(cs_last)) + _dot_tn(x * W, Bm)

    return pl.pallas_call(
        body, name="ssd_fwd", grid=(N_GROUPS, nc),
        in_specs=[xbc_s, dtr_s, row_s, row_s, drep_s],
        out_specs=[x_s, hs_s],
        out_shape=[jax.ShapeDtypeStruct((T, D_SSM), F32), jax.ShapeDtypeStruct((nc, D_SSM, N_STATE), F32)],
        scratch_shapes=[pltpu.VMEM((GW, N_STATE), F32)],
        compiler_params=_cparams(("parallel", "arbitrary")),
    )(xbc, dtr, bias, alog, drep)


def _ssd_bwd(xbc, dtr, bias, alog, drep, dy, hs):
    T = xbc.shape[0]
    nc = T // CHUNK
    x_s, xbc_s, dtr_s, row_s, drep_s, hs_s = _ssd_in_specs(nc, True)

    def body(xbc_ref, dtr_ref, bias_ref, alog_ref, drep_ref, dy_ref, hs_ref,
             dxbc_ref, ddtr_ref, dbias_ref, dalog_ref, dd_ref, dh_scr):
        @pl.when(pl.program_id(1) == 0)
        def _():
            dh_scr[...] = jnp.zeros_like(dh_scr)
            dbias_ref[...] = jnp.zeros_like(dbias_ref)
            dalog_ref[...] = jnp.zeros_like(dalog_ref)
            dd_ref[...] = jnp.zeros_like(dd_ref)

        x, Bm, Cm = xbc_ref[:, :GW], xbc_ref[:, GW:GW + N_STATE], xbc_ref[:, GW + N_STATE:]
        dY = dy_ref[...]
        dt, A, a, cs, cs_last = _ssd_common(dtr_ref[0], bias_ref[0], alog_ref[0])
        E = _head_cols(jnp.exp(cs))
        DT = _head_cols(dt)
        Wd = _head_cols(jnp.exp(cs_last - cs))
        X = x * DT
        h = hs_ref[0]
        dS = dh_scr[...]
        CB = _dot_nt(Cm, Bm)
        col = lax.broadcasted_iota(jnp.int32, (CHUNK, GW), 1) // HEADDIM
        rowid = lax.broadcasted_iota(jnp.int32, (8, CHUNK), 0)
        lane = lax.broadcasted_iota(jnp.int32, (8, CHUNK), 1)
        hsel = (lax.broadcasted_iota(jnp.int32, (8, GW), 1) // HEADDIM
                == lax.broadcasted_iota(jnp.int32, (8, GW), 0)).astype(F32)
        ones8 = jnp.ones((8, CHUNK), F32)

        dX = jnp.zeros((CHUNK, GW), F32)
        dCB = jnp.zeros((CHUNK, CHUNK), F32)
        dcs = jnp.zeros((8, CHUNK), F32)
        for r in range(HEADS_PER_GROUP):
            L = _decay_matrix(cs, r)
            M = CB * L
            G = _dot_nt(jnp.where(col == r, dY, 0.0), X)
            GL = G * L
            dCB = dCB + GL
            Wm = GL * CB
            colsum = jnp.sum(Wm, axis=0, keepdims=True)
            rowsum = _dot_nt_hi(ones8, Wm)
            dcs = dcs + jnp.where(rowid == r, rowsum - colsum, 0.0)
            dX = dX + jnp.where(col == r, _dot_tn(M, dY), 0.0)
        dC = _dot(dCB, Bm)
        dB = _dot_tn(dCB, Cm)
        T1 = _dot_nt(Bm, dS)
        dX = dX + T1 * Wd
        dB = dB + _dot(X * Wd, dS)
        pdec = _dot_nt_hi(hsel, X * T1 * Wd)
        dcs = dcs - pdec
        dlast = jnp.sum(pdec, axis=1, keepdims=True) \
            + jnp.exp(cs_last[:, 0:1]) * jnp.sum(_dot_hi(hsel, dS * h), axis=1, keepdims=True)
        dYE = dY * E
        dC = dC + _dot(dYE, h)
        yoff = _dot_nt(Cm, h) * E
        dcs = dcs + _dot_nt_hi(hsel, dY * yoff)
        dcs = dcs + jnp.where(lane == CHUNK - 1, dlast, 0.0)
        ki = lax.broadcasted_iota(jnp.int32, (CHUNK, CHUNK), 0)
        si = lax.broadcasted_iota(jnp.int32, (CHUNK, CHUNK), 1)
        lower = (ki >= si).astype(F32)
        da = _dot_hi(dcs, lower)
        ddt = da * A + _dot_nt_hi(hsel, dX * x)
        ddtr = ddt * _sigmoid(dtr_ref[0] + bias_ref[0])
        ddtr_ref[0] = ddtr
        dbias_ref[0] += ddtr
        dalog_ref[0] += da * a
        dxbc_ref[:, :GW] = dX * DT + drep_ref[...] * dY
        dd_ref[...] += jnp.sum(dY * x, axis=0, keepdims=True)
        dxbc_ref[:, GW:GW + N_STATE] = dB
        dxbc_ref[:, GW + N_STATE:] = dC
        dh_scr[...] = dS * _head_rows(jnp.exp(cs_last)) + _dot_tn(dYE, Cm)

    return pl.pallas_call(
        body, name="ssd_bwd", grid=(N_GROUPS, nc),
        in_specs=[xbc_s, dtr_s, row_s, row_s, drep_s, x_s, hs_s],
        out_specs=[xbc_s, dtr_s, row_s, row_s, drep_s],
        out_shape=[jax.ShapeDtypeStruct((T, D_XBC), F32),
                   jax.ShapeDtypeStruct((N_GROUPS, 8, T), F32),
                   jax.ShapeDtypeStruct((N_GROUPS, 8, CHUNK), F32),
                   jax.ShapeDtypeStruct((N_GROUPS, 8, CHUNK), F32),
                   jax.ShapeDtypeStruct((1, D_SSM), F32)],
        scratch_shapes=[pltpu.VMEM((GW, N_STATE), F32)],
        compiler_params=_cparams(("parallel", "arbitrary")),
    )(xbc, dtr, bias, alog, drep, dy, hs)


def _adamw(w, g, m, v, name, deps=()):
    R, C = w.shape
    tr = _tile(R, 256, 8)
    nd = len(deps)

    def body(w_ref, g_ref, m_ref, v_ref, *rest):
        d_ref, mo_ref, vo_ref = rest[nd:]
        gv = g_ref[...]
        mn = ADAM_B1 * m_ref[...] + (1.0 - ADAM_B1) * gv
        vn = ADAM_B2 * v_ref[...] + (1.0 - ADAM_B2) * (gv * gv)
        m_hat = mn / (1.0 - ADAM_B1 ** ADAM_STEP)
        v_hat = vn / (1.0 - ADAM_B2 ** ADAM_STEP)
        d_ref[...] = -ADAM_LR * (m_hat / (jnp.sqrt(v_hat) + ADAM_EPS) + ADAM_WD * w_ref[...])
        mo_ref[...] = mn
        vo_ref[...] = vn

    spec = pl.BlockSpec((tr, C), lambda i: (i, 0))
    return pl.pallas_call(
        body, name=name, grid=(R // tr,),
        in_specs=[spec] * 4 + [ANY] * nd, out_specs=[spec] * 3,
        out_shape=[jax.ShapeDtypeStruct((R, C), F32)] * 3,
        compiler_params=_cparams(("parallel",)),
    )(w, g, m, v, *deps)


def _adamw_halves(w, g2, m, v, name):
    R, C = w.shape
    hc = C // 2
    tr = _tile(R, 256, 8)

    def body(w_ref, g_ref, m_ref, v_ref, d_ref, mo_ref, vo_ref, go_ref):
        gv = g_ref[...]
        mn = ADAM_B1 * m_ref[...] + (1.0 - ADAM_B1) * gv
        vn = ADAM_B2 * v_ref[...] + (1.0 - ADAM_B2) * (gv * gv)
        m_hat = mn / (1.0 - ADAM_B1 ** ADAM_STEP)
        v_hat = vn / (1.0 - ADAM_B2 ** ADAM_STEP)
        d_ref[...] = -ADAM_LR * (m_hat / (jnp.sqrt(v_hat) + ADAM_EPS) + ADAM_WD * w_ref[...])
        mo_ref[...] = mn
        vo_ref[...] = vn
        go_ref[...] = gv

    spec = pl.BlockSpec((tr, hc), lambda h, i: (i, h))
    gspec = pl.BlockSpec((None, tr, hc), lambda h, i: (h, i, 0))
    return pl.pallas_call(
        body, name=name, grid=(2, R // tr),
        in_specs=[spec, gspec, spec, spec], out_specs=[spec] * 4,
        out_shape=[jax.ShapeDtypeStruct((R, C), F32)] * 4,
        compiler_params=_cparams(("parallel", "parallel")),
    )(w, g2, m, v)


ANY = pl.BlockSpec(memory_space=pl.ANY)


def _place():
    x, y, c = lax.axis_index("x"), lax.axis_index("y"), lax.axis_index("c")
    return x, y, c


def _other_chips(x, y):
    return [(1 - x, y), (x, 1 - y), (1 - x, 1 - y)]


def _allgather_inplace(bufs):
    n = len(bufs)

    def body(*refs):
        o_refs = refs[n:2 * n]
        send_sems, recv_sems = refs[2 * n:]
        x, y, c = _place()
        sibling = (x, y, 1 - c)
        chips = _other_chips(x, y)

        def copy(k, slot, px, py, pc, to):
            blk = o_refs[k].at[4 * px + 2 * py + pc]
            return pltpu.make_async_remote_copy(
                src_ref=blk, dst_ref=blk, send_sem=send_sems.at[k, slot], recv_sem=recv_sems.at[k, slot],
                device_id=to, device_id_type=MESH)

        sent = []
        for k in range(n):
            for j, (px, py) in enumerate(chips):
                cp = copy(k, j, x, y, c, (px, py, c))
                cp.start()
                sent.append(cp)
        for k in range(n):
            for j, (px, py) in enumerate(chips):
                copy(k, j, px, py, c, (px, py, c)).wait_recv()
                fwd = copy(k, 3 + j, px, py, c, sibling)
                fwd.start()
                sent.append(fwd)
        for k in range(n):
            for j, (px, py) in enumerate(chips):
                copy(k, 3 + j, px, py, 1 - c, sibling).wait_recv()
        for cp in sent:
            cp.wait_send()

    return pl.pallas_call(
        body, name="allgather_w_in",
        in_specs=[ANY] * n, out_specs=[ANY] * n,
        out_shape=[jax.ShapeDtypeStruct(b.shape, b.dtype) for b in bufs],
        input_output_aliases={k: k for k in range(n)},
        scratch_shapes=[pltpu.SemaphoreType.DMA((n, 6)), pltpu.SemaphoreType.DMA((n, 6))],
    )(*bufs)


HBM = pl.BlockSpec(memory_space=pltpu.HBM)
SEM = pl.BlockSpec(memory_space=pltpu.SEMAPHORE)
EFFECT = pltpu.SideEffectType.DATAFLOW_SIDE_EFFECTING


def _split_start(name, arrays, build, n_copies, after=()):
    na, nd = len(arrays), len(after)

    def body(*refs):
        send_sems, recv_sems = refs[na + nd], refs[na + nd + 1]
        for cp in build(refs[:na], send_sems, recv_sems):
            cp.start()
        refs[-1][...] = jnp.zeros((8, 128), F32)

    outs = pl.pallas_call(
        body, name=name,
        out_shape=(pltpu.SemaphoreType.DMA((n_copies,)), pltpu.SemaphoreType.DMA((n_copies,)),
                   *[pltpu.HBM(a.shape, a.dtype) for a in arrays], jax.ShapeDtypeStruct((8, 128), F32)),
        in_specs=[HBM] * na + [ANY] * nd,
        out_specs=(SEM, SEM, *[HBM] * na, pl.BlockSpec(memory_space=pltpu.VMEM)),
        input_output_aliases={i: 2 + i for i in range(na)},
        compiler_params=pltpu.CompilerParams(has_side_effects=EFFECT),
    )(*[pltpu.with_memory_space_constraint(a, pltpu.HBM) for a in arrays], *after)
    return outs[0], outs[1], list(outs[2:2 + na]), outs[-1]


def _split_wait(name, send_sems, recv_sems, arrays, build, after):
    na = len(arrays)

    def body(*refs):
        for cp in build(refs[:na], refs[na], refs[na + 1]):
            cp.wait_send()
            cp.wait_recv()

    outs = pl.pallas_call(
        body, name=name,
        out_shape=tuple(pltpu.HBM(a.shape, a.dtype) for a in arrays),
        in_specs=[HBM] * na + [SEM, SEM] + [ANY] * len(after),
        out_specs=tuple([HBM] * na),
        input_output_aliases={i: i for i in range(na)},
        compiler_params=pltpu.CompilerParams(has_side_effects=EFFECT),
    )(*arrays, send_sems, recv_sems, *after)
    return list(outs)


def _remote(src, dst, send_sems, recv_sems, i, to):
    return pltpu.make_async_remote_copy(src_ref=src, dst_ref=dst, send_sem=send_sems.at[i], recv_sem=recv_sems.at[i],
                                        device_id=to, device_id_type=MESH)


def _build_ag_ici(refs, ss, rs):
    x, y, c = _place()
    cps = []
    for k, ref in enumerate(refs):
        blk = ref.at[4 * x + 2 * y + c]
        for j, (px, py) in enumerate(_other_chips(x, y)):
            cps.append(_remote(blk, blk, ss, rs, 3 * k + j, (px, py, c)))
    return cps


def _build_ag_fwd(refs, ss, rs):
    x, y, c = _place()
    cps = []
    for k, ref in enumerate(refs):
        for j, (px, py) in enumerate(_other_chips(x, y)):
            blk = ref.at[4 * px + 2 * py + c]
            cps.append(_remote(blk, blk, ss, rs, 3 * k + j, (x, y, 1 - c)))
    return cps


def _build_rs_swap(refs, ss, rs):
    x, y, c = _place()
    n = len(refs) // 2
    return [_remote(refs[k].at[:, pl.ds(1 - c, 1)], refs[n + k], ss, rs, k, (x, y, 1 - c)) for k in range(n)]


def _build_rs_ici(refs, ss, rs):
    x, y, c = _place()
    n = len(refs) // 2
    me = 2 * x + y
    cps = []
    for k in range(n):
        for j, (px, py) in enumerate(_other_chips(x, y)):
            cps.append(_remote(refs[k].at[2 * px + py], refs[n + k].at[me], ss, rs, 3 * k + j, (px, py, c)))
    return cps


def _build_rs_share(refs, ss, rs):
    x, y, c = _place()
    return [_remote(ref.at[c], ref.at[c], ss, rs, k, (x, y, 1 - c)) for k, ref in enumerate(refs)]


def _allreduce_small(p, deps=()):
    R, C = p.shape
    nd = len(deps)

    def body(p_ref, *rest):
        gath_ref, sum_ref, send_sems, recv_sems, local_sem = rest[nd:]
        x, y, c = _place()
        me, sibling = (x, y, c), (x, y, 1 - c)
        chips = [(1 - x, y), (x, 1 - y), (1 - x, 1 - y)]

        def blk(px, py, pc):
            return gath_ref.at[4 * px + 2 * py + pc]

        def copy(k, block, to, src=None):
            return pltpu.make_async_remote_copy(
                src_ref=blk(*block) if src is None else src, dst_ref=blk(*block),
                send_sem=send_sems.at[k], recv_sem=recv_sems.at[k], device_id=to, device_id_type=MESH)

        mine = pltpu.make_async_copy(p_ref, blk(*me), local_sem)
        mine.start()
        first = [copy(0, me, sibling, src=p_ref)]
        first += [copy(1 + j, me, (*chip, c), src=p_ref) for j, chip in enumerate(chips)]
        for cp in first:
            cp.start()
        passed = [copy(4 + j, (*chip, c), sibling) for j, chip in enumerate(chips)]
        for j, chip in enumerate(chips):
            copy(1 + j, (*chip, c), me).wait_recv()
            passed[j].start()
        copy(0, sibling, me).wait_recv()
        for j, chip in enumerate(chips):
            copy(4 + j, (*chip, 1 - c), me).wait_recv()
        for cp in first + passed:
            cp.wait_send()
        mine.wait()
        s = gath_ref[0]
        for d in range(1, N_DEV):
            s = s + gath_ref[d]
        sum_ref[...] = s

    vm = pl.BlockSpec(memory_space=pltpu.VMEM)
    return pl.pallas_call(
        body, name="allreduce_small",
        in_specs=[vm] + [ANY] * nd, out_specs=[vm, vm],
        out_shape=[jax.ShapeDtypeStruct((N_DEV, R, C), F32), jax.ShapeDtypeStruct((R, C), F32)],
        scratch_shapes=[pltpu.SemaphoreType.DMA((7,)), pltpu.SemaphoreType.DMA((7,)), pltpu.SemaphoreType.DMA],
    )(p, *deps)[1]


def _rs_add_pair(p, r0, c_arr, name):
    _, _, hr, cols = p.shape
    tr = _tile(hr, 256, 8)

    def body(c_ref, p_ref, r_ref, q_ref):
        q_ref[...] = (p_ref[0] + r_ref[0]).astype(BF16)

    grid_spec = pltpu.PrefetchScalarGridSpec(
        num_scalar_prefetch=1, grid=(N_CHIPS, hr // tr),
        in_specs=[pl.BlockSpec((1, 1, tr, cols), lambda j, i, c_ref: (j, c_ref[0], i, 0)),
                  pl.BlockSpec((1, 1, tr, cols), lambda j, i, c_ref: (j, 0, i, 0))],
        out_specs=pl.BlockSpec((1, tr, cols), lambda j, i, c_ref: (j, i, 0)))
    return pl.pallas_call(
        body, name=name, grid_spec=grid_spec,
        out_shape=jax.ShapeDtypeStruct((N_CHIPS, hr, cols), BF16),
        compiler_params=_cparams(("parallel", "parallel")),
    )(c_arr, p, r0)


def _rs_add_chips(r1, q, place_arr, name):
    _, hr, cols = r1.shape
    tr = _tile(hr, 256, 8)

    def body(place_ref, r_ref, q_ref, o_ref):
        chip = place_ref[0]
        s = None
        for j in range(N_CHIPS):
            t = jnp.where(chip == j, q_ref[j], r_ref[j]).astype(F32)
            s = t if s is None else s + t
        o_ref[...] = s

    blk = pl.BlockSpec((N_CHIPS, tr, cols), lambda i, place_ref: (0, i, 0))
    grid_spec = pltpu.PrefetchScalarGridSpec(
        num_scalar_prefetch=1, grid=(hr // tr,), in_specs=[blk, blk],
        out_specs=pl.BlockSpec((None, tr, cols), lambda i, place_ref: (place_ref[1], i, 0)))
    return pl.pallas_call(
        body, name=name, grid_spec=grid_spec,
        out_shape=jax.ShapeDtypeStruct((2, hr, cols), F32),
        compiler_params=_cparams(("parallel",)),
    )(place_arr, r1, q)


def _pad_rows(a, rows):
    return jnp.pad(a, ((0, rows - a.shape[0]), (0, 0)))


def _pad_cols(a, cols):
    return jnp.pad(a, ((0, 0), (0, cols - a.shape[1])))


def _heads_to_rows(v):
    v = v.reshape(N_GROUPS, HEADS_PER_GROUP, 1)
    v = jnp.pad(v, ((0, 0), (0, 8 - HEADS_PER_GROUP), (0, 0)))
    return jnp.broadcast_to(v, (N_GROUPS, 8, CHUNK))


def _rows_to_heads(a):
    return jnp.sum(a[:, :HEADS_PER_GROUP, :], axis=-1).reshape(N_HEADS)


def _to_kernel_rows(a):
    C = a.shape[1]
    x0, b0, c0, s0 = D_SSM, 2 * D_SSM, 2 * D_SSM + 1024, D_SSM + D_XBC + N_HEADS
    xbc = jnp.concatenate([a[x0:b0].reshape(N_GROUPS, GW, C), a[b0:c0].reshape(N_GROUPS, N_STATE, C),
                           a[c0:c0 + 1024].reshape(N_GROUPS, N_STATE, C)], axis=1).reshape(D_XBC, C)
    sc = jnp.concatenate([a[s0 + k * D_MODEL:s0 + (k + 1) * D_MODEL].reshape(D_MODEL // SCB, SCB, C)
                          for k in range(3)], axis=1).reshape(3 * D_MODEL, C)
    return jnp.concatenate([a[:D_SSM], xbc, sc], axis=0)


def _from_kernel_rows(k, dt_rows):
    C = k.shape[1]
    xbc = k[D_SSM:D_SSM + D_XBC].reshape(N_GROUPS, GXBC, C)
    sc = k[D_SSM + D_XBC:].reshape(D_MODEL // SCB, SC3, C)
    return jnp.concatenate(
        [k[:D_SSM], xbc[:, :GW].reshape(D_SSM, C), xbc[:, GW:GW + N_STATE].reshape(1024, C),
         xbc[:, GW + N_STATE:].reshape(1024, C), dt_rows]
        + [sc[:, j * SCB:(j + 1) * SCB].reshape(D_MODEL, C) for j in range(3)], axis=0)


def _to_kernel_xbc(a):
    R = a.shape[0]
    return jnp.concatenate([a[:, :D_SSM].reshape(R, N_GROUPS, GW), a[:, D_SSM:D_SSM + 1024].reshape(R, N_GROUPS, N_STATE),
                            a[:, D_SSM + 1024:].reshape(R, N_GROUPS, N_STATE)], axis=2).reshape(R, D_XBC)


def _from_kernel_xbc(a):
    R = a.shape[0]
    g = a.reshape(R, N_GROUPS, GXBC)
    return jnp.concatenate([g[:, :, :GW].reshape(R, D_SSM), g[:, :, GW:GW + N_STATE].reshape(R, 1024),
                            g[:, :, GW + N_STATE:].reshape(R, 1024)], axis=1)


def kernel(x, norm_mix_g, w_in, ssm_conv_w, ssm_conv_b, ssm_dt_bias, ssm_A_log, ssm_D, ssm_norm_g, sc_conv_w, w_out, norm_ffn_g, w_gate, w_up, w_down, norm_final_g, loss_target, m_norm_mix_g, m_w_in, m_ssm_conv_w, m_ssm_conv_b, m_ssm_dt_bias, m_ssm_A_log, m_ssm_D, m_ssm_norm_g, m_sc_conv_w, m_w_out, m_norm_ffn_g, m_w_gate, m_w_up, m_w_down, m_norm_final_g, v_norm_mix_g, v_w_in, v_ssm_conv_w, v_ssm_conv_b, v_ssm_dt_bias, v_ssm_A_log, v_ssm_D, v_ssm_norm_g, v_sc_conv_w, v_w_out, v_norm_ffn_g, v_w_gate, v_w_up, v_w_down, v_norm_final_g):
    T = x.shape[1]
    xt = x[0]
    tgt = loss_target[0]
    cx, cy, cc = lax.axis_index("x"), lax.axis_index("y"), lax.axis_index("c")
    chip = 2 * cx + cy
    c_arr = jnp.reshape(cc, (1,)).astype(jnp.int32)
    chip_arr = jnp.reshape(chip, (1,)).astype(jnp.int32)
    place_arr = jnp.stack([chip, cc]).astype(jnp.int32)

    big = [w_in[0].T, w_out[0], w_gate[0], w_up[0], w_down[0]]
    names = ["w_in", "w_out", "w_gate", "w_up", "w_down"]
    gbufs = [_cast_into_gather(w, chip_arr, "cast_" + nm, split_cols=(nm == "w_in")) for w, nm in zip(big, names)]
    (g_in,) = _allgather_inplace([gbufs[0]])
    ag_ss, ag_rs, ag_bufs, ag_tok = _split_start("ag_ici_start", gbufs[1:], _build_ag_ici, 12, after=[g_in])
    cs_in = D_IN // N_CHIPS
    wt = g_in.reshape(N_CHIPS, 2, cs_in, D_MODEL // 2).transpose(0, 2, 1, 3).reshape(D_IN, D_MODEL)
    wt_main = _to_kernel_rows(wt)
    wt_dt = _pad_rows(wt[D_SSM + D_XBC:D_SSM + D_XBC + N_HEADS], DT_PAD)

    contrib = (cc == 0).astype(F32)
    place_ssm = jnp.zeros((8, D_XBC), F32)
    place_ssm = lax.dynamic_update_slice(place_ssm, _pad_rows(ssm_conv_w[0], 8) * contrib, (0, chip * (D_XBC // N_CHIPS)))
    place_sc = jnp.zeros((8, D_MODEL), F32)
    place_sc = lax.dynamic_update_slice(place_sc, _pad_rows(sc_conv_w[0], 8) * contrib, (0, chip * (D_MODEL // N_CHIPS)))
    convs = _allreduce_small(jnp.concatenate([place_ssm, _pad_cols(place_sc, D_XBC)], axis=0))
    ssm_w8 = _to_kernel_xbc(convs[:8])
    ssm_bk = _to_kernel_xbc(ssm_conv_b)
    sc_w8 = convs[8:, :D_MODEL]

    bias_rows = _heads_to_rows(ssm_dt_bias[0])
    alog_rows = _heads_to_rows(ssm_A_log[0])
    drep = jnp.repeat(ssm_D[0], HEADDIM).reshape(1, D_SSM)

    n1 = _rmsnorm_fwd(xt, _tie(norm_mix_g, ag_tok, "tie_ag_ici"), "rmsnorm_mix")
    (proj,) = _matmul([(n1, wt_main)], tb=True, out_dtypes=[F32], name="mm_proj")
    (dt_raw,) = _matmul([(n1, wt_dt)], tb=True, out_dtypes=[F32], name="mm_proj_dt")
    ag_bufs = _split_wait("ag_ici_wait", ag_ss, ag_rs, ag_bufs, _build_ag_ici, after=[dt_raw])
    fw_ss, fw_rs, fw_bufs, fw_tok = _split_start("ag_fwd_start", ag_bufs, _build_ag_fwd, 12)
    xbc = _ssm_conv_fwd(proj, ssm_w8, _tie(ssm_bk, fw_tok, "tie_ag_fwd"))
    dtr = jnp.pad(dt_raw[:, :N_HEADS].T.reshape(N_GROUPS, HEADS_PER_GROUP, T), ((0, 0), (0, 4), (0, 0)))
    y_ssd, hs = _ssd_fwd(xbc, dtr, bias_rows, alog_rows, drep)
    y_mix = _shortconv_fwd(proj, sc_w8, _gated_norm_fwd(y_ssd, proj, ssm_norm_g))
    gath = _split_wait("ag_fwd_wait", fw_ss, fw_rs, fw_bufs, _build_ag_fwd, after=[y_mix])
    w_out_f = gath[0].reshape(2 * D_MODEL, D_MODEL)
    w_gate3 = gath[1].reshape(N_CHIPS, D_MODEL, D_FF // N_CHIPS)
    w_up3 = gath[2].reshape(N_CHIPS, D_MODEL, D_FF // N_CHIPS)
    w_down_f = gath[3].reshape(D_FF, D_MODEL)
    (h1,) = _matmul([(y_mix, w_out_f)], out_dtypes=[F32], name="mm_out", extras=[xt],
                    epilogue=lambda acc, res: (acc + res,))
    n2 = _rmsnorm_fwd(h1, norm_ffn_g, "rmsnorm_ffn")
    g_act, u_act, a_act = _ffn_fwd(n2, w_gate3, w_up3)
    (h2,) = _matmul([(a_act, w_down_f)], out_dtypes=[F32], name="mm_down", extras=[h1],
                    epilogue=lambda acc, res: (acc + res,))

    dh2, dh2b, dg_final, loss_part = _loss_and_final_bwd(h2, tgt, norm_final_g.reshape(1, D_MODEL))
    dg_act, du_act = _matmul([(dh2b, w_down_f)], tb=True, out_dtypes=[BF16, BF16], name="mm_down_bwd",
                             tn=512, extras=[g_act, u_act], epilogue=_swiglu_bwd, nsub=2)
    (dw_down,) = _matmul([(a_act, dh2b)], ta=True, out_dtypes=[F32], name="mm_dw_down", tm=1408)
    (dn2,) = _matmul([(dg_act, w_gate3), (du_act, w_up3)], tb=True, b3d=True, out_dtypes=[BF16],
                     name="mm_ffn_in_bwd")
    (dw_gate,) = _matmul([(n2, dg_act)], ta=True, out_dtypes=[F32], name="mm_dw_gate", tn=1408, col_shards=True)
    (dw_up,) = _matmul([(n2, du_act)], ta=True, out_dtypes=[F32], name="mm_dw_up", tn=1408, col_shards=True)
    dh1, dh1b, dg_ffn = _rmsnorm_bwd(dn2, h1, norm_ffn_g, dh2, "rmsnorm_ffn_bwd")
    (dw_out,) = _matmul([(y_mix, dh1b)], ta=True, out_dtypes=[F32], name="mm_dw_out")

    def halves(g):
        return g.reshape(N_CHIPS, 2, g.shape[1] // 2, g.shape[2])

    def landing(shape, dtype):
        return lax.empty(shape, dtype)

    names1 = names[1:]
    ps1 = [halves(dw_out.reshape(N_CHIPS, -1, D_MODEL)), halves(dw_gate), halves(dw_up),
           halves(dw_down.reshape(N_CHIPS, -1, D_MODEL))]
    r0_1 = [landing((N_CHIPS, 1) + p.shape[2:], F32) for p in ps1]
    sw_ss, sw_rs, sw_arr, sw_tok = _split_start("rs1_swap_start", ps1 + r0_1, _build_rs_swap, 4)
    (dmix,) = _matmul([(dh1b, w_out_f)], tb=True, out_dtypes=[BF16], name="mm_out_bwd", deps=[sw_tok])
    dproj, dw_sc = _shortconv_bwd(dmix, proj, sc_w8)
    dy_ssd, dproj, dg_ssmnorm = _gated_norm_bwd(dmix, y_ssd, proj, ssm_norm_g, dproj)
    sw_arr = _split_wait("rs1_swap_wait", sw_ss, sw_rs, sw_arr, _build_rs_swap, after=[dy_ssd])
    qs1 = [_rs_add_pair(p, r, c_arr, "rs_add_pair_" + nm) for p, r, nm in zip(sw_arr[:4], sw_arr[4:], names1)]
    r1_1 = [landing(q.shape, BF16) for q in qs1]
    ic_ss, ic_rs, ic_arr, ic_tok = _split_start("rs1_ici_start", qs1 + r1_1, _build_rs_ici, 12)
    dxbc_act, ddtr, dbias_acc, dalog_acc, dD_acc = _ssd_bwd(
        xbc, dtr, bias_rows, alog_rows, _tie(drep, ic_tok, "tie_rs1_ici"), dy_ssd, hs)
    dproj, dw_ssmconv, db_ssmconv = _ssm_conv_bwd(dxbc_act, proj, ssm_w8, ssm_bk, dproj)
    dw_ssmconv, db_ssmconv = _from_kernel_xbc(dw_ssmconv), _from_kernel_xbc(db_ssmconv)
    ic_arr = _split_wait("rs1_ici_wait", ic_ss, ic_rs, ic_arr, _build_rs_ici, after=[dproj])
    g1 = [_rs_add_chips(r, q, place_arr, "rs_add_chips_" + nm) for q, r, nm in zip(ic_arr[:4], ic_arr[4:], names1)]
    sh_ss, sh_rs, sh_arr, sh_tok = _split_start("rs1_share_start", g1, _build_rs_share, 4)

    ddt_raw = _pad_cols(ddtr[:, :HEADS_PER_GROUP, :].reshape(N_HEADS, T).T, DT_PAD).astype(BF16)
    (dwt_main,) = _matmul([(dproj, n1)], ta=True, out_dtypes=[F32], name="mm_dw_main", deps=[sh_tok])
    (dwt_dt,) = _matmul([(ddt_raw, n1)], ta=True, out_dtypes=[F32], name="mm_dw_dt")
    dwt_full = _from_kernel_rows(dwt_main, dwt_dt[:N_HEADS])
    p_in = dwt_full.reshape(N_CHIPS, cs_in, 2, D_MODEL // 2).transpose(0, 2, 1, 3)
    s2_ss, s2_rs, s2_arr, s2_tok = _split_start(
        "rs2_swap_start", [p_in, landing((N_CHIPS, 1) + p_in.shape[2:], F32)], _build_rs_swap, 1)
    (dn1a,) = _matmul([(dproj, wt_main)], out_dtypes=[F32], name="mm_proj_bwd", deps=[s2_tok])
    g1 = _split_wait("rs1_share_wait", sh_ss, sh_rs, sh_arr, _build_rs_share, after=[dn1a])
    s2_arr = _split_wait("rs2_swap_wait", s2_ss, s2_rs, s2_arr, _build_rs_swap, after=[dn1a])
    q_in = _rs_add_pair(s2_arr[0], s2_arr[1], c_arr, "rs_add_pair_w_in")
    i2_ss, i2_rs, i2_arr, i2_tok = _split_start(
        "rs2_ici_start", [q_in, landing(q_in.shape, BF16)], _build_rs_ici, 3)
    (dn1,) = _matmul([(ddt_raw, wt_dt)], out_dtypes=[BF16], name="mm_proj_dt_bwd", extras=[dn1a],
                     epilogue=lambda acc, res: (acc + res,), deps=[i2_tok])
    dx, _, dg_mix = _rmsnorm_bwd(dn1, xt, norm_mix_g, dh1, "rmsnorm_mix_bwd")

    big_m = [m_w_in[0].T, m_w_out[0], m_w_gate[0], m_w_up[0], m_w_down[0]]
    big_v = [v_w_in[0].T, v_w_out[0], v_w_gate[0], v_w_up[0], v_w_down[0]]
    big_grads = [None] + [g.reshape(w.shape) for g, w in zip(g1, big[1:])]
    big_out = {}
    for k in range(1, 5):
        big_out[names[k]] = _adamw(big[k], big_grads[k], big_m[k], big_v[k], "adamw_" + names[k], deps=[i2_tok])
    i2_arr = _split_wait("rs2_ici_wait", i2_ss, i2_rs, i2_arr, _build_rs_ici, after=[big_out[names[4]][0], dx])
    g_in_red = _rs_add_chips(i2_arr[1], i2_arr[0], place_arr, "rs_add_chips_w_in")
    s3_ss, s3_rs, s3_arr, s3_tok = _split_start("rs2_share_start", [g_in_red], _build_rs_share, 1)

    dD = jnp.sum(dD_acc.reshape(N_HEADS, HEADDIM), axis=-1)
    heads_row = jnp.concatenate([_rows_to_heads(dbias_acc), _rows_to_heads(dalog_acc), dD,
                                 loss_part.reshape(1)]).reshape(1, -1)
    small = jnp.concatenate([
        dw_ssmconv,
        _pad_cols(dw_sc, D_XBC),
        db_ssmconv,
        jnp.concatenate([dg_mix, dg_ssmnorm], axis=1),
        jnp.concatenate([dg_ffn, dg_final], axis=1),
        _pad_cols(heads_row, D_XBC),
        jnp.zeros((4, D_XBC), F32),
    ], axis=0)
    tot = _allreduce_small(small, deps=[s3_tok])
    loss = tot[19, 3 * N_HEADS]

    cs_ssm, cs_sc = D_XBC // N_CHIPS, D_MODEL // N_CHIPS
    g_ssm_conv = lax.dynamic_slice(tot[0:K_SSM], (0, chip * cs_ssm), (K_SSM, cs_ssm))
    g_sc_conv = lax.dynamic_slice(tot[8:8 + K_SC, :D_MODEL], (0, chip * cs_sc), (K_SC, cs_sc))
    small_grads = {
        "norm_mix_g": tot[17:18, :D_MODEL], "ssm_conv_w": g_ssm_conv, "ssm_conv_b": tot[16:17],
        "ssm_dt_bias": tot[19:20, 0:N_HEADS], "ssm_A_log": tot[19:20, N_HEADS:2 * N_HEADS],
        "ssm_D": tot[19:20, 2 * N_HEADS:3 * N_HEADS], "ssm_norm_g": tot[17:18, D_MODEL:],
        "sc_conv_w": g_sc_conv, "norm_ffn_g": tot[18:19, :D_MODEL], "norm_final_g": tot[18:19, D_MODEL:],
    }
    small_w = {"norm_mix_g": (norm_mix_g, m_norm_mix_g, v_norm_mix_g),
               "ssm_conv_w": (ssm_conv_w[0], m_ssm_conv_w[0], v_ssm_conv_w[0]),
               "ssm_conv_b": (ssm_conv_b, m_ssm_conv_b, v_ssm_conv_b),
               "ssm_dt_bias": (ssm_dt_bias, m_ssm_dt_bias, v_ssm_dt_bias),
               "ssm_A_log": (ssm_A_log, m_ssm_A_log, v_ssm_A_log),
               "ssm_D": (ssm_D, m_ssm_D, v_ssm_D),
               "ssm_norm_g": (ssm_norm_g, m_ssm_norm_g, v_ssm_norm_g),
               "sc_conv_w": (sc_conv_w[0], m_sc_conv_w[0], v_sc_conv_w[0]),
               "norm_ffn_g": (norm_ffn_g, m_norm_ffn_g, v_norm_ffn_g),
               "norm_final_g": (norm_final_g.reshape(1, -1), m_norm_final_g.reshape(1, -1),
                                v_norm_final_g.reshape(1, -1))}
    PW = 1024
    order = list(small_w)

    def pack(arrs):
        rows = []
        for a in arrs:
            flat = a.reshape(-1)
            n = -(-flat.shape[0] // PW) * PW
            rows.append(jnp.pad(flat, (0, n - flat.shape[0])).reshape(-1, PW))
        slab = jnp.concatenate(rows, axis=0)
        return _pad_rows(slab, -(-slab.shape[0] // 8) * 8)

    wp = pack([small_w[k][0] for k in order])
    mp = pack([small_w[k][1] for k in order])
    vp = pack([small_w[k][2] for k in order])
    gp = pack([small_grads[k] for k in order])
    sd, sm, sv = _adamw(wp, gp, mp, vp, "adamw_small")

    def unpack(slab):
        out, row = {}, 0
        for k in order:
            shape = small_w[k][0].shape
            size = 1
            for s in shape:
                size *= s
            nr = -(-size // PW)
            out[k] = slab[row:row + nr].reshape(-1)[:size].reshape(shape)
            row += nr
        return out

    s_delta, s_m, s_v = unpack(sd), unpack(sm), unpack(sv)

    (g_in_full,) = _split_wait("rs2_share_wait", s3_ss, s3_rs, s3_arr, _build_rs_share, after=[sd])
    d_t, m_t, v_t, g_t = _adamw_halves(big[0], g_in_full, big_m[0], big_v[0], "adamw_" + names[0])
    big_grads[0] = g_t.T
    big_out[names[0]] = (d_t.T, m_t.T, v_t.T)
    big_g = dict(zip(names, big_grads))

    weight_order = ["norm_mix_g", "w_in", "ssm_conv_w", "ssm_conv_b", "ssm_dt_bias", "ssm_A_log", "ssm_D",
                    "ssm_norm_g", "sc_conv_w", "w_out", "norm_ffn_g", "w_gate", "w_up", "w_down", "norm_final_g"]
    lead = {"ssm_conv_w", "sc_conv_w", "w_in", "w_out", "w_gate", "w_up", "w_down"}

    def shaped(nm, a):
        if nm == "norm_final_g":
            return a.reshape(D_MODEL)
        return a[None] if nm in lead else a

    grads, deltas, new_m, new_v = [], [], [], []
    for nm in weight_order:
        if nm in big_out:
            g, (d, m, v) = big_g[nm], big_out[nm]
        else:
            g, d, m, v = small_grads[nm], s_delta[nm], s_m[nm], s_v[nm]
        grads.append(shaped(nm, g))
        deltas.append(shaped(nm, d))
        new_m.append(shaped(nm, m))
        new_v.append(shaped(nm, v))
    return (loss, dx[None], *grads, *deltas, *new_m, *new_v)


def _swiglu_bwd(da, g, u):
    gf, uf = g.astype(F32), u.astype(F32)
    return da * uf * _dsilu(gf), da * _silu(gf)


def _ffn_fwd(n2, w_gate, w_up):
    T, K = n2.shape
    tn = w_gate.shape[2]
    N = N_CHIPS * tn
    tm = _tile(T, 512)
    sub = _tile(tm, 256)

    def body(a_ref, wg_ref, wu_ref, g_ref, u_ref, act_ref):
        for s in range(tm // sub):
            rows = pl.ds(s * sub, sub)
            a = a_ref[rows, :]
            g = jnp.dot(a, wg_ref[...], preferred_element_type=F32)
            u = jnp.dot(a, wu_ref[...], preferred_element_type=F32)
            g_ref[rows, :] = g.astype(BF16)
            u_ref[rows, :] = u.astype(BF16)
            act_ref[rows, :] = (_silu(g) * u).astype(BF16)

    a_spec = pl.BlockSpec((tm, K), lambda j, i: (i, 0))
    b_spec = pl.BlockSpec((None, K, tn), lambda j, i: (j, 0, 0))
    o_spec = pl.BlockSpec((tm, tn), lambda j, i: (i, j))
    return pl.pallas_call(
        body, name="ffn_fwd", grid=(N // tn, T // tm),
        in_specs=[a_spec, b_spec, b_spec], out_specs=[o_spec] * 3,
        out_shape=[jax.ShapeDtypeStruct((T, N), BF16)] * 3,
        compiler_params=_cparams(("parallel", "parallel")),
    )(n2, w_gate, w_up)
```

```python
import functools

import jax
import jax.numpy as jnp
from jax import lax
from jax.experimental import pallas as pl
from jax.experimental.pallas import tpu as pltpu

F32 = jnp.float32
BF16 = jnp.bfloat16
MESH = pl.DeviceIdType.MESH

D_MODEL = 2048
D_SSM = 2048
HEADDIM = 64
N_HEADS = 32
N_GROUPS = 8
HEADS_PER_GROUP = 4
N_STATE = 128
CHUNK = 128
K_SSM = 4
K_SC = 3
D_XBC = 4096
D_FF = 5632
D_IN = 12320
D_MAIN = 12288
OFF_XBC, OFF_CB, OFF_CC, OFF_CX = 2048, 6144, 8192, 10240
DT_PAD = 128
EPS = 1e-5
N_CHIPS = 4
N_DEV = 8

ADAM_LR = 0.001
ADAM_B1 = 0.9
ADAM_B2 = 0.999
ADAM_EPS = 1e-08
ADAM_WD = 0.01
ADAM_STEP = 10

V7X_VMEM_BYTES = 64 * 1024 * 1024
VMEM_LIMIT = V7X_VMEM_BYTES - 8 * 1024 * 1024


def _cparams(sem=None):
    if sem is None:
        return pltpu.CompilerParams(vmem_limit_bytes=VMEM_LIMIT)
    return pltpu.CompilerParams(dimension_semantics=sem, vmem_limit_bytes=VMEM_LIMIT)


def _tile(dim, pref, unit=128):
    best = None
    t = unit
    while t <= min(dim, pref):
        if dim % t == 0:
            best = t
        t += unit
    return best if best is not None else dim


def _sigmoid(x):
    return 1.0 / (1.0 + jnp.exp(-x))


def _silu(x):
    return x * _sigmoid(x)


def _dsilu(x):
    s = _sigmoid(x)
    return s * (1.0 + x * (1.0 - s))


def _softplus(x):
    return jnp.maximum(x, 0.0) + jnp.log(1.0 + jnp.exp(-jnp.abs(x)))


MATMUL_VMEM_BUDGET = 44 * 1024 * 1024


def _matmul(pairs, *, ta=False, tb=False, out_dtypes, name, tm=1024, tn=1024, tk=None, extras=(), epilogue=None,
            deps=(), col_shards=False, nsub=1, b3d=False):
    a0, b0 = pairs[0]
    M, K = (a0.shape[1], a0.shape[0]) if ta else a0.shape
    if b3d:
        N = b0.shape[1] if tb else b0.shape[0] * b0.shape[2]
        tk, tn = (b0.shape[2], tn) if tb else (tk, b0.shape[2])
    else:
        N = b0.shape[0] if tb else b0.shape[1]
    tm, tn = _tile(M, tm, 8 if M % 128 else 128), _tile(N, tn)
    npair, nex, ndep, nout = len(pairs), len(extras), len(deps), len(out_dtypes)
    if tk is None:
        fixed = 2 * tm * tn * (sum(jnp.dtype(d).itemsize for d in out_dtypes) + sum(e.dtype.itemsize for e in extras))
        tk = K
        while tk > 128 and (K % tk or tk % 128 or
                            fixed + 2 * npair * 2 * tk * (tm + tn) + (tm * tn * 4 if tk < K else 0) > MATMUL_VMEM_BUDGET):
            tk -= 128
    else:
        tk = _tile(K, tk)
    nk = K // tk
    if nk > 1 or tm % nsub or (tm // nsub) % 128:
        nsub = 1
    sub = tm // nsub
    dims = (((0 if ta else 1,), (1 if tb else 0,)), ((), ()))

    def body(*refs):
        a_refs = refs[0:2 * npair:2]
        b_refs = refs[1:2 * npair:2]
        ex_refs = refs[2 * npair:2 * npair + nex]
        o_refs = refs[2 * npair + nex + ndep:2 * npair + nex + ndep + nout]

        def dots(rows):
            s = None
            for a_ref, b_ref in zip(a_refs, b_refs):
                a = a_ref[...] if rows is None else (a_ref[:, rows] if ta else a_ref[rows, :])
                d = lax.dot_general(a, b_ref[...], dims, preferred_element_type=F32)
                s = d if s is None else s + d
            return s

        def finish(r, rows):
            ex = [e[...] if rows is None else e[rows, :] for e in ex_refs]
            outs = (r,) if epilogue is None else epilogue(r, *ex)
            for o_ref, o in zip(o_refs, outs):
                if rows is None:
                    o_ref[...] = o.astype(o_ref.dtype)
                else:
                    o_ref[rows, :] = o.astype(o_ref.dtype)

        if nk == 1:
            for s in range(nsub):
                rows = None if nsub == 1 else pl.ds(s * sub, sub)
                finish(dots(rows), rows)
            return

        acc = refs[-1]
        k = pl.program_id(2)

        @pl.when(k == 0)
        def _():
            acc[...] = dots(None)

        @pl.when(jnp.logical_and(k > 0, k < nk - 1))
        def _():
            acc[...] += dots(None)

        @pl.when(k == nk - 1)
        def _():
            finish(acc[...] + dots(None), None)

    a_spec = pl.BlockSpec((tk, tm), lambda i, j, k: (k, i)) if ta else pl.BlockSpec((tm, tk), lambda i, j, k: (i, k))
    if b3d:
        b_spec = (pl.BlockSpec((None, tn, tk), lambda i, j, k: (k, j, 0)) if tb
                  else pl.BlockSpec((None, tk, tn), lambda i, j, k: (j, k, 0)))
    else:
        b_spec = (pl.BlockSpec((tn, tk), lambda i, j, k: (j, k)) if tb
                  else pl.BlockSpec((tk, tn), lambda i, j, k: (k, j)))
    e_spec = pl.BlockSpec((tm, tn), lambda i, j, k: (i, j))
    if col_shards:
        o_spec = pl.BlockSpec((None, tm, tn), lambda i, j, k: (j, i, 0))
        o_shape = (N // tn, M, tn)
    else:
        o_spec, o_shape = e_spec, (M, N)
    args, in_specs = [], []
    for a, b in pairs:
        args += [a, b]
        in_specs += [a_spec, b_spec]
    args += list(extras) + list(deps)
    in_specs += [e_spec] * nex + [ANY] * ndep
    outs = pl.pallas_call(
        body,
        name=name,
        grid=(M // tm, N // tn, nk),
        in_specs=in_specs,
        out_specs=[o_spec] * nout,
        out_shape=[jax.ShapeDtypeStruct(o_shape, dt) for dt in out_dtypes],
        scratch_shapes=[pltpu.VMEM((tm, tn), F32)] if nk > 1 else [],
        compiler_params=_cparams(("parallel", "parallel", "arbitrary")),
    )(*args)
    return outs


def _cast_into_gather(w, chip_arr, name, split_cols=False):
    R, C = w.shape
    hr, hc = (R, C // 2) if split_cols else (R // 2, C)
    tr = _tile(hr, 512, 8)
    nb = hr // tr

    def body(chip_ref, w_ref, o_ref):
        o_ref[...] = w_ref[...].astype(BF16)

    in_map = (lambda h, i, chip_ref: (i, h)) if split_cols else (lambda h, i, chip_ref: (h * nb + i, 0))
    grid_spec = pltpu.PrefetchScalarGridSpec(
        num_scalar_prefetch=1, grid=(2, nb),
        in_specs=[pl.BlockSpec((tr, hc), in_map)],
        out_specs=pl.BlockSpec((None, tr, hc), lambda h, i, chip_ref: (2 * chip_ref[0] + h, i, 0)))
    return pl.pallas_call(
        body, name=name, grid_spec=grid_spec,
        out_shape=jax.ShapeDtypeStruct((N_DEV, hr, hc), BF16),
        compiler_params=_cparams(("parallel", "parallel")),
    )(chip_arr, w)


def _tie(small, token, name):
    def body(s_ref, t_ref, o_ref):
        o_ref[...] = s_ref[...]

    vm = pl.BlockSpec(memory_space=pltpu.VMEM)
    return pl.pallas_call(body, name=name, in_specs=[vm, ANY], out_specs=vm,
                          out_shape=jax.ShapeDtypeStruct(small.shape, small.dtype))(small, token)


def _rmsnorm_fwd(x, g, name):
    T, D = x.shape
    tt = _tile(T, 256)

    def body(x_ref, g_ref, n_ref):
        xv = x_ref[...]
        r = lax.rsqrt(jnp.mean(xv * xv, axis=-1, keepdims=True) + EPS)
        n_ref[...] = (xv * r * g_ref[...]).astype(BF16)

    return pl.pallas_call(
        body, name=name, grid=(T // tt,),
        in_specs=[pl.BlockSpec((tt, D), lambda i: (i, 0)), pl.BlockSpec((1, D), lambda i: (0, 0))],
        out_specs=pl.BlockSpec((tt, D), lambda i: (i, 0)),
        out_shape=jax.ShapeDtypeStruct((T, D), BF16),
        compiler_params=_cparams(("parallel",)),
    )(x, g)


def _rmsnorm_bwd(dn, x, g, res, name):
    T, D = x.shape
    tt = _tile(T, 256)

    def body(dn_ref, x_ref, g_ref, res_ref, dx_ref, dxb_ref, dg_ref):
        @pl.when(pl.program_id(0) == 0)
        def _():
            dg_ref[...] = jnp.zeros_like(dg_ref)

        xv = x_ref[...]
        dy = dn_ref[...].astype(F32)
        r = lax.rsqrt(jnp.mean(xv * xv, axis=-1, keepdims=True) + EPS)
        xhat = xv * r
        dxh = dy * g_ref[...]
        dx = res_ref[...] + r * (dxh - xhat * jnp.mean(dxh * xhat, axis=-1, keepdims=True))
        dx_ref[...] = dx
        dxb_ref[...] = dx.astype(BF16)
        dg_ref[...] += jnp.sum(dy * xhat, axis=0, keepdims=True)

    tok = pl.BlockSpec((tt, D), lambda i: (i, 0))
    vec = pl.BlockSpec((1, D), lambda i: (0, 0))
    return pl.pallas_call(
        body, name=name, grid=(T // tt,),
        in_specs=[tok, tok, vec, tok],
        out_specs=[tok, tok, vec],
        out_shape=[jax.ShapeDtypeStruct((T, D), F32), jax.ShapeDtypeStruct((T, D), BF16),
                   jax.ShapeDtypeStruct((1, D), F32)],
        compiler_params=_cparams(("arbitrary",)),
    )(dn, x, g, res)


def _loss_and_final_bwd(h2, target, gf):
    T, D = h2.shape
    tt = _tile(T, 256)

    def body(h_ref, t_ref, g_ref, dh_ref, dhb_ref, dg_ref, loss_ref):
        @pl.when(pl.program_id(0) == 0)
        def _():
            dg_ref[...] = jnp.zeros_like(dg_ref)
            loss_ref[...] = jnp.zeros_like(loss_ref)

        xv = h_ref[...]
        r = lax.rsqrt(jnp.mean(xv * xv, axis=-1, keepdims=True) + EPS)
        xhat = xv * r
        err = xhat * g_ref[...] - t_ref[...]
        loss_ref[...] += 0.5 * jnp.sum(jnp.mean(err * err, axis=-1, keepdims=True), axis=0, keepdims=True)
        dy = err * (1.0 / D)
        dxh = dy * g_ref[...]
        dx = r * (dxh - xhat * jnp.mean(dxh * xhat, axis=-1, keepdims=True))
        dh_ref[...] = dx
        dhb_ref[...] = dx.astype(BF16)
        dg_ref[...] += jnp.sum(dy * xhat, axis=0, keepdims=True)

    tok = pl.BlockSpec((tt, D), lambda i: (i, 0))
    vec = pl.BlockSpec((1, D), lambda i: (0, 0))
    return pl.pallas_call(
        body, name="loss_final_bwd", grid=(T // tt,),
        in_specs=[tok, tok, vec],
        out_specs=[tok, tok, vec, pl.BlockSpec((1, 1), lambda i: (0, 0))],
        out_shape=[jax.ShapeDtypeStruct((T, D), F32), jax.ShapeDtypeStruct((T, D), BF16),
                   jax.ShapeDtypeStruct((1, D), F32), jax.ShapeDtypeStruct((1, 1), F32)],
        compiler_params=_cparams(("arbitrary",)),
    )(h2, target, gf)


def _gated_norm_fwd(y, proj, g):
    T, D = y.shape
    tt = _tile(T, 256)

    def body(y_ref, z_ref, g_ref, o_ref):
        yg = y_ref[...] * _silu(z_ref[...])
        r = lax.rsqrt(jnp.mean(yg * yg, axis=-1, keepdims=True) + EPS)
        o_ref[...] = (yg * r * g_ref[...]).astype(BF16)

    tok = pl.BlockSpec((tt, D), lambda i: (i, 0))
    return pl.pallas_call(
        body, name="gated_norm_fwd", grid=(T // tt,),
        in_specs=[tok, tok, pl.BlockSpec((1, D), lambda i: (0, 0))],
        out_specs=tok,
        out_shape=jax.ShapeDtypeStruct((T, 2 * D_MODEL), BF16),
        compiler_params=_cparams(("parallel",)),
    )(y, proj, g)


def _gated_norm_bwd(dmix, y, proj, g, dproj):
    T, D = y.shape
    tt = _tile(T, 256)

    def body(do_ref, y_ref, z_ref, g_ref, dp_ref, dy_ref, dz_ref, dg_ref):
        @pl.when(pl.program_id(0) == 0)
        def _():
            dg_ref[...] = jnp.zeros_like(dg_ref)

        yv, zv = y_ref[...], z_ref[...]
        do = do_ref[...].astype(F32)
        sz = _silu(zv)
        yg = yv * sz
        r = lax.rsqrt(jnp.mean(yg * yg, axis=-1, keepdims=True) + EPS)
        xhat = yg * r
        dxh = do * g_ref[...]
        dyg = r * (dxh - xhat * jnp.mean(dxh * xhat, axis=-1, keepdims=True))
        dy_ref[...] = dyg * sz
        dz_ref[...] = (dyg * yv * _dsilu(zv)).astype(BF16)
        dg_ref[...] += jnp.sum(do * xhat, axis=0, keepdims=True)

    tok = pl.BlockSpec((tt, D), lambda i: (i, 0))
    vec = pl.BlockSpec((1, D), lambda i: (0, 0))
    return pl.pallas_call(
        body, name="gated_norm_bwd", grid=(T // tt,),
        in_specs=[tok, tok, tok, vec, ANY],
        out_specs=[tok, tok, vec],
        out_shape=[jax.ShapeDtypeStruct((T, D), F32), jax.ShapeDtypeStruct(dproj.shape, BF16),
                   jax.ShapeDtypeStruct((1, D), F32)],
        input_output_aliases={4: 1},
        compiler_params=_cparams(("arbitrary",)),
    )(dmix, y, proj, g, dproj)


HALO = 8


def _shift_down(cur, prev8, s):
    ext = jnp.concatenate([prev8, cur], axis=0)
    return pltpu.roll(ext, s, axis=0)[HALO:]


def _shift_up(cur, next8, s):
    n = cur.shape[0]
    ext = jnp.concatenate([cur, next8], axis=0)
    return pltpu.roll(ext, n + HALO - s, axis=0)[:n]


def _conv_specs(tt, cb, col_off_blocks, nt):
    hb = tt // HALO
    cur = pl.BlockSpec((tt, cb), lambda j, i: (i, col_off_blocks + j))
    prev = pl.BlockSpec((HALO, cb), lambda j, i: (jnp.maximum(i * hb - 1, 0), col_off_blocks + j))
    nxt = pl.BlockSpec((HALO, cb), lambda j, i: (jnp.minimum((i + 1) * hb, nt * hb - 1), col_off_blocks + j))
    return cur, prev, nxt


def _causal_conv(cur, prev8, w, K):
    y = cur * w[K - 1:K, :]
    for k in range(K - 1):
        y = y + _shift_down(cur, prev8, K - 1 - k) * w[k:k + 1, :]
    return y


def _anticausal_conv(cur, next8, w, K):
    y = cur * w[K - 1:K, :]
    for k in range(K - 1):
        y = y + _shift_up(cur, next8, K - 1 - k) * w[k:k + 1, :]
    return y


def _ssm_conv_fwd(proj, w8, b):
    T = proj.shape[0]
    tt, cb = _tile(T, 512), 512
    nt = T // tt
    cur, prev, _ = _conv_specs(tt, cb, OFF_XBC // cb, nt)

    def body(u_ref, up_ref, w_ref, b_ref, o_ref):
        first = pl.program_id(1) == 0
        p8 = jnp.where(first, 0.0, up_ref[...])
        pre = _causal_conv(u_ref[...], p8, w_ref[...], K_SSM) + b_ref[...]
        o_ref[...] = _silu(pre)

    return pl.pallas_call(
        body, name="ssm_conv_fwd", grid=(D_XBC // cb, nt),
        in_specs=[cur, prev, pl.BlockSpec((8, cb), lambda j, i: (0, j)), pl.BlockSpec((1, cb), lambda j, i: (0, j))],
        out_specs=pl.BlockSpec((tt, cb), lambda j, i: (i, j)),
        out_shape=jax.ShapeDtypeStruct((T, D_XBC), F32),
        compiler_params=_cparams(("parallel", "parallel")),
    )(proj, proj, w8, b)


def _ssm_conv_bwd(dact, proj, w8, b, dproj):
    T = proj.shape[0]
    tt, cb = _tile(T, 512), 512
    nt = T // tt
    cur, prev, nxt = _conv_specs(tt, cb, OFF_XBC // cb, nt)
    dcur, dprev, dnxt = _conv_specs(tt, cb, 0, nt)

    def dpre_of(d, u, p8, w, bb):
        pre = _causal_conv(u, p8, w, K_SSM) + bb
        return d * _dsilu(pre)

    def body(d_ref, dn_ref, u_ref, up_ref, un_ref, w_ref, b_ref, dp_ref, dx_ref, dw_ref, db_ref):
        i = pl.program_id(1)

        @pl.when(i == 0)
        def _():
            dw_ref[...] = jnp.zeros_like(dw_ref)
            db_ref[...] = jnp.zeros_like(db_ref)

        w, bb = w_ref[...], b_ref[...]
        u = u_ref[...]
        p8 = jnp.where(i == 0, 0.0, up_ref[...])
        dpre = dpre_of(d_ref[...], u, p8, w, bb)
        un = un_ref[...]
        dpre_n = dpre_of(dn_ref[...], un, u[tt - HALO:, :], w, bb)
        dpre_n = jnp.where(i == nt - 1, 0.0, dpre_n)
        dx_ref[...] = _anticausal_conv(dpre, dpre_n, w, K_SSM).astype(BF16)
        rows = [jnp.sum(dpre * _shift_down(u, p8, K_SSM - 1 - k), axis=0, keepdims=True) for k in range(K_SSM - 1)]
        rows.append(jnp.sum(dpre * u, axis=0, keepdims=True))
        rows.append(jnp.zeros((8 - K_SSM, cb), F32))
        dw_ref[...] += jnp.concatenate(rows, axis=0)
        db_ref[...] += jnp.sum(dpre, axis=0, keepdims=True)

    wspec = pl.BlockSpec((8, cb), lambda j, i: (0, j))
    bspec = pl.BlockSpec((1, cb), lambda j, i: (0, j))
    return pl.pallas_call(
        body, name="ssm_conv_bwd", grid=(D_XBC // cb, nt),
        in_specs=[dcur, dnxt, cur, prev, nxt, wspec, bspec, ANY],
        out_specs=[pl.BlockSpec((tt, cb), lambda j, i: (i, OFF_XBC // cb + j)), wspec, bspec],
        out_shape=[jax.ShapeDtypeStruct(dproj.shape, BF16), jax.ShapeDtypeStruct((8, D_XBC), F32),
                   jax.ShapeDtypeStruct((1, D_XBC), F32)],
        input_output_aliases={7: 0},
        compiler_params=_cparams(("parallel", "arbitrary")),
    )(dact, dact, proj, proj, proj, w8, b, dproj)


SCB = 512
SC3 = 3 * SCB


def _sc_specs(tt, nt):
    hb = tt // HALO
    cur = pl.BlockSpec((tt, SC3), lambda j, i: (i, OFF_CB // SC3 + j))
    prev = pl.BlockSpec((HALO, SC3), lambda j, i: (jnp.maximum(i * hb - 1, 0), OFF_CB // SC3 + j))
    nxt = pl.BlockSpec((HALO, SC3), lambda j, i: (jnp.minimum((i + 1) * hb, nt * hb - 1), OFF_CB // SC3 + j))
    return cur, prev, nxt


def _shortconv_fwd(proj, w8, ymix):
    T = proj.shape[0]
    tt = _tile(T, 512)
    nt = T // tt
    cur, prev, _ = _sc_specs(tt, nt)

    def body(p_ref, pp_ref, w_ref, y_ref, o_ref):
        p, pp = p_ref[...], pp_ref[...]
        v = p[:, SCB:2 * SCB] * p[:, 2 * SCB:]
        vp = jnp.where(pl.program_id(1) == 0, 0.0, pp[:, SCB:2 * SCB] * pp[:, 2 * SCB:])
        o_ref[...] = (p[:, :SCB] * _causal_conv(v, vp, w_ref[...], K_SC)).astype(BF16)

    return pl.pallas_call(
        body, name="shortconv_fwd", grid=(D_MODEL // SCB, nt),
        in_specs=[cur, prev, pl.BlockSpec((8, SCB), lambda j, i: (0, j)), ANY],
        out_specs=pl.BlockSpec((tt, SCB), lambda j, i: (i, D_SSM // SCB + j)),
        out_shape=jax.ShapeDtypeStruct(ymix.shape, BF16),
        input_output_aliases={3: 0},
        compiler_params=_cparams(("parallel", "parallel")),
    )(proj, proj, w8, ymix)


def _shortconv_bwd(dmix, proj, w8):
    T = proj.shape[0]
    tt = _tile(T, 512)
    nt = T // tt
    hb = tt // HALO
    cur, prev, nxt = _sc_specs(tt, nt)
    d_s = pl.BlockSpec((tt, SCB), lambda j, i: (i, D_SSM // SCB + j))
    dn_s = pl.BlockSpec((HALO, SCB), lambda j, i: (jnp.minimum((i + 1) * hb, nt * hb - 1), D_SSM // SCB + j))

    def body(d_ref, dn_ref, p_ref, pp_ref, pn_ref, w_ref, dp_ref, dw_ref):
        i = pl.program_id(1)

        @pl.when(i == 0)
        def _():
            dw_ref[...] = jnp.zeros_like(dw_ref)

        w = w_ref[...]
        p, pp = p_ref[...], pp_ref[...]
        gb, gc, u = p[:, :SCB], p[:, SCB:2 * SCB], p[:, 2 * SCB:]
        v = gc * u
        vp = jnp.where(i == 0, 0.0, pp[:, SCB:2 * SCB] * pp[:, 2 * SCB:])
        d = d_ref[...].astype(F32)
        dp_ref[:, :SCB] = (d * _causal_conv(v, vp, w, K_SC)).astype(BF16)
        dcv = d * gb
        dcv_n = jnp.where(i == nt - 1, 0.0, dn_ref[...].astype(F32) * pn_ref[:, :SCB])
        dv = _anticausal_conv(dcv, dcv_n, w, K_SC)
        dp_ref[:, SCB:2 * SCB] = (dv * u).astype(BF16)
        dp_ref[:, 2 * SCB:] = (dv * gc).astype(BF16)
        rows = [jnp.sum(dcv * _shift_down(v, vp, K_SC - 1 - k), axis=0, keepdims=True) for k in range(K_SC - 1)]
        rows.append(jnp.sum(dcv * v, axis=0, keepdims=True))
        rows.append(jnp.zeros((8 - K_SC, SCB), F32))
        dw_ref[...] += jnp.concatenate(rows, axis=0)

    wspec = pl.BlockSpec((8, SCB), lambda j, i: (0, j))
    return pl.pallas_call(
        body, name="shortconv_bwd", grid=(D_MODEL // SCB, nt),
        in_specs=[d_s, dn_s, cur, prev, nxt, wspec],
        out_specs=[cur, wspec],
        out_shape=[jax.ShapeDtypeStruct((T, D_MAIN), BF16), jax.ShapeDtypeStruct((8, D_MODEL), F32)],
        compiler_params=_cparams(("parallel", "arbitrary")),
    )(dmix, dmix, proj, proj, proj, w8)


GW = HEADS_PER_GROUP * HEADDIM
HI = lax.Precision.HIGHEST


def _dot(a, b):
    return jnp.dot(a.astype(BF16), b.astype(BF16), preferred_element_type=F32)


def _dot_nt(a, b):
    return lax.dot_general(a.astype(BF16), b.astype(BF16), (((1,), (1,)), ((), ())), preferred_element_type=F32)


def _dot_tn(a, b):
    return lax.dot_general(a.astype(BF16), b.astype(BF16), (((0,), (0,)), ((), ())), preferred_element_type=F32)


def _dot_hi(a, b):
    return jnp.dot(a, b, precision=HI, preferred_element_type=F32)


def _dot_nt_hi(a, b):
    return lax.dot_general(a, b, (((1,), (1,)), ((), ())), precision=HI, preferred_element_type=F32)


def _head_cols(rows):
    parts = [jnp.broadcast_to(rows[r:r + 1, :], (HEADDIM, CHUNK)) for r in range(HEADS_PER_GROUP)]
    return jnp.concatenate(parts, axis=0).T


def _head_rows(rows):
    parts = [jnp.broadcast_to(rows[r:r + 1, :], (HEADDIM, N_STATE)) for r in range(HEADS_PER_GROUP)]
    return jnp.concatenate(parts, axis=0)


def _ssd_common(dtr, bias, alog):
    dt = _softplus(dtr + bias)
    A = -jnp.exp(alog)
    a = dt * A
    ki = lax.broadcasted_iota(jnp.int32, (CHUNK, CHUNK), 0)
    si = lax.broadcasted_iota(jnp.int32, (CHUNK, CHUNK), 1)
    upper = (ki <= si).astype(F32)
    cs = _dot_hi(a, upper)
    cs_last = jnp.broadcast_to(cs[:, CHUNK - 1:CHUNK], (8, CHUNK))
    return dt, A, a, cs, cs_last


def _decay_matrix(cs, r):
    li = lax.broadcasted_iota(jnp.int32, (CHUNK, CHUNK), 0)
    si = lax.broadcasted_iota(jnp.int32, (CHUNK, CHUNK), 1)
    causal = li >= si
    R = jnp.broadcast_to(cs[r:r + 1, :], (CHUNK, CHUNK))
    seg = jnp.where(causal, R.T - R, 0.0)
    return jnp.where(causal, jnp.exp(seg), 0.0)


GXBC = GW + 2 * N_STATE


GS = 2


def _ssd_in_specs(nc, rev):
    cix = (lambda c: nc - 1 - c) if rev else (lambda c: c)
    x_s = pl.BlockSpec((CHUNK, GS * GW), lambda g, c: (cix(c), g))
    xbc_s = pl.BlockSpec((CHUNK, GS * GXBC), lambda g, c: (cix(c), g))
    dtr_s = pl.BlockSpec((GS, 8, CHUNK), lambda g, c: (g, 0, cix(c)))
    row_s = pl.BlockSpec((GS, 8, CHUNK), lambda g, c: (g, 0, 0))
    drep_s = pl.BlockSpec((1, GS * GW), lambda g, c: (0, g))
    hs_s = pl.BlockSpec((1, GS * GW, N_STATE), lambda g, c: (cix(c), g, 0))
    return x_s, xbc_s, dtr_s, row_s, drep_s, hs_s


def _xbc_parts(xbc_ref, gi):
    o = gi * GXBC
    return xbc_ref[:, o:o + GW], xbc_ref[:, o + GW:o + GW + N_STATE], xbc_ref[:, o + GW + N_STATE:o + GXBC]


def _ssd_fwd(xbc, dtr, bias, alog, drep):
    T = xbc.shape[0]
    nc = T // CHUNK
    x_s, xbc_s, dtr_s, row_s, drep_s, hs_s = _ssd_in_specs(nc, False)

    def body(xbc_ref, dtr_ref, bias_ref, alog_ref, drep_ref, y_ref, hs_ref, h_scr):
        @pl.when(pl.program_id(1) == 0)
        def _():
            h_scr[...] = jnp.zeros_like(h_scr)

        for gi in range(GS):
            cols, rows = slice(gi * GW, (gi + 1) * GW), pl.ds(gi * GW, GW)
            x, Bm, Cm = _xbc_parts(xbc_ref, gi)
            dt, A, a, cs, cs_last = _ssd_common(dtr_ref[gi], bias_ref[gi], alog_ref[gi])
            E = _head_cols(jnp.exp(cs))
            W = _head_cols(jnp.exp(cs_last - cs) * dt)
            X = (x * _head_cols(dt)).astype(BF16)
            CB = _dot_nt(Cm, Bm)
            col = lax.broadcasted_iota(jnp.int32, (CHUNK, GW), 1) // HEADDIM
            y = jnp.zeros((CHUNK, GW), F32)
            for r in range(HEADS_PER_GROUP):
                M = CB * _decay_matrix(cs, r)
                y = y + jnp.where(col == r, _dot(M, X), 0.0)
            h = h_scr[rows, :]
            hs_ref[0, rows, :] = h
            y = y + _dot_nt(Cm, h) * E
            y_ref[:, cols] = y + drep_ref[:, cols] * x
            h_scr[rows, :] = h * _head_rows(jnp.exp(cs_last)) + _dot_tn(x * W, Bm)

    return pl.pallas_call(
        body, name="ssd_fwd", grid=(N_GROUPS // GS, nc),
        in_specs=[xbc_s, dtr_s, row_s, row_s, drep_s],
        out_specs=[x_s, hs_s],
        out_shape=[jax.ShapeDtypeStruct((T, D_SSM), F32), jax.ShapeDtypeStruct((nc, D_SSM, N_STATE), F32)],
        scratch_shapes=[pltpu.VMEM((GS * GW, N_STATE), F32)],
        compiler_params=_cparams(("parallel", "arbitrary")),
    )(xbc, dtr, bias, alog, drep)


def _ssd_bwd(xbc, dtr, bias, alog, drep, dy, hs):
    T = xbc.shape[0]
    nc = T // CHUNK
    x_s, xbc_s, dtr_s, row_s, drep_s, hs_s = _ssd_in_specs(nc, True)

    def body(xbc_ref, dtr_ref, bias_ref, alog_ref, drep_ref, dy_ref, hs_ref,
             dxbc_ref, ddtr_ref, dbias_ref, dalog_ref, dd_ref, dh_scr):
        @pl.when(pl.program_id(1) == 0)
        def _():
            dh_scr[...] = jnp.zeros_like(dh_scr)
            dbias_ref[...] = jnp.zeros_like(dbias_ref)
            dalog_ref[...] = jnp.zeros_like(dalog_ref)
            dd_ref[...] = jnp.zeros_like(dd_ref)

        for gi in range(GS):
            one_group(gi, xbc_ref, dtr_ref, bias_ref, alog_ref, drep_ref, dy_ref, hs_ref,
                      dxbc_ref, ddtr_ref, dbias_ref, dalog_ref, dd_ref, dh_scr)

    def one_group(gi, xbc_ref, dtr_ref, bias_ref, alog_ref, drep_ref, dy_ref, hs_ref,
                  dxbc_ref, ddtr_ref, dbias_ref, dalog_ref, dd_ref, dh_scr):
        cols, rows, o = slice(gi * GW, (gi + 1) * GW), pl.ds(gi * GW, GW), gi * GXBC
        x, Bm, Cm = _xbc_parts(xbc_ref, gi)
        dY = dy_ref[:, cols]
        dt, A, a, cs, cs_last = _ssd_common(dtr_ref[gi], bias_ref[gi], alog_ref[gi])
        E = _head_cols(jnp.exp(cs))
        DT = _head_cols(dt)
        Wd = _head_cols(jnp.exp(cs_last - cs))
        X = x * DT
        h = hs_ref[0, rows, :]
        dS = dh_scr[rows, :]
        CB = _dot_nt(Cm, Bm)
        col = lax.broadcasted_iota(jnp.int32, (CHUNK, GW), 1) // HEADDIM
        rowid = lax.broadcasted_iota(jnp.int32, (8, CHUNK), 0)
        lane = lax.broadcasted_iota(jnp.int32, (8, CHUNK), 1)
        hsel = (lax.broadcasted_iota(jnp.int32, (8, GW), 1) // HEADDIM
                == lax.broadcasted_iota(jnp.int32, (8, GW), 0)).astype(F32)
        ones8 = jnp.ones((8, CHUNK), F32)

        dX = jnp.zeros((CHUNK, GW), F32)
        dCB = jnp.zeros((CHUNK, CHUNK), F32)
        dcs = jnp.zeros((8, CHUNK), F32)
        for r in range(HEADS_PER_GROUP):
            L = _decay_matrix(cs, r)
            M = CB * L
            G = _dot_nt(jnp.where(col == r, dY, 0.0), X)
            GL = G * L
            dCB = dCB + GL
            Wm = GL * CB
            colsum = jnp.sum(Wm, axis=0, keepdims=True)
            rowsum = _dot_nt_hi(ones8, Wm)
            dcs = dcs + jnp.where(rowid == r, rowsum - colsum, 0.0)
            dX = dX + jnp.where(col == r, _dot_tn(M, dY), 0.0)
        dC = _dot(dCB, Bm)
        dB = _dot_tn(dCB, Cm)
        T1 = _dot_nt(Bm, dS)
        dX = dX + T1 * Wd
        dB = dB + _dot(X * Wd, dS)
        pdec = _dot_nt_hi(hsel, X * T1 * Wd)
        dcs = dcs - pdec
        dlast = jnp.sum(pdec, axis=1, keepdims=True) \
            + jnp.exp(cs_last[:, 0:1]) * jnp.sum(_dot_hi(hsel, dS * h), axis=1, keepdims=True)
        dYE = dY * E
        dC = dC + _dot(dYE, h)
        yoff = _dot_nt(Cm, h) * E
        dcs = dcs + _dot_nt_hi(hsel, dY * yoff)
        dcs = dcs + jnp.where(lane == CHUNK - 1, dlast, 0.0)
        ki = lax.broadcasted_iota(jnp.int32, (CHUNK, CHUNK), 0)
        si = lax.broadcasted_iota(jnp.int32, (CHUNK, CHUNK), 1)
        lower = (ki >= si).astype(F32)
        da = _dot_hi(dcs, lower)
        ddt = da * A + _dot_nt_hi(hsel, dX * x)
        ddtr = ddt * _sigmoid(dtr_ref[gi] + bias_ref[gi])
        ddtr_ref[gi] = ddtr
        dbias_ref[gi] += ddtr
        dalog_ref[gi] += da * a
        dxbc_ref[:, o:o + GW] = dX * DT + drep_ref[:, cols] * dY
        dd_ref[:, cols] += jnp.sum(dY * x, axis=0, keepdims=True)
        dxbc_ref[:, o + GW:o + GW + N_STATE] = dB
        dxbc_ref[:, o + GW + N_STATE:o + GXBC] = dC
        dh_scr[rows, :] = dS * _head_rows(jnp.exp(cs_last)) + _dot_tn(dYE, Cm)

    return pl.pallas_call(
        body, name="ssd_bwd", grid=(N_GROUPS // GS, nc),
        in_specs=[xbc_s, dtr_s, row_s, row_s, drep_s, x_s, hs_s],
        out_specs=[xbc_s, dtr_s, row_s, row_s, drep_s],
        out_shape=[jax.ShapeDtypeStruct((T, D_XBC), F32),
                   jax.ShapeDtypeStruct((N_GROUPS, 8, T), F32),
                   jax.ShapeDtypeStruct((N_GROUPS, 8, CHUNK), F32),
                   jax.ShapeDtypeStruct((N_GROUPS, 8, CHUNK), F32),
                   jax.ShapeDtypeStruct((1, D_SSM), F32)],
        scratch_shapes=[pltpu.VMEM((GS * GW, N_STATE), F32)],
        compiler_params=_cparams(("parallel", "arbitrary")),
    )(xbc, dtr, bias, alog, drep, dy, hs)


def _adamw(w, g, m, v, name, deps=(), emit_g=False):
    R, C = w.shape
    tr = _tile(R, 256, 8)
    nd = len(deps)
    nout = 4 if emit_g else 3

    def body(w_ref, g_ref, m_ref, v_ref, *rest):
        outs = rest[nd:]
        gv = g_ref[...]
        mn = ADAM_B1 * m_ref[...] + (1.0 - ADAM_B1) * gv
        vn = ADAM_B2 * v_ref[...] + (1.0 - ADAM_B2) * (gv * gv)
        m_hat = mn / (1.0 - ADAM_B1 ** ADAM_STEP)
        v_hat = vn / (1.0 - ADAM_B2 ** ADAM_STEP)
        outs[0][...] = -ADAM_LR * (m_hat / (jnp.sqrt(v_hat) + ADAM_EPS) + ADAM_WD * w_ref[...])
        outs[1][...] = mn
        outs[2][...] = vn
        if emit_g:
            outs[3][...] = gv

    spec = pl.BlockSpec((tr, C), lambda i: (i, 0))
    return pl.pallas_call(
        body, name=name, grid=(R // tr,),
        in_specs=[spec] * 4 + [ANY] * nd, out_specs=[spec] * nout,
        out_shape=[jax.ShapeDtypeStruct((R, C), F32)] * nout,
        compiler_params=_cparams(("parallel",)),
    )(w, g, m, v, *deps)


ANY = pl.BlockSpec(memory_space=pl.ANY)


def _place():
    x, y, c = lax.axis_index("x"), lax.axis_index("y"), lax.axis_index("c")
    return x, y, c


def _other_chips(x, y):
    return [(1 - x, y), (x, 1 - y), (1 - x, 1 - y)]


def _allgather_inplace(bufs):
    n = len(bufs)

    def body(*refs):
        o_refs = refs[n:2 * n]
        send_sems, recv_sems = refs[2 * n:]
        x, y, c = _place()
        sibling = (x, y, 1 - c)
        chips = _other_chips(x, y)

        def copy(k, slot, px, py, pc, to):
            blk = o_refs[k].at[4 * px + 2 * py + pc]
            return pltpu.make_async_remote_copy(
                src_ref=blk, dst_ref=blk, send_sem=send_sems.at[k, slot], recv_sem=recv_sems.at[k, slot],
                device_id=to, device_id_type=MESH)

        sent = []
        for k in range(n):
            for j, (px, py) in enumerate(chips):
                cp = copy(k, j, x, y, c, (px, py, c))
                cp.start()
                sent.append(cp)
        for k in range(n):
            for j, (px, py) in enumerate(chips):
                copy(k, j, px, py, c, (px, py, c)).wait_recv()
                fwd = copy(k, 3 + j, px, py, c, sibling)
                fwd.start()
                sent.append(fwd)
        for k in range(n):
            for j, (px, py) in enumerate(chips):
                copy(k, 3 + j, px, py, 1 - c, sibling).wait_recv()
        for cp in sent:
            cp.wait_send()

    return pl.pallas_call(
        body, name="allgather_w_in",
        in_specs=[ANY] * n, out_specs=[ANY] * n,
        out_shape=[jax.ShapeDtypeStruct(b.shape, b.dtype) for b in bufs],
        input_output_aliases={k: k for k in range(n)},
        scratch_shapes=[pltpu.SemaphoreType.DMA((n, 6)), pltpu.SemaphoreType.DMA((n, 6))],
    )(*bufs)


HBM = pl.BlockSpec(memory_space=pltpu.HBM)
SEM = pl.BlockSpec(memory_space=pltpu.SEMAPHORE)
EFFECT = pltpu.SideEffectType.DATAFLOW_SIDE_EFFECTING


def _split_start(name, arrays, build, n_copies, after=()):
    na, nd = len(arrays), len(after)

    def body(*refs):
        send_sems, recv_sems = refs[na + nd], refs[na + nd + 1]
        for cp in build(refs[:na], send_sems, recv_sems):
            cp.start()
        refs[-1][...] = jnp.zeros((8, 128), F32)

    outs = pl.pallas_call(
        body, name=name,
        out_shape=(pltpu.SemaphoreType.DMA((n_copies,)), pltpu.SemaphoreType.DMA((n_copies,)),
                   *[pltpu.HBM(a.shape, a.dtype) for a in arrays], jax.ShapeDtypeStruct((8, 128), F32)),
        in_specs=[HBM] * na + [ANY] * nd,
        out_specs=(SEM, SEM, *[HBM] * na, pl.BlockSpec(memory_space=pltpu.VMEM)),
        input_output_aliases={i: 2 + i for i in range(na)},
        compiler_params=pltpu.CompilerParams(has_side_effects=EFFECT),
    )(*[pltpu.with_memory_space_constraint(a, pltpu.HBM) for a in arrays], *after)
    return outs[0], outs[1], list(outs[2:2 + na]), outs[-1]


def _split_wait(name, send_sems, recv_sems, arrays, build, after):
    na = len(arrays)

    def body(*refs):
        for cp in build(refs[:na], refs[na], refs[na + 1]):
            cp.wait_send()
            cp.wait_recv()

    outs = pl.pallas_call(
        body, name=name,
        out_shape=tuple(pltpu.HBM(a.shape, a.dtype) for a in arrays),
        in_specs=[HBM] * na + [SEM, SEM] + [ANY] * len(after),
        out_specs=tuple([HBM] * na),
        input_output_aliases={i: i for i in range(na)},
        compiler_params=pltpu.CompilerParams(has_side_effects=EFFECT),
    )(*arrays, send_sems, recv_sems, *after)
    return list(outs)


def _remote(src, dst, send_sems, recv_sems, i, to):
    return pltpu.make_async_remote_copy(src_ref=src, dst_ref=dst, send_sem=send_sems.at[i], recv_sem=recv_sems.at[i],
                                        device_id=to, device_id_type=MESH)


def _build_ag_ici(refs, ss, rs):
    x, y, c = _place()
    cps = []
    for k, ref in enumerate(refs):
        blk = ref.at[4 * x + 2 * y + c]
        for j, (px, py) in enumerate(_other_chips(x, y)):
            cps.append(_remote(blk, blk, ss, rs, 3 * k + j, (px, py, c)))
    return cps


def _build_ag_fwd(refs, ss, rs):
    x, y, c = _place()
    cps = []
    for k, ref in enumerate(refs):
        for j, (px, py) in enumerate(_other_chips(x, y)):
            blk = ref.at[4 * px + 2 * py + c]
            cps.append(_remote(blk, blk, ss, rs, 3 * k + j, (x, y, 1 - c)))
    return cps


def _build_rs_swap(refs, ss, rs):
    x, y, c = _place()
    n = len(refs) // 2
    return [_remote(refs[k].at[:, pl.ds(1 - c, 1)], refs[n + k], ss, rs, k, (x, y, 1 - c)) for k in range(n)]


def _build_rs_ici(refs, ss, rs):
    x, y, c = _place()
    n = len(refs) // 2
    me = 2 * x + y
    cps = []
    for k in range(n):
        for j, (px, py) in enumerate(_other_chips(x, y)):
            cps.append(_remote(refs[k].at[2 * px + py], refs[n + k].at[me], ss, rs, 3 * k + j, (px, py, c)))
    return cps


def _build_rs_share(refs, ss, rs):
    x, y, c = _place()
    return [_remote(ref.at[c], ref.at[c], ss, rs, k, (x, y, 1 - c)) for k, ref in enumerate(refs)]


def _allreduce_small(p, deps=()):
    R, C = p.shape
    nd = len(deps)

    def body(p_ref, *rest):
        gath_ref, sum_ref, send_sems, recv_sems, local_sem = rest[nd:]
        x, y, c = _place()
        me, sibling = (x, y, c), (x, y, 1 - c)
        chips = [(1 - x, y), (x, 1 - y), (1 - x, 1 - y)]

        def blk(px, py, pc):
            return gath_ref.at[4 * px + 2 * py + pc]

        def copy(k, block, to, src=None):
            return pltpu.make_async_remote_copy(
                src_ref=blk(*block) if src is None else src, dst_ref=blk(*block),
                send_sem=send_sems.at[k], recv_sem=recv_sems.at[k], device_id=to, device_id_type=MESH)

        mine = pltpu.make_async_copy(p_ref, blk(*me), local_sem)
        mine.start()
        first = [copy(0, me, sibling, src=p_ref)]
        first += [copy(1 + j, me, (*chip, c), src=p_ref) for j, chip in enumerate(chips)]
        for cp in first:
            cp.start()
        passed = [copy(4 + j, (*chip, c), sibling) for j, chip in enumerate(chips)]
        for j, chip in enumerate(chips):
            copy(1 + j, (*chip, c), me).wait_recv()
            passed[j].start()
        copy(0, sibling, me).wait_recv()
        for j, chip in enumerate(chips):
            copy(4 + j, (*chip, 1 - c), me).wait_recv()
        for cp in first + passed:
            cp.wait_send()
        mine.wait()
        s = gath_ref[0]
        for d in range(1, N_DEV):
            s = s + gath_ref[d]
        sum_ref[...] = s

    vm = pl.BlockSpec(memory_space=pltpu.VMEM)
    return pl.pallas_call(
        body, name="allreduce_small",
        in_specs=[vm] + [ANY] * nd, out_specs=[vm, vm],
        out_shape=[jax.ShapeDtypeStruct((N_DEV, R, C), F32), jax.ShapeDtypeStruct((R, C), F32)],
        scratch_shapes=[pltpu.SemaphoreType.DMA((7,)), pltpu.SemaphoreType.DMA((7,)), pltpu.SemaphoreType.DMA],
    )(p, *deps)[1]


def _rs_add_pair(p, r0, c_arr, name):
    _, _, hr, cols = p.shape
    tr = _tile(hr, 256, 8)

    def body(c_ref, p_ref, r_ref, q_ref):
        q_ref[...] = (p_ref[0] + r_ref[0]).astype(BF16)

    grid_spec = pltpu.PrefetchScalarGridSpec(
        num_scalar_prefetch=1, grid=(N_CHIPS, hr // tr),
        in_specs=[pl.BlockSpec((1, 1, tr, cols), lambda j, i, c_ref: (j, c_ref[0], i, 0)),
                  pl.BlockSpec((1, 1, tr, cols), lambda j, i, c_ref: (j, 0, i, 0))],
        out_specs=pl.BlockSpec((1, tr, cols), lambda j, i, c_ref: (j, i, 0)))
    return pl.pallas_call(
        body, name=name, grid_spec=grid_spec,
        out_shape=jax.ShapeDtypeStruct((N_CHIPS, hr, cols), BF16),
        compiler_params=_cparams(("parallel", "parallel")),
    )(c_arr, p, r0)


def _rs_add_chips(r1, q, place_arr, name):
    _, hr, cols = r1.shape
    tr = _tile(hr, 256, 8)

    def body(place_ref, r_ref, q_ref, o_ref):
        chip = place_ref[0]
        s = None
        for j in range(N_CHIPS):
            t = jnp.where(chip == j, q_ref[j], r_ref[j]).astype(F32)
            s = t if s is None else s + t
        o_ref[...] = s

    blk = pl.BlockSpec((N_CHIPS, tr, cols), lambda i, place_ref: (0, i, 0))
    grid_spec = pltpu.PrefetchScalarGridSpec(
        num_scalar_prefetch=1, grid=(hr // tr,), in_specs=[blk, blk],
        out_specs=pl.BlockSpec((None, tr, cols), lambda i, place_ref: (place_ref[1], i, 0)))
    return pl.pallas_call(
        body, name=name, grid_spec=grid_spec,
        out_shape=jax.ShapeDtypeStruct((2, hr, cols), F32),
        compiler_params=_cparams(("parallel",)),
    )(place_arr, r1, q)


def _pad_rows(a, rows):
    return jnp.pad(a, ((0, rows - a.shape[0]), (0, 0)))


def _pad_cols(a, cols):
    return jnp.pad(a, ((0, 0), (0, cols - a.shape[1])))


def _heads_to_rows(v):
    v = v.reshape(N_GROUPS, HEADS_PER_GROUP, 1)
    v = jnp.pad(v, ((0, 0), (0, 8 - HEADS_PER_GROUP), (0, 0)))
    return jnp.broadcast_to(v, (N_GROUPS, 8, CHUNK))


def _rows_to_heads(a):
    return jnp.sum(a[:, :HEADS_PER_GROUP, :], axis=-1).reshape(N_HEADS)


def _to_kernel_rows(a):
    C = a.shape[1]
    x0, b0, c0, s0 = D_SSM, 2 * D_SSM, 2 * D_SSM + 1024, D_SSM + D_XBC + N_HEADS
    xbc = jnp.concatenate([a[x0:b0].reshape(N_GROUPS, GW, C), a[b0:c0].reshape(N_GROUPS, N_STATE, C),
                           a[c0:c0 + 1024].reshape(N_GROUPS, N_STATE, C)], axis=1).reshape(D_XBC, C)
    sc = jnp.concatenate([a[s0 + k * D_MODEL:s0 + (k + 1) * D_MODEL].reshape(D_MODEL // SCB, SCB, C)
                          for k in range(3)], axis=1).reshape(3 * D_MODEL, C)
    return jnp.concatenate([a[:D_SSM], xbc, sc], axis=0)


HR_IN = 1568


def _shard_row_plan():
    segs = [(0, 0, 0, D_SSM)]
    for g in range(N_GROUPS):
        k0 = D_SSM + g * GXBC
        segs += [(0, k0, D_SSM + g * GW, GW), (0, k0 + GW, 2 * D_SSM + g * N_STATE, N_STATE),
                 (0, k0 + GW + N_STATE, 2 * D_SSM + 1024 + g * N_STATE, N_STATE)]
    segs.append((1, 0, D_SSM + D_XBC, N_HEADS))
    for j in range(D_MODEL // SCB):
        for k in range(3):
            segs.append((0, D_SSM + D_XBC + j * SC3 + k * SCB, D_SSM + D_XBC + N_HEADS + k * D_MODEL + j * SCB, SCB))
    cs = D_IN // N_CHIPS
    plan = []
    for src, s, o, n in segs:
        while n > 0:
            chip, loc = divmod(o, cs)
            half, row = divmod(loc, HR_IN)
            m = min(n, cs - loc, HR_IN - row)
            plan.append((src, s, chip, half, row, m))
            s, o, n = s + m, o + m, n - m
    return plan


def _scatter_rows_to_shards(k_main, k_dt):
    C = k_main.shape[1]
    plan = _shard_row_plan()

    def body(m_ref, d_ref, o_ref, sems):
        cps = []
        for i, (src, s, chip, half, row, n) in enumerate(plan):
            cp = pltpu.make_async_copy((d_ref if src else m_ref).at[pl.ds(s, n)],
                                       o_ref.at[chip, half, pl.ds(row, n)], sems.at[i])
            cp.start()
            cps.append(cp)
        for cp in cps:
            cp.wait()

    return pl.pallas_call(
        body, name="scatter_dw_in_rows", in_specs=[ANY, ANY], out_specs=ANY,
        out_shape=jax.ShapeDtypeStruct((N_CHIPS, 2, HR_IN, C), k_main.dtype),
        scratch_shapes=[pltpu.SemaphoreType.DMA((len(plan),))],
    )(k_main, k_dt)


def _to_kernel_xbc(a):
    R = a.shape[0]
    return jnp.concatenate([a[:, :D_SSM].reshape(R, N_GROUPS, GW), a[:, D_SSM:D_SSM + 1024].reshape(R, N_GROUPS, N_STATE),
                            a[:, D_SSM + 1024:].reshape(R, N_GROUPS, N_STATE)], axis=2).reshape(R, D_XBC)


def _from_kernel_xbc(a):
    R = a.shape[0]
    g = a.reshape(R, N_GROUPS, GXBC)
    return jnp.concatenate([g[:, :, :GW].reshape(R, D_SSM), g[:, :, GW:GW + N_STATE].reshape(R, 1024),
                            g[:, :, GW + N_STATE:].reshape(R, 1024)], axis=1)


def kernel(x, norm_mix_g, w_in, ssm_conv_w, ssm_conv_b, ssm_dt_bias, ssm_A_log, ssm_D, ssm_norm_g, sc_conv_w, w_out, norm_ffn_g, w_gate, w_up, w_down, norm_final_g, loss_target, m_norm_mix_g, m_w_in, m_ssm_conv_w, m_ssm_conv_b, m_ssm_dt_bias, m_ssm_A_log, m_ssm_D, m_ssm_norm_g, m_sc_conv_w, m_w_out, m_norm_ffn_g, m_w_gate, m_w_up, m_w_down, m_norm_final_g, v_norm_mix_g, v_w_in, v_ssm_conv_w, v_ssm_conv_b, v_ssm_dt_bias, v_ssm_A_log, v_ssm_D, v_ssm_norm_g, v_sc_conv_w, v_w_out, v_norm_ffn_g, v_w_gate, v_w_up, v_w_down, v_norm_final_g):
    T = x.shape[1]
    xt = x[0]
    tgt = loss_target[0]
    cx, cy, cc = lax.axis_index("x"), lax.axis_index("y"), lax.axis_index("c")
    chip = 2 * cx + cy
    c_arr = jnp.reshape(cc, (1,)).astype(jnp.int32)
    chip_arr = jnp.reshape(chip, (1,)).astype(jnp.int32)
    place_arr = jnp.stack([chip, cc]).astype(jnp.int32)

    big = [w_in[0].T, w_out[0], w_gate[0], w_up[0], w_down[0]]
    names = ["w_in", "w_out", "w_gate", "w_up", "w_down"]
    gbufs = [_cast_into_gather(w, chip_arr, "cast_" + nm, split_cols=(nm == "w_in")) for w, nm in zip(big, names)]
    (g_in,) = _allgather_inplace([gbufs[0]])
    ag_ss, ag_rs, ag_bufs, ag_tok = _split_start("ag_ici_start", gbufs[1:], _build_ag_ici, 12, after=[g_in])
    cs_in = D_IN // N_CHIPS
    wt = g_in.reshape(N_CHIPS, 2, cs_in, D_MODEL // 2).transpose(0, 2, 1, 3).reshape(D_IN, D_MODEL)
    wt_main = _to_kernel_rows(wt)
    wt_dt = _pad_rows(wt[D_SSM + D_XBC:D_SSM + D_XBC + N_HEADS], DT_PAD)

    contrib = (cc == 0).astype(F32)
    place_ssm = jnp.zeros((8, D_XBC), F32)
    place_ssm = lax.dynamic_update_slice(place_ssm, _pad_rows(ssm_conv_w[0], 8) * contrib, (0, chip * (D_XBC // N_CHIPS)))
    place_sc = jnp.zeros((8, D_MODEL), F32)
    place_sc = lax.dynamic_update_slice(place_sc, _pad_rows(sc_conv_w[0], 8) * contrib, (0, chip * (D_MODEL // N_CHIPS)))
    convs = _allreduce_small(jnp.concatenate([place_ssm, _pad_cols(place_sc, D_XBC)], axis=0))
    ssm_w8 = _to_kernel_xbc(convs[:8])
    ssm_bk = _to_kernel_xbc(ssm_conv_b)
    sc_w8 = convs[8:, :D_MODEL]

    bias_rows = _heads_to_rows(ssm_dt_bias[0])
    alog_rows = _heads_to_rows(ssm_A_log[0])
    drep = jnp.repeat(ssm_D[0], HEADDIM).reshape(1, D_SSM)

    n1 = _rmsnorm_fwd(xt, _tie(norm_mix_g, ag_tok, "tie_ag_ici"), "rmsnorm_mix")
    (proj,) = _matmul([(n1, wt_main)], tb=True, out_dtypes=[F32], name="mm_proj")
    (dt_raw,) = _matmul([(n1, wt_dt)], tb=True, out_dtypes=[F32], name="mm_proj_dt")
    ag_bufs = _split_wait("ag_ici_wait", ag_ss, ag_rs, ag_bufs, _build_ag_ici, after=[dt_raw])
    fw_ss, fw_rs, fw_bufs, fw_tok = _split_start("ag_fwd_start", ag_bufs, _build_ag_fwd, 12)
    xbc = _ssm_conv_fwd(proj, ssm_w8, _tie(ssm_bk, fw_tok, "tie_ag_fwd"))
    dtr = jnp.pad(dt_raw[:, :N_HEADS].T.reshape(N_GROUPS, HEADS_PER_GROUP, T), ((0, 0), (0, 4), (0, 0)))
    y_ssd, hs = _ssd_fwd(xbc, dtr, bias_rows, alog_rows, drep)
    y_mix = _shortconv_fwd(proj, sc_w8, _gated_norm_fwd(y_ssd, proj, ssm_norm_g))
    gath = _split_wait("ag_fwd_wait", fw_ss, fw_rs, fw_bufs, _build_ag_fwd, after=[y_mix])
    w_out_f = gath[0].reshape(2 * D_MODEL, D_MODEL)
    w_gate3 = gath[1].reshape(N_CHIPS, D_MODEL, D_FF // N_CHIPS)
    w_up3 = gath[2].reshape(N_CHIPS, D_MODEL, D_FF // N_CHIPS)
    w_down_f = gath[3].reshape(D_FF, D_MODEL)
    (h1,) = _matmul([(y_mix, w_out_f)], out_dtypes=[F32], name="mm_out", extras=[xt],
                    epilogue=lambda acc, res: (acc + res,))
    n2 = _rmsnorm_fwd(h1, norm_ffn_g, "rmsnorm_ffn")
    g_act, u_act, a_act = _ffn_fwd(n2, w_gate3, w_up3)
    (h2,) = _matmul([(a_act, w_down_f)], out_dtypes=[F32], name="mm_down", extras=[h1],
                    epilogue=lambda acc, res: (acc + res,))

    dh2, dh2b, dg_final, loss_part = _loss_and_final_bwd(h2, tgt, norm_final_g.reshape(1, D_MODEL))
    dg_act, du_act = _matmul([(dh2b, w_down_f)], tb=True, out_dtypes=[BF16, BF16], name="mm_down_bwd",
                             tn=512, extras=[g_act, u_act], epilogue=_swiglu_bwd, nsub=2)
    (dw_down,) = _matmul([(a_act, dh2b)], ta=True, out_dtypes=[F32], name="mm_dw_down", tm=1408)
    (dn2,) = _matmul([(dg_act, w_gate3), (du_act, w_up3)], tb=True, b3d=True, out_dtypes=[BF16],
                     name="mm_ffn_in_bwd")
    (dw_gate,) = _matmul([(n2, dg_act)], ta=True, out_dtypes=[F32], name="mm_dw_gate", tn=1408, col_shards=True)
    (dw_up,) = _matmul([(n2, du_act)], ta=True, out_dtypes=[F32], name="mm_dw_up", tn=1408, col_shards=True)
    dh1, dh1b, dg_ffn = _rmsnorm_bwd(dn2, h1, norm_ffn_g, dh2, "rmsnorm_ffn_bwd")
    (dw_out,) = _matmul([(y_mix, dh1b)], ta=True, out_dtypes=[F32], name="mm_dw_out")

    def halves(g):
        return g.reshape(N_CHIPS, 2, g.shape[1] // 2, g.shape[2])

    def landing(shape, dtype):
        return lax.empty(shape, dtype)

    names1 = names[1:]
    ps1 = [halves(dw_out.reshape(N_CHIPS, -1, D_MODEL)), halves(dw_gate), halves(dw_up),
           halves(dw_down.reshape(N_CHIPS, -1, D_MODEL))]
    r0_1 = [landing((N_CHIPS, 1) + p.shape[2:], F32) for p in ps1]
    sw_ss, sw_rs, sw_arr, sw_tok = _split_start("rs1_swap_start", ps1 + r0_1, _build_rs_swap, 4)
    (dmix,) = _matmul([(dh1b, w_out_f)], tb=True, out_dtypes=[BF16], name="mm_out_bwd", deps=[sw_tok])
    dproj, dw_sc = _shortconv_bwd(dmix, proj, sc_w8)
    dy_ssd, dproj, dg_ssmnorm = _gated_norm_bwd(dmix, y_ssd, proj, ssm_norm_g, dproj)
    sw_arr = _split_wait("rs1_swap_wait", sw_ss, sw_rs, sw_arr, _build_rs_swap, after=[dy_ssd])
    qs1 = [_rs_add_pair(p, r, c_arr, "rs_add_pair_" + nm) for p, r, nm in zip(sw_arr[:4], sw_arr[4:], names1)]
    r1_1 = [landing(q.shape, BF16) for q in qs1]
    ic_ss, ic_rs, ic_arr, ic_tok = _split_start("rs1_ici_start", qs1 + r1_1, _build_rs_ici, 12)
    dxbc_act, ddtr, dbias_acc, dalog_acc, dD_acc = _ssd_bwd(
        xbc, dtr, bias_rows, alog_rows, _tie(drep, ic_tok, "tie_rs1_ici"), dy_ssd, hs)
    dproj, dw_ssmconv, db_ssmconv = _ssm_conv_bwd(dxbc_act, proj, ssm_w8, ssm_bk, dproj)
    dw_ssmconv, db_ssmconv = _from_kernel_xbc(dw_ssmconv), _from_kernel_xbc(db_ssmconv)
    ic_arr = _split_wait("rs1_ici_wait", ic_ss, ic_rs, ic_arr, _build_rs_ici, after=[dproj])
    g1 = [_rs_add_chips(r, q, place_arr, "rs_add_chips_" + nm) for q, r, nm in zip(ic_arr[:4], ic_arr[4:], names1)]
    sh_ss, sh_rs, sh_arr, sh_tok = _split_start("rs1_share_start", g1, _build_rs_share, 4)

    ddt_raw = _pad_cols(ddtr[:, :HEADS_PER_GROUP, :].reshape(N_HEADS, T).T, DT_PAD).astype(BF16)
    (dwt_main,) = _matmul([(dproj, n1)], ta=True, out_dtypes=[F32], name="mm_dw_main", deps=[sh_tok])
    (dwt_dt,) = _matmul([(ddt_raw, n1)], ta=True, out_dtypes=[F32], name="mm_dw_dt")
    p_in = _scatter_rows_to_shards(dwt_main, dwt_dt)
    s2_ss, s2_rs, s2_arr, s2_tok = _split_start(
        "rs2_swap_start", [p_in, landing((N_CHIPS, 1) + p_in.shape[2:], F32)], _build_rs_swap, 1)
    (dn1a,) = _matmul([(dproj, wt_main)], out_dtypes=[F32], name="mm_proj_bwd", deps=[s2_tok])
    g1 = _split_wait("rs1_share_wait", sh_ss, sh_rs, sh_arr, _build_rs_share, after=[dn1a])
    s2_arr = _split_wait("rs2_swap_wait", s2_ss, s2_rs, s2_arr, _build_rs_swap, after=[dn1a])
    q_in = _rs_add_pair(s2_arr[0], s2_arr[1], c_arr, "rs_add_pair_w_in")
    i2_ss, i2_rs, i2_arr, i2_tok = _split_start(
        "rs2_ici_start", [q_in, landing(q_in.shape, BF16)], _build_rs_ici, 3)
    (dn1,) = _matmul([(ddt_raw, wt_dt)], out_dtypes=[BF16], name="mm_proj_dt_bwd", extras=[dn1a],
                     epilogue=lambda acc, res: (acc + res,), deps=[i2_tok])
    dx, _, dg_mix = _rmsnorm_bwd(dn1, xt, norm_mix_g, dh1, "rmsnorm_mix_bwd")

    big_m = [m_w_in[0].T, m_w_out[0], m_w_gate[0], m_w_up[0], m_w_down[0]]
    big_v = [v_w_in[0].T, v_w_out[0], v_w_gate[0], v_w_up[0], v_w_down[0]]
    big_grads = [None] + [g.reshape(w.shape) for g, w in zip(g1, big[1:])]
    big_out = {}
    for k in range(1, 5):
        big_out[names[k]] = _adamw(big[k], big_grads[k], big_m[k], big_v[k], "adamw_" + names[k], deps=[i2_tok])
    i2_arr = _split_wait("rs2_ici_wait", i2_ss, i2_rs, i2_arr, _build_rs_ici, after=[big_out[names[4]][0], dx])
    g_in_red = _rs_add_chips(i2_arr[1], i2_arr[0], place_arr, "rs_add_chips_w_in")
    s3_ss, s3_rs, s3_arr, s3_tok = _split_start("rs2_share_start", [g_in_red], _build_rs_share, 1)

    dD = jnp.sum(dD_acc.reshape(N_HEADS, HEADDIM), axis=-1)
    heads_row = jnp.concatenate([_rows_to_heads(dbias_acc), _rows_to_heads(dalog_acc), dD,
                                 loss_part.reshape(1)]).reshape(1, -1)
    small = jnp.concatenate([
        dw_ssmconv,
        _pad_cols(dw_sc, D_XBC),
        db_ssmconv,
        jnp.concatenate([dg_mix, dg_ssmnorm], axis=1),
        jnp.concatenate([dg_ffn, dg_final], axis=1),
        _pad_cols(heads_row, D_XBC),
        jnp.zeros((4, D_XBC), F32),
    ], axis=0)
    tot = _allreduce_small(small, deps=[s3_tok])
    loss = tot[19, 3 * N_HEADS]

    cs_ssm, cs_sc = D_XBC // N_CHIPS, D_MODEL // N_CHIPS
    g_ssm_conv = lax.dynamic_slice(tot[0:K_SSM], (0, chip * cs_ssm), (K_SSM, cs_ssm))
    g_sc_conv = lax.dynamic_slice(tot[8:8 + K_SC, :D_MODEL], (0, chip * cs_sc), (K_SC, cs_sc))
    small_grads = {
        "norm_mix_g": tot[17:18, :D_MODEL], "ssm_conv_w": g_ssm_conv, "ssm_conv_b": tot[16:17],
        "ssm_dt_bias": tot[19:20, 0:N_HEADS], "ssm_A_log": tot[19:20, N_HEADS:2 * N_HEADS],
        "ssm_D": tot[19:20, 2 * N_HEADS:3 * N_HEADS], "ssm_norm_g": tot[17:18, D_MODEL:],
        "sc_conv_w": g_sc_conv, "norm_ffn_g": tot[18:19, :D_MODEL], "norm_final_g": tot[18:19, D_MODEL:],
    }
    small_w = {"norm_mix_g": (norm_mix_g, m_norm_mix_g, v_norm_mix_g),
               "ssm_conv_w": (ssm_conv_w[0], m_ssm_conv_w[0], v_ssm_conv_w[0]),
               "ssm_conv_b": (ssm_conv_b, m_ssm_conv_b, v_ssm_conv_b),
               "ssm_dt_bias": (ssm_dt_bias, m_ssm_dt_bias, v_ssm_dt_bias),
               "ssm_A_log": (ssm_A_log, m_ssm_A_log, v_ssm_A_log),
               "ssm_D": (ssm_D, m_ssm_D, v_ssm_D),
               "ssm_norm_g": (ssm_norm_g, m_ssm_norm_g, v_ssm_norm_g),
               "sc_conv_w": (sc_conv_w[0], m_sc_conv_w[0], v_sc_conv_w[0]),
               "norm_ffn_g": (norm_ffn_g, m_norm_ffn_g, v_norm_ffn_g),
               "norm_final_g": (norm_final_g.reshape(1, -1), m_norm_final_g.reshape(1, -1),
                                v_norm_final_g.reshape(1, -1))}
    PW = 1024
    order = list(small_w)

    def pack(arrs):
        rows = []
        for a in arrs:
            flat = a.reshape(-1)
            n = -(-flat.shape[0] // PW) * PW
            rows.append(jnp.pad(flat, (0, n - flat.shape[0])).reshape(-1, PW))
        slab = jnp.concatenate(rows, axis=0)
        return _pad_rows(slab, -(-slab.shape[0] // 8) * 8)

    wp = pack([small_w[k][0] for k in order])
    mp = pack([small_w[k][1] for k in order])
    vp = pack([small_w[k][2] for k in order])
    gp = pack([small_grads[k] for k in order])
    sd, sm, sv = _adamw(wp, gp, mp, vp, "adamw_small")

    def unpack(slab):
        out, row = {}, 0
        for k in order:
            shape = small_w[k][0].shape
            size = 1
            for s in shape:
                size *= s
            nr = -(-size // PW)
            out[k] = slab[row:row + nr].reshape(-1)[:size].reshape(shape)
            row += nr
        return out

    s_delta, s_m, s_v = unpack(sd), unpack(sm), unpack(sv)

    (g_in_full,) = _split_wait("rs2_share_wait", s3_ss, s3_rs, s3_arr, _build_rs_share, after=[sd])
    d_t, m_t, v_t, g_t = _adamw(big[0], g_in_full.reshape(2 * HR_IN, D_MODEL), big_m[0], big_v[0],
                                "adamw_" + names[0], emit_g=True)
    big_grads[0] = g_t.T
    big_out[names[0]] = (d_t.T, m_t.T, v_t.T)
    big_g = dict(zip(names, big_grads))

    weight_order = ["norm_mix_g", "w_in", "ssm_conv_w", "ssm_conv_b", "ssm_dt_bias", "ssm_A_log", "ssm_D",
                    "ssm_norm_g", "sc_conv_w", "w_out", "norm_ffn_g", "w_gate", "w_up", "w_down", "norm_final_g"]
    lead = {"ssm_conv_w", "sc_conv_w", "w_in", "w_out", "w_gate", "w_up", "w_down"}

    def shaped(nm, a):
        if nm == "norm_final_g":
            return a.reshape(D_MODEL)
        return a[None] if nm in lead else a

    grads, deltas, new_m, new_v = [], [], [], []
    for nm in weight_order:
        if nm in big_out:
            g, (d, m, v) = big_g[nm], big_out[nm]
        else:
            g, d, m, v = small_grads[nm], s_delta[nm], s_m[nm], s_v[nm]
        grads.append(shaped(nm, g))
        deltas.append(shaped(nm, d))
        new_m.append(shaped(nm, m))
        new_v.append(shaped(nm, v))
    return (loss, dx[None], *grads, *deltas, *new_m, *new_v)


def _swiglu_bwd(da, g, u):
    gf, uf = g.astype(F32), u.astype(F32)
    return da * uf * _dsilu(gf), da * _silu(gf)


def _ffn_fwd(n2, w_gate, w_up):
    T, K = n2.shape
    tn = w_gate.shape[2]
    N = N_CHIPS * tn
    tm = _tile(T, 512)
    sub = _tile(tm, 256)

    def body(a_ref, wg_ref, wu_ref, g_ref, u_ref, act_ref):
        for s in range(tm // sub):
            rows = pl.ds(s * sub, sub)
            a = a_ref[rows, :]
            g = jnp.dot(a, wg_ref[...], preferred_element_type=F32)
            u = jnp.dot(a, wu_ref[...], preferred_element_type=F32)
            g_ref[rows, :] = g.astype(BF16)
            u_ref[rows, :] = u.astype(BF16)
            act_ref[rows, :] = (_silu(g) * u).astype(BF16)

    a_spec = pl.BlockSpec((tm, K), lambda j, i: (i, 0))
    b_spec = pl.BlockSpec((None, K, tn), lambda j, i: (j, 0, 0))
    o_spec = pl.BlockSpec((tm, tn), lambda j, i: (i, j))
    return pl.pallas_call(
        body, name="ffn_fwd", grid=(N // tn, T // tm),
        in_specs=[a_spec, b_spec, b_spec], out_specs=[o_spec] * 3,
        out_shape=[jax.ShapeDtypeStruct((T, N), BF16)] * 3,
        compiler_params=_cparams(("parallel", "parallel")),
    )(n2, w_gate, w_up)
```

```python
import functools

import jax
import jax.numpy as jnp
from jax import lax
from jax.experimental import pallas as pl
from jax.experimental.pallas import tpu as pltpu

F32 = jnp.float32
BF16 = jnp.bfloat16
MESH = pl.DeviceIdType.MESH

D_MODEL = 2048
D_SSM = 2048
HEADDIM = 64
N_HEADS = 32
N_GROUPS = 8
HEADS_PER_GROUP = 4
N_STATE = 128
CHUNK = 128
K_SSM = 4
K_SC = 3
D_XBC = 4096
D_FF = 5632
D_IN = 12320
D_MAIN = 12288
OFF_XBC, OFF_CB, OFF_CC, OFF_CX = 2048, 6144, 8192, 10240
DT_PAD = 128
EPS = 1e-5
N_CHIPS = 4
N_DEV = 8

ADAM_LR = 0.001
ADAM_B1 = 0.9
ADAM_B2 = 0.999
ADAM_EPS = 1e-08
ADAM_WD = 0.01
ADAM_STEP = 10

V7X_VMEM_BYTES = 64 * 1024 * 1024
VMEM_LIMIT = V7X_VMEM_BYTES - 8 * 1024 * 1024


def _cparams(sem=None):
    if sem is None:
        return pltpu.CompilerParams(vmem_limit_bytes=VMEM_LIMIT)
    return pltpu.CompilerParams(dimension_semantics=sem, vmem_limit_bytes=VMEM_LIMIT)


def _tile(dim, pref, unit=128):
    best = None
    t = unit
    while t <= min(dim, pref):
        if dim % t == 0:
            best = t
        t += unit
    return best if best is not None else dim


def _sigmoid(x):
    return 1.0 / (1.0 + jnp.exp(-x))


def _silu(x):
    return x * _sigmoid(x)


def _dsilu(x):
    s = _sigmoid(x)
    return s * (1.0 + x * (1.0 - s))


def _softplus(x):
    return jnp.maximum(x, 0.0) + jnp.log(1.0 + jnp.exp(-jnp.abs(x)))


MATMUL_VMEM_BUDGET = 44 * 1024 * 1024


def _matmul(pairs, *, ta=False, tb=False, out_dtypes, name, tm=1024, tn=1024, tk=None, extras=(), epilogue=None,
            deps=(), col_shards=False, nsub=1, b3d=False):
    a0, b0 = pairs[0]
    M, K = (a0.shape[1], a0.shape[0]) if ta else a0.shape
    if b3d:
        N = b0.shape[1] if tb else b0.shape[0] * b0.shape[2]
        tk, tn = (b0.shape[2], tn) if tb else (tk, b0.shape[2])
    else:
        N = b0.shape[0] if tb else b0.shape[1]
    tm, tn = _tile(M, tm, 8 if M % 128 else 128), _tile(N, tn)
    npair, nex, ndep, nout = len(pairs), len(extras), len(deps), len(out_dtypes)
    if tk is None:
        fixed = 2 * tm * tn * (sum(jnp.dtype(d).itemsize for d in out_dtypes) + sum(e.dtype.itemsize for e in extras))
        tk = K
        while tk > 128 and (K % tk or tk % 128 or
                            fixed + 2 * npair * 2 * tk * (tm + tn) + (tm * tn * 4 if tk < K else 0) > MATMUL_VMEM_BUDGET):
            tk -= 128
    else:
        tk = _tile(K, tk)
    nk = K // tk
    if nk > 1 or tm % nsub or (tm // nsub) % 128:
        nsub = 1
    sub = tm // nsub
    dims = (((0 if ta else 1,), (1 if tb else 0,)), ((), ()))

    def body(*refs):
        a_refs = refs[0:2 * npair:2]
        b_refs = refs[1:2 * npair:2]
        ex_refs = refs[2 * npair:2 * npair + nex]
        o_refs = refs[2 * npair + nex + ndep:2 * npair + nex + ndep + nout]

        def dots(rows):
            s = None
            for a_ref, b_ref in zip(a_refs, b_refs):
                a = a_ref[...] if rows is None else (a_ref[:, rows] if ta else a_ref[rows, :])
                d = lax.dot_general(a, b_ref[...], dims, preferred_element_type=F32)
                s = d if s is None else s + d
            return s

        def finish(r, rows):
            ex = [e[...] if rows is None else e[rows, :] for e in ex_refs]
            outs = (r,) if epilogue is None else epilogue(r, *ex)
            for o_ref, o in zip(o_refs, outs):
                if rows is None:
                    o_ref[...] = o.astype(o_ref.dtype)
                else:
                    o_ref[rows, :] = o.astype(o_ref.dtype)

        if nk == 1:
            for s in range(nsub):
                rows = None if nsub == 1 else pl.ds(s * sub, sub)
                finish(dots(rows), rows)
            return

        acc = refs[-1]
        k = pl.program_id(2)

        @pl.when(k == 0)
        def _():
            acc[...] = dots(None)

        @pl.when(jnp.logical_and(k > 0, k < nk - 1))
        def _():
            acc[...] += dots(None)

        @pl.when(k == nk - 1)
        def _():
            finish(acc[...] + dots(None), None)

    a_spec = pl.BlockSpec((tk, tm), lambda i, j, k: (k, i)) if ta else pl.BlockSpec((tm, tk), lambda i, j, k: (i, k))
    if b3d:
        b_spec = (pl.BlockSpec((None, tn, tk), lambda i, j, k: (k, j, 0)) if tb
                  else pl.BlockSpec((None, tk, tn), lambda i, j, k: (j, k, 0)))
    else:
        b_spec = (pl.BlockSpec((tn, tk), lambda i, j, k: (j, k)) if tb
                  else pl.BlockSpec((tk, tn), lambda i, j, k: (k, j)))
    e_spec = pl.BlockSpec((tm, tn), lambda i, j, k: (i, j))
    if col_shards:
        o_spec = pl.BlockSpec((None, tm, tn), lambda i, j, k: (j, i, 0))
        o_shape = (N // tn, M, tn)
    else:
        o_spec, o_shape = e_spec, (M, N)
    args, in_specs = [], []
    for a, b in pairs:
        args += [a, b]
        in_specs += [a_spec, b_spec]
    args += list(extras) + list(deps)
    in_specs += [e_spec] * nex + [ANY] * ndep
    outs = pl.pallas_call(
        body,
        name=name,
        grid=(M // tm, N // tn, nk),
        in_specs=in_specs,
        out_specs=[o_spec] * nout,
        out_shape=[jax.ShapeDtypeStruct(o_shape, dt) for dt in out_dtypes],
        scratch_shapes=[pltpu.VMEM((tm, tn), F32)] if nk > 1 else [],
        compiler_params=_cparams(("parallel", "parallel", "arbitrary")),
    )(*args)
    return outs


def _cast_into_gather(w, chip_arr, name, split_cols=False):
    R, C = w.shape
    hr, hc = (R, C // 2) if split_cols else (R // 2, C)
    tr = _tile(hr, 512, 8)
    nb = hr // tr

    def body(chip_ref, w_ref, o_ref):
        o_ref[...] = w_ref[...].astype(BF16)

    in_map = (lambda h, i, chip_ref: (i, h)) if split_cols else (lambda h, i, chip_ref: (h * nb + i, 0))
    grid_spec = pltpu.PrefetchScalarGridSpec(
        num_scalar_prefetch=1, grid=(2, nb),
        in_specs=[pl.BlockSpec((tr, hc), in_map)],
        out_specs=pl.BlockSpec((None, tr, hc), lambda h, i, chip_ref: (2 * chip_ref[0] + h, i, 0)))
    return pl.pallas_call(
        body, name=name, grid_spec=grid_spec,
        out_shape=jax.ShapeDtypeStruct((N_DEV, hr, hc), BF16),
        compiler_params=_cparams(("parallel", "parallel")),
    )(chip_arr, w)


def _tie(small, token, name):
    def body(s_ref, t_ref, o_ref):
        o_ref[...] = s_ref[...]

    vm = pl.BlockSpec(memory_space=pltpu.VMEM)
    return pl.pallas_call(body, name=name, in_specs=[vm, ANY], out_specs=vm,
                          out_shape=jax.ShapeDtypeStruct(small.shape, small.dtype))(small, token)


def _rmsnorm_fwd(x, g, name):
    T, D = x.shape
    tt = _tile(T, 256)

    def body(x_ref, g_ref, n_ref):
        xv = x_ref[...]
        r = lax.rsqrt(jnp.mean(xv * xv, axis=-1, keepdims=True) + EPS)
        n_ref[...] = (xv * r * g_ref[...]).astype(BF16)

    return pl.pallas_call(
        body, name=name, grid=(T // tt,),
        in_specs=[pl.BlockSpec((tt, D), lambda i: (i, 0)), pl.BlockSpec((1, D), lambda i: (0, 0))],
        out_specs=pl.BlockSpec((tt, D), lambda i: (i, 0)),
        out_shape=jax.ShapeDtypeStruct((T, D), BF16),
        compiler_params=_cparams(("parallel",)),
    )(x, g)


def _rmsnorm_bwd(dn, x, g, res, name):
    T, D = x.shape
    tt = _tile(T, 256)

    def body(dn_ref, x_ref, g_ref, res_ref, dx_ref, dxb_ref, dg_ref):
        @pl.when(pl.program_id(0) == 0)
        def _():
            dg_ref[...] = jnp.zeros_like(dg_ref)

        xv = x_ref[...]
        dy = dn_ref[...].astype(F32)
        r = lax.rsqrt(jnp.mean(xv * xv, axis=-1, keepdims=True) + EPS)
        xhat = xv * r
        dxh = dy * g_ref[...]
        dx = res_ref[...] + r * (dxh - xhat * jnp.mean(dxh * xhat, axis=-1, keepdims=True))
        dx_ref[...] = dx
        dxb_ref[...] = dx.astype(BF16)
        dg_ref[...] += jnp.sum(dy * xhat, axis=0, keepdims=True)

    tok = pl.BlockSpec((tt, D), lambda i: (i, 0))
    vec = pl.BlockSpec((1, D), lambda i: (0, 0))
    return pl.pallas_call(
        body, name=name, grid=(T // tt,),
        in_specs=[tok, tok, vec, tok],
        out_specs=[tok, tok, vec],
        out_shape=[jax.ShapeDtypeStruct((T, D), F32), jax.ShapeDtypeStruct((T, D), BF16),
                   jax.ShapeDtypeStruct((1, D), F32)],
        compiler_params=_cparams(("arbitrary",)),
    )(dn, x, g, res)


def _loss_and_final_bwd(h2, target, gf):
    T, D = h2.shape
    tt = _tile(T, 256)

    def body(h_ref, t_ref, g_ref, dh_ref, dhb_ref, dg_ref, loss_ref):
        @pl.when(pl.program_id(0) == 0)
        def _():
            dg_ref[...] = jnp.zeros_like(dg_ref)
            loss_ref[...] = jnp.zeros_like(loss_ref)

        xv = h_ref[...]
        r = lax.rsqrt(jnp.mean(xv * xv, axis=-1, keepdims=True) + EPS)
        xhat = xv * r
        err = xhat * g_ref[...] - t_ref[...]
        loss_ref[...] += 0.5 * jnp.sum(jnp.mean(err * err, axis=-1, keepdims=True), axis=0, keepdims=True)
        dy = err * (1.0 / D)
        dxh = dy * g_ref[...]
        dx = r * (dxh - xhat * jnp.mean(dxh * xhat, axis=-1, keepdims=True))
        dh_ref[...] = dx
        dhb_ref[...] = dx.astype(BF16)
        dg_ref[...] += jnp.sum(dy * xhat, axis=0, keepdims=True)

    tok = pl.BlockSpec((tt, D), lambda i: (i, 0))
    vec = pl.BlockSpec((1, D), lambda i: (0, 0))
    return pl.pallas_call(
        body, name="loss_final_bwd", grid=(T // tt,),
        in_specs=[tok, tok, vec],
        out_specs=[tok, tok, vec, pl.BlockSpec((1, 1), lambda i: (0, 0))],
        out_shape=[jax.ShapeDtypeStruct((T, D), F32), jax.ShapeDtypeStruct((T, D), BF16),
                   jax.ShapeDtypeStruct((1, D), F32), jax.ShapeDtypeStruct((1, 1), F32)],
        compiler_params=_cparams(("arbitrary",)),
    )(h2, target, gf)


def _gated_norm_fwd(y, proj, g):
    T, D = y.shape
    tt = _tile(T, 256)

    def body(y_ref, z_ref, g_ref, o_ref):
        yg = y_ref[...] * _silu(z_ref[...])
        r = lax.rsqrt(jnp.mean(yg * yg, axis=-1, keepdims=True) + EPS)
        o_ref[...] = (yg * r * g_ref[...]).astype(BF16)

    tok = pl.BlockSpec((tt, D), lambda i: (i, 0))
    return pl.pallas_call(
        body, name="gated_norm_fwd", grid=(T // tt,),
        in_specs=[tok, tok, pl.BlockSpec((1, D), lambda i: (0, 0))],
        out_specs=tok,
        out_shape=jax.ShapeDtypeStruct((T, 2 * D_MODEL), BF16),
        compiler_params=_cparams(("parallel",)),
    )(y, proj, g)


def _gated_norm_bwd(dmix, y, proj, g, dproj):
    T, D = y.shape
    tt = _tile(T, 256)

    def body(do_ref, y_ref, z_ref, g_ref, dp_ref, dy_ref, dz_ref, dg_ref):
        @pl.when(pl.program_id(0) == 0)
        def _():
            dg_ref[...] = jnp.zeros_like(dg_ref)

        yv, zv = y_ref[...], z_ref[...]
        do = do_ref[...].astype(F32)
        sz = _silu(zv)
        yg = yv * sz
        r = lax.rsqrt(jnp.mean(yg * yg, axis=-1, keepdims=True) + EPS)
        xhat = yg * r
        dxh = do * g_ref[...]
        dyg = r * (dxh - xhat * jnp.mean(dxh * xhat, axis=-1, keepdims=True))
        dy_ref[...] = dyg * sz
        dz_ref[...] = (dyg * yv * _dsilu(zv)).astype(BF16)
        dg_ref[...] += jnp.sum(do * xhat, axis=0, keepdims=True)

    tok = pl.BlockSpec((tt, D), lambda i: (i, 0))
    vec = pl.BlockSpec((1, D), lambda i: (0, 0))
    return pl.pallas_call(
        body, name="gated_norm_bwd", grid=(T // tt,),
        in_specs=[tok, tok, tok, vec, ANY],
        out_specs=[tok, tok, vec],
        out_shape=[jax.ShapeDtypeStruct((T, D), F32), jax.ShapeDtypeStruct(dproj.shape, BF16),
                   jax.ShapeDtypeStruct((1, D), F32)],
        input_output_aliases={4: 1},
        compiler_params=_cparams(("arbitrary",)),
    )(dmix, y, proj, g, dproj)


HALO = 8


def _shift_down(cur, prev8, s):
    ext = jnp.concatenate([prev8, cur], axis=0)
    return pltpu.roll(ext, s, axis=0)[HALO:]


def _shift_up(cur, next8, s):
    n = cur.shape[0]
    ext = jnp.concatenate([cur, next8], axis=0)
    return pltpu.roll(ext, n + HALO - s, axis=0)[:n]


def _conv_specs(tt, cb, col_off_blocks, nt):
    hb = tt // HALO
    cur = pl.BlockSpec((tt, cb), lambda j, i: (i, col_off_blocks + j))
    prev = pl.BlockSpec((HALO, cb), lambda j, i: (jnp.maximum(i * hb - 1, 0), col_off_blocks + j))
    nxt = pl.BlockSpec((HALO, cb), lambda j, i: (jnp.minimum((i + 1) * hb, nt * hb - 1), col_off_blocks + j))
    return cur, prev, nxt


def _causal_conv(cur, prev8, w, K):
    y = cur * w[K - 1:K, :]
    for k in range(K - 1):
        y = y + _shift_down(cur, prev8, K - 1 - k) * w[k:k + 1, :]
    return y


def _anticausal_conv(cur, next8, w, K):
    y = cur * w[K - 1:K, :]
    for k in range(K - 1):
        y = y + _shift_up(cur, next8, K - 1 - k) * w[k:k + 1, :]
    return y


def _ssm_conv_fwd(proj, w8, b):
    T = proj.shape[0]
    tt, cb = _tile(T, 512), 512
    nt = T // tt
    cur, prev, _ = _conv_specs(tt, cb, OFF_XBC // cb, nt)

    def body(u_ref, up_ref, w_ref, b_ref, o_ref):
        first = pl.program_id(1) == 0
        p8 = jnp.where(first, 0.0, up_ref[...])
        pre = _causal_conv(u_ref[...], p8, w_ref[...], K_SSM) + b_ref[...]
        o_ref[...] = _silu(pre)

    return pl.pallas_call(
        body, name="ssm_conv_fwd", grid=(D_XBC // cb, nt),
        in_specs=[cur, prev, pl.BlockSpec((8, cb), lambda j, i: (0, j)), pl.BlockSpec((1, cb), lambda j, i: (0, j))],
        out_specs=pl.BlockSpec((tt, cb), lambda j, i: (i, j)),
        out_shape=jax.ShapeDtypeStruct((T, D_XBC), F32),
        compiler_params=_cparams(("parallel", "parallel")),
    )(proj, proj, w8, b)


def _ssm_conv_bwd(dact, proj, w8, b, dproj):
    T = proj.shape[0]
    tt, cb = _tile(T, 512), 512
    nt = T // tt
    cur, prev, nxt = _conv_specs(tt, cb, OFF_XBC // cb, nt)
    dcur, dprev, dnxt = _conv_specs(tt, cb, 0, nt)

    def dpre_of(d, u, p8, w, bb):
        pre = _causal_conv(u, p8, w, K_SSM) + bb
        return d * _dsilu(pre)

    def body(d_ref, dn_ref, u_ref, up_ref, un_ref, w_ref, b_ref, dp_ref, dx_ref, dw_ref, db_ref):
        i = pl.program_id(1)

        @pl.when(i == 0)
        def _():
            dw_ref[...] = jnp.zeros_like(dw_ref)
            db_ref[...] = jnp.zeros_like(db_ref)

        w, bb = w_ref[...], b_ref[...]
        u = u_ref[...]
        p8 = jnp.where(i == 0, 0.0, up_ref[...])
        dpre = dpre_of(d_ref[...], u, p8, w, bb)
        un = un_ref[...]
        dpre_n = dpre_of(dn_ref[...], un, u[tt - HALO:, :], w, bb)
        dpre_n = jnp.where(i == nt - 1, 0.0, dpre_n)
        dx_ref[...] = _anticausal_conv(dpre, dpre_n, w, K_SSM).astype(BF16)
        rows = [jnp.sum(dpre * _shift_down(u, p8, K_SSM - 1 - k), axis=0, keepdims=True) for k in range(K_SSM - 1)]
        rows.append(jnp.sum(dpre * u, axis=0, keepdims=True))
        rows.append(jnp.zeros((8 - K_SSM, cb), F32))
        dw_ref[...] += jnp.concatenate(rows, axis=0)
        db_ref[...] += jnp.sum(dpre, axis=0, keepdims=True)

    wspec = pl.BlockSpec((8, cb), lambda j, i: (0, j))
    bspec = pl.BlockSpec((1, cb), lambda j, i: (0, j))
    return pl.pallas_call(
        body, name="ssm_conv_bwd", grid=(D_XBC // cb, nt),
        in_specs=[dcur, dnxt, cur, prev, nxt, wspec, bspec, ANY],
        out_specs=[pl.BlockSpec((tt, cb), lambda j, i: (i, OFF_XBC // cb + j)), wspec, bspec],
        out_shape=[jax.ShapeDtypeStruct(dproj.shape, BF16), jax.ShapeDtypeStruct((8, D_XBC), F32),
                   jax.ShapeDtypeStruct((1, D_XBC), F32)],
        input_output_aliases={7: 0},
        compiler_params=_cparams(("parallel", "arbitrary")),
    )(dact, dact, proj, proj, proj, w8, b, dproj)


SCB = 512
SC3 = 3 * SCB


def _sc_specs(tt, nt):
    hb = tt // HALO
    cur = pl.BlockSpec((tt, SC3), lambda j, i: (i, OFF_CB // SC3 + j))
    prev = pl.BlockSpec((HALO, SC3), lambda j, i: (jnp.maximum(i * hb - 1, 0), OFF_CB // SC3 + j))
    nxt = pl.BlockSpec((HALO, SC3), lambda j, i: (jnp.minimum((i + 1) * hb, nt * hb - 1), OFF_CB // SC3 + j))
    return cur, prev, nxt


def _shortconv_fwd(proj, w8, ymix):
    T = proj.shape[0]
    tt = _tile(T, 512)
    nt = T // tt
    cur, prev, _ = _sc_specs(tt, nt)

    def body(p_ref, pp_ref, w_ref, y_ref, o_ref):
        p, pp = p_ref[...], pp_ref[...]
        v = p[:, SCB:2 * SCB] * p[:, 2 * SCB:]
        vp = jnp.where(pl.program_id(1) == 0, 0.0, pp[:, SCB:2 * SCB] * pp[:, 2 * SCB:])
        o_ref[...] = (p[:, :SCB] * _causal_conv(v, vp, w_ref[...], K_SC)).astype(BF16)

    return pl.pallas_call(
        body, name="shortconv_fwd", grid=(D_MODEL // SCB, nt),
        in_specs=[cur, prev, pl.BlockSpec((8, SCB), lambda j, i: (0, j)), ANY],
        out_specs=pl.BlockSpec((tt, SCB), lambda j, i: (i, D_SSM // SCB + j)),
        out_shape=jax.ShapeDtypeStruct(ymix.shape, BF16),
        input_output_aliases={3: 0},
        compiler_params=_cparams(("parallel", "parallel")),
    )(proj, proj, w8, ymix)


def _shortconv_bwd(dmix, proj, w8):
    T = proj.shape[0]
    tt = _tile(T, 512)
    nt = T // tt
    hb = tt // HALO
    cur, prev, nxt = _sc_specs(tt, nt)
    d_s = pl.BlockSpec((tt, SCB), lambda j, i: (i, D_SSM // SCB + j))
    dn_s = pl.BlockSpec((HALO, SCB), lambda j, i: (jnp.minimum((i + 1) * hb, nt * hb - 1), D_SSM // SCB + j))

    def body(d_ref, dn_ref, p_ref, pp_ref, pn_ref, w_ref, dp_ref, dw_ref):
        i = pl.program_id(1)

        @pl.when(i == 0)
        def _():
            dw_ref[...] = jnp.zeros_like(dw_ref)

        w = w_ref[...]
        p, pp = p_ref[...], pp_ref[...]
        gb, gc, u = p[:, :SCB], p[:, SCB:2 * SCB], p[:, 2 * SCB:]
        v = gc * u
        vp = jnp.where(i == 0, 0.0, pp[:, SCB:2 * SCB] * pp[:, 2 * SCB:])
        d = d_ref[...].astype(F32)
        dp_ref[:, :SCB] = (d * _causal_conv(v, vp, w, K_SC)).astype(BF16)
        dcv = d * gb
        dcv_n = jnp.where(i == nt - 1, 0.0, dn_ref[...].astype(F32) * pn_ref[:, :SCB])
        dv = _anticausal_conv(dcv, dcv_n, w, K_SC)
        dp_ref[:, SCB:2 * SCB] = (dv * u).astype(BF16)
        dp_ref[:, 2 * SCB:] = (dv * gc).astype(BF16)
        rows = [jnp.sum(dcv * _shift_down(v, vp, K_SC - 1 - k), axis=0, keepdims=True) for k in range(K_SC - 1)]
        rows.append(jnp.sum(dcv * v, axis=0, keepdims=True))
        rows.append(jnp.zeros((8 - K_SC, SCB), F32))
        dw_ref[...] += jnp.concatenate(rows, axis=0)

    wspec = pl.BlockSpec((8, SCB), lambda j, i: (0, j))
    return pl.pallas_call(
        body, name="shortconv_bwd", grid=(D_MODEL // SCB, nt),
        in_specs=[d_s, dn_s, cur, prev, nxt, wspec],
        out_specs=[cur, wspec],
        out_shape=[jax.ShapeDtypeStruct((T, D_MAIN), BF16), jax.ShapeDtypeStruct((8, D_MODEL), F32)],
        compiler_params=_cparams(("parallel", "arbitrary")),
    )(dmix, dmix, proj, proj, proj, w8)


GW = HEADS_PER_GROUP * HEADDIM
HI = lax.Precision.HIGHEST


def _dot(a, b):
    return jnp.dot(a.astype(BF16), b.astype(BF16), preferred_element_type=F32)


def _dot_nt(a, b):
    return lax.dot_general(a.astype(BF16), b.astype(BF16), (((1,), (1,)), ((), ())), preferred_element_type=F32)


def _dot_tn(a, b):
    return lax.dot_general(a.astype(BF16), b.astype(BF16), (((0,), (0,)), ((), ())), preferred_element_type=F32)


def _dot_hi(a, b):
    return jnp.dot(a, b, precision=HI, preferred_element_type=F32)


def _dot_nt_hi(a, b):
    return lax.dot_general(a, b, (((1,), (1,)), ((), ())), precision=HI, preferred_element_type=F32)


def _head_cols(rows):
    parts = [jnp.broadcast_to(rows[r:r + 1, :], (HEADDIM, CHUNK)) for r in range(HEADS_PER_GROUP)]
    return jnp.concatenate(parts, axis=0).T


def _head_rows(rows):
    parts = [jnp.broadcast_to(rows[r:r + 1, :], (HEADDIM, N_STATE)) for r in range(HEADS_PER_GROUP)]
    return jnp.concatenate(parts, axis=0)


def _ssd_common(dtr, bias, alog):
    dt = _softplus(dtr + bias)
    A = -jnp.exp(alog)
    a = dt * A
    ki = lax.broadcasted_iota(jnp.int32, (CHUNK, CHUNK), 0)
    si = lax.broadcasted_iota(jnp.int32, (CHUNK, CHUNK), 1)
    upper = (ki <= si).astype(F32)
    cs = _dot_hi(a, upper)
    cs_last = jnp.broadcast_to(cs[:, CHUNK - 1:CHUNK], (8, CHUNK))
    return dt, A, a, cs, cs_last


def _decay_matrix(cs, r):
    li = lax.broadcasted_iota(jnp.int32, (CHUNK, CHUNK), 0)
    si = lax.broadcasted_iota(jnp.int32, (CHUNK, CHUNK), 1)
    causal = li >= si
    R = jnp.broadcast_to(cs[r:r + 1, :], (CHUNK, CHUNK))
    seg = jnp.where(causal, R.T - R, 0.0)
    return jnp.where(causal, jnp.exp(seg), 0.0)


GXBC = GW + 2 * N_STATE


GS = 2


def _ssd_in_specs(nc, rev):
    cix = (lambda c: nc - 1 - c) if rev else (lambda c: c)
    x_s = pl.BlockSpec((CHUNK, GS * GW), lambda g, c: (cix(c), g))
    xbc_s = pl.BlockSpec((CHUNK, GS * GXBC), lambda g, c: (cix(c), g))
    dtr_s = pl.BlockSpec((GS, 8, CHUNK), lambda g, c: (g, 0, cix(c)))
    row_s = pl.BlockSpec((GS, 8, CHUNK), lambda g, c: (g, 0, 0))
    drep_s = pl.BlockSpec((1, GS * GW), lambda g, c: (0, g))
    hs_s = pl.BlockSpec((1, GS * GW, N_STATE), lambda g, c: (cix(c), g, 0))
    return x_s, xbc_s, dtr_s, row_s, drep_s, hs_s


def _xbc_parts(xbc_ref, gi):
    o = gi * GXBC
    return xbc_ref[:, o:o + GW], xbc_ref[:, o + GW:o + GW + N_STATE], xbc_ref[:, o + GW + N_STATE:o + GXBC]


def _ssd_fwd(xbc, dtr, bias, alog, drep):
    T = xbc.shape[0]
    nc = T // CHUNK
    x_s, xbc_s, dtr_s, row_s, drep_s, hs_s = _ssd_in_specs(nc, False)

    def body(xbc_ref, dtr_ref, bias_ref, alog_ref, drep_ref, y_ref, hs_ref, h_scr):
        @pl.when(pl.program_id(1) == 0)
        def _():
            h_scr[...] = jnp.zeros_like(h_scr)

        for gi in range(GS):
            cols, rows = slice(gi * GW, (gi + 1) * GW), pl.ds(gi * GW, GW)
            x, Bm, Cm = _xbc_parts(xbc_ref, gi)
            dt, A, a, cs, cs_last = _ssd_common(dtr_ref[gi], bias_ref[gi], alog_ref[gi])
            E = _head_cols(jnp.exp(cs))
            W = _head_cols(jnp.exp(cs_last - cs) * dt)
            X = (x * _head_cols(dt)).astype(BF16)
            CB = _dot_nt(Cm, Bm)
            col = lax.broadcasted_iota(jnp.int32, (CHUNK, GW), 1) // HEADDIM
            y = jnp.zeros((CHUNK, GW), F32)
            for r in range(HEADS_PER_GROUP):
                M = CB * _decay_matrix(cs, r)
                y = y + jnp.where(col == r, _dot(M, X), 0.0)
            h = h_scr[rows, :]
            hs_ref[0, rows, :] = h
            y = y + _dot_nt(Cm, h) * E
            y_ref[:, cols] = y + drep_ref[:, cols] * x
            h_scr[rows, :] = h * _head_rows(jnp.exp(cs_last)) + _dot_tn(x * W, Bm)

    return pl.pallas_call(
        body, name="ssd_fwd", grid=(N_GROUPS // GS, nc),
        in_specs=[xbc_s, dtr_s, row_s, row_s, drep_s],
        out_specs=[x_s, hs_s],
        out_shape=[jax.ShapeDtypeStruct((T, D_SSM), F32), jax.ShapeDtypeStruct((nc, D_SSM, N_STATE), F32)],
        scratch_shapes=[pltpu.VMEM((GS * GW, N_STATE), F32)],
        compiler_params=_cparams(("parallel", "arbitrary")),
    )(xbc, dtr, bias, alog, drep)


def _ssd_bwd(xbc, dtr, bias, alog, drep, dy, hs):
    T = xbc.shape[0]
    nc = T // CHUNK
    x_s, xbc_s, dtr_s, row_s, drep_s, hs_s = _ssd_in_specs(nc, True)

    def body(xbc_ref, dtr_ref, bias_ref, alog_ref, drep_ref, dy_ref, hs_ref,
             dxbc_ref, ddtr_ref, dbias_ref, dalog_ref, dd_ref, dh_scr):
        @pl.when(pl.program_id(1) == 0)
        def _():
            dh_scr[...] = jnp.zeros_like(dh_scr)
            dbias_ref[...] = jnp.zeros_like(dbias_ref)
            dalog_ref[...] = jnp.zeros_like(dalog_ref)
            dd_ref[...] = jnp.zeros_like(dd_ref)

        for gi in range(GS):
            one_group(gi, xbc_ref, dtr_ref, bias_ref, alog_ref, drep_ref, dy_ref, hs_ref,
                      dxbc_ref, ddtr_ref, dbias_ref, dalog_ref, dd_ref, dh_scr)

    def one_group(gi, xbc_ref, dtr_ref, bias_ref, alog_ref, drep_ref, dy_ref, hs_ref,
                  dxbc_ref, ddtr_ref, dbias_ref, dalog_ref, dd_ref, dh_scr):
        cols, rows, o = slice(gi * GW, (gi + 1) * GW), pl.ds(gi * GW, GW), gi * GXBC
        x, Bm, Cm = _xbc_parts(xbc_ref, gi)
        dY = dy_ref[:, cols]
        dt, A, a, cs, cs_last = _ssd_common(dtr_ref[gi], bias_ref[gi], alog_ref[gi])
        E = _head_cols(jnp.exp(cs))
        DT = _head_cols(dt)
        Wd = _head_cols(jnp.exp(cs_last - cs))
        X = x * DT
        h = hs_ref[0, rows, :]
        dS = dh_scr[rows, :]
        CB = _dot_nt(Cm, Bm)
        col = lax.broadcasted_iota(jnp.int32, (CHUNK, GW), 1) // HEADDIM
        rowid = lax.broadcasted_iota(jnp.int32, (8, CHUNK), 0)
        lane = lax.broadcasted_iota(jnp.int32, (8, CHUNK), 1)
        hsel = (lax.broadcasted_iota(jnp.int32, (8, GW), 1) // HEADDIM
                == lax.broadcasted_iota(jnp.int32, (8, GW), 0)).astype(F32)
        ones8 = jnp.ones((8, CHUNK), F32)

        dX = jnp.zeros((CHUNK, GW), F32)
        dCB = jnp.zeros((CHUNK, CHUNK), F32)
        dcs = jnp.zeros((8, CHUNK), F32)
        for r in range(HEADS_PER_GROUP):
            L = _decay_matrix(cs, r)
            M = CB * L
            G = _dot_nt(jnp.where(col == r, dY, 0.0), X)
            GL = G * L
            dCB = dCB + GL
            Wm = GL * CB
            colsum = jnp.sum(Wm, axis=0, keepdims=True)
            rowsum = _dot_nt_hi(ones8, Wm)
            dcs = dcs + jnp.where(rowid == r, rowsum - colsum, 0.0)
            dX = dX + jnp.where(col == r, _dot_tn(M, dY), 0.0)
        dC = _dot(dCB, Bm)
        dB = _dot_tn(dCB, Cm)
        T1 = _dot_nt(Bm, dS)
        dX = dX + T1 * Wd
        dB = dB + _dot(X * Wd, dS)
        pdec = _dot_nt_hi(hsel, X * T1 * Wd)
        dcs = dcs - pdec
        dlast = jnp.sum(pdec, axis=1, keepdims=True) \
            + jnp.exp(cs_last[:, 0:1]) * jnp.sum(_dot_hi(hsel, dS * h), axis=1, keepdims=True)
        dYE = dY * E
        dC = dC + _dot(dYE, h)
        yoff = _dot_nt(Cm, h) * E
        dcs = dcs + _dot_nt_hi(hsel, dY * yoff)
        dcs = dcs + jnp.where(lane == CHUNK - 1, dlast, 0.0)
        ki = lax.broadcasted_iota(jnp.int32, (CHUNK, CHUNK), 0)
        si = lax.broadcasted_iota(jnp.int32, (CHUNK, CHUNK), 1)
        lower = (ki >= si).astype(F32)
        da = _dot_hi(dcs, lower)
        ddt = da * A + _dot_nt_hi(hsel, dX * x)
        ddtr = ddt * _sigmoid(dtr_ref[gi] + bias_ref[gi])
        ddtr_ref[gi] = ddtr
        dbias_ref[gi] += ddtr
        dalog_ref[gi] += da * a
        dxbc_ref[:, o:o + GW] = dX * DT + drep_ref[:, cols] * dY
        dd_ref[:, cols] += jnp.sum(dY * x, axis=0, keepdims=True)
        dxbc_ref[:, o + GW:o + GW + N_STATE] = dB
        dxbc_ref[:, o + GW + N_STATE:o + GXBC] = dC
        dh_scr[rows, :] = dS * _head_rows(jnp.exp(cs_last)) + _dot_tn(dYE, Cm)

    return pl.pallas_call(
        body, name="ssd_bwd", grid=(N_GROUPS // GS, nc),
        in_specs=[xbc_s, dtr_s, row_s, row_s, drep_s, x_s, hs_s],
        out_specs=[xbc_s, dtr_s, row_s, row_s, drep_s],
        out_shape=[jax.ShapeDtypeStruct((T, D_XBC), F32),
                   jax.ShapeDtypeStruct((N_GROUPS, 8, T), F32),
                   jax.ShapeDtypeStruct((N_GROUPS, 8, CHUNK), F32),
                   jax.ShapeDtypeStruct((N_GROUPS, 8, CHUNK), F32),
                   jax.ShapeDtypeStruct((1, D_SSM), F32)],
        scratch_shapes=[pltpu.VMEM((GS * GW, N_STATE), F32)],
        compiler_params=_cparams(("parallel", "arbitrary")),
    )(xbc, dtr, bias, alog, drep, dy, hs)


def _adamw(w, g, m, v, name, deps=(), emit_g=False):
    R, C = w.shape
    tr = _tile(R, 256, 8)
    nd = len(deps)
    nout = 4 if emit_g else 3

    def body(w_ref, g_ref, m_ref, v_ref, *rest):
        outs = rest[nd:]
        gv = g_ref[...]
        mn = ADAM_B1 * m_ref[...] + (1.0 - ADAM_B1) * gv
        vn = ADAM_B2 * v_ref[...] + (1.0 - ADAM_B2) * (gv * gv)
        m_hat = mn / (1.0 - ADAM_B1 ** ADAM_STEP)
        v_hat = vn / (1.0 - ADAM_B2 ** ADAM_STEP)
        outs[0][...] = -ADAM_LR * (m_hat / (jnp.sqrt(v_hat) + ADAM_EPS) + ADAM_WD * w_ref[...])
        outs[1][...] = mn
        outs[2][...] = vn
        if emit_g:
            outs[3][...] = gv

    spec = pl.BlockSpec((tr, C), lambda i: (i, 0))
    return pl.pallas_call(
        body, name=name, grid=(R // tr,),
        in_specs=[spec] * 4 + [ANY] * nd, out_specs=[spec] * nout,
        out_shape=[jax.ShapeDtypeStruct((R, C), F32)] * nout,
        compiler_params=_cparams(("parallel",)),
    )(w, g, m, v, *deps)


ANY = pl.BlockSpec(memory_space=pl.ANY)


def _place():
    x, y, c = lax.axis_index("x"), lax.axis_index("y"), lax.axis_index("c")
    return x, y, c


def _other_chips(x, y):
    return [(1 - x, y), (x, 1 - y), (1 - x, 1 - y)]


def _allgather_inplace(bufs):
    n = len(bufs)

    def body(*refs):
        o_refs = refs[n:2 * n]
        send_sems, recv_sems = refs[2 * n:]
        x, y, c = _place()
        sibling = (x, y, 1 - c)
        chips = _other_chips(x, y)

        def copy(k, slot, px, py, pc, to):
            blk = o_refs[k].at[4 * px + 2 * py + pc]
            return pltpu.make_async_remote_copy(
                src_ref=blk, dst_ref=blk, send_sem=send_sems.at[k, slot], recv_sem=recv_sems.at[k, slot],
                device_id=to, device_id_type=MESH)

        sent = []
        for k in range(n):
            for j, (px, py) in enumerate(chips):
                cp = copy(k, j, x, y, c, (px, py, c))
                cp.start()
                sent.append(cp)
        for k in range(n):
            for j, (px, py) in enumerate(chips):
                copy(k, j, px, py, c, (px, py, c)).wait_recv()
                fwd = copy(k, 3 + j, px, py, c, sibling)
                fwd.start()
                sent.append(fwd)
        for k in range(n):
            for j, (px, py) in enumerate(chips):
                copy(k, 3 + j, px, py, 1 - c, sibling).wait_recv()
        for cp in sent:
            cp.wait_send()

    return pl.pallas_call(
        body, name="allgather_w_in",
        in_specs=[ANY] * n, out_specs=[ANY] * n,
        out_shape=[jax.ShapeDtypeStruct(b.shape, b.dtype) for b in bufs],
        input_output_aliases={k: k for k in range(n)},
        scratch_shapes=[pltpu.SemaphoreType.DMA((n, 6)), pltpu.SemaphoreType.DMA((n, 6))],
    )(*bufs)


HBM = pl.BlockSpec(memory_space=pltpu.HBM)
SEM = pl.BlockSpec(memory_space=pltpu.SEMAPHORE)
EFFECT = pltpu.SideEffectType.DATAFLOW_SIDE_EFFECTING


def _split_start(name, arrays, build, n_copies, after=()):
    na, nd = len(arrays), len(after)

    def body(*refs):
        send_sems, recv_sems = refs[na + nd], refs[na + nd + 1]
        for cp in build(refs[:na], send_sems, recv_sems):
            cp.start()
        refs[-1][...] = jnp.zeros((8, 128), F32)

    outs = pl.pallas_call(
        body, name=name,
        out_shape=(pltpu.SemaphoreType.DMA((n_copies,)), pltpu.SemaphoreType.DMA((n_copies,)),
                   *[pltpu.HBM(a.shape, a.dtype) for a in arrays], jax.ShapeDtypeStruct((8, 128), F32)),
        in_specs=[HBM] * na + [ANY] * nd,
        out_specs=(SEM, SEM, *[HBM] * na, pl.BlockSpec(memory_space=pltpu.VMEM)),
        input_output_aliases={i: 2 + i for i in range(na)},
        compiler_params=pltpu.CompilerParams(has_side_effects=EFFECT),
    )(*[pltpu.with_memory_space_constraint(a, pltpu.HBM) for a in arrays], *after)
    return outs[0], outs[1], list(outs[2:2 + na]), outs[-1]


def _split_wait(name, send_sems, recv_sems, arrays, build, after):
    na = len(arrays)

    def body(*refs):
        for cp in build(refs[:na], refs[na], refs[na + 1]):
            cp.wait_send()
            cp.wait_recv()

    outs = pl.pallas_call(
        body, name=name,
        out_shape=tuple(pltpu.HBM(a.shape, a.dtype) for a in arrays),
        in_specs=[HBM] * na + [SEM, SEM] + [ANY] * len(after),
        out_specs=tuple([HBM] * na),
        input_output_aliases={i: i for i in range(na)},
        compiler_params=pltpu.CompilerParams(has_side_effects=EFFECT),
    )(*arrays, send_sems, recv_sems, *after)
    return list(outs)


def _remote(src, dst, send_sems, recv_sems, i, to):
    return pltpu.make_async_remote_copy(src_ref=src, dst_ref=dst, send_sem=send_sems.at[i], recv_sem=recv_sems.at[i],
                                        device_id=to, device_id_type=MESH)


def _build_ag_ici(refs, ss, rs):
    x, y, c = _place()
    cps = []
    for k, ref in enumerate(refs):
        blk = ref.at[4 * x + 2 * y + c]
        for j, (px, py) in enumerate(_other_chips(x, y)):
            cps.append(_remote(blk, blk, ss, rs, 3 * k + j, (px, py, c)))
    return cps


def _build_ag_fwd(refs, ss, rs):
    x, y, c = _place()
    cps = []
    for k, ref in enumerate(refs):
        for j, (px, py) in enumerate(_other_chips(x, y)):
            blk = ref.at[4 * px + 2 * py + c]
            cps.append(_remote(blk, blk, ss, rs, 3 * k + j, (x, y, 1 - c)))
    return cps


def _build_rs_swap(refs, ss, rs):
    x, y, c = _place()
    n = len(refs) // 2
    return [_remote(refs[k].at[:, pl.ds(1 - c, 1)], refs[n + k], ss, rs, k, (x, y, 1 - c)) for k in range(n)]


def _build_rs_ici(refs, ss, rs):
    x, y, c = _place()
    n = len(refs) // 2
    me = 2 * x + y
    cps = []
    for k in range(n):
        for j, (px, py) in enumerate(_other_chips(x, y)):
            cps.append(_remote(refs[k].at[2 * px + py], refs[n + k].at[me], ss, rs, 3 * k + j, (px, py, c)))
    return cps


def _build_rs_share(refs, ss, rs):
    x, y, c = _place()
    return [_remote(ref.at[c], ref.at[c], ss, rs, k, (x, y, 1 - c)) for k, ref in enumerate(refs)]


def _allreduce_small(p, deps=()):
    R, C = p.shape
    nd = len(deps)

    def body(p_ref, *rest):
        gath_ref, sum_ref, send_sems, recv_sems, local_sem = rest[nd:]
        x, y, c = _place()
        me, sibling = (x, y, c), (x, y, 1 - c)
        chips = [(1 - x, y), (x, 1 - y), (1 - x, 1 - y)]

        def blk(px, py, pc):
            return gath_ref.at[4 * px + 2 * py + pc]

        def copy(k, block, to, src=None):
            return pltpu.make_async_remote_copy(
                src_ref=blk(*block) if src is None else src, dst_ref=blk(*block),
                send_sem=send_sems.at[k], recv_sem=recv_sems.at[k], device_id=to, device_id_type=MESH)

        mine = pltpu.make_async_copy(p_ref, blk(*me), local_sem)
        mine.start()
        first = [copy(0, me, sibling, src=p_ref)]
        first += [copy(1 + j, me, (*chip, c), src=p_ref) for j, chip in enumerate(chips)]
        for cp in first:
            cp.start()
        passed = [copy(4 + j, (*chip, c), sibling) for j, chip in enumerate(chips)]
        for j, chip in enumerate(chips):
            copy(1 + j, (*chip, c), me).wait_recv()
            passed[j].start()
        copy(0, sibling, me).wait_recv()
        for j, chip in enumerate(chips):
            copy(4 + j, (*chip, 1 - c), me).wait_recv()
        for cp in first + passed:
            cp.wait_send()
        mine.wait()
        s = gath_ref[0]
        for d in range(1, N_DEV):
            s = s + gath_ref[d]
        sum_ref[...] = s

    vm = pl.BlockSpec(memory_space=pltpu.VMEM)
    return pl.pallas_call(
        body, name="allreduce_small",
        in_specs=[vm] + [ANY] * nd, out_specs=[vm, vm],
        out_shape=[jax.ShapeDtypeStruct((N_DEV, R, C), F32), jax.ShapeDtypeStruct((R, C), F32)],
        scratch_shapes=[pltpu.SemaphoreType.DMA((7,)), pltpu.SemaphoreType.DMA((7,)), pltpu.SemaphoreType.DMA],
    )(p, *deps)[1]


def _rs_add_pair(p, r0, c_arr, name):
    _, _, hr, cols = p.shape
    tr = _tile(hr, 256, 8)

    def body(c_ref, p_ref, r_ref, q_ref):
        q_ref[...] = (p_ref[0] + r_ref[0]).astype(BF16)

    grid_spec = pltpu.PrefetchScalarGridSpec(
        num_scalar_prefetch=1, grid=(N_CHIPS, hr // tr),
        in_specs=[pl.BlockSpec((1, 1, tr, cols), lambda j, i, c_ref: (j, c_ref[0], i, 0)),
                  pl.BlockSpec((1, 1, tr, cols), lambda j, i, c_ref: (j, 0, i, 0))],
        out_specs=pl.BlockSpec((1, tr, cols), lambda j, i, c_ref: (j, i, 0)))
    return pl.pallas_call(
        body, name=name, grid_spec=grid_spec,
        out_shape=jax.ShapeDtypeStruct((N_CHIPS, hr, cols), BF16),
        compiler_params=_cparams(("parallel", "parallel")),
    )(c_arr, p, r0)


def _rs_add_chips(r1, q, place_arr, name):
    _, hr, cols = r1.shape
    tr = _tile(hr, 256, 8)

    def body(place_ref, r_ref, q_ref, o_ref):
        chip = place_ref[0]
        s = None
        for j in range(N_CHIPS):
            t = jnp.where(chip == j, q_ref[j], r_ref[j]).astype(F32)
            s = t if s is None else s + t
        o_ref[...] = s

    blk = pl.BlockSpec((N_CHIPS, tr, cols), lambda i, place_ref: (0, i, 0))
    grid_spec = pltpu.PrefetchScalarGridSpec(
        num_scalar_prefetch=1, grid=(hr // tr,), in_specs=[blk, blk],
        out_specs=pl.BlockSpec((None, tr, cols), lambda i, place_ref: (place_ref[1], i, 0)))
    return pl.pallas_call(
        body, name=name, grid_spec=grid_spec,
        out_shape=jax.ShapeDtypeStruct((2, hr, cols), F32),
        compiler_params=_cparams(("parallel",)),
    )(place_arr, r1, q)


def _pad_rows(a, rows):
    return jnp.pad(a, ((0, rows - a.shape[0]), (0, 0)))


def _pad_cols(a, cols):
    return jnp.pad(a, ((0, 0), (0, cols - a.shape[1])))


def _heads_to_rows(v):
    v = v.reshape(N_GROUPS, HEADS_PER_GROUP, 1)
    v = jnp.pad(v, ((0, 0), (0, 8 - HEADS_PER_GROUP), (0, 0)))
    return jnp.broadcast_to(v, (N_GROUPS, 8, CHUNK))


def _rows_to_heads(a):
    return jnp.sum(a[:, :HEADS_PER_GROUP, :], axis=-1).reshape(N_HEADS)


def _to_kernel_rows(a):
    C = a.shape[1]
    x0, b0, c0, s0 = D_SSM, 2 * D_SSM, 2 * D_SSM + 1024, D_SSM + D_XBC + N_HEADS
    xbc = jnp.concatenate([a[x0:b0].reshape(N_GROUPS, GW, C), a[b0:c0].reshape(N_GROUPS, N_STATE, C),
                           a[c0:c0 + 1024].reshape(N_GROUPS, N_STATE, C)], axis=1).reshape(D_XBC, C)
    sc = jnp.concatenate([a[s0 + k * D_MODEL:s0 + (k + 1) * D_MODEL].reshape(D_MODEL // SCB, SCB, C)
                          for k in range(3)], axis=1).reshape(3 * D_MODEL, C)
    return jnp.concatenate([a[:D_SSM], xbc, sc], axis=0)


HR_IN = 1568


def _shard_row_plan():
    segs = [(0, 0, 0, D_SSM)]
    for g in range(N_GROUPS):
        k0 = D_SSM + g * GXBC
        segs += [(0, k0, D_SSM + g * GW, GW), (0, k0 + GW, 2 * D_SSM + g * N_STATE, N_STATE),
                 (0, k0 + GW + N_STATE, 2 * D_SSM + 1024 + g * N_STATE, N_STATE)]
    segs.append((1, 0, D_SSM + D_XBC, N_HEADS))
    for j in range(D_MODEL // SCB):
        for k in range(3):
            segs.append((0, D_SSM + D_XBC + j * SC3 + k * SCB, D_SSM + D_XBC + N_HEADS + k * D_MODEL + j * SCB, SCB))
    cs = D_IN // N_CHIPS
    plan = []
    for src, s, o, n in segs:
        while n > 0:
            chip, loc = divmod(o, cs)
            half, row = divmod(loc, HR_IN)
            m = min(n, cs - loc, HR_IN - row)
            plan.append((src, s, chip, half, row, m))
            s, o, n = s + m, o + m, n - m
    return plan


SCATTER_ROWS = 512
SCATTER_SLOTS = 4


def _scatter_rows_to_shards(k_main, k_dt):
    C = k_main.shape[1]
    pieces = []
    for src, s, chip, half, row, n in _shard_row_plan():
        for o in range(0, n, SCATTER_ROWS):
            pieces.append((src, s + o, chip, half, row + o, min(SCATTER_ROWS, n - o)))
    S, lag, N = SCATTER_SLOTS, SCATTER_SLOTS // 2, len(pieces)

    def body(m_ref, d_ref, o_ref, buf, in_sems, out_sems):
        def cin(i):
            src, s, _, _, _, n = pieces[i]
            return pltpu.make_async_copy((d_ref if src else m_ref).at[pl.ds(s, n)],
                                         buf.at[i % S, pl.ds(0, n)], in_sems.at[i % S])

        def cout(i):
            _, _, chip, half, row, n = pieces[i]
            return pltpu.make_async_copy(buf.at[i % S, pl.ds(0, n)],
                                         o_ref.at[chip, half, pl.ds(row, n)], out_sems.at[i % S])

        for i in range(N + lag):
            if i < N:
                if i >= S:
                    cout(i - S).wait()
                cin(i).start()
            j = i - lag
            if 0 <= j < N:
                cin(j).wait()
                cout(j).start()
        for j in range(max(0, N - S), N):
            cout(j).wait()

    return pl.pallas_call(
        body, name="scatter_dw_in_rows", in_specs=[ANY, ANY], out_specs=ANY,
        out_shape=jax.ShapeDtypeStruct((N_CHIPS, 2, HR_IN, C), k_main.dtype),
        scratch_shapes=[pltpu.VMEM((S, SCATTER_ROWS, C), k_main.dtype),
                        pltpu.SemaphoreType.DMA((S,)), pltpu.SemaphoreType.DMA((S,))],
        compiler_params=_cparams(),
    )(k_main, k_dt)


def _to_kernel_xbc(a):
    R = a.shape[0]
    return jnp.concatenate([a[:, :D_SSM].reshape(R, N_GROUPS, GW), a[:, D_SSM:D_SSM + 1024].reshape(R, N_GROUPS, N_STATE),
                            a[:, D_SSM + 1024:].reshape(R, N_GROUPS, N_STATE)], axis=2).reshape(R, D_XBC)


def _from_kernel_xbc(a):
    R = a.shape[0]
    g = a.reshape(R, N_GROUPS, GXBC)
    return jnp.concatenate([g[:, :, :GW].reshape(R, D_SSM), g[:, :, GW:GW + N_STATE].reshape(R, 1024),
                            g[:, :, GW + N_STATE:].reshape(R, 1024)], axis=1)


def kernel(x, norm_mix_g, w_in, ssm_conv_w, ssm_conv_b, ssm_dt_bias, ssm_A_log, ssm_D, ssm_norm_g, sc_conv_w, w_out, norm_ffn_g, w_gate, w_up, w_down, norm_final_g, loss_target, m_norm_mix_g, m_w_in, m_ssm_conv_w, m_ssm_conv_b, m_ssm_dt_bias, m_ssm_A_log, m_ssm_D, m_ssm_norm_g, m_sc_conv_w, m_w_out, m_norm_ffn_g, m_w_gate, m_w_up, m_w_down, m_norm_final_g, v_norm_mix_g, v_w_in, v_ssm_conv_w, v_ssm_conv_b, v_ssm_dt_bias, v_ssm_A_log, v_ssm_D, v_ssm_norm_g, v_sc_conv_w, v_w_out, v_norm_ffn_g, v_w_gate, v_w_up, v_w_down, v_norm_final_g):
    T = x.shape[1]
    xt = x[0]
    tgt = loss_target[0]
    cx, cy, cc = lax.axis_index("x"), lax.axis_index("y"), lax.axis_index("c")
    chip = 2 * cx + cy
    c_arr = jnp.reshape(cc, (1,)).astype(jnp.int32)
    chip_arr = jnp.reshape(chip, (1,)).astype(jnp.int32)
    place_arr = jnp.stack([chip, cc]).astype(jnp.int32)

    big = [w_in[0].T, w_out[0], w_gate[0], w_up[0], w_down[0]]
    names = ["w_in", "w_out", "w_gate", "w_up", "w_down"]
    gbufs = [_cast_into_gather(w, chip_arr, "cast_" + nm, split_cols=(nm == "w_in")) for w, nm in zip(big, names)]
    (g_in,) = _allgather_inplace([gbufs[0]])
    cs_in = D_IN // N_CHIPS
    wt = g_in.reshape(N_CHIPS, 2, cs_in, D_MODEL // 2).transpose(0, 2, 1, 3).reshape(D_IN, D_MODEL)
    wt_main = _to_kernel_rows(wt)
    wt_dt = _pad_rows(wt[D_SSM + D_XBC:D_SSM + D_XBC + N_HEADS], DT_PAD)

    contrib = (cc == 0).astype(F32)
    place_ssm = jnp.zeros((8, D_XBC), F32)
    place_ssm = lax.dynamic_update_slice(place_ssm, _pad_rows(ssm_conv_w[0], 8) * contrib, (0, chip * (D_XBC // N_CHIPS)))
    place_sc = jnp.zeros((8, D_MODEL), F32)
    place_sc = lax.dynamic_update_slice(place_sc, _pad_rows(sc_conv_w[0], 8) * contrib, (0, chip * (D_MODEL // N_CHIPS)))
    convs = _allreduce_small(jnp.concatenate([place_ssm, _pad_cols(place_sc, D_XBC)], axis=0))
    ssm_w8 = _to_kernel_xbc(convs[:8])
    ssm_bk = _to_kernel_xbc(ssm_conv_b)
    sc_w8 = convs[8:, :D_MODEL]
    ag_ss, ag_rs, ag_bufs, ag_tok = _split_start("ag_ici_start", gbufs[1:], _build_ag_ici, 12, after=[g_in, convs])

    bias_rows = _heads_to_rows(ssm_dt_bias[0])
    alog_rows = _heads_to_rows(ssm_A_log[0])
    drep = jnp.repeat(ssm_D[0], HEADDIM).reshape(1, D_SSM)

    n1 = _rmsnorm_fwd(xt, _tie(norm_mix_g, ag_tok, "tie_ag_ici"), "rmsnorm_mix")
    (proj,) = _matmul([(n1, wt_main)], tb=True, out_dtypes=[F32], name="mm_proj")
    (dt_raw,) = _matmul([(n1, wt_dt)], tb=True, out_dtypes=[F32], name="mm_proj_dt")
    ag_bufs = _split_wait("ag_ici_wait", ag_ss, ag_rs, ag_bufs, _build_ag_ici, after=[dt_raw])
    fw_ss, fw_rs, fw_bufs, fw_tok = _split_start("ag_fwd_start", ag_bufs, _build_ag_fwd, 12)
    xbc = _ssm_conv_fwd(proj, ssm_w8, _tie(ssm_bk, fw_tok, "tie_ag_fwd"))
    dtr = jnp.pad(dt_raw[:, :N_HEADS].T.reshape(N_GROUPS, HEADS_PER_GROUP, T), ((0, 0), (0, 4), (0, 0)))
    y_ssd, hs = _ssd_fwd(xbc, dtr, bias_rows, alog_rows, drep)
    y_mix = _shortconv_fwd(proj, sc_w8, _gated_norm_fwd(y_ssd, proj, ssm_norm_g))
    gath = _split_wait("ag_fwd_wait", fw_ss, fw_rs, fw_bufs, _build_ag_fwd, after=[y_mix])
    w_out_f = gath[0].reshape(2 * D_MODEL, D_MODEL)
    w_gate3 = gath[1].reshape(N_CHIPS, D_MODEL, D_FF // N_CHIPS)
    w_up3 = gath[2].reshape(N_CHIPS, D_MODEL, D_FF // N_CHIPS)
    w_down_f = gath[3].reshape(D_FF, D_MODEL)
    (h1,) = _matmul([(y_mix, w_out_f)], out_dtypes=[F32], name="mm_out", extras=[xt],
                    epilogue=lambda acc, res: (acc + res,))
    n2 = _rmsnorm_fwd(h1, norm_ffn_g, "rmsnorm_ffn")
    g_act, u_act, a_act = _ffn_fwd(n2, w_gate3, w_up3)
    (h2,) = _matmul([(a_act, w_down_f)], out_dtypes=[F32], name="mm_down", extras=[h1],
                    epilogue=lambda acc, res: (acc + res,))

    dh2, dh2b, dg_final, loss_part = _loss_and_final_bwd(h2, tgt, norm_final_g.reshape(1, D_MODEL))
    dg_act, du_act = _matmul([(dh2b, w_down_f)], tb=True, out_dtypes=[BF16, BF16], name="mm_down_bwd",
                             tn=512, extras=[g_act, u_act], epilogue=_swiglu_bwd, nsub=2)
    (dw_down,) = _matmul([(a_act, dh2b)], ta=True, out_dtypes=[F32], name="mm_dw_down", tm=1408)
    (dn2,) = _matmul([(dg_act, w_gate3), (du_act, w_up3)], tb=True, b3d=True, out_dtypes=[BF16],
                     name="mm_ffn_in_bwd")
    (dw_gate,) = _matmul([(n2, dg_act)], ta=True, out_dtypes=[F32], name="mm_dw_gate", tn=1408, col_shards=True)
    (dw_up,) = _matmul([(n2, du_act)], ta=True, out_dtypes=[F32], name="mm_dw_up", tn=1408, col_shards=True)
    dh1, dh1b, dg_ffn = _rmsnorm_bwd(dn2, h1, norm_ffn_g, dh2, "rmsnorm_ffn_bwd")
    (dw_out,) = _matmul([(y_mix, dh1b)], ta=True, out_dtypes=[F32], name="mm_dw_out")

    def halves(g):
        return g.reshape(N_CHIPS, 2, g.shape[1] // 2, g.shape[2])

    def landing(shape, dtype):
        return lax.empty(shape, dtype)

    names1 = names[1:]
    ps1 = [halves(dw_out.reshape(N_CHIPS, -1, D_MODEL)), halves(dw_gate), halves(dw_up),
           halves(dw_down.reshape(N_CHIPS, -1, D_MODEL))]
    r0_1 = [landing((N_CHIPS, 1) + p.shape[2:], F32) for p in ps1]
    sw_ss, sw_rs, sw_arr, sw_tok = _split_start("rs1_swap_start", ps1 + r0_1, _build_rs_swap, 4)
    (dmix,) = _matmul([(dh1b, w_out_f)], tb=True, out_dtypes=[BF16], name="mm_out_bwd", deps=[sw_tok])
    dproj, dw_sc = _shortconv_bwd(dmix, proj, sc_w8)
    dy_ssd, dproj, dg_ssmnorm = _gated_norm_bwd(dmix, y_ssd, proj, ssm_norm_g, dproj)
    sw_arr = _split_wait("rs1_swap_wait", sw_ss, sw_rs, sw_arr, _build_rs_swap, after=[dy_ssd])
    qs1 = [_rs_add_pair(p, r, c_arr, "rs_add_pair_" + nm) for p, r, nm in zip(sw_arr[:4], sw_arr[4:], names1)]
    r1_1 = [landing(q.shape, BF16) for q in qs1]
    ic_ss, ic_rs, ic_arr, ic_tok = _split_start("rs1_ici_start", qs1 + r1_1, _build_rs_ici, 12)
    dxbc_act, ddtr, dbias_acc, dalog_acc, dD_acc = _ssd_bwd(
        xbc, dtr, bias_rows, alog_rows, _tie(drep, ic_tok, "tie_rs1_ici"), dy_ssd, hs)
    dproj, dw_ssmconv, db_ssmconv = _ssm_conv_bwd(dxbc_act, proj, ssm_w8, ssm_bk, dproj)
    dw_ssmconv, db_ssmconv = _from_kernel_xbc(dw_ssmconv), _from_kernel_xbc(db_ssmconv)
    ic_arr = _split_wait("rs1_ici_wait", ic_ss, ic_rs, ic_arr, _build_rs_ici, after=[dproj])
    g1 = [_rs_add_chips(r, q, place_arr, "rs_add_chips_" + nm) for q, r, nm in zip(ic_arr[:4], ic_arr[4:], names1)]
    sh_ss, sh_rs, sh_arr, sh_tok = _split_start("rs1_share_start", g1, _build_rs_share, 4)

    ddt_raw = _pad_cols(ddtr[:, :HEADS_PER_GROUP, :].reshape(N_HEADS, T).T, DT_PAD).astype(BF16)
    (dwt_main,) = _matmul([(dproj, n1)], ta=True, out_dtypes=[F32], name="mm_dw_main", deps=[sh_tok])
    (dwt_dt,) = _matmul([(ddt_raw, n1)], ta=True, out_dtypes=[F32], name="mm_dw_dt")
    p_in = _scatter_rows_to_shards(dwt_main, dwt_dt)
    s2_ss, s2_rs, s2_arr, s2_tok = _split_start(
        "rs2_swap_start", [p_in, landing((N_CHIPS, 1) + p_in.shape[2:], F32)], _build_rs_swap, 1)
    (dn1a,) = _matmul([(dproj, wt_main)], out_dtypes=[F32], name="mm_proj_bwd", deps=[s2_tok])
    g1 = _split_wait("rs1_share_wait", sh_ss, sh_rs, sh_arr, _build_rs_share, after=[dn1a])
    s2_arr = _split_wait("rs2_swap_wait", s2_ss, s2_rs, s2_arr, _build_rs_swap, after=[dn1a])
    q_in = _rs_add_pair(s2_arr[0], s2_arr[1], c_arr, "rs_add_pair_w_in")
    i2_ss, i2_rs, i2_arr, i2_tok = _split_start(
        "rs2_ici_start", [q_in, landing(q_in.shape, BF16)], _build_rs_ici, 3)
    (dn1,) = _matmul([(ddt_raw, wt_dt)], out_dtypes=[BF16], name="mm_proj_dt_bwd", extras=[dn1a],
                     epilogue=lambda acc, res: (acc + res,), deps=[i2_tok])
    dx, _, dg_mix = _rmsnorm_bwd(dn1, xt, norm_mix_g, dh1, "rmsnorm_mix_bwd")

    big_m = [m_w_in[0].T, m_w_out[0], m_w_gate[0], m_w_up[0], m_w_down[0]]
    big_v = [v_w_in[0].T, v_w_out[0], v_w_gate[0], v_w_up[0], v_w_down[0]]
    big_grads = [None] + [g.reshape(w.shape) for g, w in zip(g1, big[1:])]
    big_out = {}
    for k in range(1, 5):
        big_out[names[k]] = _adamw(big[k], big_grads[k], big_m[k], big_v[k], "adamw_" + names[k], deps=[i2_tok])
    i2_arr = _split_wait("rs2_ici_wait", i2_ss, i2_rs, i2_arr, _build_rs_ici, after=[big_out[names[4]][0], dx])
    g_in_red = _rs_add_chips(i2_arr[1], i2_arr[0], place_arr, "rs_add_chips_w_in")
    s3_ss, s3_rs, s3_arr, s3_tok = _split_start("rs2_share_start", [g_in_red], _build_rs_share, 1)

    dD = jnp.sum(dD_acc.reshape(N_HEADS, HEADDIM), axis=-1)
    heads_row = jnp.concatenate([_rows_to_heads(dbias_acc), _rows_to_heads(dalog_acc), dD,
                                 loss_part.reshape(1)]).reshape(1, -1)
    small = jnp.concatenate([
        dw_ssmconv,
        _pad_cols(dw_sc, D_XBC),
        db_ssmconv,
        jnp.concatenate([dg_mix, dg_ssmnorm], axis=1),
        jnp.concatenate([dg_ffn, dg_final], axis=1),
        _pad_cols(heads_row, D_XBC),
        jnp.zeros((4, D_XBC), F32),
    ], axis=0)
    tot = _allreduce_small(small, deps=[s3_tok])
    loss = tot[19, 3 * N_HEADS]

    cs_ssm, cs_sc = D_XBC // N_CHIPS, D_MODEL // N_CHIPS
    g_ssm_conv = lax.dynamic_slice(tot[0:K_SSM], (0, chip * cs_ssm), (K_SSM, cs_ssm))
    g_sc_conv = lax.dynamic_slice(tot[8:8 + K_SC, :D_MODEL], (0, chip * cs_sc), (K_SC, cs_sc))
    small_grads = {
        "norm_mix_g": tot[17:18, :D_MODEL], "ssm_conv_w": g_ssm_conv, "ssm_conv_b": tot[16:17],
        "ssm_dt_bias": tot[19:20, 0:N_HEADS], "ssm_A_log": tot[19:20, N_HEADS:2 * N_HEADS],
        "ssm_D": tot[19:20, 2 * N_HEADS:3 * N_HEADS], "ssm_norm_g": tot[17:18, D_MODEL:],
        "sc_conv_w": g_sc_conv, "norm_ffn_g": tot[18:19, :D_MODEL], "norm_final_g": tot[18:19, D_MODEL:],
    }
    small_w = {"norm_mix_g": (norm_mix_g, m_norm_mix_g, v_norm_mix_g),
               "ssm_conv_w": (ssm_conv_w[0], m_ssm_conv_w[0], v_ssm_conv_w[0]),
               "ssm_conv_b": (ssm_conv_b, m_ssm_conv_b, v_ssm_conv_b),
               "ssm_dt_bias": (ssm_dt_bias, m_ssm_dt_bias, v_ssm_dt_bias),
               "ssm_A_log": (ssm_A_log, m_ssm_A_log, v_ssm_A_log),
               "ssm_D": (ssm_D, m_ssm_D, v_ssm_D),
               "ssm_norm_g": (ssm_norm_g, m_ssm_norm_g, v_ssm_norm_g),
               "sc_conv_w": (sc_conv_w[0], m_sc_conv_w[0], v_sc_conv_w[0]),
               "norm_ffn_g": (norm_ffn_g, m_norm_ffn_g, v_norm_ffn_g),
               "norm_final_g": (norm_final_g.reshape(1, -1), m_norm_final_g.reshape(1, -1),
                                v_norm_final_g.reshape(1, -1))}
    PW = 1024
    order = list(small_w)

    def pack(arrs):
        rows = []
        for a in arrs:
            flat = a.reshape(-1)
            n = -(-flat.shape[0] // PW) * PW
            rows.append(jnp.pad(flat, (0, n - flat.shape[0])).reshape(-1, PW))
        slab = jnp.concatenate(rows, axis=0)
        return _pad_rows(slab, -(-slab.shape[0] // 8) * 8)

    wp = pack([small_w[k][0] for k in order])
    mp = pack([small_w[k][1] for k in order])
    vp = pack([small_w[k][2] for k in order])
    gp = pack([small_grads[k] for k in order])
    sd, sm, sv = _adamw(wp, gp, mp, vp, "adamw_small")

    def unpack(slab):
        out, row = {}, 0
        for k in order:
            shape = small_w[k][0].shape
            size = 1
            for s in shape:
                size *= s
            nr = -(-size // PW)
            out[k] = slab[row:row + nr].reshape(-1)[:size].reshape(shape)
            row += nr
        return out

    s_delta, s_m, s_v = unpack(sd), unpack(sm), unpack(sv)

    (g_in_full,) = _split_wait("rs2_share_wait", s3_ss, s3_rs, s3_arr, _build_rs_share, after=[sd])
    d_t, m_t, v_t, g_t = _adamw(big[0], g_in_full.reshape(2 * HR_IN, D_MODEL), big_m[0], big_v[0],
                                "adamw_" + names[0], emit_g=True)
    big_grads[0] = g_t.T
    big_out[names[0]] = (d_t.T, m_t.T, v_t.T)
    big_g = dict(zip(names, big_grads))

    weight_order = ["norm_mix_g", "w_in", "ssm_conv_w", "ssm_conv_b", "ssm_dt_bias", "ssm_A_log", "ssm_D",
                    "ssm_norm_g", "sc_conv_w", "w_out", "norm_ffn_g", "w_gate", "w_up", "w_down", "norm_final_g"]
    lead = {"ssm_conv_w", "sc_conv_w", "w_in", "w_out", "w_gate", "w_up", "w_down"}

    def shaped(nm, a):
        if nm == "norm_final_g":
            return a.reshape(D_MODEL)
        return a[None] if nm in lead else a

    grads, deltas, new_m, new_v = [], [], [], []
    for nm in weight_order:
        if nm in big_out:
            g, (d, m, v) = big_g[nm], big_out[nm]
        else:
            g, d, m, v = small_grads[nm], s_delta[nm], s_m[nm], s_v[nm]
        grads.append(shaped(nm, g))
        deltas.append(shaped(nm, d))
        new_m.append(shaped(nm, m))
        new_v.append(shaped(nm, v))
    return (loss, dx[None], *grads, *deltas, *new_m, *new_v)


def _swiglu_bwd(da, g, u):
    gf, uf = g.astype(F32), u.astype(F32)
    return da * uf * _dsilu(gf), da * _silu(gf)


def _ffn_fwd(n2, w_gate, w_up):
    T, K = n2.shape
    tn = w_gate.shape[2]
    N = N_CHIPS * tn
    tm = _tile(T, 512)
    sub = _tile(tm, 256)

    def body(a_ref, wg_ref, wu_ref, g_ref, u_ref, act_ref):
        for s in range(tm // sub):
            rows = pl.ds(s * sub, sub)
            a = a_ref[rows, :]
            g = jnp.dot(a, wg_ref[...], preferred_element_type=F32)
            u = jnp.dot(a, wu_ref[...], preferred_element_type=F32)
            g_ref[rows, :] = g.astype(BF16)
            u_ref[rows, :] = u.astype(BF16)
            act_ref[rows, :] = (_silu(g) * u).astype(BF16)

    a_spec = pl.BlockSpec((tm, K), lambda j, i: (i, 0))
    b_spec = pl.BlockSpec((None, K, tn), lambda j, i: (j, 0, 0))
    o_spec = pl.BlockSpec((tm, tn), lambda j, i: (i, j))
    return pl.pallas_call(
        body, name="ffn_fwd", grid=(N // tn, T // tm),
        in_specs=[a_spec, b_spec, b_spec], out_specs=[o_spec] * 3,
        out_shape=[jax.ShapeDtypeStruct((T, N), BF16)] * 3,
        compiler_params=_cparams(("parallel", "parallel")),
    )(n2, w_gate, w_up)
```

```python
import functools

import jax
import jax.numpy as jnp
from jax import lax
from jax.experimental import pallas as pl
from jax.experimental.pallas import tpu as pltpu

F32 = jnp.float32
BF16 = jnp.bfloat16
MESH = pl.DeviceIdType.MESH

D_MODEL = 2048
D_SSM = 2048
HEADDIM = 64
N_HEADS = 32
N_GROUPS = 8
HEADS_PER_GROUP = 4
N_STATE = 128
CHUNK = 128
K_SSM = 4
K_SC = 3
D_XBC = 4096
D_FF = 5632
D_IN = 12320
D_MAIN = 12288
OFF_XBC, OFF_CB, OFF_CC, OFF_CX = 2048, 6144, 8192, 10240
DT_PAD = 128
EPS = 1e-5
N_CHIPS = 4
N_DEV = 8

ADAM_LR = 0.001
ADAM_B1 = 0.9
ADAM_B2 = 0.999
ADAM_EPS = 1e-08
ADAM_WD = 0.01
ADAM_STEP = 10

V7X_VMEM_BYTES = 64 * 1024 * 1024
VMEM_LIMIT = V7X_VMEM_BYTES - 8 * 1024 * 1024


def _cparams(sem=None):
    if sem is None:
        return pltpu.CompilerParams(vmem_limit_bytes=VMEM_LIMIT)
    return pltpu.CompilerParams(dimension_semantics=sem, vmem_limit_bytes=VMEM_LIMIT)


def _tile(dim, pref, unit=128):
    best = None
    t = unit
    while t <= min(dim, pref):
        if dim % t == 0:
            best = t
        t += unit
    return best if best is not None else dim


def _sigmoid(x):
    return 1.0 / (1.0 + jnp.exp(-x))


def _silu(x):
    return x * _sigmoid(x)


def _dsilu(x):
    s = _sigmoid(x)
    return s * (1.0 + x * (1.0 - s))


def _softplus(x):
    return jnp.maximum(x, 0.0) + jnp.log(1.0 + jnp.exp(-jnp.abs(x)))


MATMUL_VMEM_BUDGET = 44 * 1024 * 1024


def _matmul(pairs, *, ta=False, tb=False, out_dtypes, name, tm=1024, tn=1024, tk=None, extras=(), epilogue=None,
            deps=(), col_shards=False, nsub=1, b3d=False):
    a0, b0 = pairs[0]
    M, K = (a0.shape[1], a0.shape[0]) if ta else a0.shape
    if b3d:
        N = b0.shape[1] if tb else b0.shape[0] * b0.shape[2]
        tk, tn = (b0.shape[2], tn) if tb else (tk, b0.shape[2])
    else:
        N = b0.shape[0] if tb else b0.shape[1]
    tm, tn = _tile(M, tm, 8 if M % 128 else 128), _tile(N, tn)
    npair, nex, ndep, nout = len(pairs), len(extras), len(deps), len(out_dtypes)
    if tk is None:
        fixed = 2 * tm * tn * (sum(jnp.dtype(d).itemsize for d in out_dtypes) + sum(e.dtype.itemsize for e in extras))
        tk = K
        while tk > 128 and (K % tk or tk % 128 or
                            fixed + 2 * npair * 2 * tk * (tm + tn) + (tm * tn * 4 if tk < K else 0) > MATMUL_VMEM_BUDGET):
            tk -= 128
    else:
        tk = _tile(K, tk)
    nk = K // tk
    if nk > 1 or tm % nsub or (tm // nsub) % 128:
        nsub = 1
    sub = tm // nsub
    dims = (((0 if ta else 1,), (1 if tb else 0,)), ((), ()))

    def body(*refs):
        a_refs = refs[0:2 * npair:2]
        b_refs = refs[1:2 * npair:2]
        ex_refs = refs[2 * npair:2 * npair + nex]
        o_refs = refs[2 * npair + nex + ndep:2 * npair + nex + ndep + nout]

        def dots(rows):
            s = None
            for a_ref, b_ref in zip(a_refs, b_refs):
                a = a_ref[...] if rows is None else (a_ref[:, rows] if ta else a_ref[rows, :])
                d = lax.dot_general(a, b_ref[...], dims, preferred_element_type=F32)
                s = d if s is None else s + d
            return s

        def finish(r, rows):
            ex = [e[...] if rows is None else e[rows, :] for e in ex_refs]
            outs = (r,) if epilogue is None else epilogue(r, *ex)
            for o_ref, o in zip(o_refs, outs):
                if rows is None:
                    o_ref[...] = o.astype(o_ref.dtype)
                else:
                    o_ref[rows, :] = o.astype(o_ref.dtype)

        if nk == 1:
            for s in range(nsub):
                rows = None if nsub == 1 else pl.ds(s * sub, sub)
                finish(dots(rows), rows)
            return

        acc = refs[-1]
        k = pl.program_id(2)

        @pl.when(k == 0)
        def _():
            acc[...] = dots(None)

        @pl.when(jnp.logical_and(k > 0, k < nk - 1))
        def _():
            acc[...] += dots(None)

        @pl.when(k == nk - 1)
        def _():
            finish(acc[...] + dots(None), None)

    a_spec = pl.BlockSpec((tk, tm), lambda i, j, k: (k, i)) if ta else pl.BlockSpec((tm, tk), lambda i, j, k: (i, k))
    if b3d:
        b_spec = (pl.BlockSpec((None, tn, tk), lambda i, j, k: (k, j, 0)) if tb
                  else pl.BlockSpec((None, tk, tn), lambda i, j, k: (j, k, 0)))
    else:
        b_spec = (pl.BlockSpec((tn, tk), lambda i, j, k: (j, k)) if tb
                  else pl.BlockSpec((tk, tn), lambda i, j, k: (k, j)))
    e_spec = pl.BlockSpec((tm, tn), lambda i, j, k: (i, j))
    if col_shards:
        o_spec = pl.BlockSpec((None, tm, tn), lambda i, j, k: (j, i, 0))
        o_shape = (N // tn, M, tn)
    else:
        o_spec, o_shape = e_spec, (M, N)
    args, in_specs = [], []
    for a, b in pairs:
        args += [a, b]
        in_specs += [a_spec, b_spec]
    args += list(extras) + list(deps)
    in_specs += [e_spec] * nex + [ANY] * ndep
    outs = pl.pallas_call(
        body,
        name=name,
        grid=(M // tm, N // tn, nk),
        in_specs=in_specs,
        out_specs=[o_spec] * nout,
        out_shape=[jax.ShapeDtypeStruct(o_shape, dt) for dt in out_dtypes],
        scratch_shapes=[pltpu.VMEM((tm, tn), F32)] if nk > 1 else [],
        compiler_params=_cparams(("parallel", "parallel", "arbitrary")),
    )(*args)
    return outs


def _cast_into_gather(w, chip_arr, name, split_cols=False):
    R, C = w.shape
    hr, hc = (R, C // 2) if split_cols else (R // 2, C)
    tr = _tile(hr, 512, 8)
    nb = hr // tr

    def body(chip_ref, w_ref, o_ref):
        o_ref[...] = w_ref[...].astype(BF16)

    in_map = (lambda h, i, chip_ref: (i, h)) if split_cols else (lambda h, i, chip_ref: (h * nb + i, 0))
    grid_spec = pltpu.PrefetchScalarGridSpec(
        num_scalar_prefetch=1, grid=(2, nb),
        in_specs=[pl.BlockSpec((tr, hc), in_map)],
        out_specs=pl.BlockSpec((None, tr, hc), lambda h, i, chip_ref: (2 * chip_ref[0] + h, i, 0)))
    return pl.pallas_call(
        body, name=name, grid_spec=grid_spec,
        out_shape=jax.ShapeDtypeStruct((N_DEV, hr, hc), BF16),
        compiler_params=_cparams(("parallel", "parallel")),
    )(chip_arr, w)


def _tie(small, token, name):
    def body(s_ref, t_ref, o_ref):
        o_ref[...] = s_ref[...]

    vm = pl.BlockSpec(memory_space=pltpu.VMEM)
    return pl.pallas_call(body, name=name, in_specs=[vm, ANY], out_specs=vm,
                          out_shape=jax.ShapeDtypeStruct(small.shape, small.dtype))(small, token)


def _rmsnorm_fwd(x, g, name):
    T, D = x.shape
    tt = _tile(T, 256)

    def body(x_ref, g_ref, n_ref):
        xv = x_ref[...]
        r = lax.rsqrt(jnp.mean(xv * xv, axis=-1, keepdims=True) + EPS)
        n_ref[...] = (xv * r * g_ref[...]).astype(BF16)

    return pl.pallas_call(
        body, name=name, grid=(T // tt,),
        in_specs=[pl.BlockSpec((tt, D), lambda i: (i, 0)), pl.BlockSpec((1, D), lambda i: (0, 0))],
        out_specs=pl.BlockSpec((tt, D), lambda i: (i, 0)),
        out_shape=jax.ShapeDtypeStruct((T, D), BF16),
        compiler_params=_cparams(("parallel",)),
    )(x, g)


def _rmsnorm_bwd(dn, x, g, res, name):
    T, D = x.shape
    tt = _tile(T, 256)

    def body(dn_ref, x_ref, g_ref, res_ref, dx_ref, dxb_ref, dg_ref):
        @pl.when(pl.program_id(0) == 0)
        def _():
            dg_ref[...] = jnp.zeros_like(dg_ref)

        xv = x_ref[...]
        dy = dn_ref[...].astype(F32)
        r = lax.rsqrt(jnp.mean(xv * xv, axis=-1, keepdims=True) + EPS)
        xhat = xv * r
        dxh = dy * g_ref[...]
        dx = res_ref[...] + r * (dxh - xhat * jnp.mean(dxh * xhat, axis=-1, keepdims=True))
        dx_ref[...] = dx
        dxb_ref[...] = dx.astype(BF16)
        dg_ref[...] += jnp.sum(dy * xhat, axis=0, keepdims=True)

    tok = pl.BlockSpec((tt, D), lambda i: (i, 0))
    vec = pl.BlockSpec((1, D), lambda i: (0, 0))
    return pl.pallas_call(
        body, name=name, grid=(T // tt,),
        in_specs=[tok, tok, vec, tok],
        out_specs=[tok, tok, vec],
        out_shape=[jax.ShapeDtypeStruct((T, D), F32), jax.ShapeDtypeStruct((T, D), BF16),
                   jax.ShapeDtypeStruct((1, D), F32)],
        compiler_params=_cparams(("arbitrary",)),
    )(dn, x, g, res)


def _loss_and_final_bwd(h2, target, gf):
    T, D = h2.shape
    tt = _tile(T, 256)

    def body(h_ref, t_ref, g_ref, dh_ref, dhb_ref, dg_ref, loss_ref):
        @pl.when(pl.program_id(0) == 0)
        def _():
            dg_ref[...] = jnp.zeros_like(dg_ref)
            loss_ref[...] = jnp.zeros_like(loss_ref)

        xv = h_ref[...]
        r = lax.rsqrt(jnp.mean(xv * xv, axis=-1, keepdims=True) + EPS)
        xhat = xv * r
        err = xhat * g_ref[...] - t_ref[...]
        loss_ref[...] += 0.5 * jnp.sum(jnp.mean(err * err, axis=-1, keepdims=True), axis=0, keepdims=True)
        dy = err * (1.0 / D)
        dxh = dy * g_ref[...]
        dx = r * (dxh - xhat * jnp.mean(dxh * xhat, axis=-1, keepdims=True))
        dh_ref[...] = dx
        dhb_ref[...] = dx.astype(BF16)
        dg_ref[...] += jnp.sum(dy * xhat, axis=0, keepdims=True)

    tok = pl.BlockSpec((tt, D), lambda i: (i, 0))
    vec = pl.BlockSpec((1, D), lambda i: (0, 0))
    return pl.pallas_call(
        body, name="loss_final_bwd", grid=(T // tt,),
        in_specs=[tok, tok, vec],
        out_specs=[tok, tok, vec, pl.BlockSpec((1, 1), lambda i: (0, 0))],
        out_shape=[jax.ShapeDtypeStruct((T, D), F32), jax.ShapeDtypeStruct((T, D), BF16),
                   jax.ShapeDtypeStruct((1, D), F32), jax.ShapeDtypeStruct((1, 1), F32)],
        compiler_params=_cparams(("arbitrary",)),
    )(h2, target, gf)


def _gated_norm_fwd(y, proj, g):
    T, D = y.shape
    tt = _tile(T, 256)

    def body(y_ref, z_ref, g_ref, o_ref):
        yg = y_ref[...] * _silu(z_ref[...])
        r = lax.rsqrt(jnp.mean(yg * yg, axis=-1, keepdims=True) + EPS)
        o_ref[...] = (yg * r * g_ref[...]).astype(BF16)

    tok = pl.BlockSpec((tt, D), lambda i: (i, 0))
    return pl.pallas_call(
        body, name="gated_norm_fwd", grid=(T // tt,),
        in_specs=[tok, tok, pl.BlockSpec((1, D), lambda i: (0, 0))],
        out_specs=tok,
        out_shape=jax.ShapeDtypeStruct((T, 2 * D_MODEL), BF16),
        compiler_params=_cparams(("parallel",)),
    )(y, proj, g)


def _gated_norm_bwd(dmix, y, proj, g, dproj):
    T, D = y.shape
    tt = _tile(T, 256)

    def body(do_ref, y_ref, z_ref, g_ref, dp_ref, dy_ref, dz_ref, dg_ref):
        @pl.when(pl.program_id(0) == 0)
        def _():
            dg_ref[...] = jnp.zeros_like(dg_ref)

        yv, zv = y_ref[...], z_ref[...]
        do = do_ref[...].astype(F32)
        sz = _silu(zv)
        yg = yv * sz
        r = lax.rsqrt(jnp.mean(yg * yg, axis=-1, keepdims=True) + EPS)
        xhat = yg * r
        dxh = do * g_ref[...]
        dyg = r * (dxh - xhat * jnp.mean(dxh * xhat, axis=-1, keepdims=True))
        dy_ref[...] = dyg * sz
        dz_ref[...] = (dyg * yv * _dsilu(zv)).astype(BF16)
        dg_ref[...] += jnp.sum(do * xhat, axis=0, keepdims=True)

    tok = pl.BlockSpec((tt, D), lambda i: (i, 0))
    vec = pl.BlockSpec((1, D), lambda i: (0, 0))
    return pl.pallas_call(
        body, name="gated_norm_bwd", grid=(T // tt,),
        in_specs=[tok, tok, tok, vec, ANY],
        out_specs=[tok, tok, vec],
        out_shape=[jax.ShapeDtypeStruct((T, D), F32), jax.ShapeDtypeStruct(dproj.shape, BF16),
                   jax.ShapeDtypeStruct((1, D), F32)],
        input_output_aliases={4: 1},
        compiler_params=_cparams(("arbitrary",)),
    )(dmix, y, proj, g, dproj)


HALO = 8


def _shift_down(cur, prev8, s):
    ext = jnp.concatenate([prev8, cur], axis=0)
    return pltpu.roll(ext, s, axis=0)[HALO:]


def _shift_up(cur, next8, s):
    n = cur.shape[0]
    ext = jnp.concatenate([cur, next8], axis=0)
    return pltpu.roll(ext, n + HALO - s, axis=0)[:n]


def _conv_specs(tt, cb, col_off_blocks, nt):
    hb = tt // HALO
    cur = pl.BlockSpec((tt, cb), lambda j, i: (i, col_off_blocks + j))
    prev = pl.BlockSpec((HALO, cb), lambda j, i: (jnp.maximum(i * hb - 1, 0), col_off_blocks + j))
    nxt = pl.BlockSpec((HALO, cb), lambda j, i: (jnp.minimum((i + 1) * hb, nt * hb - 1), col_off_blocks + j))
    return cur, prev, nxt


def _causal_conv(cur, prev8, w, K):
    y = cur * w[K - 1:K, :]
    for k in range(K - 1):
        y = y + _shift_down(cur, prev8, K - 1 - k) * w[k:k + 1, :]
    return y


def _anticausal_conv(cur, next8, w, K):
    y = cur * w[K - 1:K, :]
    for k in range(K - 1):
        y = y + _shift_up(cur, next8, K - 1 - k) * w[k:k + 1, :]
    return y


def _ssm_conv_fwd(proj, w8, b):
    T = proj.shape[0]
    tt, cb = _tile(T, 512), 512
    nt = T // tt
    cur, prev, _ = _conv_specs(tt, cb, OFF_XBC // cb, nt)

    def body(u_ref, up_ref, w_ref, b_ref, o_ref):
        first = pl.program_id(1) == 0
        p8 = jnp.where(first, 0.0, up_ref[...])
        pre = _causal_conv(u_ref[...], p8, w_ref[...], K_SSM) + b_ref[...]
        o_ref[...] = _silu(pre)

    return pl.pallas_call(
        body, name="ssm_conv_fwd", grid=(D_XBC // cb, nt),
        in_specs=[cur, prev, pl.BlockSpec((8, cb), lambda j, i: (0, j)), pl.BlockSpec((1, cb), lambda j, i: (0, j))],
        out_specs=pl.BlockSpec((tt, cb), lambda j, i: (i, j)),
        out_shape=jax.ShapeDtypeStruct((T, D_XBC), F32),
        compiler_params=_cparams(("parallel", "parallel")),
    )(proj, proj, w8, b)


def _ssm_conv_bwd(dact, proj, w8, b, dproj):
    T = proj.shape[0]
    tt, cb = _tile(T, 512), 512
    nt = T // tt
    cur, prev, nxt = _conv_specs(tt, cb, OFF_XBC // cb, nt)
    dcur, dprev, dnxt = _conv_specs(tt, cb, 0, nt)

    def dpre_of(d, u, p8, w, bb):
        pre = _causal_conv(u, p8, w, K_SSM) + bb
        return d * _dsilu(pre)

    def body(d_ref, dn_ref, u_ref, up_ref, un_ref, w_ref, b_ref, dp_ref, dx_ref, dw_ref, db_ref):
        i = pl.program_id(1)

        @pl.when(i == 0)
        def _():
            dw_ref[...] = jnp.zeros_like(dw_ref)
            db_ref[...] = jnp.zeros_like(db_ref)

        w, bb = w_ref[...], b_ref[...]
        u = u_ref[...]
        p8 = jnp.where(i == 0, 0.0, up_ref[...])
        dpre = dpre_of(d_ref[...], u, p8, w, bb)
        un = un_ref[...]
        dpre_n = dpre_of(dn_ref[...], un, u[tt - HALO:, :], w, bb)
        dpre_n = jnp.where(i == nt - 1, 0.0, dpre_n)
        dx_ref[...] = _anticausal_conv(dpre, dpre_n, w, K_SSM).astype(BF16)
        rows = [jnp.sum(dpre * _shift_down(u, p8, K_SSM - 1 - k), axis=0, keepdims=True) for k in range(K_SSM - 1)]
        rows.append(jnp.sum(dpre * u, axis=0, keepdims=True))
        rows.append(jnp.zeros((8 - K_SSM, cb), F32))
        dw_ref[...] += jnp.concatenate(rows, axis=0)
        db_ref[...] += jnp.sum(dpre, axis=0, keepdims=True)

    wspec = pl.BlockSpec((8, cb), lambda j, i: (0, j))
    bspec = pl.BlockSpec((1, cb), lambda j, i: (0, j))
    return pl.pallas_call(
        body, name="ssm_conv_bwd", grid=(D_XBC // cb, nt),
        in_specs=[dcur, dnxt, cur, prev, nxt, wspec, bspec, ANY],
        out_specs=[pl.BlockSpec((tt, cb), lambda j, i: (i, OFF_XBC // cb + j)), wspec, bspec],
        out_shape=[jax.ShapeDtypeStruct(dproj.shape, BF16), jax.ShapeDtypeStruct((8, D_XBC), F32),
                   jax.ShapeDtypeStruct((1, D_XBC), F32)],
        input_output_aliases={7: 0},
        compiler_params=_cparams(("parallel", "arbitrary")),
    )(dact, dact, proj, proj, proj, w8, b, dproj)


SCB = 512
SC3 = 3 * SCB


def _sc_specs(tt, nt):
    hb = tt // HALO
    cur = pl.BlockSpec((tt, SC3), lambda j, i: (i, OFF_CB // SC3 + j))
    prev = pl.BlockSpec((HALO, SC3), lambda j, i: (jnp.maximum(i * hb - 1, 0), OFF_CB // SC3 + j))
    nxt = pl.BlockSpec((HALO, SC3), lambda j, i: (jnp.minimum((i + 1) * hb, nt * hb - 1), OFF_CB // SC3 + j))
    return cur, prev, nxt


def _shortconv_fwd(proj, w8, ymix):
    T = proj.shape[0]
    tt = _tile(T, 512)
    nt = T // tt
    cur, prev, _ = _sc_specs(tt, nt)

    def body(p_ref, pp_ref, w_ref, y_ref, o_ref):
        p, pp = p_ref[...], pp_ref[...]
        v = p[:, SCB:2 * SCB] * p[:, 2 * SCB:]
        vp = jnp.where(pl.program_id(1) == 0, 0.0, pp[:, SCB:2 * SCB] * pp[:, 2 * SCB:])
        o_ref[...] = (p[:, :SCB] * _causal_conv(v, vp, w_ref[...], K_SC)).astype(BF16)

    return pl.pallas_call(
        body, name="shortconv_fwd", grid=(D_MODEL // SCB, nt),
        in_specs=[cur, prev, pl.BlockSpec((8, SCB), lambda j, i: (0, j)), ANY],
        out_specs=pl.BlockSpec((tt, SCB), lambda j, i: (i, D_SSM // SCB + j)),
        out_shape=jax.ShapeDtypeStruct(ymix.shape, BF16),
        input_output_aliases={3: 0},
        compiler_params=_cparams(("parallel", "parallel")),
    )(proj, proj, w8, ymix)


def _shortconv_bwd(dmix, proj, w8):
    T = proj.shape[0]
    tt = _tile(T, 512)
    nt = T // tt
    hb = tt // HALO
    cur, prev, nxt = _sc_specs(tt, nt)
    d_s = pl.BlockSpec((tt, SCB), lambda j, i: (i, D_SSM // SCB + j))
    dn_s = pl.BlockSpec((HALO, SCB), lambda j, i: (jnp.minimum((i + 1) * hb, nt * hb - 1), D_SSM // SCB + j))

    def body(d_ref, dn_ref, p_ref, pp_ref, pn_ref, w_ref, dp_ref, dw_ref):
        i = pl.program_id(1)

        @pl.when(i == 0)
        def _():
            dw_ref[...] = jnp.zeros_like(dw_ref)

        w = w_ref[...]
        p, pp = p_ref[...], pp_ref[...]
        gb, gc, u = p[:, :SCB], p[:, SCB:2 * SCB], p[:, 2 * SCB:]
        v = gc * u
        vp = jnp.where(i == 0, 0.0, pp[:, SCB:2 * SCB] * pp[:, 2 * SCB:])
        d = d_ref[...].astype(F32)
        dp_ref[:, :SCB] = (d * _causal_conv(v, vp, w, K_SC)).astype(BF16)
        dcv = d * gb
        dcv_n = jnp.where(i == nt - 1, 0.0, dn_ref[...].astype(F32) * pn_ref[:, :SCB])
        dv = _anticausal_conv(dcv, dcv_n, w, K_SC)
        dp_ref[:, SCB:2 * SCB] = (dv * u).astype(BF16)
        dp_ref[:, 2 * SCB:] = (dv * gc).astype(BF16)
        rows = [jnp.sum(dcv * _shift_down(v, vp, K_SC - 1 - k), axis=0, keepdims=True) for k in range(K_SC - 1)]
        rows.append(jnp.sum(dcv * v, axis=0, keepdims=True))
        rows.append(jnp.zeros((8 - K_SC, SCB), F32))
        dw_ref[...] += jnp.concatenate(rows, axis=0)

    wspec = pl.BlockSpec((8, SCB), lambda j, i: (0, j))
    return pl.pallas_call(
        body, name="shortconv_bwd", grid=(D_MODEL // SCB, nt),
        in_specs=[d_s, dn_s, cur, prev, nxt, wspec],
        out_specs=[cur, wspec],
        out_shape=[jax.ShapeDtypeStruct((T, D_MAIN), BF16), jax.ShapeDtypeStruct((8, D_MODEL), F32)],
        compiler_params=_cparams(("parallel", "arbitrary")),
    )(dmix, dmix, proj, proj, proj, w8)


GW = HEADS_PER_GROUP * HEADDIM


def _dot(a, b):
    return jnp.dot(a.astype(BF16), b.astype(BF16), preferred_element_type=F32)


def _dot_nt(a, b):
    return lax.dot_general(a.astype(BF16), b.astype(BF16), (((1,), (1,)), ((), ())), preferred_element_type=F32)


def _dot_tn(a, b):
    return lax.dot_general(a.astype(BF16), b.astype(BF16), (((0,), (0,)), ((), ())), preferred_element_type=F32)


def _bf16_terms(x, n):
    terms, r = [], x
    for _ in range(n):
        t = r.astype(BF16)
        terms.append(t)
        r = r - t.astype(F32)
    return terms


def _dot_sel(a, sel, n=2):
    s = sel.astype(BF16)
    return sum(jnp.dot(t, s, preferred_element_type=F32) for t in _bf16_terms(a, n))


def _sel_dot(sel, b, n=2):
    s = sel.astype(BF16)
    return sum(jnp.dot(s, t, preferred_element_type=F32) for t in _bf16_terms(b, n))


def _sel_dot_nt(sel, b, n=2):
    s = sel.astype(BF16)
    return sum(lax.dot_general(s, t, (((1,), (1,)), ((), ())), preferred_element_type=F32)
               for t in _bf16_terms(b, n))


def _head_cols(rows):
    parts = [jnp.broadcast_to(rows[r:r + 1, :], (HEADDIM, CHUNK)) for r in range(HEADS_PER_GROUP)]
    return jnp.concatenate(parts, axis=0).T


def _head_rows(rows):
    parts = [jnp.broadcast_to(rows[r:r + 1, :], (HEADDIM, N_STATE)) for r in range(HEADS_PER_GROUP)]
    return jnp.concatenate(parts, axis=0)


def _ssd_common(dtr, bias, alog):
    dt = _softplus(dtr + bias)
    A = -jnp.exp(alog)
    a = dt * A
    ki = lax.broadcasted_iota(jnp.int32, (CHUNK, CHUNK), 0)
    si = lax.broadcasted_iota(jnp.int32, (CHUNK, CHUNK), 1)
    upper = (ki <= si).astype(F32)
    cs = _dot_sel(a, upper, 3)
    cs_last = jnp.broadcast_to(cs[:, CHUNK - 1:CHUNK], (8, CHUNK))
    return dt, A, a, cs, cs_last


def _decay_matrix(cs, r):
    li = lax.broadcasted_iota(jnp.int32, (CHUNK, CHUNK), 0)
    si = lax.broadcasted_iota(jnp.int32, (CHUNK, CHUNK), 1)
    causal = li >= si
    R = jnp.broadcast_to(cs[r:r + 1, :], (CHUNK, CHUNK))
    seg = jnp.where(causal, R.T - R, 0.0)
    return jnp.where(causal, jnp.exp(seg), 0.0)


GXBC = GW + 2 * N_STATE


GS = 2


def _ssd_in_specs(nc, rev):
    cix = (lambda c: nc - 1 - c) if rev else (lambda c: c)
    x_s = pl.BlockSpec((CHUNK, GS * GW), lambda g, c: (cix(c), g))
    xbc_s = pl.BlockSpec((CHUNK, GS * GXBC), lambda g, c: (cix(c), g))
    dtr_s = pl.BlockSpec((GS, 8, CHUNK), lambda g, c: (g, 0, cix(c)))
    row_s = pl.BlockSpec((GS, 8, CHUNK), lambda g, c: (g, 0, 0))
    drep_s = pl.BlockSpec((1, GS * GW), lambda g, c: (0, g))
    hs_s = pl.BlockSpec((1, GS * GW, N_STATE), lambda g, c: (cix(c), g, 0))
    return x_s, xbc_s, dtr_s, row_s, drep_s, hs_s


def _xbc_parts(xbc_ref, gi):
    o = gi * GXBC
    return xbc_ref[:, o:o + GW], xbc_ref[:, o + GW:o + GW + N_STATE], xbc_ref[:, o + GW + N_STATE:o + GXBC]


def _ssd_fwd(xbc, dtr, bias, alog, drep):
    T = xbc.shape[0]
    nc = T // CHUNK
    x_s, xbc_s, dtr_s, row_s, drep_s, hs_s = _ssd_in_specs(nc, False)

    def body(xbc_ref, dtr_ref, bias_ref, alog_ref, drep_ref, y_ref, hs_ref, h_scr):
        @pl.when(pl.program_id(1) == 0)
        def _():
            h_scr[...] = jnp.zeros_like(h_scr)

        for gi in range(GS):
            cols, rows = slice(gi * GW, (gi + 1) * GW), pl.ds(gi * GW, GW)
            x, Bm, Cm = _xbc_parts(xbc_ref, gi)
            dt, A, a, cs, cs_last = _ssd_common(dtr_ref[gi], bias_ref[gi], alog_ref[gi])
            E = _head_cols(jnp.exp(cs))
            W = _head_cols(jnp.exp(cs_last - cs) * dt)
            X = (x * _head_cols(dt)).astype(BF16)
            CB = _dot_nt(Cm, Bm)
            col = lax.broadcasted_iota(jnp.int32, (CHUNK, GW), 1) // HEADDIM
            y = jnp.zeros((CHUNK, GW), F32)
            for r in range(HEADS_PER_GROUP):
                M = CB * _decay_matrix(cs, r)
                y = y + jnp.where(col == r, _dot(M, X), 0.0)
            h = h_scr[rows, :]
            hs_ref[0, rows, :] = h
            y = y + _dot_nt(Cm, h) * E
            y_ref[:, cols] = y + drep_ref[:, cols] * x
            h_scr[rows, :] = h * _head_rows(jnp.exp(cs_last)) + _dot_tn(x * W, Bm)

    return pl.pallas_call(
        body, name="ssd_fwd", grid=(N_GROUPS // GS, nc),
        in_specs=[xbc_s, dtr_s, row_s, row_s, drep_s],
        out_specs=[x_s, hs_s],
        out_shape=[jax.ShapeDtypeStruct((T, D_SSM), F32), jax.ShapeDtypeStruct((nc, D_SSM, N_STATE), F32)],
        scratch_shapes=[pltpu.VMEM((GS * GW, N_STATE), F32)],
        compiler_params=_cparams(("parallel", "arbitrary")),
    )(xbc, dtr, bias, alog, drep)


def _ssd_bwd(xbc, dtr, bias, alog, drep, dy, hs):
    T = xbc.shape[0]
    nc = T // CHUNK
    x_s, xbc_s, dtr_s, row_s, drep_s, hs_s = _ssd_in_specs(nc, True)

    def body(xbc_ref, dtr_ref, bias_ref, alog_ref, drep_ref, dy_ref, hs_ref,
             dxbc_ref, ddtr_ref, dbias_ref, dalog_ref, dd_ref, dh_scr):
        @pl.when(pl.program_id(1) == 0)
        def _():
            dh_scr[...] = jnp.zeros_like(dh_scr)
            dbias_ref[...] = jnp.zeros_like(dbias_ref)
            dalog_ref[...] = jnp.zeros_like(dalog_ref)
            dd_ref[...] = jnp.zeros_like(dd_ref)

        for gi in range(GS):
            one_group(gi, xbc_ref, dtr_ref, bias_ref, alog_ref, drep_ref, dy_ref, hs_ref,
                      dxbc_ref, ddtr_ref, dbias_ref, dalog_ref, dd_ref, dh_scr)

    def one_group(gi, xbc_ref, dtr_ref, bias_ref, alog_ref, drep_ref, dy_ref, hs_ref,
                  dxbc_ref, ddtr_ref, dbias_ref, dalog_ref, dd_ref, dh_scr):
        cols, rows, o = slice(gi * GW, (gi + 1) * GW), pl.ds(gi * GW, GW), gi * GXBC
        x, Bm, Cm = _xbc_parts(xbc_ref, gi)
        dY = dy_ref[:, cols]
        dt, A, a, cs, cs_last = _ssd_common(dtr_ref[gi], bias_ref[gi], alog_ref[gi])
        E = _head_cols(jnp.exp(cs))
        DT = _head_cols(dt)
        Wd = _head_cols(jnp.exp(cs_last - cs))
        X = x * DT
        h = hs_ref[0, rows, :]
        dS = dh_scr[rows, :]
        CB = _dot_nt(Cm, Bm)
        col = lax.broadcasted_iota(jnp.int32, (CHUNK, GW), 1) // HEADDIM
        rowid = lax.broadcasted_iota(jnp.int32, (8, CHUNK), 0)
        lane = lax.broadcasted_iota(jnp.int32, (8, CHUNK), 1)
        hsel = (lax.broadcasted_iota(jnp.int32, (8, GW), 1) // HEADDIM
                == lax.broadcasted_iota(jnp.int32, (8, GW), 0)).astype(F32)
        ones8 = jnp.ones((8, CHUNK), F32)

        dX = jnp.zeros((CHUNK, GW), F32)
        dCB = jnp.zeros((CHUNK, CHUNK), F32)
        dcs = jnp.zeros((8, CHUNK), F32)
        for r in range(HEADS_PER_GROUP):
            L = _decay_matrix(cs, r)
            M = CB * L
            G = _dot_nt(jnp.where(col == r, dY, 0.0), X)
            GL = G * L
            dCB = dCB + GL
            Wm = GL * CB
            colsum = jnp.sum(Wm, axis=0, keepdims=True)
            rowsum = _sel_dot_nt(ones8, Wm)
            dcs = dcs + jnp.where(rowid == r, rowsum - colsum, 0.0)
            dX = dX + jnp.where(col == r, _dot_tn(M, dY), 0.0)
        dC = _dot(dCB, Bm)
        dB = _dot_tn(dCB, Cm)
        T1 = _dot_nt(Bm, dS)
        dX = dX + T1 * Wd
        dB = dB + _dot(X * Wd, dS)
        pdec = _sel_dot_nt(hsel, X * T1 * Wd)
        dcs = dcs - pdec
        dlast = jnp.sum(pdec, axis=1, keepdims=True) \
            + jnp.exp(cs_last[:, 0:1]) * jnp.sum(_sel_dot(hsel, dS * h), axis=1, keepdims=True)
        dYE = dY * E
        dC = dC + _dot(dYE, h)
        yoff = _dot_nt(Cm, h) * E
        dcs = dcs + _sel_dot_nt(hsel, dY * yoff)
        dcs = dcs + jnp.where(lane == CHUNK - 1, dlast, 0.0)
        ki = lax.broadcasted_iota(jnp.int32, (CHUNK, CHUNK), 0)
        si = lax.broadcasted_iota(jnp.int32, (CHUNK, CHUNK), 1)
        lower = (ki >= si).astype(F32)
        da = _dot_sel(dcs, lower)
        ddt = da * A + _sel_dot_nt(hsel, dX * x)
        ddtr = ddt * _sigmoid(dtr_ref[gi] + bias_ref[gi])
        ddtr_ref[gi] = ddtr
        dbias_ref[gi] += ddtr
        dalog_ref[gi] += da * a
        dxbc_ref[:, o:o + GW] = dX * DT + drep_ref[:, cols] * dY
        dd_ref[:, cols] += jnp.sum(dY * x, axis=0, keepdims=True)
        dxbc_ref[:, o + GW:o + GW + N_STATE] = dB
        dxbc_ref[:, o + GW + N_STATE:o + GXBC] = dC
        dh_scr[rows, :] = dS * _head_rows(jnp.exp(cs_last)) + _dot_tn(dYE, Cm)

    return pl.pallas_call(
        body, name="ssd_bwd", grid=(N_GROUPS // GS, nc),
        in_specs=[xbc_s, dtr_s, row_s, row_s, drep_s, x_s, hs_s],
        out_specs=[xbc_s, dtr_s, row_s, row_s, drep_s],
        out_shape=[jax.ShapeDtypeStruct((T, D_XBC), F32),
                   jax.ShapeDtypeStruct((N_GROUPS, 8, T), F32),
                   jax.ShapeDtypeStruct((N_GROUPS, 8, CHUNK), F32),
                   jax.ShapeDtypeStruct((N_GROUPS, 8, CHUNK), F32),
                   jax.ShapeDtypeStruct((1, D_SSM), F32)],
        scratch_shapes=[pltpu.VMEM((GS * GW, N_STATE), F32)],
        compiler_params=_cparams(("parallel", "arbitrary")),
    )(xbc, dtr, bias, alog, drep, dy, hs)


def _adamw(w, g, m, v, name, deps=(), emit_g=False):
    R, C = w.shape
    tr = _tile(R, 256, 8)
    nd = len(deps)
    nout = 4 if emit_g else 3

    def body(w_ref, g_ref, m_ref, v_ref, *rest):
        outs = rest[nd:]
        gv = g_ref[...]
        mn = ADAM_B1 * m_ref[...] + (1.0 - ADAM_B1) * gv
        vn = ADAM_B2 * v_ref[...] + (1.0 - ADAM_B2) * (gv * gv)
        m_hat = mn / (1.0 - ADAM_B1 ** ADAM_STEP)
        v_hat = vn / (1.0 - ADAM_B2 ** ADAM_STEP)
        outs[0][...] = -ADAM_LR * (m_hat / (jnp.sqrt(v_hat) + ADAM_EPS) + ADAM_WD * w_ref[...])
        outs[1][...] = mn
        outs[2][...] = vn
        if emit_g:
            outs[3][...] = gv

    spec = pl.BlockSpec((tr, C), lambda i: (i, 0))
    return pl.pallas_call(
        body, name=name, grid=(R // tr,),
        in_specs=[spec] * 4 + [ANY] * nd, out_specs=[spec] * nout,
        out_shape=[jax.ShapeDtypeStruct((R, C), F32)] * nout,
        compiler_params=_cparams(("parallel",)),
    )(w, g, m, v, *deps)


ANY = pl.BlockSpec(memory_space=pl.ANY)


def _place():
    x, y, c = lax.axis_index("x"), lax.axis_index("y"), lax.axis_index("c")
    return x, y, c


def _other_chips(x, y):
    return [(1 - x, y), (x, 1 - y), (1 - x, 1 - y)]


def _allgather_inplace(bufs):
    n = len(bufs)

    def body(*refs):
        o_refs = refs[n:2 * n]
        send_sems, recv_sems = refs[2 * n:]
        x, y, c = _place()
        sibling = (x, y, 1 - c)
        chips = _other_chips(x, y)

        def copy(k, slot, px, py, pc, to):
            blk = o_refs[k].at[4 * px + 2 * py + pc]
            return pltpu.make_async_remote_copy(
                src_ref=blk, dst_ref=blk, send_sem=send_sems.at[k, slot], recv_sem=recv_sems.at[k, slot],
                device_id=to, device_id_type=MESH)

        sent = []
        for k in range(n):
            for j, (px, py) in enumerate(chips):
                cp = copy(k, j, x, y, c, (px, py, c))
                cp.start()
                sent.append(cp)
        for k in range(n):
            for j, (px, py) in enumerate(chips):
                copy(k, j, px, py, c, (px, py, c)).wait_recv()
                fwd = copy(k, 3 + j, px, py, c, sibling)
                fwd.start()
                sent.append(fwd)
        for k in range(n):
            for j, (px, py) in enumerate(chips):
                copy(k, 3 + j, px, py, 1 - c, sibling).wait_recv()
        for cp in sent:
            cp.wait_send()

    return pl.pallas_call(
        body, name="allgather_w_in",
        in_specs=[ANY] * n, out_specs=[ANY] * n,
        out_shape=[jax.ShapeDtypeStruct(b.shape, b.dtype) for b in bufs],
        input_output_aliases={k: k for k in range(n)},
        scratch_shapes=[pltpu.SemaphoreType.DMA((n, 6)), pltpu.SemaphoreType.DMA((n, 6))],
    )(*bufs)


HBM = pl.BlockSpec(memory_space=pltpu.HBM)
SEM = pl.BlockSpec(memory_space=pltpu.SEMAPHORE)
EFFECT = pltpu.SideEffectType.DATAFLOW_SIDE_EFFECTING


def _split_start(name, arrays, build, n_copies, after=()):
    na, nd = len(arrays), len(after)

    def body(*refs):
        send_sems, recv_sems = refs[na + nd], refs[na + nd + 1]
        for cp in build(refs[:na], send_sems, recv_sems):
            cp.start()
        refs[-1][...] = jnp.zeros((8, 128), F32)

    outs = pl.pallas_call(
        body, name=name,
        out_shape=(pltpu.SemaphoreType.DMA((n_copies,)), pltpu.SemaphoreType.DMA((n_copies,)),
                   *[pltpu.HBM(a.shape, a.dtype) for a in arrays], jax.ShapeDtypeStruct((8, 128), F32)),
        in_specs=[HBM] * na + [ANY] * nd,
        out_specs=(SEM, SEM, *[HBM] * na, pl.BlockSpec(memory_space=pltpu.VMEM)),
        input_output_aliases={i: 2 + i for i in range(na)},
        compiler_params=pltpu.CompilerParams(has_side_effects=EFFECT),
    )(*[pltpu.with_memory_space_constraint(a, pltpu.HBM) for a in arrays], *after)
    return outs[0], outs[1], list(outs[2:2 + na]), outs[-1]


def _split_wait(name, send_sems, recv_sems, arrays, build, after):
    na = len(arrays)

    def body(*refs):
        for cp in build(refs[:na], refs[na], refs[na + 1]):
            cp.wait_send()
            cp.wait_recv()

    outs = pl.pallas_call(
        body, name=name,
        out_shape=tuple(pltpu.HBM(a.shape, a.dtype) for a in arrays),
        in_specs=[HBM] * na + [SEM, SEM] + [ANY] * len(after),
        out_specs=tuple([HBM] * na),
        input_output_aliases={i: i for i in range(na)},
        compiler_params=pltpu.CompilerParams(has_side_effects=EFFECT),
    )(*arrays, send_sems, recv_sems, *after)
    return list(outs)


def _remote(src, dst, send_sems, recv_sems, i, to):
    return pltpu.make_async_remote_copy(src_ref=src, dst_ref=dst, send_sem=send_sems.at[i], recv_sem=recv_sems.at[i],
                                        device_id=to, device_id_type=MESH)


def _build_ag_ici(refs, ss, rs):
    x, y, c = _place()
    cps = []
    for k, ref in enumerate(refs):
        blk = ref.at[4 * x + 2 * y + c]
        for j, (px, py) in enumerate(_other_chips(x, y)):
            cps.append(_remote(blk, blk, ss, rs, 3 * k + j, (px, py, c)))
    return cps


def _build_ag_fwd(refs, ss, rs):
    x, y, c = _place()
    cps = []
    for k, ref in enumerate(refs):
        for j, (px, py) in enumerate(_other_chips(x, y)):
            blk = ref.at[4 * px + 2 * py + c]
            cps.append(_remote(blk, blk, ss, rs, 3 * k + j, (x, y, 1 - c)))
    return cps


def _build_rs_swap(refs, ss, rs):
    x, y, c = _place()
    n = len(refs) // 2
    return [_remote(refs[k].at[:, pl.ds(1 - c, 1)], refs[n + k], ss, rs, k, (x, y, 1 - c)) for k in range(n)]


def _build_rs_ici(refs, ss, rs):
    x, y, c = _place()
    n = len(refs) // 2
    me = 2 * x + y
    cps = []
    for k in range(n):
        for j, (px, py) in enumerate(_other_chips(x, y)):
            cps.append(_remote(refs[k].at[2 * px + py], refs[n + k].at[me], ss, rs, 3 * k + j, (px, py, c)))
    return cps


def _build_rs_share(refs, ss, rs):
    x, y, c = _place()
    return [_remote(ref.at[c], ref.at[c], ss, rs, k, (x, y, 1 - c)) for k, ref in enumerate(refs)]


def _allreduce_small(p, deps=()):
    R, C = p.shape
    nd = len(deps)

    def body(p_ref, *rest):
        gath_ref, sum_ref, send_sems, recv_sems, local_sem = rest[nd:]
        x, y, c = _place()
        me, sibling = (x, y, c), (x, y, 1 - c)
        chips = [(1 - x, y), (x, 1 - y), (1 - x, 1 - y)]

        def blk(px, py, pc):
            return gath_ref.at[4 * px + 2 * py + pc]

        def copy(k, block, to, src=None):
            return pltpu.make_async_remote_copy(
                src_ref=blk(*block) if src is None else src, dst_ref=blk(*block),
                send_sem=send_sems.at[k], recv_sem=recv_sems.at[k], device_id=to, device_id_type=MESH)

        mine = pltpu.make_async_copy(p_ref, blk(*me), local_sem)
        mine.start()
        first = [copy(0, me, sibling, src=p_ref)]
        first += [copy(1 + j, me, (*chip, c), src=p_ref) for j, chip in enumerate(chips)]
        for cp in first:
            cp.start()
        passed = [copy(4 + j, (*chip, c), sibling) for j, chip in enumerate(chips)]
        for j, chip in enumerate(chips):
            copy(1 + j, (*chip, c), me).wait_recv()
            passed[j].start()
        copy(0, sibling, me).wait_recv()
        for j, chip in enumerate(chips):
            copy(4 + j, (*chip, 1 - c), me).wait_recv()
        for cp in first + passed:
            cp.wait_send()
        mine.wait()
        s = gath_ref[0]
        for d in range(1, N_DEV):
            s = s + gath_ref[d]
        sum_ref[...] = s

    vm = pl.BlockSpec(memory_space=pltpu.VMEM)
    return pl.pallas_call(
        body, name="allreduce_small",
        in_specs=[vm] + [ANY] * nd, out_specs=[vm, vm],
        out_shape=[jax.ShapeDtypeStruct((N_DEV, R, C), F32), jax.ShapeDtypeStruct((R, C), F32)],
        scratch_shapes=[pltpu.SemaphoreType.DMA((7,)), pltpu.SemaphoreType.DMA((7,)), pltpu.SemaphoreType.DMA],
    )(p, *deps)[1]


def _rs_add_pair(p, r0, c_arr, name):
    _, _, hr, cols = p.shape
    tr = _tile(hr, 256, 8)

    def body(c_ref, p_ref, r_ref, q_ref):
        q_ref[...] = (p_ref[0] + r_ref[0]).astype(BF16)

    grid_spec = pltpu.PrefetchScalarGridSpec(
        num_scalar_prefetch=1, grid=(N_CHIPS, hr // tr),
        in_specs=[pl.BlockSpec((1, 1, tr, cols), lambda j, i, c_ref: (j, c_ref[0], i, 0)),
                  pl.BlockSpec((1, 1, tr, cols), lambda j, i, c_ref: (j, 0, i, 0))],
        out_specs=pl.BlockSpec((1, tr, cols), lambda j, i, c_ref: (j, i, 0)))
    return pl.pallas_call(
        body, name=name, grid_spec=grid_spec,
        out_shape=jax.ShapeDtypeStruct((N_CHIPS, hr, cols), BF16),
        compiler_params=_cparams(("parallel", "parallel")),
    )(c_arr, p, r0)


def _rs_add_chips(r1, q, place_arr, name):
    _, hr, cols = r1.shape
    tr = _tile(hr, 256, 8)

    def body(place_ref, r_ref, q_ref, o_ref):
        chip = place_ref[0]
        s = None
        for j in range(N_CHIPS):
            t = jnp.where(chip == j, q_ref[j], r_ref[j]).astype(F32)
            s = t if s is None else s + t
        o_ref[...] = s

    blk = pl.BlockSpec((N_CHIPS, tr, cols), lambda i, place_ref: (0, i, 0))
    grid_spec = pltpu.PrefetchScalarGridSpec(
        num_scalar_prefetch=1, grid=(hr // tr,), in_specs=[blk, blk],
        out_specs=pl.BlockSpec((None, tr, cols), lambda i, place_ref: (place_ref[1], i, 0)))
    return pl.pallas_call(
        body, name=name, grid_spec=grid_spec,
        out_shape=jax.ShapeDtypeStruct((2, hr, cols), F32),
        compiler_params=_cparams(("parallel",)),
    )(place_arr, r1, q)


def _pad_rows(a, rows):
    return jnp.pad(a, ((0, rows - a.shape[0]), (0, 0)))


def _pad_cols(a, cols):
    return jnp.pad(a, ((0, 0), (0, cols - a.shape[1])))


def _heads_to_rows(v):
    v = v.reshape(N_GROUPS, HEADS_PER_GROUP, 1)
    v = jnp.pad(v, ((0, 0), (0, 8 - HEADS_PER_GROUP), (0, 0)))
    return jnp.broadcast_to(v, (N_GROUPS, 8, CHUNK))


def _rows_to_heads(a):
    return jnp.sum(a[:, :HEADS_PER_GROUP, :], axis=-1).reshape(N_HEADS)


def _to_kernel_rows(a):
    C = a.shape[1]
    x0, b0, c0, s0 = D_SSM, 2 * D_SSM, 2 * D_SSM + 1024, D_SSM + D_XBC + N_HEADS
    xbc = jnp.concatenate([a[x0:b0].reshape(N_GROUPS, GW, C), a[b0:c0].reshape(N_GROUPS, N_STATE, C),
                           a[c0:c0 + 1024].reshape(N_GROUPS, N_STATE, C)], axis=1).reshape(D_XBC, C)
    sc = jnp.concatenate([a[s0 + k * D_MODEL:s0 + (k + 1) * D_MODEL].reshape(D_MODEL // SCB, SCB, C)
                          for k in range(3)], axis=1).reshape(3 * D_MODEL, C)
    return jnp.concatenate([a[:D_SSM], xbc, sc], axis=0)


HR_IN = 1568


def _shard_row_plan():
    segs = [(0, 0, 0, D_SSM)]
    for g in range(N_GROUPS):
        k0 = D_SSM + g * GXBC
        segs += [(0, k0, D_SSM + g * GW, GW), (0, k0 + GW, 2 * D_SSM + g * N_STATE, N_STATE),
                 (0, k0 + GW + N_STATE, 2 * D_SSM + 1024 + g * N_STATE, N_STATE)]
    segs.append((1, 0, D_SSM + D_XBC, N_HEADS))
    for j in range(D_MODEL // SCB):
        for k in range(3):
            segs.append((0, D_SSM + D_XBC + j * SC3 + k * SCB, D_SSM + D_XBC + N_HEADS + k * D_MODEL + j * SCB, SCB))
    cs = D_IN // N_CHIPS
    plan = []
    for src, s, o, n in segs:
        while n > 0:
            chip, loc = divmod(o, cs)
            half, row = divmod(loc, HR_IN)
            m = min(n, cs - loc, HR_IN - row)
            plan.append((src, s, chip, half, row, m))
            s, o, n = s + m, o + m, n - m
    return plan


SCATTER_ROWS = 512
SCATTER_SLOTS = 4


def _scatter_rows_to_shards(k_main, k_dt):
    C = k_main.shape[1]
    pieces = []
    for src, s, chip, half, row, n in _shard_row_plan():
        for o in range(0, n, SCATTER_ROWS):
            pieces.append((src, s + o, chip, half, row + o, min(SCATTER_ROWS, n - o)))
    S, lag, N = SCATTER_SLOTS, SCATTER_SLOTS // 2, len(pieces)

    def body(m_ref, d_ref, o_ref, buf, in_sems, out_sems):
        def cin(i):
            src, s, _, _, _, n = pieces[i]
            return pltpu.make_async_copy((d_ref if src else m_ref).at[pl.ds(s, n)],
                                         buf.at[i % S, pl.ds(0, n)], in_sems.at[i % S])

        def cout(i):
            _, _, chip, half, row, n = pieces[i]
            return pltpu.make_async_copy(buf.at[i % S, pl.ds(0, n)],
                                         o_ref.at[chip, half, pl.ds(row, n)], out_sems.at[i % S])

        for i in range(N + lag):
            if i < N:
                if i >= S:
                    cout(i - S).wait()
                cin(i).start()
            j = i - lag
            if 0 <= j < N:
                cin(j).wait()
                cout(j).start()
        for j in range(max(0, N - S), N):
            cout(j).wait()

    return pl.pallas_call(
        body, name="scatter_dw_in_rows", in_specs=[ANY, ANY], out_specs=ANY,
        out_shape=jax.ShapeDtypeStruct((N_CHIPS, 2, HR_IN, C), k_main.dtype),
        scratch_shapes=[pltpu.VMEM((S, SCATTER_ROWS, C), k_main.dtype),
                        pltpu.SemaphoreType.DMA((S,)), pltpu.SemaphoreType.DMA((S,))],
        compiler_params=_cparams(),
    )(k_main, k_dt)


def _to_kernel_xbc(a):
    R = a.shape[0]
    return jnp.concatenate([a[:, :D_SSM].reshape(R, N_GROUPS, GW), a[:, D_SSM:D_SSM + 1024].reshape(R, N_GROUPS, N_STATE),
                            a[:, D_SSM + 1024:].reshape(R, N_GROUPS, N_STATE)], axis=2).reshape(R, D_XBC)


def _from_kernel_xbc(a):
    R = a.shape[0]
    g = a.reshape(R, N_GROUPS, GXBC)
    return jnp.concatenate([g[:, :, :GW].reshape(R, D_SSM), g[:, :, GW:GW + N_STATE].reshape(R, 1024),
                            g[:, :, GW + N_STATE:].reshape(R, 1024)], axis=1)


def kernel(x, norm_mix_g, w_in, ssm_conv_w, ssm_conv_b, ssm_dt_bias, ssm_A_log, ssm_D, ssm_norm_g, sc_conv_w, w_out, norm_ffn_g, w_gate, w_up, w_down, norm_final_g, loss_target, m_norm_mix_g, m_w_in, m_ssm_conv_w, m_ssm_conv_b, m_ssm_dt_bias, m_ssm_A_log, m_ssm_D, m_ssm_norm_g, m_sc_conv_w, m_w_out, m_norm_ffn_g, m_w_gate, m_w_up, m_w_down, m_norm_final_g, v_norm_mix_g, v_w_in, v_ssm_conv_w, v_ssm_conv_b, v_ssm_dt_bias, v_ssm_A_log, v_ssm_D, v_ssm_norm_g, v_sc_conv_w, v_w_out, v_norm_ffn_g, v_w_gate, v_w_up, v_w_down, v_norm_final_g):
    T = x.shape[1]
    xt = x[0]
    tgt = loss_target[0]
    cx, cy, cc = lax.axis_index("x"), lax.axis_index("y"), lax.axis_index("c")
    chip = 2 * cx + cy
    c_arr = jnp.reshape(cc, (1,)).astype(jnp.int32)
    chip_arr = jnp.reshape(chip, (1,)).astype(jnp.int32)
    place_arr = jnp.stack([chip, cc]).astype(jnp.int32)

    big = [w_in[0].T, w_out[0], w_gate[0], w_up[0], w_down[0]]
    names = ["w_in", "w_out", "w_gate", "w_up", "w_down"]
    gbufs = [_cast_into_gather(w, chip_arr, "cast_" + nm, split_cols=(nm == "w_in")) for w, nm in zip(big, names)]
    (g_in,) = _allgather_inplace([gbufs[0]])
    cs_in = D_IN // N_CHIPS
    wt = g_in.reshape(N_CHIPS, 2, cs_in, D_MODEL // 2).transpose(0, 2, 1, 3).reshape(D_IN, D_MODEL)
    wt_main = _to_kernel_rows(wt)
    wt_dt = _pad_rows(wt[D_SSM + D_XBC:D_SSM + D_XBC + N_HEADS], DT_PAD)

    contrib = (cc == 0).astype(F32)
    place_ssm = jnp.zeros((8, D_XBC), F32)
    place_ssm = lax.dynamic_update_slice(place_ssm, _pad_rows(ssm_conv_w[0], 8) * contrib, (0, chip * (D_XBC // N_CHIPS)))
    place_sc = jnp.zeros((8, D_MODEL), F32)
    place_sc = lax.dynamic_update_slice(place_sc, _pad_rows(sc_conv_w[0], 8) * contrib, (0, chip * (D_MODEL // N_CHIPS)))
    convs = _allreduce_small(jnp.concatenate([place_ssm, _pad_cols(place_sc, D_XBC)], axis=0))
    ssm_w8 = _to_kernel_xbc(convs[:8])
    ssm_bk = _to_kernel_xbc(ssm_conv_b)
    sc_w8 = convs[8:, :D_MODEL]
    ag_ss, ag_rs, ag_bufs, ag_tok = _split_start("ag_ici_start", gbufs[1:], _build_ag_ici, 12, after=[g_in, convs])

    bias_rows = _heads_to_rows(ssm_dt_bias[0])
    alog_rows = _heads_to_rows(ssm_A_log[0])
    drep = jnp.repeat(ssm_D[0], HEADDIM).reshape(1, D_SSM)

    n1 = _rmsnorm_fwd(xt, _tie(norm_mix_g, ag_tok, "tie_ag_ici"), "rmsnorm_mix")
    (proj,) = _matmul([(n1, wt_main)], tb=True, out_dtypes=[F32], name="mm_proj")
    (dt_raw,) = _matmul([(n1, wt_dt)], tb=True, out_dtypes=[F32], name="mm_proj_dt")
    xbc = _ssm_conv_fwd(proj, ssm_w8, ssm_bk)
    dtr = jnp.pad(dt_raw[:, :N_HEADS].T.reshape(N_GROUPS, HEADS_PER_GROUP, T), ((0, 0), (0, 4), (0, 0)))
    y_ssd, hs = _ssd_fwd(xbc, dtr, bias_rows, alog_rows, drep)
    ag_bufs = _split_wait("ag_ici_wait", ag_ss, ag_rs, ag_bufs, _build_ag_ici, after=[y_ssd])
    fw_ss, fw_rs, fw_bufs, fw_tok = _split_start("ag_fwd_start", ag_bufs, _build_ag_fwd, 12)
    y_mix = _shortconv_fwd(proj, sc_w8, _gated_norm_fwd(y_ssd, proj, _tie(ssm_norm_g, fw_tok, "tie_ag_fwd")))
    gath = _split_wait("ag_fwd_wait", fw_ss, fw_rs, fw_bufs, _build_ag_fwd, after=[y_mix])
    w_out_f = gath[0].reshape(2 * D_MODEL, D_MODEL)
    w_gate3 = gath[1].reshape(N_CHIPS, D_MODEL, D_FF // N_CHIPS)
    w_up3 = gath[2].reshape(N_CHIPS, D_MODEL, D_FF // N_CHIPS)
    w_down_f = gath[3].reshape(D_FF, D_MODEL)
    (h1,) = _matmul([(y_mix, w_out_f)], out_dtypes=[F32], name="mm_out", extras=[xt],
                    epilogue=lambda acc, res: (acc + res,))
    n2 = _rmsnorm_fwd(h1, norm_ffn_g, "rmsnorm_ffn")
    g_act, u_act, a_act = _ffn_fwd(n2, w_gate3, w_up3)
    (h2,) = _matmul([(a_act, w_down_f)], out_dtypes=[F32], name="mm_down", extras=[h1],
                    epilogue=lambda acc, res: (acc + res,))

    dh2, dh2b, dg_final, loss_part = _loss_and_final_bwd(h2, tgt, norm_final_g.reshape(1, D_MODEL))
    dg_act, du_act = _matmul([(dh2b, w_down_f)], tb=True, out_dtypes=[BF16, BF16], name="mm_down_bwd",
                             tn=512, extras=[g_act, u_act], epilogue=_swiglu_bwd, nsub=2)
    (dw_down,) = _matmul([(a_act, dh2b)], ta=True, out_dtypes=[F32], name="mm_dw_down", tm=1408)
    (dn2,) = _matmul([(dg_act, w_gate3), (du_act, w_up3)], tb=True, b3d=True, out_dtypes=[BF16],
                     name="mm_ffn_in_bwd")
    (dw_gate,) = _matmul([(n2, dg_act)], ta=True, out_dtypes=[F32], name="mm_dw_gate", tn=1408, col_shards=True)
    (dw_up,) = _matmul([(n2, du_act)], ta=True, out_dtypes=[F32], name="mm_dw_up", tn=1408, col_shards=True)
    dh1, dh1b, dg_ffn = _rmsnorm_bwd(dn2, h1, norm_ffn_g, dh2, "rmsnorm_ffn_bwd")
    (dw_out,) = _matmul([(y_mix, dh1b)], ta=True, out_dtypes=[F32], name="mm_dw_out")

    def halves(g):
        return g.reshape(N_CHIPS, 2, g.shape[1] // 2, g.shape[2])

    def landing(shape, dtype):
        return lax.empty(shape, dtype)

    names1 = names[1:]
    ps1 = [halves(dw_out.reshape(N_CHIPS, -1, D_MODEL)), halves(dw_gate), halves(dw_up),
           halves(dw_down.reshape(N_CHIPS, -1, D_MODEL))]
    r0_1 = [landing((N_CHIPS, 1) + p.shape[2:], F32) for p in ps1]
    sw_ss, sw_rs, sw_arr, sw_tok = _split_start("rs1_swap_start", ps1 + r0_1, _build_rs_swap, 4)
    (dmix,) = _matmul([(dh1b, w_out_f)], tb=True, out_dtypes=[BF16], name="mm_out_bwd", deps=[sw_tok])
    dproj, dw_sc = _shortconv_bwd(dmix, proj, sc_w8)
    dy_ssd, dproj, dg_ssmnorm = _gated_norm_bwd(dmix, y_ssd, proj, ssm_norm_g, dproj)
    sw_arr = _split_wait("rs1_swap_wait", sw_ss, sw_rs, sw_arr, _build_rs_swap, after=[dy_ssd])
    qs1 = [_rs_add_pair(p, r, c_arr, "rs_add_pair_" + nm) for p, r, nm in zip(sw_arr[:4], sw_arr[4:], names1)]
    r1_1 = [landing(q.shape, BF16) for q in qs1]
    ic_ss, ic_rs, ic_arr, ic_tok = _split_start("rs1_ici_start", qs1 + r1_1, _build_rs_ici, 12)
    dxbc_act, ddtr, dbias_acc, dalog_acc, dD_acc = _ssd_bwd(
        xbc, dtr, bias_rows, alog_rows, _tie(drep, ic_tok, "tie_rs1_ici"), dy_ssd, hs)
    dproj, dw_ssmconv, db_ssmconv = _ssm_conv_bwd(dxbc_act, proj, ssm_w8, ssm_bk, dproj)
    dw_ssmconv, db_ssmconv = _from_kernel_xbc(dw_ssmconv), _from_kernel_xbc(db_ssmconv)
    ic_arr = _split_wait("rs1_ici_wait", ic_ss, ic_rs, ic_arr, _build_rs_ici, after=[dproj])
    g1 = [_rs_add_chips(r, q, place_arr, "rs_add_chips_" + nm) for q, r, nm in zip(ic_arr[:4], ic_arr[4:], names1)]
    sh_ss, sh_rs, sh_arr, sh_tok = _split_start("rs1_share_start", g1, _build_rs_share, 4)

    ddt_raw = _pad_cols(ddtr[:, :HEADS_PER_GROUP, :].reshape(N_HEADS, T).T, DT_PAD).astype(BF16)
    (dwt_main,) = _matmul([(dproj, n1)], ta=True, out_dtypes=[F32], name="mm_dw_main", deps=[sh_tok])
    (dwt_dt,) = _matmul([(ddt_raw, n1)], ta=True, out_dtypes=[F32], name="mm_dw_dt")
    p_in = _scatter_rows_to_shards(dwt_main, dwt_dt)
    s2_ss, s2_rs, s2_arr, s2_tok = _split_start(
        "rs2_swap_start", [p_in, landing((N_CHIPS, 1) + p_in.shape[2:], F32)], _build_rs_swap, 1)
    (dn1a,) = _matmul([(dproj, wt_main)], out_dtypes=[F32], name="mm_proj_bwd", deps=[s2_tok])
    g1 = _split_wait("rs1_share_wait", sh_ss, sh_rs, sh_arr, _build_rs_share, after=[dn1a])
    s2_arr = _split_wait("rs2_swap_wait", s2_ss, s2_rs, s2_arr, _build_rs_swap, after=[dn1a])
    q_in = _rs_add_pair(s2_arr[0], s2_arr[1], c_arr, "rs_add_pair_w_in")
    i2_ss, i2_rs, i2_arr, i2_tok = _split_start(
        "rs2_ici_start", [q_in, landing(q_in.shape, BF16)], _build_rs_ici, 3)
    (dn1,) = _matmul([(ddt_raw, wt_dt)], out_dtypes=[BF16], name="mm_proj_dt_bwd", extras=[dn1a],
                     epilogue=lambda acc, res: (acc + res,), deps=[i2_tok])
    dx, _, dg_mix = _rmsnorm_bwd(dn1, xt, norm_mix_g, dh1, "rmsnorm_mix_bwd")

    big_m = [m_w_in[0].T, m_w_out[0], m_w_gate[0], m_w_up[0], m_w_down[0]]
    big_v = [v_w_in[0].T, v_w_out[0], v_w_gate[0], v_w_up[0], v_w_down[0]]
    big_grads = [None] + [g.reshape(w.shape) for g, w in zip(g1, big[1:])]
    big_out = {}
    for k in range(1, 5):
        big_out[names[k]] = _adamw(big[k], big_grads[k], big_m[k], big_v[k], "adamw_" + names[k], deps=[i2_tok])
    i2_arr = _split_wait("rs2_ici_wait", i2_ss, i2_rs, i2_arr, _build_rs_ici, after=[big_out[names[4]][0], dx])
    g_in_red = _rs_add_chips(i2_arr[1], i2_arr[0], place_arr, "rs_add_chips_w_in")
    s3_ss, s3_rs, s3_arr, s3_tok = _split_start("rs2_share_start", [g_in_red], _build_rs_share, 1)

    dD = jnp.sum(dD_acc.reshape(N_HEADS, HEADDIM), axis=-1)
    heads_row = jnp.concatenate([_rows_to_heads(dbias_acc), _rows_to_heads(dalog_acc), dD,
                                 loss_part.reshape(1)]).reshape(1, -1)
    small = jnp.concatenate([
        dw_ssmconv,
        _pad_cols(dw_sc, D_XBC),
        db_ssmconv,
        jnp.concatenate([dg_mix, dg_ssmnorm], axis=1),
        jnp.concatenate([dg_ffn, dg_final], axis=1),
        _pad_cols(heads_row, D_XBC),
        jnp.zeros((4, D_XBC), F32),
    ], axis=0)
    tot = _allreduce_small(small, deps=[s3_tok])
    loss = tot[19, 3 * N_HEADS]

    cs_ssm, cs_sc = D_XBC // N_CHIPS, D_MODEL // N_CHIPS
    g_ssm_conv = lax.dynamic_slice(tot[0:K_SSM], (0, chip * cs_ssm), (K_SSM, cs_ssm))
    g_sc_conv = lax.dynamic_slice(tot[8:8 + K_SC, :D_MODEL], (0, chip * cs_sc), (K_SC, cs_sc))
    small_grads = {
        "norm_mix_g": tot[17:18, :D_MODEL], "ssm_conv_w": g_ssm_conv, "ssm_conv_b": tot[16:17],
        "ssm_dt_bias": tot[19:20, 0:N_HEADS], "ssm_A_log": tot[19:20, N_HEADS:2 * N_HEADS],
        "ssm_D": tot[19:20, 2 * N_HEADS:3 * N_HEADS], "ssm_norm_g": tot[17:18, D_MODEL:],
        "sc_conv_w": g_sc_conv, "norm_ffn_g": tot[18:19, :D_MODEL], "norm_final_g": tot[18:19, D_MODEL:],
    }
    small_w = {"norm_mix_g": (norm_mix_g, m_norm_mix_g, v_norm_mix_g),
               "ssm_conv_w": (ssm_conv_w[0], m_ssm_conv_w[0], v_ssm_conv_w[0]),
               "ssm_conv_b": (ssm_conv_b, m_ssm_conv_b, v_ssm_conv_b),
               "ssm_dt_bias": (ssm_dt_bias, m_ssm_dt_bias, v_ssm_dt_bias),
               "ssm_A_log": (ssm_A_log, m_ssm_A_log, v_ssm_A_log),
               "ssm_D": (ssm_D, m_ssm_D, v_ssm_D),
               "ssm_norm_g": (ssm_norm_g, m_ssm_norm_g, v_ssm_norm_g),
               "sc_conv_w": (sc_conv_w[0], m_sc_conv_w[0], v_sc_conv_w[0]),
               "norm_ffn_g": (norm_ffn_g, m_norm_ffn_g, v_norm_ffn_g),
               "norm_final_g": (norm_final_g.reshape(1, -1), m_norm_final_g.reshape(1, -1),
                                v_norm_final_g.reshape(1, -1))}
    PW = 1024
    order = list(small_w)

    def pack(arrs):
        rows = []
        for a in arrs:
            flat = a.reshape(-1)
            n = -(-flat.shape[0] // PW) * PW
            rows.append(jnp.pad(flat, (0, n - flat.shape[0])).reshape(-1, PW))
        slab = jnp.concatenate(rows, axis=0)
        return _pad_rows(slab, -(-slab.shape[0] // 8) * 8)

    wp = pack([small_w[k][0] for k in order])
    mp = pack([small_w[k][1] for k in order])
    vp = pack([small_w[k][2] for k in order])
    gp = pack([small_grads[k] for k in order])
    sd, sm, sv = _adamw(wp, gp, mp, vp, "adamw_small")

    def unpack(slab):
        out, row = {}, 0
        for k in order:
            shape = small_w[k][0].shape
            size = 1
            for s in shape:
                size *= s
            nr = -(-size // PW)
            out[k] = slab[row:row + nr].reshape(-1)[:size].reshape(shape)
            row += nr
        return out

    s_delta, s_m, s_v = unpack(sd), unpack(sm), unpack(sv)

    (g_in_full,) = _split_wait("rs2_share_wait", s3_ss, s3_rs, s3_arr, _build_rs_share, after=[sd])
    d_t, m_t, v_t, g_t = _adamw(big[0], g_in_full.reshape(2 * HR_IN, D_MODEL), big_m[0], big_v[0],
                                "adamw_" + names[0], emit_g=True)
    big_grads[0] = g_t.T
    big_out[names[0]] = (d_t.T, m_t.T, v_t.T)
    big_g = dict(zip(names, big_grads))

    weight_order = ["norm_mix_g", "w_in", "ssm_conv_w", "ssm_conv_b", "ssm_dt_bias", "ssm_A_log", "ssm_D",
                    "ssm_norm_g", "sc_conv_w", "w_out", "norm_ffn_g", "w_gate", "w_up", "w_down", "norm_final_g"]
    lead = {"ssm_conv_w", "sc_conv_w", "w_in", "w_out", "w_gate", "w_up", "w_down"}

    def shaped(nm, a):
        if nm == "norm_final_g":
            return a.reshape(D_MODEL)
        return a[None] if nm in lead else a

    grads, deltas, new_m, new_v = [], [], [], []
    for nm in weight_order:
        if nm in big_out:
            g, (d, m, v) = big_g[nm], big_out[nm]
        else:
            g, d, m, v = small_grads[nm], s_delta[nm], s_m[nm], s_v[nm]
        grads.append(shaped(nm, g))
        deltas.append(shaped(nm, d))
        new_m.append(shaped(nm, m))
        new_v.append(shaped(nm, v))
    return (loss, dx[None], *grads, *deltas, *new_m, *new_v)


def _swiglu_bwd(da, g, u):
    gf, uf = g.astype(F32), u.astype(F32)
    return da * uf * _dsilu(gf), da * _silu(gf)


def _ffn_fwd(n2, w_gate, w_up):
    T, K = n2.shape
    tn = w_gate.shape[2]
    N = N_CHIPS * tn
    tm = _tile(T, 512)
    sub = _tile(tm, 256)

    def body(a_ref, wg_ref, wu_ref, g_ref, u_ref, act_ref):
        for s in range(tm // sub):
            rows = pl.ds(s * sub, sub)
            a = a_ref[rows, :]
            g = jnp.dot(a, wg_ref[...], preferred_element_type=F32)
            u = jnp.dot(a, wu_ref[...], preferred_element_type=F32)
            g_ref[rows, :] = g.astype(BF16)
            u_ref[rows, :] = u.astype(BF16)
            act_ref[rows, :] = (_silu(g) * u).astype(BF16)

    a_spec = pl.BlockSpec((tm, K), lambda j, i: (i, 0))
    b_spec = pl.BlockSpec((None, K, tn), lambda j, i: (j, 0, 0))
    o_spec = pl.BlockSpec((tm, tn), lambda j, i: (i, j))
    return pl.pallas_call(
        body, name="ffn_fwd", grid=(N // tn, T // tm),
        in_specs=[a_spec, b_spec, b_spec], out_specs=[o_spec] * 3,
        out_shape=[jax.ShapeDtypeStruct((T, N), BF16)] * 3,
        compiler_params=_cparams(("parallel", "parallel")),
    )(n2, w_gate, w_up)
```

```python
import functools

import jax
import jax.numpy as jnp
from jax import lax
from jax.experimental import pallas as pl
from jax.experimental.pallas import tpu as pltpu

F32 = jnp.float32
BF16 = jnp.bfloat16
MESH = pl.DeviceIdType.MESH

D_MODEL = 2048
D_SSM = 2048
HEADDIM = 64
N_HEADS = 32
N_GROUPS = 8
HEADS_PER_GROUP = 4
N_STATE = 128
CHUNK = 128
K_SSM = 4
K_SC = 3
D_XBC = 4096
D_FF = 5632
D_IN = 12320
D_MAIN = 12288
OFF_XBC, OFF_CB, OFF_CC, OFF_CX = 2048, 6144, 8192, 10240
DT_PAD = 128
EPS = 1e-5
N_CHIPS = 4
N_DEV = 8

ADAM_LR = 0.001
ADAM_B1 = 0.9
ADAM_B2 = 0.999
ADAM_EPS = 1e-08
ADAM_WD = 0.01
ADAM_STEP = 10

V7X_VMEM_BYTES = 64 * 1024 * 1024
VMEM_LIMIT = V7X_VMEM_BYTES - 8 * 1024 * 1024


def _cparams(sem=None):
    if sem is None:
        return pltpu.CompilerParams(vmem_limit_bytes=VMEM_LIMIT)
    return pltpu.CompilerParams(dimension_semantics=sem, vmem_limit_bytes=VMEM_LIMIT)


def _tile(dim, pref, unit=128):
    best = None
    t = unit
    while t <= min(dim, pref):
        if dim % t == 0:
            best = t
        t += unit
    return best if best is not None else dim


def _sigmoid(x):
    return 1.0 / (1.0 + jnp.exp(-x))


def _silu(x):
    return x * _sigmoid(x)


def _dsilu(x):
    s = _sigmoid(x)
    return s * (1.0 + x * (1.0 - s))


def _softplus(x):
    return jnp.maximum(x, 0.0) + jnp.log(1.0 + jnp.exp(-jnp.abs(x)))


MATMUL_VMEM_BUDGET = 44 * 1024 * 1024


def _matmul(pairs, *, ta=False, tb=False, out_dtypes, name, tm=1024, tn=1024, tk=None, extras=(), epilogue=None,
            deps=(), col_shards=False, nsub=1, b3d=False, m_tiles=None, out_buf=None):
    a0, b0 = pairs[0]
    M, K = (a0.shape[1], a0.shape[0]) if ta else a0.shape
    if b3d:
        N = b0.shape[1] if tb else b0.shape[0] * b0.shape[2]
        tk, tn = (b0.shape[2], tn) if tb else (tk, b0.shape[2])
    else:
        N = b0.shape[0] if tb else b0.shape[1]
    tm, tn = _tile(M, tm, 8 if M % 128 else 128), _tile(N, tn)
    npair, nex, ndep, nout = len(pairs), len(extras), len(deps), len(out_dtypes)
    if tk is None:
        fixed = 2 * tm * tn * (sum(jnp.dtype(d).itemsize for d in out_dtypes) + sum(e.dtype.itemsize for e in extras))
        tk = K
        while tk > 128 and (K % tk or tk % 128 or
                            fixed + 2 * npair * 2 * tk * (tm + tn) + (tm * tn * 4 if tk < K else 0) > MATMUL_VMEM_BUDGET):
            tk -= 128
    else:
        tk = _tile(K, tk)
    nk = K // tk
    if nk > 1 or tm % nsub or (tm // nsub) % 128:
        nsub = 1
    sub = tm // nsub
    dims = (((0 if ta else 1,), (1 if tb else 0,)), ((), ()))
    i0, mi = m_tiles if m_tiles is not None else (0, M // tm)
    nbuf = 0 if out_buf is None else 1

    def body(*refs):
        a_refs = refs[0:2 * npair:2]
        b_refs = refs[1:2 * npair:2]
        ex_refs = refs[2 * npair:2 * npair + nex]
        o_refs = refs[2 * npair + nex + ndep + nbuf:2 * npair + nex + ndep + nbuf + nout]

        def dots(rows):
            s = None
            for a_ref, b_ref in zip(a_refs, b_refs):
                a = a_ref[...] if rows is None else (a_ref[:, rows] if ta else a_ref[rows, :])
                d = lax.dot_general(a, b_ref[...], dims, preferred_element_type=F32)
                s = d if s is None else s + d
            return s

        def finish(r, rows):
            ex = [e[...] if rows is None else e[rows, :] for e in ex_refs]
            outs = (r,) if epilogue is None else epilogue(r, *ex)
            for o_ref, o in zip(o_refs, outs):
                if rows is None:
                    o_ref[...] = o.astype(o_ref.dtype)
                else:
                    o_ref[rows, :] = o.astype(o_ref.dtype)

        if nk == 1:
            for s in range(nsub):
                rows = None if nsub == 1 else pl.ds(s * sub, sub)
                finish(dots(rows), rows)
            return

        acc = refs[-1]
        k = pl.program_id(2)

        @pl.when(k == 0)
        def _():
            acc[...] = dots(None)

        @pl.when(jnp.logical_and(k > 0, k < nk - 1))
        def _():
            acc[...] += dots(None)

        @pl.when(k == nk - 1)
        def _():
            finish(acc[...] + dots(None), None)

    a_spec = (pl.BlockSpec((tk, tm), lambda i, j, k: (k, i + i0)) if ta
              else pl.BlockSpec((tm, tk), lambda i, j, k: (i + i0, k)))
    if b3d:
        b_spec = (pl.BlockSpec((None, tn, tk), lambda i, j, k: (k, j, 0)) if tb
                  else pl.BlockSpec((None, tk, tn), lambda i, j, k: (j, k, 0)))
    else:
        b_spec = (pl.BlockSpec((tn, tk), lambda i, j, k: (j, k)) if tb
                  else pl.BlockSpec((tk, tn), lambda i, j, k: (k, j)))
    e_spec = pl.BlockSpec((tm, tn), lambda i, j, k: (i + i0, j))
    if col_shards:
        o_spec = pl.BlockSpec((None, tm, tn), lambda i, j, k: (j, i + i0, 0))
        o_shape = (N // tn, M, tn)
    else:
        o_spec, o_shape = e_spec, (M, N)
    args, in_specs = [], []
    for a, b in pairs:
        args += [a, b]
        in_specs += [a_spec, b_spec]
    args += list(extras) + list(deps) + ([] if out_buf is None else [out_buf])
    in_specs += [e_spec] * nex + [ANY] * (ndep + nbuf)
    outs = pl.pallas_call(
        body,
        name=name,
        grid=(mi, N // tn, nk),
        in_specs=in_specs,
        out_specs=[o_spec] * nout,
        out_shape=[jax.ShapeDtypeStruct(o_shape, dt) for dt in out_dtypes],
        input_output_aliases={} if out_buf is None else {len(args) - 1: 0},
        scratch_shapes=[pltpu.VMEM((tm, tn), F32)] if nk > 1 else [],
        compiler_params=_cparams(("parallel", "parallel", "arbitrary")),
    )(*args)
    return outs


def _cast_into_gather(w, chip_arr, name, split_cols=False):
    R, C = w.shape
    hr, hc = (R, C // 2) if split_cols else (R // 2, C)
    tr = _tile(hr, 512, 8)
    nb = hr // tr

    def body(chip_ref, w_ref, o_ref):
        o_ref[...] = w_ref[...].astype(BF16)

    in_map = (lambda h, i, chip_ref: (i, h)) if split_cols else (lambda h, i, chip_ref: (h * nb + i, 0))
    grid_spec = pltpu.PrefetchScalarGridSpec(
        num_scalar_prefetch=1, grid=(2, nb),
        in_specs=[pl.BlockSpec((tr, hc), in_map)],
        out_specs=pl.BlockSpec((None, tr, hc), lambda h, i, chip_ref: (2 * chip_ref[0] + h, i, 0)))
    return pl.pallas_call(
        body, name=name, grid_spec=grid_spec,
        out_shape=jax.ShapeDtypeStruct((N_DEV, hr, hc), BF16),
        compiler_params=_cparams(("parallel", "parallel")),
    )(chip_arr, w)


def _tie(small, token, name):
    def body(s_ref, t_ref, o_ref):
        o_ref[...] = s_ref[...]

    vm = pl.BlockSpec(memory_space=pltpu.VMEM)
    return pl.pallas_call(body, name=name, in_specs=[vm, ANY], out_specs=vm,
                          out_shape=jax.ShapeDtypeStruct(small.shape, small.dtype))(small, token)


def _rmsnorm_fwd(x, g, name):
    T, D = x.shape
    tt = _tile(T, 256)

    def body(x_ref, g_ref, n_ref):
        xv = x_ref[...]
        r = lax.rsqrt(jnp.mean(xv * xv, axis=-1, keepdims=True) + EPS)
        n_ref[...] = (xv * r * g_ref[...]).astype(BF16)

    return pl.pallas_call(
        body, name=name, grid=(T // tt,),
        in_specs=[pl.BlockSpec((tt, D), lambda i: (i, 0)), pl.BlockSpec((1, D), lambda i: (0, 0))],
        out_specs=pl.BlockSpec((tt, D), lambda i: (i, 0)),
        out_shape=jax.ShapeDtypeStruct((T, D), BF16),
        compiler_params=_cparams(("parallel",)),
    )(x, g)


def _rmsnorm_bwd(dn, x, g, res, name):
    T, D = x.shape
    tt = _tile(T, 256)

    def body(dn_ref, x_ref, g_ref, res_ref, dx_ref, dxb_ref, dg_ref):
        @pl.when(pl.program_id(0) == 0)
        def _():
            dg_ref[...] = jnp.zeros_like(dg_ref)

        xv = x_ref[...]
        dy = dn_ref[...].astype(F32)
        r = lax.rsqrt(jnp.mean(xv * xv, axis=-1, keepdims=True) + EPS)
        xhat = xv * r
        dxh = dy * g_ref[...]
        dx = res_ref[...] + r * (dxh - xhat * jnp.mean(dxh * xhat, axis=-1, keepdims=True))
        dx_ref[...] = dx
        dxb_ref[...] = dx.astype(BF16)
        dg_ref[...] += jnp.sum(dy * xhat, axis=0, keepdims=True)

    tok = pl.BlockSpec((tt, D), lambda i: (i, 0))
    vec = pl.BlockSpec((1, D), lambda i: (0, 0))
    return pl.pallas_call(
        body, name=name, grid=(T // tt,),
        in_specs=[tok, tok, vec, tok],
        out_specs=[tok, tok, vec],
        out_shape=[jax.ShapeDtypeStruct((T, D), F32), jax.ShapeDtypeStruct((T, D), BF16),
                   jax.ShapeDtypeStruct((1, D), F32)],
        compiler_params=_cparams(("arbitrary",)),
    )(dn, x, g, res)


def _loss_and_final_bwd(h2, target, gf):
    T, D = h2.shape
    tt = _tile(T, 256)

    def body(h_ref, t_ref, g_ref, dh_ref, dhb_ref, dg_ref, loss_ref):
        @pl.when(pl.program_id(0) == 0)
        def _():
            dg_ref[...] = jnp.zeros_like(dg_ref)
            loss_ref[...] = jnp.zeros_like(loss_ref)

        xv = h_ref[...]
        r = lax.rsqrt(jnp.mean(xv * xv, axis=-1, keepdims=True) + EPS)
        xhat = xv * r
        err = xhat * g_ref[...] - t_ref[...]
        loss_ref[...] += 0.5 * jnp.sum(jnp.mean(err * err, axis=-1, keepdims=True), axis=0, keepdims=True)
        dy = err * (1.0 / D)
        dxh = dy * g_ref[...]
        dx = r * (dxh - xhat * jnp.mean(dxh * xhat, axis=-1, keepdims=True))
        dh_ref[...] = dx
        dhb_ref[...] = dx.astype(BF16)
        dg_ref[...] += jnp.sum(dy * xhat, axis=0, keepdims=True)

    tok = pl.BlockSpec((tt, D), lambda i: (i, 0))
    vec = pl.BlockSpec((1, D), lambda i: (0, 0))
    return pl.pallas_call(
        body, name="loss_final_bwd", grid=(T // tt,),
        in_specs=[tok, tok, vec],
        out_specs=[tok, tok, vec, pl.BlockSpec((1, 1), lambda i: (0, 0))],
        out_shape=[jax.ShapeDtypeStruct((T, D), F32), jax.ShapeDtypeStruct((T, D), BF16),
                   jax.ShapeDtypeStruct((1, D), F32), jax.ShapeDtypeStruct((1, 1), F32)],
        compiler_params=_cparams(("arbitrary",)),
    )(h2, target, gf)


def _gated_norm_fwd(y, proj, g):
    T, D = y.shape
    tt = _tile(T, 256)

    def body(y_ref, z_ref, g_ref, o_ref):
        yg = y_ref[...] * _silu(z_ref[...])
        r = lax.rsqrt(jnp.mean(yg * yg, axis=-1, keepdims=True) + EPS)
        o_ref[...] = (yg * r * g_ref[...]).astype(BF16)

    tok = pl.BlockSpec((tt, D), lambda i: (i, 0))
    return pl.pallas_call(
        body, name="gated_norm_fwd", grid=(T // tt,),
        in_specs=[tok, tok, pl.BlockSpec((1, D), lambda i: (0, 0))],
        out_specs=tok,
        out_shape=jax.ShapeDtypeStruct((T, 2 * D_MODEL), BF16),
        compiler_params=_cparams(("parallel",)),
    )(y, proj, g)


def _gated_norm_bwd(dmix, y, proj, g, dproj):
    T, D = y.shape
    tt = _tile(T, 256)

    def body(do_ref, y_ref, z_ref, g_ref, dp_ref, dy_ref, dz_ref, dg_ref):
        @pl.when(pl.program_id(0) == 0)
        def _():
            dg_ref[...] = jnp.zeros_like(dg_ref)

        yv, zv = y_ref[...], z_ref[...]
        do = do_ref[...].astype(F32)
        sz = _silu(zv)
        yg = yv * sz
        r = lax.rsqrt(jnp.mean(yg * yg, axis=-1, keepdims=True) + EPS)
        xhat = yg * r
        dxh = do * g_ref[...]
        dyg = r * (dxh - xhat * jnp.mean(dxh * xhat, axis=-1, keepdims=True))
        dy_ref[...] = dyg * sz
        dz_ref[...] = (dyg * yv * _dsilu(zv)).astype(BF16)
        dg_ref[...] += jnp.sum(do * xhat, axis=0, keepdims=True)

    tok = pl.BlockSpec((tt, D), lambda i: (i, 0))
    vec = pl.BlockSpec((1, D), lambda i: (0, 0))
    return pl.pallas_call(
        body, name="gated_norm_bwd", grid=(T // tt,),
        in_specs=[tok, tok, tok, vec, ANY],
        out_specs=[tok, tok, vec],
        out_shape=[jax.ShapeDtypeStruct((T, D), F32), jax.ShapeDtypeStruct(dproj.shape, BF16),
                   jax.ShapeDtypeStruct((1, D), F32)],
        input_output_aliases={4: 1},
        compiler_params=_cparams(("arbitrary",)),
    )(dmix, y, proj, g, dproj)


HALO = 8


def _shift_down(cur, prev8, s):
    ext = jnp.concatenate([prev8, cur], axis=0)
    return pltpu.roll(ext, s, axis=0)[HALO:]


def _shift_up(cur, next8, s):
    n = cur.shape[0]
    ext = jnp.concatenate([cur, next8], axis=0)
    return pltpu.roll(ext, n + HALO - s, axis=0)[:n]


def _conv_specs(tt, cb, col_off_blocks, nt):
    hb = tt // HALO
    cur = pl.BlockSpec((tt, cb), lambda j, i: (i, col_off_blocks + j))
    prev = pl.BlockSpec((HALO, cb), lambda j, i: (jnp.maximum(i * hb - 1, 0), col_off_blocks + j))
    nxt = pl.BlockSpec((HALO, cb), lambda j, i: (jnp.minimum((i + 1) * hb, nt * hb - 1), col_off_blocks + j))
    return cur, prev, nxt


def _causal_conv(cur, prev8, w, K):
    y = cur * w[K - 1:K, :]
    for k in range(K - 1):
        y = y + _shift_down(cur, prev8, K - 1 - k) * w[k:k + 1, :]
    return y


def _anticausal_conv(cur, next8, w, K):
    y = cur * w[K - 1:K, :]
    for k in range(K - 1):
        y = y + _shift_up(cur, next8, K - 1 - k) * w[k:k + 1, :]
    return y


def _ssm_conv_fwd(proj, w8, b):
    T = proj.shape[0]
    tt, cb = _tile(T, 512), 512
    nt = T // tt
    cur, prev, _ = _conv_specs(tt, cb, OFF_XBC // cb, nt)

    def body(u_ref, up_ref, w_ref, b_ref, o_ref):
        first = pl.program_id(1) == 0
        p8 = jnp.where(first, 0.0, up_ref[...])
        pre = _causal_conv(u_ref[...], p8, w_ref[...], K_SSM) + b_ref[...]
        o_ref[...] = _silu(pre)

    return pl.pallas_call(
        body, name="ssm_conv_fwd", grid=(D_XBC // cb, nt),
        in_specs=[cur, prev, pl.BlockSpec((8, cb), lambda j, i: (0, j)), pl.BlockSpec((1, cb), lambda j, i: (0, j))],
        out_specs=pl.BlockSpec((tt, cb), lambda j, i: (i, j)),
        out_shape=jax.ShapeDtypeStruct((T, D_XBC), F32),
        compiler_params=_cparams(("parallel", "parallel")),
    )(proj, proj, w8, b)


def _ssm_conv_bwd(dact, proj, w8, b, dproj):
    T = proj.shape[0]
    tt, cb = _tile(T, 512), 512
    nt = T // tt
    cur, prev, nxt = _conv_specs(tt, cb, OFF_XBC // cb, nt)
    dcur, dprev, dnxt = _conv_specs(tt, cb, 0, nt)

    def dpre_of(d, u, p8, w, bb):
        pre = _causal_conv(u, p8, w, K_SSM) + bb
        return d * _dsilu(pre)

    def body(d_ref, dn_ref, u_ref, up_ref, un_ref, w_ref, b_ref, dp_ref, dx_ref, dw_ref, db_ref):
        i = pl.program_id(1)

        @pl.when(i == 0)
        def _():
            dw_ref[...] = jnp.zeros_like(dw_ref)
            db_ref[...] = jnp.zeros_like(db_ref)

        w, bb = w_ref[...], b_ref[...]
        u = u_ref[...]
        p8 = jnp.where(i == 0, 0.0, up_ref[...])
        dpre = dpre_of(d_ref[...], u, p8, w, bb)
        un = un_ref[...]
        dpre_n = dpre_of(dn_ref[...], un, u[tt - HALO:, :], w, bb)
        dpre_n = jnp.where(i == nt - 1, 0.0, dpre_n)
        dx_ref[...] = _anticausal_conv(dpre, dpre_n, w, K_SSM).astype(BF16)
        rows = [jnp.sum(dpre * _shift_down(u, p8, K_SSM - 1 - k), axis=0, keepdims=True) for k in range(K_SSM - 1)]
        rows.append(jnp.sum(dpre * u, axis=0, keepdims=True))
        rows.append(jnp.zeros((8 - K_SSM, cb), F32))
        dw_ref[...] += jnp.concatenate(rows, axis=0)
        db_ref[...] += jnp.sum(dpre, axis=0, keepdims=True)

    wspec = pl.BlockSpec((8, cb), lambda j, i: (0, j))
    bspec = pl.BlockSpec((1, cb), lambda j, i: (0, j))
    return pl.pallas_call(
        body, name="ssm_conv_bwd", grid=(D_XBC // cb, nt),
        in_specs=[dcur, dnxt, cur, prev, nxt, wspec, bspec, ANY],
        out_specs=[pl.BlockSpec((tt, cb), lambda j, i: (i, OFF_XBC // cb + j)), wspec, bspec],
        out_shape=[jax.ShapeDtypeStruct(dproj.shape, BF16), jax.ShapeDtypeStruct((8, D_XBC), F32),
                   jax.ShapeDtypeStruct((1, D_XBC), F32)],
        input_output_aliases={7: 0},
        compiler_params=_cparams(("parallel", "arbitrary")),
    )(dact, dact, proj, proj, proj, w8, b, dproj)


SCB = 512
SC3 = 3 * SCB


def _sc_specs(tt, nt):
    hb = tt // HALO
    cur = pl.BlockSpec((tt, SC3), lambda j, i: (i, OFF_CB // SC3 + j))
    prev = pl.BlockSpec((HALO, SC3), lambda j, i: (jnp.maximum(i * hb - 1, 0), OFF_CB // SC3 + j))
    nxt = pl.BlockSpec((HALO, SC3), lambda j, i: (jnp.minimum((i + 1) * hb, nt * hb - 1), OFF_CB // SC3 + j))
    return cur, prev, nxt


def _shortconv_fwd(proj, w8, ymix):
    T = proj.shape[0]
    tt = _tile(T, 512)
    nt = T // tt
    cur, prev, _ = _sc_specs(tt, nt)

    def body(p_ref, pp_ref, w_ref, y_ref, o_ref):
        p, pp = p_ref[...], pp_ref[...]
        v = p[:, SCB:2 * SCB] * p[:, 2 * SCB:]
        vp = jnp.where(pl.program_id(1) == 0, 0.0, pp[:, SCB:2 * SCB] * pp[:, 2 * SCB:])
        o_ref[...] = (p[:, :SCB] * _causal_conv(v, vp, w_ref[...], K_SC)).astype(BF16)

    return pl.pallas_call(
        body, name="shortconv_fwd", grid=(D_MODEL // SCB, nt),
        in_specs=[cur, prev, pl.BlockSpec((8, SCB), lambda j, i: (0, j)), ANY],
        out_specs=pl.BlockSpec((tt, SCB), lambda j, i: (i, D_SSM // SCB + j)),
        out_shape=jax.ShapeDtypeStruct(ymix.shape, BF16),
        input_output_aliases={3: 0},
        compiler_params=_cparams(("parallel", "parallel")),
    )(proj, proj, w8, ymix)


def _shortconv_bwd(dmix, proj, w8):
    T = proj.shape[0]
    tt = _tile(T, 512)
    nt = T // tt
    hb = tt // HALO
    cur, prev, nxt = _sc_specs(tt, nt)
    d_s = pl.BlockSpec((tt, SCB), lambda j, i: (i, D_SSM // SCB + j))
    dn_s = pl.BlockSpec((HALO, SCB), lambda j, i: (jnp.minimum((i + 1) * hb, nt * hb - 1), D_SSM // SCB + j))

    def body(d_ref, dn_ref, p_ref, pp_ref, pn_ref, w_ref, dp_ref, dw_ref):
        i = pl.program_id(1)

        @pl.when(i == 0)
        def _():
            dw_ref[...] = jnp.zeros_like(dw_ref)

        w = w_ref[...]
        p, pp = p_ref[...], pp_ref[...]
        gb, gc, u = p[:, :SCB], p[:, SCB:2 * SCB], p[:, 2 * SCB:]
        v = gc * u
        vp = jnp.where(i == 0, 0.0, pp[:, SCB:2 * SCB] * pp[:, 2 * SCB:])
        d = d_ref[...].astype(F32)
        dp_ref[:, :SCB] = (d * _causal_conv(v, vp, w, K_SC)).astype(BF16)
        dcv = d * gb
        dcv_n = jnp.where(i == nt - 1, 0.0, dn_ref[...].astype(F32) * pn_ref[:, :SCB])
        dv = _anticausal_conv(dcv, dcv_n, w, K_SC)
        dp_ref[:, SCB:2 * SCB] = (dv * u).astype(BF16)
        dp_ref[:, 2 * SCB:] = (dv * gc).astype(BF16)
        rows = [jnp.sum(dcv * _shift_down(v, vp, K_SC - 1 - k), axis=0, keepdims=True) for k in range(K_SC - 1)]
        rows.append(jnp.sum(dcv * v, axis=0, keepdims=True))
        rows.append(jnp.zeros((8 - K_SC, SCB), F32))
        dw_ref[...] += jnp.concatenate(rows, axis=0)

    wspec = pl.BlockSpec((8, SCB), lambda j, i: (0, j))
    return pl.pallas_call(
        body, name="shortconv_bwd", grid=(D_MODEL // SCB, nt),
        in_specs=[d_s, dn_s, cur, prev, nxt, wspec],
        out_specs=[cur, wspec],
        out_shape=[jax.ShapeDtypeStruct((T, D_MAIN), BF16), jax.ShapeDtypeStruct((8, D_MODEL), F32)],
        compiler_params=_cparams(("parallel", "arbitrary")),
    )(dmix, dmix, proj, proj, proj, w8)


GW = HEADS_PER_GROUP * HEADDIM


def _dot(a, b):
    return jnp.dot(a.astype(BF16), b.astype(BF16), preferred_element_type=F32)


def _dot_nt(a, b):
    return lax.dot_general(a.astype(BF16), b.astype(BF16), (((1,), (1,)), ((), ())), preferred_element_type=F32)


def _dot_tn(a, b):
    return lax.dot_general(a.astype(BF16), b.astype(BF16), (((0,), (0,)), ((), ())), preferred_element_type=F32)


def _bf16_terms(x, n):
    terms, r = [], x
    for _ in range(n):
        t = r.astype(BF16)
        terms.append(t)
        r = r - t.astype(F32)
    return terms


def _dot_sel(a, sel, n=2):
    s = sel.astype(BF16)
    return sum(jnp.dot(t, s, preferred_element_type=F32) for t in _bf16_terms(a, n))


def _sel_dot(sel, b, n=2):
    s = sel.astype(BF16)
    return sum(jnp.dot(s, t, preferred_element_type=F32) for t in _bf16_terms(b, n))


def _sel_dot_nt(sel, b, n=2):
    s = sel.astype(BF16)
    return sum(lax.dot_general(s, t, (((1,), (1,)), ((), ())), preferred_element_type=F32)
               for t in _bf16_terms(b, n))


def _head_cols(rows):
    parts = [jnp.broadcast_to(rows[r:r + 1, :], (HEADDIM, CHUNK)) for r in range(HEADS_PER_GROUP)]
    return jnp.concatenate(parts, axis=0).T


def _head_rows(rows):
    parts = [jnp.broadcast_to(rows[r:r + 1, :], (HEADDIM, N_STATE)) for r in range(HEADS_PER_GROUP)]
    return jnp.concatenate(parts, axis=0)


def _ssd_common(dtr, bias, alog):
    dt = _softplus(dtr + bias)
    A = -jnp.exp(alog)
    a = dt * A
    ki = lax.broadcasted_iota(jnp.int32, (CHUNK, CHUNK), 0)
    si = lax.broadcasted_iota(jnp.int32, (CHUNK, CHUNK), 1)
    upper = (ki <= si).astype(F32)
    cs = _dot_sel(a, upper, 3)
    cs_last = jnp.broadcast_to(cs[:, CHUNK - 1:CHUNK], (8, CHUNK))
    return dt, A, a, cs, cs_last


def _decay_matrix(cs, r):
    li = lax.broadcasted_iota(jnp.int32, (CHUNK, CHUNK), 0)
    si = lax.broadcasted_iota(jnp.int32, (CHUNK, CHUNK), 1)
    causal = li >= si
    R = jnp.broadcast_to(cs[r:r + 1, :], (CHUNK, CHUNK))
    seg = jnp.where(causal, R.T - R, 0.0)
    return jnp.where(causal, jnp.exp(seg), 0.0)


GXBC = GW + 2 * N_STATE


GS = 2


def _ssd_in_specs(nc, rev):
    cix = (lambda c: nc - 1 - c) if rev else (lambda c: c)
    x_s = pl.BlockSpec((CHUNK, GS * GW), lambda g, c: (cix(c), g))
    xbc_s = pl.BlockSpec((CHUNK, GS * GXBC), lambda g, c: (cix(c), g))
    dtr_s = pl.BlockSpec((GS, 8, CHUNK), lambda g, c: (g, 0, cix(c)))
    row_s = pl.BlockSpec((GS, 8, CHUNK), lambda g, c: (g, 0, 0))
    drep_s = pl.BlockSpec((1, GS * GW), lambda g, c: (0, g))
    hs_s = pl.BlockSpec((1, GS * GW, N_STATE), lambda g, c: (cix(c), g, 0))
    return x_s, xbc_s, dtr_s, row_s, drep_s, hs_s


def _xbc_parts(xbc_ref, gi):
    o = gi * GXBC
    return xbc_ref[:, o:o + GW], xbc_ref[:, o + GW:o + GW + N_STATE], xbc_ref[:, o + GW + N_STATE:o + GXBC]


def _ssd_fwd(xbc, dtr, bias, alog, drep):
    T = xbc.shape[0]
    nc = T // CHUNK
    x_s, xbc_s, dtr_s, row_s, drep_s, hs_s = _ssd_in_specs(nc, False)

    def body(xbc_ref, dtr_ref, bias_ref, alog_ref, drep_ref, y_ref, hs_ref, h_scr):
        @pl.when(pl.program_id(1) == 0)
        def _():
            h_scr[...] = jnp.zeros_like(h_scr)

        for gi in range(GS):
            cols, rows = slice(gi * GW, (gi + 1) * GW), pl.ds(gi * GW, GW)
            x, Bm, Cm = _xbc_parts(xbc_ref, gi)
            dt, A, a, cs, cs_last = _ssd_common(dtr_ref[gi], bias_ref[gi], alog_ref[gi])
            E = _head_cols(jnp.exp(cs))
            W = _head_cols(jnp.exp(cs_last - cs) * dt)
            X = (x * _head_cols(dt)).astype(BF16)
            CB = _dot_nt(Cm, Bm)
            col = lax.broadcasted_iota(jnp.int32, (CHUNK, GW), 1) // HEADDIM
            y = jnp.zeros((CHUNK, GW), F32)
            for r in range(HEADS_PER_GROUP):
                M = CB * _decay_matrix(cs, r)
                y = y + jnp.where(col == r, _dot(M, X), 0.0)
            h = h_scr[rows, :]
            hs_ref[0, rows, :] = h
            y = y + _dot_nt(Cm, h) * E
            y_ref[:, cols] = y + drep_ref[:, cols] * x
            h_scr[rows, :] = h * _head_rows(jnp.exp(cs_last)) + _dot_tn(x * W, Bm)

    return pl.pallas_call(
        body, name="ssd_fwd", grid=(N_GROUPS // GS, nc),
        in_specs=[xbc_s, dtr_s, row_s, row_s, drep_s],
        out_specs=[x_s, hs_s],
        out_shape=[jax.ShapeDtypeStruct((T, D_SSM), F32), jax.ShapeDtypeStruct((nc, D_SSM, N_STATE), F32)],
        scratch_shapes=[pltpu.VMEM((GS * GW, N_STATE), F32)],
        compiler_params=_cparams(("parallel", "arbitrary")),
    )(xbc, dtr, bias, alog, drep)


def _ssd_bwd(xbc, dtr, bias, alog, drep, dy, hs):
    T = xbc.shape[0]
    nc = T // CHUNK
    x_s, xbc_s, dtr_s, row_s, drep_s, hs_s = _ssd_in_specs(nc, True)

    def body(xbc_ref, dtr_ref, bias_ref, alog_ref, drep_ref, dy_ref, hs_ref,
             dxbc_ref, ddtr_ref, dbias_ref, dalog_ref, dd_ref, dh_scr):
        @pl.when(pl.program_id(1) == 0)
        def _():
            dh_scr[...] = jnp.zeros_like(dh_scr)
            dbias_ref[...] = jnp.zeros_like(dbias_ref)
            dalog_ref[...] = jnp.zeros_like(dalog_ref)
            dd_ref[...] = jnp.zeros_like(dd_ref)

        for gi in range(GS):
            one_group(gi, xbc_ref, dtr_ref, bias_ref, alog_ref, drep_ref, dy_ref, hs_ref,
                      dxbc_ref, ddtr_ref, dbias_ref, dalog_ref, dd_ref, dh_scr)

    def one_group(gi, xbc_ref, dtr_ref, bias_ref, alog_ref, drep_ref, dy_ref, hs_ref,
                  dxbc_ref, ddtr_ref, dbias_ref, dalog_ref, dd_ref, dh_scr):
        cols, rows, o = slice(gi * GW, (gi + 1) * GW), pl.ds(gi * GW, GW), gi * GXBC
        x, Bm, Cm = _xbc_parts(xbc_ref, gi)
        dY = dy_ref[:, cols]
        dt, A, a, cs, cs_last = _ssd_common(dtr_ref[gi], bias_ref[gi], alog_ref[gi])
        E = _head_cols(jnp.exp(cs))
        DT = _head_cols(dt)
        Wd = _head_cols(jnp.exp(cs_last - cs))
        X = x * DT
        h = hs_ref[0, rows, :]
        dS = dh_scr[rows, :]
        CB = _dot_nt(Cm, Bm)
        col = lax.broadcasted_iota(jnp.int32, (CHUNK, GW), 1) // HEADDIM
        rowid = lax.broadcasted_iota(jnp.int32, (8, CHUNK), 0)
        lane = lax.broadcasted_iota(jnp.int32, (8, CHUNK), 1)
        hsel = (lax.broadcasted_iota(jnp.int32, (8, GW), 1) // HEADDIM
                == lax.broadcasted_iota(jnp.int32, (8, GW), 0)).astype(F32)
        ones8 = jnp.ones((8, CHUNK), F32)

        dX = jnp.zeros((CHUNK, GW), F32)
        dCB = jnp.zeros((CHUNK, CHUNK), F32)
        dcs = jnp.zeros((8, CHUNK), F32)
        for r in range(HEADS_PER_GROUP):
            L = _decay_matrix(cs, r)
            M = CB * L
            G = _dot_nt(jnp.where(col == r, dY, 0.0), X)
            GL = G * L
            dCB = dCB + GL
            Wm = GL * CB
            colsum = jnp.sum(Wm, axis=0, keepdims=True)
            rowsum = _sel_dot_nt(ones8, Wm)
            dcs = dcs + jnp.where(rowid == r, rowsum - colsum, 0.0)
            dX = dX + jnp.where(col == r, _dot_tn(M, dY), 0.0)
        dC = _dot(dCB, Bm)
        dB = _dot_tn(dCB, Cm)
        T1 = _dot_nt(Bm, dS)
        dX = dX + T1 * Wd
        dB = dB + _dot(X * Wd, dS)
        pdec = _sel_dot_nt(hsel, X * T1 * Wd)
        dcs = dcs - pdec
        dlast = jnp.sum(pdec, axis=1, keepdims=True) \
            + jnp.exp(cs_last[:, 0:1]) * jnp.sum(_sel_dot(hsel, dS * h), axis=1, keepdims=True)
        dYE = dY * E
        dC = dC + _dot(dYE, h)
        yoff = _dot_nt(Cm, h) * E
        dcs = dcs + _sel_dot_nt(hsel, dY * yoff)
        dcs = dcs + jnp.where(lane == CHUNK - 1, dlast, 0.0)
        ki = lax.broadcasted_iota(jnp.int32, (CHUNK, CHUNK), 0)
        si = lax.broadcasted_iota(jnp.int32, (CHUNK, CHUNK), 1)
        lower = (ki >= si).astype(F32)
        da = _dot_sel(dcs, lower)
        ddt = da * A + _sel_dot_nt(hsel, dX * x)
        ddtr = ddt * _sigmoid(dtr_ref[gi] + bias_ref[gi])
        ddtr_ref[gi] = ddtr
        dbias_ref[gi] += ddtr
        dalog_ref[gi] += da * a
        dxbc_ref[:, o:o + GW] = dX * DT + drep_ref[:, cols] * dY
        dd_ref[:, cols] += jnp.sum(dY * x, axis=0, keepdims=True)
        dxbc_ref[:, o + GW:o + GW + N_STATE] = dB
        dxbc_ref[:, o + GW + N_STATE:o + GXBC] = dC
        dh_scr[rows, :] = dS * _head_rows(jnp.exp(cs_last)) + _dot_tn(dYE, Cm)

    return pl.pallas_call(
        body, name="ssd_bwd", grid=(N_GROUPS // GS, nc),
        in_specs=[xbc_s, dtr_s, row_s, row_s, drep_s, x_s, hs_s],
        out_specs=[xbc_s, dtr_s, row_s, row_s, drep_s],
        out_shape=[jax.ShapeDtypeStruct((T, D_XBC), F32),
                   jax.ShapeDtypeStruct((N_GROUPS, 8, T), F32),
                   jax.ShapeDtypeStruct((N_GROUPS, 8, CHUNK), F32),
                   jax.ShapeDtypeStruct((N_GROUPS, 8, CHUNK), F32),
                   jax.ShapeDtypeStruct((1, D_SSM), F32)],
        scratch_shapes=[pltpu.VMEM((GS * GW, N_STATE), F32)],
        compiler_params=_cparams(("parallel", "arbitrary")),
    )(xbc, dtr, bias, alog, drep, dy, hs)


def _adamw(w, g, m, v, name, deps=(), emit_g=False):
    R, C = w.shape
    tr = _tile(R, 256, 8)
    nd = len(deps)
    nout = 4 if emit_g else 3

    def body(w_ref, g_ref, m_ref, v_ref, *rest):
        outs = rest[nd:]
        gv = g_ref[...]
        mn = ADAM_B1 * m_ref[...] + (1.0 - ADAM_B1) * gv
        vn = ADAM_B2 * v_ref[...] + (1.0 - ADAM_B2) * (gv * gv)
        m_hat = mn / (1.0 - ADAM_B1 ** ADAM_STEP)
        v_hat = vn / (1.0 - ADAM_B2 ** ADAM_STEP)
        outs[0][...] = -ADAM_LR * (m_hat / (jnp.sqrt(v_hat) + ADAM_EPS) + ADAM_WD * w_ref[...])
        outs[1][...] = mn
        outs[2][...] = vn
        if emit_g:
            outs[3][...] = gv

    spec = pl.BlockSpec((tr, C), lambda i: (i, 0))
    return pl.pallas_call(
        body, name=name, grid=(R // tr,),
        in_specs=[spec] * 4 + [ANY] * nd, out_specs=[spec] * nout,
        out_shape=[jax.ShapeDtypeStruct((R, C), F32)] * nout,
        compiler_params=_cparams(("parallel",)),
    )(w, g, m, v, *deps)


ANY = pl.BlockSpec(memory_space=pl.ANY)


def _place():
    x, y, c = lax.axis_index("x"), lax.axis_index("y"), lax.axis_index("c")
    return x, y, c


def _other_chips(x, y):
    return [(1 - x, y), (x, 1 - y), (1 - x, 1 - y)]


def _allgather_inplace(bufs, splits):
    n = len(bufs)

    def body(*refs):
        o_refs = refs[n:2 * n]
        send_sems, recv_sems = refs[2 * n:]
        x, y, c = _place()
        xn, yn, dg, sibling = (1 - x, y), (x, 1 - y), (1 - x, 1 - y), (x, y, 1 - c)

        def blk(k, chip, pc):
            return o_refs[k].at[4 * chip[0] + 2 * chip[1] + pc]

        def part(k, ref, p):
            kind, s = splits[k]
            _, R, C = bufs[k].shape
            if kind == "rows":
                return ref.at[pl.ds(0, s)] if p == 0 else ref.at[pl.ds(s, R - s)]
            return ref.at[:, pl.ds(0, s)] if p == 0 else ref.at[:, pl.ds(s, C - s)]

        def copy(k, slot, ref, to):
            return pltpu.make_async_remote_copy(
                src_ref=ref, dst_ref=ref, send_sem=send_sems.at[k, slot], recv_sem=recv_sems.at[k, slot],
                device_id=to, device_id_type=MESH)

        sent = []

        def send(k, slot, ref, to):
            cp = copy(k, slot, ref, to)
            cp.start()
            sent.append(cp)

        for k in range(n):
            send(k, 0, blk(k, (x, y), c), (*xn, c))
            send(k, 1, blk(k, (x, y), c), (*yn, c))
        for k in range(n):
            bx, by = blk(k, xn, c), blk(k, yn, c)
            copy(k, 0, bx, sibling).wait_recv()
            send(k, 2, part(k, bx, 0), (*yn, c))
            send(k, 4, bx, sibling)
            copy(k, 1, by, sibling).wait_recv()
            send(k, 3, part(k, by, 1), (*xn, c))
            send(k, 5, by, sibling)
        for k in range(n):
            d0, d1 = part(k, blk(k, dg, c), 0), part(k, blk(k, dg, c), 1)
            copy(k, 2, d0, sibling).wait_recv()
            send(k, 6, d0, sibling)
            copy(k, 3, d1, sibling).wait_recv()
            send(k, 7, d1, sibling)
        for k in range(n):
            copy(k, 4, blk(k, xn, 1 - c), sibling).wait_recv()
            copy(k, 5, blk(k, yn, 1 - c), sibling).wait_recv()
            copy(k, 6, part(k, blk(k, dg, 1 - c), 0), sibling).wait_recv()
            copy(k, 7, part(k, blk(k, dg, 1 - c), 1), sibling).wait_recv()
        for cp in sent:
            cp.wait_send()

    return pl.pallas_call(
        body, name="allgather_w_in",
        in_specs=[ANY] * n, out_specs=[ANY] * n,
        out_shape=[jax.ShapeDtypeStruct(b.shape, b.dtype) for b in bufs],
        input_output_aliases={k: k for k in range(n)},
        scratch_shapes=[pltpu.SemaphoreType.DMA((n, 8)), pltpu.SemaphoreType.DMA((n, 8))],
    )(*bufs)


HBM = pl.BlockSpec(memory_space=pltpu.HBM)
SEM = pl.BlockSpec(memory_space=pltpu.SEMAPHORE)
EFFECT = pltpu.SideEffectType.DATAFLOW_SIDE_EFFECTING


def _split_start(name, arrays, build, n_copies, after=()):
    na, nd = len(arrays), len(after)

    def body(*refs):
        send_sems, recv_sems = refs[na + nd], refs[na + nd + 1]
        for cp in build(refs[:na], send_sems, recv_sems):
            cp.start()
        refs[-1][...] = jnp.zeros((8, 128), F32)

    outs = pl.pallas_call(
        body, name=name,
        out_shape=(pltpu.SemaphoreType.DMA((n_copies,)), pltpu.SemaphoreType.DMA((n_copies,)),
                   *[pltpu.HBM(a.shape, a.dtype) for a in arrays], jax.ShapeDtypeStruct((8, 128), F32)),
        in_specs=[HBM] * na + [ANY] * nd,
        out_specs=(SEM, SEM, *[HBM] * na, pl.BlockSpec(memory_space=pltpu.VMEM)),
        input_output_aliases={i: 2 + i for i in range(na)},
        compiler_params=pltpu.CompilerParams(has_side_effects=EFFECT),
    )(*[pltpu.with_memory_space_constraint(a, pltpu.HBM) for a in arrays], *after)
    return outs[0], outs[1], list(outs[2:2 + na]), outs[-1]


def _split_wait(name, send_sems, recv_sems, arrays, build, after):
    na = len(arrays)

    def body(*refs):
        for cp in build(refs[:na], refs[na], refs[na + 1]):
            cp.wait_send()
            cp.wait_recv()

    outs = pl.pallas_call(
        body, name=name,
        out_shape=tuple(pltpu.HBM(a.shape, a.dtype) for a in arrays),
        in_specs=[HBM] * na + [SEM, SEM] + [ANY] * len(after),
        out_specs=tuple([HBM] * na),
        input_output_aliases={i: i for i in range(na)},
        compiler_params=pltpu.CompilerParams(has_side_effects=EFFECT),
    )(*arrays, send_sems, recv_sems, *after)
    return list(outs)


def _remote(src, dst, send_sems, recv_sems, i, to):
    return pltpu.make_async_remote_copy(src_ref=src, dst_ref=dst, send_sem=send_sems.at[i], recv_sem=recv_sems.at[i],
                                        device_id=to, device_id_type=MESH)


def _build_ag_ici(refs, ss, rs):
    x, y, c = _place()
    cps = []
    for k, ref in enumerate(refs):
        blk = ref.at[4 * x + 2 * y + c]
        for j, (px, py) in enumerate(_other_chips(x, y)):
            cps.append(_remote(blk, blk, ss, rs, 3 * k + j, (px, py, c)))
    return cps


def _build_ag_fwd(refs, ss, rs):
    x, y, c = _place()
    cps = []
    for k, ref in enumerate(refs):
        for j, (px, py) in enumerate(_other_chips(x, y)):
            blk = ref.at[4 * px + 2 * py + c]
            cps.append(_remote(blk, blk, ss, rs, 3 * k + j, (x, y, 1 - c)))
    return cps


def _build_rs_swap(refs, ss, rs):
    x, y, c = _place()
    n = len(refs) // 2
    return [_remote(refs[k].at[:, pl.ds(1 - c, 1)], refs[n + k], ss, rs, k, (x, y, 1 - c)) for k in range(n)]


def _build_rs_ici(refs, ss, rs):
    x, y, c = _place()
    n = len(refs) // 2
    me = 2 * x + y
    cps = []
    for k in range(n):
        for j, (px, py) in enumerate(_other_chips(x, y)):
            cps.append(_remote(refs[k].at[2 * px + py], refs[n + k].at[me], ss, rs, 3 * k + j, (px, py, c)))
    return cps


def _build_rs_share(refs, ss, rs):
    x, y, c = _place()
    return [_remote(ref.at[c], ref.at[c], ss, rs, k, (x, y, 1 - c)) for k, ref in enumerate(refs)]


def _allreduce_small(p, deps=()):
    R, C = p.shape
    nd = len(deps)

    def body(p_ref, *rest):
        gath_ref, sum_ref, send_sems, recv_sems, local_sem = rest[nd:]
        x, y, c = _place()
        me, sibling = (x, y, c), (x, y, 1 - c)
        chips = [(1 - x, y), (x, 1 - y), (1 - x, 1 - y)]

        def blk(px, py, pc):
            return gath_ref.at[4 * px + 2 * py + pc]

        def copy(k, block, to, src=None):
            return pltpu.make_async_remote_copy(
                src_ref=blk(*block) if src is None else src, dst_ref=blk(*block),
                send_sem=send_sems.at[k], recv_sem=recv_sems.at[k], device_id=to, device_id_type=MESH)

        mine = pltpu.make_async_copy(p_ref, blk(*me), local_sem)
        mine.start()
        first = [copy(0, me, sibling, src=p_ref)]
        first += [copy(1 + j, me, (*chip, c), src=p_ref) for j, chip in enumerate(chips)]
        for cp in first:
            cp.start()
        passed = [copy(4 + j, (*chip, c), sibling) for j, chip in enumerate(chips)]
        for j, chip in enumerate(chips):
            copy(1 + j, (*chip, c), me).wait_recv()
            passed[j].start()
        copy(0, sibling, me).wait_recv()
        for j, chip in enumerate(chips):
            copy(4 + j, (*chip, 1 - c), me).wait_recv()
        for cp in first + passed:
            cp.wait_send()
        mine.wait()
        s = gath_ref[0]
        for d in range(1, N_DEV):
            s = s + gath_ref[d]
        sum_ref[...] = s

    vm = pl.BlockSpec(memory_space=pltpu.VMEM)
    return pl.pallas_call(
        body, name="allreduce_small",
        in_specs=[vm] + [ANY] * nd, out_specs=[vm, vm],
        out_shape=[jax.ShapeDtypeStruct((N_DEV, R, C), F32), jax.ShapeDtypeStruct((R, C), F32)],
        scratch_shapes=[pltpu.SemaphoreType.DMA((7,)), pltpu.SemaphoreType.DMA((7,)), pltpu.SemaphoreType.DMA],
    )(p, *deps)[1]


def _rs_add_pair(p, r0, c_arr, name):
    _, _, hr, cols = p.shape
    tr = _tile(hr, 256, 8)

    def body(c_ref, p_ref, r_ref, q_ref):
        q_ref[...] = (p_ref[0] + r_ref[0]).astype(BF16)

    grid_spec = pltpu.PrefetchScalarGridSpec(
        num_scalar_prefetch=1, grid=(N_CHIPS, hr // tr),
        in_specs=[pl.BlockSpec((1, 1, tr, cols), lambda j, i, c_ref: (j, c_ref[0], i, 0)),
                  pl.BlockSpec((1, 1, tr, cols), lambda j, i, c_ref: (j, 0, i, 0))],
        out_specs=pl.BlockSpec((1, tr, cols), lambda j, i, c_ref: (j, i, 0)))
    return pl.pallas_call(
        body, name=name, grid_spec=grid_spec,
        out_shape=jax.ShapeDtypeStruct((N_CHIPS, hr, cols), BF16),
        compiler_params=_cparams(("parallel", "parallel")),
    )(c_arr, p, r0)


def _rs_add_chips(r1, q, place_arr, name):
    _, hr, cols = r1.shape
    tr = _tile(hr, 256, 8)

    def body(place_ref, r_ref, q_ref, o_ref):
        chip = place_ref[0]
        s = None
        for j in range(N_CHIPS):
            t = jnp.where(chip == j, q_ref[j], r_ref[j]).astype(F32)
            s = t if s is None else s + t
        o_ref[...] = s

    blk = pl.BlockSpec((N_CHIPS, tr, cols), lambda i, place_ref: (0, i, 0))
    grid_spec = pltpu.PrefetchScalarGridSpec(
        num_scalar_prefetch=1, grid=(hr // tr,), in_specs=[blk, blk],
        out_specs=pl.BlockSpec((None, tr, cols), lambda i, place_ref: (place_ref[1], i, 0)))
    return pl.pallas_call(
        body, name=name, grid_spec=grid_spec,
        out_shape=jax.ShapeDtypeStruct((2, hr, cols), F32),
        compiler_params=_cparams(("parallel",)),
    )(place_arr, r1, q)


def _pad_rows(a, rows):
    return jnp.pad(a, ((0, rows - a.shape[0]), (0, 0)))


def _pad_cols(a, cols):
    return jnp.pad(a, ((0, 0), (0, cols - a.shape[1])))


def _heads_to_rows(v):
    v = v.reshape(N_GROUPS, HEADS_PER_GROUP, 1)
    v = jnp.pad(v, ((0, 0), (0, 8 - HEADS_PER_GROUP), (0, 0)))
    return jnp.broadcast_to(v, (N_GROUPS, 8, CHUNK))


def _rows_to_heads(a):
    return jnp.sum(a[:, :HEADS_PER_GROUP, :], axis=-1).reshape(N_HEADS)


def _to_kernel_rows(a):
    C = a.shape[1]
    x0, b0, c0, s0 = D_SSM, 2 * D_SSM, 2 * D_SSM + 1024, D_SSM + D_XBC + N_HEADS
    xbc = jnp.concatenate([a[x0:b0].reshape(N_GROUPS, GW, C), a[b0:c0].reshape(N_GROUPS, N_STATE, C),
                           a[c0:c0 + 1024].reshape(N_GROUPS, N_STATE, C)], axis=1).reshape(D_XBC, C)
    sc = jnp.concatenate([a[s0 + k * D_MODEL:s0 + (k + 1) * D_MODEL].reshape(D_MODEL // SCB, SCB, C)
                          for k in range(3)], axis=1).reshape(3 * D_MODEL, C)
    return jnp.concatenate([a[:D_SSM], xbc, sc], axis=0)


HR_IN = 1568


def _shard_row_plan():
    segs = [(0, 0, 0, D_SSM)]
    for g in range(N_GROUPS):
        k0 = D_SSM + g * GXBC
        segs += [(0, k0, D_SSM + g * GW, GW), (0, k0 + GW, 2 * D_SSM + g * N_STATE, N_STATE),
                 (0, k0 + GW + N_STATE, 2 * D_SSM + 1024 + g * N_STATE, N_STATE)]
    segs.append((1, 0, D_SSM + D_XBC, N_HEADS))
    for j in range(D_MODEL // SCB):
        for k in range(3):
            segs.append((0, D_SSM + D_XBC + j * SC3 + k * SCB, D_SSM + D_XBC + N_HEADS + k * D_MODEL + j * SCB, SCB))
    cs = D_IN // N_CHIPS
    plan = []
    for src, s, o, n in segs:
        while n > 0:
            chip, loc = divmod(o, cs)
            half, row = divmod(loc, HR_IN)
            m = min(n, cs - loc, HR_IN - row)
            plan.append((src, s, chip, half, row, m))
            s, o, n = s + m, o + m, n - m
    return plan


SCATTER_ROWS = 512
SCATTER_SLOTS = 4


def _scatter_rows_to_shards(k_main, k_dt):
    C = k_main.shape[1]
    pieces = []
    for src, s, chip, half, row, n in _shard_row_plan():
        for o in range(0, n, SCATTER_ROWS):
            pieces.append((src, s + o, chip, half, row + o, min(SCATTER_ROWS, n - o)))
    S, lag, N = SCATTER_SLOTS, SCATTER_SLOTS // 2, len(pieces)

    def body(m_ref, d_ref, o_ref, buf, in_sems, out_sems):
        def cin(i):
            src, s, _, _, _, n = pieces[i]
            return pltpu.make_async_copy((d_ref if src else m_ref).at[pl.ds(s, n)],
                                         buf.at[i % S, pl.ds(0, n)], in_sems.at[i % S])

        def cout(i):
            _, _, chip, half, row, n = pieces[i]
            return pltpu.make_async_copy(buf.at[i % S, pl.ds(0, n)],
                                         o_ref.at[chip, half, pl.ds(row, n)], out_sems.at[i % S])

        for i in range(N + lag):
            if i < N:
                if i >= S:
                    cout(i - S).wait()
                cin(i).start()
            j = i - lag
            if 0 <= j < N:
                cin(j).wait()
                cout(j).start()
        for j in range(max(0, N - S), N):
            cout(j).wait()

    return pl.pallas_call(
        body, name="scatter_dw_in_rows", in_specs=[ANY, ANY], out_specs=ANY,
        out_shape=jax.ShapeDtypeStruct((N_CHIPS, 2, HR_IN, C), k_main.dtype),
        scratch_shapes=[pltpu.VMEM((S, SCATTER_ROWS, C), k_main.dtype),
                        pltpu.SemaphoreType.DMA((S,)), pltpu.SemaphoreType.DMA((S,))],
        compiler_params=_cparams(),
    )(k_main, k_dt)


def _to_kernel_xbc(a):
    R = a.shape[0]
    return jnp.concatenate([a[:, :D_SSM].reshape(R, N_GROUPS, GW), a[:, D_SSM:D_SSM + 1024].reshape(R, N_GROUPS, N_STATE),
                            a[:, D_SSM + 1024:].reshape(R, N_GROUPS, N_STATE)], axis=2).reshape(R, D_XBC)


def _from_kernel_xbc(a):
    R = a.shape[0]
    g = a.reshape(R, N_GROUPS, GXBC)
    return jnp.concatenate([g[:, :, :GW].reshape(R, D_SSM), g[:, :, GW:GW + N_STATE].reshape(R, 1024),
                            g[:, :, GW + N_STATE:].reshape(R, 1024)], axis=1)


def kernel(x, norm_mix_g, w_in, ssm_conv_w, ssm_conv_b, ssm_dt_bias, ssm_A_log, ssm_D, ssm_norm_g, sc_conv_w, w_out, norm_ffn_g, w_gate, w_up, w_down, norm_final_g, loss_target, m_norm_mix_g, m_w_in, m_ssm_conv_w, m_ssm_conv_b, m_ssm_dt_bias, m_ssm_A_log, m_ssm_D, m_ssm_norm_g, m_sc_conv_w, m_w_out, m_norm_ffn_g, m_w_gate, m_w_up, m_w_down, m_norm_final_g, v_norm_mix_g, v_w_in, v_ssm_conv_w, v_ssm_conv_b, v_ssm_dt_bias, v_ssm_A_log, v_ssm_D, v_ssm_norm_g, v_sc_conv_w, v_w_out, v_norm_ffn_g, v_w_gate, v_w_up, v_w_down, v_norm_final_g):
    T = x.shape[1]
    xt = x[0]
    tgt = loss_target[0]
    cx, cy, cc = lax.axis_index("x"), lax.axis_index("y"), lax.axis_index("c")
    chip = 2 * cx + cy
    c_arr = jnp.reshape(cc, (1,)).astype(jnp.int32)
    chip_arr = jnp.reshape(chip, (1,)).astype(jnp.int32)
    place_arr = jnp.stack([chip, cc]).astype(jnp.int32)

    big = [w_in[0].T, w_out[0], w_gate[0], w_up[0], w_down[0]]
    names = ["w_in", "w_out", "w_gate", "w_up", "w_down"]
    gbufs = [_cast_into_gather(w, chip_arr, "cast_" + nm, split_cols=(nm == "w_in")) for w, nm in zip(big, names)]
    cs_in, cs_conv = D_IN // N_CHIPS, D_XBC // N_CHIPS
    cw = jnp.stack([_pad_rows(ssm_conv_w[0], 8), _pad_cols(_pad_rows(sc_conv_w[0], 8), cs_conv)])
    cw_buf = lax.dynamic_update_slice(jnp.zeros((N_DEV, 8, cs_conv), F32), cw, (2 * chip, 0, 0))
    g_in, cw_all = _allgather_inplace([gbufs[0], cw_buf], [("rows", (cs_in // 32) * 16), ("cols", cs_conv // 2)])
    cw_all = cw_all.reshape(N_CHIPS, 2, 8, cs_conv)
    ssm_w8 = _to_kernel_xbc(cw_all[:, 0].transpose(1, 0, 2).reshape(8, D_XBC))
    sc_w8 = cw_all[:, 1, :, :D_MODEL // N_CHIPS].transpose(1, 0, 2).reshape(8, D_MODEL)
    ssm_bk = _to_kernel_xbc(ssm_conv_b)
    wt = g_in.reshape(N_CHIPS, 2, cs_in, D_MODEL // 2).transpose(0, 2, 1, 3).reshape(D_IN, D_MODEL)
    wt_main = _to_kernel_rows(wt)
    wt_dt = _pad_rows(wt[D_SSM + D_XBC:D_SSM + D_XBC + N_HEADS], DT_PAD)
    ag_ss, ag_rs, ag_bufs, ag_tok = _split_start("ag_ici_start", gbufs[1:], _build_ag_ici, 12, after=[g_in, cw_all])

    bias_rows = _heads_to_rows(ssm_dt_bias[0])
    alog_rows = _heads_to_rows(ssm_A_log[0])
    drep = jnp.repeat(ssm_D[0], HEADDIM).reshape(1, D_SSM)

    n1 = _rmsnorm_fwd(xt, _tie(norm_mix_g, ag_tok, "tie_ag_ici"), "rmsnorm_mix")
    (proj,) = _matmul([(n1, wt_main)], tb=True, out_dtypes=[F32], name="mm_proj")
    (dt_raw,) = _matmul([(n1, wt_dt)], tb=True, out_dtypes=[F32], name="mm_proj_dt")
    xbc = _ssm_conv_fwd(proj, ssm_w8, ssm_bk)
    dtr = jnp.pad(dt_raw[:, :N_HEADS].T.reshape(N_GROUPS, HEADS_PER_GROUP, T), ((0, 0), (0, 4), (0, 0)))
    y_ssd, hs = _ssd_fwd(xbc, dtr, bias_rows, alog_rows, drep)
    ag_bufs = _split_wait("ag_ici_wait", ag_ss, ag_rs, ag_bufs, _build_ag_ici, after=[y_ssd])
    fw_ss, fw_rs, fw_bufs, fw_tok = _split_start("ag_fwd_start", ag_bufs, _build_ag_fwd, 12)
    y_mix = _shortconv_fwd(proj, sc_w8, _gated_norm_fwd(y_ssd, proj, _tie(ssm_norm_g, fw_tok, "tie_ag_fwd")))
    gath = _split_wait("ag_fwd_wait", fw_ss, fw_rs, fw_bufs, _build_ag_fwd, after=[y_mix])
    w_out_f = gath[0].reshape(2 * D_MODEL, D_MODEL)
    w_gate3 = gath[1].reshape(N_CHIPS, D_MODEL, D_FF // N_CHIPS)
    w_up3 = gath[2].reshape(N_CHIPS, D_MODEL, D_FF // N_CHIPS)
    w_down_f = gath[3].reshape(D_FF, D_MODEL)
    (h1,) = _matmul([(y_mix, w_out_f)], out_dtypes=[F32], name="mm_out", extras=[xt],
                    epilogue=lambda acc, res: (acc + res,))
    n2 = _rmsnorm_fwd(h1, norm_ffn_g, "rmsnorm_ffn")
    g_act, u_act, a_act = _ffn_fwd(n2, w_gate3, w_up3)
    (h2,) = _matmul([(a_act, w_down_f)], out_dtypes=[F32], name="mm_down", extras=[h1],
                    epilogue=lambda acc, res: (acc + res,))

    dh2, dh2b, dg_final, loss_part = _loss_and_final_bwd(h2, tgt, norm_final_g.reshape(1, D_MODEL))
    dg_act, du_act = _matmul([(dh2b, w_down_f)], tb=True, out_dtypes=[BF16, BF16], name="mm_down_bwd",
                             tn=512, extras=[g_act, u_act], epilogue=_swiglu_bwd, nsub=2)
    (dw_down,) = _matmul([(a_act, dh2b)], ta=True, out_dtypes=[F32], name="mm_dw_down", tm=1408)
    (dn2,) = _matmul([(dg_act, w_gate3), (du_act, w_up3)], tb=True, b3d=True, out_dtypes=[BF16],
                     name="mm_ffn_in_bwd")
    (dw_gate,) = _matmul([(n2, dg_act)], ta=True, out_dtypes=[F32], name="mm_dw_gate", tn=1408, col_shards=True)
    (dw_up,) = _matmul([(n2, du_act)], ta=True, out_dtypes=[F32], name="mm_dw_up", tn=1408, col_shards=True)
    dh1, dh1b, dg_ffn = _rmsnorm_bwd(dn2, h1, norm_ffn_g, dh2, "rmsnorm_ffn_bwd")
    (dw_out,) = _matmul([(y_mix, dh1b)], ta=True, out_dtypes=[F32], name="mm_dw_out")

    def halves(g):
        return g.reshape(N_CHIPS, 2, g.shape[1] // 2, g.shape[2])

    def landing(shape, dtype):
        return lax.empty(shape, dtype)

    names1 = names[1:]
    ps1 = [halves(dw_out.reshape(N_CHIPS, -1, D_MODEL)), halves(dw_gate), halves(dw_up),
           halves(dw_down.reshape(N_CHIPS, -1, D_MODEL))]
    r0_1 = [landing((N_CHIPS, 1) + p.shape[2:], F32) for p in ps1]
    sw_ss, sw_rs, sw_arr, sw_tok = _split_start("rs1_swap_start", ps1 + r0_1, _build_rs_swap, 4)
    (dmix,) = _matmul([(dh1b, w_out_f)], tb=True, out_dtypes=[BF16], name="mm_out_bwd", deps=[sw_tok])
    dproj, dw_sc = _shortconv_bwd(dmix, proj, sc_w8)
    dy_ssd, dproj, dg_ssmnorm = _gated_norm_bwd(dmix, y_ssd, proj, ssm_norm_g, dproj)
    sw_arr = _split_wait("rs1_swap_wait", sw_ss, sw_rs, sw_arr, _build_rs_swap, after=[dy_ssd])
    qs1 = [_rs_add_pair(p, r, c_arr, "rs_add_pair_" + nm) for p, r, nm in zip(sw_arr[:4], sw_arr[4:], names1)]
    r1_1 = [landing(q.shape, BF16) for q in qs1]
    ic_ss, ic_rs, ic_arr, ic_tok = _split_start("rs1_ici_start", qs1 + r1_1, _build_rs_ici, 12)
    dxbc_act, ddtr, dbias_acc, dalog_acc, dD_acc = _ssd_bwd(
        xbc, dtr, bias_rows, alog_rows, _tie(drep, ic_tok, "tie_rs1_ici"), dy_ssd, hs)
    dproj, dw_ssmconv, db_ssmconv = _ssm_conv_bwd(dxbc_act, proj, ssm_w8, ssm_bk, dproj)
    dw_ssmconv, db_ssmconv = _from_kernel_xbc(dw_ssmconv), _from_kernel_xbc(db_ssmconv)
    ic_arr = _split_wait("rs1_ici_wait", ic_ss, ic_rs, ic_arr, _build_rs_ici, after=[dproj])
    g1 = [_rs_add_chips(r, q, place_arr, "rs_add_chips_" + nm) for q, r, nm in zip(ic_arr[:4], ic_arr[4:], names1)]
    sh_ss, sh_rs, sh_arr, sh_tok = _split_start("rs1_share_start", g1, _build_rs_share, 4)

    ddt_raw = _pad_cols(ddtr[:, :HEADS_PER_GROUP, :].reshape(N_HEADS, T).T, DT_PAD).astype(BF16)
    (dwt_main,) = _matmul([(dproj, n1)], ta=True, out_dtypes=[F32], name="mm_dw_main", deps=[sh_tok])
    (dwt_dt,) = _matmul([(ddt_raw, n1)], ta=True, out_dtypes=[F32], name="mm_dw_dt")
    p_in = _scatter_rows_to_shards(dwt_main, dwt_dt)
    s2_ss, s2_rs, s2_arr, s2_tok = _split_start(
        "rs2_swap_start", [p_in, landing((N_CHIPS, 1) + p_in.shape[2:], F32)], _build_rs_swap, 1)
    mt = T // _tile(T, 1024)
    mt_a = max(mt // 2, 1)
    (dn1a,) = _matmul([(dproj, wt_main)], out_dtypes=[F32], name="mm_proj_bwd_a", deps=[s2_tok],
                      m_tiles=(0, mt_a))
    g1 = _split_wait("rs1_share_wait", sh_ss, sh_rs, sh_arr, _build_rs_share, after=[dn1a])
    s2_arr = _split_wait("rs2_swap_wait", s2_ss, s2_rs, s2_arr, _build_rs_swap, after=[dn1a])
    q_in = _rs_add_pair(s2_arr[0], s2_arr[1], c_arr, "rs_add_pair_w_in")
    i2_ss, i2_rs, i2_arr, i2_tok = _split_start(
        "rs2_ici_start", [q_in, landing(q_in.shape, BF16)], _build_rs_ici, 3)
    if mt > mt_a:
        (dn1a,) = _matmul([(dproj, wt_main)], out_dtypes=[F32], name="mm_proj_bwd_b", deps=[i2_tok],
                          m_tiles=(mt_a, mt - mt_a), out_buf=dn1a)
    (dn1,) = _matmul([(ddt_raw, wt_dt)], out_dtypes=[BF16], name="mm_proj_dt_bwd", extras=[dn1a],
                     epilogue=lambda acc, res: (acc + res,), deps=[i2_tok])
    dx, _, dg_mix = _rmsnorm_bwd(dn1, xt, norm_mix_g, dh1, "rmsnorm_mix_bwd")

    big_m = [m_w_in[0].T, m_w_out[0], m_w_gate[0], m_w_up[0], m_w_down[0]]
    big_v = [v_w_in[0].T, v_w_out[0], v_w_gate[0], v_w_up[0], v_w_down[0]]
    big_grads = [None] + [g.reshape(w.shape) for g, w in zip(g1, big[1:])]
    big_out = {}
    for k in range(1, 5):
        big_out[names[k]] = _adamw(big[k], big_grads[k], big_m[k], big_v[k], "adamw_" + names[k], deps=[i2_tok])
    i2_arr = _split_wait("rs2_ici_wait", i2_ss, i2_rs, i2_arr, _build_rs_ici, after=[big_out[names[4]][0], dx])
    g_in_red = _rs_add_chips(i2_arr[1], i2_arr[0], place_arr, "rs_add_chips_w_in")
    s3_ss, s3_rs, s3_arr, s3_tok = _split_start("rs2_share_start", [g_in_red], _build_rs_share, 1)

    dD = jnp.sum(dD_acc.reshape(N_HEADS, HEADDIM), axis=-1)
    heads_row = jnp.concatenate([_rows_to_heads(dbias_acc), _rows_to_heads(dalog_acc), dD,
                                 loss_part.reshape(1)]).reshape(1, -1)
    small = jnp.concatenate([
        dw_ssmconv,
        _pad_cols(dw_sc, D_XBC),
        db_ssmconv,
        jnp.concatenate([dg_mix, dg_ssmnorm], axis=1),
        jnp.concatenate([dg_ffn, dg_final], axis=1),
        _pad_cols(heads_row, D_XBC),
        jnp.zeros((4, D_XBC), F32),
    ], axis=0)
    tot = _allreduce_small(small, deps=[s3_tok])
    loss = tot[19, 3 * N_HEADS]

    cs_ssm, cs_sc = D_XBC // N_CHIPS, D_MODEL // N_CHIPS
    g_ssm_conv = lax.dynamic_slice(tot[0:K_SSM], (0, chip * cs_ssm), (K_SSM, cs_ssm))
    g_sc_conv = lax.dynamic_slice(tot[8:8 + K_SC, :D_MODEL], (0, chip * cs_sc), (K_SC, cs_sc))
    small_grads = {
        "norm_mix_g": tot[17:18, :D_MODEL], "ssm_conv_w": g_ssm_conv, "ssm_conv_b": tot[16:17],
        "ssm_dt_bias": tot[19:20, 0:N_HEADS], "ssm_A_log": tot[19:20, N_HEADS:2 * N_HEADS],
        "ssm_D": tot[19:20, 2 * N_HEADS:3 * N_HEADS], "ssm_norm_g": tot[17:18, D_MODEL:],
        "sc_conv_w": g_sc_conv, "norm_ffn_g": tot[18:19, :D_MODEL], "norm_final_g": tot[18:19, D_MODEL:],
    }
    small_w = {"norm_mix_g": (norm_mix_g, m_norm_mix_g, v_norm_mix_g),
               "ssm_conv_w": (ssm_conv_w[0], m_ssm_conv_w[0], v_ssm_conv_w[0]),
               "ssm_conv_b": (ssm_conv_b, m_ssm_conv_b, v_ssm_conv_b),
               "ssm_dt_bias": (ssm_dt_bias, m_ssm_dt_bias, v_ssm_dt_bias),
               "ssm_A_log": (ssm_A_log, m_ssm_A_log, v_ssm_A_log),
               "ssm_D": (ssm_D, m_ssm_D, v_ssm_D),
               "ssm_norm_g": (ssm_norm_g, m_ssm_norm_g, v_ssm_norm_g),
               "sc_conv_w": (sc_conv_w[0], m_sc_conv_w[0], v_sc_conv_w[0]),
               "norm_ffn_g": (norm_ffn_g, m_norm_ffn_g, v_norm_ffn_g),
               "norm_final_g": (norm_final_g.reshape(1, -1), m_norm_final_g.reshape(1, -1),
                                v_norm_final_g.reshape(1, -1))}
    PW = 1024
    order = list(small_w)

    def pack(arrs):
        rows = []
        for a in arrs:
            flat = a.reshape(-1)
            n = -(-flat.shape[0] // PW) * PW
            rows.append(jnp.pad(flat, (0, n - flat.shape[0])).reshape(-1, PW))
        slab = jnp.concatenate(rows, axis=0)
        return _pad_rows(slab, -(-slab.shape[0] // 8) * 8)

    wp = pack([small_w[k][0] for k in order])
    mp = pack([small_w[k][1] for k in order])
    vp = pack([small_w[k][2] for k in order])
    gp = pack([small_grads[k] for k in order])
    sd, sm, sv = _adamw(wp, gp, mp, vp, "adamw_small")

    def unpack(slab):
        out, row = {}, 0
        for k in order:
            shape = small_w[k][0].shape
            size = 1
            for s in shape:
                size *= s
            nr = -(-size // PW)
            out[k] = slab[row:row + nr].reshape(-1)[:size].reshape(shape)
            row += nr
        return out

    s_delta, s_m, s_v = unpack(sd), unpack(sm), unpack(sv)

    (g_in_full,) = _split_wait("rs2_share_wait", s3_ss, s3_rs, s3_arr, _build_rs_share, after=[sd])
    d_t, m_t, v_t, g_t = _adamw(big[0], g_in_full.reshape(2 * HR_IN, D_MODEL), big_m[0], big_v[0],
                                "adamw_" + names[0], emit_g=True)
    big_grads[0] = g_t.T
    big_out[names[0]] = (d_t.T, m_t.T, v_t.T)
    big_g = dict(zip(names, big_grads))

    weight_order = ["norm_mix_g", "w_in", "ssm_conv_w", "ssm_conv_b", "ssm_dt_bias", "ssm_A_log", "ssm_D",
                    "ssm_norm_g", "sc_conv_w", "w_out", "norm_ffn_g", "w_gate", "w_up", "w_down", "norm_final_g"]
    lead = {"ssm_conv_w", "sc_conv_w", "w_in", "w_out", "w_gate", "w_up", "w_down"}

    def shaped(nm, a):
        if nm == "norm_final_g":
            return a.reshape(D_MODEL)
        return a[None] if nm in lead else a

    grads, deltas, new_m, new_v = [], [], [], []
    for nm in weight_order:
        if nm in big_out:
            g, (d, m, v) = big_g[nm], big_out[nm]
        else:
            g, d, m, v = small_grads[nm], s_delta[nm], s_m[nm], s_v[nm]
        grads.append(shaped(nm, g))
        deltas.append(shaped(nm, d))
        new_m.append(shaped(nm, m))
        new_v.append(shaped(nm, v))
    return (loss, dx[None], *grads, *deltas, *new_m, *new_v)


def _swiglu_bwd(da, g, u):
    gf, uf = g.astype(F32), u.astype(F32)
    return da * uf * _dsilu(gf), da * _silu(gf)


def _ffn_fwd(n2, w_gate, w_up):
    T, K = n2.shape
    tn = w_gate.shape[2]
    N = N_CHIPS * tn
    tm = _tile(T, 512)
    sub = _tile(tm, 256)

    def body(a_ref, wg_ref, wu_ref, g_ref, u_ref, act_ref):
        for s in range(tm // sub):
            rows = pl.ds(s * sub, sub)
            a = a_ref[rows, :]
            g = jnp.dot(a, wg_ref[...], preferred_element_type=F32)
            u = jnp.dot(a, wu_ref[...], preferred_element_type=F32)
            g_ref[rows, :] = g.astype(BF16)
            u_ref[rows, :] = u.astype(BF16)
            act_ref[rows, :] = (_silu(g) * u).astype(BF16)

    a_spec = pl.BlockSpec((tm, K), lambda j, i: (i, 0))
    b_spec = pl.BlockSpec((None, K, tn), lambda j, i: (j, 0, 0))
    o_spec = pl.BlockSpec((tm, tn), lambda j, i: (i, j))
    return pl.pallas_call(
        body, name="ffn_fwd", grid=(N // tn, T // tm),
        in_specs=[a_spec, b_spec, b_spec], out_specs=[o_spec] * 3,
        out_shape=[jax.ShapeDtypeStruct((T, N), BF16)] * 3,
        compiler_params=_cparams(("parallel", "parallel")),
    )(n2, w_gate, w_up)
```

```python
import functools

import jax
import jax.numpy as jnp
from jax import lax
from jax.experimental import pallas as pl
from jax.experimental.pallas import tpu as pltpu

F32 = jnp.float32
BF16 = jnp.bfloat16
MESH = pl.DeviceIdType.MESH

D_MODEL = 2048
D_SSM = 2048
HEADDIM = 64
N_HEADS = 32
N_GROUPS = 8
HEADS_PER_GROUP = 4
N_STATE = 128
CHUNK = 128
K_SSM = 4
K_SC = 3
D_XBC = 4096
D_FF = 5632
D_IN = 12320
D_MAIN = 12288
OFF_XBC, OFF_CB, OFF_CC, OFF_CX = 2048, 6144, 8192, 10240
DT_PAD = 128
EPS = 1e-5
N_CHIPS = 4
N_DEV = 8

ADAM_LR = 0.001
ADAM_B1 = 0.9
ADAM_B2 = 0.999
ADAM_EPS = 1e-08
ADAM_WD = 0.01
ADAM_STEP = 10

V7X_VMEM_BYTES = 64 * 1024 * 1024
VMEM_LIMIT = V7X_VMEM_BYTES - 8 * 1024 * 1024


def _cparams(sem=None):
    if sem is None:
        return pltpu.CompilerParams(vmem_limit_bytes=VMEM_LIMIT)
    return pltpu.CompilerParams(dimension_semantics=sem, vmem_limit_bytes=VMEM_LIMIT)


def _tile(dim, pref, unit=128):
    best = None
    t = unit
    while t <= min(dim, pref):
        if dim % t == 0:
            best = t
        t += unit
    return best if best is not None else dim


def _sigmoid(x):
    return 1.0 / (1.0 + jnp.exp(-x))


def _silu(x):
    return x * _sigmoid(x)


def _dsilu(x):
    s = _sigmoid(x)
    return s * (1.0 + x * (1.0 - s))


def _softplus(x):
    return jnp.maximum(x, 0.0) + jnp.log(1.0 + jnp.exp(-jnp.abs(x)))


MATMUL_VMEM_BUDGET = 44 * 1024 * 1024


def _matmul(pairs, *, ta=False, tb=False, out_dtypes, name, tm=1024, tn=1024, tk=None, extras=(), epilogue=None,
            deps=(), col_shards=False, nsub=1, b3d=False, m_tiles=None, out_buf=None):
    a0, b0 = pairs[0]
    M, K = (a0.shape[1], a0.shape[0]) if ta else a0.shape
    if b3d:
        N = b0.shape[1] if tb else b0.shape[0] * b0.shape[2]
        tk, tn = (b0.shape[2], tn) if tb else (tk, b0.shape[2])
    else:
        N = b0.shape[0] if tb else b0.shape[1]
    tm, tn = _tile(M, tm, 8 if M % 128 else 128), _tile(N, tn)
    npair, nex, ndep, nout = len(pairs), len(extras), len(deps), len(out_dtypes)
    if tk is None:
        fixed = 2 * tm * tn * (sum(jnp.dtype(d).itemsize for d in out_dtypes) + sum(e.dtype.itemsize for e in extras))
        tk = K
        while tk > 128 and (K % tk or tk % 128 or
                            fixed + 2 * npair * 2 * tk * (tm + tn) + (tm * tn * 4 if tk < K else 0) > MATMUL_VMEM_BUDGET):
            tk -= 128
    else:
        tk = _tile(K, tk)
    nk = K // tk
    if nk > 1 or tm % nsub or (tm // nsub) % 128:
        nsub = 1
    sub = tm // nsub
    dims = (((0 if ta else 1,), (1 if tb else 0,)), ((), ()))
    i0, mi = m_tiles if m_tiles is not None else (0, M // tm)
    nbuf = 0 if out_buf is None else 1

    def body(*refs):
        a_refs = refs[0:2 * npair:2]
        b_refs = refs[1:2 * npair:2]
        ex_refs = refs[2 * npair:2 * npair + nex]
        o_refs = refs[2 * npair + nex + ndep + nbuf:2 * npair + nex + ndep + nbuf + nout]

        def dots(rows):
            s = None
            for a_ref, b_ref in zip(a_refs, b_refs):
                a = a_ref[...] if rows is None else (a_ref[:, rows] if ta else a_ref[rows, :])
                d = lax.dot_general(a, b_ref[...], dims, preferred_element_type=F32)
                s = d if s is None else s + d
            return s

        def finish(r, rows):
            ex = [e[...] if rows is None else e[rows, :] for e in ex_refs]
            outs = (r,) if epilogue is None else epilogue(r, *ex)
            for o_ref, o in zip(o_refs, outs):
                if rows is None:
                    o_ref[...] = o.astype(o_ref.dtype)
                else:
                    o_ref[rows, :] = o.astype(o_ref.dtype)

        if nk == 1:
            for s in range(nsub):
                rows = None if nsub == 1 else pl.ds(s * sub, sub)
                finish(dots(rows), rows)
            return

        acc = refs[-1]
        k = pl.program_id(2)

        @pl.when(k == 0)
        def _():
            acc[...] = dots(None)

        @pl.when(jnp.logical_and(k > 0, k < nk - 1))
        def _():
            acc[...] += dots(None)

        @pl.when(k == nk - 1)
        def _():
            finish(acc[...] + dots(None), None)

    a_spec = (pl.BlockSpec((tk, tm), lambda i, j, k: (k, i + i0)) if ta
              else pl.BlockSpec((tm, tk), lambda i, j, k: (i + i0, k)))
    if b3d:
        b_spec = (pl.BlockSpec((None, tn, tk), lambda i, j, k: (k, j, 0)) if tb
                  else pl.BlockSpec((None, tk, tn), lambda i, j, k: (j, k, 0)))
    else:
        b_spec = (pl.BlockSpec((tn, tk), lambda i, j, k: (j, k)) if tb
                  else pl.BlockSpec((tk, tn), lambda i, j, k: (k, j)))
    e_spec = pl.BlockSpec((tm, tn), lambda i, j, k: (i + i0, j))
    if col_shards:
        o_spec = pl.BlockSpec((None, tm, tn), lambda i, j, k: (j, i + i0, 0))
        o_shape = (N // tn, M, tn)
    else:
        o_spec, o_shape = e_spec, (M, N)
    args, in_specs = [], []
    for a, b in pairs:
        args += [a, b]
        in_specs += [a_spec, b_spec]
    args += list(extras) + list(deps) + ([] if out_buf is None else [out_buf])
    in_specs += [e_spec] * nex + [ANY] * (ndep + nbuf)
    outs = pl.pallas_call(
        body,
        name=name,
        grid=(mi, N // tn, nk),
        in_specs=in_specs,
        out_specs=[o_spec] * nout,
        out_shape=[jax.ShapeDtypeStruct(o_shape, dt) for dt in out_dtypes],
        input_output_aliases={} if out_buf is None else {len(args) - 1: 0},
        scratch_shapes=[pltpu.VMEM((tm, tn), F32)] if nk > 1 else [],
        compiler_params=_cparams(("parallel", "parallel", "arbitrary")),
    )(*args)
    return outs


def _cast_into_gather(w, chip_arr, name, split_cols=False):
    R, C = w.shape
    hr, hc = (R, C // 2) if split_cols else (R // 2, C)
    tr = _tile(hr, 512, 8)
    nb = hr // tr

    def body(chip_ref, w_ref, o_ref):
        o_ref[...] = w_ref[...].astype(BF16)

    in_map = (lambda h, i, chip_ref: (i, h)) if split_cols else (lambda h, i, chip_ref: (h * nb + i, 0))
    grid_spec = pltpu.PrefetchScalarGridSpec(
        num_scalar_prefetch=1, grid=(2, nb),
        in_specs=[pl.BlockSpec((tr, hc), in_map)],
        out_specs=pl.BlockSpec((None, tr, hc), lambda h, i, chip_ref: (2 * chip_ref[0] + h, i, 0)))
    return pl.pallas_call(
        body, name=name, grid_spec=grid_spec,
        out_shape=jax.ShapeDtypeStruct((N_DEV, hr, hc), BF16),
        compiler_params=_cparams(("parallel", "parallel")),
    )(chip_arr, w)


def _tie(small, token, name):
    def body(s_ref, t_ref, o_ref):
        o_ref[...] = s_ref[...]

    vm = pl.BlockSpec(memory_space=pltpu.VMEM)
    return pl.pallas_call(body, name=name, in_specs=[vm, ANY], out_specs=vm,
                          out_shape=jax.ShapeDtypeStruct(small.shape, small.dtype))(small, token)


def _rmsnorm_fwd(x, g, name):
    T, D = x.shape
    tt = _tile(T, 256)

    def body(x_ref, g_ref, n_ref):
        xv = x_ref[...]
        r = lax.rsqrt(jnp.mean(xv * xv, axis=-1, keepdims=True) + EPS)
        n_ref[...] = (xv * r * g_ref[...]).astype(BF16)

    return pl.pallas_call(
        body, name=name, grid=(T // tt,),
        in_specs=[pl.BlockSpec((tt, D), lambda i: (i, 0)), pl.BlockSpec((1, D), lambda i: (0, 0))],
        out_specs=pl.BlockSpec((tt, D), lambda i: (i, 0)),
        out_shape=jax.ShapeDtypeStruct((T, D), BF16),
        compiler_params=_cparams(("parallel",)),
    )(x, g)


def _rmsnorm_bwd(dn, x, g, res, name):
    T, D = x.shape
    tt = _tile(T, 256)

    def body(dn_ref, x_ref, g_ref, res_ref, dx_ref, dxb_ref, dg_ref):
        @pl.when(pl.program_id(0) == 0)
        def _():
            dg_ref[...] = jnp.zeros_like(dg_ref)

        xv = x_ref[...]
        dy = dn_ref[...].astype(F32)
        r = lax.rsqrt(jnp.mean(xv * xv, axis=-1, keepdims=True) + EPS)
        xhat = xv * r
        dxh = dy * g_ref[...]
        dx = res_ref[...] + r * (dxh - xhat * jnp.mean(dxh * xhat, axis=-1, keepdims=True))
        dx_ref[...] = dx
        dxb_ref[...] = dx.astype(BF16)
        dg_ref[...] += jnp.sum(dy * xhat, axis=0, keepdims=True)

    tok = pl.BlockSpec((tt, D), lambda i: (i, 0))
    vec = pl.BlockSpec((1, D), lambda i: (0, 0))
    return pl.pallas_call(
        body, name=name, grid=(T // tt,),
        in_specs=[tok, tok, vec, tok],
        out_specs=[tok, tok, vec],
        out_shape=[jax.ShapeDtypeStruct((T, D), F32), jax.ShapeDtypeStruct((T, D), BF16),
                   jax.ShapeDtypeStruct((1, D), F32)],
        compiler_params=_cparams(("arbitrary",)),
    )(dn, x, g, res)


def _loss_and_final_bwd(h2, target, gf):
    T, D = h2.shape
    tt = _tile(T, 256)

    def body(h_ref, t_ref, g_ref, dh_ref, dhb_ref, dg_ref, loss_ref):
        @pl.when(pl.program_id(0) == 0)
        def _():
            dg_ref[...] = jnp.zeros_like(dg_ref)
            loss_ref[...] = jnp.zeros_like(loss_ref)

        xv = h_ref[...]
        r = lax.rsqrt(jnp.mean(xv * xv, axis=-1, keepdims=True) + EPS)
        xhat = xv * r
        err = xhat * g_ref[...] - t_ref[...]
        loss_ref[...] += 0.5 * jnp.sum(jnp.mean(err * err, axis=-1, keepdims=True), axis=0, keepdims=True)
        dy = err * (1.0 / D)
        dxh = dy * g_ref[...]
        dx = r * (dxh - xhat * jnp.mean(dxh * xhat, axis=-1, keepdims=True))
        dh_ref[...] = dx
        dhb_ref[...] = dx.astype(BF16)
        dg_ref[...] += jnp.sum(dy * xhat, axis=0, keepdims=True)

    tok = pl.BlockSpec((tt, D), lambda i: (i, 0))
    vec = pl.BlockSpec((1, D), lambda i: (0, 0))
    return pl.pallas_call(
        body, name="loss_final_bwd", grid=(T // tt,),
        in_specs=[tok, tok, vec],
        out_specs=[tok, tok, vec, pl.BlockSpec((1, 1), lambda i: (0, 0))],
        out_shape=[jax.ShapeDtypeStruct((T, D), F32), jax.ShapeDtypeStruct((T, D), BF16),
                   jax.ShapeDtypeStruct((1, D), F32), jax.ShapeDtypeStruct((1, 1), F32)],
        compiler_params=_cparams(("arbitrary",)),
    )(h2, target, gf)


def _gated_norm_fwd(y, proj, g):
    T, D = y.shape
    tt = _tile(T, 256)

    def body(y_ref, z_ref, g_ref, o_ref):
        yg = y_ref[...] * _silu(z_ref[...])
        r = lax.rsqrt(jnp.mean(yg * yg, axis=-1, keepdims=True) + EPS)
        o_ref[...] = (yg * r * g_ref[...]).astype(BF16)

    tok = pl.BlockSpec((tt, D), lambda i: (i, 0))
    return pl.pallas_call(
        body, name="gated_norm_fwd", grid=(T // tt,),
        in_specs=[tok, tok, pl.BlockSpec((1, D), lambda i: (0, 0))],
        out_specs=tok,
        out_shape=jax.ShapeDtypeStruct((T, 2 * D_MODEL), BF16),
        compiler_params=_cparams(("parallel",)),
    )(y, proj, g)


def _gated_norm_bwd(dmix, y, proj, g, dproj):
    T, D = y.shape
    tt = _tile(T, 256)

    def body(do_ref, y_ref, z_ref, g_ref, dp_ref, dy_ref, dz_ref, dg_ref):
        @pl.when(pl.program_id(0) == 0)
        def _():
            dg_ref[...] = jnp.zeros_like(dg_ref)

        yv, zv = y_ref[...], z_ref[...]
        do = do_ref[...].astype(F32)
        sz = _silu(zv)
        yg = yv * sz
        r = lax.rsqrt(jnp.mean(yg * yg, axis=-1, keepdims=True) + EPS)
        xhat = yg * r
        dxh = do * g_ref[...]
        dyg = r * (dxh - xhat * jnp.mean(dxh * xhat, axis=-1, keepdims=True))
        dy_ref[...] = dyg * sz
        dz_ref[...] = (dyg * yv * _dsilu(zv)).astype(BF16)
        dg_ref[...] += jnp.sum(do * xhat, axis=0, keepdims=True)

    tok = pl.BlockSpec((tt, D), lambda i: (i, 0))
    vec = pl.BlockSpec((1, D), lambda i: (0, 0))
    return pl.pallas_call(
        body, name="gated_norm_bwd", grid=(T // tt,),
        in_specs=[tok, tok, tok, vec, ANY],
        out_specs=[tok, tok, vec],
        out_shape=[jax.ShapeDtypeStruct((T, D), F32), jax.ShapeDtypeStruct(dproj.shape, BF16),
                   jax.ShapeDtypeStruct((1, D), F32)],
        input_output_aliases={4: 1},
        compiler_params=_cparams(("arbitrary",)),
    )(dmix, y, proj, g, dproj)


HALO = 8


def _shift_down(cur, prev8, s):
    ext = jnp.concatenate([prev8, cur], axis=0)
    return pltpu.roll(ext, s, axis=0)[HALO:]


def _shift_up(cur, next8, s):
    n = cur.shape[0]
    ext = jnp.concatenate([cur, next8], axis=0)
    return pltpu.roll(ext, n + HALO - s, axis=0)[:n]


def _conv_specs(tt, cb, col_off_blocks, nt):
    hb = tt // HALO
    cur = pl.BlockSpec((tt, cb), lambda j, i: (i, col_off_blocks + j))
    prev = pl.BlockSpec((HALO, cb), lambda j, i: (jnp.maximum(i * hb - 1, 0), col_off_blocks + j))
    nxt = pl.BlockSpec((HALO, cb), lambda j, i: (jnp.minimum((i + 1) * hb, nt * hb - 1), col_off_blocks + j))
    return cur, prev, nxt


def _causal_conv(cur, prev8, w, K):
    y = cur * w[K - 1:K, :]
    for k in range(K - 1):
        y = y + _shift_down(cur, prev8, K - 1 - k) * w[k:k + 1, :]
    return y


def _anticausal_conv(cur, next8, w, K):
    y = cur * w[K - 1:K, :]
    for k in range(K - 1):
        y = y + _shift_up(cur, next8, K - 1 - k) * w[k:k + 1, :]
    return y


def _ssm_conv_fwd(proj, w8, b):
    T = proj.shape[0]
    tt, cb = _tile(T, 512), 512
    nt = T // tt
    cur, prev, _ = _conv_specs(tt, cb, OFF_XBC // cb, nt)

    def body(u_ref, up_ref, w_ref, b_ref, o_ref):
        first = pl.program_id(1) == 0
        p8 = jnp.where(first, 0.0, up_ref[...])
        pre = _causal_conv(u_ref[...], p8, w_ref[...], K_SSM) + b_ref[...]
        o_ref[...] = _silu(pre)

    return pl.pallas_call(
        body, name="ssm_conv_fwd", grid=(D_XBC // cb, nt),
        in_specs=[cur, prev, pl.BlockSpec((8, cb), lambda j, i: (0, j)), pl.BlockSpec((1, cb), lambda j, i: (0, j))],
        out_specs=pl.BlockSpec((tt, cb), lambda j, i: (i, j)),
        out_shape=jax.ShapeDtypeStruct((T, D_XBC), F32),
        compiler_params=_cparams(("parallel", "parallel")),
    )(proj, proj, w8, b)


def _ssm_conv_bwd(dact, proj, w8, b, dproj):
    T = proj.shape[0]
    tt, cb = _tile(T, 512), 512
    nt = T // tt
    cur, prev, nxt = _conv_specs(tt, cb, OFF_XBC // cb, nt)
    dcur, dprev, dnxt = _conv_specs(tt, cb, 0, nt)

    def dpre_of(d, u, p8, w, bb):
        pre = _causal_conv(u, p8, w, K_SSM) + bb
        return d * _dsilu(pre)

    def body(d_ref, dn_ref, u_ref, up_ref, un_ref, w_ref, b_ref, dp_ref, dx_ref, dw_ref, db_ref):
        i = pl.program_id(1)

        @pl.when(i == 0)
        def _():
            dw_ref[...] = jnp.zeros_like(dw_ref)
            db_ref[...] = jnp.zeros_like(db_ref)

        w, bb = w_ref[...], b_ref[...]
        u = u_ref[...]
        p8 = jnp.where(i == 0, 0.0, up_ref[...])
        dpre = dpre_of(d_ref[...], u, p8, w, bb)
        un = un_ref[...]
        dpre_n = dpre_of(dn_ref[...], un, u[tt - HALO:, :], w, bb)
        dpre_n = jnp.where(i == nt - 1, 0.0, dpre_n)
        dx_ref[...] = _anticausal_conv(dpre, dpre_n, w, K_SSM).astype(BF16)
        rows = [jnp.sum(dpre * _shift_down(u, p8, K_SSM - 1 - k), axis=0, keepdims=True) for k in range(K_SSM - 1)]
        rows.append(jnp.sum(dpre * u, axis=0, keepdims=True))
        rows.append(jnp.zeros((8 - K_SSM, cb), F32))
        dw_ref[...] += jnp.concatenate(rows, axis=0)
        db_ref[...] += jnp.sum(dpre, axis=0, keepdims=True)

    wspec = pl.BlockSpec((8, cb), lambda j, i: (0, j))
    bspec = pl.BlockSpec((1, cb), lambda j, i: (0, j))
    return pl.pallas_call(
        body, name="ssm_conv_bwd", grid=(D_XBC // cb, nt),
        in_specs=[dcur, dnxt, cur, prev, nxt, wspec, bspec, ANY],
        out_specs=[pl.BlockSpec((tt, cb), lambda j, i: (i, OFF_XBC // cb + j)), wspec, bspec],
        out_shape=[jax.ShapeDtypeStruct(dproj.shape, BF16), jax.ShapeDtypeStruct((8, D_XBC), F32),
                   jax.ShapeDtypeStruct((1, D_XBC), F32)],
        input_output_aliases={7: 0},
        compiler_params=_cparams(("parallel", "arbitrary")),
    )(dact, dact, proj, proj, proj, w8, b, dproj)


SCB = 512
SC3 = 3 * SCB


def _sc_specs(tt, nt):
    hb = tt // HALO
    cur = pl.BlockSpec((tt, SC3), lambda j, i: (i, OFF_CB // SC3 + j))
    prev = pl.BlockSpec((HALO, SC3), lambda j, i: (jnp.maximum(i * hb - 1, 0), OFF_CB // SC3 + j))
    nxt = pl.BlockSpec((HALO, SC3), lambda j, i: (jnp.minimum((i + 1) * hb, nt * hb - 1), OFF_CB // SC3 + j))
    return cur, prev, nxt


def _shortconv_fwd(proj, w8, ymix):
    T = proj.shape[0]
    tt = _tile(T, 512)
    nt = T // tt
    cur, prev, _ = _sc_specs(tt, nt)

    def body(p_ref, pp_ref, w_ref, y_ref, o_ref):
        p, pp = p_ref[...], pp_ref[...]
        v = p[:, SCB:2 * SCB] * p[:, 2 * SCB:]
        vp = jnp.where(pl.program_id(1) == 0, 0.0, pp[:, SCB:2 * SCB] * pp[:, 2 * SCB:])
        o_ref[...] = (p[:, :SCB] * _causal_conv(v, vp, w_ref[...], K_SC)).astype(BF16)

    return pl.pallas_call(
        body, name="shortconv_fwd", grid=(D_MODEL // SCB, nt),
        in_specs=[cur, prev, pl.BlockSpec((8, SCB), lambda j, i: (0, j)), ANY],
        out_specs=pl.BlockSpec((tt, SCB), lambda j, i: (i, D_SSM // SCB + j)),
        out_shape=jax.ShapeDtypeStruct(ymix.shape, BF16),
        input_output_aliases={3: 0},
        compiler_params=_cparams(("parallel", "parallel")),
    )(proj, proj, w8, ymix)


def _shortconv_bwd(dmix, proj, w8):
    T = proj.shape[0]
    tt = _tile(T, 512)
    nt = T // tt
    hb = tt // HALO
    cur, prev, nxt = _sc_specs(tt, nt)
    d_s = pl.BlockSpec((tt, SCB), lambda j, i: (i, D_SSM // SCB + j))
    dn_s = pl.BlockSpec((HALO, SCB), lambda j, i: (jnp.minimum((i + 1) * hb, nt * hb - 1), D_SSM // SCB + j))

    def body(d_ref, dn_ref, p_ref, pp_ref, pn_ref, w_ref, dp_ref, dw_ref):
        i = pl.program_id(1)

        @pl.when(i == 0)
        def _():
            dw_ref[...] = jnp.zeros_like(dw_ref)

        w = w_ref[...]
        p, pp = p_ref[...], pp_ref[...]
        gb, gc, u = p[:, :SCB], p[:, SCB:2 * SCB], p[:, 2 * SCB:]
        v = gc * u
        vp = jnp.where(i == 0, 0.0, pp[:, SCB:2 * SCB] * pp[:, 2 * SCB:])
        d = d_ref[...].astype(F32)
        dp_ref[:, :SCB] = (d * _causal_conv(v, vp, w, K_SC)).astype(BF16)
        dcv = d * gb
        dcv_n = jnp.where(i == nt - 1, 0.0, dn_ref[...].astype(F32) * pn_ref[:, :SCB])
        dv = _anticausal_conv(dcv, dcv_n, w, K_SC)
        dp_ref[:, SCB:2 * SCB] = (dv * u).astype(BF16)
        dp_ref[:, 2 * SCB:] = (dv * gc).astype(BF16)
        rows = [jnp.sum(dcv * _shift_down(v, vp, K_SC - 1 - k), axis=0, keepdims=True) for k in range(K_SC - 1)]
        rows.append(jnp.sum(dcv * v, axis=0, keepdims=True))
        rows.append(jnp.zeros((8 - K_SC, SCB), F32))
        dw_ref[...] += jnp.concatenate(rows, axis=0)

    wspec = pl.BlockSpec((8, SCB), lambda j, i: (0, j))
    return pl.pallas_call(
        body, name="shortconv_bwd", grid=(D_MODEL // SCB, nt),
        in_specs=[d_s, dn_s, cur, prev, nxt, wspec],
        out_specs=[cur, wspec],
        out_shape=[jax.ShapeDtypeStruct((T, D_MAIN), BF16), jax.ShapeDtypeStruct((8, D_MODEL), F32)],
        compiler_params=_cparams(("parallel", "arbitrary")),
    )(dmix, dmix, proj, proj, proj, w8)


GW = HEADS_PER_GROUP * HEADDIM


def _dot(a, b):
    return jnp.dot(a.astype(BF16), b.astype(BF16), preferred_element_type=F32)


def _dot_nt(a, b):
    return lax.dot_general(a.astype(BF16), b.astype(BF16), (((1,), (1,)), ((), ())), preferred_element_type=F32)


def _dot_tn(a, b):
    return lax.dot_general(a.astype(BF16), b.astype(BF16), (((0,), (0,)), ((), ())), preferred_element_type=F32)


def _bf16_terms(x, n):
    terms, r = [], x
    for _ in range(n):
        t = r.astype(BF16)
        terms.append(t)
        r = r - t.astype(F32)
    return terms


def _dot_sel(a, sel, n=2):
    s = sel.astype(BF16)
    return sum(jnp.dot(t, s, preferred_element_type=F32) for t in _bf16_terms(a, n))


def _sel_dot(sel, b, n=2):
    s = sel.astype(BF16)
    return sum(jnp.dot(s, t, preferred_element_type=F32) for t in _bf16_terms(b, n))


def _sel_dot_nt(sel, b, n=2):
    s = sel.astype(BF16)
    return sum(lax.dot_general(s, t, (((1,), (1,)), ((), ())), preferred_element_type=F32)
               for t in _bf16_terms(b, n))


def _head_cols(rows):
    parts = [jnp.broadcast_to(rows[r:r + 1, :], (HEADDIM, CHUNK)) for r in range(HEADS_PER_GROUP)]
    return jnp.concatenate(parts, axis=0).T


def _head_rows(rows):
    parts = [jnp.broadcast_to(rows[r:r + 1, :], (HEADDIM, N_STATE)) for r in range(HEADS_PER_GROUP)]
    return jnp.concatenate(parts, axis=0)


def _ssd_common(dtr, bias, alog):
    dt = _softplus(dtr + bias)
    A = -jnp.exp(alog)
    a = dt * A
    ki = lax.broadcasted_iota(jnp.int32, (CHUNK, CHUNK), 0)
    si = lax.broadcasted_iota(jnp.int32, (CHUNK, CHUNK), 1)
    upper = (ki <= si).astype(F32)
    cs = _dot_sel(a, upper, 3)
    cs_last = jnp.broadcast_to(cs[:, CHUNK - 1:CHUNK], (8, CHUNK))
    return dt, A, a, cs, cs_last


def _decay_matrix(cs, r):
    li = lax.broadcasted_iota(jnp.int32, (CHUNK, CHUNK), 0)
    si = lax.broadcasted_iota(jnp.int32, (CHUNK, CHUNK), 1)
    causal = li >= si
    R = jnp.broadcast_to(cs[r:r + 1, :], (CHUNK, CHUNK))
    seg = jnp.where(causal, R.T - R, 0.0)
    return jnp.where(causal, jnp.exp(seg), 0.0)


GXBC = GW + 2 * N_STATE


GS = 4


def _ssd_in_specs(nc, rev):
    cix = (lambda c: nc - 1 - c) if rev else (lambda c: c)
    x_s = pl.BlockSpec((CHUNK, GS * GW), lambda g, c: (cix(c), g))
    xbc_s = pl.BlockSpec((CHUNK, GS * GXBC), lambda g, c: (cix(c), g))
    dtr_s = pl.BlockSpec((GS, 8, CHUNK), lambda g, c: (g, 0, cix(c)))
    row_s = pl.BlockSpec((GS, 8, CHUNK), lambda g, c: (g, 0, 0))
    drep_s = pl.BlockSpec((1, GS * GW), lambda g, c: (0, g))
    hs_s = pl.BlockSpec((1, GS * GW, N_STATE), lambda g, c: (cix(c), g, 0))
    return x_s, xbc_s, dtr_s, row_s, drep_s, hs_s


def _xbc_parts(xbc_ref, gi):
    o = gi * GXBC
    return xbc_ref[:, o:o + GW], xbc_ref[:, o + GW:o + GW + N_STATE], xbc_ref[:, o + GW + N_STATE:o + GXBC]


def _ssd_fwd(xbc, dtr, bias, alog, drep):
    T = xbc.shape[0]
    nc = T // CHUNK
    x_s, xbc_s, dtr_s, row_s, drep_s, hs_s = _ssd_in_specs(nc, False)

    def body(xbc_ref, dtr_ref, bias_ref, alog_ref, drep_ref, y_ref, hs_ref, h_scr):
        @pl.when(pl.program_id(1) == 0)
        def _():
            h_scr[...] = jnp.zeros_like(h_scr)

        for gi in range(GS):
            cols, rows = slice(gi * GW, (gi + 1) * GW), pl.ds(gi * GW, GW)
            x, Bm, Cm = _xbc_parts(xbc_ref, gi)
            dt, A, a, cs, cs_last = _ssd_common(dtr_ref[gi], bias_ref[gi], alog_ref[gi])
            E = _head_cols(jnp.exp(cs))
            W = _head_cols(jnp.exp(cs_last - cs) * dt)
            X = (x * _head_cols(dt)).astype(BF16)
            CB = _dot_nt(Cm, Bm)
            col = lax.broadcasted_iota(jnp.int32, (CHUNK, GW), 1) // HEADDIM
            y = jnp.zeros((CHUNK, GW), F32)
            for r in range(HEADS_PER_GROUP):
                M = CB * _decay_matrix(cs, r)
                y = y + jnp.where(col == r, _dot(M, X), 0.0)
            h = h_scr[rows, :]
            hs_ref[0, rows, :] = h
            y = y + _dot_nt(Cm, h) * E
            y_ref[:, cols] = y + drep_ref[:, cols] * x
            h_scr[rows, :] = h * _head_rows(jnp.exp(cs_last)) + _dot_tn(x * W, Bm)

    return pl.pallas_call(
        body, name="ssd_fwd", grid=(N_GROUPS // GS, nc),
        in_specs=[xbc_s, dtr_s, row_s, row_s, drep_s],
        out_specs=[x_s, hs_s],
        out_shape=[jax.ShapeDtypeStruct((T, D_SSM), F32), jax.ShapeDtypeStruct((nc, D_SSM, N_STATE), F32)],
        scratch_shapes=[pltpu.VMEM((GS * GW, N_STATE), F32)],
        compiler_params=_cparams(("parallel", "arbitrary")),
    )(xbc, dtr, bias, alog, drep)


def _ssd_bwd(xbc, dtr, bias, alog, drep, dy, hs):
    T = xbc.shape[0]
    nc = T // CHUNK
    x_s, xbc_s, dtr_s, row_s, drep_s, hs_s = _ssd_in_specs(nc, True)

    def body(xbc_ref, dtr_ref, bias_ref, alog_ref, drep_ref, dy_ref, hs_ref,
             dxbc_ref, ddtr_ref, dbias_ref, dalog_ref, dd_ref, dh_scr):
        @pl.when(pl.program_id(1) == 0)
        def _():
            dh_scr[...] = jnp.zeros_like(dh_scr)
            dbias_ref[...] = jnp.zeros_like(dbias_ref)
            dalog_ref[...] = jnp.zeros_like(dalog_ref)
            dd_ref[...] = jnp.zeros_like(dd_ref)

        for gi in range(GS):
            one_group(gi, xbc_ref, dtr_ref, bias_ref, alog_ref, drep_ref, dy_ref, hs_ref,
                      dxbc_ref, ddtr_ref, dbias_ref, dalog_ref, dd_ref, dh_scr)

    def one_group(gi, xbc_ref, dtr_ref, bias_ref, alog_ref, drep_ref, dy_ref, hs_ref,
                  dxbc_ref, ddtr_ref, dbias_ref, dalog_ref, dd_ref, dh_scr):
        cols, rows, o = slice(gi * GW, (gi + 1) * GW), pl.ds(gi * GW, GW), gi * GXBC
        x, Bm, Cm = _xbc_parts(xbc_ref, gi)
        dY = dy_ref[:, cols]
        dt, A, a, cs, cs_last = _ssd_common(dtr_ref[gi], bias_ref[gi], alog_ref[gi])
        E = _head_cols(jnp.exp(cs))
        DT = _head_cols(dt)
        Wd = _head_cols(jnp.exp(cs_last - cs))
        X = x * DT
        h = hs_ref[0, rows, :]
        dS = dh_scr[rows, :]
        CB = _dot_nt(Cm, Bm)
        col = lax.broadcasted_iota(jnp.int32, (CHUNK, GW), 1) // HEADDIM
        rowid = lax.broadcasted_iota(jnp.int32, (8, CHUNK), 0)
        lane = lax.broadcasted_iota(jnp.int32, (8, CHUNK), 1)
        hsel = (lax.broadcasted_iota(jnp.int32, (8, GW), 1) // HEADDIM
                == lax.broadcasted_iota(jnp.int32, (8, GW), 0)).astype(F32)
        ones8 = jnp.ones((8, CHUNK), F32)

        dX = jnp.zeros((CHUNK, GW), F32)
        dCB = jnp.zeros((CHUNK, CHUNK), F32)
        dcs = jnp.zeros((8, CHUNK), F32)
        for r in range(HEADS_PER_GROUP):
            L = _decay_matrix(cs, r)
            M = CB * L
            G = _dot_nt(jnp.where(col == r, dY, 0.0), X)
            GL = G * L
            dCB = dCB + GL
            Wm = GL * CB
            colsum = jnp.sum(Wm, axis=0, keepdims=True)
            rowsum = _sel_dot_nt(ones8, Wm)
            dcs = dcs + jnp.where(rowid == r, rowsum - colsum, 0.0)
            dX = dX + jnp.where(col == r, _dot_tn(M, dY), 0.0)
        dC = _dot(dCB, Bm)
        dB = _dot_tn(dCB, Cm)
        T1 = _dot_nt(Bm, dS)
        dX = dX + T1 * Wd
        dB = dB + _dot(X * Wd, dS)
        pdec = _sel_dot_nt(hsel, X * T1 * Wd)
        dcs = dcs - pdec
        dlast = jnp.sum(pdec, axis=1, keepdims=True) \
            + jnp.exp(cs_last[:, 0:1]) * jnp.sum(_sel_dot(hsel, dS * h), axis=1, keepdims=True)
        dYE = dY * E
        dC = dC + _dot(dYE, h)
        yoff = _dot_nt(Cm, h) * E
        dcs = dcs + _sel_dot_nt(hsel, dY * yoff)
        dcs = dcs + jnp.where(lane == CHUNK - 1, dlast, 0.0)
        ki = lax.broadcasted_iota(jnp.int32, (CHUNK, CHUNK), 0)
        si = lax.broadcasted_iota(jnp.int32, (CHUNK, CHUNK), 1)
        lower = (ki >= si).astype(F32)
        da = _dot_sel(dcs, lower)
        ddt = da * A + _sel_dot_nt(hsel, dX * x)
        ddtr = ddt * _sigmoid(dtr_ref[gi] + bias_ref[gi])
        ddtr_ref[gi] = ddtr
        dbias_ref[gi] += ddtr
        dalog_ref[gi] += da * a
        dxbc_ref[:, o:o + GW] = dX * DT + drep_ref[:, cols] * dY
        dd_ref[:, cols] += jnp.sum(dY * x, axis=0, keepdims=True)
        dxbc_ref[:, o + GW:o + GW + N_STATE] = dB
        dxbc_ref[:, o + GW + N_STATE:o + GXBC] = dC
        dh_scr[rows, :] = dS * _head_rows(jnp.exp(cs_last)) + _dot_tn(dYE, Cm)

    return pl.pallas_call(
        body, name="ssd_bwd", grid=(N_GROUPS // GS, nc),
        in_specs=[xbc_s, dtr_s, row_s, row_s, drep_s, x_s, hs_s],
        out_specs=[xbc_s, dtr_s, row_s, row_s, drep_s],
        out_shape=[jax.ShapeDtypeStruct((T, D_XBC), F32),
                   jax.ShapeDtypeStruct((N_GROUPS, 8, T), F32),
                   jax.ShapeDtypeStruct((N_GROUPS, 8, CHUNK), F32),
                   jax.ShapeDtypeStruct((N_GROUPS, 8, CHUNK), F32),
                   jax.ShapeDtypeStruct((1, D_SSM), F32)],
        scratch_shapes=[pltpu.VMEM((GS * GW, N_STATE), F32)],
        compiler_params=_cparams(("parallel", "arbitrary")),
    )(xbc, dtr, bias, alog, drep, dy, hs)


def _adamw(w, g, m, v, name, deps=(), emit_g=False):
    R, C = w.shape
    tr = _tile(R, 256, 8)
    nd = len(deps)
    nout = 4 if emit_g else 3

    def body(w_ref, g_ref, m_ref, v_ref, *rest):
        outs = rest[nd:]
        gv = g_ref[...]
        mn = ADAM_B1 * m_ref[...] + (1.0 - ADAM_B1) * gv
        vn = ADAM_B2 * v_ref[...] + (1.0 - ADAM_B2) * (gv * gv)
        m_hat = mn / (1.0 - ADAM_B1 ** ADAM_STEP)
        v_hat = vn / (1.0 - ADAM_B2 ** ADAM_STEP)
        outs[0][...] = -ADAM_LR * (m_hat / (jnp.sqrt(v_hat) + ADAM_EPS) + ADAM_WD * w_ref[...])
        outs[1][...] = mn
        outs[2][...] = vn
        if emit_g:
            outs[3][...] = gv

    spec = pl.BlockSpec((tr, C), lambda i: (i, 0))
    return pl.pallas_call(
        body, name=name, grid=(R // tr,),
        in_specs=[spec] * 4 + [ANY] * nd, out_specs=[spec] * nout,
        out_shape=[jax.ShapeDtypeStruct((R, C), F32)] * nout,
        compiler_params=_cparams(("parallel",)),
    )(w, g, m, v, *deps)


ANY = pl.BlockSpec(memory_space=pl.ANY)


def _place():
    x, y, c = lax.axis_index("x"), lax.axis_index("y"), lax.axis_index("c")
    return x, y, c


def _other_chips(x, y):
    return [(1 - x, y), (x, 1 - y), (1 - x, 1 - y)]


def _allgather_inplace(bufs, splits):
    n = len(bufs)

    def body(*refs):
        o_refs = refs[n:2 * n]
        send_sems, recv_sems = refs[2 * n:]
        x, y, c = _place()
        xn, yn, dg, sibling = (1 - x, y), (x, 1 - y), (1 - x, 1 - y), (x, y, 1 - c)

        def blk(k, chip, pc):
            return o_refs[k].at[4 * chip[0] + 2 * chip[1] + pc]

        def part(k, ref, p):
            kind, s = splits[k]
            _, R, C = bufs[k].shape
            if kind == "rows":
                return ref.at[pl.ds(0, s)] if p == 0 else ref.at[pl.ds(s, R - s)]
            return ref.at[:, pl.ds(0, s)] if p == 0 else ref.at[:, pl.ds(s, C - s)]

        def copy(k, slot, ref, to):
            return pltpu.make_async_remote_copy(
                src_ref=ref, dst_ref=ref, send_sem=send_sems.at[k, slot], recv_sem=recv_sems.at[k, slot],
                device_id=to, device_id_type=MESH)

        sent = []

        def send(k, slot, ref, to):
            cp = copy(k, slot, ref, to)
            cp.start()
            sent.append(cp)

        for k in range(n):
            send(k, 0, blk(k, (x, y), c), (*xn, c))
            send(k, 1, blk(k, (x, y), c), (*yn, c))
        for k in range(n):
            bx, by = blk(k, xn, c), blk(k, yn, c)
            copy(k, 0, bx, sibling).wait_recv()
            send(k, 2, part(k, bx, 0), (*yn, c))
            send(k, 4, bx, sibling)
            copy(k, 1, by, sibling).wait_recv()
            send(k, 3, part(k, by, 1), (*xn, c))
            send(k, 5, by, sibling)
        for k in range(n):
            d0, d1 = part(k, blk(k, dg, c), 0), part(k, blk(k, dg, c), 1)
            copy(k, 2, d0, sibling).wait_recv()
            send(k, 6, d0, sibling)
            copy(k, 3, d1, sibling).wait_recv()
            send(k, 7, d1, sibling)
        for k in range(n):
            copy(k, 4, blk(k, xn, 1 - c), sibling).wait_recv()
            copy(k, 5, blk(k, yn, 1 - c), sibling).wait_recv()
            copy(k, 6, part(k, blk(k, dg, 1 - c), 0), sibling).wait_recv()
            copy(k, 7, part(k, blk(k, dg, 1 - c), 1), sibling).wait_recv()
        for cp in sent:
            cp.wait_send()

    return pl.pallas_call(
        body, name="allgather_w_in",
        in_specs=[ANY] * n, out_specs=[ANY] * n,
        out_shape=[jax.ShapeDtypeStruct(b.shape, b.dtype) for b in bufs],
        input_output_aliases={k: k for k in range(n)},
        scratch_shapes=[pltpu.SemaphoreType.DMA((n, 8)), pltpu.SemaphoreType.DMA((n, 8))],
    )(*bufs)


HBM = pl.BlockSpec(memory_space=pltpu.HBM)
SEM = pl.BlockSpec(memory_space=pltpu.SEMAPHORE)
EFFECT = pltpu.SideEffectType.DATAFLOW_SIDE_EFFECTING


def _split_start(name, arrays, build, n_copies, after=()):
    na, nd = len(arrays), len(after)

    def body(*refs):
        send_sems, recv_sems = refs[na + nd], refs[na + nd + 1]
        for cp in build(refs[:na], send_sems, recv_sems):
            cp.start()
        refs[-1][...] = jnp.zeros((8, 128), F32)

    outs = pl.pallas_call(
        body, name=name,
        out_shape=(pltpu.SemaphoreType.DMA((n_copies,)), pltpu.SemaphoreType.DMA((n_copies,)),
                   *[pltpu.HBM(a.shape, a.dtype) for a in arrays], jax.ShapeDtypeStruct((8, 128), F32)),
        in_specs=[HBM] * na + [ANY] * nd,
        out_specs=(SEM, SEM, *[HBM] * na, pl.BlockSpec(memory_space=pltpu.VMEM)),
        input_output_aliases={i: 2 + i for i in range(na)},
        compiler_params=pltpu.CompilerParams(has_side_effects=EFFECT),
    )(*[pltpu.with_memory_space_constraint(a, pltpu.HBM) for a in arrays], *after)
    return outs[0], outs[1], list(outs[2:2 + na]), outs[-1]


def _split_wait(name, send_sems, recv_sems, arrays, build, after):
    na = len(arrays)

    def body(*refs):
        for cp in build(refs[:na], refs[na], refs[na + 1]):
            cp.wait_send()
            cp.wait_recv()

    outs = pl.pallas_call(
        body, name=name,
        out_shape=tuple(pltpu.HBM(a.shape, a.dtype) for a in arrays),
        in_specs=[HBM] * na + [SEM, SEM] + [ANY] * len(after),
        out_specs=tuple([HBM] * na),
        input_output_aliases={i: i for i in range(na)},
        compiler_params=pltpu.CompilerParams(has_side_effects=EFFECT),
    )(*arrays, send_sems, recv_sems, *after)
    return list(outs)


def _remote(src, dst, send_sems, recv_sems, i, to):
    return pltpu.make_async_remote_copy(src_ref=src, dst_ref=dst, send_sem=send_sems.at[i], recv_sem=recv_sems.at[i],
                                        device_id=to, device_id_type=MESH)


def _build_ag_ici(refs, ss, rs):
    x, y, c = _place()
    cps = []
    for k, ref in enumerate(refs):
        blk = ref.at[4 * x + 2 * y + c]
        for j, (px, py) in enumerate(_other_chips(x, y)):
            cps.append(_remote(blk, blk, ss, rs, 3 * k + j, (px, py, c)))
    return cps


def _build_ag_fwd(refs, ss, rs):
    x, y, c = _place()
    cps = []
    for k, ref in enumerate(refs):
        for j, (px, py) in enumerate(_other_chips(x, y)):
            blk = ref.at[4 * px + 2 * py + c]
            cps.append(_remote(blk, blk, ss, rs, 3 * k + j, (x, y, 1 - c)))
    return cps


def _build_rs_swap(refs, ss, rs):
    x, y, c = _place()
    n = len(refs) // 2
    return [_remote(refs[k].at[:, pl.ds(1 - c, 1)], refs[n + k], ss, rs, k, (x, y, 1 - c)) for k in range(n)]


def _build_rs_ici(refs, ss, rs):
    x, y, c = _place()
    n = len(refs) // 2
    me = 2 * x + y
    cps = []
    for k in range(n):
        for j, (px, py) in enumerate(_other_chips(x, y)):
            cps.append(_remote(refs[k].at[2 * px + py], refs[n + k].at[me], ss, rs, 3 * k + j, (px, py, c)))
    return cps


def _build_rs_share(refs, ss, rs):
    x, y, c = _place()
    return [_remote(ref.at[c], ref.at[c], ss, rs, k, (x, y, 1 - c)) for k, ref in enumerate(refs)]


def _allreduce_small(p, deps=()):
    R, C = p.shape
    nd = len(deps)

    def body(p_ref, *rest):
        gath_ref, sum_ref, send_sems, recv_sems, local_sem = rest[nd:]
        x, y, c = _place()
        me, sibling = (x, y, c), (x, y, 1 - c)
        chips = [(1 - x, y), (x, 1 - y), (1 - x, 1 - y)]

        def blk(px, py, pc):
            return gath_ref.at[4 * px + 2 * py + pc]

        def copy(k, block, to, src=None):
            return pltpu.make_async_remote_copy(
                src_ref=blk(*block) if src is None else src, dst_ref=blk(*block),
                send_sem=send_sems.at[k], recv_sem=recv_sems.at[k], device_id=to, device_id_type=MESH)

        mine = pltpu.make_async_copy(p_ref, blk(*me), local_sem)
        mine.start()
        first = [copy(0, me, sibling, src=p_ref)]
        first += [copy(1 + j, me, (*chip, c), src=p_ref) for j, chip in enumerate(chips)]
        for cp in first:
            cp.start()
        passed = [copy(4 + j, (*chip, c), sibling) for j, chip in enumerate(chips)]
        for j, chip in enumerate(chips):
            copy(1 + j, (*chip, c), me).wait_recv()
            passed[j].start()
        copy(0, sibling, me).wait_recv()
        for j, chip in enumerate(chips):
            copy(4 + j, (*chip, 1 - c), me).wait_recv()
        for cp in first + passed:
            cp.wait_send()
        mine.wait()
        s = gath_ref[0]
        for d in range(1, N_DEV):
            s = s + gath_ref[d]
        sum_ref[...] = s

    vm = pl.BlockSpec(memory_space=pltpu.VMEM)
    return pl.pallas_call(
        body, name="allreduce_small",
        in_specs=[vm] + [ANY] * nd, out_specs=[vm, vm],
        out_shape=[jax.ShapeDtypeStruct((N_DEV, R, C), F32), jax.ShapeDtypeStruct((R, C), F32)],
        scratch_shapes=[pltpu.SemaphoreType.DMA((7,)), pltpu.SemaphoreType.DMA((7,)), pltpu.SemaphoreType.DMA],
    )(p, *deps)[1]


def _rs_add_pair(p, r0, c_arr, name):
    _, _, hr, cols = p.shape
    tr = _tile(hr, 256, 8)

    def body(c_ref, p_ref, r_ref, q_ref):
        q_ref[...] = (p_ref[0] + r_ref[0]).astype(BF16)

    grid_spec = pltpu.PrefetchScalarGridSpec(
        num_scalar_prefetch=1, grid=(N_CHIPS, hr // tr),
        in_specs=[pl.BlockSpec((1, 1, tr, cols), lambda j, i, c_ref: (j, c_ref[0], i, 0)),
                  pl.BlockSpec((1, 1, tr, cols), lambda j, i, c_ref: (j, 0, i, 0))],
        out_specs=pl.BlockSpec((1, tr, cols), lambda j, i, c_ref: (j, i, 0)))
    return pl.pallas_call(
        body, name=name, grid_spec=grid_spec,
        out_shape=jax.ShapeDtypeStruct((N_CHIPS, hr, cols), BF16),
        compiler_params=_cparams(("parallel", "parallel")),
    )(c_arr, p, r0)


def _rs_add_chips(r1, q, place_arr, name):
    _, hr, cols = r1.shape
    tr = _tile(hr, 256, 8)

    def body(place_ref, r_ref, q_ref, o_ref):
        chip = place_ref[0]
        s = None
        for j in range(N_CHIPS):
            t = jnp.where(chip == j, q_ref[j], r_ref[j]).astype(F32)
            s = t if s is None else s + t
        o_ref[...] = s

    blk = pl.BlockSpec((N_CHIPS, tr, cols), lambda i, place_ref: (0, i, 0))
    grid_spec = pltpu.PrefetchScalarGridSpec(
        num_scalar_prefetch=1, grid=(hr // tr,), in_specs=[blk, blk],
        out_specs=pl.BlockSpec((None, tr, cols), lambda i, place_ref: (place_ref[1], i, 0)))
    return pl.pallas_call(
        body, name=name, grid_spec=grid_spec,
        out_shape=jax.ShapeDtypeStruct((2, hr, cols), F32),
        compiler_params=_cparams(("parallel",)),
    )(place_arr, r1, q)


def _pad_rows(a, rows):
    return jnp.pad(a, ((0, rows - a.shape[0]), (0, 0)))


def _pad_cols(a, cols):
    return jnp.pad(a, ((0, 0), (0, cols - a.shape[1])))


def _heads_to_rows(v):
    v = v.reshape(N_GROUPS, HEADS_PER_GROUP, 1)
    v = jnp.pad(v, ((0, 0), (0, 8 - HEADS_PER_GROUP), (0, 0)))
    return jnp.broadcast_to(v, (N_GROUPS, 8, CHUNK))


def _rows_to_heads(a):
    return jnp.sum(a[:, :HEADS_PER_GROUP, :], axis=-1).reshape(N_HEADS)


def _to_kernel_rows(a):
    C = a.shape[1]
    x0, b0, c0, s0 = D_SSM, 2 * D_SSM, 2 * D_SSM + 1024, D_SSM + D_XBC + N_HEADS
    xbc = jnp.concatenate([a[x0:b0].reshape(N_GROUPS, GW, C), a[b0:c0].reshape(N_GROUPS, N_STATE, C),
                           a[c0:c0 + 1024].reshape(N_GROUPS, N_STATE, C)], axis=1).reshape(D_XBC, C)
    sc = jnp.concatenate([a[s0 + k * D_MODEL:s0 + (k + 1) * D_MODEL].reshape(D_MODEL // SCB, SCB, C)
                          for k in range(3)], axis=1).reshape(3 * D_MODEL, C)
    return jnp.concatenate([a[:D_SSM], xbc, sc], axis=0)


HR_IN = 1568


def _shard_row_plan():
    segs = [(0, 0, 0, D_SSM)]
    for g in range(N_GROUPS):
        k0 = D_SSM + g * GXBC
        segs += [(0, k0, D_SSM + g * GW, GW), (0, k0 + GW, 2 * D_SSM + g * N_STATE, N_STATE),
                 (0, k0 + GW + N_STATE, 2 * D_SSM + 1024 + g * N_STATE, N_STATE)]
    segs.append((1, 0, D_SSM + D_XBC, N_HEADS))
    for j in range(D_MODEL // SCB):
        for k in range(3):
            segs.append((0, D_SSM + D_XBC + j * SC3 + k * SCB, D_SSM + D_XBC + N_HEADS + k * D_MODEL + j * SCB, SCB))
    cs = D_IN // N_CHIPS
    plan = []
    for src, s, o, n in segs:
        while n > 0:
            chip, loc = divmod(o, cs)
            half, row = divmod(loc, HR_IN)
            m = min(n, cs - loc, HR_IN - row)
            plan.append((src, s, chip, half, row, m))
            s, o, n = s + m, o + m, n - m
    return plan


SCATTER_ROWS = 512
SCATTER_SLOTS = 4


def _scatter_rows_to_shards(k_main, k_dt):
    C = k_main.shape[1]
    pieces = []
    for src, s, chip, half, row, n in _shard_row_plan():
        for o in range(0, n, SCATTER_ROWS):
            pieces.append((src, s + o, chip, half, row + o, min(SCATTER_ROWS, n - o)))
    S, lag, N = SCATTER_SLOTS, SCATTER_SLOTS // 2, len(pieces)

    def body(m_ref, d_ref, o_ref, buf, in_sems, out_sems):
        def cin(i):
            src, s, _, _, _, n = pieces[i]
            return pltpu.make_async_copy((d_ref if src else m_ref).at[pl.ds(s, n)],
                                         buf.at[i % S, pl.ds(0, n)], in_sems.at[i % S])

        def cout(i):
            _, _, chip, half, row, n = pieces[i]
            return pltpu.make_async_copy(buf.at[i % S, pl.ds(0, n)],
                                         o_ref.at[chip, half, pl.ds(row, n)], out_sems.at[i % S])

        for i in range(N + lag):
            if i < N:
                if i >= S:
                    cout(i - S).wait()
                cin(i).start()
            j = i - lag
            if 0 <= j < N:
                cin(j).wait()
                cout(j).start()
        for j in range(max(0, N - S), N):
            cout(j).wait()

    return pl.pallas_call(
        body, name="scatter_dw_in_rows", in_specs=[ANY, ANY], out_specs=ANY,
        out_shape=jax.ShapeDtypeStruct((N_CHIPS, 2, HR_IN, C), k_main.dtype),
        scratch_shapes=[pltpu.VMEM((S, SCATTER_ROWS, C), k_main.dtype),
                        pltpu.SemaphoreType.DMA((S,)), pltpu.SemaphoreType.DMA((S,))],
        compiler_params=_cparams(),
    )(k_main, k_dt)


def _to_kernel_xbc(a):
    R = a.shape[0]
    return jnp.concatenate([a[:, :D_SSM].reshape(R, N_GROUPS, GW), a[:, D_SSM:D_SSM + 1024].reshape(R, N_GROUPS, N_STATE),
                            a[:, D_SSM + 1024:].reshape(R, N_GROUPS, N_STATE)], axis=2).reshape(R, D_XBC)


def _from_kernel_xbc(a):
    R = a.shape[0]
    g = a.reshape(R, N_GROUPS, GXBC)
    return jnp.concatenate([g[:, :, :GW].reshape(R, D_SSM), g[:, :, GW:GW + N_STATE].reshape(R, 1024),
                            g[:, :, GW + N_STATE:].reshape(R, 1024)], axis=1)


def kernel(x, norm_mix_g, w_in, ssm_conv_w, ssm_conv_b, ssm_dt_bias, ssm_A_log, ssm_D, ssm_norm_g, sc_conv_w, w_out, norm_ffn_g, w_gate, w_up, w_down, norm_final_g, loss_target, m_norm_mix_g, m_w_in, m_ssm_conv_w, m_ssm_conv_b, m_ssm_dt_bias, m_ssm_A_log, m_ssm_D, m_ssm_norm_g, m_sc_conv_w, m_w_out, m_norm_ffn_g, m_w_gate, m_w_up, m_w_down, m_norm_final_g, v_norm_mix_g, v_w_in, v_ssm_conv_w, v_ssm_conv_b, v_ssm_dt_bias, v_ssm_A_log, v_ssm_D, v_ssm_norm_g, v_sc_conv_w, v_w_out, v_norm_ffn_g, v_w_gate, v_w_up, v_w_down, v_norm_final_g):
    T = x.shape[1]
    xt = x[0]
    tgt = loss_target[0]
    cx, cy, cc = lax.axis_index("x"), lax.axis_index("y"), lax.axis_index("c")
    chip = 2 * cx + cy
    c_arr = jnp.reshape(cc, (1,)).astype(jnp.int32)
    chip_arr = jnp.reshape(chip, (1,)).astype(jnp.int32)
    place_arr = jnp.stack([chip, cc]).astype(jnp.int32)

    big = [w_in[0].T, w_out[0], w_gate[0], w_up[0], w_down[0]]
    names = ["w_in", "w_out", "w_gate", "w_up", "w_down"]
    gbufs = [_cast_into_gather(w, chip_arr, "cast_" + nm, split_cols=(nm == "w_in")) for w, nm in zip(big, names)]
    cs_in, cs_conv = D_IN // N_CHIPS, D_XBC // N_CHIPS
    cw = jnp.stack([_pad_rows(ssm_conv_w[0], 8), _pad_cols(_pad_rows(sc_conv_w[0], 8), cs_conv)])
    cw_buf = lax.dynamic_update_slice(jnp.zeros((N_DEV, 8, cs_conv), F32), cw, (2 * chip, 0, 0))
    g_in, cw_all = _allgather_inplace([gbufs[0], cw_buf], [("rows", (cs_in // 32) * 16), ("cols", cs_conv // 2)])
    cw_all = cw_all.reshape(N_CHIPS, 2, 8, cs_conv)
    ssm_w8 = _to_kernel_xbc(cw_all[:, 0].transpose(1, 0, 2).reshape(8, D_XBC))
    sc_w8 = cw_all[:, 1, :, :D_MODEL // N_CHIPS].transpose(1, 0, 2).reshape(8, D_MODEL)
    ssm_bk = _to_kernel_xbc(ssm_conv_b)
    wt = g_in.reshape(N_CHIPS, 2, cs_in, D_MODEL // 2).transpose(0, 2, 1, 3).reshape(D_IN, D_MODEL)
    wt_main = _to_kernel_rows(wt)
    wt_dt = _pad_rows(wt[D_SSM + D_XBC:D_SSM + D_XBC + N_HEADS], DT_PAD)
    ag_ss, ag_rs, ag_bufs, ag_tok = _split_start("ag_ici_start", gbufs[1:], _build_ag_ici, 12, after=[g_in, cw_all])

    bias_rows = _heads_to_rows(ssm_dt_bias[0])
    alog_rows = _heads_to_rows(ssm_A_log[0])
    drep = jnp.repeat(ssm_D[0], HEADDIM).reshape(1, D_SSM)

    n1 = _rmsnorm_fwd(xt, _tie(norm_mix_g, ag_tok, "tie_ag_ici"), "rmsnorm_mix")
    (proj,) = _matmul([(n1, wt_main)], tb=True, out_dtypes=[F32], name="mm_proj")
    (dt_raw,) = _matmul([(n1, wt_dt)], tb=True, out_dtypes=[F32], name="mm_proj_dt")
    xbc = _ssm_conv_fwd(proj, ssm_w8, ssm_bk)
    dtr = jnp.pad(dt_raw[:, :N_HEADS].T.reshape(N_GROUPS, HEADS_PER_GROUP, T), ((0, 0), (0, 4), (0, 0)))
    y_ssd, hs = _ssd_fwd(xbc, dtr, bias_rows, alog_rows, drep)
    ag_bufs = _split_wait("ag_ici_wait", ag_ss, ag_rs, ag_bufs, _build_ag_ici, after=[y_ssd])
    fw_ss, fw_rs, fw_bufs, fw_tok = _split_start("ag_fwd_start", ag_bufs, _build_ag_fwd, 12)
    y_mix = _shortconv_fwd(proj, sc_w8, _gated_norm_fwd(y_ssd, proj, _tie(ssm_norm_g, fw_tok, "tie_ag_fwd")))
    gath = _split_wait("ag_fwd_wait", fw_ss, fw_rs, fw_bufs, _build_ag_fwd, after=[y_mix])
    w_out_f = gath[0].reshape(2 * D_MODEL, D_MODEL)
    w_gate3 = gath[1].reshape(N_CHIPS, D_MODEL, D_FF // N_CHIPS)
    w_up3 = gath[2].reshape(N_CHIPS, D_MODEL, D_FF // N_CHIPS)
    w_down_f = gath[3].reshape(D_FF, D_MODEL)
    (h1,) = _matmul([(y_mix, w_out_f)], out_dtypes=[F32], name="mm_out", extras=[xt],
                    epilogue=lambda acc, res: (acc + res,))
    n2 = _rmsnorm_fwd(h1, norm_ffn_g, "rmsnorm_ffn")
    g_act, u_act, a_act = _ffn_fwd(n2, w_gate3, w_up3)
    (h2,) = _matmul([(a_act, w_down_f)], out_dtypes=[F32], name="mm_down", extras=[h1], tm=512,
                    epilogue=lambda acc, res: (acc + res,))

    dh2, dh2b, dg_final, loss_part = _loss_and_final_bwd(h2, tgt, norm_final_g.reshape(1, D_MODEL))
    dg_act, du_act = _matmul([(dh2b, w_down_f)], tb=True, out_dtypes=[BF16, BF16], name="mm_down_bwd",
                             tn=512, extras=[g_act, u_act], epilogue=_swiglu_bwd, nsub=2)
    (dw_down,) = _matmul([(a_act, dh2b)], ta=True, out_dtypes=[F32], name="mm_dw_down", tm=1408, tn=512)
    (dn2,) = _matmul([(dg_act, w_gate3), (du_act, w_up3)], tb=True, b3d=True, out_dtypes=[BF16],
                     name="mm_ffn_in_bwd")
    (dw_gate,) = _matmul([(n2, dg_act)], ta=True, out_dtypes=[F32], name="mm_dw_gate", tm=512, tn=1408,
                         col_shards=True)
    (dw_up,) = _matmul([(n2, du_act)], ta=True, out_dtypes=[F32], name="mm_dw_up", tm=512, tn=1408,
                       col_shards=True)
    dh1, dh1b, dg_ffn = _rmsnorm_bwd(dn2, h1, norm_ffn_g, dh2, "rmsnorm_ffn_bwd")
    (dw_out,) = _matmul([(y_mix, dh1b)], ta=True, out_dtypes=[F32], name="mm_dw_out")

    def halves(g):
        return g.reshape(N_CHIPS, 2, g.shape[1] // 2, g.shape[2])

    def landing(shape, dtype):
        return lax.empty(shape, dtype)

    names1 = names[1:]
    ps1 = [halves(dw_out.reshape(N_CHIPS, -1, D_MODEL)), halves(dw_gate), halves(dw_up),
           halves(dw_down.reshape(N_CHIPS, -1, D_MODEL))]
    r0_1 = [landing((N_CHIPS, 1) + p.shape[2:], F32) for p in ps1]
    sw_ss, sw_rs, sw_arr, sw_tok = _split_start("rs1_swap_start", ps1 + r0_1, _build_rs_swap, 4)
    (dmix,) = _matmul([(dh1b, w_out_f)], tb=True, out_dtypes=[BF16], name="mm_out_bwd", deps=[sw_tok])
    dproj, dw_sc = _shortconv_bwd(dmix, proj, sc_w8)
    dy_ssd, dproj, dg_ssmnorm = _gated_norm_bwd(dmix, y_ssd, proj, ssm_norm_g, dproj)
    sw_arr = _split_wait("rs1_swap_wait", sw_ss, sw_rs, sw_arr, _build_rs_swap, after=[dy_ssd])
    qs1 = [_rs_add_pair(p, r, c_arr, "rs_add_pair_" + nm) for p, r, nm in zip(sw_arr[:4], sw_arr[4:], names1)]
    r1_1 = [landing(q.shape, BF16) for q in qs1]
    ic_ss, ic_rs, ic_arr, ic_tok = _split_start("rs1_ici_start", qs1 + r1_1, _build_rs_ici, 12)
    dxbc_act, ddtr, dbias_acc, dalog_acc, dD_acc = _ssd_bwd(
        xbc, dtr, bias_rows, alog_rows, _tie(drep, ic_tok, "tie_rs1_ici"), dy_ssd, hs)
    dproj, dw_ssmconv, db_ssmconv = _ssm_conv_bwd(dxbc_act, proj, ssm_w8, ssm_bk, dproj)
    dw_ssmconv, db_ssmconv = _from_kernel_xbc(dw_ssmconv), _from_kernel_xbc(db_ssmconv)
    ic_arr = _split_wait("rs1_ici_wait", ic_ss, ic_rs, ic_arr, _build_rs_ici, after=[dproj])
    g1 = [_rs_add_chips(r, q, place_arr, "rs_add_chips_" + nm) for q, r, nm in zip(ic_arr[:4], ic_arr[4:], names1)]
    sh_ss, sh_rs, sh_arr, sh_tok = _split_start("rs1_share_start", g1, _build_rs_share, 4)

    ddt_raw = _pad_cols(ddtr[:, :HEADS_PER_GROUP, :].reshape(N_HEADS, T).T, DT_PAD).astype(BF16)
    (dwt_main,) = _matmul([(dproj, n1)], ta=True, out_dtypes=[F32], name="mm_dw_main", deps=[sh_tok])
    (dwt_dt,) = _matmul([(ddt_raw, n1)], ta=True, out_dtypes=[F32], name="mm_dw_dt")
    p_in = _scatter_rows_to_shards(dwt_main, dwt_dt)
    s2_ss, s2_rs, s2_arr, s2_tok = _split_start(
        "rs2_swap_start", [p_in, landing((N_CHIPS, 1) + p_in.shape[2:], F32)], _build_rs_swap, 1)
    mt = T // _tile(T, 1024)
    mt_a = max(mt // 2, 1)
    (dn1a,) = _matmul([(dproj, wt_main)], out_dtypes=[F32], name="mm_proj_bwd_a", deps=[s2_tok],
                      m_tiles=(0, mt_a))
    g1 = _split_wait("rs1_share_wait", sh_ss, sh_rs, sh_arr, _build_rs_share, after=[dn1a])
    s2_arr = _split_wait("rs2_swap_wait", s2_ss, s2_rs, s2_arr, _build_rs_swap, after=[dn1a])
    q_in = _rs_add_pair(s2_arr[0], s2_arr[1], c_arr, "rs_add_pair_w_in")
    i2_ss, i2_rs, i2_arr, i2_tok = _split_start(
        "rs2_ici_start", [q_in, landing(q_in.shape, BF16)], _build_rs_ici, 3)
    if mt > mt_a:
        (dn1a,) = _matmul([(dproj, wt_main)], out_dtypes=[F32], name="mm_proj_bwd_b", deps=[i2_tok],
                          m_tiles=(mt_a, mt - mt_a), out_buf=dn1a)
    (dn1,) = _matmul([(ddt_raw, wt_dt)], out_dtypes=[BF16], name="mm_proj_dt_bwd", extras=[dn1a],
                     epilogue=lambda acc, res: (acc + res,), deps=[i2_tok])
    dx, _, dg_mix = _rmsnorm_bwd(dn1, xt, norm_mix_g, dh1, "rmsnorm_mix_bwd")

    big_m = [m_w_in[0].T, m_w_out[0], m_w_gate[0], m_w_up[0], m_w_down[0]]
    big_v = [v_w_in[0].T, v_w_out[0], v_w_gate[0], v_w_up[0], v_w_down[0]]
    big_grads = [None] + [g.reshape(w.shape) for g, w in zip(g1, big[1:])]
    big_out = {}
    for k in range(1, 5):
        big_out[names[k]] = _adamw(big[k], big_grads[k], big_m[k], big_v[k], "adamw_" + names[k], deps=[i2_tok])
    i2_arr = _split_wait("rs2_ici_wait", i2_ss, i2_rs, i2_arr, _build_rs_ici, after=[big_out[names[4]][0], dx])
    g_in_red = _rs_add_chips(i2_arr[1], i2_arr[0], place_arr, "rs_add_chips_w_in")
    s3_ss, s3_rs, s3_arr, s3_tok = _split_start("rs2_share_start", [g_in_red], _build_rs_share, 1)

    dD = jnp.sum(dD_acc.reshape(N_HEADS, HEADDIM), axis=-1)
    heads_row = jnp.concatenate([_rows_to_heads(dbias_acc), _rows_to_heads(dalog_acc), dD,
                                 loss_part.reshape(1)]).reshape(1, -1)
    small = jnp.concatenate([
        dw_ssmconv,
        _pad_cols(dw_sc, D_XBC),
        db_ssmconv,
        jnp.concatenate([dg_mix, dg_ssmnorm], axis=1),
        jnp.concatenate([dg_ffn, dg_final], axis=1),
        _pad_cols(heads_row, D_XBC),
        jnp.zeros((4, D_XBC), F32),
    ], axis=0)
    tot = _allreduce_small(small, deps=[s3_tok])
    loss = tot[19, 3 * N_HEADS]

    cs_ssm, cs_sc = D_XBC // N_CHIPS, D_MODEL // N_CHIPS
    g_ssm_conv = lax.dynamic_slice(tot[0:K_SSM], (0, chip * cs_ssm), (K_SSM, cs_ssm))
    g_sc_conv = lax.dynamic_slice(tot[8:8 + K_SC, :D_MODEL], (0, chip * cs_sc), (K_SC, cs_sc))
    small_grads = {
        "norm_mix_g": tot[17:18, :D_MODEL], "ssm_conv_w": g_ssm_conv, "ssm_conv_b": tot[16:17],
        "ssm_dt_bias": tot[19:20, 0:N_HEADS], "ssm_A_log": tot[19:20, N_HEADS:2 * N_HEADS],
        "ssm_D": tot[19:20, 2 * N_HEADS:3 * N_HEADS], "ssm_norm_g": tot[17:18, D_MODEL:],
        "sc_conv_w": g_sc_conv, "norm_ffn_g": tot[18:19, :D_MODEL], "norm_final_g": tot[18:19, D_MODEL:],
    }
    small_w = {"norm_mix_g": (norm_mix_g, m_norm_mix_g, v_norm_mix_g),
               "ssm_conv_w": (ssm_conv_w[0], m_ssm_conv_w[0], v_ssm_conv_w[0]),
               "ssm_conv_b": (ssm_conv_b, m_ssm_conv_b, v_ssm_conv_b),
               "ssm_dt_bias": (ssm_dt_bias, m_ssm_dt_bias, v_ssm_dt_bias),
               "ssm_A_log": (ssm_A_log, m_ssm_A_log, v_ssm_A_log),
               "ssm_D": (ssm_D, m_ssm_D, v_ssm_D),
               "ssm_norm_g": (ssm_norm_g, m_ssm_norm_g, v_ssm_norm_g),
               "sc_conv_w": (sc_conv_w[0], m_sc_conv_w[0], v_sc_conv_w[0]),
               "norm_ffn_g": (norm_ffn_g, m_norm_ffn_g, v_norm_ffn_g),
               "norm_final_g": (norm_final_g.reshape(1, -1), m_norm_final_g.reshape(1, -1),
                                v_norm_final_g.reshape(1, -1))}
    PW = 1024
    order = list(small_w)

    def pack(arrs):
        rows = []
        for a in arrs:
            flat = a.reshape(-1)
            n = -(-flat.shape[0] // PW) * PW
            rows.append(jnp.pad(flat, (0, n - flat.shape[0])).reshape(-1, PW))
        slab = jnp.concatenate(rows, axis=0)
        return _pad_rows(slab, -(-slab.shape[0] // 8) * 8)

    wp = pack([small_w[k][0] for k in order])
    mp = pack([small_w[k][1] for k in order])
    vp = pack([small_w[k][2] for k in order])
    gp = pack([small_grads[k] for k in order])
    sd, sm, sv = _adamw(wp, gp, mp, vp, "adamw_small")

    def unpack(slab):
        out, row = {}, 0
        for k in order:
            shape = small_w[k][0].shape
            size = 1
            for s in shape:
                size *= s
            nr = -(-size // PW)
            out[k] = slab[row:row + nr].reshape(-1)[:size].reshape(shape)
            row += nr
        return out

    s_delta, s_m, s_v = unpack(sd), unpack(sm), unpack(sv)

    (g_in_full,) = _split_wait("rs2_share_wait", s3_ss, s3_rs, s3_arr, _build_rs_share, after=[sd])
    d_t, m_t, v_t, g_t = _adamw(big[0], g_in_full.reshape(2 * HR_IN, D_MODEL), big_m[0], big_v[0],
                                "adamw_" + names[0], emit_g=True)
    big_grads[0] = g_t.T
    big_out[names[0]] = (d_t.T, m_t.T, v_t.T)
    big_g = dict(zip(names, big_grads))

    weight_order = ["norm_mix_g", "w_in", "ssm_conv_w", "ssm_conv_b", "ssm_dt_bias", "ssm_A_log", "ssm_D",
                    "ssm_norm_g", "sc_conv_w", "w_out", "norm_ffn_g", "w_gate", "w_up", "w_down", "norm_final_g"]
    lead = {"ssm_conv_w", "sc_conv_w", "w_in", "w_out", "w_gate", "w_up", "w_down"}

    def shaped(nm, a):
        if nm == "norm_final_g":
            return a.reshape(D_MODEL)
        return a[None] if nm in lead else a

    grads, deltas, new_m, new_v = [], [], [], []
    for nm in weight_order:
        if nm in big_out:
            g, (d, m, v) = big_g[nm], big_out[nm]
        else:
            g, d, m, v = small_grads[nm], s_delta[nm], s_m[nm], s_v[nm]
        grads.append(shaped(nm, g))
        deltas.append(shaped(nm, d))
        new_m.append(shaped(nm, m))
        new_v.append(shaped(nm, v))
    return (loss, dx[None], *grads, *deltas, *new_m, *new_v)


def _swiglu_bwd(da, dg_factor, du_factor):
    return da * dg_factor.astype(F32), da * du_factor.astype(F32)


def _ffn_fwd(n2, w_gate, w_up):
    T, K = n2.shape
    tn = w_gate.shape[2]
    N = N_CHIPS * tn
    tm = _tile(T, 512)
    sub = _tile(tm, 256)

    def body(a_ref, wg_ref, wu_ref, g_ref, u_ref, act_ref):
        for s in range(tm // sub):
            rows = pl.ds(s * sub, sub)
            a = a_ref[rows, :]
            g = jnp.dot(a, wg_ref[...], preferred_element_type=F32)
            u = jnp.dot(a, wu_ref[...], preferred_element_type=F32)
            sig = _sigmoid(g)
            sg = g * sig
            g_ref[rows, :] = (u * (sig * (1.0 + g - sg))).astype(BF16)
            u_ref[rows, :] = sg.astype(BF16)
            act_ref[rows, :] = (sg * u).astype(BF16)

    a_spec = pl.BlockSpec((tm, K), lambda j, i: (i, 0))
    b_spec = pl.BlockSpec((None, K, tn), lambda j, i: (j, 0, 0))
    o_spec = pl.BlockSpec((tm, tn), lambda j, i: (i, j))
    return pl.pallas_call(
        body, name="ffn_fwd", grid=(N // tn, T // tm),
        in_specs=[a_spec, b_spec, b_spec], out_specs=[o_spec] * 3,
        out_shape=[jax.ShapeDtypeStruct((T, N), BF16)] * 3,
        compiler_params=_cparams(("parallel", "parallel")),
    )(n2, w_gate, w_up)
```

```python
import functools

import jax
import jax.numpy as jnp
from jax import lax
from jax.experimental import pallas as pl
from jax.experimental.pallas import tpu as pltpu

F32 = jnp.float32
BF16 = jnp.bfloat16
MESH = pl.DeviceIdType.MESH

D_MODEL = 2048
D_SSM = 2048
HEADDIM = 64
N_HEADS = 32
N_GROUPS = 8
HEADS_PER_GROUP = 4
N_STATE = 128
CHUNK = 128
K_SSM = 4
K_SC = 3
D_XBC = 4096
D_FF = 5632
D_IN = 12320
D_MAIN = 12288
OFF_XBC, OFF_CB, OFF_CC, OFF_CX = 2048, 6144, 8192, 10240
DT_PAD = 128
EPS = 1e-5
N_CHIPS = 4
N_DEV = 8

ADAM_LR = 0.001
ADAM_B1 = 0.9
ADAM_B2 = 0.999
ADAM_EPS = 1e-08
ADAM_WD = 0.01
ADAM_STEP = 10

V7X_VMEM_BYTES = 64 * 1024 * 1024
VMEM_LIMIT = V7X_VMEM_BYTES - 8 * 1024 * 1024


def _cparams(sem=None):
    if sem is None:
        return pltpu.CompilerParams(vmem_limit_bytes=VMEM_LIMIT)
    return pltpu.CompilerParams(dimension_semantics=sem, vmem_limit_bytes=VMEM_LIMIT)


def _tile(dim, pref, unit=128):
    best = None
    t = unit
    while t <= min(dim, pref):
        if dim % t == 0:
            best = t
        t += unit
    return best if best is not None else dim


def _sigmoid(x):
    return 1.0 / (1.0 + jnp.exp(-x))


def _silu(x):
    return x * _sigmoid(x)


def _dsilu(x):
    s = _sigmoid(x)
    return s * (1.0 + x * (1.0 - s))


def _softplus(x):
    return jnp.maximum(x, 0.0) + jnp.log(1.0 + jnp.exp(-jnp.abs(x)))


MATMUL_VMEM_BUDGET = 44 * 1024 * 1024


def _matmul(pairs, *, ta=False, tb=False, out_dtypes, name, tm=1024, tn=1024, tk=None, extras=(), epilogue=None,
            deps=(), col_shards=False, nsub=1, b3d=False, m_tiles=None, out_buf=None):
    a0, b0 = pairs[0]
    M, K = (a0.shape[1], a0.shape[0]) if ta else a0.shape
    if b3d:
        N = b0.shape[1] if tb else b0.shape[0] * b0.shape[2]
        tk, tn = (b0.shape[2], tn) if tb else (tk, b0.shape[2])
    else:
        N = b0.shape[0] if tb else b0.shape[1]
    tm, tn = _tile(M, tm, 8 if M % 128 else 128), _tile(N, tn)
    npair, nex, ndep, nout = len(pairs), len(extras), len(deps), len(out_dtypes)
    if tk is None:
        fixed = 2 * tm * tn * (sum(jnp.dtype(d).itemsize for d in out_dtypes) + sum(e.dtype.itemsize for e in extras))
        tk = K
        while tk > 128 and (K % tk or tk % 128 or
                            fixed + 2 * npair * 2 * tk * (tm + tn) + (tm * tn * 4 if tk < K else 0) > MATMUL_VMEM_BUDGET):
            tk -= 128
    else:
        tk = _tile(K, tk)
    nk = K // tk
    if nk > 1 or tm % nsub or (tm // nsub) % 128:
        nsub = 1
    sub = tm // nsub
    dims = (((0 if ta else 1,), (1 if tb else 0,)), ((), ()))
    i0, mi = m_tiles if m_tiles is not None else (0, M // tm)
    nbuf = 0 if out_buf is None else 1

    def body(*refs):
        a_refs = refs[0:2 * npair:2]
        b_refs = refs[1:2 * npair:2]
        ex_refs = refs[2 * npair:2 * npair + nex]
        o_refs = refs[2 * npair + nex + ndep + nbuf:2 * npair + nex + ndep + nbuf + nout]

        def dots(rows):
            s = None
            for a_ref, b_ref in zip(a_refs, b_refs):
                a = a_ref[...] if rows is None else (a_ref[:, rows] if ta else a_ref[rows, :])
                d = lax.dot_general(a, b_ref[...], dims, preferred_element_type=F32)
                s = d if s is None else s + d
            return s

        def finish(r, rows):
            ex = [e[...] if rows is None else e[rows, :] for e in ex_refs]
            outs = (r,) if epilogue is None else epilogue(r, *ex)
            for o_ref, o in zip(o_refs, outs):
                if rows is None:
                    o_ref[...] = o.astype(o_ref.dtype)
                else:
                    o_ref[rows, :] = o.astype(o_ref.dtype)

        if nk == 1:
            for s in range(nsub):
                rows = None if nsub == 1 else pl.ds(s * sub, sub)
                finish(dots(rows), rows)
            return

        acc = refs[-1]
        k = pl.program_id(2)

        @pl.when(k == 0)
        def _():
            acc[...] = dots(None)

        @pl.when(jnp.logical_and(k > 0, k < nk - 1))
        def _():
            acc[...] += dots(None)

        @pl.when(k == nk - 1)
        def _():
            finish(acc[...] + dots(None), None)

    a_spec = (pl.BlockSpec((tk, tm), lambda i, j, k: (k, i + i0)) if ta
              else pl.BlockSpec((tm, tk), lambda i, j, k: (i + i0, k)))
    if b3d:
        b_spec = (pl.BlockSpec((None, tn, tk), lambda i, j, k: (k, j, 0)) if tb
                  else pl.BlockSpec((None, tk, tn), lambda i, j, k: (j, k, 0)))
    else:
        b_spec = (pl.BlockSpec((tn, tk), lambda i, j, k: (j, k)) if tb
                  else pl.BlockSpec((tk, tn), lambda i, j, k: (k, j)))
    e_spec = pl.BlockSpec((tm, tn), lambda i, j, k: (i + i0, j))
    if col_shards:
        o_spec = pl.BlockSpec((None, tm, tn), lambda i, j, k: (j, i + i0, 0))
        o_shape = (N // tn, M, tn)
    else:
        o_spec, o_shape = e_spec, (M, N)
    args, in_specs = [], []
    for a, b in pairs:
        args += [a, b]
        in_specs += [a_spec, b_spec]
    args += list(extras) + list(deps) + ([] if out_buf is None else [out_buf])
    in_specs += [e_spec] * nex + [ANY] * (ndep + nbuf)
    outs = pl.pallas_call(
        body,
        name=name,
        grid=(mi, N // tn, nk),
        in_specs=in_specs,
        out_specs=[o_spec] * nout,
        out_shape=[jax.ShapeDtypeStruct(o_shape, dt) for dt in out_dtypes],
        input_output_aliases={} if out_buf is None else {len(args) - 1: 0},
        scratch_shapes=[pltpu.VMEM((tm, tn), F32)] if nk > 1 else [],
        compiler_params=_cparams(("parallel", "parallel", "arbitrary")),
    )(*args)
    return outs


def _cast_into_gather(w, chip_arr, name, split_cols=False, deps=()):
    R, C = w.shape
    hr, hc = (R, C // 2) if split_cols else (R // 2, C)
    tr = _tile(hr, 512, 8)
    nb = hr // tr

    def body(chip_ref, w_ref, *rest):
        rest[-1][...] = w_ref[...].astype(BF16)

    in_map = (lambda h, i, chip_ref: (i, h)) if split_cols else (lambda h, i, chip_ref: (h * nb + i, 0))
    grid_spec = pltpu.PrefetchScalarGridSpec(
        num_scalar_prefetch=1, grid=(2, nb),
        in_specs=[pl.BlockSpec((tr, hc), in_map)] + [ANY] * len(deps),
        out_specs=pl.BlockSpec((None, tr, hc), lambda h, i, chip_ref: (2 * chip_ref[0] + h, i, 0)))
    return pl.pallas_call(
        body, name=name, grid_spec=grid_spec,
        out_shape=jax.ShapeDtypeStruct((N_DEV, hr, hc), BF16),
        compiler_params=_cparams(("parallel", "parallel")),
    )(chip_arr, w, *deps)


def _tie(small, token, name):
    def body(s_ref, t_ref, o_ref):
        o_ref[...] = s_ref[...]

    vm = pl.BlockSpec(memory_space=pltpu.VMEM)
    return pl.pallas_call(body, name=name, in_specs=[vm, ANY], out_specs=vm,
                          out_shape=jax.ShapeDtypeStruct(small.shape, small.dtype))(small, token)


def _rmsnorm_fwd(x, g, name):
    T, D = x.shape
    tt = _tile(T, 256)

    def body(x_ref, g_ref, n_ref):
        xv = x_ref[...]
        r = lax.rsqrt(jnp.mean(xv * xv, axis=-1, keepdims=True) + EPS)
        n_ref[...] = (xv * r * g_ref[...]).astype(BF16)

    return pl.pallas_call(
        body, name=name, grid=(T // tt,),
        in_specs=[pl.BlockSpec((tt, D), lambda i: (i, 0)), pl.BlockSpec((1, D), lambda i: (0, 0))],
        out_specs=pl.BlockSpec((tt, D), lambda i: (i, 0)),
        out_shape=jax.ShapeDtypeStruct((T, D), BF16),
        compiler_params=_cparams(("parallel",)),
    )(x, g)


def _rmsnorm_bwd(dn, x, g, res, name):
    T, D = x.shape
    tt = _tile(T, 256)

    def body(dn_ref, x_ref, g_ref, res_ref, dx_ref, dxb_ref, dg_ref):
        @pl.when(pl.program_id(0) == 0)
        def _():
            dg_ref[...] = jnp.zeros_like(dg_ref)

        xv = x_ref[...]
        dy = dn_ref[...].astype(F32)
        r = lax.rsqrt(jnp.mean(xv * xv, axis=-1, keepdims=True) + EPS)
        xhat = xv * r
        dxh = dy * g_ref[...]
        dx = res_ref[...] + r * (dxh - xhat * jnp.mean(dxh * xhat, axis=-1, keepdims=True))
        dx_ref[...] = dx
        dxb_ref[...] = dx.astype(BF16)
        dg_ref[...] += jnp.sum(dy * xhat, axis=0, keepdims=True)

    tok = pl.BlockSpec((tt, D), lambda i: (i, 0))
    vec = pl.BlockSpec((1, D), lambda i: (0, 0))
    return pl.pallas_call(
        body, name=name, grid=(T // tt,),
        in_specs=[tok, tok, vec, tok],
        out_specs=[tok, tok, vec],
        out_shape=[jax.ShapeDtypeStruct((T, D), F32), jax.ShapeDtypeStruct((T, D), BF16),
                   jax.ShapeDtypeStruct((1, D), F32)],
        compiler_params=_cparams(("arbitrary",)),
    )(dn, x, g, res)


def _loss_and_final_bwd(h2, target, gf):
    T, D = h2.shape
    tt = _tile(T, 256)

    def body(h_ref, t_ref, g_ref, dh_ref, dhb_ref, dg_ref, loss_ref):
        @pl.when(pl.program_id(0) == 0)
        def _():
            dg_ref[...] = jnp.zeros_like(dg_ref)
            loss_ref[...] = jnp.zeros_like(loss_ref)

        xv = h_ref[...]
        r = lax.rsqrt(jnp.mean(xv * xv, axis=-1, keepdims=True) + EPS)
        xhat = xv * r
        err = xhat * g_ref[...] - t_ref[...]
        loss_ref[...] += 0.5 * jnp.sum(jnp.mean(err * err, axis=-1, keepdims=True), axis=0, keepdims=True)
        dy = err * (1.0 / D)
        dxh = dy * g_ref[...]
        dx = r * (dxh - xhat * jnp.mean(dxh * xhat, axis=-1, keepdims=True))
        dh_ref[...] = dx
        dhb_ref[...] = dx.astype(BF16)
        dg_ref[...] += jnp.sum(dy * xhat, axis=0, keepdims=True)

    tok = pl.BlockSpec((tt, D), lambda i: (i, 0))
    vec = pl.BlockSpec((1, D), lambda i: (0, 0))
    return pl.pallas_call(
        body, name="loss_final_bwd", grid=(T // tt,),
        in_specs=[tok, tok, vec],
        out_specs=[tok, tok, vec, pl.BlockSpec((1, 1), lambda i: (0, 0))],
        out_shape=[jax.ShapeDtypeStruct((T, D), F32), jax.ShapeDtypeStruct((T, D), BF16),
                   jax.ShapeDtypeStruct((1, D), F32), jax.ShapeDtypeStruct((1, 1), F32)],
        compiler_params=_cparams(("arbitrary",)),
    )(h2, target, gf)


def _gated_norm_fwd(y, proj, g):
    T, D = y.shape
    tt = _tile(T, 256)

    def body(y_ref, z_ref, g_ref, o_ref):
        yg = y_ref[...] * _silu(z_ref[...])
        r = lax.rsqrt(jnp.mean(yg * yg, axis=-1, keepdims=True) + EPS)
        o_ref[...] = (yg * r * g_ref[...]).astype(BF16)

    tok = pl.BlockSpec((tt, D), lambda i: (i, 0))
    return pl.pallas_call(
        body, name="gated_norm_fwd", grid=(T // tt,),
        in_specs=[tok, tok, pl.BlockSpec((1, D), lambda i: (0, 0))],
        out_specs=tok,
        out_shape=jax.ShapeDtypeStruct((T, 2 * D_MODEL), BF16),
        compiler_params=_cparams(("parallel",)),
    )(y, proj, g)


def _gated_norm_bwd(dmix, y, proj, g, dproj):
    T, D = y.shape
    tt = _tile(T, 256)

    def body(do_ref, y_ref, z_ref, g_ref, dp_ref, dy_ref, dz_ref, dg_ref):
        @pl.when(pl.program_id(0) == 0)
        def _():
            dg_ref[...] = jnp.zeros_like(dg_ref)

        yv, zv = y_ref[...], z_ref[...]
        do = do_ref[...].astype(F32)
        sz = _silu(zv)
        yg = yv * sz
        r = lax.rsqrt(jnp.mean(yg * yg, axis=-1, keepdims=True) + EPS)
        xhat = yg * r
        dxh = do * g_ref[...]
        dyg = r * (dxh - xhat * jnp.mean(dxh * xhat, axis=-1, keepdims=True))
        dy_ref[...] = dyg * sz
        dz_ref[...] = (dyg * yv * _dsilu(zv)).astype(BF16)
        dg_ref[...] += jnp.sum(do * xhat, axis=0, keepdims=True)

    tok = pl.BlockSpec((tt, D), lambda i: (i, 0))
    vec = pl.BlockSpec((1, D), lambda i: (0, 0))
    return pl.pallas_call(
        body, name="gated_norm_bwd", grid=(T // tt,),
        in_specs=[tok, tok, tok, vec, ANY],
        out_specs=[tok, tok, vec],
        out_shape=[jax.ShapeDtypeStruct((T, D), F32), jax.ShapeDtypeStruct(dproj.shape, BF16),
                   jax.ShapeDtypeStruct((1, D), F32)],
        input_output_aliases={4: 1},
        compiler_params=_cparams(("arbitrary",)),
    )(dmix, y, proj, g, dproj)


HALO = 8


def _shift_down(cur, prev8, s):
    ext = jnp.concatenate([prev8, cur], axis=0)
    return pltpu.roll(ext, s, axis=0)[HALO:]


def _shift_up(cur, next8, s):
    n = cur.shape[0]
    ext = jnp.concatenate([cur, next8], axis=0)
    return pltpu.roll(ext, n + HALO - s, axis=0)[:n]


def _conv_specs(tt, cb, col_off_blocks, nt):
    hb = tt // HALO
    cur = pl.BlockSpec((tt, cb), lambda j, i: (i, col_off_blocks + j))
    prev = pl.BlockSpec((HALO, cb), lambda j, i: (jnp.maximum(i * hb - 1, 0), col_off_blocks + j))
    nxt = pl.BlockSpec((HALO, cb), lambda j, i: (jnp.minimum((i + 1) * hb, nt * hb - 1), col_off_blocks + j))
    return cur, prev, nxt


def _causal_conv(cur, prev8, w, K):
    y = cur * w[K - 1:K, :]
    for k in range(K - 1):
        y = y + _shift_down(cur, prev8, K - 1 - k) * w[k:k + 1, :]
    return y


def _anticausal_conv(cur, next8, w, K):
    y = cur * w[K - 1:K, :]
    for k in range(K - 1):
        y = y + _shift_up(cur, next8, K - 1 - k) * w[k:k + 1, :]
    return y


def _ssm_conv_fwd(proj, w8, b):
    T = proj.shape[0]
    tt, cb = _tile(T, 512), 512
    nt = T // tt
    cur, prev, _ = _conv_specs(tt, cb, OFF_XBC // cb, nt)

    def body(u_ref, up_ref, w_ref, b_ref, o_ref):
        first = pl.program_id(1) == 0
        p8 = jnp.where(first, 0.0, up_ref[...])
        pre = _causal_conv(u_ref[...], p8, w_ref[...], K_SSM) + b_ref[...]
        o_ref[...] = _silu(pre)

    return pl.pallas_call(
        body, name="ssm_conv_fwd", grid=(D_XBC // cb, nt),
        in_specs=[cur, prev, pl.BlockSpec((8, cb), lambda j, i: (0, j)), pl.BlockSpec((1, cb), lambda j, i: (0, j))],
        out_specs=pl.BlockSpec((tt, cb), lambda j, i: (i, j)),
        out_shape=jax.ShapeDtypeStruct((T, D_XBC), F32),
        compiler_params=_cparams(("parallel", "parallel")),
    )(proj, proj, w8, b)


def _ssm_conv_bwd(dact, proj, w8, b, dproj):
    T = proj.shape[0]
    tt, cb = _tile(T, 512), 512
    nt = T // tt
    cur, prev, nxt = _conv_specs(tt, cb, OFF_XBC // cb, nt)
    dcur, dprev, dnxt = _conv_specs(tt, cb, 0, nt)

    def dpre_of(d, u, p8, w, bb):
        pre = _causal_conv(u, p8, w, K_SSM) + bb
        return d * _dsilu(pre)

    def body(d_ref, dn_ref, u_ref, up_ref, un_ref, w_ref, b_ref, dp_ref, dx_ref, dw_ref, db_ref):
        i = pl.program_id(1)

        @pl.when(i == 0)
        def _():
            dw_ref[...] = jnp.zeros_like(dw_ref)
            db_ref[...] = jnp.zeros_like(db_ref)

        w, bb = w_ref[...], b_ref[...]
        u = u_ref[...]
        p8 = jnp.where(i == 0, 0.0, up_ref[...])
        dpre = dpre_of(d_ref[...], u, p8, w, bb)
        un = un_ref[...]
        dpre_n = dpre_of(dn_ref[...], un, u[tt - HALO:, :], w, bb)
        dpre_n = jnp.where(i == nt - 1, 0.0, dpre_n)
        dx_ref[...] = _anticausal_conv(dpre, dpre_n, w, K_SSM).astype(BF16)
        rows = [jnp.sum(dpre * _shift_down(u, p8, K_SSM - 1 - k), axis=0, keepdims=True) for k in range(K_SSM - 1)]
        rows.append(jnp.sum(dpre * u, axis=0, keepdims=True))
        rows.append(jnp.zeros((8 - K_SSM, cb), F32))
        dw_ref[...] += jnp.concatenate(rows, axis=0)
        db_ref[...] += jnp.sum(dpre, axis=0, keepdims=True)

    wspec = pl.BlockSpec((8, cb), lambda j, i: (0, j))
    bspec = pl.BlockSpec((1, cb), lambda j, i: (0, j))
    return pl.pallas_call(
        body, name="ssm_conv_bwd", grid=(D_XBC // cb, nt),
        in_specs=[dcur, dnxt, cur, prev, nxt, wspec, bspec, ANY],
        out_specs=[pl.BlockSpec((tt, cb), lambda j, i: (i, OFF_XBC // cb + j)), wspec, bspec],
        out_shape=[jax.ShapeDtypeStruct(dproj.shape, BF16), jax.ShapeDtypeStruct((8, D_XBC), F32),
                   jax.ShapeDtypeStruct((1, D_XBC), F32)],
        input_output_aliases={7: 0},
        compiler_params=_cparams(("parallel", "arbitrary")),
    )(dact, dact, proj, proj, proj, w8, b, dproj)


SCB = 512
SC3 = 3 * SCB


def _sc_specs(tt, nt):
    hb = tt // HALO
    cur = pl.BlockSpec((tt, SC3), lambda j, i: (i, OFF_CB // SC3 + j))
    prev = pl.BlockSpec((HALO, SC3), lambda j, i: (jnp.maximum(i * hb - 1, 0), OFF_CB // SC3 + j))
    nxt = pl.BlockSpec((HALO, SC3), lambda j, i: (jnp.minimum((i + 1) * hb, nt * hb - 1), OFF_CB // SC3 + j))
    return cur, prev, nxt


def _shortconv_fwd(proj, w8, ymix):
    T = proj.shape[0]
    tt = _tile(T, 512)
    nt = T // tt
    cur, prev, _ = _sc_specs(tt, nt)

    def body(p_ref, pp_ref, w_ref, y_ref, o_ref):
        p, pp = p_ref[...], pp_ref[...]
        v = p[:, SCB:2 * SCB] * p[:, 2 * SCB:]
        vp = jnp.where(pl.program_id(1) == 0, 0.0, pp[:, SCB:2 * SCB] * pp[:, 2 * SCB:])
        o_ref[...] = (p[:, :SCB] * _causal_conv(v, vp, w_ref[...], K_SC)).astype(BF16)

    return pl.pallas_call(
        body, name="shortconv_fwd", grid=(D_MODEL // SCB, nt),
        in_specs=[cur, prev, pl.BlockSpec((8, SCB), lambda j, i: (0, j)), ANY],
        out_specs=pl.BlockSpec((tt, SCB), lambda j, i: (i, D_SSM // SCB + j)),
        out_shape=jax.ShapeDtypeStruct(ymix.shape, BF16),
        input_output_aliases={3: 0},
        compiler_params=_cparams(("parallel", "parallel")),
    )(proj, proj, w8, ymix)


def _shortconv_bwd(dmix, proj, w8):
    T = proj.shape[0]
    tt = _tile(T, 512)
    nt = T // tt
    hb = tt // HALO
    cur, prev, nxt = _sc_specs(tt, nt)
    d_s = pl.BlockSpec((tt, SCB), lambda j, i: (i, D_SSM // SCB + j))
    dn_s = pl.BlockSpec((HALO, SCB), lambda j, i: (jnp.minimum((i + 1) * hb, nt * hb - 1), D_SSM // SCB + j))

    def body(d_ref, dn_ref, p_ref, pp_ref, pn_ref, w_ref, dp_ref, dw_ref):
        i = pl.program_id(1)

        @pl.when(i == 0)
        def _():
            dw_ref[...] = jnp.zeros_like(dw_ref)

        w = w_ref[...]
        p, pp = p_ref[...], pp_ref[...]
        gb, gc, u = p[:, :SCB], p[:, SCB:2 * SCB], p[:, 2 * SCB:]
        v = gc * u
        vp = jnp.where(i == 0, 0.0, pp[:, SCB:2 * SCB] * pp[:, 2 * SCB:])
        d = d_ref[...].astype(F32)
        dp_ref[:, :SCB] = (d * _causal_conv(v, vp, w, K_SC)).astype(BF16)
        dcv = d * gb
        dcv_n = jnp.where(i == nt - 1, 0.0, dn_ref[...].astype(F32) * pn_ref[:, :SCB])
        dv = _anticausal_conv(dcv, dcv_n, w, K_SC)
        dp_ref[:, SCB:2 * SCB] = (dv * u).astype(BF16)
        dp_ref[:, 2 * SCB:] = (dv * gc).astype(BF16)
        rows = [jnp.sum(dcv * _shift_down(v, vp, K_SC - 1 - k), axis=0, keepdims=True) for k in range(K_SC - 1)]
        rows.append(jnp.sum(dcv * v, axis=0, keepdims=True))
        rows.append(jnp.zeros((8 - K_SC, SCB), F32))
        dw_ref[...] += jnp.concatenate(rows, axis=0)

    wspec = pl.BlockSpec((8, SCB), lambda j, i: (0, j))
    return pl.pallas_call(
        body, name="shortconv_bwd", grid=(D_MODEL // SCB, nt),
        in_specs=[d_s, dn_s, cur, prev, nxt, wspec],
        out_specs=[cur, wspec],
        out_shape=[jax.ShapeDtypeStruct((T, D_MAIN), BF16), jax.ShapeDtypeStruct((8, D_MODEL), F32)],
        compiler_params=_cparams(("parallel", "arbitrary")),
    )(dmix, dmix, proj, proj, proj, w8)


GW = HEADS_PER_GROUP * HEADDIM


def _dot(a, b):
    return jnp.dot(a.astype(BF16), b.astype(BF16), preferred_element_type=F32)


def _dot_nt(a, b):
    return lax.dot_general(a.astype(BF16), b.astype(BF16), (((1,), (1,)), ((), ())), preferred_element_type=F32)


def _dot_tn(a, b):
    return lax.dot_general(a.astype(BF16), b.astype(BF16), (((0,), (0,)), ((), ())), preferred_element_type=F32)


def _bf16_terms(x, n):
    terms, r = [], x
    for _ in range(n):
        t = r.astype(BF16)
        terms.append(t)
        r = r - t.astype(F32)
    return terms


def _dot_sel(a, sel, n=2):
    s = sel.astype(BF16)
    return sum(jnp.dot(t, s, preferred_element_type=F32) for t in _bf16_terms(a, n))


def _sel_dot(sel, b, n=2):
    s = sel.astype(BF16)
    return sum(jnp.dot(s, t, preferred_element_type=F32) for t in _bf16_terms(b, n))


def _sel_dot_nt(sel, b, n=2):
    s = sel.astype(BF16)
    return sum(lax.dot_general(s, t, (((1,), (1,)), ((), ())), preferred_element_type=F32)
               for t in _bf16_terms(b, n))


def _head_cols(rows):
    parts = [jnp.broadcast_to(rows[r:r + 1, :], (HEADDIM, CHUNK)) for r in range(HEADS_PER_GROUP)]
    return jnp.concatenate(parts, axis=0).T


def _head_rows(rows):
    parts = [jnp.broadcast_to(rows[r:r + 1, :], (HEADDIM, N_STATE)) for r in range(HEADS_PER_GROUP)]
    return jnp.concatenate(parts, axis=0)


def _ssd_common(dtr, bias, alog):
    dt = _softplus(dtr + bias)
    A = -jnp.exp(alog)
    a = dt * A
    ki = lax.broadcasted_iota(jnp.int32, (CHUNK, CHUNK), 0)
    si = lax.broadcasted_iota(jnp.int32, (CHUNK, CHUNK), 1)
    upper = (ki <= si).astype(F32)
    cs = _dot_sel(a, upper, 3)
    cs_last = jnp.broadcast_to(cs[:, CHUNK - 1:CHUNK], (8, CHUNK))
    return dt, A, a, cs, cs_last


def _decay_matrix(cs, r):
    li = lax.broadcasted_iota(jnp.int32, (CHUNK, CHUNK), 0)
    si = lax.broadcasted_iota(jnp.int32, (CHUNK, CHUNK), 1)
    causal = li >= si
    R = jnp.broadcast_to(cs[r:r + 1, :], (CHUNK, CHUNK))
    seg = jnp.where(causal, R.T - R, 0.0)
    return jnp.where(causal, jnp.exp(seg), 0.0)


GXBC = GW + 2 * N_STATE


GS = 4


def _ssd_in_specs(nc, rev):
    cix = (lambda c: nc - 1 - c) if rev else (lambda c: c)
    x_s = pl.BlockSpec((CHUNK, GS * GW), lambda g, c: (cix(c), g))
    xbc_s = pl.BlockSpec((CHUNK, GS * GXBC), lambda g, c: (cix(c), g))
    dtr_s = pl.BlockSpec((GS, 8, CHUNK), lambda g, c: (g, 0, cix(c)))
    row_s = pl.BlockSpec((GS, 8, CHUNK), lambda g, c: (g, 0, 0))
    drep_s = pl.BlockSpec((1, GS * GW), lambda g, c: (0, g))
    hs_s = pl.BlockSpec((1, GS * GW, N_STATE), lambda g, c: (cix(c), g, 0))
    return x_s, xbc_s, dtr_s, row_s, drep_s, hs_s


def _xbc_parts(xbc_ref, gi):
    o = gi * GXBC
    return xbc_ref[:, o:o + GW], xbc_ref[:, o + GW:o + GW + N_STATE], xbc_ref[:, o + GW + N_STATE:o + GXBC]


def _ssd_fwd(xbc, dtr, bias, alog, drep):
    T = xbc.shape[0]
    nc = T // CHUNK
    x_s, xbc_s, dtr_s, row_s, drep_s, hs_s = _ssd_in_specs(nc, False)

    def body(xbc_ref, dtr_ref, bias_ref, alog_ref, drep_ref, y_ref, hs_ref, h_scr):
        @pl.when(pl.program_id(1) == 0)
        def _():
            h_scr[...] = jnp.zeros_like(h_scr)

        for gi in range(GS):
            cols, rows = slice(gi * GW, (gi + 1) * GW), pl.ds(gi * GW, GW)
            x, Bm, Cm = _xbc_parts(xbc_ref, gi)
            dt, A, a, cs, cs_last = _ssd_common(dtr_ref[gi], bias_ref[gi], alog_ref[gi])
            E = _head_cols(jnp.exp(cs))
            W = _head_cols(jnp.exp(cs_last - cs) * dt)
            X = (x * _head_cols(dt)).astype(BF16)
            CB = _dot_nt(Cm, Bm)
            col = lax.broadcasted_iota(jnp.int32, (CHUNK, GW), 1) // HEADDIM
            y = jnp.zeros((CHUNK, GW), F32)
            for r in range(HEADS_PER_GROUP):
                M = CB * _decay_matrix(cs, r)
                y = y + jnp.where(col == r, _dot(M, X), 0.0)
            h = h_scr[rows, :]
            hs_ref[0, rows, :] = h
            y = y + _dot_nt(Cm, h) * E
            y_ref[:, cols] = y + drep_ref[:, cols] * x
            h_scr[rows, :] = h * _head_rows(jnp.exp(cs_last)) + _dot_tn(x * W, Bm)

    return pl.pallas_call(
        body, name="ssd_fwd", grid=(N_GROUPS // GS, nc),
        in_specs=[xbc_s, dtr_s, row_s, row_s, drep_s],
        out_specs=[x_s, hs_s],
        out_shape=[jax.ShapeDtypeStruct((T, D_SSM), F32), jax.ShapeDtypeStruct((nc, D_SSM, N_STATE), F32)],
        scratch_shapes=[pltpu.VMEM((GS * GW, N_STATE), F32)],
        compiler_params=_cparams(("parallel", "arbitrary")),
    )(xbc, dtr, bias, alog, drep)


def _ssd_bwd(xbc, dtr, bias, alog, drep, dy, hs):
    T = xbc.shape[0]
    nc = T // CHUNK
    x_s, xbc_s, dtr_s, row_s, drep_s, hs_s = _ssd_in_specs(nc, True)

    def body(xbc_ref, dtr_ref, bias_ref, alog_ref, drep_ref, dy_ref, hs_ref,
             dxbc_ref, ddtr_ref, dbias_ref, dalog_ref, dd_ref, dh_scr):
        @pl.when(pl.program_id(1) == 0)
        def _():
            dh_scr[...] = jnp.zeros_like(dh_scr)
            dbias_ref[...] = jnp.zeros_like(dbias_ref)
            dalog_ref[...] = jnp.zeros_like(dalog_ref)
            dd_ref[...] = jnp.zeros_like(dd_ref)

        for gi in range(GS):
            one_group(gi, xbc_ref, dtr_ref, bias_ref, alog_ref, drep_ref, dy_ref, hs_ref,
                      dxbc_ref, ddtr_ref, dbias_ref, dalog_ref, dd_ref, dh_scr)

    def one_group(gi, xbc_ref, dtr_ref, bias_ref, alog_ref, drep_ref, dy_ref, hs_ref,
                  dxbc_ref, ddtr_ref, dbias_ref, dalog_ref, dd_ref, dh_scr):
        cols, rows, o = slice(gi * GW, (gi + 1) * GW), pl.ds(gi * GW, GW), gi * GXBC
        x, Bm, Cm = _xbc_parts(xbc_ref, gi)
        dY = dy_ref[:, cols]
        dt, A, a, cs, cs_last = _ssd_common(dtr_ref[gi], bias_ref[gi], alog_ref[gi])
        E = _head_cols(jnp.exp(cs))
        DT = _head_cols(dt)
        Wd = _head_cols(jnp.exp(cs_last - cs))
        X = x * DT
        h = hs_ref[0, rows, :]
        dS = dh_scr[rows, :]
        CB = _dot_nt(Cm, Bm)
        col = lax.broadcasted_iota(jnp.int32, (CHUNK, GW), 1) // HEADDIM
        rowid = lax.broadcasted_iota(jnp.int32, (8, CHUNK), 0)
        lane = lax.broadcasted_iota(jnp.int32, (8, CHUNK), 1)
        hsel = (lax.broadcasted_iota(jnp.int32, (8, GW), 1) // HEADDIM
                == lax.broadcasted_iota(jnp.int32, (8, GW), 0)).astype(F32)
        ones8 = jnp.ones((8, CHUNK), F32)

        dX = jnp.zeros((CHUNK, GW), F32)
        dCB = jnp.zeros((CHUNK, CHUNK), F32)
        dcs = jnp.zeros((8, CHUNK), F32)
        for r in range(HEADS_PER_GROUP):
            L = _decay_matrix(cs, r)
            M = CB * L
            G = _dot_nt(jnp.where(col == r, dY, 0.0), X)
            GL = G * L
            dCB = dCB + GL
            Wm = GL * CB
            colsum = jnp.sum(Wm, axis=0, keepdims=True)
            rowsum = _sel_dot_nt(ones8, Wm)
            dcs = dcs + jnp.where(rowid == r, rowsum - colsum, 0.0)
            dX = dX + jnp.where(col == r, _dot_tn(M, dY), 0.0)
        dC = _dot(dCB, Bm)
        dB = _dot_tn(dCB, Cm)
        T1 = _dot_nt(Bm, dS)
        dX = dX + T1 * Wd
        dB = dB + _dot(X * Wd, dS)
        pdec = _sel_dot_nt(hsel, X * T1 * Wd)
        dcs = dcs - pdec
        dlast = jnp.sum(pdec, axis=1, keepdims=True) \
            + jnp.exp(cs_last[:, 0:1]) * jnp.sum(_sel_dot(hsel, dS * h), axis=1, keepdims=True)
        dYE = dY * E
        dC = dC + _dot(dYE, h)
        yoff = _dot_nt(Cm, h) * E
        dcs = dcs + _sel_dot_nt(hsel, dY * yoff)
        dcs = dcs + jnp.where(lane == CHUNK - 1, dlast, 0.0)
        ki = lax.broadcasted_iota(jnp.int32, (CHUNK, CHUNK), 0)
        si = lax.broadcasted_iota(jnp.int32, (CHUNK, CHUNK), 1)
        lower = (ki >= si).astype(F32)
        da = _dot_sel(dcs, lower)
        ddt = da * A + _sel_dot_nt(hsel, dX * x)
        ddtr = ddt * _sigmoid(dtr_ref[gi] + bias_ref[gi])
        ddtr_ref[gi] = ddtr
        dbias_ref[gi] += ddtr
        dalog_ref[gi] += da * a
        dxbc_ref[:, o:o + GW] = dX * DT + drep_ref[:, cols] * dY
        dd_ref[:, cols] += jnp.sum(dY * x, axis=0, keepdims=True)
        dxbc_ref[:, o + GW:o + GW + N_STATE] = dB
        dxbc_ref[:, o + GW + N_STATE:o + GXBC] = dC
        dh_scr[rows, :] = dS * _head_rows(jnp.exp(cs_last)) + _dot_tn(dYE, Cm)

    return pl.pallas_call(
        body, name="ssd_bwd", grid=(N_GROUPS // GS, nc),
        in_specs=[xbc_s, dtr_s, row_s, row_s, drep_s, x_s, hs_s],
        out_specs=[xbc_s, dtr_s, row_s, row_s, drep_s],
        out_shape=[jax.ShapeDtypeStruct((T, D_XBC), F32),
                   jax.ShapeDtypeStruct((N_GROUPS, 8, T), F32),
                   jax.ShapeDtypeStruct((N_GROUPS, 8, CHUNK), F32),
                   jax.ShapeDtypeStruct((N_GROUPS, 8, CHUNK), F32),
                   jax.ShapeDtypeStruct((1, D_SSM), F32)],
        scratch_shapes=[pltpu.VMEM((GS * GW, N_STATE), F32)],
        compiler_params=_cparams(("parallel", "arbitrary")),
    )(xbc, dtr, bias, alog, drep, dy, hs)


def _adamw(w, g, m, v, name, deps=(), emit_g=False):
    R, C = w.shape
    tr = _tile(R, 256, 8)
    nd = len(deps)
    nout = 4 if emit_g else 3

    def body(w_ref, g_ref, m_ref, v_ref, *rest):
        outs = rest[nd:]
        gv = g_ref[...]
        mn = ADAM_B1 * m_ref[...] + (1.0 - ADAM_B1) * gv
        vn = ADAM_B2 * v_ref[...] + (1.0 - ADAM_B2) * (gv * gv)
        m_hat = mn / (1.0 - ADAM_B1 ** ADAM_STEP)
        v_hat = vn / (1.0 - ADAM_B2 ** ADAM_STEP)
        outs[0][...] = -ADAM_LR * (m_hat / (jnp.sqrt(v_hat) + ADAM_EPS) + ADAM_WD * w_ref[...])
        outs[1][...] = mn
        outs[2][...] = vn
        if emit_g:
            outs[3][...] = gv

    spec = pl.BlockSpec((tr, C), lambda i: (i, 0))
    return pl.pallas_call(
        body, name=name, grid=(R // tr,),
        in_specs=[spec] * 4 + [ANY] * nd, out_specs=[spec] * nout,
        out_shape=[jax.ShapeDtypeStruct((R, C), F32)] * nout,
        compiler_params=_cparams(("parallel",)),
    )(w, g, m, v, *deps)


ANY = pl.BlockSpec(memory_space=pl.ANY)


def _place():
    x, y, c = lax.axis_index("x"), lax.axis_index("y"), lax.axis_index("c")
    return x, y, c


def _other_chips(x, y):
    return [(1 - x, y), (x, 1 - y), (1 - x, 1 - y)]


def _allgather_inplace(bufs, splits, first_done=False):
    n = len(bufs)

    def body(*refs):
        o_refs = refs[n:2 * n]
        send_sems, recv_sems = refs[2 * n:]
        x, y, c = _place()
        xn, yn, dg, sibling = (1 - x, y), (x, 1 - y), (1 - x, 1 - y), (x, y, 1 - c)

        def blk(k, chip, pc):
            return o_refs[k].at[4 * chip[0] + 2 * chip[1] + pc]

        def part(k, ref, p):
            kind, s = splits[k]
            _, R, C = bufs[k].shape
            if kind == "rows":
                return ref.at[pl.ds(0, s)] if p == 0 else ref.at[pl.ds(s, R - s)]
            return ref.at[:, pl.ds(0, s)] if p == 0 else ref.at[:, pl.ds(s, C - s)]

        def copy(k, slot, ref, to):
            return pltpu.make_async_remote_copy(
                src_ref=ref, dst_ref=ref, send_sem=send_sems.at[k, slot], recv_sem=recv_sems.at[k, slot],
                device_id=to, device_id_type=MESH)

        sent = []

        def send(k, slot, ref, to):
            cp = copy(k, slot, ref, to)
            cp.start()
            sent.append(cp)

        if not first_done:
            for k in range(n):
                send(k, 0, blk(k, (x, y), c), (*xn, c))
                send(k, 1, blk(k, (x, y), c), (*yn, c))
        for k in range(n):
            bx, by = blk(k, xn, c), blk(k, yn, c)
            if not first_done:
                copy(k, 0, bx, sibling).wait_recv()
            send(k, 2, part(k, bx, 0), (*yn, c))
            send(k, 4, bx, sibling)
            if not first_done:
                copy(k, 1, by, sibling).wait_recv()
            send(k, 3, part(k, by, 1), (*xn, c))
            send(k, 5, by, sibling)
        for k in range(n):
            d0, d1 = part(k, blk(k, dg, c), 0), part(k, blk(k, dg, c), 1)
            copy(k, 2, d0, sibling).wait_recv()
            send(k, 6, d0, sibling)
            copy(k, 3, d1, sibling).wait_recv()
            send(k, 7, d1, sibling)
        for k in range(n):
            copy(k, 4, blk(k, xn, 1 - c), sibling).wait_recv()
            copy(k, 5, blk(k, yn, 1 - c), sibling).wait_recv()
            copy(k, 6, part(k, blk(k, dg, 1 - c), 0), sibling).wait_recv()
            copy(k, 7, part(k, blk(k, dg, 1 - c), 1), sibling).wait_recv()
        for cp in sent:
            cp.wait_send()

    return pl.pallas_call(
        body, name="allgather_w_in",
        in_specs=[ANY] * n, out_specs=[ANY] * n,
        out_shape=[jax.ShapeDtypeStruct(b.shape, b.dtype) for b in bufs],
        input_output_aliases={k: k for k in range(n)},
        scratch_shapes=[pltpu.SemaphoreType.DMA((n, 8)), pltpu.SemaphoreType.DMA((n, 8))],
    )(*bufs)


HBM = pl.BlockSpec(memory_space=pltpu.HBM)
SEM = pl.BlockSpec(memory_space=pltpu.SEMAPHORE)
EFFECT = pltpu.SideEffectType.DATAFLOW_SIDE_EFFECTING


def _split_start(name, arrays, build, n_copies, after=()):
    na, nd = len(arrays), len(after)

    def body(*refs):
        send_sems, recv_sems = refs[na + nd], refs[na + nd + 1]
        for cp in build(refs[:na], send_sems, recv_sems):
            cp.start()
        refs[-1][...] = jnp.zeros((8, 128), F32)

    outs = pl.pallas_call(
        body, name=name,
        out_shape=(pltpu.SemaphoreType.DMA((n_copies,)), pltpu.SemaphoreType.DMA((n_copies,)),
                   *[pltpu.HBM(a.shape, a.dtype) for a in arrays], jax.ShapeDtypeStruct((8, 128), F32)),
        in_specs=[HBM] * na + [ANY] * nd,
        out_specs=(SEM, SEM, *[HBM] * na, pl.BlockSpec(memory_space=pltpu.VMEM)),
        input_output_aliases={i: 2 + i for i in range(na)},
        compiler_params=pltpu.CompilerParams(has_side_effects=EFFECT),
    )(*[pltpu.with_memory_space_constraint(a, pltpu.HBM) for a in arrays], *after)
    return outs[0], outs[1], list(outs[2:2 + na]), outs[-1]


def _split_wait(name, send_sems, recv_sems, arrays, build, after):
    na = len(arrays)

    def body(*refs):
        for cp in build(refs[:na], refs[na], refs[na + 1]):
            cp.wait_send()
            cp.wait_recv()

    outs = pl.pallas_call(
        body, name=name,
        out_shape=tuple(pltpu.HBM(a.shape, a.dtype) for a in arrays),
        in_specs=[HBM] * na + [SEM, SEM] + [ANY] * len(after),
        out_specs=tuple([HBM] * na),
        input_output_aliases={i: i for i in range(na)},
        compiler_params=pltpu.CompilerParams(has_side_effects=EFFECT),
    )(*arrays, send_sems, recv_sems, *after)
    return list(outs)


def _remote(src, dst, send_sems, recv_sems, i, to):
    return pltpu.make_async_remote_copy(src_ref=src, dst_ref=dst, send_sem=send_sems.at[i], recv_sem=recv_sems.at[i],
                                        device_id=to, device_id_type=MESH)


def _build_ag_first(refs, ss, rs):
    x, y, c = _place()
    cps = []
    for k, ref in enumerate(refs):
        blk = ref.at[4 * x + 2 * y + c]
        cps += [_remote(blk, blk, ss, rs, 2 * k, (1 - x, y, c)), _remote(blk, blk, ss, rs, 2 * k + 1, (x, 1 - y, c))]
    return cps


def _build_ag_ici(refs, ss, rs):
    x, y, c = _place()
    cps = []
    for k, ref in enumerate(refs):
        blk = ref.at[4 * x + 2 * y + c]
        for j, (px, py) in enumerate(_other_chips(x, y)):
            cps.append(_remote(blk, blk, ss, rs, 3 * k + j, (px, py, c)))
    return cps


def _build_ag_fwd(refs, ss, rs):
    x, y, c = _place()
    cps = []
    for k, ref in enumerate(refs):
        for j, (px, py) in enumerate(_other_chips(x, y)):
            blk = ref.at[4 * px + 2 * py + c]
            cps.append(_remote(blk, blk, ss, rs, 3 * k + j, (x, y, 1 - c)))
    return cps


def _build_rs_swap(refs, ss, rs):
    x, y, c = _place()
    n = len(refs) // 2
    return [_remote(refs[k].at[:, pl.ds(1 - c, 1)], refs[n + k], ss, rs, k, (x, y, 1 - c)) for k in range(n)]


def _build_rs_ici(refs, ss, rs):
    x, y, c = _place()
    n = len(refs) // 2
    me = 2 * x + y
    cps = []
    for k in range(n):
        for j, (px, py) in enumerate(_other_chips(x, y)):
            cps.append(_remote(refs[k].at[2 * px + py], refs[n + k].at[me], ss, rs, 3 * k + j, (px, py, c)))
    return cps


def _build_rs_share(refs, ss, rs):
    x, y, c = _place()
    return [_remote(ref.at[c], ref.at[c], ss, rs, k, (x, y, 1 - c)) for k, ref in enumerate(refs)]


def _allreduce_small(p, deps=()):
    R, C = p.shape
    nd = len(deps)

    def body(p_ref, *rest):
        gath_ref, sum_ref, send_sems, recv_sems, local_sem = rest[nd:]
        x, y, c = _place()
        me, sibling = (x, y, c), (x, y, 1 - c)
        chips = [(1 - x, y), (x, 1 - y), (1 - x, 1 - y)]

        def blk(px, py, pc):
            return gath_ref.at[4 * px + 2 * py + pc]

        def copy(k, block, to, src=None):
            return pltpu.make_async_remote_copy(
                src_ref=blk(*block) if src is None else src, dst_ref=blk(*block),
                send_sem=send_sems.at[k], recv_sem=recv_sems.at[k], device_id=to, device_id_type=MESH)

        mine = pltpu.make_async_copy(p_ref, blk(*me), local_sem)
        mine.start()
        first = [copy(0, me, sibling, src=p_ref)]
        first += [copy(1 + j, me, (*chip, c), src=p_ref) for j, chip in enumerate(chips)]
        for cp in first:
            cp.start()
        passed = [copy(4 + j, (*chip, c), sibling) for j, chip in enumerate(chips)]
        for j, chip in enumerate(chips):
            copy(1 + j, (*chip, c), me).wait_recv()
            passed[j].start()
        copy(0, sibling, me).wait_recv()
        for j, chip in enumerate(chips):
            copy(4 + j, (*chip, 1 - c), me).wait_recv()
        for cp in first + passed:
            cp.wait_send()
        mine.wait()
        s = gath_ref[0]
        for d in range(1, N_DEV):
            s = s + gath_ref[d]
        sum_ref[...] = s

    vm = pl.BlockSpec(memory_space=pltpu.VMEM)
    return pl.pallas_call(
        body, name="allreduce_small",
        in_specs=[vm] + [ANY] * nd, out_specs=[vm, vm],
        out_shape=[jax.ShapeDtypeStruct((N_DEV, R, C), F32), jax.ShapeDtypeStruct((R, C), F32)],
        scratch_shapes=[pltpu.SemaphoreType.DMA((7,)), pltpu.SemaphoreType.DMA((7,)), pltpu.SemaphoreType.DMA],
    )(p, *deps)[1]


def _rs_add_pair(p, r0, c_arr, name):
    _, _, hr, cols = p.shape
    tr = _tile(hr, 256, 8)

    def body(c_ref, p_ref, r_ref, q_ref):
        q_ref[...] = (p_ref[0] + r_ref[0]).astype(BF16)

    grid_spec = pltpu.PrefetchScalarGridSpec(
        num_scalar_prefetch=1, grid=(N_CHIPS, hr // tr),
        in_specs=[pl.BlockSpec((1, 1, tr, cols), lambda j, i, c_ref: (j, c_ref[0], i, 0)),
                  pl.BlockSpec((1, 1, tr, cols), lambda j, i, c_ref: (j, 0, i, 0))],
        out_specs=pl.BlockSpec((1, tr, cols), lambda j, i, c_ref: (j, i, 0)))
    return pl.pallas_call(
        body, name=name, grid_spec=grid_spec,
        out_shape=jax.ShapeDtypeStruct((N_CHIPS, hr, cols), BF16),
        compiler_params=_cparams(("parallel", "parallel")),
    )(c_arr, p, r0)


def _rs_add_chips(r1, q, place_arr, name):
    _, hr, cols = r1.shape
    tr = _tile(hr, 256, 8)

    def body(place_ref, r_ref, q_ref, o_ref):
        chip = place_ref[0]
        s = None
        for j in range(N_CHIPS):
            t = jnp.where(chip == j, q_ref[j], r_ref[j]).astype(F32)
            s = t if s is None else s + t
        o_ref[...] = s

    blk = pl.BlockSpec((N_CHIPS, tr, cols), lambda i, place_ref: (0, i, 0))
    grid_spec = pltpu.PrefetchScalarGridSpec(
        num_scalar_prefetch=1, grid=(hr // tr,), in_specs=[blk, blk],
        out_specs=pl.BlockSpec((None, tr, cols), lambda i, place_ref: (place_ref[1], i, 0)))
    return pl.pallas_call(
        body, name=name, grid_spec=grid_spec,
        out_shape=jax.ShapeDtypeStruct((2, hr, cols), F32),
        compiler_params=_cparams(("parallel",)),
    )(place_arr, r1, q)


def _pad_rows(a, rows):
    return jnp.pad(a, ((0, rows - a.shape[0]), (0, 0)))


def _pad_cols(a, cols):
    return jnp.pad(a, ((0, 0), (0, cols - a.shape[1])))


def _heads_to_rows(v):
    v = v.reshape(N_GROUPS, HEADS_PER_GROUP, 1)
    v = jnp.pad(v, ((0, 0), (0, 8 - HEADS_PER_GROUP), (0, 0)))
    return jnp.broadcast_to(v, (N_GROUPS, 8, CHUNK))


def _rows_to_heads(a):
    return jnp.sum(a[:, :HEADS_PER_GROUP, :], axis=-1).reshape(N_HEADS)


def _to_kernel_rows(a):
    C = a.shape[1]
    x0, b0, c0, s0 = D_SSM, 2 * D_SSM, 2 * D_SSM + 1024, D_SSM + D_XBC + N_HEADS
    xbc = jnp.concatenate([a[x0:b0].reshape(N_GROUPS, GW, C), a[b0:c0].reshape(N_GROUPS, N_STATE, C),
                           a[c0:c0 + 1024].reshape(N_GROUPS, N_STATE, C)], axis=1).reshape(D_XBC, C)
    sc = jnp.concatenate([a[s0 + k * D_MODEL:s0 + (k + 1) * D_MODEL].reshape(D_MODEL // SCB, SCB, C)
                          for k in range(3)], axis=1).reshape(3 * D_MODEL, C)
    return jnp.concatenate([a[:D_SSM], xbc, sc], axis=0)


HR_IN = 1568


def _shard_row_plan():
    segs = [(0, 0, 0, D_SSM)]
    for g in range(N_GROUPS):
        k0 = D_SSM + g * GXBC
        segs += [(0, k0, D_SSM + g * GW, GW), (0, k0 + GW, 2 * D_SSM + g * N_STATE, N_STATE),
                 (0, k0 + GW + N_STATE, 2 * D_SSM + 1024 + g * N_STATE, N_STATE)]
    segs.append((1, 0, D_SSM + D_XBC, N_HEADS))
    for j in range(D_MODEL // SCB):
        for k in range(3):
            segs.append((0, D_SSM + D_XBC + j * SC3 + k * SCB, D_SSM + D_XBC + N_HEADS + k * D_MODEL + j * SCB, SCB))
    cs = D_IN // N_CHIPS
    plan = []
    for src, s, o, n in segs:
        while n > 0:
            chip, loc = divmod(o, cs)
            half, row = divmod(loc, HR_IN)
            m = min(n, cs - loc, HR_IN - row)
            plan.append((src, s, chip, half, row, m))
            s, o, n = s + m, o + m, n - m
    return plan


SCATTER_ROWS = 512
SCATTER_SLOTS = 4


def _scatter_rows_to_shards(k_main, k_dt):
    C = k_main.shape[1]
    pieces = []
    for src, s, chip, half, row, n in _shard_row_plan():
        for o in range(0, n, SCATTER_ROWS):
            pieces.append((src, s + o, chip, half, row + o, min(SCATTER_ROWS, n - o)))
    S, lag, N = SCATTER_SLOTS, SCATTER_SLOTS // 2, len(pieces)

    def body(m_ref, d_ref, o_ref, buf, in_sems, out_sems):
        def cin(i):
            src, s, _, _, _, n = pieces[i]
            return pltpu.make_async_copy((d_ref if src else m_ref).at[pl.ds(s, n)],
                                         buf.at[i % S, pl.ds(0, n)], in_sems.at[i % S])

        def cout(i):
            _, _, chip, half, row, n = pieces[i]
            return pltpu.make_async_copy(buf.at[i % S, pl.ds(0, n)],
                                         o_ref.at[chip, half, pl.ds(row, n)], out_sems.at[i % S])

        for i in range(N + lag):
            if i < N:
                if i >= S:
                    cout(i - S).wait()
                cin(i).start()
            j = i - lag
            if 0 <= j < N:
                cin(j).wait()
                cout(j).start()
        for j in range(max(0, N - S), N):
            cout(j).wait()

    return pl.pallas_call(
        body, name="scatter_dw_in_rows", in_specs=[ANY, ANY], out_specs=ANY,
        out_shape=jax.ShapeDtypeStruct((N_CHIPS, 2, HR_IN, C), k_main.dtype),
        scratch_shapes=[pltpu.VMEM((S, SCATTER_ROWS, C), k_main.dtype),
                        pltpu.SemaphoreType.DMA((S,)), pltpu.SemaphoreType.DMA((S,))],
        compiler_params=_cparams(),
    )(k_main, k_dt)


def _to_kernel_xbc(a):
    R = a.shape[0]
    return jnp.concatenate([a[:, :D_SSM].reshape(R, N_GROUPS, GW), a[:, D_SSM:D_SSM + 1024].reshape(R, N_GROUPS, N_STATE),
                            a[:, D_SSM + 1024:].reshape(R, N_GROUPS, N_STATE)], axis=2).reshape(R, D_XBC)


def _from_kernel_xbc(a):
    R = a.shape[0]
    g = a.reshape(R, N_GROUPS, GXBC)
    return jnp.concatenate([g[:, :, :GW].reshape(R, D_SSM), g[:, :, GW:GW + N_STATE].reshape(R, 1024),
                            g[:, :, GW + N_STATE:].reshape(R, 1024)], axis=1)


def kernel(x, norm_mix_g, w_in, ssm_conv_w, ssm_conv_b, ssm_dt_bias, ssm_A_log, ssm_D, ssm_norm_g, sc_conv_w, w_out, norm_ffn_g, w_gate, w_up, w_down, norm_final_g, loss_target, m_norm_mix_g, m_w_in, m_ssm_conv_w, m_ssm_conv_b, m_ssm_dt_bias, m_ssm_A_log, m_ssm_D, m_ssm_norm_g, m_sc_conv_w, m_w_out, m_norm_ffn_g, m_w_gate, m_w_up, m_w_down, m_norm_final_g, v_norm_mix_g, v_w_in, v_ssm_conv_w, v_ssm_conv_b, v_ssm_dt_bias, v_ssm_A_log, v_ssm_D, v_ssm_norm_g, v_sc_conv_w, v_w_out, v_norm_ffn_g, v_w_gate, v_w_up, v_w_down, v_norm_final_g):
    T = x.shape[1]
    xt = x[0]
    tgt = loss_target[0]
    cx, cy, cc = lax.axis_index("x"), lax.axis_index("y"), lax.axis_index("c")
    chip = 2 * cx + cy
    c_arr = jnp.reshape(cc, (1,)).astype(jnp.int32)
    chip_arr = jnp.reshape(chip, (1,)).astype(jnp.int32)
    place_arr = jnp.stack([chip, cc]).astype(jnp.int32)

    big = [w_in[0].T, w_out[0], w_gate[0], w_up[0], w_down[0]]
    names = ["w_in", "w_out", "w_gate", "w_up", "w_down"]
    gb_in = _cast_into_gather(big[0], chip_arr, "cast_w_in", split_cols=True)
    cs_in, cs_conv = D_IN // N_CHIPS, D_XBC // N_CHIPS
    cw = jnp.stack([_pad_rows(ssm_conv_w[0], 8), _pad_cols(_pad_rows(sc_conv_w[0], 8), cs_conv)])
    cw_buf = lax.dynamic_update_slice(jnp.zeros((N_DEV, 8, cs_conv), F32), cw, (2 * chip, 0, 0))
    f_ss, f_rs, f_arr, f_tok = _split_start("ag_in_first_start", [gb_in, cw_buf], _build_ag_first, 4)
    gbufs = [None] + [_cast_into_gather(w, chip_arr, "cast_" + nm, deps=[f_tok]) for w, nm in zip(big[1:], names[1:])]
    n1 = _rmsnorm_fwd(xt, _tie(norm_mix_g, f_tok, "tie_ag_first"), "rmsnorm_mix")
    f_arr = _split_wait("ag_in_first_wait", f_ss, f_rs, f_arr, _build_ag_first, after=gbufs[1:] + [n1])
    g_in, cw_all = _allgather_inplace(f_arr, [("rows", (cs_in // 32) * 16), ("cols", cs_conv // 2)], first_done=True)
    cw_all = cw_all.reshape(N_CHIPS, 2, 8, cs_conv)
    ssm_w8 = _to_kernel_xbc(cw_all[:, 0].transpose(1, 0, 2).reshape(8, D_XBC))
    sc_w8 = cw_all[:, 1, :, :D_MODEL // N_CHIPS].transpose(1, 0, 2).reshape(8, D_MODEL)
    ssm_bk = _to_kernel_xbc(ssm_conv_b)
    wt = g_in.reshape(N_CHIPS, 2, cs_in, D_MODEL // 2).transpose(0, 2, 1, 3).reshape(D_IN, D_MODEL)
    wt_main = _to_kernel_rows(wt)
    wt_dt = _pad_rows(wt[D_SSM + D_XBC:D_SSM + D_XBC + N_HEADS], DT_PAD)
    ag_ss, ag_rs, ag_bufs, ag_tok = _split_start("ag_ici_start", gbufs[1:], _build_ag_ici, 12, after=[g_in, cw_all])

    bias_rows = _heads_to_rows(ssm_dt_bias[0])
    alog_rows = _heads_to_rows(ssm_A_log[0])
    drep = jnp.repeat(ssm_D[0], HEADDIM).reshape(1, D_SSM)

    (proj,) = _matmul([(n1, wt_main)], tb=True, out_dtypes=[F32], name="mm_proj", deps=[ag_tok])
    (dt_raw,) = _matmul([(n1, wt_dt)], tb=True, out_dtypes=[F32], name="mm_proj_dt")
    xbc = _ssm_conv_fwd(proj, ssm_w8, ssm_bk)
    dtr = jnp.pad(dt_raw[:, :N_HEADS].T.reshape(N_GROUPS, HEADS_PER_GROUP, T), ((0, 0), (0, 4), (0, 0)))
    y_ssd, hs = _ssd_fwd(xbc, dtr, bias_rows, alog_rows, drep)
    ag_bufs = _split_wait("ag_ici_wait", ag_ss, ag_rs, ag_bufs, _build_ag_ici, after=[y_ssd])
    fw_ss, fw_rs, fw_bufs, fw_tok = _split_start("ag_fwd_start", ag_bufs, _build_ag_fwd, 12)
    y_mix = _shortconv_fwd(proj, sc_w8, _gated_norm_fwd(y_ssd, proj, _tie(ssm_norm_g, fw_tok, "tie_ag_fwd")))
    gath = _split_wait("ag_fwd_wait", fw_ss, fw_rs, fw_bufs, _build_ag_fwd, after=[y_mix])
    w_out_f = gath[0].reshape(2 * D_MODEL, D_MODEL)
    w_gate3 = gath[1].reshape(N_CHIPS, D_MODEL, D_FF // N_CHIPS)
    w_up3 = gath[2].reshape(N_CHIPS, D_MODEL, D_FF // N_CHIPS)
    w_down_f = gath[3].reshape(D_FF, D_MODEL)
    (h1,) = _matmul([(y_mix, w_out_f)], out_dtypes=[F32], name="mm_out", extras=[xt],
                    epilogue=lambda acc, res: (acc + res,))
    n2 = _rmsnorm_fwd(h1, norm_ffn_g, "rmsnorm_ffn")
    g_act, u_act, a_act = _ffn_fwd(n2, w_gate3, w_up3)
    (h2,) = _matmul([(a_act, w_down_f)], out_dtypes=[F32], name="mm_down", extras=[h1], tm=512,
                    epilogue=lambda acc, res: (acc + res,))

    dh2, dh2b, dg_final, loss_part = _loss_and_final_bwd(h2, tgt, norm_final_g.reshape(1, D_MODEL))
    dg_act, du_act = _matmul([(dh2b, w_down_f)], tb=True, out_dtypes=[BF16, BF16], name="mm_down_bwd",
                             tn=512, extras=[g_act, u_act], epilogue=_swiglu_bwd, nsub=2)
    (dw_down,) = _matmul([(a_act, dh2b)], ta=True, out_dtypes=[F32], name="mm_dw_down", tm=1408, tn=512)
    (dn2,) = _matmul([(dg_act, w_gate3), (du_act, w_up3)], tb=True, b3d=True, out_dtypes=[BF16],
                     name="mm_ffn_in_bwd")
    (dw_gate,) = _matmul([(n2, dg_act)], ta=True, out_dtypes=[F32], name="mm_dw_gate", tm=512, tn=1408,
                         col_shards=True)
    (dw_up,) = _matmul([(n2, du_act)], ta=True, out_dtypes=[F32], name="mm_dw_up", tm=512, tn=1408,
                       col_shards=True)
    dh1, dh1b, dg_ffn = _rmsnorm_bwd(dn2, h1, norm_ffn_g, dh2, "rmsnorm_ffn_bwd")
    (dw_out,) = _matmul([(y_mix, dh1b)], ta=True, out_dtypes=[F32], name="mm_dw_out")

    def halves(g):
        return g.reshape(N_CHIPS, 2, g.shape[1] // 2, g.shape[2])

    def landing(shape, dtype):
        return lax.empty(shape, dtype)

    names1 = names[1:]
    ps1 = [halves(dw_out.reshape(N_CHIPS, -1, D_MODEL)), halves(dw_gate), halves(dw_up),
           halves(dw_down.reshape(N_CHIPS, -1, D_MODEL))]
    r0_1 = [landing((N_CHIPS, 1) + p.shape[2:], F32) for p in ps1]
    sw_ss, sw_rs, sw_arr, sw_tok = _split_start("rs1_swap_start", ps1 + r0_1, _build_rs_swap, 4)
    (dmix,) = _matmul([(dh1b, w_out_f)], tb=True, out_dtypes=[BF16], name="mm_out_bwd", deps=[sw_tok])
    dproj, dw_sc = _shortconv_bwd(dmix, proj, sc_w8)
    dy_ssd, dproj, dg_ssmnorm = _gated_norm_bwd(dmix, y_ssd, proj, ssm_norm_g, dproj)
    sw_arr = _split_wait("rs1_swap_wait", sw_ss, sw_rs, sw_arr, _build_rs_swap, after=[dy_ssd])
    qs1 = [_rs_add_pair(p, r, c_arr, "rs_add_pair_" + nm) for p, r, nm in zip(sw_arr[:4], sw_arr[4:], names1)]
    r1_1 = [landing(q.shape, BF16) for q in qs1]
    ic_ss, ic_rs, ic_arr, ic_tok = _split_start("rs1_ici_start", qs1 + r1_1, _build_rs_ici, 12)
    dxbc_act, ddtr, dbias_acc, dalog_acc, dD_acc = _ssd_bwd(
        xbc, dtr, bias_rows, alog_rows, _tie(drep, ic_tok, "tie_rs1_ici"), dy_ssd, hs)
    dproj, dw_ssmconv, db_ssmconv = _ssm_conv_bwd(dxbc_act, proj, ssm_w8, ssm_bk, dproj)
    dw_ssmconv, db_ssmconv = _from_kernel_xbc(dw_ssmconv), _from_kernel_xbc(db_ssmconv)
    ic_arr = _split_wait("rs1_ici_wait", ic_ss, ic_rs, ic_arr, _build_rs_ici, after=[dproj])
    g1 = [_rs_add_chips(r, q, place_arr, "rs_add_chips_" + nm) for q, r, nm in zip(ic_arr[:4], ic_arr[4:], names1)]
    sh_ss, sh_rs, sh_arr, sh_tok = _split_start("rs1_share_start", g1, _build_rs_share, 4)

    ddt_raw = _pad_cols(ddtr[:, :HEADS_PER_GROUP, :].reshape(N_HEADS, T).T, DT_PAD).astype(BF16)
    (dwt_main,) = _matmul([(dproj, n1)], ta=True, out_dtypes=[F32], name="mm_dw_main", deps=[sh_tok])
    (dwt_dt,) = _matmul([(ddt_raw, n1)], ta=True, out_dtypes=[F32], name="mm_dw_dt")
    p_in = _scatter_rows_to_shards(dwt_main, dwt_dt)
    s2_ss, s2_rs, s2_arr, s2_tok = _split_start(
        "rs2_swap_start", [p_in, landing((N_CHIPS, 1) + p_in.shape[2:], F32)], _build_rs_swap, 1)
    mt = T // _tile(T, 1024)
    mt_a = max(mt // 4, 1)
    (dn1a,) = _matmul([(dproj, wt_main)], out_dtypes=[F32], name="mm_proj_bwd_a", deps=[s2_tok],
                      m_tiles=(0, mt_a))
    g1 = _split_wait("rs1_share_wait", sh_ss, sh_rs, sh_arr, _build_rs_share, after=[dn1a])
    s2_arr = _split_wait("rs2_swap_wait", s2_ss, s2_rs, s2_arr, _build_rs_swap, after=[dn1a])
    q_in = _rs_add_pair(s2_arr[0], s2_arr[1], c_arr, "rs_add_pair_w_in")
    i2_ss, i2_rs, i2_arr, i2_tok = _split_start(
        "rs2_ici_start", [q_in, landing(q_in.shape, BF16)], _build_rs_ici, 3)
    if mt > mt_a:
        (dn1a,) = _matmul([(dproj, wt_main)], out_dtypes=[F32], name="mm_proj_bwd_b", deps=[i2_tok],
                          m_tiles=(mt_a, mt - mt_a), out_buf=dn1a)
    (dn1,) = _matmul([(ddt_raw, wt_dt)], out_dtypes=[BF16], name="mm_proj_dt_bwd", extras=[dn1a],
                     epilogue=lambda acc, res: (acc + res,), deps=[i2_tok])
    dx, _, dg_mix = _rmsnorm_bwd(dn1, xt, norm_mix_g, dh1, "rmsnorm_mix_bwd")

    big_m = [m_w_in[0].T, m_w_out[0], m_w_gate[0], m_w_up[0], m_w_down[0]]
    big_v = [v_w_in[0].T, v_w_out[0], v_w_gate[0], v_w_up[0], v_w_down[0]]
    big_grads = [None] + [g.reshape(w.shape) for g, w in zip(g1, big[1:])]
    big_out = {}
    for k in range(1, 5):
        big_out[names[k]] = _adamw(big[k], big_grads[k], big_m[k], big_v[k], "adamw_" + names[k], deps=[i2_tok])
    i2_arr = _split_wait("rs2_ici_wait", i2_ss, i2_rs, i2_arr, _build_rs_ici, after=[big_out[names[4]][0], dx])
    g_in_red = _rs_add_chips(i2_arr[1], i2_arr[0], place_arr, "rs_add_chips_w_in")
    s3_ss, s3_rs, s3_arr, s3_tok = _split_start("rs2_share_start", [g_in_red], _build_rs_share, 1)

    dD = jnp.sum(dD_acc.reshape(N_HEADS, HEADDIM), axis=-1)
    heads_row = jnp.concatenate([_rows_to_heads(dbias_acc), _rows_to_heads(dalog_acc), dD,
                                 loss_part.reshape(1)]).reshape(1, -1)
    small = jnp.concatenate([
        dw_ssmconv,
        _pad_cols(dw_sc, D_XBC),
        db_ssmconv,
        jnp.concatenate([dg_mix, dg_ssmnorm], axis=1),
        jnp.concatenate([dg_ffn, dg_final], axis=1),
        _pad_cols(heads_row, D_XBC),
        jnp.zeros((4, D_XBC), F32),
    ], axis=0)
    tot = _allreduce_small(small, deps=[s3_tok])
    loss = tot[19, 3 * N_HEADS]

    cs_ssm, cs_sc = D_XBC // N_CHIPS, D_MODEL // N_CHIPS
    g_ssm_conv = lax.dynamic_slice(tot[0:K_SSM], (0, chip * cs_ssm), (K_SSM, cs_ssm))
    g_sc_conv = lax.dynamic_slice(tot[8:8 + K_SC, :D_MODEL], (0, chip * cs_sc), (K_SC, cs_sc))
    small_grads = {
        "norm_mix_g": tot[17:18, :D_MODEL], "ssm_conv_w": g_ssm_conv, "ssm_conv_b": tot[16:17],
        "ssm_dt_bias": tot[19:20, 0:N_HEADS], "ssm_A_log": tot[19:20, N_HEADS:2 * N_HEADS],
        "ssm_D": tot[19:20, 2 * N_HEADS:3 * N_HEADS], "ssm_norm_g": tot[17:18, D_MODEL:],
        "sc_conv_w": g_sc_conv, "norm_ffn_g": tot[18:19, :D_MODEL], "norm_final_g": tot[18:19, D_MODEL:],
    }
    small_w = {"norm_mix_g": (norm_mix_g, m_norm_mix_g, v_norm_mix_g),
               "ssm_conv_w": (ssm_conv_w[0], m_ssm_conv_w[0], v_ssm_conv_w[0]),
               "ssm_conv_b": (ssm_conv_b, m_ssm_conv_b, v_ssm_conv_b),
               "ssm_dt_bias": (ssm_dt_bias, m_ssm_dt_bias, v_ssm_dt_bias),
               "ssm_A_log": (ssm_A_log, m_ssm_A_log, v_ssm_A_log),
               "ssm_D": (ssm_D, m_ssm_D, v_ssm_D),
               "ssm_norm_g": (ssm_norm_g, m_ssm_norm_g, v_ssm_norm_g),
               "sc_conv_w": (sc_conv_w[0], m_sc_conv_w[0], v_sc_conv_w[0]),
               "norm_ffn_g": (norm_ffn_g, m_norm_ffn_g, v_norm_ffn_g),
               "norm_final_g": (norm_final_g.reshape(1, -1), m_norm_final_g.reshape(1, -1),
                                v_norm_final_g.reshape(1, -1))}
    PW = 1024
    order = list(small_w)

    def pack(arrs):
        rows = []
        for a in arrs:
            flat = a.reshape(-1)
            n = -(-flat.shape[0] // PW) * PW
            rows.append(jnp.pad(flat, (0, n - flat.shape[0])).reshape(-1, PW))
        slab = jnp.concatenate(rows, axis=0)
        return _pad_rows(slab, -(-slab.shape[0] // 8) * 8)

    wp = pack([small_w[k][0] for k in order])
    mp = pack([small_w[k][1] for k in order])
    vp = pack([small_w[k][2] for k in order])
    gp = pack([small_grads[k] for k in order])
    sd, sm, sv = _adamw(wp, gp, mp, vp, "adamw_small")

    def unpack(slab):
        out, row = {}, 0
        for k in order:
            shape = small_w[k][0].shape
            size = 1
            for s in shape:
                size *= s
            nr = -(-size // PW)
            out[k] = slab[row:row + nr].reshape(-1)[:size].reshape(shape)
            row += nr
        return out

    s_delta, s_m, s_v = unpack(sd), unpack(sm), unpack(sv)

    (g_in_full,) = _split_wait("rs2_share_wait", s3_ss, s3_rs, s3_arr, _build_rs_share, after=[sd])
    d_t, m_t, v_t, g_t = _adamw(big[0], g_in_full.reshape(2 * HR_IN, D_MODEL), big_m[0], big_v[0],
                                "adamw_" + names[0], emit_g=True)
    big_grads[0] = g_t.T
    big_out[names[0]] = (d_t.T, m_t.T, v_t.T)
    big_g = dict(zip(names, big_grads))

    weight_order = ["norm_mix_g", "w_in", "ssm_conv_w", "ssm_conv_b", "ssm_dt_bias", "ssm_A_log", "ssm_D",
                    "ssm_norm_g", "sc_conv_w", "w_out", "norm_ffn_g", "w_gate", "w_up", "w_down", "norm_final_g"]
    lead = {"ssm_conv_w", "sc_conv_w", "w_in", "w_out", "w_gate", "w_up", "w_down"}

    def shaped(nm, a):
        if nm == "norm_final_g":
            return a.reshape(D_MODEL)
        return a[None] if nm in lead else a

    grads, deltas, new_m, new_v = [], [], [], []
    for nm in weight_order:
        if nm in big_out:
            g, (d, m, v) = big_g[nm], big_out[nm]
        else:
            g, d, m, v = small_grads[nm], s_delta[nm], s_m[nm], s_v[nm]
        grads.append(shaped(nm, g))
        deltas.append(shaped(nm, d))
        new_m.append(shaped(nm, m))
        new_v.append(shaped(nm, v))
    return (loss, dx[None], *grads, *deltas, *new_m, *new_v)


def _swiglu_bwd(da, dg_factor, du_factor):
    return da * dg_factor.astype(F32), da * du_factor.astype(F32)


def _ffn_fwd(n2, w_gate, w_up):
    T, K = n2.shape
    tn = w_gate.shape[2]
    N = N_CHIPS * tn
    tm = _tile(T, 512)
    sub = _tile(tm, 256)

    def body(a_ref, wg_ref, wu_ref, g_ref, u_ref, act_ref):
        for s in range(tm // sub):
            rows = pl.ds(s * sub, sub)
            a = a_ref[rows, :]
            g = jnp.dot(a, wg_ref[...], preferred_element_type=F32)
            u = jnp.dot(a, wu_ref[...], preferred_element_type=F32)
            sig = _sigmoid(g)
            sg = g * sig
            g_ref[rows, :] = (u * (sig * (1.0 + g - sg))).astype(BF16)
            u_ref[rows, :] = sg.astype(BF16)
            act_ref[rows, :] = (sg * u).astype(BF16)

    a_spec = pl.BlockSpec((tm, K), lambda j, i: (i, 0))
    b_spec = pl.BlockSpec((None, K, tn), lambda j, i: (j, 0, 0))
    o_spec = pl.BlockSpec((tm, tn), lambda j, i: (i, j))
    return pl.pallas_call(
        body, name="ffn_fwd", grid=(N // tn, T // tm),
        in_specs=[a_spec, b_spec, b_spec], out_specs=[o_spec] * 3,
        out_shape=[jax.ShapeDtypeStruct((T, N), BF16)] * 3,
        compiler_params=_cparams(("parallel", "parallel")),
    )(n2, w_gate, w_up)
```

```python
import functools

import jax
import jax.numpy as jnp
from jax import lax
from jax.experimental import pallas as pl
from jax.experimental.pallas import tpu as pltpu

F32 = jnp.float32
BF16 = jnp.bfloat16
MESH = pl.DeviceIdType.MESH

D_MODEL = 2048
D_SSM = 2048
HEADDIM = 64
N_HEADS = 32
N_GROUPS = 8
HEADS_PER_GROUP = 4
N_STATE = 128
CHUNK = 128
K_SSM = 4
K_SC = 3
D_XBC = 4096
D_FF = 5632
D_IN = 12320
D_MAIN = 12288
OFF_XBC, OFF_CB, OFF_CC, OFF_CX = 2048, 6144, 8192, 10240
DT_PAD = 128
EPS = 1e-5
N_CHIPS = 4
N_DEV = 8

ADAM_LR = 0.001
ADAM_B1 = 0.9
ADAM_B2 = 0.999
ADAM_EPS = 1e-08
ADAM_WD = 0.01
ADAM_STEP = 10

V7X_VMEM_BYTES = 64 * 1024 * 1024
VMEM_LIMIT = V7X_VMEM_BYTES - 8 * 1024 * 1024


def _cparams(sem=None):
    if sem is None:
        return pltpu.CompilerParams(vmem_limit_bytes=VMEM_LIMIT)
    return pltpu.CompilerParams(dimension_semantics=sem, vmem_limit_bytes=VMEM_LIMIT)


def _tile(dim, pref, unit=128):
    best = None
    t = unit
    while t <= min(dim, pref):
        if dim % t == 0:
            best = t
        t += unit
    return best if best is not None else dim


def _sigmoid(x):
    return 1.0 / (1.0 + jnp.exp(-x))


def _silu(x):
    return x * _sigmoid(x)


def _dsilu(x):
    s = _sigmoid(x)
    return s * (1.0 + x * (1.0 - s))


def _softplus(x):
    return jnp.maximum(x, 0.0) + jnp.log(1.0 + jnp.exp(-jnp.abs(x)))


MATMUL_VMEM_BUDGET = 44 * 1024 * 1024


def _matmul(pairs, *, ta=False, tb=False, out_dtypes, name, tm=1024, tn=1024, tk=None, extras=(), epilogue=None,
            deps=(), col_shards=False, nsub=1, b3d=False, m_tiles=None, out_buf=None):
    a0, b0 = pairs[0]
    M, K = (a0.shape[1], a0.shape[0]) if ta else a0.shape
    if b3d:
        N = b0.shape[1] if tb else b0.shape[0] * b0.shape[2]
        tk, tn = (b0.shape[2], tn) if tb else (tk, b0.shape[2])
    else:
        N = b0.shape[0] if tb else b0.shape[1]
    tm, tn = _tile(M, tm, 8 if M % 128 else 128), _tile(N, tn)
    npair, nex, ndep, nout = len(pairs), len(extras), len(deps), len(out_dtypes)
    if tk is None:
        fixed = 2 * tm * tn * (sum(jnp.dtype(d).itemsize for d in out_dtypes) + sum(e.dtype.itemsize for e in extras))
        tk = K
        while tk > 128 and (K % tk or tk % 128 or
                            fixed + 2 * npair * 2 * tk * (tm + tn) + (tm * tn * 4 if tk < K else 0) > MATMUL_VMEM_BUDGET):
            tk -= 128
    else:
        tk = _tile(K, tk)
    nk = K // tk
    if nk > 1 or tm % nsub or (tm // nsub) % 128:
        nsub = 1
    sub = tm // nsub
    dims = (((0 if ta else 1,), (1 if tb else 0,)), ((), ()))
    i0, mi = m_tiles if m_tiles is not None else (0, M // tm)
    nbuf = 0 if out_buf is None else 1

    def body(*refs):
        a_refs = refs[0:2 * npair:2]
        b_refs = refs[1:2 * npair:2]
        ex_refs = refs[2 * npair:2 * npair + nex]
        o_refs = refs[2 * npair + nex + ndep + nbuf:2 * npair + nex + ndep + nbuf + nout]

        def dots(rows):
            s = None
            for a_ref, b_ref in zip(a_refs, b_refs):
                a = a_ref[...] if rows is None else (a_ref[:, rows] if ta else a_ref[rows, :])
                d = lax.dot_general(a, b_ref[...], dims, preferred_element_type=F32)
                s = d if s is None else s + d
            return s

        def finish(r, rows):
            ex = [e[...] if rows is None else e[rows, :] for e in ex_refs]
            outs = (r,) if epilogue is None else epilogue(r, *ex)
            for o_ref, o in zip(o_refs, outs):
                if rows is None:
                    o_ref[...] = o.astype(o_ref.dtype)
                else:
                    o_ref[rows, :] = o.astype(o_ref.dtype)

        if nk == 1:
            for s in range(nsub):
                rows = None if nsub == 1 else pl.ds(s * sub, sub)
                finish(dots(rows), rows)
            return

        acc = refs[-1]
        k = pl.program_id(2)

        @pl.when(k == 0)
        def _():
            acc[...] = dots(None)

        @pl.when(jnp.logical_and(k > 0, k < nk - 1))
        def _():
            acc[...] += dots(None)

        @pl.when(k == nk - 1)
        def _():
            finish(acc[...] + dots(None), None)

    a_spec = (pl.BlockSpec((tk, tm), lambda i, j, k: (k, i + i0)) if ta
              else pl.BlockSpec((tm, tk), lambda i, j, k: (i + i0, k)))
    if b3d:
        b_spec = (pl.BlockSpec((None, tn, tk), lambda i, j, k: (k, j, 0)) if tb
                  else pl.BlockSpec((None, tk, tn), lambda i, j, k: (j, k, 0)))
    else:
        b_spec = (pl.BlockSpec((tn, tk), lambda i, j, k: (j, k)) if tb
                  else pl.BlockSpec((tk, tn), lambda i, j, k: (k, j)))
    e_spec = pl.BlockSpec((tm, tn), lambda i, j, k: (i + i0, j))
    if col_shards:
        o_spec = pl.BlockSpec((None, tm, tn), lambda i, j, k: (j, i + i0, 0))
        o_shape = (N // tn, M, tn)
    else:
        o_spec, o_shape = e_spec, (M, N)
    args, in_specs = [], []
    for a, b in pairs:
        args += [a, b]
        in_specs += [a_spec, b_spec]
    args += list(extras) + list(deps) + ([] if out_buf is None else [out_buf])
    in_specs += [e_spec] * nex + [ANY] * (ndep + nbuf)
    outs = pl.pallas_call(
        body,
        name=name,
        grid=(mi, N // tn, nk),
        in_specs=in_specs,
        out_specs=[o_spec] * nout,
        out_shape=[jax.ShapeDtypeStruct(o_shape, dt) for dt in out_dtypes],
        input_output_aliases={} if out_buf is None else {len(args) - 1: 0},
        scratch_shapes=[pltpu.VMEM((tm, tn), F32)] if nk > 1 else [],
        compiler_params=_cparams(("parallel", "parallel", "arbitrary")),
    )(*args)
    return outs


def _cast_into_gather(w, chip_arr, name, split_cols=False, deps=()):
    R, C = w.shape
    hr, hc = (R, C // 2) if split_cols else (R // 2, C)
    tr = _tile(hr, 512, 8)
    nb = hr // tr

    def body(chip_ref, w_ref, *rest):
        rest[-1][...] = w_ref[...].astype(BF16)

    in_map = (lambda h, i, chip_ref: (i, h)) if split_cols else (lambda h, i, chip_ref: (h * nb + i, 0))
    grid_spec = pltpu.PrefetchScalarGridSpec(
        num_scalar_prefetch=1, grid=(2, nb),
        in_specs=[pl.BlockSpec((tr, hc), in_map)] + [ANY] * len(deps),
        out_specs=pl.BlockSpec((None, tr, hc), lambda h, i, chip_ref: (2 * chip_ref[0] + h, i, 0)))
    return pl.pallas_call(
        body, name=name, grid_spec=grid_spec,
        out_shape=jax.ShapeDtypeStruct((N_DEV, hr, hc), BF16),
        compiler_params=_cparams(("parallel", "parallel")),
    )(chip_arr, w, *deps)


def _tie(small, token, name):
    def body(s_ref, t_ref, o_ref):
        o_ref[...] = s_ref[...]

    vm = pl.BlockSpec(memory_space=pltpu.VMEM)
    return pl.pallas_call(body, name=name, in_specs=[vm, ANY], out_specs=vm,
                          out_shape=jax.ShapeDtypeStruct(small.shape, small.dtype))(small, token)


def _rmsnorm_fwd(x, g, name):
    T, D = x.shape
    tt = _tile(T, 256)

    def body(x_ref, g_ref, n_ref):
        xv = x_ref[...]
        r = lax.rsqrt(jnp.mean(xv * xv, axis=-1, keepdims=True) + EPS)
        n_ref[...] = (xv * r * g_ref[...]).astype(BF16)

    return pl.pallas_call(
        body, name=name, grid=(T // tt,),
        in_specs=[pl.BlockSpec((tt, D), lambda i: (i, 0)), pl.BlockSpec((1, D), lambda i: (0, 0))],
        out_specs=pl.BlockSpec((tt, D), lambda i: (i, 0)),
        out_shape=jax.ShapeDtypeStruct((T, D), BF16),
        compiler_params=_cparams(("parallel",)),
    )(x, g)


def _rmsnorm_bwd(dn, x, g, res, name):
    T, D = x.shape
    tt = _tile(T, 256)

    def body(dn_ref, x_ref, g_ref, res_ref, dx_ref, dxb_ref, dg_ref):
        @pl.when(pl.program_id(0) == 0)
        def _():
            dg_ref[...] = jnp.zeros_like(dg_ref)

        xv = x_ref[...]
        dy = dn_ref[...].astype(F32)
        r = lax.rsqrt(jnp.mean(xv * xv, axis=-1, keepdims=True) + EPS)
        xhat = xv * r
        dxh = dy * g_ref[...]
        dx = res_ref[...] + r * (dxh - xhat * jnp.mean(dxh * xhat, axis=-1, keepdims=True))
        dx_ref[...] = dx
        dxb_ref[...] = dx.astype(BF16)
        dg_ref[...] += jnp.sum(dy * xhat, axis=0, keepdims=True)

    tok = pl.BlockSpec((tt, D), lambda i: (i, 0))
    vec = pl.BlockSpec((1, D), lambda i: (0, 0))
    return pl.pallas_call(
        body, name=name, grid=(T // tt,),
        in_specs=[tok, tok, vec, tok],
        out_specs=[tok, tok, vec],
        out_shape=[jax.ShapeDtypeStruct((T, D), F32), jax.ShapeDtypeStruct((T, D), BF16),
                   jax.ShapeDtypeStruct((1, D), F32)],
        compiler_params=_cparams(("arbitrary",)),
    )(dn, x, g, res)


def _loss_and_final_bwd(h2, target, gf):
    T, D = h2.shape
    tt = _tile(T, 256)

    def body(h_ref, t_ref, g_ref, dh_ref, dhb_ref, dg_ref, loss_ref):
        @pl.when(pl.program_id(0) == 0)
        def _():
            dg_ref[...] = jnp.zeros_like(dg_ref)
            loss_ref[...] = jnp.zeros_like(loss_ref)

        xv = h_ref[...]
        r = lax.rsqrt(jnp.mean(xv * xv, axis=-1, keepdims=True) + EPS)
        xhat = xv * r
        err = xhat * g_ref[...] - t_ref[...]
        loss_ref[...] += 0.5 * jnp.sum(jnp.mean(err * err, axis=-1, keepdims=True), axis=0, keepdims=True)
        dy = err * (1.0 / D)
        dxh = dy * g_ref[...]
        dx = r * (dxh - xhat * jnp.mean(dxh * xhat, axis=-1, keepdims=True))
        dh_ref[...] = dx
        dhb_ref[...] = dx.astype(BF16)
        dg_ref[...] += jnp.sum(dy * xhat, axis=0, keepdims=True)

    tok = pl.BlockSpec((tt, D), lambda i: (i, 0))
    vec = pl.BlockSpec((1, D), lambda i: (0, 0))
    return pl.pallas_call(
        body, name="loss_final_bwd", grid=(T // tt,),
        in_specs=[tok, tok, vec],
        out_specs=[tok, tok, vec, pl.BlockSpec((1, 1), lambda i: (0, 0))],
        out_shape=[jax.ShapeDtypeStruct((T, D), F32), jax.ShapeDtypeStruct((T, D), BF16),
                   jax.ShapeDtypeStruct((1, D), F32), jax.ShapeDtypeStruct((1, 1), F32)],
        compiler_params=_cparams(("arbitrary",)),
    )(h2, target, gf)


def _gated_norm_fwd(y, proj, g):
    T, D = y.shape
    tt = _tile(T, 256)

    def body(y_ref, z_ref, g_ref, o_ref):
        yg = y_ref[...] * _silu(z_ref[...])
        r = lax.rsqrt(jnp.mean(yg * yg, axis=-1, keepdims=True) + EPS)
        o_ref[...] = (yg * r * g_ref[...]).astype(BF16)

    tok = pl.BlockSpec((tt, D), lambda i: (i, 0))
    return pl.pallas_call(
        body, name="gated_norm_fwd", grid=(T // tt,),
        in_specs=[tok, tok, pl.BlockSpec((1, D), lambda i: (0, 0))],
        out_specs=tok,
        out_shape=jax.ShapeDtypeStruct((T, 2 * D_MODEL), BF16),
        compiler_params=_cparams(("parallel",)),
    )(y, proj, g)


def _gated_norm_bwd(dmix, y, proj, g, dproj):
    T, D = y.shape
    tt = _tile(T, 256)

    def body(do_ref, y_ref, z_ref, g_ref, dp_ref, dy_ref, dz_ref, dg_ref):
        @pl.when(pl.program_id(0) == 0)
        def _():
            dg_ref[...] = jnp.zeros_like(dg_ref)

        yv, zv = y_ref[...], z_ref[...]
        do = do_ref[...].astype(F32)
        sz = _silu(zv)
        yg = yv * sz
        r = lax.rsqrt(jnp.mean(yg * yg, axis=-1, keepdims=True) + EPS)
        xhat = yg * r
        dxh = do * g_ref[...]
        dyg = r * (dxh - xhat * jnp.mean(dxh * xhat, axis=-1, keepdims=True))
        dy_ref[...] = dyg * sz
        dz_ref[...] = (dyg * yv * _dsilu(zv)).astype(BF16)
        dg_ref[...] += jnp.sum(do * xhat, axis=0, keepdims=True)

    tok = pl.BlockSpec((tt, D), lambda i: (i, 0))
    vec = pl.BlockSpec((1, D), lambda i: (0, 0))
    return pl.pallas_call(
        body, name="gated_norm_bwd", grid=(T // tt,),
        in_specs=[tok, tok, tok, vec, ANY],
        out_specs=[tok, tok, vec],
        out_shape=[jax.ShapeDtypeStruct((T, D), F32), jax.ShapeDtypeStruct(dproj.shape, BF16),
                   jax.ShapeDtypeStruct((1, D), F32)],
        input_output_aliases={4: 1},
        compiler_params=_cparams(("arbitrary",)),
    )(dmix, y, proj, g, dproj)


HALO = 8


def _shift_down(cur, prev8, s):
    ext = jnp.concatenate([prev8, cur], axis=0)
    return pltpu.roll(ext, s, axis=0)[HALO:]


def _shift_up(cur, next8, s):
    n = cur.shape[0]
    ext = jnp.concatenate([cur, next8], axis=0)
    return pltpu.roll(ext, n + HALO - s, axis=0)[:n]


def _conv_specs(tt, cb, col_off_blocks, nt):
    hb = tt // HALO
    cur = pl.BlockSpec((tt, cb), lambda j, i: (i, col_off_blocks + j))
    prev = pl.BlockSpec((HALO, cb), lambda j, i: (jnp.maximum(i * hb - 1, 0), col_off_blocks + j))
    nxt = pl.BlockSpec((HALO, cb), lambda j, i: (jnp.minimum((i + 1) * hb, nt * hb - 1), col_off_blocks + j))
    return cur, prev, nxt


def _taps(cur, prev8, K):
    return [_shift_down(cur, prev8, K - 1 - k) for k in range(K - 1)] + [cur]


def _conv_of_taps(taps, w):
    y = taps[-1] * w[len(taps) - 1:len(taps), :]
    for k, t in enumerate(taps[:-1]):
        y = y + t * w[k:k + 1, :]
    return y


def _causal_conv(cur, prev8, w, K):
    return _conv_of_taps(_taps(cur, prev8, K), w)


def _anticausal_conv(cur, next8, w, K):
    y = cur * w[K - 1:K, :]
    for k in range(K - 1):
        y = y + _shift_up(cur, next8, K - 1 - k) * w[k:k + 1, :]
    return y


def _ssm_conv_fwd(proj, w8, b):
    T = proj.shape[0]
    tt, cb = _tile(T, 512), 512
    nt = T // tt
    cur, prev, _ = _conv_specs(tt, cb, OFF_XBC // cb, nt)

    def body(u_ref, up_ref, w_ref, b_ref, o_ref):
        first = pl.program_id(1) == 0
        p8 = jnp.where(first, 0.0, up_ref[...])
        pre = _causal_conv(u_ref[...], p8, w_ref[...], K_SSM) + b_ref[...]
        o_ref[...] = _silu(pre)

    return pl.pallas_call(
        body, name="ssm_conv_fwd", grid=(D_XBC // cb, nt),
        in_specs=[cur, prev, pl.BlockSpec((8, cb), lambda j, i: (0, j)), pl.BlockSpec((1, cb), lambda j, i: (0, j))],
        out_specs=pl.BlockSpec((tt, cb), lambda j, i: (i, j)),
        out_shape=jax.ShapeDtypeStruct((T, D_XBC), F32),
        compiler_params=_cparams(("parallel", "parallel")),
    )(proj, proj, w8, b)


def _ssm_conv_bwd(dact, proj, w8, b, dproj):
    T = proj.shape[0]
    tt, cb = _tile(T, 512), 512
    nt = T // tt
    cur, prev, nxt = _conv_specs(tt, cb, OFF_XBC // cb, nt)
    dcur, dprev, dnxt = _conv_specs(tt, cb, 0, nt)

    def dpre_of(d, u, p8, w, bb):
        pre = _causal_conv(u, p8, w, K_SSM) + bb
        return d * _dsilu(pre)

    def body(d_ref, dn_ref, u_ref, up_ref, un_ref, w_ref, b_ref, dp_ref, dx_ref, dw_ref, db_ref):
        i = pl.program_id(1)

        @pl.when(i == 0)
        def _():
            dw_ref[...] = jnp.zeros_like(dw_ref)
            db_ref[...] = jnp.zeros_like(db_ref)

        w, bb = w_ref[...], b_ref[...]
        u = u_ref[...]
        p8 = jnp.where(i == 0, 0.0, up_ref[...])
        taps = _taps(u, p8, K_SSM)
        dpre = d_ref[...] * _dsilu(_conv_of_taps(taps, w) + bb)
        un = un_ref[...]
        dpre_n = dpre_of(dn_ref[...], un, u[tt - HALO:, :], w, bb)
        dpre_n = jnp.where(i == nt - 1, 0.0, dpre_n)
        dx_ref[...] = _anticausal_conv(dpre, dpre_n, w, K_SSM).astype(BF16)
        rows = [jnp.sum(dpre * t, axis=0, keepdims=True) for t in taps]
        rows.append(jnp.zeros((8 - K_SSM, cb), F32))
        dw_ref[...] += jnp.concatenate(rows, axis=0)
        db_ref[...] += jnp.sum(dpre, axis=0, keepdims=True)

    wspec = pl.BlockSpec((8, cb), lambda j, i: (0, j))
    bspec = pl.BlockSpec((1, cb), lambda j, i: (0, j))
    return pl.pallas_call(
        body, name="ssm_conv_bwd", grid=(D_XBC // cb, nt),
        in_specs=[dcur, dnxt, cur, prev, nxt, wspec, bspec, ANY],
        out_specs=[pl.BlockSpec((tt, cb), lambda j, i: (i, OFF_XBC // cb + j)), wspec, bspec],
        out_shape=[jax.ShapeDtypeStruct(dproj.shape, BF16), jax.ShapeDtypeStruct((8, D_XBC), F32),
                   jax.ShapeDtypeStruct((1, D_XBC), F32)],
        input_output_aliases={7: 0},
        compiler_params=_cparams(("parallel", "arbitrary")),
    )(dact, dact, proj, proj, proj, w8, b, dproj)


SCB = 512
SC3 = 3 * SCB


def _sc_specs(tt, nt):
    hb = tt // HALO
    cur = pl.BlockSpec((tt, SC3), lambda j, i: (i, OFF_CB // SC3 + j))
    prev = pl.BlockSpec((HALO, SC3), lambda j, i: (jnp.maximum(i * hb - 1, 0), OFF_CB // SC3 + j))
    nxt = pl.BlockSpec((HALO, SC3), lambda j, i: (jnp.minimum((i + 1) * hb, nt * hb - 1), OFF_CB // SC3 + j))
    return cur, prev, nxt


def _shortconv_fwd(proj, w8, ymix):
    T = proj.shape[0]
    tt = _tile(T, 512)
    nt = T // tt
    cur, prev, _ = _sc_specs(tt, nt)

    def body(p_ref, pp_ref, w_ref, y_ref, o_ref):
        p, pp = p_ref[...], pp_ref[...]
        v = p[:, SCB:2 * SCB] * p[:, 2 * SCB:]
        vp = jnp.where(pl.program_id(1) == 0, 0.0, pp[:, SCB:2 * SCB] * pp[:, 2 * SCB:])
        o_ref[...] = (p[:, :SCB] * _causal_conv(v, vp, w_ref[...], K_SC)).astype(BF16)

    return pl.pallas_call(
        body, name="shortconv_fwd", grid=(D_MODEL // SCB, nt),
        in_specs=[cur, prev, pl.BlockSpec((8, SCB), lambda j, i: (0, j)), ANY],
        out_specs=pl.BlockSpec((tt, SCB), lambda j, i: (i, D_SSM // SCB + j)),
        out_shape=jax.ShapeDtypeStruct(ymix.shape, BF16),
        input_output_aliases={3: 0},
        compiler_params=_cparams(("parallel", "parallel")),
    )(proj, proj, w8, ymix)


def _shortconv_bwd(dmix, proj, w8):
    T = proj.shape[0]
    tt = _tile(T, 512)
    nt = T // tt
    hb = tt // HALO
    cur, prev, nxt = _sc_specs(tt, nt)
    d_s = pl.BlockSpec((tt, SCB), lambda j, i: (i, D_SSM // SCB + j))
    dn_s = pl.BlockSpec((HALO, SCB), lambda j, i: (jnp.minimum((i + 1) * hb, nt * hb - 1), D_SSM // SCB + j))

    def body(d_ref, dn_ref, p_ref, pp_ref, pn_ref, w_ref, dp_ref, dw_ref):
        i = pl.program_id(1)

        @pl.when(i == 0)
        def _():
            dw_ref[...] = jnp.zeros_like(dw_ref)

        w = w_ref[...]
        p, pp = p_ref[...], pp_ref[...]
        gb, gc, u = p[:, :SCB], p[:, SCB:2 * SCB], p[:, 2 * SCB:]
        v = gc * u
        vp = jnp.where(i == 0, 0.0, pp[:, SCB:2 * SCB] * pp[:, 2 * SCB:])
        d = d_ref[...].astype(F32)
        taps = _taps(v, vp, K_SC)
        dp_ref[:, :SCB] = (d * _conv_of_taps(taps, w)).astype(BF16)
        dcv = d * gb
        dcv_n = jnp.where(i == nt - 1, 0.0, dn_ref[...].astype(F32) * pn_ref[:, :SCB])
        dv = _anticausal_conv(dcv, dcv_n, w, K_SC)
        dp_ref[:, SCB:2 * SCB] = (dv * u).astype(BF16)
        dp_ref[:, 2 * SCB:] = (dv * gc).astype(BF16)
        rows = [jnp.sum(dcv * t, axis=0, keepdims=True) for t in taps]
        rows.append(jnp.zeros((8 - K_SC, SCB), F32))
        dw_ref[...] += jnp.concatenate(rows, axis=0)

    wspec = pl.BlockSpec((8, SCB), lambda j, i: (0, j))
    return pl.pallas_call(
        body, name="shortconv_bwd", grid=(D_MODEL // SCB, nt),
        in_specs=[d_s, dn_s, cur, prev, nxt, wspec],
        out_specs=[cur, wspec],
        out_shape=[jax.ShapeDtypeStruct((T, D_MAIN), BF16), jax.ShapeDtypeStruct((8, D_MODEL), F32)],
        compiler_params=_cparams(("parallel", "arbitrary")),
    )(dmix, dmix, proj, proj, proj, w8)


GW = HEADS_PER_GROUP * HEADDIM


def _dot(a, b):
    return jnp.dot(a.astype(BF16), b.astype(BF16), preferred_element_type=F32)


def _dot_nt(a, b):
    return lax.dot_general(a.astype(BF16), b.astype(BF16), (((1,), (1,)), ((), ())), preferred_element_type=F32)


def _dot_tn(a, b):
    return lax.dot_general(a.astype(BF16), b.astype(BF16), (((0,), (0,)), ((), ())), preferred_element_type=F32)


def _bf16_terms(x, n):
    terms, r = [], x
    for _ in range(n):
        t = r.astype(BF16)
        terms.append(t)
        r = r - t.astype(F32)
    return terms


def _dot_sel(a, sel, n=2):
    s = sel.astype(BF16)
    return sum(jnp.dot(t, s, preferred_element_type=F32) for t in _bf16_terms(a, n))


def _sel_dot(sel, b, n=2):
    s = sel.astype(BF16)
    return sum(jnp.dot(s, t, preferred_element_type=F32) for t in _bf16_terms(b, n))


def _sel_dot_nt(sel, b, n=2):
    s = sel.astype(BF16)
    return sum(lax.dot_general(s, t, (((1,), (1,)), ((), ())), preferred_element_type=F32)
               for t in _bf16_terms(b, n))


def _head_cols(rows):
    parts = [jnp.broadcast_to(rows[r:r + 1, :], (HEADDIM, CHUNK)) for r in range(HEADS_PER_GROUP)]
    return jnp.concatenate(parts, axis=0).T


def _head_rows(rows):
    parts = [jnp.broadcast_to(rows[r:r + 1, :], (HEADDIM, N_STATE)) for r in range(HEADS_PER_GROUP)]
    return jnp.concatenate(parts, axis=0)


def _ssd_common(dtr, bias, alog):
    dt = _softplus(dtr + bias)
    A = -jnp.exp(alog)
    a = dt * A
    ki = lax.broadcasted_iota(jnp.int32, (CHUNK, CHUNK), 0)
    si = lax.broadcasted_iota(jnp.int32, (CHUNK, CHUNK), 1)
    upper = (ki <= si).astype(F32)
    cs = _dot_sel(a, upper, 3)
    cs_last = jnp.broadcast_to(cs[:, CHUNK - 1:CHUNK], (8, CHUNK))
    return dt, A, a, cs, cs_last


def _decay_matrix(cs, r):
    li = lax.broadcasted_iota(jnp.int32, (CHUNK, CHUNK), 0)
    si = lax.broadcasted_iota(jnp.int32, (CHUNK, CHUNK), 1)
    causal = li >= si
    R = jnp.broadcast_to(cs[r:r + 1, :], (CHUNK, CHUNK))
    seg = jnp.where(causal, R.T - R, 0.0)
    return jnp.where(causal, jnp.exp(seg), 0.0)


def _decay_cat(cs):
    return jnp.concatenate([_decay_matrix(cs, r) for r in range(HEADS_PER_GROUP)], axis=1)


def _lanes4(m):
    return jnp.concatenate([m] * HEADS_PER_GROUP, axis=1)


def _head_blocks(v):
    col = lax.broadcasted_iota(jnp.int32, v.shape, 1) // HEADDIM
    return jnp.concatenate([jnp.where(col == r, v, jnp.zeros_like(v)) for r in range(HEADS_PER_GROUP)], axis=0)


GXBC = GW + 2 * N_STATE


GS = 4


def _ssd_in_specs(nc, rev):
    cix = (lambda c: nc - 1 - c) if rev else (lambda c: c)
    x_s = pl.BlockSpec((CHUNK, GS * GW), lambda g, c: (cix(c), g))
    xbc_s = pl.BlockSpec((CHUNK, GS * GXBC), lambda g, c: (cix(c), g))
    dtr_s = pl.BlockSpec((GS, 8, CHUNK), lambda g, c: (g, 0, cix(c)))
    row_s = pl.BlockSpec((GS, 8, CHUNK), lambda g, c: (g, 0, 0))
    drep_s = pl.BlockSpec((1, GS * GW), lambda g, c: (0, g))
    hs_s = pl.BlockSpec((1, GS * GW, N_STATE), lambda g, c: (cix(c), g, 0))
    return x_s, xbc_s, dtr_s, row_s, drep_s, hs_s


def _xbc_parts(xbc_ref, gi):
    o = gi * GXBC
    return xbc_ref[:, o:o + GW], xbc_ref[:, o + GW:o + GW + N_STATE], xbc_ref[:, o + GW + N_STATE:o + GXBC]


def _ssd_fwd(xbc, dtr, bias, alog, drep):
    T = xbc.shape[0]
    nc = T // CHUNK
    x_s, xbc_s, dtr_s, row_s, drep_s, hs_s = _ssd_in_specs(nc, False)

    def body(xbc_ref, dtr_ref, bias_ref, alog_ref, drep_ref, y_ref, hs_ref, h_scr):
        @pl.when(pl.program_id(1) == 0)
        def _():
            h_scr[...] = jnp.zeros_like(h_scr)

        for gi in range(GS):
            cols, rows = slice(gi * GW, (gi + 1) * GW), pl.ds(gi * GW, GW)
            x, Bm, Cm = _xbc_parts(xbc_ref, gi)
            dt, A, a, cs, cs_last = _ssd_common(dtr_ref[gi], bias_ref[gi], alog_ref[gi])
            E = _head_cols(jnp.exp(cs))
            W = _head_cols(jnp.exp(cs_last - cs) * dt)
            X = (x * _head_cols(dt)).astype(BF16)
            CB = _dot_nt(Cm, Bm)
            y = jnp.dot((_lanes4(CB) * _decay_cat(cs)).astype(BF16), _head_blocks(X), preferred_element_type=F32)
            h = h_scr[rows, :]
            hs_ref[0, rows, :] = h
            y = y + _dot_nt(Cm, h) * E
            y_ref[:, cols] = y + drep_ref[:, cols] * x
            h_scr[rows, :] = h * _head_rows(jnp.exp(cs_last)) + _dot_tn(x * W, Bm)

    return pl.pallas_call(
        body, name="ssd_fwd", grid=(N_GROUPS // GS, nc),
        in_specs=[xbc_s, dtr_s, row_s, row_s, drep_s],
        out_specs=[x_s, hs_s],
        out_shape=[jax.ShapeDtypeStruct((T, D_SSM), F32), jax.ShapeDtypeStruct((nc, D_SSM, N_STATE), F32)],
        scratch_shapes=[pltpu.VMEM((GS * GW, N_STATE), F32)],
        compiler_params=_cparams(("parallel", "arbitrary")),
    )(xbc, dtr, bias, alog, drep)


def _ssd_bwd(xbc, dtr, bias, alog, drep, dy, hs):
    T = xbc.shape[0]
    nc = T // CHUNK
    x_s, xbc_s, dtr_s, row_s, drep_s, hs_s = _ssd_in_specs(nc, True)

    def body(xbc_ref, dtr_ref, bias_ref, alog_ref, drep_ref, dy_ref, hs_ref,
             dxbc_ref, ddtr_ref, dbias_ref, dalog_ref, dd_ref, dh_scr):
        @pl.when(pl.program_id(1) == 0)
        def _():
            dh_scr[...] = jnp.zeros_like(dh_scr)
            dbias_ref[...] = jnp.zeros_like(dbias_ref)
            dalog_ref[...] = jnp.zeros_like(dalog_ref)
            dd_ref[...] = jnp.zeros_like(dd_ref)

        for gi in range(GS):
            one_group(gi, xbc_ref, dtr_ref, bias_ref, alog_ref, drep_ref, dy_ref, hs_ref,
                      dxbc_ref, ddtr_ref, dbias_ref, dalog_ref, dd_ref, dh_scr)

    def one_group(gi, xbc_ref, dtr_ref, bias_ref, alog_ref, drep_ref, dy_ref, hs_ref,
                  dxbc_ref, ddtr_ref, dbias_ref, dalog_ref, dd_ref, dh_scr):
        cols, rows, o = slice(gi * GW, (gi + 1) * GW), pl.ds(gi * GW, GW), gi * GXBC
        x, Bm, Cm = _xbc_parts(xbc_ref, gi)
        dY = dy_ref[:, cols]
        dt, A, a, cs, cs_last = _ssd_common(dtr_ref[gi], bias_ref[gi], alog_ref[gi])
        E = _head_cols(jnp.exp(cs))
        DT = _head_cols(dt)
        Wd = _head_cols(jnp.exp(cs_last - cs))
        X = x * DT
        h = hs_ref[0, rows, :]
        dS = dh_scr[rows, :]
        CB = _dot_nt(Cm, Bm)
        rowid = lax.broadcasted_iota(jnp.int32, (8, CHUNK), 0)
        lane = lax.broadcasted_iota(jnp.int32, (8, CHUNK), 1)
        hsel = (lax.broadcasted_iota(jnp.int32, (8, GW), 1) // HEADDIM
                == lax.broadcasted_iota(jnp.int32, (8, GW), 0)).astype(F32)
        hsel_l = (lax.broadcasted_iota(jnp.int32, (8, HEADS_PER_GROUP * CHUNK), 1) // CHUNK
                  == lax.broadcasted_iota(jnp.int32, (8, HEADS_PER_GROUP * CHUNK), 0)).astype(F32)

        Lc, CBc = _decay_cat(cs), _lanes4(CB)
        Mc = CBc * Lc
        GLc = _dot_nt(dY, _head_blocks(X.astype(BF16))) * Lc
        Wc = GLc * CBc
        colsum = jnp.sum(Wc, axis=0, keepdims=True)
        dcs = _sel_dot_nt(hsel_l, Wc)
        dCB = jnp.zeros((CHUNK, CHUNK), F32)
        for r in range(HEADS_PER_GROUP):
            blk = slice(r * CHUNK, (r + 1) * CHUNK)
            dCB = dCB + GLc[:, blk]
            dcs = dcs - jnp.where(rowid == r, colsum[:, blk], 0.0)
        m_stack = jnp.concatenate([Mc[:, r * CHUNK:(r + 1) * CHUNK].astype(BF16) for r in range(HEADS_PER_GROUP)],
                                  axis=0)
        dX = lax.dot_general(m_stack, _head_blocks(dY.astype(BF16)), (((0,), (0,)), ((), ())),
                             preferred_element_type=F32)
        dC = _dot(dCB, Bm)
        dB = _dot_tn(dCB, Cm)
        T1 = _dot_nt(Bm, dS)
        dX = dX + T1 * Wd
        dB = dB + _dot(X * Wd, dS)
        pdec = _sel_dot_nt(hsel, X * T1 * Wd)
        dcs = dcs - pdec
        dlast = jnp.sum(pdec, axis=1, keepdims=True) \
            + jnp.exp(cs_last[:, 0:1]) * jnp.sum(_sel_dot(hsel, dS * h), axis=1, keepdims=True)
        dYE = dY * E
        dC = dC + _dot(dYE, h)
        yoff = _dot_nt(Cm, h) * E
        dcs = dcs + _sel_dot_nt(hsel, dY * yoff)
        dcs = dcs + jnp.where(lane == CHUNK - 1, dlast, 0.0)
        ki = lax.broadcasted_iota(jnp.int32, (CHUNK, CHUNK), 0)
        si = lax.broadcasted_iota(jnp.int32, (CHUNK, CHUNK), 1)
        lower = (ki >= si).astype(F32)
        da = _dot_sel(dcs, lower)
        ddt = da * A + _sel_dot_nt(hsel, dX * x)
        ddtr = ddt * _sigmoid(dtr_ref[gi] + bias_ref[gi])
        ddtr_ref[gi] = ddtr
        dbias_ref[gi] += ddtr
        dalog_ref[gi] += da * a
        dxbc_ref[:, o:o + GW] = dX * DT + drep_ref[:, cols] * dY
        dd_ref[:, cols] += jnp.sum(dY * x, axis=0, keepdims=True)
        dxbc_ref[:, o + GW:o + GW + N_STATE] = dB
        dxbc_ref[:, o + GW + N_STATE:o + GXBC] = dC
        dh_scr[rows, :] = dS * _head_rows(jnp.exp(cs_last)) + _dot_tn(dYE, Cm)

    return pl.pallas_call(
        body, name="ssd_bwd", grid=(N_GROUPS // GS, nc),
        in_specs=[xbc_s, dtr_s, row_s, row_s, drep_s, x_s, hs_s],
        out_specs=[xbc_s, dtr_s, row_s, row_s, drep_s],
        out_shape=[jax.ShapeDtypeStruct((T, D_XBC), F32),
                   jax.ShapeDtypeStruct((N_GROUPS, 8, T), F32),
                   jax.ShapeDtypeStruct((N_GROUPS, 8, CHUNK), F32),
                   jax.ShapeDtypeStruct((N_GROUPS, 8, CHUNK), F32),
                   jax.ShapeDtypeStruct((1, D_SSM), F32)],
        scratch_shapes=[pltpu.VMEM((GS * GW, N_STATE), F32)],
        compiler_params=_cparams(("parallel", "arbitrary")),
    )(xbc, dtr, bias, alog, drep, dy, hs)


def _adamw(w, g, m, v, name, deps=(), emit_g=False):
    R, C = w.shape
    tr = _tile(R, 256, 8)
    nd = len(deps)
    nout = 4 if emit_g else 3

    def body(w_ref, g_ref, m_ref, v_ref, *rest):
        outs = rest[nd:]
        gv = g_ref[...]
        mn = ADAM_B1 * m_ref[...] + (1.0 - ADAM_B1) * gv
        vn = ADAM_B2 * v_ref[...] + (1.0 - ADAM_B2) * (gv * gv)
        m_hat = mn / (1.0 - ADAM_B1 ** ADAM_STEP)
        v_hat = vn / (1.0 - ADAM_B2 ** ADAM_STEP)
        outs[0][...] = -ADAM_LR * (m_hat / (jnp.sqrt(v_hat) + ADAM_EPS) + ADAM_WD * w_ref[...])
        outs[1][...] = mn
        outs[2][...] = vn
        if emit_g:
            outs[3][...] = gv

    spec = pl.BlockSpec((tr, C), lambda i: (i, 0))
    return pl.pallas_call(
        body, name=name, grid=(R // tr,),
        in_specs=[spec] * 4 + [ANY] * nd, out_specs=[spec] * nout,
        out_shape=[jax.ShapeDtypeStruct((R, C), F32)] * nout,
        compiler_params=_cparams(("parallel",)),
    )(w, g, m, v, *deps)


ANY = pl.BlockSpec(memory_space=pl.ANY)


def _place():
    x, y, c = lax.axis_index("x"), lax.axis_index("y"), lax.axis_index("c")
    return x, y, c


def _other_chips(x, y):
    return [(1 - x, y), (x, 1 - y), (1 - x, 1 - y)]


def _allgather_inplace(bufs, splits, first_done=False):
    n = len(bufs)

    def body(*refs):
        o_refs = refs[n:2 * n]
        send_sems, recv_sems = refs[2 * n:]
        x, y, c = _place()
        xn, yn, dg, sibling = (1 - x, y), (x, 1 - y), (1 - x, 1 - y), (x, y, 1 - c)

        def blk(k, chip, pc):
            return o_refs[k].at[4 * chip[0] + 2 * chip[1] + pc]

        def part(k, ref, p):
            kind, s = splits[k]
            _, R, C = bufs[k].shape
            if kind == "rows":
                return ref.at[pl.ds(0, s)] if p == 0 else ref.at[pl.ds(s, R - s)]
            return ref.at[:, pl.ds(0, s)] if p == 0 else ref.at[:, pl.ds(s, C - s)]

        def copy(k, slot, ref, to):
            return pltpu.make_async_remote_copy(
                src_ref=ref, dst_ref=ref, send_sem=send_sems.at[k, slot], recv_sem=recv_sems.at[k, slot],
                device_id=to, device_id_type=MESH)

        sent = []

        def send(k, slot, ref, to):
            cp = copy(k, slot, ref, to)
            cp.start()
            sent.append(cp)

        if not first_done:
            for k in range(n):
                send(k, 0, blk(k, (x, y), c), (*xn, c))
                send(k, 1, blk(k, (x, y), c), (*yn, c))
        for k in range(n):
            bx, by = blk(k, xn, c), blk(k, yn, c)
            if not first_done:
                copy(k, 0, bx, sibling).wait_recv()
            send(k, 2, part(k, bx, 0), (*yn, c))
            send(k, 4, bx, sibling)
            if not first_done:
                copy(k, 1, by, sibling).wait_recv()
            send(k, 3, part(k, by, 1), (*xn, c))
            send(k, 5, by, sibling)
        for k in range(n):
            d0, d1 = part(k, blk(k, dg, c), 0), part(k, blk(k, dg, c), 1)
            copy(k, 2, d0, sibling).wait_recv()
            send(k, 6, d0, sibling)
            copy(k, 3, d1, sibling).wait_recv()
            send(k, 7, d1, sibling)
        for k in range(n):
            copy(k, 4, blk(k, xn, 1 - c), sibling).wait_recv()
            copy(k, 5, blk(k, yn, 1 - c), sibling).wait_recv()
            copy(k, 6, part(k, blk(k, dg, 1 - c), 0), sibling).wait_recv()
            copy(k, 7, part(k, blk(k, dg, 1 - c), 1), sibling).wait_recv()
        for cp in sent:
            cp.wait_send()

    return pl.pallas_call(
        body, name="allgather_w_in",
        in_specs=[ANY] * n, out_specs=[ANY] * n,
        out_shape=[jax.ShapeDtypeStruct(b.shape, b.dtype) for b in bufs],
        input_output_aliases={k: k for k in range(n)},
        scratch_shapes=[pltpu.SemaphoreType.DMA((n, 8)), pltpu.SemaphoreType.DMA((n, 8))],
    )(*bufs)


HBM = pl.BlockSpec(memory_space=pltpu.HBM)
SEM = pl.BlockSpec(memory_space=pltpu.SEMAPHORE)
EFFECT = pltpu.SideEffectType.DATAFLOW_SIDE_EFFECTING


def _split_start(name, arrays, build, n_copies, after=()):
    na, nd = len(arrays), len(after)

    def body(*refs):
        send_sems, recv_sems = refs[na + nd], refs[na + nd + 1]
        for cp in build(refs[:na], send_sems, recv_sems):
            cp.start()
        refs[-1][...] = jnp.zeros((8, 128), F32)

    outs = pl.pallas_call(
        body, name=name,
        out_shape=(pltpu.SemaphoreType.DMA((n_copies,)), pltpu.SemaphoreType.DMA((n_copies,)),
                   *[pltpu.HBM(a.shape, a.dtype) for a in arrays], jax.ShapeDtypeStruct((8, 128), F32)),
        in_specs=[HBM] * na + [ANY] * nd,
        out_specs=(SEM, SEM, *[HBM] * na, pl.BlockSpec(memory_space=pltpu.VMEM)),
        input_output_aliases={i: 2 + i for i in range(na)},
        compiler_params=pltpu.CompilerParams(has_side_effects=EFFECT),
    )(*[pltpu.with_memory_space_constraint(a, pltpu.HBM) for a in arrays], *after)
    return outs[0], outs[1], list(outs[2:2 + na]), outs[-1]


def _split_wait(name, send_sems, recv_sems, arrays, build, after):
    na = len(arrays)

    def body(*refs):
        for cp in build(refs[:na], refs[na], refs[na + 1]):
            cp.wait_send()
            cp.wait_recv()

    outs = pl.pallas_call(
        body, name=name,
        out_shape=tuple(pltpu.HBM(a.shape, a.dtype) for a in arrays),
        in_specs=[HBM] * na + [SEM, SEM] + [ANY] * len(after),
        out_specs=tuple([HBM] * na),
        input_output_aliases={i: i for i in range(na)},
        compiler_params=pltpu.CompilerParams(has_side_effects=EFFECT),
    )(*arrays, send_sems, recv_sems, *after)
    return list(outs)


def _remote(src, dst, send_sems, recv_sems, i, to):
    return pltpu.make_async_remote_copy(src_ref=src, dst_ref=dst, send_sem=send_sems.at[i], recv_sem=recv_sems.at[i],
                                        device_id=to, device_id_type=MESH)


def _build_ag_first(refs, ss, rs):
    x, y, c = _place()
    cps = []
    for k, ref in enumerate(refs):
        blk = ref.at[4 * x + 2 * y + c]
        cps += [_remote(blk, blk, ss, rs, 2 * k, (1 - x, y, c)), _remote(blk, blk, ss, rs, 2 * k + 1, (x, 1 - y, c))]
    return cps


def _build_ag_ici(refs, ss, rs):
    x, y, c = _place()
    cps = []
    for k, ref in enumerate(refs):
        blk = ref.at[4 * x + 2 * y + c]
        for j, (px, py) in enumerate(_other_chips(x, y)):
            cps.append(_remote(blk, blk, ss, rs, 3 * k + j, (px, py, c)))
    return cps


def _build_ag_fwd(refs, ss, rs):
    x, y, c = _place()
    cps = []
    for k, ref in enumerate(refs):
        for j, (px, py) in enumerate(_other_chips(x, y)):
            blk = ref.at[4 * px + 2 * py + c]
            cps.append(_remote(blk, blk, ss, rs, 3 * k + j, (x, y, 1 - c)))
    return cps


def _build_rs_swap(refs, ss, rs):
    x, y, c = _place()
    n = len(refs) // 2
    return [_remote(refs[k].at[:, pl.ds(1 - c, 1)], refs[n + k], ss, rs, k, (x, y, 1 - c)) for k in range(n)]


def _build_rs_ici(refs, ss, rs):
    x, y, c = _place()
    n = len(refs) // 2
    me = 2 * x + y
    cps = []
    for k in range(n):
        for j, (px, py) in enumerate(_other_chips(x, y)):
            cps.append(_remote(refs[k].at[2 * px + py], refs[n + k].at[me], ss, rs, 3 * k + j, (px, py, c)))
    return cps


def _build_rs_share(refs, ss, rs):
    x, y, c = _place()
    return [_remote(ref.at[c], ref.at[c], ss, rs, k, (x, y, 1 - c)) for k, ref in enumerate(refs)]


def _allreduce_small(p, deps=()):
    R, C = p.shape
    nd = len(deps)

    def body(p_ref, *rest):
        gath_ref, sum_ref, send_sems, recv_sems, local_sem = rest[nd:]
        x, y, c = _place()
        me, sibling = (x, y, c), (x, y, 1 - c)
        chips = [(1 - x, y), (x, 1 - y), (1 - x, 1 - y)]

        def blk(px, py, pc):
            return gath_ref.at[4 * px + 2 * py + pc]

        def copy(k, block, to, src=None):
            return pltpu.make_async_remote_copy(
                src_ref=blk(*block) if src is None else src, dst_ref=blk(*block),
                send_sem=send_sems.at[k], recv_sem=recv_sems.at[k], device_id=to, device_id_type=MESH)

        mine = pltpu.make_async_copy(p_ref, blk(*me), local_sem)
        mine.start()
        first = [copy(0, me, sibling, src=p_ref)]
        first += [copy(1 + j, me, (*chip, c), src=p_ref) for j, chip in enumerate(chips)]
        for cp in first:
            cp.start()
        passed = [copy(4 + j, (*chip, c), sibling) for j, chip in enumerate(chips)]
        for j, chip in enumerate(chips):
            copy(1 + j, (*chip, c), me).wait_recv()
            passed[j].start()
        copy(0, sibling, me).wait_recv()
        for j, chip in enumerate(chips):
            copy(4 + j, (*chip, 1 - c), me).wait_recv()
        for cp in first + passed:
            cp.wait_send()
        mine.wait()
        s = gath_ref[0]
        for d in range(1, N_DEV):
            s = s + gath_ref[d]
        sum_ref[...] = s

    vm = pl.BlockSpec(memory_space=pltpu.VMEM)
    return pl.pallas_call(
        body, name="allreduce_small",
        in_specs=[vm] + [ANY] * nd, out_specs=[vm, vm],
        out_shape=[jax.ShapeDtypeStruct((N_DEV, R, C), F32), jax.ShapeDtypeStruct((R, C), F32)],
        scratch_shapes=[pltpu.SemaphoreType.DMA((7,)), pltpu.SemaphoreType.DMA((7,)), pltpu.SemaphoreType.DMA],
    )(p, *deps)[1]


def _rs_add_pair(p, r0, c_arr, name):
    _, _, hr, cols = p.shape
    tr = _tile(hr, 256, 8)

    def body(c_ref, p_ref, r_ref, q_ref):
        q_ref[...] = (p_ref[0] + r_ref[0]).astype(BF16)

    grid_spec = pltpu.PrefetchScalarGridSpec(
        num_scalar_prefetch=1, grid=(N_CHIPS, hr // tr),
        in_specs=[pl.BlockSpec((1, 1, tr, cols), lambda j, i, c_ref: (j, c_ref[0], i, 0)),
                  pl.BlockSpec((1, 1, tr, cols), lambda j, i, c_ref: (j, 0, i, 0))],
        out_specs=pl.BlockSpec((1, tr, cols), lambda j, i, c_ref: (j, i, 0)))
    return pl.pallas_call(
        body, name=name, grid_spec=grid_spec,
        out_shape=jax.ShapeDtypeStruct((N_CHIPS, hr, cols), BF16),
        compiler_params=_cparams(("parallel", "parallel")),
    )(c_arr, p, r0)


def _rs_add_chips(r1, q, place_arr, name):
    _, hr, cols = r1.shape
    tr = _tile(hr, 256, 8)

    def body(place_ref, r_ref, q_ref, o_ref):
        chip = place_ref[0]
        s = None
        for j in range(N_CHIPS):
            t = jnp.where(chip == j, q_ref[j], r_ref[j]).astype(F32)
            s = t if s is None else s + t
        o_ref[...] = s

    blk = pl.BlockSpec((N_CHIPS, tr, cols), lambda i, place_ref: (0, i, 0))
    grid_spec = pltpu.PrefetchScalarGridSpec(
        num_scalar_prefetch=1, grid=(hr // tr,), in_specs=[blk, blk],
        out_specs=pl.BlockSpec((None, tr, cols), lambda i, place_ref: (place_ref[1], i, 0)))
    return pl.pallas_call(
        body, name=name, grid_spec=grid_spec,
        out_shape=jax.ShapeDtypeStruct((2, hr, cols), F32),
        compiler_params=_cparams(("parallel",)),
    )(place_arr, r1, q)


def _pad_rows(a, rows):
    return jnp.pad(a, ((0, rows - a.shape[0]), (0, 0)))


def _pad_cols(a, cols):
    return jnp.pad(a, ((0, 0), (0, cols - a.shape[1])))


def _heads_to_rows(v):
    v = v.reshape(N_GROUPS, HEADS_PER_GROUP, 1)
    v = jnp.pad(v, ((0, 0), (0, 8 - HEADS_PER_GROUP), (0, 0)))
    return jnp.broadcast_to(v, (N_GROUPS, 8, CHUNK))


def _rows_to_heads(a):
    return jnp.sum(a[:, :HEADS_PER_GROUP, :], axis=-1).reshape(N_HEADS)


def _to_kernel_rows(a):
    C = a.shape[1]
    x0, b0, c0, s0 = D_SSM, 2 * D_SSM, 2 * D_SSM + 1024, D_SSM + D_XBC + N_HEADS
    xbc = jnp.concatenate([a[x0:b0].reshape(N_GROUPS, GW, C), a[b0:c0].reshape(N_GROUPS, N_STATE, C),
                           a[c0:c0 + 1024].reshape(N_GROUPS, N_STATE, C)], axis=1).reshape(D_XBC, C)
    sc = jnp.concatenate([a[s0 + k * D_MODEL:s0 + (k + 1) * D_MODEL].reshape(D_MODEL // SCB, SCB, C)
                          for k in range(3)], axis=1).reshape(3 * D_MODEL, C)
    return jnp.concatenate([a[:D_SSM], xbc, sc], axis=0)


HR_IN = 1568


def _shard_row_plan():
    segs = [(0, 0, 0, D_SSM)]
    for g in range(N_GROUPS):
        k0 = D_SSM + g * GXBC
        segs += [(0, k0, D_SSM + g * GW, GW), (0, k0 + GW, 2 * D_SSM + g * N_STATE, N_STATE),
                 (0, k0 + GW + N_STATE, 2 * D_SSM + 1024 + g * N_STATE, N_STATE)]
    segs.append((1, 0, D_SSM + D_XBC, N_HEADS))
    for j in range(D_MODEL // SCB):
        for k in range(3):
            segs.append((0, D_SSM + D_XBC + j * SC3 + k * SCB, D_SSM + D_XBC + N_HEADS + k * D_MODEL + j * SCB, SCB))
    cs = D_IN // N_CHIPS
    plan = []
    for src, s, o, n in segs:
        while n > 0:
            chip, loc = divmod(o, cs)
            half, row = divmod(loc, HR_IN)
            m = min(n, cs - loc, HR_IN - row)
            plan.append((src, s, chip, half, row, m))
            s, o, n = s + m, o + m, n - m
    return plan


SCATTER_ROWS = 512
SCATTER_SLOTS = 4


def _scatter_rows_to_shards(k_main, k_dt):
    C = k_main.shape[1]
    pieces = []
    for src, s, chip, half, row, n in _shard_row_plan():
        for o in range(0, n, SCATTER_ROWS):
            pieces.append((src, s + o, chip, half, row + o, min(SCATTER_ROWS, n - o)))
    S, lag, N = SCATTER_SLOTS, SCATTER_SLOTS // 2, len(pieces)

    def body(m_ref, d_ref, o_ref, buf, in_sems, out_sems):
        def cin(i):
            src, s, _, _, _, n = pieces[i]
            return pltpu.make_async_copy((d_ref if src else m_ref).at[pl.ds(s, n)],
                                         buf.at[i % S, pl.ds(0, n)], in_sems.at[i % S])

        def cout(i):
            _, _, chip, half, row, n = pieces[i]
            return pltpu.make_async_copy(buf.at[i % S, pl.ds(0, n)],
                                         o_ref.at[chip, half, pl.ds(row, n)], out_sems.at[i % S])

        for i in range(N + lag):
            if i < N:
                if i >= S:
                    cout(i - S).wait()
                cin(i).start()
            j = i - lag
            if 0 <= j < N:
                cin(j).wait()
                cout(j).start()
        for j in range(max(0, N - S), N):
            cout(j).wait()

    return pl.pallas_call(
        body, name="scatter_dw_in_rows", in_specs=[ANY, ANY], out_specs=ANY,
        out_shape=jax.ShapeDtypeStruct((N_CHIPS, 2, HR_IN, C), k_main.dtype),
        scratch_shapes=[pltpu.VMEM((S, SCATTER_ROWS, C), k_main.dtype),
                        pltpu.SemaphoreType.DMA((S,)), pltpu.SemaphoreType.DMA((S,))],
        compiler_params=_cparams(),
    )(k_main, k_dt)


def _to_kernel_xbc(a):
    R = a.shape[0]
    return jnp.concatenate([a[:, :D_SSM].reshape(R, N_GROUPS, GW), a[:, D_SSM:D_SSM + 1024].reshape(R, N_GROUPS, N_STATE),
                            a[:, D_SSM + 1024:].reshape(R, N_GROUPS, N_STATE)], axis=2).reshape(R, D_XBC)


def _from_kernel_xbc(a):
    R = a.shape[0]
    g = a.reshape(R, N_GROUPS, GXBC)
    return jnp.concatenate([g[:, :, :GW].reshape(R, D_SSM), g[:, :, GW:GW + N_STATE].reshape(R, 1024),
                            g[:, :, GW + N_STATE:].reshape(R, 1024)], axis=1)


def kernel(x, norm_mix_g, w_in, ssm_conv_w, ssm_conv_b, ssm_dt_bias, ssm_A_log, ssm_D, ssm_norm_g, sc_conv_w, w_out, norm_ffn_g, w_gate, w_up, w_down, norm_final_g, loss_target, m_norm_mix_g, m_w_in, m_ssm_conv_w, m_ssm_conv_b, m_ssm_dt_bias, m_ssm_A_log, m_ssm_D, m_ssm_norm_g, m_sc_conv_w, m_w_out, m_norm_ffn_g, m_w_gate, m_w_up, m_w_down, m_norm_final_g, v_norm_mix_g, v_w_in, v_ssm_conv_w, v_ssm_conv_b, v_ssm_dt_bias, v_ssm_A_log, v_ssm_D, v_ssm_norm_g, v_sc_conv_w, v_w_out, v_norm_ffn_g, v_w_gate, v_w_up, v_w_down, v_norm_final_g):
    T = x.shape[1]
    xt = x[0]
    tgt = loss_target[0]
    cx, cy, cc = lax.axis_index("x"), lax.axis_index("y"), lax.axis_index("c")
    chip = 2 * cx + cy
    c_arr = jnp.reshape(cc, (1,)).astype(jnp.int32)
    chip_arr = jnp.reshape(chip, (1,)).astype(jnp.int32)
    place_arr = jnp.stack([chip, cc]).astype(jnp.int32)

    big = [w_in[0].T, w_out[0], w_gate[0], w_up[0], w_down[0]]
    names = ["w_in", "w_out", "w_gate", "w_up", "w_down"]
    gb_in = _cast_into_gather(big[0], chip_arr, "cast_w_in", split_cols=True)
    cs_in, cs_conv = D_IN // N_CHIPS, D_XBC // N_CHIPS
    cw = jnp.stack([_pad_rows(ssm_conv_w[0], 8), _pad_cols(_pad_rows(sc_conv_w[0], 8), cs_conv)])
    cw_buf = lax.dynamic_update_slice(jnp.zeros((N_DEV, 8, cs_conv), F32), cw, (2 * chip, 0, 0))
    f_ss, f_rs, f_arr, f_tok = _split_start("ag_in_first_start", [gb_in, cw_buf], _build_ag_first, 4)
    gbufs = [None] + [_cast_into_gather(w, chip_arr, "cast_" + nm, deps=[f_tok]) for w, nm in zip(big[1:], names[1:])]
    n1 = _rmsnorm_fwd(xt, _tie(norm_mix_g, f_tok, "tie_ag_first"), "rmsnorm_mix")
    f_arr = _split_wait("ag_in_first_wait", f_ss, f_rs, f_arr, _build_ag_first, after=gbufs[1:] + [n1])
    g_in, cw_all = _allgather_inplace(f_arr, [("rows", (cs_in // 32) * 16), ("cols", cs_conv // 2)], first_done=True)
    cw_all = cw_all.reshape(N_CHIPS, 2, 8, cs_conv)
    ssm_w8 = _to_kernel_xbc(cw_all[:, 0].transpose(1, 0, 2).reshape(8, D_XBC))
    sc_w8 = cw_all[:, 1, :, :D_MODEL // N_CHIPS].transpose(1, 0, 2).reshape(8, D_MODEL)
    ssm_bk = _to_kernel_xbc(ssm_conv_b)
    wt = g_in.reshape(N_CHIPS, 2, cs_in, D_MODEL // 2).transpose(0, 2, 1, 3).reshape(D_IN, D_MODEL)
    wt_main = _to_kernel_rows(wt)
    wt_dt = _pad_rows(wt[D_SSM + D_XBC:D_SSM + D_XBC + N_HEADS], DT_PAD)
    ag_ss, ag_rs, ag_bufs, ag_tok = _split_start("ag_ici_start", gbufs[1:], _build_ag_ici, 12, after=[g_in, cw_all])

    bias_rows = _heads_to_rows(ssm_dt_bias[0])
    alog_rows = _heads_to_rows(ssm_A_log[0])
    drep = jnp.repeat(ssm_D[0], HEADDIM).reshape(1, D_SSM)

    (proj,) = _matmul([(n1, wt_main)], tb=True, out_dtypes=[F32], name="mm_proj", deps=[ag_tok])
    (dt_raw,) = _matmul([(n1, wt_dt)], tb=True, out_dtypes=[F32], name="mm_proj_dt")
    xbc = _ssm_conv_fwd(proj, ssm_w8, ssm_bk)
    dtr = jnp.pad(dt_raw[:, :N_HEADS].T.reshape(N_GROUPS, HEADS_PER_GROUP, T), ((0, 0), (0, 4), (0, 0)))
    y_ssd, hs = _ssd_fwd(xbc, dtr, bias_rows, alog_rows, drep)
    ag_bufs = _split_wait("ag_ici_wait", ag_ss, ag_rs, ag_bufs, _build_ag_ici, after=[y_ssd])
    fw_ss, fw_rs, fw_bufs, fw_tok = _split_start("ag_fwd_start", ag_bufs, _build_ag_fwd, 12)
    y_mix = _shortconv_fwd(proj, sc_w8, _gated_norm_fwd(y_ssd, proj, _tie(ssm_norm_g, fw_tok, "tie_ag_fwd")))
    gath = _split_wait("ag_fwd_wait", fw_ss, fw_rs, fw_bufs, _build_ag_fwd, after=[y_mix])
    w_out_f = gath[0].reshape(2 * D_MODEL, D_MODEL)
    w_gate3 = gath[1].reshape(N_CHIPS, D_MODEL, D_FF // N_CHIPS)
    w_up3 = gath[2].reshape(N_CHIPS, D_MODEL, D_FF // N_CHIPS)
    w_down_f = gath[3].reshape(D_FF, D_MODEL)
    (h1,) = _matmul([(y_mix, w_out_f)], out_dtypes=[F32], name="mm_out", extras=[xt],
                    epilogue=lambda acc, res: (acc + res,))
    n2 = _rmsnorm_fwd(h1, norm_ffn_g, "rmsnorm_ffn")
    g_act, u_act, a_act = _ffn_fwd(n2, w_gate3, w_up3)
    (h2,) = _matmul([(a_act, w_down_f)], out_dtypes=[F32], name="mm_down", extras=[h1],
                    epilogue=lambda acc, res: (acc + res,))

    dh2, dh2b, dg_final, loss_part = _loss_and_final_bwd(h2, tgt, norm_final_g.reshape(1, D_MODEL))
    dg_act, du_act = _matmul([(dh2b, w_down_f)], tb=True, out_dtypes=[BF16, BF16], name="mm_down_bwd",
                             tn=512, extras=[g_act, u_act], epilogue=_swiglu_bwd, nsub=2)
    (dw_down,) = _matmul([(a_act, dh2b)], ta=True, out_dtypes=[F32], name="mm_dw_down", tm=1408, tn=512)
    (dn2,) = _matmul([(dg_act, w_gate3), (du_act, w_up3)], tb=True, b3d=True, out_dtypes=[BF16],
                     name="mm_ffn_in_bwd")
    (dw_gate,) = _matmul([(n2, dg_act)], ta=True, out_dtypes=[F32], name="mm_dw_gate", tm=512, tn=1408,
                         col_shards=True)
    (dw_up,) = _matmul([(n2, du_act)], ta=True, out_dtypes=[F32], name="mm_dw_up", tm=512, tn=1408,
                       col_shards=True)
    dh1, dh1b, dg_ffn = _rmsnorm_bwd(dn2, h1, norm_ffn_g, dh2, "rmsnorm_ffn_bwd")
    (dw_out,) = _matmul([(y_mix, dh1b)], ta=True, out_dtypes=[F32], name="mm_dw_out")

    def halves(g):
        return g.reshape(N_CHIPS, 2, g.shape[1] // 2, g.shape[2])

    def landing(shape, dtype):
        return lax.empty(shape, dtype)

    names1 = names[1:]
    ps1 = [halves(dw_out.reshape(N_CHIPS, -1, D_MODEL)), halves(dw_gate), halves(dw_up),
           halves(dw_down.reshape(N_CHIPS, -1, D_MODEL))]
    r0_1 = [landing((N_CHIPS, 1) + p.shape[2:], F32) for p in ps1]
    sw_ss, sw_rs, sw_arr, sw_tok = _split_start("rs1_swap_start", ps1 + r0_1, _build_rs_swap, 4)
    (dmix,) = _matmul([(dh1b, w_out_f)], tb=True, out_dtypes=[BF16], name="mm_out_bwd", deps=[sw_tok])
    dproj, dw_sc = _shortconv_bwd(dmix, proj, sc_w8)
    dy_ssd, dproj, dg_ssmnorm = _gated_norm_bwd(dmix, y_ssd, proj, ssm_norm_g, dproj)
    sw_arr = _split_wait("rs1_swap_wait", sw_ss, sw_rs, sw_arr, _build_rs_swap, after=[dy_ssd])
    qs1 = [_rs_add_pair(p, r, c_arr, "rs_add_pair_" + nm) for p, r, nm in zip(sw_arr[:4], sw_arr[4:], names1)]
    r1_1 = [landing(q.shape, BF16) for q in qs1]
    ic_ss, ic_rs, ic_arr, ic_tok = _split_start("rs1_ici_start", qs1 + r1_1, _build_rs_ici, 12)
    dxbc_act, ddtr, dbias_acc, dalog_acc, dD_acc = _ssd_bwd(
        xbc, dtr, bias_rows, alog_rows, _tie(drep, ic_tok, "tie_rs1_ici"), dy_ssd, hs)
    dproj, dw_ssmconv, db_ssmconv = _ssm_conv_bwd(dxbc_act, proj, ssm_w8, ssm_bk, dproj)
    dw_ssmconv, db_ssmconv = _from_kernel_xbc(dw_ssmconv), _from_kernel_xbc(db_ssmconv)
    ic_arr = _split_wait("rs1_ici_wait", ic_ss, ic_rs, ic_arr, _build_rs_ici, after=[dproj])
    g1 = [_rs_add_chips(r, q, place_arr, "rs_add_chips_" + nm) for q, r, nm in zip(ic_arr[:4], ic_arr[4:], names1)]
    sh_ss, sh_rs, sh_arr, sh_tok = _split_start("rs1_share_start", g1, _build_rs_share, 4)

    ddt_raw = _pad_cols(ddtr[:, :HEADS_PER_GROUP, :].reshape(N_HEADS, T).T, DT_PAD).astype(BF16)
    (dwt_main,) = _matmul([(dproj, n1)], ta=True, out_dtypes=[F32], name="mm_dw_main", deps=[sh_tok])
    (dwt_dt,) = _matmul([(ddt_raw, n1)], ta=True, out_dtypes=[F32], name="mm_dw_dt")
    p_in = _scatter_rows_to_shards(dwt_main, dwt_dt)
    s2_ss, s2_rs, s2_arr, s2_tok = _split_start(
        "rs2_swap_start", [p_in, landing((N_CHIPS, 1) + p_in.shape[2:], F32)], _build_rs_swap, 1)
    mt = T // _tile(T, 1024)
    mt_a = max(mt // 4, 1)
    (dn1a,) = _matmul([(dproj, wt_main)], out_dtypes=[F32], name="mm_proj_bwd_a", deps=[s2_tok],
                      m_tiles=(0, mt_a))
    g1 = _split_wait("rs1_share_wait", sh_ss, sh_rs, sh_arr, _build_rs_share, after=[dn1a])
    s2_arr = _split_wait("rs2_swap_wait", s2_ss, s2_rs, s2_arr, _build_rs_swap, after=[dn1a])
    q_in = _rs_add_pair(s2_arr[0], s2_arr[1], c_arr, "rs_add_pair_w_in")
    i2_ss, i2_rs, i2_arr, i2_tok = _split_start(
        "rs2_ici_start", [q_in, landing(q_in.shape, BF16)], _build_rs_ici, 3)
    if mt > mt_a:
        (dn1a,) = _matmul([(dproj, wt_main)], out_dtypes=[F32], name="mm_proj_bwd_b", deps=[i2_tok],
                          m_tiles=(mt_a, mt - mt_a), out_buf=dn1a)
    (dn1,) = _matmul([(ddt_raw, wt_dt)], out_dtypes=[BF16], name="mm_proj_dt_bwd", extras=[dn1a],
                     epilogue=lambda acc, res: (acc + res,), deps=[i2_tok])
    dx, _, dg_mix = _rmsnorm_bwd(dn1, xt, norm_mix_g, dh1, "rmsnorm_mix_bwd")

    big_m = [m_w_in[0].T, m_w_out[0], m_w_gate[0], m_w_up[0], m_w_down[0]]
    big_v = [v_w_in[0].T, v_w_out[0], v_w_gate[0], v_w_up[0], v_w_down[0]]
    big_grads = [None] + [g.reshape(w.shape) for g, w in zip(g1, big[1:])]
    big_out = {}
    for k in range(1, 5):
        big_out[names[k]] = _adamw(big[k], big_grads[k], big_m[k], big_v[k], "adamw_" + names[k], deps=[i2_tok])
    i2_arr = _split_wait("rs2_ici_wait", i2_ss, i2_rs, i2_arr, _build_rs_ici, after=[big_out[names[4]][0], dx])
    g_in_red = _rs_add_chips(i2_arr[1], i2_arr[0], place_arr, "rs_add_chips_w_in")
    s3_ss, s3_rs, s3_arr, s3_tok = _split_start("rs2_share_start", [g_in_red], _build_rs_share, 1)

    dD = jnp.sum(dD_acc.reshape(N_HEADS, HEADDIM), axis=-1)
    heads_row = jnp.concatenate([_rows_to_heads(dbias_acc), _rows_to_heads(dalog_acc), dD,
                                 loss_part.reshape(1)]).reshape(1, -1)
    small = jnp.concatenate([
        dw_ssmconv,
        _pad_cols(dw_sc, D_XBC),
        db_ssmconv,
        jnp.concatenate([dg_mix, dg_ssmnorm], axis=1),
        jnp.concatenate([dg_ffn, dg_final], axis=1),
        _pad_cols(heads_row, D_XBC),
        jnp.zeros((4, D_XBC), F32),
    ], axis=0)
    tot = _allreduce_small(small, deps=[s3_tok])
    loss = tot[19, 3 * N_HEADS]

    cs_ssm, cs_sc = D_XBC // N_CHIPS, D_MODEL // N_CHIPS
    g_ssm_conv = lax.dynamic_slice(tot[0:K_SSM], (0, chip * cs_ssm), (K_SSM, cs_ssm))
    g_sc_conv = lax.dynamic_slice(tot[8:8 + K_SC, :D_MODEL], (0, chip * cs_sc), (K_SC, cs_sc))
    small_grads = {
        "norm_mix_g": tot[17:18, :D_MODEL], "ssm_conv_w": g_ssm_conv, "ssm_conv_b": tot[16:17],
        "ssm_dt_bias": tot[19:20, 0:N_HEADS], "ssm_A_log": tot[19:20, N_HEADS:2 * N_HEADS],
        "ssm_D": tot[19:20, 2 * N_HEADS:3 * N_HEADS], "ssm_norm_g": tot[17:18, D_MODEL:],
        "sc_conv_w": g_sc_conv, "norm_ffn_g": tot[18:19, :D_MODEL], "norm_final_g": tot[18:19, D_MODEL:],
    }
    small_w = {"norm_mix_g": (norm_mix_g, m_norm_mix_g, v_norm_mix_g),
               "ssm_conv_w": (ssm_conv_w[0], m_ssm_conv_w[0], v_ssm_conv_w[0]),
               "ssm_conv_b": (ssm_conv_b, m_ssm_conv_b, v_ssm_conv_b),
               "ssm_dt_bias": (ssm_dt_bias, m_ssm_dt_bias, v_ssm_dt_bias),
               "ssm_A_log": (ssm_A_log, m_ssm_A_log, v_ssm_A_log),
               "ssm_D": (ssm_D, m_ssm_D, v_ssm_D),
               "ssm_norm_g": (ssm_norm_g, m_ssm_norm_g, v_ssm_norm_g),
               "sc_conv_w": (sc_conv_w[0], m_sc_conv_w[0], v_sc_conv_w[0]),
               "norm_ffn_g": (norm_ffn_g, m_norm_ffn_g, v_norm_ffn_g),
               "norm_final_g": (norm_final_g.reshape(1, -1), m_norm_final_g.reshape(1, -1),
                                v_norm_final_g.reshape(1, -1))}
    PW = 1024
    order = list(small_w)

    def pack(arrs):
        rows = []
        for a in arrs:
            flat = a.reshape(-1)
            n = -(-flat.shape[0] // PW) * PW
            rows.append(jnp.pad(flat, (0, n - flat.shape[0])).reshape(-1, PW))
        slab = jnp.concatenate(rows, axis=0)
        return _pad_rows(slab, -(-slab.shape[0] // 8) * 8)

    wp = pack([small_w[k][0] for k in order])
    mp = pack([small_w[k][1] for k in order])
    vp = pack([small_w[k][2] for k in order])
    gp = pack([small_grads[k] for k in order])
    sd, sm, sv = _adamw(wp, gp, mp, vp, "adamw_small")

    def unpack(slab):
        out, row = {}, 0
        for k in order:
            shape = small_w[k][0].shape
            size = 1
            for s in shape:
                size *= s
            nr = -(-size // PW)
            out[k] = slab[row:row + nr].reshape(-1)[:size].reshape(shape)
            row += nr
        return out

    s_delta, s_m, s_v = unpack(sd), unpack(sm), unpack(sv)

    (g_in_full,) = _split_wait("rs2_share_wait", s3_ss, s3_rs, s3_arr, _build_rs_share, after=[sd])
    d_t, m_t, v_t, g_t = _adamw(big[0], g_in_full.reshape(2 * HR_IN, D_MODEL), big_m[0], big_v[0],
                                "adamw_" + names[0], emit_g=True)
    big_grads[0] = g_t.T
    big_out[names[0]] = (d_t.T, m_t.T, v_t.T)
    big_g = dict(zip(names, big_grads))

    weight_order = ["norm_mix_g", "w_in", "ssm_conv_w", "ssm_conv_b", "ssm_dt_bias", "ssm_A_log", "ssm_D",
                    "ssm_norm_g", "sc_conv_w", "w_out", "norm_ffn_g", "w_gate", "w_up", "w_down", "norm_final_g"]
    lead = {"ssm_conv_w", "sc_conv_w", "w_in", "w_out", "w_gate", "w_up", "w_down"}

    def shaped(nm, a):
        if nm == "norm_final_g":
            return a.reshape(D_MODEL)
        return a[None] if nm in lead else a

    grads, deltas, new_m, new_v = [], [], [], []
    for nm in weight_order:
        if nm in big_out:
            g, (d, m, v) = big_g[nm], big_out[nm]
        else:
            g, d, m, v = small_grads[nm], s_delta[nm], s_m[nm], s_v[nm]
        grads.append(shaped(nm, g))
        deltas.append(shaped(nm, d))
        new_m.append(shaped(nm, m))
        new_v.append(shaped(nm, v))
    return (loss, dx[None], *grads, *deltas, *new_m, *new_v)


def _swiglu_bwd(da, dg_factor, du_factor):
    return da * dg_factor.astype(F32), da * du_factor.astype(F32)


def _ffn_fwd(n2, w_gate, w_up):
    T, K = n2.shape
    tn = w_gate.shape[2]
    N = N_CHIPS * tn
    tm = _tile(T, 512)
    sub = _tile(tm, 256)

    def body(a_ref, wg_ref, wu_ref, g_ref, u_ref, act_ref):
        for s in range(tm // sub):
            rows = pl.ds(s * sub, sub)
            a = a_ref[rows, :]
            g = jnp.dot(a, wg_ref[...], preferred_element_type=F32)
            u = jnp.dot(a, wu_ref[...], preferred_element_type=F32)
            sig = _sigmoid(g)
            sg = g * sig
            g_ref[rows, :] = (u * (sig * (1.0 + g - sg))).astype(BF16)
            u_ref[rows, :] = sg.astype(BF16)
            act_ref[rows, :] = (sg * u).astype(BF16)

    a_spec = pl.BlockSpec((tm, K), lambda j, i: (i, 0))
    b_spec = pl.BlockSpec((None, K, tn), lambda j, i: (j, 0, 0))
    o_spec = pl.BlockSpec((tm, tn), lambda j, i: (i, j))
    return pl.pallas_call(
        body, name="ffn_fwd", grid=(N // tn, T // tm),
        in_specs=[a_spec, b_spec, b_spec], out_specs=[o_spec] * 3,
        out_shape=[jax.ShapeDtypeStruct((T, N), BF16)] * 3,
        compiler_params=_cparams(("parallel", "parallel")),
    )(n2, w_gate, w_up)
```

```python
import functools

import jax
import jax.numpy as jnp
from jax import lax
from jax.experimental import pallas as pl
from jax.experimental.pallas import tpu as pltpu

F32 = jnp.float32
BF16 = jnp.bfloat16
MESH = pl.DeviceIdType.MESH

D_MODEL = 2048
D_SSM = 2048
HEADDIM = 64
N_HEADS = 32
N_GROUPS = 8
HEADS_PER_GROUP = 4
N_STATE = 128
CHUNK = 128
K_SSM = 4
K_SC = 3
D_XBC = 4096
D_FF = 5632
D_IN = 12320
D_MAIN = 12288
OFF_XBC, OFF_CB, OFF_CC, OFF_CX = 2048, 6144, 8192, 10240
DT_PAD = 128
EPS = 1e-5
N_CHIPS = 4
N_DEV = 8

ADAM_LR = 0.001
ADAM_B1 = 0.9
ADAM_B2 = 0.999
ADAM_EPS = 1e-08
ADAM_WD = 0.01
ADAM_STEP = 10

V7X_VMEM_BYTES = 64 * 1024 * 1024
VMEM_LIMIT = V7X_VMEM_BYTES - 8 * 1024 * 1024


def _cparams(sem=None):
    if sem is None:
        return pltpu.CompilerParams(vmem_limit_bytes=VMEM_LIMIT)
    return pltpu.CompilerParams(dimension_semantics=sem, vmem_limit_bytes=VMEM_LIMIT)


def _tile(dim, pref, unit=128):
    best = None
    t = unit
    while t <= min(dim, pref):
        if dim % t == 0:
            best = t
        t += unit
    return best if best is not None else dim


def _sigmoid(x):
    return 1.0 / (1.0 + jnp.exp(-x))


def _silu(x):
    return x * _sigmoid(x)


def _dsilu(x):
    s = _sigmoid(x)
    return s * (1.0 + x * (1.0 - s))


def _softplus(x):
    return jnp.maximum(x, 0.0) + jnp.log(1.0 + jnp.exp(-jnp.abs(x)))


MATMUL_VMEM_BUDGET = 44 * 1024 * 1024


def _matmul(pairs, *, ta=False, tb=False, out_dtypes, name, tm=1024, tn=1024, tk=None, extras=(), epilogue=None,
            deps=(), col_shards=False, nsub=1, b3d=False, m_tiles=None, out_buf=None):
    a0, b0 = pairs[0]
    M, K = (a0.shape[1], a0.shape[0]) if ta else a0.shape
    full3d = b3d == "full"
    if full3d:
        N, tk = b0.shape[1], K
    elif b3d:
        N = b0.shape[1] if tb else b0.shape[0] * b0.shape[2]
        tk, tn = (b0.shape[2], tn) if tb else (tk, b0.shape[2])
    else:
        N = b0.shape[0] if tb else b0.shape[1]
    tm, tn = _tile(M, tm, 8 if M % 128 else 128), _tile(N, tn)
    npair, nex, ndep, nout = len(pairs), len(extras), len(deps), len(out_dtypes)
    if tk is None:
        fixed = 2 * tm * tn * (sum(jnp.dtype(d).itemsize for d in out_dtypes) + sum(e.dtype.itemsize for e in extras))
        tk = K
        while tk > 128 and (K % tk or tk % 128 or
                            fixed + 2 * npair * 2 * tk * (tm + tn) + (tm * tn * 4 if tk < K else 0) > MATMUL_VMEM_BUDGET):
            tk -= 128
    else:
        tk = _tile(K, tk)
    nk = K // tk
    if nk > 1 or tm % nsub or (tm // nsub) % 128:
        nsub = 1
    sub = tm // nsub
    dims = (((0 if ta else 1,), (1 if tb else 0,)), ((), ()))
    i0, mi = m_tiles if m_tiles is not None else (0, M // tm)
    nbuf = 0 if out_buf is None else 1

    def body(*refs):
        a_refs = refs[0:2 * npair:2]
        b_refs = refs[1:2 * npair:2]
        ex_refs = refs[2 * npair:2 * npair + nex]
        o_refs = refs[2 * npair + nex + ndep + nbuf:2 * npair + nex + ndep + nbuf + nout]

        def dots(rows):
            s = None
            for a_ref, b_ref in zip(a_refs, b_refs):
                if full3d:
                    kc = b_ref.shape[2]
                    for c3 in range(b_ref.shape[0]):
                        a = a_ref[:, pl.ds(c3 * kc, kc)] if rows is None else a_ref[rows, pl.ds(c3 * kc, kc)]
                        d = lax.dot_general(a, b_ref[c3], dims, preferred_element_type=F32)
                        s = d if s is None else s + d
                    continue
                a = a_ref[...] if rows is None else (a_ref[:, rows] if ta else a_ref[rows, :])
                d = lax.dot_general(a, b_ref[...], dims, preferred_element_type=F32)
                s = d if s is None else s + d
            return s

        def finish(r, rows):
            ex = [e[...] if rows is None else e[rows, :] for e in ex_refs]
            outs = (r,) if epilogue is None else epilogue(r, *ex)
            for o_ref, o in zip(o_refs, outs):
                if rows is None:
                    o_ref[...] = o.astype(o_ref.dtype)
                else:
                    o_ref[rows, :] = o.astype(o_ref.dtype)

        if nk == 1:
            for s in range(nsub):
                rows = None if nsub == 1 else pl.ds(s * sub, sub)
                finish(dots(rows), rows)
            return

        acc = refs[-1]
        k = pl.program_id(2)

        @pl.when(k == 0)
        def _():
            acc[...] = dots(None)

        @pl.when(jnp.logical_and(k > 0, k < nk - 1))
        def _():
            acc[...] += dots(None)

        @pl.when(k == nk - 1)
        def _():
            finish(acc[...] + dots(None), None)

    a_spec = (pl.BlockSpec((tk, tm), lambda i, j, k: (k, i + i0)) if ta
              else pl.BlockSpec((tm, tk), lambda i, j, k: (i + i0, k)))
    if full3d:
        b_spec = pl.BlockSpec((b0.shape[0], tn, b0.shape[2]), lambda i, j, k: (0, j, 0))
    elif b3d:
        b_spec = (pl.BlockSpec((None, tn, tk), lambda i, j, k: (k, j, 0)) if tb
                  else pl.BlockSpec((None, tk, tn), lambda i, j, k: (j, k, 0)))
    else:
        b_spec = (pl.BlockSpec((tn, tk), lambda i, j, k: (j, k)) if tb
                  else pl.BlockSpec((tk, tn), lambda i, j, k: (k, j)))
    e_spec = pl.BlockSpec((tm, tn), lambda i, j, k: (i + i0, j))
    if col_shards:
        o_spec = pl.BlockSpec((None, tm, tn), lambda i, j, k: (j, i + i0, 0))
        o_shape = (N // tn, M, tn)
    else:
        o_spec, o_shape = e_spec, (M, N)
    args, in_specs = [], []
    for a, b in pairs:
        args += [a, b]
        in_specs += [a_spec, b_spec]
    args += list(extras) + list(deps) + ([] if out_buf is None else [out_buf])
    in_specs += [e_spec] * nex + [ANY] * (ndep + nbuf)
    outs = pl.pallas_call(
        body,
        name=name,
        grid=(mi, N // tn, nk),
        in_specs=in_specs,
        out_specs=[o_spec] * nout,
        out_shape=[jax.ShapeDtypeStruct(o_shape, dt) for dt in out_dtypes],
        input_output_aliases={} if out_buf is None else {len(args) - 1: 0},
        scratch_shapes=[pltpu.VMEM((tm, tn), F32)] if nk > 1 else [],
        compiler_params=_cparams(("parallel", "parallel", "arbitrary")),
    )(*args)
    return outs


def _cast_into_gather(w, chip_arr, name, split_cols=False, deps=()):
    R, C = w.shape
    hr, hc = (R, C // 2) if split_cols else (R // 2, C)
    tr = _tile(hr, 512, 8)
    nb = hr // tr

    def body(chip_ref, w_ref, *rest):
        rest[-1][...] = w_ref[...].astype(BF16)

    in_map = (lambda h, i, chip_ref: (i, h)) if split_cols else (lambda h, i, chip_ref: (h * nb + i, 0))
    grid_spec = pltpu.PrefetchScalarGridSpec(
        num_scalar_prefetch=1, grid=(2, nb),
        in_specs=[pl.BlockSpec((tr, hc), in_map)] + [ANY] * len(deps),
        out_specs=pl.BlockSpec((None, tr, hc), lambda h, i, chip_ref: (2 * chip_ref[0] + h, i, 0)))
    return pl.pallas_call(
        body, name=name, grid_spec=grid_spec,
        out_shape=jax.ShapeDtypeStruct((N_DEV, hr, hc), BF16),
        compiler_params=_cparams(("parallel", "parallel")),
    )(chip_arr, w, *deps)


def _tie(small, token, name):
    def body(s_ref, t_ref, o_ref):
        o_ref[...] = s_ref[...]

    vm = pl.BlockSpec(memory_space=pltpu.VMEM)
    return pl.pallas_call(body, name=name, in_specs=[vm, ANY], out_specs=vm,
                          out_shape=jax.ShapeDtypeStruct(small.shape, small.dtype))(small, token)


def _rmsnorm_fwd(x, g, name):
    T, D = x.shape
    tt = _tile(T, 256)

    def body(x_ref, g_ref, n_ref):
        xv = x_ref[...]
        r = lax.rsqrt(jnp.mean(xv * xv, axis=-1, keepdims=True) + EPS)
        n_ref[...] = (xv * r * g_ref[...]).astype(BF16)

    return pl.pallas_call(
        body, name=name, grid=(T // tt,),
        in_specs=[pl.BlockSpec((tt, D), lambda i: (i, 0)), pl.BlockSpec((1, D), lambda i: (0, 0))],
        out_specs=pl.BlockSpec((tt, D), lambda i: (i, 0)),
        out_shape=jax.ShapeDtypeStruct((T, D), BF16),
        compiler_params=_cparams(("parallel",)),
    )(x, g)


def _rmsnorm_bwd(dn, x, g, res, name):
    T, D = x.shape
    tt = _tile(T, 256)

    def body(dn_ref, x_ref, g_ref, res_ref, dx_ref, dxb_ref, dg_ref):
        @pl.when(pl.program_id(0) == 0)
        def _():
            dg_ref[...] = jnp.zeros_like(dg_ref)

        xv = x_ref[...]
        dy = dn_ref[...].astype(F32)
        r = lax.rsqrt(jnp.mean(xv * xv, axis=-1, keepdims=True) + EPS)
        xhat = xv * r
        dxh = dy * g_ref[...]
        dx = res_ref[...] + r * (dxh - xhat * jnp.mean(dxh * xhat, axis=-1, keepdims=True))
        dx_ref[...] = dx
        dxb_ref[...] = dx.astype(BF16)
        dg_ref[...] += jnp.sum(dy * xhat, axis=0, keepdims=True)

    tok = pl.BlockSpec((tt, D), lambda i: (i, 0))
    vec = pl.BlockSpec((1, D), lambda i: (0, 0))
    return pl.pallas_call(
        body, name=name, grid=(T // tt,),
        in_specs=[tok, tok, vec, tok],
        out_specs=[tok, tok, vec],
        out_shape=[jax.ShapeDtypeStruct((T, D), F32), jax.ShapeDtypeStruct((T, D), BF16),
                   jax.ShapeDtypeStruct((1, D), F32)],
        compiler_params=_cparams(("arbitrary",)),
    )(dn, x, g, res)


def _loss_and_final_bwd(h2, target, gf):
    T, D = h2.shape
    tt = _tile(T, 256)

    def body(h_ref, t_ref, g_ref, dh_ref, dhb_ref, dg_ref, loss_ref):
        @pl.when(pl.program_id(0) == 0)
        def _():
            dg_ref[...] = jnp.zeros_like(dg_ref)
            loss_ref[...] = jnp.zeros_like(loss_ref)

        xv = h_ref[...]
        r = lax.rsqrt(jnp.mean(xv * xv, axis=-1, keepdims=True) + EPS)
        xhat = xv * r
        err = xhat * g_ref[...] - t_ref[...]
        loss_ref[...] += 0.5 * jnp.sum(jnp.mean(err * err, axis=-1, keepdims=True), axis=0, keepdims=True)
        dy = err * (1.0 / D)
        dxh = dy * g_ref[...]
        dx = r * (dxh - xhat * jnp.mean(dxh * xhat, axis=-1, keepdims=True))
        dh_ref[...] = dx
        dhb_ref[...] = dx.astype(BF16)
        dg_ref[...] += jnp.sum(dy * xhat, axis=0, keepdims=True)

    tok = pl.BlockSpec((tt, D), lambda i: (i, 0))
    vec = pl.BlockSpec((1, D), lambda i: (0, 0))
    return pl.pallas_call(
        body, name="loss_final_bwd", grid=(T // tt,),
        in_specs=[tok, tok, vec],
        out_specs=[tok, tok, vec, pl.BlockSpec((1, 1), lambda i: (0, 0))],
        out_shape=[jax.ShapeDtypeStruct((T, D), F32), jax.ShapeDtypeStruct((T, D), BF16),
                   jax.ShapeDtypeStruct((1, D), F32), jax.ShapeDtypeStruct((1, 1), F32)],
        compiler_params=_cparams(("arbitrary",)),
    )(h2, target, gf)


def _gated_norm_fwd(y, proj, g):
    T, D = y.shape
    tt = _tile(T, 256)

    def body(y_ref, z_ref, g_ref, o_ref):
        yg = y_ref[...] * _silu(z_ref[...])
        r = lax.rsqrt(jnp.mean(yg * yg, axis=-1, keepdims=True) + EPS)
        o_ref[...] = (yg * r * g_ref[...]).astype(BF16)

    tok = pl.BlockSpec((tt, D), lambda i: (i, 0))
    return pl.pallas_call(
        body, name="gated_norm_fwd", grid=(T // tt,),
        in_specs=[tok, tok, pl.BlockSpec((1, D), lambda i: (0, 0))],
        out_specs=tok,
        out_shape=jax.ShapeDtypeStruct((T, 2 * D_MODEL), BF16),
        compiler_params=_cparams(("parallel",)),
    )(y, proj, g)


def _gated_norm_bwd(dmix, y, proj, g, dproj):
    T, D = y.shape
    tt = _tile(T, 256)

    def body(do_ref, y_ref, z_ref, g_ref, dp_ref, dy_ref, dz_ref, dg_ref):
        @pl.when(pl.program_id(0) == 0)
        def _():
            dg_ref[...] = jnp.zeros_like(dg_ref)

        yv, zv = y_ref[...], z_ref[...]
        do = do_ref[...].astype(F32)
        sz = _silu(zv)
        yg = yv * sz
        r = lax.rsqrt(jnp.mean(yg * yg, axis=-1, keepdims=True) + EPS)
        xhat = yg * r
        dxh = do * g_ref[...]
        dyg = r * (dxh - xhat * jnp.mean(dxh * xhat, axis=-1, keepdims=True))
        dy_ref[...] = dyg * sz
        dz_ref[...] = (dyg * yv * _dsilu(zv)).astype(BF16)
        dg_ref[...] += jnp.sum(do * xhat, axis=0, keepdims=True)

    tok = pl.BlockSpec((tt, D), lambda i: (i, 0))
    vec = pl.BlockSpec((1, D), lambda i: (0, 0))
    return pl.pallas_call(
        body, name="gated_norm_bwd", grid=(T // tt,),
        in_specs=[tok, tok, tok, vec, ANY],
        out_specs=[tok, tok, vec],
        out_shape=[jax.ShapeDtypeStruct((T, D), F32), jax.ShapeDtypeStruct(dproj.shape, BF16),
                   jax.ShapeDtypeStruct((1, D), F32)],
        input_output_aliases={4: 1},
        compiler_params=_cparams(("arbitrary",)),
    )(dmix, y, proj, g, dproj)


HALO = 8


def _shift_down(cur, prev8, s):
    ext = jnp.concatenate([prev8, cur], axis=0)
    return pltpu.roll(ext, s, axis=0)[HALO:]


def _shift_up(cur, next8, s):
    n = cur.shape[0]
    ext = jnp.concatenate([cur, next8], axis=0)
    return pltpu.roll(ext, n + HALO - s, axis=0)[:n]


def _conv_specs(tt, cb, col_off_blocks, nt):
    hb = tt // HALO
    cur = pl.BlockSpec((tt, cb), lambda j, i: (i, col_off_blocks + j))
    prev = pl.BlockSpec((HALO, cb), lambda j, i: (jnp.maximum(i * hb - 1, 0), col_off_blocks + j))
    nxt = pl.BlockSpec((HALO, cb), lambda j, i: (jnp.minimum((i + 1) * hb, nt * hb - 1), col_off_blocks + j))
    return cur, prev, nxt


def _taps(cur, prev8, K):
    return [_shift_down(cur, prev8, K - 1 - k) for k in range(K - 1)] + [cur]


def _conv_of_taps(taps, w):
    y = taps[-1] * w[len(taps) - 1:len(taps), :]
    for k, t in enumerate(taps[:-1]):
        y = y + t * w[k:k + 1, :]
    return y


def _causal_conv(cur, prev8, w, K):
    return _conv_of_taps(_taps(cur, prev8, K), w)


def _anticausal_conv(cur, next8, w, K):
    y = cur * w[K - 1:K, :]
    for k in range(K - 1):
        y = y + _shift_up(cur, next8, K - 1 - k) * w[k:k + 1, :]
    return y


def _ssm_conv_fwd(proj, w8, b):
    T = proj.shape[0]
    tt, cb = _tile(T, 512), 512
    nt = T // tt
    cur, prev, _ = _conv_specs(tt, cb, OFF_XBC // cb, nt)

    def body(u_ref, up_ref, w_ref, b_ref, o_ref):
        first = pl.program_id(1) == 0
        p8 = jnp.where(first, 0.0, up_ref[...])
        pre = _causal_conv(u_ref[...], p8, w_ref[...], K_SSM) + b_ref[...]
        o_ref[...] = _silu(pre)

    return pl.pallas_call(
        body, name="ssm_conv_fwd", grid=(D_XBC // cb, nt),
        in_specs=[cur, prev, pl.BlockSpec((8, cb), lambda j, i: (0, j)), pl.BlockSpec((1, cb), lambda j, i: (0, j))],
        out_specs=pl.BlockSpec((tt, cb), lambda j, i: (i, j)),
        out_shape=jax.ShapeDtypeStruct((T, D_XBC), F32),
        compiler_params=_cparams(("parallel", "parallel")),
    )(proj, proj, w8, b)


def _ssm_conv_bwd(dact, proj, w8, b, dproj):
    T = proj.shape[0]
    tt, cb = _tile(T, 512), 512
    nt = T // tt
    cur, prev, nxt = _conv_specs(tt, cb, OFF_XBC // cb, nt)
    dcur, dprev, dnxt = _conv_specs(tt, cb, 0, nt)

    def dpre_of(d, u, p8, w, bb):
        pre = _causal_conv(u, p8, w, K_SSM) + bb
        return d * _dsilu(pre)

    def body(d_ref, dn_ref, u_ref, up_ref, un_ref, w_ref, b_ref, dp_ref, dx_ref, dw_ref, db_ref):
        i = pl.program_id(1)

        @pl.when(i == 0)
        def _():
            dw_ref[...] = jnp.zeros_like(dw_ref)
            db_ref[...] = jnp.zeros_like(db_ref)

        w, bb = w_ref[...], b_ref[...]
        u = u_ref[...]
        p8 = jnp.where(i == 0, 0.0, up_ref[...])
        taps = _taps(u, p8, K_SSM)
        dpre = d_ref[...] * _dsilu(_conv_of_taps(taps, w) + bb)
        un = un_ref[...]
        dpre_n = dpre_of(dn_ref[...], un, u[tt - HALO:, :], w, bb)
        dpre_n = jnp.where(i == nt - 1, 0.0, dpre_n)
        dx_ref[...] = _anticausal_conv(dpre, dpre_n, w, K_SSM).astype(BF16)
        rows = [jnp.sum(dpre * t, axis=0, keepdims=True) for t in taps]
        rows.append(jnp.zeros((8 - K_SSM, cb), F32))
        dw_ref[...] += jnp.concatenate(rows, axis=0)
        db_ref[...] += jnp.sum(dpre, axis=0, keepdims=True)

    wspec = pl.BlockSpec((8, cb), lambda j, i: (0, j))
    bspec = pl.BlockSpec((1, cb), lambda j, i: (0, j))
    return pl.pallas_call(
        body, name="ssm_conv_bwd", grid=(D_XBC // cb, nt),
        in_specs=[dcur, dnxt, cur, prev, nxt, wspec, bspec, ANY],
        out_specs=[pl.BlockSpec((tt, cb), lambda j, i: (i, OFF_XBC // cb + j)), wspec, bspec],
        out_shape=[jax.ShapeDtypeStruct(dproj.shape, BF16), jax.ShapeDtypeStruct((8, D_XBC), F32),
                   jax.ShapeDtypeStruct((1, D_XBC), F32)],
        input_output_aliases={7: 0},
        compiler_params=_cparams(("parallel", "arbitrary")),
    )(dact, dact, proj, proj, proj, w8, b, dproj)


SCB = 512
SC3 = 3 * SCB


def _sc_specs(tt, nt):
    hb = tt // HALO
    cur = pl.BlockSpec((tt, SC3), lambda j, i: (i, OFF_CB // SC3 + j))
    prev = pl.BlockSpec((HALO, SC3), lambda j, i: (jnp.maximum(i * hb - 1, 0), OFF_CB // SC3 + j))
    nxt = pl.BlockSpec((HALO, SC3), lambda j, i: (jnp.minimum((i + 1) * hb, nt * hb - 1), OFF_CB // SC3 + j))
    return cur, prev, nxt


def _shortconv_fwd(proj, w8, ymix):
    T = proj.shape[0]
    tt = _tile(T, 512)
    nt = T // tt
    cur, prev, _ = _sc_specs(tt, nt)

    def body(p_ref, pp_ref, w_ref, y_ref, o_ref):
        p, pp = p_ref[...], pp_ref[...]
        v = p[:, SCB:2 * SCB] * p[:, 2 * SCB:]
        vp = jnp.where(pl.program_id(1) == 0, 0.0, pp[:, SCB:2 * SCB] * pp[:, 2 * SCB:])
        o_ref[...] = (p[:, :SCB] * _causal_conv(v, vp, w_ref[...], K_SC)).astype(BF16)

    return pl.pallas_call(
        body, name="shortconv_fwd", grid=(D_MODEL // SCB, nt),
        in_specs=[cur, prev, pl.BlockSpec((8, SCB), lambda j, i: (0, j)), ANY],
        out_specs=pl.BlockSpec((tt, SCB), lambda j, i: (i, D_SSM // SCB + j)),
        out_shape=jax.ShapeDtypeStruct(ymix.shape, BF16),
        input_output_aliases={3: 0},
        compiler_params=_cparams(("parallel", "parallel")),
    )(proj, proj, w8, ymix)


def _shortconv_bwd(dmix, proj, w8):
    T = proj.shape[0]
    tt = _tile(T, 512)
    nt = T // tt
    hb = tt // HALO
    cur, prev, nxt = _sc_specs(tt, nt)
    d_s = pl.BlockSpec((tt, SCB), lambda j, i: (i, D_SSM // SCB + j))
    dn_s = pl.BlockSpec((HALO, SCB), lambda j, i: (jnp.minimum((i + 1) * hb, nt * hb - 1), D_SSM // SCB + j))

    def body(d_ref, dn_ref, p_ref, pp_ref, pn_ref, w_ref, dp_ref, dw_ref):
        i = pl.program_id(1)

        @pl.when(i == 0)
        def _():
            dw_ref[...] = jnp.zeros_like(dw_ref)

        w = w_ref[...]
        p, pp = p_ref[...], pp_ref[...]
        gb, gc, u = p[:, :SCB], p[:, SCB:2 * SCB], p[:, 2 * SCB:]
        v = gc * u
        vp = jnp.where(i == 0, 0.0, pp[:, SCB:2 * SCB] * pp[:, 2 * SCB:])
        d = d_ref[...].astype(F32)
        taps = _taps(v, vp, K_SC)
        dp_ref[:, :SCB] = (d * _conv_of_taps(taps, w)).astype(BF16)
        dcv = d * gb
        dcv_n = jnp.where(i == nt - 1, 0.0, dn_ref[...].astype(F32) * pn_ref[:, :SCB])
        dv = _anticausal_conv(dcv, dcv_n, w, K_SC)
        dp_ref[:, SCB:2 * SCB] = (dv * u).astype(BF16)
        dp_ref[:, 2 * SCB:] = (dv * gc).astype(BF16)
        rows = [jnp.sum(dcv * t, axis=0, keepdims=True) for t in taps]
        rows.append(jnp.zeros((8 - K_SC, SCB), F32))
        dw_ref[...] += jnp.concatenate(rows, axis=0)

    wspec = pl.BlockSpec((8, SCB), lambda j, i: (0, j))
    return pl.pallas_call(
        body, name="shortconv_bwd", grid=(D_MODEL // SCB, nt),
        in_specs=[d_s, dn_s, cur, prev, nxt, wspec],
        out_specs=[cur, wspec],
        out_shape=[jax.ShapeDtypeStruct((T, D_MAIN), BF16), jax.ShapeDtypeStruct((8, D_MODEL), F32)],
        compiler_params=_cparams(("parallel", "arbitrary")),
    )(dmix, dmix, proj, proj, proj, w8)


GW = HEADS_PER_GROUP * HEADDIM


def _dot(a, b):
    return jnp.dot(a.astype(BF16), b.astype(BF16), preferred_element_type=F32)


def _dot_nt(a, b):
    return lax.dot_general(a.astype(BF16), b.astype(BF16), (((1,), (1,)), ((), ())), preferred_element_type=F32)


def _dot_tn(a, b):
    return lax.dot_general(a.astype(BF16), b.astype(BF16), (((0,), (0,)), ((), ())), preferred_element_type=F32)


def _bf16_terms(x, n):
    terms, r = [], x
    for _ in range(n):
        t = r.astype(BF16)
        terms.append(t)
        r = r - t.astype(F32)
    return terms


def _dot_sel(a, sel, n=2):
    s = sel.astype(BF16)
    return sum(jnp.dot(t, s, preferred_element_type=F32) for t in _bf16_terms(a, n))


def _sel_dot(sel, b, n=2):
    s = sel.astype(BF16)
    return sum(jnp.dot(s, t, preferred_element_type=F32) for t in _bf16_terms(b, n))


def _sel_dot_nt(sel, b, n=2):
    s = sel.astype(BF16)
    return sum(lax.dot_general(s, t, (((1,), (1,)), ((), ())), preferred_element_type=F32)
               for t in _bf16_terms(b, n))


def _head_cols(rows):
    parts = [jnp.broadcast_to(rows[r:r + 1, :], (HEADDIM, CHUNK)) for r in range(HEADS_PER_GROUP)]
    return jnp.concatenate(parts, axis=0).T


def _head_rows(rows):
    parts = [jnp.broadcast_to(rows[r:r + 1, :], (HEADDIM, N_STATE)) for r in range(HEADS_PER_GROUP)]
    return jnp.concatenate(parts, axis=0)


def _ssd_common(dtr, bias, alog):
    dt = _softplus(dtr + bias)
    A = -jnp.exp(alog)
    a = dt * A
    ki = lax.broadcasted_iota(jnp.int32, (CHUNK, CHUNK), 0)
    si = lax.broadcasted_iota(jnp.int32, (CHUNK, CHUNK), 1)
    upper = (ki <= si).astype(F32)
    cs = _dot_sel(a, upper, 3)
    cs_last = jnp.broadcast_to(cs[:, CHUNK - 1:CHUNK], (8, CHUNK))
    return dt, A, a, cs, cs_last


def _decay_matrix(cs, r):
    li = lax.broadcasted_iota(jnp.int32, (CHUNK, CHUNK), 0)
    si = lax.broadcasted_iota(jnp.int32, (CHUNK, CHUNK), 1)
    causal = li >= si
    R = jnp.broadcast_to(cs[r:r + 1, :], (CHUNK, CHUNK))
    seg = jnp.where(causal, R.T - R, 0.0)
    return jnp.where(causal, jnp.exp(seg), 0.0)


def _decay_cat(cs):
    return jnp.concatenate([_decay_matrix(cs, r) for r in range(HEADS_PER_GROUP)], axis=1)


def _lanes4(m):
    return jnp.concatenate([m] * HEADS_PER_GROUP, axis=1)


def _head_blocks(v):
    col = lax.broadcasted_iota(jnp.int32, v.shape, 1) // HEADDIM
    return jnp.concatenate([jnp.where(col == r, v, jnp.zeros_like(v)) for r in range(HEADS_PER_GROUP)], axis=0)


GXBC = GW + 2 * N_STATE


GS = 4


def _ssd_in_specs(nc, rev):
    cix = (lambda c: nc - 1 - c) if rev else (lambda c: c)
    x_s = pl.BlockSpec((CHUNK, GS * GW), lambda g, c: (cix(c), g))
    xbc_s = pl.BlockSpec((CHUNK, GS * GXBC), lambda g, c: (cix(c), g))
    dtr_s = pl.BlockSpec((GS, 8, CHUNK), lambda g, c: (g, 0, cix(c)))
    row_s = pl.BlockSpec((GS, 8, CHUNK), lambda g, c: (g, 0, 0))
    drep_s = pl.BlockSpec((1, GS * GW), lambda g, c: (0, g))
    hs_s = pl.BlockSpec((1, GS * GW, N_STATE), lambda g, c: (cix(c), g, 0))
    return x_s, xbc_s, dtr_s, row_s, drep_s, hs_s


def _xbc_parts(xbc_ref, gi):
    o = gi * GXBC
    return xbc_ref[:, o:o + GW], xbc_ref[:, o + GW:o + GW + N_STATE], xbc_ref[:, o + GW + N_STATE:o + GXBC]


def _ssd_fwd(xbc, dtr, bias, alog, drep):
    T = xbc.shape[0]
    nc = T // CHUNK
    x_s, xbc_s, dtr_s, row_s, drep_s, hs_s = _ssd_in_specs(nc, False)

    def body(xbc_ref, dtr_ref, bias_ref, alog_ref, drep_ref, y_ref, hs_ref, h_scr):
        @pl.when(pl.program_id(1) == 0)
        def _():
            h_scr[...] = jnp.zeros_like(h_scr)

        for gi in range(GS):
            cols, rows = slice(gi * GW, (gi + 1) * GW), pl.ds(gi * GW, GW)
            x, Bm, Cm = _xbc_parts(xbc_ref, gi)
            dt, A, a, cs, cs_last = _ssd_common(dtr_ref[gi], bias_ref[gi], alog_ref[gi])
            E = _head_cols(jnp.exp(cs))
            W = _head_cols(jnp.exp(cs_last - cs) * dt)
            X = (x * _head_cols(dt)).astype(BF16)
            CB = _dot_nt(Cm, Bm)
            y = jnp.dot((_lanes4(CB) * _decay_cat(cs)).astype(BF16), _head_blocks(X), preferred_element_type=F32)
            h = h_scr[rows, :]
            hs_ref[0, rows, :] = h
            y = y + _dot_nt(Cm, h) * E
            y_ref[:, cols] = y + drep_ref[:, cols] * x
            h_scr[rows, :] = h * _head_rows(jnp.exp(cs_last)) + _dot_tn(x * W, Bm)

    return pl.pallas_call(
        body, name="ssd_fwd", grid=(N_GROUPS // GS, nc),
        in_specs=[xbc_s, dtr_s, row_s, row_s, drep_s],
        out_specs=[x_s, hs_s],
        out_shape=[jax.ShapeDtypeStruct((T, D_SSM), F32), jax.ShapeDtypeStruct((nc, D_SSM, N_STATE), F32)],
        scratch_shapes=[pltpu.VMEM((GS * GW, N_STATE), F32)],
        compiler_params=_cparams(("parallel", "arbitrary")),
    )(xbc, dtr, bias, alog, drep)


def _ssd_bwd(xbc, dtr, bias, alog, drep, dy, hs):
    T = xbc.shape[0]
    nc = T // CHUNK
    x_s, xbc_s, dtr_s, row_s, drep_s, hs_s = _ssd_in_specs(nc, True)

    def body(xbc_ref, dtr_ref, bias_ref, alog_ref, drep_ref, dy_ref, hs_ref,
             dxbc_ref, ddtr_ref, dbias_ref, dalog_ref, dd_ref, dh_scr):
        @pl.when(pl.program_id(1) == 0)
        def _():
            dh_scr[...] = jnp.zeros_like(dh_scr)
            dbias_ref[...] = jnp.zeros_like(dbias_ref)
            dalog_ref[...] = jnp.zeros_like(dalog_ref)
            dd_ref[...] = jnp.zeros_like(dd_ref)

        for gi in range(GS):
            one_group(gi, xbc_ref, dtr_ref, bias_ref, alog_ref, drep_ref, dy_ref, hs_ref,
                      dxbc_ref, ddtr_ref, dbias_ref, dalog_ref, dd_ref, dh_scr)

    def one_group(gi, xbc_ref, dtr_ref, bias_ref, alog_ref, drep_ref, dy_ref, hs_ref,
                  dxbc_ref, ddtr_ref, dbias_ref, dalog_ref, dd_ref, dh_scr):
        cols, rows, o = slice(gi * GW, (gi + 1) * GW), pl.ds(gi * GW, GW), gi * GXBC
        x, Bm, Cm = _xbc_parts(xbc_ref, gi)
        dY = dy_ref[:, cols]
        dt, A, a, cs, cs_last = _ssd_common(dtr_ref[gi], bias_ref[gi], alog_ref[gi])
        E = _head_cols(jnp.exp(cs))
        DT = _head_cols(dt)
        Wd = _head_cols(jnp.exp(cs_last - cs))
        X = x * DT
        h = hs_ref[0, rows, :]
        dS = dh_scr[rows, :]
        CB = _dot_nt(Cm, Bm)
        rowid = lax.broadcasted_iota(jnp.int32, (8, CHUNK), 0)
        lane = lax.broadcasted_iota(jnp.int32, (8, CHUNK), 1)
        hsel = (lax.broadcasted_iota(jnp.int32, (8, GW), 1) // HEADDIM
                == lax.broadcasted_iota(jnp.int32, (8, GW), 0)).astype(F32)
        hsel_l = (lax.broadcasted_iota(jnp.int32, (8, HEADS_PER_GROUP * CHUNK), 1) // CHUNK
                  == lax.broadcasted_iota(jnp.int32, (8, HEADS_PER_GROUP * CHUNK), 0)).astype(F32)

        Lc, CBc = _decay_cat(cs), _lanes4(CB)
        Mc = CBc * Lc
        GLc = _dot_nt(dY, _head_blocks(X.astype(BF16))) * Lc
        Wc = GLc * CBc
        colsum = jnp.sum(Wc, axis=0, keepdims=True)
        dcs = _sel_dot_nt(hsel_l, Wc)
        dCB = jnp.zeros((CHUNK, CHUNK), F32)
        for r in range(HEADS_PER_GROUP):
            blk = slice(r * CHUNK, (r + 1) * CHUNK)
            dCB = dCB + GLc[:, blk]
            dcs = dcs - jnp.where(rowid == r, colsum[:, blk], 0.0)
        m_stack = jnp.concatenate([Mc[:, r * CHUNK:(r + 1) * CHUNK].astype(BF16) for r in range(HEADS_PER_GROUP)],
                                  axis=0)
        dX = lax.dot_general(m_stack, _head_blocks(dY.astype(BF16)), (((0,), (0,)), ((), ())),
                             preferred_element_type=F32)
        dC = _dot(dCB, Bm)
        dB = _dot_tn(dCB, Cm)
        T1 = _dot_nt(Bm, dS)
        dX = dX + T1 * Wd
        dB = dB + _dot(X * Wd, dS)
        pdec = _sel_dot_nt(hsel, X * T1 * Wd)
        dcs = dcs - pdec
        dlast = jnp.sum(pdec, axis=1, keepdims=True) \
            + jnp.exp(cs_last[:, 0:1]) * jnp.sum(_sel_dot(hsel, dS * h), axis=1, keepdims=True)
        dYE = dY * E
        dC = dC + _dot(dYE, h)
        yoff = _dot_nt(Cm, h) * E
        dcs = dcs + _sel_dot_nt(hsel, dY * yoff)
        dcs = dcs + jnp.where(lane == CHUNK - 1, dlast, 0.0)
        ki = lax.broadcasted_iota(jnp.int32, (CHUNK, CHUNK), 0)
        si = lax.broadcasted_iota(jnp.int32, (CHUNK, CHUNK), 1)
        lower = (ki >= si).astype(F32)
        da = _dot_sel(dcs, lower)
        ddt = da * A + _sel_dot_nt(hsel, dX * x)
        ddtr = ddt * _sigmoid(dtr_ref[gi] + bias_ref[gi])
        ddtr_ref[gi] = ddtr
        dbias_ref[gi] += ddtr
        dalog_ref[gi] += da * a
        dxbc_ref[:, o:o + GW] = dX * DT + drep_ref[:, cols] * dY
        dd_ref[:, cols] += jnp.sum(dY * x, axis=0, keepdims=True)
        dxbc_ref[:, o + GW:o + GW + N_STATE] = dB
        dxbc_ref[:, o + GW + N_STATE:o + GXBC] = dC
        dh_scr[rows, :] = dS * _head_rows(jnp.exp(cs_last)) + _dot_tn(dYE, Cm)

    return pl.pallas_call(
        body, name="ssd_bwd", grid=(N_GROUPS // GS, nc),
        in_specs=[xbc_s, dtr_s, row_s, row_s, drep_s, x_s, hs_s],
        out_specs=[xbc_s, dtr_s, row_s, row_s, drep_s],
        out_shape=[jax.ShapeDtypeStruct((T, D_XBC), F32),
                   jax.ShapeDtypeStruct((N_GROUPS, 8, T), F32),
                   jax.ShapeDtypeStruct((N_GROUPS, 8, CHUNK), F32),
                   jax.ShapeDtypeStruct((N_GROUPS, 8, CHUNK), F32),
                   jax.ShapeDtypeStruct((1, D_SSM), F32)],
        scratch_shapes=[pltpu.VMEM((GS * GW, N_STATE), F32)],
        compiler_params=_cparams(("parallel", "arbitrary")),
    )(xbc, dtr, bias, alog, drep, dy, hs)


def _adamw(w, g, m, v, name, deps=(), emit_g=False):
    R, C = w.shape
    tr = _tile(R, 256, 8)
    nd = len(deps)
    nout = 4 if emit_g else 3

    def body(w_ref, g_ref, m_ref, v_ref, *rest):
        outs = rest[nd:]
        gv = g_ref[...]
        mn = ADAM_B1 * m_ref[...] + (1.0 - ADAM_B1) * gv
        vn = ADAM_B2 * v_ref[...] + (1.0 - ADAM_B2) * (gv * gv)
        m_hat = mn / (1.0 - ADAM_B1 ** ADAM_STEP)
        v_hat = vn / (1.0 - ADAM_B2 ** ADAM_STEP)
        outs[0][...] = -ADAM_LR * (m_hat / (jnp.sqrt(v_hat) + ADAM_EPS) + ADAM_WD * w_ref[...])
        outs[1][...] = mn
        outs[2][...] = vn
        if emit_g:
            outs[3][...] = gv

    spec = pl.BlockSpec((tr, C), lambda i: (i, 0))
    return pl.pallas_call(
        body, name=name, grid=(R // tr,),
        in_specs=[spec] * 4 + [ANY] * nd, out_specs=[spec] * nout,
        out_shape=[jax.ShapeDtypeStruct((R, C), F32)] * nout,
        compiler_params=_cparams(("parallel",)),
    )(w, g, m, v, *deps)


ANY = pl.BlockSpec(memory_space=pl.ANY)


def _place():
    x, y, c = lax.axis_index("x"), lax.axis_index("y"), lax.axis_index("c")
    return x, y, c


def _other_chips(x, y):
    return [(1 - x, y), (x, 1 - y), (1 - x, 1 - y)]


def _allgather_inplace(bufs, splits, first_done=False):
    n = len(bufs)

    def body(*refs):
        o_refs = refs[n:2 * n]
        send_sems, recv_sems = refs[2 * n:]
        x, y, c = _place()
        xn, yn, dg, sibling = (1 - x, y), (x, 1 - y), (1 - x, 1 - y), (x, y, 1 - c)

        def blk(k, chip, pc):
            return o_refs[k].at[4 * chip[0] + 2 * chip[1] + pc]

        def part(k, ref, p):
            kind, s = splits[k]
            _, R, C = bufs[k].shape
            if kind == "rows":
                return ref.at[pl.ds(0, s)] if p == 0 else ref.at[pl.ds(s, R - s)]
            return ref.at[:, pl.ds(0, s)] if p == 0 else ref.at[:, pl.ds(s, C - s)]

        def copy(k, slot, ref, to):
            return pltpu.make_async_remote_copy(
                src_ref=ref, dst_ref=ref, send_sem=send_sems.at[k, slot], recv_sem=recv_sems.at[k, slot],
                device_id=to, device_id_type=MESH)

        sent = []

        def send(k, slot, ref, to):
            cp = copy(k, slot, ref, to)
            cp.start()
            sent.append(cp)

        if not first_done:
            for k in range(n):
                send(k, 0, blk(k, (x, y), c), (*xn, c))
                send(k, 1, blk(k, (x, y), c), (*yn, c))
        for k in range(n):
            bx, by = blk(k, xn, c), blk(k, yn, c)
            if not first_done:
                copy(k, 0, bx, sibling).wait_recv()
            send(k, 2, part(k, bx, 0), (*yn, c))
            send(k, 4, bx, sibling)
            if not first_done:
                copy(k, 1, by, sibling).wait_recv()
            send(k, 3, part(k, by, 1), (*xn, c))
            send(k, 5, by, sibling)
        for k in range(n):
            d0, d1 = part(k, blk(k, dg, c), 0), part(k, blk(k, dg, c), 1)
            copy(k, 2, d0, sibling).wait_recv()
            send(k, 6, d0, sibling)
            copy(k, 3, d1, sibling).wait_recv()
            send(k, 7, d1, sibling)
        for k in range(n):
            copy(k, 4, blk(k, xn, 1 - c), sibling).wait_recv()
            copy(k, 5, blk(k, yn, 1 - c), sibling).wait_recv()
            copy(k, 6, part(k, blk(k, dg, 1 - c), 0), sibling).wait_recv()
            copy(k, 7, part(k, blk(k, dg, 1 - c), 1), sibling).wait_recv()
        for cp in sent:
            cp.wait_send()

    return pl.pallas_call(
        body, name="allgather_w_in",
        in_specs=[ANY] * n, out_specs=[ANY] * n,
        out_shape=[jax.ShapeDtypeStruct(b.shape, b.dtype) for b in bufs],
        input_output_aliases={k: k for k in range(n)},
        scratch_shapes=[pltpu.SemaphoreType.DMA((n, 8)), pltpu.SemaphoreType.DMA((n, 8))],
    )(*bufs)


HBM = pl.BlockSpec(memory_space=pltpu.HBM)
SEM = pl.BlockSpec(memory_space=pltpu.SEMAPHORE)
EFFECT = pltpu.SideEffectType.DATAFLOW_SIDE_EFFECTING


def _split_start(name, arrays, build, n_copies, after=()):
    na, nd = len(arrays), len(after)

    def body(*refs):
        send_sems, recv_sems = refs[na + nd], refs[na + nd + 1]
        for cp in build(refs[:na], send_sems, recv_sems):
            cp.start()
        refs[-1][...] = jnp.zeros((8, 128), F32)

    outs = pl.pallas_call(
        body, name=name,
        out_shape=(pltpu.SemaphoreType.DMA((n_copies,)), pltpu.SemaphoreType.DMA((n_copies,)),
                   *[pltpu.HBM(a.shape, a.dtype) for a in arrays], jax.ShapeDtypeStruct((8, 128), F32)),
        in_specs=[HBM] * na + [ANY] * nd,
        out_specs=(SEM, SEM, *[HBM] * na, pl.BlockSpec(memory_space=pltpu.VMEM)),
        input_output_aliases={i: 2 + i for i in range(na)},
        compiler_params=pltpu.CompilerParams(has_side_effects=EFFECT),
    )(*[pltpu.with_memory_space_constraint(a, pltpu.HBM) for a in arrays], *after)
    return outs[0], outs[1], list(outs[2:2 + na]), outs[-1]


def _split_wait(name, send_sems, recv_sems, arrays, build, after):
    na = len(arrays)

    def body(*refs):
        for cp in build(refs[:na], refs[na], refs[na + 1]):
            cp.wait_send()
            cp.wait_recv()

    outs = pl.pallas_call(
        body, name=name,
        out_shape=tuple(pltpu.HBM(a.shape, a.dtype) for a in arrays),
        in_specs=[HBM] * na + [SEM, SEM] + [ANY] * len(after),
        out_specs=tuple([HBM] * na),
        input_output_aliases={i: i for i in range(na)},
        compiler_params=pltpu.CompilerParams(has_side_effects=EFFECT),
    )(*arrays, send_sems, recv_sems, *after)
    return list(outs)


def _remote(src, dst, send_sems, recv_sems, i, to):
    return pltpu.make_async_remote_copy(src_ref=src, dst_ref=dst, send_sem=send_sems.at[i], recv_sem=recv_sems.at[i],
                                        device_id=to, device_id_type=MESH)


def _build_ag_first(refs, ss, rs):
    x, y, c = _place()
    cps = []
    for k, ref in enumerate(refs):
        blk = ref.at[4 * x + 2 * y + c]
        cps += [_remote(blk, blk, ss, rs, 2 * k, (1 - x, y, c)), _remote(blk, blk, ss, rs, 2 * k + 1, (x, 1 - y, c))]
    return cps


def _build_ag_ici(refs, ss, rs):
    x, y, c = _place()
    cps = []
    for k, ref in enumerate(refs):
        blk = ref.at[4 * x + 2 * y + c]
        for j, (px, py) in enumerate(_other_chips(x, y)):
            cps.append(_remote(blk, blk, ss, rs, 3 * k + j, (px, py, c)))
    return cps


def _build_ag_fwd(refs, ss, rs):
    x, y, c = _place()
    cps = []
    for k, ref in enumerate(refs):
        for j, (px, py) in enumerate(_other_chips(x, y)):
            blk = ref.at[4 * px + 2 * py + c]
            cps.append(_remote(blk, blk, ss, rs, 3 * k + j, (x, y, 1 - c)))
    return cps


def _build_rs_swap(refs, ss, rs):
    x, y, c = _place()
    n = len(refs) // 2
    return [_remote(refs[k].at[:, pl.ds(1 - c, 1)], refs[n + k], ss, rs, k, (x, y, 1 - c)) for k in range(n)]


def _build_rs_ici(refs, ss, rs):
    x, y, c = _place()
    n = len(refs) // 2
    me = 2 * x + y
    cps = []
    for k in range(n):
        for j, (px, py) in enumerate(_other_chips(x, y)):
            cps.append(_remote(refs[k].at[2 * px + py], refs[n + k].at[me], ss, rs, 3 * k + j, (px, py, c)))
    return cps


def _build_rs_share(refs, ss, rs):
    x, y, c = _place()
    return [_remote(ref.at[c], ref.at[c], ss, rs, k, (x, y, 1 - c)) for k, ref in enumerate(refs)]


def _allreduce_small(p, deps=()):
    R, C = p.shape
    nd = len(deps)

    def body(p_ref, *rest):
        gath_ref, sum_ref, send_sems, recv_sems, local_sem = rest[nd:]
        x, y, c = _place()
        me, sibling = (x, y, c), (x, y, 1 - c)
        chips = [(1 - x, y), (x, 1 - y), (1 - x, 1 - y)]

        def blk(px, py, pc):
            return gath_ref.at[4 * px + 2 * py + pc]

        def copy(k, block, to, src=None):
            return pltpu.make_async_remote_copy(
                src_ref=blk(*block) if src is None else src, dst_ref=blk(*block),
                send_sem=send_sems.at[k], recv_sem=recv_sems.at[k], device_id=to, device_id_type=MESH)

        mine = pltpu.make_async_copy(p_ref, blk(*me), local_sem)
        mine.start()
        first = [copy(0, me, sibling, src=p_ref)]
        first += [copy(1 + j, me, (*chip, c), src=p_ref) for j, chip in enumerate(chips)]
        for cp in first:
            cp.start()
        passed = [copy(4 + j, (*chip, c), sibling) for j, chip in enumerate(chips)]
        for j, chip in enumerate(chips):
            copy(1 + j, (*chip, c), me).wait_recv()
            passed[j].start()
        copy(0, sibling, me).wait_recv()
        for j, chip in enumerate(chips):
            copy(4 + j, (*chip, 1 - c), me).wait_recv()
        for cp in first + passed:
            cp.wait_send()
        mine.wait()
        s = gath_ref[0]
        for d in range(1, N_DEV):
            s = s + gath_ref[d]
        sum_ref[...] = s

    vm = pl.BlockSpec(memory_space=pltpu.VMEM)
    return pl.pallas_call(
        body, name="allreduce_small",
        in_specs=[vm] + [ANY] * nd, out_specs=[vm, vm],
        out_shape=[jax.ShapeDtypeStruct((N_DEV, R, C), F32), jax.ShapeDtypeStruct((R, C), F32)],
        scratch_shapes=[pltpu.SemaphoreType.DMA((7,)), pltpu.SemaphoreType.DMA((7,)), pltpu.SemaphoreType.DMA],
    )(p, *deps)[1]


def _rs_add_pair(p, r0, c_arr, name):
    _, _, hr, cols = p.shape
    tr = _tile(hr, 256, 8)

    def body(c_ref, p_ref, r_ref, q_ref):
        q_ref[...] = (p_ref[0] + r_ref[0]).astype(BF16)

    grid_spec = pltpu.PrefetchScalarGridSpec(
        num_scalar_prefetch=1, grid=(N_CHIPS, hr // tr),
        in_specs=[pl.BlockSpec((1, 1, tr, cols), lambda j, i, c_ref: (j, c_ref[0], i, 0)),
                  pl.BlockSpec((1, 1, tr, cols), lambda j, i, c_ref: (j, 0, i, 0))],
        out_specs=pl.BlockSpec((1, tr, cols), lambda j, i, c_ref: (j, i, 0)))
    return pl.pallas_call(
        body, name=name, grid_spec=grid_spec,
        out_shape=jax.ShapeDtypeStruct((N_CHIPS, hr, cols), BF16),
        compiler_params=_cparams(("parallel", "parallel")),
    )(c_arr, p, r0)


def _rs_add_chips(r1, q, place_arr, name):
    _, hr, cols = r1.shape
    tr = _tile(hr, 256, 8)

    def body(place_ref, r_ref, q_ref, o_ref):
        chip = place_ref[0]
        s = None
        for j in range(N_CHIPS):
            t = jnp.where(chip == j, q_ref[j], r_ref[j]).astype(F32)
            s = t if s is None else s + t
        o_ref[...] = s

    blk = pl.BlockSpec((N_CHIPS, tr, cols), lambda i, place_ref: (0, i, 0))
    grid_spec = pltpu.PrefetchScalarGridSpec(
        num_scalar_prefetch=1, grid=(hr // tr,), in_specs=[blk, blk],
        out_specs=pl.BlockSpec((None, tr, cols), lambda i, place_ref: (place_ref[1], i, 0)))
    return pl.pallas_call(
        body, name=name, grid_spec=grid_spec,
        out_shape=jax.ShapeDtypeStruct((2, hr, cols), F32),
        compiler_params=_cparams(("parallel",)),
    )(place_arr, r1, q)


def _pad_rows(a, rows):
    return jnp.pad(a, ((0, rows - a.shape[0]), (0, 0)))


def _pad_cols(a, cols):
    return jnp.pad(a, ((0, 0), (0, cols - a.shape[1])))


def _heads_to_rows(v):
    v = v.reshape(N_GROUPS, HEADS_PER_GROUP, 1)
    v = jnp.pad(v, ((0, 0), (0, 8 - HEADS_PER_GROUP), (0, 0)))
    return jnp.broadcast_to(v, (N_GROUPS, 8, CHUNK))


def _rows_to_heads(a):
    return jnp.sum(a[:, :HEADS_PER_GROUP, :], axis=-1).reshape(N_HEADS)


def _to_kernel_rows(a):
    C = a.shape[1]
    x0, b0, c0, s0 = D_SSM, 2 * D_SSM, 2 * D_SSM + 1024, D_SSM + D_XBC + N_HEADS
    xbc = jnp.concatenate([a[x0:b0].reshape(N_GROUPS, GW, C), a[b0:c0].reshape(N_GROUPS, N_STATE, C),
                           a[c0:c0 + 1024].reshape(N_GROUPS, N_STATE, C)], axis=1).reshape(D_XBC, C)
    sc = jnp.concatenate([a[s0 + k * D_MODEL:s0 + (k + 1) * D_MODEL].reshape(D_MODEL // SCB, SCB, C)
                          for k in range(3)], axis=1).reshape(3 * D_MODEL, C)
    return jnp.concatenate([a[:D_SSM], xbc, sc], axis=0)


HR_IN = 1568


def _shard_row_plan():
    segs = [(0, 0, 0, D_SSM)]
    for g in range(N_GROUPS):
        k0 = D_SSM + g * GXBC
        segs += [(0, k0, D_SSM + g * GW, GW), (0, k0 + GW, 2 * D_SSM + g * N_STATE, N_STATE),
                 (0, k0 + GW + N_STATE, 2 * D_SSM + 1024 + g * N_STATE, N_STATE)]
    segs.append((1, 0, D_SSM + D_XBC, N_HEADS))
    for j in range(D_MODEL // SCB):
        for k in range(3):
            segs.append((0, D_SSM + D_XBC + j * SC3 + k * SCB, D_SSM + D_XBC + N_HEADS + k * D_MODEL + j * SCB, SCB))
    cs = D_IN // N_CHIPS
    plan = []
    for src, s, o, n in segs:
        while n > 0:
            chip, loc = divmod(o, cs)
            half, row = divmod(loc, HR_IN)
            m = min(n, cs - loc, HR_IN - row)
            plan.append((src, s, chip, half, row, m))
            s, o, n = s + m, o + m, n - m
    return plan


SCATTER_ROWS = 512
SCATTER_SLOTS = 4


def _scatter_rows_to_shards(k_main, k_dt):
    C = k_main.shape[1]
    pieces = []
    for src, s, chip, half, row, n in _shard_row_plan():
        for o in range(0, n, SCATTER_ROWS):
            pieces.append((src, s + o, chip, half, row + o, min(SCATTER_ROWS, n - o)))
    S, lag, N = SCATTER_SLOTS, SCATTER_SLOTS // 2, len(pieces)

    def body(m_ref, d_ref, o_ref, buf, in_sems, out_sems):
        def cin(i):
            src, s, _, _, _, n = pieces[i]
            return pltpu.make_async_copy((d_ref if src else m_ref).at[pl.ds(s, n)],
                                         buf.at[i % S, pl.ds(0, n)], in_sems.at[i % S])

        def cout(i):
            _, _, chip, half, row, n = pieces[i]
            return pltpu.make_async_copy(buf.at[i % S, pl.ds(0, n)],
                                         o_ref.at[chip, half, pl.ds(row, n)], out_sems.at[i % S])

        for i in range(N + lag):
            if i < N:
                if i >= S:
                    cout(i - S).wait()
                cin(i).start()
            j = i - lag
            if 0 <= j < N:
                cin(j).wait()
                cout(j).start()
        for j in range(max(0, N - S), N):
            cout(j).wait()

    return pl.pallas_call(
        body, name="scatter_dw_in_rows", in_specs=[ANY, ANY], out_specs=ANY,
        out_shape=jax.ShapeDtypeStruct((N_CHIPS, 2, HR_IN, C), k_main.dtype),
        scratch_shapes=[pltpu.VMEM((S, SCATTER_ROWS, C), k_main.dtype),
                        pltpu.SemaphoreType.DMA((S,)), pltpu.SemaphoreType.DMA((S,))],
        compiler_params=_cparams(),
    )(k_main, k_dt)


def _to_kernel_xbc(a):
    R = a.shape[0]
    return jnp.concatenate([a[:, :D_SSM].reshape(R, N_GROUPS, GW), a[:, D_SSM:D_SSM + 1024].reshape(R, N_GROUPS, N_STATE),
                            a[:, D_SSM + 1024:].reshape(R, N_GROUPS, N_STATE)], axis=2).reshape(R, D_XBC)


def _from_kernel_xbc(a):
    R = a.shape[0]
    g = a.reshape(R, N_GROUPS, GXBC)
    return jnp.concatenate([g[:, :, :GW].reshape(R, D_SSM), g[:, :, GW:GW + N_STATE].reshape(R, 1024),
                            g[:, :, GW + N_STATE:].reshape(R, 1024)], axis=1)


def kernel(x, norm_mix_g, w_in, ssm_conv_w, ssm_conv_b, ssm_dt_bias, ssm_A_log, ssm_D, ssm_norm_g, sc_conv_w, w_out, norm_ffn_g, w_gate, w_up, w_down, norm_final_g, loss_target, m_norm_mix_g, m_w_in, m_ssm_conv_w, m_ssm_conv_b, m_ssm_dt_bias, m_ssm_A_log, m_ssm_D, m_ssm_norm_g, m_sc_conv_w, m_w_out, m_norm_ffn_g, m_w_gate, m_w_up, m_w_down, m_norm_final_g, v_norm_mix_g, v_w_in, v_ssm_conv_w, v_ssm_conv_b, v_ssm_dt_bias, v_ssm_A_log, v_ssm_D, v_ssm_norm_g, v_sc_conv_w, v_w_out, v_norm_ffn_g, v_w_gate, v_w_up, v_w_down, v_norm_final_g):
    T = x.shape[1]
    xt = x[0]
    tgt = loss_target[0]
    cx, cy, cc = lax.axis_index("x"), lax.axis_index("y"), lax.axis_index("c")
    chip = 2 * cx + cy
    c_arr = jnp.reshape(cc, (1,)).astype(jnp.int32)
    chip_arr = jnp.reshape(chip, (1,)).astype(jnp.int32)
    place_arr = jnp.stack([chip, cc]).astype(jnp.int32)

    big = [w_in[0].T, w_out[0], w_gate[0], w_up[0], w_down[0]]
    names = ["w_in", "w_out", "w_gate", "w_up", "w_down"]
    gb_in = _cast_into_gather(big[0], chip_arr, "cast_w_in", split_cols=True)
    cs_in, cs_conv = D_IN // N_CHIPS, D_XBC // N_CHIPS
    cw = jnp.stack([_pad_rows(ssm_conv_w[0], 8), _pad_cols(_pad_rows(sc_conv_w[0], 8), cs_conv)])
    cw_buf = lax.dynamic_update_slice(jnp.zeros((N_DEV, 8, cs_conv), F32), cw, (2 * chip, 0, 0))
    f_ss, f_rs, f_arr, f_tok = _split_start("ag_in_first_start", [gb_in, cw_buf], _build_ag_first, 4)
    gbufs = [None] + [_cast_into_gather(w, chip_arr, "cast_" + nm, deps=[f_tok]) for w, nm in zip(big[1:], names[1:])]
    n1 = _rmsnorm_fwd(xt, _tie(norm_mix_g, f_tok, "tie_ag_first"), "rmsnorm_mix")
    f_arr = _split_wait("ag_in_first_wait", f_ss, f_rs, f_arr, _build_ag_first, after=gbufs[1:] + [n1])
    g_in, cw_all = _allgather_inplace(f_arr, [("rows", (cs_in // 32) * 16), ("cols", cs_conv // 2)], first_done=True)
    cw_all = cw_all.reshape(N_CHIPS, 2, 8, cs_conv)
    ssm_w8 = _to_kernel_xbc(cw_all[:, 0].transpose(1, 0, 2).reshape(8, D_XBC))
    sc_w8 = cw_all[:, 1, :, :D_MODEL // N_CHIPS].transpose(1, 0, 2).reshape(8, D_MODEL)
    ssm_bk = _to_kernel_xbc(ssm_conv_b)
    g4 = g_in.reshape(N_CHIPS, 2, cs_in, D_MODEL // 2)
    wt_h = [g4[:, h].reshape(D_IN, D_MODEL // 2) for h in range(2)]
    wt_main = jnp.stack([_to_kernel_rows(w) for w in wt_h])
    wt_dt = _pad_rows(jnp.concatenate([w[D_SSM + D_XBC:D_SSM + D_XBC + N_HEADS] for w in wt_h], axis=1), DT_PAD)
    ag_ss, ag_rs, ag_bufs, ag_tok = _split_start("ag_ici_start", gbufs[1:], _build_ag_ici, 12, after=[g_in, cw_all])

    bias_rows = _heads_to_rows(ssm_dt_bias[0])
    alog_rows = _heads_to_rows(ssm_A_log[0])
    drep = jnp.repeat(ssm_D[0], HEADDIM).reshape(1, D_SSM)

    (proj,) = _matmul([(n1, wt_main)], tb=True, b3d="full", out_dtypes=[F32], name="mm_proj", deps=[ag_tok])
    (dt_raw,) = _matmul([(n1, wt_dt)], tb=True, out_dtypes=[F32], name="mm_proj_dt")
    xbc = _ssm_conv_fwd(proj, ssm_w8, ssm_bk)
    dtr = jnp.pad(dt_raw[:, :N_HEADS].T.reshape(N_GROUPS, HEADS_PER_GROUP, T), ((0, 0), (0, 4), (0, 0)))
    y_ssd, hs = _ssd_fwd(xbc, dtr, bias_rows, alog_rows, drep)
    ag_bufs = _split_wait("ag_ici_wait", ag_ss, ag_rs, ag_bufs, _build_ag_ici, after=[y_ssd])
    fw_ss, fw_rs, fw_bufs, fw_tok = _split_start("ag_fwd_start", ag_bufs, _build_ag_fwd, 12)
    y_mix = _shortconv_fwd(proj, sc_w8, _gated_norm_fwd(y_ssd, proj, _tie(ssm_norm_g, fw_tok, "tie_ag_fwd")))
    gath = _split_wait("ag_fwd_wait", fw_ss, fw_rs, fw_bufs, _build_ag_fwd, after=[y_mix])
    w_out_f = gath[0].reshape(2 * D_MODEL, D_MODEL)
    w_gate3 = gath[1].reshape(N_CHIPS, D_MODEL, D_FF // N_CHIPS)
    w_up3 = gath[2].reshape(N_CHIPS, D_MODEL, D_FF // N_CHIPS)
    w_down_f = gath[3].reshape(D_FF, D_MODEL)
    (h1,) = _matmul([(y_mix, w_out_f)], out_dtypes=[F32], name="mm_out", extras=[xt],
                    epilogue=lambda acc, res: (acc + res,))
    n2 = _rmsnorm_fwd(h1, norm_ffn_g, "rmsnorm_ffn")
    g_act, u_act, a_act = _ffn_fwd(n2, w_gate3, w_up3)
    (h2,) = _matmul([(a_act, w_down_f)], out_dtypes=[F32], name="mm_down", extras=[h1],
                    epilogue=lambda acc, res: (acc + res,))

    dh2, dh2b, dg_final, loss_part = _loss_and_final_bwd(h2, tgt, norm_final_g.reshape(1, D_MODEL))
    dg_act, du_act = _matmul([(dh2b, w_down_f)], tb=True, out_dtypes=[BF16, BF16], name="mm_down_bwd",
                             tn=512, extras=[g_act, u_act], epilogue=_swiglu_bwd, nsub=2)
    (dw_down,) = _matmul([(a_act, dh2b)], ta=True, out_dtypes=[F32], name="mm_dw_down", tm=1408, tn=512)
    (dn2,) = _matmul([(dg_act, w_gate3), (du_act, w_up3)], tb=True, b3d=True, out_dtypes=[BF16],
                     name="mm_ffn_in_bwd")
    (dw_gate,) = _matmul([(n2, dg_act)], ta=True, out_dtypes=[F32], name="mm_dw_gate", tm=512, tn=1408,
                         col_shards=True)
    (dw_up,) = _matmul([(n2, du_act)], ta=True, out_dtypes=[F32], name="mm_dw_up", tm=512, tn=1408,
                       col_shards=True)
    dh1, dh1b, dg_ffn = _rmsnorm_bwd(dn2, h1, norm_ffn_g, dh2, "rmsnorm_ffn_bwd")
    (dw_out,) = _matmul([(y_mix, dh1b)], ta=True, out_dtypes=[F32], name="mm_dw_out")

    def halves(g):
        return g.reshape(N_CHIPS, 2, g.shape[1] // 2, g.shape[2])

    def landing(shape, dtype):
        return lax.empty(shape, dtype)

    names1 = names[1:]
    ps1 = [halves(dw_out.reshape(N_CHIPS, -1, D_MODEL)), halves(dw_gate), halves(dw_up),
           halves(dw_down.reshape(N_CHIPS, -1, D_MODEL))]
    r0_1 = [landing((N_CHIPS, 1) + p.shape[2:], F32) for p in ps1]
    sw_ss, sw_rs, sw_arr, sw_tok = _split_start("rs1_swap_start", ps1 + r0_1, _build_rs_swap, 4)
    (dmix,) = _matmul([(dh1b, w_out_f)], tb=True, out_dtypes=[BF16], name="mm_out_bwd", deps=[sw_tok])
    dproj, dw_sc = _shortconv_bwd(dmix, proj, sc_w8)
    dy_ssd, dproj, dg_ssmnorm = _gated_norm_bwd(dmix, y_ssd, proj, ssm_norm_g, dproj)
    sw_arr = _split_wait("rs1_swap_wait", sw_ss, sw_rs, sw_arr, _build_rs_swap, after=[dy_ssd])
    qs1 = [_rs_add_pair(p, r, c_arr, "rs_add_pair_" + nm) for p, r, nm in zip(sw_arr[:4], sw_arr[4:], names1)]
    r1_1 = [landing(q.shape, BF16) for q in qs1]
    ic_ss, ic_rs, ic_arr, ic_tok = _split_start("rs1_ici_start", qs1 + r1_1, _build_rs_ici, 12)
    dxbc_act, ddtr, dbias_acc, dalog_acc, dD_acc = _ssd_bwd(
        xbc, dtr, bias_rows, alog_rows, _tie(drep, ic_tok, "tie_rs1_ici"), dy_ssd, hs)
    dproj, dw_ssmconv, db_ssmconv = _ssm_conv_bwd(dxbc_act, proj, ssm_w8, ssm_bk, dproj)
    dw_ssmconv, db_ssmconv = _from_kernel_xbc(dw_ssmconv), _from_kernel_xbc(db_ssmconv)
    ic_arr = _split_wait("rs1_ici_wait", ic_ss, ic_rs, ic_arr, _build_rs_ici, after=[dproj])
    g1 = [_rs_add_chips(r, q, place_arr, "rs_add_chips_" + nm) for q, r, nm in zip(ic_arr[:4], ic_arr[4:], names1)]
    sh_ss, sh_rs, sh_arr, sh_tok = _split_start("rs1_share_start", g1, _build_rs_share, 4)

    ddt_raw = _pad_cols(ddtr[:, :HEADS_PER_GROUP, :].reshape(N_HEADS, T).T, DT_PAD).astype(BF16)
    (dwt_main,) = _matmul([(dproj, n1)], ta=True, out_dtypes=[F32], name="mm_dw_main", deps=[sh_tok])
    (dwt_dt,) = _matmul([(ddt_raw, n1)], ta=True, out_dtypes=[F32], name="mm_dw_dt")
    p_in = _scatter_rows_to_shards(dwt_main, dwt_dt)
    s2_ss, s2_rs, s2_arr, s2_tok = _split_start(
        "rs2_swap_start", [p_in, landing((N_CHIPS, 1) + p_in.shape[2:], F32)], _build_rs_swap, 1)
    mt = T // _tile(T, 1024)
    mt_a = max(mt // 4, 1)
    (dn1a,) = _matmul([(dproj, wt_main)], b3d=True, out_dtypes=[F32], name="mm_proj_bwd_a", deps=[s2_tok],
                      m_tiles=(0, mt_a))
    g1 = _split_wait("rs1_share_wait", sh_ss, sh_rs, sh_arr, _build_rs_share, after=[dn1a])
    s2_arr = _split_wait("rs2_swap_wait", s2_ss, s2_rs, s2_arr, _build_rs_swap, after=[dn1a])
    q_in = _rs_add_pair(s2_arr[0], s2_arr[1], c_arr, "rs_add_pair_w_in")
    i2_ss, i2_rs, i2_arr, i2_tok = _split_start(
        "rs2_ici_start", [q_in, landing(q_in.shape, BF16)], _build_rs_ici, 3)
    if mt > mt_a:
        (dn1a,) = _matmul([(dproj, wt_main)], b3d=True, out_dtypes=[F32], name="mm_proj_bwd_b", deps=[i2_tok],
                          m_tiles=(mt_a, mt - mt_a), out_buf=dn1a)
    (dn1,) = _matmul([(ddt_raw, wt_dt)], out_dtypes=[BF16], name="mm_proj_dt_bwd", extras=[dn1a],
                     epilogue=lambda acc, res: (acc + res,), deps=[i2_tok])
    dx, _, dg_mix = _rmsnorm_bwd(dn1, xt, norm_mix_g, dh1, "rmsnorm_mix_bwd")

    big_m = [m_w_in[0].T, m_w_out[0], m_w_gate[0], m_w_up[0], m_w_down[0]]
    big_v = [v_w_in[0].T, v_w_out[0], v_w_gate[0], v_w_up[0], v_w_down[0]]
    big_grads = [None] + [g.reshape(w.shape) for g, w in zip(g1, big[1:])]
    big_out = {}
    for k in range(1, 5):
        big_out[names[k]] = _adamw(big[k], big_grads[k], big_m[k], big_v[k], "adamw_" + names[k], deps=[i2_tok])
    i2_arr = _split_wait("rs2_ici_wait", i2_ss, i2_rs, i2_arr, _build_rs_ici, after=[big_out[names[4]][0], dx])
    g_in_red = _rs_add_chips(i2_arr[1], i2_arr[0], place_arr, "rs_add_chips_w_in")
    s3_ss, s3_rs, s3_arr, s3_tok = _split_start("rs2_share_start", [g_in_red], _build_rs_share, 1)

    dD = jnp.sum(dD_acc.reshape(N_HEADS, HEADDIM), axis=-1)
    heads_row = jnp.concatenate([_rows_to_heads(dbias_acc), _rows_to_heads(dalog_acc), dD,
                                 loss_part.reshape(1)]).reshape(1, -1)
    small = jnp.concatenate([
        dw_ssmconv,
        _pad_cols(dw_sc, D_XBC),
        db_ssmconv,
        jnp.concatenate([dg_mix, dg_ssmnorm], axis=1),
        jnp.concatenate([dg_ffn, dg_final], axis=1),
        _pad_cols(heads_row, D_XBC),
        jnp.zeros((4, D_XBC), F32),
    ], axis=0)
    tot = _allreduce_small(small, deps=[s3_tok])
    loss = tot[19, 3 * N_HEADS]

    cs_ssm, cs_sc = D_XBC // N_CHIPS, D_MODEL // N_CHIPS
    g_ssm_conv = lax.dynamic_slice(tot[0:K_SSM], (0, chip * cs_ssm), (K_SSM, cs_ssm))
    g_sc_conv = lax.dynamic_slice(tot[8:8 + K_SC, :D_MODEL], (0, chip * cs_sc), (K_SC, cs_sc))
    small_grads = {
        "norm_mix_g": tot[17:18, :D_MODEL], "ssm_conv_w": g_ssm_conv, "ssm_conv_b": tot[16:17],
        "ssm_dt_bias": tot[19:20, 0:N_HEADS], "ssm_A_log": tot[19:20, N_HEADS:2 * N_HEADS],
        "ssm_D": tot[19:20, 2 * N_HEADS:3 * N_HEADS], "ssm_norm_g": tot[17:18, D_MODEL:],
        "sc_conv_w": g_sc_conv, "norm_ffn_g": tot[18:19, :D_MODEL], "norm_final_g": tot[18:19, D_MODEL:],
    }
    small_w = {"norm_mix_g": (norm_mix_g, m_norm_mix_g, v_norm_mix_g),
               "ssm_conv_w": (ssm_conv_w[0], m_ssm_conv_w[0], v_ssm_conv_w[0]),
               "ssm_conv_b": (ssm_conv_b, m_ssm_conv_b, v_ssm_conv_b),
               "ssm_dt_bias": (ssm_dt_bias, m_ssm_dt_bias, v_ssm_dt_bias),
               "ssm_A_log": (ssm_A_log, m_ssm_A_log, v_ssm_A_log),
               "ssm_D": (ssm_D, m_ssm_D, v_ssm_D),
               "ssm_norm_g": (ssm_norm_g, m_ssm_norm_g, v_ssm_norm_g),
               "sc_conv_w": (sc_conv_w[0], m_sc_conv_w[0], v_sc_conv_w[0]),
               "norm_ffn_g": (norm_ffn_g, m_norm_ffn_g, v_norm_ffn_g),
               "norm_final_g": (norm_final_g.reshape(1, -1), m_norm_final_g.reshape(1, -1),
                                v_norm_final_g.reshape(1, -1))}
    PW = 1024
    order = list(small_w)

    def pack(arrs):
        rows = []
        for a in arrs:
            flat = a.reshape(-1)
            n = -(-flat.shape[0] // PW) * PW
            rows.append(jnp.pad(flat, (0, n - flat.shape[0])).reshape(-1, PW))
        slab = jnp.concatenate(rows, axis=0)
        return _pad_rows(slab, -(-slab.shape[0] // 8) * 8)

    wp = pack([small_w[k][0] for k in order])
    mp = pack([small_w[k][1] for k in order])
    vp = pack([small_w[k][2] for k in order])
    gp = pack([small_grads[k] for k in order])
    sd, sm, sv = _adamw(wp, gp, mp, vp, "adamw_small")

    def unpack(slab):
        out, row = {}, 0
        for k in order:
            shape = small_w[k][0].shape
            size = 1
            for s in shape:
                size *= s
            nr = -(-size // PW)
            out[k] = slab[row:row + nr].reshape(-1)[:size].reshape(shape)
            row += nr
        return out

    s_delta, s_m, s_v = unpack(sd), unpack(sm), unpack(sv)

    (g_in_full,) = _split_wait("rs2_share_wait", s3_ss, s3_rs, s3_arr, _build_rs_share, after=[sd])
    d_t, m_t, v_t, g_t = _adamw(big[0], g_in_full.reshape(2 * HR_IN, D_MODEL), big_m[0], big_v[0],
                                "adamw_" + names[0], emit_g=True)
    big_grads[0] = g_t.T
    big_out[names[0]] = (d_t.T, m_t.T, v_t.T)
    big_g = dict(zip(names, big_grads))

    weight_order = ["norm_mix_g", "w_in", "ssm_conv_w", "ssm_conv_b", "ssm_dt_bias", "ssm_A_log", "ssm_D",
                    "ssm_norm_g", "sc_conv_w", "w_out", "norm_ffn_g", "w_gate", "w_up", "w_down", "norm_final_g"]
    lead = {"ssm_conv_w", "sc_conv_w", "w_in", "w_out", "w_gate", "w_up", "w_down"}

    def shaped(nm, a):
        if nm == "norm_final_g":
            return a.reshape(D_MODEL)
        return a[None] if nm in lead else a

    grads, deltas, new_m, new_v = [], [], [], []
    for nm in weight_order:
        if nm in big_out:
            g, (d, m, v) = big_g[nm], big_out[nm]
        else:
            g, d, m, v = small_grads[nm], s_delta[nm], s_m[nm], s_v[nm]
        grads.append(shaped(nm, g))
        deltas.append(shaped(nm, d))
        new_m.append(shaped(nm, m))
        new_v.append(shaped(nm, v))
    return (loss, dx[None], *grads, *deltas, *new_m, *new_v)


def _swiglu_bwd(da, dg_factor, du_factor):
    return da * dg_factor.astype(F32), da * du_factor.astype(F32)


def _ffn_fwd(n2, w_gate, w_up):
    T, K = n2.shape
    tn = w_gate.shape[2]
    N = N_CHIPS * tn
    tm = _tile(T, 512)
    sub = _tile(tm, 256)

    def body(a_ref, wg_ref, wu_ref, g_ref, u_ref, act_ref):
        for s in range(tm // sub):
            rows = pl.ds(s * sub, sub)
            a = a_ref[rows, :]
            g = jnp.dot(a, wg_ref[...], preferred_element_type=F32)
            u = jnp.dot(a, wu_ref[...], preferred_element_type=F32)
            sig = _sigmoid(g)
            sg = g * sig
            g_ref[rows, :] = (u * (sig * (1.0 + g - sg))).astype(BF16)
            u_ref[rows, :] = sg.astype(BF16)
            act_ref[rows, :] = (sg * u).astype(BF16)

    a_spec = pl.BlockSpec((tm, K), lambda j, i: (i, 0))
    b_spec = pl.BlockSpec((None, K, tn), lambda j, i: (j, 0, 0))
    o_spec = pl.BlockSpec((tm, tn), lambda j, i: (i, j))
    return pl.pallas_call(
        body, name="ffn_fwd", grid=(N // tn, T // tm),
        in_specs=[a_spec, b_spec, b_spec], out_specs=[o_spec] * 3,
        out_shape=[jax.ShapeDtypeStruct((T, N), BF16)] * 3,
        compiler_params=_cparams(("parallel", "parallel")),
    )(n2, w_gate, w_up)
```

```python
import functools

import jax
import jax.numpy as jnp
from jax import lax
from jax.experimental import pallas as pl
from jax.experimental.pallas import tpu as pltpu

F32 = jnp.float32
BF16 = jnp.bfloat16
MESH = pl.DeviceIdType.MESH

D_MODEL = 2048
D_SSM = 2048
HEADDIM = 64
N_HEADS = 32
N_GROUPS = 8
HEADS_PER_GROUP = 4
N_STATE = 128
CHUNK = 128
K_SSM = 4
K_SC = 3
D_XBC = 4096
D_FF = 5632
D_IN = 12320
D_MAIN = 12288
OFF_XBC, OFF_CB, OFF_CC, OFF_CX = 2048, 6144, 8192, 10240
DT_PAD = 128
EPS = 1e-5
N_CHIPS = 4
N_DEV = 8

ADAM_LR = 0.001
ADAM_B1 = 0.9
ADAM_B2 = 0.999
ADAM_EPS = 1e-08
ADAM_WD = 0.01
ADAM_STEP = 10

V7X_VMEM_BYTES = 64 * 1024 * 1024
VMEM_LIMIT = V7X_VMEM_BYTES - 8 * 1024 * 1024


def _cparams(sem=None):
    if sem is None:
        return pltpu.CompilerParams(vmem_limit_bytes=VMEM_LIMIT)
    return pltpu.CompilerParams(dimension_semantics=sem, vmem_limit_bytes=VMEM_LIMIT)


def _tile(dim, pref, unit=128):
    best = None
    t = unit
    while t <= min(dim, pref):
        if dim % t == 0:
            best = t
        t += unit
    return best if best is not None else dim


def _sigmoid(x):
    return 1.0 / (1.0 + jnp.exp(-x))


def _silu(x):
    return x * _sigmoid(x)


def _dsilu(x):
    s = _sigmoid(x)
    return s * (1.0 + x * (1.0 - s))


def _softplus(x):
    return jnp.maximum(x, 0.0) + jnp.log(1.0 + jnp.exp(-jnp.abs(x)))


MATMUL_VMEM_BUDGET = 44 * 1024 * 1024


def _matmul(pairs, *, ta=False, tb=False, out_dtypes, name, tm=1024, tn=1024, tk=None, extras=(), epilogue=None,
            deps=(), col_shards=False, nsub=1, b3d=False, m_tiles=None, out_buf=None):
    a0, b0 = pairs[0]
    M, K = (a0.shape[1], a0.shape[0]) if ta else a0.shape
    if b3d:
        N = b0.shape[1] if tb else b0.shape[0] * b0.shape[2]
        tk, tn = (b0.shape[2], tn) if tb else (tk, b0.shape[2])
    else:
        N = b0.shape[0] if tb else b0.shape[1]
    tm, tn = _tile(M, tm, 8 if M % 128 else 128), _tile(N, tn)
    npair, nex, ndep, nout = len(pairs), len(extras), len(deps), len(out_dtypes)
    if tk is None:
        fixed = 2 * tm * tn * (sum(jnp.dtype(d).itemsize for d in out_dtypes) + sum(e.dtype.itemsize for e in extras))
        tk = K
        while tk > 128 and (K % tk or tk % 128 or
                            fixed + 2 * npair * 2 * tk * (tm + tn) + (tm * tn * 4 if tk < K else 0) > MATMUL_VMEM_BUDGET):
            tk -= 128
    else:
        tk = _tile(K, tk)
    nk = K // tk
    if nk > 1 or tm % nsub or (tm // nsub) % 128:
        nsub = 1
    sub = tm // nsub
    dims = (((0 if ta else 1,), (1 if tb else 0,)), ((), ()))
    i0, mi = m_tiles if m_tiles is not None else (0, M // tm)
    nbuf = 0 if out_buf is None else 1

    def body(*refs):
        a_refs = refs[0:2 * npair:2]
        b_refs = refs[1:2 * npair:2]
        ex_refs = refs[2 * npair:2 * npair + nex]
        o_refs = refs[2 * npair + nex + ndep + nbuf:2 * npair + nex + ndep + nbuf + nout]

        def dots(rows):
            s = None
            for a_ref, b_ref in zip(a_refs, b_refs):
                a = a_ref[...] if rows is None else (a_ref[:, rows] if ta else a_ref[rows, :])
                d = lax.dot_general(a, b_ref[...], dims, preferred_element_type=F32)
                s = d if s is None else s + d
            return s

        def finish(r, rows):
            ex = [e[...] if rows is None else e[rows, :] for e in ex_refs]
            outs = (r,) if epilogue is None else epilogue(r, *ex)
            for o_ref, o in zip(o_refs, outs):
                if rows is None:
                    o_ref[...] = o.astype(o_ref.dtype)
                else:
                    o_ref[rows, :] = o.astype(o_ref.dtype)

        if nk == 1:
            for s in range(nsub):
                rows = None if nsub == 1 else pl.ds(s * sub, sub)
                finish(dots(rows), rows)
            return

        acc = refs[-1]
        k = pl.program_id(2)

        @pl.when(k == 0)
        def _():
            acc[...] = dots(None)

        @pl.when(jnp.logical_and(k > 0, k < nk - 1))
        def _():
            acc[...] += dots(None)

        @pl.when(k == nk - 1)
        def _():
            finish(acc[...] + dots(None), None)

    a_spec = (pl.BlockSpec((tk, tm), lambda i, j, k: (k, i + i0)) if ta
              else pl.BlockSpec((tm, tk), lambda i, j, k: (i + i0, k)))
    if b3d:
        b_spec = (pl.BlockSpec((None, tn, tk), lambda i, j, k: (k, j, 0)) if tb
                  else pl.BlockSpec((None, tk, tn), lambda i, j, k: (j, k, 0)))
    else:
        b_spec = (pl.BlockSpec((tn, tk), lambda i, j, k: (j, k)) if tb
                  else pl.BlockSpec((tk, tn), lambda i, j, k: (k, j)))
    e_spec = pl.BlockSpec((tm, tn), lambda i, j, k: (i + i0, j))
    if col_shards:
        o_spec = pl.BlockSpec((None, tm, tn), lambda i, j, k: (j, i + i0, 0))
        o_shape = (N // tn, M, tn)
    else:
        o_spec, o_shape = e_spec, (M, N)
    args, in_specs = [], []
    for a, b in pairs:
        args += [a, b]
        in_specs += [a_spec, b_spec]
    args += list(extras) + list(deps) + ([] if out_buf is None else [out_buf])
    in_specs += [e_spec] * nex + [ANY] * (ndep + nbuf)
    outs = pl.pallas_call(
        body,
        name=name,
        grid=(mi, N // tn, nk),
        in_specs=in_specs,
        out_specs=[o_spec] * nout,
        out_shape=[jax.ShapeDtypeStruct(o_shape, dt) for dt in out_dtypes],
        input_output_aliases={} if out_buf is None else {len(args) - 1: 0},
        scratch_shapes=[pltpu.VMEM((tm, tn), F32)] if nk > 1 else [],
        compiler_params=_cparams(("parallel", "parallel", "arbitrary")),
    )(*args)
    return outs


def _cast_into_gather(w, chip_arr, name, split_cols=False, deps=()):
    R, C = w.shape
    hr, hc = (R, C // 2) if split_cols else (R // 2, C)
    tr = _tile(hr, 512, 8)
    nb = hr // tr

    def body(chip_ref, w_ref, *rest):
        rest[-1][...] = w_ref[...].astype(BF16)

    in_map = (lambda h, i, chip_ref: (i, h)) if split_cols else (lambda h, i, chip_ref: (h * nb + i, 0))
    grid_spec = pltpu.PrefetchScalarGridSpec(
        num_scalar_prefetch=1, grid=(2, nb),
        in_specs=[pl.BlockSpec((tr, hc), in_map)] + [ANY] * len(deps),
        out_specs=pl.BlockSpec((None, tr, hc), lambda h, i, chip_ref: (2 * chip_ref[0] + h, i, 0)))
    return pl.pallas_call(
        body, name=name, grid_spec=grid_spec,
        out_shape=jax.ShapeDtypeStruct((N_DEV, hr, hc), BF16),
        compiler_params=_cparams(("parallel", "parallel")),
    )(chip_arr, w, *deps)


def _tie(small, token, name):
    def body(s_ref, t_ref, o_ref):
        o_ref[...] = s_ref[...]

    vm = pl.BlockSpec(memory_space=pltpu.VMEM)
    return pl.pallas_call(body, name=name, in_specs=[vm, ANY], out_specs=vm,
                          out_shape=jax.ShapeDtypeStruct(small.shape, small.dtype))(small, token)


def _rmsnorm_fwd(x, g, name):
    T, D = x.shape
    tt = _tile(T, 256)

    def body(x_ref, g_ref, n_ref):
        xv = x_ref[...]
        r = lax.rsqrt(jnp.mean(xv * xv, axis=-1, keepdims=True) + EPS)
        n_ref[...] = (xv * r * g_ref[...]).astype(BF16)

    return pl.pallas_call(
        body, name=name, grid=(T // tt,),
        in_specs=[pl.BlockSpec((tt, D), lambda i: (i, 0)), pl.BlockSpec((1, D), lambda i: (0, 0))],
        out_specs=pl.BlockSpec((tt, D), lambda i: (i, 0)),
        out_shape=jax.ShapeDtypeStruct((T, D), BF16),
        compiler_params=_cparams(("parallel",)),
    )(x, g)


def _rmsnorm_bwd(dn, x, g, res, name):
    T, D = x.shape
    tt = _tile(T, 256)

    def body(dn_ref, x_ref, g_ref, res_ref, dx_ref, dxb_ref, dg_ref):
        @pl.when(pl.program_id(0) == 0)
        def _():
            dg_ref[...] = jnp.zeros_like(dg_ref)

        xv = x_ref[...]
        dy = dn_ref[...].astype(F32)
        r = lax.rsqrt(jnp.mean(xv * xv, axis=-1, keepdims=True) + EPS)
        xhat = xv * r
        dxh = dy * g_ref[...]
        dx = res_ref[...] + r * (dxh - xhat * jnp.mean(dxh * xhat, axis=-1, keepdims=True))
        dx_ref[...] = dx
        dxb_ref[...] = dx.astype(BF16)
        dg_ref[...] += jnp.sum(dy * xhat, axis=0, keepdims=True)

    tok = pl.BlockSpec((tt, D), lambda i: (i, 0))
    vec = pl.BlockSpec((1, D), lambda i: (0, 0))
    return pl.pallas_call(
        body, name=name, grid=(T // tt,),
        in_specs=[tok, tok, vec, tok],
        out_specs=[tok, tok, vec],
        out_shape=[jax.ShapeDtypeStruct((T, D), F32), jax.ShapeDtypeStruct((T, D), BF16),
                   jax.ShapeDtypeStruct((1, D), F32)],
        compiler_params=_cparams(("arbitrary",)),
    )(dn, x, g, res)


def _loss_and_final_bwd(h2, target, gf):
    T, D = h2.shape
    tt = _tile(T, 256)

    def body(h_ref, t_ref, g_ref, dh_ref, dhb_ref, dg_ref, loss_ref):
        @pl.when(pl.program_id(0) == 0)
        def _():
            dg_ref[...] = jnp.zeros_like(dg_ref)
            loss_ref[...] = jnp.zeros_like(loss_ref)

        xv = h_ref[...]
        r = lax.rsqrt(jnp.mean(xv * xv, axis=-1, keepdims=True) + EPS)
        xhat = xv * r
        err = xhat * g_ref[...] - t_ref[...]
        loss_ref[...] += 0.5 * jnp.sum(jnp.mean(err * err, axis=-1, keepdims=True), axis=0, keepdims=True)
        dy = err * (1.0 / D)
        dxh = dy * g_ref[...]
        dx = r * (dxh - xhat * jnp.mean(dxh * xhat, axis=-1, keepdims=True))
        dh_ref[...] = dx
        dhb_ref[...] = dx.astype(BF16)
        dg_ref[...] += jnp.sum(dy * xhat, axis=0, keepdims=True)

    tok = pl.BlockSpec((tt, D), lambda i: (i, 0))
    vec = pl.BlockSpec((1, D), lambda i: (0, 0))
    return pl.pallas_call(
        body, name="loss_final_bwd", grid=(T // tt,),
        in_specs=[tok, tok, vec],
        out_specs=[tok, tok, vec, pl.BlockSpec((1, 1), lambda i: (0, 0))],
        out_shape=[jax.ShapeDtypeStruct((T, D), F32), jax.ShapeDtypeStruct((T, D), BF16),
                   jax.ShapeDtypeStruct((1, D), F32), jax.ShapeDtypeStruct((1, 1), F32)],
        compiler_params=_cparams(("arbitrary",)),
    )(h2, target, gf)


def _gated_norm_fwd(y, proj, g):
    T, D = y.shape
    tt = _tile(T, 256)

    def body(y_ref, z_ref, g_ref, o_ref):
        yg = y_ref[...] * _silu(z_ref[...])
        r = lax.rsqrt(jnp.mean(yg * yg, axis=-1, keepdims=True) + EPS)
        o_ref[...] = (yg * r * g_ref[...]).astype(BF16)

    tok = pl.BlockSpec((tt, D), lambda i: (i, 0))
    return pl.pallas_call(
        body, name="gated_norm_fwd", grid=(T // tt,),
        in_specs=[tok, tok, pl.BlockSpec((1, D), lambda i: (0, 0))],
        out_specs=tok,
        out_shape=jax.ShapeDtypeStruct((T, 2 * D_MODEL), BF16),
        compiler_params=_cparams(("parallel",)),
    )(y, proj, g)


def _gated_norm_bwd(dmix, y, proj, g, dproj):
    T, D = y.shape
    tt = _tile(T, 256)

    def body(do_ref, y_ref, z_ref, g_ref, dp_ref, dy_ref, dz_ref, dg_ref):
        @pl.when(pl.program_id(0) == 0)
        def _():
            dg_ref[...] = jnp.zeros_like(dg_ref)

        yv, zv = y_ref[...], z_ref[...]
        do = do_ref[...].astype(F32)
        sz = _silu(zv)
        yg = yv * sz
        r = lax.rsqrt(jnp.mean(yg * yg, axis=-1, keepdims=True) + EPS)
        xhat = yg * r
        dxh = do * g_ref[...]
        dyg = r * (dxh - xhat * jnp.mean(dxh * xhat, axis=-1, keepdims=True))
        dy_ref[...] = dyg * sz
        dz_ref[...] = (dyg * yv * _dsilu(zv)).astype(BF16)
        dg_ref[...] += jnp.sum(do * xhat, axis=0, keepdims=True)

    tok = pl.BlockSpec((tt, D), lambda i: (i, 0))
    vec = pl.BlockSpec((1, D), lambda i: (0, 0))
    return pl.pallas_call(
        body, name="gated_norm_bwd", grid=(T // tt,),
        in_specs=[tok, tok, tok, vec, ANY],
        out_specs=[tok, tok, vec],
        out_shape=[jax.ShapeDtypeStruct((T, D), F32), jax.ShapeDtypeStruct(dproj.shape, BF16),
                   jax.ShapeDtypeStruct((1, D), F32)],
        input_output_aliases={4: 1},
        compiler_params=_cparams(("arbitrary",)),
    )(dmix, y, proj, g, dproj)


HALO = 8


def _shift_down(cur, prev8, s):
    ext = jnp.concatenate([prev8, cur], axis=0)
    return pltpu.roll(ext, s, axis=0)[HALO:]


def _shift_up(cur, next8, s):
    n = cur.shape[0]
    ext = jnp.concatenate([cur, next8], axis=0)
    return pltpu.roll(ext, n + HALO - s, axis=0)[:n]


def _conv_specs(tt, cb, col_off_blocks, nt):
    hb = tt // HALO
    cur = pl.BlockSpec((tt, cb), lambda j, i: (i, col_off_blocks + j))
    prev = pl.BlockSpec((HALO, cb), lambda j, i: (jnp.maximum(i * hb - 1, 0), col_off_blocks + j))
    nxt = pl.BlockSpec((HALO, cb), lambda j, i: (jnp.minimum((i + 1) * hb, nt * hb - 1), col_off_blocks + j))
    return cur, prev, nxt


def _taps(cur, prev8, K):
    return [_shift_down(cur, prev8, K - 1 - k) for k in range(K - 1)] + [cur]


def _conv_of_taps(taps, w):
    y = taps[-1] * w[len(taps) - 1:len(taps), :]
    for k, t in enumerate(taps[:-1]):
        y = y + t * w[k:k + 1, :]
    return y


def _causal_conv(cur, prev8, w, K):
    return _conv_of_taps(_taps(cur, prev8, K), w)


def _anticausal_conv(cur, next8, w, K):
    y = cur * w[K - 1:K, :]
    for k in range(K - 1):
        y = y + _shift_up(cur, next8, K - 1 - k) * w[k:k + 1, :]
    return y


def _ssm_conv_fwd(proj, w8, b):
    T = proj.shape[0]
    tt, cb = _tile(T, 512), 512
    nt = T // tt
    cur, prev, _ = _conv_specs(tt, cb, OFF_XBC // cb, nt)

    def body(u_ref, up_ref, w_ref, b_ref, o_ref):
        first = pl.program_id(1) == 0
        p8 = jnp.where(first, 0.0, up_ref[...])
        pre = _causal_conv(u_ref[...], p8, w_ref[...], K_SSM) + b_ref[...]
        o_ref[...] = _silu(pre)

    return pl.pallas_call(
        body, name="ssm_conv_fwd", grid=(D_XBC // cb, nt),
        in_specs=[cur, prev, pl.BlockSpec((8, cb), lambda j, i: (0, j)), pl.BlockSpec((1, cb), lambda j, i: (0, j))],
        out_specs=pl.BlockSpec((tt, cb), lambda j, i: (i, j)),
        out_shape=jax.ShapeDtypeStruct((T, D_XBC), F32),
        compiler_params=_cparams(("parallel", "parallel")),
    )(proj, proj, w8, b)


def _ssm_conv_bwd(dact, proj, w8, b, dproj):
    T = proj.shape[0]
    tt, cb = _tile(T, 512), 512
    nt = T // tt
    cur, prev, nxt = _conv_specs(tt, cb, OFF_XBC // cb, nt)
    dcur, dprev, dnxt = _conv_specs(tt, cb, 0, nt)

    def dpre_of(d, u, p8, w, bb):
        pre = _causal_conv(u, p8, w, K_SSM) + bb
        return d * _dsilu(pre)

    def body(d_ref, dn_ref, u_ref, up_ref, un_ref, w_ref, b_ref, dp_ref, dx_ref, dw_ref, db_ref):
        i = pl.program_id(1)

        @pl.when(i == 0)
        def _():
            dw_ref[...] = jnp.zeros_like(dw_ref)
            db_ref[...] = jnp.zeros_like(db_ref)

        w, bb = w_ref[...], b_ref[...]
        u = u_ref[...]
        p8 = jnp.where(i == 0, 0.0, up_ref[...])
        taps = _taps(u, p8, K_SSM)
        dpre = d_ref[...] * _dsilu(_conv_of_taps(taps, w) + bb)
        un = un_ref[...]
        dpre_n = dpre_of(dn_ref[...], un, u[tt - HALO:, :], w, bb)
        dpre_n = jnp.where(i == nt - 1, 0.0, dpre_n)
        dx_ref[...] = _anticausal_conv(dpre, dpre_n, w, K_SSM).astype(BF16)
        rows = [jnp.sum(dpre * t, axis=0, keepdims=True) for t in taps]
        rows.append(jnp.zeros((8 - K_SSM, cb), F32))
        dw_ref[...] += jnp.concatenate(rows, axis=0)
        db_ref[...] += jnp.sum(dpre, axis=0, keepdims=True)

    wspec = pl.BlockSpec((8, cb), lambda j, i: (0, j))
    bspec = pl.BlockSpec((1, cb), lambda j, i: (0, j))
    return pl.pallas_call(
        body, name="ssm_conv_bwd", grid=(D_XBC // cb, nt),
        in_specs=[dcur, dnxt, cur, prev, nxt, wspec, bspec, ANY],
        out_specs=[pl.BlockSpec((tt, cb), lambda j, i: (i, OFF_XBC // cb + j)), wspec, bspec],
        out_shape=[jax.ShapeDtypeStruct(dproj.shape, BF16), jax.ShapeDtypeStruct((8, D_XBC), F32),
                   jax.ShapeDtypeStruct((1, D_XBC), F32)],
        input_output_aliases={7: 0},
        compiler_params=_cparams(("parallel", "arbitrary")),
    )(dact, dact, proj, proj, proj, w8, b, dproj)


SCB = 512
SC3 = 3 * SCB


def _sc_specs(tt, nt):
    hb = tt // HALO
    cur = pl.BlockSpec((tt, SC3), lambda j, i: (i, OFF_CB // SC3 + j))
    prev = pl.BlockSpec((HALO, SC3), lambda j, i: (jnp.maximum(i * hb - 1, 0), OFF_CB // SC3 + j))
    nxt = pl.BlockSpec((HALO, SC3), lambda j, i: (jnp.minimum((i + 1) * hb, nt * hb - 1), OFF_CB // SC3 + j))
    return cur, prev, nxt


def _shortconv_fwd(proj, w8, ymix):
    T = proj.shape[0]
    tt = _tile(T, 512)
    nt = T // tt
    cur, prev, _ = _sc_specs(tt, nt)

    def body(p_ref, pp_ref, w_ref, y_ref, o_ref):
        p, pp = p_ref[...], pp_ref[...]
        v = p[:, SCB:2 * SCB] * p[:, 2 * SCB:]
        vp = jnp.where(pl.program_id(1) == 0, 0.0, pp[:, SCB:2 * SCB] * pp[:, 2 * SCB:])
        o_ref[...] = (p[:, :SCB] * _causal_conv(v, vp, w_ref[...], K_SC)).astype(BF16)

    return pl.pallas_call(
        body, name="shortconv_fwd", grid=(D_MODEL // SCB, nt),
        in_specs=[cur, prev, pl.BlockSpec((8, SCB), lambda j, i: (0, j)), ANY],
        out_specs=pl.BlockSpec((tt, SCB), lambda j, i: (i, D_SSM // SCB + j)),
        out_shape=jax.ShapeDtypeStruct(ymix.shape, BF16),
        input_output_aliases={3: 0},
        compiler_params=_cparams(("parallel", "parallel")),
    )(proj, proj, w8, ymix)


def _shortconv_bwd(dmix, proj, w8):
    T = proj.shape[0]
    tt = _tile(T, 512)
    nt = T // tt
    hb = tt // HALO
    cur, prev, nxt = _sc_specs(tt, nt)
    d_s = pl.BlockSpec((tt, SCB), lambda j, i: (i, D_SSM // SCB + j))
    dn_s = pl.BlockSpec((HALO, SCB), lambda j, i: (jnp.minimum((i + 1) * hb, nt * hb - 1), D_SSM // SCB + j))

    def body(d_ref, dn_ref, p_ref, pp_ref, pn_ref, w_ref, dp_ref, dw_ref):
        i = pl.program_id(1)

        @pl.when(i == 0)
        def _():
            dw_ref[...] = jnp.zeros_like(dw_ref)

        w = w_ref[...]
        p, pp = p_ref[...], pp_ref[...]
        gb, gc, u = p[:, :SCB], p[:, SCB:2 * SCB], p[:, 2 * SCB:]
        v = gc * u
        vp = jnp.where(i == 0, 0.0, pp[:, SCB:2 * SCB] * pp[:, 2 * SCB:])
        d = d_ref[...].astype(F32)
        taps = _taps(v, vp, K_SC)
        dp_ref[:, :SCB] = (d * _conv_of_taps(taps, w)).astype(BF16)
        dcv = d * gb
        dcv_n = jnp.where(i == nt - 1, 0.0, dn_ref[...].astype(F32) * pn_ref[:, :SCB])
        dv = _anticausal_conv(dcv, dcv_n, w, K_SC)
        dp_ref[:, SCB:2 * SCB] = (dv * u).astype(BF16)
        dp_ref[:, 2 * SCB:] = (dv * gc).astype(BF16)
        rows = [jnp.sum(dcv * t, axis=0, keepdims=True) for t in taps]
        rows.append(jnp.zeros((8 - K_SC, SCB), F32))
        dw_ref[...] += jnp.concatenate(rows, axis=0)

    wspec = pl.BlockSpec((8, SCB), lambda j, i: (0, j))
    return pl.pallas_call(
        body, name="shortconv_bwd", grid=(D_MODEL // SCB, nt),
        in_specs=[d_s, dn_s, cur, prev, nxt, wspec],
        out_specs=[cur, wspec],
        out_shape=[jax.ShapeDtypeStruct((T, D_MAIN), BF16), jax.ShapeDtypeStruct((8, D_MODEL), F32)],
        compiler_params=_cparams(("parallel", "arbitrary")),
    )(dmix, dmix, proj, proj, proj, w8)


GW = HEADS_PER_GROUP * HEADDIM


def _dot(a, b):
    return jnp.dot(a.astype(BF16), b.astype(BF16), preferred_element_type=F32)


def _dot_nt(a, b):
    return lax.dot_general(a.astype(BF16), b.astype(BF16), (((1,), (1,)), ((), ())), preferred_element_type=F32)


def _dot_tn(a, b):
    return lax.dot_general(a.astype(BF16), b.astype(BF16), (((0,), (0,)), ((), ())), preferred_element_type=F32)


def _bf16_terms(x, n):
    terms, r = [], x
    for _ in range(n):
        t = r.astype(BF16)
        terms.append(t)
        r = r - t.astype(F32)
    return terms


def _dot_sel(a, sel, n=2):
    s = sel.astype(BF16)
    return sum(jnp.dot(t, s, preferred_element_type=F32) for t in _bf16_terms(a, n))


def _sel_dot(sel, b, n=2):
    s = sel.astype(BF16)
    return sum(jnp.dot(s, t, preferred_element_type=F32) for t in _bf16_terms(b, n))


def _sel_dot_nt(sel, b, n=2):
    s = sel.astype(BF16)
    return sum(lax.dot_general(s, t, (((1,), (1,)), ((), ())), preferred_element_type=F32)
               for t in _bf16_terms(b, n))


def _head_cols(rows):
    parts = [jnp.broadcast_to(rows[r:r + 1, :], (HEADDIM, CHUNK)) for r in range(HEADS_PER_GROUP)]
    return jnp.concatenate(parts, axis=0).T


def _head_rows(rows):
    parts = [jnp.broadcast_to(rows[r:r + 1, :], (HEADDIM, N_STATE)) for r in range(HEADS_PER_GROUP)]
    return jnp.concatenate(parts, axis=0)


def _ssd_common(dtr, bias, alog):
    dt = _softplus(dtr + bias)
    A = -jnp.exp(alog)
    a = dt * A
    ki = lax.broadcasted_iota(jnp.int32, (CHUNK, CHUNK), 0)
    si = lax.broadcasted_iota(jnp.int32, (CHUNK, CHUNK), 1)
    upper = (ki <= si).astype(F32)
    cs = _dot_sel(a, upper, 3)
    cs_last = jnp.broadcast_to(cs[:, CHUNK - 1:CHUNK], (8, CHUNK))
    return dt, A, a, cs, cs_last


def _decay_matrix(cs, r):
    li = lax.broadcasted_iota(jnp.int32, (CHUNK, CHUNK), 0)
    si = lax.broadcasted_iota(jnp.int32, (CHUNK, CHUNK), 1)
    causal = li >= si
    R = jnp.broadcast_to(cs[r:r + 1, :], (CHUNK, CHUNK))
    seg = jnp.where(causal, R.T - R, 0.0)
    return jnp.where(causal, jnp.exp(seg), 0.0)


def _decay_cat(cs):
    return jnp.concatenate([_decay_matrix(cs, r) for r in range(HEADS_PER_GROUP)], axis=1)


def _lanes4(m):
    return jnp.concatenate([m] * HEADS_PER_GROUP, axis=1)


def _head_blocks(v):
    col = lax.broadcasted_iota(jnp.int32, v.shape, 1) // HEADDIM
    return jnp.concatenate([jnp.where(col == r, v, jnp.zeros_like(v)) for r in range(HEADS_PER_GROUP)], axis=0)


GXBC = GW + 2 * N_STATE


GS = 4


def _ssd_in_specs(nc, rev):
    cix = (lambda c: nc - 1 - c) if rev else (lambda c: c)
    x_s = pl.BlockSpec((CHUNK, GS * GW), lambda g, c: (cix(c), g))
    xbc_s = pl.BlockSpec((CHUNK, GS * GXBC), lambda g, c: (cix(c), g))
    dtr_s = pl.BlockSpec((GS, 8, CHUNK), lambda g, c: (g, 0, cix(c)))
    row_s = pl.BlockSpec((GS, 8, CHUNK), lambda g, c: (g, 0, 0))
    drep_s = pl.BlockSpec((1, GS * GW), lambda g, c: (0, g))
    hs_s = pl.BlockSpec((1, GS * GW, N_STATE), lambda g, c: (cix(c), g, 0))
    return x_s, xbc_s, dtr_s, row_s, drep_s, hs_s


def _xbc_parts(xbc_ref, gi):
    o = gi * GXBC
    return xbc_ref[:, o:o + GW], xbc_ref[:, o + GW:o + GW + N_STATE], xbc_ref[:, o + GW + N_STATE:o + GXBC]


def _ssd_fwd(xbc, dtr, bias, alog, drep):
    T = xbc.shape[0]
    nc = T // CHUNK
    x_s, xbc_s, dtr_s, row_s, drep_s, hs_s = _ssd_in_specs(nc, False)

    def body(xbc_ref, dtr_ref, bias_ref, alog_ref, drep_ref, y_ref, hs_ref, h_scr):
        @pl.when(pl.program_id(1) == 0)
        def _():
            h_scr[...] = jnp.zeros_like(h_scr)

        for gi in range(GS):
            cols, rows = slice(gi * GW, (gi + 1) * GW), pl.ds(gi * GW, GW)
            x, Bm, Cm = _xbc_parts(xbc_ref, gi)
            dt, A, a, cs, cs_last = _ssd_common(dtr_ref[gi], bias_ref[gi], alog_ref[gi])
            E = _head_cols(jnp.exp(cs))
            W = _head_cols(jnp.exp(cs_last - cs) * dt)
            X = (x * _head_cols(dt)).astype(BF16)
            CB = _dot_nt(Cm, Bm)
            y = jnp.dot((_lanes4(CB) * _decay_cat(cs)).astype(BF16), _head_blocks(X), preferred_element_type=F32)
            h = h_scr[rows, :]
            hs_ref[0, rows, :] = h
            y = y + _dot_nt(Cm, h) * E
            y_ref[:, cols] = y + drep_ref[:, cols] * x
            h_scr[rows, :] = h * _head_rows(jnp.exp(cs_last)) + _dot_tn(x * W, Bm)

    return pl.pallas_call(
        body, name="ssd_fwd", grid=(N_GROUPS // GS, nc),
        in_specs=[xbc_s, dtr_s, row_s, row_s, drep_s],
        out_specs=[x_s, hs_s],
        out_shape=[jax.ShapeDtypeStruct((T, D_SSM), F32), jax.ShapeDtypeStruct((nc, D_SSM, N_STATE), F32)],
        scratch_shapes=[pltpu.VMEM((GS * GW, N_STATE), F32)],
        compiler_params=_cparams(("parallel", "arbitrary")),
    )(xbc, dtr, bias, alog, drep)


def _ssd_bwd(xbc, dtr, bias, alog, drep, dy, hs):
    T = xbc.shape[0]
    nc = T // CHUNK
    x_s, xbc_s, dtr_s, row_s, drep_s, hs_s = _ssd_in_specs(nc, True)

    def body(xbc_ref, dtr_ref, bias_ref, alog_ref, drep_ref, dy_ref, hs_ref,
             dxbc_ref, ddtr_ref, dbias_ref, dalog_ref, dd_ref, dh_scr):
        @pl.when(pl.program_id(1) == 0)
        def _():
            dh_scr[...] = jnp.zeros_like(dh_scr)
            dbias_ref[...] = jnp.zeros_like(dbias_ref)
            dalog_ref[...] = jnp.zeros_like(dalog_ref)
            dd_ref[...] = jnp.zeros_like(dd_ref)

        for gi in range(GS):
            one_group(gi, xbc_ref, dtr_ref, bias_ref, alog_ref, drep_ref, dy_ref, hs_ref,
                      dxbc_ref, ddtr_ref, dbias_ref, dalog_ref, dd_ref, dh_scr)

    def one_group(gi, xbc_ref, dtr_ref, bias_ref, alog_ref, drep_ref, dy_ref, hs_ref,
                  dxbc_ref, ddtr_ref, dbias_ref, dalog_ref, dd_ref, dh_scr):
        cols, rows, o = slice(gi * GW, (gi + 1) * GW), pl.ds(gi * GW, GW), gi * GXBC
        x, Bm, Cm = _xbc_parts(xbc_ref, gi)
        dY = dy_ref[:, cols]
        dt, A, a, cs, cs_last = _ssd_common(dtr_ref[gi], bias_ref[gi], alog_ref[gi])
        E = _head_cols(jnp.exp(cs))
        DT = _head_cols(dt)
        Wd = _head_cols(jnp.exp(cs_last - cs))
        X = x * DT
        h = hs_ref[0, rows, :]
        dS = dh_scr[rows, :]
        CB = _dot_nt(Cm, Bm)
        rowid = lax.broadcasted_iota(jnp.int32, (8, CHUNK), 0)
        lane = lax.broadcasted_iota(jnp.int32, (8, CHUNK), 1)
        hsel = (lax.broadcasted_iota(jnp.int32, (8, GW), 1) // HEADDIM
                == lax.broadcasted_iota(jnp.int32, (8, GW), 0)).astype(F32)
        hsel_l = (lax.broadcasted_iota(jnp.int32, (8, HEADS_PER_GROUP * CHUNK), 1) // CHUNK
                  == lax.broadcasted_iota(jnp.int32, (8, HEADS_PER_GROUP * CHUNK), 0)).astype(F32)

        Lc, CBc = _decay_cat(cs), _lanes4(CB)
        Mc = CBc * Lc
        GLc = _dot_nt(dY, _head_blocks(X.astype(BF16))) * Lc
        Wc = GLc * CBc
        colsum = jnp.sum(Wc, axis=0, keepdims=True)
        dcs = _sel_dot_nt(hsel_l, Wc)
        dCB = jnp.zeros((CHUNK, CHUNK), F32)
        for r in range(HEADS_PER_GROUP):
            blk = slice(r * CHUNK, (r + 1) * CHUNK)
            dCB = dCB + GLc[:, blk]
            dcs = dcs - jnp.where(rowid == r, colsum[:, blk], 0.0)
        m_stack = jnp.concatenate([Mc[:, r * CHUNK:(r + 1) * CHUNK].astype(BF16) for r in range(HEADS_PER_GROUP)],
                                  axis=0)
        dX = lax.dot_general(m_stack, _head_blocks(dY.astype(BF16)), (((0,), (0,)), ((), ())),
                             preferred_element_type=F32)
        dC = _dot(dCB, Bm)
        dB = _dot_tn(dCB, Cm)
        T1 = _dot_nt(Bm, dS)
        dX = dX + T1 * Wd
        dB = dB + _dot(X * Wd, dS)
        pdec = _sel_dot_nt(hsel, X * T1 * Wd)
        dcs = dcs - pdec
        dlast = jnp.sum(pdec, axis=1, keepdims=True) \
            + jnp.exp(cs_last[:, 0:1]) * jnp.sum(_sel_dot(hsel, dS * h), axis=1, keepdims=True)
        dYE = dY * E
        dC = dC + _dot(dYE, h)
        yoff = _dot_nt(Cm, h) * E
        dcs = dcs + _sel_dot_nt(hsel, dY * yoff)
        dcs = dcs + jnp.where(lane == CHUNK - 1, dlast, 0.0)
        ki = lax.broadcasted_iota(jnp.int32, (CHUNK, CHUNK), 0)
        si = lax.broadcasted_iota(jnp.int32, (CHUNK, CHUNK), 1)
        lower = (ki >= si).astype(F32)
        da = _dot_sel(dcs, lower)
        ddt = da * A + _sel_dot_nt(hsel, dX * x)
        ddtr = ddt * _sigmoid(dtr_ref[gi] + bias_ref[gi])
        ddtr_ref[gi] = ddtr
        dbias_ref[gi] += ddtr
        dalog_ref[gi] += da * a
        dxbc_ref[:, o:o + GW] = dX * DT + drep_ref[:, cols] * dY
        dd_ref[:, cols] += jnp.sum(dY * x, axis=0, keepdims=True)
        dxbc_ref[:, o + GW:o + GW + N_STATE] = dB
        dxbc_ref[:, o + GW + N_STATE:o + GXBC] = dC
        dh_scr[rows, :] = dS * _head_rows(jnp.exp(cs_last)) + _dot_tn(dYE, Cm)

    return pl.pallas_call(
        body, name="ssd_bwd", grid=(N_GROUPS // GS, nc),
        in_specs=[xbc_s, dtr_s, row_s, row_s, drep_s, x_s, hs_s],
        out_specs=[xbc_s, dtr_s, row_s, row_s, drep_s],
        out_shape=[jax.ShapeDtypeStruct((T, D_XBC), F32),
                   jax.ShapeDtypeStruct((N_GROUPS, 8, T), F32),
                   jax.ShapeDtypeStruct((N_GROUPS, 8, CHUNK), F32),
                   jax.ShapeDtypeStruct((N_GROUPS, 8, CHUNK), F32),
                   jax.ShapeDtypeStruct((1, D_SSM), F32)],
        scratch_shapes=[pltpu.VMEM((GS * GW, N_STATE), F32)],
        compiler_params=_cparams(("parallel", "arbitrary")),
    )(xbc, dtr, bias, alog, drep, dy, hs)


def _adamw(w, g, m, v, name, deps=(), emit_g=False):
    R, C = w.shape
    tr = _tile(R, 256, 8)
    nd = len(deps)
    nout = 4 if emit_g else 3

    def body(w_ref, g_ref, m_ref, v_ref, *rest):
        outs = rest[nd:]
        gv = g_ref[...]
        mn = ADAM_B1 * m_ref[...] + (1.0 - ADAM_B1) * gv
        vn = ADAM_B2 * v_ref[...] + (1.0 - ADAM_B2) * (gv * gv)
        m_hat = mn / (1.0 - ADAM_B1 ** ADAM_STEP)
        v_hat = vn / (1.0 - ADAM_B2 ** ADAM_STEP)
        outs[0][...] = -ADAM_LR * (m_hat / (jnp.sqrt(v_hat) + ADAM_EPS) + ADAM_WD * w_ref[...])
        outs[1][...] = mn
        outs[2][...] = vn
        if emit_g:
            outs[3][...] = gv

    spec = pl.BlockSpec((tr, C), lambda i: (i, 0))
    return pl.pallas_call(
        body, name=name, grid=(R // tr,),
        in_specs=[spec] * 4 + [ANY] * nd, out_specs=[spec] * nout,
        out_shape=[jax.ShapeDtypeStruct((R, C), F32)] * nout,
        compiler_params=_cparams(("parallel",)),
    )(w, g, m, v, *deps)


ANY = pl.BlockSpec(memory_space=pl.ANY)


def _place():
    x, y, c = lax.axis_index("x"), lax.axis_index("y"), lax.axis_index("c")
    return x, y, c


def _other_chips(x, y):
    return [(1 - x, y), (x, 1 - y), (1 - x, 1 - y)]


def _allgather_inplace(bufs, splits, first_done=False):
    n = len(bufs)

    def body(*refs):
        o_refs = refs[n:2 * n]
        send_sems, recv_sems = refs[2 * n:]
        x, y, c = _place()
        xn, yn, dg, sibling = (1 - x, y), (x, 1 - y), (1 - x, 1 - y), (x, y, 1 - c)

        def blk(k, chip, pc):
            return o_refs[k].at[4 * chip[0] + 2 * chip[1] + pc]

        def part(k, ref, p):
            kind, s = splits[k]
            _, R, C = bufs[k].shape
            if kind == "rows":
                return ref.at[pl.ds(0, s)] if p == 0 else ref.at[pl.ds(s, R - s)]
            return ref.at[:, pl.ds(0, s)] if p == 0 else ref.at[:, pl.ds(s, C - s)]

        def copy(k, slot, ref, to):
            return pltpu.make_async_remote_copy(
                src_ref=ref, dst_ref=ref, send_sem=send_sems.at[k, slot], recv_sem=recv_sems.at[k, slot],
                device_id=to, device_id_type=MESH)

        sent = []

        def send(k, slot, ref, to):
            cp = copy(k, slot, ref, to)
            cp.start()
            sent.append(cp)

        if not first_done:
            for k in range(n):
                send(k, 0, blk(k, (x, y), c), (*xn, c))
                send(k, 1, blk(k, (x, y), c), (*yn, c))
        for k in range(n):
            bx, by = blk(k, xn, c), blk(k, yn, c)
            if not first_done:
                copy(k, 0, bx, sibling).wait_recv()
            send(k, 2, part(k, bx, 0), (*yn, c))
            send(k, 4, bx, sibling)
            if not first_done:
                copy(k, 1, by, sibling).wait_recv()
            send(k, 3, part(k, by, 1), (*xn, c))
            send(k, 5, by, sibling)
        for k in range(n):
            d0, d1 = part(k, blk(k, dg, c), 0), part(k, blk(k, dg, c), 1)
            copy(k, 2, d0, sibling).wait_recv()
            send(k, 6, d0, sibling)
            copy(k, 3, d1, sibling).wait_recv()
            send(k, 7, d1, sibling)
        for k in range(n):
            copy(k, 4, blk(k, xn, 1 - c), sibling).wait_recv()
            copy(k, 5, blk(k, yn, 1 - c), sibling).wait_recv()
            copy(k, 6, part(k, blk(k, dg, 1 - c), 0), sibling).wait_recv()
            copy(k, 7, part(k, blk(k, dg, 1 - c), 1), sibling).wait_recv()
        for cp in sent:
            cp.wait_send()

    return pl.pallas_call(
        body, name="allgather_w_in",
        in_specs=[ANY] * n, out_specs=[ANY] * n,
        out_shape=[jax.ShapeDtypeStruct(b.shape, b.dtype) for b in bufs],
        input_output_aliases={k: k for k in range(n)},
        scratch_shapes=[pltpu.SemaphoreType.DMA((n, 8)), pltpu.SemaphoreType.DMA((n, 8))],
    )(*bufs)


HBM = pl.BlockSpec(memory_space=pltpu.HBM)
SEM = pl.BlockSpec(memory_space=pltpu.SEMAPHORE)
EFFECT = pltpu.SideEffectType.DATAFLOW_SIDE_EFFECTING


def _split_start(name, arrays, build, n_copies, after=()):
    na, nd = len(arrays), len(after)

    def body(*refs):
        send_sems, recv_sems = refs[na + nd], refs[na + nd + 1]
        for cp in build(refs[:na], send_sems, recv_sems):
            cp.start()
        refs[-1][...] = jnp.zeros((8, 128), F32)

    outs = pl.pallas_call(
        body, name=name,
        out_shape=(pltpu.SemaphoreType.DMA((n_copies,)), pltpu.SemaphoreType.DMA((n_copies,)),
                   *[pltpu.HBM(a.shape, a.dtype) for a in arrays], jax.ShapeDtypeStruct((8, 128), F32)),
        in_specs=[HBM] * na + [ANY] * nd,
        out_specs=(SEM, SEM, *[HBM] * na, pl.BlockSpec(memory_space=pltpu.VMEM)),
        input_output_aliases={i: 2 + i for i in range(na)},
        compiler_params=pltpu.CompilerParams(has_side_effects=EFFECT),
    )(*[pltpu.with_memory_space_constraint(a, pltpu.HBM) for a in arrays], *after)
    return outs[0], outs[1], list(outs[2:2 + na]), outs[-1]


def _split_wait(name, send_sems, recv_sems, arrays, build, after):
    na = len(arrays)

    def body(*refs):
        for cp in build(refs[:na], refs[na], refs[na + 1]):
            cp.wait_send()
            cp.wait_recv()

    outs = pl.pallas_call(
        body, name=name,
        out_shape=tuple(pltpu.HBM(a.shape, a.dtype) for a in arrays),
        in_specs=[HBM] * na + [SEM, SEM] + [ANY] * len(after),
        out_specs=tuple([HBM] * na),
        input_output_aliases={i: i for i in range(na)},
        compiler_params=pltpu.CompilerParams(has_side_effects=EFFECT),
    )(*arrays, send_sems, recv_sems, *after)
    return list(outs)


def _remote(src, dst, send_sems, recv_sems, i, to):
    return pltpu.make_async_remote_copy(src_ref=src, dst_ref=dst, send_sem=send_sems.at[i], recv_sem=recv_sems.at[i],
                                        device_id=to, device_id_type=MESH)


def _build_ag_first(refs, ss, rs):
    x, y, c = _place()
    cps = []
    for k, ref in enumerate(refs):
        blk = ref.at[4 * x + 2 * y + c]
        cps += [_remote(blk, blk, ss, rs, 2 * k, (1 - x, y, c)), _remote(blk, blk, ss, rs, 2 * k + 1, (x, 1 - y, c))]
    return cps


def _build_ag_ici(refs, ss, rs):
    x, y, c = _place()
    cps = []
    for k, ref in enumerate(refs):
        blk = ref.at[4 * x + 2 * y + c]
        for j, (px, py) in enumerate(_other_chips(x, y)):
            cps.append(_remote(blk, blk, ss, rs, 3 * k + j, (px, py, c)))
    return cps


def _build_ag_fwd(refs, ss, rs):
    x, y, c = _place()
    cps = []
    for k, ref in enumerate(refs):
        for j, (px, py) in enumerate(_other_chips(x, y)):
            blk = ref.at[4 * px + 2 * py + c]
            cps.append(_remote(blk, blk, ss, rs, 3 * k + j, (x, y, 1 - c)))
    return cps


def _build_rs_swap(refs, ss, rs):
    x, y, c = _place()
    n = len(refs) // 2
    return [_remote(refs[k].at[:, pl.ds(1 - c, 1)], refs[n + k], ss, rs, k, (x, y, 1 - c)) for k in range(n)]


def _build_rs_ici(refs, ss, rs):
    x, y, c = _place()
    n = len(refs) // 2
    me = 2 * x + y
    cps = []
    for k in range(n):
        for j, (px, py) in enumerate(_other_chips(x, y)):
            cps.append(_remote(refs[k].at[2 * px + py], refs[n + k].at[me], ss, rs, 3 * k + j, (px, py, c)))
    return cps


def _build_rs_share(refs, ss, rs):
    x, y, c = _place()
    return [_remote(ref.at[c], ref.at[c], ss, rs, k, (x, y, 1 - c)) for k, ref in enumerate(refs)]


def _allreduce_small(p, deps=()):
    R, C = p.shape
    nd = len(deps)

    def body(p_ref, *rest):
        gath_ref, sum_ref, send_sems, recv_sems, local_sem = rest[nd:]
        x, y, c = _place()
        me, sibling = (x, y, c), (x, y, 1 - c)
        chips = [(1 - x, y), (x, 1 - y), (1 - x, 1 - y)]

        def blk(px, py, pc):
            return gath_ref.at[4 * px + 2 * py + pc]

        def copy(k, block, to, src=None):
            return pltpu.make_async_remote_copy(
                src_ref=blk(*block) if src is None else src, dst_ref=blk(*block),
                send_sem=send_sems.at[k], recv_sem=recv_sems.at[k], device_id=to, device_id_type=MESH)

        mine = pltpu.make_async_copy(p_ref, blk(*me), local_sem)
        mine.start()
        first = [copy(0, me, sibling, src=p_ref)]
        first += [copy(1 + j, me, (*chip, c), src=p_ref) for j, chip in enumerate(chips)]
        for cp in first:
            cp.start()
        passed = [copy(4 + j, (*chip, c), sibling) for j, chip in enumerate(chips)]
        for j, chip in enumerate(chips):
            copy(1 + j, (*chip, c), me).wait_recv()
            passed[j].start()
        copy(0, sibling, me).wait_recv()
        for j, chip in enumerate(chips):
            copy(4 + j, (*chip, 1 - c), me).wait_recv()
        for cp in first + passed:
            cp.wait_send()
        mine.wait()
        s = gath_ref[0]
        for d in range(1, N_DEV):
            s = s + gath_ref[d]
        sum_ref[...] = s

    vm = pl.BlockSpec(memory_space=pltpu.VMEM)
    return pl.pallas_call(
        body, name="allreduce_small",
        in_specs=[vm] + [ANY] * nd, out_specs=[vm, vm],
        out_shape=[jax.ShapeDtypeStruct((N_DEV, R, C), F32), jax.ShapeDtypeStruct((R, C), F32)],
        scratch_shapes=[pltpu.SemaphoreType.DMA((7,)), pltpu.SemaphoreType.DMA((7,)), pltpu.SemaphoreType.DMA],
    )(p, *deps)[1]


def _rs_add_pair(p, r0, c_arr, name):
    _, _, hr, cols = p.shape
    tr = _tile(hr, 256, 8)

    def body(c_ref, p_ref, r_ref, q_ref):
        q_ref[...] = (p_ref[0].astype(F32) + r_ref[0].astype(F32)).astype(BF16)

    grid_spec = pltpu.PrefetchScalarGridSpec(
        num_scalar_prefetch=1, grid=(N_CHIPS, hr // tr),
        in_specs=[pl.BlockSpec((1, 1, tr, cols), lambda j, i, c_ref: (j, c_ref[0], i, 0)),
                  pl.BlockSpec((1, 1, tr, cols), lambda j, i, c_ref: (j, 0, i, 0))],
        out_specs=pl.BlockSpec((1, tr, cols), lambda j, i, c_ref: (j, i, 0)))
    return pl.pallas_call(
        body, name=name, grid_spec=grid_spec,
        out_shape=jax.ShapeDtypeStruct((N_CHIPS, hr, cols), BF16),
        compiler_params=_cparams(("parallel", "parallel")),
    )(c_arr, p, r0)


def _rs_add_chips(r1, q, place_arr, name):
    _, hr, cols = r1.shape
    tr = _tile(hr, 256, 8)

    def body(place_ref, r_ref, q_ref, o_ref):
        chip = place_ref[0]
        s = None
        for j in range(N_CHIPS):
            t = jnp.where(chip == j, q_ref[j], r_ref[j]).astype(F32)
            s = t if s is None else s + t
        o_ref[...] = s

    blk = pl.BlockSpec((N_CHIPS, tr, cols), lambda i, place_ref: (0, i, 0))
    grid_spec = pltpu.PrefetchScalarGridSpec(
        num_scalar_prefetch=1, grid=(hr // tr,), in_specs=[blk, blk],
        out_specs=pl.BlockSpec((None, tr, cols), lambda i, place_ref: (place_ref[1], i, 0)))
    return pl.pallas_call(
        body, name=name, grid_spec=grid_spec,
        out_shape=jax.ShapeDtypeStruct((2, hr, cols), F32),
        compiler_params=_cparams(("parallel",)),
    )(place_arr, r1, q)


def _pad_rows(a, rows):
    return jnp.pad(a, ((0, rows - a.shape[0]), (0, 0)))


def _pad_cols(a, cols):
    return jnp.pad(a, ((0, 0), (0, cols - a.shape[1])))


def _heads_to_rows(v):
    v = v.reshape(N_GROUPS, HEADS_PER_GROUP, 1)
    v = jnp.pad(v, ((0, 0), (0, 8 - HEADS_PER_GROUP), (0, 0)))
    return jnp.broadcast_to(v, (N_GROUPS, 8, CHUNK))


def _rows_to_heads(a):
    return jnp.sum(a[:, :HEADS_PER_GROUP, :], axis=-1).reshape(N_HEADS)


def _to_kernel_rows(a):
    C = a.shape[1]
    x0, b0, c0, s0 = D_SSM, 2 * D_SSM, 2 * D_SSM + 1024, D_SSM + D_XBC + N_HEADS
    xbc = jnp.concatenate([a[x0:b0].reshape(N_GROUPS, GW, C), a[b0:c0].reshape(N_GROUPS, N_STATE, C),
                           a[c0:c0 + 1024].reshape(N_GROUPS, N_STATE, C)], axis=1).reshape(D_XBC, C)
    sc = jnp.concatenate([a[s0 + k * D_MODEL:s0 + (k + 1) * D_MODEL].reshape(D_MODEL // SCB, SCB, C)
                          for k in range(3)], axis=1).reshape(3 * D_MODEL, C)
    return jnp.concatenate([a[:D_SSM], xbc, sc], axis=0)


HR_IN = 1568


def _shard_row_plan():
    segs = [(0, 0, 0, D_SSM)]
    for g in range(N_GROUPS):
        k0 = D_SSM + g * GXBC
        segs += [(0, k0, D_SSM + g * GW, GW), (0, k0 + GW, 2 * D_SSM + g * N_STATE, N_STATE),
                 (0, k0 + GW + N_STATE, 2 * D_SSM + 1024 + g * N_STATE, N_STATE)]
    segs.append((1, 0, D_SSM + D_XBC, N_HEADS))
    for j in range(D_MODEL // SCB):
        for k in range(3):
            segs.append((0, D_SSM + D_XBC + j * SC3 + k * SCB, D_SSM + D_XBC + N_HEADS + k * D_MODEL + j * SCB, SCB))
    cs = D_IN // N_CHIPS
    plan = []
    for src, s, o, n in segs:
        while n > 0:
            chip, loc = divmod(o, cs)
            half, row = divmod(loc, HR_IN)
            m = min(n, cs - loc, HR_IN - row)
            plan.append((src, s, chip, half, row, m))
            s, o, n = s + m, o + m, n - m
    return plan


SCATTER_ROWS = 512
SCATTER_SLOTS = 4


def _scatter_rows_to_shards(k_main, k_dt):
    C = k_main.shape[1]
    pieces = []
    for src, s, chip, half, row, n in _shard_row_plan():
        for o in range(0, n, SCATTER_ROWS):
            pieces.append((src, s + o, chip, half, row + o, min(SCATTER_ROWS, n - o)))
    S, lag, N = SCATTER_SLOTS, SCATTER_SLOTS // 2, len(pieces)

    def body(m_ref, d_ref, o_ref, buf, in_sems, out_sems):
        def cin(i):
            src, s, _, _, _, n = pieces[i]
            return pltpu.make_async_copy((d_ref if src else m_ref).at[pl.ds(s, n)],
                                         buf.at[i % S, pl.ds(0, n)], in_sems.at[i % S])

        def cout(i):
            _, _, chip, half, row, n = pieces[i]
            return pltpu.make_async_copy(buf.at[i % S, pl.ds(0, n)],
                                         o_ref.at[chip, half, pl.ds(row, n)], out_sems.at[i % S])

        for i in range(N + lag):
            if i < N:
                if i >= S:
                    cout(i - S).wait()
                cin(i).start()
            j = i - lag
            if 0 <= j < N:
                cin(j).wait()
                cout(j).start()
        for j in range(max(0, N - S), N):
            cout(j).wait()

    return pl.pallas_call(
        body, name="scatter_dw_in_rows", in_specs=[ANY, ANY], out_specs=ANY,
        out_shape=jax.ShapeDtypeStruct((N_CHIPS, 2, HR_IN, C), k_main.dtype),
        scratch_shapes=[pltpu.VMEM((S, SCATTER_ROWS, C), k_main.dtype),
                        pltpu.SemaphoreType.DMA((S,)), pltpu.SemaphoreType.DMA((S,))],
        compiler_params=_cparams(),
    )(k_main, k_dt)


def _to_kernel_xbc(a):
    R = a.shape[0]
    return jnp.concatenate([a[:, :D_SSM].reshape(R, N_GROUPS, GW), a[:, D_SSM:D_SSM + 1024].reshape(R, N_GROUPS, N_STATE),
                            a[:, D_SSM + 1024:].reshape(R, N_GROUPS, N_STATE)], axis=2).reshape(R, D_XBC)


def _from_kernel_xbc(a):
    R = a.shape[0]
    g = a.reshape(R, N_GROUPS, GXBC)
    return jnp.concatenate([g[:, :, :GW].reshape(R, D_SSM), g[:, :, GW:GW + N_STATE].reshape(R, 1024),
                            g[:, :, GW + N_STATE:].reshape(R, 1024)], axis=1)


def kernel(x, norm_mix_g, w_in, ssm_conv_w, ssm_conv_b, ssm_dt_bias, ssm_A_log, ssm_D, ssm_norm_g, sc_conv_w, w_out, norm_ffn_g, w_gate, w_up, w_down, norm_final_g, loss_target, m_norm_mix_g, m_w_in, m_ssm_conv_w, m_ssm_conv_b, m_ssm_dt_bias, m_ssm_A_log, m_ssm_D, m_ssm_norm_g, m_sc_conv_w, m_w_out, m_norm_ffn_g, m_w_gate, m_w_up, m_w_down, m_norm_final_g, v_norm_mix_g, v_w_in, v_ssm_conv_w, v_ssm_conv_b, v_ssm_dt_bias, v_ssm_A_log, v_ssm_D, v_ssm_norm_g, v_sc_conv_w, v_w_out, v_norm_ffn_g, v_w_gate, v_w_up, v_w_down, v_norm_final_g):
    T = x.shape[1]
    xt = x[0]
    tgt = loss_target[0]
    cx, cy, cc = lax.axis_index("x"), lax.axis_index("y"), lax.axis_index("c")
    chip = 2 * cx + cy
    c_arr = jnp.reshape(cc, (1,)).astype(jnp.int32)
    chip_arr = jnp.reshape(chip, (1,)).astype(jnp.int32)
    place_arr = jnp.stack([chip, cc]).astype(jnp.int32)

    big = [w_in[0].T, w_out[0], w_gate[0], w_up[0], w_down[0]]
    names = ["w_in", "w_out", "w_gate", "w_up", "w_down"]
    gb_in = _cast_into_gather(big[0], chip_arr, "cast_w_in", split_cols=True)
    cs_in, cs_conv = D_IN // N_CHIPS, D_XBC // N_CHIPS
    cw = jnp.stack([_pad_rows(ssm_conv_w[0], 8), _pad_cols(_pad_rows(sc_conv_w[0], 8), cs_conv)])
    cw_buf = lax.dynamic_update_slice(jnp.zeros((N_DEV, 8, cs_conv), F32), cw, (2 * chip, 0, 0))
    f_ss, f_rs, f_arr, f_tok = _split_start("ag_in_first_start", [gb_in, cw_buf], _build_ag_first, 4)
    gbufs = [None] + [_cast_into_gather(w, chip_arr, "cast_" + nm, deps=[f_tok]) for w, nm in zip(big[1:], names[1:])]
    n1 = _rmsnorm_fwd(xt, _tie(norm_mix_g, f_tok, "tie_ag_first"), "rmsnorm_mix")
    f_arr = _split_wait("ag_in_first_wait", f_ss, f_rs, f_arr, _build_ag_first, after=gbufs[1:] + [n1])
    g_in, cw_all = _allgather_inplace(f_arr, [("rows", (cs_in // 32) * 16), ("cols", cs_conv // 2)], first_done=True)
    cw_all = cw_all.reshape(N_CHIPS, 2, 8, cs_conv)
    ssm_w8 = _to_kernel_xbc(cw_all[:, 0].transpose(1, 0, 2).reshape(8, D_XBC))
    sc_w8 = cw_all[:, 1, :, :D_MODEL // N_CHIPS].transpose(1, 0, 2).reshape(8, D_MODEL)
    ssm_bk = _to_kernel_xbc(ssm_conv_b)
    wt = g_in.reshape(N_CHIPS, 2, cs_in, D_MODEL // 2).transpose(0, 2, 1, 3).reshape(D_IN, D_MODEL)
    wt_main = _to_kernel_rows(wt)
    wt_dt = _pad_rows(wt[D_SSM + D_XBC:D_SSM + D_XBC + N_HEADS], DT_PAD)
    ag_ss, ag_rs, ag_bufs, ag_tok = _split_start("ag_ici_start", gbufs[1:], _build_ag_ici, 12, after=[g_in, cw_all])

    bias_rows = _heads_to_rows(ssm_dt_bias[0])
    alog_rows = _heads_to_rows(ssm_A_log[0])
    drep = jnp.repeat(ssm_D[0], HEADDIM).reshape(1, D_SSM)

    (proj,) = _matmul([(n1, wt_main)], tb=True, out_dtypes=[F32], name="mm_proj", deps=[ag_tok])
    (dt_raw,) = _matmul([(n1, wt_dt)], tb=True, out_dtypes=[F32], name="mm_proj_dt")
    xbc = _ssm_conv_fwd(proj, ssm_w8, ssm_bk)
    dtr = jnp.pad(dt_raw[:, :N_HEADS].T.reshape(N_GROUPS, HEADS_PER_GROUP, T), ((0, 0), (0, 4), (0, 0)))
    y_ssd, hs = _ssd_fwd(xbc, dtr, bias_rows, alog_rows, drep)
    ag_bufs = _split_wait("ag_ici_wait", ag_ss, ag_rs, ag_bufs, _build_ag_ici, after=[y_ssd])
    fw_ss, fw_rs, fw_bufs, fw_tok = _split_start("ag_fwd_start", ag_bufs, _build_ag_fwd, 12)
    y_mix = _shortconv_fwd(proj, sc_w8, _gated_norm_fwd(y_ssd, proj, _tie(ssm_norm_g, fw_tok, "tie_ag_fwd")))
    gath = _split_wait("ag_fwd_wait", fw_ss, fw_rs, fw_bufs, _build_ag_fwd, after=[y_mix])
    w_out_f = gath[0].reshape(2 * D_MODEL, D_MODEL)
    w_gate3 = gath[1].reshape(N_CHIPS, D_MODEL, D_FF // N_CHIPS)
    w_up3 = gath[2].reshape(N_CHIPS, D_MODEL, D_FF // N_CHIPS)
    w_down_f = gath[3].reshape(D_FF, D_MODEL)
    (h1,) = _matmul([(y_mix, w_out_f)], out_dtypes=[F32], name="mm_out", extras=[xt],
                    epilogue=lambda acc, res: (acc + res,))
    n2 = _rmsnorm_fwd(h1, norm_ffn_g, "rmsnorm_ffn")
    g_act, u_act, a_act = _ffn_fwd(n2, w_gate3, w_up3)
    (h2,) = _matmul([(a_act, w_down_f)], out_dtypes=[F32], name="mm_down", extras=[h1],
                    epilogue=lambda acc, res: (acc + res,))

    dh2, dh2b, dg_final, loss_part = _loss_and_final_bwd(h2, tgt, norm_final_g.reshape(1, D_MODEL))
    dg_act, du_act = _matmul([(dh2b, w_down_f)], tb=True, out_dtypes=[BF16, BF16], name="mm_down_bwd",
                             tn=512, extras=[g_act, u_act], epilogue=_swiglu_bwd, nsub=2)
    (dw_down,) = _matmul([(a_act, dh2b)], ta=True, out_dtypes=[BF16], name="mm_dw_down", tm=1408, tn=512)
    (dn2,) = _matmul([(dg_act, w_gate3), (du_act, w_up3)], tb=True, b3d=True, out_dtypes=[BF16],
                     name="mm_ffn_in_bwd")
    (dw_gate,) = _matmul([(n2, dg_act)], ta=True, out_dtypes=[BF16], name="mm_dw_gate", tm=512, tn=1408,
                         col_shards=True)
    (dw_up,) = _matmul([(n2, du_act)], ta=True, out_dtypes=[BF16], name="mm_dw_up", tm=512, tn=1408,
                       col_shards=True)
    dh1, dh1b, dg_ffn = _rmsnorm_bwd(dn2, h1, norm_ffn_g, dh2, "rmsnorm_ffn_bwd")
    (dw_out,) = _matmul([(y_mix, dh1b)], ta=True, out_dtypes=[BF16], name="mm_dw_out")

    def halves(g):
        return g.reshape(N_CHIPS, 2, g.shape[1] // 2, g.shape[2])

    def landing(shape, dtype):
        return lax.empty(shape, dtype)

    names1 = names[1:]
    ps1 = [halves(dw_out.reshape(N_CHIPS, -1, D_MODEL)), halves(dw_gate), halves(dw_up),
           halves(dw_down.reshape(N_CHIPS, -1, D_MODEL))]
    r0_1 = [landing((N_CHIPS, 1) + p.shape[2:], p.dtype) for p in ps1]
    sw_ss, sw_rs, sw_arr, sw_tok = _split_start("rs1_swap_start", ps1 + r0_1, _build_rs_swap, 4)
    (dmix,) = _matmul([(dh1b, w_out_f)], tb=True, out_dtypes=[BF16], name="mm_out_bwd", deps=[sw_tok])
    dproj, dw_sc = _shortconv_bwd(dmix, proj, sc_w8)
    dy_ssd, dproj, dg_ssmnorm = _gated_norm_bwd(dmix, y_ssd, proj, ssm_norm_g, dproj)
    sw_arr = _split_wait("rs1_swap_wait", sw_ss, sw_rs, sw_arr, _build_rs_swap, after=[dy_ssd])
    qs1 = [_rs_add_pair(p, r, c_arr, "rs_add_pair_" + nm) for p, r, nm in zip(sw_arr[:4], sw_arr[4:], names1)]
    r1_1 = [landing(q.shape, BF16) for q in qs1]
    ic_ss, ic_rs, ic_arr, ic_tok = _split_start("rs1_ici_start", qs1 + r1_1, _build_rs_ici, 12)
    dxbc_act, ddtr, dbias_acc, dalog_acc, dD_acc = _ssd_bwd(
        xbc, dtr, bias_rows, alog_rows, _tie(drep, ic_tok, "tie_rs1_ici"), dy_ssd, hs)
    dproj, dw_ssmconv, db_ssmconv = _ssm_conv_bwd(dxbc_act, proj, ssm_w8, ssm_bk, dproj)
    dw_ssmconv, db_ssmconv = _from_kernel_xbc(dw_ssmconv), _from_kernel_xbc(db_ssmconv)
    ic_arr = _split_wait("rs1_ici_wait", ic_ss, ic_rs, ic_arr, _build_rs_ici, after=[dproj])
    g1 = [_rs_add_chips(r, q, place_arr, "rs_add_chips_" + nm) for q, r, nm in zip(ic_arr[:4], ic_arr[4:], names1)]
    sh_ss, sh_rs, sh_arr, sh_tok = _split_start("rs1_share_start", g1, _build_rs_share, 4)

    ddt_raw = _pad_cols(ddtr[:, :HEADS_PER_GROUP, :].reshape(N_HEADS, T).T, DT_PAD).astype(BF16)
    (dwt_main,) = _matmul([(dproj, n1)], ta=True, out_dtypes=[F32], name="mm_dw_main", deps=[sh_tok])
    (dwt_dt,) = _matmul([(ddt_raw, n1)], ta=True, out_dtypes=[F32], name="mm_dw_dt")
    p_in = _scatter_rows_to_shards(dwt_main, dwt_dt)
    s2_ss, s2_rs, s2_arr, s2_tok = _split_start(
        "rs2_swap_start", [p_in, landing((N_CHIPS, 1) + p_in.shape[2:], F32)], _build_rs_swap, 1)
    mt = T // _tile(T, 1024)
    mt_a = max(mt // 4, 1)
    (dn1a,) = _matmul([(dproj, wt_main)], out_dtypes=[F32], name="mm_proj_bwd_a", deps=[s2_tok],
                      m_tiles=(0, mt_a))
    g1 = _split_wait("rs1_share_wait", sh_ss, sh_rs, sh_arr, _build_rs_share, after=[dn1a])
    s2_arr = _split_wait("rs2_swap_wait", s2_ss, s2_rs, s2_arr, _build_rs_swap, after=[dn1a])
    q_in = _rs_add_pair(s2_arr[0], s2_arr[1], c_arr, "rs_add_pair_w_in")
    i2_ss, i2_rs, i2_arr, i2_tok = _split_start(
        "rs2_ici_start", [q_in, landing(q_in.shape, BF16)], _build_rs_ici, 3)
    if mt > mt_a:
        (dn1a,) = _matmul([(dproj, wt_main)], out_dtypes=[F32], name="mm_proj_bwd_b", deps=[i2_tok],
                          m_tiles=(mt_a, mt - mt_a), out_buf=dn1a)
    (dn1,) = _matmul([(ddt_raw, wt_dt)], out_dtypes=[BF16], name="mm_proj_dt_bwd", extras=[dn1a],
                     epilogue=lambda acc, res: (acc + res,), deps=[i2_tok])
    dx, _, dg_mix = _rmsnorm_bwd(dn1, xt, norm_mix_g, dh1, "rmsnorm_mix_bwd")

    big_m = [m_w_in[0].T, m_w_out[0], m_w_gate[0], m_w_up[0], m_w_down[0]]
    big_v = [v_w_in[0].T, v_w_out[0], v_w_gate[0], v_w_up[0], v_w_down[0]]
    big_grads = [None] + [g.reshape(w.shape) for g, w in zip(g1, big[1:])]
    big_out = {}
    for k in range(1, 5):
        *big_out[names[k]], big_grads[k] = _adamw(big[k], big_grads[k], big_m[k], big_v[k], "adamw_" + names[k],
                                                   deps=[i2_tok], emit_g=True)
    i2_arr = _split_wait("rs2_ici_wait", i2_ss, i2_rs, i2_arr, _build_rs_ici, after=[big_out[names[4]][0], dx])
    g_in_red = _rs_add_chips(i2_arr[1], i2_arr[0], place_arr, "rs_add_chips_w_in")
    s3_ss, s3_rs, s3_arr, s3_tok = _split_start("rs2_share_start", [g_in_red], _build_rs_share, 1)

    dD = jnp.sum(dD_acc.reshape(N_HEADS, HEADDIM), axis=-1)
    heads_row = jnp.concatenate([_rows_to_heads(dbias_acc), _rows_to_heads(dalog_acc), dD,
                                 loss_part.reshape(1)]).reshape(1, -1)
    small = jnp.concatenate([
        dw_ssmconv,
        _pad_cols(dw_sc, D_XBC),
        db_ssmconv,
        jnp.concatenate([dg_mix, dg_ssmnorm], axis=1),
        jnp.concatenate([dg_ffn, dg_final], axis=1),
        _pad_cols(heads_row, D_XBC),
        jnp.zeros((4, D_XBC), F32),
    ], axis=0)
    tot = _allreduce_small(small, deps=[s3_tok])
    loss = tot[19, 3 * N_HEADS]

    cs_ssm, cs_sc = D_XBC // N_CHIPS, D_MODEL // N_CHIPS
    g_ssm_conv = lax.dynamic_slice(tot[0:K_SSM], (0, chip * cs_ssm), (K_SSM, cs_ssm))
    g_sc_conv = lax.dynamic_slice(tot[8:8 + K_SC, :D_MODEL], (0, chip * cs_sc), (K_SC, cs_sc))
    small_grads = {
        "norm_mix_g": tot[17:18, :D_MODEL], "ssm_conv_w": g_ssm_conv, "ssm_conv_b": tot[16:17],
        "ssm_dt_bias": tot[19:20, 0:N_HEADS], "ssm_A_log": tot[19:20, N_HEADS:2 * N_HEADS],
        "ssm_D": tot[19:20, 2 * N_HEADS:3 * N_HEADS], "ssm_norm_g": tot[17:18, D_MODEL:],
        "sc_conv_w": g_sc_conv, "norm_ffn_g": tot[18:19, :D_MODEL], "norm_final_g": tot[18:19, D_MODEL:],
    }
    small_w = {"norm_mix_g": (norm_mix_g, m_norm_mix_g, v_norm_mix_g),
               "ssm_conv_w": (ssm_conv_w[0], m_ssm_conv_w[0], v_ssm_conv_w[0]),
               "ssm_conv_b": (ssm_conv_b, m_ssm_conv_b, v_ssm_conv_b),
               "ssm_dt_bias": (ssm_dt_bias, m_ssm_dt_bias, v_ssm_dt_bias),
               "ssm_A_log": (ssm_A_log, m_ssm_A_log, v_ssm_A_log),
               "ssm_D": (ssm_D, m_ssm_D, v_ssm_D),
               "ssm_norm_g": (ssm_norm_g, m_ssm_norm_g, v_ssm_norm_g),
               "sc_conv_w": (sc_conv_w[0], m_sc_conv_w[0], v_sc_conv_w[0]),
               "norm_ffn_g": (norm_ffn_g, m_norm_ffn_g, v_norm_ffn_g),
               "norm_final_g": (norm_final_g.reshape(1, -1), m_norm_final_g.reshape(1, -1),
                                v_norm_final_g.reshape(1, -1))}
    PW = 1024
    order = list(small_w)

    def pack(arrs):
        rows = []
        for a in arrs:
            flat = a.reshape(-1)
            n = -(-flat.shape[0] // PW) * PW
            rows.append(jnp.pad(flat, (0, n - flat.shape[0])).reshape(-1, PW))
        slab = jnp.concatenate(rows, axis=0)
        return _pad_rows(slab, -(-slab.shape[0] // 8) * 8)

    wp = pack([small_w[k][0] for k in order])
    mp = pack([small_w[k][1] for k in order])
    vp = pack([small_w[k][2] for k in order])
    gp = pack([small_grads[k] for k in order])
    sd, sm, sv = _adamw(wp, gp, mp, vp, "adamw_small")

    def unpack(slab):
        out, row = {}, 0
        for k in order:
            shape = small_w[k][0].shape
            size = 1
            for s in shape:
                size *= s
            nr = -(-size // PW)
            out[k] = slab[row:row + nr].reshape(-1)[:size].reshape(shape)
            row += nr
        return out

    s_delta, s_m, s_v = unpack(sd), unpack(sm), unpack(sv)

    (g_in_full,) = _split_wait("rs2_share_wait", s3_ss, s3_rs, s3_arr, _build_rs_share, after=[sd])
    d_t, m_t, v_t, g_t = _adamw(big[0], g_in_full.reshape(2 * HR_IN, D_MODEL), big_m[0], big_v[0],
                                "adamw_" + names[0], emit_g=True)
    big_grads[0] = g_t.T
    big_out[names[0]] = (d_t.T, m_t.T, v_t.T)
    big_g = dict(zip(names, big_grads))

    weight_order = ["norm_mix_g", "w_in", "ssm_conv_w", "ssm_conv_b", "ssm_dt_bias", "ssm_A_log", "ssm_D",
                    "ssm_norm_g", "sc_conv_w", "w_out", "norm_ffn_g", "w_gate", "w_up", "w_down", "norm_final_g"]
    lead = {"ssm_conv_w", "sc_conv_w", "w_in", "w_out", "w_gate", "w_up", "w_down"}

    def shaped(nm, a):
        if nm == "norm_final_g":
            return a.reshape(D_MODEL)
        return a[None] if nm in lead else a

    grads, deltas, new_m, new_v = [], [], [], []
    for nm in weight_order:
        if nm in big_out:
            g, (d, m, v) = big_g[nm], big_out[nm]
        else:
            g, d, m, v = small_grads[nm], s_delta[nm], s_m[nm], s_v[nm]
        grads.append(shaped(nm, g))
        deltas.append(shaped(nm, d))
        new_m.append(shaped(nm, m))
        new_v.append(shaped(nm, v))
    return (loss, dx[None], *grads, *deltas, *new_m, *new_v)


def _swiglu_bwd(da, dg_factor, du_factor):
    return da * dg_factor.astype(F32), da * du_factor.astype(F32)


def _ffn_fwd(n2, w_gate, w_up):
    T, K = n2.shape
    tn = w_gate.shape[2]
    N = N_CHIPS * tn
    tm = _tile(T, 512)
    sub = _tile(tm, 256)

    def body(a_ref, wg_ref, wu_ref, g_ref, u_ref, act_ref):
        for s in range(tm // sub):
            rows = pl.ds(s * sub, sub)
            a = a_ref[rows, :]
            g = jnp.dot(a, wg_ref[...], preferred_element_type=F32)
            u = jnp.dot(a, wu_ref[...], preferred_element_type=F32)
            sig = _sigmoid(g)
            sg = g * sig
            g_ref[rows, :] = (u * (sig * (1.0 + g - sg))).astype(BF16)
            u_ref[rows, :] = sg.astype(BF16)
            act_ref[rows, :] = (sg * u).astype(BF16)

    a_spec = pl.BlockSpec((tm, K), lambda j, i: (i, 0))
    b_spec = pl.BlockSpec((None, K, tn), lambda j, i: (j, 0, 0))
    o_spec = pl.BlockSpec((tm, tn), lambda j, i: (i, j))
    return pl.pallas_call(
        body, name="ffn_fwd", grid=(N // tn, T // tm),
        in_specs=[a_spec, b_spec, b_spec], out_specs=[o_spec] * 3,
        out_shape=[jax.ShapeDtypeStruct((T, N), BF16)] * 3,
        compiler_params=_cparams(("parallel", "parallel")),
    )(n2, w_gate, w_up)
```

```python
import functools

import jax
import jax.numpy as jnp
from jax import lax
from jax.experimental import pallas as pl
from jax.experimental.pallas import tpu as pltpu

F32 = jnp.float32
BF16 = jnp.bfloat16
MESH = pl.DeviceIdType.MESH

D_MODEL = 2048
D_SSM = 2048
HEADDIM = 64
N_HEADS = 32
N_GROUPS = 8
HEADS_PER_GROUP = 4
N_STATE = 128
CHUNK = 128
K_SSM = 4
K_SC = 3
D_XBC = 4096
D_FF = 5632
D_IN = 12320
D_MAIN = 12288
OFF_XBC, OFF_CB, OFF_CC, OFF_CX = 2048, 6144, 8192, 10240
DT_PAD = 128
EPS = 1e-5
N_CHIPS = 4
N_DEV = 8

ADAM_LR = 0.001
ADAM_B1 = 0.9
ADAM_B2 = 0.999
ADAM_EPS = 1e-08
ADAM_WD = 0.01
ADAM_STEP = 10

V7X_VMEM_BYTES = 64 * 1024 * 1024
VMEM_LIMIT = V7X_VMEM_BYTES - 8 * 1024 * 1024


def _cparams(sem=None):
    if sem is None:
        return pltpu.CompilerParams(vmem_limit_bytes=VMEM_LIMIT)
    return pltpu.CompilerParams(dimension_semantics=sem, vmem_limit_bytes=VMEM_LIMIT)


def _tile(dim, pref, unit=128):
    best = None
    t = unit
    while t <= min(dim, pref):
        if dim % t == 0:
            best = t
        t += unit
    return best if best is not None else dim


def _sigmoid(x):
    return 1.0 / (1.0 + jnp.exp(-x))


def _silu(x):
    return x * _sigmoid(x)


def _dsilu(x):
    s = _sigmoid(x)
    return s * (1.0 + x * (1.0 - s))


def _softplus(x):
    return jnp.maximum(x, 0.0) + jnp.log(1.0 + jnp.exp(-jnp.abs(x)))


MATMUL_VMEM_BUDGET = 44 * 1024 * 1024


def _matmul(pairs, *, ta=False, tb=False, out_dtypes, name, tm=1024, tn=1024, tk=None, extras=(), epilogue=None,
            deps=(), col_shards=False, nsub=1, b3d=False, m_tiles=None, out_buf=None):
    a0, b0 = pairs[0]
    M, K = (a0.shape[1], a0.shape[0]) if ta else a0.shape
    if b3d:
        N = b0.shape[1] if tb else b0.shape[0] * b0.shape[2]
        tk, tn = (b0.shape[2], tn) if tb else (tk, b0.shape[2])
    else:
        N = b0.shape[0] if tb else b0.shape[1]
    tm, tn = _tile(M, tm, 8 if M % 128 else 128), _tile(N, tn)
    npair, nex, ndep, nout = len(pairs), len(extras), len(deps), len(out_dtypes)
    if tk is None:
        fixed = 2 * tm * tn * (sum(jnp.dtype(d).itemsize for d in out_dtypes) + sum(e.dtype.itemsize for e in extras))
        tk = K
        while tk > 128 and (K % tk or tk % 128 or
                            fixed + 2 * npair * 2 * tk * (tm + tn) + (tm * tn * 4 if tk < K else 0) > MATMUL_VMEM_BUDGET):
            tk -= 128
    else:
        tk = _tile(K, tk)
    nk = K // tk
    if nk > 1 or tm % nsub or (tm // nsub) % 128:
        nsub = 1
    sub = tm // nsub
    dims = (((0 if ta else 1,), (1 if tb else 0,)), ((), ()))
    i0, mi = m_tiles if m_tiles is not None else (0, M // tm)
    nbuf = 0 if out_buf is None else 1

    def body(*refs):
        a_refs = refs[0:2 * npair:2]
        b_refs = refs[1:2 * npair:2]
        ex_refs = refs[2 * npair:2 * npair + nex]
        o_refs = refs[2 * npair + nex + ndep + nbuf:2 * npair + nex + ndep + nbuf + nout]

        def dots(rows):
            s = None
            for a_ref, b_ref in zip(a_refs, b_refs):
                a = a_ref[...] if rows is None else (a_ref[:, rows] if ta else a_ref[rows, :])
                d = lax.dot_general(a, b_ref[...], dims, preferred_element_type=F32)
                s = d if s is None else s + d
            return s

        def finish(r, rows):
            ex = [e[...] if rows is None else e[rows, :] for e in ex_refs]
            outs = (r,) if epilogue is None else epilogue(r, *ex)
            for o_ref, o in zip(o_refs, outs):
                if rows is None:
                    o_ref[...] = o.astype(o_ref.dtype)
                else:
                    o_ref[rows, :] = o.astype(o_ref.dtype)

        if nk == 1:
            for s in range(nsub):
                rows = None if nsub == 1 else pl.ds(s * sub, sub)
                finish(dots(rows), rows)
            return

        acc = refs[-1]
        k = pl.program_id(2)

        @pl.when(k == 0)
        def _():
            acc[...] = dots(None)

        @pl.when(jnp.logical_and(k > 0, k < nk - 1))
        def _():
            acc[...] += dots(None)

        @pl.when(k == nk - 1)
        def _():
            finish(acc[...] + dots(None), None)

    a_spec = (pl.BlockSpec((tk, tm), lambda i, j, k: (k, i + i0)) if ta
              else pl.BlockSpec((tm, tk), lambda i, j, k: (i + i0, k)))
    if b3d:
        b_spec = (pl.BlockSpec((None, tn, tk), lambda i, j, k: (k, j, 0)) if tb
                  else pl.BlockSpec((None, tk, tn), lambda i, j, k: (j, k, 0)))
    else:
        b_spec = (pl.BlockSpec((tn, tk), lambda i, j, k: (j, k)) if tb
                  else pl.BlockSpec((tk, tn), lambda i, j, k: (k, j)))
    e_spec = pl.BlockSpec((tm, tn), lambda i, j, k: (i + i0, j))
    if col_shards:
        o_spec = pl.BlockSpec((None, tm, tn), lambda i, j, k: (j, i + i0, 0))
        o_shape = (N // tn, M, tn)
    else:
        o_spec, o_shape = e_spec, (M, N)
    args, in_specs = [], []
    for a, b in pairs:
        args += [a, b]
        in_specs += [a_spec, b_spec]
    args += list(extras) + list(deps) + ([] if out_buf is None else [out_buf])
    in_specs += [e_spec] * nex + [ANY] * (ndep + nbuf)
    outs = pl.pallas_call(
        body,
        name=name,
        grid=(mi, N // tn, nk),
        in_specs=in_specs,
        out_specs=[o_spec] * nout,
        out_shape=[jax.ShapeDtypeStruct(o_shape, dt) for dt in out_dtypes],
        input_output_aliases={} if out_buf is None else {len(args) - 1: 0},
        scratch_shapes=[pltpu.VMEM((tm, tn), F32)] if nk > 1 else [],
        compiler_params=_cparams(("parallel", "parallel", "arbitrary")),
    )(*args)
    return outs


def _cast_into_gather(w, chip_arr, name, split_cols=False, deps=()):
    R, C = w.shape
    hr, hc = (R, C // 2) if split_cols else (R // 2, C)
    tr = _tile(hr, 512, 8)
    nb = hr // tr

    def body(chip_ref, w_ref, *rest):
        rest[-1][...] = w_ref[...].astype(BF16)

    in_map = (lambda h, i, chip_ref: (i, h)) if split_cols else (lambda h, i, chip_ref: (h * nb + i, 0))
    grid_spec = pltpu.PrefetchScalarGridSpec(
        num_scalar_prefetch=1, grid=(2, nb),
        in_specs=[pl.BlockSpec((tr, hc), in_map)] + [ANY] * len(deps),
        out_specs=pl.BlockSpec((None, tr, hc), lambda h, i, chip_ref: (2 * chip_ref[0] + h, i, 0)))
    return pl.pallas_call(
        body, name=name, grid_spec=grid_spec,
        out_shape=jax.ShapeDtypeStruct((N_DEV, hr, hc), BF16),
        compiler_params=_cparams(("parallel", "parallel")),
    )(chip_arr, w, *deps)


def _tie(small, token, name):
    def body(s_ref, t_ref, o_ref):
        o_ref[...] = s_ref[...]

    vm = pl.BlockSpec(memory_space=pltpu.VMEM)
    return pl.pallas_call(body, name=name, in_specs=[vm, ANY], out_specs=vm,
                          out_shape=jax.ShapeDtypeStruct(small.shape, small.dtype))(small, token)


def _rmsnorm_fwd(x, g, name):
    T, D = x.shape
    tt = _tile(T, 256)

    def body(x_ref, g_ref, n_ref):
        xv = x_ref[...]
        r = lax.rsqrt(jnp.mean(xv * xv, axis=-1, keepdims=True) + EPS)
        n_ref[...] = (xv * r * g_ref[...]).astype(BF16)

    return pl.pallas_call(
        body, name=name, grid=(T // tt,),
        in_specs=[pl.BlockSpec((tt, D), lambda i: (i, 0)), pl.BlockSpec((1, D), lambda i: (0, 0))],
        out_specs=pl.BlockSpec((tt, D), lambda i: (i, 0)),
        out_shape=jax.ShapeDtypeStruct((T, D), BF16),
        compiler_params=_cparams(("parallel",)),
    )(x, g)


def _rmsnorm_bwd(dn, x, g, res, name):
    T, D = x.shape
    tt = _tile(T, 256)

    def body(dn_ref, x_ref, g_ref, res_ref, dx_ref, dxb_ref, dg_ref):
        @pl.when(pl.program_id(0) == 0)
        def _():
            dg_ref[...] = jnp.zeros_like(dg_ref)

        xv = x_ref[...]
        dy = dn_ref[...].astype(F32)
        r = lax.rsqrt(jnp.mean(xv * xv, axis=-1, keepdims=True) + EPS)
        xhat = xv * r
        dxh = dy * g_ref[...]
        dx = res_ref[...] + r * (dxh - xhat * jnp.mean(dxh * xhat, axis=-1, keepdims=True))
        dx_ref[...] = dx
        dxb_ref[...] = dx.astype(BF16)
        dg_ref[...] += jnp.sum(dy * xhat, axis=0, keepdims=True)

    tok = pl.BlockSpec((tt, D), lambda i: (i, 0))
    vec = pl.BlockSpec((1, D), lambda i: (0, 0))
    return pl.pallas_call(
        body, name=name, grid=(T // tt,),
        in_specs=[tok, tok, vec, tok],
        out_specs=[tok, tok, vec],
        out_shape=[jax.ShapeDtypeStruct((T, D), F32), jax.ShapeDtypeStruct((T, D), BF16),
                   jax.ShapeDtypeStruct((1, D), F32)],
        compiler_params=_cparams(("arbitrary",)),
    )(dn, x, g, res)


def _loss_and_final_bwd(h2, target, gf):
    T, D = h2.shape
    tt = _tile(T, 256)

    def body(h_ref, t_ref, g_ref, dh_ref, dhb_ref, dg_ref, loss_ref):
        @pl.when(pl.program_id(0) == 0)
        def _():
            dg_ref[...] = jnp.zeros_like(dg_ref)
            loss_ref[...] = jnp.zeros_like(loss_ref)

        xv = h_ref[...]
        r = lax.rsqrt(jnp.mean(xv * xv, axis=-1, keepdims=True) + EPS)
        xhat = xv * r
        err = xhat * g_ref[...] - t_ref[...]
        loss_ref[...] += 0.5 * jnp.sum(jnp.mean(err * err, axis=-1, keepdims=True), axis=0, keepdims=True)
        dy = err * (1.0 / D)
        dxh = dy * g_ref[...]
        dx = r * (dxh - xhat * jnp.mean(dxh * xhat, axis=-1, keepdims=True))
        dh_ref[...] = dx
        dhb_ref[...] = dx.astype(BF16)
        dg_ref[...] += jnp.sum(dy * xhat, axis=0, keepdims=True)

    tok = pl.BlockSpec((tt, D), lambda i: (i, 0))
    vec = pl.BlockSpec((1, D), lambda i: (0, 0))
    return pl.pallas_call(
        body, name="loss_final_bwd", grid=(T // tt,),
        in_specs=[tok, tok, vec],
        out_specs=[tok, tok, vec, pl.BlockSpec((1, 1), lambda i: (0, 0))],
        out_shape=[jax.ShapeDtypeStruct((T, D), F32), jax.ShapeDtypeStruct((T, D), BF16),
                   jax.ShapeDtypeStruct((1, D), F32), jax.ShapeDtypeStruct((1, 1), F32)],
        compiler_params=_cparams(("arbitrary",)),
    )(h2, target, gf)


def _gated_norm_fwd(y, proj, g):
    T, D = y.shape
    tt = _tile(T, 256)

    def body(y_ref, z_ref, g_ref, o_ref):
        yg = y_ref[...] * _silu(z_ref[...])
        r = lax.rsqrt(jnp.mean(yg * yg, axis=-1, keepdims=True) + EPS)
        o_ref[...] = (yg * r * g_ref[...]).astype(BF16)

    tok = pl.BlockSpec((tt, D), lambda i: (i, 0))
    return pl.pallas_call(
        body, name="gated_norm_fwd", grid=(T // tt,),
        in_specs=[tok, tok, pl.BlockSpec((1, D), lambda i: (0, 0))],
        out_specs=tok,
        out_shape=jax.ShapeDtypeStruct((T, 2 * D_MODEL), BF16),
        compiler_params=_cparams(("parallel",)),
    )(y, proj, g)


def _gated_norm_bwd(dmix, y, proj, g, dproj):
    T, D = y.shape
    tt = _tile(T, 256)

    def body(do_ref, y_ref, z_ref, g_ref, dp_ref, dy_ref, dz_ref, dg_ref):
        @pl.when(pl.program_id(0) == 0)
        def _():
            dg_ref[...] = jnp.zeros_like(dg_ref)

        yv, zv = y_ref[...], z_ref[...]
        do = do_ref[...].astype(F32)
        sz = _silu(zv)
        yg = yv * sz
        r = lax.rsqrt(jnp.mean(yg * yg, axis=-1, keepdims=True) + EPS)
        xhat = yg * r
        dxh = do * g_ref[...]
        dyg = r * (dxh - xhat * jnp.mean(dxh * xhat, axis=-1, keepdims=True))
        dy_ref[...] = dyg * sz
        dz_ref[...] = (dyg * yv * _dsilu(zv)).astype(BF16)
        dg_ref[...] += jnp.sum(do * xhat, axis=0, keepdims=True)

    tok = pl.BlockSpec((tt, D), lambda i: (i, 0))
    vec = pl.BlockSpec((1, D), lambda i: (0, 0))
    return pl.pallas_call(
        body, name="gated_norm_bwd", grid=(T // tt,),
        in_specs=[tok, tok, tok, vec, ANY],
        out_specs=[tok, tok, vec],
        out_shape=[jax.ShapeDtypeStruct((T, D), F32), jax.ShapeDtypeStruct(dproj.shape, BF16),
                   jax.ShapeDtypeStruct((1, D), F32)],
        input_output_aliases={4: 1},
        compiler_params=_cparams(("arbitrary",)),
    )(dmix, y, proj, g, dproj)


HALO = 8


def _shift_down(cur, prev8, s):
    ext = jnp.concatenate([prev8, cur], axis=0)
    return pltpu.roll(ext, s, axis=0)[HALO:]


def _shift_up(cur, next8, s):
    n = cur.shape[0]
    ext = jnp.concatenate([cur, next8], axis=0)
    return pltpu.roll(ext, n + HALO - s, axis=0)[:n]


def _conv_specs(tt, cb, col_off_blocks, nt):
    hb = tt // HALO
    cur = pl.BlockSpec((tt, cb), lambda j, i: (i, col_off_blocks + j))
    prev = pl.BlockSpec((HALO, cb), lambda j, i: (jnp.maximum(i * hb - 1, 0), col_off_blocks + j))
    nxt = pl.BlockSpec((HALO, cb), lambda j, i: (jnp.minimum((i + 1) * hb, nt * hb - 1), col_off_blocks + j))
    return cur, prev, nxt


def _taps(cur, prev8, K):
    return [_shift_down(cur, prev8, K - 1 - k) for k in range(K - 1)] + [cur]


def _conv_of_taps(taps, w):
    y = taps[-1] * w[len(taps) - 1:len(taps), :]
    for k, t in enumerate(taps[:-1]):
        y = y + t * w[k:k + 1, :]
    return y


def _causal_conv(cur, prev8, w, K):
    return _conv_of_taps(_taps(cur, prev8, K), w)


def _anticausal_conv(cur, next8, w, K):
    y = cur * w[K - 1:K, :]
    for k in range(K - 1):
        y = y + _shift_up(cur, next8, K - 1 - k) * w[k:k + 1, :]
    return y


def _ssm_conv_fwd(proj, w8, b):
    T = proj.shape[0]
    tt, cb = _tile(T, 512), 512
    nt = T // tt
    cur, prev, _ = _conv_specs(tt, cb, OFF_XBC // cb, nt)

    def body(u_ref, up_ref, w_ref, b_ref, o_ref):
        first = pl.program_id(1) == 0
        p8 = jnp.where(first, 0.0, up_ref[...])
        pre = _causal_conv(u_ref[...], p8, w_ref[...], K_SSM) + b_ref[...]
        o_ref[...] = _silu(pre)

    return pl.pallas_call(
        body, name="ssm_conv_fwd", grid=(D_XBC // cb, nt),
        in_specs=[cur, prev, pl.BlockSpec((8, cb), lambda j, i: (0, j)), pl.BlockSpec((1, cb), lambda j, i: (0, j))],
        out_specs=pl.BlockSpec((tt, cb), lambda j, i: (i, j)),
        out_shape=jax.ShapeDtypeStruct((T, D_XBC), F32),
        compiler_params=_cparams(("parallel", "parallel")),
    )(proj, proj, w8, b)


def _ssm_conv_bwd(dact, proj, w8, b, dproj):
    T = proj.shape[0]
    tt, cb = _tile(T, 512), 512
    nt = T // tt
    cur, prev, nxt = _conv_specs(tt, cb, OFF_XBC // cb, nt)
    dcur, dprev, dnxt = _conv_specs(tt, cb, 0, nt)

    def dpre_of(d, u, p8, w, bb):
        pre = _causal_conv(u, p8, w, K_SSM) + bb
        return d * _dsilu(pre)

    def body(d_ref, dn_ref, u_ref, up_ref, un_ref, w_ref, b_ref, dp_ref, dx_ref, dw_ref, db_ref):
        i = pl.program_id(1)

        @pl.when(i == 0)
        def _():
            dw_ref[...] = jnp.zeros_like(dw_ref)
            db_ref[...] = jnp.zeros_like(db_ref)

        w, bb = w_ref[...], b_ref[...]
        u = u_ref[...]
        p8 = jnp.where(i == 0, 0.0, up_ref[...])
        taps = _taps(u, p8, K_SSM)
        dpre = d_ref[...] * _dsilu(_conv_of_taps(taps, w) + bb)
        un = un_ref[...]
        dpre_n = dpre_of(dn_ref[...], un, u[tt - HALO:, :], w, bb)
        dpre_n = jnp.where(i == nt - 1, 0.0, dpre_n)
        dx_ref[...] = _anticausal_conv(dpre, dpre_n, w, K_SSM).astype(BF16)
        rows = [jnp.sum(dpre * t, axis=0, keepdims=True) for t in taps]
        rows.append(jnp.zeros((8 - K_SSM, cb), F32))
        dw_ref[...] += jnp.concatenate(rows, axis=0)
        db_ref[...] += jnp.sum(dpre, axis=0, keepdims=True)

    wspec = pl.BlockSpec((8, cb), lambda j, i: (0, j))
    bspec = pl.BlockSpec((1, cb), lambda j, i: (0, j))
    return pl.pallas_call(
        body, name="ssm_conv_bwd", grid=(D_XBC // cb, nt),
        in_specs=[dcur, dnxt, cur, prev, nxt, wspec, bspec, ANY],
        out_specs=[pl.BlockSpec((tt, cb), lambda j, i: (i, OFF_XBC // cb + j)), wspec, bspec],
        out_shape=[jax.ShapeDtypeStruct(dproj.shape, BF16), jax.ShapeDtypeStruct((8, D_XBC), F32),
                   jax.ShapeDtypeStruct((1, D_XBC), F32)],
        input_output_aliases={7: 0},
        compiler_params=_cparams(("parallel", "arbitrary")),
    )(dact, dact, proj, proj, proj, w8, b, dproj)


SCB = 512
SC3 = 3 * SCB


def _sc_specs(tt, nt):
    hb = tt // HALO
    cur = pl.BlockSpec((tt, SC3), lambda j, i: (i, OFF_CB // SC3 + j))
    prev = pl.BlockSpec((HALO, SC3), lambda j, i: (jnp.maximum(i * hb - 1, 0), OFF_CB // SC3 + j))
    nxt = pl.BlockSpec((HALO, SC3), lambda j, i: (jnp.minimum((i + 1) * hb, nt * hb - 1), OFF_CB // SC3 + j))
    return cur, prev, nxt


def _shortconv_fwd(proj, w8, ymix):
    T = proj.shape[0]
    tt = _tile(T, 512)
    nt = T // tt
    cur, prev, _ = _sc_specs(tt, nt)

    def body(p_ref, pp_ref, w_ref, y_ref, o_ref):
        p, pp = p_ref[...], pp_ref[...]
        v = p[:, SCB:2 * SCB] * p[:, 2 * SCB:]
        vp = jnp.where(pl.program_id(1) == 0, 0.0, pp[:, SCB:2 * SCB] * pp[:, 2 * SCB:])
        o_ref[...] = (p[:, :SCB] * _causal_conv(v, vp, w_ref[...], K_SC)).astype(BF16)

    return pl.pallas_call(
        body, name="shortconv_fwd", grid=(D_MODEL // SCB, nt),
        in_specs=[cur, prev, pl.BlockSpec((8, SCB), lambda j, i: (0, j)), ANY],
        out_specs=pl.BlockSpec((tt, SCB), lambda j, i: (i, D_SSM // SCB + j)),
        out_shape=jax.ShapeDtypeStruct(ymix.shape, BF16),
        input_output_aliases={3: 0},
        compiler_params=_cparams(("parallel", "parallel")),
    )(proj, proj, w8, ymix)


def _shortconv_bwd(dmix, proj, w8):
    T = proj.shape[0]
    tt = _tile(T, 512)
    nt = T // tt
    hb = tt // HALO
    cur, prev, nxt = _sc_specs(tt, nt)
    d_s = pl.BlockSpec((tt, SCB), lambda j, i: (i, D_SSM // SCB + j))
    dn_s = pl.BlockSpec((HALO, SCB), lambda j, i: (jnp.minimum((i + 1) * hb, nt * hb - 1), D_SSM // SCB + j))

    def body(d_ref, dn_ref, p_ref, pp_ref, pn_ref, w_ref, dp_ref, dw_ref):
        i = pl.program_id(1)

        @pl.when(i == 0)
        def _():
            dw_ref[...] = jnp.zeros_like(dw_ref)

        w = w_ref[...]
        p, pp = p_ref[...], pp_ref[...]
        gb, gc, u = p[:, :SCB], p[:, SCB:2 * SCB], p[:, 2 * SCB:]
        v = gc * u
        vp = jnp.where(i == 0, 0.0, pp[:, SCB:2 * SCB] * pp[:, 2 * SCB:])
        d = d_ref[...].astype(F32)
        taps = _taps(v, vp, K_SC)
        dp_ref[:, :SCB] = (d * _conv_of_taps(taps, w)).astype(BF16)
        dcv = d * gb
        dcv_n = jnp.where(i == nt - 1, 0.0, dn_ref[...].astype(F32) * pn_ref[:, :SCB])
        dv = _anticausal_conv(dcv, dcv_n, w, K_SC)
        dp_ref[:, SCB:2 * SCB] = (dv * u).astype(BF16)
        dp_ref[:, 2 * SCB:] = (dv * gc).astype(BF16)
        rows = [jnp.sum(dcv * t, axis=0, keepdims=True) for t in taps]
        rows.append(jnp.zeros((8 - K_SC, SCB), F32))
        dw_ref[...] += jnp.concatenate(rows, axis=0)

    wspec = pl.BlockSpec((8, SCB), lambda j, i: (0, j))
    return pl.pallas_call(
        body, name="shortconv_bwd", grid=(D_MODEL // SCB, nt),
        in_specs=[d_s, dn_s, cur, prev, nxt, wspec],
        out_specs=[cur, wspec],
        out_shape=[jax.ShapeDtypeStruct((T, D_MAIN), BF16), jax.ShapeDtypeStruct((8, D_MODEL), F32)],
        compiler_params=_cparams(("parallel", "arbitrary")),
    )(dmix, dmix, proj, proj, proj, w8)


GW = HEADS_PER_GROUP * HEADDIM


def _dot(a, b):
    return jnp.dot(a.astype(BF16), b.astype(BF16), preferred_element_type=F32)


def _dot_nt(a, b):
    return lax.dot_general(a.astype(BF16), b.astype(BF16), (((1,), (1,)), ((), ())), preferred_element_type=F32)


def _dot_tn(a, b):
    return lax.dot_general(a.astype(BF16), b.astype(BF16), (((0,), (0,)), ((), ())), preferred_element_type=F32)


def _bf16_terms(x, n):
    terms, r = [], x
    for _ in range(n):
        t = r.astype(BF16)
        terms.append(t)
        r = r - t.astype(F32)
    return terms


def _dot_sel(a, sel, n=2):
    s = sel.astype(BF16)
    return sum(jnp.dot(t, s, preferred_element_type=F32) for t in _bf16_terms(a, n))


def _sel_dot(sel, b, n=2):
    s = sel.astype(BF16)
    return sum(jnp.dot(s, t, preferred_element_type=F32) for t in _bf16_terms(b, n))


def _sel_dot_nt(sel, b, n=2):
    s = sel.astype(BF16)
    return sum(lax.dot_general(s, t, (((1,), (1,)), ((), ())), preferred_element_type=F32)
               for t in _bf16_terms(b, n))


def _head_cols(rows):
    parts = [jnp.broadcast_to(rows[r:r + 1, :], (HEADDIM, CHUNK)) for r in range(HEADS_PER_GROUP)]
    return jnp.concatenate(parts, axis=0).T


def _head_rows(rows):
    parts = [jnp.broadcast_to(rows[r:r + 1, :], (HEADDIM, N_STATE)) for r in range(HEADS_PER_GROUP)]
    return jnp.concatenate(parts, axis=0)


def _ssd_common(dtr, bias, alog):
    dt = _softplus(dtr + bias)
    A = -jnp.exp(alog)
    a = dt * A
    ki = lax.broadcasted_iota(jnp.int32, (CHUNK, CHUNK), 0)
    si = lax.broadcasted_iota(jnp.int32, (CHUNK, CHUNK), 1)
    upper = (ki <= si).astype(F32)
    cs = _dot_sel(a, upper, 3)
    cs_last = jnp.broadcast_to(cs[:, CHUNK - 1:CHUNK], (8, CHUNK))
    return dt, A, a, cs, cs_last


def _decay_matrix(cs, r):
    li = lax.broadcasted_iota(jnp.int32, (CHUNK, CHUNK), 0)
    si = lax.broadcasted_iota(jnp.int32, (CHUNK, CHUNK), 1)
    causal = li >= si
    R = jnp.broadcast_to(cs[r:r + 1, :], (CHUNK, CHUNK))
    seg = jnp.where(causal, R.T - R, 0.0)
    return jnp.where(causal, jnp.exp(seg), 0.0)


def _decay_cat(cs):
    return jnp.concatenate([_decay_matrix(cs, r) for r in range(HEADS_PER_GROUP)], axis=1)


def _lanes4(m):
    return jnp.concatenate([m] * HEADS_PER_GROUP, axis=1)


def _head_blocks(v):
    col = lax.broadcasted_iota(jnp.int32, v.shape, 1) // HEADDIM
    return jnp.concatenate([jnp.where(col == r, v, jnp.zeros_like(v)) for r in range(HEADS_PER_GROUP)], axis=0)


GXBC = GW + 2 * N_STATE


GS = 4


def _ssd_in_specs(nc, rev):
    cix = (lambda c: nc - 1 - c) if rev else (lambda c: c)
    x_s = pl.BlockSpec((CHUNK, GS * GW), lambda g, c: (cix(c), g))
    xbc_s = pl.BlockSpec((CHUNK, GS * GXBC), lambda g, c: (cix(c), g))
    dtr_s = pl.BlockSpec((GS, 8, CHUNK), lambda g, c: (g, 0, cix(c)))
    row_s = pl.BlockSpec((GS, 8, CHUNK), lambda g, c: (g, 0, 0))
    drep_s = pl.BlockSpec((1, GS * GW), lambda g, c: (0, g))
    hs_s = pl.BlockSpec((1, GS * GW, N_STATE), lambda g, c: (cix(c), g, 0))
    return x_s, xbc_s, dtr_s, row_s, drep_s, hs_s


def _xbc_parts(xbc_ref, gi):
    o = gi * GXBC
    return xbc_ref[:, o:o + GW], xbc_ref[:, o + GW:o + GW + N_STATE], xbc_ref[:, o + GW + N_STATE:o + GXBC]


def _ssd_fwd(xbc, dtr, bias, alog, drep):
    T = xbc.shape[0]
    nc = T // CHUNK
    x_s, xbc_s, dtr_s, row_s, drep_s, hs_s = _ssd_in_specs(nc, False)

    def body(xbc_ref, dtr_ref, bias_ref, alog_ref, drep_ref, y_ref, hs_ref, h_scr):
        @pl.when(pl.program_id(1) == 0)
        def _():
            h_scr[...] = jnp.zeros_like(h_scr)

        for gi in range(GS):
            cols, rows = slice(gi * GW, (gi + 1) * GW), pl.ds(gi * GW, GW)
            x, Bm, Cm = _xbc_parts(xbc_ref, gi)
            dt, A, a, cs, cs_last = _ssd_common(dtr_ref[gi], bias_ref[gi], alog_ref[gi])
            E = _head_cols(jnp.exp(cs))
            W = _head_cols(jnp.exp(cs_last - cs) * dt)
            X = (x * _head_cols(dt)).astype(BF16)
            CB = _dot_nt(Cm, Bm)
            y = jnp.dot((_lanes4(CB) * _decay_cat(cs)).astype(BF16), _head_blocks(X), preferred_element_type=F32)
            h = h_scr[rows, :]
            hs_ref[0, rows, :] = h
            y = y + _dot_nt(Cm, h) * E
            y_ref[:, cols] = y + drep_ref[:, cols] * x
            h_scr[rows, :] = h * _head_rows(jnp.exp(cs_last)) + _dot_tn(x * W, Bm)

    return pl.pallas_call(
        body, name="ssd_fwd", grid=(N_GROUPS // GS, nc),
        in_specs=[xbc_s, dtr_s, row_s, row_s, drep_s],
        out_specs=[x_s, hs_s],
        out_shape=[jax.ShapeDtypeStruct((T, D_SSM), F32), jax.ShapeDtypeStruct((nc, D_SSM, N_STATE), F32)],
        scratch_shapes=[pltpu.VMEM((GS * GW, N_STATE), F32)],
        compiler_params=_cparams(("parallel", "arbitrary")),
    )(xbc, dtr, bias, alog, drep)


def _ssd_bwd(xbc, dtr, bias, alog, drep, dy, hs):
    T = xbc.shape[0]
    nc = T // CHUNK
    x_s, xbc_s, dtr_s, row_s, drep_s, hs_s = _ssd_in_specs(nc, True)

    def body(xbc_ref, dtr_ref, bias_ref, alog_ref, drep_ref, dy_ref, hs_ref,
             dxbc_ref, ddtr_ref, dbias_ref, dalog_ref, dd_ref, dh_scr):
        @pl.when(pl.program_id(1) == 0)
        def _():
            dh_scr[...] = jnp.zeros_like(dh_scr)
            dbias_ref[...] = jnp.zeros_like(dbias_ref)
            dalog_ref[...] = jnp.zeros_like(dalog_ref)
            dd_ref[...] = jnp.zeros_like(dd_ref)

        for gi in range(GS):
            one_group(gi, xbc_ref, dtr_ref, bias_ref, alog_ref, drep_ref, dy_ref, hs_ref,
                      dxbc_ref, ddtr_ref, dbias_ref, dalog_ref, dd_ref, dh_scr)

    def one_group(gi, xbc_ref, dtr_ref, bias_ref, alog_ref, drep_ref, dy_ref, hs_ref,
                  dxbc_ref, ddtr_ref, dbias_ref, dalog_ref, dd_ref, dh_scr):
        cols, rows, o = slice(gi * GW, (gi + 1) * GW), pl.ds(gi * GW, GW), gi * GXBC
        x, Bm, Cm = _xbc_parts(xbc_ref, gi)
        dY = dy_ref[:, cols]
        dt, A, a, cs, cs_last = _ssd_common(dtr_ref[gi], bias_ref[gi], alog_ref[gi])
        E = _head_cols(jnp.exp(cs))
        DT = _head_cols(dt)
        Wd = _head_cols(jnp.exp(cs_last - cs))
        X = x * DT
        h = hs_ref[0, rows, :]
        dS = dh_scr[rows, :]
        CB = _dot_nt(Cm, Bm)
        rowid = lax.broadcasted_iota(jnp.int32, (8, CHUNK), 0)
        lane = lax.broadcasted_iota(jnp.int32, (8, CHUNK), 1)
        hsel = (lax.broadcasted_iota(jnp.int32, (8, GW), 1) // HEADDIM
                == lax.broadcasted_iota(jnp.int32, (8, GW), 0)).astype(F32)
        hsel_l = (lax.broadcasted_iota(jnp.int32, (8, HEADS_PER_GROUP * CHUNK), 1) // CHUNK
                  == lax.broadcasted_iota(jnp.int32, (8, HEADS_PER_GROUP * CHUNK), 0)).astype(F32)

        Lc, CBc = _decay_cat(cs), _lanes4(CB)
        Mc = CBc * Lc
        GLc = _dot_nt(dY, _head_blocks(X.astype(BF16))) * Lc
        Wc = GLc * CBc
        colsum = jnp.sum(Wc, axis=0, keepdims=True)
        dcs = _sel_dot_nt(hsel_l, Wc)
        dCB = jnp.zeros((CHUNK, CHUNK), F32)
        for r in range(HEADS_PER_GROUP):
            blk = slice(r * CHUNK, (r + 1) * CHUNK)
            dCB = dCB + GLc[:, blk]
            dcs = dcs - jnp.where(rowid == r, colsum[:, blk], 0.0)
        m_stack = jnp.concatenate([Mc[:, r * CHUNK:(r + 1) * CHUNK].astype(BF16) for r in range(HEADS_PER_GROUP)],
                                  axis=0)
        dX = lax.dot_general(m_stack, _head_blocks(dY.astype(BF16)), (((0,), (0,)), ((), ())),
                             preferred_element_type=F32)
        dC = _dot(dCB, Bm)
        dB = _dot_tn(dCB, Cm)
        T1 = _dot_nt(Bm, dS)
        dX = dX + T1 * Wd
        dB = dB + _dot(X * Wd, dS)
        pdec = _sel_dot_nt(hsel, X * T1 * Wd)
        dcs = dcs - pdec
        dlast = jnp.sum(pdec, axis=1, keepdims=True) \
            + jnp.exp(cs_last[:, 0:1]) * jnp.sum(_sel_dot(hsel, dS * h), axis=1, keepdims=True)
        dYE = dY * E
        dC = dC + _dot(dYE, h)
        yoff = _dot_nt(Cm, h) * E
        dcs = dcs + _sel_dot_nt(hsel, dY * yoff)
        dcs = dcs + jnp.where(lane == CHUNK - 1, dlast, 0.0)
        ki = lax.broadcasted_iota(jnp.int32, (CHUNK, CHUNK), 0)
        si = lax.broadcasted_iota(jnp.int32, (CHUNK, CHUNK), 1)
        lower = (ki >= si).astype(F32)
        da = _dot_sel(dcs, lower)
        ddt = da * A + _sel_dot_nt(hsel, dX * x)
        ddtr = ddt * _sigmoid(dtr_ref[gi] + bias_ref[gi])
        ddtr_ref[gi] = ddtr
        dbias_ref[gi] += ddtr
        dalog_ref[gi] += da * a
        dxbc_ref[:, o:o + GW] = dX * DT + drep_ref[:, cols] * dY
        dd_ref[:, cols] += jnp.sum(dY * x, axis=0, keepdims=True)
        dxbc_ref[:, o + GW:o + GW + N_STATE] = dB
        dxbc_ref[:, o + GW + N_STATE:o + GXBC] = dC
        dh_scr[rows, :] = dS * _head_rows(jnp.exp(cs_last)) + _dot_tn(dYE, Cm)

    return pl.pallas_call(
        body, name="ssd_bwd", grid=(N_GROUPS // GS, nc),
        in_specs=[xbc_s, dtr_s, row_s, row_s, drep_s, x_s, hs_s],
        out_specs=[xbc_s, dtr_s, row_s, row_s, drep_s],
        out_shape=[jax.ShapeDtypeStruct((T, D_XBC), F32),
                   jax.ShapeDtypeStruct((N_GROUPS, 8, T), F32),
                   jax.ShapeDtypeStruct((N_GROUPS, 8, CHUNK), F32),
                   jax.ShapeDtypeStruct((N_GROUPS, 8, CHUNK), F32),
                   jax.ShapeDtypeStruct((1, D_SSM), F32)],
        scratch_shapes=[pltpu.VMEM((GS * GW, N_STATE), F32)],
        compiler_params=_cparams(("parallel", "arbitrary")),
    )(xbc, dtr, bias, alog, drep, dy, hs)


def _adamw(w, g, m, v, name, deps=(), emit_g=False):
    R, C = w.shape
    tr = _tile(R, 256, 8)
    nd = len(deps)
    nout = 4 if emit_g else 3

    def body(w_ref, g_ref, m_ref, v_ref, *rest):
        outs = rest[nd:]
        gv = g_ref[...]
        mn = ADAM_B1 * m_ref[...] + (1.0 - ADAM_B1) * gv
        vn = ADAM_B2 * v_ref[...] + (1.0 - ADAM_B2) * (gv * gv)
        m_hat = mn / (1.0 - ADAM_B1 ** ADAM_STEP)
        v_hat = vn / (1.0 - ADAM_B2 ** ADAM_STEP)
        outs[0][...] = -ADAM_LR * (m_hat / (jnp.sqrt(v_hat) + ADAM_EPS) + ADAM_WD * w_ref[...])
        outs[1][...] = mn
        outs[2][...] = vn
        if emit_g:
            outs[3][...] = gv

    spec = pl.BlockSpec((tr, C), lambda i: (i, 0))
    return pl.pallas_call(
        body, name=name, grid=(R // tr,),
        in_specs=[spec] * 4 + [ANY] * nd, out_specs=[spec] * nout,
        out_shape=[jax.ShapeDtypeStruct((R, C), F32)] * nout,
        compiler_params=_cparams(("parallel",)),
    )(w, g, m, v, *deps)


ANY = pl.BlockSpec(memory_space=pl.ANY)


def _place():
    x, y, c = lax.axis_index("x"), lax.axis_index("y"), lax.axis_index("c")
    return x, y, c


def _other_chips(x, y):
    return [(1 - x, y), (x, 1 - y), (1 - x, 1 - y)]


def _allgather_inplace(bufs, splits, first_done=False):
    n = len(bufs)

    def body(*refs):
        o_refs = refs[n:2 * n]
        send_sems, recv_sems = refs[2 * n:]
        x, y, c = _place()
        xn, yn, dg, sibling = (1 - x, y), (x, 1 - y), (1 - x, 1 - y), (x, y, 1 - c)

        def blk(k, chip, pc):
            return o_refs[k].at[4 * chip[0] + 2 * chip[1] + pc]

        def part(k, ref, p):
            kind, s = splits[k]
            _, R, C = bufs[k].shape
            if kind == "rows":
                return ref.at[pl.ds(0, s)] if p == 0 else ref.at[pl.ds(s, R - s)]
            return ref.at[:, pl.ds(0, s)] if p == 0 else ref.at[:, pl.ds(s, C - s)]

        def copy(k, slot, ref, to):
            return pltpu.make_async_remote_copy(
                src_ref=ref, dst_ref=ref, send_sem=send_sems.at[k, slot], recv_sem=recv_sems.at[k, slot],
                device_id=to, device_id_type=MESH)

        sent = []

        def send(k, slot, ref, to):
            cp = copy(k, slot, ref, to)
            cp.start()
            sent.append(cp)

        if not first_done:
            for k in range(n):
                send(k, 0, blk(k, (x, y), c), (*xn, c))
                send(k, 1, blk(k, (x, y), c), (*yn, c))
        for k in range(n):
            bx, by = blk(k, xn, c), blk(k, yn, c)
            if not first_done:
                copy(k, 0, bx, sibling).wait_recv()
            send(k, 2, part(k, bx, 0), (*yn, c))
            send(k, 4, bx, sibling)
            if not first_done:
                copy(k, 1, by, sibling).wait_recv()
            send(k, 3, part(k, by, 1), (*xn, c))
            send(k, 5, by, sibling)
        for k in range(n):
            d0, d1 = part(k, blk(k, dg, c), 0), part(k, blk(k, dg, c), 1)
            copy(k, 2, d0, sibling).wait_recv()
            send(k, 6, d0, sibling)
            copy(k, 3, d1, sibling).wait_recv()
            send(k, 7, d1, sibling)
        for k in range(n):
            copy(k, 4, blk(k, xn, 1 - c), sibling).wait_recv()
            copy(k, 5, blk(k, yn, 1 - c), sibling).wait_recv()
            copy(k, 6, part(k, blk(k, dg, 1 - c), 0), sibling).wait_recv()
            copy(k, 7, part(k, blk(k, dg, 1 - c), 1), sibling).wait_recv()
        for cp in sent:
            cp.wait_send()

    return pl.pallas_call(
        body, name="allgather_w_in",
        in_specs=[ANY] * n, out_specs=[ANY] * n,
        out_shape=[jax.ShapeDtypeStruct(b.shape, b.dtype) for b in bufs],
        input_output_aliases={k: k for k in range(n)},
        scratch_shapes=[pltpu.SemaphoreType.DMA((n, 8)), pltpu.SemaphoreType.DMA((n, 8))],
    )(*bufs)


HBM = pl.BlockSpec(memory_space=pltpu.HBM)
SEM = pl.BlockSpec(memory_space=pltpu.SEMAPHORE)
EFFECT = pltpu.SideEffectType.DATAFLOW_SIDE_EFFECTING


def _split_start(name, arrays, build, n_copies, after=()):
    na, nd = len(arrays), len(after)

    def body(*refs):
        send_sems, recv_sems = refs[na + nd], refs[na + nd + 1]
        for cp in build(refs[:na], send_sems, recv_sems):
            cp.start()
        refs[-1][...] = jnp.zeros((8, 128), F32)

    outs = pl.pallas_call(
        body, name=name,
        out_shape=(pltpu.SemaphoreType.DMA((n_copies,)), pltpu.SemaphoreType.DMA((n_copies,)),
                   *[pltpu.HBM(a.shape, a.dtype) for a in arrays], jax.ShapeDtypeStruct((8, 128), F32)),
        in_specs=[HBM] * na + [ANY] * nd,
        out_specs=(SEM, SEM, *[HBM] * na, pl.BlockSpec(memory_space=pltpu.VMEM)),
        input_output_aliases={i: 2 + i for i in range(na)},
        compiler_params=pltpu.CompilerParams(has_side_effects=EFFECT),
    )(*[pltpu.with_memory_space_constraint(a, pltpu.HBM) for a in arrays], *after)
    return outs[0], outs[1], list(outs[2:2 + na]), outs[-1]


def _split_wait(name, send_sems, recv_sems, arrays, build, after):
    na = len(arrays)

    def body(*refs):
        for cp in build(refs[:na], refs[na], refs[na + 1]):
            cp.wait_send()
            cp.wait_recv()

    outs = pl.pallas_call(
        body, name=name,
        out_shape=tuple(pltpu.HBM(a.shape, a.dtype) for a in arrays),
        in_specs=[HBM] * na + [SEM, SEM] + [ANY] * len(after),
        out_specs=tuple([HBM] * na),
        input_output_aliases={i: i for i in range(na)},
        compiler_params=pltpu.CompilerParams(has_side_effects=EFFECT),
    )(*arrays, send_sems, recv_sems, *after)
    return list(outs)


def _remote(src, dst, send_sems, recv_sems, i, to):
    return pltpu.make_async_remote_copy(src_ref=src, dst_ref=dst, send_sem=send_sems.at[i], recv_sem=recv_sems.at[i],
                                        device_id=to, device_id_type=MESH)


def _build_ag_first(refs, ss, rs):
    x, y, c = _place()
    cps = []
    for k, ref in enumerate(refs):
        blk = ref.at[4 * x + 2 * y + c]
        cps += [_remote(blk, blk, ss, rs, 2 * k, (1 - x, y, c)), _remote(blk, blk, ss, rs, 2 * k + 1, (x, 1 - y, c))]
    return cps


def _build_ag_ici(refs, ss, rs):
    x, y, c = _place()
    cps = []
    for k, ref in enumerate(refs):
        blk = ref.at[4 * x + 2 * y + c]
        for j, (px, py) in enumerate(_other_chips(x, y)):
            cps.append(_remote(blk, blk, ss, rs, 3 * k + j, (px, py, c)))
    return cps


def _build_ag_fwd(refs, ss, rs):
    x, y, c = _place()
    cps = []
    for k, ref in enumerate(refs):
        for j, (px, py) in enumerate(_other_chips(x, y)):
            blk = ref.at[4 * px + 2 * py + c]
            cps.append(_remote(blk, blk, ss, rs, 3 * k + j, (x, y, 1 - c)))
    return cps


def _build_rs_swap(refs, ss, rs):
    x, y, c = _place()
    n = len(refs) // 2
    return [_remote(refs[k].at[:, pl.ds(1 - c, 1)], refs[n + k], ss, rs, k, (x, y, 1 - c)) for k in range(n)]


def _build_rs_ici(refs, ss, rs):
    x, y, c = _place()
    n = len(refs) // 2
    me = 2 * x + y
    cps = []
    for k in range(n):
        for j, (px, py) in enumerate(_other_chips(x, y)):
            cps.append(_remote(refs[k].at[2 * px + py], refs[n + k].at[me], ss, rs, 3 * k + j, (px, py, c)))
    return cps


def _build_rs_share(refs, ss, rs):
    x, y, c = _place()
    return [_remote(ref.at[c], ref.at[c], ss, rs, k, (x, y, 1 - c)) for k, ref in enumerate(refs)]


def _build_small_gather(refs, ss, rs):
    x, y, c = _place()
    me = 4 * x + 2 * y + c
    cps = []
    for d in range(1, N_DEV):
        to = (1 - x if d & 4 else x, 1 - y if d & 2 else y, 1 - c if d & 1 else c)
        cps.append(_remote(refs[0], refs[1].at[me], ss, rs, d - 1, to))
    return cps


def _sum_gathered(mine, landed, me_arr):
    R, C = mine.shape

    def body(me_ref, m_ref, l_ref, o_ref):
        me = me_ref[0]
        s = None
        for d in range(N_DEV):
            t = jnp.where(me == d, m_ref[...], l_ref[d])
            s = t if s is None else s + t
        o_ref[...] = s

    grid_spec = pltpu.PrefetchScalarGridSpec(
        num_scalar_prefetch=1, grid=(1,),
        in_specs=[pl.BlockSpec((R, C), lambda i, me_ref: (0, 0)),
                  pl.BlockSpec((N_DEV, R, C), lambda i, me_ref: (0, 0, 0))],
        out_specs=pl.BlockSpec((R, C), lambda i, me_ref: (0, 0)))
    return pl.pallas_call(
        body, name="sum_small", grid_spec=grid_spec,
        out_shape=jax.ShapeDtypeStruct((R, C), F32),
        compiler_params=_cparams(("arbitrary",)),
    )(me_arr, mine, landed)


def _rs_add_pair(p, r0, c_arr, name):
    _, _, hr, cols = p.shape
    tr = _tile(hr, 256, 8)

    def body(c_ref, p_ref, r_ref, q_ref):
        q_ref[...] = (p_ref[0].astype(F32) + r_ref[0].astype(F32)).astype(BF16)

    grid_spec = pltpu.PrefetchScalarGridSpec(
        num_scalar_prefetch=1, grid=(N_CHIPS, hr // tr),
        in_specs=[pl.BlockSpec((1, 1, tr, cols), lambda j, i, c_ref: (j, c_ref[0], i, 0)),
                  pl.BlockSpec((1, 1, tr, cols), lambda j, i, c_ref: (j, 0, i, 0))],
        out_specs=pl.BlockSpec((1, tr, cols), lambda j, i, c_ref: (j, i, 0)))
    return pl.pallas_call(
        body, name=name, grid_spec=grid_spec,
        out_shape=jax.ShapeDtypeStruct((N_CHIPS, hr, cols), BF16),
        compiler_params=_cparams(("parallel", "parallel")),
    )(c_arr, p, r0)


def _rs_add_chips(r1, q, place_arr, name):
    _, hr, cols = r1.shape
    tr = _tile(hr, 256, 8)

    def body(place_ref, r_ref, q_ref, o_ref):
        chip = place_ref[0]
        s = None
        for j in range(N_CHIPS):
            t = jnp.where(chip == j, q_ref[j], r_ref[j]).astype(F32)
            s = t if s is None else s + t
        o_ref[...] = s

    blk = pl.BlockSpec((N_CHIPS, tr, cols), lambda i, place_ref: (0, i, 0))
    grid_spec = pltpu.PrefetchScalarGridSpec(
        num_scalar_prefetch=1, grid=(hr // tr,), in_specs=[blk, blk],
        out_specs=pl.BlockSpec((None, tr, cols), lambda i, place_ref: (place_ref[1], i, 0)))
    return pl.pallas_call(
        body, name=name, grid_spec=grid_spec,
        out_shape=jax.ShapeDtypeStruct((2, hr, cols), F32),
        compiler_params=_cparams(("parallel",)),
    )(place_arr, r1, q)


def _pad_rows(a, rows):
    return jnp.pad(a, ((0, rows - a.shape[0]), (0, 0)))


def _pad_cols(a, cols):
    return jnp.pad(a, ((0, 0), (0, cols - a.shape[1])))


def _heads_to_rows(v):
    v = v.reshape(N_GROUPS, HEADS_PER_GROUP, 1)
    v = jnp.pad(v, ((0, 0), (0, 8 - HEADS_PER_GROUP), (0, 0)))
    return jnp.broadcast_to(v, (N_GROUPS, 8, CHUNK))


def _rows_to_heads(a):
    return jnp.sum(a[:, :HEADS_PER_GROUP, :], axis=-1).reshape(N_HEADS)


def _to_kernel_rows(a):
    C = a.shape[1]
    x0, b0, c0, s0 = D_SSM, 2 * D_SSM, 2 * D_SSM + 1024, D_SSM + D_XBC + N_HEADS
    xbc = jnp.concatenate([a[x0:b0].reshape(N_GROUPS, GW, C), a[b0:c0].reshape(N_GROUPS, N_STATE, C),
                           a[c0:c0 + 1024].reshape(N_GROUPS, N_STATE, C)], axis=1).reshape(D_XBC, C)
    sc = jnp.concatenate([a[s0 + k * D_MODEL:s0 + (k + 1) * D_MODEL].reshape(D_MODEL // SCB, SCB, C)
                          for k in range(3)], axis=1).reshape(3 * D_MODEL, C)
    return jnp.concatenate([a[:D_SSM], xbc, sc], axis=0)


HR_IN = 1568


def _shard_row_plan():
    segs = [(0, 0, 0, D_SSM)]
    for g in range(N_GROUPS):
        k0 = D_SSM + g * GXBC
        segs += [(0, k0, D_SSM + g * GW, GW), (0, k0 + GW, 2 * D_SSM + g * N_STATE, N_STATE),
                 (0, k0 + GW + N_STATE, 2 * D_SSM + 1024 + g * N_STATE, N_STATE)]
    segs.append((1, 0, D_SSM + D_XBC, N_HEADS))
    for j in range(D_MODEL // SCB):
        for k in range(3):
            segs.append((0, D_SSM + D_XBC + j * SC3 + k * SCB, D_SSM + D_XBC + N_HEADS + k * D_MODEL + j * SCB, SCB))
    cs = D_IN // N_CHIPS
    plan = []
    for src, s, o, n in segs:
        while n > 0:
            chip, loc = divmod(o, cs)
            half, row = divmod(loc, HR_IN)
            m = min(n, cs - loc, HR_IN - row)
            plan.append((src, s, chip, half, row, m))
            s, o, n = s + m, o + m, n - m
    return plan


SCATTER_ROWS = 512
SCATTER_SLOTS = 4


def _scatter_rows_to_shards(k_main, k_dt):
    C = k_main.shape[1]
    pieces = []
    for src, s, chip, half, row, n in _shard_row_plan():
        for o in range(0, n, SCATTER_ROWS):
            pieces.append((src, s + o, chip, half, row + o, min(SCATTER_ROWS, n - o)))
    S, lag, N = SCATTER_SLOTS, SCATTER_SLOTS // 2, len(pieces)

    def body(m_ref, d_ref, o_ref, buf, in_sems, out_sems):
        def cin(i):
            src, s, _, _, _, n = pieces[i]
            return pltpu.make_async_copy((d_ref if src else m_ref).at[pl.ds(s, n)],
                                         buf.at[i % S, pl.ds(0, n)], in_sems.at[i % S])

        def cout(i):
            _, _, chip, half, row, n = pieces[i]
            return pltpu.make_async_copy(buf.at[i % S, pl.ds(0, n)],
                                         o_ref.at[chip, half, pl.ds(row, n)], out_sems.at[i % S])

        for i in range(N + lag):
            if i < N:
                if i >= S:
                    cout(i - S).wait()
                cin(i).start()
            j = i - lag
            if 0 <= j < N:
                cin(j).wait()
                cout(j).start()
        for j in range(max(0, N - S), N):
            cout(j).wait()

    return pl.pallas_call(
        body, name="scatter_dw_in_rows", in_specs=[ANY, ANY], out_specs=ANY,
        out_shape=jax.ShapeDtypeStruct((N_CHIPS, 2, HR_IN, C), k_main.dtype),
        scratch_shapes=[pltpu.VMEM((S, SCATTER_ROWS, C), k_main.dtype),
                        pltpu.SemaphoreType.DMA((S,)), pltpu.SemaphoreType.DMA((S,))],
        compiler_params=_cparams(),
    )(k_main, k_dt)


def _to_kernel_xbc(a):
    R = a.shape[0]
    return jnp.concatenate([a[:, :D_SSM].reshape(R, N_GROUPS, GW), a[:, D_SSM:D_SSM + 1024].reshape(R, N_GROUPS, N_STATE),
                            a[:, D_SSM + 1024:].reshape(R, N_GROUPS, N_STATE)], axis=2).reshape(R, D_XBC)


def _from_kernel_xbc(a):
    R = a.shape[0]
    g = a.reshape(R, N_GROUPS, GXBC)
    return jnp.concatenate([g[:, :, :GW].reshape(R, D_SSM), g[:, :, GW:GW + N_STATE].reshape(R, 1024),
                            g[:, :, GW + N_STATE:].reshape(R, 1024)], axis=1)


def kernel(x, norm_mix_g, w_in, ssm_conv_w, ssm_conv_b, ssm_dt_bias, ssm_A_log, ssm_D, ssm_norm_g, sc_conv_w, w_out, norm_ffn_g, w_gate, w_up, w_down, norm_final_g, loss_target, m_norm_mix_g, m_w_in, m_ssm_conv_w, m_ssm_conv_b, m_ssm_dt_bias, m_ssm_A_log, m_ssm_D, m_ssm_norm_g, m_sc_conv_w, m_w_out, m_norm_ffn_g, m_w_gate, m_w_up, m_w_down, m_norm_final_g, v_norm_mix_g, v_w_in, v_ssm_conv_w, v_ssm_conv_b, v_ssm_dt_bias, v_ssm_A_log, v_ssm_D, v_ssm_norm_g, v_sc_conv_w, v_w_out, v_norm_ffn_g, v_w_gate, v_w_up, v_w_down, v_norm_final_g):
    T = x.shape[1]
    xt = x[0]
    tgt = loss_target[0]
    cx, cy, cc = lax.axis_index("x"), lax.axis_index("y"), lax.axis_index("c")
    chip = 2 * cx + cy
    c_arr = jnp.reshape(cc, (1,)).astype(jnp.int32)
    chip_arr = jnp.reshape(chip, (1,)).astype(jnp.int32)
    place_arr = jnp.stack([chip, cc]).astype(jnp.int32)

    big = [w_in[0].T, w_out[0], w_gate[0], w_up[0], w_down[0]]
    names = ["w_in", "w_out", "w_gate", "w_up", "w_down"]
    gb_in = _cast_into_gather(big[0], chip_arr, "cast_w_in", split_cols=True)
    cs_in, cs_conv = D_IN // N_CHIPS, D_XBC // N_CHIPS
    cw = jnp.stack([_pad_rows(ssm_conv_w[0], 8), _pad_cols(_pad_rows(sc_conv_w[0], 8), cs_conv)])
    cw_buf = lax.dynamic_update_slice(jnp.zeros((N_DEV, 8, cs_conv), F32), cw, (2 * chip, 0, 0))
    f_ss, f_rs, f_arr, f_tok = _split_start("ag_in_first_start", [gb_in, cw_buf], _build_ag_first, 4)
    gbufs = [None] + [_cast_into_gather(w, chip_arr, "cast_" + nm, deps=[f_tok]) for w, nm in zip(big[1:], names[1:])]
    n1 = _rmsnorm_fwd(xt, _tie(norm_mix_g, f_tok, "tie_ag_first"), "rmsnorm_mix")
    f_arr = _split_wait("ag_in_first_wait", f_ss, f_rs, f_arr, _build_ag_first, after=gbufs[1:] + [n1])
    g_in, cw_all = _allgather_inplace(f_arr, [("rows", (cs_in // 32) * 16), ("cols", cs_conv // 2)], first_done=True)
    cw_all = cw_all.reshape(N_CHIPS, 2, 8, cs_conv)
    ssm_w8 = _to_kernel_xbc(cw_all[:, 0].transpose(1, 0, 2).reshape(8, D_XBC))
    sc_w8 = cw_all[:, 1, :, :D_MODEL // N_CHIPS].transpose(1, 0, 2).reshape(8, D_MODEL)
    ssm_bk = _to_kernel_xbc(ssm_conv_b)
    wt = g_in.reshape(N_CHIPS, 2, cs_in, D_MODEL // 2).transpose(0, 2, 1, 3).reshape(D_IN, D_MODEL)
    wt_main = _to_kernel_rows(wt)
    wt_dt = _pad_rows(wt[D_SSM + D_XBC:D_SSM + D_XBC + N_HEADS], DT_PAD)
    ag_ss, ag_rs, ag_bufs, ag_tok = _split_start("ag_ici_start", gbufs[1:], _build_ag_ici, 12, after=[g_in, cw_all])

    bias_rows = _heads_to_rows(ssm_dt_bias[0])
    alog_rows = _heads_to_rows(ssm_A_log[0])
    drep = jnp.repeat(ssm_D[0], HEADDIM).reshape(1, D_SSM)

    (proj,) = _matmul([(n1, wt_main)], tb=True, out_dtypes=[F32], name="mm_proj", deps=[ag_tok])
    (dt_raw,) = _matmul([(n1, wt_dt)], tb=True, out_dtypes=[F32], name="mm_proj_dt")
    xbc = _ssm_conv_fwd(proj, ssm_w8, ssm_bk)
    dtr = jnp.pad(dt_raw[:, :N_HEADS].T.reshape(N_GROUPS, HEADS_PER_GROUP, T), ((0, 0), (0, 4), (0, 0)))
    y_ssd, hs = _ssd_fwd(xbc, dtr, bias_rows, alog_rows, drep)
    ag_bufs = _split_wait("ag_ici_wait", ag_ss, ag_rs, ag_bufs, _build_ag_ici, after=[y_ssd])
    fw_ss, fw_rs, fw_bufs, fw_tok = _split_start("ag_fwd_start", ag_bufs, _build_ag_fwd, 12)
    y_mix = _shortconv_fwd(proj, sc_w8, _gated_norm_fwd(y_ssd, proj, _tie(ssm_norm_g, fw_tok, "tie_ag_fwd")))
    gath = _split_wait("ag_fwd_wait", fw_ss, fw_rs, fw_bufs, _build_ag_fwd, after=[y_mix])
    w_out_f = gath[0].reshape(2 * D_MODEL, D_MODEL)
    w_gate3 = gath[1].reshape(N_CHIPS, D_MODEL, D_FF // N_CHIPS)
    w_up3 = gath[2].reshape(N_CHIPS, D_MODEL, D_FF // N_CHIPS)
    w_down_f = gath[3].reshape(D_FF, D_MODEL)
    (h1,) = _matmul([(y_mix, w_out_f)], out_dtypes=[F32], name="mm_out", extras=[xt],
                    epilogue=lambda acc, res: (acc + res,))
    n2 = _rmsnorm_fwd(h1, norm_ffn_g, "rmsnorm_ffn")
    g_act, u_act, a_act = _ffn_fwd(n2, w_gate3, w_up3)
    (h2,) = _matmul([(a_act, w_down_f)], out_dtypes=[F32], name="mm_down", extras=[h1],
                    epilogue=lambda acc, res: (acc + res,))

    dh2, dh2b, dg_final, loss_part = _loss_and_final_bwd(h2, tgt, norm_final_g.reshape(1, D_MODEL))
    dg_act, du_act = _matmul([(dh2b, w_down_f)], tb=True, out_dtypes=[BF16, BF16], name="mm_down_bwd",
                             tn=512, extras=[g_act, u_act], epilogue=_swiglu_bwd, nsub=2)
    (dw_down,) = _matmul([(a_act, dh2b)], ta=True, out_dtypes=[BF16], name="mm_dw_down", tm=1408, tn=512)
    (dn2,) = _matmul([(dg_act, w_gate3), (du_act, w_up3)], tb=True, b3d=True, out_dtypes=[BF16],
                     name="mm_ffn_in_bwd")
    (dw_gate,) = _matmul([(n2, dg_act)], ta=True, out_dtypes=[BF16], name="mm_dw_gate", tm=512, tn=1408,
                         col_shards=True)
    (dw_up,) = _matmul([(n2, du_act)], ta=True, out_dtypes=[BF16], name="mm_dw_up", tm=512, tn=1408,
                       col_shards=True)
    dh1, dh1b, dg_ffn = _rmsnorm_bwd(dn2, h1, norm_ffn_g, dh2, "rmsnorm_ffn_bwd")
    (dw_out,) = _matmul([(y_mix, dh1b)], ta=True, out_dtypes=[BF16], name="mm_dw_out")

    def halves(g):
        return g.reshape(N_CHIPS, 2, g.shape[1] // 2, g.shape[2])

    def landing(shape, dtype):
        return lax.empty(shape, dtype)

    names1 = names[1:]
    ps1 = [halves(dw_out.reshape(N_CHIPS, -1, D_MODEL)), halves(dw_gate), halves(dw_up),
           halves(dw_down.reshape(N_CHIPS, -1, D_MODEL))]
    r0_1 = [landing((N_CHIPS, 1) + p.shape[2:], p.dtype) for p in ps1]
    sw_ss, sw_rs, sw_arr, sw_tok = _split_start("rs1_swap_start", ps1 + r0_1, _build_rs_swap, 4)
    (dmix,) = _matmul([(dh1b, w_out_f)], tb=True, out_dtypes=[BF16], name="mm_out_bwd", deps=[sw_tok])
    dproj, dw_sc = _shortconv_bwd(dmix, proj, sc_w8)
    dy_ssd, dproj, dg_ssmnorm = _gated_norm_bwd(dmix, y_ssd, proj, ssm_norm_g, dproj)
    sw_arr = _split_wait("rs1_swap_wait", sw_ss, sw_rs, sw_arr, _build_rs_swap, after=[dy_ssd])
    qs1 = [_rs_add_pair(p, r, c_arr, "rs_add_pair_" + nm) for p, r, nm in zip(sw_arr[:4], sw_arr[4:], names1)]
    r1_1 = [landing(q.shape, BF16) for q in qs1]
    ic_ss, ic_rs, ic_arr, ic_tok = _split_start("rs1_ici_start", qs1 + r1_1, _build_rs_ici, 12)
    dxbc_act, ddtr, dbias_acc, dalog_acc, dD_acc = _ssd_bwd(
        xbc, dtr, bias_rows, alog_rows, _tie(drep, ic_tok, "tie_rs1_ici"), dy_ssd, hs)
    dproj, dw_ssmconv, db_ssmconv = _ssm_conv_bwd(dxbc_act, proj, ssm_w8, ssm_bk, dproj)
    dw_ssmconv, db_ssmconv = _from_kernel_xbc(dw_ssmconv), _from_kernel_xbc(db_ssmconv)
    ic_arr = _split_wait("rs1_ici_wait", ic_ss, ic_rs, ic_arr, _build_rs_ici, after=[dproj])
    g1 = [_rs_add_chips(r, q, place_arr, "rs_add_chips_" + nm) for q, r, nm in zip(ic_arr[:4], ic_arr[4:], names1)]
    sh_ss, sh_rs, sh_arr, sh_tok = _split_start("rs1_share_start", g1, _build_rs_share, 4)

    ddt_raw = _pad_cols(ddtr[:, :HEADS_PER_GROUP, :].reshape(N_HEADS, T).T, DT_PAD).astype(BF16)
    (dwt_main,) = _matmul([(dproj, n1)], ta=True, out_dtypes=[F32], name="mm_dw_main", deps=[sh_tok])
    (dwt_dt,) = _matmul([(ddt_raw, n1)], ta=True, out_dtypes=[F32], name="mm_dw_dt")
    p_in = _scatter_rows_to_shards(dwt_main, dwt_dt)
    s2_ss, s2_rs, s2_arr, s2_tok = _split_start(
        "rs2_swap_start", [p_in, landing((N_CHIPS, 1) + p_in.shape[2:], F32)], _build_rs_swap, 1)
    mt = T // _tile(T, 1024)
    mt_a = max(mt // 4, 1)
    (dn1a,) = _matmul([(dproj, wt_main)], out_dtypes=[F32], name="mm_proj_bwd_a", deps=[s2_tok],
                      m_tiles=(0, mt_a))
    g1 = _split_wait("rs1_share_wait", sh_ss, sh_rs, sh_arr, _build_rs_share, after=[dn1a])
    s2_arr = _split_wait("rs2_swap_wait", s2_ss, s2_rs, s2_arr, _build_rs_swap, after=[dn1a])
    q_in = _rs_add_pair(s2_arr[0], s2_arr[1], c_arr, "rs_add_pair_w_in")
    i2_ss, i2_rs, i2_arr, i2_tok = _split_start(
        "rs2_ici_start", [q_in, landing(q_in.shape, BF16)], _build_rs_ici, 3)
    if mt > mt_a:
        (dn1a,) = _matmul([(dproj, wt_main)], out_dtypes=[F32], name="mm_proj_bwd_b", deps=[i2_tok],
                          m_tiles=(mt_a, mt - mt_a), out_buf=dn1a)
    (dn1,) = _matmul([(ddt_raw, wt_dt)], out_dtypes=[BF16], name="mm_proj_dt_bwd", extras=[dn1a],
                     epilogue=lambda acc, res: (acc + res,), deps=[i2_tok])
    dx, _, dg_mix = _rmsnorm_bwd(dn1, xt, norm_mix_g, dh1, "rmsnorm_mix_bwd")

    big_m = [m_w_in[0].T, m_w_out[0], m_w_gate[0], m_w_up[0], m_w_down[0]]
    big_v = [v_w_in[0].T, v_w_out[0], v_w_gate[0], v_w_up[0], v_w_down[0]]
    big_grads = [None] + [g.reshape(w.shape) for g, w in zip(g1, big[1:])]
    big_out = {}
    for k in range(1, 5):
        *big_out[names[k]], big_grads[k] = _adamw(big[k], big_grads[k], big_m[k], big_v[k], "adamw_" + names[k],
                                                   deps=[i2_tok], emit_g=True)
    i2_arr = _split_wait("rs2_ici_wait", i2_ss, i2_rs, i2_arr, _build_rs_ici, after=[big_out[names[4]][0], dx])
    g_in_red = _rs_add_chips(i2_arr[1], i2_arr[0], place_arr, "rs_add_chips_w_in")
    s3_ss, s3_rs, s3_arr, s3_tok = _split_start("rs2_share_start", [g_in_red], _build_rs_share, 1)

    dD = jnp.sum(dD_acc.reshape(N_HEADS, HEADDIM), axis=-1)
    heads_row = jnp.concatenate([_rows_to_heads(dbias_acc), _rows_to_heads(dalog_acc), dD,
                                 loss_part.reshape(1)]).reshape(1, -1)
    small = jnp.concatenate([
        dw_ssmconv,
        _pad_cols(dw_sc, D_XBC),
        db_ssmconv,
        jnp.concatenate([dg_mix, dg_ssmnorm], axis=1),
        jnp.concatenate([dg_ffn, dg_final], axis=1),
        _pad_cols(heads_row, D_XBC),
        jnp.zeros((4, D_XBC), F32),
    ], axis=0)
    sm_ss, sm_rs, sm_arr, sm_tok = _split_start(
        "small_gather_start", [small, landing((N_DEV,) + small.shape, F32)], _build_small_gather, N_DEV - 1,
        after=[s3_tok])
    (g_in_full,) = _split_wait("rs2_share_wait", s3_ss, s3_rs, s3_arr, _build_rs_share, after=[sm_tok])
    d_t, m_t, v_t, g_t = _adamw(big[0], g_in_full.reshape(2 * HR_IN, D_MODEL), big_m[0], big_v[0],
                                "adamw_" + names[0], emit_g=True)
    big_grads[0] = g_t.T
    big_out[names[0]] = (d_t.T, m_t.T, v_t.T)
    sm_arr = _split_wait("small_gather_wait", sm_ss, sm_rs, sm_arr, _build_small_gather, after=[d_t])
    tot = _sum_gathered(sm_arr[0], sm_arr[1], jnp.reshape(4 * cx + 2 * cy + cc, (1,)).astype(jnp.int32))
    loss = tot[19, 3 * N_HEADS]

    cs_ssm, cs_sc = D_XBC // N_CHIPS, D_MODEL // N_CHIPS
    g_ssm_conv = lax.dynamic_slice(tot[0:K_SSM], (0, chip * cs_ssm), (K_SSM, cs_ssm))
    g_sc_conv = lax.dynamic_slice(tot[8:8 + K_SC, :D_MODEL], (0, chip * cs_sc), (K_SC, cs_sc))
    small_grads = {
        "norm_mix_g": tot[17:18, :D_MODEL], "ssm_conv_w": g_ssm_conv, "ssm_conv_b": tot[16:17],
        "ssm_dt_bias": tot[19:20, 0:N_HEADS], "ssm_A_log": tot[19:20, N_HEADS:2 * N_HEADS],
        "ssm_D": tot[19:20, 2 * N_HEADS:3 * N_HEADS], "ssm_norm_g": tot[17:18, D_MODEL:],
        "sc_conv_w": g_sc_conv, "norm_ffn_g": tot[18:19, :D_MODEL], "norm_final_g": tot[18:19, D_MODEL:],
    }
    small_w = {"norm_mix_g": (norm_mix_g, m_norm_mix_g, v_norm_mix_g),
               "ssm_conv_w": (ssm_conv_w[0], m_ssm_conv_w[0], v_ssm_conv_w[0]),
               "ssm_conv_b": (ssm_conv_b, m_ssm_conv_b, v_ssm_conv_b),
               "ssm_dt_bias": (ssm_dt_bias, m_ssm_dt_bias, v_ssm_dt_bias),
               "ssm_A_log": (ssm_A_log, m_ssm_A_log, v_ssm_A_log),
               "ssm_D": (ssm_D, m_ssm_D, v_ssm_D),
               "ssm_norm_g": (ssm_norm_g, m_ssm_norm_g, v_ssm_norm_g),
               "sc_conv_w": (sc_conv_w[0], m_sc_conv_w[0], v_sc_conv_w[0]),
               "norm_ffn_g": (norm_ffn_g, m_norm_ffn_g, v_norm_ffn_g),
               "norm_final_g": (norm_final_g.reshape(1, -1), m_norm_final_g.reshape(1, -1),
                                v_norm_final_g.reshape(1, -1))}
    PW = 1024
    order = list(small_w)

    def pack(arrs):
        rows = []
        for a in arrs:
            flat = a.reshape(-1)
            n = -(-flat.shape[0] // PW) * PW
            rows.append(jnp.pad(flat, (0, n - flat.shape[0])).reshape(-1, PW))
        slab = jnp.concatenate(rows, axis=0)
        return _pad_rows(slab, -(-slab.shape[0] // 8) * 8)

    wp = pack([small_w[k][0] for k in order])
    mp = pack([small_w[k][1] for k in order])
    vp = pack([small_w[k][2] for k in order])
    gp = pack([small_grads[k] for k in order])
    sd, sm, sv = _adamw(wp, gp, mp, vp, "adamw_small")

    def unpack(slab):
        out, row = {}, 0
        for k in order:
            shape = small_w[k][0].shape
            size = 1
            for s in shape:
                size *= s
            nr = -(-size // PW)
            out[k] = slab[row:row + nr].reshape(-1)[:size].reshape(shape)
            row += nr
        return out

    s_delta, s_m, s_v = unpack(sd), unpack(sm), unpack(sv)

    big_g = dict(zip(names, big_grads))

    weight_order = ["norm_mix_g", "w_in", "ssm_conv_w", "ssm_conv_b", "ssm_dt_bias", "ssm_A_log", "ssm_D",
                    "ssm_norm_g", "sc_conv_w", "w_out", "norm_ffn_g", "w_gate", "w_up", "w_down", "norm_final_g"]
    lead = {"ssm_conv_w", "sc_conv_w", "w_in", "w_out", "w_gate", "w_up", "w_down"}

    def shaped(nm, a):
        if nm == "norm_final_g":
            return a.reshape(D_MODEL)
        return a[None] if nm in lead else a

    grads, deltas, new_m, new_v = [], [], [], []
    for nm in weight_order:
        if nm in big_out:
            g, (d, m, v) = big_g[nm], big_out[nm]
        else:
            g, d, m, v = small_grads[nm], s_delta[nm], s_m[nm], s_v[nm]
        grads.append(shaped(nm, g))
        deltas.append(shaped(nm, d))
        new_m.append(shaped(nm, m))
        new_v.append(shaped(nm, v))
    return (loss, dx[None], *grads, *deltas, *new_m, *new_v)


def _swiglu_bwd(da, dg_factor, du_factor):
    return da * dg_factor.astype(F32), da * du_factor.astype(F32)


def _ffn_fwd(n2, w_gate, w_up):
    T, K = n2.shape
    tn = w_gate.shape[2]
    N = N_CHIPS * tn
    tm = _tile(T, 512)
    sub = _tile(tm, 256)

    def body(a_ref, wg_ref, wu_ref, g_ref, u_ref, act_ref):
        for s in range(tm // sub):
            rows = pl.ds(s * sub, sub)
            a = a_ref[rows, :]
            g = jnp.dot(a, wg_ref[...], preferred_element_type=F32)
            u = jnp.dot(a, wu_ref[...], preferred_element_type=F32)
            sig = _sigmoid(g)
            sg = g * sig
            g_ref[rows, :] = (u * (sig * (1.0 + g - sg))).astype(BF16)
            u_ref[rows, :] = sg.astype(BF16)
            act_ref[rows, :] = (sg * u).astype(BF16)

    a_spec = pl.BlockSpec((tm, K), lambda j, i: (i, 0))
    b_spec = pl.BlockSpec((None, K, tn), lambda j, i: (j, 0, 0))
    o_spec = pl.BlockSpec((tm, tn), lambda j, i: (i, j))
    return pl.pallas_call(
        body, name="ffn_fwd", grid=(N // tn, T // tm),
        in_specs=[a_spec, b_spec, b_spec], out_specs=[o_spec] * 3,
        out_shape=[jax.ShapeDtypeStruct((T, N), BF16)] * 3,
        compiler_params=_cparams(("parallel", "parallel")),
    )(n2, w_gate, w_up)
```

```python
import functools

import jax
import jax.numpy as jnp
from jax import lax
from jax.experimental import pallas as pl
from jax.experimental.pallas import tpu as pltpu

F32 = jnp.float32
BF16 = jnp.bfloat16
MESH = pl.DeviceIdType.MESH

D_MODEL = 2048
D_SSM = 2048
HEADDIM = 64
N_HEADS = 32
N_GROUPS = 8
HEADS_PER_GROUP = 4
N_STATE = 128
CHUNK = 128
K_SSM = 4
K_SC = 3
D_XBC = 4096
D_FF = 5632
D_IN = 12320
D_MAIN = 12288
OFF_XBC, OFF_CB, OFF_CC, OFF_CX = 2048, 6144, 8192, 10240
DT_PAD = 128
EPS = 1e-5
N_CHIPS = 4
N_DEV = 8

ADAM_LR = 0.001
ADAM_B1 = 0.9
ADAM_B2 = 0.999
ADAM_EPS = 1e-08
ADAM_WD = 0.01
ADAM_STEP = 10

V7X_VMEM_BYTES = 64 * 1024 * 1024
VMEM_LIMIT = V7X_VMEM_BYTES - 8 * 1024 * 1024


def _cparams(sem=None):
    if sem is None:
        return pltpu.CompilerParams(vmem_limit_bytes=VMEM_LIMIT)
    return pltpu.CompilerParams(dimension_semantics=sem, vmem_limit_bytes=VMEM_LIMIT)


def _tile(dim, pref, unit=128):
    best = None
    t = unit
    while t <= min(dim, pref):
        if dim % t == 0:
            best = t
        t += unit
    return best if best is not None else dim


def _sigmoid(x):
    return 1.0 / (1.0 + jnp.exp(-x))


def _silu(x):
    return x * _sigmoid(x)


def _dsilu(x):
    s = _sigmoid(x)
    return s * (1.0 + x * (1.0 - s))


def _softplus(x):
    return jnp.maximum(x, 0.0) + jnp.log(1.0 + jnp.exp(-jnp.abs(x)))


MATMUL_VMEM_BUDGET = 44 * 1024 * 1024


def _matmul(pairs, *, ta=False, tb=False, out_dtypes, name, tm=1024, tn=1024, tk=None, extras=(), epilogue=None,
            deps=(), col_shards=False, nsub=1, b3d=False, m_tiles=None, out_buf=None):
    a0, b0 = pairs[0]
    M, K = (a0.shape[1], a0.shape[0]) if ta else a0.shape
    if b3d:
        N = b0.shape[1] if tb else b0.shape[0] * b0.shape[2]
        tk, tn = (b0.shape[2], tn) if tb else (tk, b0.shape[2])
    else:
        N = b0.shape[0] if tb else b0.shape[1]
    tm, tn = _tile(M, tm, 8 if M % 128 else 128), _tile(N, tn)
    npair, nex, ndep, nout = len(pairs), len(extras), len(deps), len(out_dtypes)
    if tk is None:
        fixed = 2 * tm * tn * (sum(jnp.dtype(d).itemsize for d in out_dtypes) + sum(e.dtype.itemsize for e in extras))
        tk = K
        while tk > 128 and (K % tk or tk % 128 or
                            fixed + 2 * npair * 2 * tk * (tm + tn) + (tm * tn * 4 if tk < K else 0) > MATMUL_VMEM_BUDGET):
            tk -= 128
    else:
        tk = _tile(K, tk)
    nk = K // tk
    if nk > 1 or tm % nsub or (tm // nsub) % 128:
        nsub = 1
    sub = tm // nsub
    dims = (((0 if ta else 1,), (1 if tb else 0,)), ((), ()))
    i0, mi = m_tiles if m_tiles is not None else (0, M // tm)
    nbuf = 0 if out_buf is None else 1

    def body(*refs):
        a_refs = refs[0:2 * npair:2]
        b_refs = refs[1:2 * npair:2]
        ex_refs = refs[2 * npair:2 * npair + nex]
        o_refs = refs[2 * npair + nex + ndep + nbuf:2 * npair + nex + ndep + nbuf + nout]

        def dots(rows):
            s = None
            for a_ref, b_ref in zip(a_refs, b_refs):
                a = a_ref[...] if rows is None else (a_ref[:, rows] if ta else a_ref[rows, :])
                d = lax.dot_general(a, b_ref[...], dims, preferred_element_type=F32)
                s = d if s is None else s + d
            return s

        def finish(r, rows):
            ex = [e[...] if rows is None else e[rows, :] for e in ex_refs]
            outs = (r,) if epilogue is None else epilogue(r, *ex)
            for o_ref, o in zip(o_refs, outs):
                if rows is None:
                    o_ref[...] = o.astype(o_ref.dtype)
                else:
                    o_ref[rows, :] = o.astype(o_ref.dtype)

        if nk == 1:
            for s in range(nsub):
                rows = None if nsub == 1 else pl.ds(s * sub, sub)
                finish(dots(rows), rows)
            return

        acc = refs[-1]
        k = pl.program_id(2)

        @pl.when(k == 0)
        def _():
            acc[...] = dots(None)

        @pl.when(jnp.logical_and(k > 0, k < nk - 1))
        def _():
            acc[...] += dots(None)

        @pl.when(k == nk - 1)
        def _():
            finish(acc[...] + dots(None), None)

    a_spec = (pl.BlockSpec((tk, tm), lambda i, j, k: (k, i + i0)) if ta
              else pl.BlockSpec((tm, tk), lambda i, j, k: (i + i0, k)))
    if b3d:
        b_spec = (pl.BlockSpec((None, tn, tk), lambda i, j, k: (k, j, 0)) if tb
                  else pl.BlockSpec((None, tk, tn), lambda i, j, k: (j, k, 0)))
    else:
        b_spec = (pl.BlockSpec((tn, tk), lambda i, j, k: (j, k)) if tb
                  else pl.BlockSpec((tk, tn), lambda i, j, k: (k, j)))
    e_spec = pl.BlockSpec((tm, tn), lambda i, j, k: (i + i0, j))
    if col_shards:
        o_spec = pl.BlockSpec((None, tm, tn), lambda i, j, k: (j, i + i0, 0))
        o_shape = (N // tn, M, tn)
    else:
        o_spec, o_shape = e_spec, (M, N)
    args, in_specs = [], []
    for a, b in pairs:
        args += [a, b]
        in_specs += [a_spec, b_spec]
    args += list(extras) + list(deps) + ([] if out_buf is None else [out_buf])
    in_specs += [e_spec] * nex + [ANY] * (ndep + nbuf)
    outs = pl.pallas_call(
        body,
        name=name,
        grid=(mi, N // tn, nk),
        in_specs=in_specs,
        out_specs=[o_spec] * nout,
        out_shape=[jax.ShapeDtypeStruct(o_shape, dt) for dt in out_dtypes],
        input_output_aliases={} if out_buf is None else {len(args) - 1: 0},
        scratch_shapes=[pltpu.VMEM((tm, tn), F32)] if nk > 1 else [],
        compiler_params=_cparams(("parallel", "parallel", "arbitrary")),
    )(*args)
    return outs


def _cast_into_gather(w, chip_arr, name, split_cols=False, deps=()):
    R, C = w.shape
    hr, hc = (R, C // 2) if split_cols else (R // 2, C)
    tr = _tile(hr, 512, 8)
    nb = hr // tr

    def body(chip_ref, w_ref, *rest):
        rest[-1][...] = w_ref[...].astype(BF16)

    in_map = (lambda h, i, chip_ref: (i, h)) if split_cols else (lambda h, i, chip_ref: (h * nb + i, 0))
    grid_spec = pltpu.PrefetchScalarGridSpec(
        num_scalar_prefetch=1, grid=(2, nb),
        in_specs=[pl.BlockSpec((tr, hc), in_map)] + [ANY] * len(deps),
        out_specs=pl.BlockSpec((None, tr, hc), lambda h, i, chip_ref: (2 * chip_ref[0] + h, i, 0)))
    return pl.pallas_call(
        body, name=name, grid_spec=grid_spec,
        out_shape=jax.ShapeDtypeStruct((N_DEV, hr, hc), BF16),
        compiler_params=_cparams(("parallel", "parallel")),
    )(chip_arr, w, *deps)


def _tie(small, token, name):
    def body(s_ref, t_ref, o_ref):
        o_ref[...] = s_ref[...]

    vm = pl.BlockSpec(memory_space=pltpu.VMEM)
    return pl.pallas_call(body, name=name, in_specs=[vm, ANY], out_specs=vm,
                          out_shape=jax.ShapeDtypeStruct(small.shape, small.dtype))(small, token)


def _rmsnorm_fwd(x, g, name):
    T, D = x.shape
    tt = _tile(T, 256)

    def body(x_ref, g_ref, n_ref):
        xv = x_ref[...]
        r = lax.rsqrt(jnp.mean(xv * xv, axis=-1, keepdims=True) + EPS)
        n_ref[...] = (xv * r * g_ref[...]).astype(BF16)

    return pl.pallas_call(
        body, name=name, grid=(T // tt,),
        in_specs=[pl.BlockSpec((tt, D), lambda i: (i, 0)), pl.BlockSpec((1, D), lambda i: (0, 0))],
        out_specs=pl.BlockSpec((tt, D), lambda i: (i, 0)),
        out_shape=jax.ShapeDtypeStruct((T, D), BF16),
        compiler_params=_cparams(("parallel",)),
    )(x, g)


def _rmsnorm_bwd(dn, x, g, res, name):
    T, D = x.shape
    tt = _tile(T, 256)

    def body(dn_ref, x_ref, g_ref, res_ref, dx_ref, dxb_ref, dg_ref):
        @pl.when(pl.program_id(0) == 0)
        def _():
            dg_ref[...] = jnp.zeros_like(dg_ref)

        xv = x_ref[...]
        dy = dn_ref[...].astype(F32)
        r = lax.rsqrt(jnp.mean(xv * xv, axis=-1, keepdims=True) + EPS)
        xhat = xv * r
        dxh = dy * g_ref[...]
        dx = res_ref[...] + r * (dxh - xhat * jnp.mean(dxh * xhat, axis=-1, keepdims=True))
        dx_ref[...] = dx
        dxb_ref[...] = dx.astype(BF16)
        dg_ref[...] += jnp.sum(dy * xhat, axis=0, keepdims=True)

    tok = pl.BlockSpec((tt, D), lambda i: (i, 0))
    vec = pl.BlockSpec((1, D), lambda i: (0, 0))
    return pl.pallas_call(
        body, name=name, grid=(T // tt,),
        in_specs=[tok, tok, vec, tok],
        out_specs=[tok, tok, vec],
        out_shape=[jax.ShapeDtypeStruct((T, D), F32), jax.ShapeDtypeStruct((T, D), BF16),
                   jax.ShapeDtypeStruct((1, D), F32)],
        compiler_params=_cparams(("arbitrary",)),
    )(dn, x, g, res)


def _loss_and_final_bwd(h2, target, gf):
    T, D = h2.shape
    tt = _tile(T, 256)

    def body(h_ref, t_ref, g_ref, dh_ref, dhb_ref, dg_ref, loss_ref):
        @pl.when(pl.program_id(0) == 0)
        def _():
            dg_ref[...] = jnp.zeros_like(dg_ref)
            loss_ref[...] = jnp.zeros_like(loss_ref)

        xv = h_ref[...]
        r = lax.rsqrt(jnp.mean(xv * xv, axis=-1, keepdims=True) + EPS)
        xhat = xv * r
        err = xhat * g_ref[...] - t_ref[...]
        loss_ref[...] += 0.5 * jnp.sum(jnp.mean(err * err, axis=-1, keepdims=True), axis=0, keepdims=True)
        dy = err * (1.0 / D)
        dxh = dy * g_ref[...]
        dx = r * (dxh - xhat * jnp.mean(dxh * xhat, axis=-1, keepdims=True))
        dh_ref[...] = dx
        dhb_ref[...] = dx.astype(BF16)
        dg_ref[...] += jnp.sum(dy * xhat, axis=0, keepdims=True)

    tok = pl.BlockSpec((tt, D), lambda i: (i, 0))
    vec = pl.BlockSpec((1, D), lambda i: (0, 0))
    return pl.pallas_call(
        body, name="loss_final_bwd", grid=(T // tt,),
        in_specs=[tok, tok, vec],
        out_specs=[tok, tok, vec, pl.BlockSpec((1, 1), lambda i: (0, 0))],
        out_shape=[jax.ShapeDtypeStruct((T, D), F32), jax.ShapeDtypeStruct((T, D), BF16),
                   jax.ShapeDtypeStruct((1, D), F32), jax.ShapeDtypeStruct((1, 1), F32)],
        compiler_params=_cparams(("arbitrary",)),
    )(h2, target, gf)


def _gated_norm_fwd(y, proj, g):
    T, D = y.shape
    tt = _tile(T, 256)

    def body(y_ref, z_ref, g_ref, o_ref):
        yg = y_ref[...] * _silu(z_ref[...])
        r = lax.rsqrt(jnp.mean(yg * yg, axis=-1, keepdims=True) + EPS)
        o_ref[...] = (yg * r * g_ref[...]).astype(BF16)

    tok = pl.BlockSpec((tt, D), lambda i: (i, 0))
    return pl.pallas_call(
        body, name="gated_norm_fwd", grid=(T // tt,),
        in_specs=[tok, tok, pl.BlockSpec((1, D), lambda i: (0, 0))],
        out_specs=tok,
        out_shape=jax.ShapeDtypeStruct((T, 2 * D_MODEL), BF16),
        compiler_params=_cparams(("parallel",)),
    )(y, proj, g)


def _gated_norm_bwd(dmix, y, proj, g, dproj):
    T, D = y.shape
    tt = _tile(T, 256)

    def body(do_ref, y_ref, z_ref, g_ref, dp_ref, dy_ref, dz_ref, dg_ref):
        @pl.when(pl.program_id(0) == 0)
        def _():
            dg_ref[...] = jnp.zeros_like(dg_ref)

        yv, zv = y_ref[...], z_ref[...]
        do = do_ref[...].astype(F32)
        sz = _silu(zv)
        yg = yv * sz
        r = lax.rsqrt(jnp.mean(yg * yg, axis=-1, keepdims=True) + EPS)
        xhat = yg * r
        dxh = do * g_ref[...]
        dyg = r * (dxh - xhat * jnp.mean(dxh * xhat, axis=-1, keepdims=True))
        dy_ref[...] = dyg * sz
        dz_ref[...] = (dyg * yv * _dsilu(zv)).astype(BF16)
        dg_ref[...] += jnp.sum(do * xhat, axis=0, keepdims=True)

    tok = pl.BlockSpec((tt, D), lambda i: (i, 0))
    vec = pl.BlockSpec((1, D), lambda i: (0, 0))
    return pl.pallas_call(
        body, name="gated_norm_bwd", grid=(T // tt,),
        in_specs=[tok, tok, tok, vec, ANY],
        out_specs=[tok, tok, vec],
        out_shape=[jax.ShapeDtypeStruct((T, D), F32), jax.ShapeDtypeStruct(dproj.shape, BF16),
                   jax.ShapeDtypeStruct((1, D), F32)],
        input_output_aliases={4: 1},
        compiler_params=_cparams(("arbitrary",)),
    )(dmix, y, proj, g, dproj)


HALO = 8


def _shift_down(cur, prev8, s):
    ext = jnp.concatenate([prev8, cur], axis=0)
    return pltpu.roll(ext, s, axis=0)[HALO:]


def _shift_up(cur, next8, s):
    n = cur.shape[0]
    ext = jnp.concatenate([cur, next8], axis=0)
    return pltpu.roll(ext, n + HALO - s, axis=0)[:n]


def _conv_specs(tt, cb, col_off_blocks, nt):
    hb = tt // HALO
    cur = pl.BlockSpec((tt, cb), lambda j, i: (i, col_off_blocks + j))
    prev = pl.BlockSpec((HALO, cb), lambda j, i: (jnp.maximum(i * hb - 1, 0), col_off_blocks + j))
    nxt = pl.BlockSpec((HALO, cb), lambda j, i: (jnp.minimum((i + 1) * hb, nt * hb - 1), col_off_blocks + j))
    return cur, prev, nxt


def _taps(cur, prev8, K):
    return [_shift_down(cur, prev8, K - 1 - k) for k in range(K - 1)] + [cur]


def _conv_of_taps(taps, w):
    y = taps[-1] * w[len(taps) - 1:len(taps), :]
    for k, t in enumerate(taps[:-1]):
        y = y + t * w[k:k + 1, :]
    return y


def _causal_conv(cur, prev8, w, K):
    return _conv_of_taps(_taps(cur, prev8, K), w)


def _anticausal_conv(cur, next8, w, K):
    y = cur * w[K - 1:K, :]
    for k in range(K - 1):
        y = y + _shift_up(cur, next8, K - 1 - k) * w[k:k + 1, :]
    return y


def _ssm_conv_fwd(proj, w8, b):
    T = proj.shape[0]
    tt, cb = _tile(T, 512), 512
    nt = T // tt
    cur, prev, _ = _conv_specs(tt, cb, OFF_XBC // cb, nt)

    def body(u_ref, up_ref, w_ref, b_ref, o_ref):
        first = pl.program_id(1) == 0
        p8 = jnp.where(first, 0.0, up_ref[...])
        pre = _causal_conv(u_ref[...], p8, w_ref[...], K_SSM) + b_ref[...]
        o_ref[...] = _silu(pre)

    return pl.pallas_call(
        body, name="ssm_conv_fwd", grid=(D_XBC // cb, nt),
        in_specs=[cur, prev, pl.BlockSpec((8, cb), lambda j, i: (0, j)), pl.BlockSpec((1, cb), lambda j, i: (0, j))],
        out_specs=pl.BlockSpec((tt, cb), lambda j, i: (i, j)),
        out_shape=jax.ShapeDtypeStruct((T, D_XBC), F32),
        compiler_params=_cparams(("parallel", "parallel")),
    )(proj, proj, w8, b)


def _ssm_conv_bwd(dact, proj, w8, b, dproj):
    T = proj.shape[0]
    tt, cb = _tile(T, 512), 512
    nt = T // tt
    cur, prev, nxt = _conv_specs(tt, cb, OFF_XBC // cb, nt)
    dcur, dprev, dnxt = _conv_specs(tt, cb, 0, nt)

    def dpre_of(d, u, p8, w, bb):
        pre = _causal_conv(u, p8, w, K_SSM) + bb
        return d * _dsilu(pre)

    def body(d_ref, dn_ref, u_ref, up_ref, un_ref, w_ref, b_ref, dp_ref, dx_ref, dw_ref, db_ref):
        i = pl.program_id(1)

        @pl.when(i == 0)
        def _():
            dw_ref[...] = jnp.zeros_like(dw_ref)
            db_ref[...] = jnp.zeros_like(db_ref)

        w, bb = w_ref[...], b_ref[...]
        u = u_ref[...]
        p8 = jnp.where(i == 0, 0.0, up_ref[...])
        taps = _taps(u, p8, K_SSM)
        dpre = d_ref[...] * _dsilu(_conv_of_taps(taps, w) + bb)
        un = un_ref[...]
        dpre_n = dpre_of(dn_ref[...], un, u[tt - HALO:, :], w, bb)
        dpre_n = jnp.where(i == nt - 1, 0.0, dpre_n)
        dx_ref[...] = _anticausal_conv(dpre, dpre_n, w, K_SSM).astype(BF16)
        rows = [jnp.sum(dpre * t, axis=0, keepdims=True) for t in taps]
        rows.append(jnp.zeros((8 - K_SSM, cb), F32))
        dw_ref[...] += jnp.concatenate(rows, axis=0)
        db_ref[...] += jnp.sum(dpre, axis=0, keepdims=True)

    wspec = pl.BlockSpec((8, cb), lambda j, i: (0, j))
    bspec = pl.BlockSpec((1, cb), lambda j, i: (0, j))
    return pl.pallas_call(
        body, name="ssm_conv_bwd", grid=(D_XBC // cb, nt),
        in_specs=[dcur, dnxt, cur, prev, nxt, wspec, bspec, ANY],
        out_specs=[pl.BlockSpec((tt, cb), lambda j, i: (i, OFF_XBC // cb + j)), wspec, bspec],
        out_shape=[jax.ShapeDtypeStruct(dproj.shape, BF16), jax.ShapeDtypeStruct((8, D_XBC), F32),
                   jax.ShapeDtypeStruct((1, D_XBC), F32)],
        input_output_aliases={7: 0},
        compiler_params=_cparams(("parallel", "arbitrary")),
    )(dact, dact, proj, proj, proj, w8, b, dproj)


SCB = 512
SC3 = 3 * SCB


def _sc_specs(tt, nt):
    hb = tt // HALO
    cur = pl.BlockSpec((tt, SC3), lambda j, i: (i, OFF_CB // SC3 + j))
    prev = pl.BlockSpec((HALO, SC3), lambda j, i: (jnp.maximum(i * hb - 1, 0), OFF_CB // SC3 + j))
    nxt = pl.BlockSpec((HALO, SC3), lambda j, i: (jnp.minimum((i + 1) * hb, nt * hb - 1), OFF_CB // SC3 + j))
    return cur, prev, nxt


def _shortconv_fwd(proj, w8, ymix):
    T = proj.shape[0]
    tt = _tile(T, 512)
    nt = T // tt
    cur, prev, _ = _sc_specs(tt, nt)

    def body(p_ref, pp_ref, w_ref, y_ref, o_ref):
        p, pp = p_ref[...], pp_ref[...]
        v = p[:, SCB:2 * SCB] * p[:, 2 * SCB:]
        vp = jnp.where(pl.program_id(1) == 0, 0.0, pp[:, SCB:2 * SCB] * pp[:, 2 * SCB:])
        o_ref[...] = (p[:, :SCB] * _causal_conv(v, vp, w_ref[...], K_SC)).astype(BF16)

    return pl.pallas_call(
        body, name="shortconv_fwd", grid=(D_MODEL // SCB, nt),
        in_specs=[cur, prev, pl.BlockSpec((8, SCB), lambda j, i: (0, j)), ANY],
        out_specs=pl.BlockSpec((tt, SCB), lambda j, i: (i, D_SSM // SCB + j)),
        out_shape=jax.ShapeDtypeStruct(ymix.shape, BF16),
        input_output_aliases={3: 0},
        compiler_params=_cparams(("parallel", "parallel")),
    )(proj, proj, w8, ymix)


def _shortconv_bwd(dmix, proj, w8):
    T = proj.shape[0]
    tt = _tile(T, 512)
    nt = T // tt
    hb = tt // HALO
    cur, prev, nxt = _sc_specs(tt, nt)
    d_s = pl.BlockSpec((tt, SCB), lambda j, i: (i, D_SSM // SCB + j))
    dn_s = pl.BlockSpec((HALO, SCB), lambda j, i: (jnp.minimum((i + 1) * hb, nt * hb - 1), D_SSM // SCB + j))

    def body(d_ref, dn_ref, p_ref, pp_ref, pn_ref, w_ref, dp_ref, dw_ref):
        i = pl.program_id(1)

        @pl.when(i == 0)
        def _():
            dw_ref[...] = jnp.zeros_like(dw_ref)

        w = w_ref[...]
        p, pp = p_ref[...], pp_ref[...]
        gb, gc, u = p[:, :SCB], p[:, SCB:2 * SCB], p[:, 2 * SCB:]
        v = gc * u
        vp = jnp.where(i == 0, 0.0, pp[:, SCB:2 * SCB] * pp[:, 2 * SCB:])
        d = d_ref[...].astype(F32)
        taps = _taps(v, vp, K_SC)
        dp_ref[:, :SCB] = (d * _conv_of_taps(taps, w)).astype(BF16)
        dcv = d * gb
        dcv_n = jnp.where(i == nt - 1, 0.0, dn_ref[...].astype(F32) * pn_ref[:, :SCB])
        dv = _anticausal_conv(dcv, dcv_n, w, K_SC)
        dp_ref[:, SCB:2 * SCB] = (dv * u).astype(BF16)
        dp_ref[:, 2 * SCB:] = (dv * gc).astype(BF16)
        rows = [jnp.sum(dcv * t, axis=0, keepdims=True) for t in taps]
        rows.append(jnp.zeros((8 - K_SC, SCB), F32))
        dw_ref[...] += jnp.concatenate(rows, axis=0)

    wspec = pl.BlockSpec((8, SCB), lambda j, i: (0, j))
    return pl.pallas_call(
        body, name="shortconv_bwd", grid=(D_MODEL // SCB, nt),
        in_specs=[d_s, dn_s, cur, prev, nxt, wspec],
        out_specs=[cur, wspec],
        out_shape=[jax.ShapeDtypeStruct((T, D_MAIN), BF16), jax.ShapeDtypeStruct((8, D_MODEL), F32)],
        compiler_params=_cparams(("parallel", "arbitrary")),
    )(dmix, dmix, proj, proj, proj, w8)


GW = HEADS_PER_GROUP * HEADDIM


def _dot(a, b):
    return jnp.dot(a.astype(BF16), b.astype(BF16), preferred_element_type=F32)


def _dot_nt(a, b):
    return lax.dot_general(a.astype(BF16), b.astype(BF16), (((1,), (1,)), ((), ())), preferred_element_type=F32)


def _dot_tn(a, b):
    return lax.dot_general(a.astype(BF16), b.astype(BF16), (((0,), (0,)), ((), ())), preferred_element_type=F32)


def _bf16_terms(x, n):
    terms, r = [], x
    for _ in range(n):
        t = r.astype(BF16)
        terms.append(t)
        r = r - t.astype(F32)
    return terms


def _dot_sel(a, sel, n=2):
    s = sel.astype(BF16)
    return sum(jnp.dot(t, s, preferred_element_type=F32) for t in _bf16_terms(a, n))


def _sel_dot(sel, b, n=2):
    s = sel.astype(BF16)
    return sum(jnp.dot(s, t, preferred_element_type=F32) for t in _bf16_terms(b, n))


def _sel_dot_nt(sel, b, n=2):
    s = sel.astype(BF16)
    return sum(lax.dot_general(s, t, (((1,), (1,)), ((), ())), preferred_element_type=F32)
               for t in _bf16_terms(b, n))


def _head_cols(rows):
    parts = [jnp.broadcast_to(rows[r:r + 1, :], (HEADDIM, CHUNK)) for r in range(HEADS_PER_GROUP)]
    return jnp.concatenate(parts, axis=0).T


def _head_rows(rows):
    parts = [jnp.broadcast_to(rows[r:r + 1, :], (HEADDIM, N_STATE)) for r in range(HEADS_PER_GROUP)]
    return jnp.concatenate(parts, axis=0)


def _ssd_common(dtr, bias, alog):
    dt = _softplus(dtr + bias)
    A = -jnp.exp(alog)
    a = dt * A
    ki = lax.broadcasted_iota(jnp.int32, (CHUNK, CHUNK), 0)
    si = lax.broadcasted_iota(jnp.int32, (CHUNK, CHUNK), 1)
    upper = (ki <= si).astype(F32)
    cs = _dot_sel(a, upper, 3)
    cs_last = jnp.broadcast_to(cs[:, CHUNK - 1:CHUNK], (8, CHUNK))
    return dt, A, a, cs, cs_last


def _decay_matrix(cs, r):
    li = lax.broadcasted_iota(jnp.int32, (CHUNK, CHUNK), 0)
    si = lax.broadcasted_iota(jnp.int32, (CHUNK, CHUNK), 1)
    causal = li >= si
    R = jnp.broadcast_to(cs[r:r + 1, :], (CHUNK, CHUNK))
    seg = jnp.where(causal, R.T - R, 0.0)
    return jnp.where(causal, jnp.exp(seg), 0.0)


def _decay_cat(cs):
    return jnp.concatenate([_decay_matrix(cs, r) for r in range(HEADS_PER_GROUP)], axis=1)


def _lanes4(m):
    return jnp.concatenate([m] * HEADS_PER_GROUP, axis=1)


def _head_blocks(v):
    col = lax.broadcasted_iota(jnp.int32, v.shape, 1) // HEADDIM
    return jnp.concatenate([jnp.where(col == r, v, jnp.zeros_like(v)) for r in range(HEADS_PER_GROUP)], axis=0)


GXBC = GW + 2 * N_STATE


GS = 4


def _ssd_in_specs(nc, rev):
    cix = (lambda c: nc - 1 - c) if rev else (lambda c: c)
    x_s = pl.BlockSpec((CHUNK, GS * GW), lambda g, c: (cix(c), g))
    xbc_s = pl.BlockSpec((CHUNK, GS * GXBC), lambda g, c: (cix(c), g))
    dtr_s = pl.BlockSpec((GS, 8, CHUNK), lambda g, c: (g, 0, cix(c)))
    row_s = pl.BlockSpec((GS, 8, CHUNK), lambda g, c: (g, 0, 0))
    drep_s = pl.BlockSpec((1, GS * GW), lambda g, c: (0, g))
    hs_s = pl.BlockSpec((1, GS * GW, N_STATE), lambda g, c: (cix(c), g, 0))
    return x_s, xbc_s, dtr_s, row_s, drep_s, hs_s


def _xbc_parts(xbc_ref, gi):
    o = gi * GXBC
    return xbc_ref[:, o:o + GW], xbc_ref[:, o + GW:o + GW + N_STATE], xbc_ref[:, o + GW + N_STATE:o + GXBC]


def _ssd_fwd(xbc, dtr, bias, alog, drep):
    T = xbc.shape[0]
    nc = T // CHUNK
    x_s, xbc_s, dtr_s, row_s, drep_s, hs_s = _ssd_in_specs(nc, False)

    def body(xbc_ref, dtr_ref, bias_ref, alog_ref, drep_ref, y_ref, hs_ref, h_scr):
        @pl.when(pl.program_id(1) == 0)
        def _():
            h_scr[...] = jnp.zeros_like(h_scr)

        for gi in range(GS):
            cols, rows = slice(gi * GW, (gi + 1) * GW), pl.ds(gi * GW, GW)
            x, Bm, Cm = _xbc_parts(xbc_ref, gi)
            dt, A, a, cs, cs_last = _ssd_common(dtr_ref[gi], bias_ref[gi], alog_ref[gi])
            E = _head_cols(jnp.exp(cs))
            W = _head_cols(jnp.exp(cs_last - cs) * dt)
            X = (x * _head_cols(dt)).astype(BF16)
            CB = _dot_nt(Cm, Bm)
            col = lax.broadcasted_iota(jnp.int32, (CHUNK, GW), 1) // HEADDIM
            y = jnp.zeros((CHUNK, GW), F32)
            for r in range(HEADS_PER_GROUP):
                y = y + jnp.where(col == r, _dot(CB * _decay_matrix(cs, r), X), 0.0)
            h = h_scr[rows, :]
            hs_ref[0, rows, :] = h
            y = y + _dot_nt(Cm, h) * E
            y_ref[:, cols] = y + drep_ref[:, cols] * x
            h_scr[rows, :] = h * _head_rows(jnp.exp(cs_last)) + _dot_tn(x * W, Bm)

    return pl.pallas_call(
        body, name="ssd_fwd", grid=(N_GROUPS // GS, nc),
        in_specs=[xbc_s, dtr_s, row_s, row_s, drep_s],
        out_specs=[x_s, hs_s],
        out_shape=[jax.ShapeDtypeStruct((T, D_SSM), F32), jax.ShapeDtypeStruct((nc, D_SSM, N_STATE), F32)],
        scratch_shapes=[pltpu.VMEM((GS * GW, N_STATE), F32)],
        compiler_params=_cparams(("parallel", "arbitrary")),
    )(xbc, dtr, bias, alog, drep)


def _ssd_bwd(xbc, dtr, bias, alog, drep, dy, hs):
    T = xbc.shape[0]
    nc = T // CHUNK
    x_s, xbc_s, dtr_s, row_s, drep_s, hs_s = _ssd_in_specs(nc, True)

    def body(xbc_ref, dtr_ref, bias_ref, alog_ref, drep_ref, dy_ref, hs_ref,
             dxbc_ref, ddtr_ref, dbias_ref, dalog_ref, dd_ref, dh_scr):
        @pl.when(pl.program_id(1) == 0)
        def _():
            dh_scr[...] = jnp.zeros_like(dh_scr)
            dbias_ref[...] = jnp.zeros_like(dbias_ref)
            dalog_ref[...] = jnp.zeros_like(dalog_ref)
            dd_ref[...] = jnp.zeros_like(dd_ref)

        for gi in range(GS):
            one_group(gi, xbc_ref, dtr_ref, bias_ref, alog_ref, drep_ref, dy_ref, hs_ref,
                      dxbc_ref, ddtr_ref, dbias_ref, dalog_ref, dd_ref, dh_scr)

    def one_group(gi, xbc_ref, dtr_ref, bias_ref, alog_ref, drep_ref, dy_ref, hs_ref,
                  dxbc_ref, ddtr_ref, dbias_ref, dalog_ref, dd_ref, dh_scr):
        cols, rows, o = slice(gi * GW, (gi + 1) * GW), pl.ds(gi * GW, GW), gi * GXBC
        x, Bm, Cm = _xbc_parts(xbc_ref, gi)
        dY = dy_ref[:, cols]
        dt, A, a, cs, cs_last = _ssd_common(dtr_ref[gi], bias_ref[gi], alog_ref[gi])
        E = _head_cols(jnp.exp(cs))
        DT = _head_cols(dt)
        Wd = _head_cols(jnp.exp(cs_last - cs))
        X = x * DT
        h = hs_ref[0, rows, :]
        dS = dh_scr[rows, :]
        CB = _dot_nt(Cm, Bm)
        rowid = lax.broadcasted_iota(jnp.int32, (8, CHUNK), 0)
        lane = lax.broadcasted_iota(jnp.int32, (8, CHUNK), 1)
        hsel = (lax.broadcasted_iota(jnp.int32, (8, GW), 1) // HEADDIM
                == lax.broadcasted_iota(jnp.int32, (8, GW), 0)).astype(F32)
        hsel_l = (lax.broadcasted_iota(jnp.int32, (8, HEADS_PER_GROUP * CHUNK), 1) // CHUNK
                  == lax.broadcasted_iota(jnp.int32, (8, HEADS_PER_GROUP * CHUNK), 0)).astype(F32)

        Lc, CBc = _decay_cat(cs), _lanes4(CB)
        Mc = CBc * Lc
        GLc = _dot_nt(dY, _head_blocks(X.astype(BF16))) * Lc
        Wc = GLc * CBc
        colsum = jnp.sum(Wc, axis=0, keepdims=True)
        dcs = _sel_dot_nt(hsel_l, Wc)
        dCB = jnp.zeros((CHUNK, CHUNK), F32)
        for r in range(HEADS_PER_GROUP):
            blk = slice(r * CHUNK, (r + 1) * CHUNK)
            dCB = dCB + GLc[:, blk]
            dcs = dcs - jnp.where(rowid == r, colsum[:, blk], 0.0)
        m_stack = jnp.concatenate([Mc[:, r * CHUNK:(r + 1) * CHUNK].astype(BF16) for r in range(HEADS_PER_GROUP)],
                                  axis=0)
        dX = lax.dot_general(m_stack, _head_blocks(dY.astype(BF16)), (((0,), (0,)), ((), ())),
                             preferred_element_type=F32)
        dC = _dot(dCB, Bm)
        dB = _dot_tn(dCB, Cm)
        T1 = _dot_nt(Bm, dS)
        dX = dX + T1 * Wd
        dB = dB + _dot(X * Wd, dS)
        pdec = _sel_dot_nt(hsel, X * T1 * Wd)
        dcs = dcs - pdec
        dlast = jnp.sum(pdec, axis=1, keepdims=True) \
            + jnp.exp(cs_last[:, 0:1]) * jnp.sum(_sel_dot(hsel, dS * h), axis=1, keepdims=True)
        dYE = dY * E
        dC = dC + _dot(dYE, h)
        yoff = _dot_nt(Cm, h) * E
        dcs = dcs + _sel_dot_nt(hsel, dY * yoff)
        dcs = dcs + jnp.where(lane == CHUNK - 1, dlast, 0.0)
        ki = lax.broadcasted_iota(jnp.int32, (CHUNK, CHUNK), 0)
        si = lax.broadcasted_iota(jnp.int32, (CHUNK, CHUNK), 1)
        lower = (ki >= si).astype(F32)
        da = _dot_sel(dcs, lower)
        ddt = da * A + _sel_dot_nt(hsel, dX * x)
        ddtr = ddt * _sigmoid(dtr_ref[gi] + bias_ref[gi])
        ddtr_ref[gi] = ddtr
        dbias_ref[gi] += ddtr
        dalog_ref[gi] += da * a
        dxbc_ref[:, o:o + GW] = dX * DT + drep_ref[:, cols] * dY
        dd_ref[:, cols] += jnp.sum(dY * x, axis=0, keepdims=True)
        dxbc_ref[:, o + GW:o + GW + N_STATE] = dB
        dxbc_ref[:, o + GW + N_STATE:o + GXBC] = dC
        dh_scr[rows, :] = dS * _head_rows(jnp.exp(cs_last)) + _dot_tn(dYE, Cm)

    return pl.pallas_call(
        body, name="ssd_bwd", grid=(N_GROUPS // GS, nc),
        in_specs=[xbc_s, dtr_s, row_s, row_s, drep_s, x_s, hs_s],
        out_specs=[xbc_s, dtr_s, row_s, row_s, drep_s],
        out_shape=[jax.ShapeDtypeStruct((T, D_XBC), F32),
                   jax.ShapeDtypeStruct((N_GROUPS, 8, T), F32),
                   jax.ShapeDtypeStruct((N_GROUPS, 8, CHUNK), F32),
                   jax.ShapeDtypeStruct((N_GROUPS, 8, CHUNK), F32),
                   jax.ShapeDtypeStruct((1, D_SSM), F32)],
        scratch_shapes=[pltpu.VMEM((GS * GW, N_STATE), F32)],
        compiler_params=_cparams(("parallel", "arbitrary")),
    )(xbc, dtr, bias, alog, drep, dy, hs)


def _adamw(w, g, m, v, name, deps=(), emit_g=False):
    R, C = w.shape
    tr = _tile(R, 256, 8)
    nd = len(deps)
    nout = 4 if emit_g else 3

    def body(w_ref, g_ref, m_ref, v_ref, *rest):
        outs = rest[nd:]
        gv = g_ref[...]
        mn = ADAM_B1 * m_ref[...] + (1.0 - ADAM_B1) * gv
        vn = ADAM_B2 * v_ref[...] + (1.0 - ADAM_B2) * (gv * gv)
        m_hat = mn / (1.0 - ADAM_B1 ** ADAM_STEP)
        v_hat = vn / (1.0 - ADAM_B2 ** ADAM_STEP)
        outs[0][...] = -ADAM_LR * (m_hat / (jnp.sqrt(v_hat) + ADAM_EPS) + ADAM_WD * w_ref[...])
        outs[1][...] = mn
        outs[2][...] = vn
        if emit_g:
            outs[3][...] = gv

    spec = pl.BlockSpec((tr, C), lambda i: (i, 0))
    return pl.pallas_call(
        body, name=name, grid=(R // tr,),
        in_specs=[spec] * 4 + [ANY] * nd, out_specs=[spec] * nout,
        out_shape=[jax.ShapeDtypeStruct((R, C), F32)] * nout,
        compiler_params=_cparams(("parallel",)),
    )(w, g, m, v, *deps)


ANY = pl.BlockSpec(memory_space=pl.ANY)


def _place():
    x, y, c = lax.axis_index("x"), lax.axis_index("y"), lax.axis_index("c")
    return x, y, c


def _other_chips(x, y):
    return [(1 - x, y), (x, 1 - y), (1 - x, 1 - y)]


def _allgather_inplace(bufs, splits, first_done=False):
    n = len(bufs)

    def body(*refs):
        o_refs = refs[n:2 * n]
        send_sems, recv_sems = refs[2 * n:]
        x, y, c = _place()
        xn, yn, dg, sibling = (1 - x, y), (x, 1 - y), (1 - x, 1 - y), (x, y, 1 - c)

        def blk(k, chip, pc):
            return o_refs[k].at[4 * chip[0] + 2 * chip[1] + pc]

        def part(k, ref, p):
            kind, s = splits[k]
            _, R, C = bufs[k].shape
            if kind == "rows":
                return ref.at[pl.ds(0, s)] if p == 0 else ref.at[pl.ds(s, R - s)]
            return ref.at[:, pl.ds(0, s)] if p == 0 else ref.at[:, pl.ds(s, C - s)]

        def copy(k, slot, ref, to):
            return pltpu.make_async_remote_copy(
                src_ref=ref, dst_ref=ref, send_sem=send_sems.at[k, slot], recv_sem=recv_sems.at[k, slot],
                device_id=to, device_id_type=MESH)

        sent = []

        def send(k, slot, ref, to):
            cp = copy(k, slot, ref, to)
            cp.start()
            sent.append(cp)

        if not first_done:
            for k in range(n):
                send(k, 0, blk(k, (x, y), c), (*xn, c))
                send(k, 1, blk(k, (x, y), c), (*yn, c))
        for k in range(n):
            bx, by = blk(k, xn, c), blk(k, yn, c)
            if not first_done:
                copy(k, 0, bx, sibling).wait_recv()
            send(k, 2, part(k, bx, 0), (*yn, c))
            send(k, 4, bx, sibling)
            if not first_done:
                copy(k, 1, by, sibling).wait_recv()
            send(k, 3, part(k, by, 1), (*xn, c))
            send(k, 5, by, sibling)
        for k in range(n):
            d0, d1 = part(k, blk(k, dg, c), 0), part(k, blk(k, dg, c), 1)
            copy(k, 2, d0, sibling).wait_recv()
            send(k, 6, d0, sibling)
            copy(k, 3, d1, sibling).wait_recv()
            send(k, 7, d1, sibling)
        for k in range(n):
            copy(k, 4, blk(k, xn, 1 - c), sibling).wait_recv()
            copy(k, 5, blk(k, yn, 1 - c), sibling).wait_recv()
            copy(k, 6, part(k, blk(k, dg, 1 - c), 0), sibling).wait_recv()
            copy(k, 7, part(k, blk(k, dg, 1 - c), 1), sibling).wait_recv()
        for cp in sent:
            cp.wait_send()

    return pl.pallas_call(
        body, name="allgather_w_in",
        in_specs=[ANY] * n, out_specs=[ANY] * n,
        out_shape=[jax.ShapeDtypeStruct(b.shape, b.dtype) for b in bufs],
        input_output_aliases={k: k for k in range(n)},
        scratch_shapes=[pltpu.SemaphoreType.DMA((n, 8)), pltpu.SemaphoreType.DMA((n, 8))],
    )(*bufs)


HBM = pl.BlockSpec(memory_space=pltpu.HBM)
SEM = pl.BlockSpec(memory_space=pltpu.SEMAPHORE)
EFFECT = pltpu.SideEffectType.DATAFLOW_SIDE_EFFECTING


def _split_start(name, arrays, build, n_copies, after=()):
    na, nd = len(arrays), len(after)

    def body(*refs):
        send_sems, recv_sems = refs[na + nd], refs[na + nd + 1]
        for cp in build(refs[:na], send_sems, recv_sems):
            cp.start()
        refs[-1][...] = jnp.zeros((8, 128), F32)

    outs = pl.pallas_call(
        body, name=name,
        out_shape=(pltpu.SemaphoreType.DMA((n_copies,)), pltpu.SemaphoreType.DMA((n_copies,)),
                   *[pltpu.HBM(a.shape, a.dtype) for a in arrays], jax.ShapeDtypeStruct((8, 128), F32)),
        in_specs=[HBM] * na + [ANY] * nd,
        out_specs=(SEM, SEM, *[HBM] * na, pl.BlockSpec(memory_space=pltpu.VMEM)),
        input_output_aliases={i: 2 + i for i in range(na)},
        compiler_params=pltpu.CompilerParams(has_side_effects=EFFECT),
    )(*[pltpu.with_memory_space_constraint(a, pltpu.HBM) for a in arrays], *after)
    return outs[0], outs[1], list(outs[2:2 + na]), outs[-1]


def _split_wait(name, send_sems, recv_sems, arrays, build, after):
    na = len(arrays)

    def body(*refs):
        for cp in build(refs[:na], refs[na], refs[na + 1]):
            cp.wait_send()
            cp.wait_recv()

    outs = pl.pallas_call(
        body, name=name,
        out_shape=tuple(pltpu.HBM(a.shape, a.dtype) for a in arrays),
        in_specs=[HBM] * na + [SEM, SEM] + [ANY] * len(after),
        out_specs=tuple([HBM] * na),
        input_output_aliases={i: i for i in range(na)},
        compiler_params=pltpu.CompilerParams(has_side_effects=EFFECT),
    )(*arrays, send_sems, recv_sems, *after)
    return list(outs)


def _remote(src, dst, send_sems, recv_sems, i, to):
    return pltpu.make_async_remote_copy(src_ref=src, dst_ref=dst, send_sem=send_sems.at[i], recv_sem=recv_sems.at[i],
                                        device_id=to, device_id_type=MESH)


def _build_ag_first(refs, ss, rs):
    x, y, c = _place()
    cps = []
    for k, ref in enumerate(refs):
        blk = ref.at[4 * x + 2 * y + c]
        cps += [_remote(blk, blk, ss, rs, 2 * k, (1 - x, y, c)), _remote(blk, blk, ss, rs, 2 * k + 1, (x, 1 - y, c))]
    return cps


def _build_ag_ici(refs, ss, rs):
    x, y, c = _place()
    cps = []
    for k, ref in enumerate(refs):
        blk = ref.at[4 * x + 2 * y + c]
        for j, (px, py) in enumerate(_other_chips(x, y)):
            cps.append(_remote(blk, blk, ss, rs, 3 * k + j, (px, py, c)))
    return cps


def _build_ag_fwd(refs, ss, rs):
    x, y, c = _place()
    cps = []
    for k, ref in enumerate(refs):
        for j, (px, py) in enumerate(_other_chips(x, y)):
            blk = ref.at[4 * px + 2 * py + c]
            cps.append(_remote(blk, blk, ss, rs, 3 * k + j, (x, y, 1 - c)))
    return cps


def _build_rs_swap(refs, ss, rs):
    x, y, c = _place()
    n = len(refs) // 2
    return [_remote(refs[k].at[:, pl.ds(1 - c, 1)], refs[n + k], ss, rs, k, (x, y, 1 - c)) for k in range(n)]


def _build_rs_ici(refs, ss, rs):
    x, y, c = _place()
    n = len(refs) // 2
    me = 2 * x + y
    cps = []
    for k in range(n):
        for j, (px, py) in enumerate(_other_chips(x, y)):
            cps.append(_remote(refs[k].at[2 * px + py], refs[n + k].at[me], ss, rs, 3 * k + j, (px, py, c)))
    return cps


def _build_rs_share(refs, ss, rs):
    x, y, c = _place()
    return [_remote(ref.at[c], ref.at[c], ss, rs, k, (x, y, 1 - c)) for k, ref in enumerate(refs)]


def _build_small_gather(refs, ss, rs):
    x, y, c = _place()
    me = 4 * x + 2 * y + c
    cps = []
    for d in range(1, N_DEV):
        to = (1 - x if d & 4 else x, 1 - y if d & 2 else y, 1 - c if d & 1 else c)
        cps.append(_remote(refs[0], refs[1].at[me], ss, rs, d - 1, to))
    return cps


def _sum_gathered(mine, landed, me_arr):
    R, C = mine.shape

    def body(me_ref, m_ref, l_ref, o_ref):
        me = me_ref[0]
        s = None
        for d in range(N_DEV):
            t = jnp.where(me == d, m_ref[...], l_ref[d])
            s = t if s is None else s + t
        o_ref[...] = s

    grid_spec = pltpu.PrefetchScalarGridSpec(
        num_scalar_prefetch=1, grid=(1,),
        in_specs=[pl.BlockSpec((R, C), lambda i, me_ref: (0, 0)),
                  pl.BlockSpec((N_DEV, R, C), lambda i, me_ref: (0, 0, 0))],
        out_specs=pl.BlockSpec((R, C), lambda i, me_ref: (0, 0)))
    return pl.pallas_call(
        body, name="sum_small", grid_spec=grid_spec,
        out_shape=jax.ShapeDtypeStruct((R, C), F32),
        compiler_params=_cparams(("arbitrary",)),
    )(me_arr, mine, landed)


def _rs_add_pair(p, r0, c_arr, name):
    _, _, hr, cols = p.shape
    tr = _tile(hr, 256, 8)

    def body(c_ref, p_ref, r_ref, q_ref):
        q_ref[...] = (p_ref[0].astype(F32) + r_ref[0].astype(F32)).astype(BF16)

    grid_spec = pltpu.PrefetchScalarGridSpec(
        num_scalar_prefetch=1, grid=(N_CHIPS, hr // tr),
        in_specs=[pl.BlockSpec((1, 1, tr, cols), lambda j, i, c_ref: (j, c_ref[0], i, 0)),
                  pl.BlockSpec((1, 1, tr, cols), lambda j, i, c_ref: (j, 0, i, 0))],
        out_specs=pl.BlockSpec((1, tr, cols), lambda j, i, c_ref: (j, i, 0)))
    return pl.pallas_call(
        body, name=name, grid_spec=grid_spec,
        out_shape=jax.ShapeDtypeStruct((N_CHIPS, hr, cols), BF16),
        compiler_params=_cparams(("parallel", "parallel")),
    )(c_arr, p, r0)


def _rs_add_chips(r1, q, place_arr, name):
    _, hr, cols = r1.shape
    tr = _tile(hr, 256, 8)

    def body(place_ref, r_ref, q_ref, o_ref):
        chip = place_ref[0]
        s = None
        for j in range(N_CHIPS):
            t = jnp.where(chip == j, q_ref[j], r_ref[j]).astype(F32)
            s = t if s is None else s + t
        o_ref[...] = s

    blk = pl.BlockSpec((N_CHIPS, tr, cols), lambda i, place_ref: (0, i, 0))
    grid_spec = pltpu.PrefetchScalarGridSpec(
        num_scalar_prefetch=1, grid=(hr // tr,), in_specs=[blk, blk],
        out_specs=pl.BlockSpec((None, tr, cols), lambda i, place_ref: (place_ref[1], i, 0)))
    return pl.pallas_call(
        body, name=name, grid_spec=grid_spec,
        out_shape=jax.ShapeDtypeStruct((2, hr, cols), F32),
        compiler_params=_cparams(("parallel",)),
    )(place_arr, r1, q)


def _pad_rows(a, rows):
    return jnp.pad(a, ((0, rows - a.shape[0]), (0, 0)))


def _pad_cols(a, cols):
    return jnp.pad(a, ((0, 0), (0, cols - a.shape[1])))


def _heads_to_rows(v):
    v = v.reshape(N_GROUPS, HEADS_PER_GROUP, 1)
    v = jnp.pad(v, ((0, 0), (0, 8 - HEADS_PER_GROUP), (0, 0)))
    return jnp.broadcast_to(v, (N_GROUPS, 8, CHUNK))


def _rows_to_heads(a):
    return jnp.sum(a[:, :HEADS_PER_GROUP, :], axis=-1).reshape(N_HEADS)


def _to_kernel_rows(a):
    C = a.shape[1]
    x0, b0, c0, s0 = D_SSM, 2 * D_SSM, 2 * D_SSM + 1024, D_SSM + D_XBC + N_HEADS
    xbc = jnp.concatenate([a[x0:b0].reshape(N_GROUPS, GW, C), a[b0:c0].reshape(N_GROUPS, N_STATE, C),
                           a[c0:c0 + 1024].reshape(N_GROUPS, N_STATE, C)], axis=1).reshape(D_XBC, C)
    sc = jnp.concatenate([a[s0 + k * D_MODEL:s0 + (k + 1) * D_MODEL].reshape(D_MODEL // SCB, SCB, C)
                          for k in range(3)], axis=1).reshape(3 * D_MODEL, C)
    return jnp.concatenate([a[:D_SSM], xbc, sc], axis=0)


HR_IN = 1568


def _shard_row_plan():
    segs = [(0, 0, 0, D_SSM)]
    for g in range(N_GROUPS):
        k0 = D_SSM + g * GXBC
        segs += [(0, k0, D_SSM + g * GW, GW), (0, k0 + GW, 2 * D_SSM + g * N_STATE, N_STATE),
                 (0, k0 + GW + N_STATE, 2 * D_SSM + 1024 + g * N_STATE, N_STATE)]
    segs.append((1, 0, D_SSM + D_XBC, N_HEADS))
    for j in range(D_MODEL // SCB):
        for k in range(3):
            segs.append((0, D_SSM + D_XBC + j * SC3 + k * SCB, D_SSM + D_XBC + N_HEADS + k * D_MODEL + j * SCB, SCB))
    cs = D_IN // N_CHIPS
    plan = []
    for src, s, o, n in segs:
        while n > 0:
            chip, loc = divmod(o, cs)
            half, row = divmod(loc, HR_IN)
            m = min(n, cs - loc, HR_IN - row)
            plan.append((src, s, chip, half, row, m))
            s, o, n = s + m, o + m, n - m
    return plan


SCATTER_ROWS = 512
SCATTER_SLOTS = 4


def _scatter_rows_to_shards(k_main, k_dt):
    C = k_main.shape[1]
    pieces = []
    for src, s, chip, half, row, n in _shard_row_plan():
        for o in range(0, n, SCATTER_ROWS):
            pieces.append((src, s + o, chip, half, row + o, min(SCATTER_ROWS, n - o)))
    S, lag, N = SCATTER_SLOTS, SCATTER_SLOTS // 2, len(pieces)

    def body(m_ref, d_ref, o_ref, buf, in_sems, out_sems):
        def cin(i):
            src, s, _, _, _, n = pieces[i]
            return pltpu.make_async_copy((d_ref if src else m_ref).at[pl.ds(s, n)],
                                         buf.at[i % S, pl.ds(0, n)], in_sems.at[i % S])

        def cout(i):
            _, _, chip, half, row, n = pieces[i]
            return pltpu.make_async_copy(buf.at[i % S, pl.ds(0, n)],
                                         o_ref.at[chip, half, pl.ds(row, n)], out_sems.at[i % S])

        for i in range(N + lag):
            if i < N:
                if i >= S:
                    cout(i - S).wait()
                cin(i).start()
            j = i - lag
            if 0 <= j < N:
                cin(j).wait()
                cout(j).start()
        for j in range(max(0, N - S), N):
            cout(j).wait()

    return pl.pallas_call(
        body, name="scatter_dw_in_rows", in_specs=[ANY, ANY], out_specs=ANY,
        out_shape=jax.ShapeDtypeStruct((N_CHIPS, 2, HR_IN, C), k_main.dtype),
        scratch_shapes=[pltpu.VMEM((S, SCATTER_ROWS, C), k_main.dtype),
                        pltpu.SemaphoreType.DMA((S,)), pltpu.SemaphoreType.DMA((S,))],
        compiler_params=_cparams(),
    )(k_main, k_dt)


def _to_kernel_xbc(a):
    R = a.shape[0]
    return jnp.concatenate([a[:, :D_SSM].reshape(R, N_GROUPS, GW), a[:, D_SSM:D_SSM + 1024].reshape(R, N_GROUPS, N_STATE),
                            a[:, D_SSM + 1024:].reshape(R, N_GROUPS, N_STATE)], axis=2).reshape(R, D_XBC)


def _from_kernel_xbc(a):
    R = a.shape[0]
    g = a.reshape(R, N_GROUPS, GXBC)
    return jnp.concatenate([g[:, :, :GW].reshape(R, D_SSM), g[:, :, GW:GW + N_STATE].reshape(R, 1024),
                            g[:, :, GW + N_STATE:].reshape(R, 1024)], axis=1)


def kernel(x, norm_mix_g, w_in, ssm_conv_w, ssm_conv_b, ssm_dt_bias, ssm_A_log, ssm_D, ssm_norm_g, sc_conv_w, w_out, norm_ffn_g, w_gate, w_up, w_down, norm_final_g, loss_target, m_norm_mix_g, m_w_in, m_ssm_conv_w, m_ssm_conv_b, m_ssm_dt_bias, m_ssm_A_log, m_ssm_D, m_ssm_norm_g, m_sc_conv_w, m_w_out, m_norm_ffn_g, m_w_gate, m_w_up, m_w_down, m_norm_final_g, v_norm_mix_g, v_w_in, v_ssm_conv_w, v_ssm_conv_b, v_ssm_dt_bias, v_ssm_A_log, v_ssm_D, v_ssm_norm_g, v_sc_conv_w, v_w_out, v_norm_ffn_g, v_w_gate, v_w_up, v_w_down, v_norm_final_g):
    T = x.shape[1]
    xt = x[0]
    tgt = loss_target[0]
    cx, cy, cc = lax.axis_index("x"), lax.axis_index("y"), lax.axis_index("c")
    chip = 2 * cx + cy
    c_arr = jnp.reshape(cc, (1,)).astype(jnp.int32)
    chip_arr = jnp.reshape(chip, (1,)).astype(jnp.int32)
    place_arr = jnp.stack([chip, cc]).astype(jnp.int32)

    big = [w_in[0].T, w_out[0], w_gate[0], w_up[0], w_down[0]]
    names = ["w_in", "w_out", "w_gate", "w_up", "w_down"]
    gb_in = _cast_into_gather(big[0], chip_arr, "cast_w_in", split_cols=True)
    cs_in, cs_conv = D_IN // N_CHIPS, D_XBC // N_CHIPS
    cw = jnp.stack([_pad_rows(ssm_conv_w[0], 8), _pad_cols(_pad_rows(sc_conv_w[0], 8), cs_conv)])
    cw_buf = lax.dynamic_update_slice(jnp.zeros((N_DEV, 8, cs_conv), F32), cw, (2 * chip, 0, 0))
    f_ss, f_rs, f_arr, f_tok = _split_start("ag_in_first_start", [gb_in, cw_buf], _build_ag_first, 4)
    gbufs = [None] + [_cast_into_gather(w, chip_arr, "cast_" + nm, deps=[f_tok]) for w, nm in zip(big[1:], names[1:])]
    n1 = _rmsnorm_fwd(xt, _tie(norm_mix_g, f_tok, "tie_ag_first"), "rmsnorm_mix")
    f_arr = _split_wait("ag_in_first_wait", f_ss, f_rs, f_arr, _build_ag_first, after=gbufs[1:] + [n1])
    g_in, cw_all = _allgather_inplace(f_arr, [("rows", (cs_in // 32) * 16), ("cols", cs_conv // 2)], first_done=True)
    cw_all = cw_all.reshape(N_CHIPS, 2, 8, cs_conv)
    ssm_w8 = _to_kernel_xbc(cw_all[:, 0].transpose(1, 0, 2).reshape(8, D_XBC))
    sc_w8 = cw_all[:, 1, :, :D_MODEL // N_CHIPS].transpose(1, 0, 2).reshape(8, D_MODEL)
    ssm_bk = _to_kernel_xbc(ssm_conv_b)
    wt = g_in.reshape(N_CHIPS, 2, cs_in, D_MODEL // 2).transpose(0, 2, 1, 3).reshape(D_IN, D_MODEL)
    wt_main = _to_kernel_rows(wt)
    wt_dt = _pad_rows(wt[D_SSM + D_XBC:D_SSM + D_XBC + N_HEADS], DT_PAD)
    ag_ss, ag_rs, ag_bufs, ag_tok = _split_start("ag_ici_start", gbufs[1:], _build_ag_ici, 12, after=[g_in, cw_all])

    bias_rows = _heads_to_rows(ssm_dt_bias[0])
    alog_rows = _heads_to_rows(ssm_A_log[0])
    drep = jnp.repeat(ssm_D[0], HEADDIM).reshape(1, D_SSM)

    (proj,) = _matmul([(n1, wt_main)], tb=True, out_dtypes=[F32], name="mm_proj", deps=[ag_tok])
    (dt_raw,) = _matmul([(n1, wt_dt)], tb=True, out_dtypes=[F32], name="mm_proj_dt")
    xbc = _ssm_conv_fwd(proj, ssm_w8, ssm_bk)
    dtr = jnp.pad(dt_raw[:, :N_HEADS].T.reshape(N_GROUPS, HEADS_PER_GROUP, T), ((0, 0), (0, 4), (0, 0)))
    y_ssd, hs = _ssd_fwd(xbc, dtr, bias_rows, alog_rows, drep)
    ag_bufs = _split_wait("ag_ici_wait", ag_ss, ag_rs, ag_bufs, _build_ag_ici, after=[y_ssd])
    fw_ss, fw_rs, fw_bufs, fw_tok = _split_start("ag_fwd_start", ag_bufs, _build_ag_fwd, 12)
    y_mix = _shortconv_fwd(proj, sc_w8, _gated_norm_fwd(y_ssd, proj, _tie(ssm_norm_g, fw_tok, "tie_ag_fwd")))
    gath = _split_wait("ag_fwd_wait", fw_ss, fw_rs, fw_bufs, _build_ag_fwd, after=[y_mix])
    w_out_f = gath[0].reshape(2 * D_MODEL, D_MODEL)
    w_gate3 = gath[1].reshape(N_CHIPS, D_MODEL, D_FF // N_CHIPS)
    w_up3 = gath[2].reshape(N_CHIPS, D_MODEL, D_FF // N_CHIPS)
    w_down_f = gath[3].reshape(D_FF, D_MODEL)
    (h1,) = _matmul([(y_mix, w_out_f)], out_dtypes=[F32], name="mm_out", extras=[xt],
                    epilogue=lambda acc, res: (acc + res,))
    n2 = _rmsnorm_fwd(h1, norm_ffn_g, "rmsnorm_ffn")
    g_act, u_act, a_act = _ffn_fwd(n2, w_gate3, w_up3)
    (h2,) = _matmul([(a_act, w_down_f)], out_dtypes=[F32], name="mm_down", extras=[h1],
                    epilogue=lambda acc, res: (acc + res,))

    dh2, dh2b, dg_final, loss_part = _loss_and_final_bwd(h2, tgt, norm_final_g.reshape(1, D_MODEL))
    dg_act, du_act = _matmul([(dh2b, w_down_f)], tb=True, out_dtypes=[BF16, BF16], name="mm_down_bwd",
                             tn=512, extras=[g_act, u_act], epilogue=_swiglu_bwd, nsub=2)
    (dw_down,) = _matmul([(a_act, dh2b)], ta=True, out_dtypes=[BF16], name="mm_dw_down", tm=1408, tn=512)
    (dn2,) = _matmul([(dg_act, w_gate3), (du_act, w_up3)], tb=True, b3d=True, out_dtypes=[BF16],
                     name="mm_ffn_in_bwd")
    (dw_gate,) = _matmul([(n2, dg_act)], ta=True, out_dtypes=[BF16], name="mm_dw_gate", tm=512, tn=1408,
                         col_shards=True)
    (dw_up,) = _matmul([(n2, du_act)], ta=True, out_dtypes=[BF16], name="mm_dw_up", tm=512, tn=1408,
                       col_shards=True)
    dh1, dh1b, dg_ffn = _rmsnorm_bwd(dn2, h1, norm_ffn_g, dh2, "rmsnorm_ffn_bwd")
    (dw_out,) = _matmul([(y_mix, dh1b)], ta=True, out_dtypes=[BF16], name="mm_dw_out")

    def halves(g):
        return g.reshape(N_CHIPS, 2, g.shape[1] // 2, g.shape[2])

    def landing(shape, dtype):
        return lax.empty(shape, dtype)

    names1 = names[1:]
    ps1 = [halves(dw_out.reshape(N_CHIPS, -1, D_MODEL)), halves(dw_gate), halves(dw_up),
           halves(dw_down.reshape(N_CHIPS, -1, D_MODEL))]
    r0_1 = [landing((N_CHIPS, 1) + p.shape[2:], p.dtype) for p in ps1]
    sw_ss, sw_rs, sw_arr, sw_tok = _split_start("rs1_swap_start", ps1 + r0_1, _build_rs_swap, 4)
    (dmix,) = _matmul([(dh1b, w_out_f)], tb=True, out_dtypes=[BF16], name="mm_out_bwd", deps=[sw_tok])
    dproj, dw_sc = _shortconv_bwd(dmix, proj, sc_w8)
    dy_ssd, dproj, dg_ssmnorm = _gated_norm_bwd(dmix, y_ssd, proj, ssm_norm_g, dproj)
    sw_arr = _split_wait("rs1_swap_wait", sw_ss, sw_rs, sw_arr, _build_rs_swap, after=[dy_ssd])
    qs1 = [_rs_add_pair(p, r, c_arr, "rs_add_pair_" + nm) for p, r, nm in zip(sw_arr[:4], sw_arr[4:], names1)]
    r1_1 = [landing(q.shape, BF16) for q in qs1]
    ic_ss, ic_rs, ic_arr, ic_tok = _split_start("rs1_ici_start", qs1 + r1_1, _build_rs_ici, 12)
    dxbc_act, ddtr, dbias_acc, dalog_acc, dD_acc = _ssd_bwd(
        xbc, dtr, bias_rows, alog_rows, _tie(drep, ic_tok, "tie_rs1_ici"), dy_ssd, hs)
    dproj, dw_ssmconv, db_ssmconv = _ssm_conv_bwd(dxbc_act, proj, ssm_w8, ssm_bk, dproj)
    dw_ssmconv, db_ssmconv = _from_kernel_xbc(dw_ssmconv), _from_kernel_xbc(db_ssmconv)
    ic_arr = _split_wait("rs1_ici_wait", ic_ss, ic_rs, ic_arr, _build_rs_ici, after=[dproj])
    g1 = [_rs_add_chips(r, q, place_arr, "rs_add_chips_" + nm) for q, r, nm in zip(ic_arr[:4], ic_arr[4:], names1)]
    sh_ss, sh_rs, sh_arr, sh_tok = _split_start("rs1_share_start", g1, _build_rs_share, 4)

    ddt_raw = _pad_cols(ddtr[:, :HEADS_PER_GROUP, :].reshape(N_HEADS, T).T, DT_PAD).astype(BF16)
    (dwt_main,) = _matmul([(dproj, n1)], ta=True, out_dtypes=[F32], name="mm_dw_main", deps=[sh_tok])
    (dwt_dt,) = _matmul([(ddt_raw, n1)], ta=True, out_dtypes=[F32], name="mm_dw_dt")
    p_in = _scatter_rows_to_shards(dwt_main, dwt_dt)
    s2_ss, s2_rs, s2_arr, s2_tok = _split_start(
        "rs2_swap_start", [p_in, landing((N_CHIPS, 1) + p_in.shape[2:], F32)], _build_rs_swap, 1)
    tm_pb = 512
    mt = T // _tile(T, tm_pb)
    mt_a = max(mt // 4, 1)
    (dn1a,) = _matmul([(dproj, wt_main)], out_dtypes=[F32], name="mm_proj_bwd_a", deps=[s2_tok], tm=tm_pb,
                      m_tiles=(0, mt_a))
    g1 = _split_wait("rs1_share_wait", sh_ss, sh_rs, sh_arr, _build_rs_share, after=[dn1a])
    s2_arr = _split_wait("rs2_swap_wait", s2_ss, s2_rs, s2_arr, _build_rs_swap, after=[dn1a])
    q_in = _rs_add_pair(s2_arr[0], s2_arr[1], c_arr, "rs_add_pair_w_in")
    i2_ss, i2_rs, i2_arr, i2_tok = _split_start(
        "rs2_ici_start", [q_in, landing(q_in.shape, BF16)], _build_rs_ici, 3)
    if mt > mt_a:
        (dn1a,) = _matmul([(dproj, wt_main)], out_dtypes=[F32], name="mm_proj_bwd_b", deps=[i2_tok], tm=tm_pb,
                          m_tiles=(mt_a, mt - mt_a), out_buf=dn1a)
    (dn1,) = _matmul([(ddt_raw, wt_dt)], out_dtypes=[BF16], name="mm_proj_dt_bwd", extras=[dn1a],
                     epilogue=lambda acc, res: (acc + res,), deps=[i2_tok])
    dx, _, dg_mix = _rmsnorm_bwd(dn1, xt, norm_mix_g, dh1, "rmsnorm_mix_bwd")

    big_m = [m_w_in[0].T, m_w_out[0], m_w_gate[0], m_w_up[0], m_w_down[0]]
    big_v = [v_w_in[0].T, v_w_out[0], v_w_gate[0], v_w_up[0], v_w_down[0]]
    big_grads = [None] + [g.reshape(w.shape) for g, w in zip(g1, big[1:])]
    big_out = {}
    for k in range(1, 5):
        *big_out[names[k]], big_grads[k] = _adamw(big[k], big_grads[k], big_m[k], big_v[k], "adamw_" + names[k],
                                                   deps=[i2_tok], emit_g=True)
    i2_arr = _split_wait("rs2_ici_wait", i2_ss, i2_rs, i2_arr, _build_rs_ici, after=[big_out[names[4]][0], dx])
    g_in_red = _rs_add_chips(i2_arr[1], i2_arr[0], place_arr, "rs_add_chips_w_in")
    s3_ss, s3_rs, s3_arr, s3_tok = _split_start("rs2_share_start", [g_in_red], _build_rs_share, 1)

    dD = jnp.sum(dD_acc.reshape(N_HEADS, HEADDIM), axis=-1)
    heads_row = jnp.concatenate([_rows_to_heads(dbias_acc), _rows_to_heads(dalog_acc), dD,
                                 loss_part.reshape(1)]).reshape(1, -1)
    small = jnp.concatenate([
        dw_ssmconv,
        _pad_cols(dw_sc, D_XBC),
        db_ssmconv,
        jnp.concatenate([dg_mix, dg_ssmnorm], axis=1),
        jnp.concatenate([dg_ffn, dg_final], axis=1),
        _pad_cols(heads_row, D_XBC),
        jnp.zeros((4, D_XBC), F32),
    ], axis=0)
    sm_ss, sm_rs, sm_arr, sm_tok = _split_start(
        "small_gather_start", [small, landing((N_DEV,) + small.shape, F32)], _build_small_gather, N_DEV - 1,
        after=[s3_tok])
    (g_in_full,) = _split_wait("rs2_share_wait", s3_ss, s3_rs, s3_arr, _build_rs_share, after=[sm_tok])
    d_t, m_t, v_t, g_t = _adamw(big[0], g_in_full.reshape(2 * HR_IN, D_MODEL), big_m[0], big_v[0],
                                "adamw_" + names[0], emit_g=True)
    big_grads[0] = g_t.T
    big_out[names[0]] = (d_t.T, m_t.T, v_t.T)
    sm_arr = _split_wait("small_gather_wait", sm_ss, sm_rs, sm_arr, _build_small_gather, after=[d_t])
    tot = _sum_gathered(sm_arr[0], sm_arr[1], jnp.reshape(4 * cx + 2 * cy + cc, (1,)).astype(jnp.int32))
    loss = tot[19, 3 * N_HEADS]

    cs_ssm, cs_sc = D_XBC // N_CHIPS, D_MODEL // N_CHIPS
    g_ssm_conv = lax.dynamic_slice(tot[0:K_SSM], (0, chip * cs_ssm), (K_SSM, cs_ssm))
    g_sc_conv = lax.dynamic_slice(tot[8:8 + K_SC, :D_MODEL], (0, chip * cs_sc), (K_SC, cs_sc))
    small_grads = {
        "norm_mix_g": tot[17:18, :D_MODEL], "ssm_conv_w": g_ssm_conv, "ssm_conv_b": tot[16:17],
        "ssm_dt_bias": tot[19:20, 0:N_HEADS], "ssm_A_log": tot[19:20, N_HEADS:2 * N_HEADS],
        "ssm_D": tot[19:20, 2 * N_HEADS:3 * N_HEADS], "ssm_norm_g": tot[17:18, D_MODEL:],
        "sc_conv_w": g_sc_conv, "norm_ffn_g": tot[18:19, :D_MODEL], "norm_final_g": tot[18:19, D_MODEL:],
    }
    small_w = {"norm_mix_g": (norm_mix_g, m_norm_mix_g, v_norm_mix_g),
               "ssm_conv_w": (ssm_conv_w[0], m_ssm_conv_w[0], v_ssm_conv_w[0]),
               "ssm_conv_b": (ssm_conv_b, m_ssm_conv_b, v_ssm_conv_b),
               "ssm_dt_bias": (ssm_dt_bias, m_ssm_dt_bias, v_ssm_dt_bias),
               "ssm_A_log": (ssm_A_log, m_ssm_A_log, v_ssm_A_log),
               "ssm_D": (ssm_D, m_ssm_D, v_ssm_D),
               "ssm_norm_g": (ssm_norm_g, m_ssm_norm_g, v_ssm_norm_g),
               "sc_conv_w": (sc_conv_w[0], m_sc_conv_w[0], v_sc_conv_w[0]),
               "norm_ffn_g": (norm_ffn_g, m_norm_ffn_g, v_norm_ffn_g),
               "norm_final_g": (norm_final_g.reshape(1, -1), m_norm_final_g.reshape(1, -1),
                                v_norm_final_g.reshape(1, -1))}
    PW = 1024
    order = list(small_w)

    def pack(arrs):
        rows = []
        for a in arrs:
            flat = a.reshape(-1)
            n = -(-flat.shape[0] // PW) * PW
            rows.append(jnp.pad(flat, (0, n - flat.shape[0])).reshape(-1, PW))
        slab = jnp.concatenate(rows, axis=0)
        return _pad_rows(slab, -(-slab.shape[0] // 8) * 8)

    wp = pack([small_w[k][0] for k in order])
    mp = pack([small_w[k][1] for k in order])
    vp = pack([small_w[k][2] for k in order])
    gp = pack([small_grads[k] for k in order])
    sd, sm, sv = _adamw(wp, gp, mp, vp, "adamw_small")

    def unpack(slab):
        out, row = {}, 0
        for k in order:
            shape = small_w[k][0].shape
            size = 1
            for s in shape:
                size *= s
            nr = -(-size // PW)
            out[k] = slab[row:row + nr].reshape(-1)[:size].reshape(shape)
            row += nr
        return out

    s_delta, s_m, s_v = unpack(sd), unpack(sm), unpack(sv)

    big_g = dict(zip(names, big_grads))

    weight_order = ["norm_mix_g", "w_in", "ssm_conv_w", "ssm_conv_b", "ssm_dt_bias", "ssm_A_log", "ssm_D",
                    "ssm_norm_g", "sc_conv_w", "w_out", "norm_ffn_g", "w_gate", "w_up", "w_down", "norm_final_g"]
    lead = {"ssm_conv_w", "sc_conv_w", "w_in", "w_out", "w_gate", "w_up", "w_down"}

    def shaped(nm, a):
        if nm == "norm_final_g":
            return a.reshape(D_MODEL)
        return a[None] if nm in lead else a

    grads, deltas, new_m, new_v = [], [], [], []
    for nm in weight_order:
        if nm in big_out:
            g, (d, m, v) = big_g[nm], big_out[nm]
        else:
            g, d, m, v = small_grads[nm], s_delta[nm], s_m[nm], s_v[nm]
        grads.append(shaped(nm, g))
        deltas.append(shaped(nm, d))
        new_m.append(shaped(nm, m))
        new_v.append(shaped(nm, v))
    return (loss, dx[None], *grads, *deltas, *new_m, *new_v)


def _swiglu_bwd(da, dg_factor, du_factor):
    return da * dg_factor.astype(F32), da * du_factor.astype(F32)


def _ffn_fwd(n2, w_gate, w_up):
    T, K = n2.shape
    tn = w_gate.shape[2]
    N = N_CHIPS * tn
    tm = _tile(T, 512)
    sub = _tile(tm, 256)

    def body(a_ref, wg_ref, wu_ref, g_ref, u_ref, act_ref):
        for s in range(tm // sub):
            rows = pl.ds(s * sub, sub)
            a = a_ref[rows, :]
            g = jnp.dot(a, wg_ref[...], preferred_element_type=F32)
            u = jnp.dot(a, wu_ref[...], preferred_element_type=F32)
            sig = _sigmoid(g)
            sg = g * sig
            g_ref[rows, :] = (u * (sig * (1.0 + g - sg))).astype(BF16)
            u_ref[rows, :] = sg.astype(BF16)
            act_ref[rows, :] = (sg * u).astype(BF16)

    a_spec = pl.BlockSpec((tm, K), lambda j, i: (i, 0))
    b_spec = pl.BlockSpec((None, K, tn), lambda j, i: (j, 0, 0))
    o_spec = pl.BlockSpec((tm, tn), lambda j, i: (i, j))
    return pl.pallas_call(
        body, name="ffn_fwd", grid=(N // tn, T // tm),
        in_specs=[a_spec, b_spec, b_spec], out_specs=[o_spec] * 3,
        out_shape=[jax.ShapeDtypeStruct((T, N), BF16)] * 3,
        compiler_params=_cparams(("parallel", "parallel")),
    )(n2, w_gate, w_up)
```

```python
import functools

import jax
import jax.numpy as jnp
from jax import lax
from jax.experimental import pallas as pl
from jax.experimental.pallas import tpu as pltpu

F32 = jnp.float32
BF16 = jnp.bfloat16
MESH = pl.DeviceIdType.MESH

D_MODEL = 2048
D_SSM = 2048
HEADDIM = 64
N_HEADS = 32
N_GROUPS = 8
HEADS_PER_GROUP = 4
N_STATE = 128
CHUNK = 128
K_SSM = 4
K_SC = 3
D_XBC = 4096
D_FF = 5632
D_IN = 12320
D_MAIN = 12288
OFF_XBC, OFF_CB, OFF_CC, OFF_CX = 2048, 6144, 8192, 10240
DT_PAD = 128
EPS = 1e-5
N_CHIPS = 4
N_DEV = 8

ADAM_LR = 0.001
ADAM_B1 = 0.9
ADAM_B2 = 0.999
ADAM_EPS = 1e-08
ADAM_WD = 0.01
ADAM_STEP = 10

V7X_VMEM_BYTES = 64 * 1024 * 1024
VMEM_LIMIT = V7X_VMEM_BYTES - 8 * 1024 * 1024


def _cparams(sem=None):
    if sem is None:
        return pltpu.CompilerParams(vmem_limit_bytes=VMEM_LIMIT)
    return pltpu.CompilerParams(dimension_semantics=sem, vmem_limit_bytes=VMEM_LIMIT)


def _tile(dim, pref, unit=128):
    best = None
    t = unit
    while t <= min(dim, pref):
        if dim % t == 0:
            best = t
        t += unit
    return best if best is not None else dim


def _sigmoid(x):
    return 1.0 / (1.0 + jnp.exp(-x))


def _silu(x):
    return x * _sigmoid(x)


def _dsilu(x):
    s = _sigmoid(x)
    return s * (1.0 + x * (1.0 - s))


def _softplus(x):
    return jnp.maximum(x, 0.0) + jnp.log(1.0 + jnp.exp(-jnp.abs(x)))


MATMUL_VMEM_BUDGET = 44 * 1024 * 1024


def _matmul(pairs, *, ta=False, tb=False, out_dtypes, name, tm=1024, tn=1024, tk=None, extras=(), epilogue=None,
            deps=(), col_shards=False, nsub=1, b3d=False, m_tiles=None, out_buf=None):
    a0, b0 = pairs[0]
    M, K = (a0.shape[1], a0.shape[0]) if ta else a0.shape
    if b3d:
        N = b0.shape[1] if tb else b0.shape[0] * b0.shape[2]
        tk, tn = (b0.shape[2], tn) if tb else (tk, b0.shape[2])
    else:
        N = b0.shape[0] if tb else b0.shape[1]
    tm, tn = _tile(M, tm, 8 if M % 128 else 128), _tile(N, tn)
    npair, nex, ndep, nout = len(pairs), len(extras), len(deps), len(out_dtypes)
    if tk is None:
        fixed = 2 * tm * tn * (sum(jnp.dtype(d).itemsize for d in out_dtypes) + sum(e.dtype.itemsize for e in extras))
        tk = K
        while tk > 128 and (K % tk or tk % 128 or
                            fixed + 2 * npair * 2 * tk * (tm + tn) + (tm * tn * 4 if tk < K else 0) > MATMUL_VMEM_BUDGET):
            tk -= 128
    else:
        tk = _tile(K, tk)
    nk = K // tk
    if nk > 1 or tm % nsub or (tm // nsub) % 128:
        nsub = 1
    sub = tm // nsub
    dims = (((0 if ta else 1,), (1 if tb else 0,)), ((), ()))
    i0, mi = m_tiles if m_tiles is not None else (0, M // tm)
    nbuf = 0 if out_buf is None else 1

    def body(*refs):
        a_refs = refs[0:2 * npair:2]
        b_refs = refs[1:2 * npair:2]
        ex_refs = refs[2 * npair:2 * npair + nex]
        o_refs = refs[2 * npair + nex + ndep + nbuf:2 * npair + nex + ndep + nbuf + nout]

        def dots(rows):
            s = None
            for a_ref, b_ref in zip(a_refs, b_refs):
                a = a_ref[...] if rows is None else (a_ref[:, rows] if ta else a_ref[rows, :])
                d = lax.dot_general(a, b_ref[...], dims, preferred_element_type=F32)
                s = d if s is None else s + d
            return s

        def finish(r, rows):
            ex = [e[...] if rows is None else e[rows, :] for e in ex_refs]
            outs = (r,) if epilogue is None else epilogue(r, *ex)
            for o_ref, o in zip(o_refs, outs):
                if rows is None:
                    o_ref[...] = o.astype(o_ref.dtype)
                else:
                    o_ref[rows, :] = o.astype(o_ref.dtype)

        if nk == 1:
            for s in range(nsub):
                rows = None if nsub == 1 else pl.ds(s * sub, sub)
                finish(dots(rows), rows)
            return

        acc = refs[-1]
        k = pl.program_id(2)

        @pl.when(k == 0)
        def _():
            acc[...] = dots(None)

        @pl.when(jnp.logical_and(k > 0, k < nk - 1))
        def _():
            acc[...] += dots(None)

        @pl.when(k == nk - 1)
        def _():
            finish(acc[...] + dots(None), None)

    a_spec = (pl.BlockSpec((tk, tm), lambda i, j, k: (k, i + i0)) if ta
              else pl.BlockSpec((tm, tk), lambda i, j, k: (i + i0, k)))
    if b3d:
        b_spec = (pl.BlockSpec((None, tn, tk), lambda i, j, k: (k, j, 0)) if tb
                  else pl.BlockSpec((None, tk, tn), lambda i, j, k: (j, k, 0)))
    else:
        b_spec = (pl.BlockSpec((tn, tk), lambda i, j, k: (j, k)) if tb
                  else pl.BlockSpec((tk, tn), lambda i, j, k: (k, j)))
    e_spec = pl.BlockSpec((tm, tn), lambda i, j, k: (i + i0, j))
    if col_shards:
        o_spec = pl.BlockSpec((None, tm, tn), lambda i, j, k: (j, i + i0, 0))
        o_shape = (N // tn, M, tn)
    else:
        o_spec, o_shape = e_spec, (M, N)
    args, in_specs = [], []
    for a, b in pairs:
        args += [a, b]
        in_specs += [a_spec, b_spec]
    args += list(extras) + list(deps) + ([] if out_buf is None else [out_buf])
    in_specs += [e_spec] * nex + [ANY] * (ndep + nbuf)
    outs = pl.pallas_call(
        body,
        name=name,
        grid=(mi, N // tn, nk),
        in_specs=in_specs,
        out_specs=[o_spec] * nout,
        out_shape=[jax.ShapeDtypeStruct(o_shape, dt) for dt in out_dtypes],
        input_output_aliases={} if out_buf is None else {len(args) - 1: 0},
        scratch_shapes=[pltpu.VMEM((tm, tn), F32)] if nk > 1 else [],
        compiler_params=_cparams(("parallel", "parallel", "arbitrary")),
    )(*args)
    return outs


def _cast_into_gather(w, chip_arr, name, split_cols=False, deps=()):
    R, C = w.shape
    hr, hc = (R, C // 2) if split_cols else (R // 2, C)
    tr = _tile(hr, 512, 8)
    nb = hr // tr

    def body(chip_ref, w_ref, *rest):
        rest[-1][...] = w_ref[...].astype(BF16)

    in_map = (lambda h, i, chip_ref: (i, h)) if split_cols else (lambda h, i, chip_ref: (h * nb + i, 0))
    grid_spec = pltpu.PrefetchScalarGridSpec(
        num_scalar_prefetch=1, grid=(2, nb),
        in_specs=[pl.BlockSpec((tr, hc), in_map)] + [ANY] * len(deps),
        out_specs=pl.BlockSpec((None, tr, hc), lambda h, i, chip_ref: (2 * chip_ref[0] + h, i, 0)))
    return pl.pallas_call(
        body, name=name, grid_spec=grid_spec,
        out_shape=jax.ShapeDtypeStruct((N_DEV, hr, hc), BF16),
        compiler_params=_cparams(("parallel", "parallel")),
    )(chip_arr, w, *deps)


def _tie(small, token, name):
    def body(s_ref, t_ref, o_ref):
        o_ref[...] = s_ref[...]

    vm = pl.BlockSpec(memory_space=pltpu.VMEM)
    return pl.pallas_call(body, name=name, in_specs=[vm, ANY], out_specs=vm,
                          out_shape=jax.ShapeDtypeStruct(small.shape, small.dtype))(small, token)


def _rmsnorm_fwd(x, g, name):
    T, D = x.shape
    tt = _tile(T, 256)

    def body(x_ref, g_ref, n_ref):
        xv = x_ref[...]
        r = lax.rsqrt(jnp.mean(xv * xv, axis=-1, keepdims=True) + EPS)
        n_ref[...] = (xv * r * g_ref[...]).astype(BF16)

    return pl.pallas_call(
        body, name=name, grid=(T // tt,),
        in_specs=[pl.BlockSpec((tt, D), lambda i: (i, 0)), pl.BlockSpec((1, D), lambda i: (0, 0))],
        out_specs=pl.BlockSpec((tt, D), lambda i: (i, 0)),
        out_shape=jax.ShapeDtypeStruct((T, D), BF16),
        compiler_params=_cparams(("parallel",)),
    )(x, g)


def _rmsnorm_bwd(dn, x, g, res, name):
    T, D = x.shape
    tt = _tile(T, 256)

    def body(dn_ref, x_ref, g_ref, res_ref, dx_ref, dxb_ref, dg_ref):
        @pl.when(pl.program_id(0) == 0)
        def _():
            dg_ref[...] = jnp.zeros_like(dg_ref)

        xv = x_ref[...]
        dy = dn_ref[...].astype(F32)
        r = lax.rsqrt(jnp.mean(xv * xv, axis=-1, keepdims=True) + EPS)
        xhat = xv * r
        dxh = dy * g_ref[...]
        dx = res_ref[...] + r * (dxh - xhat * jnp.mean(dxh * xhat, axis=-1, keepdims=True))
        dx_ref[...] = dx
        dxb_ref[...] = dx.astype(BF16)
        dg_ref[...] += jnp.sum(dy * xhat, axis=0, keepdims=True)

    tok = pl.BlockSpec((tt, D), lambda i: (i, 0))
    vec = pl.BlockSpec((1, D), lambda i: (0, 0))
    return pl.pallas_call(
        body, name=name, grid=(T // tt,),
        in_specs=[tok, tok, vec, tok],
        out_specs=[tok, tok, vec],
        out_shape=[jax.ShapeDtypeStruct((T, D), F32), jax.ShapeDtypeStruct((T, D), BF16),
                   jax.ShapeDtypeStruct((1, D), F32)],
        compiler_params=_cparams(("arbitrary",)),
    )(dn, x, g, res)


def _loss_and_final_bwd(h2, target, gf):
    T, D = h2.shape
    tt = _tile(T, 256)

    def body(h_ref, t_ref, g_ref, dh_ref, dhb_ref, dg_ref, loss_ref):
        @pl.when(pl.program_id(0) == 0)
        def _():
            dg_ref[...] = jnp.zeros_like(dg_ref)
            loss_ref[...] = jnp.zeros_like(loss_ref)

        xv = h_ref[...]
        r = lax.rsqrt(jnp.mean(xv * xv, axis=-1, keepdims=True) + EPS)
        xhat = xv * r
        err = xhat * g_ref[...] - t_ref[...]
        loss_ref[...] += 0.5 * jnp.sum(jnp.mean(err * err, axis=-1, keepdims=True), axis=0, keepdims=True)
        dy = err * (1.0 / D)
        dxh = dy * g_ref[...]
        dx = r * (dxh - xhat * jnp.mean(dxh * xhat, axis=-1, keepdims=True))
        dh_ref[...] = dx
        dhb_ref[...] = dx.astype(BF16)
        dg_ref[...] += jnp.sum(dy * xhat, axis=0, keepdims=True)

    tok = pl.BlockSpec((tt, D), lambda i: (i, 0))
    vec = pl.BlockSpec((1, D), lambda i: (0, 0))
    return pl.pallas_call(
        body, name="loss_final_bwd", grid=(T // tt,),
        in_specs=[tok, tok, vec],
        out_specs=[tok, tok, vec, pl.BlockSpec((1, 1), lambda i: (0, 0))],
        out_shape=[jax.ShapeDtypeStruct((T, D), F32), jax.ShapeDtypeStruct((T, D), BF16),
                   jax.ShapeDtypeStruct((1, D), F32), jax.ShapeDtypeStruct((1, 1), F32)],
        compiler_params=_cparams(("arbitrary",)),
    )(h2, target, gf)


def _gated_norm_fwd(y, proj, g):
    T, D = y.shape
    tt = _tile(T, 256)

    def body(y_ref, z_ref, g_ref, o_ref):
        yg = y_ref[...] * _silu(z_ref[...])
        r = lax.rsqrt(jnp.mean(yg * yg, axis=-1, keepdims=True) + EPS)
        o_ref[...] = (yg * r * g_ref[...]).astype(BF16)

    tok = pl.BlockSpec((tt, D), lambda i: (i, 0))
    return pl.pallas_call(
        body, name="gated_norm_fwd", grid=(T // tt,),
        in_specs=[tok, tok, pl.BlockSpec((1, D), lambda i: (0, 0))],
        out_specs=tok,
        out_shape=jax.ShapeDtypeStruct((T, 2 * D_MODEL), BF16),
        compiler_params=_cparams(("parallel",)),
    )(y, proj, g)


def _gated_norm_bwd(dmix, y, proj, g, dproj):
    T, D = y.shape
    tt = _tile(T, 256)

    def body(do_ref, y_ref, z_ref, g_ref, dp_ref, dy_ref, dz_ref, dg_ref):
        @pl.when(pl.program_id(0) == 0)
        def _():
            dg_ref[...] = jnp.zeros_like(dg_ref)

        yv, zv = y_ref[...], z_ref[...]
        do = do_ref[...].astype(F32)
        sz = _silu(zv)
        yg = yv * sz
        r = lax.rsqrt(jnp.mean(yg * yg, axis=-1, keepdims=True) + EPS)
        xhat = yg * r
        dxh = do * g_ref[...]
        dyg = r * (dxh - xhat * jnp.mean(dxh * xhat, axis=-1, keepdims=True))
        dy_ref[...] = dyg * sz
        dz_ref[...] = (dyg * yv * _dsilu(zv)).astype(BF16)
        dg_ref[...] += jnp.sum(do * xhat, axis=0, keepdims=True)

    tok = pl.BlockSpec((tt, D), lambda i: (i, 0))
    vec = pl.BlockSpec((1, D), lambda i: (0, 0))
    return pl.pallas_call(
        body, name="gated_norm_bwd", grid=(T // tt,),
        in_specs=[tok, tok, tok, vec, ANY],
        out_specs=[tok, tok, vec],
        out_shape=[jax.ShapeDtypeStruct((T, D), F32), jax.ShapeDtypeStruct(dproj.shape, BF16),
                   jax.ShapeDtypeStruct((1, D), F32)],
        input_output_aliases={4: 1},
        compiler_params=_cparams(("arbitrary",)),
    )(dmix, y, proj, g, dproj)


HALO = 8


def _shift_down(cur, prev8, s):
    ext = jnp.concatenate([prev8, cur], axis=0)
    return pltpu.roll(ext, s, axis=0)[HALO:]


def _shift_up(cur, next8, s):
    n = cur.shape[0]
    ext = jnp.concatenate([cur, next8], axis=0)
    return pltpu.roll(ext, n + HALO - s, axis=0)[:n]


def _conv_specs(tt, cb, col_off_blocks, nt):
    hb = tt // HALO
    cur = pl.BlockSpec((tt, cb), lambda j, i: (i, col_off_blocks + j))
    prev = pl.BlockSpec((HALO, cb), lambda j, i: (jnp.maximum(i * hb - 1, 0), col_off_blocks + j))
    nxt = pl.BlockSpec((HALO, cb), lambda j, i: (jnp.minimum((i + 1) * hb, nt * hb - 1), col_off_blocks + j))
    return cur, prev, nxt


def _taps(cur, prev8, K):
    return [_shift_down(cur, prev8, K - 1 - k) for k in range(K - 1)] + [cur]


def _conv_of_taps(taps, w):
    y = taps[-1] * w[len(taps) - 1:len(taps), :]
    for k, t in enumerate(taps[:-1]):
        y = y + t * w[k:k + 1, :]
    return y


def _causal_conv(cur, prev8, w, K):
    return _conv_of_taps(_taps(cur, prev8, K), w)


def _anticausal_conv(cur, next8, w, K):
    y = cur * w[K - 1:K, :]
    for k in range(K - 1):
        y = y + _shift_up(cur, next8, K - 1 - k) * w[k:k + 1, :]
    return y


def _ssm_conv_fwd(proj, w8, b):
    T = proj.shape[0]
    tt, cb = _tile(T, 512), 512
    nt = T // tt
    cur, prev, _ = _conv_specs(tt, cb, OFF_XBC // cb, nt)

    def body(u_ref, up_ref, w_ref, b_ref, o_ref):
        first = pl.program_id(1) == 0
        p8 = jnp.where(first, 0.0, up_ref[...])
        pre = _causal_conv(u_ref[...], p8, w_ref[...], K_SSM) + b_ref[...]
        o_ref[...] = _silu(pre)

    return pl.pallas_call(
        body, name="ssm_conv_fwd", grid=(D_XBC // cb, nt),
        in_specs=[cur, prev, pl.BlockSpec((8, cb), lambda j, i: (0, j)), pl.BlockSpec((1, cb), lambda j, i: (0, j))],
        out_specs=pl.BlockSpec((tt, cb), lambda j, i: (i, j)),
        out_shape=jax.ShapeDtypeStruct((T, D_XBC), F32),
        compiler_params=_cparams(("parallel", "parallel")),
    )(proj, proj, w8, b)


def _ssm_conv_bwd(dact, proj, w8, b, dproj):
    T = proj.shape[0]
    tt, cb = _tile(T, 512), 512
    nt = T // tt
    cur, prev, nxt = _conv_specs(tt, cb, OFF_XBC // cb, nt)
    dcur, dprev, dnxt = _conv_specs(tt, cb, 0, nt)

    def dpre_of(d, u, p8, w, bb):
        pre = _causal_conv(u, p8, w, K_SSM) + bb
        return d * _dsilu(pre)

    def body(d_ref, dn_ref, u_ref, up_ref, un_ref, w_ref, b_ref, dp_ref, dx_ref, dw_ref, db_ref):
        i = pl.program_id(1)

        @pl.when(i == 0)
        def _():
            dw_ref[...] = jnp.zeros_like(dw_ref)
            db_ref[...] = jnp.zeros_like(db_ref)

        w, bb = w_ref[...], b_ref[...]
        u = u_ref[...]
        p8 = jnp.where(i == 0, 0.0, up_ref[...])
        taps = _taps(u, p8, K_SSM)
        dpre = d_ref[...] * _dsilu(_conv_of_taps(taps, w) + bb)
        un = un_ref[...]
        dpre_n = dpre_of(dn_ref[...], un, u[tt - HALO:, :], w, bb)
        dpre_n = jnp.where(i == nt - 1, 0.0, dpre_n)
        dx_ref[...] = _anticausal_conv(dpre, dpre_n, w, K_SSM).astype(BF16)
        rows = [jnp.sum(dpre * t, axis=0, keepdims=True) for t in taps]
        rows.append(jnp.zeros((8 - K_SSM, cb), F32))
        dw_ref[...] += jnp.concatenate(rows, axis=0)
        db_ref[...] += jnp.sum(dpre, axis=0, keepdims=True)

    wspec = pl.BlockSpec((8, cb), lambda j, i: (0, j))
    bspec = pl.BlockSpec((1, cb), lambda j, i: (0, j))
    return pl.pallas_call(
        body, name="ssm_conv_bwd", grid=(D_XBC // cb, nt),
        in_specs=[dcur, dnxt, cur, prev, nxt, wspec, bspec, ANY],
        out_specs=[pl.BlockSpec((tt, cb), lambda j, i: (i, OFF_XBC // cb + j)), wspec, bspec],
        out_shape=[jax.ShapeDtypeStruct(dproj.shape, BF16), jax.ShapeDtypeStruct((8, D_XBC), F32),
                   jax.ShapeDtypeStruct((1, D_XBC), F32)],
        input_output_aliases={7: 0},
        compiler_params=_cparams(("parallel", "arbitrary")),
    )(dact, dact, proj, proj, proj, w8, b, dproj)


SCB = 512
SC3 = 3 * SCB


def _sc_specs(tt, nt):
    hb = tt // HALO
    cur = pl.BlockSpec((tt, SC3), lambda j, i: (i, OFF_CB // SC3 + j))
    prev = pl.BlockSpec((HALO, SC3), lambda j, i: (jnp.maximum(i * hb - 1, 0), OFF_CB // SC3 + j))
    nxt = pl.BlockSpec((HALO, SC3), lambda j, i: (jnp.minimum((i + 1) * hb, nt * hb - 1), OFF_CB // SC3 + j))
    return cur, prev, nxt


def _shortconv_fwd(proj, w8, ymix):
    T = proj.shape[0]
    tt = _tile(T, 512)
    nt = T // tt
    cur, prev, _ = _sc_specs(tt, nt)

    def body(p_ref, pp_ref, w_ref, y_ref, o_ref):
        p, pp = p_ref[...], pp_ref[...]
        v = p[:, SCB:2 * SCB] * p[:, 2 * SCB:]
        vp = jnp.where(pl.program_id(1) == 0, 0.0, pp[:, SCB:2 * SCB] * pp[:, 2 * SCB:])
        o_ref[...] = (p[:, :SCB] * _causal_conv(v, vp, w_ref[...], K_SC)).astype(BF16)

    return pl.pallas_call(
        body, name="shortconv_fwd", grid=(D_MODEL // SCB, nt),
        in_specs=[cur, prev, pl.BlockSpec((8, SCB), lambda j, i: (0, j)), ANY],
        out_specs=pl.BlockSpec((tt, SCB), lambda j, i: (i, D_SSM // SCB + j)),
        out_shape=jax.ShapeDtypeStruct(ymix.shape, BF16),
        input_output_aliases={3: 0},
        compiler_params=_cparams(("parallel", "parallel")),
    )(proj, proj, w8, ymix)


def _shortconv_bwd(dmix, proj, w8):
    T = proj.shape[0]
    tt = _tile(T, 512)
    nt = T // tt
    hb = tt // HALO
    cur, prev, nxt = _sc_specs(tt, nt)
    d_s = pl.BlockSpec((tt, SCB), lambda j, i: (i, D_SSM // SCB + j))
    dn_s = pl.BlockSpec((HALO, SCB), lambda j, i: (jnp.minimum((i + 1) * hb, nt * hb - 1), D_SSM // SCB + j))

    def body(d_ref, dn_ref, p_ref, pp_ref, pn_ref, w_ref, dp_ref, dw_ref):
        i = pl.program_id(1)

        @pl.when(i == 0)
        def _():
            dw_ref[...] = jnp.zeros_like(dw_ref)

        w = w_ref[...]
        p, pp = p_ref[...], pp_ref[...]
        gb, gc, u = p[:, :SCB], p[:, SCB:2 * SCB], p[:, 2 * SCB:]
        v = gc * u
        vp = jnp.where(i == 0, 0.0, pp[:, SCB:2 * SCB] * pp[:, 2 * SCB:])
        d = d_ref[...].astype(F32)
        taps = _taps(v, vp, K_SC)
        dp_ref[:, :SCB] = (d * _conv_of_taps(taps, w)).astype(BF16)
        dcv = d * gb
        dcv_n = jnp.where(i == nt - 1, 0.0, dn_ref[...].astype(F32) * pn_ref[:, :SCB])
        dv = _anticausal_conv(dcv, dcv_n, w, K_SC)
        dp_ref[:, SCB:2 * SCB] = (dv * u).astype(BF16)
        dp_ref[:, 2 * SCB:] = (dv * gc).astype(BF16)
        rows = [jnp.sum(dcv * t, axis=0, keepdims=True) for t in taps]
        rows.append(jnp.zeros((8 - K_SC, SCB), F32))
        dw_ref[...] += jnp.concatenate(rows, axis=0)

    wspec = pl.BlockSpec((8, SCB), lambda j, i: (0, j))
    return pl.pallas_call(
        body, name="shortconv_bwd", grid=(D_MODEL // SCB, nt),
        in_specs=[d_s, dn_s, cur, prev, nxt, wspec],
        out_specs=[cur, wspec],
        out_shape=[jax.ShapeDtypeStruct((T, D_MAIN), BF16), jax.ShapeDtypeStruct((8, D_MODEL), F32)],
        compiler_params=_cparams(("parallel", "arbitrary")),
    )(dmix, dmix, proj, proj, proj, w8)


GW = HEADS_PER_GROUP * HEADDIM


def _dot(a, b):
    return jnp.dot(a.astype(BF16), b.astype(BF16), preferred_element_type=F32)


def _dot_nt(a, b):
    return lax.dot_general(a.astype(BF16), b.astype(BF16), (((1,), (1,)), ((), ())), preferred_element_type=F32)


def _dot_tn(a, b):
    return lax.dot_general(a.astype(BF16), b.astype(BF16), (((0,), (0,)), ((), ())), preferred_element_type=F32)


def _bf16_terms(x, n):
    terms, r = [], x
    for _ in range(n):
        t = r.astype(BF16)
        terms.append(t)
        r = r - t.astype(F32)
    return terms


def _dot_sel(a, sel, n=2):
    s = sel.astype(BF16)
    return sum(jnp.dot(t, s, preferred_element_type=F32) for t in _bf16_terms(a, n))


def _sel_dot(sel, b, n=2):
    s = sel.astype(BF16)
    return sum(jnp.dot(s, t, preferred_element_type=F32) for t in _bf16_terms(b, n))


def _sel_dot_nt(sel, b, n=2):
    s = sel.astype(BF16)
    return sum(lax.dot_general(s, t, (((1,), (1,)), ((), ())), preferred_element_type=F32)
               for t in _bf16_terms(b, n))


def _head_cols(rows):
    parts = [jnp.broadcast_to(rows[r:r + 1, :], (HEADDIM, CHUNK)) for r in range(HEADS_PER_GROUP)]
    return jnp.concatenate(parts, axis=0).T


def _head_rows(rows):
    parts = [jnp.broadcast_to(rows[r:r + 1, :], (HEADDIM, N_STATE)) for r in range(HEADS_PER_GROUP)]
    return jnp.concatenate(parts, axis=0)


def _ssd_common(dtr, bias, alog):
    dt = _softplus(dtr + bias)
    A = -jnp.exp(alog)
    a = dt * A
    ki = lax.broadcasted_iota(jnp.int32, (CHUNK, CHUNK), 0)
    si = lax.broadcasted_iota(jnp.int32, (CHUNK, CHUNK), 1)
    upper = (ki <= si).astype(F32)
    cs = _dot_sel(a, upper, 3)
    cs_last = jnp.broadcast_to(cs[:, CHUNK - 1:CHUNK], (8, CHUNK))
    return dt, A, a, cs, cs_last


def _decay_matrix(cs, r):
    li = lax.broadcasted_iota(jnp.int32, (CHUNK, CHUNK), 0)
    si = lax.broadcasted_iota(jnp.int32, (CHUNK, CHUNK), 1)
    causal = li >= si
    R = jnp.broadcast_to(cs[r:r + 1, :], (CHUNK, CHUNK))
    seg = jnp.where(causal, R.T - R, 0.0)
    return jnp.where(causal, jnp.exp(seg), 0.0)


def _decay_cat(cs):
    return jnp.concatenate([_decay_matrix(cs, r) for r in range(HEADS_PER_GROUP)], axis=1)


def _lanes4(m):
    return jnp.concatenate([m] * HEADS_PER_GROUP, axis=1)


def _head_blocks(v):
    col = lax.broadcasted_iota(jnp.int32, v.shape, 1) // HEADDIM
    return jnp.concatenate([jnp.where(col == r, v, jnp.zeros_like(v)) for r in range(HEADS_PER_GROUP)], axis=0)


GXBC = GW + 2 * N_STATE


GS_FWD = 8
GS_BWD = 8


def _ssd_in_specs(nc, rev):
    GS = GS_BWD if rev else GS_FWD
    cix = (lambda c: nc - 1 - c) if rev else (lambda c: c)
    x_s = pl.BlockSpec((CHUNK, GS * GW), lambda g, c: (cix(c), g))
    xbc_s = pl.BlockSpec((CHUNK, GS * GXBC), lambda g, c: (cix(c), g))
    dtr_s = pl.BlockSpec((GS, 8, CHUNK), lambda g, c: (g, 0, cix(c)))
    row_s = pl.BlockSpec((GS, 8, CHUNK), lambda g, c: (g, 0, 0))
    drep_s = pl.BlockSpec((1, GS * GW), lambda g, c: (0, g))
    hs_s = pl.BlockSpec((1, GS * GW, N_STATE), lambda g, c: (cix(c), g, 0))
    return x_s, xbc_s, dtr_s, row_s, drep_s, hs_s


def _xbc_parts(xbc_ref, gi):
    o = gi * GXBC
    return xbc_ref[:, o:o + GW], xbc_ref[:, o + GW:o + GW + N_STATE], xbc_ref[:, o + GW + N_STATE:o + GXBC]


def _ssd_fwd(xbc, dtr, bias, alog, drep):
    T = xbc.shape[0]
    nc = T // CHUNK
    x_s, xbc_s, dtr_s, row_s, drep_s, hs_s = _ssd_in_specs(nc, False)

    def body(xbc_ref, dtr_ref, bias_ref, alog_ref, drep_ref, y_ref, hs_ref, h_scr):
        @pl.when(pl.program_id(1) == 0)
        def _():
            h_scr[...] = jnp.zeros_like(h_scr)

        for gi in range(GS_FWD):
            cols, rows = slice(gi * GW, (gi + 1) * GW), pl.ds(gi * GW, GW)
            x, Bm, Cm = _xbc_parts(xbc_ref, gi)
            dt, A, a, cs, cs_last = _ssd_common(dtr_ref[gi], bias_ref[gi], alog_ref[gi])
            E = _head_cols(jnp.exp(cs))
            W = _head_cols(jnp.exp(cs_last - cs) * dt)
            X = (x * _head_cols(dt)).astype(BF16)
            CB = _dot_nt(Cm, Bm)
            col = lax.broadcasted_iota(jnp.int32, (CHUNK, GW), 1) // HEADDIM
            y = jnp.zeros((CHUNK, GW), F32)
            for r in range(HEADS_PER_GROUP):
                y = y + jnp.where(col == r, _dot(CB * _decay_matrix(cs, r), X), 0.0)
            h = h_scr[rows, :]
            hs_ref[0, rows, :] = h
            y = y + _dot_nt(Cm, h) * E
            y_ref[:, cols] = y + drep_ref[:, cols] * x
            h_scr[rows, :] = h * _head_rows(jnp.exp(cs_last)) + _dot_tn(x * W, Bm)

    return pl.pallas_call(
        body, name="ssd_fwd", grid=(N_GROUPS // GS_FWD, nc),
        in_specs=[xbc_s, dtr_s, row_s, row_s, drep_s],
        out_specs=[x_s, hs_s],
        out_shape=[jax.ShapeDtypeStruct((T, D_SSM), F32), jax.ShapeDtypeStruct((nc, D_SSM, N_STATE), F32)],
        scratch_shapes=[pltpu.VMEM((GS_FWD * GW, N_STATE), F32)],
        compiler_params=_cparams(("parallel", "arbitrary")),
    )(xbc, dtr, bias, alog, drep)


def _ssd_bwd(xbc, dtr, bias, alog, drep, dy, hs):
    T = xbc.shape[0]
    nc = T // CHUNK
    x_s, xbc_s, dtr_s, row_s, drep_s, hs_s = _ssd_in_specs(nc, True)

    def body(xbc_ref, dtr_ref, bias_ref, alog_ref, drep_ref, dy_ref, hs_ref,
             dxbc_ref, ddtr_ref, dbias_ref, dalog_ref, dd_ref, dh_scr):
        @pl.when(pl.program_id(1) == 0)
        def _():
            dh_scr[...] = jnp.zeros_like(dh_scr)
            dbias_ref[...] = jnp.zeros_like(dbias_ref)
            dalog_ref[...] = jnp.zeros_like(dalog_ref)
            dd_ref[...] = jnp.zeros_like(dd_ref)

        for gi in range(GS_BWD):
            one_group(gi, xbc_ref, dtr_ref, bias_ref, alog_ref, drep_ref, dy_ref, hs_ref,
                      dxbc_ref, ddtr_ref, dbias_ref, dalog_ref, dd_ref, dh_scr)

    def one_group(gi, xbc_ref, dtr_ref, bias_ref, alog_ref, drep_ref, dy_ref, hs_ref,
                  dxbc_ref, ddtr_ref, dbias_ref, dalog_ref, dd_ref, dh_scr):
        cols, rows, o = slice(gi * GW, (gi + 1) * GW), pl.ds(gi * GW, GW), gi * GXBC
        x, Bm, Cm = _xbc_parts(xbc_ref, gi)
        dY = dy_ref[:, cols]
        dt, A, a, cs, cs_last = _ssd_common(dtr_ref[gi], bias_ref[gi], alog_ref[gi])
        E = _head_cols(jnp.exp(cs))
        DT = _head_cols(dt)
        Wd = _head_cols(jnp.exp(cs_last - cs))
        X = x * DT
        h = hs_ref[0, rows, :]
        dS = dh_scr[rows, :]
        CB = _dot_nt(Cm, Bm)
        rowid = lax.broadcasted_iota(jnp.int32, (8, CHUNK), 0)
        lane = lax.broadcasted_iota(jnp.int32, (8, CHUNK), 1)
        hsel = (lax.broadcasted_iota(jnp.int32, (8, GW), 1) // HEADDIM
                == lax.broadcasted_iota(jnp.int32, (8, GW), 0)).astype(F32)
        hsel_l = (lax.broadcasted_iota(jnp.int32, (8, HEADS_PER_GROUP * CHUNK), 1) // CHUNK
                  == lax.broadcasted_iota(jnp.int32, (8, HEADS_PER_GROUP * CHUNK), 0)).astype(F32)

        Lc, CBc = _decay_cat(cs), _lanes4(CB)
        Mc = CBc * Lc
        GLc = _dot_nt(dY, _head_blocks(X.astype(BF16))) * Lc
        Wc = GLc * CBc
        colsum = jnp.sum(Wc, axis=0, keepdims=True)
        dcs = _sel_dot_nt(hsel_l, Wc)
        dCB = jnp.zeros((CHUNK, CHUNK), F32)
        for r in range(HEADS_PER_GROUP):
            blk = slice(r * CHUNK, (r + 1) * CHUNK)
            dCB = dCB + GLc[:, blk]
            dcs = dcs - jnp.where(rowid == r, colsum[:, blk], 0.0)
        m_stack = jnp.concatenate([Mc[:, r * CHUNK:(r + 1) * CHUNK].astype(BF16) for r in range(HEADS_PER_GROUP)],
                                  axis=0)
        dX = lax.dot_general(m_stack, _head_blocks(dY.astype(BF16)), (((0,), (0,)), ((), ())),
                             preferred_element_type=F32)
        dC = _dot(dCB, Bm)
        dB = _dot_tn(dCB, Cm)
        T1 = _dot_nt(Bm, dS)
        dX = dX + T1 * Wd
        dB = dB + _dot(X * Wd, dS)
        pdec = _sel_dot_nt(hsel, X * T1 * Wd)
        dcs = dcs - pdec
        dlast = jnp.sum(pdec, axis=1, keepdims=True) \
            + jnp.exp(cs_last[:, 0:1]) * jnp.sum(_sel_dot(hsel, dS * h), axis=1, keepdims=True)
        dYE = dY * E
        dC = dC + _dot(dYE, h)
        yoff = _dot_nt(Cm, h) * E
        dcs = dcs + _sel_dot_nt(hsel, dY * yoff)
        dcs = dcs + jnp.where(lane == CHUNK - 1, dlast, 0.0)
        ki = lax.broadcasted_iota(jnp.int32, (CHUNK, CHUNK), 0)
        si = lax.broadcasted_iota(jnp.int32, (CHUNK, CHUNK), 1)
        lower = (ki >= si).astype(F32)
        da = _dot_sel(dcs, lower)
        ddt = da * A + _sel_dot_nt(hsel, dX * x)
        ddtr = ddt * _sigmoid(dtr_ref[gi] + bias_ref[gi])
        ddtr_ref[gi] = ddtr
        dbias_ref[gi] += ddtr
        dalog_ref[gi] += da * a
        dxbc_ref[:, o:o + GW] = dX * DT + drep_ref[:, cols] * dY
        dd_ref[:, cols] += jnp.sum(dY * x, axis=0, keepdims=True)
        dxbc_ref[:, o + GW:o + GW + N_STATE] = dB
        dxbc_ref[:, o + GW + N_STATE:o + GXBC] = dC
        dh_scr[rows, :] = dS * _head_rows(jnp.exp(cs_last)) + _dot_tn(dYE, Cm)

    return pl.pallas_call(
        body, name="ssd_bwd", grid=(N_GROUPS // GS_BWD, nc),
        in_specs=[xbc_s, dtr_s, row_s, row_s, drep_s, x_s, hs_s],
        out_specs=[xbc_s, dtr_s, row_s, row_s, drep_s],
        out_shape=[jax.ShapeDtypeStruct((T, D_XBC), F32),
                   jax.ShapeDtypeStruct((N_GROUPS, 8, T), F32),
                   jax.ShapeDtypeStruct((N_GROUPS, 8, CHUNK), F32),
                   jax.ShapeDtypeStruct((N_GROUPS, 8, CHUNK), F32),
                   jax.ShapeDtypeStruct((1, D_SSM), F32)],
        scratch_shapes=[pltpu.VMEM((GS_BWD * GW, N_STATE), F32)],
        compiler_params=_cparams(("parallel", "arbitrary")),
    )(xbc, dtr, bias, alog, drep, dy, hs)


def _adamw(w, g, m, v, name, deps=(), emit_g=False):
    R, C = w.shape
    tr = _tile(R, 256, 8)
    nd = len(deps)
    nout = 4 if emit_g else 3

    def body(w_ref, g_ref, m_ref, v_ref, *rest):
        outs = rest[nd:]
        gv = g_ref[...]
        mn = ADAM_B1 * m_ref[...] + (1.0 - ADAM_B1) * gv
        vn = ADAM_B2 * v_ref[...] + (1.0 - ADAM_B2) * (gv * gv)
        m_hat = mn / (1.0 - ADAM_B1 ** ADAM_STEP)
        v_hat = vn / (1.0 - ADAM_B2 ** ADAM_STEP)
        outs[0][...] = -ADAM_LR * (m_hat / (jnp.sqrt(v_hat) + ADAM_EPS) + ADAM_WD * w_ref[...])
        outs[1][...] = mn
        outs[2][...] = vn
        if emit_g:
            outs[3][...] = gv

    spec = pl.BlockSpec((tr, C), lambda i: (i, 0))
    return pl.pallas_call(
        body, name=name, grid=(R // tr,),
        in_specs=[spec] * 4 + [ANY] * nd, out_specs=[spec] * nout,
        out_shape=[jax.ShapeDtypeStruct((R, C), F32)] * nout,
        compiler_params=_cparams(("parallel",)),
    )(w, g, m, v, *deps)


ANY = pl.BlockSpec(memory_space=pl.ANY)


def _place():
    x, y, c = lax.axis_index("x"), lax.axis_index("y"), lax.axis_index("c")
    return x, y, c


def _other_chips(x, y):
    return [(1 - x, y), (x, 1 - y), (1 - x, 1 - y)]


def _allgather_inplace(bufs, splits, first_done=False):
    n = len(bufs)

    def body(*refs):
        o_refs = refs[n:2 * n]
        send_sems, recv_sems = refs[2 * n:]
        x, y, c = _place()
        xn, yn, dg, sibling = (1 - x, y), (x, 1 - y), (1 - x, 1 - y), (x, y, 1 - c)

        def blk(k, chip, pc):
            return o_refs[k].at[4 * chip[0] + 2 * chip[1] + pc]

        def part(k, ref, p):
            kind, s = splits[k]
            _, R, C = bufs[k].shape
            if kind == "rows":
                return ref.at[pl.ds(0, s)] if p == 0 else ref.at[pl.ds(s, R - s)]
            return ref.at[:, pl.ds(0, s)] if p == 0 else ref.at[:, pl.ds(s, C - s)]

        def copy(k, slot, ref, to):
            return pltpu.make_async_remote_copy(
                src_ref=ref, dst_ref=ref, send_sem=send_sems.at[k, slot], recv_sem=recv_sems.at[k, slot],
                device_id=to, device_id_type=MESH)

        sent = []

        def send(k, slot, ref, to):
            cp = copy(k, slot, ref, to)
            cp.start()
            sent.append(cp)

        if not first_done:
            for k in range(n):
                send(k, 0, blk(k, (x, y), c), (*xn, c))
                send(k, 1, blk(k, (x, y), c), (*yn, c))
        for k in range(n):
            bx, by = blk(k, xn, c), blk(k, yn, c)
            if not first_done:
                copy(k, 0, bx, sibling).wait_recv()
            send(k, 2, part(k, bx, 0), (*yn, c))
            send(k, 4, bx, sibling)
            if not first_done:
                copy(k, 1, by, sibling).wait_recv()
            send(k, 3, part(k, by, 1), (*xn, c))
            send(k, 5, by, sibling)
        for k in range(n):
            d0, d1 = part(k, blk(k, dg, c), 0), part(k, blk(k, dg, c), 1)
            copy(k, 2, d0, sibling).wait_recv()
            send(k, 6, d0, sibling)
            copy(k, 3, d1, sibling).wait_recv()
            send(k, 7, d1, sibling)
        for k in range(n):
            copy(k, 4, blk(k, xn, 1 - c), sibling).wait_recv()
            copy(k, 5, blk(k, yn, 1 - c), sibling).wait_recv()
            copy(k, 6, part(k, blk(k, dg, 1 - c), 0), sibling).wait_recv()
            copy(k, 7, part(k, blk(k, dg, 1 - c), 1), sibling).wait_recv()
        for cp in sent:
            cp.wait_send()

    return pl.pallas_call(
        body, name="allgather_w_in",
        in_specs=[ANY] * n, out_specs=[ANY] * n,
        out_shape=[jax.ShapeDtypeStruct(b.shape, b.dtype) for b in bufs],
        input_output_aliases={k: k for k in range(n)},
        scratch_shapes=[pltpu.SemaphoreType.DMA((n, 8)), pltpu.SemaphoreType.DMA((n, 8))],
    )(*bufs)


HBM = pl.BlockSpec(memory_space=pltpu.HBM)
SEM = pl.BlockSpec(memory_space=pltpu.SEMAPHORE)
EFFECT = pltpu.SideEffectType.DATAFLOW_SIDE_EFFECTING


def _split_start(name, arrays, build, n_copies, after=()):
    na, nd = len(arrays), len(after)

    def body(*refs):
        send_sems, recv_sems = refs[na + nd], refs[na + nd + 1]
        for cp in build(refs[:na], send_sems, recv_sems):
            cp.start()
        refs[-1][...] = jnp.zeros((8, 128), F32)

    outs = pl.pallas_call(
        body, name=name,
        out_shape=(pltpu.SemaphoreType.DMA((n_copies,)), pltpu.SemaphoreType.DMA((n_copies,)),
                   *[pltpu.HBM(a.shape, a.dtype) for a in arrays], jax.ShapeDtypeStruct((8, 128), F32)),
        in_specs=[HBM] * na + [ANY] * nd,
        out_specs=(SEM, SEM, *[HBM] * na, pl.BlockSpec(memory_space=pltpu.VMEM)),
        input_output_aliases={i: 2 + i for i in range(na)},
        compiler_params=pltpu.CompilerParams(has_side_effects=EFFECT),
    )(*[pltpu.with_memory_space_constraint(a, pltpu.HBM) for a in arrays], *after)
    return outs[0], outs[1], list(outs[2:2 + na]), outs[-1]


def _split_wait(name, send_sems, recv_sems, arrays, build, after):
    na = len(arrays)

    def body(*refs):
        for cp in build(refs[:na], refs[na], refs[na + 1]):
            cp.wait_send()
            cp.wait_recv()

    outs = pl.pallas_call(
        body, name=name,
        out_shape=tuple(pltpu.HBM(a.shape, a.dtype) for a in arrays),
        in_specs=[HBM] * na + [SEM, SEM] + [ANY] * len(after),
        out_specs=tuple([HBM] * na),
        input_output_aliases={i: i for i in range(na)},
        compiler_params=pltpu.CompilerParams(has_side_effects=EFFECT),
    )(*arrays, send_sems, recv_sems, *after)
    return list(outs)


def _remote(src, dst, send_sems, recv_sems, i, to):
    return pltpu.make_async_remote_copy(src_ref=src, dst_ref=dst, send_sem=send_sems.at[i], recv_sem=recv_sems.at[i],
                                        device_id=to, device_id_type=MESH)


def _build_ag_first(refs, ss, rs):
    x, y, c = _place()
    cps = []
    for k, ref in enumerate(refs):
        blk = ref.at[4 * x + 2 * y + c]
        cps += [_remote(blk, blk, ss, rs, 2 * k, (1 - x, y, c)), _remote(blk, blk, ss, rs, 2 * k + 1, (x, 1 - y, c))]
    return cps


def _build_ag_ici(refs, ss, rs):
    x, y, c = _place()
    cps = []
    for k, ref in enumerate(refs):
        blk = ref.at[4 * x + 2 * y + c]
        for j, (px, py) in enumerate(_other_chips(x, y)):
            cps.append(_remote(blk, blk, ss, rs, 3 * k + j, (px, py, c)))
    return cps


def _build_ag_fwd(refs, ss, rs):
    x, y, c = _place()
    cps = []
    for k, ref in enumerate(refs):
        for j, (px, py) in enumerate(_other_chips(x, y)):
            blk = ref.at[4 * px + 2 * py + c]
            cps.append(_remote(blk, blk, ss, rs, 3 * k + j, (x, y, 1 - c)))
    return cps


def _build_rs_swap(refs, ss, rs):
    x, y, c = _place()
    n = len(refs) // 2
    return [_remote(refs[k].at[:, pl.ds(1 - c, 1)], refs[n + k], ss, rs, k, (x, y, 1 - c)) for k in range(n)]


def _build_rs_ici(refs, ss, rs):
    x, y, c = _place()
    n = len(refs) // 2
    me = 2 * x + y
    cps = []
    for k in range(n):
        for j, (px, py) in enumerate(_other_chips(x, y)):
            cps.append(_remote(refs[k].at[2 * px + py], refs[n + k].at[me], ss, rs, 3 * k + j, (px, py, c)))
    return cps


def _build_rs_share(refs, ss, rs):
    x, y, c = _place()
    return [_remote(ref.at[c], ref.at[c], ss, rs, k, (x, y, 1 - c)) for k, ref in enumerate(refs)]


def _build_small_gather(refs, ss, rs):
    x, y, c = _place()
    me = 4 * x + 2 * y + c
    cps = []
    for d in range(1, N_DEV):
        to = (1 - x if d & 4 else x, 1 - y if d & 2 else y, 1 - c if d & 1 else c)
        cps.append(_remote(refs[0], refs[1].at[me], ss, rs, d - 1, to))
    return cps


def _sum_gathered(mine, landed, me_arr):
    R, C = mine.shape

    def body(me_ref, m_ref, l_ref, o_ref):
        me = me_ref[0]
        s = None
        for d in range(N_DEV):
            t = jnp.where(me == d, m_ref[...], l_ref[d])
            s = t if s is None else s + t
        o_ref[...] = s

    grid_spec = pltpu.PrefetchScalarGridSpec(
        num_scalar_prefetch=1, grid=(1,),
        in_specs=[pl.BlockSpec((R, C), lambda i, me_ref: (0, 0)),
                  pl.BlockSpec((N_DEV, R, C), lambda i, me_ref: (0, 0, 0))],
        out_specs=pl.BlockSpec((R, C), lambda i, me_ref: (0, 0)))
    return pl.pallas_call(
        body, name="sum_small", grid_spec=grid_spec,
        out_shape=jax.ShapeDtypeStruct((R, C), F32),
        compiler_params=_cparams(("arbitrary",)),
    )(me_arr, mine, landed)


def _rs_add_pair(p, r0, c_arr, name):
    _, _, hr, cols = p.shape
    tr = _tile(hr, 256, 8)

    def body(c_ref, p_ref, r_ref, q_ref):
        q_ref[...] = (p_ref[0].astype(F32) + r_ref[0].astype(F32)).astype(BF16)

    grid_spec = pltpu.PrefetchScalarGridSpec(
        num_scalar_prefetch=1, grid=(N_CHIPS, hr // tr),
        in_specs=[pl.BlockSpec((1, 1, tr, cols), lambda j, i, c_ref: (j, c_ref[0], i, 0)),
                  pl.BlockSpec((1, 1, tr, cols), lambda j, i, c_ref: (j, 0, i, 0))],
        out_specs=pl.BlockSpec((1, tr, cols), lambda j, i, c_ref: (j, i, 0)))
    return pl.pallas_call(
        body, name=name, grid_spec=grid_spec,
        out_shape=jax.ShapeDtypeStruct((N_CHIPS, hr, cols), BF16),
        compiler_params=_cparams(("parallel", "parallel")),
    )(c_arr, p, r0)


def _rs_add_chips(r1, q, place_arr, name):
    _, hr, cols = r1.shape
    tr = _tile(hr, 256, 8)

    def body(place_ref, r_ref, q_ref, o_ref):
        chip = place_ref[0]
        s = None
        for j in range(N_CHIPS):
            t = jnp.where(chip == j, q_ref[j], r_ref[j]).astype(F32)
            s = t if s is None else s + t
        o_ref[...] = s

    blk = pl.BlockSpec((N_CHIPS, tr, cols), lambda i, place_ref: (0, i, 0))
    grid_spec = pltpu.PrefetchScalarGridSpec(
        num_scalar_prefetch=1, grid=(hr // tr,), in_specs=[blk, blk],
        out_specs=pl.BlockSpec((None, tr, cols), lambda i, place_ref: (place_ref[1], i, 0)))
    return pl.pallas_call(
        body, name=name, grid_spec=grid_spec,
        out_shape=jax.ShapeDtypeStruct((2, hr, cols), F32),
        compiler_params=_cparams(("parallel",)),
    )(place_arr, r1, q)


def _pad_rows(a, rows):
    return jnp.pad(a, ((0, rows - a.shape[0]), (0, 0)))


def _pad_cols(a, cols):
    return jnp.pad(a, ((0, 0), (0, cols - a.shape[1])))


def _heads_to_rows(v):
    v = v.reshape(N_GROUPS, HEADS_PER_GROUP, 1)
    v = jnp.pad(v, ((0, 0), (0, 8 - HEADS_PER_GROUP), (0, 0)))
    return jnp.broadcast_to(v, (N_GROUPS, 8, CHUNK))


def _rows_to_heads(a):
    return jnp.sum(a[:, :HEADS_PER_GROUP, :], axis=-1).reshape(N_HEADS)


def _to_kernel_rows(a):
    C = a.shape[1]
    x0, b0, c0, s0 = D_SSM, 2 * D_SSM, 2 * D_SSM + 1024, D_SSM + D_XBC + N_HEADS
    xbc = jnp.concatenate([a[x0:b0].reshape(N_GROUPS, GW, C), a[b0:c0].reshape(N_GROUPS, N_STATE, C),
                           a[c0:c0 + 1024].reshape(N_GROUPS, N_STATE, C)], axis=1).reshape(D_XBC, C)
    sc = jnp.concatenate([a[s0 + k * D_MODEL:s0 + (k + 1) * D_MODEL].reshape(D_MODEL // SCB, SCB, C)
                          for k in range(3)], axis=1).reshape(3 * D_MODEL, C)
    return jnp.concatenate([a[:D_SSM], xbc, sc], axis=0)


HR_IN = 1568


def _shard_row_plan():
    segs = [(0, 0, 0, D_SSM)]
    for g in range(N_GROUPS):
        k0 = D_SSM + g * GXBC
        segs += [(0, k0, D_SSM + g * GW, GW), (0, k0 + GW, 2 * D_SSM + g * N_STATE, N_STATE),
                 (0, k0 + GW + N_STATE, 2 * D_SSM + 1024 + g * N_STATE, N_STATE)]
    segs.append((1, 0, D_SSM + D_XBC, N_HEADS))
    for j in range(D_MODEL // SCB):
        for k in range(3):
            segs.append((0, D_SSM + D_XBC + j * SC3 + k * SCB, D_SSM + D_XBC + N_HEADS + k * D_MODEL + j * SCB, SCB))
    cs = D_IN // N_CHIPS
    plan = []
    for src, s, o, n in segs:
        while n > 0:
            chip, loc = divmod(o, cs)
            half, row = divmod(loc, HR_IN)
            m = min(n, cs - loc, HR_IN - row)
            plan.append((src, s, chip, half, row, m))
            s, o, n = s + m, o + m, n - m
    return plan


SCATTER_ROWS = 512
SCATTER_SLOTS = 4


def _scatter_rows_to_shards(k_main, k_dt):
    C = k_main.shape[1]
    pieces = []
    for src, s, chip, half, row, n in _shard_row_plan():
        for o in range(0, n, SCATTER_ROWS):
            pieces.append((src, s + o, chip, half, row + o, min(SCATTER_ROWS, n - o)))
    S, lag, N = SCATTER_SLOTS, SCATTER_SLOTS // 2, len(pieces)

    def body(m_ref, d_ref, o_ref, buf, in_sems, out_sems):
        def cin(i):
            src, s, _, _, _, n = pieces[i]
            return pltpu.make_async_copy((d_ref if src else m_ref).at[pl.ds(s, n)],
                                         buf.at[i % S, pl.ds(0, n)], in_sems.at[i % S])

        def cout(i):
            _, _, chip, half, row, n = pieces[i]
            return pltpu.make_async_copy(buf.at[i % S, pl.ds(0, n)],
                                         o_ref.at[chip, half, pl.ds(row, n)], out_sems.at[i % S])

        for i in range(N + lag):
            if i < N:
                if i >= S:
                    cout(i - S).wait()
                cin(i).start()
            j = i - lag
            if 0 <= j < N:
                cin(j).wait()
                cout(j).start()
        for j in range(max(0, N - S), N):
            cout(j).wait()

    return pl.pallas_call(
        body, name="scatter_dw_in_rows", in_specs=[ANY, ANY], out_specs=ANY,
        out_shape=jax.ShapeDtypeStruct((N_CHIPS, 2, HR_IN, C), k_main.dtype),
        scratch_shapes=[pltpu.VMEM((S, SCATTER_ROWS, C), k_main.dtype),
                        pltpu.SemaphoreType.DMA((S,)), pltpu.SemaphoreType.DMA((S,))],
        compiler_params=_cparams(),
    )(k_main, k_dt)


def _to_kernel_xbc(a):
    R = a.shape[0]
    return jnp.concatenate([a[:, :D_SSM].reshape(R, N_GROUPS, GW), a[:, D_SSM:D_SSM + 1024].reshape(R, N_GROUPS, N_STATE),
                            a[:, D_SSM + 1024:].reshape(R, N_GROUPS, N_STATE)], axis=2).reshape(R, D_XBC)


def _from_kernel_xbc(a):
    R = a.shape[0]
    g = a.reshape(R, N_GROUPS, GXBC)
    return jnp.concatenate([g[:, :, :GW].reshape(R, D_SSM), g[:, :, GW:GW + N_STATE].reshape(R, 1024),
                            g[:, :, GW + N_STATE:].reshape(R, 1024)], axis=1)


def kernel(x, norm_mix_g, w_in, ssm_conv_w, ssm_conv_b, ssm_dt_bias, ssm_A_log, ssm_D, ssm_norm_g, sc_conv_w, w_out, norm_ffn_g, w_gate, w_up, w_down, norm_final_g, loss_target, m_norm_mix_g, m_w_in, m_ssm_conv_w, m_ssm_conv_b, m_ssm_dt_bias, m_ssm_A_log, m_ssm_D, m_ssm_norm_g, m_sc_conv_w, m_w_out, m_norm_ffn_g, m_w_gate, m_w_up, m_w_down, m_norm_final_g, v_norm_mix_g, v_w_in, v_ssm_conv_w, v_ssm_conv_b, v_ssm_dt_bias, v_ssm_A_log, v_ssm_D, v_ssm_norm_g, v_sc_conv_w, v_w_out, v_norm_ffn_g, v_w_gate, v_w_up, v_w_down, v_norm_final_g):
    T = x.shape[1]
    xt = x[0]
    tgt = loss_target[0]
    cx, cy, cc = lax.axis_index("x"), lax.axis_index("y"), lax.axis_index("c")
    chip = 2 * cx + cy
    c_arr = jnp.reshape(cc, (1,)).astype(jnp.int32)
    chip_arr = jnp.reshape(chip, (1,)).astype(jnp.int32)
    place_arr = jnp.stack([chip, cc]).astype(jnp.int32)

    big = [w_in[0].T, w_out[0], w_gate[0], w_up[0], w_down[0]]
    names = ["w_in", "w_out", "w_gate", "w_up", "w_down"]
    gb_in = _cast_into_gather(big[0], chip_arr, "cast_w_in", split_cols=True)
    cs_in, cs_conv = D_IN // N_CHIPS, D_XBC // N_CHIPS
    cw = jnp.stack([_pad_rows(ssm_conv_w[0], 8), _pad_cols(_pad_rows(sc_conv_w[0], 8), cs_conv)])
    cw_buf = lax.dynamic_update_slice(jnp.zeros((N_DEV, 8, cs_conv), F32), cw, (2 * chip, 0, 0))
    f_ss, f_rs, f_arr, f_tok = _split_start("ag_in_first_start", [gb_in, cw_buf], _build_ag_first, 4)
    gbufs = [None] + [_cast_into_gather(w, chip_arr, "cast_" + nm, deps=[f_tok]) for w, nm in zip(big[1:], names[1:])]
    n1 = _rmsnorm_fwd(xt, _tie(norm_mix_g, f_tok, "tie_ag_first"), "rmsnorm_mix")
    f_arr = _split_wait("ag_in_first_wait", f_ss, f_rs, f_arr, _build_ag_first, after=gbufs[1:] + [n1])
    g_in, cw_all = _allgather_inplace(f_arr, [("rows", (cs_in // 32) * 16), ("cols", cs_conv // 2)], first_done=True)
    cw_all = cw_all.reshape(N_CHIPS, 2, 8, cs_conv)
    ssm_w8 = _to_kernel_xbc(cw_all[:, 0].transpose(1, 0, 2).reshape(8, D_XBC))
    sc_w8 = cw_all[:, 1, :, :D_MODEL // N_CHIPS].transpose(1, 0, 2).reshape(8, D_MODEL)
    ssm_bk = _to_kernel_xbc(ssm_conv_b)
    wt = g_in.reshape(N_CHIPS, 2, cs_in, D_MODEL // 2).transpose(0, 2, 1, 3).reshape(D_IN, D_MODEL)
    wt_main = _to_kernel_rows(wt)
    wt_dt = _pad_rows(wt[D_SSM + D_XBC:D_SSM + D_XBC + N_HEADS], DT_PAD)
    ag_ss, ag_rs, ag_bufs, ag_tok = _split_start("ag_ici_start", gbufs[1:], _build_ag_ici, 12, after=[g_in, cw_all])

    bias_rows = _heads_to_rows(ssm_dt_bias[0])
    alog_rows = _heads_to_rows(ssm_A_log[0])
    drep = jnp.repeat(ssm_D[0], HEADDIM).reshape(1, D_SSM)

    (proj,) = _matmul([(n1, wt_main)], tb=True, out_dtypes=[F32], name="mm_proj", deps=[ag_tok])
    (dt_raw,) = _matmul([(n1, wt_dt)], tb=True, out_dtypes=[F32], name="mm_proj_dt")
    xbc = _ssm_conv_fwd(proj, ssm_w8, ssm_bk)
    dtr = jnp.pad(dt_raw[:, :N_HEADS].T.reshape(N_GROUPS, HEADS_PER_GROUP, T), ((0, 0), (0, 4), (0, 0)))
    y_ssd, hs = _ssd_fwd(xbc, dtr, bias_rows, alog_rows, drep)
    ag_bufs = _split_wait("ag_ici_wait", ag_ss, ag_rs, ag_bufs, _build_ag_ici, after=[y_ssd])
    fw_ss, fw_rs, fw_bufs, fw_tok = _split_start("ag_fwd_start", ag_bufs, _build_ag_fwd, 12)
    y_mix = _shortconv_fwd(proj, sc_w8, _gated_norm_fwd(y_ssd, proj, _tie(ssm_norm_g, fw_tok, "tie_ag_fwd")))
    gath = _split_wait("ag_fwd_wait", fw_ss, fw_rs, fw_bufs, _build_ag_fwd, after=[y_mix])
    w_out_f = gath[0].reshape(2 * D_MODEL, D_MODEL)
    w_gate3 = gath[1].reshape(N_CHIPS, D_MODEL, D_FF // N_CHIPS)
    w_up3 = gath[2].reshape(N_CHIPS, D_MODEL, D_FF // N_CHIPS)
    w_down_f = gath[3].reshape(D_FF, D_MODEL)
    (h1,) = _matmul([(y_mix, w_out_f)], out_dtypes=[F32], name="mm_out", extras=[xt],
                    epilogue=lambda acc, res: (acc + res,))
    n2 = _rmsnorm_fwd(h1, norm_ffn_g, "rmsnorm_ffn")
    g_act, u_act, a_act = _ffn_fwd(n2, w_gate3, w_up3)
    (h2,) = _matmul([(a_act, w_down_f)], out_dtypes=[F32], name="mm_down", extras=[h1],
                    epilogue=lambda acc, res: (acc + res,))

    dh2, dh2b, dg_final, loss_part = _loss_and_final_bwd(h2, tgt, norm_final_g.reshape(1, D_MODEL))
    dg_act, du_act = _matmul([(dh2b, w_down_f)], tb=True, out_dtypes=[BF16, BF16], name="mm_down_bwd",
                             tn=512, extras=[g_act, u_act], epilogue=_swiglu_bwd, nsub=2)
    (dw_down,) = _matmul([(a_act, dh2b)], ta=True, out_dtypes=[BF16], name="mm_dw_down", tm=1408, tn=512)
    (dn2,) = _matmul([(dg_act, w_gate3), (du_act, w_up3)], tb=True, b3d=True, out_dtypes=[BF16],
                     name="mm_ffn_in_bwd")
    (dw_gate,) = _matmul([(n2, dg_act)], ta=True, out_dtypes=[BF16], name="mm_dw_gate", tm=512, tn=1408,
                         col_shards=True)
    (dw_up,) = _matmul([(n2, du_act)], ta=True, out_dtypes=[BF16], name="mm_dw_up", tm=512, tn=1408,
                       col_shards=True)
    dh1, dh1b, dg_ffn = _rmsnorm_bwd(dn2, h1, norm_ffn_g, dh2, "rmsnorm_ffn_bwd")
    (dw_out,) = _matmul([(y_mix, dh1b)], ta=True, out_dtypes=[BF16], name="mm_dw_out")

    def halves(g):
        return g.reshape(N_CHIPS, 2, g.shape[1] // 2, g.shape[2])

    def landing(shape, dtype):
        return lax.empty(shape, dtype)

    names1 = names[1:]
    ps1 = [halves(dw_out.reshape(N_CHIPS, -1, D_MODEL)), halves(dw_gate), halves(dw_up),
           halves(dw_down.reshape(N_CHIPS, -1, D_MODEL))]
    r0_1 = [landing((N_CHIPS, 1) + p.shape[2:], p.dtype) for p in ps1]
    sw_ss, sw_rs, sw_arr, sw_tok = _split_start("rs1_swap_start", ps1 + r0_1, _build_rs_swap, 4)
    (dmix,) = _matmul([(dh1b, w_out_f)], tb=True, out_dtypes=[BF16], name="mm_out_bwd", deps=[sw_tok])
    dproj, dw_sc = _shortconv_bwd(dmix, proj, sc_w8)
    dy_ssd, dproj, dg_ssmnorm = _gated_norm_bwd(dmix, y_ssd, proj, ssm_norm_g, dproj)
    sw_arr = _split_wait("rs1_swap_wait", sw_ss, sw_rs, sw_arr, _build_rs_swap, after=[dy_ssd])
    qs1 = [_rs_add_pair(p, r, c_arr, "rs_add_pair_" + nm) for p, r, nm in zip(sw_arr[:4], sw_arr[4:], names1)]
    r1_1 = [landing(q.shape, BF16) for q in qs1]
    ic_ss, ic_rs, ic_arr, ic_tok = _split_start("rs1_ici_start", qs1 + r1_1, _build_rs_ici, 12)
    dxbc_act, ddtr, dbias_acc, dalog_acc, dD_acc = _ssd_bwd(
        xbc, dtr, bias_rows, alog_rows, _tie(drep, ic_tok, "tie_rs1_ici"), dy_ssd, hs)
    dproj, dw_ssmconv, db_ssmconv = _ssm_conv_bwd(dxbc_act, proj, ssm_w8, ssm_bk, dproj)
    dw_ssmconv, db_ssmconv = _from_kernel_xbc(dw_ssmconv), _from_kernel_xbc(db_ssmconv)
    ic_arr = _split_wait("rs1_ici_wait", ic_ss, ic_rs, ic_arr, _build_rs_ici, after=[dproj])
    g1 = [_rs_add_chips(r, q, place_arr, "rs_add_chips_" + nm) for q, r, nm in zip(ic_arr[:4], ic_arr[4:], names1)]
    sh_ss, sh_rs, sh_arr, sh_tok = _split_start("rs1_share_start", g1, _build_rs_share, 4)

    ddt_raw = _pad_cols(ddtr[:, :HEADS_PER_GROUP, :].reshape(N_HEADS, T).T, DT_PAD).astype(BF16)
    (dwt_main,) = _matmul([(dproj, n1)], ta=True, out_dtypes=[F32], name="mm_dw_main", deps=[sh_tok])
    (dwt_dt,) = _matmul([(ddt_raw, n1)], ta=True, out_dtypes=[F32], name="mm_dw_dt")
    p_in = _scatter_rows_to_shards(dwt_main, dwt_dt)
    s2_ss, s2_rs, s2_arr, s2_tok = _split_start(
        "rs2_swap_start", [p_in, landing((N_CHIPS, 1) + p_in.shape[2:], F32)], _build_rs_swap, 1)
    tm_pb = 1024
    mt = T // _tile(T, tm_pb)
    mt_a = max(mt // 4, 1)
    (dn1a,) = _matmul([(dproj, wt_main)], out_dtypes=[F32], name="mm_proj_bwd_a", deps=[s2_tok], tm=tm_pb,
                      m_tiles=(0, mt_a))
    g1 = _split_wait("rs1_share_wait", sh_ss, sh_rs, sh_arr, _build_rs_share, after=[dn1a])
    s2_arr = _split_wait("rs2_swap_wait", s2_ss, s2_rs, s2_arr, _build_rs_swap, after=[dn1a])
    q_in = _rs_add_pair(s2_arr[0], s2_arr[1], c_arr, "rs_add_pair_w_in")
    i2_ss, i2_rs, i2_arr, i2_tok = _split_start(
        "rs2_ici_start", [q_in, landing(q_in.shape, BF16)], _build_rs_ici, 3)
    if mt > mt_a:
        (dn1a,) = _matmul([(dproj, wt_main)], out_dtypes=[F32], name="mm_proj_bwd_b", deps=[i2_tok], tm=tm_pb,
                          m_tiles=(mt_a, mt - mt_a), out_buf=dn1a)
    (dn1,) = _matmul([(ddt_raw, wt_dt)], out_dtypes=[BF16], name="mm_proj_dt_bwd", extras=[dn1a],
                     epilogue=lambda acc, res: (acc + res,), deps=[i2_tok])
    dx, _, dg_mix = _rmsnorm_bwd(dn1, xt, norm_mix_g, dh1, "rmsnorm_mix_bwd")

    big_m = [m_w_in[0].T, m_w_out[0], m_w_gate[0], m_w_up[0], m_w_down[0]]
    big_v = [v_w_in[0].T, v_w_out[0], v_w_gate[0], v_w_up[0], v_w_down[0]]
    big_grads = [None] + [g.reshape(w.shape) for g, w in zip(g1, big[1:])]
    big_out = {}
    for k in range(1, 5):
        *big_out[names[k]], big_grads[k] = _adamw(big[k], big_grads[k], big_m[k], big_v[k], "adamw_" + names[k],
                                                   deps=[i2_tok], emit_g=True)
    i2_arr = _split_wait("rs2_ici_wait", i2_ss, i2_rs, i2_arr, _build_rs_ici, after=[big_out[names[4]][0], dx])
    g_in_red = _rs_add_chips(i2_arr[1], i2_arr[0], place_arr, "rs_add_chips_w_in")
    s3_ss, s3_rs, s3_arr, s3_tok = _split_start("rs2_share_start", [g_in_red], _build_rs_share, 1)

    dD = jnp.sum(dD_acc.reshape(N_HEADS, HEADDIM), axis=-1)
    heads_row = jnp.concatenate([_rows_to_heads(dbias_acc), _rows_to_heads(dalog_acc), dD,
                                 loss_part.reshape(1)]).reshape(1, -1)
    small = jnp.concatenate([
        dw_ssmconv,
        _pad_cols(dw_sc, D_XBC),
        db_ssmconv,
        jnp.concatenate([dg_mix, dg_ssmnorm], axis=1),
        jnp.concatenate([dg_ffn, dg_final], axis=1),
        _pad_cols(heads_row, D_XBC),
        jnp.zeros((4, D_XBC), F32),
    ], axis=0)
    sm_ss, sm_rs, sm_arr, sm_tok = _split_start(
        "small_gather_start", [small, landing((N_DEV,) + small.shape, F32)], _build_small_gather, N_DEV - 1,
        after=[s3_tok])
    (g_in_full,) = _split_wait("rs2_share_wait", s3_ss, s3_rs, s3_arr, _build_rs_share, after=[sm_tok])
    d_t, m_t, v_t, g_t = _adamw(big[0], g_in_full.reshape(2 * HR_IN, D_MODEL), big_m[0], big_v[0],
                                "adamw_" + names[0], emit_g=True)
    big_grads[0] = g_t.T
    big_out[names[0]] = (d_t.T, m_t.T, v_t.T)
    sm_arr = _split_wait("small_gather_wait", sm_ss, sm_rs, sm_arr, _build_small_gather, after=[d_t])
    tot = _sum_gathered(sm_arr[0], sm_arr[1], jnp.reshape(4 * cx + 2 * cy + cc, (1,)).astype(jnp.int32))
    loss = tot[19, 3 * N_HEADS]

    cs_ssm, cs_sc = D_XBC // N_CHIPS, D_MODEL // N_CHIPS
    g_ssm_conv = lax.dynamic_slice(tot[0:K_SSM], (0, chip * cs_ssm), (K_SSM, cs_ssm))
    g_sc_conv = lax.dynamic_slice(tot[8:8 + K_SC, :D_MODEL], (0, chip * cs_sc), (K_SC, cs_sc))
    small_grads = {
        "norm_mix_g": tot[17:18, :D_MODEL], "ssm_conv_w": g_ssm_conv, "ssm_conv_b": tot[16:17],
        "ssm_dt_bias": tot[19:20, 0:N_HEADS], "ssm_A_log": tot[19:20, N_HEADS:2 * N_HEADS],
        "ssm_D": tot[19:20, 2 * N_HEADS:3 * N_HEADS], "ssm_norm_g": tot[17:18, D_MODEL:],
        "sc_conv_w": g_sc_conv, "norm_ffn_g": tot[18:19, :D_MODEL], "norm_final_g": tot[18:19, D_MODEL:],
    }
    small_w = {"norm_mix_g": (norm_mix_g, m_norm_mix_g, v_norm_mix_g),
               "ssm_conv_w": (ssm_conv_w[0], m_ssm_conv_w[0], v_ssm_conv_w[0]),
               "ssm_conv_b": (ssm_conv_b, m_ssm_conv_b, v_ssm_conv_b),
               "ssm_dt_bias": (ssm_dt_bias, m_ssm_dt_bias, v_ssm_dt_bias),
               "ssm_A_log": (ssm_A_log, m_ssm_A_log, v_ssm_A_log),
               "ssm_D": (ssm_D, m_ssm_D, v_ssm_D),
               "ssm_norm_g": (ssm_norm_g, m_ssm_norm_g, v_ssm_norm_g),
               "sc_conv_w": (sc_conv_w[0], m_sc_conv_w[0], v_sc_conv_w[0]),
               "norm_ffn_g": (norm_ffn_g, m_norm_ffn_g, v_norm_ffn_g),
               "norm_final_g": (norm_final_g.reshape(1, -1), m_norm_final_g.reshape(1, -1),
                                v_norm_final_g.reshape(1, -1))}
    PW = 1024
    order = list(small_w)

    def pack(arrs):
        rows = []
        for a in arrs:
            flat = a.reshape(-1)
            n = -(-flat.shape[0] // PW) * PW
            rows.append(jnp.pad(flat, (0, n - flat.shape[0])).reshape(-1, PW))
        slab = jnp.concatenate(rows, axis=0)
        return _pad_rows(slab, -(-slab.shape[0] // 8) * 8)

    wp = pack([small_w[k][0] for k in order])
    mp = pack([small_w[k][1] for k in order])
    vp = pack([small_w[k][2] for k in order])
    gp = pack([small_grads[k] for k in order])
    sd, sm, sv = _adamw(wp, gp, mp, vp, "adamw_small")

    def unpack(slab):
        out, row = {}, 0
        for k in order:
            shape = small_w[k][0].shape
            size = 1
            for s in shape:
                size *= s
            nr = -(-size // PW)
            out[k] = slab[row:row + nr].reshape(-1)[:size].reshape(shape)
            row += nr
        return out

    s_delta, s_m, s_v = unpack(sd), unpack(sm), unpack(sv)

    big_g = dict(zip(names, big_grads))

    weight_order = ["norm_mix_g", "w_in", "ssm_conv_w", "ssm_conv_b", "ssm_dt_bias", "ssm_A_log", "ssm_D",
                    "ssm_norm_g", "sc_conv_w", "w_out", "norm_ffn_g", "w_gate", "w_up", "w_down", "norm_final_g"]
    lead = {"ssm_conv_w", "sc_conv_w", "w_in", "w_out", "w_gate", "w_up", "w_down"}

    def shaped(nm, a):
        if nm == "norm_final_g":
            return a.reshape(D_MODEL)
        return a[None] if nm in lead else a

    grads, deltas, new_m, new_v = [], [], [], []
    for nm in weight_order:
        if nm in big_out:
            g, (d, m, v) = big_g[nm], big_out[nm]
        else:
            g, d, m, v = small_grads[nm], s_delta[nm], s_m[nm], s_v[nm]
        grads.append(shaped(nm, g))
        deltas.append(shaped(nm, d))
        new_m.append(shaped(nm, m))
        new_v.append(shaped(nm, v))
    return (loss, dx[None], *grads, *deltas, *new_m, *new_v)


def _swiglu_bwd(da, dg_factor, du_factor):
    return da * dg_factor.astype(F32), da * du_factor.astype(F32)


def _ffn_fwd(n2, w_gate, w_up):
    T, K = n2.shape
    tn = w_gate.shape[2]
    N = N_CHIPS * tn
    tm = _tile(T, 512)
    sub = _tile(tm, 256)

    def body(a_ref, wg_ref, wu_ref, g_ref, u_ref, act_ref):
        for s in range(tm // sub):
            rows = pl.ds(s * sub, sub)
            a = a_ref[rows, :]
            g = jnp.dot(a, wg_ref[...], preferred_element_type=F32)
            u = jnp.dot(a, wu_ref[...], preferred_element_type=F32)
            sig = _sigmoid(g)
            sg = g * sig
            g_ref[rows, :] = (u * (sig * (1.0 + g - sg))).astype(BF16)
            u_ref[rows, :] = sg.astype(BF16)
            act_ref[rows, :] = (sg * u).astype(BF16)

    a_spec = pl.BlockSpec((tm, K), lambda j, i: (i, 0))
    b_spec = pl.BlockSpec((None, K, tn), lambda j, i: (j, 0, 0))
    o_spec = pl.BlockSpec((tm, tn), lambda j, i: (i, j))
    return pl.pallas_call(
        body, name="ffn_fwd", grid=(N // tn, T // tm),
        in_specs=[a_spec, b_spec, b_spec], out_specs=[o_spec] * 3,
        out_shape=[jax.ShapeDtypeStruct((T, N), BF16)] * 3,
        compiler_params=_cparams(("parallel", "parallel")),
    )(n2, w_gate, w_up)
```

```python
import functools

import jax
import jax.numpy as jnp
from jax import lax
from jax.experimental import pallas as pl
from jax.experimental.pallas import tpu as pltpu

F32 = jnp.float32
BF16 = jnp.bfloat16
MESH = pl.DeviceIdType.MESH

D_MODEL = 2048
D_SSM = 2048
HEADDIM = 64
N_HEADS = 32
N_GROUPS = 8
HEADS_PER_GROUP = 4
N_STATE = 128
CHUNK = 128
K_SSM = 4
K_SC = 3
D_XBC = 4096
D_FF = 5632
D_IN = 12320
D_MAIN = 12288
OFF_XBC, OFF_CB, OFF_CC, OFF_CX = 2048, 6144, 8192, 10240
DT_PAD = 128
EPS = 1e-5
N_CHIPS = 4
N_DEV = 8

ADAM_LR = 0.001
ADAM_B1 = 0.9
ADAM_B2 = 0.999
ADAM_EPS = 1e-08
ADAM_WD = 0.01
ADAM_STEP = 10

V7X_VMEM_BYTES = 64 * 1024 * 1024
VMEM_LIMIT = V7X_VMEM_BYTES - 8 * 1024 * 1024


def _cparams(sem=None):
    if sem is None:
        return pltpu.CompilerParams(vmem_limit_bytes=VMEM_LIMIT)
    return pltpu.CompilerParams(dimension_semantics=sem, vmem_limit_bytes=VMEM_LIMIT)


def _tile(dim, pref, unit=128):
    best = None
    t = unit
    while t <= min(dim, pref):
        if dim % t == 0:
            best = t
        t += unit
    return best if best is not None else dim


def _sigmoid(x):
    return 1.0 / (1.0 + jnp.exp(-x))


def _silu(x):
    return x * _sigmoid(x)


def _dsilu(x):
    s = _sigmoid(x)
    return s * (1.0 + x * (1.0 - s))


def _softplus(x):
    return jnp.maximum(x, 0.0) + jnp.log(1.0 + jnp.exp(-jnp.abs(x)))


MATMUL_VMEM_BUDGET = 44 * 1024 * 1024


def _matmul(pairs, *, ta=False, tb=False, out_dtypes, name, tm=1024, tn=1024, tk=None, extras=(), epilogue=None,
            deps=(), col_shards=False, nsub=1, b3d=False, m_tiles=None, out_buf=None):
    a0, b0 = pairs[0]
    M, K = (a0.shape[1], a0.shape[0]) if ta else a0.shape
    if b3d:
        N = b0.shape[1] if tb else b0.shape[0] * b0.shape[2]
        tk, tn = (b0.shape[2], tn) if tb else (tk, b0.shape[2])
    else:
        N = b0.shape[0] if tb else b0.shape[1]
    tm, tn = _tile(M, tm, 8 if M % 128 else 128), _tile(N, tn)
    npair, nex, ndep, nout = len(pairs), len(extras), len(deps), len(out_dtypes)
    if tk is None:
        fixed = 2 * tm * tn * (sum(jnp.dtype(d).itemsize for d in out_dtypes) + sum(e.dtype.itemsize for e in extras))
        tk = K
        while tk > 128 and (K % tk or tk % 128 or
                            fixed + 2 * npair * 2 * tk * (tm + tn) + (tm * tn * 4 if tk < K else 0) > MATMUL_VMEM_BUDGET):
            tk -= 128
    else:
        tk = _tile(K, tk)
    nk = K // tk
    if nk > 1 or tm % nsub or (tm // nsub) % 128:
        nsub = 1
    sub = tm // nsub
    dims = (((0 if ta else 1,), (1 if tb else 0,)), ((), ()))
    i0, mi = m_tiles if m_tiles is not None else (0, M // tm)
    nbuf = 0 if out_buf is None else 1

    def body(*refs):
        a_refs = refs[0:2 * npair:2]
        b_refs = refs[1:2 * npair:2]
        ex_refs = refs[2 * npair:2 * npair + nex]
        o_refs = refs[2 * npair + nex + ndep + nbuf:2 * npair + nex + ndep + nbuf + nout]

        def dots(rows):
            s = None
            for a_ref, b_ref in zip(a_refs, b_refs):
                a = a_ref[...] if rows is None else (a_ref[:, rows] if ta else a_ref[rows, :])
                d = lax.dot_general(a, b_ref[...], dims, preferred_element_type=F32)
                s = d if s is None else s + d
            return s

        def finish(r, rows):
            ex = [e[...] if rows is None else e[rows, :] for e in ex_refs]
            outs = (r,) if epilogue is None else epilogue(r, *ex)
            for o_ref, o in zip(o_refs, outs):
                if rows is None:
                    o_ref[...] = o.astype(o_ref.dtype)
                else:
                    o_ref[rows, :] = o.astype(o_ref.dtype)

        if nk == 1:
            for s in range(nsub):
                rows = None if nsub == 1 else pl.ds(s * sub, sub)
                finish(dots(rows), rows)
            return

        acc = refs[-1]
        k = pl.program_id(2)

        @pl.when(k == 0)
        def _():
            acc[...] = dots(None)

        @pl.when(jnp.logical_and(k > 0, k < nk - 1))
        def _():
            acc[...] += dots(None)

        @pl.when(k == nk - 1)
        def _():
            finish(acc[...] + dots(None), None)

    a_spec = (pl.BlockSpec((tk, tm), lambda i, j, k: (k, i + i0)) if ta
              else pl.BlockSpec((tm, tk), lambda i, j, k: (i + i0, k)))
    if b3d:
        b_spec = (pl.BlockSpec((None, tn, tk), lambda i, j, k: (k, j, 0)) if tb
                  else pl.BlockSpec((None, tk, tn), lambda i, j, k: (j, k, 0)))
    else:
        b_spec = (pl.BlockSpec((tn, tk), lambda i, j, k: (j, k)) if tb
                  else pl.BlockSpec((tk, tn), lambda i, j, k: (k, j)))
    e_spec = pl.BlockSpec((tm, tn), lambda i, j, k: (i + i0, j))
    if col_shards:
        o_spec = pl.BlockSpec((None, tm, tn), lambda i, j, k: (j, i + i0, 0))
        o_shape = (N // tn, M, tn)
    else:
        o_spec, o_shape = e_spec, (M, N)
    args, in_specs = [], []
    for a, b in pairs:
        args += [a, b]
        in_specs += [a_spec, b_spec]
    args += list(extras) + list(deps) + ([] if out_buf is None else [out_buf])
    in_specs += [e_spec] * nex + [ANY] * (ndep + nbuf)
    outs = pl.pallas_call(
        body,
        name=name,
        grid=(mi, N // tn, nk),
        in_specs=in_specs,
        out_specs=[o_spec] * nout,
        out_shape=[jax.ShapeDtypeStruct(o_shape, dt) for dt in out_dtypes],
        input_output_aliases={} if out_buf is None else {len(args) - 1: 0},
        scratch_shapes=[pltpu.VMEM((tm, tn), F32)] if nk > 1 else [],
        compiler_params=_cparams(("parallel", "parallel", "arbitrary")),
    )(*args)
    return outs


def _cast_into_gather(w, chip_arr, name, split_cols=False, deps=()):
    R, C = w.shape
    hr, hc = (R, C // 2) if split_cols else (R // 2, C)
    tr = _tile(hr, 512, 8)
    nb = hr // tr

    def body(chip_ref, w_ref, *rest):
        rest[-1][...] = w_ref[...].astype(BF16)

    in_map = (lambda h, i, chip_ref: (i, h)) if split_cols else (lambda h, i, chip_ref: (h * nb + i, 0))
    grid_spec = pltpu.PrefetchScalarGridSpec(
        num_scalar_prefetch=1, grid=(2, nb),
        in_specs=[pl.BlockSpec((tr, hc), in_map)] + [ANY] * len(deps),
        out_specs=pl.BlockSpec((None, tr, hc), lambda h, i, chip_ref: (2 * chip_ref[0] + h, i, 0)))
    return pl.pallas_call(
        body, name=name, grid_spec=grid_spec,
        out_shape=jax.ShapeDtypeStruct((N_DEV, hr, hc), BF16),
        compiler_params=_cparams(("parallel", "parallel")),
    )(chip_arr, w, *deps)


def _tie(small, token, name):
    def body(s_ref, t_ref, o_ref):
        o_ref[...] = s_ref[...]

    vm = pl.BlockSpec(memory_space=pltpu.VMEM)
    return pl.pallas_call(body, name=name, in_specs=[vm, ANY], out_specs=vm,
                          out_shape=jax.ShapeDtypeStruct(small.shape, small.dtype))(small, token)


def _rmsnorm_fwd(x, g, name):
    T, D = x.shape
    tt = _tile(T, 256)

    def body(x_ref, g_ref, n_ref):
        xv = x_ref[...]
        r = lax.rsqrt(jnp.mean(xv * xv, axis=-1, keepdims=True) + EPS)
        n_ref[...] = (xv * r * g_ref[...]).astype(BF16)

    return pl.pallas_call(
        body, name=name, grid=(T // tt,),
        in_specs=[pl.BlockSpec((tt, D), lambda i: (i, 0)), pl.BlockSpec((1, D), lambda i: (0, 0))],
        out_specs=pl.BlockSpec((tt, D), lambda i: (i, 0)),
        out_shape=jax.ShapeDtypeStruct((T, D), BF16),
        compiler_params=_cparams(("parallel",)),
    )(x, g)


def _rmsnorm_bwd(dn, x, g, res, name):
    T, D = x.shape
    tt = _tile(T, 256)

    def body(dn_ref, x_ref, g_ref, res_ref, dx_ref, dxb_ref, dg_ref):
        @pl.when(pl.program_id(0) == 0)
        def _():
            dg_ref[...] = jnp.zeros_like(dg_ref)

        xv = x_ref[...]
        dy = dn_ref[...].astype(F32)
        r = lax.rsqrt(jnp.mean(xv * xv, axis=-1, keepdims=True) + EPS)
        xhat = xv * r
        dxh = dy * g_ref[...]
        dx = res_ref[...] + r * (dxh - xhat * jnp.mean(dxh * xhat, axis=-1, keepdims=True))
        dx_ref[...] = dx
        dxb_ref[...] = dx.astype(BF16)
        dg_ref[...] += jnp.sum(dy * xhat, axis=0, keepdims=True)

    tok = pl.BlockSpec((tt, D), lambda i: (i, 0))
    vec = pl.BlockSpec((1, D), lambda i: (0, 0))
    return pl.pallas_call(
        body, name=name, grid=(T // tt,),
        in_specs=[tok, tok, vec, tok],
        out_specs=[tok, tok, vec],
        out_shape=[jax.ShapeDtypeStruct((T, D), F32), jax.ShapeDtypeStruct((T, D), BF16),
                   jax.ShapeDtypeStruct((1, D), F32)],
        compiler_params=_cparams(("arbitrary",)),
    )(dn, x, g, res)


def _loss_and_final_bwd(h2, target, gf):
    T, D = h2.shape
    tt = _tile(T, 256)

    def body(h_ref, t_ref, g_ref, dh_ref, dhb_ref, dg_ref, loss_ref):
        @pl.when(pl.program_id(0) == 0)
        def _():
            dg_ref[...] = jnp.zeros_like(dg_ref)
            loss_ref[...] = jnp.zeros_like(loss_ref)

        xv = h_ref[...]
        r = lax.rsqrt(jnp.mean(xv * xv, axis=-1, keepdims=True) + EPS)
        xhat = xv * r
        err = xhat * g_ref[...] - t_ref[...]
        loss_ref[...] += 0.5 * jnp.sum(jnp.mean(err * err, axis=-1, keepdims=True), axis=0, keepdims=True)
        dy = err * (1.0 / D)
        dxh = dy * g_ref[...]
        dx = r * (dxh - xhat * jnp.mean(dxh * xhat, axis=-1, keepdims=True))
        dh_ref[...] = dx
        dhb_ref[...] = dx.astype(BF16)
        dg_ref[...] += jnp.sum(dy * xhat, axis=0, keepdims=True)

    tok = pl.BlockSpec((tt, D), lambda i: (i, 0))
    vec = pl.BlockSpec((1, D), lambda i: (0, 0))
    return pl.pallas_call(
        body, name="loss_final_bwd", grid=(T // tt,),
        in_specs=[tok, tok, vec],
        out_specs=[tok, tok, vec, pl.BlockSpec((1, 1), lambda i: (0, 0))],
        out_shape=[jax.ShapeDtypeStruct((T, D), F32), jax.ShapeDtypeStruct((T, D), BF16),
                   jax.ShapeDtypeStruct((1, D), F32), jax.ShapeDtypeStruct((1, 1), F32)],
        compiler_params=_cparams(("arbitrary",)),
    )(h2, target, gf)


def _gated_norm_fwd(y, proj, g):
    T, D = y.shape
    tt = _tile(T, 256)

    def body(y_ref, z_ref, g_ref, o_ref):
        yg = y_ref[...] * _silu(z_ref[...])
        r = lax.rsqrt(jnp.mean(yg * yg, axis=-1, keepdims=True) + EPS)
        o_ref[...] = (yg * r * g_ref[...]).astype(BF16)

    tok = pl.BlockSpec((tt, D), lambda i: (i, 0))
    return pl.pallas_call(
        body, name="gated_norm_fwd", grid=(T // tt,),
        in_specs=[tok, tok, pl.BlockSpec((1, D), lambda i: (0, 0))],
        out_specs=tok,
        out_shape=jax.ShapeDtypeStruct((T, 2 * D_MODEL), BF16),
        compiler_params=_cparams(("parallel",)),
    )(y, proj, g)


def _gated_norm_bwd(dmix, y, proj, g, dproj):
    T, D = y.shape
    tt = _tile(T, 256)

    def body(do_ref, y_ref, z_ref, g_ref, dp_ref, dy_ref, dz_ref, dg_ref):
        @pl.when(pl.program_id(0) == 0)
        def _():
            dg_ref[...] = jnp.zeros_like(dg_ref)

        yv, zv = y_ref[...], z_ref[...]
        do = do_ref[...].astype(F32)
        sz = _silu(zv)
        yg = yv * sz
        r = lax.rsqrt(jnp.mean(yg * yg, axis=-1, keepdims=True) + EPS)
        xhat = yg * r
        dxh = do * g_ref[...]
        dyg = r * (dxh - xhat * jnp.mean(dxh * xhat, axis=-1, keepdims=True))
        dy_ref[...] = dyg * sz
        dz_ref[...] = (dyg * yv * _dsilu(zv)).astype(BF16)
        dg_ref[...] += jnp.sum(do * xhat, axis=0, keepdims=True)

    tok = pl.BlockSpec((tt, D), lambda i: (i, 0))
    vec = pl.BlockSpec((1, D), lambda i: (0, 0))
    return pl.pallas_call(
        body, name="gated_norm_bwd", grid=(T // tt,),
        in_specs=[tok, tok, tok, vec, ANY],
        out_specs=[tok, tok, vec],
        out_shape=[jax.ShapeDtypeStruct((T, D), F32), jax.ShapeDtypeStruct(dproj.shape, BF16),
                   jax.ShapeDtypeStruct((1, D), F32)],
        input_output_aliases={4: 1},
        compiler_params=_cparams(("arbitrary",)),
    )(dmix, y, proj, g, dproj)


HALO = 8


def _shift_down(cur, prev8, s):
    ext = jnp.concatenate([prev8, cur], axis=0)
    return pltpu.roll(ext, s, axis=0)[HALO:]


def _shift_up(cur, next8, s):
    n = cur.shape[0]
    ext = jnp.concatenate([cur, next8], axis=0)
    return pltpu.roll(ext, n + HALO - s, axis=0)[:n]


def _conv_specs(tt, cb, col_off_blocks, nt):
    hb = tt // HALO
    cur = pl.BlockSpec((tt, cb), lambda j, i: (i, col_off_blocks + j))
    prev = pl.BlockSpec((HALO, cb), lambda j, i: (jnp.maximum(i * hb - 1, 0), col_off_blocks + j))
    nxt = pl.BlockSpec((HALO, cb), lambda j, i: (jnp.minimum((i + 1) * hb, nt * hb - 1), col_off_blocks + j))
    return cur, prev, nxt


def _taps(cur, prev8, K):
    return [_shift_down(cur, prev8, K - 1 - k) for k in range(K - 1)] + [cur]


def _conv_of_taps(taps, w):
    y = taps[-1] * w[len(taps) - 1:len(taps), :]
    for k, t in enumerate(taps[:-1]):
        y = y + t * w[k:k + 1, :]
    return y


def _causal_conv(cur, prev8, w, K):
    return _conv_of_taps(_taps(cur, prev8, K), w)


def _anticausal_conv(cur, next8, w, K):
    y = cur * w[K - 1:K, :]
    for k in range(K - 1):
        y = y + _shift_up(cur, next8, K - 1 - k) * w[k:k + 1, :]
    return y


def _ssm_conv_fwd(proj, w8, b):
    T = proj.shape[0]
    tt, cb = _tile(T, 512), 512
    nt = T // tt
    cur, prev, _ = _conv_specs(tt, cb, OFF_XBC // cb, nt)

    def body(u_ref, up_ref, w_ref, b_ref, o_ref):
        first = pl.program_id(1) == 0
        p8 = jnp.where(first, 0.0, up_ref[...])
        pre = _causal_conv(u_ref[...], p8, w_ref[...], K_SSM) + b_ref[...]
        o_ref[...] = _silu(pre)

    return pl.pallas_call(
        body, name="ssm_conv_fwd", grid=(D_XBC // cb, nt),
        in_specs=[cur, prev, pl.BlockSpec((8, cb), lambda j, i: (0, j)), pl.BlockSpec((1, cb), lambda j, i: (0, j))],
        out_specs=pl.BlockSpec((tt, cb), lambda j, i: (i, j)),
        out_shape=jax.ShapeDtypeStruct((T, D_XBC), F32),
        compiler_params=_cparams(("parallel", "parallel")),
    )(proj, proj, w8, b)


def _ssm_conv_bwd(dact, proj, w8, b, dproj):
    T = proj.shape[0]
    tt, cb = _tile(T, 512), 512
    nt = T // tt
    cur, prev, nxt = _conv_specs(tt, cb, OFF_XBC // cb, nt)
    dcur, dprev, dnxt = _conv_specs(tt, cb, 0, nt)

    def dpre_of(d, u, p8, w, bb):
        pre = _causal_conv(u, p8, w, K_SSM) + bb
        return d * _dsilu(pre)

    def body(d_ref, dn_ref, u_ref, up_ref, un_ref, w_ref, b_ref, dp_ref, dx_ref, dw_ref, db_ref):
        i = pl.program_id(1)

        @pl.when(i == 0)
        def _():
            dw_ref[...] = jnp.zeros_like(dw_ref)
            db_ref[...] = jnp.zeros_like(db_ref)

        w, bb = w_ref[...], b_ref[...]
        u = u_ref[...]
        p8 = jnp.where(i == 0, 0.0, up_ref[...])
        taps = _taps(u, p8, K_SSM)
        dpre = d_ref[...] * _dsilu(_conv_of_taps(taps, w) + bb)
        un = un_ref[...]
        dpre_n = dpre_of(dn_ref[...], un, u[tt - HALO:, :], w, bb)
        dpre_n = jnp.where(i == nt - 1, 0.0, dpre_n)
        dx_ref[...] = _anticausal_conv(dpre, dpre_n, w, K_SSM).astype(BF16)
        rows = [jnp.sum(dpre * t, axis=0, keepdims=True) for t in taps]
        rows.append(jnp.zeros((8 - K_SSM, cb), F32))
        dw_ref[...] += jnp.concatenate(rows, axis=0)
        db_ref[...] += jnp.sum(dpre, axis=0, keepdims=True)

    wspec = pl.BlockSpec((8, cb), lambda j, i: (0, j))
    bspec = pl.BlockSpec((1, cb), lambda j, i: (0, j))
    return pl.pallas_call(
        body, name="ssm_conv_bwd", grid=(D_XBC // cb, nt),
        in_specs=[dcur, dnxt, cur, prev, nxt, wspec, bspec, ANY],
        out_specs=[pl.BlockSpec((tt, cb), lambda j, i: (i, OFF_XBC // cb + j)), wspec, bspec],
        out_shape=[jax.ShapeDtypeStruct(dproj.shape, BF16), jax.ShapeDtypeStruct((8, D_XBC), F32),
                   jax.ShapeDtypeStruct((1, D_XBC), F32)],
        input_output_aliases={7: 0},
        compiler_params=_cparams(("parallel", "arbitrary")),
    )(dact, dact, proj, proj, proj, w8, b, dproj)


SCB = 512
SC3 = 3 * SCB


def _sc_specs(tt, nt):
    hb = tt // HALO
    cur = pl.BlockSpec((tt, SC3), lambda j, i: (i, OFF_CB // SC3 + j))
    prev = pl.BlockSpec((HALO, SC3), lambda j, i: (jnp.maximum(i * hb - 1, 0), OFF_CB // SC3 + j))
    nxt = pl.BlockSpec((HALO, SC3), lambda j, i: (jnp.minimum((i + 1) * hb, nt * hb - 1), OFF_CB // SC3 + j))
    return cur, prev, nxt


def _shortconv_fwd(proj, w8, ymix):
    T = proj.shape[0]
    tt = _tile(T, 512)
    nt = T // tt
    cur, prev, _ = _sc_specs(tt, nt)

    def body(p_ref, pp_ref, w_ref, y_ref, o_ref):
        p, pp = p_ref[...], pp_ref[...]
        v = p[:, SCB:2 * SCB] * p[:, 2 * SCB:]
        vp = jnp.where(pl.program_id(1) == 0, 0.0, pp[:, SCB:2 * SCB] * pp[:, 2 * SCB:])
        o_ref[...] = (p[:, :SCB] * _causal_conv(v, vp, w_ref[...], K_SC)).astype(BF16)

    return pl.pallas_call(
        body, name="shortconv_fwd", grid=(D_MODEL // SCB, nt),
        in_specs=[cur, prev, pl.BlockSpec((8, SCB), lambda j, i: (0, j)), ANY],
        out_specs=pl.BlockSpec((tt, SCB), lambda j, i: (i, D_SSM // SCB + j)),
        out_shape=jax.ShapeDtypeStruct(ymix.shape, BF16),
        input_output_aliases={3: 0},
        compiler_params=_cparams(("parallel", "parallel")),
    )(proj, proj, w8, ymix)


def _shortconv_bwd(dmix, proj, w8):
    T = proj.shape[0]
    tt = _tile(T, 512)
    nt = T // tt
    hb = tt // HALO
    cur, prev, nxt = _sc_specs(tt, nt)
    d_s = pl.BlockSpec((tt, SCB), lambda j, i: (i, D_SSM // SCB + j))
    dn_s = pl.BlockSpec((HALO, SCB), lambda j, i: (jnp.minimum((i + 1) * hb, nt * hb - 1), D_SSM // SCB + j))

    def body(d_ref, dn_ref, p_ref, pp_ref, pn_ref, w_ref, dp_ref, dw_ref):
        i = pl.program_id(1)

        @pl.when(i == 0)
        def _():
            dw_ref[...] = jnp.zeros_like(dw_ref)

        w = w_ref[...]
        p, pp = p_ref[...], pp_ref[...]
        gb, gc, u = p[:, :SCB], p[:, SCB:2 * SCB], p[:, 2 * SCB:]
        v = gc * u
        vp = jnp.where(i == 0, 0.0, pp[:, SCB:2 * SCB] * pp[:, 2 * SCB:])
        d = d_ref[...].astype(F32)
        taps = _taps(v, vp, K_SC)
        dp_ref[:, :SCB] = (d * _conv_of_taps(taps, w)).astype(BF16)
        dcv = d * gb
        dcv_n = jnp.where(i == nt - 1, 0.0, dn_ref[...].astype(F32) * pn_ref[:, :SCB])
        dv = _anticausal_conv(dcv, dcv_n, w, K_SC)
        dp_ref[:, SCB:2 * SCB] = (dv * u).astype(BF16)
        dp_ref[:, 2 * SCB:] = (dv * gc).astype(BF16)
        rows = [jnp.sum(dcv * t, axis=0, keepdims=True) for t in taps]
        rows.append(jnp.zeros((8 - K_SC, SCB), F32))
        dw_ref[...] += jnp.concatenate(rows, axis=0)

    wspec = pl.BlockSpec((8, SCB), lambda j, i: (0, j))
    return pl.pallas_call(
        body, name="shortconv_bwd", grid=(D_MODEL // SCB, nt),
        in_specs=[d_s, dn_s, cur, prev, nxt, wspec],
        out_specs=[cur, wspec],
        out_shape=[jax.ShapeDtypeStruct((T, D_MAIN), BF16), jax.ShapeDtypeStruct((8, D_MODEL), F32)],
        compiler_params=_cparams(("parallel", "arbitrary")),
    )(dmix, dmix, proj, proj, proj, w8)


GW = HEADS_PER_GROUP * HEADDIM


def _dot(a, b):
    return jnp.dot(a.astype(BF16), b.astype(BF16), preferred_element_type=F32)


def _dot_nt(a, b):
    return lax.dot_general(a.astype(BF16), b.astype(BF16), (((1,), (1,)), ((), ())), preferred_element_type=F32)


def _dot_tn(a, b):
    return lax.dot_general(a.astype(BF16), b.astype(BF16), (((0,), (0,)), ((), ())), preferred_element_type=F32)


def _bf16_terms(x, n):
    terms, r = [], x
    for _ in range(n):
        t = r.astype(BF16)
        terms.append(t)
        r = r - t.astype(F32)
    return terms


def _dot_sel(a, sel, n=2):
    s = sel.astype(BF16)
    return sum(jnp.dot(t, s, preferred_element_type=F32) for t in _bf16_terms(a, n))


def _sel_dot(sel, b, n=2):
    s = sel.astype(BF16)
    return sum(jnp.dot(s, t, preferred_element_type=F32) for t in _bf16_terms(b, n))


def _sel_dot_nt(sel, b, n=2):
    s = sel.astype(BF16)
    return sum(lax.dot_general(s, t, (((1,), (1,)), ((), ())), preferred_element_type=F32)
               for t in _bf16_terms(b, n))


def _head_cols(rows):
    parts = [jnp.broadcast_to(rows[r:r + 1, :], (HEADDIM, CHUNK)) for r in range(HEADS_PER_GROUP)]
    return jnp.concatenate(parts, axis=0).T


def _head_rows(rows):
    parts = [jnp.broadcast_to(rows[r:r + 1, :], (HEADDIM, N_STATE)) for r in range(HEADS_PER_GROUP)]
    return jnp.concatenate(parts, axis=0)


def _ssd_common(dtr, bias, alog):
    dt = _softplus(dtr + bias)
    A = -jnp.exp(alog)
    a = dt * A
    ki = lax.broadcasted_iota(jnp.int32, (CHUNK, CHUNK), 0)
    si = lax.broadcasted_iota(jnp.int32, (CHUNK, CHUNK), 1)
    upper = (ki <= si).astype(F32)
    cs = _dot_sel(a, upper, 3)
    cs_last = jnp.broadcast_to(cs[:, CHUNK - 1:CHUNK], (8, CHUNK))
    return dt, A, a, cs, cs_last


def _decay_matrix(cs, r):
    li = lax.broadcasted_iota(jnp.int32, (CHUNK, CHUNK), 0)
    si = lax.broadcasted_iota(jnp.int32, (CHUNK, CHUNK), 1)
    causal = li >= si
    R = jnp.broadcast_to(cs[r:r + 1, :], (CHUNK, CHUNK))
    seg = jnp.where(causal, R.T - R, 0.0)
    return jnp.where(causal, jnp.exp(seg), 0.0)


def _decay_cat(cs):
    return jnp.concatenate([_decay_matrix(cs, r) for r in range(HEADS_PER_GROUP)], axis=1)


def _lanes4(m):
    return jnp.concatenate([m] * HEADS_PER_GROUP, axis=1)


def _head_blocks(v):
    col = lax.broadcasted_iota(jnp.int32, v.shape, 1) // HEADDIM
    return jnp.concatenate([jnp.where(col == r, v, jnp.zeros_like(v)) for r in range(HEADS_PER_GROUP)], axis=0)


GXBC = GW + 2 * N_STATE


GS_FWD = 8
GS_BWD = 8


def _ssd_in_specs(nc, rev):
    GS = GS_BWD if rev else GS_FWD
    cix = (lambda c: nc - 1 - c) if rev else (lambda c: c)
    x_s = pl.BlockSpec((CHUNK, GS * GW), lambda g, c: (cix(c), g))
    xbc_s = pl.BlockSpec((CHUNK, GS * GXBC), lambda g, c: (cix(c), g))
    dtr_s = pl.BlockSpec((GS, 8, CHUNK), lambda g, c: (g, 0, cix(c)))
    row_s = pl.BlockSpec((GS, 8, CHUNK), lambda g, c: (g, 0, 0))
    drep_s = pl.BlockSpec((1, GS * GW), lambda g, c: (0, g))
    hs_s = pl.BlockSpec((1, GS * GW, N_STATE), lambda g, c: (cix(c), g, 0))
    return x_s, xbc_s, dtr_s, row_s, drep_s, hs_s


def _xbc_parts(xbc_ref, gi):
    o = gi * GXBC
    return xbc_ref[:, o:o + GW], xbc_ref[:, o + GW:o + GW + N_STATE], xbc_ref[:, o + GW + N_STATE:o + GXBC]


def _ssd_fwd(xbc, dtr, bias, alog, drep):
    T = xbc.shape[0]
    nc = T // CHUNK
    x_s, xbc_s, dtr_s, row_s, drep_s, hs_s = _ssd_in_specs(nc, False)

    def body(xbc_ref, dtr_ref, bias_ref, alog_ref, drep_ref, y_ref, hs_ref, h_scr):
        @pl.when(pl.program_id(1) == 0)
        def _():
            h_scr[...] = jnp.zeros_like(h_scr)

        for gi in range(GS_FWD):
            cols, rows = slice(gi * GW, (gi + 1) * GW), pl.ds(gi * GW, GW)
            x, Bm, Cm = _xbc_parts(xbc_ref, gi)
            dt, A, a, cs, cs_last = _ssd_common(dtr_ref[gi], bias_ref[gi], alog_ref[gi])
            E = _head_cols(jnp.exp(cs))
            W = _head_cols(jnp.exp(cs_last - cs) * dt)
            X = (x * _head_cols(dt)).astype(BF16)
            CB = _dot_nt(Cm, Bm)
            col = lax.broadcasted_iota(jnp.int32, (CHUNK, GW), 1) // HEADDIM
            y = jnp.zeros((CHUNK, GW), F32)
            for r in range(HEADS_PER_GROUP):
                y = y + jnp.where(col == r, _dot(CB * _decay_matrix(cs, r), X), 0.0)
            h = h_scr[rows, :]
            hs_ref[0, rows, :] = h
            y = y + _dot_nt(Cm, h) * E
            y_ref[:, cols] = y + drep_ref[:, cols] * x
            h_scr[rows, :] = h * _head_rows(jnp.exp(cs_last)) + _dot_tn(x * W, Bm)

    return pl.pallas_call(
        body, name="ssd_fwd", grid=(N_GROUPS // GS_FWD, nc),
        in_specs=[xbc_s, dtr_s, row_s, row_s, drep_s],
        out_specs=[x_s, hs_s],
        out_shape=[jax.ShapeDtypeStruct((T, D_SSM), F32), jax.ShapeDtypeStruct((nc, D_SSM, N_STATE), F32)],
        scratch_shapes=[pltpu.VMEM((GS_FWD * GW, N_STATE), F32)],
        compiler_params=_cparams(("parallel", "arbitrary")),
    )(xbc, dtr, bias, alog, drep)


def _ssd_bwd(xbc, dtr, bias, alog, drep, dy, hs):
    T = xbc.shape[0]
    nc = T // CHUNK
    x_s, xbc_s, dtr_s, row_s, drep_s, hs_s = _ssd_in_specs(nc, True)

    def body(xbc_ref, dtr_ref, bias_ref, alog_ref, drep_ref, dy_ref, hs_ref,
             dxbc_ref, ddtr_ref, dbias_ref, dalog_ref, dd_ref, dh_scr):
        @pl.when(pl.program_id(1) == 0)
        def _():
            dh_scr[...] = jnp.zeros_like(dh_scr)
            dbias_ref[...] = jnp.zeros_like(dbias_ref)
            dalog_ref[...] = jnp.zeros_like(dalog_ref)
            dd_ref[...] = jnp.zeros_like(dd_ref)

        for gi in range(GS_BWD):
            one_group(gi, xbc_ref, dtr_ref, bias_ref, alog_ref, drep_ref, dy_ref, hs_ref,
                      dxbc_ref, ddtr_ref, dbias_ref, dalog_ref, dd_ref, dh_scr)

    def one_group(gi, xbc_ref, dtr_ref, bias_ref, alog_ref, drep_ref, dy_ref, hs_ref,
                  dxbc_ref, ddtr_ref, dbias_ref, dalog_ref, dd_ref, dh_scr):
        cols, rows, o = slice(gi * GW, (gi + 1) * GW), pl.ds(gi * GW, GW), gi * GXBC
        x, Bm, Cm = _xbc_parts(xbc_ref, gi)
        dY = dy_ref[:, cols]
        dt, A, a, cs, cs_last = _ssd_common(dtr_ref[gi], bias_ref[gi], alog_ref[gi])
        E = _head_cols(jnp.exp(cs))
        DT = _head_cols(dt)
        Wd = _head_cols(jnp.exp(cs_last - cs))
        X = x * DT
        h = hs_ref[0, rows, :]
        dS = dh_scr[rows, :]
        CB = _dot_nt(Cm, Bm)
        rowid = lax.broadcasted_iota(jnp.int32, (8, CHUNK), 0)
        lane = lax.broadcasted_iota(jnp.int32, (8, CHUNK), 1)
        hsel = (lax.broadcasted_iota(jnp.int32, (8, GW), 1) // HEADDIM
                == lax.broadcasted_iota(jnp.int32, (8, GW), 0)).astype(F32)
        hsel_l = (lax.broadcasted_iota(jnp.int32, (8, HEADS_PER_GROUP * CHUNK), 1) // CHUNK
                  == lax.broadcasted_iota(jnp.int32, (8, HEADS_PER_GROUP * CHUNK), 0)).astype(F32)

        Lc, CBc = _decay_cat(cs), _lanes4(CB)
        Mc = CBc * Lc
        GLc = _dot_nt(dY, _head_blocks(X.astype(BF16))) * Lc
        Wc = GLc * CBc
        colsum = jnp.sum(Wc, axis=0, keepdims=True)
        dcs = _sel_dot_nt(hsel_l, Wc)
        dCB = jnp.zeros((CHUNK, CHUNK), F32)
        for r in range(HEADS_PER_GROUP):
            blk = slice(r * CHUNK, (r + 1) * CHUNK)
            dCB = dCB + GLc[:, blk]
            dcs = dcs - jnp.where(rowid == r, colsum[:, blk], 0.0)
        m_stack = jnp.concatenate([Mc[:, r * CHUNK:(r + 1) * CHUNK].astype(BF16) for r in range(HEADS_PER_GROUP)],
                                  axis=0)
        dX = lax.dot_general(m_stack, _head_blocks(dY.astype(BF16)), (((0,), (0,)), ((), ())),
                             preferred_element_type=F32)
        dC = _dot(dCB, Bm)
        dB = _dot_tn(dCB, Cm)
        T1 = _dot_nt(Bm, dS)
        dX = dX + T1 * Wd
        dB = dB + _dot(X * Wd, dS)
        pdec = _sel_dot_nt(hsel, X * T1 * Wd)
        dcs = dcs - pdec
        dlast = jnp.sum(pdec, axis=1, keepdims=True) \
            + jnp.exp(cs_last[:, 0:1]) * jnp.sum(_sel_dot(hsel, dS * h), axis=1, keepdims=True)
        dYE = dY * E
        dC = dC + _dot(dYE, h)
        yoff = _dot_nt(Cm, h) * E
        dcs = dcs + _sel_dot_nt(hsel, dY * yoff)
        dcs = dcs + jnp.where(lane == CHUNK - 1, dlast, 0.0)
        ki = lax.broadcasted_iota(jnp.int32, (CHUNK, CHUNK), 0)
        si = lax.broadcasted_iota(jnp.int32, (CHUNK, CHUNK), 1)
        lower = (ki >= si).astype(F32)
        da = _dot_sel(dcs, lower)
        ddt = da * A + _sel_dot_nt(hsel, dX * x)
        ddtr = ddt * _sigmoid(dtr_ref[gi] + bias_ref[gi])
        ddtr_ref[gi] = ddtr
        dbias_ref[gi] += ddtr
        dalog_ref[gi] += da * a
        dxbc_ref[:, o:o + GW] = dX * DT + drep_ref[:, cols] * dY
        dd_ref[:, cols] += jnp.sum(dY * x, axis=0, keepdims=True)
        dxbc_ref[:, o + GW:o + GW + N_STATE] = dB
        dxbc_ref[:, o + GW + N_STATE:o + GXBC] = dC
        dh_scr[rows, :] = dS * _head_rows(jnp.exp(cs_last)) + _dot_tn(dYE, Cm)

    return pl.pallas_call(
        body, name="ssd_bwd", grid=(N_GROUPS // GS_BWD, nc),
        in_specs=[xbc_s, dtr_s, row_s, row_s, drep_s, x_s, hs_s],
        out_specs=[xbc_s, dtr_s, row_s, row_s, drep_s],
        out_shape=[jax.ShapeDtypeStruct((T, D_XBC), F32),
                   jax.ShapeDtypeStruct((N_GROUPS, 8, T), F32),
                   jax.ShapeDtypeStruct((N_GROUPS, 8, CHUNK), F32),
                   jax.ShapeDtypeStruct((N_GROUPS, 8, CHUNK), F32),
                   jax.ShapeDtypeStruct((1, D_SSM), F32)],
        scratch_shapes=[pltpu.VMEM((GS_BWD * GW, N_STATE), F32)],
        compiler_params=_cparams(("parallel", "arbitrary")),
    )(xbc, dtr, bias, alog, drep, dy, hs)


def _adamw(w, g, m, v, name, deps=(), emit_g=False):
    R, C = w.shape
    tr = _tile(R, 256, 8)
    nd = len(deps)
    nout = 4 if emit_g else 3

    def body(w_ref, g_ref, m_ref, v_ref, *rest):
        outs = rest[nd:]
        gv = g_ref[...]
        mn = ADAM_B1 * m_ref[...] + (1.0 - ADAM_B1) * gv
        vn = ADAM_B2 * v_ref[...] + (1.0 - ADAM_B2) * (gv * gv)
        m_hat = mn / (1.0 - ADAM_B1 ** ADAM_STEP)
        v_hat = vn / (1.0 - ADAM_B2 ** ADAM_STEP)
        outs[0][...] = -ADAM_LR * (m_hat / (jnp.sqrt(v_hat) + ADAM_EPS) + ADAM_WD * w_ref[...])
        outs[1][...] = mn
        outs[2][...] = vn
        if emit_g:
            outs[3][...] = gv

    spec = pl.BlockSpec((tr, C), lambda i: (i, 0))
    return pl.pallas_call(
        body, name=name, grid=(R // tr,),
        in_specs=[spec] * 4 + [ANY] * nd, out_specs=[spec] * nout,
        out_shape=[jax.ShapeDtypeStruct((R, C), F32)] * nout,
        compiler_params=_cparams(("parallel",)),
    )(w, g, m, v, *deps)


ANY = pl.BlockSpec(memory_space=pl.ANY)


def _place():
    x, y, c = lax.axis_index("x"), lax.axis_index("y"), lax.axis_index("c")
    return x, y, c


def _other_chips(x, y):
    return [(1 - x, y), (x, 1 - y), (1 - x, 1 - y)]


def _allgather_inplace(bufs, splits, first_done=False):
    n = len(bufs)

    def body(*refs):
        o_refs = refs[n:2 * n]
        send_sems, recv_sems = refs[2 * n:]
        x, y, c = _place()
        xn, yn, dg, sibling = (1 - x, y), (x, 1 - y), (1 - x, 1 - y), (x, y, 1 - c)

        def blk(k, chip, pc):
            return o_refs[k].at[4 * chip[0] + 2 * chip[1] + pc]

        def part(k, ref, p):
            kind, s = splits[k]
            _, R, C = bufs[k].shape
            if kind == "rows":
                return ref.at[pl.ds(0, s)] if p == 0 else ref.at[pl.ds(s, R - s)]
            return ref.at[:, pl.ds(0, s)] if p == 0 else ref.at[:, pl.ds(s, C - s)]

        def copy(k, slot, ref, to):
            return pltpu.make_async_remote_copy(
                src_ref=ref, dst_ref=ref, send_sem=send_sems.at[k, slot], recv_sem=recv_sems.at[k, slot],
                device_id=to, device_id_type=MESH)

        sent = []

        def send(k, slot, ref, to):
            cp = copy(k, slot, ref, to)
            cp.start()
            sent.append(cp)

        if not first_done:
            for k in range(n):
                send(k, 0, blk(k, (x, y), c), (*xn, c))
                send(k, 1, blk(k, (x, y), c), (*yn, c))
        for k in range(n):
            bx, by = blk(k, xn, c), blk(k, yn, c)
            if not first_done:
                copy(k, 0, bx, sibling).wait_recv()
            send(k, 2, part(k, bx, 0), (*yn, c))
            send(k, 4, bx, sibling)
            if not first_done:
                copy(k, 1, by, sibling).wait_recv()
            send(k, 3, part(k, by, 1), (*xn, c))
            send(k, 5, by, sibling)
        for k in range(n):
            d0, d1 = part(k, blk(k, dg, c), 0), part(k, blk(k, dg, c), 1)
            copy(k, 2, d0, sibling).wait_recv()
            send(k, 6, d0, sibling)
            copy(k, 3, d1, sibling).wait_recv()
            send(k, 7, d1, sibling)
        for k in range(n):
            copy(k, 4, blk(k, xn, 1 - c), sibling).wait_recv()
            copy(k, 5, blk(k, yn, 1 - c), sibling).wait_recv()
            copy(k, 6, part(k, blk(k, dg, 1 - c), 0), sibling).wait_recv()
            copy(k, 7, part(k, blk(k, dg, 1 - c), 1), sibling).wait_recv()
        for cp in sent:
            cp.wait_send()

    return pl.pallas_call(
        body, name="allgather_w_in",
        in_specs=[ANY] * n, out_specs=[ANY] * n,
        out_shape=[jax.ShapeDtypeStruct(b.shape, b.dtype) for b in bufs],
        input_output_aliases={k: k for k in range(n)},
        scratch_shapes=[pltpu.SemaphoreType.DMA((n, 8)), pltpu.SemaphoreType.DMA((n, 8))],
    )(*bufs)


HBM = pl.BlockSpec(memory_space=pltpu.HBM)
SEM = pl.BlockSpec(memory_space=pltpu.SEMAPHORE)
EFFECT = pltpu.SideEffectType.DATAFLOW_SIDE_EFFECTING


def _split_start(name, arrays, build, n_copies, after=()):
    na, nd = len(arrays), len(after)

    def body(*refs):
        send_sems, recv_sems = refs[na + nd], refs[na + nd + 1]
        for cp in build(refs[:na], send_sems, recv_sems):
            cp.start()
        refs[-1][...] = jnp.zeros((8, 128), F32)

    outs = pl.pallas_call(
        body, name=name,
        out_shape=(pltpu.SemaphoreType.DMA((n_copies,)), pltpu.SemaphoreType.DMA((n_copies,)),
                   *[pltpu.HBM(a.shape, a.dtype) for a in arrays], jax.ShapeDtypeStruct((8, 128), F32)),
        in_specs=[HBM] * na + [ANY] * nd,
        out_specs=(SEM, SEM, *[HBM] * na, pl.BlockSpec(memory_space=pltpu.VMEM)),
        input_output_aliases={i: 2 + i for i in range(na)},
        compiler_params=pltpu.CompilerParams(has_side_effects=EFFECT),
    )(*[pltpu.with_memory_space_constraint(a, pltpu.HBM) for a in arrays], *after)
    return outs[0], outs[1], list(outs[2:2 + na]), outs[-1]


def _split_wait(name, send_sems, recv_sems, arrays, build, after):
    na = len(arrays)

    def body(*refs):
        for cp in build(refs[:na], refs[na], refs[na + 1]):
            cp.wait_send()
            cp.wait_recv()

    outs = pl.pallas_call(
        body, name=name,
        out_shape=tuple(pltpu.HBM(a.shape, a.dtype) for a in arrays),
        in_specs=[HBM] * na + [SEM, SEM] + [ANY] * len(after),
        out_specs=tuple([HBM] * na),
        input_output_aliases={i: i for i in range(na)},
        compiler_params=pltpu.CompilerParams(has_side_effects=EFFECT),
    )(*arrays, send_sems, recv_sems, *after)
    return list(outs)


def _remote(src, dst, send_sems, recv_sems, i, to):
    return pltpu.make_async_remote_copy(src_ref=src, dst_ref=dst, send_sem=send_sems.at[i], recv_sem=recv_sems.at[i],
                                        device_id=to, device_id_type=MESH)


def _build_ag_first(refs, ss, rs):
    x, y, c = _place()
    cps = []
    for k, ref in enumerate(refs):
        blk = ref.at[4 * x + 2 * y + c]
        cps += [_remote(blk, blk, ss, rs, 2 * k, (1 - x, y, c)), _remote(blk, blk, ss, rs, 2 * k + 1, (x, 1 - y, c))]
    return cps


def _build_ag_ici(refs, ss, rs):
    x, y, c = _place()
    cps = []
    for k, ref in enumerate(refs):
        blk = ref.at[4 * x + 2 * y + c]
        for j, (px, py) in enumerate(_other_chips(x, y)):
            cps.append(_remote(blk, blk, ss, rs, 3 * k + j, (px, py, c)))
    return cps


def _build_ag_fwd(refs, ss, rs):
    x, y, c = _place()
    cps = []
    for k, ref in enumerate(refs):
        for j, (px, py) in enumerate(_other_chips(x, y)):
            blk = ref.at[4 * px + 2 * py + c]
            cps.append(_remote(blk, blk, ss, rs, 3 * k + j, (x, y, 1 - c)))
    return cps


def _build_rs_swap(refs, ss, rs):
    x, y, c = _place()
    n = len(refs) // 2
    return [_remote(refs[k].at[:, pl.ds(1 - c, 1)], refs[n + k], ss, rs, k, (x, y, 1 - c)) for k in range(n)]


def _build_rs_ici(refs, ss, rs):
    x, y, c = _place()
    n = len(refs) // 2
    me = 2 * x + y
    cps = []
    for k in range(n):
        for j, (px, py) in enumerate(_other_chips(x, y)):
            cps.append(_remote(refs[k].at[2 * px + py], refs[n + k].at[me], ss, rs, 3 * k + j, (px, py, c)))
    return cps


def _build_rs_share(refs, ss, rs):
    x, y, c = _place()
    return [_remote(ref.at[c], ref.at[c], ss, rs, k, (x, y, 1 - c)) for k, ref in enumerate(refs)]


def _build_small_gather(refs, ss, rs):
    x, y, c = _place()
    me = 4 * x + 2 * y + c
    cps = []
    for d in range(1, N_DEV):
        to = (1 - x if d & 4 else x, 1 - y if d & 2 else y, 1 - c if d & 1 else c)
        cps.append(_remote(refs[0], refs[1].at[me], ss, rs, d - 1, to))
    return cps


def _sum_gathered(mine, landed, me_arr):
    R, C = mine.shape

    def body(me_ref, m_ref, l_ref, o_ref):
        me = me_ref[0]
        s = None
        for d in range(N_DEV):
            t = jnp.where(me == d, m_ref[...], l_ref[d])
            s = t if s is None else s + t
        o_ref[...] = s

    grid_spec = pltpu.PrefetchScalarGridSpec(
        num_scalar_prefetch=1, grid=(1,),
        in_specs=[pl.BlockSpec((R, C), lambda i, me_ref: (0, 0)),
                  pl.BlockSpec((N_DEV, R, C), lambda i, me_ref: (0, 0, 0))],
        out_specs=pl.BlockSpec((R, C), lambda i, me_ref: (0, 0)))
    return pl.pallas_call(
        body, name="sum_small", grid_spec=grid_spec,
        out_shape=jax.ShapeDtypeStruct((R, C), F32),
        compiler_params=_cparams(("arbitrary",)),
    )(me_arr, mine, landed)


def _rs_add_pair(p, r0, c_arr, name):
    _, _, hr, cols = p.shape
    tr = _tile(hr, 256, 8)

    def body(c_ref, p_ref, r_ref, q_ref):
        q_ref[...] = (p_ref[0].astype(F32) + r_ref[0].astype(F32)).astype(BF16)

    grid_spec = pltpu.PrefetchScalarGridSpec(
        num_scalar_prefetch=1, grid=(N_CHIPS, hr // tr),
        in_specs=[pl.BlockSpec((1, 1, tr, cols), lambda j, i, c_ref: (j, c_ref[0], i, 0)),
                  pl.BlockSpec((1, 1, tr, cols), lambda j, i, c_ref: (j, 0, i, 0))],
        out_specs=pl.BlockSpec((1, tr, cols), lambda j, i, c_ref: (j, i, 0)))
    return pl.pallas_call(
        body, name=name, grid_spec=grid_spec,
        out_shape=jax.ShapeDtypeStruct((N_CHIPS, hr, cols), BF16),
        compiler_params=_cparams(("parallel", "parallel")),
    )(c_arr, p, r0)


def _rs_add_chips(r1, q, place_arr, name):
    _, hr, cols = r1.shape
    tr = _tile(hr, 256, 8)

    def body(place_ref, r_ref, q_ref, o_ref):
        chip = place_ref[0]
        s = None
        for j in range(N_CHIPS):
            t = jnp.where(chip == j, q_ref[j], r_ref[j]).astype(F32)
            s = t if s is None else s + t
        o_ref[...] = s

    blk = pl.BlockSpec((N_CHIPS, tr, cols), lambda i, place_ref: (0, i, 0))
    grid_spec = pltpu.PrefetchScalarGridSpec(
        num_scalar_prefetch=1, grid=(hr // tr,), in_specs=[blk, blk],
        out_specs=pl.BlockSpec((None, tr, cols), lambda i, place_ref: (place_ref[1], i, 0)))
    return pl.pallas_call(
        body, name=name, grid_spec=grid_spec,
        out_shape=jax.ShapeDtypeStruct((2, hr, cols), F32),
        compiler_params=_cparams(("parallel",)),
    )(place_arr, r1, q)


def _pad_rows(a, rows):
    return jnp.pad(a, ((0, rows - a.shape[0]), (0, 0)))


def _pad_cols(a, cols):
    return jnp.pad(a, ((0, 0), (0, cols - a.shape[1])))


def _heads_to_rows(v):
    v = v.reshape(N_GROUPS, HEADS_PER_GROUP, 1)
    v = jnp.pad(v, ((0, 0), (0, 8 - HEADS_PER_GROUP), (0, 0)))
    return jnp.broadcast_to(v, (N_GROUPS, 8, CHUNK))


def _rows_to_heads(a):
    return jnp.sum(a[:, :HEADS_PER_GROUP, :], axis=-1).reshape(N_HEADS)


def _to_kernel_rows(a):
    C = a.shape[1]
    x0, b0, c0, s0 = D_SSM, 2 * D_SSM, 2 * D_SSM + 1024, D_SSM + D_XBC + N_HEADS
    xbc = jnp.concatenate([a[x0:b0].reshape(N_GROUPS, GW, C), a[b0:c0].reshape(N_GROUPS, N_STATE, C),
                           a[c0:c0 + 1024].reshape(N_GROUPS, N_STATE, C)], axis=1).reshape(D_XBC, C)
    sc = jnp.concatenate([a[s0 + k * D_MODEL:s0 + (k + 1) * D_MODEL].reshape(D_MODEL // SCB, SCB, C)
                          for k in range(3)], axis=1).reshape(3 * D_MODEL, C)
    return jnp.concatenate([a[:D_SSM], xbc, sc], axis=0)


HR_IN = 1568


def _shard_row_plan():
    segs = [(0, 0, 0, D_SSM)]
    for g in range(N_GROUPS):
        k0 = D_SSM + g * GXBC
        segs += [(0, k0, D_SSM + g * GW, GW), (0, k0 + GW, 2 * D_SSM + g * N_STATE, N_STATE),
                 (0, k0 + GW + N_STATE, 2 * D_SSM + 1024 + g * N_STATE, N_STATE)]
    segs.append((1, 0, D_SSM + D_XBC, N_HEADS))
    for j in range(D_MODEL // SCB):
        for k in range(3):
            segs.append((0, D_SSM + D_XBC + j * SC3 + k * SCB, D_SSM + D_XBC + N_HEADS + k * D_MODEL + j * SCB, SCB))
    cs = D_IN // N_CHIPS
    plan = []
    for src, s, o, n in segs:
        while n > 0:
            chip, loc = divmod(o, cs)
            half, row = divmod(loc, HR_IN)
            m = min(n, cs - loc, HR_IN - row)
            plan.append((src, s, chip, half, row, m))
            s, o, n = s + m, o + m, n - m
    return plan


SCATTER_ROWS = 512
SCATTER_SLOTS = 4


def _scatter_rows_to_shards(k_main, k_dt):
    C = k_main.shape[1]
    pieces = []
    for src, s, chip, half, row, n in _shard_row_plan():
        for o in range(0, n, SCATTER_ROWS):
            pieces.append((src, s + o, chip, half, row + o, min(SCATTER_ROWS, n - o)))
    S, lag, N = SCATTER_SLOTS, SCATTER_SLOTS // 2, len(pieces)

    def body(m_ref, d_ref, o_ref, buf, in_sems, out_sems):
        def cin(i):
            src, s, _, _, _, n = pieces[i]
            return pltpu.make_async_copy((d_ref if src else m_ref).at[pl.ds(s, n)],
                                         buf.at[i % S, pl.ds(0, n)], in_sems.at[i % S])

        def cout(i):
            _, _, chip, half, row, n = pieces[i]
            return pltpu.make_async_copy(buf.at[i % S, pl.ds(0, n)],
                                         o_ref.at[chip, half, pl.ds(row, n)], out_sems.at[i % S])

        for i in range(N + lag):
            if i < N:
                if i >= S:
                    cout(i - S).wait()
                cin(i).start()
            j = i - lag
            if 0 <= j < N:
                cin(j).wait()
                cout(j).start()
        for j in range(max(0, N - S), N):
            cout(j).wait()

    return pl.pallas_call(
        body, name="scatter_dw_in_rows", in_specs=[ANY, ANY], out_specs=ANY,
        out_shape=jax.ShapeDtypeStruct((N_CHIPS, 2, HR_IN, C), k_main.dtype),
        scratch_shapes=[pltpu.VMEM((S, SCATTER_ROWS, C), k_main.dtype),
                        pltpu.SemaphoreType.DMA((S,)), pltpu.SemaphoreType.DMA((S,))],
        compiler_params=_cparams(),
    )(k_main, k_dt)


def _to_kernel_xbc(a):
    R = a.shape[0]
    return jnp.concatenate([a[:, :D_SSM].reshape(R, N_GROUPS, GW), a[:, D_SSM:D_SSM + 1024].reshape(R, N_GROUPS, N_STATE),
                            a[:, D_SSM + 1024:].reshape(R, N_GROUPS, N_STATE)], axis=2).reshape(R, D_XBC)


def _from_kernel_xbc(a):
    R = a.shape[0]
    g = a.reshape(R, N_GROUPS, GXBC)
    return jnp.concatenate([g[:, :, :GW].reshape(R, D_SSM), g[:, :, GW:GW + N_STATE].reshape(R, 1024),
                            g[:, :, GW + N_STATE:].reshape(R, 1024)], axis=1)


def kernel(x, norm_mix_g, w_in, ssm_conv_w, ssm_conv_b, ssm_dt_bias, ssm_A_log, ssm_D, ssm_norm_g, sc_conv_w, w_out, norm_ffn_g, w_gate, w_up, w_down, norm_final_g, loss_target, m_norm_mix_g, m_w_in, m_ssm_conv_w, m_ssm_conv_b, m_ssm_dt_bias, m_ssm_A_log, m_ssm_D, m_ssm_norm_g, m_sc_conv_w, m_w_out, m_norm_ffn_g, m_w_gate, m_w_up, m_w_down, m_norm_final_g, v_norm_mix_g, v_w_in, v_ssm_conv_w, v_ssm_conv_b, v_ssm_dt_bias, v_ssm_A_log, v_ssm_D, v_ssm_norm_g, v_sc_conv_w, v_w_out, v_norm_ffn_g, v_w_gate, v_w_up, v_w_down, v_norm_final_g):
    T = x.shape[1]
    xt = x[0]
    tgt = loss_target[0]
    cx, cy, cc = lax.axis_index("x"), lax.axis_index("y"), lax.axis_index("c")
    chip = 2 * cx + cy
    c_arr = jnp.reshape(cc, (1,)).astype(jnp.int32)
    chip_arr = jnp.reshape(chip, (1,)).astype(jnp.int32)
    place_arr = jnp.stack([chip, cc]).astype(jnp.int32)

    big = [w_in[0].T, w_out[0], w_gate[0], w_up[0], w_down[0]]
    names = ["w_in", "w_out", "w_gate", "w_up", "w_down"]
    gb_in = _cast_into_gather(big[0], chip_arr, "cast_w_in", split_cols=True)
    cs_in, cs_conv = D_IN // N_CHIPS, D_XBC // N_CHIPS
    cw = jnp.stack([_pad_rows(ssm_conv_w[0], 8), _pad_cols(_pad_rows(sc_conv_w[0], 8), cs_conv)])
    cw_buf = lax.dynamic_update_slice(jnp.zeros((N_DEV, 8, cs_conv), F32), cw, (2 * chip, 0, 0))
    f_ss, f_rs, f_arr, f_tok = _split_start("ag_in_first_start", [gb_in, cw_buf], _build_ag_first, 4)
    gbufs = [None] + [_cast_into_gather(w, chip_arr, "cast_" + nm, deps=[f_tok]) for w, nm in zip(big[1:], names[1:])]
    n1 = _rmsnorm_fwd(xt, _tie(norm_mix_g, f_tok, "tie_ag_first"), "rmsnorm_mix")
    f_arr = _split_wait("ag_in_first_wait", f_ss, f_rs, f_arr, _build_ag_first, after=gbufs[1:] + [n1])
    g_in, cw_all = _allgather_inplace(f_arr, [("rows", (cs_in // 32) * 16), ("cols", cs_conv // 2)], first_done=True)
    cw_all = cw_all.reshape(N_CHIPS, 2, 8, cs_conv)
    ssm_w8 = _to_kernel_xbc(cw_all[:, 0].transpose(1, 0, 2).reshape(8, D_XBC))
    sc_w8 = cw_all[:, 1, :, :D_MODEL // N_CHIPS].transpose(1, 0, 2).reshape(8, D_MODEL)
    ssm_bk = _to_kernel_xbc(ssm_conv_b)
    wt = g_in.reshape(N_CHIPS, 2, cs_in, D_MODEL // 2).transpose(0, 2, 1, 3).reshape(D_IN, D_MODEL)
    wt_main = _to_kernel_rows(wt)
    wt_dt = _pad_rows(wt[D_SSM + D_XBC:D_SSM + D_XBC + N_HEADS], DT_PAD)
    ag_ss, ag_rs, ag_bufs, ag_tok = _split_start("ag_ici_start", gbufs[1:], _build_ag_ici, 12, after=[g_in, cw_all])

    bias_rows = _heads_to_rows(ssm_dt_bias[0])
    alog_rows = _heads_to_rows(ssm_A_log[0])
    drep = jnp.repeat(ssm_D[0], HEADDIM).reshape(1, D_SSM)

    (proj,) = _matmul([(n1, wt_main)], tb=True, out_dtypes=[F32], name="mm_proj", deps=[ag_tok])
    (dt_raw,) = _matmul([(n1, wt_dt)], tb=True, out_dtypes=[F32], name="mm_proj_dt")
    xbc = _ssm_conv_fwd(proj, ssm_w8, ssm_bk)
    dtr = jnp.pad(dt_raw[:, :N_HEADS].T.reshape(N_GROUPS, HEADS_PER_GROUP, T), ((0, 0), (0, 4), (0, 0)))
    y_ssd, hs = _ssd_fwd(xbc, dtr, bias_rows, alog_rows, drep)
    ag_bufs = _split_wait("ag_ici_wait", ag_ss, ag_rs, ag_bufs, _build_ag_ici, after=[y_ssd])
    fw_ss, fw_rs, fw_bufs, fw_tok = _split_start("ag_fwd_start", ag_bufs, _build_ag_fwd, 12)
    y_mix = _shortconv_fwd(proj, sc_w8, _gated_norm_fwd(y_ssd, proj, _tie(ssm_norm_g, fw_tok, "tie_ag_fwd")))
    gath = _split_wait("ag_fwd_wait", fw_ss, fw_rs, fw_bufs, _build_ag_fwd, after=[y_mix])
    w_out_f = gath[0].reshape(2 * D_MODEL, D_MODEL)
    w_gate3 = gath[1].reshape(N_CHIPS, D_MODEL, D_FF // N_CHIPS)
    w_up3 = gath[2].reshape(N_CHIPS, D_MODEL, D_FF // N_CHIPS)
    w_down_f = gath[3].reshape(D_FF, D_MODEL)
    (h1,) = _matmul([(y_mix, w_out_f)], out_dtypes=[F32], name="mm_out", extras=[xt],
                    epilogue=lambda acc, res: (acc + res,))
    n2 = _rmsnorm_fwd(h1, norm_ffn_g, "rmsnorm_ffn")
    g_act, u_act, a_act = _ffn_fwd(n2, w_gate3, w_up3)
    (h2,) = _matmul([(a_act, w_down_f)], out_dtypes=[F32], name="mm_down", extras=[h1],
                    epilogue=lambda acc, res: (acc + res,))

    dh2, dh2b, dg_final, loss_part = _loss_and_final_bwd(h2, tgt, norm_final_g.reshape(1, D_MODEL))
    dg_act, du_act = _matmul([(dh2b, w_down_f)], tb=True, out_dtypes=[BF16, BF16], name="mm_down_bwd",
                             tn=512, extras=[g_act, u_act], epilogue=_swiglu_bwd, nsub=2)
    (dw_down,) = _matmul([(a_act, dh2b)], ta=True, out_dtypes=[BF16], name="mm_dw_down", tm=1408, tn=512)
    (dn2,) = _matmul([(dg_act, w_gate3), (du_act, w_up3)], tb=True, b3d=True, out_dtypes=[BF16],
                     name="mm_ffn_in_bwd")
    (dw_gate,) = _matmul([(n2, dg_act)], ta=True, out_dtypes=[BF16], name="mm_dw_gate", tm=512, tn=1408,
                         col_shards=True)
    (dw_up,) = _matmul([(n2, du_act)], ta=True, out_dtypes=[BF16], name="mm_dw_up", tm=512, tn=1408,
                       col_shards=True)
    dh1, dh1b, dg_ffn = _rmsnorm_bwd(dn2, h1, norm_ffn_g, dh2, "rmsnorm_ffn_bwd")
    (dw_out,) = _matmul([(y_mix, dh1b)], ta=True, out_dtypes=[BF16], name="mm_dw_out")

    def halves(g):
        return g.reshape(N_CHIPS, 2, g.shape[1] // 2, g.shape[2])

    def landing(shape, dtype):
        return lax.empty(shape, dtype)

    names1 = names[1:]
    ps1 = [halves(dw_out.reshape(N_CHIPS, -1, D_MODEL)), halves(dw_gate), halves(dw_up),
           halves(dw_down.reshape(N_CHIPS, -1, D_MODEL))]
    r0_1 = [landing((N_CHIPS, 1) + p.shape[2:], p.dtype) for p in ps1]
    sw_ss, sw_rs, sw_arr, sw_tok = _split_start("rs1_swap_start", ps1 + r0_1, _build_rs_swap, 4)
    (dmix,) = _matmul([(dh1b, w_out_f)], tb=True, out_dtypes=[BF16], name="mm_out_bwd", deps=[sw_tok])
    dproj, dw_sc = _shortconv_bwd(dmix, proj, sc_w8)
    dy_ssd, dproj, dg_ssmnorm = _gated_norm_bwd(dmix, y_ssd, proj, ssm_norm_g, dproj)
    sw_arr = _split_wait("rs1_swap_wait", sw_ss, sw_rs, sw_arr, _build_rs_swap, after=[dy_ssd])
    qs1 = [_rs_add_pair(p, r, c_arr, "rs_add_pair_" + nm) for p, r, nm in zip(sw_arr[:4], sw_arr[4:], names1)]
    r1_1 = [landing(q.shape, BF16) for q in qs1]
    ic_ss, ic_rs, ic_arr, ic_tok = _split_start("rs1_ici_start", qs1 + r1_1, _build_rs_ici, 12)
    dxbc_act, ddtr, dbias_acc, dalog_acc, dD_acc = _ssd_bwd(
        xbc, dtr, bias_rows, alog_rows, _tie(drep, ic_tok, "tie_rs1_ici"), dy_ssd, hs)
    dproj, dw_ssmconv, db_ssmconv = _ssm_conv_bwd(dxbc_act, proj, ssm_w8, ssm_bk, dproj)
    dw_ssmconv, db_ssmconv = _from_kernel_xbc(dw_ssmconv), _from_kernel_xbc(db_ssmconv)
    ddt_raw = _pad_cols(ddtr[:, :HEADS_PER_GROUP, :].reshape(N_HEADS, T).T, DT_PAD).astype(BF16)
    (dwt_main,) = _matmul([(dproj, n1)], ta=True, out_dtypes=[F32], name="mm_dw_main")
    (dwt_dt,) = _matmul([(ddt_raw, n1)], ta=True, out_dtypes=[F32], name="mm_dw_dt")
    ic_arr = _split_wait("rs1_ici_wait", ic_ss, ic_rs, ic_arr, _build_rs_ici, after=[dwt_main])
    g1 = [_rs_add_chips(r, q, place_arr, "rs_add_chips_" + nm) for q, r, nm in zip(ic_arr[:4], ic_arr[4:], names1)]
    sh_ss, sh_rs, sh_arr, sh_tok = _split_start("rs1_share_start", g1, _build_rs_share, 4)
    p_in = _scatter_rows_to_shards(dwt_main, dwt_dt)
    s2_ss, s2_rs, s2_arr, s2_tok = _split_start(
        "rs2_swap_start", [p_in, landing((N_CHIPS, 1) + p_in.shape[2:], F32)], _build_rs_swap, 1)
    tm_pb = 1024
    mt = T // _tile(T, tm_pb)
    mt_a = max(mt // 4, 1)
    (dn1a,) = _matmul([(dproj, wt_main)], out_dtypes=[F32], name="mm_proj_bwd_a", deps=[s2_tok], tm=tm_pb,
                      m_tiles=(0, mt_a))
    g1 = _split_wait("rs1_share_wait", sh_ss, sh_rs, sh_arr, _build_rs_share, after=[dn1a])
    s2_arr = _split_wait("rs2_swap_wait", s2_ss, s2_rs, s2_arr, _build_rs_swap, after=[dn1a])
    q_in = _rs_add_pair(s2_arr[0], s2_arr[1], c_arr, "rs_add_pair_w_in")
    i2_ss, i2_rs, i2_arr, i2_tok = _split_start(
        "rs2_ici_start", [q_in, landing(q_in.shape, BF16)], _build_rs_ici, 3)
    if mt > mt_a:
        (dn1a,) = _matmul([(dproj, wt_main)], out_dtypes=[F32], name="mm_proj_bwd_b", deps=[i2_tok], tm=tm_pb,
                          m_tiles=(mt_a, mt - mt_a), out_buf=dn1a)
    (dn1,) = _matmul([(ddt_raw, wt_dt)], out_dtypes=[BF16], name="mm_proj_dt_bwd", extras=[dn1a],
                     epilogue=lambda acc, res: (acc + res,), deps=[i2_tok])
    dx, _, dg_mix = _rmsnorm_bwd(dn1, xt, norm_mix_g, dh1, "rmsnorm_mix_bwd")

    big_m = [m_w_in[0].T, m_w_out[0], m_w_gate[0], m_w_up[0], m_w_down[0]]
    big_v = [v_w_in[0].T, v_w_out[0], v_w_gate[0], v_w_up[0], v_w_down[0]]
    big_grads = [None] + [g.reshape(w.shape) for g, w in zip(g1, big[1:])]
    big_out = {}
    for k in range(1, 5):
        *big_out[names[k]], big_grads[k] = _adamw(big[k], big_grads[k], big_m[k], big_v[k], "adamw_" + names[k],
                                                   deps=[i2_tok], emit_g=True)
    i2_arr = _split_wait("rs2_ici_wait", i2_ss, i2_rs, i2_arr, _build_rs_ici, after=[big_out[names[4]][0], dx])
    g_in_red = _rs_add_chips(i2_arr[1], i2_arr[0], place_arr, "rs_add_chips_w_in")
    s3_ss, s3_rs, s3_arr, s3_tok = _split_start("rs2_share_start", [g_in_red], _build_rs_share, 1)

    dD = jnp.sum(dD_acc.reshape(N_HEADS, HEADDIM), axis=-1)
    heads_row = jnp.concatenate([_rows_to_heads(dbias_acc), _rows_to_heads(dalog_acc), dD,
                                 loss_part.reshape(1)]).reshape(1, -1)
    small = jnp.concatenate([
        dw_ssmconv,
        _pad_cols(dw_sc, D_XBC),
        db_ssmconv,
        jnp.concatenate([dg_mix, dg_ssmnorm], axis=1),
        jnp.concatenate([dg_ffn, dg_final], axis=1),
        _pad_cols(heads_row, D_XBC),
        jnp.zeros((4, D_XBC), F32),
    ], axis=0)
    sm_ss, sm_rs, sm_arr, sm_tok = _split_start(
        "small_gather_start", [small, landing((N_DEV,) + small.shape, F32)], _build_small_gather, N_DEV - 1,
        after=[s3_tok])
    (g_in_full,) = _split_wait("rs2_share_wait", s3_ss, s3_rs, s3_arr, _build_rs_share, after=[sm_tok])
    d_t, m_t, v_t, g_t = _adamw(big[0], g_in_full.reshape(2 * HR_IN, D_MODEL), big_m[0], big_v[0],
                                "adamw_" + names[0], emit_g=True)
    big_grads[0] = g_t.T
    big_out[names[0]] = (d_t.T, m_t.T, v_t.T)
    sm_arr = _split_wait("small_gather_wait", sm_ss, sm_rs, sm_arr, _build_small_gather, after=[d_t])
    tot = _sum_gathered(sm_arr[0], sm_arr[1], jnp.reshape(4 * cx + 2 * cy + cc, (1,)).astype(jnp.int32))
    loss = tot[19, 3 * N_HEADS]

    cs_ssm, cs_sc = D_XBC // N_CHIPS, D_MODEL // N_CHIPS
    g_ssm_conv = lax.dynamic_slice(tot[0:K_SSM], (0, chip * cs_ssm), (K_SSM, cs_ssm))
    g_sc_conv = lax.dynamic_slice(tot[8:8 + K_SC, :D_MODEL], (0, chip * cs_sc), (K_SC, cs_sc))
    small_grads = {
        "norm_mix_g": tot[17:18, :D_MODEL], "ssm_conv_w": g_ssm_conv, "ssm_conv_b": tot[16:17],
        "ssm_dt_bias": tot[19:20, 0:N_HEADS], "ssm_A_log": tot[19:20, N_HEADS:2 * N_HEADS],
        "ssm_D": tot[19:20, 2 * N_HEADS:3 * N_HEADS], "ssm_norm_g": tot[17:18, D_MODEL:],
        "sc_conv_w": g_sc_conv, "norm_ffn_g": tot[18:19, :D_MODEL], "norm_final_g": tot[18:19, D_MODEL:],
    }
    small_w = {"norm_mix_g": (norm_mix_g, m_norm_mix_g, v_norm_mix_g),
               "ssm_conv_w": (ssm_conv_w[0], m_ssm_conv_w[0], v_ssm_conv_w[0]),
               "ssm_conv_b": (ssm_conv_b, m_ssm_conv_b, v_ssm_conv_b),
               "ssm_dt_bias": (ssm_dt_bias, m_ssm_dt_bias, v_ssm_dt_bias),
               "ssm_A_log": (ssm_A_log, m_ssm_A_log, v_ssm_A_log),
               "ssm_D": (ssm_D, m_ssm_D, v_ssm_D),
               "ssm_norm_g": (ssm_norm_g, m_ssm_norm_g, v_ssm_norm_g),
               "sc_conv_w": (sc_conv_w[0], m_sc_conv_w[0], v_sc_conv_w[0]),
               "norm_ffn_g": (norm_ffn_g, m_norm_ffn_g, v_norm_ffn_g),
               "norm_final_g": (norm_final_g.reshape(1, -1), m_norm_final_g.reshape(1, -1),
                                v_norm_final_g.reshape(1, -1))}
    PW = 1024
    order = list(small_w)

    def pack(arrs):
        rows = []
        for a in arrs:
            flat = a.reshape(-1)
            n = -(-flat.shape[0] // PW) * PW
            rows.append(jnp.pad(flat, (0, n - flat.shape[0])).reshape(-1, PW))
        slab = jnp.concatenate(rows, axis=0)
        return _pad_rows(slab, -(-slab.shape[0] // 8) * 8)

    wp = pack([small_w[k][0] for k in order])
    mp = pack([small_w[k][1] for k in order])
    vp = pack([small_w[k][2] for k in order])
    gp = pack([small_grads[k] for k in order])
    sd, sm, sv = _adamw(wp, gp, mp, vp, "adamw_small")

    def unpack(slab):
        out, row = {}, 0
        for k in order:
            shape = small_w[k][0].shape
            size = 1
            for s in shape:
                size *= s
            nr = -(-size // PW)
            out[k] = slab[row:row + nr].reshape(-1)[:size].reshape(shape)
            row += nr
        return out

    s_delta, s_m, s_v = unpack(sd), unpack(sm), unpack(sv)

    big_g = dict(zip(names, big_grads))

    weight_order = ["norm_mix_g", "w_in", "ssm_conv_w", "ssm_conv_b", "ssm_dt_bias", "ssm_A_log", "ssm_D",
                    "ssm_norm_g", "sc_conv_w", "w_out", "norm_ffn_g", "w_gate", "w_up", "w_down", "norm_final_g"]
    lead = {"ssm_conv_w", "sc_conv_w", "w_in", "w_out", "w_gate", "w_up", "w_down"}

    def shaped(nm, a):
        if nm == "norm_final_g":
            return a.reshape(D_MODEL)
        return a[None] if nm in lead else a

    grads, deltas, new_m, new_v = [], [], [], []
    for nm in weight_order:
        if nm in big_out:
            g, (d, m, v) = big_g[nm], big_out[nm]
        else:
            g, d, m, v = small_grads[nm], s_delta[nm], s_m[nm], s_v[nm]
        grads.append(shaped(nm, g))
        deltas.append(shaped(nm, d))
        new_m.append(shaped(nm, m))
        new_v.append(shaped(nm, v))
    return (loss, dx[None], *grads, *deltas, *new_m, *new_v)


def _swiglu_bwd(da, dg_factor, du_factor):
    return da * dg_factor.astype(F32), da * du_factor.astype(F32)


def _ffn_fwd(n2, w_gate, w_up):
    T, K = n2.shape
    tn = w_gate.shape[2]
    N = N_CHIPS * tn
    tm = _tile(T, 512)
    sub = _tile(tm, 256)

    def body(a_ref, wg_ref, wu_ref, g_ref, u_ref, act_ref):
        for s in range(tm // sub):
            rows = pl.ds(s * sub, sub)
            a = a_ref[rows, :]
            g = jnp.dot(a, wg_ref[...], preferred_element_type=F32)
            u = jnp.dot(a, wu_ref[...], preferred_element_type=F32)
            sig = _sigmoid(g)
            sg = g * sig
            g_ref[rows, :] = (u * (sig * (1.0 + g - sg))).astype(BF16)
            u_ref[rows, :] = sg.astype(BF16)
            act_ref[rows, :] = (sg * u).astype(BF16)

    a_spec = pl.BlockSpec((tm, K), lambda j, i: (i, 0))
    b_spec = pl.BlockSpec((None, K, tn), lambda j, i: (j, 0, 0))
    o_spec = pl.BlockSpec((tm, tn), lambda j, i: (i, j))
    return pl.pallas_call(
        body, name="ffn_fwd", grid=(N // tn, T // tm),
        in_specs=[a_spec, b_spec, b_spec], out_specs=[o_spec] * 3,
        out_shape=[jax.ShapeDtypeStruct((T, N), BF16)] * 3,
        compiler_params=_cparams(("parallel", "parallel")),
    )(n2, w_gate, w_up)
```

```python
import functools

import jax
import jax.numpy as jnp
from jax import lax
from jax.experimental import pallas as pl
from jax.experimental.pallas import tpu as pltpu

F32 = jnp.float32
BF16 = jnp.bfloat16
MESH = pl.DeviceIdType.MESH

D_MODEL = 2048
D_SSM = 2048
HEADDIM = 64
N_HEADS = 32
N_GROUPS = 8
HEADS_PER_GROUP = 4
N_STATE = 128
CHUNK = 128
K_SSM = 4
K_SC = 3
D_XBC = 4096
D_FF = 5632
D_IN = 12320
D_MAIN = 12288
OFF_XBC, OFF_CB, OFF_CC, OFF_CX = 2048, 6144, 8192, 10240
DT_PAD = 128
EPS = 1e-5
N_CHIPS = 4
N_DEV = 8

ADAM_LR = 0.001
ADAM_B1 = 0.9
ADAM_B2 = 0.999
ADAM_EPS = 1e-08
ADAM_WD = 0.01
ADAM_STEP = 10

V7X_VMEM_BYTES = 64 * 1024 * 1024
VMEM_LIMIT = V7X_VMEM_BYTES - 8 * 1024 * 1024


def _cparams(sem=None):
    if sem is None:
        return pltpu.CompilerParams(vmem_limit_bytes=VMEM_LIMIT)
    return pltpu.CompilerParams(dimension_semantics=sem, vmem_limit_bytes=VMEM_LIMIT)


def _tile(dim, pref, unit=128):
    best = None
    t = unit
    while t <= min(dim, pref):
        if dim % t == 0:
            best = t
        t += unit
    return best if best is not None else dim


def _sigmoid(x):
    return 1.0 / (1.0 + jnp.exp(-x))


def _silu(x):
    return x * _sigmoid(x)


def _dsilu(x):
    s = _sigmoid(x)
    return s * (1.0 + x * (1.0 - s))


def _softplus(x):
    return jnp.maximum(x, 0.0) + jnp.log(1.0 + jnp.exp(-jnp.abs(x)))


MATMUL_VMEM_BUDGET = 44 * 1024 * 1024


def _matmul(pairs, *, ta=False, tb=False, out_dtypes, name, tm=1024, tn=1024, tk=None, extras=(), epilogue=None,
            deps=(), col_shards=False, nsub=1, b3d=False, m_tiles=None, out_buf=None):
    a0, b0 = pairs[0]
    M, K = (a0.shape[1], a0.shape[0]) if ta else a0.shape
    if b3d:
        N = b0.shape[1] if tb else b0.shape[0] * b0.shape[2]
        tk, tn = (b0.shape[2], tn) if tb else (tk, b0.shape[2])
    else:
        N = b0.shape[0] if tb else b0.shape[1]
    tm, tn = _tile(M, tm, 8 if M % 128 else 128), _tile(N, tn)
    npair, nex, ndep, nout = len(pairs), len(extras), len(deps), len(out_dtypes)
    if tk is None:
        fixed = 2 * tm * tn * (sum(jnp.dtype(d).itemsize for d in out_dtypes) + sum(e.dtype.itemsize for e in extras))
        tk = K
        while tk > 128 and (K % tk or tk % 128 or
                            fixed + 2 * npair * 2 * tk * (tm + tn) + (tm * tn * 4 if tk < K else 0) > MATMUL_VMEM_BUDGET):
            tk -= 128
    else:
        tk = _tile(K, tk)
    nk = K // tk
    if nk > 1 or tm % nsub or (tm // nsub) % 128:
        nsub = 1
    sub = tm // nsub
    dims = (((0 if ta else 1,), (1 if tb else 0,)), ((), ()))
    i0, mi = m_tiles if m_tiles is not None else (0, M // tm)
    nbuf = 0 if out_buf is None else 1

    def body(*refs):
        a_refs = refs[0:2 * npair:2]
        b_refs = refs[1:2 * npair:2]
        ex_refs = refs[2 * npair:2 * npair + nex]
        o_refs = refs[2 * npair + nex + ndep + nbuf:2 * npair + nex + ndep + nbuf + nout]

        def dots(rows):
            s = None
            for a_ref, b_ref in zip(a_refs, b_refs):
                a = a_ref[...] if rows is None else (a_ref[:, rows] if ta else a_ref[rows, :])
                d = lax.dot_general(a, b_ref[...], dims, preferred_element_type=F32)
                s = d if s is None else s + d
            return s

        def finish(r, rows):
            ex = [e[...] if rows is None else e[rows, :] for e in ex_refs]
            outs = (r,) if epilogue is None else epilogue(r, *ex)
            for o_ref, o in zip(o_refs, outs):
                if rows is None:
                    o_ref[...] = o.astype(o_ref.dtype)
                else:
                    o_ref[rows, :] = o.astype(o_ref.dtype)

        if nk == 1:
            for s in range(nsub):
                rows = None if nsub == 1 else pl.ds(s * sub, sub)
                finish(dots(rows), rows)
            return

        acc = refs[-1]
        k = pl.program_id(2)

        @pl.when(k == 0)
        def _():
            acc[...] = dots(None)

        @pl.when(jnp.logical_and(k > 0, k < nk - 1))
        def _():
            acc[...] += dots(None)

        @pl.when(k == nk - 1)
        def _():
            finish(acc[...] + dots(None), None)

    a_spec = (pl.BlockSpec((tk, tm), lambda i, j, k: (k, i + i0)) if ta
              else pl.BlockSpec((tm, tk), lambda i, j, k: (i + i0, k)))
    if b3d:
        b_spec = (pl.BlockSpec((None, tn, tk), lambda i, j, k: (k, j, 0)) if tb
                  else pl.BlockSpec((None, tk, tn), lambda i, j, k: (j, k, 0)))
    else:
        b_spec = (pl.BlockSpec((tn, tk), lambda i, j, k: (j, k)) if tb
                  else pl.BlockSpec((tk, tn), lambda i, j, k: (k, j)))
    e_spec = pl.BlockSpec((tm, tn), lambda i, j, k: (i + i0, j))
    if col_shards:
        o_spec = pl.BlockSpec((None, tm, tn), lambda i, j, k: (j, i + i0, 0))
        o_shape = (N // tn, M, tn)
    else:
        o_spec, o_shape = e_spec, (M, N)
    args, in_specs = [], []
    for a, b in pairs:
        args += [a, b]
        in_specs += [a_spec, b_spec]
    args += list(extras) + list(deps) + ([] if out_buf is None else [out_buf])
    in_specs += [e_spec] * nex + [ANY] * (ndep + nbuf)
    outs = pl.pallas_call(
        body,
        name=name,
        grid=(mi, N // tn, nk),
        in_specs=in_specs,
        out_specs=[o_spec] * nout,
        out_shape=[jax.ShapeDtypeStruct(o_shape, dt) for dt in out_dtypes],
        input_output_aliases={} if out_buf is None else {len(args) - 1: 0},
        scratch_shapes=[pltpu.VMEM((tm, tn), F32)] if nk > 1 else [],
        compiler_params=_cparams(("parallel", "parallel", "arbitrary")),
    )(*args)
    return outs


def _cast_into_gather(w, chip_arr, name, split_cols=False, deps=()):
    R, C = w.shape
    hr, hc = (R, C // 2) if split_cols else (R // 2, C)
    tr = _tile(hr, 512, 8)
    nb = hr // tr

    def body(chip_ref, w_ref, *rest):
        rest[-1][...] = w_ref[...].astype(BF16)

    in_map = (lambda h, i, chip_ref: (i, h)) if split_cols else (lambda h, i, chip_ref: (h * nb + i, 0))
    grid_spec = pltpu.PrefetchScalarGridSpec(
        num_scalar_prefetch=1, grid=(2, nb),
        in_specs=[pl.BlockSpec((tr, hc), in_map)] + [ANY] * len(deps),
        out_specs=pl.BlockSpec((None, tr, hc), lambda h, i, chip_ref: (2 * chip_ref[0] + h, i, 0)))
    return pl.pallas_call(
        body, name=name, grid_spec=grid_spec,
        out_shape=jax.ShapeDtypeStruct((N_DEV, hr, hc), BF16),
        compiler_params=_cparams(("parallel", "parallel")),
    )(chip_arr, w, *deps)


def _tie(small, token, name):
    def body(s_ref, t_ref, o_ref):
        o_ref[...] = s_ref[...]

    vm = pl.BlockSpec(memory_space=pltpu.VMEM)
    return pl.pallas_call(body, name=name, in_specs=[vm, ANY], out_specs=vm,
                          out_shape=jax.ShapeDtypeStruct(small.shape, small.dtype))(small, token)


def _rmsnorm_fwd(x, g, name):
    T, D = x.shape
    tt = _tile(T, 256)

    def body(x_ref, g_ref, n_ref):
        xv = x_ref[...]
        r = lax.rsqrt(jnp.mean(xv * xv, axis=-1, keepdims=True) + EPS)
        n_ref[...] = (xv * r * g_ref[...]).astype(BF16)

    return pl.pallas_call(
        body, name=name, grid=(T // tt,),
        in_specs=[pl.BlockSpec((tt, D), lambda i: (i, 0)), pl.BlockSpec((1, D), lambda i: (0, 0))],
        out_specs=pl.BlockSpec((tt, D), lambda i: (i, 0)),
        out_shape=jax.ShapeDtypeStruct((T, D), BF16),
        compiler_params=_cparams(("parallel",)),
    )(x, g)


def _rmsnorm_bwd(dn, x, g, res, name):
    T, D = x.shape
    tt = _tile(T, 256)

    def body(dn_ref, x_ref, g_ref, res_ref, dx_ref, dxb_ref, dg_ref):
        @pl.when(pl.program_id(0) == 0)
        def _():
            dg_ref[...] = jnp.zeros_like(dg_ref)

        xv = x_ref[...]
        dy = dn_ref[...].astype(F32)
        r = lax.rsqrt(jnp.mean(xv * xv, axis=-1, keepdims=True) + EPS)
        xhat = xv * r
        dxh = dy * g_ref[...]
        dx = res_ref[...] + r * (dxh - xhat * jnp.mean(dxh * xhat, axis=-1, keepdims=True))
        dx_ref[...] = dx
        dxb_ref[...] = dx.astype(BF16)
        dg_ref[...] += jnp.sum(dy * xhat, axis=0, keepdims=True)

    tok = pl.BlockSpec((tt, D), lambda i: (i, 0))
    vec = pl.BlockSpec((1, D), lambda i: (0, 0))
    return pl.pallas_call(
        body, name=name, grid=(T // tt,),
        in_specs=[tok, tok, vec, tok],
        out_specs=[tok, tok, vec],
        out_shape=[jax.ShapeDtypeStruct((T, D), F32), jax.ShapeDtypeStruct((T, D), BF16),
                   jax.ShapeDtypeStruct((1, D), F32)],
        compiler_params=_cparams(("arbitrary",)),
    )(dn, x, g, res)


def _loss_and_final_bwd(h2, target, gf):
    T, D = h2.shape
    tt = _tile(T, 256)

    def body(h_ref, t_ref, g_ref, dh_ref, dhb_ref, dg_ref, loss_ref):
        @pl.when(pl.program_id(0) == 0)
        def _():
            dg_ref[...] = jnp.zeros_like(dg_ref)
            loss_ref[...] = jnp.zeros_like(loss_ref)

        xv = h_ref[...]
        r = lax.rsqrt(jnp.mean(xv * xv, axis=-1, keepdims=True) + EPS)
        xhat = xv * r
        err = xhat * g_ref[...] - t_ref[...]
        loss_ref[...] += 0.5 * jnp.sum(jnp.mean(err * err, axis=-1, keepdims=True), axis=0, keepdims=True)
        dy = err * (1.0 / D)
        dxh = dy * g_ref[...]
        dx = r * (dxh - xhat * jnp.mean(dxh * xhat, axis=-1, keepdims=True))
        dh_ref[...] = dx
        dhb_ref[...] = dx.astype(BF16)
        dg_ref[...] += jnp.sum(dy * xhat, axis=0, keepdims=True)

    tok = pl.BlockSpec((tt, D), lambda i: (i, 0))
    vec = pl.BlockSpec((1, D), lambda i: (0, 0))
    return pl.pallas_call(
        body, name="loss_final_bwd", grid=(T // tt,),
        in_specs=[tok, tok, vec],
        out_specs=[tok, tok, vec, pl.BlockSpec((1, 1), lambda i: (0, 0))],
        out_shape=[jax.ShapeDtypeStruct((T, D), F32), jax.ShapeDtypeStruct((T, D), BF16),
                   jax.ShapeDtypeStruct((1, D), F32), jax.ShapeDtypeStruct((1, 1), F32)],
        compiler_params=_cparams(("arbitrary",)),
    )(h2, target, gf)


def _gated_norm_fwd(y, proj, g):
    T, D = y.shape
    tt = _tile(T, 256)

    def body(y_ref, z_ref, g_ref, o_ref):
        yg = y_ref[...] * _silu(z_ref[...])
        r = lax.rsqrt(jnp.mean(yg * yg, axis=-1, keepdims=True) + EPS)
        o_ref[...] = (yg * r * g_ref[...]).astype(BF16)

    tok = pl.BlockSpec((tt, D), lambda i: (i, 0))
    return pl.pallas_call(
        body, name="gated_norm_fwd", grid=(T // tt,),
        in_specs=[tok, tok, pl.BlockSpec((1, D), lambda i: (0, 0))],
        out_specs=tok,
        out_shape=jax.ShapeDtypeStruct((T, 2 * D_MODEL), BF16),
        compiler_params=_cparams(("parallel",)),
    )(y, proj, g)


def _gated_norm_bwd(dmix, y, proj, g, dproj):
    T, D = y.shape
    tt = _tile(T, 256)

    def body(do_ref, y_ref, z_ref, g_ref, dp_ref, dy_ref, dz_ref, dg_ref):
        @pl.when(pl.program_id(0) == 0)
        def _():
            dg_ref[...] = jnp.zeros_like(dg_ref)

        yv, zv = y_ref[...], z_ref[...]
        do = do_ref[...].astype(F32)
        sz = _silu(zv)
        yg = yv * sz
        r = lax.rsqrt(jnp.mean(yg * yg, axis=-1, keepdims=True) + EPS)
        xhat = yg * r
        dxh = do * g_ref[...]
        dyg = r * (dxh - xhat * jnp.mean(dxh * xhat, axis=-1, keepdims=True))
        dy_ref[...] = dyg * sz
        dz_ref[...] = (dyg * yv * _dsilu(zv)).astype(BF16)
        dg_ref[...] += jnp.sum(do * xhat, axis=0, keepdims=True)

    tok = pl.BlockSpec((tt, D), lambda i: (i, 0))
    vec = pl.BlockSpec((1, D), lambda i: (0, 0))
    return pl.pallas_call(
        body, name="gated_norm_bwd", grid=(T // tt,),
        in_specs=[tok, tok, tok, vec, ANY],
        out_specs=[tok, tok, vec],
        out_shape=[jax.ShapeDtypeStruct((T, D), F32), jax.ShapeDtypeStruct(dproj.shape, BF16),
                   jax.ShapeDtypeStruct((1, D), F32)],
        input_output_aliases={4: 1},
        compiler_params=_cparams(("arbitrary",)),
    )(dmix, y, proj, g, dproj)


HALO = 8


def _shift_down(cur, prev8, s):
    ext = jnp.concatenate([prev8, cur], axis=0)
    return pltpu.roll(ext, s, axis=0)[HALO:]


def _shift_up(cur, next8, s):
    n = cur.shape[0]
    ext = jnp.concatenate([cur, next8], axis=0)
    return pltpu.roll(ext, n + HALO - s, axis=0)[:n]


def _conv_specs(tt, cb, col_off_blocks, nt):
    hb = tt // HALO
    cur = pl.BlockSpec((tt, cb), lambda j, i: (i, col_off_blocks + j))
    prev = pl.BlockSpec((HALO, cb), lambda j, i: (jnp.maximum(i * hb - 1, 0), col_off_blocks + j))
    nxt = pl.BlockSpec((HALO, cb), lambda j, i: (jnp.minimum((i + 1) * hb, nt * hb - 1), col_off_blocks + j))
    return cur, prev, nxt


def _taps(cur, prev8, K):
    return [_shift_down(cur, prev8, K - 1 - k) for k in range(K - 1)] + [cur]


def _conv_of_taps(taps, w):
    y = taps[-1] * w[len(taps) - 1:len(taps), :]
    for k, t in enumerate(taps[:-1]):
        y = y + t * w[k:k + 1, :]
    return y


def _causal_conv(cur, prev8, w, K):
    return _conv_of_taps(_taps(cur, prev8, K), w)


def _anticausal_conv(cur, next8, w, K):
    y = cur * w[K - 1:K, :]
    for k in range(K - 1):
        y = y + _shift_up(cur, next8, K - 1 - k) * w[k:k + 1, :]
    return y


def _ssm_conv_fwd(proj, w8, b):
    T = proj.shape[0]
    tt, cb = _tile(T, 512), 512
    nt = T // tt
    cur, prev, _ = _conv_specs(tt, cb, OFF_XBC // cb, nt)

    def body(u_ref, up_ref, w_ref, b_ref, o_ref):
        first = pl.program_id(1) == 0
        p8 = jnp.where(first, 0.0, up_ref[...])
        pre = _causal_conv(u_ref[...], p8, w_ref[...], K_SSM) + b_ref[...]
        o_ref[...] = _silu(pre)

    return pl.pallas_call(
        body, name="ssm_conv_fwd", grid=(D_XBC // cb, nt),
        in_specs=[cur, prev, pl.BlockSpec((8, cb), lambda j, i: (0, j)), pl.BlockSpec((1, cb), lambda j, i: (0, j))],
        out_specs=pl.BlockSpec((tt, cb), lambda j, i: (i, j)),
        out_shape=jax.ShapeDtypeStruct((T, D_XBC), F32),
        compiler_params=_cparams(("parallel", "parallel")),
    )(proj, proj, w8, b)


def _ssm_conv_bwd(dact, proj, w8, b, dproj):
    T = proj.shape[0]
    tt, cb = _tile(T, 512), 512
    nt = T // tt
    cur, prev, nxt = _conv_specs(tt, cb, OFF_XBC // cb, nt)
    dcur, dprev, dnxt = _conv_specs(tt, cb, 0, nt)

    def dpre_of(d, u, p8, w, bb):
        pre = _causal_conv(u, p8, w, K_SSM) + bb
        return d * _dsilu(pre)

    def body(d_ref, dn_ref, u_ref, up_ref, un_ref, w_ref, b_ref, dp_ref, dx_ref, dw_ref, db_ref):
        i = pl.program_id(1)

        @pl.when(i == 0)
        def _():
            dw_ref[...] = jnp.zeros_like(dw_ref)
            db_ref[...] = jnp.zeros_like(db_ref)

        w, bb = w_ref[...], b_ref[...]
        u = u_ref[...]
        p8 = jnp.where(i == 0, 0.0, up_ref[...])
        taps = _taps(u, p8, K_SSM)
        dpre = d_ref[...] * _dsilu(_conv_of_taps(taps, w) + bb)
        un = un_ref[...]
        dpre_n = dpre_of(dn_ref[...], un, u[tt - HALO:, :], w, bb)
        dpre_n = jnp.where(i == nt - 1, 0.0, dpre_n)
        dx_ref[...] = _anticausal_conv(dpre, dpre_n, w, K_SSM).astype(BF16)
        rows = [jnp.sum(dpre * t, axis=0, keepdims=True) for t in taps]
        rows.append(jnp.zeros((8 - K_SSM, cb), F32))
        dw_ref[...] += jnp.concatenate(rows, axis=0)
        db_ref[...] += jnp.sum(dpre, axis=0, keepdims=True)

    wspec = pl.BlockSpec((8, cb), lambda j, i: (0, j))
    bspec = pl.BlockSpec((1, cb), lambda j, i: (0, j))
    return pl.pallas_call(
        body, name="ssm_conv_bwd", grid=(D_XBC // cb, nt),
        in_specs=[dcur, dnxt, cur, prev, nxt, wspec, bspec, ANY],
        out_specs=[pl.BlockSpec((tt, cb), lambda j, i: (i, OFF_XBC // cb + j)), wspec, bspec],
        out_shape=[jax.ShapeDtypeStruct(dproj.shape, BF16), jax.ShapeDtypeStruct((8, D_XBC), F32),
                   jax.ShapeDtypeStruct((1, D_XBC), F32)],
        input_output_aliases={7: 0},
        compiler_params=_cparams(("parallel", "arbitrary")),
    )(dact, dact, proj, proj, proj, w8, b, dproj)


SCB = 512
SC3 = 3 * SCB


def _sc_specs(tt, nt):
    hb = tt // HALO
    cur = pl.BlockSpec((tt, SC3), lambda j, i: (i, OFF_CB // SC3 + j))
    prev = pl.BlockSpec((HALO, SC3), lambda j, i: (jnp.maximum(i * hb - 1, 0), OFF_CB // SC3 + j))
    nxt = pl.BlockSpec((HALO, SC3), lambda j, i: (jnp.minimum((i + 1) * hb, nt * hb - 1), OFF_CB // SC3 + j))
    return cur, prev, nxt


def _shortconv_fwd(proj, w8, ymix):
    T = proj.shape[0]
    tt = _tile(T, 512)
    nt = T // tt
    cur, prev, _ = _sc_specs(tt, nt)

    def body(p_ref, pp_ref, w_ref, y_ref, o_ref):
        p, pp = p_ref[...], pp_ref[...]
        v = p[:, SCB:2 * SCB] * p[:, 2 * SCB:]
        vp = jnp.where(pl.program_id(1) == 0, 0.0, pp[:, SCB:2 * SCB] * pp[:, 2 * SCB:])
        o_ref[...] = (p[:, :SCB] * _causal_conv(v, vp, w_ref[...], K_SC)).astype(BF16)

    return pl.pallas_call(
        body, name="shortconv_fwd", grid=(D_MODEL // SCB, nt),
        in_specs=[cur, prev, pl.BlockSpec((8, SCB), lambda j, i: (0, j)), ANY],
        out_specs=pl.BlockSpec((tt, SCB), lambda j, i: (i, D_SSM // SCB + j)),
        out_shape=jax.ShapeDtypeStruct(ymix.shape, BF16),
        input_output_aliases={3: 0},
        compiler_params=_cparams(("parallel", "parallel")),
    )(proj, proj, w8, ymix)


def _shortconv_bwd(dmix, proj, w8):
    T = proj.shape[0]
    tt = _tile(T, 512)
    nt = T // tt
    hb = tt // HALO
    cur, prev, nxt = _sc_specs(tt, nt)
    d_s = pl.BlockSpec((tt, SCB), lambda j, i: (i, D_SSM // SCB + j))
    dn_s = pl.BlockSpec((HALO, SCB), lambda j, i: (jnp.minimum((i + 1) * hb, nt * hb - 1), D_SSM // SCB + j))

    def body(d_ref, dn_ref, p_ref, pp_ref, pn_ref, w_ref, dp_ref, dw_ref):
        i = pl.program_id(1)

        @pl.when(i == 0)
        def _():
            dw_ref[...] = jnp.zeros_like(dw_ref)

        w = w_ref[...]
        p, pp = p_ref[...], pp_ref[...]
        gb, gc, u = p[:, :SCB], p[:, SCB:2 * SCB], p[:, 2 * SCB:]
        v = gc * u
        vp = jnp.where(i == 0, 0.0, pp[:, SCB:2 * SCB] * pp[:, 2 * SCB:])
        d = d_ref[...].astype(F32)
        taps = _taps(v, vp, K_SC)
        dp_ref[:, :SCB] = (d * _conv_of_taps(taps, w)).astype(BF16)
        dcv = d * gb
        dcv_n = jnp.where(i == nt - 1, 0.0, dn_ref[...].astype(F32) * pn_ref[:, :SCB])
        dv = _anticausal_conv(dcv, dcv_n, w, K_SC)
        dp_ref[:, SCB:2 * SCB] = (dv * u).astype(BF16)
        dp_ref[:, 2 * SCB:] = (dv * gc).astype(BF16)
        rows = [jnp.sum(dcv * t, axis=0, keepdims=True) for t in taps]
        rows.append(jnp.zeros((8 - K_SC, SCB), F32))
        dw_ref[...] += jnp.concatenate(rows, axis=0)

    wspec = pl.BlockSpec((8, SCB), lambda j, i: (0, j))
    return pl.pallas_call(
        body, name="shortconv_bwd", grid=(D_MODEL // SCB, nt),
        in_specs=[d_s, dn_s, cur, prev, nxt, wspec],
        out_specs=[cur, wspec],
        out_shape=[jax.ShapeDtypeStruct((T, D_MAIN), BF16), jax.ShapeDtypeStruct((8, D_MODEL), F32)],
        compiler_params=_cparams(("parallel", "arbitrary")),
    )(dmix, dmix, proj, proj, proj, w8)


GW = HEADS_PER_GROUP * HEADDIM


def _dot(a, b):
    return jnp.dot(a.astype(BF16), b.astype(BF16), preferred_element_type=F32)


def _dot_nt(a, b):
    return lax.dot_general(a.astype(BF16), b.astype(BF16), (((1,), (1,)), ((), ())), preferred_element_type=F32)


def _dot_tn(a, b):
    return lax.dot_general(a.astype(BF16), b.astype(BF16), (((0,), (0,)), ((), ())), preferred_element_type=F32)


def _bf16_terms(x, n):
    terms, r = [], x
    for _ in range(n):
        t = r.astype(BF16)
        terms.append(t)
        r = r - t.astype(F32)
    return terms


def _dot_sel(a, sel, n=2):
    s = sel.astype(BF16)
    return sum(jnp.dot(t, s, preferred_element_type=F32) for t in _bf16_terms(a, n))


def _sel_dot(sel, b, n=2):
    s = sel.astype(BF16)
    return sum(jnp.dot(s, t, preferred_element_type=F32) for t in _bf16_terms(b, n))


def _sel_dot_nt(sel, b, n=2):
    s = sel.astype(BF16)
    return sum(lax.dot_general(s, t, (((1,), (1,)), ((), ())), preferred_element_type=F32)
               for t in _bf16_terms(b, n))


def _head_cols(rows):
    parts = [jnp.broadcast_to(rows[r:r + 1, :], (HEADDIM, CHUNK)) for r in range(HEADS_PER_GROUP)]
    return jnp.concatenate(parts, axis=0).T


def _head_rows(rows):
    parts = [jnp.broadcast_to(rows[r:r + 1, :], (HEADDIM, N_STATE)) for r in range(HEADS_PER_GROUP)]
    return jnp.concatenate(parts, axis=0)


def _ssd_common(dtr, bias, alog):
    dt = _softplus(dtr + bias)
    A = -jnp.exp(alog)
    a = dt * A
    ki = lax.broadcasted_iota(jnp.int32, (CHUNK, CHUNK), 0)
    si = lax.broadcasted_iota(jnp.int32, (CHUNK, CHUNK), 1)
    upper = (ki <= si).astype(F32)
    cs = _dot_sel(a, upper, 3)
    cs_last = jnp.broadcast_to(cs[:, CHUNK - 1:CHUNK], (8, CHUNK))
    return dt, A, a, cs, cs_last


def _decay_matrix(cs, r):
    li = lax.broadcasted_iota(jnp.int32, (CHUNK, CHUNK), 0)
    si = lax.broadcasted_iota(jnp.int32, (CHUNK, CHUNK), 1)
    causal = li >= si
    R = jnp.broadcast_to(cs[r:r + 1, :], (CHUNK, CHUNK))
    seg = jnp.where(causal, R.T - R, 0.0)
    return jnp.where(causal, jnp.exp(seg), 0.0)


def _decay_cat(cs):
    return jnp.concatenate([_decay_matrix(cs, r) for r in range(HEADS_PER_GROUP)], axis=1)


def _lanes4(m):
    return jnp.concatenate([m] * HEADS_PER_GROUP, axis=1)


def _head_blocks(v):
    col = lax.broadcasted_iota(jnp.int32, v.shape, 1) // HEADDIM
    return jnp.concatenate([jnp.where(col == r, v, jnp.zeros_like(v)) for r in range(HEADS_PER_GROUP)], axis=0)


GXBC = GW + 2 * N_STATE


GS_FWD = 8
GS_BWD = 8


def _ssd_in_specs(nc, rev):
    GS = GS_BWD if rev else GS_FWD
    cix = (lambda c: nc - 1 - c) if rev else (lambda c: c)
    x_s = pl.BlockSpec((CHUNK, GS * GW), lambda g, c: (cix(c), g))
    xbc_s = pl.BlockSpec((CHUNK, GS * GXBC), lambda g, c: (cix(c), g))
    dtr_s = pl.BlockSpec((GS, 8, CHUNK), lambda g, c: (g, 0, cix(c)))
    row_s = pl.BlockSpec((GS, 8, CHUNK), lambda g, c: (g, 0, 0))
    drep_s = pl.BlockSpec((1, GS * GW), lambda g, c: (0, g))
    hs_s = pl.BlockSpec((1, GS * GW, N_STATE), lambda g, c: (cix(c), g, 0))
    return x_s, xbc_s, dtr_s, row_s, drep_s, hs_s


def _xbc_parts(xbc_ref, gi):
    o = gi * GXBC
    return xbc_ref[:, o:o + GW], xbc_ref[:, o + GW:o + GW + N_STATE], xbc_ref[:, o + GW + N_STATE:o + GXBC]


def _ssd_fwd(xbc, dtr, bias, alog, drep):
    T = xbc.shape[0]
    nc = T // CHUNK
    x_s, xbc_s, dtr_s, row_s, drep_s, hs_s = _ssd_in_specs(nc, False)

    def body(xbc_ref, dtr_ref, bias_ref, alog_ref, drep_ref, y_ref, hs_ref, h_scr):
        @pl.when(pl.program_id(1) == 0)
        def _():
            h_scr[...] = jnp.zeros_like(h_scr)

        for gi in range(GS_FWD):
            cols, rows = slice(gi * GW, (gi + 1) * GW), pl.ds(gi * GW, GW)
            x, Bm, Cm = _xbc_parts(xbc_ref, gi)
            dt, A, a, cs, cs_last = _ssd_common(dtr_ref[gi], bias_ref[gi], alog_ref[gi])
            E = _head_cols(jnp.exp(cs))
            W = _head_cols(jnp.exp(cs_last - cs) * dt)
            X = (x * _head_cols(dt)).astype(BF16)
            CB = _dot_nt(Cm, Bm)
            col = lax.broadcasted_iota(jnp.int32, (CHUNK, GW), 1) // HEADDIM
            y = jnp.zeros((CHUNK, GW), F32)
            for r in range(HEADS_PER_GROUP):
                y = y + jnp.where(col == r, _dot(CB * _decay_matrix(cs, r), X), 0.0)
            h = h_scr[rows, :]
            hs_ref[0, rows, :] = h
            y = y + _dot_nt(Cm, h) * E
            y_ref[:, cols] = y + drep_ref[:, cols] * x
            h_scr[rows, :] = h * _head_rows(jnp.exp(cs_last)) + _dot_tn(x * W, Bm)

    return pl.pallas_call(
        body, name="ssd_fwd", grid=(N_GROUPS // GS_FWD, nc),
        in_specs=[xbc_s, dtr_s, row_s, row_s, drep_s],
        out_specs=[x_s, hs_s],
        out_shape=[jax.ShapeDtypeStruct((T, D_SSM), F32), jax.ShapeDtypeStruct((nc, D_SSM, N_STATE), F32)],
        scratch_shapes=[pltpu.VMEM((GS_FWD * GW, N_STATE), F32)],
        compiler_params=_cparams(("parallel", "arbitrary")),
    )(xbc, dtr, bias, alog, drep)


def _ssd_bwd(xbc, dtr, bias, alog, drep, dy, hs):
    T = xbc.shape[0]
    nc = T // CHUNK
    x_s, xbc_s, dtr_s, row_s, drep_s, hs_s = _ssd_in_specs(nc, True)

    def body(xbc_ref, dtr_ref, bias_ref, alog_ref, drep_ref, dy_ref, hs_ref,
             dxbc_ref, ddtr_ref, dbias_ref, dalog_ref, dd_ref, dh_scr):
        @pl.when(pl.program_id(1) == 0)
        def _():
            dh_scr[...] = jnp.zeros_like(dh_scr)
            dbias_ref[...] = jnp.zeros_like(dbias_ref)
            dalog_ref[...] = jnp.zeros_like(dalog_ref)
            dd_ref[...] = jnp.zeros_like(dd_ref)

        for gi in range(GS_BWD):
            one_group(gi, xbc_ref, dtr_ref, bias_ref, alog_ref, drep_ref, dy_ref, hs_ref,
                      dxbc_ref, ddtr_ref, dbias_ref, dalog_ref, dd_ref, dh_scr)

    def one_group(gi, xbc_ref, dtr_ref, bias_ref, alog_ref, drep_ref, dy_ref, hs_ref,
                  dxbc_ref, ddtr_ref, dbias_ref, dalog_ref, dd_ref, dh_scr):
        cols, rows, o = slice(gi * GW, (gi + 1) * GW), pl.ds(gi * GW, GW), gi * GXBC
        x, Bm, Cm = _xbc_parts(xbc_ref, gi)
        dY = dy_ref[:, cols]
        dt, A, a, cs, cs_last = _ssd_common(dtr_ref[gi], bias_ref[gi], alog_ref[gi])
        E = _head_cols(jnp.exp(cs))
        DT = _head_cols(dt)
        Wd = _head_cols(jnp.exp(cs_last - cs))
        X = x * DT
        h = hs_ref[0, rows, :]
        dS = dh_scr[rows, :]
        CB = _dot_nt(Cm, Bm)
        rowid = lax.broadcasted_iota(jnp.int32, (8, CHUNK), 0)
        lane = lax.broadcasted_iota(jnp.int32, (8, CHUNK), 1)
        hsel = (lax.broadcasted_iota(jnp.int32, (8, GW), 1) // HEADDIM
                == lax.broadcasted_iota(jnp.int32, (8, GW), 0)).astype(F32)
        hsel_l = (lax.broadcasted_iota(jnp.int32, (8, HEADS_PER_GROUP * CHUNK), 1) // CHUNK
                  == lax.broadcasted_iota(jnp.int32, (8, HEADS_PER_GROUP * CHUNK), 0)).astype(F32)

        Lc, CBc = _decay_cat(cs), _lanes4(CB)
        Mc = CBc * Lc
        GLc = _dot_nt(dY, _head_blocks(X.astype(BF16))) * Lc
        Wc = GLc * CBc
        colsum = jnp.sum(Wc, axis=0, keepdims=True)
        dcs = _sel_dot_nt(hsel_l, Wc)
        dCB = jnp.zeros((CHUNK, CHUNK), F32)
        for r in range(HEADS_PER_GROUP):
            blk = slice(r * CHUNK, (r + 1) * CHUNK)
            dCB = dCB + GLc[:, blk]
            dcs = dcs - jnp.where(rowid == r, colsum[:, blk], 0.0)
        m_stack = jnp.concatenate([Mc[:, r * CHUNK:(r + 1) * CHUNK].astype(BF16) for r in range(HEADS_PER_GROUP)],
                                  axis=0)
        dX = lax.dot_general(m_stack, _head_blocks(dY.astype(BF16)), (((0,), (0,)), ((), ())),
                             preferred_element_type=F32)
        dC = _dot(dCB, Bm)
        dB = _dot_tn(dCB, Cm)
        T1 = _dot_nt(Bm, dS)
        dX = dX + T1 * Wd
        dB = dB + _dot(X * Wd, dS)
        pdec = _sel_dot_nt(hsel, X * T1 * Wd)
        dcs = dcs - pdec
        dlast = jnp.sum(pdec, axis=1, keepdims=True) \
            + jnp.exp(cs_last[:, 0:1]) * jnp.sum(_sel_dot(hsel, dS * h), axis=1, keepdims=True)
        dYE = dY * E
        dC = dC + _dot(dYE, h)
        yoff = _dot_nt(Cm, h) * E
        dcs = dcs + _sel_dot_nt(hsel, dY * yoff)
        dcs = dcs + jnp.where(lane == CHUNK - 1, dlast, 0.0)
        ki = lax.broadcasted_iota(jnp.int32, (CHUNK, CHUNK), 0)
        si = lax.broadcasted_iota(jnp.int32, (CHUNK, CHUNK), 1)
        lower = (ki >= si).astype(F32)
        da = _dot_sel(dcs, lower)
        ddt = da * A + _sel_dot_nt(hsel, dX * x)
        ddtr = ddt * _sigmoid(dtr_ref[gi] + bias_ref[gi])
        ddtr_ref[gi] = ddtr
        dbias_ref[gi] += ddtr
        dalog_ref[gi] += da * a
        dxbc_ref[:, o:o + GW] = dX * DT + drep_ref[:, cols] * dY
        dd_ref[:, cols] += jnp.sum(dY * x, axis=0, keepdims=True)
        dxbc_ref[:, o + GW:o + GW + N_STATE] = dB
        dxbc_ref[:, o + GW + N_STATE:o + GXBC] = dC
        dh_scr[rows, :] = dS * _head_rows(jnp.exp(cs_last)) + _dot_tn(dYE, Cm)

    return pl.pallas_call(
        body, name="ssd_bwd", grid=(N_GROUPS // GS_BWD, nc),
        in_specs=[xbc_s, dtr_s, row_s, row_s, drep_s, x_s, hs_s],
        out_specs=[xbc_s, dtr_s, row_s, row_s, drep_s],
        out_shape=[jax.ShapeDtypeStruct((T, D_XBC), F32),
                   jax.ShapeDtypeStruct((N_GROUPS, 8, T), F32),
                   jax.ShapeDtypeStruct((N_GROUPS, 8, CHUNK), F32),
                   jax.ShapeDtypeStruct((N_GROUPS, 8, CHUNK), F32),
                   jax.ShapeDtypeStruct((1, D_SSM), F32)],
        scratch_shapes=[pltpu.VMEM((GS_BWD * GW, N_STATE), F32)],
        compiler_params=_cparams(("parallel", "arbitrary")),
    )(xbc, dtr, bias, alog, drep, dy, hs)


def _adamw(w, g, m, v, name, deps=(), emit_g=False):
    R, C = w.shape
    tr = _tile(R, 256, 8)
    nd = len(deps)
    nout = 4 if emit_g else 3

    def body(w_ref, g_ref, m_ref, v_ref, *rest):
        outs = rest[nd:]
        gv = g_ref[...]
        mn = ADAM_B1 * m_ref[...] + (1.0 - ADAM_B1) * gv
        vn = ADAM_B2 * v_ref[...] + (1.0 - ADAM_B2) * (gv * gv)
        m_hat = mn / (1.0 - ADAM_B1 ** ADAM_STEP)
        v_hat = vn / (1.0 - ADAM_B2 ** ADAM_STEP)
        outs[0][...] = -ADAM_LR * (m_hat / (jnp.sqrt(v_hat) + ADAM_EPS) + ADAM_WD * w_ref[...])
        outs[1][...] = mn
        outs[2][...] = vn
        if emit_g:
            outs[3][...] = gv

    spec = pl.BlockSpec((tr, C), lambda i: (i, 0))
    return pl.pallas_call(
        body, name=name, grid=(R // tr,),
        in_specs=[spec] * 4 + [ANY] * nd, out_specs=[spec] * nout,
        out_shape=[jax.ShapeDtypeStruct((R, C), F32)] * nout,
        compiler_params=_cparams(("parallel",)),
    )(w, g, m, v, *deps)


ANY = pl.BlockSpec(memory_space=pl.ANY)


def _place():
    x, y, c = lax.axis_index("x"), lax.axis_index("y"), lax.axis_index("c")
    return x, y, c


def _other_chips(x, y):
    return [(1 - x, y), (x, 1 - y), (1 - x, 1 - y)]


def _allgather_inplace(bufs, splits, first_done=False):
    n = len(bufs)

    def body(*refs):
        o_refs = refs[n:2 * n]
        send_sems, recv_sems = refs[2 * n:]
        x, y, c = _place()
        xn, yn, dg, sibling = (1 - x, y), (x, 1 - y), (1 - x, 1 - y), (x, y, 1 - c)

        def blk(k, chip, pc):
            return o_refs[k].at[4 * chip[0] + 2 * chip[1] + pc]

        def part(k, ref, p):
            kind, s = splits[k]
            _, R, C = bufs[k].shape
            if kind == "rows":
                return ref.at[pl.ds(0, s)] if p == 0 else ref.at[pl.ds(s, R - s)]
            return ref.at[:, pl.ds(0, s)] if p == 0 else ref.at[:, pl.ds(s, C - s)]

        def copy(k, slot, ref, to):
            return pltpu.make_async_remote_copy(
                src_ref=ref, dst_ref=ref, send_sem=send_sems.at[k, slot], recv_sem=recv_sems.at[k, slot],
                device_id=to, device_id_type=MESH)

        sent = []

        def send(k, slot, ref, to):
            cp = copy(k, slot, ref, to)
            cp.start()
            sent.append(cp)

        if not first_done:
            for k in range(n):
                send(k, 0, blk(k, (x, y), c), (*xn, c))
                send(k, 1, blk(k, (x, y), c), (*yn, c))
        for k in range(n):
            bx, by = blk(k, xn, c), blk(k, yn, c)
            if not first_done:
                copy(k, 0, bx, sibling).wait_recv()
            send(k, 2, part(k, bx, 0), (*yn, c))
            send(k, 4, bx, sibling)
            if not first_done:
                copy(k, 1, by, sibling).wait_recv()
            send(k, 3, part(k, by, 1), (*xn, c))
            send(k, 5, by, sibling)
        for k in range(n):
            d0, d1 = part(k, blk(k, dg, c), 0), part(k, blk(k, dg, c), 1)
            copy(k, 2, d0, sibling).wait_recv()
            send(k, 6, d0, sibling)
            copy(k, 3, d1, sibling).wait_recv()
            send(k, 7, d1, sibling)
        for k in range(n):
            copy(k, 4, blk(k, xn, 1 - c), sibling).wait_recv()
            copy(k, 5, blk(k, yn, 1 - c), sibling).wait_recv()
            copy(k, 6, part(k, blk(k, dg, 1 - c), 0), sibling).wait_recv()
            copy(k, 7, part(k, blk(k, dg, 1 - c), 1), sibling).wait_recv()
        for cp in sent:
            cp.wait_send()

    return pl.pallas_call(
        body, name="allgather_w_in",
        in_specs=[ANY] * n, out_specs=[ANY] * n,
        out_shape=[jax.ShapeDtypeStruct(b.shape, b.dtype) for b in bufs],
        input_output_aliases={k: k for k in range(n)},
        scratch_shapes=[pltpu.SemaphoreType.DMA((n, 8)), pltpu.SemaphoreType.DMA((n, 8))],
    )(*bufs)


HBM = pl.BlockSpec(memory_space=pltpu.HBM)
SEM = pl.BlockSpec(memory_space=pltpu.SEMAPHORE)
EFFECT = pltpu.SideEffectType.DATAFLOW_SIDE_EFFECTING


def _split_start(name, arrays, build, n_copies, after=()):
    na, nd = len(arrays), len(after)

    def body(*refs):
        send_sems, recv_sems = refs[na + nd], refs[na + nd + 1]
        for cp in build(refs[:na], send_sems, recv_sems):
            cp.start()
        refs[-1][...] = jnp.zeros((8, 128), F32)

    outs = pl.pallas_call(
        body, name=name,
        out_shape=(pltpu.SemaphoreType.DMA((n_copies,)), pltpu.SemaphoreType.DMA((n_copies,)),
                   *[pltpu.HBM(a.shape, a.dtype) for a in arrays], jax.ShapeDtypeStruct((8, 128), F32)),
        in_specs=[HBM] * na + [ANY] * nd,
        out_specs=(SEM, SEM, *[HBM] * na, pl.BlockSpec(memory_space=pltpu.VMEM)),
        input_output_aliases={i: 2 + i for i in range(na)},
        compiler_params=pltpu.CompilerParams(has_side_effects=EFFECT),
    )(*[pltpu.with_memory_space_constraint(a, pltpu.HBM) for a in arrays], *after)
    return outs[0], outs[1], list(outs[2:2 + na]), outs[-1]


def _split_wait(name, send_sems, recv_sems, arrays, build, after):
    na = len(arrays)

    def body(*refs):
        for cp in build(refs[:na], refs[na], refs[na + 1]):
            cp.wait_send()
            cp.wait_recv()

    outs = pl.pallas_call(
        body, name=name,
        out_shape=tuple(pltpu.HBM(a.shape, a.dtype) for a in arrays),
        in_specs=[HBM] * na + [SEM, SEM] + [ANY] * len(after),
        out_specs=tuple([HBM] * na),
        input_output_aliases={i: i for i in range(na)},
        compiler_params=pltpu.CompilerParams(has_side_effects=EFFECT),
    )(*arrays, send_sems, recv_sems, *after)
    return list(outs)


def _remote(src, dst, send_sems, recv_sems, i, to):
    return pltpu.make_async_remote_copy(src_ref=src, dst_ref=dst, send_sem=send_sems.at[i], recv_sem=recv_sems.at[i],
                                        device_id=to, device_id_type=MESH)


def _build_ag_first(refs, ss, rs):
    x, y, c = _place()
    cps = []
    for k, ref in enumerate(refs):
        blk = ref.at[4 * x + 2 * y + c]
        cps += [_remote(blk, blk, ss, rs, 2 * k, (1 - x, y, c)), _remote(blk, blk, ss, rs, 2 * k + 1, (x, 1 - y, c))]
    return cps


def _build_ag_ici(refs, ss, rs):
    x, y, c = _place()
    cps = []
    for k, ref in enumerate(refs):
        blk = ref.at[4 * x + 2 * y + c]
        for j, (px, py) in enumerate(_other_chips(x, y)):
            cps.append(_remote(blk, blk, ss, rs, 3 * k + j, (px, py, c)))
    return cps


def _build_ag_fwd(refs, ss, rs):
    x, y, c = _place()
    cps = []
    for k, ref in enumerate(refs):
        for j, (px, py) in enumerate(_other_chips(x, y)):
            blk = ref.at[4 * px + 2 * py + c]
            cps.append(_remote(blk, blk, ss, rs, 3 * k + j, (x, y, 1 - c)))
    return cps


def _build_rs_swap(refs, ss, rs):
    x, y, c = _place()
    n = len(refs) // 2
    return [_remote(refs[k].at[:, pl.ds(1 - c, 1)], refs[n + k], ss, rs, k, (x, y, 1 - c)) for k in range(n)]


def _build_rs_ici(refs, ss, rs):
    x, y, c = _place()
    n = len(refs) // 2
    me = 2 * x + y
    cps = []
    for k in range(n):
        for j, (px, py) in enumerate(_other_chips(x, y)):
            cps.append(_remote(refs[k].at[2 * px + py], refs[n + k].at[me], ss, rs, 3 * k + j, (px, py, c)))
    return cps


def _build_rs_share(refs, ss, rs):
    x, y, c = _place()
    return [_remote(ref.at[c], ref.at[c], ss, rs, k, (x, y, 1 - c)) for k, ref in enumerate(refs)]


def _build_small_gather(refs, ss, rs):
    x, y, c = _place()
    me = 4 * x + 2 * y + c
    cps = []
    for d in range(1, N_DEV):
        to = (1 - x if d & 4 else x, 1 - y if d & 2 else y, 1 - c if d & 1 else c)
        cps.append(_remote(refs[0], refs[1].at[me], ss, rs, d - 1, to))
    return cps


def _sum_gathered(mine, landed, me_arr):
    R, C = mine.shape

    def body(me_ref, m_ref, l_ref, o_ref):
        me = me_ref[0]
        s = None
        for d in range(N_DEV):
            t = jnp.where(me == d, m_ref[...], l_ref[d])
            s = t if s is None else s + t
        o_ref[...] = s

    grid_spec = pltpu.PrefetchScalarGridSpec(
        num_scalar_prefetch=1, grid=(1,),
        in_specs=[pl.BlockSpec((R, C), lambda i, me_ref: (0, 0)),
                  pl.BlockSpec((N_DEV, R, C), lambda i, me_ref: (0, 0, 0))],
        out_specs=pl.BlockSpec((R, C), lambda i, me_ref: (0, 0)))
    return pl.pallas_call(
        body, name="sum_small", grid_spec=grid_spec,
        out_shape=jax.ShapeDtypeStruct((R, C), F32),
        compiler_params=_cparams(("arbitrary",)),
    )(me_arr, mine, landed)


def _rs_add_pair(p, r0, c_arr, name):
    _, _, hr, cols = p.shape
    tr = _tile(hr, 256, 8)

    def body(c_ref, p_ref, r_ref, q_ref):
        q_ref[...] = (p_ref[0].astype(F32) + r_ref[0].astype(F32)).astype(BF16)

    grid_spec = pltpu.PrefetchScalarGridSpec(
        num_scalar_prefetch=1, grid=(N_CHIPS, hr // tr),
        in_specs=[pl.BlockSpec((1, 1, tr, cols), lambda j, i, c_ref: (j, c_ref[0], i, 0)),
                  pl.BlockSpec((1, 1, tr, cols), lambda j, i, c_ref: (j, 0, i, 0))],
        out_specs=pl.BlockSpec((1, tr, cols), lambda j, i, c_ref: (j, i, 0)))
    return pl.pallas_call(
        body, name=name, grid_spec=grid_spec,
        out_shape=jax.ShapeDtypeStruct((N_CHIPS, hr, cols), BF16),
        compiler_params=_cparams(("parallel", "parallel")),
    )(c_arr, p, r0)


def _rs_add_chips(r1, q, place_arr, name):
    _, hr, cols = r1.shape
    tr = _tile(hr, 256, 8)

    def body(place_ref, r_ref, q_ref, o_ref):
        chip = place_ref[0]
        s = None
        for j in range(N_CHIPS):
            t = jnp.where(chip == j, q_ref[j], r_ref[j]).astype(F32)
            s = t if s is None else s + t
        o_ref[...] = s

    blk = pl.BlockSpec((N_CHIPS, tr, cols), lambda i, place_ref: (0, i, 0))
    grid_spec = pltpu.PrefetchScalarGridSpec(
        num_scalar_prefetch=1, grid=(hr // tr,), in_specs=[blk, blk],
        out_specs=pl.BlockSpec((None, tr, cols), lambda i, place_ref: (place_ref[1], i, 0)))
    return pl.pallas_call(
        body, name=name, grid_spec=grid_spec,
        out_shape=jax.ShapeDtypeStruct((2, hr, cols), F32),
        compiler_params=_cparams(("parallel",)),
    )(place_arr, r1, q)


def _pad_rows(a, rows):
    return jnp.pad(a, ((0, rows - a.shape[0]), (0, 0)))


def _pad_cols(a, cols):
    return jnp.pad(a, ((0, 0), (0, cols - a.shape[1])))


def _heads_to_rows(v):
    v = v.reshape(N_GROUPS, HEADS_PER_GROUP, 1)
    v = jnp.pad(v, ((0, 0), (0, 8 - HEADS_PER_GROUP), (0, 0)))
    return jnp.broadcast_to(v, (N_GROUPS, 8, CHUNK))


def _rows_to_heads(a):
    return jnp.sum(a[:, :HEADS_PER_GROUP, :], axis=-1).reshape(N_HEADS)


def _to_kernel_rows(a):
    C = a.shape[1]
    x0, b0, c0, s0 = D_SSM, 2 * D_SSM, 2 * D_SSM + 1024, D_SSM + D_XBC + N_HEADS
    xbc = jnp.concatenate([a[x0:b0].reshape(N_GROUPS, GW, C), a[b0:c0].reshape(N_GROUPS, N_STATE, C),
                           a[c0:c0 + 1024].reshape(N_GROUPS, N_STATE, C)], axis=1).reshape(D_XBC, C)
    sc = jnp.concatenate([a[s0 + k * D_MODEL:s0 + (k + 1) * D_MODEL].reshape(D_MODEL // SCB, SCB, C)
                          for k in range(3)], axis=1).reshape(3 * D_MODEL, C)
    return jnp.concatenate([a[:D_SSM], xbc, sc], axis=0)


HR_IN = 1568


def _shard_row_plan():
    segs = [(0, 0, 0, D_SSM)]
    for g in range(N_GROUPS):
        k0 = D_SSM + g * GXBC
        segs += [(0, k0, D_SSM + g * GW, GW), (0, k0 + GW, 2 * D_SSM + g * N_STATE, N_STATE),
                 (0, k0 + GW + N_STATE, 2 * D_SSM + 1024 + g * N_STATE, N_STATE)]
    segs.append((1, 0, D_SSM + D_XBC, N_HEADS))
    for j in range(D_MODEL // SCB):
        for k in range(3):
            segs.append((0, D_SSM + D_XBC + j * SC3 + k * SCB, D_SSM + D_XBC + N_HEADS + k * D_MODEL + j * SCB, SCB))
    cs = D_IN // N_CHIPS
    plan = []
    for src, s, o, n in segs:
        while n > 0:
            chip, loc = divmod(o, cs)
            half, row = divmod(loc, HR_IN)
            m = min(n, cs - loc, HR_IN - row)
            plan.append((src, s, chip, half, row, m))
            s, o, n = s + m, o + m, n - m
    return plan


SCATTER_ROWS = 512
SCATTER_SLOTS = 4


def _scatter_rows_to_shards(k_main, k_dt):
    C = k_main.shape[1]
    pieces = []
    for src, s, chip, half, row, n in _shard_row_plan():
        for o in range(0, n, SCATTER_ROWS):
            pieces.append((src, s + o, chip, half, row + o, min(SCATTER_ROWS, n - o)))
    S, lag, N = SCATTER_SLOTS, SCATTER_SLOTS // 2, len(pieces)

    def body(m_ref, d_ref, o_ref, buf, in_sems, out_sems):
        def cin(i):
            src, s, _, _, _, n = pieces[i]
            return pltpu.make_async_copy((d_ref if src else m_ref).at[pl.ds(s, n)],
                                         buf.at[i % S, pl.ds(0, n)], in_sems.at[i % S])

        def cout(i):
            _, _, chip, half, row, n = pieces[i]
            return pltpu.make_async_copy(buf.at[i % S, pl.ds(0, n)],
                                         o_ref.at[chip, half, pl.ds(row, n)], out_sems.at[i % S])

        for i in range(N + lag):
            if i < N:
                if i >= S:
                    cout(i - S).wait()
                cin(i).start()
            j = i - lag
            if 0 <= j < N:
                cin(j).wait()
                cout(j).start()
        for j in range(max(0, N - S), N):
            cout(j).wait()

    return pl.pallas_call(
        body, name="scatter_dw_in_rows", in_specs=[ANY, ANY], out_specs=ANY,
        out_shape=jax.ShapeDtypeStruct((N_CHIPS, 2, HR_IN, C), k_main.dtype),
        scratch_shapes=[pltpu.VMEM((S, SCATTER_ROWS, C), k_main.dtype),
                        pltpu.SemaphoreType.DMA((S,)), pltpu.SemaphoreType.DMA((S,))],
        compiler_params=_cparams(),
    )(k_main, k_dt)


def _to_kernel_xbc(a):
    R = a.shape[0]
    return jnp.concatenate([a[:, :D_SSM].reshape(R, N_GROUPS, GW), a[:, D_SSM:D_SSM + 1024].reshape(R, N_GROUPS, N_STATE),
                            a[:, D_SSM + 1024:].reshape(R, N_GROUPS, N_STATE)], axis=2).reshape(R, D_XBC)


def _from_kernel_xbc(a):
    R = a.shape[0]
    g = a.reshape(R, N_GROUPS, GXBC)
    return jnp.concatenate([g[:, :, :GW].reshape(R, D_SSM), g[:, :, GW:GW + N_STATE].reshape(R, 1024),
                            g[:, :, GW + N_STATE:].reshape(R, 1024)], axis=1)


def kernel(x, norm_mix_g, w_in, ssm_conv_w, ssm_conv_b, ssm_dt_bias, ssm_A_log, ssm_D, ssm_norm_g, sc_conv_w, w_out, norm_ffn_g, w_gate, w_up, w_down, norm_final_g, loss_target, m_norm_mix_g, m_w_in, m_ssm_conv_w, m_ssm_conv_b, m_ssm_dt_bias, m_ssm_A_log, m_ssm_D, m_ssm_norm_g, m_sc_conv_w, m_w_out, m_norm_ffn_g, m_w_gate, m_w_up, m_w_down, m_norm_final_g, v_norm_mix_g, v_w_in, v_ssm_conv_w, v_ssm_conv_b, v_ssm_dt_bias, v_ssm_A_log, v_ssm_D, v_ssm_norm_g, v_sc_conv_w, v_w_out, v_norm_ffn_g, v_w_gate, v_w_up, v_w_down, v_norm_final_g):
    T = x.shape[1]
    xt = x[0]
    tgt = loss_target[0]
    cx, cy, cc = lax.axis_index("x"), lax.axis_index("y"), lax.axis_index("c")
    chip = 2 * cx + cy
    c_arr = jnp.reshape(cc, (1,)).astype(jnp.int32)
    chip_arr = jnp.reshape(chip, (1,)).astype(jnp.int32)
    place_arr = jnp.stack([chip, cc]).astype(jnp.int32)

    big = [w_in[0].T, w_out[0], w_gate[0], w_up[0], w_down[0]]
    names = ["w_in", "w_out", "w_gate", "w_up", "w_down"]
    gb_in = _cast_into_gather(big[0], chip_arr, "cast_w_in", split_cols=True)
    cs_in, cs_conv = D_IN // N_CHIPS, D_XBC // N_CHIPS
    cw = jnp.stack([_pad_rows(ssm_conv_w[0], 8), _pad_cols(_pad_rows(sc_conv_w[0], 8), cs_conv)])
    cw_buf = lax.dynamic_update_slice(jnp.zeros((N_DEV, 8, cs_conv), F32), cw, (2 * chip, 0, 0))
    f_ss, f_rs, f_arr, f_tok = _split_start("ag_in_first_start", [gb_in, cw_buf], _build_ag_first, 4)
    gbufs = [None] + [_cast_into_gather(w, chip_arr, "cast_" + nm, deps=[f_tok]) for w, nm in zip(big[1:], names[1:])]
    n1 = _rmsnorm_fwd(xt, _tie(norm_mix_g, f_tok, "tie_ag_first"), "rmsnorm_mix")
    f_arr = _split_wait("ag_in_first_wait", f_ss, f_rs, f_arr, _build_ag_first, after=gbufs[1:] + [n1])
    g_in, cw_all = _allgather_inplace(f_arr, [("rows", (cs_in // 32) * 16), ("cols", cs_conv // 2)], first_done=True)
    cw_all = cw_all.reshape(N_CHIPS, 2, 8, cs_conv)
    ssm_w8 = _to_kernel_xbc(cw_all[:, 0].transpose(1, 0, 2).reshape(8, D_XBC))
    sc_w8 = cw_all[:, 1, :, :D_MODEL // N_CHIPS].transpose(1, 0, 2).reshape(8, D_MODEL)
    ssm_bk = _to_kernel_xbc(ssm_conv_b)
    wt = g_in.reshape(N_CHIPS, 2, cs_in, D_MODEL // 2).transpose(0, 2, 1, 3).reshape(D_IN, D_MODEL)
    wt_main = _to_kernel_rows(wt)
    wt_dt = _pad_rows(wt[D_SSM + D_XBC:D_SSM + D_XBC + N_HEADS], DT_PAD)
    ag_ss, ag_rs, ag_bufs, ag_tok = _split_start("ag_ici_start", gbufs[1:], _build_ag_ici, 12, after=[g_in, cw_all])

    bias_rows = _heads_to_rows(ssm_dt_bias[0])
    alog_rows = _heads_to_rows(ssm_A_log[0])
    drep = jnp.repeat(ssm_D[0], HEADDIM).reshape(1, D_SSM)

    (proj,) = _matmul([(n1, wt_main)], tb=True, out_dtypes=[F32], name="mm_proj", deps=[ag_tok])
    (dt_raw,) = _matmul([(n1, wt_dt)], tb=True, out_dtypes=[F32], name="mm_proj_dt")
    xbc = _ssm_conv_fwd(proj, ssm_w8, ssm_bk)
    dtr = jnp.pad(dt_raw[:, :N_HEADS].T.reshape(N_GROUPS, HEADS_PER_GROUP, T), ((0, 0), (0, 4), (0, 0)))
    y_ssd, hs = _ssd_fwd(xbc, dtr, bias_rows, alog_rows, drep)
    ag_bufs = _split_wait("ag_ici_wait", ag_ss, ag_rs, ag_bufs, _build_ag_ici, after=[y_ssd])
    fw_ss, fw_rs, fw_bufs, fw_tok = _split_start("ag_fwd_start", ag_bufs, _build_ag_fwd, 12)
    y_mix = _shortconv_fwd(proj, sc_w8, _gated_norm_fwd(y_ssd, proj, _tie(ssm_norm_g, fw_tok, "tie_ag_fwd")))
    gath = _split_wait("ag_fwd_wait", fw_ss, fw_rs, fw_bufs, _build_ag_fwd, after=[y_mix])
    w_out_f = gath[0].reshape(2 * D_MODEL, D_MODEL)
    w_gate3 = gath[1].reshape(N_CHIPS, D_MODEL, D_FF // N_CHIPS)
    w_up3 = gath[2].reshape(N_CHIPS, D_MODEL, D_FF // N_CHIPS)
    w_down_f = gath[3].reshape(D_FF, D_MODEL)
    (h1,) = _matmul([(y_mix, w_out_f)], out_dtypes=[F32], name="mm_out", extras=[xt],
                    epilogue=lambda acc, res: (acc + res,))
    n2 = _rmsnorm_fwd(h1, norm_ffn_g, "rmsnorm_ffn")
    g_act, u_act, a_act = _ffn_fwd(n2, w_gate3, w_up3)
    (h2,) = _matmul([(a_act, w_down_f)], out_dtypes=[F32], name="mm_down", extras=[h1],
                    epilogue=lambda acc, res: (acc + res,))

    dh2, dh2b, dg_final, loss_part = _loss_and_final_bwd(h2, tgt, norm_final_g.reshape(1, D_MODEL))
    dg_act, du_act = _matmul([(dh2b, w_down_f)], tb=True, out_dtypes=[BF16, BF16], name="mm_down_bwd",
                             tn=512, extras=[g_act, u_act], epilogue=_swiglu_bwd, nsub=2)
    (dw_down,) = _matmul([(a_act, dh2b)], ta=True, out_dtypes=[BF16], name="mm_dw_down", tm=1408, tn=512)
    (dn2,) = _matmul([(dg_act, w_gate3), (du_act, w_up3)], tb=True, b3d=True, out_dtypes=[BF16],
                     name="mm_ffn_in_bwd")
    (dw_gate,) = _matmul([(n2, dg_act)], ta=True, out_dtypes=[BF16], name="mm_dw_gate", tm=512, tn=1408,
                         col_shards=True)
    (dw_up,) = _matmul([(n2, du_act)], ta=True, out_dtypes=[BF16], name="mm_dw_up", tm=512, tn=1408,
                       col_shards=True)
    dh1, dh1b, dg_ffn = _rmsnorm_bwd(dn2, h1, norm_ffn_g, dh2, "rmsnorm_ffn_bwd")
    (dw_out,) = _matmul([(y_mix, dh1b)], ta=True, out_dtypes=[BF16], name="mm_dw_out")

    def halves(g):
        return g.reshape(N_CHIPS, 2, g.shape[1] // 2, g.shape[2])

    def landing(shape, dtype):
        return lax.empty(shape, dtype)

    names1 = names[1:]
    ps1 = [halves(dw_out.reshape(N_CHIPS, -1, D_MODEL)), halves(dw_gate), halves(dw_up),
           halves(dw_down.reshape(N_CHIPS, -1, D_MODEL))]
    r0_1 = [landing((N_CHIPS, 1) + p.shape[2:], p.dtype) for p in ps1]
    sw_ss, sw_rs, sw_arr, sw_tok = _split_start("rs1_swap_start", ps1 + r0_1, _build_rs_swap, 4)
    (dmix,) = _matmul([(dh1b, w_out_f)], tb=True, out_dtypes=[BF16], name="mm_out_bwd", deps=[sw_tok])
    dproj, dw_sc = _shortconv_bwd(dmix, proj, sc_w8)
    dy_ssd, dproj, dg_ssmnorm = _gated_norm_bwd(dmix, y_ssd, proj, ssm_norm_g, dproj)
    sw_arr = _split_wait("rs1_swap_wait", sw_ss, sw_rs, sw_arr, _build_rs_swap, after=[dy_ssd])
    qs1 = [_rs_add_pair(p, r, c_arr, "rs_add_pair_" + nm) for p, r, nm in zip(sw_arr[:4], sw_arr[4:], names1)]
    r1_1 = [landing(q.shape, BF16) for q in qs1]
    ic_ss, ic_rs, ic_arr, ic_tok = _split_start("rs1_ici_start", qs1 + r1_1, _build_rs_ici, 12)
    dxbc_act, ddtr, dbias_acc, dalog_acc, dD_acc = _ssd_bwd(
        xbc, dtr, bias_rows, alog_rows, _tie(drep, ic_tok, "tie_rs1_ici"), dy_ssd, hs)
    dproj, dw_ssmconv, db_ssmconv = _ssm_conv_bwd(dxbc_act, proj, ssm_w8, ssm_bk, dproj)
    dw_ssmconv, db_ssmconv = _from_kernel_xbc(dw_ssmconv), _from_kernel_xbc(db_ssmconv)
    ddt_raw = _pad_cols(ddtr[:, :HEADS_PER_GROUP, :].reshape(N_HEADS, T).T, DT_PAD).astype(BF16)
    (dwt_main,) = _matmul([(dproj, n1)], ta=True, out_dtypes=[F32], name="mm_dw_main")
    (dwt_dt,) = _matmul([(ddt_raw, n1)], ta=True, out_dtypes=[F32], name="mm_dw_dt")
    ic_arr = _split_wait("rs1_ici_wait", ic_ss, ic_rs, ic_arr, _build_rs_ici, after=[dwt_main])
    g1 = [_rs_add_chips(r, q, place_arr, "rs_add_chips_" + nm) for q, r, nm in zip(ic_arr[:4], ic_arr[4:], names1)]
    sh_ss, sh_rs, sh_arr, sh_tok = _split_start("rs1_share_start", g1, _build_rs_share, 4)
    p_in = _scatter_rows_to_shards(dwt_main, dwt_dt)
    s2_ss, s2_rs, s2_arr, s2_tok = _split_start(
        "rs2_swap_start", [p_in, landing((N_CHIPS, 1) + p_in.shape[2:], F32)], _build_rs_swap, 1)
    tm_pb = 1024
    mt = T // _tile(T, tm_pb)
    mt_a = max(mt // 4, 1)
    (dn1a,) = _matmul([(dproj, wt_main)], out_dtypes=[F32], name="mm_proj_bwd_a", deps=[s2_tok], tm=tm_pb,
                      m_tiles=(0, mt_a))
    s2_arr = _split_wait("rs2_swap_wait", s2_ss, s2_rs, s2_arr, _build_rs_swap, after=[dn1a])
    q_in = _rs_add_pair(s2_arr[0], s2_arr[1], c_arr, "rs_add_pair_w_in")
    i2_ss, i2_rs, i2_arr, i2_tok = _split_start(
        "rs2_ici_start", [q_in, landing(q_in.shape, BF16)], _build_rs_ici, 3)
    if mt > mt_a:
        (dn1a,) = _matmul([(dproj, wt_main)], out_dtypes=[F32], name="mm_proj_bwd_b", deps=[i2_tok], tm=tm_pb,
                          m_tiles=(mt_a, mt - mt_a), out_buf=dn1a)
    (dn1,) = _matmul([(ddt_raw, wt_dt)], out_dtypes=[BF16], name="mm_proj_dt_bwd", extras=[dn1a],
                     epilogue=lambda acc, res: (acc + res,), deps=[i2_tok])
    dx, _, dg_mix = _rmsnorm_bwd(dn1, xt, norm_mix_g, dh1, "rmsnorm_mix_bwd")
    g1 = _split_wait("rs1_share_wait", sh_ss, sh_rs, sh_arr, _build_rs_share, after=[dx])

    big_m = [m_w_in[0].T, m_w_out[0], m_w_gate[0], m_w_up[0], m_w_down[0]]
    big_v = [v_w_in[0].T, v_w_out[0], v_w_gate[0], v_w_up[0], v_w_down[0]]
    big_grads = [None] + [g.reshape(w.shape) for g, w in zip(g1, big[1:])]
    big_out = {}
    for k in range(1, 5):
        *big_out[names[k]], big_grads[k] = _adamw(big[k], big_grads[k], big_m[k], big_v[k], "adamw_" + names[k],
                                                   deps=[i2_tok], emit_g=True)
    i2_arr = _split_wait("rs2_ici_wait", i2_ss, i2_rs, i2_arr, _build_rs_ici, after=[big_out[names[4]][0], dx])
    g_in_red = _rs_add_chips(i2_arr[1], i2_arr[0], place_arr, "rs_add_chips_w_in")
    s3_ss, s3_rs, s3_arr, s3_tok = _split_start("rs2_share_start", [g_in_red], _build_rs_share, 1)

    dD = jnp.sum(dD_acc.reshape(N_HEADS, HEADDIM), axis=-1)
    heads_row = jnp.concatenate([_rows_to_heads(dbias_acc), _rows_to_heads(dalog_acc), dD,
                                 loss_part.reshape(1)]).reshape(1, -1)
    small = jnp.concatenate([
        dw_ssmconv,
        _pad_cols(dw_sc, D_XBC),
        db_ssmconv,
        jnp.concatenate([dg_mix, dg_ssmnorm], axis=1),
        jnp.concatenate([dg_ffn, dg_final], axis=1),
        _pad_cols(heads_row, D_XBC),
        jnp.zeros((4, D_XBC), F32),
    ], axis=0)
    sm_ss, sm_rs, sm_arr, sm_tok = _split_start(
        "small_gather_start", [small, landing((N_DEV,) + small.shape, F32)], _build_small_gather, N_DEV - 1,
        after=[s3_tok])
    (g_in_full,) = _split_wait("rs2_share_wait", s3_ss, s3_rs, s3_arr, _build_rs_share, after=[sm_tok])
    d_t, m_t, v_t, g_t = _adamw(big[0], g_in_full.reshape(2 * HR_IN, D_MODEL), big_m[0], big_v[0],
                                "adamw_" + names[0], emit_g=True)
    big_grads[0] = g_t.T
    big_out[names[0]] = (d_t.T, m_t.T, v_t.T)
    sm_arr = _split_wait("small_gather_wait", sm_ss, sm_rs, sm_arr, _build_small_gather, after=[d_t])
    tot = _sum_gathered(sm_arr[0], sm_arr[1], jnp.reshape(4 * cx + 2 * cy + cc, (1,)).astype(jnp.int32))
    loss = tot[19, 3 * N_HEADS]

    cs_ssm, cs_sc = D_XBC // N_CHIPS, D_MODEL // N_CHIPS
    g_ssm_conv = lax.dynamic_slice(tot[0:K_SSM], (0, chip * cs_ssm), (K_SSM, cs_ssm))
    g_sc_conv = lax.dynamic_slice(tot[8:8 + K_SC, :D_MODEL], (0, chip * cs_sc), (K_SC, cs_sc))
    small_grads = {
        "norm_mix_g": tot[17:18, :D_MODEL], "ssm_conv_w": g_ssm_conv, "ssm_conv_b": tot[16:17],
        "ssm_dt_bias": tot[19:20, 0:N_HEADS], "ssm_A_log": tot[19:20, N_HEADS:2 * N_HEADS],
        "ssm_D": tot[19:20, 2 * N_HEADS:3 * N_HEADS], "ssm_norm_g": tot[17:18, D_MODEL:],
        "sc_conv_w": g_sc_conv, "norm_ffn_g": tot[18:19, :D_MODEL], "norm_final_g": tot[18:19, D_MODEL:],
    }
    small_w = {"norm_mix_g": (norm_mix_g, m_norm_mix_g, v_norm_mix_g),
               "ssm_conv_w": (ssm_conv_w[0], m_ssm_conv_w[0], v_ssm_conv_w[0]),
               "ssm_conv_b": (ssm_conv_b, m_ssm_conv_b, v_ssm_conv_b),
               "ssm_dt_bias": (ssm_dt_bias, m_ssm_dt_bias, v_ssm_dt_bias),
               "ssm_A_log": (ssm_A_log, m_ssm_A_log, v_ssm_A_log),
               "ssm_D": (ssm_D, m_ssm_D, v_ssm_D),
               "ssm_norm_g": (ssm_norm_g, m_ssm_norm_g, v_ssm_norm_g),
               "sc_conv_w": (sc_conv_w[0], m_sc_conv_w[0], v_sc_conv_w[0]),
               "norm_ffn_g": (norm_ffn_g, m_norm_ffn_g, v_norm_ffn_g),
               "norm_final_g": (norm_final_g.reshape(1, -1), m_norm_final_g.reshape(1, -1),
                                v_norm_final_g.reshape(1, -1))}
    PW = 1024
    order = list(small_w)

    def pack(arrs):
        rows = []
        for a in arrs:
            flat = a.reshape(-1)
            n = -(-flat.shape[0] // PW) * PW
            rows.append(jnp.pad(flat, (0, n - flat.shape[0])).reshape(-1, PW))
        slab = jnp.concatenate(rows, axis=0)
        return _pad_rows(slab, -(-slab.shape[0] // 8) * 8)

    wp = pack([small_w[k][0] for k in order])
    mp = pack([small_w[k][1] for k in order])
    vp = pack([small_w[k][2] for k in order])
    gp = pack([small_grads[k] for k in order])
    sd, sm, sv = _adamw(wp, gp, mp, vp, "adamw_small")

    def unpack(slab):
        out, row = {}, 0
        for k in order:
            shape = small_w[k][0].shape
            size = 1
            for s in shape:
                size *= s
            nr = -(-size // PW)
            out[k] = slab[row:row + nr].reshape(-1)[:size].reshape(shape)
            row += nr
        return out

    s_delta, s_m, s_v = unpack(sd), unpack(sm), unpack(sv)

    big_g = dict(zip(names, big_grads))

    weight_order = ["norm_mix_g", "w_in", "ssm_conv_w", "ssm_conv_b", "ssm_dt_bias", "ssm_A_log", "ssm_D",
                    "ssm_norm_g", "sc_conv_w", "w_out", "norm_ffn_g", "w_gate", "w_up", "w_down", "norm_final_g"]
    lead = {"ssm_conv_w", "sc_conv_w", "w_in", "w_out", "w_gate", "w_up", "w_down"}

    def shaped(nm, a):
        if nm == "norm_final_g":
            return a.reshape(D_MODEL)
        return a[None] if nm in lead else a

    grads, deltas, new_m, new_v = [], [], [], []
    for nm in weight_order:
        if nm in big_out:
            g, (d, m, v) = big_g[nm], big_out[nm]
        else:
            g, d, m, v = small_grads[nm], s_delta[nm], s_m[nm], s_v[nm]
        grads.append(shaped(nm, g))
        deltas.append(shaped(nm, d))
        new_m.append(shaped(nm, m))
        new_v.append(shaped(nm, v))
    return (loss, dx[None], *grads, *deltas, *new_m, *new_v)


def _swiglu_bwd(da, dg_factor, du_factor):
    return da * dg_factor.astype(F32), da * du_factor.astype(F32)


def _ffn_fwd(n2, w_gate, w_up):
    T, K = n2.shape
    tn = w_gate.shape[2]
    N = N_CHIPS * tn
    tm = _tile(T, 512)
    sub = _tile(tm, 256)

    def body(a_ref, wg_ref, wu_ref, g_ref, u_ref, act_ref):
        for s in range(tm // sub):
            rows = pl.ds(s * sub, sub)
            a = a_ref[rows, :]
            g = jnp.dot(a, wg_ref[...], preferred_element_type=F32)
            u = jnp.dot(a, wu_ref[...], preferred_element_type=F32)
            sig = _sigmoid(g)
            sg = g * sig
            g_ref[rows, :] = (u * (sig * (1.0 + g - sg))).astype(BF16)
            u_ref[rows, :] = sg.astype(BF16)
            act_ref[rows, :] = (sg * u).astype(BF16)

    a_spec = pl.BlockSpec((tm, K), lambda j, i: (i, 0))
    b_spec = pl.BlockSpec((None, K, tn), lambda j, i: (j, 0, 0))
    o_spec = pl.BlockSpec((tm, tn), lambda j, i: (i, j))
    return pl.pallas_call(
        body, name="ffn_fwd", grid=(N // tn, T // tm),
        in_specs=[a_spec, b_spec, b_spec], out_specs=[o_spec] * 3,
        out_shape=[jax.ShapeDtypeStruct((T, N), BF16)] * 3,
        compiler_params=_cparams(("parallel", "parallel")),
    )(n2, w_gate, w_up)
```

```python
import functools

import jax
import jax.numpy as jnp
from jax import lax
from jax.experimental import pallas as pl
from jax.experimental.pallas import tpu as pltpu

F32 = jnp.float32
BF16 = jnp.bfloat16
MESH = pl.DeviceIdType.MESH

D_MODEL = 2048
D_SSM = 2048
HEADDIM = 64
N_HEADS = 32
N_GROUPS = 8
HEADS_PER_GROUP = 4
N_STATE = 128
CHUNK = 128
K_SSM = 4
K_SC = 3
D_XBC = 4096
D_FF = 5632
D_IN = 12320
D_MAIN = 12288
OFF_XBC, OFF_CB, OFF_CC, OFF_CX = 2048, 6144, 8192, 10240
DT_PAD = 128
EPS = 1e-5
N_CHIPS = 4
N_DEV = 8

ADAM_LR = 0.001
ADAM_B1 = 0.9
ADAM_B2 = 0.999
ADAM_EPS = 1e-08
ADAM_WD = 0.01
ADAM_STEP = 10

V7X_VMEM_BYTES = 64 * 1024 * 1024
VMEM_LIMIT = V7X_VMEM_BYTES - 8 * 1024 * 1024


def _cparams(sem=None):
    if sem is None:
        return pltpu.CompilerParams(vmem_limit_bytes=VMEM_LIMIT)
    return pltpu.CompilerParams(dimension_semantics=sem, vmem_limit_bytes=VMEM_LIMIT)


def _tile(dim, pref, unit=128):
    best = None
    t = unit
    while t <= min(dim, pref):
        if dim % t == 0:
            best = t
        t += unit
    return best if best is not None else dim


def _sigmoid(x):
    return 1.0 / (1.0 + jnp.exp(-x))


def _silu(x):
    return x * _sigmoid(x)


def _dsilu(x):
    s = _sigmoid(x)
    return s * (1.0 + x * (1.0 - s))


def _softplus(x):
    return jnp.maximum(x, 0.0) + jnp.log(1.0 + jnp.exp(-jnp.abs(x)))


MATMUL_VMEM_BUDGET = 44 * 1024 * 1024


def _matmul(pairs, *, ta=False, tb=False, out_dtypes, name, tm=1024, tn=1024, tk=None, extras=(), epilogue=None,
            deps=(), col_shards=False, nsub=1, b3d=False, m_tiles=None, out_buf=None):
    a0, b0 = pairs[0]
    M, K = (a0.shape[1], a0.shape[0]) if ta else a0.shape
    if b3d:
        N = b0.shape[1] if tb else b0.shape[0] * b0.shape[2]
        tk, tn = (b0.shape[2], tn) if tb else (tk, b0.shape[2])
    else:
        N = b0.shape[0] if tb else b0.shape[1]
    tm, tn = _tile(M, tm, 8 if M % 128 else 128), _tile(N, tn)
    npair, nex, ndep, nout = len(pairs), len(extras), len(deps), len(out_dtypes)
    if tk is None:
        fixed = 2 * tm * tn * (sum(jnp.dtype(d).itemsize for d in out_dtypes) + sum(e.dtype.itemsize for e in extras))
        tk = K
        while tk > 128 and (K % tk or tk % 128 or
                            fixed + 2 * npair * 2 * tk * (tm + tn) + (tm * tn * 4 if tk < K else 0) > MATMUL_VMEM_BUDGET):
            tk -= 128
    else:
        tk = _tile(K, tk)
    nk = K // tk
    if nk > 1 or tm % nsub or (tm // nsub) % 128:
        nsub = 1
    sub = tm // nsub
    dims = (((0 if ta else 1,), (1 if tb else 0,)), ((), ()))
    i0, mi = m_tiles if m_tiles is not None else (0, M // tm)
    nbuf = 0 if out_buf is None else 1

    def body(*refs):
        a_refs = refs[0:2 * npair:2]
        b_refs = refs[1:2 * npair:2]
        ex_refs = refs[2 * npair:2 * npair + nex]
        o_refs = refs[2 * npair + nex + ndep + nbuf:2 * npair + nex + ndep + nbuf + nout]

        def dots(rows):
            s = None
            for a_ref, b_ref in zip(a_refs, b_refs):
                a = a_ref[...] if rows is None else (a_ref[:, rows] if ta else a_ref[rows, :])
                d = lax.dot_general(a, b_ref[...], dims, preferred_element_type=F32)
                s = d if s is None else s + d
            return s

        def finish(r, rows):
            ex = [e[...] if rows is None else e[rows, :] for e in ex_refs]
            outs = (r,) if epilogue is None else epilogue(r, *ex)
            for o_ref, o in zip(o_refs, outs):
                if rows is None:
                    o_ref[...] = o.astype(o_ref.dtype)
                else:
                    o_ref[rows, :] = o.astype(o_ref.dtype)

        if nk == 1:
            for s in range(nsub):
                rows = None if nsub == 1 else pl.ds(s * sub, sub)
                finish(dots(rows), rows)
            return

        acc = refs[-1]
        k = pl.program_id(2)

        @pl.when(k == 0)
        def _():
            acc[...] = dots(None)

        @pl.when(jnp.logical_and(k > 0, k < nk - 1))
        def _():
            acc[...] += dots(None)

        @pl.when(k == nk - 1)
        def _():
            finish(acc[...] + dots(None), None)

    a_spec = (pl.BlockSpec((tk, tm), lambda i, j, k: (k, i + i0)) if ta
              else pl.BlockSpec((tm, tk), lambda i, j, k: (i + i0, k)))
    if b3d:
        b_spec = (pl.BlockSpec((None, tn, tk), lambda i, j, k: (k, j, 0)) if tb
                  else pl.BlockSpec((None, tk, tn), lambda i, j, k: (j, k, 0)))
    else:
        b_spec = (pl.BlockSpec((tn, tk), lambda i, j, k: (j, k)) if tb
                  else pl.BlockSpec((tk, tn), lambda i, j, k: (k, j)))
    e_spec = pl.BlockSpec((tm, tn), lambda i, j, k: (i + i0, j))
    if col_shards:
        o_spec = pl.BlockSpec((None, tm, tn), lambda i, j, k: (j, i + i0, 0))
        o_shape = (N // tn, M, tn)
    else:
        o_spec, o_shape = e_spec, (M, N)
    args, in_specs = [], []
    for a, b in pairs:
        args += [a, b]
        in_specs += [a_spec, b_spec]
    args += list(extras) + list(deps) + ([] if out_buf is None else [out_buf])
    in_specs += [e_spec] * nex + [ANY] * (ndep + nbuf)
    outs = pl.pallas_call(
        body,
        name=name,
        grid=(mi, N // tn, nk),
        in_specs=in_specs,
        out_specs=[o_spec] * nout,
        out_shape=[jax.ShapeDtypeStruct(o_shape, dt) for dt in out_dtypes],
        input_output_aliases={} if out_buf is None else {len(args) - 1: 0},
        scratch_shapes=[pltpu.VMEM((tm, tn), F32)] if nk > 1 else [],
        compiler_params=_cparams(("parallel", "parallel", "arbitrary")),
    )(*args)
    return outs


def _cast_into_gather(w, chip_arr, name, split_cols=False, deps=()):
    R, C = w.shape
    hr, hc = (R, C // 2) if split_cols else (R // 2, C)
    tr = _tile(hr, 512, 8)
    nb = hr // tr

    def body(chip_ref, w_ref, *rest):
        rest[-1][...] = w_ref[...].astype(BF16)

    in_map = (lambda h, i, chip_ref: (i, h)) if split_cols else (lambda h, i, chip_ref: (h * nb + i, 0))
    grid_spec = pltpu.PrefetchScalarGridSpec(
        num_scalar_prefetch=1, grid=(2, nb),
        in_specs=[pl.BlockSpec((tr, hc), in_map)] + [ANY] * len(deps),
        out_specs=pl.BlockSpec((None, tr, hc), lambda h, i, chip_ref: (2 * chip_ref[0] + h, i, 0)))
    return pl.pallas_call(
        body, name=name, grid_spec=grid_spec,
        out_shape=jax.ShapeDtypeStruct((N_DEV, hr, hc), BF16),
        compiler_params=_cparams(("parallel", "parallel")),
    )(chip_arr, w, *deps)


def _tie(small, token, name):
    def body(s_ref, t_ref, o_ref):
        o_ref[...] = s_ref[...]

    vm = pl.BlockSpec(memory_space=pltpu.VMEM)
    return pl.pallas_call(body, name=name, in_specs=[vm, ANY], out_specs=vm,
                          out_shape=jax.ShapeDtypeStruct(small.shape, small.dtype))(small, token)


def _rmsnorm_fwd(x, g, name):
    T, D = x.shape
    tt = _tile(T, 256)

    def body(x_ref, g_ref, n_ref):
        xv = x_ref[...]
        r = lax.rsqrt(jnp.mean(xv * xv, axis=-1, keepdims=True) + EPS)
        n_ref[...] = (xv * r * g_ref[...]).astype(BF16)

    return pl.pallas_call(
        body, name=name, grid=(T // tt,),
        in_specs=[pl.BlockSpec((tt, D), lambda i: (i, 0)), pl.BlockSpec((1, D), lambda i: (0, 0))],
        out_specs=pl.BlockSpec((tt, D), lambda i: (i, 0)),
        out_shape=jax.ShapeDtypeStruct((T, D), BF16),
        compiler_params=_cparams(("parallel",)),
    )(x, g)


def _rmsnorm_bwd(dn, x, g, res, name):
    T, D = x.shape
    tt = _tile(T, 256)

    def body(dn_ref, x_ref, g_ref, res_ref, dx_ref, dxb_ref, dg_ref):
        @pl.when(pl.program_id(0) == 0)
        def _():
            dg_ref[...] = jnp.zeros_like(dg_ref)

        xv = x_ref[...]
        dy = dn_ref[...].astype(F32)
        r = lax.rsqrt(jnp.mean(xv * xv, axis=-1, keepdims=True) + EPS)
        xhat = xv * r
        dxh = dy * g_ref[...]
        dx = res_ref[...] + r * (dxh - xhat * jnp.mean(dxh * xhat, axis=-1, keepdims=True))
        dx_ref[...] = dx
        dxb_ref[...] = dx.astype(BF16)
        dg_ref[...] += jnp.sum(dy * xhat, axis=0, keepdims=True)

    tok = pl.BlockSpec((tt, D), lambda i: (i, 0))
    vec = pl.BlockSpec((1, D), lambda i: (0, 0))
    return pl.pallas_call(
        body, name=name, grid=(T // tt,),
        in_specs=[tok, tok, vec, tok],
        out_specs=[tok, tok, vec],
        out_shape=[jax.ShapeDtypeStruct((T, D), F32), jax.ShapeDtypeStruct((T, D), BF16),
                   jax.ShapeDtypeStruct((1, D), F32)],
        compiler_params=_cparams(("arbitrary",)),
    )(dn, x, g, res)


def _loss_and_final_bwd(h2, target, gf):
    T, D = h2.shape
    tt = _tile(T, 256)

    def body(h_ref, t_ref, g_ref, dh_ref, dhb_ref, dg_ref, loss_ref):
        @pl.when(pl.program_id(0) == 0)
        def _():
            dg_ref[...] = jnp.zeros_like(dg_ref)
            loss_ref[...] = jnp.zeros_like(loss_ref)

        xv = h_ref[...]
        r = lax.rsqrt(jnp.mean(xv * xv, axis=-1, keepdims=True) + EPS)
        xhat = xv * r
        err = xhat * g_ref[...] - t_ref[...]
        loss_ref[...] += 0.5 * jnp.sum(jnp.mean(err * err, axis=-1, keepdims=True), axis=0, keepdims=True)
        dy = err * (1.0 / D)
        dxh = dy * g_ref[...]
        dx = r * (dxh - xhat * jnp.mean(dxh * xhat, axis=-1, keepdims=True))
        dh_ref[...] = dx
        dhb_ref[...] = dx.astype(BF16)
        dg_ref[...] += jnp.sum(dy * xhat, axis=0, keepdims=True)

    tok = pl.BlockSpec((tt, D), lambda i: (i, 0))
    vec = pl.BlockSpec((1, D), lambda i: (0, 0))
    return pl.pallas_call(
        body, name="loss_final_bwd", grid=(T // tt,),
        in_specs=[tok, tok, vec],
        out_specs=[tok, tok, vec, pl.BlockSpec((1, 1), lambda i: (0, 0))],
        out_shape=[jax.ShapeDtypeStruct((T, D), F32), jax.ShapeDtypeStruct((T, D), BF16),
                   jax.ShapeDtypeStruct((1, D), F32), jax.ShapeDtypeStruct((1, 1), F32)],
        compiler_params=_cparams(("arbitrary",)),
    )(h2, target, gf)


def _gated_norm_fwd(y, proj, g):
    T, D = y.shape
    tt = _tile(T, 256)

    def body(y_ref, z_ref, g_ref, o_ref):
        yg = y_ref[...] * _silu(z_ref[...])
        r = lax.rsqrt(jnp.mean(yg * yg, axis=-1, keepdims=True) + EPS)
        o_ref[...] = (yg * r * g_ref[...]).astype(BF16)

    tok = pl.BlockSpec((tt, D), lambda i: (i, 0))
    return pl.pallas_call(
        body, name="gated_norm_fwd", grid=(T // tt,),
        in_specs=[tok, tok, pl.BlockSpec((1, D), lambda i: (0, 0))],
        out_specs=tok,
        out_shape=jax.ShapeDtypeStruct((T, 2 * D_MODEL), BF16),
        compiler_params=_cparams(("parallel",)),
    )(y, proj, g)


def _gated_norm_bwd(dmix, y, proj, g, dproj):
    T, D = y.shape
    tt = _tile(T, 256)

    def body(do_ref, y_ref, z_ref, g_ref, dp_ref, dy_ref, dz_ref, dg_ref):
        @pl.when(pl.program_id(0) == 0)
        def _():
            dg_ref[...] = jnp.zeros_like(dg_ref)

        yv, zv = y_ref[...], z_ref[...]
        do = do_ref[...].astype(F32)
        sz = _silu(zv)
        yg = yv * sz
        r = lax.rsqrt(jnp.mean(yg * yg, axis=-1, keepdims=True) + EPS)
        xhat = yg * r
        dxh = do * g_ref[...]
        dyg = r * (dxh - xhat * jnp.mean(dxh * xhat, axis=-1, keepdims=True))
        dy_ref[...] = dyg * sz
        dz_ref[...] = (dyg * yv * _dsilu(zv)).astype(BF16)
        dg_ref[...] += jnp.sum(do * xhat, axis=0, keepdims=True)

    tok = pl.BlockSpec((tt, D), lambda i: (i, 0))
    vec = pl.BlockSpec((1, D), lambda i: (0, 0))
    return pl.pallas_call(
        body, name="gated_norm_bwd", grid=(T // tt,),
        in_specs=[tok, tok, tok, vec, ANY],
        out_specs=[tok, tok, vec],
        out_shape=[jax.ShapeDtypeStruct((T, D), F32), jax.ShapeDtypeStruct(dproj.shape, BF16),
                   jax.ShapeDtypeStruct((1, D), F32)],
        input_output_aliases={4: 1},
        compiler_params=_cparams(("arbitrary",)),
    )(dmix, y, proj, g, dproj)


HALO = 8


def _shift_down(cur, prev8, s):
    ext = jnp.concatenate([prev8, cur], axis=0)
    return pltpu.roll(ext, s, axis=0)[HALO:]


def _shift_up(cur, next8, s):
    n = cur.shape[0]
    ext = jnp.concatenate([cur, next8], axis=0)
    return pltpu.roll(ext, n + HALO - s, axis=0)[:n]


def _conv_specs(tt, cb, col_off_blocks, nt):
    hb = tt // HALO
    cur = pl.BlockSpec((tt, cb), lambda j, i: (i, col_off_blocks + j))
    prev = pl.BlockSpec((HALO, cb), lambda j, i: (jnp.maximum(i * hb - 1, 0), col_off_blocks + j))
    nxt = pl.BlockSpec((HALO, cb), lambda j, i: (jnp.minimum((i + 1) * hb, nt * hb - 1), col_off_blocks + j))
    return cur, prev, nxt


def _taps(cur, prev8, K):
    return [_shift_down(cur, prev8, K - 1 - k) for k in range(K - 1)] + [cur]


def _conv_of_taps(taps, w):
    y = taps[-1] * w[len(taps) - 1:len(taps), :]
    for k, t in enumerate(taps[:-1]):
        y = y + t * w[k:k + 1, :]
    return y


def _causal_conv(cur, prev8, w, K):
    return _conv_of_taps(_taps(cur, prev8, K), w)


def _anticausal_conv(cur, next8, w, K):
    y = cur * w[K - 1:K, :]
    for k in range(K - 1):
        y = y + _shift_up(cur, next8, K - 1 - k) * w[k:k + 1, :]
    return y


def _ssm_conv_fwd(proj, w8, b):
    T = proj.shape[0]
    tt, cb = _tile(T, 512), 512
    nt = T // tt
    cur, prev, _ = _conv_specs(tt, cb, OFF_XBC // cb, nt)

    def body(u_ref, up_ref, w_ref, b_ref, o_ref):
        first = pl.program_id(1) == 0
        p8 = jnp.where(first, 0.0, up_ref[...])
        pre = _causal_conv(u_ref[...], p8, w_ref[...], K_SSM) + b_ref[...]
        o_ref[...] = _silu(pre)

    return pl.pallas_call(
        body, name="ssm_conv_fwd", grid=(D_XBC // cb, nt),
        in_specs=[cur, prev, pl.BlockSpec((8, cb), lambda j, i: (0, j)), pl.BlockSpec((1, cb), lambda j, i: (0, j))],
        out_specs=pl.BlockSpec((tt, cb), lambda j, i: (i, j)),
        out_shape=jax.ShapeDtypeStruct((T, D_XBC), F32),
        compiler_params=_cparams(("parallel", "parallel")),
    )(proj, proj, w8, b)


def _ssm_conv_bwd(dact, proj, w8, b, dproj):
    T = proj.shape[0]
    tt, cb = _tile(T, 512), 512
    nt = T // tt
    cur, prev, nxt = _conv_specs(tt, cb, OFF_XBC // cb, nt)
    dcur, dprev, dnxt = _conv_specs(tt, cb, 0, nt)

    def dpre_of(d, u, p8, w, bb):
        pre = _causal_conv(u, p8, w, K_SSM) + bb
        return d * _dsilu(pre)

    def body(d_ref, dn_ref, u_ref, up_ref, un_ref, w_ref, b_ref, dp_ref, dx_ref, dw_ref, db_ref):
        i = pl.program_id(1)

        @pl.when(i == 0)
        def _():
            dw_ref[...] = jnp.zeros_like(dw_ref)
            db_ref[...] = jnp.zeros_like(db_ref)

        w, bb = w_ref[...], b_ref[...]
        u = u_ref[...]
        p8 = jnp.where(i == 0, 0.0, up_ref[...])
        taps = _taps(u, p8, K_SSM)
        dpre = d_ref[...] * _dsilu(_conv_of_taps(taps, w) + bb)
        un = un_ref[...]
        dpre_n = dpre_of(dn_ref[...], un, u[tt - HALO:, :], w, bb)
        dpre_n = jnp.where(i == nt - 1, 0.0, dpre_n)
        dx_ref[...] = _anticausal_conv(dpre, dpre_n, w, K_SSM).astype(BF16)
        rows = [jnp.sum(dpre * t, axis=0, keepdims=True) for t in taps]
        rows.append(jnp.zeros((8 - K_SSM, cb), F32))
        dw_ref[...] += jnp.concatenate(rows, axis=0)
        db_ref[...] += jnp.sum(dpre, axis=0, keepdims=True)

    wspec = pl.BlockSpec((8, cb), lambda j, i: (0, j))
    bspec = pl.BlockSpec((1, cb), lambda j, i: (0, j))
    return pl.pallas_call(
        body, name="ssm_conv_bwd", grid=(D_XBC // cb, nt),
        in_specs=[dcur, dnxt, cur, prev, nxt, wspec, bspec, ANY],
        out_specs=[pl.BlockSpec((tt, cb), lambda j, i: (i, OFF_XBC // cb + j)), wspec, bspec],
        out_shape=[jax.ShapeDtypeStruct(dproj.shape, BF16), jax.ShapeDtypeStruct((8, D_XBC), F32),
                   jax.ShapeDtypeStruct((1, D_XBC), F32)],
        input_output_aliases={7: 0},
        compiler_params=_cparams(("parallel", "arbitrary")),
    )(dact, dact, proj, proj, proj, w8, b, dproj)


SCB = 512
SC3 = 3 * SCB


def _sc_specs(tt, nt):
    hb = tt // HALO
    cur = pl.BlockSpec((tt, SC3), lambda j, i: (i, OFF_CB // SC3 + j))
    prev = pl.BlockSpec((HALO, SC3), lambda j, i: (jnp.maximum(i * hb - 1, 0), OFF_CB // SC3 + j))
    nxt = pl.BlockSpec((HALO, SC3), lambda j, i: (jnp.minimum((i + 1) * hb, nt * hb - 1), OFF_CB // SC3 + j))
    return cur, prev, nxt


def _shortconv_fwd(proj, w8, ymix):
    T = proj.shape[0]
    tt = _tile(T, 512)
    nt = T // tt
    cur, prev, _ = _sc_specs(tt, nt)

    def body(p_ref, pp_ref, w_ref, y_ref, o_ref):
        p, pp = p_ref[...], pp_ref[...]
        v = p[:, SCB:2 * SCB] * p[:, 2 * SCB:]
        vp = jnp.where(pl.program_id(1) == 0, 0.0, pp[:, SCB:2 * SCB] * pp[:, 2 * SCB:])
        o_ref[...] = (p[:, :SCB] * _causal_conv(v, vp, w_ref[...], K_SC)).astype(BF16)

    return pl.pallas_call(
        body, name="shortconv_fwd", grid=(D_MODEL // SCB, nt),
        in_specs=[cur, prev, pl.BlockSpec((8, SCB), lambda j, i: (0, j)), ANY],
        out_specs=pl.BlockSpec((tt, SCB), lambda j, i: (i, D_SSM // SCB + j)),
        out_shape=jax.ShapeDtypeStruct(ymix.shape, BF16),
        input_output_aliases={3: 0},
        compiler_params=_cparams(("parallel", "parallel")),
    )(proj, proj, w8, ymix)


def _shortconv_bwd(dmix, proj, w8):
    T = proj.shape[0]
    tt = _tile(T, 512)
    nt = T // tt
    hb = tt // HALO
    cur, prev, nxt = _sc_specs(tt, nt)
    d_s = pl.BlockSpec((tt, SCB), lambda j, i: (i, D_SSM // SCB + j))
    dn_s = pl.BlockSpec((HALO, SCB), lambda j, i: (jnp.minimum((i + 1) * hb, nt * hb - 1), D_SSM // SCB + j))

    def body(d_ref, dn_ref, p_ref, pp_ref, pn_ref, w_ref, dp_ref, dw_ref):
        i = pl.program_id(1)

        @pl.when(i == 0)
        def _():
            dw_ref[...] = jnp.zeros_like(dw_ref)

        w = w_ref[...]
        p, pp = p_ref[...], pp_ref[...]
        gb, gc, u = p[:, :SCB], p[:, SCB:2 * SCB], p[:, 2 * SCB:]
        v = gc * u
        vp = jnp.where(i == 0, 0.0, pp[:, SCB:2 * SCB] * pp[:, 2 * SCB:])
        d = d_ref[...].astype(F32)
        taps = _taps(v, vp, K_SC)
        dp_ref[:, :SCB] = (d * _conv_of_taps(taps, w)).astype(BF16)
        dcv = d * gb
        dcv_n = jnp.where(i == nt - 1, 0.0, dn_ref[...].astype(F32) * pn_ref[:, :SCB])
        dv = _anticausal_conv(dcv, dcv_n, w, K_SC)
        dp_ref[:, SCB:2 * SCB] = (dv * u).astype(BF16)
        dp_ref[:, 2 * SCB:] = (dv * gc).astype(BF16)
        rows = [jnp.sum(dcv * t, axis=0, keepdims=True) for t in taps]
        rows.append(jnp.zeros((8 - K_SC, SCB), F32))
        dw_ref[...] += jnp.concatenate(rows, axis=0)

    wspec = pl.BlockSpec((8, SCB), lambda j, i: (0, j))
    return pl.pallas_call(
        body, name="shortconv_bwd", grid=(D_MODEL // SCB, nt),
        in_specs=[d_s, dn_s, cur, prev, nxt, wspec],
        out_specs=[cur, wspec],
        out_shape=[jax.ShapeDtypeStruct((T, D_MAIN), BF16), jax.ShapeDtypeStruct((8, D_MODEL), F32)],
        compiler_params=_cparams(("parallel", "arbitrary")),
    )(dmix, dmix, proj, proj, proj, w8)


GW = HEADS_PER_GROUP * HEADDIM


def _dot(a, b):
    return jnp.dot(a.astype(BF16), b.astype(BF16), preferred_element_type=F32)


def _dot_nt(a, b):
    return lax.dot_general(a.astype(BF16), b.astype(BF16), (((1,), (1,)), ((), ())), preferred_element_type=F32)


def _dot_tn(a, b):
    return lax.dot_general(a.astype(BF16), b.astype(BF16), (((0,), (0,)), ((), ())), preferred_element_type=F32)


def _bf16_terms(x, n):
    terms, r = [], x
    for _ in range(n):
        t = r.astype(BF16)
        terms.append(t)
        r = r - t.astype(F32)
    return terms


def _dot_sel(a, sel, n=2):
    s = sel.astype(BF16)
    return sum(jnp.dot(t, s, preferred_element_type=F32) for t in _bf16_terms(a, n))


def _sel_dot(sel, b, n=2):
    s = sel.astype(BF16)
    return sum(jnp.dot(s, t, preferred_element_type=F32) for t in _bf16_terms(b, n))


def _sel_dot_nt(sel, b, n=2):
    s = sel.astype(BF16)
    return sum(lax.dot_general(s, t, (((1,), (1,)), ((), ())), preferred_element_type=F32)
               for t in _bf16_terms(b, n))


def _head_cols(rows):
    parts = [jnp.broadcast_to(rows[r:r + 1, :], (HEADDIM, CHUNK)) for r in range(HEADS_PER_GROUP)]
    return jnp.concatenate(parts, axis=0).T


def _head_rows(rows):
    parts = [jnp.broadcast_to(rows[r:r + 1, :], (HEADDIM, N_STATE)) for r in range(HEADS_PER_GROUP)]
    return jnp.concatenate(parts, axis=0)


def _ssd_common(dtr, bias, alog):
    dt = _softplus(dtr + bias)
    A = -jnp.exp(alog)
    a = dt * A
    ki = lax.broadcasted_iota(jnp.int32, (CHUNK, CHUNK), 0)
    si = lax.broadcasted_iota(jnp.int32, (CHUNK, CHUNK), 1)
    upper = (ki <= si).astype(F32)
    cs = _dot_sel(a, upper, 3)
    cs_last = jnp.broadcast_to(cs[:, CHUNK - 1:CHUNK], (8, CHUNK))
    return dt, A, a, cs, cs_last


def _decay_matrix(cs, r):
    li = lax.broadcasted_iota(jnp.int32, (CHUNK, CHUNK), 0)
    si = lax.broadcasted_iota(jnp.int32, (CHUNK, CHUNK), 1)
    causal = li >= si
    R = jnp.broadcast_to(cs[r:r + 1, :], (CHUNK, CHUNK))
    seg = jnp.where(causal, R.T - R, 0.0)
    return jnp.where(causal, jnp.exp(seg), 0.0)


def _decay_cat(cs):
    return jnp.concatenate([_decay_matrix(cs, r) for r in range(HEADS_PER_GROUP)], axis=1)


def _lanes4(m):
    return jnp.concatenate([m] * HEADS_PER_GROUP, axis=1)


def _head_blocks(v):
    col = lax.broadcasted_iota(jnp.int32, v.shape, 1) // HEADDIM
    return jnp.concatenate([jnp.where(col == r, v, jnp.zeros_like(v)) for r in range(HEADS_PER_GROUP)], axis=0)


GXBC = GW + 2 * N_STATE


GS_FWD = 8
GS_BWD = 8


def _ssd_in_specs(nc, rev):
    GS = GS_BWD if rev else GS_FWD
    cix = (lambda c: nc - 1 - c) if rev else (lambda c: c)
    x_s = pl.BlockSpec((CHUNK, GS * GW), lambda g, c: (cix(c), g))
    xbc_s = pl.BlockSpec((CHUNK, GS * GXBC), lambda g, c: (cix(c), g))
    dtr_s = pl.BlockSpec((GS, 8, CHUNK), lambda g, c: (g, 0, cix(c)))
    row_s = pl.BlockSpec((GS, 8, CHUNK), lambda g, c: (g, 0, 0))
    drep_s = pl.BlockSpec((1, GS * GW), lambda g, c: (0, g))
    hs_s = pl.BlockSpec((1, GS * GW, N_STATE), lambda g, c: (cix(c), g, 0))
    return x_s, xbc_s, dtr_s, row_s, drep_s, hs_s


def _xbc_parts(xbc_ref, gi):
    o = gi * GXBC
    return xbc_ref[:, o:o + GW], xbc_ref[:, o + GW:o + GW + N_STATE], xbc_ref[:, o + GW + N_STATE:o + GXBC]


def _ssd_fwd(xbc, dtr, bias, alog, drep):
    T = xbc.shape[0]
    nc = T // CHUNK
    x_s, xbc_s, dtr_s, row_s, drep_s, hs_s = _ssd_in_specs(nc, False)

    def body(xbc_ref, dtr_ref, bias_ref, alog_ref, drep_ref, y_ref, hs_ref, h_scr):
        @pl.when(pl.program_id(1) == 0)
        def _():
            h_scr[...] = jnp.zeros_like(h_scr)

        for gi in range(GS_FWD):
            cols, rows = slice(gi * GW, (gi + 1) * GW), pl.ds(gi * GW, GW)
            x, Bm, Cm = _xbc_parts(xbc_ref, gi)
            dt, A, a, cs, cs_last = _ssd_common(dtr_ref[gi], bias_ref[gi], alog_ref[gi])
            E = _head_cols(jnp.exp(cs))
            W = _head_cols(jnp.exp(cs_last - cs) * dt)
            X = (x * _head_cols(dt)).astype(BF16)
            CB = _dot_nt(Cm, Bm)
            col = lax.broadcasted_iota(jnp.int32, (CHUNK, GW), 1) // HEADDIM
            y = jnp.zeros((CHUNK, GW), F32)
            for r in range(HEADS_PER_GROUP):
                y = y + jnp.where(col == r, _dot(CB * _decay_matrix(cs, r), X), 0.0)
            h = h_scr[rows, :]
            hs_ref[0, rows, :] = h
            y = y + _dot_nt(Cm, h) * E
            y_ref[:, cols] = y + drep_ref[:, cols] * x
            h_scr[rows, :] = h * _head_rows(jnp.exp(cs_last)) + _dot_tn(x * W, Bm)

    return pl.pallas_call(
        body, name="ssd_fwd", grid=(N_GROUPS // GS_FWD, nc),
        in_specs=[xbc_s, dtr_s, row_s, row_s, drep_s],
        out_specs=[x_s, hs_s],
        out_shape=[jax.ShapeDtypeStruct((T, D_SSM), F32), jax.ShapeDtypeStruct((nc, D_SSM, N_STATE), F32)],
        scratch_shapes=[pltpu.VMEM((GS_FWD * GW, N_STATE), F32)],
        compiler_params=_cparams(("parallel", "arbitrary")),
    )(xbc, dtr, bias, alog, drep)


def _ssd_bwd(xbc, dtr, bias, alog, drep, dy, hs):
    T = xbc.shape[0]
    nc = T // CHUNK
    x_s, xbc_s, dtr_s, row_s, drep_s, hs_s = _ssd_in_specs(nc, True)

    def body(xbc_ref, dtr_ref, bias_ref, alog_ref, drep_ref, dy_ref, hs_ref,
             dxbc_ref, ddtr_ref, dbias_ref, dalog_ref, dd_ref, dh_scr):
        @pl.when(pl.program_id(1) == 0)
        def _():
            dh_scr[...] = jnp.zeros_like(dh_scr)
            dbias_ref[...] = jnp.zeros_like(dbias_ref)
            dalog_ref[...] = jnp.zeros_like(dalog_ref)
            dd_ref[...] = jnp.zeros_like(dd_ref)

        for gi in range(GS_BWD):
            one_group(gi, xbc_ref, dtr_ref, bias_ref, alog_ref, drep_ref, dy_ref, hs_ref,
                      dxbc_ref, ddtr_ref, dbias_ref, dalog_ref, dd_ref, dh_scr)

    def one_group(gi, xbc_ref, dtr_ref, bias_ref, alog_ref, drep_ref, dy_ref, hs_ref,
                  dxbc_ref, ddtr_ref, dbias_ref, dalog_ref, dd_ref, dh_scr):
        cols, rows, o = slice(gi * GW, (gi + 1) * GW), pl.ds(gi * GW, GW), gi * GXBC
        x, Bm, Cm = _xbc_parts(xbc_ref, gi)
        dY = dy_ref[:, cols]
        dt, A, a, cs, cs_last = _ssd_common(dtr_ref[gi], bias_ref[gi], alog_ref[gi])
        E = _head_cols(jnp.exp(cs))
        DT = _head_cols(dt)
        Wd = _head_cols(jnp.exp(cs_last - cs))
        X = x * DT
        h = hs_ref[0, rows, :]
        dS = dh_scr[rows, :]
        CB = _dot_nt(Cm, Bm)
        rowid = lax.broadcasted_iota(jnp.int32, (8, CHUNK), 0)
        lane = lax.broadcasted_iota(jnp.int32, (8, CHUNK), 1)
        hsel = (lax.broadcasted_iota(jnp.int32, (8, GW), 1) // HEADDIM
                == lax.broadcasted_iota(jnp.int32, (8, GW), 0)).astype(F32)
        hsel_l = (lax.broadcasted_iota(jnp.int32, (8, HEADS_PER_GROUP * CHUNK), 1) // CHUNK
                  == lax.broadcasted_iota(jnp.int32, (8, HEADS_PER_GROUP * CHUNK), 0)).astype(F32)

        Lc, CBc = _decay_cat(cs), _lanes4(CB)
        Mc = CBc * Lc
        GLc = _dot_nt(dY, _head_blocks(X.astype(BF16))) * Lc
        Wc = GLc * CBc
        colsum = jnp.sum(Wc, axis=0, keepdims=True)
        dcs = _sel_dot_nt(hsel_l, Wc)
        dCB = jnp.zeros((CHUNK, CHUNK), F32)
        for r in range(HEADS_PER_GROUP):
            blk = slice(r * CHUNK, (r + 1) * CHUNK)
            dCB = dCB + GLc[:, blk]
            dcs = dcs - jnp.where(rowid == r, colsum[:, blk], 0.0)
        m_stack = jnp.concatenate([Mc[:, r * CHUNK:(r + 1) * CHUNK].astype(BF16) for r in range(HEADS_PER_GROUP)],
                                  axis=0)
        dX = lax.dot_general(m_stack, _head_blocks(dY.astype(BF16)), (((0,), (0,)), ((), ())),
                             preferred_element_type=F32)
        dC = _dot(dCB, Bm)
        dB = _dot_tn(dCB, Cm)
        T1 = _dot_nt(Bm, dS)
        dX = dX + T1 * Wd
        dB = dB + _dot(X * Wd, dS)
        pdec = _sel_dot_nt(hsel, X * T1 * Wd)
        dcs = dcs - pdec
        dlast = jnp.sum(pdec, axis=1, keepdims=True) \
            + jnp.exp(cs_last[:, 0:1]) * jnp.sum(_sel_dot(hsel, dS * h), axis=1, keepdims=True)
        dYE = dY * E
        dC = dC + _dot(dYE, h)
        yoff = _dot_nt(Cm, h) * E
        dcs = dcs + _sel_dot_nt(hsel, dY * yoff)
        dcs = dcs + jnp.where(lane == CHUNK - 1, dlast, 0.0)
        ki = lax.broadcasted_iota(jnp.int32, (CHUNK, CHUNK), 0)
        si = lax.broadcasted_iota(jnp.int32, (CHUNK, CHUNK), 1)
        lower = (ki >= si).astype(F32)
        da = _dot_sel(dcs, lower)
        ddt = da * A + _sel_dot_nt(hsel, dX * x)
        ddtr = ddt * _sigmoid(dtr_ref[gi] + bias_ref[gi])
        ddtr_ref[gi] = ddtr
        dbias_ref[gi] += ddtr
        dalog_ref[gi] += da * a
        dxbc_ref[:, o:o + GW] = dX * DT + drep_ref[:, cols] * dY
        dd_ref[:, cols] += jnp.sum(dY * x, axis=0, keepdims=True)
        dxbc_ref[:, o + GW:o + GW + N_STATE] = dB
        dxbc_ref[:, o + GW + N_STATE:o + GXBC] = dC
        dh_scr[rows, :] = dS * _head_rows(jnp.exp(cs_last)) + _dot_tn(dYE, Cm)

    return pl.pallas_call(
        body, name="ssd_bwd", grid=(N_GROUPS // GS_BWD, nc),
        in_specs=[xbc_s, dtr_s, row_s, row_s, drep_s, x_s, hs_s],
        out_specs=[xbc_s, dtr_s, row_s, row_s, drep_s],
        out_shape=[jax.ShapeDtypeStruct((T, D_XBC), F32),
                   jax.ShapeDtypeStruct((N_GROUPS, 8, T), F32),
                   jax.ShapeDtypeStruct((N_GROUPS, 8, CHUNK), F32),
                   jax.ShapeDtypeStruct((N_GROUPS, 8, CHUNK), F32),
                   jax.ShapeDtypeStruct((1, D_SSM), F32)],
        scratch_shapes=[pltpu.VMEM((GS_BWD * GW, N_STATE), F32)],
        compiler_params=_cparams(("parallel", "arbitrary")),
    )(xbc, dtr, bias, alog, drep, dy, hs)


def _adamw(w, g, m, v, name, deps=(), emit_g=False):
    R, C = w.shape
    tr = _tile(R, 256, 8)
    nd = len(deps)
    nout = 4 if emit_g else 3

    def body(w_ref, g_ref, m_ref, v_ref, *rest):
        outs = rest[nd:]
        gv = g_ref[...]
        mn = ADAM_B1 * m_ref[...] + (1.0 - ADAM_B1) * gv
        vn = ADAM_B2 * v_ref[...] + (1.0 - ADAM_B2) * (gv * gv)
        m_hat = mn / (1.0 - ADAM_B1 ** ADAM_STEP)
        v_hat = vn / (1.0 - ADAM_B2 ** ADAM_STEP)
        outs[0][...] = -ADAM_LR * (m_hat / (jnp.sqrt(v_hat) + ADAM_EPS) + ADAM_WD * w_ref[...])
        outs[1][...] = mn
        outs[2][...] = vn
        if emit_g:
            outs[3][...] = gv

    spec = pl.BlockSpec((tr, C), lambda i: (i, 0))
    return pl.pallas_call(
        body, name=name, grid=(R // tr,),
        in_specs=[spec] * 4 + [ANY] * nd, out_specs=[spec] * nout,
        out_shape=[jax.ShapeDtypeStruct((R, C), F32)] * nout,
        compiler_params=_cparams(("parallel",)),
    )(w, g, m, v, *deps)


ANY = pl.BlockSpec(memory_space=pl.ANY)


def _place():
    x, y, c = lax.axis_index("x"), lax.axis_index("y"), lax.axis_index("c")
    return x, y, c


def _other_chips(x, y):
    return [(1 - x, y), (x, 1 - y), (1 - x, 1 - y)]


def _allgather_inplace(bufs, splits, first_done=False):
    n = len(bufs)

    def body(*refs):
        o_refs = refs[n:2 * n]
        send_sems, recv_sems = refs[2 * n:]
        x, y, c = _place()
        xn, yn, dg, sibling = (1 - x, y), (x, 1 - y), (1 - x, 1 - y), (x, y, 1 - c)

        def blk(k, chip, pc):
            return o_refs[k].at[4 * chip[0] + 2 * chip[1] + pc]

        def part(k, ref, p):
            kind, s = splits[k]
            _, R, C = bufs[k].shape
            if kind == "rows":
                return ref.at[pl.ds(0, s)] if p == 0 else ref.at[pl.ds(s, R - s)]
            return ref.at[:, pl.ds(0, s)] if p == 0 else ref.at[:, pl.ds(s, C - s)]

        def copy(k, slot, ref, to):
            return pltpu.make_async_remote_copy(
                src_ref=ref, dst_ref=ref, send_sem=send_sems.at[k, slot], recv_sem=recv_sems.at[k, slot],
                device_id=to, device_id_type=MESH)

        sent = []

        def send(k, slot, ref, to):
            cp = copy(k, slot, ref, to)
            cp.start()
            sent.append(cp)

        if not first_done:
            for k in range(n):
                send(k, 0, blk(k, (x, y), c), (*xn, c))
                send(k, 1, blk(k, (x, y), c), (*yn, c))
        for k in range(n):
            bx, by = blk(k, xn, c), blk(k, yn, c)
            if not first_done:
                copy(k, 0, bx, sibling).wait_recv()
            send(k, 2, part(k, bx, 0), (*yn, c))
            send(k, 4, bx, sibling)
            if not first_done:
                copy(k, 1, by, sibling).wait_recv()
            send(k, 3, part(k, by, 1), (*xn, c))
            send(k, 5, by, sibling)
        for k in range(n):
            d0, d1 = part(k, blk(k, dg, c), 0), part(k, blk(k, dg, c), 1)
            copy(k, 2, d0, sibling).wait_recv()
            send(k, 6, d0, sibling)
            copy(k, 3, d1, sibling).wait_recv()
            send(k, 7, d1, sibling)
        for k in range(n):
            copy(k, 4, blk(k, xn, 1 - c), sibling).wait_recv()
            copy(k, 5, blk(k, yn, 1 - c), sibling).wait_recv()
            copy(k, 6, part(k, blk(k, dg, 1 - c), 0), sibling).wait_recv()
            copy(k, 7, part(k, blk(k, dg, 1 - c), 1), sibling).wait_recv()
        for cp in sent:
            cp.wait_send()

    return pl.pallas_call(
        body, name="allgather_w_in",
        in_specs=[ANY] * n, out_specs=[ANY] * n,
        out_shape=[jax.ShapeDtypeStruct(b.shape, b.dtype) for b in bufs],
        input_output_aliases={k: k for k in range(n)},
        scratch_shapes=[pltpu.SemaphoreType.DMA((n, 8)), pltpu.SemaphoreType.DMA((n, 8))],
    )(*bufs)


HBM = pl.BlockSpec(memory_space=pltpu.HBM)
SEM = pl.BlockSpec(memory_space=pltpu.SEMAPHORE)
EFFECT = pltpu.SideEffectType.DATAFLOW_SIDE_EFFECTING


def _split_start(name, arrays, build, n_copies, after=()):
    na, nd = len(arrays), len(after)

    def body(*refs):
        send_sems, recv_sems = refs[na + nd], refs[na + nd + 1]
        for cp in build(refs[:na], send_sems, recv_sems):
            cp.start()
        refs[-1][...] = jnp.zeros((8, 128), F32)

    outs = pl.pallas_call(
        body, name=name,
        out_shape=(pltpu.SemaphoreType.DMA((n_copies,)), pltpu.SemaphoreType.DMA((n_copies,)),
                   *[pltpu.HBM(a.shape, a.dtype) for a in arrays], jax.ShapeDtypeStruct((8, 128), F32)),
        in_specs=[HBM] * na + [ANY] * nd,
        out_specs=(SEM, SEM, *[HBM] * na, pl.BlockSpec(memory_space=pltpu.VMEM)),
        input_output_aliases={i: 2 + i for i in range(na)},
        compiler_params=pltpu.CompilerParams(has_side_effects=EFFECT),
    )(*[pltpu.with_memory_space_constraint(a, pltpu.HBM) for a in arrays], *after)
    return outs[0], outs[1], list(outs[2:2 + na]), outs[-1]


def _split_wait(name, send_sems, recv_sems, arrays, build, after):
    na = len(arrays)

    def body(*refs):
        for cp in build(refs[:na], refs[na], refs[na + 1]):
            cp.wait_send()
            cp.wait_recv()

    outs = pl.pallas_call(
        body, name=name,
        out_shape=tuple(pltpu.HBM(a.shape, a.dtype) for a in arrays),
        in_specs=[HBM] * na + [SEM, SEM] + [ANY] * len(after),
        out_specs=tuple([HBM] * na),
        input_output_aliases={i: i for i in range(na)},
        compiler_params=pltpu.CompilerParams(has_side_effects=EFFECT),
    )(*arrays, send_sems, recv_sems, *after)
    return list(outs)


def _remote(src, dst, send_sems, recv_sems, i, to):
    return pltpu.make_async_remote_copy(src_ref=src, dst_ref=dst, send_sem=send_sems.at[i], recv_sem=recv_sems.at[i],
                                        device_id=to, device_id_type=MESH)


def _build_ag_first(refs, ss, rs):
    x, y, c = _place()
    cps = []
    for k, ref in enumerate(refs):
        blk = ref.at[4 * x + 2 * y + c]
        cps += [_remote(blk, blk, ss, rs, 2 * k, (1 - x, y, c)), _remote(blk, blk, ss, rs, 2 * k + 1, (x, 1 - y, c))]
    return cps


def _build_ag_ici(refs, ss, rs):
    x, y, c = _place()
    cps = []
    for k, ref in enumerate(refs):
        blk = ref.at[4 * x + 2 * y + c]
        for j, (px, py) in enumerate(_other_chips(x, y)):
            cps.append(_remote(blk, blk, ss, rs, 3 * k + j, (px, py, c)))
    return cps


def _build_ag_fwd(refs, ss, rs):
    x, y, c = _place()
    cps = []
    for k, ref in enumerate(refs):
        for j, (px, py) in enumerate(_other_chips(x, y)):
            blk = ref.at[4 * px + 2 * py + c]
            cps.append(_remote(blk, blk, ss, rs, 3 * k + j, (x, y, 1 - c)))
    return cps


def _build_rs_swap(refs, ss, rs):
    x, y, c = _place()
    n = len(refs) // 2
    return [_remote(refs[k].at[:, pl.ds(1 - c, 1)], refs[n + k], ss, rs, k, (x, y, 1 - c)) for k in range(n)]


def _build_rs_ici(refs, ss, rs):
    x, y, c = _place()
    n = len(refs) // 2
    me = 2 * x + y
    cps = []
    for k in range(n):
        for j, (px, py) in enumerate(_other_chips(x, y)):
            cps.append(_remote(refs[k].at[2 * px + py], refs[n + k].at[me], ss, rs, 3 * k + j, (px, py, c)))
    return cps


def _build_rs_share(refs, ss, rs):
    x, y, c = _place()
    return [_remote(ref.at[c], ref.at[c], ss, rs, k, (x, y, 1 - c)) for k, ref in enumerate(refs)]


def _build_small_gather(refs, ss, rs):
    x, y, c = _place()
    me = 4 * x + 2 * y + c
    cps = []
    for d in range(1, N_DEV):
        to = (1 - x if d & 4 else x, 1 - y if d & 2 else y, 1 - c if d & 1 else c)
        cps.append(_remote(refs[0], refs[1].at[me], ss, rs, d - 1, to))
    return cps


def _sum_gathered(mine, landed, me_arr):
    R, C = mine.shape

    def body(me_ref, m_ref, l_ref, o_ref):
        me = me_ref[0]
        s = None
        for d in range(N_DEV):
            t = jnp.where(me == d, m_ref[...], l_ref[d])
            s = t if s is None else s + t
        o_ref[...] = s

    grid_spec = pltpu.PrefetchScalarGridSpec(
        num_scalar_prefetch=1, grid=(1,),
        in_specs=[pl.BlockSpec((R, C), lambda i, me_ref: (0, 0)),
                  pl.BlockSpec((N_DEV, R, C), lambda i, me_ref: (0, 0, 0))],
        out_specs=pl.BlockSpec((R, C), lambda i, me_ref: (0, 0)))
    return pl.pallas_call(
        body, name="sum_small", grid_spec=grid_spec,
        out_shape=jax.ShapeDtypeStruct((R, C), F32),
        compiler_params=_cparams(("arbitrary",)),
    )(me_arr, mine, landed)


def _rs_add_pair(p, r0, c_arr, name):
    _, _, hr, cols = p.shape
    tr = _tile(hr, 256, 8)

    def body(c_ref, p_ref, r_ref, q_ref):
        q_ref[...] = (p_ref[0].astype(F32) + r_ref[0].astype(F32)).astype(BF16)

    grid_spec = pltpu.PrefetchScalarGridSpec(
        num_scalar_prefetch=1, grid=(N_CHIPS, hr // tr),
        in_specs=[pl.BlockSpec((1, 1, tr, cols), lambda j, i, c_ref: (j, c_ref[0], i, 0)),
                  pl.BlockSpec((1, 1, tr, cols), lambda j, i, c_ref: (j, 0, i, 0))],
        out_specs=pl.BlockSpec((1, tr, cols), lambda j, i, c_ref: (j, i, 0)))
    return pl.pallas_call(
        body, name=name, grid_spec=grid_spec,
        out_shape=jax.ShapeDtypeStruct((N_CHIPS, hr, cols), BF16),
        compiler_params=_cparams(("parallel", "parallel")),
    )(c_arr, p, r0)


def _rs_add_chips(r1, q, place_arr, name):
    _, hr, cols = r1.shape
    tr = _tile(hr, 256, 8)

    def body(place_ref, r_ref, q_ref, o_ref):
        chip = place_ref[0]
        s = None
        for j in range(N_CHIPS):
            t = jnp.where(chip == j, q_ref[j], r_ref[j]).astype(F32)
            s = t if s is None else s + t
        o_ref[...] = s

    blk = pl.BlockSpec((N_CHIPS, tr, cols), lambda i, place_ref: (0, i, 0))
    grid_spec = pltpu.PrefetchScalarGridSpec(
        num_scalar_prefetch=1, grid=(hr // tr,), in_specs=[blk, blk],
        out_specs=pl.BlockSpec((None, tr, cols), lambda i, place_ref: (place_ref[1], i, 0)))
    return pl.pallas_call(
        body, name=name, grid_spec=grid_spec,
        out_shape=jax.ShapeDtypeStruct((2, hr, cols), F32),
        compiler_params=_cparams(("parallel",)),
    )(place_arr, r1, q)


def _pad_rows(a, rows):
    return jnp.pad(a, ((0, rows - a.shape[0]), (0, 0)))


def _pad_cols(a, cols):
    return jnp.pad(a, ((0, 0), (0, cols - a.shape[1])))


def _heads_to_rows(v):
    v = v.reshape(N_GROUPS, HEADS_PER_GROUP, 1)
    v = jnp.pad(v, ((0, 0), (0, 8 - HEADS_PER_GROUP), (0, 0)))
    return jnp.broadcast_to(v, (N_GROUPS, 8, CHUNK))


def _rows_to_heads(a):
    return jnp.sum(a[:, :HEADS_PER_GROUP, :], axis=-1).reshape(N_HEADS)


def _to_kernel_rows(a):
    C = a.shape[1]
    x0, b0, c0, s0 = D_SSM, 2 * D_SSM, 2 * D_SSM + 1024, D_SSM + D_XBC + N_HEADS
    xbc = jnp.concatenate([a[x0:b0].reshape(N_GROUPS, GW, C), a[b0:c0].reshape(N_GROUPS, N_STATE, C),
                           a[c0:c0 + 1024].reshape(N_GROUPS, N_STATE, C)], axis=1).reshape(D_XBC, C)
    sc = jnp.concatenate([a[s0 + k * D_MODEL:s0 + (k + 1) * D_MODEL].reshape(D_MODEL // SCB, SCB, C)
                          for k in range(3)], axis=1).reshape(3 * D_MODEL, C)
    return jnp.concatenate([a[:D_SSM], xbc, sc], axis=0)


HR_IN = 1568


def _kernel_segments():
    segs = [(0, 0, 0, D_SSM)]
    for g in range(N_GROUPS):
        k0 = D_SSM + g * GXBC
        segs += [(0, k0, D_SSM + g * GW, GW), (0, k0 + GW, 2 * D_SSM + g * N_STATE, N_STATE),
                 (0, k0 + GW + N_STATE, 2 * D_SSM + 1024 + g * N_STATE, N_STATE)]
    segs.append((1, 0, D_SSM + D_XBC, N_HEADS))
    for j in range(D_MODEL // SCB):
        for k in range(3):
            segs.append((0, D_SSM + D_XBC + j * SC3 + k * SCB, D_SSM + D_XBC + N_HEADS + k * D_MODEL + j * SCB, SCB))
    return segs


def _shard_row_plan():
    cs = D_IN // N_CHIPS
    plan = []
    for src, s, o, n in _kernel_segments():
        while n > 0:
            chip, loc = divmod(o, cs)
            half, row = divmod(loc, HR_IN)
            m = min(n, cs - loc, HR_IN - row)
            plan.append((src, s, chip, half, row, m))
            s, o, n = s + m, o + m, n - m
    return plan


SCATTER_ROWS = 512
SCATTER_SLOTS = 4


def _scatter_rows_to_shards(k_main, k_dt):
    C = k_main.shape[1]
    pieces = []
    for src, s, chip, half, row, n in _shard_row_plan():
        for o in range(0, n, SCATTER_ROWS):
            pieces.append((src, s + o, chip, half, row + o, min(SCATTER_ROWS, n - o)))
    S, lag, N = SCATTER_SLOTS, SCATTER_SLOTS // 2, len(pieces)

    def body(m_ref, d_ref, o_ref, buf, in_sems, out_sems):
        def cin(i):
            src, s, _, _, _, n = pieces[i]
            return pltpu.make_async_copy((d_ref if src else m_ref).at[pl.ds(s, n)],
                                         buf.at[i % S, pl.ds(0, n)], in_sems.at[i % S])

        def cout(i):
            _, _, chip, half, row, n = pieces[i]
            return pltpu.make_async_copy(buf.at[i % S, pl.ds(0, n)],
                                         o_ref.at[chip, half, pl.ds(row, n)], out_sems.at[i % S])

        for i in range(N + lag):
            if i < N:
                if i >= S:
                    cout(i - S).wait()
                cin(i).start()
            j = i - lag
            if 0 <= j < N:
                cin(j).wait()
                cout(j).start()
        for j in range(max(0, N - S), N):
            cout(j).wait()

    return pl.pallas_call(
        body, name="scatter_dw_in_rows", in_specs=[ANY, ANY], out_specs=ANY,
        out_shape=jax.ShapeDtypeStruct((N_CHIPS, 2, HR_IN, C), k_main.dtype),
        scratch_shapes=[pltpu.VMEM((S, SCATTER_ROWS, C), k_main.dtype),
                        pltpu.SemaphoreType.DMA((S,)), pltpu.SemaphoreType.DMA((S,))],
        compiler_params=_cparams(),
    )(k_main, k_dt)


ROWS_IN = D_IN // N_CHIPS
ROWS_IN_PAD = ROWS_IN + 8


def _cast_w_in_into_gather(wt32, chip_arr):
    R, C = wt32.shape
    hc, cbk = C // 2, 256

    def body(chip_ref, w_ref, o_ref):
        y = jnp.concatenate([w_ref[...], jnp.zeros((ROWS_IN_PAD - R, cbk), F32)], axis=0)
        odd = chip_ref[0] % 2 == 1
        o_ref[...] = jnp.where(odd, pltpu.roll(y, ROWS_IN_PAD - R, axis=0), y).astype(BF16)

    grid_spec = pltpu.PrefetchScalarGridSpec(
        num_scalar_prefetch=1, grid=(2, hc // cbk),
        in_specs=[pl.BlockSpec((R, cbk), lambda h, s, chip_ref: (0, h * (hc // cbk) + s))],
        out_specs=pl.BlockSpec((None, ROWS_IN_PAD, cbk), lambda h, s, chip_ref: (2 * chip_ref[0] + h, 0, s)))
    return pl.pallas_call(
        body, name="cast_w_in", grid_spec=grid_spec,
        out_shape=jax.ShapeDtypeStruct((N_DEV, ROWS_IN_PAD, hc), BF16),
        compiler_params=_cparams(("parallel", "parallel")),
    )(chip_arr, wt32)


def _gather_row_plan():
    segs = [(s, o, n) for src, s, o, n in _kernel_segments() if src == 0]
    plan, merges = [], []
    for k, o, n in segs:
        while n > 0:
            chip, loc = divmod(o, ROWS_IN)
            m = min(n, ROWS_IN - loc)
            ps, kd, cnt = loc + 8 * (chip % 2), k, m
            if ps % 16:
                ps, kd, cnt = ps - 8, kd - 8, cnt + 8
            if (ps + cnt) % 16:
                cnt -= 8
                merges.append((kd + cnt, chip, chip + 1))
            if cnt:
                plan.append((chip, ps, kd, cnt))
            k, o, n = k + m, o + m, n - m
    return plan, merges


def _gather_to_kernel_rows(g):
    hc = g.shape[2]
    plan, merges = _gather_row_plan()
    pieces = []
    for chip, ps, kd, n in plan:
        for o in range(0, n, SCATTER_ROWS):
            pieces.append((chip, ps + o, kd + o, min(SCATTER_ROWS, n - o)))
    S, lag, N = SCATTER_SLOTS, SCATTER_SLOTS // 2, len(pieces)

    def body(g_ref, o_ref, buf, mbuf, in_sems, out_sems, m_sems):
        def cins(i):
            chip, ps, _, n = pieces[i]
            return [pltpu.make_async_copy(g_ref.at[2 * chip + h, pl.ds(ps, n)],
                                          buf.at[i % S, pl.ds(0, n), pl.ds(h * hc, hc)], in_sems.at[i % S, h])
                    for h in range(2)]

        def cout(i):
            _, _, kd, n = pieces[i]
            return pltpu.make_async_copy(buf.at[i % S, pl.ds(0, n)], o_ref.at[pl.ds(kd, n)], out_sems.at[i % S])

        for i in range(N + lag):
            if i < N:
                if i >= S:
                    cout(i - S).wait()
                for cp in cins(i):
                    cp.start()
            j = i - lag
            if 0 <= j < N:
                for cp in cins(j):
                    cp.wait()
                cout(j).start()
        for j in range(max(0, N - S), N):
            cout(j).wait()
        for t, (kd, ce, co) in enumerate(merges):
            loads = []
            for h in range(2):
                loads.append(pltpu.make_async_copy(g_ref.at[2 * ce + h, pl.ds(ROWS_IN - 8, 16)],
                                                   mbuf.at[0, :, pl.ds(h * hc, hc)], m_sems.at[2 * h]))
                loads.append(pltpu.make_async_copy(g_ref.at[2 * co + h, pl.ds(0, 16)],
                                                   mbuf.at[1, :, pl.ds(h * hc, hc)], m_sems.at[2 * h + 1]))
            for cp in loads:
                cp.start()
            for cp in loads:
                cp.wait()
            row = lax.broadcasted_iota(jnp.int32, (16, 2 * hc), 0)
            mbuf[2] = jnp.where(row < 8, mbuf[0].astype(F32), mbuf[1].astype(F32)).astype(g.dtype)
            st = pltpu.make_async_copy(mbuf.at[2], o_ref.at[pl.ds(kd, 16)], m_sems.at[4])
            st.start()
            st.wait()

    return pl.pallas_call(
        body, name="w_in_to_kernel_rows", in_specs=[ANY], out_specs=ANY,
        out_shape=jax.ShapeDtypeStruct((D_MAIN, 2 * hc), g.dtype),
        scratch_shapes=[pltpu.VMEM((S, SCATTER_ROWS, 2 * hc), g.dtype), pltpu.VMEM((3, 16, 2 * hc), g.dtype),
                        pltpu.SemaphoreType.DMA((S, 2)), pltpu.SemaphoreType.DMA((S,)),
                        pltpu.SemaphoreType.DMA((5,))],
        compiler_params=_cparams(),
    )(g)


def _to_kernel_xbc(a):
    R = a.shape[0]
    return jnp.concatenate([a[:, :D_SSM].reshape(R, N_GROUPS, GW), a[:, D_SSM:D_SSM + 1024].reshape(R, N_GROUPS, N_STATE),
                            a[:, D_SSM + 1024:].reshape(R, N_GROUPS, N_STATE)], axis=2).reshape(R, D_XBC)


def _from_kernel_xbc(a):
    R = a.shape[0]
    g = a.reshape(R, N_GROUPS, GXBC)
    return jnp.concatenate([g[:, :, :GW].reshape(R, D_SSM), g[:, :, GW:GW + N_STATE].reshape(R, 1024),
                            g[:, :, GW + N_STATE:].reshape(R, 1024)], axis=1)


def kernel(x, norm_mix_g, w_in, ssm_conv_w, ssm_conv_b, ssm_dt_bias, ssm_A_log, ssm_D, ssm_norm_g, sc_conv_w, w_out, norm_ffn_g, w_gate, w_up, w_down, norm_final_g, loss_target, m_norm_mix_g, m_w_in, m_ssm_conv_w, m_ssm_conv_b, m_ssm_dt_bias, m_ssm_A_log, m_ssm_D, m_ssm_norm_g, m_sc_conv_w, m_w_out, m_norm_ffn_g, m_w_gate, m_w_up, m_w_down, m_norm_final_g, v_norm_mix_g, v_w_in, v_ssm_conv_w, v_ssm_conv_b, v_ssm_dt_bias, v_ssm_A_log, v_ssm_D, v_ssm_norm_g, v_sc_conv_w, v_w_out, v_norm_ffn_g, v_w_gate, v_w_up, v_w_down, v_norm_final_g):
    T = x.shape[1]
    xt = x[0]
    tgt = loss_target[0]
    cx, cy, cc = lax.axis_index("x"), lax.axis_index("y"), lax.axis_index("c")
    chip = 2 * cx + cy
    c_arr = jnp.reshape(cc, (1,)).astype(jnp.int32)
    chip_arr = jnp.reshape(chip, (1,)).astype(jnp.int32)
    place_arr = jnp.stack([chip, cc]).astype(jnp.int32)

    big = [w_in[0].T, w_out[0], w_gate[0], w_up[0], w_down[0]]
    names = ["w_in", "w_out", "w_gate", "w_up", "w_down"]
    gb_in = _cast_w_in_into_gather(big[0], chip_arr)
    cs_in, cs_conv = D_IN // N_CHIPS, D_XBC // N_CHIPS
    cw = jnp.stack([_pad_rows(ssm_conv_w[0], 8), _pad_cols(_pad_rows(sc_conv_w[0], 8), cs_conv)])
    cw_buf = lax.dynamic_update_slice(jnp.zeros((N_DEV, 8, cs_conv), F32), cw, (2 * chip, 0, 0))
    f_ss, f_rs, f_arr, f_tok = _split_start("ag_in_first_start", [gb_in, cw_buf], _build_ag_first, 4)
    gbufs = [None] + [_cast_into_gather(w, chip_arr, "cast_" + nm, deps=[f_tok]) for w, nm in zip(big[1:], names[1:])]
    n1 = _rmsnorm_fwd(xt, _tie(norm_mix_g, f_tok, "tie_ag_first"), "rmsnorm_mix")
    f_arr = _split_wait("ag_in_first_wait", f_ss, f_rs, f_arr, _build_ag_first, after=gbufs[1:] + [n1])
    g_in, cw_all = _allgather_inplace(f_arr, [("rows", (ROWS_IN_PAD // 32) * 16), ("cols", cs_conv // 2)],
                                      first_done=True)
    cw_all = cw_all.reshape(N_CHIPS, 2, 8, cs_conv)
    ssm_w8 = _to_kernel_xbc(cw_all[:, 0].transpose(1, 0, 2).reshape(8, D_XBC))
    sc_w8 = cw_all[:, 1, :, :D_MODEL // N_CHIPS].transpose(1, 0, 2).reshape(8, D_MODEL)
    ssm_bk = _to_kernel_xbc(ssm_conv_b)
    wt_main = _gather_to_kernel_rows(g_in)
    dt_rows = [jnp.concatenate([g_in[2 * ch, r0:r0 + 16], g_in[2 * ch + 1, r0:r0 + 16]], axis=1)
               for ch, r0 in ((1, ROWS_IN_PAD - 16), (2, 0))]
    wt_dt = _pad_rows(jnp.concatenate(dt_rows, axis=0), DT_PAD)
    ag_ss, ag_rs, ag_bufs, ag_tok = _split_start("ag_ici_start", gbufs[1:], _build_ag_ici, 12, after=[g_in, cw_all])

    bias_rows = _heads_to_rows(ssm_dt_bias[0])
    alog_rows = _heads_to_rows(ssm_A_log[0])
    drep = jnp.repeat(ssm_D[0], HEADDIM).reshape(1, D_SSM)

    (proj,) = _matmul([(n1, wt_main)], tb=True, out_dtypes=[F32], name="mm_proj", deps=[ag_tok])
    (dt_raw,) = _matmul([(n1, wt_dt)], tb=True, out_dtypes=[F32], name="mm_proj_dt")
    xbc = _ssm_conv_fwd(proj, ssm_w8, ssm_bk)
    dtr = jnp.pad(dt_raw[:, :N_HEADS].T.reshape(N_GROUPS, HEADS_PER_GROUP, T), ((0, 0), (0, 4), (0, 0)))
    y_ssd, hs = _ssd_fwd(xbc, dtr, bias_rows, alog_rows, drep)
    ag_bufs = _split_wait("ag_ici_wait", ag_ss, ag_rs, ag_bufs, _build_ag_ici, after=[y_ssd])
    fw_ss, fw_rs, fw_bufs, fw_tok = _split_start("ag_fwd_start", ag_bufs, _build_ag_fwd, 12)
    y_mix = _shortconv_fwd(proj, sc_w8, _gated_norm_fwd(y_ssd, proj, _tie(ssm_norm_g, fw_tok, "tie_ag_fwd")))
    gath = _split_wait("ag_fwd_wait", fw_ss, fw_rs, fw_bufs, _build_ag_fwd, after=[y_mix])
    w_out_f = gath[0].reshape(2 * D_MODEL, D_MODEL)
    w_gate3 = gath[1].reshape(N_CHIPS, D_MODEL, D_FF // N_CHIPS)
    w_up3 = gath[2].reshape(N_CHIPS, D_MODEL, D_FF // N_CHIPS)
    w_down_f = gath[3].reshape(D_FF, D_MODEL)
    (h1,) = _matmul([(y_mix, w_out_f)], out_dtypes=[F32], name="mm_out", extras=[xt],
                    epilogue=lambda acc, res: (acc + res,))
    n2 = _rmsnorm_fwd(h1, norm_ffn_g, "rmsnorm_ffn")
    g_act, u_act, a_act = _ffn_fwd(n2, w_gate3, w_up3)
    (h2,) = _matmul([(a_act, w_down_f)], out_dtypes=[F32], name="mm_down", extras=[h1],
                    epilogue=lambda acc, res: (acc + res,))

    dh2, dh2b, dg_final, loss_part = _loss_and_final_bwd(h2, tgt, norm_final_g.reshape(1, D_MODEL))
    dg_act, du_act = _matmul([(dh2b, w_down_f)], tb=True, out_dtypes=[BF16, BF16], name="mm_down_bwd",
                             tn=512, extras=[g_act, u_act], epilogue=_swiglu_bwd, nsub=2)
    (dw_down,) = _matmul([(a_act, dh2b)], ta=True, out_dtypes=[BF16], name="mm_dw_down", tm=1408, tn=512)
    (dn2,) = _matmul([(dg_act, w_gate3), (du_act, w_up3)], tb=True, b3d=True, out_dtypes=[BF16],
                     name="mm_ffn_in_bwd")
    (dw_gate,) = _matmul([(n2, dg_act)], ta=True, out_dtypes=[BF16], name="mm_dw_gate", tm=512, tn=1408,
                         col_shards=True)
    (dw_up,) = _matmul([(n2, du_act)], ta=True, out_dtypes=[BF16], name="mm_dw_up", tm=512, tn=1408,
                       col_shards=True)
    dh1, dh1b, dg_ffn = _rmsnorm_bwd(dn2, h1, norm_ffn_g, dh2, "rmsnorm_ffn_bwd")
    (dw_out,) = _matmul([(y_mix, dh1b)], ta=True, out_dtypes=[BF16], name="mm_dw_out")

    def halves(g):
        return g.reshape(N_CHIPS, 2, g.shape[1] // 2, g.shape[2])

    def landing(shape, dtype):
        return lax.empty(shape, dtype)

    names1 = names[1:]
    ps1 = [halves(dw_out.reshape(N_CHIPS, -1, D_MODEL)), halves(dw_gate), halves(dw_up),
           halves(dw_down.reshape(N_CHIPS, -1, D_MODEL))]
    r0_1 = [landing((N_CHIPS, 1) + p.shape[2:], p.dtype) for p in ps1]
    sw_ss, sw_rs, sw_arr, sw_tok = _split_start("rs1_swap_start", ps1 + r0_1, _build_rs_swap, 4)
    (dmix,) = _matmul([(dh1b, w_out_f)], tb=True, out_dtypes=[BF16], name="mm_out_bwd", deps=[sw_tok])
    dproj, dw_sc = _shortconv_bwd(dmix, proj, sc_w8)
    dy_ssd, dproj, dg_ssmnorm = _gated_norm_bwd(dmix, y_ssd, proj, ssm_norm_g, dproj)
    sw_arr = _split_wait("rs1_swap_wait", sw_ss, sw_rs, sw_arr, _build_rs_swap, after=[dy_ssd])
    qs1 = [_rs_add_pair(p, r, c_arr, "rs_add_pair_" + nm) for p, r, nm in zip(sw_arr[:4], sw_arr[4:], names1)]
    r1_1 = [landing(q.shape, BF16) for q in qs1]
    ic_ss, ic_rs, ic_arr, ic_tok = _split_start("rs1_ici_start", qs1 + r1_1, _build_rs_ici, 12)
    dxbc_act, ddtr, dbias_acc, dalog_acc, dD_acc = _ssd_bwd(
        xbc, dtr, bias_rows, alog_rows, _tie(drep, ic_tok, "tie_rs1_ici"), dy_ssd, hs)
    dproj, dw_ssmconv, db_ssmconv = _ssm_conv_bwd(dxbc_act, proj, ssm_w8, ssm_bk, dproj)
    dw_ssmconv, db_ssmconv = _from_kernel_xbc(dw_ssmconv), _from_kernel_xbc(db_ssmconv)
    ddt_raw = _pad_cols(ddtr[:, :HEADS_PER_GROUP, :].reshape(N_HEADS, T).T, DT_PAD).astype(BF16)
    (dwt_main,) = _matmul([(dproj, n1)], ta=True, out_dtypes=[F32], name="mm_dw_main")
    (dwt_dt,) = _matmul([(ddt_raw, n1)], ta=True, out_dtypes=[F32], name="mm_dw_dt")
    ic_arr = _split_wait("rs1_ici_wait", ic_ss, ic_rs, ic_arr, _build_rs_ici, after=[dwt_main])
    g1 = [_rs_add_chips(r, q, place_arr, "rs_add_chips_" + nm) for q, r, nm in zip(ic_arr[:4], ic_arr[4:], names1)]
    sh_ss, sh_rs, sh_arr, sh_tok = _split_start("rs1_share_start", g1, _build_rs_share, 4)
    p_in = _scatter_rows_to_shards(dwt_main, dwt_dt)
    s2_ss, s2_rs, s2_arr, s2_tok = _split_start(
        "rs2_swap_start", [p_in, landing((N_CHIPS, 1) + p_in.shape[2:], F32)], _build_rs_swap, 1)
    tm_pb = 1024
    mt = T // _tile(T, tm_pb)
    mt_a = max(mt // 4, 1)
    (dn1a,) = _matmul([(dproj, wt_main)], out_dtypes=[F32], name="mm_proj_bwd_a", deps=[s2_tok], tm=tm_pb,
                      m_tiles=(0, mt_a))
    s2_arr = _split_wait("rs2_swap_wait", s2_ss, s2_rs, s2_arr, _build_rs_swap, after=[dn1a])
    q_in = _rs_add_pair(s2_arr[0], s2_arr[1], c_arr, "rs_add_pair_w_in")
    i2_ss, i2_rs, i2_arr, i2_tok = _split_start(
        "rs2_ici_start", [q_in, landing(q_in.shape, BF16)], _build_rs_ici, 3)
    if mt > mt_a:
        (dn1a,) = _matmul([(dproj, wt_main)], out_dtypes=[F32], name="mm_proj_bwd_b", deps=[i2_tok], tm=tm_pb,
                          m_tiles=(mt_a, mt - mt_a), out_buf=dn1a)
    (dn1,) = _matmul([(ddt_raw, wt_dt)], out_dtypes=[BF16], name="mm_proj_dt_bwd", extras=[dn1a],
                     epilogue=lambda acc, res: (acc + res,), deps=[i2_tok])
    dx, _, dg_mix = _rmsnorm_bwd(dn1, xt, norm_mix_g, dh1, "rmsnorm_mix_bwd")
    g1 = _split_wait("rs1_share_wait", sh_ss, sh_rs, sh_arr, _build_rs_share, after=[dx])

    big_m = [m_w_in[0].T, m_w_out[0], m_w_gate[0], m_w_up[0], m_w_down[0]]
    big_v = [v_w_in[0].T, v_w_out[0], v_w_gate[0], v_w_up[0], v_w_down[0]]
    big_grads = [None] + [g.reshape(w.shape) for g, w in zip(g1, big[1:])]
    big_out = {}
    for k in range(1, 5):
        *big_out[names[k]], big_grads[k] = _adamw(big[k], big_grads[k], big_m[k], big_v[k], "adamw_" + names[k],
                                                   deps=[i2_tok], emit_g=True)
    i2_arr = _split_wait("rs2_ici_wait", i2_ss, i2_rs, i2_arr, _build_rs_ici, after=[big_out[names[4]][0], dx])
    g_in_red = _rs_add_chips(i2_arr[1], i2_arr[0], place_arr, "rs_add_chips_w_in")
    s3_ss, s3_rs, s3_arr, s3_tok = _split_start("rs2_share_start", [g_in_red], _build_rs_share, 1)

    dD = jnp.sum(dD_acc.reshape(N_HEADS, HEADDIM), axis=-1)
    heads_row = jnp.concatenate([_rows_to_heads(dbias_acc), _rows_to_heads(dalog_acc), dD,
                                 loss_part.reshape(1)]).reshape(1, -1)
    small = jnp.concatenate([
        dw_ssmconv,
        _pad_cols(dw_sc, D_XBC),
        db_ssmconv,
        jnp.concatenate([dg_mix, dg_ssmnorm], axis=1),
        jnp.concatenate([dg_ffn, dg_final], axis=1),
        _pad_cols(heads_row, D_XBC),
        jnp.zeros((4, D_XBC), F32),
    ], axis=0)
    sm_ss, sm_rs, sm_arr, sm_tok = _split_start(
        "small_gather_start", [small, landing((N_DEV,) + small.shape, F32)], _build_small_gather, N_DEV - 1,
        after=[s3_tok])
    (g_in_full,) = _split_wait("rs2_share_wait", s3_ss, s3_rs, s3_arr, _build_rs_share, after=[sm_tok])
    d_t, m_t, v_t, g_t = _adamw(big[0], g_in_full.reshape(2 * HR_IN, D_MODEL), big_m[0], big_v[0],
                                "adamw_" + names[0], emit_g=True)
    big_grads[0] = g_t.T
    big_out[names[0]] = (d_t.T, m_t.T, v_t.T)
    sm_arr = _split_wait("small_gather_wait", sm_ss, sm_rs, sm_arr, _build_small_gather, after=[d_t])
    tot = _sum_gathered(sm_arr[0], sm_arr[1], jnp.reshape(4 * cx + 2 * cy + cc, (1,)).astype(jnp.int32))
    loss = tot[19, 3 * N_HEADS]

    cs_ssm, cs_sc = D_XBC // N_CHIPS, D_MODEL // N_CHIPS
    g_ssm_conv = lax.dynamic_slice(tot[0:K_SSM], (0, chip * cs_ssm), (K_SSM, cs_ssm))
    g_sc_conv = lax.dynamic_slice(tot[8:8 + K_SC, :D_MODEL], (0, chip * cs_sc), (K_SC, cs_sc))
    small_grads = {
        "norm_mix_g": tot[17:18, :D_MODEL], "ssm_conv_w": g_ssm_conv, "ssm_conv_b": tot[16:17],
        "ssm_dt_bias": tot[19:20, 0:N_HEADS], "ssm_A_log": tot[19:20, N_HEADS:2 * N_HEADS],
        "ssm_D": tot[19:20, 2 * N_HEADS:3 * N_HEADS], "ssm_norm_g": tot[17:18, D_MODEL:],
        "sc_conv_w": g_sc_conv, "norm_ffn_g": tot[18:19, :D_MODEL], "norm_final_g": tot[18:19, D_MODEL:],
    }
    small_w = {"norm_mix_g": (norm_mix_g, m_norm_mix_g, v_norm_mix_g),
               "ssm_conv_w": (ssm_conv_w[0], m_ssm_conv_w[0], v_ssm_conv_w[0]),
               "ssm_conv_b": (ssm_conv_b, m_ssm_conv_b, v_ssm_conv_b),
               "ssm_dt_bias": (ssm_dt_bias, m_ssm_dt_bias, v_ssm_dt_bias),
               "ssm_A_log": (ssm_A_log, m_ssm_A_log, v_ssm_A_log),
               "ssm_D": (ssm_D, m_ssm_D, v_ssm_D),
               "ssm_norm_g": (ssm_norm_g, m_ssm_norm_g, v_ssm_norm_g),
               "sc_conv_w": (sc_conv_w[0], m_sc_conv_w[0], v_sc_conv_w[0]),
               "norm_ffn_g": (norm_ffn_g, m_norm_ffn_g, v_norm_ffn_g),
               "norm_final_g": (norm_final_g.reshape(1, -1), m_norm_final_g.reshape(1, -1),
                                v_norm_final_g.reshape(1, -1))}
    PW = 1024
    order = list(small_w)

    def pack(arrs):
        rows = []
        for a in arrs:
            flat = a.reshape(-1)
            n = -(-flat.shape[0] // PW) * PW
            rows.append(jnp.pad(flat, (0, n - flat.shape[0])).reshape(-1, PW))
        slab = jnp.concatenate(rows, axis=0)
        return _pad_rows(slab, -(-slab.shape[0] // 8) * 8)

    wp = pack([small_w[k][0] for k in order])
    mp = pack([small_w[k][1] for k in order])
    vp = pack([small_w[k][2] for k in order])
    gp = pack([small_grads[k] for k in order])
    sd, sm, sv = _adamw(wp, gp, mp, vp, "adamw_small")

    def unpack(slab):
        out, row = {}, 0
        for k in order:
            shape = small_w[k][0].shape
            size = 1
            for s in shape:
                size *= s
            nr = -(-size // PW)
            out[k] = slab[row:row + nr].reshape(-1)[:size].reshape(shape)
            row += nr
        return out

    s_delta, s_m, s_v = unpack(sd), unpack(sm), unpack(sv)

    big_g = dict(zip(names, big_grads))

    weight_order = ["norm_mix_g", "w_in", "ssm_conv_w", "ssm_conv_b", "ssm_dt_bias", "ssm_A_log", "ssm_D",
                    "ssm_norm_g", "sc_conv_w", "w_out", "norm_ffn_g", "w_gate", "w_up", "w_down", "norm_final_g"]
    lead = {"ssm_conv_w", "sc_conv_w", "w_in", "w_out", "w_gate", "w_up", "w_down"}

    def shaped(nm, a):
        if nm == "norm_final_g":
            return a.reshape(D_MODEL)
        return a[None] if nm in lead else a

    grads, deltas, new_m, new_v = [], [], [], []
    for nm in weight_order:
        if nm in big_out:
            g, (d, m, v) = big_g[nm], big_out[nm]
        else:
            g, d, m, v = small_grads[nm], s_delta[nm], s_m[nm], s_v[nm]
        grads.append(shaped(nm, g))
        deltas.append(shaped(nm, d))
        new_m.append(shaped(nm, m))
        new_v.append(shaped(nm, v))
    return (loss, dx[None], *grads, *deltas, *new_m, *new_v)


def _swiglu_bwd(da, dg_factor, du_factor):
    return da * dg_factor.astype(F32), da * du_factor.astype(F32)


def _ffn_fwd(n2, w_gate, w_up):
    T, K = n2.shape
    tn = w_gate.shape[2]
    N = N_CHIPS * tn
    tm = _tile(T, 512)
    sub = _tile(tm, 256)

    def body(a_ref, wg_ref, wu_ref, g_ref, u_ref, act_ref):
        for s in range(tm // sub):
            rows = pl.ds(s * sub, sub)
            a = a_ref[rows, :]
            g = jnp.dot(a, wg_ref[...], preferred_element_type=F32)
            u = jnp.dot(a, wu_ref[...], preferred_element_type=F32)
            sig = _sigmoid(g)
            sg = g * sig
            g_ref[rows, :] = (u * (sig * (1.0 + g - sg))).astype(BF16)
            u_ref[rows, :] = sg.astype(BF16)
            act_ref[rows, :] = (sg * u).astype(BF16)

    a_spec = pl.BlockSpec((tm, K), lambda j, i: (i, 0))
    b_spec = pl.BlockSpec((None, K, tn), lambda j, i: (j, 0, 0))
    o_spec = pl.BlockSpec((tm, tn), lambda j, i: (i, j))
    return pl.pallas_call(
        body, name="ffn_fwd", grid=(N // tn, T // tm),
        in_specs=[a_spec, b_spec, b_spec], out_specs=[o_spec] * 3,
        out_shape=[jax.ShapeDtypeStruct((T, N), BF16)] * 3,
        compiler_params=_cparams(("parallel", "parallel")),
    )(n2, w_gate, w_up)
```

```python
import functools

import jax
import jax.numpy as jnp
from jax import lax
from jax.experimental import pallas as pl
from jax.experimental.pallas import tpu as pltpu

F32 = jnp.float32
BF16 = jnp.bfloat16
MESH = pl.DeviceIdType.MESH

D_MODEL = 2048
D_SSM = 2048
HEADDIM = 64
N_HEADS = 32
N_GROUPS = 8
HEADS_PER_GROUP = 4
N_STATE = 128
CHUNK = 128
K_SSM = 4
K_SC = 3
D_XBC = 4096
D_FF = 5632
D_IN = 12320
D_MAIN = 12288
OFF_XBC, OFF_CB, OFF_CC, OFF_CX = 2048, 6144, 8192, 10240
DT_PAD = 128
EPS = 1e-5
N_CHIPS = 4
N_DEV = 8

ADAM_LR = 0.001
ADAM_B1 = 0.9
ADAM_B2 = 0.999
ADAM_EPS = 1e-08
ADAM_WD = 0.01
ADAM_STEP = 10

V7X_VMEM_BYTES = 64 * 1024 * 1024
VMEM_LIMIT = V7X_VMEM_BYTES - 8 * 1024 * 1024


def _cparams(sem=None):
    if sem is None:
        return pltpu.CompilerParams(vmem_limit_bytes=VMEM_LIMIT)
    return pltpu.CompilerParams(dimension_semantics=sem, vmem_limit_bytes=VMEM_LIMIT)


def _tile(dim, pref, unit=128):
    best = None
    t = unit
    while t <= min(dim, pref):
        if dim % t == 0:
            best = t
        t += unit
    return best if best is not None else dim


def _sigmoid(x):
    return 1.0 / (1.0 + jnp.exp(-x))


def _silu(x):
    return x * _sigmoid(x)


def _dsilu(x):
    s = _sigmoid(x)
    return s * (1.0 + x * (1.0 - s))


def _softplus(x):
    return jnp.maximum(x, 0.0) + jnp.log(1.0 + jnp.exp(-jnp.abs(x)))


MATMUL_VMEM_BUDGET = 44 * 1024 * 1024


def _matmul(pairs, *, ta=False, tb=False, out_dtypes, name, tm=1024, tn=1024, tk=None, extras=(), epilogue=None,
            deps=(), col_shards=False, nsub=1, b3d=False, m_tiles=None, out_buf=None):
    a0, b0 = pairs[0]
    M, K = (a0.shape[1], a0.shape[0]) if ta else a0.shape
    if b3d:
        N = b0.shape[1] if tb else b0.shape[0] * b0.shape[2]
        tk, tn = (b0.shape[2], tn) if tb else (tk, b0.shape[2])
    else:
        N = b0.shape[0] if tb else b0.shape[1]
    tm, tn = _tile(M, tm, 8 if M % 128 else 128), _tile(N, tn)
    npair, nex, ndep, nout = len(pairs), len(extras), len(deps), len(out_dtypes)
    if tk is None:
        fixed = 2 * tm * tn * (sum(jnp.dtype(d).itemsize for d in out_dtypes) + sum(e.dtype.itemsize for e in extras))
        tk = K
        while tk > 128 and (K % tk or tk % 128 or
                            fixed + 2 * npair * 2 * tk * (tm + tn) + (tm * tn * 4 if tk < K else 0) > MATMUL_VMEM_BUDGET):
            tk -= 128
    else:
        tk = _tile(K, tk)
    nk = K // tk
    if nk > 1 or tm % nsub or (tm // nsub) % 128:
        nsub = 1
    sub = tm // nsub
    dims = (((0 if ta else 1,), (1 if tb else 0,)), ((), ()))
    i0, mi = m_tiles if m_tiles is not None else (0, M // tm)
    nbuf = 0 if out_buf is None else 1

    def body(*refs):
        a_refs = refs[0:2 * npair:2]
        b_refs = refs[1:2 * npair:2]
        ex_refs = refs[2 * npair:2 * npair + nex]
        o_refs = refs[2 * npair + nex + ndep + nbuf:2 * npair + nex + ndep + nbuf + nout]

        def dots(rows):
            s = None
            for a_ref, b_ref in zip(a_refs, b_refs):
                a = a_ref[...] if rows is None else (a_ref[:, rows] if ta else a_ref[rows, :])
                d = lax.dot_general(a, b_ref[...], dims, preferred_element_type=F32)
                s = d if s is None else s + d
            return s

        def finish(r, rows):
            ex = [e[...] if rows is None else e[rows, :] for e in ex_refs]
            outs = (r,) if epilogue is None else epilogue(r, *ex)
            for o_ref, o in zip(o_refs, outs):
                if rows is None:
                    o_ref[...] = o.astype(o_ref.dtype)
                else:
                    o_ref[rows, :] = o.astype(o_ref.dtype)

        if nk == 1:
            for s in range(nsub):
                rows = None if nsub == 1 else pl.ds(s * sub, sub)
                finish(dots(rows), rows)
            return

        acc = refs[-1]
        k = pl.program_id(2)

        @pl.when(k == 0)
        def _():
            acc[...] = dots(None)

        @pl.when(jnp.logical_and(k > 0, k < nk - 1))
        def _():
            acc[...] += dots(None)

        @pl.when(k == nk - 1)
        def _():
            finish(acc[...] + dots(None), None)

    a_spec = (pl.BlockSpec((tk, tm), lambda i, j, k: (k, i + i0)) if ta
              else pl.BlockSpec((tm, tk), lambda i, j, k: (i + i0, k)))
    if b3d:
        b_spec = (pl.BlockSpec((None, tn, tk), lambda i, j, k: (k, j, 0)) if tb
                  else pl.BlockSpec((None, tk, tn), lambda i, j, k: (j, k, 0)))
    else:
        b_spec = (pl.BlockSpec((tn, tk), lambda i, j, k: (j, k)) if tb
                  else pl.BlockSpec((tk, tn), lambda i, j, k: (k, j)))
    e_spec = pl.BlockSpec((tm, tn), lambda i, j, k: (i + i0, j))
    if col_shards:
        o_spec = pl.BlockSpec((None, tm, tn), lambda i, j, k: (j, i + i0, 0))
        o_shape = (N // tn, M, tn)
    else:
        o_spec, o_shape = e_spec, (M, N)
    args, in_specs = [], []
    for a, b in pairs:
        args += [a, b]
        in_specs += [a_spec, b_spec]
    args += list(extras) + list(deps) + ([] if out_buf is None else [out_buf])
    in_specs += [e_spec] * nex + [ANY] * (ndep + nbuf)
    outs = pl.pallas_call(
        body,
        name=name,
        grid=(mi, N // tn, nk),
        in_specs=in_specs,
        out_specs=[o_spec] * nout,
        out_shape=[jax.ShapeDtypeStruct(o_shape, dt) for dt in out_dtypes],
        input_output_aliases={} if out_buf is None else {len(args) - 1: 0},
        scratch_shapes=[pltpu.VMEM((tm, tn), F32)] if nk > 1 else [],
        compiler_params=_cparams(("parallel", "parallel", "arbitrary")),
    )(*args)
    return outs


def _cast_into_gather(w, chip_arr, name, deps=()):
    R, C = w.shape
    hr = R // 2
    tr = _tile(hr, 512, 8)
    nb = hr // tr

    def body(chip_ref, w_ref, *rest):
        rest[-1][...] = w_ref[...].astype(BF16)

    grid_spec = pltpu.PrefetchScalarGridSpec(
        num_scalar_prefetch=1, grid=(2, nb),
        in_specs=[pl.BlockSpec((tr, C), lambda h, i, chip_ref: (h * nb + i, 0))] + [ANY] * len(deps),
        out_specs=pl.BlockSpec((None, tr, C), lambda h, i, chip_ref: (2 * chip_ref[0] + h, i, 0)))
    return pl.pallas_call(
        body, name=name, grid_spec=grid_spec,
        out_shape=jax.ShapeDtypeStruct((N_DEV, hr, C), BF16),
        compiler_params=_cparams(("parallel", "parallel")),
    )(chip_arr, w, *deps)


def _tie(small, token, name):
    def body(s_ref, t_ref, o_ref):
        o_ref[...] = s_ref[...]

    vm = pl.BlockSpec(memory_space=pltpu.VMEM)
    return pl.pallas_call(body, name=name, in_specs=[vm, ANY], out_specs=vm,
                          out_shape=jax.ShapeDtypeStruct(small.shape, small.dtype))(small, token)


def _rmsnorm_fwd(x, g, name):
    T, D = x.shape
    tt = _tile(T, 256)

    def body(x_ref, g_ref, n_ref):
        xv = x_ref[...]
        r = lax.rsqrt(jnp.mean(xv * xv, axis=-1, keepdims=True) + EPS)
        n_ref[...] = (xv * r * g_ref[...]).astype(BF16)

    return pl.pallas_call(
        body, name=name, grid=(T // tt,),
        in_specs=[pl.BlockSpec((tt, D), lambda i: (i, 0)), pl.BlockSpec((1, D), lambda i: (0, 0))],
        out_specs=pl.BlockSpec((tt, D), lambda i: (i, 0)),
        out_shape=jax.ShapeDtypeStruct((T, D), BF16),
        compiler_params=_cparams(("parallel",)),
    )(x, g)


def _rmsnorm_bwd(dn, x, g, res, name):
    T, D = x.shape
    tt = _tile(T, 256)

    def body(dn_ref, x_ref, g_ref, res_ref, dx_ref, dxb_ref, dg_ref):
        @pl.when(pl.program_id(0) == 0)
        def _():
            dg_ref[...] = jnp.zeros_like(dg_ref)

        xv = x_ref[...]
        dy = dn_ref[...].astype(F32)
        r = lax.rsqrt(jnp.mean(xv * xv, axis=-1, keepdims=True) + EPS)
        xhat = xv * r
        dxh = dy * g_ref[...]
        dx = res_ref[...] + r * (dxh - xhat * jnp.mean(dxh * xhat, axis=-1, keepdims=True))
        dx_ref[...] = dx
        dxb_ref[...] = dx.astype(BF16)
        dg_ref[...] += jnp.sum(dy * xhat, axis=0, keepdims=True)

    tok = pl.BlockSpec((tt, D), lambda i: (i, 0))
    vec = pl.BlockSpec((1, D), lambda i: (0, 0))
    return pl.pallas_call(
        body, name=name, grid=(T // tt,),
        in_specs=[tok, tok, vec, tok],
        out_specs=[tok, tok, vec],
        out_shape=[jax.ShapeDtypeStruct((T, D), F32), jax.ShapeDtypeStruct((T, D), BF16),
                   jax.ShapeDtypeStruct((1, D), F32)],
        compiler_params=_cparams(("arbitrary",)),
    )(dn, x, g, res)


def _loss_and_final_bwd(h2, target, gf):
    T, D = h2.shape
    tt = _tile(T, 256)

    def body(h_ref, t_ref, g_ref, dh_ref, dhb_ref, dg_ref, loss_ref):
        @pl.when(pl.program_id(0) == 0)
        def _():
            dg_ref[...] = jnp.zeros_like(dg_ref)
            loss_ref[...] = jnp.zeros_like(loss_ref)

        xv = h_ref[...]
        r = lax.rsqrt(jnp.mean(xv * xv, axis=-1, keepdims=True) + EPS)
        xhat = xv * r
        err = xhat * g_ref[...] - t_ref[...]
        loss_ref[...] += 0.5 * jnp.sum(jnp.mean(err * err, axis=-1, keepdims=True), axis=0, keepdims=True)
        dy = err * (1.0 / D)
        dxh = dy * g_ref[...]
        dx = r * (dxh - xhat * jnp.mean(dxh * xhat, axis=-1, keepdims=True))
        dh_ref[...] = dx
        dhb_ref[...] = dx.astype(BF16)
        dg_ref[...] += jnp.sum(dy * xhat, axis=0, keepdims=True)

    tok = pl.BlockSpec((tt, D), lambda i: (i, 0))
    vec = pl.BlockSpec((1, D), lambda i: (0, 0))
    return pl.pallas_call(
        body, name="loss_final_bwd", grid=(T // tt,),
        in_specs=[tok, tok, vec],
        out_specs=[tok, tok, vec, pl.BlockSpec((1, 1), lambda i: (0, 0))],
        out_shape=[jax.ShapeDtypeStruct((T, D), F32), jax.ShapeDtypeStruct((T, D), BF16),
                   jax.ShapeDtypeStruct((1, D), F32), jax.ShapeDtypeStruct((1, 1), F32)],
        compiler_params=_cparams(("arbitrary",)),
    )(h2, target, gf)


def _gated_norm_fwd(y, proj, g):
    T, D = y.shape
    tt = _tile(T, 256)

    def body(y_ref, z_ref, g_ref, o_ref):
        yg = y_ref[...] * _silu(z_ref[...])
        r = lax.rsqrt(jnp.mean(yg * yg, axis=-1, keepdims=True) + EPS)
        o_ref[...] = (yg * r * g_ref[...]).astype(BF16)

    tok = pl.BlockSpec((tt, D), lambda i: (i, 0))
    return pl.pallas_call(
        body, name="gated_norm_fwd", grid=(T // tt,),
        in_specs=[tok, tok, pl.BlockSpec((1, D), lambda i: (0, 0))],
        out_specs=tok,
        out_shape=jax.ShapeDtypeStruct((T, 2 * D_MODEL), BF16),
        compiler_params=_cparams(("parallel",)),
    )(y, proj, g)


def _gated_norm_bwd(dmix, y, proj, g, dproj):
    T, D = y.shape
    tt = _tile(T, 256)

    def body(do_ref, y_ref, z_ref, g_ref, dp_ref, dy_ref, dz_ref, dg_ref):
        @pl.when(pl.program_id(0) == 0)
        def _():
            dg_ref[...] = jnp.zeros_like(dg_ref)

        yv, zv = y_ref[...], z_ref[...]
        do = do_ref[...].astype(F32)
        sz = _silu(zv)
        yg = yv * sz
        r = lax.rsqrt(jnp.mean(yg * yg, axis=-1, keepdims=True) + EPS)
        xhat = yg * r
        dxh = do * g_ref[...]
        dyg = r * (dxh - xhat * jnp.mean(dxh * xhat, axis=-1, keepdims=True))
        dy_ref[...] = dyg * sz
        dz_ref[...] = (dyg * yv * _dsilu(zv)).astype(BF16)
        dg_ref[...] += jnp.sum(do * xhat, axis=0, keepdims=True)

    tok = pl.BlockSpec((tt, D), lambda i: (i, 0))
    vec = pl.BlockSpec((1, D), lambda i: (0, 0))
    return pl.pallas_call(
        body, name="gated_norm_bwd", grid=(T // tt,),
        in_specs=[tok, tok, tok, vec, ANY],
        out_specs=[tok, tok, vec],
        out_shape=[jax.ShapeDtypeStruct((T, D), F32), jax.ShapeDtypeStruct(dproj.shape, BF16),
                   jax.ShapeDtypeStruct((1, D), F32)],
        input_output_aliases={4: 1},
        compiler_params=_cparams(("arbitrary",)),
    )(dmix, y, proj, g, dproj)


HALO = 8


def _shift_down(cur, prev8, s):
    ext = jnp.concatenate([prev8, cur], axis=0)
    return pltpu.roll(ext, s, axis=0)[HALO:]


def _shift_up(cur, next8, s):
    n = cur.shape[0]
    ext = jnp.concatenate([cur, next8], axis=0)
    return pltpu.roll(ext, n + HALO - s, axis=0)[:n]


def _conv_specs(tt, cb, col_off_blocks, nt):
    hb = tt // HALO
    cur = pl.BlockSpec((tt, cb), lambda j, i: (i, col_off_blocks + j))
    prev = pl.BlockSpec((HALO, cb), lambda j, i: (jnp.maximum(i * hb - 1, 0), col_off_blocks + j))
    nxt = pl.BlockSpec((HALO, cb), lambda j, i: (jnp.minimum((i + 1) * hb, nt * hb - 1), col_off_blocks + j))
    return cur, prev, nxt


def _taps(cur, prev8, K):
    return [_shift_down(cur, prev8, K - 1 - k) for k in range(K - 1)] + [cur]


def _conv_of_taps(taps, w):
    y = taps[-1] * w[len(taps) - 1:len(taps), :]
    for k, t in enumerate(taps[:-1]):
        y = y + t * w[k:k + 1, :]
    return y


def _causal_conv(cur, prev8, w, K):
    return _conv_of_taps(_taps(cur, prev8, K), w)


def _anticausal_conv(cur, next8, w, K):
    y = cur * w[K - 1:K, :]
    for k in range(K - 1):
        y = y + _shift_up(cur, next8, K - 1 - k) * w[k:k + 1, :]
    return y


def _ssm_conv_fwd(proj, w8, b):
    T = proj.shape[0]
    tt, cb = _tile(T, 512), 512
    nt = T // tt
    cur, prev, _ = _conv_specs(tt, cb, OFF_XBC // cb, nt)

    def body(u_ref, up_ref, w_ref, b_ref, o_ref):
        first = pl.program_id(1) == 0
        p8 = jnp.where(first, 0.0, up_ref[...])
        pre = _causal_conv(u_ref[...], p8, w_ref[...], K_SSM) + b_ref[...]
        o_ref[...] = _silu(pre)

    return pl.pallas_call(
        body, name="ssm_conv_fwd", grid=(D_XBC // cb, nt),
        in_specs=[cur, prev, pl.BlockSpec((8, cb), lambda j, i: (0, j)), pl.BlockSpec((1, cb), lambda j, i: (0, j))],
        out_specs=pl.BlockSpec((tt, cb), lambda j, i: (i, j)),
        out_shape=jax.ShapeDtypeStruct((T, D_XBC), F32),
        compiler_params=_cparams(("parallel", "parallel")),
    )(proj, proj, w8, b)


def _ssm_conv_bwd(dact, proj, w8, b, dproj):
    T = proj.shape[0]
    tt, cb = _tile(T, 512), 512
    nt = T // tt
    cur, prev, nxt = _conv_specs(tt, cb, OFF_XBC // cb, nt)
    dcur, dprev, dnxt = _conv_specs(tt, cb, 0, nt)

    def dpre_of(d, u, p8, w, bb):
        pre = _causal_conv(u, p8, w, K_SSM) + bb
        return d * _dsilu(pre)

    def body(d_ref, dn_ref, u_ref, up_ref, un_ref, w_ref, b_ref, dp_ref, dx_ref, dw_ref, db_ref):
        i = pl.program_id(1)

        @pl.when(i == 0)
        def _():
            dw_ref[...] = jnp.zeros_like(dw_ref)
            db_ref[...] = jnp.zeros_like(db_ref)

        w, bb = w_ref[...], b_ref[...]
        u = u_ref[...]
        p8 = jnp.where(i == 0, 0.0, up_ref[...])
        taps = _taps(u, p8, K_SSM)
        dpre = d_ref[...] * _dsilu(_conv_of_taps(taps, w) + bb)
        un = un_ref[...]
        dpre_n = dpre_of(dn_ref[...], un, u[tt - HALO:, :], w, bb)
        dpre_n = jnp.where(i == nt - 1, 0.0, dpre_n)
        dx_ref[...] = _anticausal_conv(dpre, dpre_n, w, K_SSM).astype(BF16)
        rows = [jnp.sum(dpre * t, axis=0, keepdims=True) for t in taps]
        rows.append(jnp.zeros((8 - K_SSM, cb), F32))
        dw_ref[...] += jnp.concatenate(rows, axis=0)
        db_ref[...] += jnp.sum(dpre, axis=0, keepdims=True)

    wspec = pl.BlockSpec((8, cb), lambda j, i: (0, j))
    bspec = pl.BlockSpec((1, cb), lambda j, i: (0, j))
    return pl.pallas_call(
        body, name="ssm_conv_bwd", grid=(D_XBC // cb, nt),
        in_specs=[dcur, dnxt, cur, prev, nxt, wspec, bspec, ANY],
        out_specs=[pl.BlockSpec((tt, cb), lambda j, i: (i, OFF_XBC // cb + j)), wspec, bspec],
        out_shape=[jax.ShapeDtypeStruct(dproj.shape, BF16), jax.ShapeDtypeStruct((8, D_XBC), F32),
                   jax.ShapeDtypeStruct((1, D_XBC), F32)],
        input_output_aliases={7: 0},
        compiler_params=_cparams(("parallel", "arbitrary")),
    )(dact, dact, proj, proj, proj, w8, b, dproj)


SCB = 512
SC3 = 3 * SCB


def _sc_specs(tt, nt):
    hb = tt // HALO
    cur = pl.BlockSpec((tt, SC3), lambda j, i: (i, OFF_CB // SC3 + j))
    prev = pl.BlockSpec((HALO, SC3), lambda j, i: (jnp.maximum(i * hb - 1, 0), OFF_CB // SC3 + j))
    nxt = pl.BlockSpec((HALO, SC3), lambda j, i: (jnp.minimum((i + 1) * hb, nt * hb - 1), OFF_CB // SC3 + j))
    return cur, prev, nxt


def _shortconv_fwd(proj, w8, ymix):
    T = proj.shape[0]
    tt = _tile(T, 512)
    nt = T // tt
    cur, prev, _ = _sc_specs(tt, nt)

    def body(p_ref, pp_ref, w_ref, y_ref, o_ref):
        p, pp = p_ref[...], pp_ref[...]
        v = p[:, SCB:2 * SCB] * p[:, 2 * SCB:]
        vp = jnp.where(pl.program_id(1) == 0, 0.0, pp[:, SCB:2 * SCB] * pp[:, 2 * SCB:])
        o_ref[...] = (p[:, :SCB] * _causal_conv(v, vp, w_ref[...], K_SC)).astype(BF16)

    return pl.pallas_call(
        body, name="shortconv_fwd", grid=(D_MODEL // SCB, nt),
        in_specs=[cur, prev, pl.BlockSpec((8, SCB), lambda j, i: (0, j)), ANY],
        out_specs=pl.BlockSpec((tt, SCB), lambda j, i: (i, D_SSM // SCB + j)),
        out_shape=jax.ShapeDtypeStruct(ymix.shape, BF16),
        input_output_aliases={3: 0},
        compiler_params=_cparams(("parallel", "parallel")),
    )(proj, proj, w8, ymix)


def _shortconv_bwd(dmix, proj, w8):
    T = proj.shape[0]
    tt = _tile(T, 512)
    nt = T // tt
    hb = tt // HALO
    cur, prev, nxt = _sc_specs(tt, nt)
    d_s = pl.BlockSpec((tt, SCB), lambda j, i: (i, D_SSM // SCB + j))
    dn_s = pl.BlockSpec((HALO, SCB), lambda j, i: (jnp.minimum((i + 1) * hb, nt * hb - 1), D_SSM // SCB + j))

    def body(d_ref, dn_ref, p_ref, pp_ref, pn_ref, w_ref, dp_ref, dw_ref):
        i = pl.program_id(1)

        @pl.when(i == 0)
        def _():
            dw_ref[...] = jnp.zeros_like(dw_ref)

        w = w_ref[...]
        p, pp = p_ref[...], pp_ref[...]
        gb, gc, u = p[:, :SCB], p[:, SCB:2 * SCB], p[:, 2 * SCB:]
        v = gc * u
        vp = jnp.where(i == 0, 0.0, pp[:, SCB:2 * SCB] * pp[:, 2 * SCB:])
        d = d_ref[...].astype(F32)
        taps = _taps(v, vp, K_SC)
        dp_ref[:, :SCB] = (d * _conv_of_taps(taps, w)).astype(BF16)
        dcv = d * gb
        dcv_n = jnp.where(i == nt - 1, 0.0, dn_ref[...].astype(F32) * pn_ref[:, :SCB])
        dv = _anticausal_conv(dcv, dcv_n, w, K_SC)
        dp_ref[:, SCB:2 * SCB] = (dv * u).astype(BF16)
        dp_ref[:, 2 * SCB:] = (dv * gc).astype(BF16)
        rows = [jnp.sum(dcv * t, axis=0, keepdims=True) for t in taps]
        rows.append(jnp.zeros((8 - K_SC, SCB), F32))
        dw_ref[...] += jnp.concatenate(rows, axis=0)

    wspec = pl.BlockSpec((8, SCB), lambda j, i: (0, j))
    return pl.pallas_call(
        body, name="shortconv_bwd", grid=(D_MODEL // SCB, nt),
        in_specs=[d_s, dn_s, cur, prev, nxt, wspec],
        out_specs=[cur, wspec],
        out_shape=[jax.ShapeDtypeStruct((T, D_MAIN), BF16), jax.ShapeDtypeStruct((8, D_MODEL), F32)],
        compiler_params=_cparams(("parallel", "arbitrary")),
    )(dmix, dmix, proj, proj, proj, w8)


GW = HEADS_PER_GROUP * HEADDIM


def _dot(a, b):
    return jnp.dot(a.astype(BF16), b.astype(BF16), preferred_element_type=F32)


def _dot_nt(a, b):
    return lax.dot_general(a.astype(BF16), b.astype(BF16), (((1,), (1,)), ((), ())), preferred_element_type=F32)


def _dot_tn(a, b):
    return lax.dot_general(a.astype(BF16), b.astype(BF16), (((0,), (0,)), ((), ())), preferred_element_type=F32)


def _bf16_terms(x, n):
    terms, r = [], x
    for _ in range(n):
        t = r.astype(BF16)
        terms.append(t)
        r = r - t.astype(F32)
    return terms


def _dot_sel(a, sel, n=2):
    s = sel.astype(BF16)
    return sum(jnp.dot(t, s, preferred_element_type=F32) for t in _bf16_terms(a, n))


def _sel_dot(sel, b, n=2):
    s = sel.astype(BF16)
    return sum(jnp.dot(s, t, preferred_element_type=F32) for t in _bf16_terms(b, n))


def _sel_dot_nt(sel, b, n=2):
    s = sel.astype(BF16)
    return sum(lax.dot_general(s, t, (((1,), (1,)), ((), ())), preferred_element_type=F32)
               for t in _bf16_terms(b, n))


def _head_cols(rows):
    parts = [jnp.broadcast_to(rows[r:r + 1, :], (HEADDIM, CHUNK)) for r in range(HEADS_PER_GROUP)]
    return jnp.concatenate(parts, axis=0).T


def _head_rows(rows):
    parts = [jnp.broadcast_to(rows[r:r + 1, :], (HEADDIM, N_STATE)) for r in range(HEADS_PER_GROUP)]
    return jnp.concatenate(parts, axis=0)


def _ssd_common(dtr, bias, alog):
    dt = _softplus(dtr + bias)
    A = -jnp.exp(alog)
    a = dt * A
    ki = lax.broadcasted_iota(jnp.int32, (CHUNK, CHUNK), 0)
    si = lax.broadcasted_iota(jnp.int32, (CHUNK, CHUNK), 1)
    upper = (ki <= si).astype(F32)
    cs = _dot_sel(a, upper, 3)
    cs_last = jnp.broadcast_to(cs[:, CHUNK - 1:CHUNK], (8, CHUNK))
    return dt, A, a, cs, cs_last


def _decay_matrix(cs, r):
    li = lax.broadcasted_iota(jnp.int32, (CHUNK, CHUNK), 0)
    si = lax.broadcasted_iota(jnp.int32, (CHUNK, CHUNK), 1)
    causal = li >= si
    R = jnp.broadcast_to(cs[r:r + 1, :], (CHUNK, CHUNK))
    seg = jnp.where(causal, R.T - R, 0.0)
    return jnp.where(causal, jnp.exp(seg), 0.0)


def _decay_cat(cs):
    return jnp.concatenate([_decay_matrix(cs, r) for r in range(HEADS_PER_GROUP)], axis=1)


def _lanes4(m):
    return jnp.concatenate([m] * HEADS_PER_GROUP, axis=1)


def _head_blocks(v):
    col = lax.broadcasted_iota(jnp.int32, v.shape, 1) // HEADDIM
    return jnp.concatenate([jnp.where(col == r, v, jnp.zeros_like(v)) for r in range(HEADS_PER_GROUP)], axis=0)


GXBC = GW + 2 * N_STATE


GS_FWD = 8
GS_BWD = 8


def _ssd_in_specs(nc, rev):
    GS = GS_BWD if rev else GS_FWD
    cix = (lambda c: nc - 1 - c) if rev else (lambda c: c)
    x_s = pl.BlockSpec((CHUNK, GS * GW), lambda g, c: (cix(c), g))
    xbc_s = pl.BlockSpec((CHUNK, GS * GXBC), lambda g, c: (cix(c), g))
    dtr_s = pl.BlockSpec((GS, 8, CHUNK), lambda g, c: (g, 0, cix(c)))
    row_s = pl.BlockSpec((GS, 8, CHUNK), lambda g, c: (g, 0, 0))
    drep_s = pl.BlockSpec((1, GS * GW), lambda g, c: (0, g))
    hs_s = pl.BlockSpec((1, GS * GW, N_STATE), lambda g, c: (cix(c), g, 0))
    return x_s, xbc_s, dtr_s, row_s, drep_s, hs_s


def _xbc_parts(xbc_ref, gi):
    o = gi * GXBC
    return xbc_ref[:, o:o + GW], xbc_ref[:, o + GW:o + GW + N_STATE], xbc_ref[:, o + GW + N_STATE:o + GXBC]


def _ssd_fwd(xbc, dtr, bias, alog, drep):
    T = xbc.shape[0]
    nc = T // CHUNK
    x_s, xbc_s, dtr_s, row_s, drep_s, hs_s = _ssd_in_specs(nc, False)

    def body(xbc_ref, dtr_ref, bias_ref, alog_ref, drep_ref, y_ref, hs_ref, h_scr):
        @pl.when(pl.program_id(1) == 0)
        def _():
            h_scr[...] = jnp.zeros_like(h_scr)

        for gi in range(GS_FWD):
            cols, rows = slice(gi * GW, (gi + 1) * GW), pl.ds(gi * GW, GW)
            x, Bm, Cm = _xbc_parts(xbc_ref, gi)
            dt, A, a, cs, cs_last = _ssd_common(dtr_ref[gi], bias_ref[gi], alog_ref[gi])
            E = _head_cols(jnp.exp(cs))
            W = _head_cols(jnp.exp(cs_last - cs) * dt)
            X = (x * _head_cols(dt)).astype(BF16)
            CB = _dot_nt(Cm, Bm)
            col = lax.broadcasted_iota(jnp.int32, (CHUNK, GW), 1) // HEADDIM
            y = jnp.zeros((CHUNK, GW), F32)
            for r in range(HEADS_PER_GROUP):
                y = y + jnp.where(col == r, _dot(CB * _decay_matrix(cs, r), X), 0.0)
            h = h_scr[rows, :]
            hs_ref[0, rows, :] = h
            y = y + _dot_nt(Cm, h) * E
            y_ref[:, cols] = y + drep_ref[:, cols] * x
            h_scr[rows, :] = h * _head_rows(jnp.exp(cs_last)) + _dot_tn(x * W, Bm)

    return pl.pallas_call(
        body, name="ssd_fwd", grid=(N_GROUPS // GS_FWD, nc),
        in_specs=[xbc_s, dtr_s, row_s, row_s, drep_s],
        out_specs=[x_s, hs_s],
        out_shape=[jax.ShapeDtypeStruct((T, D_SSM), F32), jax.ShapeDtypeStruct((nc, D_SSM, N_STATE), F32)],
        scratch_shapes=[pltpu.VMEM((GS_FWD * GW, N_STATE), F32)],
        compiler_params=_cparams(("parallel", "arbitrary")),
    )(xbc, dtr, bias, alog, drep)


def _ssd_bwd(xbc, dtr, bias, alog, drep, dy, hs):
    T = xbc.shape[0]
    nc = T // CHUNK
    x_s, xbc_s, dtr_s, row_s, drep_s, hs_s = _ssd_in_specs(nc, True)

    def body(xbc_ref, dtr_ref, bias_ref, alog_ref, drep_ref, dy_ref, hs_ref,
             dxbc_ref, ddtr_ref, dbias_ref, dalog_ref, dd_ref, dh_scr):
        @pl.when(pl.program_id(1) == 0)
        def _():
            dh_scr[...] = jnp.zeros_like(dh_scr)
            dbias_ref[...] = jnp.zeros_like(dbias_ref)
            dalog_ref[...] = jnp.zeros_like(dalog_ref)
            dd_ref[...] = jnp.zeros_like(dd_ref)

        for gi in range(GS_BWD):
            one_group(gi, xbc_ref, dtr_ref, bias_ref, alog_ref, drep_ref, dy_ref, hs_ref,
                      dxbc_ref, ddtr_ref, dbias_ref, dalog_ref, dd_ref, dh_scr)

    def one_group(gi, xbc_ref, dtr_ref, bias_ref, alog_ref, drep_ref, dy_ref, hs_ref,
                  dxbc_ref, ddtr_ref, dbias_ref, dalog_ref, dd_ref, dh_scr):
        cols, rows, o = slice(gi * GW, (gi + 1) * GW), pl.ds(gi * GW, GW), gi * GXBC
        x, Bm, Cm = _xbc_parts(xbc_ref, gi)
        dY = dy_ref[:, cols]
        dt, A, a, cs, cs_last = _ssd_common(dtr_ref[gi], bias_ref[gi], alog_ref[gi])
        E = _head_cols(jnp.exp(cs))
        DT = _head_cols(dt)
        Wd = _head_cols(jnp.exp(cs_last - cs))
        X = x * DT
        h = hs_ref[0, rows, :]
        dS = dh_scr[rows, :]
        CB = _dot_nt(Cm, Bm)
        rowid = lax.broadcasted_iota(jnp.int32, (8, CHUNK), 0)
        lane = lax.broadcasted_iota(jnp.int32, (8, CHUNK), 1)
        hsel = (lax.broadcasted_iota(jnp.int32, (8, GW), 1) // HEADDIM
                == lax.broadcasted_iota(jnp.int32, (8, GW), 0)).astype(F32)
        hsel_l = (lax.broadcasted_iota(jnp.int32, (8, HEADS_PER_GROUP * CHUNK), 1) // CHUNK
                  == lax.broadcasted_iota(jnp.int32, (8, HEADS_PER_GROUP * CHUNK), 0)).astype(F32)

        Lc, CBc = _decay_cat(cs), _lanes4(CB)
        Mc = CBc * Lc
        GLc = _dot_nt(dY, _head_blocks(X.astype(BF16))) * Lc
        Wc = GLc * CBc
        colsum = jnp.sum(Wc, axis=0, keepdims=True)
        dcs = _sel_dot_nt(hsel_l, Wc)
        dCB = jnp.zeros((CHUNK, CHUNK), F32)
        for r in range(HEADS_PER_GROUP):
            blk = slice(r * CHUNK, (r + 1) * CHUNK)
            dCB = dCB + GLc[:, blk]
            dcs = dcs - jnp.where(rowid == r, colsum[:, blk], 0.0)
        m_stack = jnp.concatenate([Mc[:, r * CHUNK:(r + 1) * CHUNK].astype(BF16) for r in range(HEADS_PER_GROUP)],
                                  axis=0)
        dX = lax.dot_general(m_stack, _head_blocks(dY.astype(BF16)), (((0,), (0,)), ((), ())),
                             preferred_element_type=F32)
        dC = _dot(dCB, Bm)
        dB = _dot_tn(dCB, Cm)
        T1 = _dot_nt(Bm, dS)
        dX = dX + T1 * Wd
        dB = dB + _dot(X * Wd, dS)
        pdec = _sel_dot_nt(hsel, X * T1 * Wd)
        dcs = dcs - pdec
        dlast = jnp.sum(pdec, axis=1, keepdims=True) \
            + jnp.exp(cs_last[:, 0:1]) * jnp.sum(_sel_dot(hsel, dS * h), axis=1, keepdims=True)
        dYE = dY * E
        dC = dC + _dot(dYE, h)
        yoff = _dot_nt(Cm, h) * E
        dcs = dcs + _sel_dot_nt(hsel, dY * yoff)
        dcs = dcs + jnp.where(lane == CHUNK - 1, dlast, 0.0)
        ki = lax.broadcasted_iota(jnp.int32, (CHUNK, CHUNK), 0)
        si = lax.broadcasted_iota(jnp.int32, (CHUNK, CHUNK), 1)
        lower = (ki >= si).astype(F32)
        da = _dot_sel(dcs, lower)
        ddt = da * A + _sel_dot_nt(hsel, dX * x)
        ddtr = ddt * _sigmoid(dtr_ref[gi] + bias_ref[gi])
        ddtr_ref[gi] = ddtr
        dbias_ref[gi] += ddtr
        dalog_ref[gi] += da * a
        dxbc_ref[:, o:o + GW] = dX * DT + drep_ref[:, cols] * dY
        dd_ref[:, cols] += jnp.sum(dY * x, axis=0, keepdims=True)
        dxbc_ref[:, o + GW:o + GW + N_STATE] = dB
        dxbc_ref[:, o + GW + N_STATE:o + GXBC] = dC
        dh_scr[rows, :] = dS * _head_rows(jnp.exp(cs_last)) + _dot_tn(dYE, Cm)

    return pl.pallas_call(
        body, name="ssd_bwd", grid=(N_GROUPS // GS_BWD, nc),
        in_specs=[xbc_s, dtr_s, row_s, row_s, drep_s, x_s, hs_s],
        out_specs=[xbc_s, dtr_s, row_s, row_s, drep_s],
        out_shape=[jax.ShapeDtypeStruct((T, D_XBC), F32),
                   jax.ShapeDtypeStruct((N_GROUPS, 8, T), F32),
                   jax.ShapeDtypeStruct((N_GROUPS, 8, CHUNK), F32),
                   jax.ShapeDtypeStruct((N_GROUPS, 8, CHUNK), F32),
                   jax.ShapeDtypeStruct((1, D_SSM), F32)],
        scratch_shapes=[pltpu.VMEM((GS_BWD * GW, N_STATE), F32)],
        compiler_params=_cparams(("parallel", "arbitrary")),
    )(xbc, dtr, bias, alog, drep, dy, hs)


def _adamw(w, g, m, v, name, deps=(), emit_g=False):
    R, C = w.shape
    tr = _tile(R, 256, 8)
    nd = len(deps)
    nout = 4 if emit_g else 3

    def body(w_ref, g_ref, m_ref, v_ref, *rest):
        outs = rest[nd:]
        gv = g_ref[...]
        mn = ADAM_B1 * m_ref[...] + (1.0 - ADAM_B1) * gv
        vn = ADAM_B2 * v_ref[...] + (1.0 - ADAM_B2) * (gv * gv)
        m_hat = mn / (1.0 - ADAM_B1 ** ADAM_STEP)
        v_hat = vn / (1.0 - ADAM_B2 ** ADAM_STEP)
        outs[0][...] = -ADAM_LR * (m_hat / (jnp.sqrt(v_hat) + ADAM_EPS) + ADAM_WD * w_ref[...])
        outs[1][...] = mn
        outs[2][...] = vn
        if emit_g:
            outs[3][...] = gv

    spec = pl.BlockSpec((tr, C), lambda i: (i, 0))
    return pl.pallas_call(
        body, name=name, grid=(R // tr,),
        in_specs=[spec] * 4 + [ANY] * nd, out_specs=[spec] * nout,
        out_shape=[jax.ShapeDtypeStruct((R, C), F32)] * nout,
        compiler_params=_cparams(("parallel",)),
    )(w, g, m, v, *deps)


def _adamw_blocks(w, g, m, v, sp, nblk, tr, name, g_local, prev=(), deps=()):
    R, C = w.shape
    nextra = len(prev) + len(deps)

    def body(sp_ref, w_ref, g_ref, m_ref, v_ref, *rest):
        d_ref, mo_ref, vo_ref, go_ref = rest[nextra:]
        gv = g_ref[...]
        mn = ADAM_B1 * m_ref[...] + (1.0 - ADAM_B1) * gv
        vn = ADAM_B2 * v_ref[...] + (1.0 - ADAM_B2) * (gv * gv)
        m_hat = mn / (1.0 - ADAM_B1 ** ADAM_STEP)
        v_hat = vn / (1.0 - ADAM_B2 ** ADAM_STEP)
        d_ref[...] = -ADAM_LR * (m_hat / (jnp.sqrt(v_hat) + ADAM_EPS) + ADAM_WD * w_ref[...])
        mo_ref[...] = mn
        vo_ref[...] = vn
        go_ref[...] = gv

    wspec = pl.BlockSpec((tr, C), lambda i, sp_ref: (sp_ref[0] + i, 0))
    gspec = pl.BlockSpec((tr, C), lambda i, sp_ref: (i, 0)) if g_local else wspec
    grid_spec = pltpu.PrefetchScalarGridSpec(
        num_scalar_prefetch=1, grid=(nblk,),
        in_specs=[wspec, gspec, wspec, wspec] + [ANY] * nextra, out_specs=[wspec] * 4)
    return pl.pallas_call(
        body, name=name, grid_spec=grid_spec,
        out_shape=[jax.ShapeDtypeStruct((R, C), F32)] * 4,
        input_output_aliases={5 + k: k for k in range(len(prev))},
        compiler_params=_cparams(("parallel",)),
    )(sp, w, g, m, v, *prev, *deps)


ANY = pl.BlockSpec(memory_space=pl.ANY)


def _place():
    x, y, c = lax.axis_index("x"), lax.axis_index("y"), lax.axis_index("c")
    return x, y, c


def _other_chips(x, y):
    return [(1 - x, y), (x, 1 - y), (1 - x, 1 - y)]


def _allgather_inplace(bufs, splits, first_done=False):
    n = len(bufs)

    def body(*refs):
        o_refs = refs[n:2 * n]
        send_sems, recv_sems = refs[2 * n:]
        x, y, c = _place()
        xn, yn, dg, sibling = (1 - x, y), (x, 1 - y), (1 - x, 1 - y), (x, y, 1 - c)

        def blk(k, chip, pc):
            return o_refs[k].at[4 * chip[0] + 2 * chip[1] + pc]

        def part(k, ref, p):
            kind, s = splits[k]
            _, R, C = bufs[k].shape
            if kind == "rows":
                return ref.at[pl.ds(0, s)] if p == 0 else ref.at[pl.ds(s, R - s)]
            return ref.at[:, pl.ds(0, s)] if p == 0 else ref.at[:, pl.ds(s, C - s)]

        def copy(k, slot, ref, to):
            return pltpu.make_async_remote_copy(
                src_ref=ref, dst_ref=ref, send_sem=send_sems.at[k, slot], recv_sem=recv_sems.at[k, slot],
                device_id=to, device_id_type=MESH)

        sent = []

        def send(k, slot, ref, to):
            cp = copy(k, slot, ref, to)
            cp.start()
            sent.append(cp)

        if not first_done:
            for k in range(n):
                send(k, 0, blk(k, (x, y), c), (*xn, c))
                send(k, 1, blk(k, (x, y), c), (*yn, c))
        for k in range(n):
            bx, by = blk(k, xn, c), blk(k, yn, c)
            if not first_done:
                copy(k, 0, bx, sibling).wait_recv()
            send(k, 2, part(k, bx, 0), (*yn, c))
            send(k, 4, bx, sibling)
            if not first_done:
                copy(k, 1, by, sibling).wait_recv()
            send(k, 3, part(k, by, 1), (*xn, c))
            send(k, 5, by, sibling)
        for k in range(n):
            d0, d1 = part(k, blk(k, dg, c), 0), part(k, blk(k, dg, c), 1)
            copy(k, 2, d0, sibling).wait_recv()
            send(k, 6, d0, sibling)
            copy(k, 3, d1, sibling).wait_recv()
            send(k, 7, d1, sibling)
        for k in range(n):
            copy(k, 4, blk(k, xn, 1 - c), sibling).wait_recv()
            copy(k, 5, blk(k, yn, 1 - c), sibling).wait_recv()
            copy(k, 6, part(k, blk(k, dg, 1 - c), 0), sibling).wait_recv()
            copy(k, 7, part(k, blk(k, dg, 1 - c), 1), sibling).wait_recv()
        for cp in sent:
            cp.wait_send()

    return pl.pallas_call(
        body, name="allgather_w_in",
        in_specs=[ANY] * n, out_specs=[ANY] * n,
        out_shape=[jax.ShapeDtypeStruct(b.shape, b.dtype) for b in bufs],
        input_output_aliases={k: k for k in range(n)},
        scratch_shapes=[pltpu.SemaphoreType.DMA((n, 8)), pltpu.SemaphoreType.DMA((n, 8))],
    )(*bufs)


HBM = pl.BlockSpec(memory_space=pltpu.HBM)
SEM = pl.BlockSpec(memory_space=pltpu.SEMAPHORE)
EFFECT = pltpu.SideEffectType.DATAFLOW_SIDE_EFFECTING


def _split_start(name, arrays, build, n_copies, after=()):
    na, nd = len(arrays), len(after)

    def body(*refs):
        send_sems, recv_sems = refs[na + nd], refs[na + nd + 1]
        for cp in build(refs[:na], send_sems, recv_sems):
            cp.start()
        refs[-1][...] = jnp.zeros((8, 128), F32)

    outs = pl.pallas_call(
        body, name=name,
        out_shape=(pltpu.SemaphoreType.DMA((n_copies,)), pltpu.SemaphoreType.DMA((n_copies,)),
                   *[pltpu.HBM(a.shape, a.dtype) for a in arrays], jax.ShapeDtypeStruct((8, 128), F32)),
        in_specs=[HBM] * na + [ANY] * nd,
        out_specs=(SEM, SEM, *[HBM] * na, pl.BlockSpec(memory_space=pltpu.VMEM)),
        input_output_aliases={i: 2 + i for i in range(na)},
        compiler_params=pltpu.CompilerParams(has_side_effects=EFFECT),
    )(*[pltpu.with_memory_space_constraint(a, pltpu.HBM) for a in arrays], *after)
    return outs[0], outs[1], list(outs[2:2 + na]), outs[-1]


def _split_wait(name, send_sems, recv_sems, arrays, build, after):
    na = len(arrays)

    def body(*refs):
        for cp in build(refs[:na], refs[na], refs[na + 1]):
            cp.wait_send()
            cp.wait_recv()

    outs = pl.pallas_call(
        body, name=name,
        out_shape=tuple(pltpu.HBM(a.shape, a.dtype) for a in arrays),
        in_specs=[HBM] * na + [SEM, SEM] + [ANY] * len(after),
        out_specs=tuple([HBM] * na),
        input_output_aliases={i: i for i in range(na)},
        compiler_params=pltpu.CompilerParams(has_side_effects=EFFECT),
    )(*arrays, send_sems, recv_sems, *after)
    return list(outs)


def _remote(src, dst, send_sems, recv_sems, i, to):
    return pltpu.make_async_remote_copy(src_ref=src, dst_ref=dst, send_sem=send_sems.at[i], recv_sem=recv_sems.at[i],
                                        device_id=to, device_id_type=MESH)


def _build_ag_first(refs, ss, rs):
    x, y, c = _place()
    cps = []
    for k, ref in enumerate(refs):
        blk = ref.at[4 * x + 2 * y + c]
        cps += [_remote(blk, blk, ss, rs, 2 * k, (1 - x, y, c)), _remote(blk, blk, ss, rs, 2 * k + 1, (x, 1 - y, c))]
    return cps


def _build_ag_ici(refs, ss, rs):
    x, y, c = _place()
    cps = []
    for k, ref in enumerate(refs):
        blk = ref.at[4 * x + 2 * y + c]
        for j, (px, py) in enumerate(_other_chips(x, y)):
            cps.append(_remote(blk, blk, ss, rs, 3 * k + j, (px, py, c)))
    return cps


def _build_ag_fwd(refs, ss, rs):
    x, y, c = _place()
    cps = []
    for k, ref in enumerate(refs):
        for j, (px, py) in enumerate(_other_chips(x, y)):
            blk = ref.at[4 * px + 2 * py + c]
            cps.append(_remote(blk, blk, ss, rs, 3 * k + j, (x, y, 1 - c)))
    return cps


def _build_rs_swap(refs, ss, rs):
    x, y, c = _place()
    n = len(refs) // 2
    return [_remote(refs[k].at[:, pl.ds(1 - c, 1)], refs[n + k], ss, rs, k, (x, y, 1 - c)) for k in range(n)]


def _build_rs_ici(refs, ss, rs):
    x, y, c = _place()
    n = len(refs) // 2
    me = 2 * x + y
    cps = []
    for k in range(n):
        for j, (px, py) in enumerate(_other_chips(x, y)):
            cps.append(_remote(refs[k].at[2 * px + py], refs[n + k].at[me], ss, rs, 3 * k + j, (px, py, c)))
    return cps


def _build_rs_share(refs, ss, rs):
    x, y, c = _place()
    return [_remote(ref.at[c], ref.at[c], ss, rs, k, (x, y, 1 - c)) for k, ref in enumerate(refs)]


def _build_small_gather(refs, ss, rs):
    x, y, c = _place()
    me = 4 * x + 2 * y + c
    cps = []
    for d in range(1, N_DEV):
        to = (1 - x if d & 4 else x, 1 - y if d & 2 else y, 1 - c if d & 1 else c)
        cps.append(_remote(refs[0], refs[1].at[me], ss, rs, d - 1, to))
    return cps


def _sum_gathered(mine, landed, me_arr):
    R, C = mine.shape

    def body(me_ref, m_ref, l_ref, o_ref):
        me = me_ref[0]
        s = None
        for d in range(N_DEV):
            t = jnp.where(me == d, m_ref[...], l_ref[d])
            s = t if s is None else s + t
        o_ref[...] = s

    grid_spec = pltpu.PrefetchScalarGridSpec(
        num_scalar_prefetch=1, grid=(1,),
        in_specs=[pl.BlockSpec((R, C), lambda i, me_ref: (0, 0)),
                  pl.BlockSpec((N_DEV, R, C), lambda i, me_ref: (0, 0, 0))],
        out_specs=pl.BlockSpec((R, C), lambda i, me_ref: (0, 0)))
    return pl.pallas_call(
        body, name="sum_small", grid_spec=grid_spec,
        out_shape=jax.ShapeDtypeStruct((R, C), F32),
        compiler_params=_cparams(("arbitrary",)),
    )(me_arr, mine, landed)


def _rs_add_pair(p, r0, c_arr, name):
    _, _, hr, cols = p.shape
    tr = _tile(hr, 256, 8)

    def body(c_ref, p_ref, r_ref, q_ref):
        q_ref[...] = (p_ref[0].astype(F32) + r_ref[0].astype(F32)).astype(BF16)

    grid_spec = pltpu.PrefetchScalarGridSpec(
        num_scalar_prefetch=1, grid=(N_CHIPS, hr // tr),
        in_specs=[pl.BlockSpec((1, 1, tr, cols), lambda j, i, c_ref: (j, c_ref[0], i, 0)),
                  pl.BlockSpec((1, 1, tr, cols), lambda j, i, c_ref: (j, 0, i, 0))],
        out_specs=pl.BlockSpec((1, tr, cols), lambda j, i, c_ref: (j, i, 0)))
    return pl.pallas_call(
        body, name=name, grid_spec=grid_spec,
        out_shape=jax.ShapeDtypeStruct((N_CHIPS, hr, cols), BF16),
        compiler_params=_cparams(("parallel", "parallel")),
    )(c_arr, p, r0)


def _rs_add_chips(r1, q, place_arr, name, own_copy=False):
    _, hr, cols = r1.shape
    tr = _tile(hr, 256, 8)

    def body(place_ref, r_ref, q_ref, *o_refs):
        chip = place_ref[0]
        s = None
        for j in range(N_CHIPS):
            t = jnp.where(chip == j, q_ref[j], r_ref[j]).astype(F32)
            s = t if s is None else s + t
        for o_ref in o_refs:
            o_ref[...] = s

    blk = pl.BlockSpec((N_CHIPS, tr, cols), lambda i, place_ref: (0, i, 0))
    out_specs = [pl.BlockSpec((None, tr, cols), lambda i, place_ref: (place_ref[1], i, 0))]
    out_shape = [jax.ShapeDtypeStruct((2, hr, cols), F32)]
    if own_copy:
        out_specs.append(pl.BlockSpec((tr, cols), lambda i, place_ref: (i, 0)))
        out_shape.append(jax.ShapeDtypeStruct((hr, cols), F32))
    grid_spec = pltpu.PrefetchScalarGridSpec(
        num_scalar_prefetch=1, grid=(hr // tr,), in_specs=[blk, blk], out_specs=out_specs)
    outs = pl.pallas_call(
        body, name=name, grid_spec=grid_spec, out_shape=out_shape,
        compiler_params=_cparams(("parallel",)),
    )(place_arr, r1, q)
    return outs if own_copy else outs[0]


def _pad_rows(a, rows):
    return jnp.pad(a, ((0, rows - a.shape[0]), (0, 0)))


def _pad_cols(a, cols):
    return jnp.pad(a, ((0, 0), (0, cols - a.shape[1])))


def _heads_to_rows(v):
    v = v.reshape(N_GROUPS, HEADS_PER_GROUP, 1)
    v = jnp.pad(v, ((0, 0), (0, 8 - HEADS_PER_GROUP), (0, 0)))
    return jnp.broadcast_to(v, (N_GROUPS, 8, CHUNK))


def _rows_to_heads(a):
    return jnp.sum(a[:, :HEADS_PER_GROUP, :], axis=-1).reshape(N_HEADS)


HR_IN = 1568


def _kernel_segments():
    segs = [(0, 0, 0, D_SSM)]
    for g in range(N_GROUPS):
        k0 = D_SSM + g * GXBC
        segs += [(0, k0, D_SSM + g * GW, GW), (0, k0 + GW, 2 * D_SSM + g * N_STATE, N_STATE),
                 (0, k0 + GW + N_STATE, 2 * D_SSM + 1024 + g * N_STATE, N_STATE)]
    segs.append((1, 0, D_SSM + D_XBC, N_HEADS))
    for j in range(D_MODEL // SCB):
        for k in range(3):
            segs.append((0, D_SSM + D_XBC + j * SC3 + k * SCB, D_SSM + D_XBC + N_HEADS + k * D_MODEL + j * SCB, SCB))
    return segs


def _shard_row_plan():
    cs = D_IN // N_CHIPS
    plan = []
    for src, s, o, n in _kernel_segments():
        while n > 0:
            chip, loc = divmod(o, cs)
            half, row = divmod(loc, HR_IN)
            m = min(n, cs - loc, HR_IN - row)
            plan.append((src, s, chip, half, row, m))
            s, o, n = s + m, o + m, n - m
    return plan


SCATTER_ROWS = 512
SCATTER_SLOTS = 4


def _scatter_rows_to_shards(k_main, k_dt):
    C = k_main.shape[1]
    pieces = []
    for src, s, chip, half, row, n in _shard_row_plan():
        for o in range(0, n, SCATTER_ROWS):
            pieces.append((src, s + o, chip, half, row + o, min(SCATTER_ROWS, n - o)))
    S, lag, N = SCATTER_SLOTS, SCATTER_SLOTS // 2, len(pieces)

    def body(m_ref, d_ref, o_ref, buf, in_sems, out_sems):
        def cin(i):
            src, s, _, _, _, n = pieces[i]
            return pltpu.make_async_copy((d_ref if src else m_ref).at[pl.ds(s, n)],
                                         buf.at[i % S, pl.ds(0, n)], in_sems.at[i % S])

        def cout(i):
            _, _, chip, half, row, n = pieces[i]
            return pltpu.make_async_copy(buf.at[i % S, pl.ds(0, n)],
                                         o_ref.at[chip, half, pl.ds(row, n)], out_sems.at[i % S])

        for i in range(N + lag):
            if i < N:
                if i >= S:
                    cout(i - S).wait()
                cin(i).start()
            j = i - lag
            if 0 <= j < N:
                cin(j).wait()
                cout(j).start()
        for j in range(max(0, N - S), N):
            cout(j).wait()

    return pl.pallas_call(
        body, name="scatter_dw_in_rows", in_specs=[ANY, ANY], out_specs=ANY,
        out_shape=jax.ShapeDtypeStruct((N_CHIPS, 2, HR_IN, C), k_main.dtype),
        scratch_shapes=[pltpu.VMEM((S, SCATTER_ROWS, C), k_main.dtype),
                        pltpu.SemaphoreType.DMA((S,)), pltpu.SemaphoreType.DMA((S,))],
        compiler_params=_cparams(),
    )(k_main, k_dt)


ROWS_IN = D_IN // N_CHIPS
ROWS_IN_PAD = ROWS_IN + 8


def _cast_w_in_into_gather(wt32, chip_arr):
    R, C = wt32.shape
    hc, cbk = C // 2, 256

    def body(chip_ref, w_ref, o_ref):
        y = jnp.concatenate([w_ref[...], jnp.zeros((ROWS_IN_PAD - R, cbk), F32)], axis=0)
        odd = chip_ref[0] % 2 == 1
        o_ref[...] = jnp.where(odd, pltpu.roll(y, ROWS_IN_PAD - R, axis=0), y).astype(BF16)

    grid_spec = pltpu.PrefetchScalarGridSpec(
        num_scalar_prefetch=1, grid=(2, hc // cbk),
        in_specs=[pl.BlockSpec((R, cbk), lambda h, s, chip_ref: (0, h * (hc // cbk) + s))],
        out_specs=pl.BlockSpec((None, ROWS_IN_PAD, cbk), lambda h, s, chip_ref: (2 * chip_ref[0] + h, 0, s)))
    return pl.pallas_call(
        body, name="cast_w_in", grid_spec=grid_spec,
        out_shape=jax.ShapeDtypeStruct((N_DEV, ROWS_IN_PAD, hc), BF16),
        compiler_params=_cparams(("parallel", "parallel")),
    )(chip_arr, wt32)


def _gather_row_plan():
    segs = [(s, o, n) for src, s, o, n in _kernel_segments() if src == 0]
    plan, merges = [], []
    for k, o, n in segs:
        while n > 0:
            chip, loc = divmod(o, ROWS_IN)
            m = min(n, ROWS_IN - loc)
            ps, kd, cnt = loc + 8 * (chip % 2), k, m
            if ps % 16:
                ps, kd, cnt = ps - 8, kd - 8, cnt + 8
            if (ps + cnt) % 16:
                cnt -= 8
                merges.append((kd + cnt, chip, chip + 1))
            if cnt:
                plan.append((chip, ps, kd, cnt))
            k, o, n = k + m, o + m, n - m
    return plan, merges


def _gather_to_kernel_rows(g):
    hc = g.shape[2]
    plan, merges = _gather_row_plan()
    pieces = []
    for chip, ps, kd, n in plan:
        for o in range(0, n, SCATTER_ROWS):
            pieces.append((chip, ps + o, kd + o, min(SCATTER_ROWS, n - o)))
    S, lag, N = SCATTER_SLOTS, SCATTER_SLOTS // 2, len(pieces)

    def body(g_ref, o_ref, buf, mbuf, in_sems, out_sems, m_sems):
        def cins(i):
            chip, ps, _, n = pieces[i]
            return [pltpu.make_async_copy(g_ref.at[2 * chip + h, pl.ds(ps, n)],
                                          buf.at[i % S, pl.ds(0, n), pl.ds(h * hc, hc)], in_sems.at[i % S, h])
                    for h in range(2)]

        def cout(i):
            _, _, kd, n = pieces[i]
            return pltpu.make_async_copy(buf.at[i % S, pl.ds(0, n)], o_ref.at[pl.ds(kd, n)], out_sems.at[i % S])

        for i in range(N + lag):
            if i < N:
                if i >= S:
                    cout(i - S).wait()
                for cp in cins(i):
                    cp.start()
            j = i - lag
            if 0 <= j < N:
                for cp in cins(j):
                    cp.wait()
                cout(j).start()
        for j in range(max(0, N - S), N):
            cout(j).wait()
        for t, (kd, ce, co) in enumerate(merges):
            loads = []
            for h in range(2):
                loads.append(pltpu.make_async_copy(g_ref.at[2 * ce + h, pl.ds(ROWS_IN - 8, 16)],
                                                   mbuf.at[0, :, pl.ds(h * hc, hc)], m_sems.at[2 * h]))
                loads.append(pltpu.make_async_copy(g_ref.at[2 * co + h, pl.ds(0, 16)],
                                                   mbuf.at[1, :, pl.ds(h * hc, hc)], m_sems.at[2 * h + 1]))
            for cp in loads:
                cp.start()
            for cp in loads:
                cp.wait()
            row = lax.broadcasted_iota(jnp.int32, (16, 2 * hc), 0)
            mbuf[2] = jnp.where(row < 8, mbuf[0].astype(F32), mbuf[1].astype(F32)).astype(g.dtype)
            st = pltpu.make_async_copy(mbuf.at[2], o_ref.at[pl.ds(kd, 16)], m_sems.at[4])
            st.start()
            st.wait()

    return pl.pallas_call(
        body, name="w_in_to_kernel_rows", in_specs=[ANY], out_specs=ANY,
        out_shape=jax.ShapeDtypeStruct((D_MAIN, 2 * hc), g.dtype),
        scratch_shapes=[pltpu.VMEM((S, SCATTER_ROWS, 2 * hc), g.dtype), pltpu.VMEM((3, 16, 2 * hc), g.dtype),
                        pltpu.SemaphoreType.DMA((S, 2)), pltpu.SemaphoreType.DMA((S,)),
                        pltpu.SemaphoreType.DMA((5,))],
        compiler_params=_cparams(),
    )(g)


def _to_kernel_xbc(a):
    R = a.shape[0]
    return jnp.concatenate([a[:, :D_SSM].reshape(R, N_GROUPS, GW), a[:, D_SSM:D_SSM + 1024].reshape(R, N_GROUPS, N_STATE),
                            a[:, D_SSM + 1024:].reshape(R, N_GROUPS, N_STATE)], axis=2).reshape(R, D_XBC)


def _from_kernel_xbc(a):
    R = a.shape[0]
    g = a.reshape(R, N_GROUPS, GXBC)
    return jnp.concatenate([g[:, :, :GW].reshape(R, D_SSM), g[:, :, GW:GW + N_STATE].reshape(R, 1024),
                            g[:, :, GW + N_STATE:].reshape(R, 1024)], axis=1)


def kernel(x, norm_mix_g, w_in, ssm_conv_w, ssm_conv_b, ssm_dt_bias, ssm_A_log, ssm_D, ssm_norm_g, sc_conv_w, w_out, norm_ffn_g, w_gate, w_up, w_down, norm_final_g, loss_target, m_norm_mix_g, m_w_in, m_ssm_conv_w, m_ssm_conv_b, m_ssm_dt_bias, m_ssm_A_log, m_ssm_D, m_ssm_norm_g, m_sc_conv_w, m_w_out, m_norm_ffn_g, m_w_gate, m_w_up, m_w_down, m_norm_final_g, v_norm_mix_g, v_w_in, v_ssm_conv_w, v_ssm_conv_b, v_ssm_dt_bias, v_ssm_A_log, v_ssm_D, v_ssm_norm_g, v_sc_conv_w, v_w_out, v_norm_ffn_g, v_w_gate, v_w_up, v_w_down, v_norm_final_g):
    T = x.shape[1]
    xt = x[0]
    tgt = loss_target[0]
    cx, cy, cc = lax.axis_index("x"), lax.axis_index("y"), lax.axis_index("c")
    chip = 2 * cx + cy
    c_arr = jnp.reshape(cc, (1,)).astype(jnp.int32)
    chip_arr = jnp.reshape(chip, (1,)).astype(jnp.int32)
    place_arr = jnp.stack([chip, cc]).astype(jnp.int32)

    big = [w_in[0].T, w_out[0], w_gate[0], w_up[0], w_down[0]]
    names = ["w_in", "w_out", "w_gate", "w_up", "w_down"]
    gb_in = _cast_w_in_into_gather(big[0], chip_arr)
    cs_in, cs_conv = D_IN // N_CHIPS, D_XBC // N_CHIPS
    cw = jnp.stack([_pad_rows(ssm_conv_w[0], 8), _pad_cols(_pad_rows(sc_conv_w[0], 8), cs_conv)])
    cw_buf = lax.dynamic_update_slice(jnp.zeros((N_DEV, 8, cs_conv), F32), cw, (2 * chip, 0, 0))
    f_ss, f_rs, f_arr, f_tok = _split_start("ag_in_first_start", [gb_in, cw_buf], _build_ag_first, 4)
    gbufs = [None] + [_cast_into_gather(w, chip_arr, "cast_" + nm, deps=[f_tok]) for w, nm in zip(big[1:], names[1:])]
    n1 = _rmsnorm_fwd(xt, _tie(norm_mix_g, f_tok, "tie_ag_first"), "rmsnorm_mix")
    f_arr = _split_wait("ag_in_first_wait", f_ss, f_rs, f_arr, _build_ag_first, after=gbufs[1:] + [n1])
    g_in, cw_all = _allgather_inplace(f_arr, [("rows", (ROWS_IN_PAD // 32) * 16), ("cols", cs_conv // 2)],
                                      first_done=True)
    cw_all = cw_all.reshape(N_CHIPS, 2, 8, cs_conv)
    ssm_w8 = _to_kernel_xbc(cw_all[:, 0].transpose(1, 0, 2).reshape(8, D_XBC))
    sc_w8 = cw_all[:, 1, :, :D_MODEL // N_CHIPS].transpose(1, 0, 2).reshape(8, D_MODEL)
    ssm_bk = _to_kernel_xbc(ssm_conv_b)
    wt_main = _gather_to_kernel_rows(g_in)
    dt_rows = [jnp.concatenate([g_in[2 * ch, r0:r0 + 16], g_in[2 * ch + 1, r0:r0 + 16]], axis=1)
               for ch, r0 in ((1, ROWS_IN_PAD - 16), (2, 0))]
    wt_dt = _pad_rows(jnp.concatenate(dt_rows, axis=0), DT_PAD)
    ag_ss, ag_rs, ag_bufs, ag_tok = _split_start("ag_ici_start", gbufs[1:], _build_ag_ici, 12, after=[g_in, cw_all])

    bias_rows = _heads_to_rows(ssm_dt_bias[0])
    alog_rows = _heads_to_rows(ssm_A_log[0])
    drep = jnp.repeat(ssm_D[0], HEADDIM).reshape(1, D_SSM)

    (proj,) = _matmul([(n1, wt_main)], tb=True, out_dtypes=[F32], name="mm_proj", deps=[ag_tok])
    (dt_raw,) = _matmul([(n1, wt_dt)], tb=True, out_dtypes=[F32], name="mm_proj_dt")
    xbc = _ssm_conv_fwd(proj, ssm_w8, ssm_bk)
    dtr = jnp.pad(dt_raw[:, :N_HEADS].T.reshape(N_GROUPS, HEADS_PER_GROUP, T), ((0, 0), (0, 4), (0, 0)))
    y_ssd, hs = _ssd_fwd(xbc, dtr, bias_rows, alog_rows, drep)
    ag_bufs = _split_wait("ag_ici_wait", ag_ss, ag_rs, ag_bufs, _build_ag_ici, after=[y_ssd])
    fw_ss, fw_rs, fw_bufs, fw_tok = _split_start("ag_fwd_start", ag_bufs, _build_ag_fwd, 12)
    y_mix = _shortconv_fwd(proj, sc_w8, _gated_norm_fwd(y_ssd, proj, _tie(ssm_norm_g, fw_tok, "tie_ag_fwd")))
    gath = _split_wait("ag_fwd_wait", fw_ss, fw_rs, fw_bufs, _build_ag_fwd, after=[y_mix])
    w_out_f = gath[0].reshape(2 * D_MODEL, D_MODEL)
    w_gate3 = gath[1].reshape(N_CHIPS, D_MODEL, D_FF // N_CHIPS)
    w_up3 = gath[2].reshape(N_CHIPS, D_MODEL, D_FF // N_CHIPS)
    w_down_f = gath[3].reshape(D_FF, D_MODEL)
    (h1,) = _matmul([(y_mix, w_out_f)], out_dtypes=[F32], name="mm_out", extras=[xt],
                    epilogue=lambda acc, res: (acc + res,))
    n2 = _rmsnorm_fwd(h1, norm_ffn_g, "rmsnorm_ffn")
    g_act, u_act, a_act = _ffn_fwd(n2, w_gate3, w_up3)
    (h2,) = _matmul([(a_act, w_down_f)], out_dtypes=[F32], name="mm_down", extras=[h1],
                    epilogue=lambda acc, res: (acc + res,))

    dh2, dh2b, dg_final, loss_part = _loss_and_final_bwd(h2, tgt, norm_final_g.reshape(1, D_MODEL))
    dg_act, du_act = _matmul([(dh2b, w_down_f)], tb=True, out_dtypes=[BF16, BF16], name="mm_down_bwd",
                             tn=512, extras=[g_act, u_act], epilogue=_swiglu_bwd, nsub=2)
    (dw_down,) = _matmul([(a_act, dh2b)], ta=True, out_dtypes=[BF16], name="mm_dw_down", tm=1408, tn=512)
    (dn2,) = _matmul([(dg_act, w_gate3), (du_act, w_up3)], tb=True, b3d=True, out_dtypes=[BF16],
                     name="mm_ffn_in_bwd")
    (dw_gate,) = _matmul([(n2, dg_act)], ta=True, out_dtypes=[BF16], name="mm_dw_gate", tm=512, tn=1408,
                         col_shards=True)
    (dw_up,) = _matmul([(n2, du_act)], ta=True, out_dtypes=[BF16], name="mm_dw_up", tm=512, tn=1408,
                       col_shards=True)
    dh1, dh1b, dg_ffn = _rmsnorm_bwd(dn2, h1, norm_ffn_g, dh2, "rmsnorm_ffn_bwd")
    (dw_out,) = _matmul([(y_mix, dh1b)], ta=True, out_dtypes=[BF16], name="mm_dw_out")

    def halves(g):
        return g.reshape(N_CHIPS, 2, g.shape[1] // 2, g.shape[2])

    def landing(shape, dtype):
        return lax.empty(shape, dtype)

    names1 = names[1:]
    ps1 = [halves(dw_out.reshape(N_CHIPS, -1, D_MODEL)), halves(dw_gate), halves(dw_up),
           halves(dw_down.reshape(N_CHIPS, -1, D_MODEL))]
    r0_1 = [landing((N_CHIPS, 1) + p.shape[2:], p.dtype) for p in ps1]
    sw_ss, sw_rs, sw_arr, sw_tok = _split_start("rs1_swap_start", ps1 + r0_1, _build_rs_swap, 4)
    (dmix,) = _matmul([(dh1b, w_out_f)], tb=True, out_dtypes=[BF16], name="mm_out_bwd", deps=[sw_tok])
    dproj, dw_sc = _shortconv_bwd(dmix, proj, sc_w8)
    dy_ssd, dproj, dg_ssmnorm = _gated_norm_bwd(dmix, y_ssd, proj, ssm_norm_g, dproj)
    sw_arr = _split_wait("rs1_swap_wait", sw_ss, sw_rs, sw_arr, _build_rs_swap, after=[dy_ssd])
    qs1 = [_rs_add_pair(p, r, c_arr, "rs_add_pair_" + nm) for p, r, nm in zip(sw_arr[:4], sw_arr[4:], names1)]
    r1_1 = [landing(q.shape, BF16) for q in qs1]
    ic_ss, ic_rs, ic_arr, ic_tok = _split_start("rs1_ici_start", qs1 + r1_1, _build_rs_ici, 12)
    dxbc_act, ddtr, dbias_acc, dalog_acc, dD_acc = _ssd_bwd(
        xbc, dtr, bias_rows, alog_rows, _tie(drep, ic_tok, "tie_rs1_ici"), dy_ssd, hs)
    dproj, dw_ssmconv, db_ssmconv = _ssm_conv_bwd(dxbc_act, proj, ssm_w8, ssm_bk, dproj)
    dw_ssmconv, db_ssmconv = _from_kernel_xbc(dw_ssmconv), _from_kernel_xbc(db_ssmconv)
    ddt_raw = _pad_cols(ddtr[:, :HEADS_PER_GROUP, :].reshape(N_HEADS, T).T, DT_PAD).astype(BF16)
    (dwt_main,) = _matmul([(dproj, n1)], ta=True, out_dtypes=[F32], name="mm_dw_main")
    (dwt_dt,) = _matmul([(ddt_raw, n1)], ta=True, out_dtypes=[F32], name="mm_dw_dt")
    ic_arr = _split_wait("rs1_ici_wait", ic_ss, ic_rs, ic_arr, _build_rs_ici, after=[dwt_main])
    g1 = [_rs_add_chips(r, q, place_arr, "rs_add_chips_" + nm) for q, r, nm in zip(ic_arr[:4], ic_arr[4:], names1)]
    sh_ss, sh_rs, sh_arr, sh_tok = _split_start("rs1_share_start", g1, _build_rs_share, 4)
    p_in = _scatter_rows_to_shards(dwt_main, dwt_dt)
    s2_ss, s2_rs, s2_arr, s2_tok = _split_start(
        "rs2_swap_start", [p_in, landing((N_CHIPS, 1) + p_in.shape[2:], F32)], _build_rs_swap, 1)
    tm_pb = 1024
    mt = T // _tile(T, tm_pb)
    mt_a = max(mt // 4, 1)
    (dn1a,) = _matmul([(dproj, wt_main)], out_dtypes=[F32], name="mm_proj_bwd_a", deps=[s2_tok], tm=tm_pb,
                      m_tiles=(0, mt_a))
    s2_arr = _split_wait("rs2_swap_wait", s2_ss, s2_rs, s2_arr, _build_rs_swap, after=[dn1a])
    q_in = _rs_add_pair(s2_arr[0], s2_arr[1], c_arr, "rs_add_pair_w_in")
    i2_ss, i2_rs, i2_arr, i2_tok = _split_start(
        "rs2_ici_start", [q_in, landing(q_in.shape, BF16)], _build_rs_ici, 3)
    if mt > mt_a:
        (dn1a,) = _matmul([(dproj, wt_main)], out_dtypes=[F32], name="mm_proj_bwd_b", deps=[i2_tok], tm=tm_pb,
                          m_tiles=(mt_a, mt - mt_a), out_buf=dn1a)
    (dn1,) = _matmul([(ddt_raw, wt_dt)], out_dtypes=[BF16], name="mm_proj_dt_bwd", extras=[dn1a],
                     epilogue=lambda acc, res: (acc + res,), deps=[i2_tok])
    dx, _, dg_mix = _rmsnorm_bwd(dn1, xt, norm_mix_g, dh1, "rmsnorm_mix_bwd")
    g1 = _split_wait("rs1_share_wait", sh_ss, sh_rs, sh_arr, _build_rs_share, after=[dx])

    big_m = [m_w_in[0].T, m_w_out[0], m_w_gate[0], m_w_up[0], m_w_down[0]]
    big_v = [v_w_in[0].T, v_w_out[0], v_w_gate[0], v_w_up[0], v_w_down[0]]
    big_grads = [None] + [g.reshape(w.shape) for g, w in zip(g1, big[1:])]
    big_out = {}
    for k in range(1, 5):
        *big_out[names[k]], big_grads[k] = _adamw(big[k], big_grads[k], big_m[k], big_v[k], "adamw_" + names[k],
                                                   deps=[i2_tok], emit_g=True)
    i2_arr = _split_wait("rs2_ici_wait", i2_ss, i2_rs, i2_arr, _build_rs_ici, after=[big_out[names[4]][0], dx])
    g_in_red, g_in_own = _rs_add_chips(i2_arr[1], i2_arr[0], place_arr, "rs_add_chips_w_in", own_copy=True)
    s3_ss, s3_rs, s3_arr, s3_tok = _split_start("rs2_share_start", [g_in_red], _build_rs_share, 1)

    dD = jnp.sum(dD_acc.reshape(N_HEADS, HEADDIM), axis=-1)
    heads_row = jnp.concatenate([_rows_to_heads(dbias_acc), _rows_to_heads(dalog_acc), dD,
                                 loss_part.reshape(1)]).reshape(1, -1)
    small = jnp.concatenate([
        dw_ssmconv,
        _pad_cols(dw_sc, D_XBC),
        db_ssmconv,
        jnp.concatenate([dg_mix, dg_ssmnorm], axis=1),
        jnp.concatenate([dg_ffn, dg_final], axis=1),
        _pad_cols(heads_row, D_XBC),
        jnp.zeros((4, D_XBC), F32),
    ], axis=0)
    sm_ss, sm_rs, sm_arr, sm_tok = _split_start(
        "small_gather_start", [small, landing((N_DEV,) + small.shape, F32)], _build_small_gather, N_DEV - 1,
        after=[s3_tok])
    tr_in = 56
    nb_half = HR_IN // tr_in
    mine = _adamw_blocks(big[0], g_in_own, big_m[0], big_v[0], jnp.stack([cc * nb_half, cc]).astype(jnp.int32),
                         nb_half - 1, tr_in, "adamw_w_in_mine", True, deps=[sm_tok])
    (g_in_full,) = _split_wait("rs2_share_wait", s3_ss, s3_rs, s3_arr, _build_rs_share, after=[mine[0]])
    d_t, m_t, v_t, g_t = _adamw_blocks(
        big[0], g_in_full.reshape(2 * HR_IN, D_MODEL), big_m[0], big_v[0],
        jnp.stack([(1 - cc) * (nb_half - 1), cc]).astype(jnp.int32), nb_half, tr_in, "adamw_w_in_rest", False,
        prev=mine)
    big_grads[0] = g_t.T
    big_out[names[0]] = (d_t.T, m_t.T, v_t.T)
    sm_arr = _split_wait("small_gather_wait", sm_ss, sm_rs, sm_arr, _build_small_gather, after=[d_t])
    tot = _sum_gathered(sm_arr[0], sm_arr[1], jnp.reshape(4 * cx + 2 * cy + cc, (1,)).astype(jnp.int32))
    loss = tot[19, 3 * N_HEADS]

    cs_ssm, cs_sc = D_XBC // N_CHIPS, D_MODEL // N_CHIPS
    g_ssm_conv = lax.dynamic_slice(tot[0:K_SSM], (0, chip * cs_ssm), (K_SSM, cs_ssm))
    g_sc_conv = lax.dynamic_slice(tot[8:8 + K_SC, :D_MODEL], (0, chip * cs_sc), (K_SC, cs_sc))
    small_grads = {
        "norm_mix_g": tot[17:18, :D_MODEL], "ssm_conv_w": g_ssm_conv, "ssm_conv_b": tot[16:17],
        "ssm_dt_bias": tot[19:20, 0:N_HEADS], "ssm_A_log": tot[19:20, N_HEADS:2 * N_HEADS],
        "ssm_D": tot[19:20, 2 * N_HEADS:3 * N_HEADS], "ssm_norm_g": tot[17:18, D_MODEL:],
        "sc_conv_w": g_sc_conv, "norm_ffn_g": tot[18:19, :D_MODEL], "norm_final_g": tot[18:19, D_MODEL:],
    }
    small_w = {"norm_mix_g": (norm_mix_g, m_norm_mix_g, v_norm_mix_g),
               "ssm_conv_w": (ssm_conv_w[0], m_ssm_conv_w[0], v_ssm_conv_w[0]),
               "ssm_conv_b": (ssm_conv_b, m_ssm_conv_b, v_ssm_conv_b),
               "ssm_dt_bias": (ssm_dt_bias, m_ssm_dt_bias, v_ssm_dt_bias),
               "ssm_A_log": (ssm_A_log, m_ssm_A_log, v_ssm_A_log),
               "ssm_D": (ssm_D, m_ssm_D, v_ssm_D),
               "ssm_norm_g": (ssm_norm_g, m_ssm_norm_g, v_ssm_norm_g),
               "sc_conv_w": (sc_conv_w[0], m_sc_conv_w[0], v_sc_conv_w[0]),
               "norm_ffn_g": (norm_ffn_g, m_norm_ffn_g, v_norm_ffn_g),
               "norm_final_g": (norm_final_g.reshape(1, -1), m_norm_final_g.reshape(1, -1),
                                v_norm_final_g.reshape(1, -1))}
    PW = 1024
    order = list(small_w)

    def pack(arrs):
        rows = []
        for a in arrs:
            flat = a.reshape(-1)
            n = -(-flat.shape[0] // PW) * PW
            rows.append(jnp.pad(flat, (0, n - flat.shape[0])).reshape(-1, PW))
        slab = jnp.concatenate(rows, axis=0)
        return _pad_rows(slab, -(-slab.shape[0] // 8) * 8)

    wp = pack([small_w[k][0] for k in order])
    mp = pack([small_w[k][1] for k in order])
    vp = pack([small_w[k][2] for k in order])
    gp = pack([small_grads[k] for k in order])
    sd, sm, sv = _adamw(wp, gp, mp, vp, "adamw_small")

    def unpack(slab):
        out, row = {}, 0
        for k in order:
            shape = small_w[k][0].shape
            size = 1
            for s in shape:
                size *= s
            nr = -(-size // PW)
            out[k] = slab[row:row + nr].reshape(-1)[:size].reshape(shape)
            row += nr
        return out

    s_delta, s_m, s_v = unpack(sd), unpack(sm), unpack(sv)

    big_g = dict(zip(names, big_grads))

    weight_order = ["norm_mix_g", "w_in", "ssm_conv_w", "ssm_conv_b", "ssm_dt_bias", "ssm_A_log", "ssm_D",
                    "ssm_norm_g", "sc_conv_w", "w_out", "norm_ffn_g", "w_gate", "w_up", "w_down", "norm_final_g"]
    lead = {"ssm_conv_w", "sc_conv_w", "w_in", "w_out", "w_gate", "w_up", "w_down"}

    def shaped(nm, a):
        if nm == "norm_final_g":
            return a.reshape(D_MODEL)
        return a[None] if nm in lead else a

    grads, deltas, new_m, new_v = [], [], [], []
    for nm in weight_order:
        if nm in big_out:
            g, (d, m, v) = big_g[nm], big_out[nm]
        else:
            g, d, m, v = small_grads[nm], s_delta[nm], s_m[nm], s_v[nm]
        grads.append(shaped(nm, g))
        deltas.append(shaped(nm, d))
        new_m.append(shaped(nm, m))
        new_v.append(shaped(nm, v))
    return (loss, dx[None], *grads, *deltas, *new_m, *new_v)


def _swiglu_bwd(da, dg_factor, du_factor):
    return da * dg_factor.astype(F32), da * du_factor.astype(F32)


def _ffn_fwd(n2, w_gate, w_up):
    T, K = n2.shape
    tn = w_gate.shape[2]
    N = N_CHIPS * tn
    tm = _tile(T, 512)
    sub = _tile(tm, 256)

    def body(a_ref, wg_ref, wu_ref, g_ref, u_ref, act_ref):
        for s in range(tm // sub):
            rows = pl.ds(s * sub, sub)
            a = a_ref[rows, :]
            g = jnp.dot(a, wg_ref[...], preferred_element_type=F32)
            u = jnp.dot(a, wu_ref[...], preferred_element_type=F32)
            sig = _sigmoid(g)
            sg = g * sig
            g_ref[rows, :] = (u * (sig * (1.0 + g - sg))).astype(BF16)
            u_ref[rows, :] = sg.astype(BF16)
            act_ref[rows, :] = (sg * u).astype(BF16)

    a_spec = pl.BlockSpec((tm, K), lambda j, i: (i, 0))
    b_spec = pl.BlockSpec((None, K, tn), lambda j, i: (j, 0, 0))
    o_spec = pl.BlockSpec((tm, tn), lambda j, i: (i, j))
    return pl.pallas_call(
        body, name="ffn_fwd", grid=(N // tn, T // tm),
        in_specs=[a_spec, b_spec, b_spec], out_specs=[o_spec] * 3,
        out_shape=[jax.ShapeDtypeStruct((T, N), BF16)] * 3,
        compiler_params=_cparams(("parallel", "parallel")),
    )(n2, w_gate, w_up)
```

```python
import functools

import jax
import jax.numpy as jnp
from jax import lax
from jax.experimental import pallas as pl
from jax.experimental.pallas import tpu as pltpu

F32 = jnp.float32
BF16 = jnp.bfloat16
MESH = pl.DeviceIdType.MESH

D_MODEL = 2048
D_SSM = 2048
HEADDIM = 64
N_HEADS = 32
N_GROUPS = 8
HEADS_PER_GROUP = 4
N_STATE = 128
CHUNK = 128
K_SSM = 4
K_SC = 3
D_XBC = 4096
D_FF = 5632
D_IN = 12320
D_MAIN = 12288
OFF_XBC, OFF_CB, OFF_CC, OFF_CX = 2048, 6144, 8192, 10240
DT_PAD = 128
EPS = 1e-5
N_CHIPS = 4
N_DEV = 8

ADAM_LR = 0.001
ADAM_B1 = 0.9
ADAM_B2 = 0.999
ADAM_EPS = 1e-08
ADAM_WD = 0.01
ADAM_STEP = 10

V7X_VMEM_BYTES = 64 * 1024 * 1024
VMEM_LIMIT = V7X_VMEM_BYTES - 8 * 1024 * 1024


def _cparams(sem=None):
    if sem is None:
        return pltpu.CompilerParams(vmem_limit_bytes=VMEM_LIMIT)
    return pltpu.CompilerParams(dimension_semantics=sem, vmem_limit_bytes=VMEM_LIMIT)


def _tile(dim, pref, unit=128):
    best = None
    t = unit
    while t <= min(dim, pref):
        if dim % t == 0:
            best = t
        t += unit
    return best if best is not None else dim


def _sigmoid(x):
    return 1.0 / (1.0 + jnp.exp(-x))


def _silu(x):
    return x * _sigmoid(x)


def _dsilu(x):
    s = _sigmoid(x)
    return s * (1.0 + x * (1.0 - s))


def _softplus(x):
    return jnp.maximum(x, 0.0) + jnp.log(1.0 + jnp.exp(-jnp.abs(x)))


MATMUL_VMEM_BUDGET = 44 * 1024 * 1024


def _matmul(pairs, *, ta=False, tb=False, out_dtypes, name, tm=1024, tn=1024, tk=None, extras=(), epilogue=None,
            deps=(), col_shards=False, nsub=1, b3d=False, m_tiles=None, out_buf=None):
    a0, b0 = pairs[0]
    M, K = (a0.shape[1], a0.shape[0]) if ta else a0.shape
    if b3d:
        N = b0.shape[1] if tb else b0.shape[0] * b0.shape[2]
        tk, tn = (b0.shape[2], tn) if tb else (tk, b0.shape[2])
    else:
        N = b0.shape[0] if tb else b0.shape[1]
    tm, tn = _tile(M, tm, 8 if M % 128 else 128), _tile(N, tn)
    npair, nex, ndep, nout = len(pairs), len(extras), len(deps), len(out_dtypes)
    if tk is None:
        fixed = 2 * tm * tn * (sum(jnp.dtype(d).itemsize for d in out_dtypes) + sum(e.dtype.itemsize for e in extras))
        tk = K
        while tk > 128 and (K % tk or tk % 128 or
                            fixed + 2 * npair * 2 * tk * (tm + tn) + (tm * tn * 4 if tk < K else 0) > MATMUL_VMEM_BUDGET):
            tk -= 128
    else:
        tk = _tile(K, tk)
    nk = K // tk
    if nk > 1 or tm % nsub or (tm // nsub) % 128:
        nsub = 1
    sub = tm // nsub
    dims = (((0 if ta else 1,), (1 if tb else 0,)), ((), ()))
    i0, mi = m_tiles if m_tiles is not None else (0, M // tm)
    nbuf = 0 if out_buf is None else 1

    def body(*refs):
        a_refs = refs[0:2 * npair:2]
        b_refs = refs[1:2 * npair:2]
        ex_refs = refs[2 * npair:2 * npair + nex]
        o_refs = refs[2 * npair + nex + ndep + nbuf:2 * npair + nex + ndep + nbuf + nout]

        def dots(rows):
            s = None
            for a_ref, b_ref in zip(a_refs, b_refs):
                a = a_ref[...] if rows is None else (a_ref[:, rows] if ta else a_ref[rows, :])
                d = lax.dot_general(a, b_ref[...], dims, preferred_element_type=F32)
                s = d if s is None else s + d
            return s

        def finish(r, rows):
            ex = [e[...] if rows is None else e[rows, :] for e in ex_refs]
            outs = (r,) if epilogue is None else epilogue(r, *ex)
            for o_ref, o in zip(o_refs, outs):
                if rows is None:
                    o_ref[...] = o.astype(o_ref.dtype)
                else:
                    o_ref[rows, :] = o.astype(o_ref.dtype)

        if nk == 1:
            for s in range(nsub):
                rows = None if nsub == 1 else pl.ds(s * sub, sub)
                finish(dots(rows), rows)
            return

        acc = refs[-1]
        k = pl.program_id(2)

        @pl.when(k == 0)
        def _():
            acc[...] = dots(None)

        @pl.when(jnp.logical_and(k > 0, k < nk - 1))
        def _():
            acc[...] += dots(None)

        @pl.when(k == nk - 1)
        def _():
            finish(acc[...] + dots(None), None)

    a_spec = (pl.BlockSpec((tk, tm), lambda i, j, k: (k, i + i0)) if ta
              else pl.BlockSpec((tm, tk), lambda i, j, k: (i + i0, k)))
    if b3d:
        b_spec = (pl.BlockSpec((None, tn, tk), lambda i, j, k: (k, j, 0)) if tb
                  else pl.BlockSpec((None, tk, tn), lambda i, j, k: (j, k, 0)))
    else:
        b_spec = (pl.BlockSpec((tn, tk), lambda i, j, k: (j, k)) if tb
                  else pl.BlockSpec((tk, tn), lambda i, j, k: (k, j)))
    e_spec = pl.BlockSpec((tm, tn), lambda i, j, k: (i + i0, j))
    if col_shards:
        o_spec = pl.BlockSpec((None, tm, tn), lambda i, j, k: (j, i + i0, 0))
        o_shape = (N // tn, M, tn)
    else:
        o_spec, o_shape = e_spec, (M, N)
    args, in_specs = [], []
    for a, b in pairs:
        args += [a, b]
        in_specs += [a_spec, b_spec]
    args += list(extras) + list(deps) + ([] if out_buf is None else [out_buf])
    in_specs += [e_spec] * nex + [ANY] * (ndep + nbuf)
    outs = pl.pallas_call(
        body,
        name=name,
        grid=(mi, N // tn, nk),
        in_specs=in_specs,
        out_specs=[o_spec] * nout,
        out_shape=[jax.ShapeDtypeStruct(o_shape, dt) for dt in out_dtypes],
        input_output_aliases={} if out_buf is None else {len(args) - 1: 0},
        scratch_shapes=[pltpu.VMEM((tm, tn), F32)] if nk > 1 else [],
        compiler_params=_cparams(("parallel", "parallel", "arbitrary")),
    )(*args)
    return outs


def _cast_into_gather(w, chip_arr, name, split_cols=False, deps=()):
    R, C = w.shape
    hr, hc = (R, C // 2) if split_cols else (R // 2, C)
    tr = _tile(hr, 512, 8)
    nb = hr // tr

    def body(chip_ref, w_ref, *rest):
        rest[-1][...] = w_ref[...].astype(BF16)

    in_map = (lambda h, i, chip_ref: (i, h)) if split_cols else (lambda h, i, chip_ref: (h * nb + i, 0))
    grid_spec = pltpu.PrefetchScalarGridSpec(
        num_scalar_prefetch=1, grid=(2, nb),
        in_specs=[pl.BlockSpec((tr, hc), in_map)] + [ANY] * len(deps),
        out_specs=pl.BlockSpec((None, tr, hc), lambda h, i, chip_ref: (2 * chip_ref[0] + h, i, 0)))
    return pl.pallas_call(
        body, name=name, grid_spec=grid_spec,
        out_shape=jax.ShapeDtypeStruct((N_DEV, hr, hc), BF16),
        compiler_params=_cparams(("parallel", "parallel")),
    )(chip_arr, w, *deps)


def _tie(small, token, name):
    def body(s_ref, t_ref, o_ref):
        o_ref[...] = s_ref[...]

    vm = pl.BlockSpec(memory_space=pltpu.VMEM)
    return pl.pallas_call(body, name=name, in_specs=[vm, ANY], out_specs=vm,
                          out_shape=jax.ShapeDtypeStruct(small.shape, small.dtype))(small, token)


def _rmsnorm_fwd(x, g, name):
    T, D = x.shape
    tt = _tile(T, 256)

    def body(x_ref, g_ref, n_ref):
        xv = x_ref[...]
        r = lax.rsqrt(jnp.mean(xv * xv, axis=-1, keepdims=True) + EPS)
        n_ref[...] = (xv * r * g_ref[...]).astype(BF16)

    return pl.pallas_call(
        body, name=name, grid=(T // tt,),
        in_specs=[pl.BlockSpec((tt, D), lambda i: (i, 0)), pl.BlockSpec((1, D), lambda i: (0, 0))],
        out_specs=pl.BlockSpec((tt, D), lambda i: (i, 0)),
        out_shape=jax.ShapeDtypeStruct((T, D), BF16),
        compiler_params=_cparams(("parallel",)),
    )(x, g)


def _rmsnorm_bwd(dn, x, g, res, name):
    T, D = x.shape
    tt = _tile(T, 256)

    def body(dn_ref, x_ref, g_ref, res_ref, dx_ref, dxb_ref, dg_ref):
        @pl.when(pl.program_id(0) == 0)
        def _():
            dg_ref[...] = jnp.zeros_like(dg_ref)

        xv = x_ref[...]
        dy = dn_ref[...].astype(F32)
        r = lax.rsqrt(jnp.mean(xv * xv, axis=-1, keepdims=True) + EPS)
        xhat = xv * r
        dxh = dy * g_ref[...]
        dx = res_ref[...] + r * (dxh - xhat * jnp.mean(dxh * xhat, axis=-1, keepdims=True))
        dx_ref[...] = dx
        dxb_ref[...] = dx.astype(BF16)
        dg_ref[...] += jnp.sum(dy * xhat, axis=0, keepdims=True)

    tok = pl.BlockSpec((tt, D), lambda i: (i, 0))
    vec = pl.BlockSpec((1, D), lambda i: (0, 0))
    return pl.pallas_call(
        body, name=name, grid=(T // tt,),
        in_specs=[tok, tok, vec, tok],
        out_specs=[tok, tok, vec],
        out_shape=[jax.ShapeDtypeStruct((T, D), F32), jax.ShapeDtypeStruct((T, D), BF16),
                   jax.ShapeDtypeStruct((1, D), F32)],
        compiler_params=_cparams(("arbitrary",)),
    )(dn, x, g, res)


def _loss_and_final_bwd(h2, target, gf):
    T, D = h2.shape
    tt = _tile(T, 256)

    def body(h_ref, t_ref, g_ref, dh_ref, dhb_ref, dg_ref, loss_ref):
        @pl.when(pl.program_id(0) == 0)
        def _():
            dg_ref[...] = jnp.zeros_like(dg_ref)
            loss_ref[...] = jnp.zeros_like(loss_ref)

        xv = h_ref[...]
        r = lax.rsqrt(jnp.mean(xv * xv, axis=-1, keepdims=True) + EPS)
        xhat = xv * r
        err = xhat * g_ref[...] - t_ref[...]
        loss_ref[...] += 0.5 * jnp.sum(jnp.mean(err * err, axis=-1, keepdims=True), axis=0, keepdims=True)
        dy = err * (1.0 / D)
        dxh = dy * g_ref[...]
        dx = r * (dxh - xhat * jnp.mean(dxh * xhat, axis=-1, keepdims=True))
        dh_ref[...] = dx
        dhb_ref[...] = dx.astype(BF16)
        dg_ref[...] += jnp.sum(dy * xhat, axis=0, keepdims=True)

    tok = pl.BlockSpec((tt, D), lambda i: (i, 0))
    vec = pl.BlockSpec((1, D), lambda i: (0, 0))
    return pl.pallas_call(
        body, name="loss_final_bwd", grid=(T // tt,),
        in_specs=[tok, tok, vec],
        out_specs=[tok, tok, vec, pl.BlockSpec((1, 1), lambda i: (0, 0))],
        out_shape=[jax.ShapeDtypeStruct((T, D), F32), jax.ShapeDtypeStruct((T, D), BF16),
                   jax.ShapeDtypeStruct((1, D), F32), jax.ShapeDtypeStruct((1, 1), F32)],
        compiler_params=_cparams(("arbitrary",)),
    )(h2, target, gf)


def _gated_norm_fwd(y, proj, g):
    T, D = y.shape
    tt = _tile(T, 256)

    def body(y_ref, z_ref, g_ref, o_ref):
        yg = y_ref[...] * _silu(z_ref[...])
        r = lax.rsqrt(jnp.mean(yg * yg, axis=-1, keepdims=True) + EPS)
        o_ref[...] = (yg * r * g_ref[...]).astype(BF16)

    tok = pl.BlockSpec((tt, D), lambda i: (i, 0))
    return pl.pallas_call(
        body, name="gated_norm_fwd", grid=(T // tt,),
        in_specs=[tok, tok, pl.BlockSpec((1, D), lambda i: (0, 0))],
        out_specs=tok,
        out_shape=jax.ShapeDtypeStruct((T, 2 * D_MODEL), BF16),
        compiler_params=_cparams(("parallel",)),
    )(y, proj, g)


def _gated_norm_bwd(dmix, y, proj, g, dproj):
    T, D = y.shape
    tt = _tile(T, 256)

    def body(do_ref, y_ref, z_ref, g_ref, dp_ref, dy_ref, dz_ref, dg_ref):
        @pl.when(pl.program_id(0) == 0)
        def _():
            dg_ref[...] = jnp.zeros_like(dg_ref)

        yv, zv = y_ref[...], z_ref[...]
        do = do_ref[...].astype(F32)
        sz = _silu(zv)
        yg = yv * sz
        r = lax.rsqrt(jnp.mean(yg * yg, axis=-1, keepdims=True) + EPS)
        xhat = yg * r
        dxh = do * g_ref[...]
        dyg = r * (dxh - xhat * jnp.mean(dxh * xhat, axis=-1, keepdims=True))
        dy_ref[...] = dyg * sz
        dz_ref[...] = (dyg * yv * _dsilu(zv)).astype(BF16)
        dg_ref[...] += jnp.sum(do * xhat, axis=0, keepdims=True)

    tok = pl.BlockSpec((tt, D), lambda i: (i, 0))
    vec = pl.BlockSpec((1, D), lambda i: (0, 0))
    return pl.pallas_call(
        body, name="gated_norm_bwd", grid=(T // tt,),
        in_specs=[tok, tok, tok, vec, ANY],
        out_specs=[tok, tok, vec],
        out_shape=[jax.ShapeDtypeStruct((T, D), F32), jax.ShapeDtypeStruct(dproj.shape, BF16),
                   jax.ShapeDtypeStruct((1, D), F32)],
        input_output_aliases={4: 1},
        compiler_params=_cparams(("arbitrary",)),
    )(dmix, y, proj, g, dproj)


HALO = 8


def _shift_down(cur, prev8, s):
    ext = jnp.concatenate([prev8, cur], axis=0)
    return pltpu.roll(ext, s, axis=0)[HALO:]


def _shift_up(cur, next8, s):
    n = cur.shape[0]
    ext = jnp.concatenate([cur, next8], axis=0)
    return pltpu.roll(ext, n + HALO - s, axis=0)[:n]


def _conv_specs(tt, cb, col_off_blocks, nt):
    hb = tt // HALO
    cur = pl.BlockSpec((tt, cb), lambda j, i: (i, col_off_blocks + j))
    prev = pl.BlockSpec((HALO, cb), lambda j, i: (jnp.maximum(i * hb - 1, 0), col_off_blocks + j))
    nxt = pl.BlockSpec((HALO, cb), lambda j, i: (jnp.minimum((i + 1) * hb, nt * hb - 1), col_off_blocks + j))
    return cur, prev, nxt


def _taps(cur, prev8, K):
    return [_shift_down(cur, prev8, K - 1 - k) for k in range(K - 1)] + [cur]


def _conv_of_taps(taps, w):
    y = taps[-1] * w[len(taps) - 1:len(taps), :]
    for k, t in enumerate(taps[:-1]):
        y = y + t * w[k:k + 1, :]
    return y


def _causal_conv(cur, prev8, w, K):
    return _conv_of_taps(_taps(cur, prev8, K), w)


def _anticausal_conv(cur, next8, w, K):
    y = cur * w[K - 1:K, :]
    for k in range(K - 1):
        y = y + _shift_up(cur, next8, K - 1 - k) * w[k:k + 1, :]
    return y


def _ssm_conv_fwd(proj, w8, b):
    T = proj.shape[0]
    tt, cb = _tile(T, 1024), 512
    nt = T // tt
    cur, prev, _ = _conv_specs(tt, cb, OFF_XBC // cb, nt)

    def body(u_ref, up_ref, w_ref, b_ref, o_ref):
        first = pl.program_id(1) == 0
        p8 = jnp.where(first, 0.0, up_ref[...])
        pre = _causal_conv(u_ref[...], p8, w_ref[...], K_SSM) + b_ref[...]
        o_ref[...] = _silu(pre)

    return pl.pallas_call(
        body, name="ssm_conv_fwd", grid=(D_XBC // cb, nt),
        in_specs=[cur, prev, pl.BlockSpec((8, cb), lambda j, i: (0, j)), pl.BlockSpec((1, cb), lambda j, i: (0, j))],
        out_specs=pl.BlockSpec((tt, cb), lambda j, i: (i, j)),
        out_shape=jax.ShapeDtypeStruct((T, D_XBC), F32),
        compiler_params=_cparams(("parallel", "parallel")),
    )(proj, proj, w8, b)


def _ssm_conv_bwd(dact, proj, w8, b, dproj):
    T = proj.shape[0]
    tt, cb = _tile(T, 512), 512
    nt = T // tt
    cur, prev, nxt = _conv_specs(tt, cb, OFF_XBC // cb, nt)
    dcur, dprev, dnxt = _conv_specs(tt, cb, 0, nt)

    def dpre_of(d, u, p8, w, bb):
        pre = _causal_conv(u, p8, w, K_SSM) + bb
        return d * _dsilu(pre)

    def body(d_ref, dn_ref, u_ref, up_ref, un_ref, w_ref, b_ref, dp_ref, dx_ref, dw_ref, db_ref):
        i = pl.program_id(1)

        @pl.when(i == 0)
        def _():
            dw_ref[...] = jnp.zeros_like(dw_ref)
            db_ref[...] = jnp.zeros_like(db_ref)

        w, bb = w_ref[...], b_ref[...]
        u = u_ref[...]
        p8 = jnp.where(i == 0, 0.0, up_ref[...])
        taps = _taps(u, p8, K_SSM)
        dpre = d_ref[...] * _dsilu(_conv_of_taps(taps, w) + bb)
        un = un_ref[...]
        dpre_n = dpre_of(dn_ref[...], un, u[tt - HALO:, :], w, bb)
        dpre_n = jnp.where(i == nt - 1, 0.0, dpre_n)
        dx_ref[...] = _anticausal_conv(dpre, dpre_n, w, K_SSM).astype(BF16)
        rows = [jnp.sum(dpre * t, axis=0, keepdims=True) for t in taps]
        rows.append(jnp.zeros((8 - K_SSM, cb), F32))
        dw_ref[...] += jnp.concatenate(rows, axis=0)
        db_ref[...] += jnp.sum(dpre, axis=0, keepdims=True)

    wspec = pl.BlockSpec((8, cb), lambda j, i: (0, j))
    bspec = pl.BlockSpec((1, cb), lambda j, i: (0, j))
    return pl.pallas_call(
        body, name="ssm_conv_bwd", grid=(D_XBC // cb, nt),
        in_specs=[dcur, dnxt, cur, prev, nxt, wspec, bspec, ANY],
        out_specs=[pl.BlockSpec((tt, cb), lambda j, i: (i, OFF_XBC // cb + j)), wspec, bspec],
        out_shape=[jax.ShapeDtypeStruct(dproj.shape, BF16), jax.ShapeDtypeStruct((8, D_XBC), F32),
                   jax.ShapeDtypeStruct((1, D_XBC), F32)],
        input_output_aliases={7: 0},
        compiler_params=_cparams(("parallel", "arbitrary")),
    )(dact, dact, proj, proj, proj, w8, b, dproj)


SCB = 512
SC3 = 3 * SCB


def _sc_specs(tt, nt):
    hb = tt // HALO
    cur = pl.BlockSpec((tt, SC3), lambda j, i: (i, OFF_CB // SC3 + j))
    prev = pl.BlockSpec((HALO, SC3), lambda j, i: (jnp.maximum(i * hb - 1, 0), OFF_CB // SC3 + j))
    nxt = pl.BlockSpec((HALO, SC3), lambda j, i: (jnp.minimum((i + 1) * hb, nt * hb - 1), OFF_CB // SC3 + j))
    return cur, prev, nxt


def _shortconv_fwd(proj, w8, ymix):
    T = proj.shape[0]
    tt = _tile(T, 1024)
    nt = T // tt
    cur, prev, _ = _sc_specs(tt, nt)

    def body(p_ref, pp_ref, w_ref, y_ref, o_ref):
        p, pp = p_ref[...], pp_ref[...]
        v = p[:, SCB:2 * SCB] * p[:, 2 * SCB:]
        vp = jnp.where(pl.program_id(1) == 0, 0.0, pp[:, SCB:2 * SCB] * pp[:, 2 * SCB:])
        o_ref[...] = (p[:, :SCB] * _causal_conv(v, vp, w_ref[...], K_SC)).astype(BF16)

    return pl.pallas_call(
        body, name="shortconv_fwd", grid=(D_MODEL // SCB, nt),
        in_specs=[cur, prev, pl.BlockSpec((8, SCB), lambda j, i: (0, j)), ANY],
        out_specs=pl.BlockSpec((tt, SCB), lambda j, i: (i, D_SSM // SCB + j)),
        out_shape=jax.ShapeDtypeStruct(ymix.shape, BF16),
        input_output_aliases={3: 0},
        compiler_params=_cparams(("parallel", "parallel")),
    )(proj, proj, w8, ymix)


def _shortconv_bwd(dmix, proj, w8):
    T = proj.shape[0]
    tt = _tile(T, 512)
    nt = T // tt
    hb = tt // HALO
    cur, prev, nxt = _sc_specs(tt, nt)
    d_s = pl.BlockSpec((tt, SCB), lambda j, i: (i, D_SSM // SCB + j))
    dn_s = pl.BlockSpec((HALO, SCB), lambda j, i: (jnp.minimum((i + 1) * hb, nt * hb - 1), D_SSM // SCB + j))

    def body(d_ref, dn_ref, p_ref, pp_ref, pn_ref, w_ref, dp_ref, dw_ref):
        i = pl.program_id(1)

        @pl.when(i == 0)
        def _():
            dw_ref[...] = jnp.zeros_like(dw_ref)

        w = w_ref[...]
        p, pp = p_ref[...], pp_ref[...]
        gb, gc, u = p[:, :SCB], p[:, SCB:2 * SCB], p[:, 2 * SCB:]
        v = gc * u
        vp = jnp.where(i == 0, 0.0, pp[:, SCB:2 * SCB] * pp[:, 2 * SCB:])
        d = d_ref[...].astype(F32)
        taps = _taps(v, vp, K_SC)
        dp_ref[:, :SCB] = (d * _conv_of_taps(taps, w)).astype(BF16)
        dcv = d * gb
        dcv_n = jnp.where(i == nt - 1, 0.0, dn_ref[...].astype(F32) * pn_ref[:, :SCB])
        dv = _anticausal_conv(dcv, dcv_n, w, K_SC)
        dp_ref[:, SCB:2 * SCB] = (dv * u).astype(BF16)
        dp_ref[:, 2 * SCB:] = (dv * gc).astype(BF16)
        rows = [jnp.sum(dcv * t, axis=0, keepdims=True) for t in taps]
        rows.append(jnp.zeros((8 - K_SC, SCB), F32))
        dw_ref[...] += jnp.concatenate(rows, axis=0)

    wspec = pl.BlockSpec((8, SCB), lambda j, i: (0, j))
    return pl.pallas_call(
        body, name="shortconv_bwd", grid=(D_MODEL // SCB, nt),
        in_specs=[d_s, dn_s, cur, prev, nxt, wspec],
        out_specs=[cur, wspec],
        out_shape=[jax.ShapeDtypeStruct((T, D_MAIN), BF16), jax.ShapeDtypeStruct((8, D_MODEL), F32)],
        compiler_params=_cparams(("parallel", "arbitrary")),
    )(dmix, dmix, proj, proj, proj, w8)


GW = HEADS_PER_GROUP * HEADDIM


def _dot(a, b):
    return jnp.dot(a.astype(BF16), b.astype(BF16), preferred_element_type=F32)


def _dot_nt(a, b):
    return lax.dot_general(a.astype(BF16), b.astype(BF16), (((1,), (1,)), ((), ())), preferred_element_type=F32)


def _dot_tn(a, b):
    return lax.dot_general(a.astype(BF16), b.astype(BF16), (((0,), (0,)), ((), ())), preferred_element_type=F32)


def _bf16_terms(x, n):
    terms, r = [], x
    for _ in range(n):
        t = r.astype(BF16)
        terms.append(t)
        r = r - t.astype(F32)
    return terms


def _dot_sel(a, sel, n=2):
    s = sel.astype(BF16)
    return sum(jnp.dot(t, s, preferred_element_type=F32) for t in _bf16_terms(a, n))


def _sel_dot(sel, b, n=2):
    s = sel.astype(BF16)
    return sum(jnp.dot(s, t, preferred_element_type=F32) for t in _bf16_terms(b, n))


def _sel_dot_nt(sel, b, n=2):
    s = sel.astype(BF16)
    return sum(lax.dot_general(s, t, (((1,), (1,)), ((), ())), preferred_element_type=F32)
               for t in _bf16_terms(b, n))


def _head_cols(rows):
    parts = [jnp.broadcast_to(rows[r:r + 1, :], (HEADDIM, CHUNK)) for r in range(HEADS_PER_GROUP)]
    return jnp.concatenate(parts, axis=0).T


def _head_rows(rows):
    parts = [jnp.broadcast_to(rows[r:r + 1, :], (HEADDIM, N_STATE)) for r in range(HEADS_PER_GROUP)]
    return jnp.concatenate(parts, axis=0)


def _ssd_common(dtr, bias, alog):
    dt = _softplus(dtr + bias)
    A = -jnp.exp(alog)
    a = dt * A
    ki = lax.broadcasted_iota(jnp.int32, (CHUNK, CHUNK), 0)
    si = lax.broadcasted_iota(jnp.int32, (CHUNK, CHUNK), 1)
    upper = (ki <= si).astype(F32)
    cs = _dot_sel(a, upper, 3)
    cs_last = jnp.broadcast_to(cs[:, CHUNK - 1:CHUNK], (8, CHUNK))
    return dt, A, a, cs, cs_last


def _decay_matrix(cs, r):
    li = lax.broadcasted_iota(jnp.int32, (CHUNK, CHUNK), 0)
    si = lax.broadcasted_iota(jnp.int32, (CHUNK, CHUNK), 1)
    causal = li >= si
    R = jnp.broadcast_to(cs[r:r + 1, :], (CHUNK, CHUNK))
    seg = jnp.where(causal, R.T - R, 0.0)
    return jnp.where(causal, jnp.exp(seg), 0.0)


def _decay_cat(cs):
    return jnp.concatenate([_decay_matrix(cs, r) for r in range(HEADS_PER_GROUP)], axis=1)


def _lanes4(m):
    return jnp.concatenate([m] * HEADS_PER_GROUP, axis=1)


def _head_blocks(v):
    col = lax.broadcasted_iota(jnp.int32, v.shape, 1) // HEADDIM
    return jnp.concatenate([jnp.where(col == r, v, jnp.zeros_like(v)) for r in range(HEADS_PER_GROUP)], axis=0)


GXBC = GW + 2 * N_STATE


GS_FWD = 8
GS_BWD = 8


def _ssd_in_specs(nc, rev):
    GS = GS_BWD if rev else GS_FWD
    cix = (lambda c: nc - 1 - c) if rev else (lambda c: c)
    x_s = pl.BlockSpec((CHUNK, GS * GW), lambda g, c: (cix(c), g))
    xbc_s = pl.BlockSpec((CHUNK, GS * GXBC), lambda g, c: (cix(c), g))
    dtr_s = pl.BlockSpec((GS, 8, CHUNK), lambda g, c: (g, 0, cix(c)))
    row_s = pl.BlockSpec((GS, 8, CHUNK), lambda g, c: (g, 0, 0))
    drep_s = pl.BlockSpec((1, GS * GW), lambda g, c: (0, g))
    hs_s = pl.BlockSpec((1, GS * GW, N_STATE), lambda g, c: (cix(c), g, 0))
    return x_s, xbc_s, dtr_s, row_s, drep_s, hs_s


def _xbc_parts(xbc_ref, gi):
    o = gi * GXBC
    return xbc_ref[:, o:o + GW], xbc_ref[:, o + GW:o + GW + N_STATE], xbc_ref[:, o + GW + N_STATE:o + GXBC]


def _ssd_fwd(xbc, dtr, bias, alog, drep):
    T = xbc.shape[0]
    nc = T // CHUNK
    x_s, xbc_s, dtr_s, row_s, drep_s, hs_s = _ssd_in_specs(nc, False)

    def body(xbc_ref, dtr_ref, bias_ref, alog_ref, drep_ref, y_ref, hs_ref, h_scr):
        @pl.when(pl.program_id(1) == 0)
        def _():
            h_scr[...] = jnp.zeros_like(h_scr)

        for gi in range(GS_FWD):
            cols, rows = slice(gi * GW, (gi + 1) * GW), pl.ds(gi * GW, GW)
            x, Bm, Cm = _xbc_parts(xbc_ref, gi)
            dt, A, a, cs, cs_last = _ssd_common(dtr_ref[gi], bias_ref[gi], alog_ref[gi])
            E = _head_cols(jnp.exp(cs))
            W = _head_cols(jnp.exp(cs_last - cs) * dt)
            X = (x * _head_cols(dt)).astype(BF16)
            CB = _dot_nt(Cm, Bm)
            col = lax.broadcasted_iota(jnp.int32, (CHUNK, GW), 1) // HEADDIM
            y = jnp.zeros((CHUNK, GW), F32)
            for r in range(HEADS_PER_GROUP):
                y = y + jnp.where(col == r, _dot(CB * _decay_matrix(cs, r), X), 0.0)
            h = h_scr[rows, :]
            hs_ref[0, rows, :] = h
            y = y + _dot_nt(Cm, h) * E
            y_ref[:, cols] = y + drep_ref[:, cols] * x
            h_scr[rows, :] = h * _head_rows(jnp.exp(cs_last)) + _dot_tn(x * W, Bm)

    return pl.pallas_call(
        body, name="ssd_fwd", grid=(N_GROUPS // GS_FWD, nc),
        in_specs=[xbc_s, dtr_s, row_s, row_s, drep_s],
        out_specs=[x_s, hs_s],
        out_shape=[jax.ShapeDtypeStruct((T, D_SSM), F32), jax.ShapeDtypeStruct((nc, D_SSM, N_STATE), F32)],
        scratch_shapes=[pltpu.VMEM((GS_FWD * GW, N_STATE), F32)],
        compiler_params=_cparams(("parallel", "arbitrary")),
    )(xbc, dtr, bias, alog, drep)


def _ssd_bwd(xbc, dtr, bias, alog, drep, dy, hs):
    T = xbc.shape[0]
    nc = T // CHUNK
    x_s, xbc_s, dtr_s, row_s, drep_s, hs_s = _ssd_in_specs(nc, True)

    def body(xbc_ref, dtr_ref, bias_ref, alog_ref, drep_ref, dy_ref, hs_ref,
             dxbc_ref, ddtr_ref, dbias_ref, dalog_ref, dd_ref, dh_scr):
        @pl.when(pl.program_id(1) == 0)
        def _():
            dh_scr[...] = jnp.zeros_like(dh_scr)
            dbias_ref[...] = jnp.zeros_like(dbias_ref)
            dalog_ref[...] = jnp.zeros_like(dalog_ref)
            dd_ref[...] = jnp.zeros_like(dd_ref)

        for gi in range(GS_BWD):
            one_group(gi, xbc_ref, dtr_ref, bias_ref, alog_ref, drep_ref, dy_ref, hs_ref,
                      dxbc_ref, ddtr_ref, dbias_ref, dalog_ref, dd_ref, dh_scr)

    def one_group(gi, xbc_ref, dtr_ref, bias_ref, alog_ref, drep_ref, dy_ref, hs_ref,
                  dxbc_ref, ddtr_ref, dbias_ref, dalog_ref, dd_ref, dh_scr):
        cols, rows, o = slice(gi * GW, (gi + 1) * GW), pl.ds(gi * GW, GW), gi * GXBC
        x, Bm, Cm = _xbc_parts(xbc_ref, gi)
        dY = dy_ref[:, cols]
        dt, A, a, cs, cs_last = _ssd_common(dtr_ref[gi], bias_ref[gi], alog_ref[gi])
        E = _head_cols(jnp.exp(cs))
        DT = _head_cols(dt)
        Wd = _head_cols(jnp.exp(cs_last - cs))
        X = x * DT
        h = hs_ref[0, rows, :]
        dS = dh_scr[rows, :]
        CB = _dot_nt(Cm, Bm)
        rowid = lax.broadcasted_iota(jnp.int32, (8, CHUNK), 0)
        lane = lax.broadcasted_iota(jnp.int32, (8, CHUNK), 1)
        hsel = (lax.broadcasted_iota(jnp.int32, (8, GW), 1) // HEADDIM
                == lax.broadcasted_iota(jnp.int32, (8, GW), 0)).astype(F32)
        hsel_l = (lax.broadcasted_iota(jnp.int32, (8, HEADS_PER_GROUP * CHUNK), 1) // CHUNK
                  == lax.broadcasted_iota(jnp.int32, (8, HEADS_PER_GROUP * CHUNK), 0)).astype(F32)

        Lc, CBc = _decay_cat(cs), _lanes4(CB)
        Mc = CBc * Lc
        GLc = _dot_nt(dY, _head_blocks(X.astype(BF16))) * Lc
        Wc = GLc * CBc
        colsum = jnp.sum(Wc, axis=0, keepdims=True)
        dcs = _sel_dot_nt(hsel_l, Wc)
        dCB = jnp.zeros((CHUNK, CHUNK), F32)
        for r in range(HEADS_PER_GROUP):
            blk = slice(r * CHUNK, (r + 1) * CHUNK)
            dCB = dCB + GLc[:, blk]
            dcs = dcs - jnp.where(rowid == r, colsum[:, blk], 0.0)
        m_stack = jnp.concatenate([Mc[:, r * CHUNK:(r + 1) * CHUNK].astype(BF16) for r in range(HEADS_PER_GROUP)],
                                  axis=0)
        dX = lax.dot_general(m_stack, _head_blocks(dY.astype(BF16)), (((0,), (0,)), ((), ())),
                             preferred_element_type=F32)
        dC = _dot(dCB, Bm)
        dB = _dot_tn(dCB, Cm)
        T1 = _dot_nt(Bm, dS)
        dX = dX + T1 * Wd
        dB = dB + _dot(X * Wd, dS)
        pdec = _sel_dot_nt(hsel, X * T1 * Wd)
        dcs = dcs - pdec
        dlast = jnp.sum(pdec, axis=1, keepdims=True) \
            + jnp.exp(cs_last[:, 0:1]) * jnp.sum(_sel_dot(hsel, dS * h), axis=1, keepdims=True)
        dYE = dY * E
        dC = dC + _dot(dYE, h)
        yoff = _dot_nt(Cm, h) * E
        dcs = dcs + _sel_dot_nt(hsel, dY * yoff)
        dcs = dcs + jnp.where(lane == CHUNK - 1, dlast, 0.0)
        ki = lax.broadcasted_iota(jnp.int32, (CHUNK, CHUNK), 0)
        si = lax.broadcasted_iota(jnp.int32, (CHUNK, CHUNK), 1)
        lower = (ki >= si).astype(F32)
        da = _dot_sel(dcs, lower)
        ddt = da * A + _sel_dot_nt(hsel, dX * x)
        ddtr = ddt * _sigmoid(dtr_ref[gi] + bias_ref[gi])
        ddtr_ref[gi] = ddtr
        dbias_ref[gi] += ddtr
        dalog_ref[gi] += da * a
        dxbc_ref[:, o:o + GW] = dX * DT + drep_ref[:, cols] * dY
        dd_ref[:, cols] += jnp.sum(dY * x, axis=0, keepdims=True)
        dxbc_ref[:, o + GW:o + GW + N_STATE] = dB
        dxbc_ref[:, o + GW + N_STATE:o + GXBC] = dC
        dh_scr[rows, :] = dS * _head_rows(jnp.exp(cs_last)) + _dot_tn(dYE, Cm)

    return pl.pallas_call(
        body, name="ssd_bwd", grid=(N_GROUPS // GS_BWD, nc),
        in_specs=[xbc_s, dtr_s, row_s, row_s, drep_s, x_s, hs_s],
        out_specs=[xbc_s, dtr_s, row_s, row_s, drep_s],
        out_shape=[jax.ShapeDtypeStruct((T, D_XBC), F32),
                   jax.ShapeDtypeStruct((N_GROUPS, 8, T), F32),
                   jax.ShapeDtypeStruct((N_GROUPS, 8, CHUNK), F32),
                   jax.ShapeDtypeStruct((N_GROUPS, 8, CHUNK), F32),
                   jax.ShapeDtypeStruct((1, D_SSM), F32)],
        scratch_shapes=[pltpu.VMEM((GS_BWD * GW, N_STATE), F32)],
        compiler_params=_cparams(("parallel", "arbitrary")),
    )(xbc, dtr, bias, alog, drep, dy, hs)


def _adamw(w, g, m, v, name, deps=(), emit_g=False):
    R, C = w.shape
    tr = _tile(R, 256, 8)
    nd = len(deps)
    nout = 4 if emit_g else 3

    def body(w_ref, g_ref, m_ref, v_ref, *rest):
        outs = rest[nd:]
        gv = g_ref[...]
        mn = ADAM_B1 * m_ref[...] + (1.0 - ADAM_B1) * gv
        vn = ADAM_B2 * v_ref[...] + (1.0 - ADAM_B2) * (gv * gv)
        m_hat = mn / (1.0 - ADAM_B1 ** ADAM_STEP)
        v_hat = vn / (1.0 - ADAM_B2 ** ADAM_STEP)
        outs[0][...] = -ADAM_LR * (m_hat / (jnp.sqrt(v_hat) + ADAM_EPS) + ADAM_WD * w_ref[...])
        outs[1][...] = mn
        outs[2][...] = vn
        if emit_g:
            outs[3][...] = gv

    spec = pl.BlockSpec((tr, C), lambda i: (i, 0))
    return pl.pallas_call(
        body, name=name, grid=(R // tr,),
        in_specs=[spec] * 4 + [ANY] * nd, out_specs=[spec] * nout,
        out_shape=[jax.ShapeDtypeStruct((R, C), F32)] * nout,
        compiler_params=_cparams(("parallel",)),
    )(w, g, m, v, *deps)


ANY = pl.BlockSpec(memory_space=pl.ANY)


def _place():
    x, y, c = lax.axis_index("x"), lax.axis_index("y"), lax.axis_index("c")
    return x, y, c


def _other_chips(x, y):
    return [(1 - x, y), (x, 1 - y), (1 - x, 1 - y)]


def _allgather_inplace(bufs, splits, first_done=False):
    n = len(bufs)

    def body(*refs):
        o_refs = refs[n:2 * n]
        send_sems, recv_sems = refs[2 * n:]
        x, y, c = _place()
        xn, yn, dg, sibling = (1 - x, y), (x, 1 - y), (1 - x, 1 - y), (x, y, 1 - c)

        def blk(k, chip, pc):
            return o_refs[k].at[4 * chip[0] + 2 * chip[1] + pc]

        def part(k, ref, p):
            kind, s = splits[k]
            _, R, C = bufs[k].shape
            if kind == "rows":
                return ref.at[pl.ds(0, s)] if p == 0 else ref.at[pl.ds(s, R - s)]
            return ref.at[:, pl.ds(0, s)] if p == 0 else ref.at[:, pl.ds(s, C - s)]

        def copy(k, slot, ref, to):
            return pltpu.make_async_remote_copy(
                src_ref=ref, dst_ref=ref, send_sem=send_sems.at[k, slot], recv_sem=recv_sems.at[k, slot],
                device_id=to, device_id_type=MESH)

        sent = []

        def send(k, slot, ref, to):
            cp = copy(k, slot, ref, to)
            cp.start()
            sent.append(cp)

        if not first_done:
            for k in range(n):
                send(k, 0, blk(k, (x, y), c), (*xn, c))
                send(k, 1, blk(k, (x, y), c), (*yn, c))
        for k in range(n):
            bx, by = blk(k, xn, c), blk(k, yn, c)
            if not first_done:
                copy(k, 0, bx, sibling).wait_recv()
            send(k, 2, part(k, bx, 0), (*yn, c))
            send(k, 4, bx, sibling)
            if not first_done:
                copy(k, 1, by, sibling).wait_recv()
            send(k, 3, part(k, by, 1), (*xn, c))
            send(k, 5, by, sibling)
        for k in range(n):
            d0, d1 = part(k, blk(k, dg, c), 0), part(k, blk(k, dg, c), 1)
            copy(k, 2, d0, sibling).wait_recv()
            send(k, 6, d0, sibling)
            copy(k, 3, d1, sibling).wait_recv()
            send(k, 7, d1, sibling)
        for k in range(n):
            copy(k, 4, blk(k, xn, 1 - c), sibling).wait_recv()
            copy(k, 5, blk(k, yn, 1 - c), sibling).wait_recv()
            copy(k, 6, part(k, blk(k, dg, 1 - c), 0), sibling).wait_recv()
            copy(k, 7, part(k, blk(k, dg, 1 - c), 1), sibling).wait_recv()
        for cp in sent:
            cp.wait_send()

    return pl.pallas_call(
        body, name="allgather_w_in",
        in_specs=[ANY] * n, out_specs=[ANY] * n,
        out_shape=[jax.ShapeDtypeStruct(b.shape, b.dtype) for b in bufs],
        input_output_aliases={k: k for k in range(n)},
        scratch_shapes=[pltpu.SemaphoreType.DMA((n, 8)), pltpu.SemaphoreType.DMA((n, 8))],
    )(*bufs)


HBM = pl.BlockSpec(memory_space=pltpu.HBM)
SEM = pl.BlockSpec(memory_space=pltpu.SEMAPHORE)
EFFECT = pltpu.SideEffectType.DATAFLOW_SIDE_EFFECTING


def _split_start(name, arrays, build, n_copies, after=()):
    na, nd = len(arrays), len(after)

    def body(*refs):
        send_sems, recv_sems = refs[na + nd], refs[na + nd + 1]
        for cp in build(refs[:na], send_sems, recv_sems):
            cp.start()
        refs[-1][...] = jnp.zeros((8, 128), F32)

    outs = pl.pallas_call(
        body, name=name,
        out_shape=(pltpu.SemaphoreType.DMA((n_copies,)), pltpu.SemaphoreType.DMA((n_copies,)),
                   *[pltpu.HBM(a.shape, a.dtype) for a in arrays], jax.ShapeDtypeStruct((8, 128), F32)),
        in_specs=[HBM] * na + [ANY] * nd,
        out_specs=(SEM, SEM, *[HBM] * na, pl.BlockSpec(memory_space=pltpu.VMEM)),
        input_output_aliases={i: 2 + i for i in range(na)},
        compiler_params=pltpu.CompilerParams(has_side_effects=EFFECT),
    )(*[pltpu.with_memory_space_constraint(a, pltpu.HBM) for a in arrays], *after)
    return outs[0], outs[1], list(outs[2:2 + na]), outs[-1]


def _split_wait(name, send_sems, recv_sems, arrays, build, after):
    na = len(arrays)

    def body(*refs):
        for cp in build(refs[:na], refs[na], refs[na + 1]):
            cp.wait_send()
            cp.wait_recv()

    outs = pl.pallas_call(
        body, name=name,
        out_shape=tuple(pltpu.HBM(a.shape, a.dtype) for a in arrays),
        in_specs=[HBM] * na + [SEM, SEM] + [ANY] * len(after),
        out_specs=tuple([HBM] * na),
        input_output_aliases={i: i for i in range(na)},
        compiler_params=pltpu.CompilerParams(has_side_effects=EFFECT),
    )(*arrays, send_sems, recv_sems, *after)
    return list(outs)


def _remote(src, dst, send_sems, recv_sems, i, to):
    return pltpu.make_async_remote_copy(src_ref=src, dst_ref=dst, send_sem=send_sems.at[i], recv_sem=recv_sems.at[i],
                                        device_id=to, device_id_type=MESH)


def _build_ag_first(refs, ss, rs):
    x, y, c = _place()
    cps = []
    for k, ref in enumerate(refs):
        blk = ref.at[4 * x + 2 * y + c]
        cps += [_remote(blk, blk, ss, rs, 2 * k, (1 - x, y, c)), _remote(blk, blk, ss, rs, 2 * k + 1, (x, 1 - y, c))]
    return cps


def _build_ag_ici(refs, ss, rs):
    x, y, c = _place()
    cps = []
    for k, ref in enumerate(refs):
        blk = ref.at[4 * x + 2 * y + c]
        for j, (px, py) in enumerate(_other_chips(x, y)):
            cps.append(_remote(blk, blk, ss, rs, 3 * k + j, (px, py, c)))
    return cps


def _build_ag_fwd(refs, ss, rs):
    x, y, c = _place()
    cps = []
    for k, ref in enumerate(refs):
        for j, (px, py) in enumerate(_other_chips(x, y)):
            blk = ref.at[4 * px + 2 * py + c]
            cps.append(_remote(blk, blk, ss, rs, 3 * k + j, (x, y, 1 - c)))
    return cps


def _build_rs_swap(refs, ss, rs):
    x, y, c = _place()
    n = len(refs) // 2
    return [_remote(refs[k].at[:, pl.ds(1 - c, 1)], refs[n + k], ss, rs, k, (x, y, 1 - c)) for k in range(n)]


def _build_rs_ici(refs, ss, rs):
    x, y, c = _place()
    n = len(refs) // 2
    me = 2 * x + y
    cps = []
    for k in range(n):
        for j, (px, py) in enumerate(_other_chips(x, y)):
            cps.append(_remote(refs[k].at[2 * px + py], refs[n + k].at[me], ss, rs, 3 * k + j, (px, py, c)))
    return cps


def _build_rs_share(refs, ss, rs):
    x, y, c = _place()
    return [_remote(ref.at[c], ref.at[c], ss, rs, k, (x, y, 1 - c)) for k, ref in enumerate(refs)]


def _build_small_gather(refs, ss, rs):
    x, y, c = _place()
    me = 4 * x + 2 * y + c
    cps = []
    for d in range(1, N_DEV):
        to = (1 - x if d & 4 else x, 1 - y if d & 2 else y, 1 - c if d & 1 else c)
        cps.append(_remote(refs[0], refs[1].at[me], ss, rs, d - 1, to))
    return cps


def _sum_gathered(mine, landed, me_arr):
    R, C = mine.shape

    def body(me_ref, m_ref, l_ref, o_ref):
        me = me_ref[0]
        s = None
        for d in range(N_DEV):
            t = jnp.where(me == d, m_ref[...], l_ref[d])
            s = t if s is None else s + t
        o_ref[...] = s

    grid_spec = pltpu.PrefetchScalarGridSpec(
        num_scalar_prefetch=1, grid=(1,),
        in_specs=[pl.BlockSpec((R, C), lambda i, me_ref: (0, 0)),
                  pl.BlockSpec((N_DEV, R, C), lambda i, me_ref: (0, 0, 0))],
        out_specs=pl.BlockSpec((R, C), lambda i, me_ref: (0, 0)))
    return pl.pallas_call(
        body, name="sum_small", grid_spec=grid_spec,
        out_shape=jax.ShapeDtypeStruct((R, C), F32),
        compiler_params=_cparams(("arbitrary",)),
    )(me_arr, mine, landed)


def _rs_add_pair(p, r0, c_arr, name):
    _, _, hr, cols = p.shape
    tr = _tile(hr, 256, 8)

    def body(c_ref, p_ref, r_ref, q_ref):
        q_ref[...] = (p_ref[0].astype(F32) + r_ref[0].astype(F32)).astype(BF16)

    grid_spec = pltpu.PrefetchScalarGridSpec(
        num_scalar_prefetch=1, grid=(N_CHIPS, hr // tr),
        in_specs=[pl.BlockSpec((1, 1, tr, cols), lambda j, i, c_ref: (j, c_ref[0], i, 0)),
                  pl.BlockSpec((1, 1, tr, cols), lambda j, i, c_ref: (j, 0, i, 0))],
        out_specs=pl.BlockSpec((1, tr, cols), lambda j, i, c_ref: (j, i, 0)))
    return pl.pallas_call(
        body, name=name, grid_spec=grid_spec,
        out_shape=jax.ShapeDtypeStruct((N_CHIPS, hr, cols), BF16),
        compiler_params=_cparams(("parallel", "parallel")),
    )(c_arr, p, r0)


def _rs_add_chips(r1, q, place_arr, name):
    _, hr, cols = r1.shape
    tr = _tile(hr, 256, 8)

    def body(place_ref, r_ref, q_ref, o_ref):
        chip = place_ref[0]
        s = None
        for j in range(N_CHIPS):
            t = jnp.where(chip == j, q_ref[j], r_ref[j]).astype(F32)
            s = t if s is None else s + t
        o_ref[...] = s

    blk = pl.BlockSpec((N_CHIPS, tr, cols), lambda i, place_ref: (0, i, 0))
    grid_spec = pltpu.PrefetchScalarGridSpec(
        num_scalar_prefetch=1, grid=(hr // tr,), in_specs=[blk, blk],
        out_specs=pl.BlockSpec((None, tr, cols), lambda i, place_ref: (place_ref[1], i, 0)))
    return pl.pallas_call(
        body, name=name, grid_spec=grid_spec,
        out_shape=jax.ShapeDtypeStruct((2, hr, cols), F32),
        compiler_params=_cparams(("parallel",)),
    )(place_arr, r1, q)


def _pad_rows(a, rows):
    return jnp.pad(a, ((0, rows - a.shape[0]), (0, 0)))


def _pad_cols(a, cols):
    return jnp.pad(a, ((0, 0), (0, cols - a.shape[1])))


def _heads_to_rows(v):
    v = v.reshape(N_GROUPS, HEADS_PER_GROUP, 1)
    v = jnp.pad(v, ((0, 0), (0, 8 - HEADS_PER_GROUP), (0, 0)))
    return jnp.broadcast_to(v, (N_GROUPS, 8, CHUNK))


def _rows_to_heads(a):
    return jnp.sum(a[:, :HEADS_PER_GROUP, :], axis=-1).reshape(N_HEADS)


def _to_kernel_rows(a):
    C = a.shape[1]
    x0, b0, c0, s0 = D_SSM, 2 * D_SSM, 2 * D_SSM + 1024, D_SSM + D_XBC + N_HEADS
    xbc = jnp.concatenate([a[x0:b0].reshape(N_GROUPS, GW, C), a[b0:c0].reshape(N_GROUPS, N_STATE, C),
                           a[c0:c0 + 1024].reshape(N_GROUPS, N_STATE, C)], axis=1).reshape(D_XBC, C)
    sc = jnp.concatenate([a[s0 + k * D_MODEL:s0 + (k + 1) * D_MODEL].reshape(D_MODEL // SCB, SCB, C)
                          for k in range(3)], axis=1).reshape(3 * D_MODEL, C)
    return jnp.concatenate([a[:D_SSM], xbc, sc], axis=0)


HR_IN = 1568


def _kernel_segments():
    segs = [(0, 0, 0, D_SSM)]
    for g in range(N_GROUPS):
        k0 = D_SSM + g * GXBC
        segs += [(0, k0, D_SSM + g * GW, GW), (0, k0 + GW, 2 * D_SSM + g * N_STATE, N_STATE),
                 (0, k0 + GW + N_STATE, 2 * D_SSM + 1024 + g * N_STATE, N_STATE)]
    segs.append((1, 0, D_SSM + D_XBC, N_HEADS))
    for j in range(D_MODEL // SCB):
        for k in range(3):
            segs.append((0, D_SSM + D_XBC + j * SC3 + k * SCB, D_SSM + D_XBC + N_HEADS + k * D_MODEL + j * SCB, SCB))
    return segs


def _shard_row_plan():
    cs = D_IN // N_CHIPS
    plan = []
    for src, s, o, n in _kernel_segments():
        while n > 0:
            chip, loc = divmod(o, cs)
            half, row = divmod(loc, HR_IN)
            m = min(n, cs - loc, HR_IN - row)
            plan.append((src, s, chip, half, row, m))
            s, o, n = s + m, o + m, n - m
    return plan


SCATTER_ROWS = 512
SCATTER_SLOTS = 4


def _scatter_rows_to_shards(k_main, k_dt):
    C = k_main.shape[1]
    pieces = []
    for src, s, chip, half, row, n in _shard_row_plan():
        for o in range(0, n, SCATTER_ROWS):
            pieces.append((src, s + o, chip, half, row + o, min(SCATTER_ROWS, n - o)))
    S, lag, N = SCATTER_SLOTS, SCATTER_SLOTS // 2, len(pieces)

    def body(m_ref, d_ref, o_ref, buf, in_sems, out_sems):
        def cin(i):
            src, s, _, _, _, n = pieces[i]
            return pltpu.make_async_copy((d_ref if src else m_ref).at[pl.ds(s, n)],
                                         buf.at[i % S, pl.ds(0, n)], in_sems.at[i % S])

        def cout(i):
            _, _, chip, half, row, n = pieces[i]
            return pltpu.make_async_copy(buf.at[i % S, pl.ds(0, n)],
                                         o_ref.at[chip, half, pl.ds(row, n)], out_sems.at[i % S])

        for i in range(N + lag):
            if i < N:
                if i >= S:
                    cout(i - S).wait()
                cin(i).start()
            j = i - lag
            if 0 <= j < N:
                cin(j).wait()
                cout(j).start()
        for j in range(max(0, N - S), N):
            cout(j).wait()

    return pl.pallas_call(
        body, name="scatter_dw_in_rows", in_specs=[ANY, ANY], out_specs=ANY,
        out_shape=jax.ShapeDtypeStruct((N_CHIPS, 2, HR_IN, C), k_main.dtype),
        scratch_shapes=[pltpu.VMEM((S, SCATTER_ROWS, C), k_main.dtype),
                        pltpu.SemaphoreType.DMA((S,)), pltpu.SemaphoreType.DMA((S,))],
        compiler_params=_cparams(),
    )(k_main, k_dt)


ROWS_IN = D_IN // N_CHIPS
ROWS_IN_PAD = ROWS_IN + 8


def _cast_w_in_into_gather(wt32, chip_arr):
    R, C = wt32.shape
    hc, cbk = C // 2, 256

    def body(chip_ref, w_ref, o_ref):
        y = jnp.concatenate([w_ref[...], jnp.zeros((ROWS_IN_PAD - R, cbk), F32)], axis=0)
        odd = chip_ref[0] % 2 == 1
        o_ref[...] = jnp.where(odd, pltpu.roll(y, ROWS_IN_PAD - R, axis=0), y).astype(BF16)

    grid_spec = pltpu.PrefetchScalarGridSpec(
        num_scalar_prefetch=1, grid=(2, hc // cbk),
        in_specs=[pl.BlockSpec((R, cbk), lambda h, s, chip_ref: (0, h * (hc // cbk) + s))],
        out_specs=pl.BlockSpec((None, ROWS_IN_PAD, cbk), lambda h, s, chip_ref: (2 * chip_ref[0] + h, 0, s)))
    return pl.pallas_call(
        body, name="cast_w_in", grid_spec=grid_spec,
        out_shape=jax.ShapeDtypeStruct((N_DEV, ROWS_IN_PAD, hc), BF16),
        compiler_params=_cparams(("parallel", "parallel")),
    )(chip_arr, wt32)


def _gather_row_plan():
    segs = [(s, o, n) for src, s, o, n in _kernel_segments() if src == 0]
    plan, merges = [], []
    for k, o, n in segs:
        while n > 0:
            chip, loc = divmod(o, ROWS_IN)
            m = min(n, ROWS_IN - loc)
            ps, kd, cnt = loc + 8 * (chip % 2), k, m
            if ps % 16:
                ps, kd, cnt = ps - 8, kd - 8, cnt + 8
            if (ps + cnt) % 16:
                cnt -= 8
                merges.append((kd + cnt, chip, chip + 1))
            if cnt:
                plan.append((chip, ps, kd, cnt))
            k, o, n = k + m, o + m, n - m
    return plan, merges


def _gather_to_kernel_rows(g):
    hc = g.shape[2]
    plan, merges = _gather_row_plan()
    pieces = []
    for chip, ps, kd, n in plan:
        for o in range(0, n, SCATTER_ROWS):
            pieces.append((chip, ps + o, kd + o, min(SCATTER_ROWS, n - o)))
    S, lag, N = SCATTER_SLOTS, SCATTER_SLOTS // 2, len(pieces)

    def body(g_ref, o_ref, buf, mbuf, in_sems, out_sems, m_sems):
        def cins(i):
            chip, ps, _, n = pieces[i]
            return [pltpu.make_async_copy(g_ref.at[2 * chip + h, pl.ds(ps, n)],
                                          buf.at[i % S, pl.ds(0, n), pl.ds(h * hc, hc)], in_sems.at[i % S, h])
                    for h in range(2)]

        def cout(i):
            _, _, kd, n = pieces[i]
            return pltpu.make_async_copy(buf.at[i % S, pl.ds(0, n)], o_ref.at[pl.ds(kd, n)], out_sems.at[i % S])

        for i in range(N + lag):
            if i < N:
                if i >= S:
                    cout(i - S).wait()
                for cp in cins(i):
                    cp.start()
            j = i - lag
            if 0 <= j < N:
                for cp in cins(j):
                    cp.wait()
                cout(j).start()
        for j in range(max(0, N - S), N):
            cout(j).wait()
        for t, (kd, ce, co) in enumerate(merges):
            loads = []
            for h in range(2):
                loads.append(pltpu.make_async_copy(g_ref.at[2 * ce + h, pl.ds(ROWS_IN - 8, 16)],
                                                   mbuf.at[0, :, pl.ds(h * hc, hc)], m_sems.at[2 * h]))
                loads.append(pltpu.make_async_copy(g_ref.at[2 * co + h, pl.ds(0, 16)],
                                                   mbuf.at[1, :, pl.ds(h * hc, hc)], m_sems.at[2 * h + 1]))
            for cp in loads:
                cp.start()
            for cp in loads:
                cp.wait()
            row = lax.broadcasted_iota(jnp.int32, (16, 2 * hc), 0)
            mbuf[2] = jnp.where(row < 8, mbuf[0].astype(F32), mbuf[1].astype(F32)).astype(g.dtype)
            st = pltpu.make_async_copy(mbuf.at[2], o_ref.at[pl.ds(kd, 16)], m_sems.at[4])
            st.start()
            st.wait()

    return pl.pallas_call(
        body, name="w_in_to_kernel_rows", in_specs=[ANY], out_specs=ANY,
        out_shape=jax.ShapeDtypeStruct((D_MAIN, 2 * hc), g.dtype),
        scratch_shapes=[pltpu.VMEM((S, SCATTER_ROWS, 2 * hc), g.dtype), pltpu.VMEM((3, 16, 2 * hc), g.dtype),
                        pltpu.SemaphoreType.DMA((S, 2)), pltpu.SemaphoreType.DMA((S,)),
                        pltpu.SemaphoreType.DMA((5,))],
        compiler_params=_cparams(),
    )(g)


def _to_kernel_xbc(a):
    R = a.shape[0]
    return jnp.concatenate([a[:, :D_SSM].reshape(R, N_GROUPS, GW), a[:, D_SSM:D_SSM + 1024].reshape(R, N_GROUPS, N_STATE),
                            a[:, D_SSM + 1024:].reshape(R, N_GROUPS, N_STATE)], axis=2).reshape(R, D_XBC)


def _from_kernel_xbc(a):
    R = a.shape[0]
    g = a.reshape(R, N_GROUPS, GXBC)
    return jnp.concatenate([g[:, :, :GW].reshape(R, D_SSM), g[:, :, GW:GW + N_STATE].reshape(R, 1024),
                            g[:, :, GW + N_STATE:].reshape(R, 1024)], axis=1)


def kernel(x, norm_mix_g, w_in, ssm_conv_w, ssm_conv_b, ssm_dt_bias, ssm_A_log, ssm_D, ssm_norm_g, sc_conv_w, w_out, norm_ffn_g, w_gate, w_up, w_down, norm_final_g, loss_target, m_norm_mix_g, m_w_in, m_ssm_conv_w, m_ssm_conv_b, m_ssm_dt_bias, m_ssm_A_log, m_ssm_D, m_ssm_norm_g, m_sc_conv_w, m_w_out, m_norm_ffn_g, m_w_gate, m_w_up, m_w_down, m_norm_final_g, v_norm_mix_g, v_w_in, v_ssm_conv_w, v_ssm_conv_b, v_ssm_dt_bias, v_ssm_A_log, v_ssm_D, v_ssm_norm_g, v_sc_conv_w, v_w_out, v_norm_ffn_g, v_w_gate, v_w_up, v_w_down, v_norm_final_g):
    T = x.shape[1]
    xt = x[0]
    tgt = loss_target[0]
    cx, cy, cc = lax.axis_index("x"), lax.axis_index("y"), lax.axis_index("c")
    chip = 2 * cx + cy
    c_arr = jnp.reshape(cc, (1,)).astype(jnp.int32)
    chip_arr = jnp.reshape(chip, (1,)).astype(jnp.int32)
    place_arr = jnp.stack([chip, cc]).astype(jnp.int32)

    big = [w_in[0].T, w_out[0], w_gate[0], w_up[0], w_down[0]]
    names = ["w_in", "w_out", "w_gate", "w_up", "w_down"]
    gb_in = _cast_w_in_into_gather(big[0], chip_arr)
    cs_in, cs_conv = D_IN // N_CHIPS, D_XBC // N_CHIPS
    cw = jnp.stack([_pad_rows(ssm_conv_w[0], 8), _pad_cols(_pad_rows(sc_conv_w[0], 8), cs_conv)])
    cw_buf = lax.dynamic_update_slice(jnp.zeros((N_DEV, 8, cs_conv), F32), cw, (2 * chip, 0, 0))
    f_ss, f_rs, f_arr, f_tok = _split_start("ag_in_first_start", [gb_in, cw_buf], _build_ag_first, 4)
    gbufs = [None] + [_cast_into_gather(w, chip_arr, "cast_" + nm, deps=[f_tok]) for w, nm in zip(big[1:], names[1:])]
    n1 = _rmsnorm_fwd(xt, _tie(norm_mix_g, f_tok, "tie_ag_first"), "rmsnorm_mix")
    f_arr = _split_wait("ag_in_first_wait", f_ss, f_rs, f_arr, _build_ag_first, after=gbufs[1:] + [n1])
    g_in, cw_all = _allgather_inplace(f_arr, [("rows", (ROWS_IN_PAD // 32) * 16), ("cols", cs_conv // 2)],
                                      first_done=True)
    cw_all = cw_all.reshape(N_CHIPS, 2, 8, cs_conv)
    ssm_w8 = _to_kernel_xbc(cw_all[:, 0].transpose(1, 0, 2).reshape(8, D_XBC))
    sc_w8 = cw_all[:, 1, :, :D_MODEL // N_CHIPS].transpose(1, 0, 2).reshape(8, D_MODEL)
    ssm_bk = _to_kernel_xbc(ssm_conv_b)
    wt_main = _gather_to_kernel_rows(g_in)
    dt_rows = [jnp.concatenate([g_in[2 * ch, r0:r0 + 16], g_in[2 * ch + 1, r0:r0 + 16]], axis=1)
               for ch, r0 in ((1, ROWS_IN_PAD - 16), (2, 0))]
    wt_dt = _pad_rows(jnp.concatenate(dt_rows, axis=0), DT_PAD)
    ag_ss, ag_rs, ag_bufs, ag_tok = _split_start("ag_ici_start", gbufs[1:], _build_ag_ici, 12, after=[g_in, cw_all])

    bias_rows = _heads_to_rows(ssm_dt_bias[0])
    alog_rows = _heads_to_rows(ssm_A_log[0])
    drep = jnp.repeat(ssm_D[0], HEADDIM).reshape(1, D_SSM)

    (proj,) = _matmul([(n1, wt_main)], tb=True, out_dtypes=[F32], name="mm_proj", deps=[ag_tok])
    (dt_raw,) = _matmul([(n1, wt_dt)], tb=True, out_dtypes=[F32], name="mm_proj_dt")
    xbc = _ssm_conv_fwd(proj, ssm_w8, ssm_bk)
    dtr = jnp.pad(dt_raw[:, :N_HEADS].T.reshape(N_GROUPS, HEADS_PER_GROUP, T), ((0, 0), (0, 4), (0, 0)))
    y_ssd, hs = _ssd_fwd(xbc, dtr, bias_rows, alog_rows, drep)
    ag_bufs = _split_wait("ag_ici_wait", ag_ss, ag_rs, ag_bufs, _build_ag_ici, after=[y_ssd])
    fw_ss, fw_rs, fw_bufs, fw_tok = _split_start("ag_fwd_start", ag_bufs, _build_ag_fwd, 12)
    y_mix = _shortconv_fwd(proj, sc_w8, _gated_norm_fwd(y_ssd, proj, _tie(ssm_norm_g, fw_tok, "tie_ag_fwd")))
    gath = _split_wait("ag_fwd_wait", fw_ss, fw_rs, fw_bufs, _build_ag_fwd, after=[y_mix])
    w_out_f = gath[0].reshape(2 * D_MODEL, D_MODEL)
    w_gate3 = gath[1].reshape(N_CHIPS, D_MODEL, D_FF // N_CHIPS)
    w_up3 = gath[2].reshape(N_CHIPS, D_MODEL, D_FF // N_CHIPS)
    w_down_f = gath[3].reshape(D_FF, D_MODEL)
    (h1,) = _matmul([(y_mix, w_out_f)], out_dtypes=[F32], name="mm_out", extras=[xt],
                    epilogue=lambda acc, res: (acc + res,))
    n2 = _rmsnorm_fwd(h1, norm_ffn_g, "rmsnorm_ffn")
    g_act, u_act, a_act = _ffn_fwd(n2, w_gate3, w_up3)
    (h2,) = _matmul([(a_act, w_down_f)], out_dtypes=[F32], name="mm_down", extras=[h1],
                    epilogue=lambda acc, res: (acc + res,))

    dh2, dh2b, dg_final, loss_part = _loss_and_final_bwd(h2, tgt, norm_final_g.reshape(1, D_MODEL))
    dg_act, du_act = _matmul([(dh2b, w_down_f)], tb=True, out_dtypes=[BF16, BF16], name="mm_down_bwd",
                             tn=512, extras=[g_act, u_act], epilogue=_swiglu_bwd, nsub=2)
    (dw_down,) = _matmul([(a_act, dh2b)], ta=True, out_dtypes=[BF16], name="mm_dw_down", tm=1408, tn=512)
    (dn2,) = _matmul([(dg_act, w_gate3), (du_act, w_up3)], tb=True, b3d=True, out_dtypes=[BF16],
                     name="mm_ffn_in_bwd")
    (dw_gate,) = _matmul([(n2, dg_act)], ta=True, out_dtypes=[BF16], name="mm_dw_gate", tm=512, tn=1408,
                         col_shards=True)
    (dw_up,) = _matmul([(n2, du_act)], ta=True, out_dtypes=[BF16], name="mm_dw_up", tm=512, tn=1408,
                       col_shards=True)
    dh1, dh1b, dg_ffn = _rmsnorm_bwd(dn2, h1, norm_ffn_g, dh2, "rmsnorm_ffn_bwd")
    (dw_out,) = _matmul([(y_mix, dh1b)], ta=True, out_dtypes=[BF16], name="mm_dw_out")

    def halves(g):
        return g.reshape(N_CHIPS, 2, g.shape[1] // 2, g.shape[2])

    def landing(shape, dtype):
        return lax.empty(shape, dtype)

    names1 = names[1:]
    ps1 = [halves(dw_out.reshape(N_CHIPS, -1, D_MODEL)), halves(dw_gate), halves(dw_up),
           halves(dw_down.reshape(N_CHIPS, -1, D_MODEL))]
    r0_1 = [landing((N_CHIPS, 1) + p.shape[2:], p.dtype) for p in ps1]
    sw_ss, sw_rs, sw_arr, sw_tok = _split_start("rs1_swap_start", ps1 + r0_1, _build_rs_swap, 4)
    (dmix,) = _matmul([(dh1b, w_out_f)], tb=True, out_dtypes=[BF16], name="mm_out_bwd", deps=[sw_tok])
    dproj, dw_sc = _shortconv_bwd(dmix, proj, sc_w8)
    dy_ssd, dproj, dg_ssmnorm = _gated_norm_bwd(dmix, y_ssd, proj, ssm_norm_g, dproj)
    sw_arr = _split_wait("rs1_swap_wait", sw_ss, sw_rs, sw_arr, _build_rs_swap, after=[dy_ssd])
    qs1 = [_rs_add_pair(p, r, c_arr, "rs_add_pair_" + nm) for p, r, nm in zip(sw_arr[:4], sw_arr[4:], names1)]
    r1_1 = [landing(q.shape, BF16) for q in qs1]
    ic_ss, ic_rs, ic_arr, ic_tok = _split_start("rs1_ici_start", qs1 + r1_1, _build_rs_ici, 12)
    dxbc_act, ddtr, dbias_acc, dalog_acc, dD_acc = _ssd_bwd(
        xbc, dtr, bias_rows, alog_rows, _tie(drep, ic_tok, "tie_rs1_ici"), dy_ssd, hs)
    dproj, dw_ssmconv, db_ssmconv = _ssm_conv_bwd(dxbc_act, proj, ssm_w8, ssm_bk, dproj)
    dw_ssmconv, db_ssmconv = _from_kernel_xbc(dw_ssmconv), _from_kernel_xbc(db_ssmconv)
    ddt_raw = _pad_cols(ddtr[:, :HEADS_PER_GROUP, :].reshape(N_HEADS, T).T, DT_PAD).astype(BF16)
    (dwt_main,) = _matmul([(dproj, n1)], ta=True, out_dtypes=[F32], name="mm_dw_main")
    (dwt_dt,) = _matmul([(ddt_raw, n1)], ta=True, out_dtypes=[F32], name="mm_dw_dt")
    ic_arr = _split_wait("rs1_ici_wait", ic_ss, ic_rs, ic_arr, _build_rs_ici, after=[dwt_main])
    g1 = [_rs_add_chips(r, q, place_arr, "rs_add_chips_" + nm) for q, r, nm in zip(ic_arr[:4], ic_arr[4:], names1)]
    sh_ss, sh_rs, sh_arr, sh_tok = _split_start("rs1_share_start", g1, _build_rs_share, 4)
    p_in = _scatter_rows_to_shards(dwt_main, dwt_dt)
    s2_ss, s2_rs, s2_arr, s2_tok = _split_start(
        "rs2_swap_start", [p_in, landing((N_CHIPS, 1) + p_in.shape[2:], F32)], _build_rs_swap, 1)
    tm_pb = 1024
    mt = T // _tile(T, tm_pb)
    mt_a = max(mt // 4, 1)
    (dn1a,) = _matmul([(dproj, wt_main)], out_dtypes=[F32], name="mm_proj_bwd_a", deps=[s2_tok], tm=tm_pb,
                      m_tiles=(0, mt_a))
    s2_arr = _split_wait("rs2_swap_wait", s2_ss, s2_rs, s2_arr, _build_rs_swap, after=[dn1a])
    q_in = _rs_add_pair(s2_arr[0], s2_arr[1], c_arr, "rs_add_pair_w_in")
    i2_ss, i2_rs, i2_arr, i2_tok = _split_start(
        "rs2_ici_start", [q_in, landing(q_in.shape, BF16)], _build_rs_ici, 3)
    if mt > mt_a:
        (dn1a,) = _matmul([(dproj, wt_main)], out_dtypes=[F32], name="mm_proj_bwd_b", deps=[i2_tok], tm=tm_pb,
                          m_tiles=(mt_a, mt - mt_a), out_buf=dn1a)
    (dn1,) = _matmul([(ddt_raw, wt_dt)], out_dtypes=[BF16], name="mm_proj_dt_bwd", extras=[dn1a],
                     epilogue=lambda acc, res: (acc + res,), deps=[i2_tok])
    dx, _, dg_mix = _rmsnorm_bwd(dn1, xt, norm_mix_g, dh1, "rmsnorm_mix_bwd")
    g1 = _split_wait("rs1_share_wait", sh_ss, sh_rs, sh_arr, _build_rs_share, after=[dx])

    big_m = [m_w_in[0].T, m_w_out[0], m_w_gate[0], m_w_up[0], m_w_down[0]]
    big_v = [v_w_in[0].T, v_w_out[0], v_w_gate[0], v_w_up[0], v_w_down[0]]
    big_grads = [None] + [g.reshape(w.shape) for g, w in zip(g1, big[1:])]
    big_out = {}
    for k in range(1, 5):
        *big_out[names[k]], big_grads[k] = _adamw(big[k], big_grads[k], big_m[k], big_v[k], "adamw_" + names[k],
                                                   deps=[i2_tok], emit_g=True)
    i2_arr = _split_wait("rs2_ici_wait", i2_ss, i2_rs, i2_arr, _build_rs_ici, after=[big_out[names[4]][0], dx])
    g_in_red = _rs_add_chips(i2_arr[1], i2_arr[0], place_arr, "rs_add_chips_w_in")
    s3_ss, s3_rs, s3_arr, s3_tok = _split_start("rs2_share_start", [g_in_red], _build_rs_share, 1)

    dD = jnp.sum(dD_acc.reshape(N_HEADS, HEADDIM), axis=-1)
    heads_row = jnp.concatenate([_rows_to_heads(dbias_acc), _rows_to_heads(dalog_acc), dD,
                                 loss_part.reshape(1)]).reshape(1, -1)
    small = jnp.concatenate([
        dw_ssmconv,
        _pad_cols(dw_sc, D_XBC),
        db_ssmconv,
        jnp.concatenate([dg_mix, dg_ssmnorm], axis=1),
        jnp.concatenate([dg_ffn, dg_final], axis=1),
        _pad_cols(heads_row, D_XBC),
        jnp.zeros((4, D_XBC), F32),
    ], axis=0)
    sm_ss, sm_rs, sm_arr, sm_tok = _split_start(
        "small_gather_start", [small, landing((N_DEV,) + small.shape, F32)], _build_small_gather, N_DEV - 1,
        after=[s3_tok])
    (g_in_full,) = _split_wait("rs2_share_wait", s3_ss, s3_rs, s3_arr, _build_rs_share, after=[sm_tok])
    d_t, m_t, v_t, g_t = _adamw(big[0], g_in_full.reshape(2 * HR_IN, D_MODEL), big_m[0], big_v[0],
                                "adamw_" + names[0], emit_g=True)
    big_grads[0] = g_t.T
    big_out[names[0]] = (d_t.T, m_t.T, v_t.T)
    sm_arr = _split_wait("small_gather_wait", sm_ss, sm_rs, sm_arr, _build_small_gather, after=[d_t])
    tot = _sum_gathered(sm_arr[0], sm_arr[1], jnp.reshape(4 * cx + 2 * cy + cc, (1,)).astype(jnp.int32))
    loss = tot[19, 3 * N_HEADS]

    cs_ssm, cs_sc = D_XBC // N_CHIPS, D_MODEL // N_CHIPS
    g_ssm_conv = lax.dynamic_slice(tot[0:K_SSM], (0, chip * cs_ssm), (K_SSM, cs_ssm))
    g_sc_conv = lax.dynamic_slice(tot[8:8 + K_SC, :D_MODEL], (0, chip * cs_sc), (K_SC, cs_sc))
    small_grads = {
        "norm_mix_g": tot[17:18, :D_MODEL], "ssm_conv_w": g_ssm_conv, "ssm_conv_b": tot[16:17],
        "ssm_dt_bias": tot[19:20, 0:N_HEADS], "ssm_A_log": tot[19:20, N_HEADS:2 * N_HEADS],
        "ssm_D": tot[19:20, 2 * N_HEADS:3 * N_HEADS], "ssm_norm_g": tot[17:18, D_MODEL:],
        "sc_conv_w": g_sc_conv, "norm_ffn_g": tot[18:19, :D_MODEL], "norm_final_g": tot[18:19, D_MODEL:],
    }
    small_w = {"norm_mix_g": (norm_mix_g, m_norm_mix_g, v_norm_mix_g),
               "ssm_conv_w": (ssm_conv_w[0], m_ssm_conv_w[0], v_ssm_conv_w[0]),
               "ssm_conv_b": (ssm_conv_b, m_ssm_conv_b, v_ssm_conv_b),
               "ssm_dt_bias": (ssm_dt_bias, m_ssm_dt_bias, v_ssm_dt_bias),
               "ssm_A_log": (ssm_A_log, m_ssm_A_log, v_ssm_A_log),
               "ssm_D": (ssm_D, m_ssm_D, v_ssm_D),
               "ssm_norm_g": (ssm_norm_g, m_ssm_norm_g, v_ssm_norm_g),
               "sc_conv_w": (sc_conv_w[0], m_sc_conv_w[0], v_sc_conv_w[0]),
               "norm_ffn_g": (norm_ffn_g, m_norm_ffn_g, v_norm_ffn_g),
               "norm_final_g": (norm_final_g.reshape(1, -1), m_norm_final_g.reshape(1, -1),
                                v_norm_final_g.reshape(1, -1))}
    PW = 1024
    order = list(small_w)

    def pack(arrs):
        rows = []
        for a in arrs:
            flat = a.reshape(-1)
            n = -(-flat.shape[0] // PW) * PW
            rows.append(jnp.pad(flat, (0, n - flat.shape[0])).reshape(-1, PW))
        slab = jnp.concatenate(rows, axis=0)
        return _pad_rows(slab, -(-slab.shape[0] // 8) * 8)

    wp = pack([small_w[k][0] for k in order])
    mp = pack([small_w[k][1] for k in order])
    vp = pack([small_w[k][2] for k in order])
    gp = pack([small_grads[k] for k in order])
    sd, sm, sv = _adamw(wp, gp, mp, vp, "adamw_small")

    def unpack(slab):
        out, row = {}, 0
        for k in order:
            shape = small_w[k][0].shape
            size = 1
            for s in shape:
                size *= s
            nr = -(-size // PW)
            out[k] = slab[row:row + nr].reshape(-1)[:size].reshape(shape)
            row += nr
        return out

    s_delta, s_m, s_v = unpack(sd), unpack(sm), unpack(sv)

    big_g = dict(zip(names, big_grads))

    weight_order = ["norm_mix_g", "w_in", "ssm_conv_w", "ssm_conv_b", "ssm_dt_bias", "ssm_A_log", "ssm_D",
                    "ssm_norm_g", "sc_conv_w", "w_out", "norm_ffn_g", "w_gate", "w_up", "w_down", "norm_final_g"]
    lead = {"ssm_conv_w", "sc_conv_w", "w_in", "w_out", "w_gate", "w_up", "w_down"}

    def shaped(nm, a):
        if nm == "norm_final_g":
            return a.reshape(D_MODEL)
        return a[None] if nm in lead else a

    grads, deltas, new_m, new_v = [], [], [], []
    for nm in weight_order:
        if nm in big_out:
            g, (d, m, v) = big_g[nm], big_out[nm]
        else:
            g, d, m, v = small_grads[nm], s_delta[nm], s_m[nm], s_v[nm]
        grads.append(shaped(nm, g))
        deltas.append(shaped(nm, d))
        new_m.append(shaped(nm, m))
        new_v.append(shaped(nm, v))
    return (loss, dx[None], *grads, *deltas, *new_m, *new_v)


def _swiglu_bwd(da, dg_factor, du_factor):
    return da * dg_factor.astype(F32), da * du_factor.astype(F32)


def _ffn_fwd(n2, w_gate, w_up):
    T, K = n2.shape
    tn = w_gate.shape[2]
    N = N_CHIPS * tn
    tm = _tile(T, 512)
    sub = _tile(tm, 256)

    def body(a_ref, wg_ref, wu_ref, g_ref, u_ref, act_ref):
        for s in range(tm // sub):
            rows = pl.ds(s * sub, sub)
            a = a_ref[rows, :]
            g = jnp.dot(a, wg_ref[...], preferred_element_type=F32)
            u = jnp.dot(a, wu_ref[...], preferred_element_type=F32)
            sig = _sigmoid(g)
            sg = g * sig
            g_ref[rows, :] = (u * (sig * (1.0 + g - sg))).astype(BF16)
            u_ref[rows, :] = sg.astype(BF16)
            act_ref[rows, :] = (sg * u).astype(BF16)

    a_spec = pl.BlockSpec((tm, K), lambda j, i: (i, 0))
    b_spec = pl.BlockSpec((None, K, tn), lambda j, i: (j, 0, 0))
    o_spec = pl.BlockSpec((tm, tn), lambda j, i: (i, j))
    return pl.pallas_call(
        body, name="ffn_fwd", grid=(N // tn, T // tm),
        in_specs=[a_spec, b_spec, b_spec], out_specs=[o_spec] * 3,
        out_shape=[jax.ShapeDtypeStruct((T, N), BF16)] * 3,
        compiler_params=_cparams(("parallel", "parallel")),
    )(n2, w_gate, w_up)
```

```python
import functools

import jax
import jax.numpy as jnp
from jax import lax
from jax.experimental import pallas as pl
from jax.experimental.pallas import tpu as pltpu

F32 = jnp.float32
BF16 = jnp.bfloat16
MESH = pl.DeviceIdType.MESH

D_MODEL = 2048
D_SSM = 2048
HEADDIM = 64
N_HEADS = 32
N_GROUPS = 8
HEADS_PER_GROUP = 4
N_STATE = 128
CHUNK = 128
K_SSM = 4
K_SC = 3
D_XBC = 4096
D_FF = 5632
D_IN = 12320
D_MAIN = 12288
OFF_XBC, OFF_CB, OFF_CC, OFF_CX = 2048, 6144, 8192, 10240
DT_PAD = 128
EPS = 1e-5
N_CHIPS = 4
N_DEV = 8

ADAM_LR = 0.001
ADAM_B1 = 0.9
ADAM_B2 = 0.999
ADAM_EPS = 1e-08
ADAM_WD = 0.01
ADAM_STEP = 10

V7X_VMEM_BYTES = 64 * 1024 * 1024
VMEM_LIMIT = V7X_VMEM_BYTES - 8 * 1024 * 1024


def _cparams(sem=None):
    if sem is None:
        return pltpu.CompilerParams(vmem_limit_bytes=VMEM_LIMIT)
    return pltpu.CompilerParams(dimension_semantics=sem, vmem_limit_bytes=VMEM_LIMIT)


def _tile(dim, pref, unit=128):
    best = None
    t = unit
    while t <= min(dim, pref):
        if dim % t == 0:
            best = t
        t += unit
    return best if best is not None else dim


def _sigmoid(x):
    return 1.0 / (1.0 + jnp.exp(-x))


def _silu(x):
    return x * _sigmoid(x)


def _dsilu(x):
    s = _sigmoid(x)
    return s * (1.0 + x * (1.0 - s))


def _softplus(x):
    return jnp.maximum(x, 0.0) + jnp.log(1.0 + jnp.exp(-jnp.abs(x)))


MATMUL_VMEM_BUDGET = 44 * 1024 * 1024


def _matmul(pairs, *, ta=False, tb=False, out_dtypes, name, tm=1024, tn=1024, tk=None, extras=(), epilogue=None,
            deps=(), col_shards=False, nsub=1, b3d=False, m_tiles=None, out_buf=None):
    a0, b0 = pairs[0]
    M, K = (a0.shape[1], a0.shape[0]) if ta else a0.shape
    if b3d:
        N = b0.shape[1] if tb else b0.shape[0] * b0.shape[2]
        tk, tn = (b0.shape[2], tn) if tb else (tk, b0.shape[2])
    else:
        N = b0.shape[0] if tb else b0.shape[1]
    tm, tn = _tile(M, tm, 8 if M % 128 else 128), _tile(N, tn)
    npair, nex, ndep, nout = len(pairs), len(extras), len(deps), len(out_dtypes)
    if tk is None:
        fixed = 2 * tm * tn * (sum(jnp.dtype(d).itemsize for d in out_dtypes) + sum(e.dtype.itemsize for e in extras))
        tk = K
        while tk > 128 and (K % tk or tk % 128 or
                            fixed + 2 * npair * 2 * tk * (tm + tn) + (tm * tn * 4 if tk < K else 0) > MATMUL_VMEM_BUDGET):
            tk -= 128
    else:
        tk = _tile(K, tk)
    nk = K // tk
    if nk > 1 or tm % nsub or (tm // nsub) % 128:
        nsub = 1
    sub = tm // nsub
    dims = (((0 if ta else 1,), (1 if tb else 0,)), ((), ()))
    i0, mi = m_tiles if m_tiles is not None else (0, M // tm)
    nbuf = 0 if out_buf is None else 1

    def body(*refs):
        a_refs = refs[0:2 * npair:2]
        b_refs = refs[1:2 * npair:2]
        ex_refs = refs[2 * npair:2 * npair + nex]
        o_refs = refs[2 * npair + nex + ndep + nbuf:2 * npair + nex + ndep + nbuf + nout]

        def dots(rows):
            s = None
            for a_ref, b_ref in zip(a_refs, b_refs):
                a = a_ref[...] if rows is None else (a_ref[:, rows] if ta else a_ref[rows, :])
                d = lax.dot_general(a, b_ref[...], dims, preferred_element_type=F32)
                s = d if s is None else s + d
            return s

        def finish(r, rows):
            ex = [e[...] if rows is None else e[rows, :] for e in ex_refs]
            outs = (r,) if epilogue is None else epilogue(r, *ex)
            for o_ref, o in zip(o_refs, outs):
                if rows is None:
                    o_ref[...] = o.astype(o_ref.dtype)
                else:
                    o_ref[rows, :] = o.astype(o_ref.dtype)

        if nk == 1:
            for s in range(nsub):
                rows = None if nsub == 1 else pl.ds(s * sub, sub)
                finish(dots(rows), rows)
            return

        acc = refs[-1]
        k = pl.program_id(2)

        @pl.when(k == 0)
        def _():
            acc[...] = dots(None)

        @pl.when(jnp.logical_and(k > 0, k < nk - 1))
        def _():
            acc[...] += dots(None)

        @pl.when(k == nk - 1)
        def _():
            finish(acc[...] + dots(None), None)

    a_spec = (pl.BlockSpec((tk, tm), lambda i, j, k: (k, i + i0)) if ta
              else pl.BlockSpec((tm, tk), lambda i, j, k: (i + i0, k)))
    if b3d:
        b_spec = (pl.BlockSpec((None, tn, tk), lambda i, j, k: (k, j, 0)) if tb
                  else pl.BlockSpec((None, tk, tn), lambda i, j, k: (j, k, 0)))
    else:
        b_spec = (pl.BlockSpec((tn, tk), lambda i, j, k: (j, k)) if tb
                  else pl.BlockSpec((tk, tn), lambda i, j, k: (k, j)))
    e_spec = pl.BlockSpec((tm, tn), lambda i, j, k: (i + i0, j))
    if col_shards:
        o_spec = pl.BlockSpec((None, tm, tn), lambda i, j, k: (j, i + i0, 0))
        o_shape = (N // tn, M, tn)
    else:
        o_spec, o_shape = e_spec, (M, N)
    args, in_specs = [], []
    for a, b in pairs:
        args += [a, b]
        in_specs += [a_spec, b_spec]
    args += list(extras) + list(deps) + ([] if out_buf is None else [out_buf])
    in_specs += [e_spec] * nex + [ANY] * (ndep + nbuf)
    outs = pl.pallas_call(
        body,
        name=name,
        grid=(mi, N // tn, nk),
        in_specs=in_specs,
        out_specs=[o_spec] * nout,
        out_shape=[jax.ShapeDtypeStruct(o_shape, dt) for dt in out_dtypes],
        input_output_aliases={} if out_buf is None else {len(args) - 1: 0},
        scratch_shapes=[pltpu.VMEM((tm, tn), F32)] if nk > 1 else [],
        compiler_params=_cparams(("parallel", "parallel", "arbitrary")),
    )(*args)
    return outs


def _cast_into_gather(w, chip_arr, name, split_cols=False, deps=()):
    R, C = w.shape
    hr, hc = (R, C // 2) if split_cols else (R // 2, C)
    tr = _tile(hr, 512, 8)
    nb = hr // tr

    def body(chip_ref, w_ref, *rest):
        rest[-1][...] = w_ref[...].astype(BF16)

    in_map = (lambda h, i, chip_ref: (i, h)) if split_cols else (lambda h, i, chip_ref: (h * nb + i, 0))
    grid_spec = pltpu.PrefetchScalarGridSpec(
        num_scalar_prefetch=1, grid=(2, nb),
        in_specs=[pl.BlockSpec((tr, hc), in_map)] + [ANY] * len(deps),
        out_specs=pl.BlockSpec((None, tr, hc), lambda h, i, chip_ref: (2 * chip_ref[0] + h, i, 0)))
    return pl.pallas_call(
        body, name=name, grid_spec=grid_spec,
        out_shape=jax.ShapeDtypeStruct((N_DEV, hr, hc), BF16),
        compiler_params=_cparams(("parallel", "parallel")),
    )(chip_arr, w, *deps)


def _tie(small, token, name):
    def body(s_ref, t_ref, o_ref):
        o_ref[...] = s_ref[...]

    vm = pl.BlockSpec(memory_space=pltpu.VMEM)
    return pl.pallas_call(body, name=name, in_specs=[vm, ANY], out_specs=vm,
                          out_shape=jax.ShapeDtypeStruct(small.shape, small.dtype))(small, token)


def _rmsnorm_fwd(x, g, name):
    T, D = x.shape
    tt = _tile(T, 256)

    def body(x_ref, g_ref, n_ref):
        xv = x_ref[...]
        r = lax.rsqrt(jnp.mean(xv * xv, axis=-1, keepdims=True) + EPS)
        n_ref[...] = (xv * r * g_ref[...]).astype(BF16)

    return pl.pallas_call(
        body, name=name, grid=(T // tt,),
        in_specs=[pl.BlockSpec((tt, D), lambda i: (i, 0)), pl.BlockSpec((1, D), lambda i: (0, 0))],
        out_specs=pl.BlockSpec((tt, D), lambda i: (i, 0)),
        out_shape=jax.ShapeDtypeStruct((T, D), BF16),
        compiler_params=_cparams(("parallel",)),
    )(x, g)


def _rmsnorm_bwd(dn, x, g, res, name):
    T, D = x.shape
    tt = _tile(T, 256)

    def body(dn_ref, x_ref, g_ref, res_ref, dx_ref, dxb_ref, dg_ref):
        @pl.when(pl.program_id(0) == 0)
        def _():
            dg_ref[...] = jnp.zeros_like(dg_ref)

        xv = x_ref[...]
        dy = dn_ref[...].astype(F32)
        r = lax.rsqrt(jnp.mean(xv * xv, axis=-1, keepdims=True) + EPS)
        xhat = xv * r
        dxh = dy * g_ref[...]
        dx = res_ref[...] + r * (dxh - xhat * jnp.mean(dxh * xhat, axis=-1, keepdims=True))
        dx_ref[...] = dx
        dxb_ref[...] = dx.astype(BF16)
        dg_ref[...] += jnp.sum(dy * xhat, axis=0, keepdims=True)

    tok = pl.BlockSpec((tt, D), lambda i: (i, 0))
    vec = pl.BlockSpec((1, D), lambda i: (0, 0))
    return pl.pallas_call(
        body, name=name, grid=(T // tt,),
        in_specs=[tok, tok, vec, tok],
        out_specs=[tok, tok, vec],
        out_shape=[jax.ShapeDtypeStruct((T, D), F32), jax.ShapeDtypeStruct((T, D), BF16),
                   jax.ShapeDtypeStruct((1, D), F32)],
        compiler_params=_cparams(("arbitrary",)),
    )(dn, x, g, res)


def _loss_and_final_bwd(h2, target, gf):
    T, D = h2.shape
    tt = _tile(T, 256)

    def body(h_ref, t_ref, g_ref, dh_ref, dhb_ref, dg_ref, loss_ref):
        @pl.when(pl.program_id(0) == 0)
        def _():
            dg_ref[...] = jnp.zeros_like(dg_ref)
            loss_ref[...] = jnp.zeros_like(loss_ref)

        xv = h_ref[...]
        r = lax.rsqrt(jnp.mean(xv * xv, axis=-1, keepdims=True) + EPS)
        xhat = xv * r
        err = xhat * g_ref[...] - t_ref[...]
        loss_ref[...] += 0.5 * jnp.sum(jnp.mean(err * err, axis=-1, keepdims=True), axis=0, keepdims=True)
        dy = err * (1.0 / D)
        dxh = dy * g_ref[...]
        dx = r * (dxh - xhat * jnp.mean(dxh * xhat, axis=-1, keepdims=True))
        dh_ref[...] = dx
        dhb_ref[...] = dx.astype(BF16)
        dg_ref[...] += jnp.sum(dy * xhat, axis=0, keepdims=True)

    tok = pl.BlockSpec((tt, D), lambda i: (i, 0))
    vec = pl.BlockSpec((1, D), lambda i: (0, 0))
    return pl.pallas_call(
        body, name="loss_final_bwd", grid=(T // tt,),
        in_specs=[tok, tok, vec],
        out_specs=[tok, tok, vec, pl.BlockSpec((1, 1), lambda i: (0, 0))],
        out_shape=[jax.ShapeDtypeStruct((T, D), F32), jax.ShapeDtypeStruct((T, D), BF16),
                   jax.ShapeDtypeStruct((1, D), F32), jax.ShapeDtypeStruct((1, 1), F32)],
        compiler_params=_cparams(("arbitrary",)),
    )(h2, target, gf)


def _gated_norm_fwd(y, proj, g):
    T, D = y.shape
    tt = _tile(T, 256)

    def body(y_ref, z_ref, g_ref, o_ref):
        yg = y_ref[...] * _silu(z_ref[...])
        r = lax.rsqrt(jnp.mean(yg * yg, axis=-1, keepdims=True) + EPS)
        o_ref[...] = (yg * r * g_ref[...]).astype(BF16)

    tok = pl.BlockSpec((tt, D), lambda i: (i, 0))
    return pl.pallas_call(
        body, name="gated_norm_fwd", grid=(T // tt,),
        in_specs=[tok, tok, pl.BlockSpec((1, D), lambda i: (0, 0))],
        out_specs=tok,
        out_shape=jax.ShapeDtypeStruct((T, 2 * D_MODEL), BF16),
        compiler_params=_cparams(("parallel",)),
    )(y, proj, g)


def _gated_norm_bwd(dmix, y, proj, g, dproj):
    T, D = y.shape
    tt = _tile(T, 256)

    def body(do_ref, y_ref, z_ref, g_ref, dp_ref, dy_ref, dz_ref, dg_ref):
        @pl.when(pl.program_id(0) == 0)
        def _():
            dg_ref[...] = jnp.zeros_like(dg_ref)

        yv, zv = y_ref[...], z_ref[...]
        do = do_ref[...].astype(F32)
        sz = _silu(zv)
        yg = yv * sz
        r = lax.rsqrt(jnp.mean(yg * yg, axis=-1, keepdims=True) + EPS)
        xhat = yg * r
        dxh = do * g_ref[...]
        dyg = r * (dxh - xhat * jnp.mean(dxh * xhat, axis=-1, keepdims=True))
        dy_ref[...] = dyg * sz
        dz_ref[...] = (dyg * yv * _dsilu(zv)).astype(BF16)
        dg_ref[...] += jnp.sum(do * xhat, axis=0, keepdims=True)

    tok = pl.BlockSpec((tt, D), lambda i: (i, 0))
    vec = pl.BlockSpec((1, D), lambda i: (0, 0))
    return pl.pallas_call(
        body, name="gated_norm_bwd", grid=(T // tt,),
        in_specs=[tok, tok, tok, vec, ANY],
        out_specs=[tok, tok, vec],
        out_shape=[jax.ShapeDtypeStruct((T, D), F32), jax.ShapeDtypeStruct(dproj.shape, BF16),
                   jax.ShapeDtypeStruct((1, D), F32)],
        input_output_aliases={4: 1},
        compiler_params=_cparams(("arbitrary",)),
    )(dmix, y, proj, g, dproj)


HALO = 8


def _shift_down(cur, prev8, s):
    ext = jnp.concatenate([prev8, cur], axis=0)
    return pltpu.roll(ext, s, axis=0)[HALO:]


def _shift_up(cur, next8, s):
    n = cur.shape[0]
    ext = jnp.concatenate([cur, next8], axis=0)
    return pltpu.roll(ext, n + HALO - s, axis=0)[:n]


def _conv_specs(tt, cb, col_off_blocks, nt):
    hb = tt // HALO
    cur = pl.BlockSpec((tt, cb), lambda j, i: (i, col_off_blocks + j))
    prev = pl.BlockSpec((HALO, cb), lambda j, i: (jnp.maximum(i * hb - 1, 0), col_off_blocks + j))
    nxt = pl.BlockSpec((HALO, cb), lambda j, i: (jnp.minimum((i + 1) * hb, nt * hb - 1), col_off_blocks + j))
    return cur, prev, nxt


def _taps(cur, prev8, K):
    return [_shift_down(cur, prev8, K - 1 - k) for k in range(K - 1)] + [cur]


def _conv_of_taps(taps, w):
    y = taps[-1] * w[len(taps) - 1:len(taps), :]
    for k, t in enumerate(taps[:-1]):
        y = y + t * w[k:k + 1, :]
    return y


def _causal_conv(cur, prev8, w, K):
    return _conv_of_taps(_taps(cur, prev8, K), w)


def _anticausal_conv(cur, next8, w, K):
    y = cur * w[K - 1:K, :]
    for k in range(K - 1):
        y = y + _shift_up(cur, next8, K - 1 - k) * w[k:k + 1, :]
    return y


def _ssm_conv_fwd(proj, w8, b):
    T = proj.shape[0]
    tt, cb = _tile(T, 1024), 512
    nt = T // tt
    cur, prev, _ = _conv_specs(tt, cb, OFF_XBC // cb, nt)

    def body(u_ref, up_ref, w_ref, b_ref, o_ref):
        first = pl.program_id(1) == 0
        p8 = jnp.where(first, 0.0, up_ref[...])
        pre = _causal_conv(u_ref[...], p8, w_ref[...], K_SSM) + b_ref[...]
        o_ref[...] = _silu(pre)

    return pl.pallas_call(
        body, name="ssm_conv_fwd", grid=(D_XBC // cb, nt),
        in_specs=[cur, prev, pl.BlockSpec((8, cb), lambda j, i: (0, j)), pl.BlockSpec((1, cb), lambda j, i: (0, j))],
        out_specs=pl.BlockSpec((tt, cb), lambda j, i: (i, j)),
        out_shape=jax.ShapeDtypeStruct((T, D_XBC), F32),
        compiler_params=_cparams(("parallel", "parallel")),
    )(proj, proj, w8, b)


def _ssm_conv_bwd(dact, proj, w8, b, dproj):
    T = proj.shape[0]
    tt, cb = _tile(T, 1024), 512
    nt = T // tt
    cur, prev, nxt = _conv_specs(tt, cb, OFF_XBC // cb, nt)
    dcur, dprev, dnxt = _conv_specs(tt, cb, 0, nt)

    def dpre_of(d, u, p8, w, bb):
        pre = _causal_conv(u, p8, w, K_SSM) + bb
        return d * _dsilu(pre)

    def body(d_ref, dn_ref, u_ref, up_ref, un_ref, w_ref, b_ref, dp_ref, dx_ref, dw_ref, db_ref):
        i = pl.program_id(1)

        @pl.when(i == 0)
        def _():
            dw_ref[...] = jnp.zeros_like(dw_ref)
            db_ref[...] = jnp.zeros_like(db_ref)

        w, bb = w_ref[...], b_ref[...]
        u = u_ref[...]
        p8 = jnp.where(i == 0, 0.0, up_ref[...])
        taps = _taps(u, p8, K_SSM)
        dpre = d_ref[...] * _dsilu(_conv_of_taps(taps, w) + bb)
        un = un_ref[...]
        dpre_n = dpre_of(dn_ref[...], un, u[tt - HALO:, :], w, bb)
        dpre_n = jnp.where(i == nt - 1, 0.0, dpre_n)
        dx_ref[...] = _anticausal_conv(dpre, dpre_n, w, K_SSM).astype(BF16)
        rows = [jnp.sum(dpre * t, axis=0, keepdims=True) for t in taps]
        rows.append(jnp.zeros((8 - K_SSM, cb), F32))
        dw_ref[...] += jnp.concatenate(rows, axis=0)
        db_ref[...] += jnp.sum(dpre, axis=0, keepdims=True)

    wspec = pl.BlockSpec((8, cb), lambda j, i: (0, j))
    bspec = pl.BlockSpec((1, cb), lambda j, i: (0, j))
    return pl.pallas_call(
        body, name="ssm_conv_bwd", grid=(D_XBC // cb, nt),
        in_specs=[dcur, dnxt, cur, prev, nxt, wspec, bspec, ANY],
        out_specs=[pl.BlockSpec((tt, cb), lambda j, i: (i, OFF_XBC // cb + j)), wspec, bspec],
        out_shape=[jax.ShapeDtypeStruct(dproj.shape, BF16), jax.ShapeDtypeStruct((8, D_XBC), F32),
                   jax.ShapeDtypeStruct((1, D_XBC), F32)],
        input_output_aliases={7: 0},
        compiler_params=_cparams(("parallel", "arbitrary")),
    )(dact, dact, proj, proj, proj, w8, b, dproj)


SCB = 512
SC3 = 3 * SCB


def _sc_specs(tt, nt):
    hb = tt // HALO
    cur = pl.BlockSpec((tt, SC3), lambda j, i: (i, OFF_CB // SC3 + j))
    prev = pl.BlockSpec((HALO, SC3), lambda j, i: (jnp.maximum(i * hb - 1, 0), OFF_CB // SC3 + j))
    nxt = pl.BlockSpec((HALO, SC3), lambda j, i: (jnp.minimum((i + 1) * hb, nt * hb - 1), OFF_CB // SC3 + j))
    return cur, prev, nxt


def _shortconv_fwd(proj, w8, ymix):
    T = proj.shape[0]
    tt = _tile(T, 1024)
    nt = T // tt
    cur, prev, _ = _sc_specs(tt, nt)

    def body(p_ref, pp_ref, w_ref, y_ref, o_ref):
        p, pp = p_ref[...], pp_ref[...]
        v = p[:, SCB:2 * SCB] * p[:, 2 * SCB:]
        vp = jnp.where(pl.program_id(1) == 0, 0.0, pp[:, SCB:2 * SCB] * pp[:, 2 * SCB:])
        o_ref[...] = (p[:, :SCB] * _causal_conv(v, vp, w_ref[...], K_SC)).astype(BF16)

    return pl.pallas_call(
        body, name="shortconv_fwd", grid=(D_MODEL // SCB, nt),
        in_specs=[cur, prev, pl.BlockSpec((8, SCB), lambda j, i: (0, j)), ANY],
        out_specs=pl.BlockSpec((tt, SCB), lambda j, i: (i, D_SSM // SCB + j)),
        out_shape=jax.ShapeDtypeStruct(ymix.shape, BF16),
        input_output_aliases={3: 0},
        compiler_params=_cparams(("parallel", "parallel")),
    )(proj, proj, w8, ymix)


def _shortconv_bwd(dmix, proj, w8):
    T = proj.shape[0]
    tt = _tile(T, 1024)
    nt = T // tt
    hb = tt // HALO
    cur, prev, nxt = _sc_specs(tt, nt)
    d_s = pl.BlockSpec((tt, SCB), lambda j, i: (i, D_SSM // SCB + j))
    dn_s = pl.BlockSpec((HALO, SCB), lambda j, i: (jnp.minimum((i + 1) * hb, nt * hb - 1), D_SSM // SCB + j))

    def body(d_ref, dn_ref, p_ref, pp_ref, pn_ref, w_ref, dp_ref, dw_ref):
        i = pl.program_id(1)

        @pl.when(i == 0)
        def _():
            dw_ref[...] = jnp.zeros_like(dw_ref)

        w = w_ref[...]
        p, pp = p_ref[...], pp_ref[...]
        gb, gc, u = p[:, :SCB], p[:, SCB:2 * SCB], p[:, 2 * SCB:]
        v = gc * u
        vp = jnp.where(i == 0, 0.0, pp[:, SCB:2 * SCB] * pp[:, 2 * SCB:])
        d = d_ref[...].astype(F32)
        taps = _taps(v, vp, K_SC)
        dp_ref[:, :SCB] = (d * _conv_of_taps(taps, w)).astype(BF16)
        dcv = d * gb
        dcv_n = jnp.where(i == nt - 1, 0.0, dn_ref[...].astype(F32) * pn_ref[:, :SCB])
        dv = _anticausal_conv(dcv, dcv_n, w, K_SC)
        dp_ref[:, SCB:2 * SCB] = (dv * u).astype(BF16)
        dp_ref[:, 2 * SCB:] = (dv * gc).astype(BF16)
        rows = [jnp.sum(dcv * t, axis=0, keepdims=True) for t in taps]
        rows.append(jnp.zeros((8 - K_SC, SCB), F32))
        dw_ref[...] += jnp.concatenate(rows, axis=0)

    wspec = pl.BlockSpec((8, SCB), lambda j, i: (0, j))
    return pl.pallas_call(
        body, name="shortconv_bwd", grid=(D_MODEL // SCB, nt),
        in_specs=[d_s, dn_s, cur, prev, nxt, wspec],
        out_specs=[cur, wspec],
        out_shape=[jax.ShapeDtypeStruct((T, D_MAIN), BF16), jax.ShapeDtypeStruct((8, D_MODEL), F32)],
        compiler_params=_cparams(("parallel", "arbitrary")),
    )(dmix, dmix, proj, proj, proj, w8)


GW = HEADS_PER_GROUP * HEADDIM


def _dot(a, b):
    return jnp.dot(a.astype(BF16), b.astype(BF16), preferred_element_type=F32)


def _dot_nt(a, b):
    return lax.dot_general(a.astype(BF16), b.astype(BF16), (((1,), (1,)), ((), ())), preferred_element_type=F32)


def _dot_tn(a, b):
    return lax.dot_general(a.astype(BF16), b.astype(BF16), (((0,), (0,)), ((), ())), preferred_element_type=F32)


def _bf16_terms(x, n):
    terms, r = [], x
    for _ in range(n):
        t = r.astype(BF16)
        terms.append(t)
        r = r - t.astype(F32)
    return terms


def _dot_sel(a, sel, n=2):
    s = sel.astype(BF16)
    return sum(jnp.dot(t, s, preferred_element_type=F32) for t in _bf16_terms(a, n))


def _sel_dot(sel, b, n=2):
    s = sel.astype(BF16)
    return sum(jnp.dot(s, t, preferred_element_type=F32) for t in _bf16_terms(b, n))


def _sel_dot_nt(sel, b, n=2):
    s = sel.astype(BF16)
    return sum(lax.dot_general(s, t, (((1,), (1,)), ((), ())), preferred_element_type=F32)
               for t in _bf16_terms(b, n))


def _head_cols(rows):
    parts = [jnp.broadcast_to(rows[r:r + 1, :], (HEADDIM, CHUNK)) for r in range(HEADS_PER_GROUP)]
    return jnp.concatenate(parts, axis=0).T


def _head_rows(rows):
    parts = [jnp.broadcast_to(rows[r:r + 1, :], (HEADDIM, N_STATE)) for r in range(HEADS_PER_GROUP)]
    return jnp.concatenate(parts, axis=0)


def _ssd_common(dtr, bias, alog):
    dt = _softplus(dtr + bias)
    A = -jnp.exp(alog)
    a = dt * A
    ki = lax.broadcasted_iota(jnp.int32, (CHUNK, CHUNK), 0)
    si = lax.broadcasted_iota(jnp.int32, (CHUNK, CHUNK), 1)
    upper = (ki <= si).astype(F32)
    cs = _dot_sel(a, upper, 3)
    cs_last = jnp.broadcast_to(cs[:, CHUNK - 1:CHUNK], (8, CHUNK))
    return dt, A, a, cs, cs_last


def _decay_matrix(cs, r):
    li = lax.broadcasted_iota(jnp.int32, (CHUNK, CHUNK), 0)
    si = lax.broadcasted_iota(jnp.int32, (CHUNK, CHUNK), 1)
    causal = li >= si
    R = jnp.broadcast_to(cs[r:r + 1, :], (CHUNK, CHUNK))
    seg = jnp.where(causal, R.T - R, 0.0)
    return jnp.where(causal, jnp.exp(seg), 0.0)


def _decay_cat(cs):
    return jnp.concatenate([_decay_matrix(cs, r) for r in range(HEADS_PER_GROUP)], axis=1)


def _lanes4(m):
    return jnp.concatenate([m] * HEADS_PER_GROUP, axis=1)


def _head_blocks(v):
    col = lax.broadcasted_iota(jnp.int32, v.shape, 1) // HEADDIM
    return jnp.concatenate([jnp.where(col == r, v, jnp.zeros_like(v)) for r in range(HEADS_PER_GROUP)], axis=0)


GXBC = GW + 2 * N_STATE


GS_FWD = 8
GS_BWD = 8


def _ssd_in_specs(nc, rev):
    GS = GS_BWD if rev else GS_FWD
    cix = (lambda c: nc - 1 - c) if rev else (lambda c: c)
    x_s = pl.BlockSpec((CHUNK, GS * GW), lambda g, c: (cix(c), g))
    xbc_s = pl.BlockSpec((CHUNK, GS * GXBC), lambda g, c: (cix(c), g))
    dtr_s = pl.BlockSpec((GS, 8, CHUNK), lambda g, c: (g, 0, cix(c)))
    row_s = pl.BlockSpec((GS, 8, CHUNK), lambda g, c: (g, 0, 0))
    drep_s = pl.BlockSpec((1, GS * GW), lambda g, c: (0, g))
    hs_s = pl.BlockSpec((1, GS * GW, N_STATE), lambda g, c: (cix(c), g, 0))
    return x_s, xbc_s, dtr_s, row_s, drep_s, hs_s


def _xbc_parts(xbc_ref, gi):
    o = gi * GXBC
    return xbc_ref[:, o:o + GW], xbc_ref[:, o + GW:o + GW + N_STATE], xbc_ref[:, o + GW + N_STATE:o + GXBC]


def _ssd_fwd(xbc, dtr, bias, alog, drep):
    T = xbc.shape[0]
    nc = T // CHUNK
    x_s, xbc_s, dtr_s, row_s, drep_s, hs_s = _ssd_in_specs(nc, False)

    def body(xbc_ref, dtr_ref, bias_ref, alog_ref, drep_ref, y_ref, hs_ref, h_scr):
        @pl.when(pl.program_id(1) == 0)
        def _():
            h_scr[...] = jnp.zeros_like(h_scr)

        for gi in range(GS_FWD):
            cols, rows = slice(gi * GW, (gi + 1) * GW), pl.ds(gi * GW, GW)
            x, Bm, Cm = _xbc_parts(xbc_ref, gi)
            dt, A, a, cs, cs_last = _ssd_common(dtr_ref[gi], bias_ref[gi], alog_ref[gi])
            E = _head_cols(jnp.exp(cs))
            W = _head_cols(jnp.exp(cs_last - cs) * dt)
            X = (x * _head_cols(dt)).astype(BF16)
            CB = _dot_nt(Cm, Bm)
            col = lax.broadcasted_iota(jnp.int32, (CHUNK, GW), 1) // HEADDIM
            y = jnp.zeros((CHUNK, GW), F32)
            for r in range(HEADS_PER_GROUP):
                y = y + jnp.where(col == r, _dot(CB * _decay_matrix(cs, r), X), 0.0)
            h = h_scr[rows, :]
            hs_ref[0, rows, :] = h
            y = y + _dot_nt(Cm, h) * E
            y_ref[:, cols] = y + drep_ref[:, cols] * x
            h_scr[rows, :] = h * _head_rows(jnp.exp(cs_last)) + _dot_tn(x * W, Bm)

    return pl.pallas_call(
        body, name="ssd_fwd", grid=(N_GROUPS // GS_FWD, nc),
        in_specs=[xbc_s, dtr_s, row_s, row_s, drep_s],
        out_specs=[x_s, hs_s],
        out_shape=[jax.ShapeDtypeStruct((T, D_SSM), F32), jax.ShapeDtypeStruct((nc, D_SSM, N_STATE), F32)],
        scratch_shapes=[pltpu.VMEM((GS_FWD * GW, N_STATE), F32)],
        compiler_params=_cparams(("parallel", "arbitrary")),
    )(xbc, dtr, bias, alog, drep)


def _ssd_bwd(xbc, dtr, bias, alog, drep, dy, hs):
    T = xbc.shape[0]
    nc = T // CHUNK
    x_s, xbc_s, dtr_s, row_s, drep_s, hs_s = _ssd_in_specs(nc, True)

    def body(xbc_ref, dtr_ref, bias_ref, alog_ref, drep_ref, dy_ref, hs_ref,
             dxbc_ref, ddtr_ref, dbias_ref, dalog_ref, dd_ref, dh_scr):
        @pl.when(pl.program_id(1) == 0)
        def _():
            dh_scr[...] = jnp.zeros_like(dh_scr)
            dbias_ref[...] = jnp.zeros_like(dbias_ref)
            dalog_ref[...] = jnp.zeros_like(dalog_ref)
            dd_ref[...] = jnp.zeros_like(dd_ref)

        for gi in range(GS_BWD):
            one_group(gi, xbc_ref, dtr_ref, bias_ref, alog_ref, drep_ref, dy_ref, hs_ref,
                      dxbc_ref, ddtr_ref, dbias_ref, dalog_ref, dd_ref, dh_scr)

    def one_group(gi, xbc_ref, dtr_ref, bias_ref, alog_ref, drep_ref, dy_ref, hs_ref,
                  dxbc_ref, ddtr_ref, dbias_ref, dalog_ref, dd_ref, dh_scr):
        cols, rows, o = slice(gi * GW, (gi + 1) * GW), pl.ds(gi * GW, GW), gi * GXBC
        x, Bm, Cm = _xbc_parts(xbc_ref, gi)
        dY = dy_ref[:, cols]
        dt, A, a, cs, cs_last = _ssd_common(dtr_ref[gi], bias_ref[gi], alog_ref[gi])
        E = _head_cols(jnp.exp(cs))
        DT = _head_cols(dt)
        Wd = _head_cols(jnp.exp(cs_last - cs))
        X = x * DT
        h = hs_ref[0, rows, :]
        dS = dh_scr[rows, :]
        CB = _dot_nt(Cm, Bm)
        rowid = lax.broadcasted_iota(jnp.int32, (8, CHUNK), 0)
        lane = lax.broadcasted_iota(jnp.int32, (8, CHUNK), 1)
        hsel = (lax.broadcasted_iota(jnp.int32, (8, GW), 1) // HEADDIM
                == lax.broadcasted_iota(jnp.int32, (8, GW), 0)).astype(F32)
        hsel_l = (lax.broadcasted_iota(jnp.int32, (8, HEADS_PER_GROUP * CHUNK), 1) // CHUNK
                  == lax.broadcasted_iota(jnp.int32, (8, HEADS_PER_GROUP * CHUNK), 0)).astype(F32)

        Lc, CBc = _decay_cat(cs), _lanes4(CB)
        Mc = CBc * Lc
        GLc = _dot_nt(dY, _head_blocks(X.astype(BF16))) * Lc
        Wc = GLc * CBc
        colsum = jnp.sum(Wc, axis=0, keepdims=True)
        dcs = _sel_dot_nt(hsel_l, Wc)
        dCB = jnp.zeros((CHUNK, CHUNK), F32)
        for r in range(HEADS_PER_GROUP):
            blk = slice(r * CHUNK, (r + 1) * CHUNK)
            dCB = dCB + GLc[:, blk]
            dcs = dcs - jnp.where(rowid == r, colsum[:, blk], 0.0)
        m_stack = jnp.concatenate([Mc[:, r * CHUNK:(r + 1) * CHUNK].astype(BF16) for r in range(HEADS_PER_GROUP)],
                                  axis=0)
        dX = lax.dot_general(m_stack, _head_blocks(dY.astype(BF16)), (((0,), (0,)), ((), ())),
                             preferred_element_type=F32)
        dC = _dot(dCB, Bm)
        dB = _dot_tn(dCB, Cm)
        T1 = _dot_nt(Bm, dS)
        dX = dX + T1 * Wd
        dB = dB + _dot(X * Wd, dS)
        pdec = _sel_dot_nt(hsel, X * T1 * Wd)
        dcs = dcs - pdec
        dlast = jnp.sum(pdec, axis=1, keepdims=True) \
            + jnp.exp(cs_last[:, 0:1]) * jnp.sum(_sel_dot(hsel, dS * h), axis=1, keepdims=True)
        dYE = dY * E
        dC = dC + _dot(dYE, h)
        yoff = _dot_nt(Cm, h) * E
        dcs = dcs + _sel_dot_nt(hsel, dY * yoff)
        dcs = dcs + jnp.where(lane == CHUNK - 1, dlast, 0.0)
        ki = lax.broadcasted_iota(jnp.int32, (CHUNK, CHUNK), 0)
        si = lax.broadcasted_iota(jnp.int32, (CHUNK, CHUNK), 1)
        lower = (ki >= si).astype(F32)
        da = _dot_sel(dcs, lower)
        ddt = da * A + _sel_dot_nt(hsel, dX * x)
        ddtr = ddt * _sigmoid(dtr_ref[gi] + bias_ref[gi])
        ddtr_ref[gi] = ddtr
        dbias_ref[gi] += ddtr
        dalog_ref[gi] += da * a
        dxbc_ref[:, o:o + GW] = dX * DT + drep_ref[:, cols] * dY
        dd_ref[:, cols] += jnp.sum(dY * x, axis=0, keepdims=True)
        dxbc_ref[:, o + GW:o + GW + N_STATE] = dB
        dxbc_ref[:, o + GW + N_STATE:o + GXBC] = dC
        dh_scr[rows, :] = dS * _head_rows(jnp.exp(cs_last)) + _dot_tn(dYE, Cm)

    return pl.pallas_call(
        body, name="ssd_bwd", grid=(N_GROUPS // GS_BWD, nc),
        in_specs=[xbc_s, dtr_s, row_s, row_s, drep_s, x_s, hs_s],
        out_specs=[xbc_s, dtr_s, row_s, row_s, drep_s],
        out_shape=[jax.ShapeDtypeStruct((T, D_XBC), F32),
                   jax.ShapeDtypeStruct((N_GROUPS, 8, T), F32),
                   jax.ShapeDtypeStruct((N_GROUPS, 8, CHUNK), F32),
                   jax.ShapeDtypeStruct((N_GROUPS, 8, CHUNK), F32),
                   jax.ShapeDtypeStruct((1, D_SSM), F32)],
        scratch_shapes=[pltpu.VMEM((GS_BWD * GW, N_STATE), F32)],
        compiler_params=_cparams(("parallel", "arbitrary")),
    )(xbc, dtr, bias, alog, drep, dy, hs)


def _adamw(w, g, m, v, name, deps=(), emit_g=False):
    R, C = w.shape
    tr = _tile(R, 256, 8)
    nd = len(deps)
    nout = 4 if emit_g else 3

    def body(w_ref, g_ref, m_ref, v_ref, *rest):
        outs = rest[nd:]
        gv = g_ref[...]
        mn = ADAM_B1 * m_ref[...] + (1.0 - ADAM_B1) * gv
        vn = ADAM_B2 * v_ref[...] + (1.0 - ADAM_B2) * (gv * gv)
        m_hat = mn / (1.0 - ADAM_B1 ** ADAM_STEP)
        v_hat = vn / (1.0 - ADAM_B2 ** ADAM_STEP)
        outs[0][...] = -ADAM_LR * (m_hat / (jnp.sqrt(v_hat) + ADAM_EPS) + ADAM_WD * w_ref[...])
        outs[1][...] = mn
        outs[2][...] = vn
        if emit_g:
            outs[3][...] = gv

    spec = pl.BlockSpec((tr, C), lambda i: (i, 0))
    return pl.pallas_call(
        body, name=name, grid=(R // tr,),
        in_specs=[spec] * 4 + [ANY] * nd, out_specs=[spec] * nout,
        out_shape=[jax.ShapeDtypeStruct((R, C), F32)] * nout,
        compiler_params=_cparams(("parallel",)),
    )(w, g, m, v, *deps)


ANY = pl.BlockSpec(memory_space=pl.ANY)


def _place():
    x, y, c = lax.axis_index("x"), lax.axis_index("y"), lax.axis_index("c")
    return x, y, c


def _other_chips(x, y):
    return [(1 - x, y), (x, 1 - y), (1 - x, 1 - y)]


def _allgather_inplace(bufs, splits, first_done=False):
    n = len(bufs)

    def body(*refs):
        o_refs = refs[n:2 * n]
        send_sems, recv_sems = refs[2 * n:]
        x, y, c = _place()
        xn, yn, dg, sibling = (1 - x, y), (x, 1 - y), (1 - x, 1 - y), (x, y, 1 - c)

        def blk(k, chip, pc):
            return o_refs[k].at[4 * chip[0] + 2 * chip[1] + pc]

        def part(k, ref, p):
            kind, s = splits[k]
            _, R, C = bufs[k].shape
            if kind == "rows":
                return ref.at[pl.ds(0, s)] if p == 0 else ref.at[pl.ds(s, R - s)]
            return ref.at[:, pl.ds(0, s)] if p == 0 else ref.at[:, pl.ds(s, C - s)]

        def copy(k, slot, ref, to):
            return pltpu.make_async_remote_copy(
                src_ref=ref, dst_ref=ref, send_sem=send_sems.at[k, slot], recv_sem=recv_sems.at[k, slot],
                device_id=to, device_id_type=MESH)

        sent = []

        def send(k, slot, ref, to):
            cp = copy(k, slot, ref, to)
            cp.start()
            sent.append(cp)

        if not first_done:
            for k in range(n):
                send(k, 0, blk(k, (x, y), c), (*xn, c))
                send(k, 1, blk(k, (x, y), c), (*yn, c))
        for k in range(n):
            bx, by = blk(k, xn, c), blk(k, yn, c)
            if not first_done:
                copy(k, 0, bx, sibling).wait_recv()
            send(k, 2, part(k, bx, 0), (*yn, c))
            send(k, 4, bx, sibling)
            if not first_done:
                copy(k, 1, by, sibling).wait_recv()
            send(k, 3, part(k, by, 1), (*xn, c))
            send(k, 5, by, sibling)
        for k in range(n):
            d0, d1 = part(k, blk(k, dg, c), 0), part(k, blk(k, dg, c), 1)
            copy(k, 2, d0, sibling).wait_recv()
            send(k, 6, d0, sibling)
            copy(k, 3, d1, sibling).wait_recv()
            send(k, 7, d1, sibling)
        for k in range(n):
            copy(k, 4, blk(k, xn, 1 - c), sibling).wait_recv()
            copy(k, 5, blk(k, yn, 1 - c), sibling).wait_recv()
            copy(k, 6, part(k, blk(k, dg, 1 - c), 0), sibling).wait_recv()
            copy(k, 7, part(k, blk(k, dg, 1 - c), 1), sibling).wait_recv()
        for cp in sent:
            cp.wait_send()

    return pl.pallas_call(
        body, name="allgather_w_in",
        in_specs=[ANY] * n, out_specs=[ANY] * n,
        out_shape=[jax.ShapeDtypeStruct(b.shape, b.dtype) for b in bufs],
        input_output_aliases={k: k for k in range(n)},
        scratch_shapes=[pltpu.SemaphoreType.DMA((n, 8)), pltpu.SemaphoreType.DMA((n, 8))],
    )(*bufs)


HBM = pl.BlockSpec(memory_space=pltpu.HBM)
SEM = pl.BlockSpec(memory_space=pltpu.SEMAPHORE)
EFFECT = pltpu.SideEffectType.DATAFLOW_SIDE_EFFECTING


def _split_start(name, arrays, build, n_copies, after=()):
    na, nd = len(arrays), len(after)

    def body(*refs):
        send_sems, recv_sems = refs[na + nd], refs[na + nd + 1]
        for cp in build(refs[:na], send_sems, recv_sems):
            cp.start()
        refs[-1][...] = jnp.zeros((8, 128), F32)

    outs = pl.pallas_call(
        body, name=name,
        out_shape=(pltpu.SemaphoreType.DMA((n_copies,)), pltpu.SemaphoreType.DMA((n_copies,)),
                   *[pltpu.HBM(a.shape, a.dtype) for a in arrays], jax.ShapeDtypeStruct((8, 128), F32)),
        in_specs=[HBM] * na + [ANY] * nd,
        out_specs=(SEM, SEM, *[HBM] * na, pl.BlockSpec(memory_space=pltpu.VMEM)),
        input_output_aliases={i: 2 + i for i in range(na)},
        compiler_params=pltpu.CompilerParams(has_side_effects=EFFECT),
    )(*[pltpu.with_memory_space_constraint(a, pltpu.HBM) for a in arrays], *after)
    return outs[0], outs[1], list(outs[2:2 + na]), outs[-1]


def _split_wait(name, send_sems, recv_sems, arrays, build, after):
    na = len(arrays)

    def body(*refs):
        for cp in build(refs[:na], refs[na], refs[na + 1]):
            cp.wait_send()
            cp.wait_recv()

    outs = pl.pallas_call(
        body, name=name,
        out_shape=tuple(pltpu.HBM(a.shape, a.dtype) for a in arrays),
        in_specs=[HBM] * na + [SEM, SEM] + [ANY] * len(after),
        out_specs=tuple([HBM] * na),
        input_output_aliases={i: i for i in range(na)},
        compiler_params=pltpu.CompilerParams(has_side_effects=EFFECT),
    )(*arrays, send_sems, recv_sems, *after)
    return list(outs)


def _remote(src, dst, send_sems, recv_sems, i, to):
    return pltpu.make_async_remote_copy(src_ref=src, dst_ref=dst, send_sem=send_sems.at[i], recv_sem=recv_sems.at[i],
                                        device_id=to, device_id_type=MESH)


def _build_ag_first(refs, ss, rs):
    x, y, c = _place()
    cps = []
    for k, ref in enumerate(refs):
        blk = ref.at[4 * x + 2 * y + c]
        cps += [_remote(blk, blk, ss, rs, 2 * k, (1 - x, y, c)), _remote(blk, blk, ss, rs, 2 * k + 1, (x, 1 - y, c))]
    return cps


def _build_ag_ici(refs, ss, rs):
    x, y, c = _place()
    cps = []
    for k, ref in enumerate(refs):
        blk = ref.at[4 * x + 2 * y + c]
        for j, (px, py) in enumerate(_other_chips(x, y)):
            cps.append(_remote(blk, blk, ss, rs, 3 * k + j, (px, py, c)))
    return cps


def _build_ag_fwd(refs, ss, rs):
    x, y, c = _place()
    cps = []
    for k, ref in enumerate(refs):
        for j, (px, py) in enumerate(_other_chips(x, y)):
            blk = ref.at[4 * px + 2 * py + c]
            cps.append(_remote(blk, blk, ss, rs, 3 * k + j, (x, y, 1 - c)))
    return cps


def _build_rs_swap(refs, ss, rs):
    x, y, c = _place()
    n = len(refs) // 2
    return [_remote(refs[k].at[:, pl.ds(1 - c, 1)], refs[n + k], ss, rs, k, (x, y, 1 - c)) for k in range(n)]


def _build_rs_ici(refs, ss, rs):
    x, y, c = _place()
    n = len(refs) // 2
    me = 2 * x + y
    cps = []
    for k in range(n):
        for j, (px, py) in enumerate(_other_chips(x, y)):
            cps.append(_remote(refs[k].at[2 * px + py], refs[n + k].at[me], ss, rs, 3 * k + j, (px, py, c)))
    return cps


def _build_rs_share(refs, ss, rs):
    x, y, c = _place()
    return [_remote(ref.at[c], ref.at[c], ss, rs, k, (x, y, 1 - c)) for k, ref in enumerate(refs)]


def _build_small_gather(refs, ss, rs):
    x, y, c = _place()
    me = 4 * x + 2 * y + c
    cps = []
    for d in range(1, N_DEV):
        to = (1 - x if d & 4 else x, 1 - y if d & 2 else y, 1 - c if d & 1 else c)
        cps.append(_remote(refs[0], refs[1].at[me], ss, rs, d - 1, to))
    return cps


def _sum_gathered(mine, landed, me_arr):
    R, C = mine.shape

    def body(me_ref, m_ref, l_ref, o_ref):
        me = me_ref[0]
        s = None
        for d in range(N_DEV):
            t = jnp.where(me == d, m_ref[...], l_ref[d])
            s = t if s is None else s + t
        o_ref[...] = s

    grid_spec = pltpu.PrefetchScalarGridSpec(
        num_scalar_prefetch=1, grid=(1,),
        in_specs=[pl.BlockSpec((R, C), lambda i, me_ref: (0, 0)),
                  pl.BlockSpec((N_DEV, R, C), lambda i, me_ref: (0, 0, 0))],
        out_specs=pl.BlockSpec((R, C), lambda i, me_ref: (0, 0)))
    return pl.pallas_call(
        body, name="sum_small", grid_spec=grid_spec,
        out_shape=jax.ShapeDtypeStruct((R, C), F32),
        compiler_params=_cparams(("arbitrary",)),
    )(me_arr, mine, landed)


def _rs_add_pair(p, r0, c_arr, name):
    _, _, hr, cols = p.shape
    tr = _tile(hr, 256, 8)

    def body(c_ref, p_ref, r_ref, q_ref):
        q_ref[...] = (p_ref[0].astype(F32) + r_ref[0].astype(F32)).astype(BF16)

    grid_spec = pltpu.PrefetchScalarGridSpec(
        num_scalar_prefetch=1, grid=(N_CHIPS, hr // tr),
        in_specs=[pl.BlockSpec((1, 1, tr, cols), lambda j, i, c_ref: (j, c_ref[0], i, 0)),
                  pl.BlockSpec((1, 1, tr, cols), lambda j, i, c_ref: (j, 0, i, 0))],
        out_specs=pl.BlockSpec((1, tr, cols), lambda j, i, c_ref: (j, i, 0)))
    return pl.pallas_call(
        body, name=name, grid_spec=grid_spec,
        out_shape=jax.ShapeDtypeStruct((N_CHIPS, hr, cols), BF16),
        compiler_params=_cparams(("parallel", "parallel")),
    )(c_arr, p, r0)


def _rs_add_chips(r1, q, place_arr, name):
    _, hr, cols = r1.shape
    tr = _tile(hr, 256, 8)

    def body(place_ref, r_ref, q_ref, o_ref):
        chip = place_ref[0]
        s = None
        for j in range(N_CHIPS):
            t = jnp.where(chip == j, q_ref[j], r_ref[j]).astype(F32)
            s = t if s is None else s + t
        o_ref[...] = s

    blk = pl.BlockSpec((N_CHIPS, tr, cols), lambda i, place_ref: (0, i, 0))
    grid_spec = pltpu.PrefetchScalarGridSpec(
        num_scalar_prefetch=1, grid=(hr // tr,), in_specs=[blk, blk],
        out_specs=pl.BlockSpec((None, tr, cols), lambda i, place_ref: (place_ref[1], i, 0)))
    return pl.pallas_call(
        body, name=name, grid_spec=grid_spec,
        out_shape=jax.ShapeDtypeStruct((2, hr, cols), F32),
        compiler_params=_cparams(("parallel",)),
    )(place_arr, r1, q)


def _pad_rows(a, rows):
    return jnp.pad(a, ((0, rows - a.shape[0]), (0, 0)))


def _pad_cols(a, cols):
    return jnp.pad(a, ((0, 0), (0, cols - a.shape[1])))


def _heads_to_rows(v):
    v = v.reshape(N_GROUPS, HEADS_PER_GROUP, 1)
    v = jnp.pad(v, ((0, 0), (0, 8 - HEADS_PER_GROUP), (0, 0)))
    return jnp.broadcast_to(v, (N_GROUPS, 8, CHUNK))


def _rows_to_heads(a):
    return jnp.sum(a[:, :HEADS_PER_GROUP, :], axis=-1).reshape(N_HEADS)


def _to_kernel_rows(a):
    C = a.shape[1]
    x0, b0, c0, s0 = D_SSM, 2 * D_SSM, 2 * D_SSM + 1024, D_SSM + D_XBC + N_HEADS
    xbc = jnp.concatenate([a[x0:b0].reshape(N_GROUPS, GW, C), a[b0:c0].reshape(N_GROUPS, N_STATE, C),
                           a[c0:c0 + 1024].reshape(N_GROUPS, N_STATE, C)], axis=1).reshape(D_XBC, C)
    sc = jnp.concatenate([a[s0 + k * D_MODEL:s0 + (k + 1) * D_MODEL].reshape(D_MODEL // SCB, SCB, C)
                          for k in range(3)], axis=1).reshape(3 * D_MODEL, C)
    return jnp.concatenate([a[:D_SSM], xbc, sc], axis=0)


HR_IN = 1568


def _kernel_segments():
    segs = [(0, 0, 0, D_SSM)]
    for g in range(N_GROUPS):
        k0 = D_SSM + g * GXBC
        segs += [(0, k0, D_SSM + g * GW, GW), (0, k0 + GW, 2 * D_SSM + g * N_STATE, N_STATE),
                 (0, k0 + GW + N_STATE, 2 * D_SSM + 1024 + g * N_STATE, N_STATE)]
    segs.append((1, 0, D_SSM + D_XBC, N_HEADS))
    for j in range(D_MODEL // SCB):
        for k in range(3):
            segs.append((0, D_SSM + D_XBC + j * SC3 + k * SCB, D_SSM + D_XBC + N_HEADS + k * D_MODEL + j * SCB, SCB))
    return segs


def _shard_row_plan():
    cs = D_IN // N_CHIPS
    plan = []
    for src, s, o, n in _kernel_segments():
        while n > 0:
            chip, loc = divmod(o, cs)
            half, row = divmod(loc, HR_IN)
            m = min(n, cs - loc, HR_IN - row)
            plan.append((src, s, chip, half, row, m))
            s, o, n = s + m, o + m, n - m
    return plan


SCATTER_ROWS = 512
SCATTER_SLOTS = 4


def _scatter_rows_to_shards(k_main, k_dt):
    C = k_main.shape[1]
    pieces = []
    for src, s, chip, half, row, n in _shard_row_plan():
        for o in range(0, n, SCATTER_ROWS):
            pieces.append((src, s + o, chip, half, row + o, min(SCATTER_ROWS, n - o)))
    S, lag, N = SCATTER_SLOTS, SCATTER_SLOTS // 2, len(pieces)

    def body(m_ref, d_ref, o_ref, buf, in_sems, out_sems):
        def cin(i):
            src, s, _, _, _, n = pieces[i]
            return pltpu.make_async_copy((d_ref if src else m_ref).at[pl.ds(s, n)],
                                         buf.at[i % S, pl.ds(0, n)], in_sems.at[i % S])

        def cout(i):
            _, _, chip, half, row, n = pieces[i]
            return pltpu.make_async_copy(buf.at[i % S, pl.ds(0, n)],
                                         o_ref.at[chip, half, pl.ds(row, n)], out_sems.at[i % S])

        for i in range(N + lag):
            if i < N:
                if i >= S:
                    cout(i - S).wait()
                cin(i).start()
            j = i - lag
            if 0 <= j < N:
                cin(j).wait()
                cout(j).start()
        for j in range(max(0, N - S), N):
            cout(j).wait()

    return pl.pallas_call(
        body, name="scatter_dw_in_rows", in_specs=[ANY, ANY], out_specs=ANY,
        out_shape=jax.ShapeDtypeStruct((N_CHIPS, 2, HR_IN, C), k_main.dtype),
        scratch_shapes=[pltpu.VMEM((S, SCATTER_ROWS, C), k_main.dtype),
                        pltpu.SemaphoreType.DMA((S,)), pltpu.SemaphoreType.DMA((S,))],
        compiler_params=_cparams(),
    )(k_main, k_dt)


ROWS_IN = D_IN // N_CHIPS
ROWS_IN_PAD = ROWS_IN + 8


def _cast_w_in_into_gather(wt32, chip_arr):
    R, C = wt32.shape
    hc, cbk = C // 2, 256

    def body(chip_ref, w_ref, o_ref):
        y = jnp.concatenate([w_ref[...], jnp.zeros((ROWS_IN_PAD - R, cbk), F32)], axis=0)
        odd = chip_ref[0] % 2 == 1
        o_ref[...] = jnp.where(odd, pltpu.roll(y, ROWS_IN_PAD - R, axis=0), y).astype(BF16)

    grid_spec = pltpu.PrefetchScalarGridSpec(
        num_scalar_prefetch=1, grid=(2, hc // cbk),
        in_specs=[pl.BlockSpec((R, cbk), lambda h, s, chip_ref: (0, h * (hc // cbk) + s))],
        out_specs=pl.BlockSpec((None, ROWS_IN_PAD, cbk), lambda h, s, chip_ref: (2 * chip_ref[0] + h, 0, s)))
    return pl.pallas_call(
        body, name="cast_w_in", grid_spec=grid_spec,
        out_shape=jax.ShapeDtypeStruct((N_DEV, ROWS_IN_PAD, hc), BF16),
        compiler_params=_cparams(("parallel", "parallel")),
    )(chip_arr, wt32)


def _gather_row_plan():
    segs = [(s, o, n) for src, s, o, n in _kernel_segments() if src == 0]
    plan, merges = [], []
    for k, o, n in segs:
        while n > 0:
            chip, loc = divmod(o, ROWS_IN)
            m = min(n, ROWS_IN - loc)
            ps, kd, cnt = loc + 8 * (chip % 2), k, m
            if ps % 16:
                ps, kd, cnt = ps - 8, kd - 8, cnt + 8
            if (ps + cnt) % 16:
                cnt -= 8
                merges.append((kd + cnt, chip, chip + 1))
            if cnt:
                plan.append((chip, ps, kd, cnt))
            k, o, n = k + m, o + m, n - m
    return plan, merges


def _gather_to_kernel_rows(g):
    hc = g.shape[2]
    plan, merges = _gather_row_plan()
    pieces = []
    for chip, ps, kd, n in plan:
        for o in range(0, n, SCATTER_ROWS):
            pieces.append((chip, ps + o, kd + o, min(SCATTER_ROWS, n - o)))
    S, lag, N = SCATTER_SLOTS, SCATTER_SLOTS // 2, len(pieces)

    def body(g_ref, o_ref, buf, mbuf, in_sems, out_sems, m_sems):
        def cins(i):
            chip, ps, _, n = pieces[i]
            return [pltpu.make_async_copy(g_ref.at[2 * chip + h, pl.ds(ps, n)],
                                          buf.at[i % S, pl.ds(0, n), pl.ds(h * hc, hc)], in_sems.at[i % S, h])
                    for h in range(2)]

        def cout(i):
            _, _, kd, n = pieces[i]
            return pltpu.make_async_copy(buf.at[i % S, pl.ds(0, n)], o_ref.at[pl.ds(kd, n)], out_sems.at[i % S])

        for i in range(N + lag):
            if i < N:
                if i >= S:
                    cout(i - S).wait()
                for cp in cins(i):
                    cp.start()
            j = i - lag
            if 0 <= j < N:
                for cp in cins(j):
                    cp.wait()
                cout(j).start()
        for j in range(max(0, N - S), N):
            cout(j).wait()
        for t, (kd, ce, co) in enumerate(merges):
            loads = []
            for h in range(2):
                loads.append(pltpu.make_async_copy(g_ref.at[2 * ce + h, pl.ds(ROWS_IN - 8, 16)],
                                                   mbuf.at[0, :, pl.ds(h * hc, hc)], m_sems.at[2 * h]))
                loads.append(pltpu.make_async_copy(g_ref.at[2 * co + h, pl.ds(0, 16)],
                                                   mbuf.at[1, :, pl.ds(h * hc, hc)], m_sems.at[2 * h + 1]))
            for cp in loads:
                cp.start()
            for cp in loads:
                cp.wait()
            row = lax.broadcasted_iota(jnp.int32, (16, 2 * hc), 0)
            mbuf[2] = jnp.where(row < 8, mbuf[0].astype(F32), mbuf[1].astype(F32)).astype(g.dtype)
            st = pltpu.make_async_copy(mbuf.at[2], o_ref.at[pl.ds(kd, 16)], m_sems.at[4])
            st.start()
            st.wait()

    return pl.pallas_call(
        body, name="w_in_to_kernel_rows", in_specs=[ANY], out_specs=ANY,
        out_shape=jax.ShapeDtypeStruct((D_MAIN, 2 * hc), g.dtype),
        scratch_shapes=[pltpu.VMEM((S, SCATTER_ROWS, 2 * hc), g.dtype), pltpu.VMEM((3, 16, 2 * hc), g.dtype),
                        pltpu.SemaphoreType.DMA((S, 2)), pltpu.SemaphoreType.DMA((S,)),
                        pltpu.SemaphoreType.DMA((5,))],
        compiler_params=_cparams(),
    )(g)


def _to_kernel_xbc(a):
    R = a.shape[0]
    return jnp.concatenate([a[:, :D_SSM].reshape(R, N_GROUPS, GW), a[:, D_SSM:D_SSM + 1024].reshape(R, N_GROUPS, N_STATE),
                            a[:, D_SSM + 1024:].reshape(R, N_GROUPS, N_STATE)], axis=2).reshape(R, D_XBC)


def _from_kernel_xbc(a):
    R = a.shape[0]
    g = a.reshape(R, N_GROUPS, GXBC)
    return jnp.concatenate([g[:, :, :GW].reshape(R, D_SSM), g[:, :, GW:GW + N_STATE].reshape(R, 1024),
                            g[:, :, GW + N_STATE:].reshape(R, 1024)], axis=1)


def kernel(x, norm_mix_g, w_in, ssm_conv_w, ssm_conv_b, ssm_dt_bias, ssm_A_log, ssm_D, ssm_norm_g, sc_conv_w, w_out, norm_ffn_g, w_gate, w_up, w_down, norm_final_g, loss_target, m_norm_mix_g, m_w_in, m_ssm_conv_w, m_ssm_conv_b, m_ssm_dt_bias, m_ssm_A_log, m_ssm_D, m_ssm_norm_g, m_sc_conv_w, m_w_out, m_norm_ffn_g, m_w_gate, m_w_up, m_w_down, m_norm_final_g, v_norm_mix_g, v_w_in, v_ssm_conv_w, v_ssm_conv_b, v_ssm_dt_bias, v_ssm_A_log, v_ssm_D, v_ssm_norm_g, v_sc_conv_w, v_w_out, v_norm_ffn_g, v_w_gate, v_w_up, v_w_down, v_norm_final_g):
    T = x.shape[1]
    xt = x[0]
    tgt = loss_target[0]
    cx, cy, cc = lax.axis_index("x"), lax.axis_index("y"), lax.axis_index("c")
    chip = 2 * cx + cy
    c_arr = jnp.reshape(cc, (1,)).astype(jnp.int32)
    chip_arr = jnp.reshape(chip, (1,)).astype(jnp.int32)
    place_arr = jnp.stack([chip, cc]).astype(jnp.int32)

    big = [w_in[0].T, w_out[0], w_gate[0], w_up[0], w_down[0]]
    names = ["w_in", "w_out", "w_gate", "w_up", "w_down"]
    gb_in = _cast_w_in_into_gather(big[0], chip_arr)
    cs_in, cs_conv = D_IN // N_CHIPS, D_XBC // N_CHIPS
    cw = jnp.stack([_pad_rows(ssm_conv_w[0], 8), _pad_cols(_pad_rows(sc_conv_w[0], 8), cs_conv)])
    cw_buf = lax.dynamic_update_slice(jnp.zeros((N_DEV, 8, cs_conv), F32), cw, (2 * chip, 0, 0))
    f_ss, f_rs, f_arr, f_tok = _split_start("ag_in_first_start", [gb_in, cw_buf], _build_ag_first, 4)
    gbufs = [None] + [_cast_into_gather(w, chip_arr, "cast_" + nm, deps=[f_tok]) for w, nm in zip(big[1:], names[1:])]
    n1 = _rmsnorm_fwd(xt, _tie(norm_mix_g, f_tok, "tie_ag_first"), "rmsnorm_mix")
    f_arr = _split_wait("ag_in_first_wait", f_ss, f_rs, f_arr, _build_ag_first, after=gbufs[1:] + [n1])
    g_in, cw_all = _allgather_inplace(f_arr, [("rows", (ROWS_IN_PAD // 32) * 16), ("cols", cs_conv // 2)],
                                      first_done=True)
    cw_all = cw_all.reshape(N_CHIPS, 2, 8, cs_conv)
    ssm_w8 = _to_kernel_xbc(cw_all[:, 0].transpose(1, 0, 2).reshape(8, D_XBC))
    sc_w8 = cw_all[:, 1, :, :D_MODEL // N_CHIPS].transpose(1, 0, 2).reshape(8, D_MODEL)
    ssm_bk = _to_kernel_xbc(ssm_conv_b)
    wt_main = _gather_to_kernel_rows(g_in)
    dt_rows = [jnp.concatenate([g_in[2 * ch, r0:r0 + 16], g_in[2 * ch + 1, r0:r0 + 16]], axis=1)
               for ch, r0 in ((1, ROWS_IN_PAD - 16), (2, 0))]
    wt_dt = _pad_rows(jnp.concatenate(dt_rows, axis=0), DT_PAD)
    ag_ss, ag_rs, ag_bufs, ag_tok = _split_start("ag_ici_start", gbufs[1:], _build_ag_ici, 12, after=[g_in, cw_all])

    bias_rows = _heads_to_rows(ssm_dt_bias[0])
    alog_rows = _heads_to_rows(ssm_A_log[0])
    drep = jnp.repeat(ssm_D[0], HEADDIM).reshape(1, D_SSM)

    (proj,) = _matmul([(n1, wt_main)], tb=True, out_dtypes=[F32], name="mm_proj", deps=[ag_tok])
    (dt_raw,) = _matmul([(n1, wt_dt)], tb=True, out_dtypes=[F32], name="mm_proj_dt")
    xbc = _ssm_conv_fwd(proj, ssm_w8, ssm_bk)
    dtr = jnp.pad(dt_raw[:, :N_HEADS].T.reshape(N_GROUPS, HEADS_PER_GROUP, T), ((0, 0), (0, 4), (0, 0)))
    y_ssd, hs = _ssd_fwd(xbc, dtr, bias_rows, alog_rows, drep)
    ag_bufs = _split_wait("ag_ici_wait", ag_ss, ag_rs, ag_bufs, _build_ag_ici, after=[y_ssd])
    fw_ss, fw_rs, fw_bufs, fw_tok = _split_start("ag_fwd_start", ag_bufs, _build_ag_fwd, 12)
    y_mix = _shortconv_fwd(proj, sc_w8, _gated_norm_fwd(y_ssd, proj, _tie(ssm_norm_g, fw_tok, "tie_ag_fwd")))
    gath = _split_wait("ag_fwd_wait", fw_ss, fw_rs, fw_bufs, _build_ag_fwd, after=[y_mix])
    w_out_f = gath[0].reshape(2 * D_MODEL, D_MODEL)
    w_gate3 = gath[1].reshape(N_CHIPS, D_MODEL, D_FF // N_CHIPS)
    w_up3 = gath[2].reshape(N_CHIPS, D_MODEL, D_FF // N_CHIPS)
    w_down_f = gath[3].reshape(D_FF, D_MODEL)
    (h1,) = _matmul([(y_mix, w_out_f)], out_dtypes=[F32], name="mm_out", extras=[xt],
                    epilogue=lambda acc, res: (acc + res,))
    n2 = _rmsnorm_fwd(h1, norm_ffn_g, "rmsnorm_ffn")
    g_act, u_act, a_act = _ffn_fwd(n2, w_gate3, w_up3)
    (h2,) = _matmul([(a_act, w_down_f)], out_dtypes=[F32], name="mm_down", extras=[h1],
                    epilogue=lambda acc, res: (acc + res,))

    dh2, dh2b, dg_final, loss_part = _loss_and_final_bwd(h2, tgt, norm_final_g.reshape(1, D_MODEL))
    dg_act, du_act = _matmul([(dh2b, w_down_f)], tb=True, out_dtypes=[BF16, BF16], name="mm_down_bwd",
                             tn=512, extras=[g_act, u_act], epilogue=_swiglu_bwd, nsub=2)
    (dw_down,) = _matmul([(a_act, dh2b)], ta=True, out_dtypes=[BF16], name="mm_dw_down", tm=1408, tn=512)
    (dn2,) = _matmul([(dg_act, w_gate3), (du_act, w_up3)], tb=True, b3d=True, out_dtypes=[BF16],
                     name="mm_ffn_in_bwd")
    (dw_gate,) = _matmul([(n2, dg_act)], ta=True, out_dtypes=[BF16], name="mm_dw_gate", tm=512, tn=1408,
                         col_shards=True)
    (dw_up,) = _matmul([(n2, du_act)], ta=True, out_dtypes=[BF16], name="mm_dw_up", tm=512, tn=1408,
                       col_shards=True)
    dh1, dh1b, dg_ffn = _rmsnorm_bwd(dn2, h1, norm_ffn_g, dh2, "rmsnorm_ffn_bwd")
    (dw_out,) = _matmul([(y_mix, dh1b)], ta=True, out_dtypes=[BF16], name="mm_dw_out")

    def halves(g):
        return g.reshape(N_CHIPS, 2, g.shape[1] // 2, g.shape[2])

    def landing(shape, dtype):
        return lax.empty(shape, dtype)

    names1 = names[1:]
    ps1 = [halves(dw_out.reshape(N_CHIPS, -1, D_MODEL)), halves(dw_gate), halves(dw_up),
           halves(dw_down.reshape(N_CHIPS, -1, D_MODEL))]
    r0_1 = [landing((N_CHIPS, 1) + p.shape[2:], p.dtype) for p in ps1]
    sw_ss, sw_rs, sw_arr, sw_tok = _split_start("rs1_swap_start", ps1 + r0_1, _build_rs_swap, 4)
    (dmix,) = _matmul([(dh1b, w_out_f)], tb=True, out_dtypes=[BF16], name="mm_out_bwd", deps=[sw_tok])
    dproj, dw_sc = _shortconv_bwd(dmix, proj, sc_w8)
    dy_ssd, dproj, dg_ssmnorm = _gated_norm_bwd(dmix, y_ssd, proj, ssm_norm_g, dproj)
    sw_arr = _split_wait("rs1_swap_wait", sw_ss, sw_rs, sw_arr, _build_rs_swap, after=[dy_ssd])
    qs1 = [_rs_add_pair(p, r, c_arr, "rs_add_pair_" + nm) for p, r, nm in zip(sw_arr[:4], sw_arr[4:], names1)]
    r1_1 = [landing(q.shape, BF16) for q in qs1]
    ic_ss, ic_rs, ic_arr, ic_tok = _split_start("rs1_ici_start", qs1 + r1_1, _build_rs_ici, 12)
    dxbc_act, ddtr, dbias_acc, dalog_acc, dD_acc = _ssd_bwd(
        xbc, dtr, bias_rows, alog_rows, _tie(drep, ic_tok, "tie_rs1_ici"), dy_ssd, hs)
    dproj, dw_ssmconv, db_ssmconv = _ssm_conv_bwd(dxbc_act, proj, ssm_w8, ssm_bk, dproj)
    dw_ssmconv, db_ssmconv = _from_kernel_xbc(dw_ssmconv), _from_kernel_xbc(db_ssmconv)
    ddt_raw = _pad_cols(ddtr[:, :HEADS_PER_GROUP, :].reshape(N_HEADS, T).T, DT_PAD).astype(BF16)
    (dwt_main,) = _matmul([(dproj, n1)], ta=True, out_dtypes=[F32], name="mm_dw_main")
    (dwt_dt,) = _matmul([(ddt_raw, n1)], ta=True, out_dtypes=[F32], name="mm_dw_dt")
    ic_arr = _split_wait("rs1_ici_wait", ic_ss, ic_rs, ic_arr, _build_rs_ici, after=[dwt_main])
    g1 = [_rs_add_chips(r, q, place_arr, "rs_add_chips_" + nm) for q, r, nm in zip(ic_arr[:4], ic_arr[4:], names1)]
    sh_ss, sh_rs, sh_arr, sh_tok = _split_start("rs1_share_start", g1, _build_rs_share, 4)
    p_in = _scatter_rows_to_shards(dwt_main, dwt_dt)
    s2_ss, s2_rs, s2_arr, s2_tok = _split_start(
        "rs2_swap_start", [p_in, landing((N_CHIPS, 1) + p_in.shape[2:], F32)], _build_rs_swap, 1)
    tm_pb = 1024
    mt = T // _tile(T, tm_pb)
    mt_a = max(mt // 4, 1)
    (dn1a,) = _matmul([(dproj, wt_main)], out_dtypes=[F32], name="mm_proj_bwd_a", deps=[s2_tok], tm=tm_pb,
                      m_tiles=(0, mt_a))
    s2_arr = _split_wait("rs2_swap_wait", s2_ss, s2_rs, s2_arr, _build_rs_swap, after=[dn1a])
    q_in = _rs_add_pair(s2_arr[0], s2_arr[1], c_arr, "rs_add_pair_w_in")
    i2_ss, i2_rs, i2_arr, i2_tok = _split_start(
        "rs2_ici_start", [q_in, landing(q_in.shape, BF16)], _build_rs_ici, 3)
    if mt > mt_a:
        (dn1a,) = _matmul([(dproj, wt_main)], out_dtypes=[F32], name="mm_proj_bwd_b", deps=[i2_tok], tm=tm_pb,
                          m_tiles=(mt_a, mt - mt_a), out_buf=dn1a)
    (dn1,) = _matmul([(ddt_raw, wt_dt)], out_dtypes=[BF16], name="mm_proj_dt_bwd", extras=[dn1a],
                     epilogue=lambda acc, res: (acc + res,), deps=[i2_tok])
    dx, _, dg_mix = _rmsnorm_bwd(dn1, xt, norm_mix_g, dh1, "rmsnorm_mix_bwd")
    g1 = _split_wait("rs1_share_wait", sh_ss, sh_rs, sh_arr, _build_rs_share, after=[dx])

    big_m = [m_w_in[0].T, m_w_out[0], m_w_gate[0], m_w_up[0], m_w_down[0]]
    big_v = [v_w_in[0].T, v_w_out[0], v_w_gate[0], v_w_up[0], v_w_down[0]]
    big_grads = [None] + [g.reshape(w.shape) for g, w in zip(g1, big[1:])]
    big_out = {}
    for k in range(1, 5):
        *big_out[names[k]], big_grads[k] = _adamw(big[k], big_grads[k], big_m[k], big_v[k], "adamw_" + names[k],
                                                   deps=[i2_tok], emit_g=True)
    i2_arr = _split_wait("rs2_ici_wait", i2_ss, i2_rs, i2_arr, _build_rs_ici, after=[big_out[names[4]][0], dx])
    g_in_red = _rs_add_chips(i2_arr[1], i2_arr[0], place_arr, "rs_add_chips_w_in")
    s3_ss, s3_rs, s3_arr, s3_tok = _split_start("rs2_share_start", [g_in_red], _build_rs_share, 1)

    dD = jnp.sum(dD_acc.reshape(N_HEADS, HEADDIM), axis=-1)
    heads_row = jnp.concatenate([_rows_to_heads(dbias_acc), _rows_to_heads(dalog_acc), dD,
                                 loss_part.reshape(1)]).reshape(1, -1)
    small = jnp.concatenate([
        dw_ssmconv,
        _pad_cols(dw_sc, D_XBC),
        db_ssmconv,
        jnp.concatenate([dg_mix, dg_ssmnorm], axis=1),
        jnp.concatenate([dg_ffn, dg_final], axis=1),
        _pad_cols(heads_row, D_XBC),
        jnp.zeros((4, D_XBC), F32),
    ], axis=0)
    sm_ss, sm_rs, sm_arr, sm_tok = _split_start(
        "small_gather_start", [small, landing((N_DEV,) + small.shape, F32)], _build_small_gather, N_DEV - 1,
        after=[s3_tok])
    (g_in_full,) = _split_wait("rs2_share_wait", s3_ss, s3_rs, s3_arr, _build_rs_share, after=[sm_tok])
    d_t, m_t, v_t, g_t = _adamw(big[0], g_in_full.reshape(2 * HR_IN, D_MODEL), big_m[0], big_v[0],
                                "adamw_" + names[0], emit_g=True)
    big_grads[0] = g_t.T
    big_out[names[0]] = (d_t.T, m_t.T, v_t.T)
    sm_arr = _split_wait("small_gather_wait", sm_ss, sm_rs, sm_arr, _build_small_gather, after=[d_t])
    tot = _sum_gathered(sm_arr[0], sm_arr[1], jnp.reshape(4 * cx + 2 * cy + cc, (1,)).astype(jnp.int32))
    loss = tot[19, 3 * N_HEADS]

    cs_ssm, cs_sc = D_XBC // N_CHIPS, D_MODEL // N_CHIPS
    g_ssm_conv = lax.dynamic_slice(tot[0:K_SSM], (0, chip * cs_ssm), (K_SSM, cs_ssm))
    g_sc_conv = lax.dynamic_slice(tot[8:8 + K_SC, :D_MODEL], (0, chip * cs_sc), (K_SC, cs_sc))
    small_grads = {
        "norm_mix_g": tot[17:18, :D_MODEL], "ssm_conv_w": g_ssm_conv, "ssm_conv_b": tot[16:17],
        "ssm_dt_bias": tot[19:20, 0:N_HEADS], "ssm_A_log": tot[19:20, N_HEADS:2 * N_HEADS],
        "ssm_D": tot[19:20, 2 * N_HEADS:3 * N_HEADS], "ssm_norm_g": tot[17:18, D_MODEL:],
        "sc_conv_w": g_sc_conv, "norm_ffn_g": tot[18:19, :D_MODEL], "norm_final_g": tot[18:19, D_MODEL:],
    }
    small_w = {"norm_mix_g": (norm_mix_g, m_norm_mix_g, v_norm_mix_g),
               "ssm_conv_w": (ssm_conv_w[0], m_ssm_conv_w[0], v_ssm_conv_w[0]),
               "ssm_conv_b": (ssm_conv_b, m_ssm_conv_b, v_ssm_conv_b),
               "ssm_dt_bias": (ssm_dt_bias, m_ssm_dt_bias, v_ssm_dt_bias),
               "ssm_A_log": (ssm_A_log, m_ssm_A_log, v_ssm_A_log),
               "ssm_D": (ssm_D, m_ssm_D, v_ssm_D),
               "ssm_norm_g": (ssm_norm_g, m_ssm_norm_g, v_ssm_norm_g),
               "sc_conv_w": (sc_conv_w[0], m_sc_conv_w[0], v_sc_conv_w[0]),
               "norm_ffn_g": (norm_ffn_g, m_norm_ffn_g, v_norm_ffn_g),
               "norm_final_g": (norm_final_g.reshape(1, -1), m_norm_final_g.reshape(1, -1),
                                v_norm_final_g.reshape(1, -1))}
    PW = 1024
    order = list(small_w)

    def pack(arrs):
        rows = []
        for a in arrs:
            flat = a.reshape(-1)
            n = -(-flat.shape[0] // PW) * PW
            rows.append(jnp.pad(flat, (0, n - flat.shape[0])).reshape(-1, PW))
        slab = jnp.concatenate(rows, axis=0)
        return _pad_rows(slab, -(-slab.shape[0] // 8) * 8)

    wp = pack([small_w[k][0] for k in order])
    mp = pack([small_w[k][1] for k in order])
    vp = pack([small_w[k][2] for k in order])
    gp = pack([small_grads[k] for k in order])
    sd, sm, sv = _adamw(wp, gp, mp, vp, "adamw_small")

    def unpack(slab):
        out, row = {}, 0
        for k in order:
            shape = small_w[k][0].shape
            size = 1
            for s in shape:
                size *= s
            nr = -(-size // PW)
            out[k] = slab[row:row + nr].reshape(-1)[:size].reshape(shape)
            row += nr
        return out

    s_delta, s_m, s_v = unpack(sd), unpack(sm), unpack(sv)

    big_g = dict(zip(names, big_grads))

    weight_order = ["norm_mix_g", "w_in", "ssm_conv_w", "ssm_conv_b", "ssm_dt_bias", "ssm_A_log", "ssm_D",
                    "ssm_norm_g", "sc_conv_w", "w_out", "norm_ffn_g", "w_gate", "w_up", "w_down", "norm_final_g"]
    lead = {"ssm_conv_w", "sc_conv_w", "w_in", "w_out", "w_gate", "w_up", "w_down"}

    def shaped(nm, a):
        if nm == "norm_final_g":
            return a.reshape(D_MODEL)
        return a[None] if nm in lead else a

    grads, deltas, new_m, new_v = [], [], [], []
    for nm in weight_order:
        if nm in big_out:
            g, (d, m, v) = big_g[nm], big_out[nm]
        else:
            g, d, m, v = small_grads[nm], s_delta[nm], s_m[nm], s_v[nm]
        grads.append(shaped(nm, g))
        deltas.append(shaped(nm, d))
        new_m.append(shaped(nm, m))
        new_v.append(shaped(nm, v))
    return (loss, dx[None], *grads, *deltas, *new_m, *new_v)


def _swiglu_bwd(da, dg_factor, du_factor):
    return da * dg_factor.astype(F32), da * du_factor.astype(F32)


def _ffn_fwd(n2, w_gate, w_up):
    T, K = n2.shape
    tn = w_gate.shape[2]
    N = N_CHIPS * tn
    tm = _tile(T, 512)
    sub = _tile(tm, 256)

    def body(a_ref, wg_ref, wu_ref, g_ref, u_ref, act_ref):
        for s in range(tm // sub):
            rows = pl.ds(s * sub, sub)
            a = a_ref[rows, :]
            g = jnp.dot(a, wg_ref[...], preferred_element_type=F32)
            u = jnp.dot(a, wu_ref[...], preferred_element_type=F32)
            sig = _sigmoid(g)
            sg = g * sig
            g_ref[rows, :] = (u * (sig * (1.0 + g - sg))).astype(BF16)
            u_ref[rows, :] = sg.astype(BF16)
            act_ref[rows, :] = (sg * u).astype(BF16)

    a_spec = pl.BlockSpec((tm, K), lambda j, i: (i, 0))
    b_spec = pl.BlockSpec((None, K, tn), lambda j, i: (j, 0, 0))
    o_spec = pl.BlockSpec((tm, tn), lambda j, i: (i, j))
    return pl.pallas_call(
        body, name="ffn_fwd", grid=(N // tn, T // tm),
        in_specs=[a_spec, b_spec, b_spec], out_specs=[o_spec] * 3,
        out_shape=[jax.ShapeDtypeStruct((T, N), BF16)] * 3,
        compiler_params=_cparams(("parallel", "parallel")),
    )(n2, w_gate, w_up)
```

```python
import functools

import jax
import jax.numpy as jnp
from jax import lax
from jax.experimental import pallas as pl
from jax.experimental.pallas import tpu as pltpu

F32 = jnp.float32
BF16 = jnp.bfloat16
MESH = pl.DeviceIdType.MESH

D_MODEL = 2048
D_SSM = 2048
HEADDIM = 64
N_HEADS = 32
N_GROUPS = 8
HEADS_PER_GROUP = 4
N_STATE = 128
CHUNK = 128
K_SSM = 4
K_SC = 3
D_XBC = 4096
D_FF = 5632
D_IN = 12320
D_MAIN = 12288
OFF_XBC, OFF_CB, OFF_CC, OFF_CX = 2048, 6144, 8192, 10240
DT_PAD = 128
EPS = 1e-5
N_CHIPS = 4
N_DEV = 8

ADAM_LR = 0.001
ADAM_B1 = 0.9
ADAM_B2 = 0.999
ADAM_EPS = 1e-08
ADAM_WD = 0.01
ADAM_STEP = 10

V7X_VMEM_BYTES = 64 * 1024 * 1024
VMEM_LIMIT = V7X_VMEM_BYTES - 8 * 1024 * 1024


def _cparams(sem=None):
    if sem is None:
        return pltpu.CompilerParams(vmem_limit_bytes=VMEM_LIMIT)
    return pltpu.CompilerParams(dimension_semantics=sem, vmem_limit_bytes=VMEM_LIMIT)


def _tile(dim, pref, unit=128):
    best = None
    t = unit
    while t <= min(dim, pref):
        if dim % t == 0:
            best = t
        t += unit
    return best if best is not None else dim


def _sigmoid(x):
    return 1.0 / (1.0 + jnp.exp(-x))


def _silu(x):
    return x * _sigmoid(x)


def _dsilu(x):
    s = _sigmoid(x)
    return s * (1.0 + x * (1.0 - s))


def _softplus(x):
    return jnp.maximum(x, 0.0) + jnp.log(1.0 + jnp.exp(-jnp.abs(x)))


MATMUL_VMEM_BUDGET = 44 * 1024 * 1024


def _matmul(pairs, *, ta=False, tb=False, out_dtypes, name, tm=1024, tn=1024, tk=None, extras=(), epilogue=None,
            deps=(), col_shards=False, nsub=1, b3d=False, m_tiles=None, out_buf=None):
    a0, b0 = pairs[0]
    M, K = (a0.shape[1], a0.shape[0]) if ta else a0.shape
    if b3d:
        N = b0.shape[1] if tb else b0.shape[0] * b0.shape[2]
        tk, tn = (b0.shape[2], tn) if tb else (tk, b0.shape[2])
    else:
        N = b0.shape[0] if tb else b0.shape[1]
    tm, tn = _tile(M, tm, 8 if M % 128 else 128), _tile(N, tn)
    npair, nex, ndep, nout = len(pairs), len(extras), len(deps), len(out_dtypes)
    if tk is None:
        fixed = 2 * tm * tn * (sum(jnp.dtype(d).itemsize for d in out_dtypes) + sum(e.dtype.itemsize for e in extras))
        tk = K
        while tk > 128 and (K % tk or tk % 128 or
                            fixed + 2 * npair * 2 * tk * (tm + tn) + (tm * tn * 4 if tk < K else 0) > MATMUL_VMEM_BUDGET):
            tk -= 128
    else:
        tk = _tile(K, tk)
    nk = K // tk
    if nk > 1 or tm % nsub or (tm // nsub) % 128:
        nsub = 1
    sub = tm // nsub
    dims = (((0 if ta else 1,), (1 if tb else 0,)), ((), ()))
    i0, mi = m_tiles if m_tiles is not None else (0, M // tm)
    nbuf = 0 if out_buf is None else 1

    def body(*refs):
        a_refs = refs[0:2 * npair:2]
        b_refs = refs[1:2 * npair:2]
        ex_refs = refs[2 * npair:2 * npair + nex]
        o_refs = refs[2 * npair + nex + ndep + nbuf:2 * npair + nex + ndep + nbuf + nout]

        def dots(rows):
            s = None
            for a_ref, b_ref in zip(a_refs, b_refs):
                a = a_ref[...] if rows is None else (a_ref[:, rows] if ta else a_ref[rows, :])
                d = lax.dot_general(a, b_ref[...], dims, preferred_element_type=F32)
                s = d if s is None else s + d
            return s

        def finish(r, rows):
            ex = [e[...] if rows is None else e[rows, :] for e in ex_refs]
            outs = (r,) if epilogue is None else epilogue(r, *ex)
            for o_ref, o in zip(o_refs, outs):
                if rows is None:
                    o_ref[...] = o.astype(o_ref.dtype)
                else:
                    o_ref[rows, :] = o.astype(o_ref.dtype)

        if nk == 1:
            for s in range(nsub):
                rows = None if nsub == 1 else pl.ds(s * sub, sub)
                finish(dots(rows), rows)
            return

        acc = refs[-1]
        k = pl.program_id(2)

        @pl.when(k == 0)
        def _():
            acc[...] = dots(None)

        @pl.when(jnp.logical_and(k > 0, k < nk - 1))
        def _():
            acc[...] += dots(None)

        @pl.when(k == nk - 1)
        def _():
            finish(acc[...] + dots(None), None)

    a_spec = (pl.BlockSpec((tk, tm), lambda i, j, k: (k, i + i0)) if ta
              else pl.BlockSpec((tm, tk), lambda i, j, k: (i + i0, k)))
    if b3d:
        b_spec = (pl.BlockSpec((None, tn, tk), lambda i, j, k: (k, j, 0)) if tb
                  else pl.BlockSpec((None, tk, tn), lambda i, j, k: (j, k, 0)))
    else:
        b_spec = (pl.BlockSpec((tn, tk), lambda i, j, k: (j, k)) if tb
                  else pl.BlockSpec((tk, tn), lambda i, j, k: (k, j)))
    e_spec = pl.BlockSpec((tm, tn), lambda i, j, k: (i + i0, j))
    if col_shards:
        o_spec = pl.BlockSpec((None, tm, tn), lambda i, j, k: (j, i + i0, 0))
        o_shape = (N // tn, M, tn)
    else:
        o_spec, o_shape = e_spec, (M, N)
    args, in_specs = [], []
    for a, b in pairs:
        args += [a, b]
        in_specs += [a_spec, b_spec]
    args += list(extras) + list(deps) + ([] if out_buf is None else [out_buf])
    in_specs += [e_spec] * nex + [ANY] * (ndep + nbuf)
    outs = pl.pallas_call(
        body,
        name=name,
        grid=(mi, N // tn, nk),
        in_specs=in_specs,
        out_specs=[o_spec] * nout,
        out_shape=[jax.ShapeDtypeStruct(o_shape, dt) for dt in out_dtypes],
        input_output_aliases={} if out_buf is None else {len(args) - 1: 0},
        scratch_shapes=[pltpu.VMEM((tm, tn), F32)] if nk > 1 else [],
        compiler_params=_cparams(("parallel", "parallel", "arbitrary")),
    )(*args)
    return outs


def _cast_into_gather(w, chip_arr, name, split_cols=False, deps=()):
    R, C = w.shape
    hr, hc = (R, C // 2) if split_cols else (R // 2, C)
    tr = _tile(hr, 512, 8)
    nb = hr // tr

    def body(chip_ref, w_ref, *rest):
        rest[-1][...] = w_ref[...].astype(BF16)

    in_map = (lambda h, i, chip_ref: (i, h)) if split_cols else (lambda h, i, chip_ref: (h * nb + i, 0))
    grid_spec = pltpu.PrefetchScalarGridSpec(
        num_scalar_prefetch=1, grid=(2, nb),
        in_specs=[pl.BlockSpec((tr, hc), in_map)] + [ANY] * len(deps),
        out_specs=pl.BlockSpec((None, tr, hc), lambda h, i, chip_ref: (2 * chip_ref[0] + h, i, 0)))
    return pl.pallas_call(
        body, name=name, grid_spec=grid_spec,
        out_shape=jax.ShapeDtypeStruct((N_DEV, hr, hc), BF16),
        compiler_params=_cparams(("parallel", "parallel")),
    )(chip_arr, w, *deps)


def _tie(small, token, name):
    def body(s_ref, t_ref, o_ref):
        o_ref[...] = s_ref[...]

    vm = pl.BlockSpec(memory_space=pltpu.VMEM)
    return pl.pallas_call(body, name=name, in_specs=[vm, ANY], out_specs=vm,
                          out_shape=jax.ShapeDtypeStruct(small.shape, small.dtype))(small, token)


def _rmsnorm_fwd(x, g, name):
    T, D = x.shape
    tt = _tile(T, 512)

    def body(x_ref, g_ref, n_ref):
        xv = x_ref[...]
        r = lax.rsqrt(jnp.mean(xv * xv, axis=-1, keepdims=True) + EPS)
        n_ref[...] = (xv * r * g_ref[...]).astype(BF16)

    return pl.pallas_call(
        body, name=name, grid=(T // tt,),
        in_specs=[pl.BlockSpec((tt, D), lambda i: (i, 0)), pl.BlockSpec((1, D), lambda i: (0, 0))],
        out_specs=pl.BlockSpec((tt, D), lambda i: (i, 0)),
        out_shape=jax.ShapeDtypeStruct((T, D), BF16),
        compiler_params=_cparams(("parallel",)),
    )(x, g)


def _rmsnorm_bwd(dn, x, g, res, name):
    T, D = x.shape
    tt = _tile(T, 256)

    def body(dn_ref, x_ref, g_ref, res_ref, dx_ref, dxb_ref, dg_ref):
        @pl.when(pl.program_id(0) == 0)
        def _():
            dg_ref[...] = jnp.zeros_like(dg_ref)

        xv = x_ref[...]
        dy = dn_ref[...].astype(F32)
        r = lax.rsqrt(jnp.mean(xv * xv, axis=-1, keepdims=True) + EPS)
        xhat = xv * r
        dxh = dy * g_ref[...]
        dx = res_ref[...] + r * (dxh - xhat * jnp.mean(dxh * xhat, axis=-1, keepdims=True))
        dx_ref[...] = dx
        dxb_ref[...] = dx.astype(BF16)
        dg_ref[...] += jnp.sum(dy * xhat, axis=0, keepdims=True)

    tok = pl.BlockSpec((tt, D), lambda i: (i, 0))
    vec = pl.BlockSpec((1, D), lambda i: (0, 0))
    return pl.pallas_call(
        body, name=name, grid=(T // tt,),
        in_specs=[tok, tok, vec, tok],
        out_specs=[tok, tok, vec],
        out_shape=[jax.ShapeDtypeStruct((T, D), F32), jax.ShapeDtypeStruct((T, D), BF16),
                   jax.ShapeDtypeStruct((1, D), F32)],
        compiler_params=_cparams(("arbitrary",)),
    )(dn, x, g, res)


def _loss_and_final_bwd(h2, target, gf):
    T, D = h2.shape
    tt = _tile(T, 256)

    def body(h_ref, t_ref, g_ref, dh_ref, dhb_ref, dg_ref, loss_ref):
        @pl.when(pl.program_id(0) == 0)
        def _():
            dg_ref[...] = jnp.zeros_like(dg_ref)
            loss_ref[...] = jnp.zeros_like(loss_ref)

        xv = h_ref[...]
        r = lax.rsqrt(jnp.mean(xv * xv, axis=-1, keepdims=True) + EPS)
        xhat = xv * r
        err = xhat * g_ref[...] - t_ref[...]
        loss_ref[...] += 0.5 * jnp.sum(jnp.mean(err * err, axis=-1, keepdims=True), axis=0, keepdims=True)
        dy = err * (1.0 / D)
        dxh = dy * g_ref[...]
        dx = r * (dxh - xhat * jnp.mean(dxh * xhat, axis=-1, keepdims=True))
        dh_ref[...] = dx
        dhb_ref[...] = dx.astype(BF16)
        dg_ref[...] += jnp.sum(dy * xhat, axis=0, keepdims=True)

    tok = pl.BlockSpec((tt, D), lambda i: (i, 0))
    vec = pl.BlockSpec((1, D), lambda i: (0, 0))
    return pl.pallas_call(
        body, name="loss_final_bwd", grid=(T // tt,),
        in_specs=[tok, tok, vec],
        out_specs=[tok, tok, vec, pl.BlockSpec((1, 1), lambda i: (0, 0))],
        out_shape=[jax.ShapeDtypeStruct((T, D), F32), jax.ShapeDtypeStruct((T, D), BF16),
                   jax.ShapeDtypeStruct((1, D), F32), jax.ShapeDtypeStruct((1, 1), F32)],
        compiler_params=_cparams(("arbitrary",)),
    )(h2, target, gf)


def _gated_norm_fwd(y, proj, g):
    T, D = y.shape
    tt = _tile(T, 512)

    def body(y_ref, z_ref, g_ref, o_ref):
        yg = y_ref[...] * _silu(z_ref[...])
        r = lax.rsqrt(jnp.mean(yg * yg, axis=-1, keepdims=True) + EPS)
        o_ref[...] = (yg * r * g_ref[...]).astype(BF16)

    tok = pl.BlockSpec((tt, D), lambda i: (i, 0))
    return pl.pallas_call(
        body, name="gated_norm_fwd", grid=(T // tt,),
        in_specs=[tok, tok, pl.BlockSpec((1, D), lambda i: (0, 0))],
        out_specs=tok,
        out_shape=jax.ShapeDtypeStruct((T, 2 * D_MODEL), BF16),
        compiler_params=_cparams(("parallel",)),
    )(y, proj, g)


def _gated_norm_bwd(dmix, y, proj, g, dproj):
    T, D = y.shape
    tt = _tile(T, 256)

    def body(do_ref, y_ref, z_ref, g_ref, dp_ref, dy_ref, dz_ref, dg_ref):
        @pl.when(pl.program_id(0) == 0)
        def _():
            dg_ref[...] = jnp.zeros_like(dg_ref)

        yv, zv = y_ref[...], z_ref[...]
        do = do_ref[...].astype(F32)
        sz = _silu(zv)
        yg = yv * sz
        r = lax.rsqrt(jnp.mean(yg * yg, axis=-1, keepdims=True) + EPS)
        xhat = yg * r
        dxh = do * g_ref[...]
        dyg = r * (dxh - xhat * jnp.mean(dxh * xhat, axis=-1, keepdims=True))
        dy_ref[...] = dyg * sz
        dz_ref[...] = (dyg * yv * _dsilu(zv)).astype(BF16)
        dg_ref[...] += jnp.sum(do * xhat, axis=0, keepdims=True)

    tok = pl.BlockSpec((tt, D), lambda i: (i, 0))
    vec = pl.BlockSpec((1, D), lambda i: (0, 0))
    return pl.pallas_call(
        body, name="gated_norm_bwd", grid=(T // tt,),
        in_specs=[tok, tok, tok, vec, ANY],
        out_specs=[tok, tok, vec],
        out_shape=[jax.ShapeDtypeStruct((T, D), F32), jax.ShapeDtypeStruct(dproj.shape, BF16),
                   jax.ShapeDtypeStruct((1, D), F32)],
        input_output_aliases={4: 1},
        compiler_params=_cparams(("arbitrary",)),
    )(dmix, y, proj, g, dproj)


HALO = 8


def _shift_down(cur, prev8, s):
    ext = jnp.concatenate([prev8, cur], axis=0)
    return pltpu.roll(ext, s, axis=0)[HALO:]


def _shift_up(cur, next8, s):
    n = cur.shape[0]
    ext = jnp.concatenate([cur, next8], axis=0)
    return pltpu.roll(ext, n + HALO - s, axis=0)[:n]


def _conv_specs(tt, cb, col_off_blocks, nt):
    hb = tt // HALO
    cur = pl.BlockSpec((tt, cb), lambda j, i: (i, col_off_blocks + j))
    prev = pl.BlockSpec((HALO, cb), lambda j, i: (jnp.maximum(i * hb - 1, 0), col_off_blocks + j))
    nxt = pl.BlockSpec((HALO, cb), lambda j, i: (jnp.minimum((i + 1) * hb, nt * hb - 1), col_off_blocks + j))
    return cur, prev, nxt


def _taps(cur, prev8, K):
    return [_shift_down(cur, prev8, K - 1 - k) for k in range(K - 1)] + [cur]


def _conv_of_taps(taps, w):
    y = taps[-1] * w[len(taps) - 1:len(taps), :]
    for k, t in enumerate(taps[:-1]):
        y = y + t * w[k:k + 1, :]
    return y


def _causal_conv(cur, prev8, w, K):
    return _conv_of_taps(_taps(cur, prev8, K), w)


def _anticausal_conv(cur, next8, w, K):
    y = cur * w[K - 1:K, :]
    for k in range(K - 1):
        y = y + _shift_up(cur, next8, K - 1 - k) * w[k:k + 1, :]
    return y


def _ssm_conv_fwd(proj, w8, b):
    T = proj.shape[0]
    tt, cb = _tile(T, 1024), 1024
    nt = T // tt
    cur, prev, _ = _conv_specs(tt, cb, OFF_XBC // cb, nt)

    def body(u_ref, up_ref, w_ref, b_ref, o_ref):
        first = pl.program_id(1) == 0
        p8 = jnp.where(first, 0.0, up_ref[...])
        pre = _causal_conv(u_ref[...], p8, w_ref[...], K_SSM) + b_ref[...]
        o_ref[...] = _silu(pre)

    return pl.pallas_call(
        body, name="ssm_conv_fwd", grid=(D_XBC // cb, nt),
        in_specs=[cur, prev, pl.BlockSpec((8, cb), lambda j, i: (0, j)), pl.BlockSpec((1, cb), lambda j, i: (0, j))],
        out_specs=pl.BlockSpec((tt, cb), lambda j, i: (i, j)),
        out_shape=jax.ShapeDtypeStruct((T, D_XBC), F32),
        compiler_params=_cparams(("parallel", "parallel")),
    )(proj, proj, w8, b)


def _ssm_conv_bwd(dact, proj, w8, b, dproj):
    T = proj.shape[0]
    tt, cb = _tile(T, 512), 512
    nt = T // tt
    cur, prev, nxt = _conv_specs(tt, cb, OFF_XBC // cb, nt)
    dcur, dprev, dnxt = _conv_specs(tt, cb, 0, nt)

    def dpre_of(d, u, p8, w, bb):
        pre = _causal_conv(u, p8, w, K_SSM) + bb
        return d * _dsilu(pre)

    def body(d_ref, dn_ref, u_ref, up_ref, un_ref, w_ref, b_ref, dp_ref, dx_ref, dw_ref, db_ref):
        i = pl.program_id(1)

        @pl.when(i == 0)
        def _():
            dw_ref[...] = jnp.zeros_like(dw_ref)
            db_ref[...] = jnp.zeros_like(db_ref)

        w, bb = w_ref[...], b_ref[...]
        u = u_ref[...]
        p8 = jnp.where(i == 0, 0.0, up_ref[...])
        taps = _taps(u, p8, K_SSM)
        dpre = d_ref[...] * _dsilu(_conv_of_taps(taps, w) + bb)
        un = un_ref[...]
        dpre_n = dpre_of(dn_ref[...], un, u[tt - HALO:, :], w, bb)
        dpre_n = jnp.where(i == nt - 1, 0.0, dpre_n)
        dx_ref[...] = _anticausal_conv(dpre, dpre_n, w, K_SSM).astype(BF16)
        rows = [jnp.sum(dpre * t, axis=0, keepdims=True) for t in taps]
        rows.append(jnp.zeros((8 - K_SSM, cb), F32))
        dw_ref[...] += jnp.concatenate(rows, axis=0)
        db_ref[...] += jnp.sum(dpre, axis=0, keepdims=True)

    wspec = pl.BlockSpec((8, cb), lambda j, i: (0, j))
    bspec = pl.BlockSpec((1, cb), lambda j, i: (0, j))
    return pl.pallas_call(
        body, name="ssm_conv_bwd", grid=(D_XBC // cb, nt),
        in_specs=[dcur, dnxt, cur, prev, nxt, wspec, bspec, ANY],
        out_specs=[pl.BlockSpec((tt, cb), lambda j, i: (i, OFF_XBC // cb + j)), wspec, bspec],
        out_shape=[jax.ShapeDtypeStruct(dproj.shape, BF16), jax.ShapeDtypeStruct((8, D_XBC), F32),
                   jax.ShapeDtypeStruct((1, D_XBC), F32)],
        input_output_aliases={7: 0},
        compiler_params=_cparams(("parallel", "arbitrary")),
    )(dact, dact, proj, proj, proj, w8, b, dproj)


SCB = 512
SC3 = 3 * SCB


def _sc_specs(tt, nt):
    hb = tt // HALO
    cur = pl.BlockSpec((tt, SC3), lambda j, i: (i, OFF_CB // SC3 + j))
    prev = pl.BlockSpec((HALO, SC3), lambda j, i: (jnp.maximum(i * hb - 1, 0), OFF_CB // SC3 + j))
    nxt = pl.BlockSpec((HALO, SC3), lambda j, i: (jnp.minimum((i + 1) * hb, nt * hb - 1), OFF_CB // SC3 + j))
    return cur, prev, nxt


def _shortconv_fwd(proj, w8, ymix):
    T = proj.shape[0]
    tt = _tile(T, 1024)
    nt = T // tt
    cur, prev, _ = _sc_specs(tt, nt)

    def body(p_ref, pp_ref, w_ref, y_ref, o_ref):
        p, pp = p_ref[...], pp_ref[...]
        v = p[:, SCB:2 * SCB] * p[:, 2 * SCB:]
        vp = jnp.where(pl.program_id(1) == 0, 0.0, pp[:, SCB:2 * SCB] * pp[:, 2 * SCB:])
        o_ref[...] = (p[:, :SCB] * _causal_conv(v, vp, w_ref[...], K_SC)).astype(BF16)

    return pl.pallas_call(
        body, name="shortconv_fwd", grid=(D_MODEL // SCB, nt),
        in_specs=[cur, prev, pl.BlockSpec((8, SCB), lambda j, i: (0, j)), ANY],
        out_specs=pl.BlockSpec((tt, SCB), lambda j, i: (i, D_SSM // SCB + j)),
        out_shape=jax.ShapeDtypeStruct(ymix.shape, BF16),
        input_output_aliases={3: 0},
        compiler_params=_cparams(("parallel", "parallel")),
    )(proj, proj, w8, ymix)


def _shortconv_bwd(dmix, proj, w8):
    T = proj.shape[0]
    tt = _tile(T, 512)
    nt = T // tt
    hb = tt // HALO
    cur, prev, nxt = _sc_specs(tt, nt)
    d_s = pl.BlockSpec((tt, SCB), lambda j, i: (i, D_SSM // SCB + j))
    dn_s = pl.BlockSpec((HALO, SCB), lambda j, i: (jnp.minimum((i + 1) * hb, nt * hb - 1), D_SSM // SCB + j))

    def body(d_ref, dn_ref, p_ref, pp_ref, pn_ref, w_ref, dp_ref, dw_ref):
        i = pl.program_id(1)

        @pl.when(i == 0)
        def _():
            dw_ref[...] = jnp.zeros_like(dw_ref)

        w = w_ref[...]
        p, pp = p_ref[...], pp_ref[...]
        gb, gc, u = p[:, :SCB], p[:, SCB:2 * SCB], p[:, 2 * SCB:]
        v = gc * u
        vp = jnp.where(i == 0, 0.0, pp[:, SCB:2 * SCB] * pp[:, 2 * SCB:])
        d = d_ref[...].astype(F32)
        taps = _taps(v, vp, K_SC)
        dp_ref[:, :SCB] = (d * _conv_of_taps(taps, w)).astype(BF16)
        dcv = d * gb
        dcv_n = jnp.where(i == nt - 1, 0.0, dn_ref[...].astype(F32) * pn_ref[:, :SCB])
        dv = _anticausal_conv(dcv, dcv_n, w, K_SC)
        dp_ref[:, SCB:2 * SCB] = (dv * u).astype(BF16)
        dp_ref[:, 2 * SCB:] = (dv * gc).astype(BF16)
        rows = [jnp.sum(dcv * t, axis=0, keepdims=True) for t in taps]
        rows.append(jnp.zeros((8 - K_SC, SCB), F32))
        dw_ref[...] += jnp.concatenate(rows, axis=0)

    wspec = pl.BlockSpec((8, SCB), lambda j, i: (0, j))
    return pl.pallas_call(
        body, name="shortconv_bwd", grid=(D_MODEL // SCB, nt),
        in_specs=[d_s, dn_s, cur, prev, nxt, wspec],
        out_specs=[cur, wspec],
        out_shape=[jax.ShapeDtypeStruct((T, D_MAIN), BF16), jax.ShapeDtypeStruct((8, D_MODEL), F32)],
        compiler_params=_cparams(("parallel", "arbitrary")),
    )(dmix, dmix, proj, proj, proj, w8)


GW = HEADS_PER_GROUP * HEADDIM


def _dot(a, b):
    return jnp.dot(a.astype(BF16), b.astype(BF16), preferred_element_type=F32)


def _dot_nt(a, b):
    return lax.dot_general(a.astype(BF16), b.astype(BF16), (((1,), (1,)), ((), ())), preferred_element_type=F32)


def _dot_tn(a, b):
    return lax.dot_general(a.astype(BF16), b.astype(BF16), (((0,), (0,)), ((), ())), preferred_element_type=F32)


def _bf16_terms(x, n):
    terms, r = [], x
    for _ in range(n):
        t = r.astype(BF16)
        terms.append(t)
        r = r - t.astype(F32)
    return terms


def _dot_sel(a, sel, n=2):
    s = sel.astype(BF16)
    return sum(jnp.dot(t, s, preferred_element_type=F32) for t in _bf16_terms(a, n))


def _sel_dot(sel, b, n=2):
    s = sel.astype(BF16)
    return sum(jnp.dot(s, t, preferred_element_type=F32) for t in _bf16_terms(b, n))


def _sel_dot_nt(sel, b, n=2):
    s = sel.astype(BF16)
    return sum(lax.dot_general(s, t, (((1,), (1,)), ((), ())), preferred_element_type=F32)
               for t in _bf16_terms(b, n))


def _head_cols(rows):
    parts = [jnp.broadcast_to(rows[r:r + 1, :], (HEADDIM, CHUNK)) for r in range(HEADS_PER_GROUP)]
    return jnp.concatenate(parts, axis=0).T


def _head_rows(rows):
    parts = [jnp.broadcast_to(rows[r:r + 1, :], (HEADDIM, N_STATE)) for r in range(HEADS_PER_GROUP)]
    return jnp.concatenate(parts, axis=0)


def _ssd_common(dtr, bias, alog):
    dt = _softplus(dtr + bias)
    A = -jnp.exp(alog)
    a = dt * A
    ki = lax.broadcasted_iota(jnp.int32, (CHUNK, CHUNK), 0)
    si = lax.broadcasted_iota(jnp.int32, (CHUNK, CHUNK), 1)
    upper = (ki <= si).astype(F32)
    cs = _dot_sel(a, upper, 3)
    cs_last = jnp.broadcast_to(cs[:, CHUNK - 1:CHUNK], (8, CHUNK))
    return dt, A, a, cs, cs_last


def _decay_matrix(cs, r):
    li = lax.broadcasted_iota(jnp.int32, (CHUNK, CHUNK), 0)
    si = lax.broadcasted_iota(jnp.int32, (CHUNK, CHUNK), 1)
    causal = li >= si
    R = jnp.broadcast_to(cs[r:r + 1, :], (CHUNK, CHUNK))
    seg = jnp.where(causal, R.T - R, 0.0)
    return jnp.where(causal, jnp.exp(seg), 0.0)


def _decay_cat(cs):
    return jnp.concatenate([_decay_matrix(cs, r) for r in range(HEADS_PER_GROUP)], axis=1)


def _lanes4(m):
    return jnp.concatenate([m] * HEADS_PER_GROUP, axis=1)


def _head_blocks(v):
    col = lax.broadcasted_iota(jnp.int32, v.shape, 1) // HEADDIM
    return jnp.concatenate([jnp.where(col == r, v, jnp.zeros_like(v)) for r in range(HEADS_PER_GROUP)], axis=0)


GXBC = GW + 2 * N_STATE


GS_FWD = 8
GS_BWD = 8


def _ssd_in_specs(nc, rev):
    GS = GS_BWD if rev else GS_FWD
    cix = (lambda c: nc - 1 - c) if rev else (lambda c: c)
    x_s = pl.BlockSpec((CHUNK, GS * GW), lambda g, c: (cix(c), g))
    xbc_s = pl.BlockSpec((CHUNK, GS * GXBC), lambda g, c: (cix(c), g))
    dtr_s = pl.BlockSpec((GS, 8, CHUNK), lambda g, c: (g, 0, cix(c)))
    row_s = pl.BlockSpec((GS, 8, CHUNK), lambda g, c: (g, 0, 0))
    drep_s = pl.BlockSpec((1, GS * GW), lambda g, c: (0, g))
    hs_s = pl.BlockSpec((1, GS * GW, N_STATE), lambda g, c: (cix(c), g, 0))
    return x_s, xbc_s, dtr_s, row_s, drep_s, hs_s


def _xbc_parts(xbc_ref, gi):
    o = gi * GXBC
    return xbc_ref[:, o:o + GW], xbc_ref[:, o + GW:o + GW + N_STATE], xbc_ref[:, o + GW + N_STATE:o + GXBC]


def _ssd_fwd(xbc, dtr, bias, alog, drep):
    T = xbc.shape[0]
    nc = T // CHUNK
    x_s, xbc_s, dtr_s, row_s, drep_s, hs_s = _ssd_in_specs(nc, False)

    def body(xbc_ref, dtr_ref, bias_ref, alog_ref, drep_ref, y_ref, hs_ref, h_scr):
        @pl.when(pl.program_id(1) == 0)
        def _():
            h_scr[...] = jnp.zeros_like(h_scr)

        for gi in range(GS_FWD):
            cols, rows = slice(gi * GW, (gi + 1) * GW), pl.ds(gi * GW, GW)
            x, Bm, Cm = _xbc_parts(xbc_ref, gi)
            dt, A, a, cs, cs_last = _ssd_common(dtr_ref[gi], bias_ref[gi], alog_ref[gi])
            E = _head_cols(jnp.exp(cs))
            W = _head_cols(jnp.exp(cs_last - cs) * dt)
            X = (x * _head_cols(dt)).astype(BF16)
            CB = _dot_nt(Cm, Bm)
            col = lax.broadcasted_iota(jnp.int32, (CHUNK, GW), 1) // HEADDIM
            y = jnp.zeros((CHUNK, GW), F32)
            for r in range(HEADS_PER_GROUP):
                y = y + jnp.where(col == r, _dot(CB * _decay_matrix(cs, r), X), 0.0)
            h = h_scr[rows, :]
            hs_ref[0, rows, :] = h
            y = y + _dot_nt(Cm, h) * E
            y_ref[:, cols] = y + drep_ref[:, cols] * x
            h_scr[rows, :] = h * _head_rows(jnp.exp(cs_last)) + _dot_tn(x * W, Bm)

    return pl.pallas_call(
        body, name="ssd_fwd", grid=(N_GROUPS // GS_FWD, nc),
        in_specs=[xbc_s, dtr_s, row_s, row_s, drep_s],
        out_specs=[x_s, hs_s],
        out_shape=[jax.ShapeDtypeStruct((T, D_SSM), F32), jax.ShapeDtypeStruct((nc, D_SSM, N_STATE), F32)],
        scratch_shapes=[pltpu.VMEM((GS_FWD * GW, N_STATE), F32)],
        compiler_params=_cparams(("parallel", "arbitrary")),
    )(xbc, dtr, bias, alog, drep)


def _ssd_bwd(xbc, dtr, bias, alog, drep, dy, hs):
    T = xbc.shape[0]
    nc = T // CHUNK
    x_s, xbc_s, dtr_s, row_s, drep_s, hs_s = _ssd_in_specs(nc, True)

    def body(xbc_ref, dtr_ref, bias_ref, alog_ref, drep_ref, dy_ref, hs_ref,
             dxbc_ref, ddtr_ref, dbias_ref, dalog_ref, dd_ref, dh_scr):
        @pl.when(pl.program_id(1) == 0)
        def _():
            dh_scr[...] = jnp.zeros_like(dh_scr)
            dbias_ref[...] = jnp.zeros_like(dbias_ref)
            dalog_ref[...] = jnp.zeros_like(dalog_ref)
            dd_ref[...] = jnp.zeros_like(dd_ref)

        for gi in range(GS_BWD):
            one_group(gi, xbc_ref, dtr_ref, bias_ref, alog_ref, drep_ref, dy_ref, hs_ref,
                      dxbc_ref, ddtr_ref, dbias_ref, dalog_ref, dd_ref, dh_scr)

    def one_group(gi, xbc_ref, dtr_ref, bias_ref, alog_ref, drep_ref, dy_ref, hs_ref,
                  dxbc_ref, ddtr_ref, dbias_ref, dalog_ref, dd_ref, dh_scr):
        cols, rows, o = slice(gi * GW, (gi + 1) * GW), pl.ds(gi * GW, GW), gi * GXBC
        x, Bm, Cm = _xbc_parts(xbc_ref, gi)
        dY = dy_ref[:, cols]
        dt, A, a, cs, cs_last = _ssd_common(dtr_ref[gi], bias_ref[gi], alog_ref[gi])
        E = _head_cols(jnp.exp(cs))
        DT = _head_cols(dt)
        Wd = _head_cols(jnp.exp(cs_last - cs))
        X = x * DT
        h = hs_ref[0, rows, :]
        dS = dh_scr[rows, :]
        CB = _dot_nt(Cm, Bm)
        rowid = lax.broadcasted_iota(jnp.int32, (8, CHUNK), 0)
        lane = lax.broadcasted_iota(jnp.int32, (8, CHUNK), 1)
        hsel = (lax.broadcasted_iota(jnp.int32, (8, GW), 1) // HEADDIM
                == lax.broadcasted_iota(jnp.int32, (8, GW), 0)).astype(F32)
        hsel_l = (lax.broadcasted_iota(jnp.int32, (8, HEADS_PER_GROUP * CHUNK), 1) // CHUNK
                  == lax.broadcasted_iota(jnp.int32, (8, HEADS_PER_GROUP * CHUNK), 0)).astype(F32)

        Lc, CBc = _decay_cat(cs), _lanes4(CB)
        Mc = CBc * Lc
        GLc = _dot_nt(dY, _head_blocks(X.astype(BF16))) * Lc
        Wc = GLc * CBc
        colsum = jnp.sum(Wc, axis=0, keepdims=True)
        dcs = _sel_dot_nt(hsel_l, Wc)
        dCB = jnp.zeros((CHUNK, CHUNK), F32)
        for r in range(HEADS_PER_GROUP):
            blk = slice(r * CHUNK, (r + 1) * CHUNK)
            dCB = dCB + GLc[:, blk]
            dcs = dcs - jnp.where(rowid == r, colsum[:, blk], 0.0)
        m_stack = jnp.concatenate([Mc[:, r * CHUNK:(r + 1) * CHUNK].astype(BF16) for r in range(HEADS_PER_GROUP)],
                                  axis=0)
        dX = lax.dot_general(m_stack, _head_blocks(dY.astype(BF16)), (((0,), (0,)), ((), ())),
                             preferred_element_type=F32)
        dC = _dot(dCB, Bm)
        dB = _dot_tn(dCB, Cm)
        T1 = _dot_nt(Bm, dS)
        dX = dX + T1 * Wd
        dB = dB + _dot(X * Wd, dS)
        pdec = _sel_dot_nt(hsel, X * T1 * Wd)
        dcs = dcs - pdec
        dlast = jnp.sum(pdec, axis=1, keepdims=True) \
            + jnp.exp(cs_last[:, 0:1]) * jnp.sum(_sel_dot(hsel, dS * h), axis=1, keepdims=True)
        dYE = dY * E
        dC = dC + _dot(dYE, h)
        yoff = _dot_nt(Cm, h) * E
        dcs = dcs + _sel_dot_nt(hsel, dY * yoff)
        dcs = dcs + jnp.where(lane == CHUNK - 1, dlast, 0.0)
        ki = lax.broadcasted_iota(jnp.int32, (CHUNK, CHUNK), 0)
        si = lax.broadcasted_iota(jnp.int32, (CHUNK, CHUNK), 1)
        lower = (ki >= si).astype(F32)
        da = _dot_sel(dcs, lower)
        ddt = da * A + _sel_dot_nt(hsel, dX * x)
        ddtr = ddt * _sigmoid(dtr_ref[gi] + bias_ref[gi])
        ddtr_ref[gi] = ddtr
        dbias_ref[gi] += ddtr
        dalog_ref[gi] += da * a
        dxbc_ref[:, o:o + GW] = dX * DT + drep_ref[:, cols] * dY
        dd_ref[:, cols] += jnp.sum(dY * x, axis=0, keepdims=True)
        dxbc_ref[:, o + GW:o + GW + N_STATE] = dB
        dxbc_ref[:, o + GW + N_STATE:o + GXBC] = dC
        dh_scr[rows, :] = dS * _head_rows(jnp.exp(cs_last)) + _dot_tn(dYE, Cm)

    return pl.pallas_call(
        body, name="ssd_bwd", grid=(N_GROUPS // GS_BWD, nc),
        in_specs=[xbc_s, dtr_s, row_s, row_s, drep_s, x_s, hs_s],
        out_specs=[xbc_s, dtr_s, row_s, row_s, drep_s],
        out_shape=[jax.ShapeDtypeStruct((T, D_XBC), F32),
                   jax.ShapeDtypeStruct((N_GROUPS, 8, T), F32),
                   jax.ShapeDtypeStruct((N_GROUPS, 8, CHUNK), F32),
                   jax.ShapeDtypeStruct((N_GROUPS, 8, CHUNK), F32),
                   jax.ShapeDtypeStruct((1, D_SSM), F32)],
        scratch_shapes=[pltpu.VMEM((GS_BWD * GW, N_STATE), F32)],
        compiler_params=_cparams(("parallel", "arbitrary")),
    )(xbc, dtr, bias, alog, drep, dy, hs)


def _adamw(w, g, m, v, name, deps=(), emit_g=False):
    R, C = w.shape
    tr = _tile(R, 256, 8)
    nd = len(deps)
    nout = 4 if emit_g else 3

    def body(w_ref, g_ref, m_ref, v_ref, *rest):
        outs = rest[nd:]
        gv = g_ref[...]
        mn = ADAM_B1 * m_ref[...] + (1.0 - ADAM_B1) * gv
        vn = ADAM_B2 * v_ref[...] + (1.0 - ADAM_B2) * (gv * gv)
        m_hat = mn / (1.0 - ADAM_B1 ** ADAM_STEP)
        v_hat = vn / (1.0 - ADAM_B2 ** ADAM_STEP)
        outs[0][...] = -ADAM_LR * (m_hat / (jnp.sqrt(v_hat) + ADAM_EPS) + ADAM_WD * w_ref[...])
        outs[1][...] = mn
        outs[2][...] = vn
        if emit_g:
            outs[3][...] = gv

    spec = pl.BlockSpec((tr, C), lambda i: (i, 0))
    return pl.pallas_call(
        body, name=name, grid=(R // tr,),
        in_specs=[spec] * 4 + [ANY] * nd, out_specs=[spec] * nout,
        out_shape=[jax.ShapeDtypeStruct((R, C), F32)] * nout,
        compiler_params=_cparams(("parallel",)),
    )(w, g, m, v, *deps)


ANY = pl.BlockSpec(memory_space=pl.ANY)


def _place():
    x, y, c = lax.axis_index("x"), lax.axis_index("y"), lax.axis_index("c")
    return x, y, c


def _other_chips(x, y):
    return [(1 - x, y), (x, 1 - y), (1 - x, 1 - y)]


def _allgather_inplace(bufs, splits, first_done=False):
    n = len(bufs)

    def body(*refs):
        o_refs = refs[n:2 * n]
        send_sems, recv_sems = refs[2 * n:]
        x, y, c = _place()
        xn, yn, dg, sibling = (1 - x, y), (x, 1 - y), (1 - x, 1 - y), (x, y, 1 - c)

        def blk(k, chip, pc):
            return o_refs[k].at[4 * chip[0] + 2 * chip[1] + pc]

        def part(k, ref, p):
            kind, s = splits[k]
            _, R, C = bufs[k].shape
            if kind == "rows":
                return ref.at[pl.ds(0, s)] if p == 0 else ref.at[pl.ds(s, R - s)]
            return ref.at[:, pl.ds(0, s)] if p == 0 else ref.at[:, pl.ds(s, C - s)]

        def copy(k, slot, ref, to):
            return pltpu.make_async_remote_copy(
                src_ref=ref, dst_ref=ref, send_sem=send_sems.at[k, slot], recv_sem=recv_sems.at[k, slot],
                device_id=to, device_id_type=MESH)

        sent = []

        def send(k, slot, ref, to):
            cp = copy(k, slot, ref, to)
            cp.start()
            sent.append(cp)

        if not first_done:
            for k in range(n):
                send(k, 0, blk(k, (x, y), c), (*xn, c))
                send(k, 1, blk(k, (x, y), c), (*yn, c))
        for k in range(n):
            bx, by = blk(k, xn, c), blk(k, yn, c)
            if not first_done:
                copy(k, 0, bx, sibling).wait_recv()
            send(k, 2, part(k, bx, 0), (*yn, c))
            send(k, 4, bx, sibling)
            if not first_done:
                copy(k, 1, by, sibling).wait_recv()
            send(k, 3, part(k, by, 1), (*xn, c))
            send(k, 5, by, sibling)
        for k in range(n):
            d0, d1 = part(k, blk(k, dg, c), 0), part(k, blk(k, dg, c), 1)
            copy(k, 2, d0, sibling).wait_recv()
            send(k, 6, d0, sibling)
            copy(k, 3, d1, sibling).wait_recv()
            send(k, 7, d1, sibling)
        for k in range(n):
            copy(k, 4, blk(k, xn, 1 - c), sibling).wait_recv()
            copy(k, 5, blk(k, yn, 1 - c), sibling).wait_recv()
            copy(k, 6, part(k, blk(k, dg, 1 - c), 0), sibling).wait_recv()
            copy(k, 7, part(k, blk(k, dg, 1 - c), 1), sibling).wait_recv()
        for cp in sent:
            cp.wait_send()

    return pl.pallas_call(
        body, name="allgather_w_in",
        in_specs=[ANY] * n, out_specs=[ANY] * n,
        out_shape=[jax.ShapeDtypeStruct(b.shape, b.dtype) for b in bufs],
        input_output_aliases={k: k for k in range(n)},
        scratch_shapes=[pltpu.SemaphoreType.DMA((n, 8)), pltpu.SemaphoreType.DMA((n, 8))],
    )(*bufs)


HBM = pl.BlockSpec(memory_space=pltpu.HBM)
SEM = pl.BlockSpec(memory_space=pltpu.SEMAPHORE)
EFFECT = pltpu.SideEffectType.DATAFLOW_SIDE_EFFECTING


def _split_start(name, arrays, build, n_copies, after=()):
    na, nd = len(arrays), len(after)

    def body(*refs):
        send_sems, recv_sems = refs[na + nd], refs[na + nd + 1]
        for cp in build(refs[:na], send_sems, recv_sems):
            cp.start()
        refs[-1][...] = jnp.zeros((8, 128), F32)

    outs = pl.pallas_call(
        body, name=name,
        out_shape=(pltpu.SemaphoreType.DMA((n_copies,)), pltpu.SemaphoreType.DMA((n_copies,)),
                   *[pltpu.HBM(a.shape, a.dtype) for a in arrays], jax.ShapeDtypeStruct((8, 128), F32)),
        in_specs=[HBM] * na + [ANY] * nd,
        out_specs=(SEM, SEM, *[HBM] * na, pl.BlockSpec(memory_space=pltpu.VMEM)),
        input_output_aliases={i: 2 + i for i in range(na)},
        compiler_params=pltpu.CompilerParams(has_side_effects=EFFECT),
    )(*[pltpu.with_memory_space_constraint(a, pltpu.HBM) for a in arrays], *after)
    return outs[0], outs[1], list(outs[2:2 + na]), outs[-1]


def _split_wait(name, send_sems, recv_sems, arrays, build, after):
    na = len(arrays)

    def body(*refs):
        for cp in build(refs[:na], refs[na], refs[na + 1]):
            cp.wait_send()
            cp.wait_recv()

    outs = pl.pallas_call(
        body, name=name,
        out_shape=tuple(pltpu.HBM(a.shape, a.dtype) for a in arrays),
        in_specs=[HBM] * na + [SEM, SEM] + [ANY] * len(after),
        out_specs=tuple([HBM] * na),
        input_output_aliases={i: i for i in range(na)},
        compiler_params=pltpu.CompilerParams(has_side_effects=EFFECT),
    )(*arrays, send_sems, recv_sems, *after)
    return list(outs)


def _remote(src, dst, send_sems, recv_sems, i, to):
    return pltpu.make_async_remote_copy(src_ref=src, dst_ref=dst, send_sem=send_sems.at[i], recv_sem=recv_sems.at[i],
                                        device_id=to, device_id_type=MESH)


def _build_ag_first(refs, ss, rs):
    x, y, c = _place()
    cps = []
    for k, ref in enumerate(refs):
        blk = ref.at[4 * x + 2 * y + c]
        cps += [_remote(blk, blk, ss, rs, 2 * k, (1 - x, y, c)), _remote(blk, blk, ss, rs, 2 * k + 1, (x, 1 - y, c))]
    return cps


def _build_ag_ici(refs, ss, rs):
    x, y, c = _place()
    cps = []
    for k, ref in enumerate(refs):
        blk = ref.at[4 * x + 2 * y + c]
        for j, (px, py) in enumerate(_other_chips(x, y)):
            cps.append(_remote(blk, blk, ss, rs, 3 * k + j, (px, py, c)))
    return cps


def _build_ag_fwd(refs, ss, rs):
    x, y, c = _place()
    cps = []
    for k, ref in enumerate(refs):
        for j, (px, py) in enumerate(_other_chips(x, y)):
            blk = ref.at[4 * px + 2 * py + c]
            cps.append(_remote(blk, blk, ss, rs, 3 * k + j, (x, y, 1 - c)))
    return cps


def _build_rs_swap(refs, ss, rs):
    x, y, c = _place()
    n = len(refs) // 2
    return [_remote(refs[k].at[:, pl.ds(1 - c, 1)], refs[n + k], ss, rs, k, (x, y, 1 - c)) for k in range(n)]


def _build_rs_ici(refs, ss, rs):
    x, y, c = _place()
    n = len(refs) // 2
    me = 2 * x + y
    cps = []
    for k in range(n):
        for j, (px, py) in enumerate(_other_chips(x, y)):
            cps.append(_remote(refs[k].at[2 * px + py], refs[n + k].at[me], ss, rs, 3 * k + j, (px, py, c)))
    return cps


def _build_rs_share(refs, ss, rs):
    x, y, c = _place()
    return [_remote(ref.at[c], ref.at[c], ss, rs, k, (x, y, 1 - c)) for k, ref in enumerate(refs)]


def _build_small_gather(refs, ss, rs):
    x, y, c = _place()
    me = 4 * x + 2 * y + c
    cps = []
    for d in range(1, N_DEV):
        to = (1 - x if d & 4 else x, 1 - y if d & 2 else y, 1 - c if d & 1 else c)
        cps.append(_remote(refs[0], refs[1].at[me], ss, rs, d - 1, to))
    return cps


def _sum_gathered(mine, landed, me_arr):
    R, C = mine.shape

    def body(me_ref, m_ref, l_ref, o_ref):
        me = me_ref[0]
        s = None
        for d in range(N_DEV):
            t = jnp.where(me == d, m_ref[...], l_ref[d])
            s = t if s is None else s + t
        o_ref[...] = s

    grid_spec = pltpu.PrefetchScalarGridSpec(
        num_scalar_prefetch=1, grid=(1,),
        in_specs=[pl.BlockSpec((R, C), lambda i, me_ref: (0, 0)),
                  pl.BlockSpec((N_DEV, R, C), lambda i, me_ref: (0, 0, 0))],
        out_specs=pl.BlockSpec((R, C), lambda i, me_ref: (0, 0)))
    return pl.pallas_call(
        body, name="sum_small", grid_spec=grid_spec,
        out_shape=jax.ShapeDtypeStruct((R, C), F32),
        compiler_params=_cparams(("arbitrary",)),
    )(me_arr, mine, landed)


def _rs_add_pair(p, r0, c_arr, name):
    _, _, hr, cols = p.shape
    tr = _tile(hr, 256, 8)

    def body(c_ref, p_ref, r_ref, q_ref):
        q_ref[...] = (p_ref[0].astype(F32) + r_ref[0].astype(F32)).astype(BF16)

    grid_spec = pltpu.PrefetchScalarGridSpec(
        num_scalar_prefetch=1, grid=(N_CHIPS, hr // tr),
        in_specs=[pl.BlockSpec((1, 1, tr, cols), lambda j, i, c_ref: (j, c_ref[0], i, 0)),
                  pl.BlockSpec((1, 1, tr, cols), lambda j, i, c_ref: (j, 0, i, 0))],
        out_specs=pl.BlockSpec((1, tr, cols), lambda j, i, c_ref: (j, i, 0)))
    return pl.pallas_call(
        body, name=name, grid_spec=grid_spec,
        out_shape=jax.ShapeDtypeStruct((N_CHIPS, hr, cols), BF16),
        compiler_params=_cparams(("parallel", "parallel")),
    )(c_arr, p, r0)


def _rs_add_chips(r1, q, place_arr, name):
    _, hr, cols = r1.shape
    tr = _tile(hr, 256, 8)

    def body(place_ref, r_ref, q_ref, o_ref):
        chip = place_ref[0]
        s = None
        for j in range(N_CHIPS):
            t = jnp.where(chip == j, q_ref[j], r_ref[j]).astype(F32)
            s = t if s is None else s + t
        o_ref[...] = s

    blk = pl.BlockSpec((N_CHIPS, tr, cols), lambda i, place_ref: (0, i, 0))
    grid_spec = pltpu.PrefetchScalarGridSpec(
        num_scalar_prefetch=1, grid=(hr // tr,), in_specs=[blk, blk],
        out_specs=pl.BlockSpec((None, tr, cols), lambda i, place_ref: (place_ref[1], i, 0)))
    return pl.pallas_call(
        body, name=name, grid_spec=grid_spec,
        out_shape=jax.ShapeDtypeStruct((2, hr, cols), F32),
        compiler_params=_cparams(("parallel",)),
    )(place_arr, r1, q)


def _pad_rows(a, rows):
    return jnp.pad(a, ((0, rows - a.shape[0]), (0, 0)))


def _pad_cols(a, cols):
    return jnp.pad(a, ((0, 0), (0, cols - a.shape[1])))


def _heads_to_rows(v):
    v = v.reshape(N_GROUPS, HEADS_PER_GROUP, 1)
    v = jnp.pad(v, ((0, 0), (0, 8 - HEADS_PER_GROUP), (0, 0)))
    return jnp.broadcast_to(v, (N_GROUPS, 8, CHUNK))


def _rows_to_heads(a):
    return jnp.sum(a[:, :HEADS_PER_GROUP, :], axis=-1).reshape(N_HEADS)


def _to_kernel_rows(a):
    C = a.shape[1]
    x0, b0, c0, s0 = D_SSM, 2 * D_SSM, 2 * D_SSM + 1024, D_SSM + D_XBC + N_HEADS
    xbc = jnp.concatenate([a[x0:b0].reshape(N_GROUPS, GW, C), a[b0:c0].reshape(N_GROUPS, N_STATE, C),
                           a[c0:c0 + 1024].reshape(N_GROUPS, N_STATE, C)], axis=1).reshape(D_XBC, C)
    sc = jnp.concatenate([a[s0 + k * D_MODEL:s0 + (k + 1) * D_MODEL].reshape(D_MODEL // SCB, SCB, C)
                          for k in range(3)], axis=1).reshape(3 * D_MODEL, C)
    return jnp.concatenate([a[:D_SSM], xbc, sc], axis=0)


HR_IN = 1568


def _kernel_segments():
    segs = [(0, 0, 0, D_SSM)]
    for g in range(N_GROUPS):
        k0 = D_SSM + g * GXBC
        segs += [(0, k0, D_SSM + g * GW, GW), (0, k0 + GW, 2 * D_SSM + g * N_STATE, N_STATE),
                 (0, k0 + GW + N_STATE, 2 * D_SSM + 1024 + g * N_STATE, N_STATE)]
    segs.append((1, 0, D_SSM + D_XBC, N_HEADS))
    for j in range(D_MODEL // SCB):
        for k in range(3):
            segs.append((0, D_SSM + D_XBC + j * SC3 + k * SCB, D_SSM + D_XBC + N_HEADS + k * D_MODEL + j * SCB, SCB))
    return segs


def _shard_row_plan():
    cs = D_IN // N_CHIPS
    plan = []
    for src, s, o, n in _kernel_segments():
        while n > 0:
            chip, loc = divmod(o, cs)
            half, row = divmod(loc, HR_IN)
            m = min(n, cs - loc, HR_IN - row)
            plan.append((src, s, chip, half, row, m))
            s, o, n = s + m, o + m, n - m
    return plan


SCATTER_ROWS = 512
SCATTER_SLOTS = 4


def _scatter_rows_to_shards(k_main, k_dt):
    C = k_main.shape[1]
    pieces = []
    for src, s, chip, half, row, n in _shard_row_plan():
        for o in range(0, n, SCATTER_ROWS):
            pieces.append((src, s + o, chip, half, row + o, min(SCATTER_ROWS, n - o)))
    S, lag, N = SCATTER_SLOTS, SCATTER_SLOTS // 2, len(pieces)

    def body(m_ref, d_ref, o_ref, buf, in_sems, out_sems):
        def cin(i):
            src, s, _, _, _, n = pieces[i]
            return pltpu.make_async_copy((d_ref if src else m_ref).at[pl.ds(s, n)],
                                         buf.at[i % S, pl.ds(0, n)], in_sems.at[i % S])

        def cout(i):
            _, _, chip, half, row, n = pieces[i]
            return pltpu.make_async_copy(buf.at[i % S, pl.ds(0, n)],
                                         o_ref.at[chip, half, pl.ds(row, n)], out_sems.at[i % S])

        for i in range(N + lag):
            if i < N:
                if i >= S:
                    cout(i - S).wait()
                cin(i).start()
            j = i - lag
            if 0 <= j < N:
                cin(j).wait()
                cout(j).start()
        for j in range(max(0, N - S), N):
            cout(j).wait()

    return pl.pallas_call(
        body, name="scatter_dw_in_rows", in_specs=[ANY, ANY], out_specs=ANY,
        out_shape=jax.ShapeDtypeStruct((N_CHIPS, 2, HR_IN, C), k_main.dtype),
        scratch_shapes=[pltpu.VMEM((S, SCATTER_ROWS, C), k_main.dtype),
                        pltpu.SemaphoreType.DMA((S,)), pltpu.SemaphoreType.DMA((S,))],
        compiler_params=_cparams(),
    )(k_main, k_dt)


ROWS_IN = D_IN // N_CHIPS
ROWS_IN_PAD = ROWS_IN + 8


def _cast_w_in_into_gather(wt32, chip_arr):
    R, C = wt32.shape
    hc, cbk = C // 2, 256

    def body(chip_ref, w_ref, o_ref):
        y = jnp.concatenate([w_ref[...], jnp.zeros((ROWS_IN_PAD - R, cbk), F32)], axis=0)
        odd = chip_ref[0] % 2 == 1
        o_ref[...] = jnp.where(odd, pltpu.roll(y, ROWS_IN_PAD - R, axis=0), y).astype(BF16)

    grid_spec = pltpu.PrefetchScalarGridSpec(
        num_scalar_prefetch=1, grid=(2, hc // cbk),
        in_specs=[pl.BlockSpec((R, cbk), lambda h, s, chip_ref: (0, h * (hc // cbk) + s))],
        out_specs=pl.BlockSpec((None, ROWS_IN_PAD, cbk), lambda h, s, chip_ref: (2 * chip_ref[0] + h, 0, s)))
    return pl.pallas_call(
        body, name="cast_w_in", grid_spec=grid_spec,
        out_shape=jax.ShapeDtypeStruct((N_DEV, ROWS_IN_PAD, hc), BF16),
        compiler_params=_cparams(("parallel", "parallel")),
    )(chip_arr, wt32)


def _gather_row_plan():
    segs = [(s, o, n) for src, s, o, n in _kernel_segments() if src == 0]
    plan, merges = [], []
    for k, o, n in segs:
        while n > 0:
            chip, loc = divmod(o, ROWS_IN)
            m = min(n, ROWS_IN - loc)
            ps, kd, cnt = loc + 8 * (chip % 2), k, m
            if ps % 16:
                ps, kd, cnt = ps - 8, kd - 8, cnt + 8
            if (ps + cnt) % 16:
                cnt -= 8
                merges.append((kd + cnt, chip, chip + 1))
            if cnt:
                plan.append((chip, ps, kd, cnt))
            k, o, n = k + m, o + m, n - m
    return plan, merges


def _gather_to_kernel_rows(g):
    hc = g.shape[2]
    plan, merges = _gather_row_plan()
    pieces = []
    for chip, ps, kd, n in plan:
        for o in range(0, n, SCATTER_ROWS):
            pieces.append((chip, ps + o, kd + o, min(SCATTER_ROWS, n - o)))
    S, lag, N = SCATTER_SLOTS, SCATTER_SLOTS // 2, len(pieces)

    def body(g_ref, o_ref, buf, mbuf, in_sems, out_sems, m_sems):
        def cins(i):
            chip, ps, _, n = pieces[i]
            return [pltpu.make_async_copy(g_ref.at[2 * chip + h, pl.ds(ps, n)],
                                          buf.at[i % S, pl.ds(0, n), pl.ds(h * hc, hc)], in_sems.at[i % S, h])
                    for h in range(2)]

        def cout(i):
            _, _, kd, n = pieces[i]
            return pltpu.make_async_copy(buf.at[i % S, pl.ds(0, n)], o_ref.at[pl.ds(kd, n)], out_sems.at[i % S])

        for i in range(N + lag):
            if i < N:
                if i >= S:
                    cout(i - S).wait()
                for cp in cins(i):
                    cp.start()
            j = i - lag
            if 0 <= j < N:
                for cp in cins(j):
                    cp.wait()
                cout(j).start()
        for j in range(max(0, N - S), N):
            cout(j).wait()
        for t, (kd, ce, co) in enumerate(merges):
            loads = []
            for h in range(2):
                loads.append(pltpu.make_async_copy(g_ref.at[2 * ce + h, pl.ds(ROWS_IN - 8, 16)],
                                                   mbuf.at[0, :, pl.ds(h * hc, hc)], m_sems.at[2 * h]))
                loads.append(pltpu.make_async_copy(g_ref.at[2 * co + h, pl.ds(0, 16)],
                                                   mbuf.at[1, :, pl.ds(h * hc, hc)], m_sems.at[2 * h + 1]))
            for cp in loads:
                cp.start()
            for cp in loads:
                cp.wait()
            row = lax.broadcasted_iota(jnp.int32, (16, 2 * hc), 0)
            mbuf[2] = jnp.where(row < 8, mbuf[0].astype(F32), mbuf[1].astype(F32)).astype(g.dtype)
            st = pltpu.make_async_copy(mbuf.at[2], o_ref.at[pl.ds(kd, 16)], m_sems.at[4])
            st.start()
            st.wait()

    return pl.pallas_call(
        body, name="w_in_to_kernel_rows", in_specs=[ANY], out_specs=ANY,
        out_shape=jax.ShapeDtypeStruct((D_MAIN, 2 * hc), g.dtype),
        scratch_shapes=[pltpu.VMEM((S, SCATTER_ROWS, 2 * hc), g.dtype), pltpu.VMEM((3, 16, 2 * hc), g.dtype),
                        pltpu.SemaphoreType.DMA((S, 2)), pltpu.SemaphoreType.DMA((S,)),
                        pltpu.SemaphoreType.DMA((5,))],
        compiler_params=_cparams(),
    )(g)


def _to_kernel_xbc(a):
    R = a.shape[0]
    return jnp.concatenate([a[:, :D_SSM].reshape(R, N_GROUPS, GW), a[:, D_SSM:D_SSM + 1024].reshape(R, N_GROUPS, N_STATE),
                            a[:, D_SSM + 1024:].reshape(R, N_GROUPS, N_STATE)], axis=2).reshape(R, D_XBC)


def _from_kernel_xbc(a):
    R = a.shape[0]
    g = a.reshape(R, N_GROUPS, GXBC)
    return jnp.concatenate([g[:, :, :GW].reshape(R, D_SSM), g[:, :, GW:GW + N_STATE].reshape(R, 1024),
                            g[:, :, GW + N_STATE:].reshape(R, 1024)], axis=1)


def kernel(x, norm_mix_g, w_in, ssm_conv_w, ssm_conv_b, ssm_dt_bias, ssm_A_log, ssm_D, ssm_norm_g, sc_conv_w, w_out, norm_ffn_g, w_gate, w_up, w_down, norm_final_g, loss_target, m_norm_mix_g, m_w_in, m_ssm_conv_w, m_ssm_conv_b, m_ssm_dt_bias, m_ssm_A_log, m_ssm_D, m_ssm_norm_g, m_sc_conv_w, m_w_out, m_norm_ffn_g, m_w_gate, m_w_up, m_w_down, m_norm_final_g, v_norm_mix_g, v_w_in, v_ssm_conv_w, v_ssm_conv_b, v_ssm_dt_bias, v_ssm_A_log, v_ssm_D, v_ssm_norm_g, v_sc_conv_w, v_w_out, v_norm_ffn_g, v_w_gate, v_w_up, v_w_down, v_norm_final_g):
    T = x.shape[1]
    xt = x[0]
    tgt = loss_target[0]
    cx, cy, cc = lax.axis_index("x"), lax.axis_index("y"), lax.axis_index("c")
    chip = 2 * cx + cy
    c_arr = jnp.reshape(cc, (1,)).astype(jnp.int32)
    chip_arr = jnp.reshape(chip, (1,)).astype(jnp.int32)
    place_arr = jnp.stack([chip, cc]).astype(jnp.int32)

    big = [w_in[0].T, w_out[0], w_gate[0], w_up[0], w_down[0]]
    names = ["w_in", "w_out", "w_gate", "w_up", "w_down"]
    gb_in = _cast_w_in_into_gather(big[0], chip_arr)
    cs_in, cs_conv = D_IN // N_CHIPS, D_XBC // N_CHIPS
    cw = jnp.stack([_pad_rows(ssm_conv_w[0], 8), _pad_cols(_pad_rows(sc_conv_w[0], 8), cs_conv)])
    cw_buf = lax.dynamic_update_slice(jnp.zeros((N_DEV, 8, cs_conv), F32), cw, (2 * chip, 0, 0))
    f_ss, f_rs, f_arr, f_tok = _split_start("ag_in_first_start", [gb_in, cw_buf], _build_ag_first, 4)
    gbufs = [None] + [_cast_into_gather(w, chip_arr, "cast_" + nm, deps=[f_tok]) for w, nm in zip(big[1:], names[1:])]
    n1 = _rmsnorm_fwd(xt, _tie(norm_mix_g, f_tok, "tie_ag_first"), "rmsnorm_mix")
    f_arr = _split_wait("ag_in_first_wait", f_ss, f_rs, f_arr, _build_ag_first, after=gbufs[1:] + [n1])
    g_in, cw_all = _allgather_inplace(f_arr, [("rows", (ROWS_IN_PAD // 32) * 16), ("cols", cs_conv // 2)],
                                      first_done=True)
    cw_all = cw_all.reshape(N_CHIPS, 2, 8, cs_conv)
    ssm_w8 = _to_kernel_xbc(cw_all[:, 0].transpose(1, 0, 2).reshape(8, D_XBC))
    sc_w8 = cw_all[:, 1, :, :D_MODEL // N_CHIPS].transpose(1, 0, 2).reshape(8, D_MODEL)
    ssm_bk = _to_kernel_xbc(ssm_conv_b)
    wt_main = _gather_to_kernel_rows(g_in)
    dt_rows = [jnp.concatenate([g_in[2 * ch, r0:r0 + 16], g_in[2 * ch + 1, r0:r0 + 16]], axis=1)
               for ch, r0 in ((1, ROWS_IN_PAD - 16), (2, 0))]
    wt_dt = _pad_rows(jnp.concatenate(dt_rows, axis=0), DT_PAD)
    ag_ss, ag_rs, ag_bufs, ag_tok = _split_start("ag_ici_start", gbufs[1:], _build_ag_ici, 12, after=[g_in, cw_all])

    bias_rows = _heads_to_rows(ssm_dt_bias[0])
    alog_rows = _heads_to_rows(ssm_A_log[0])
    drep = jnp.repeat(ssm_D[0], HEADDIM).reshape(1, D_SSM)

    (proj,) = _matmul([(n1, wt_main)], tb=True, out_dtypes=[F32], name="mm_proj", deps=[ag_tok])
    (dt_raw,) = _matmul([(n1, wt_dt)], tb=True, out_dtypes=[F32], name="mm_proj_dt")
    xbc = _ssm_conv_fwd(proj, ssm_w8, ssm_bk)
    dtr = jnp.pad(dt_raw[:, :N_HEADS].T.reshape(N_GROUPS, HEADS_PER_GROUP, T), ((0, 0), (0, 4), (0, 0)))
    y_ssd, hs = _ssd_fwd(xbc, dtr, bias_rows, alog_rows, drep)
    ag_bufs = _split_wait("ag_ici_wait", ag_ss, ag_rs, ag_bufs, _build_ag_ici, after=[y_ssd])
    fw_ss, fw_rs, fw_bufs, fw_tok = _split_start("ag_fwd_start", ag_bufs, _build_ag_fwd, 12)
    y_mix = _shortconv_fwd(proj, sc_w8, _gated_norm_fwd(y_ssd, proj, _tie(ssm_norm_g, fw_tok, "tie_ag_fwd")))
    gath = _split_wait("ag_fwd_wait", fw_ss, fw_rs, fw_bufs, _build_ag_fwd, after=[y_mix])
    w_out_f = gath[0].reshape(2 * D_MODEL, D_MODEL)
    w_gate3 = gath[1].reshape(N_CHIPS, D_MODEL, D_FF // N_CHIPS)
    w_up3 = gath[2].reshape(N_CHIPS, D_MODEL, D_FF // N_CHIPS)
    w_down_f = gath[3].reshape(D_FF, D_MODEL)
    (h1,) = _matmul([(y_mix, w_out_f)], out_dtypes=[F32], name="mm_out", extras=[xt],
                    epilogue=lambda acc, res: (acc + res,))
    n2 = _rmsnorm_fwd(h1, norm_ffn_g, "rmsnorm_ffn")
    g_act, u_act, a_act = _ffn_fwd(n2, w_gate3, w_up3)
    (h2,) = _matmul([(a_act, w_down_f)], out_dtypes=[F32], name="mm_down", extras=[h1],
                    epilogue=lambda acc, res: (acc + res,))

    dh2, dh2b, dg_final, loss_part = _loss_and_final_bwd(h2, tgt, norm_final_g.reshape(1, D_MODEL))
    dg_act, du_act = _matmul([(dh2b, w_down_f)], tb=True, out_dtypes=[BF16, BF16], name="mm_down_bwd",
                             tn=512, extras=[g_act, u_act], epilogue=_swiglu_bwd, nsub=2)
    (dw_down,) = _matmul([(a_act, dh2b)], ta=True, out_dtypes=[BF16], name="mm_dw_down", tm=1408, tn=512)
    (dn2,) = _matmul([(dg_act, w_gate3), (du_act, w_up3)], tb=True, b3d=True, out_dtypes=[BF16],
                     name="mm_ffn_in_bwd")
    (dw_gate,) = _matmul([(n2, dg_act)], ta=True, out_dtypes=[BF16], name="mm_dw_gate", tm=512, tn=1408,
                         col_shards=True)
    (dw_up,) = _matmul([(n2, du_act)], ta=True, out_dtypes=[BF16], name="mm_dw_up", tm=512, tn=1408,
                       col_shards=True)
    dh1, dh1b, dg_ffn = _rmsnorm_bwd(dn2, h1, norm_ffn_g, dh2, "rmsnorm_ffn_bwd")
    (dw_out,) = _matmul([(y_mix, dh1b)], ta=True, out_dtypes=[BF16], name="mm_dw_out")

    def halves(g):
        return g.reshape(N_CHIPS, 2, g.shape[1] // 2, g.shape[2])

    def landing(shape, dtype):
        return lax.empty(shape, dtype)

    names1 = names[1:]
    ps1 = [halves(dw_out.reshape(N_CHIPS, -1, D_MODEL)), halves(dw_gate), halves(dw_up),
           halves(dw_down.reshape(N_CHIPS, -1, D_MODEL))]
    r0_1 = [landing((N_CHIPS, 1) + p.shape[2:], p.dtype) for p in ps1]
    sw_ss, sw_rs, sw_arr, sw_tok = _split_start("rs1_swap_start", ps1 + r0_1, _build_rs_swap, 4)
    (dmix,) = _matmul([(dh1b, w_out_f)], tb=True, out_dtypes=[BF16], name="mm_out_bwd", deps=[sw_tok])
    dproj, dw_sc = _shortconv_bwd(dmix, proj, sc_w8)
    dy_ssd, dproj, dg_ssmnorm = _gated_norm_bwd(dmix, y_ssd, proj, ssm_norm_g, dproj)
    sw_arr = _split_wait("rs1_swap_wait", sw_ss, sw_rs, sw_arr, _build_rs_swap, after=[dy_ssd])
    qs1 = [_rs_add_pair(p, r, c_arr, "rs_add_pair_" + nm) for p, r, nm in zip(sw_arr[:4], sw_arr[4:], names1)]
    r1_1 = [landing(q.shape, BF16) for q in qs1]
    ic_ss, ic_rs, ic_arr, ic_tok = _split_start("rs1_ici_start", qs1 + r1_1, _build_rs_ici, 12)
    dxbc_act, ddtr, dbias_acc, dalog_acc, dD_acc = _ssd_bwd(
        xbc, dtr, bias_rows, alog_rows, _tie(drep, ic_tok, "tie_rs1_ici"), dy_ssd, hs)
    dproj, dw_ssmconv, db_ssmconv = _ssm_conv_bwd(dxbc_act, proj, ssm_w8, ssm_bk, dproj)
    dw_ssmconv, db_ssmconv = _from_kernel_xbc(dw_ssmconv), _from_kernel_xbc(db_ssmconv)
    ddt_raw = _pad_cols(ddtr[:, :HEADS_PER_GROUP, :].reshape(N_HEADS, T).T, DT_PAD).astype(BF16)
    (dwt_main,) = _matmul([(dproj, n1)], ta=True, out_dtypes=[F32], name="mm_dw_main")
    (dwt_dt,) = _matmul([(ddt_raw, n1)], ta=True, out_dtypes=[F32], name="mm_dw_dt")
    ic_arr = _split_wait("rs1_ici_wait", ic_ss, ic_rs, ic_arr, _build_rs_ici, after=[dwt_main])
    g1 = [_rs_add_chips(r, q, place_arr, "rs_add_chips_" + nm) for q, r, nm in zip(ic_arr[:4], ic_arr[4:], names1)]
    sh_ss, sh_rs, sh_arr, sh_tok = _split_start("rs1_share_start", g1, _build_rs_share, 4)
    p_in = _scatter_rows_to_shards(dwt_main, dwt_dt)
    s2_ss, s2_rs, s2_arr, s2_tok = _split_start(
        "rs2_swap_start", [p_in, landing((N_CHIPS, 1) + p_in.shape[2:], F32)], _build_rs_swap, 1)
    tm_pb = 1024
    mt = T // _tile(T, tm_pb)
    mt_a = max(mt // 4, 1)
    (dn1a,) = _matmul([(dproj, wt_main)], out_dtypes=[F32], name="mm_proj_bwd_a", deps=[s2_tok], tm=tm_pb,
                      m_tiles=(0, mt_a))
    s2_arr = _split_wait("rs2_swap_wait", s2_ss, s2_rs, s2_arr, _build_rs_swap, after=[dn1a])
    q_in = _rs_add_pair(s2_arr[0], s2_arr[1], c_arr, "rs_add_pair_w_in")
    i2_ss, i2_rs, i2_arr, i2_tok = _split_start(
        "rs2_ici_start", [q_in, landing(q_in.shape, BF16)], _build_rs_ici, 3)
    if mt > mt_a:
        (dn1a,) = _matmul([(dproj, wt_main)], out_dtypes=[F32], name="mm_proj_bwd_b", deps=[i2_tok], tm=tm_pb,
                          m_tiles=(mt_a, mt - mt_a), out_buf=dn1a)
    (dn1,) = _matmul([(ddt_raw, wt_dt)], out_dtypes=[BF16], name="mm_proj_dt_bwd", extras=[dn1a],
                     epilogue=lambda acc, res: (acc + res,), deps=[i2_tok])
    dx, _, dg_mix = _rmsnorm_bwd(dn1, xt, norm_mix_g, dh1, "rmsnorm_mix_bwd")
    g1 = _split_wait("rs1_share_wait", sh_ss, sh_rs, sh_arr, _build_rs_share, after=[dx])

    big_m = [m_w_in[0].T, m_w_out[0], m_w_gate[0], m_w_up[0], m_w_down[0]]
    big_v = [v_w_in[0].T, v_w_out[0], v_w_gate[0], v_w_up[0], v_w_down[0]]
    big_grads = [None] + [g.reshape(w.shape) for g, w in zip(g1, big[1:])]
    big_out = {}
    for k in range(1, 5):
        *big_out[names[k]], big_grads[k] = _adamw(big[k], big_grads[k], big_m[k], big_v[k], "adamw_" + names[k],
                                                   deps=[i2_tok], emit_g=True)
    i2_arr = _split_wait("rs2_ici_wait", i2_ss, i2_rs, i2_arr, _build_rs_ici, after=[big_out[names[4]][0], dx])
    g_in_red = _rs_add_chips(i2_arr[1], i2_arr[0], place_arr, "rs_add_chips_w_in")
    s3_ss, s3_rs, s3_arr, s3_tok = _split_start("rs2_share_start", [g_in_red], _build_rs_share, 1)

    dD = jnp.sum(dD_acc.reshape(N_HEADS, HEADDIM), axis=-1)
    heads_row = jnp.concatenate([_rows_to_heads(dbias_acc), _rows_to_heads(dalog_acc), dD,
                                 loss_part.reshape(1)]).reshape(1, -1)
    small = jnp.concatenate([
        dw_ssmconv,
        _pad_cols(dw_sc, D_XBC),
        db_ssmconv,
        jnp.concatenate([dg_mix, dg_ssmnorm], axis=1),
        jnp.concatenate([dg_ffn, dg_final], axis=1),
        _pad_cols(heads_row, D_XBC),
        jnp.zeros((4, D_XBC), F32),
    ], axis=0)
    sm_ss, sm_rs, sm_arr, sm_tok = _split_start(
        "small_gather_start", [small, landing((N_DEV,) + small.shape, F32)], _build_small_gather, N_DEV - 1,
        after=[s3_tok])
    (g_in_full,) = _split_wait("rs2_share_wait", s3_ss, s3_rs, s3_arr, _build_rs_share, after=[sm_tok])
    d_t, m_t, v_t, g_t = _adamw(big[0], g_in_full.reshape(2 * HR_IN, D_MODEL), big_m[0], big_v[0],
                                "adamw_" + names[0], emit_g=True)
    big_grads[0] = g_t.T
    big_out[names[0]] = (d_t.T, m_t.T, v_t.T)
    sm_arr = _split_wait("small_gather_wait", sm_ss, sm_rs, sm_arr, _build_small_gather, after=[d_t])
    tot = _sum_gathered(sm_arr[0], sm_arr[1], jnp.reshape(4 * cx + 2 * cy + cc, (1,)).astype(jnp.int32))
    loss = tot[19, 3 * N_HEADS]

    cs_ssm, cs_sc = D_XBC // N_CHIPS, D_MODEL // N_CHIPS
    g_ssm_conv = lax.dynamic_slice(tot[0:K_SSM], (0, chip * cs_ssm), (K_SSM, cs_ssm))
    g_sc_conv = lax.dynamic_slice(tot[8:8 + K_SC, :D_MODEL], (0, chip * cs_sc), (K_SC, cs_sc))
    small_grads = {
        "norm_mix_g": tot[17:18, :D_MODEL], "ssm_conv_w": g_ssm_conv, "ssm_conv_b": tot[16:17],
        "ssm_dt_bias": tot[19:20, 0:N_HEADS], "ssm_A_log": tot[19:20, N_HEADS:2 * N_HEADS],
        "ssm_D": tot[19:20, 2 * N_HEADS:3 * N_HEADS], "ssm_norm_g": tot[17:18, D_MODEL:],
        "sc_conv_w": g_sc_conv, "norm_ffn_g": tot[18:19, :D_MODEL], "norm_final_g": tot[18:19, D_MODEL:],
    }
    small_w = {"norm_mix_g": (norm_mix_g, m_norm_mix_g, v_norm_mix_g),
               "ssm_conv_w": (ssm_conv_w[0], m_ssm_conv_w[0], v_ssm_conv_w[0]),
               "ssm_conv_b": (ssm_conv_b, m_ssm_conv_b, v_ssm_conv_b),
               "ssm_dt_bias": (ssm_dt_bias, m_ssm_dt_bias, v_ssm_dt_bias),
               "ssm_A_log": (ssm_A_log, m_ssm_A_log, v_ssm_A_log),
               "ssm_D": (ssm_D, m_ssm_D, v_ssm_D),
               "ssm_norm_g": (ssm_norm_g, m_ssm_norm_g, v_ssm_norm_g),
               "sc_conv_w": (sc_conv_w[0], m_sc_conv_w[0], v_sc_conv_w[0]),
               "norm_ffn_g": (norm_ffn_g, m_norm_ffn_g, v_norm_ffn_g),
               "norm_final_g": (norm_final_g.reshape(1, -1), m_norm_final_g.reshape(1, -1),
                                v_norm_final_g.reshape(1, -1))}
    PW = 1024
    order = list(small_w)

    def pack(arrs):
        rows = []
        for a in arrs:
            flat = a.reshape(-1)
            n = -(-flat.shape[0] // PW) * PW
            rows.append(jnp.pad(flat, (0, n - flat.shape[0])).reshape(-1, PW))
        slab = jnp.concatenate(rows, axis=0)
        return _pad_rows(slab, -(-slab.shape[0] // 8) * 8)

    wp = pack([small_w[k][0] for k in order])
    mp = pack([small_w[k][1] for k in order])
    vp = pack([small_w[k][2] for k in order])
    gp = pack([small_grads[k] for k in order])
    sd, sm, sv = _adamw(wp, gp, mp, vp, "adamw_small")

    def unpack(slab):
        out, row = {}, 0
        for k in order:
            shape = small_w[k][0].shape
            size = 1
            for s in shape:
                size *= s
            nr = -(-size // PW)
            out[k] = slab[row:row + nr].reshape(-1)[:size].reshape(shape)
            row += nr
        return out

    s_delta, s_m, s_v = unpack(sd), unpack(sm), unpack(sv)

    big_g = dict(zip(names, big_grads))

    weight_order = ["norm_mix_g", "w_in", "ssm_conv_w", "ssm_conv_b", "ssm_dt_bias", "ssm_A_log", "ssm_D",
                    "ssm_norm_g", "sc_conv_w", "w_out", "norm_ffn_g", "w_gate", "w_up", "w_down", "norm_final_g"]
    lead = {"ssm_conv_w", "sc_conv_w", "w_in", "w_out", "w_gate", "w_up", "w_down"}

    def shaped(nm, a):
        if nm == "norm_final_g":
            return a.reshape(D_MODEL)
        return a[None] if nm in lead else a

    grads, deltas, new_m, new_v = [], [], [], []
    for nm in weight_order:
        if nm in big_out:
            g, (d, m, v) = big_g[nm], big_out[nm]
        else:
            g, d, m, v = small_grads[nm], s_delta[nm], s_m[nm], s_v[nm]
        grads.append(shaped(nm, g))
        deltas.append(shaped(nm, d))
        new_m.append(shaped(nm, m))
        new_v.append(shaped(nm, v))
    return (loss, dx[None], *grads, *deltas, *new_m, *new_v)


def _swiglu_bwd(da, dg_factor, du_factor):
    return da * dg_factor.astype(F32), da * du_factor.astype(F32)


def _ffn_fwd(n2, w_gate, w_up):
    T, K = n2.shape
    tn = w_gate.shape[2]
    N = N_CHIPS * tn
    tm = _tile(T, 512)
    sub = _tile(tm, 256)

    def body(a_ref, wg_ref, wu_ref, g_ref, u_ref, act_ref):
        for s in range(tm // sub):
            rows = pl.ds(s * sub, sub)
            a = a_ref[rows, :]
            g = jnp.dot(a, wg_ref[...], preferred_element_type=F32)
            u = jnp.dot(a, wu_ref[...], preferred_element_type=F32)
            sig = _sigmoid(g)
            sg = g * sig
            g_ref[rows, :] = (u * (sig * (1.0 + g - sg))).astype(BF16)
            u_ref[rows, :] = sg.astype(BF16)
            act_ref[rows, :] = (sg * u).astype(BF16)

    a_spec = pl.BlockSpec((tm, K), lambda j, i: (i, 0))
    b_spec = pl.BlockSpec((None, K, tn), lambda j, i: (j, 0, 0))
    o_spec = pl.BlockSpec((tm, tn), lambda j, i: (i, j))
    return pl.pallas_call(
        body, name="ffn_fwd", grid=(N // tn, T // tm),
        in_specs=[a_spec, b_spec, b_spec], out_specs=[o_spec] * 3,
        out_shape=[jax.ShapeDtypeStruct((T, N), BF16)] * 3,
        compiler_params=_cparams(("parallel", "parallel")),
    )(n2, w_gate, w_up)
```
